```python
import jax, jax.numpy as jnp
from jax import lax
import numpy as np

D_MODEL = 1024
BATCH = 8
SEQ = 2048
DEPTH = 2

MEM_LEN = 256
HEAD_DIM = 64
SB_HEADS = 8
FOX_HEADS = 8
MEM_HEADS = 4
SB_WIDTH = SB_HEADS * HEAD_DIM
FOX_WIDTH = FOX_HEADS * HEAD_DIM
MEM_WIDTH = MEM_HEADS * HEAD_DIM
MIX_WIDTH = SB_WIDTH + FOX_WIDTH + MEM_WIDTH
TOTAL_HEADS = SB_HEADS + FOX_HEADS + MEM_HEADS
IN_WIDTH = 3 * SB_WIDTH + 3 * FOX_WIDTH + FOX_HEADS + MEM_WIDTH + MIX_WIDTH
Q_BLOCK = 128
EPS = 1e-6

kernel_name = "hybrid_stickbreak_fox_memxattn"


def rmsnorm(x, g):
    xf = x.astype(jnp.float32)
    y = xf * lax.rsqrt(jnp.mean(xf * xf, axis=-1, keepdims=True) + EPS)
    return (y * g.astype(jnp.float32)).astype(x.dtype)


def split_heads(t, n_heads):
    b, s, _ = t.shape
    return t.reshape(b, s, n_heads, HEAD_DIM).transpose(0, 2, 1, 3)


def merge_heads(t):
    b, h, s, d = t.shape
    return t.transpose(0, 2, 1, 3).reshape(b, s, h * d)


def stick_breaking_attention(q, k, v):
    seq = q.shape[2]
    scale = HEAD_DIM ** -0.5
    outs = []
    for i in range(seq // Q_BLOCK):
        start, end = i * Q_BLOCK, (i + 1) * Q_BLOCK
        qb, kb, vb = q[:, :, start:end], k[:, :, :end], v[:, :, :end]
        z = jnp.einsum('bhqd,bhkd->bhqk', qb, kb,
                       preferred_element_type=jnp.float32) * scale
        q_pos = start + jnp.arange(Q_BLOCK)[:, None]
        k_pos = jnp.arange(end)[None, :]
        strict = k_pos < q_pos
        log_fail = jnp.where(strict, -jax.nn.softplus(z), 0.0)
        suffix = lax.cumsum(log_fail, axis=3, reverse=True) - log_fail
        log_a = jax.nn.log_sigmoid(z) + suffix
        a = jnp.where(strict, jnp.exp(log_a), 0.0)
        outs.append(jnp.einsum('bhqk,bhkd->bhqd', a.astype(vb.dtype), vb))
    return jnp.concatenate(outs, axis=2)


def forgetting_attention(q, k, v, log_f_cum):
    seq = q.shape[2]
    scale = HEAD_DIM ** -0.5
    outs = []
    for i in range(seq // Q_BLOCK):
        start, end = i * Q_BLOCK, (i + 1) * Q_BLOCK
        qb, kb, vb = q[:, :, start:end], k[:, :, :end], v[:, :, :end]
        logits = jnp.einsum('bhqd,bhkd->bhqk', qb, kb,
                            preferred_element_type=jnp.float32) * scale
        logits = logits + log_f_cum[:, :, start:end, None] - log_f_cum[:, :, None, :end]
        q_pos = start + jnp.arange(Q_BLOCK)[:, None]
        k_pos = jnp.arange(end)[None, :]
        logits = jnp.where(k_pos <= q_pos, logits, -jnp.inf)
        p = jax.nn.softmax(logits, axis=-1)
        outs.append(jnp.einsum('bhqk,bhkd->bhqd', p.astype(vb.dtype), vb))
    return jnp.concatenate(outs, axis=2)


def memory_attention(q, k, v):
    logits = jnp.einsum('bhqd,bhkd->bhqk', q, k,
                        preferred_element_type=jnp.float32) * (HEAD_DIM ** -0.5)
    p = jax.nn.softmax(logits, axis=-1)
    return jnp.einsum('bhqk,bhkd->bhqd', p.astype(v.dtype), v)


def _fwd_setup_inputs(seed: int = 0) -> dict:
    key = jax.random.key(seed)
    ks = jax.random.split(key, 11)
    f32 = jnp.float32
    x = jax.random.normal(ks[0], (BATCH, SEQ, D_MODEL), f32)
    mem = jax.random.normal(ks[1], (BATCH, MEM_LEN, D_MODEL), f32)
    norm_w = 1.0 + 0.02 * jax.random.normal(ks[2], (DEPTH, D_MODEL), f32)
    w_in = jax.random.normal(ks[3], (DEPTH, D_MODEL, IN_WIDTH), f32) * D_MODEL ** -0.5
    b_forget = 0.1 * jax.random.normal(ks[4], (DEPTH, FOX_HEADS), f32)
    mem_norm_w = 1.0 + 0.02 * jax.random.normal(ks[5], (DEPTH, D_MODEL), f32)
    w_mem_kv = jax.random.normal(ks[6], (DEPTH, D_MODEL, 2 * MEM_WIDTH), f32) * D_MODEL ** -0.5
    out_norm_w = 1.0 + 0.02 * jax.random.normal(ks[7], (DEPTH, MIX_WIDTH), f32)
    w_out = jax.random.normal(ks[8], (DEPTH, MIX_WIDTH, D_MODEL), f32) * MIX_WIDTH ** -0.5
    final_norm_w = 1.0 + 0.02 * jax.random.normal(ks[9], (D_MODEL,), f32)
    return {"x": x, "mem": mem, "norm_w": norm_w, "w_in": w_in, "b_forget": b_forget,
            "mem_norm_w": mem_norm_w, "w_mem_kv": w_mem_kv, "out_norm_w": out_norm_w,
            "w_out": w_out, "final_norm_w": final_norm_w}


def _fwd_reference(x, mem, norm_w, w_in, b_forget, mem_norm_w, w_mem_kv, out_norm_w, w_out,
              final_norm_w):
    sizes = [SB_WIDTH] * 3 + [FOX_WIDTH] * 3 + [FOX_HEADS, MEM_WIDTH, MIX_WIDTH]
    offsets = [int(o) for o in np.cumsum(sizes)[:-1]]
    b, s, _ = x.shape
    for layer in range(DEPTH):
        h = rmsnorm(x, norm_w[layer])
        proj = h @ w_in[layer]
        (sb_q, sb_k, sb_v, fx_q, fx_k, fx_v, f_logit, m_q, gate) = jnp.split(proj, offsets, axis=-1)

        sb_out = stick_breaking_attention(split_heads(sb_q, SB_HEADS), split_heads(sb_k, SB_HEADS),
                                          split_heads(sb_v, SB_HEADS))

        log_f = jax.nn.log_sigmoid((f_logit + b_forget[layer]).astype(jnp.float32))
        log_f_cum = jnp.cumsum(log_f.transpose(0, 2, 1), axis=-1)
        fx_out = forgetting_attention(split_heads(fx_q, FOX_HEADS), split_heads(fx_k, FOX_HEADS),
                                      split_heads(fx_v, FOX_HEADS), log_f_cum)

        mem_kv = rmsnorm(mem, mem_norm_w[layer]) @ w_mem_kv[layer]
        m_k, m_v = jnp.split(mem_kv, 2, axis=-1)
        m_out = memory_attention(split_heads(m_q, MEM_HEADS), split_heads(m_k, MEM_HEADS),
                                 split_heads(m_v, MEM_HEADS))

        y = jnp.concatenate([merge_heads(sb_out), merge_heads(fx_out), merge_heads(m_out)], axis=-1)
        yf = y.astype(jnp.float32).reshape(b, s, TOTAL_HEADS, HEAD_DIM)
        yf = yf * lax.rsqrt(jnp.mean(yf * yf, axis=-1, keepdims=True) + EPS)
        yf = yf.reshape(b, s, MIX_WIDTH) * out_norm_w[layer].astype(jnp.float32)
        y = (yf * jax.nn.silu(gate.astype(jnp.float32))).astype(x.dtype)
        x = x + y @ w_out[layer]
    return rmsnorm(x, final_norm_w)


import jax as _jax
import jax.numpy as _jnp

TWIN_FORMAT = 'train_step'
FWD_PARAMS = ['x', 'mem', 'norm_w', 'w_in', 'b_forget', 'mem_norm_w', 'w_mem_kv', 'out_norm_w', 'w_out', 'final_norm_w']
TWIN_WEIGHTS = ['norm_w', 'w_in', 'b_forget', 'mem_norm_w', 'w_mem_kv', 'out_norm_w', 'w_out', 'final_norm_w']
TWIN_DIFF_INPUT = 'x'
TWIN_INPUTS = ['x', 'mem', 'norm_w', 'w_in', 'b_forget', 'mem_norm_w', 'w_mem_kv', 'out_norm_w', 'w_out', 'final_norm_w', 'loss_target', 'm_norm_w', 'm_w_in', 'm_b_forget', 'm_mem_norm_w', 'm_w_mem_kv', 'm_out_norm_w', 'm_w_out', 'm_final_norm_w', 'v_norm_w', 'v_w_in', 'v_b_forget', 'v_mem_norm_w', 'v_w_mem_kv', 'v_out_norm_w', 'v_w_out', 'v_final_norm_w']
TWIN_OUTPUTS = ['loss', 'grad_x', 'grad_norm_w', 'grad_w_in', 'grad_b_forget', 'grad_mem_norm_w', 'grad_w_mem_kv', 'grad_out_norm_w', 'grad_w_out', 'grad_final_norm_w', 'delta_norm_w', 'delta_w_in', 'delta_b_forget', 'delta_mem_norm_w', 'delta_w_mem_kv', 'delta_out_norm_w', 'delta_w_out', 'delta_final_norm_w', 'new_m_norm_w', 'new_m_w_in', 'new_m_b_forget', 'new_m_mem_norm_w', 'new_m_w_mem_kv', 'new_m_out_norm_w', 'new_m_w_out', 'new_m_final_norm_w', 'new_v_norm_w', 'new_v_w_in', 'new_v_b_forget', 'new_v_mem_norm_w', 'new_v_w_mem_kv', 'new_v_out_norm_w', 'new_v_w_out', 'new_v_final_norm_w']
TWIN_LEAF_KINDS = {'loss': 'loss', 'grad_x': 'grad_x', 'grad_norm_w': 'grad_w', 'grad_w_in': 'grad_w', 'grad_b_forget': 'grad_w', 'grad_mem_norm_w': 'grad_w', 'grad_w_mem_kv': 'grad_w', 'grad_out_norm_w': 'grad_w', 'grad_w_out': 'grad_w', 'grad_final_norm_w': 'grad_w', 'delta_norm_w': 'delta_w', 'delta_w_in': 'delta_w', 'delta_b_forget': 'delta_w', 'delta_mem_norm_w': 'delta_w', 'delta_w_mem_kv': 'delta_w', 'delta_out_norm_w': 'delta_w', 'delta_w_out': 'delta_w', 'delta_final_norm_w': 'delta_w', 'new_m_norm_w': 'new_m', 'new_m_w_in': 'new_m', 'new_m_b_forget': 'new_m', 'new_m_mem_norm_w': 'new_m', 'new_m_w_mem_kv': 'new_m', 'new_m_out_norm_w': 'new_m', 'new_m_w_out': 'new_m', 'new_m_final_norm_w': 'new_m', 'new_v_norm_w': 'new_v', 'new_v_w_in': 'new_v', 'new_v_b_forget': 'new_v', 'new_v_mem_norm_w': 'new_v', 'new_v_w_mem_kv': 'new_v', 'new_v_out_norm_w': 'new_v', 'new_v_w_out': 'new_v', 'new_v_final_norm_w': 'new_v'}


def _forward(args):
    return _fwd_reference(*[args[k] for k in FWD_PARAMS])


def _output_shape():
    out = _jax.eval_shape(lambda: _forward(_fwd_setup_inputs(0)))
    return out.shape, out.dtype

N_MICROBATCH = 1
ADAM_LR = 0.001
ADAM_B1 = 0.9
ADAM_B2 = 0.999
ADAM_EPS = 1e-08
ADAM_WD = 0.01
ADAM_STEP = 10
PER_EXAMPLE_BATCH_AXIS = {'x': 0, 'mem': 0, 'loss_target': 0}
SHARED_INPUTS = []
_WEIGHT_DTYPES = {'norm_w': _jnp.float32, 'w_in': _jnp.float32, 'b_forget': _jnp.float32, 'mem_norm_w': _jnp.float32, 'w_mem_kv': _jnp.float32, 'out_norm_w': _jnp.float32, 'w_out': _jnp.float32, 'final_norm_w': _jnp.float32}
MOMENT_SCALE = {'norm_w': 1.075384e-01, 'w_in': 4.941533e-02, 'b_forget': 1.567996e-01, 'mem_norm_w': 4.349851e-02, 'w_mem_kv': 5.880396e-02, 'out_norm_w': 6.028891e-02, 'w_out': 6.588349e-02, 'final_norm_w': 1.599007e+01}


def _to_microbatches(a, axis):
    t = _jnp.moveaxis(a, axis, 0)
    t = t.reshape((N_MICROBATCH, t.shape[0] // N_MICROBATCH) + t.shape[1:])
    return _jnp.moveaxis(t, 1, axis + 1)


def setup_inputs(seed: int = 0) -> dict:
    inp = _fwd_setup_inputs(seed)
    key = _jax.random.fold_in(_jax.random.key(seed), 7919)
    shape, _ = _output_shape()
    out = dict(inp)
    out["loss_target"] = _jax.random.normal(_jax.random.fold_in(key, 0), shape, _jnp.float32)
    for i, name in enumerate(TWIN_WEIGHTS):
        w = inp[name].astype(_jnp.float32)
        if MOMENT_SCALE is None:
            s = _jnp.sqrt(_jnp.mean(_jnp.square(w)) + 1e-30)
        else:
            s = MOMENT_SCALE[name]
        km, kv = _jax.random.split(_jax.random.fold_in(key, i + 1))
        out[name] = w
        out["m_" + name] = s * _jax.random.normal(km, w.shape, _jnp.float32)
        out["v_" + name] = (s * s) * _jax.random.uniform(kv, w.shape, _jnp.float32, 0.5, 1.5)
    if N_MICROBATCH > 1:
        for name, axis in PER_EXAMPLE_BATCH_AXIS.items():
            out[name] = _to_microbatches(out[name], axis)
    return {'x': out['x'], 'mem': out['mem'], 'norm_w': out['norm_w'], 'w_in': out['w_in'], 'b_forget': out['b_forget'], 'mem_norm_w': out['mem_norm_w'], 'w_mem_kv': out['w_mem_kv'], 'out_norm_w': out['out_norm_w'], 'w_out': out['w_out'], 'final_norm_w': out['final_norm_w'], 'loss_target': out['loss_target'], 'm_norm_w': out['m_norm_w'], 'm_w_in': out['m_w_in'], 'm_b_forget': out['m_b_forget'], 'm_mem_norm_w': out['m_mem_norm_w'], 'm_w_mem_kv': out['m_w_mem_kv'], 'm_out_norm_w': out['m_out_norm_w'], 'm_w_out': out['m_w_out'], 'm_final_norm_w': out['m_final_norm_w'], 'v_norm_w': out['v_norm_w'], 'v_w_in': out['v_w_in'], 'v_b_forget': out['v_b_forget'], 'v_mem_norm_w': out['v_mem_norm_w'], 'v_w_mem_kv': out['v_w_mem_kv'], 'v_out_norm_w': out['v_out_norm_w'], 'v_w_out': out['v_w_out'], 'v_final_norm_w': out['v_final_norm_w']}


def _loss(weights, diff, rest, loss_target):
    with _jax.named_scope("forward"):
        args = {**rest, TWIN_DIFF_INPUT: diff, **{k: w.astype(_WEIGHT_DTYPES[k]) for k, w in weights.items()}}
        y = _forward(args)
    with _jax.named_scope("loss_head"):
        err = _jnp.square(y.astype(_jnp.float32) - loss_target)
        return 0.5 * _jnp.sum(_jnp.mean(err, axis=-1)) if err.ndim else 0.5 * err


def _adamw(w, g, m, v):
    m = ADAM_B1 * m + (1.0 - ADAM_B1) * g
    v = ADAM_B2 * v + (1.0 - ADAM_B2) * _jnp.square(g)
    m_hat = m / (1.0 - ADAM_B1 ** ADAM_STEP)
    v_hat = v / (1.0 - ADAM_B2 ** ADAM_STEP)
    delta = -ADAM_LR * (m_hat / (_jnp.sqrt(v_hat) + ADAM_EPS) + ADAM_WD * w)
    return delta, m, v


def reference(x, mem, norm_w, w_in, b_forget, mem_norm_w, w_mem_kv, out_norm_w, w_out, final_norm_w, loss_target, m_norm_w, m_w_in, m_b_forget, m_mem_norm_w, m_w_mem_kv, m_out_norm_w, m_w_out, m_final_norm_w, v_norm_w, v_w_in, v_b_forget, v_mem_norm_w, v_w_mem_kv, v_out_norm_w, v_w_out, v_final_norm_w):
    given = dict(x=x, mem=mem, norm_w=norm_w, w_in=w_in, b_forget=b_forget, mem_norm_w=mem_norm_w, w_mem_kv=w_mem_kv, out_norm_w=out_norm_w, w_out=w_out, final_norm_w=final_norm_w, loss_target=loss_target, m_norm_w=m_norm_w, m_w_in=m_w_in, m_b_forget=m_b_forget, m_mem_norm_w=m_mem_norm_w, m_w_mem_kv=m_w_mem_kv, m_out_norm_w=m_out_norm_w, m_w_out=m_w_out, m_final_norm_w=m_final_norm_w, v_norm_w=v_norm_w, v_w_in=v_w_in, v_b_forget=v_b_forget, v_mem_norm_w=v_mem_norm_w, v_w_mem_kv=v_w_mem_kv, v_out_norm_w=v_out_norm_w, v_w_out=v_w_out, v_final_norm_w=v_final_norm_w)
    weights = {n: given[n] for n in TWIN_WEIGHTS}
    shared = {n: given[n] for n in SHARED_INPUTS}
    per_example = {n: given[n] for n in ['x', 'mem']}
    grad_fn = _jax.value_and_grad(_loss, argnums=(0, 1))

    def one_microbatch(ex, loss_target):
        ex = dict(ex)
        diff = ex.pop(TWIN_DIFF_INPUT)
        return grad_fn(weights, diff, {**shared, **ex}, loss_target)

    if N_MICROBATCH == 1:
        loss, (grad_w, grad_x) = one_microbatch(per_example, given["loss_target"])
    else:
        def body(carry, xs):
            loss_sum, grad_sum = carry
            l_k, (gw_k, gx_k) = one_microbatch(xs[0], xs[1])
            with _jax.named_scope("update"):
                return (loss_sum + l_k, _jax.tree.map(_jnp.add, grad_sum, gw_k)), gx_k

        init = (_jnp.zeros((), _jnp.float32), _jax.tree.map(_jnp.zeros_like, weights))
        (loss, grad_w), grad_x = _jax.lax.scan(body, init, (per_example, given["loss_target"]))
    with _jax.named_scope("update"):
        delta_w, new_m, new_v = {}, {}, {}
        for n in TWIN_WEIGHTS:
            delta_w[n], new_m[n], new_v[n] = _adamw(weights[n], grad_w[n], given["m_" + n], given["v_" + n])
    return (loss, grad_x, *[grad_w[n] for n in TWIN_WEIGHTS], *[delta_w[n] for n in TWIN_WEIGHTS],
            *[new_m[n] for n in TWIN_WEIGHTS], *[new_v[n] for n in TWIN_WEIGHTS])
```

```python
import functools

import jax
import jax.numpy as jnp
from jax import lax
from jax.experimental import pallas as pl
from jax.experimental.pallas import tpu as pltpu

F32 = jnp.float32
BF16 = jnp.bfloat16

N_DEV = 8
D_MODEL = 1024
HEAD_DIM = 64
LANES = 128
SB_W = 512
FX_W = 512
MEM_W = 256
MIX_W = 1280
FOX_HEADS = 8
IN_W = 4616
SHARD_W = IN_W // N_DEV
QKV_W = 3 * SB_W + 3 * FX_W + MEM_W
FL_PAD = 256
GF_W = MIX_W + FL_PAD
WR_W = QKV_W + GF_W
EPS = 1e-6
T = 256
TM = 256
Q_SCALE = 0.125
NEG = -1e30

ADAM_LR = 0.001
ADAM_B1 = 0.9
ADAM_B2 = 0.999
ADAM_EPS = 1e-08
ADAM_WD = 0.01
ADAM_STEP = 10

SMALL_ROWS = 72

_NT = (((1,), (1,)), ((), ()))
_TN = (((0,), (0,)), ((), ()))

_ARB1 = pltpu.CompilerParams(dimension_semantics=("arbitrary",))
_ARB2 = pltpu.CompilerParams(dimension_semantics=("arbitrary", "arbitrary"))


def _dot(a, b):
    return jnp.dot(a, b, preferred_element_type=F32)


def _dot_nt(a, b):
    return lax.dot_general(a, b, _NT, preferred_element_type=F32)


def _dot_tn(a, b):
    return lax.dot_general(a, b, _TN, preferred_element_type=F32)


def _split2(x):
    hi = x.astype(BF16)
    lo = (x - hi.astype(F32)).astype(BF16)
    return hi, lo


def _cum2(x, u):
    hi, lo = _split2(x)
    return _dot(hi, u) + _dot(lo, u)


def _tri3(tri, x):
    hi = x.astype(BF16)
    r1 = x - hi.astype(F32)
    mid = r1.astype(BF16)
    lo = (r1 - mid.astype(F32)).astype(BF16)
    return _dot(tri, hi) + _dot(tri, mid) + _dot(tri, lo)


def _iota2(shape, dim):
    return lax.broadcasted_iota(jnp.int32, shape, dim)


def _head_block_diag():
    r = _iota2((LANES, LANES), 0) // HEAD_DIM
    c = _iota2((LANES, LANES), 1) // HEAD_DIM
    return jnp.where(r == c, 1.0, 0.0).astype(BF16)


def _head_mean(x, bd):
    return _cum2(x, bd) * (1.0 / HEAD_DIM)


def _sigmoid(x):
    return 1.0 / (1.0 + jnp.exp(-x))


def _log_sigmoid(x):
    return jnp.minimum(x, 0.0) - jnp.log(1.0 + jnp.exp(-jnp.abs(x)))


def _pair_masks():
    lane = _iota2((1, LANES), 1)
    return lane < HEAD_DIM


def _split_pair(x, m0):
    zero = jnp.zeros_like(x)
    return jnp.where(m0, x, zero), jnp.where(m0, zero, x)


def _inproj_fwd(x, nw, w_r, name):
    s = x.shape[0]

    def body(x_ref, nw_ref, w_ref, qkv_ref, gf_ref):
        xv = x_ref[...]
        r = lax.rsqrt(jnp.mean(xv * xv, axis=-1, keepdims=True) + EPS)
        h = (xv * r * nw_ref[...]).astype(BF16)
        for c in range(0, QKV_W, 256):
            qkv_ref[:, c:c + 256] = _dot(h, w_ref[:, c:c + 256]).astype(BF16)
        for c in range(0, GF_W, 256):
            gf_ref[:, c:c + 256] = _dot(h, w_ref[:, QKV_W + c:QKV_W + c + 256])

    return pl.pallas_call(
        body, name=name, grid=(s // TM,),
        in_specs=[pl.BlockSpec((TM, D_MODEL), lambda i: (i, 0)),
                  pl.BlockSpec((1, D_MODEL), lambda i: (0, 0)),
                  pl.BlockSpec((D_MODEL, WR_W), lambda i: (0, 0))],
        out_specs=[pl.BlockSpec((TM, QKV_W), lambda i: (i, 0)),
                   pl.BlockSpec((TM, GF_W), lambda i: (i, 0))],
        out_shape=[jax.ShapeDtypeStruct((s, QKV_W), BF16), jax.ShapeDtypeStruct((s, GF_W), F32)],
        compiler_params=_ARB1,
    )(x, nw, w_r)


def _fox_prep_fwd(gf, bpad, name):
    s = gf.shape[0]

    def body(fl_ref, b_ref, f_ref):
        tri = jnp.where(_iota2((T, T), 0) >= _iota2((T, T), 1), 1.0, 0.0).astype(BF16)
        carry = jnp.zeros((1, LANES), F32)
        for blk in range(s // T):
            lf = _log_sigmoid(fl_ref[blk * T:(blk + 1) * T, :] + b_ref[...])
            c = _tri3(tri, lf) + carry
            f_ref[blk * T:(blk + 1) * T, :] = c
            carry = c[T - 1:T, :]

    return pl.pallas_call(
        body, name=name, grid=(1,),
        in_specs=[pl.BlockSpec((s, LANES), lambda i: (0, MIX_W // LANES)),
                  pl.BlockSpec((1, LANES), lambda i: (0, 0))],
        out_specs=pl.BlockSpec((s, LANES), lambda i: (0, 0)),
        out_shape=jax.ShapeDtypeStruct((s, LANES), F32),
        compiler_params=_ARB1,
    )(gf, bpad)


def _sb_fwd(qkv, name):
    s = qkv.shape[0]

    def body(q_ref, k_ref, v_ref, o_ref, acc_ref, r_ref):
        i = pl.program_id(1)
        m0 = _pair_masks()
        qh = _split_pair(q_ref[...] * jnp.asarray(Q_SCALE, BF16), m0)
        strict = _iota2((T, T), 0) > _iota2((T, T), 1)
        u = jnp.where(strict, 1.0, 0.0).astype(BF16)
        acc_ref[...] = jnp.zeros_like(acc_ref)
        r_ref[...] = jnp.zeros_like(r_ref)

        def tile(j, diag):
            off = pl.multiple_of(j * T, T)
            k = k_ref[pl.ds(off, T), :]
            v = v_ref[pl.ds(off, T), :]
            for h in range(2):
                z = _dot_nt(qh[h], k)
                lf = -(jnp.maximum(z, 0.0) + jnp.log(1.0 + jnp.exp(-jnp.abs(z))))
                if diag:
                    lf = jnp.where(strict, lf, 0.0)
                cin = _cum2(lf, u)
                run = r_ref[h]
                a = jnp.exp(z + lf + cin + run)
                if diag:
                    a = jnp.where(strict, a, 0.0)
                r_ref[h] = run + cin[:, 0:1] + lf[:, 0:1]
                acc_ref[h] += _dot(a.astype(BF16), v)

        tile(i, True)

        def step(n, carry):
            tile(i - 1 - n, False)
            return carry

        lax.fori_loop(0, i, step, 0)
        o_ref[...] = jnp.where(m0, acc_ref[0], acc_ref[1])

    nb = SB_W // LANES
    return pl.pallas_call(
        body, name=name, grid=(nb, s // T),
        in_specs=[pl.BlockSpec((T, LANES), lambda p, i: (i, p)),
                  pl.BlockSpec((s, LANES), lambda p, i: (0, nb + p)),
                  pl.BlockSpec((s, LANES), lambda p, i: (0, 2 * nb + p))],
        out_specs=pl.BlockSpec((T, LANES), lambda p, i: (i, p)),
        out_shape=jax.ShapeDtypeStruct((s, SB_W), F32),
        scratch_shapes=[pltpu.VMEM((2, T, LANES), F32), pltpu.VMEM((2, T, 1), F32)],
        compiler_params=_ARB2,
    )(qkv, qkv, qkv)


def _fox_fwd(qkv, fqb, frow, name):
    s = qkv.shape[0]

    def body(q_ref, k_ref, v_ref, fq_ref, fr_ref, o_ref, lse_ref, acc_ref, m_ref, l_ref):
        i = pl.program_id(1)
        m0 = _pair_masks()
        qh = _split_pair(q_ref[...] * jnp.asarray(Q_SCALE, BF16), m0)
        fq = fq_ref[...]
        fqh = (fq[:, 0:1], fq[:, HEAD_DIM:HEAD_DIM + 1])
        causal = _iota2((T, T), 0) >= _iota2((T, T), 1)
        acc_ref[...] = jnp.zeros_like(acc_ref)
        l_ref[...] = jnp.zeros_like(l_ref)
        m_ref[...] = jnp.full_like(m_ref, NEG)

        def tile(j, diag):
            off = pl.multiple_of(j * T, T)
            k = k_ref[pl.ds(off, T), :]
            v = v_ref[pl.ds(off, T), :]
            for h in range(2):
                fk = fr_ref[h:h + 1, pl.ds(off, T)]
                sc = _dot_nt(qh[h], k) + fqh[h] - fk
                if diag:
                    sc = jnp.where(causal, sc, NEG)
                m_old = m_ref[h]
                m_new = jnp.maximum(m_old, jnp.max(sc, axis=1, keepdims=True))
                p = jnp.exp(sc - m_new)
                alpha = jnp.exp(m_old - m_new)
                l_ref[h] = alpha * l_ref[h] + jnp.sum(p, axis=1, keepdims=True)
                p_hi, p_lo = _split2(p)
                acc_ref[h] = alpha * acc_ref[h] + _dot(p_hi, v) + _dot(p_lo, v)
                m_ref[h] = m_new

        tile(i, True)

        def step(n, carry):
            tile(i - 1 - n, False)
            return carry

        lax.fori_loop(0, i, step, 0)
        o_ref[...] = jnp.where(m0, acc_ref[0] / l_ref[0], acc_ref[1] / l_ref[1])
        lse_ref[...] = jnp.where(m0, m_ref[0] + jnp.log(l_ref[0]), m_ref[1] + jnp.log(l_ref[1]))

    nb = FX_W // LANES
    base = 3 * SB_W // LANES
    return pl.pallas_call(
        body, name=name, grid=(nb, s // T),
        in_specs=[pl.BlockSpec((T, LANES), lambda p, i: (i, base + p)),
                  pl.BlockSpec((s, LANES), lambda p, i: (0, base + nb + p)),
                  pl.BlockSpec((s, LANES), lambda p, i: (0, base + 2 * nb + p)),
                  pl.BlockSpec((T, LANES), lambda p, i: (i, p)),
                  pl.BlockSpec((None, 2, s), lambda p, i: (p, 0, 0))],
        out_specs=[pl.BlockSpec((T, LANES), lambda p, i: (i, p)),
                   pl.BlockSpec((T, LANES), lambda p, i: (i, p))],
        out_shape=[jax.ShapeDtypeStruct((s, FX_W), F32), jax.ShapeDtypeStruct((s, FX_W), F32)],
        scratch_shapes=[pltpu.VMEM((2, T, LANES), F32), pltpu.VMEM((2, T, 1), F32),
                        pltpu.VMEM((2, T, 1), F32)],
        compiler_params=_ARB2,
    )(qkv, qkv, qkv, fqb, frow)


def _memkv_fwd(mem, mnw, wkv, name):
    n = mem.shape[0]

    def body(mem_ref, mnw_ref, w_ref, kv_ref):
        mv = mem_ref[...]
        r = lax.rsqrt(jnp.mean(mv * mv, axis=-1, keepdims=True) + EPS)
        hm = (mv * r * mnw_ref[...]).astype(BF16)
        kv_ref[...] = _dot(hm, w_ref[...]).astype(BF16)

    return pl.pallas_call(
        body, name=name, grid=(1,),
        in_specs=[pl.BlockSpec((n, D_MODEL), lambda i: (0, 0)),
                  pl.BlockSpec((1, D_MODEL), lambda i: (0, 0)),
                  pl.BlockSpec((D_MODEL, 2 * MEM_W), lambda i: (0, 0))],
        out_specs=pl.BlockSpec((n, 2 * MEM_W), lambda i: (0, 0)),
        out_shape=jax.ShapeDtypeStruct((n, 2 * MEM_W), BF16),
        compiler_params=_ARB1,
    )(mem, mnw, wkv)


def _mem_fwd(qkv, kv, name):
    s = qkv.shape[0]
    n = kv.shape[0]

    def body(q_ref, k_ref, v_ref, o_ref, lse_ref):
        m0 = _pair_masks()
        qh = _split_pair(q_ref[...] * jnp.asarray(Q_SCALE, BF16), m0)
        k = k_ref[...]
        v = v_ref[...]
        outs, lses = [], []
        for h in range(2):
            sc = _dot_nt(qh[h], k)
            mx = jnp.max(sc, axis=1, keepdims=True)
            p = jnp.exp(sc - mx)
            l = jnp.sum(p, axis=1, keepdims=True)
            outs.append(_dot(p.astype(BF16), v) / l)
            lses.append(mx + jnp.log(l))
        o_ref[...] = jnp.where(m0, outs[0], outs[1])
        lse_ref[...] = jnp.where(m0, lses[0], lses[1])

    nb = MEM_W // LANES
    base = (3 * SB_W + 3 * FX_W) // LANES
    return pl.pallas_call(
        body, name=name, grid=(nb, s // T),
        in_specs=[pl.BlockSpec((T, LANES), lambda p, i: (i, base + p)),
                  pl.BlockSpec((n, LANES), lambda p, i: (0, p)),
                  pl.BlockSpec((n, LANES), lambda p, i: (0, nb + p))],
        out_specs=[pl.BlockSpec((T, LANES), lambda p, i: (i, p)),
                   pl.BlockSpec((T, LANES), lambda p, i: (i, p))],
        out_shape=[jax.ShapeDtypeStruct((s, MEM_W), F32), jax.ShapeDtypeStruct((s, MEM_W), F32)],
        compiler_params=_ARB2,
    )(qkv, kv, kv)


def _mix_chunk(c, ysb_ref, yfx_ref, ym_ref):
    if c < SB_W // LANES:
        return ysb_ref[:, c * LANES:(c + 1) * LANES]
    c -= SB_W // LANES
    if c < FX_W // LANES:
        return yfx_ref[:, c * LANES:(c + 1) * LANES]
    c -= FX_W // LANES
    return ym_ref[:, c * LANES:(c + 1) * LANES]


def _outproj_fwd(ysb, yfx, ym, gf, onw, wout, x, name):
    s = x.shape[0]

    def body(ysb_ref, yfx_ref, ym_ref, g_ref, onw_ref, w_ref, x_ref, o_ref, yg_ref):
        bd = _head_block_diag()
        for c in range(MIX_W // LANES):
            sl = slice(c * LANES, (c + 1) * LANES)
            u = _mix_chunk(c, ysb_ref, yfx_ref, ym_ref)
            r = lax.rsqrt(_head_mean(u * u, bd) + EPS)
            g = g_ref[:, sl]
            yg_ref[:, sl] = (u * r * onw_ref[:, sl] * (g * _sigmoid(g))).astype(BF16)
        o_ref[...] = x_ref[...] + _dot(yg_ref[...], w_ref[...])

    return pl.pallas_call(
        body, name=name, grid=(s // TM,),
        in_specs=[pl.BlockSpec((TM, SB_W), lambda i: (i, 0)),
                  pl.BlockSpec((TM, FX_W), lambda i: (i, 0)),
                  pl.BlockSpec((TM, MEM_W), lambda i: (i, 0)),
                  pl.BlockSpec((TM, MIX_W), lambda i: (i, 0)),
                  pl.BlockSpec((1, MIX_W), lambda i: (0, 0)),
                  pl.BlockSpec((MIX_W, D_MODEL), lambda i: (0, 0)),
                  pl.BlockSpec((TM, D_MODEL), lambda i: (i, 0))],
        out_specs=pl.BlockSpec((TM, D_MODEL), lambda i: (i, 0)),
        out_shape=jax.ShapeDtypeStruct((s, D_MODEL), F32),
        scratch_shapes=[pltpu.VMEM((TM, MIX_W), BF16)],
        compiler_params=_ARB1,
    )(ysb, yfx, ym, gf, onw, wout, x)


def _final_fwd_bwd(x, fnw, target, name):
    s = x.shape[0]

    def body(x_ref, w_ref, t_ref, dx_ref, loss_ref, dw_ref):
        @pl.when(pl.program_id(0) == 0)
        def _():
            loss_ref[...] = jnp.zeros_like(loss_ref)
            dw_ref[...] = jnp.zeros_like(dw_ref)

        xv = x_ref[...]
        w = w_ref[...]
        r = lax.rsqrt(jnp.mean(xv * xv, axis=-1, keepdims=True) + EPS)
        xh = xv * r
        err = xh * w - t_ref[...]
        part = jnp.sum(jnp.sum(err * err, axis=1, keepdims=True), axis=0, keepdims=True)
        loss_ref[...] += part * (0.5 / D_MODEL)
        dy = err * (1.0 / D_MODEL)
        dw_ref[...] += jnp.sum(dy * xh, axis=0, keepdims=True)
        dxh = dy * w
        dx_ref[...] = r * (dxh - xh * jnp.mean(dxh * xh, axis=-1, keepdims=True))

    return pl.pallas_call(
        body, name=name, grid=(s // TM,),
        in_specs=[pl.BlockSpec((TM, D_MODEL), lambda i: (i, 0)),
                  pl.BlockSpec((1, D_MODEL), lambda i: (0, 0)),
                  pl.BlockSpec((TM, D_MODEL), lambda i: (i, 0))],
        out_specs=[pl.BlockSpec((TM, D_MODEL), lambda i: (i, 0)),
                   pl.BlockSpec((1, LANES), lambda i: (0, 0)),
                   pl.BlockSpec((1, D_MODEL), lambda i: (0, 0))],
        out_shape=[jax.ShapeDtypeStruct((s, D_MODEL), F32), jax.ShapeDtypeStruct((1, LANES), F32),
                   jax.ShapeDtypeStruct((1, D_MODEL), F32)],
        compiler_params=_ARB1,
    )(x, fnw, target)


def _outproj_bwd(dxo, wout, ysb, yfx, ym, gf, onw, name):
    s = dxo.shape[0]

    def body(dx_ref, w_ref, ysb_ref, yfx_ref, ym_ref, g_ref, onw_ref,
             dysb_ref, dyfx_ref, dym_ref, dg_ref, dw_ref, donw_ref, yg_ref):
        @pl.when(pl.program_id(0) == 0)
        def _():
            dw_ref[...] = jnp.zeros_like(dw_ref)
            donw_ref[...] = jnp.zeros_like(donw_ref)

        dxb = dx_ref[...].astype(BF16)
        dyg = _dot_nt(dxb, w_ref[...])
        bd = _head_block_diag()
        for c in range(MIX_W // LANES):
            sl = slice(c * LANES, (c + 1) * LANES)
            u = _mix_chunk(c, ysb_ref, yfx_ref, ym_ref)
            r = lax.rsqrt(_head_mean(u * u, bd) + EPS)
            yn = u * r
            g = g_ref[:, sl]
            sg = _sigmoid(g)
            sil = g * sg
            onw = onw_ref[:, sl]
            e = dyg[:, sl]
            yg_ref[:, sl] = (yn * onw * sil).astype(BF16)
            donw_ref[:, sl] += jnp.sum(e * yn * sil, axis=0, keepdims=True)
            dg_ref[:, sl] = (e * yn * onw * (sg * (1.0 + g * (1.0 - sg)))).astype(BF16)
            dyn = e * onw * sil
            du = (r * (dyn - yn * _head_mean(dyn * yn, bd))).astype(BF16)
            if c < 4:
                dysb_ref[:, c * LANES:(c + 1) * LANES] = du
            elif c < 8:
                dyfx_ref[:, (c - 4) * LANES:(c - 3) * LANES] = du
            else:
                dym_ref[:, (c - 8) * LANES:(c - 7) * LANES] = du
        dw_ref[...] += _dot_tn(yg_ref[...], dxb)

    return pl.pallas_call(
        body, name=name, grid=(s // TM,),
        in_specs=[pl.BlockSpec((TM, D_MODEL), lambda i: (i, 0)),
                  pl.BlockSpec((MIX_W, D_MODEL), lambda i: (0, 0)),
                  pl.BlockSpec((TM, SB_W), lambda i: (i, 0)),
                  pl.BlockSpec((TM, FX_W), lambda i: (i, 0)),
                  pl.BlockSpec((TM, MEM_W), lambda i: (i, 0)),
                  pl.BlockSpec((TM, MIX_W), lambda i: (i, 0)),
                  pl.BlockSpec((1, MIX_W), lambda i: (0, 0))],
        out_specs=[pl.BlockSpec((TM, SB_W), lambda i: (i, 0)),
                   pl.BlockSpec((TM, FX_W), lambda i: (i, 0)),
                   pl.BlockSpec((TM, MEM_W), lambda i: (i, 0)),
                   pl.BlockSpec((TM, MIX_W), lambda i: (i, 0)),
                   pl.BlockSpec((MIX_W, D_MODEL), lambda i: (0, 0)),
                   pl.BlockSpec((1, MIX_W), lambda i: (0, 0))],
        out_shape=[jax.ShapeDtypeStruct((s, SB_W), BF16), jax.ShapeDtypeStruct((s, FX_W), BF16),
                   jax.ShapeDtypeStruct((s, MEM_W), BF16), jax.ShapeDtypeStruct((s, MIX_W), BF16),
                   jax.ShapeDtypeStruct((MIX_W, D_MODEL), F32), jax.ShapeDtypeStruct((1, MIX_W), F32)],
        scratch_shapes=[pltpu.VMEM((TM, MIX_W), BF16)],
        compiler_params=_ARB1,
    )(dxo, wout, ysb, yfx, ym, gf, onw)


def _row_dots(do_ref, o_ref, m0):
    prod = do_ref[...].astype(F32) * o_ref[...]
    zero = jnp.zeros_like(prod)
    return (jnp.sum(jnp.where(m0, prod, zero), axis=1, keepdims=True),
            jnp.sum(jnp.where(m0, zero, prod), axis=1, keepdims=True))


def _sb_bwd(qkv, o, do, name):
    s = qkv.shape[0]
    nq = s // T

    def body(q_ref, k_ref, v_ref, o_ref, do_ref, dq_ref, dk_ref, dv_ref,
             dqa_ref, dka_ref, dva_ref, rl_ref, rg_ref):
        i = pl.program_id(1)

        @pl.when(i == 0)
        def _():
            dka_ref[...] = jnp.zeros_like(dka_ref)
            dva_ref[...] = jnp.zeros_like(dva_ref)

        m0 = _pair_masks()
        qh = _split_pair(q_ref[...] * jnp.asarray(Q_SCALE, BF16), m0)
        doh = _split_pair(do_ref[...], m0)
        dsum = _row_dots(do_ref, o_ref, m0)
        strict = _iota2((T, T), 0) > _iota2((T, T), 1)
        u = jnp.where(strict, 1.0, 0.0).astype(BF16)
        dqa_ref[...] = jnp.zeros_like(dqa_ref)
        rl_ref[...] = jnp.zeros_like(rl_ref)
        rg_ref[...] = jnp.zeros_like(rg_ref)

        def tile(j, diag):
            off = pl.multiple_of(j * T, T)
            k = k_ref[pl.ds(off, T), :]
            v = v_ref[pl.ds(off, T), :]
            for h in range(2):
                z = _dot_nt(qh[h], k)
                lf = -(jnp.maximum(z, 0.0) + jnp.log(1.0 + jnp.exp(-jnp.abs(z))))
                if diag:
                    lf = jnp.where(strict, lf, 0.0)
                cin = _cum2(lf, u)
                la = z + lf
                beta = jnp.exp(la)
                a = jnp.exp(la + cin + rl_ref[h])
                if diag:
                    a = jnp.where(strict, a, 0.0)
                ab = a.astype(BF16)
                g = ab.astype(F32) * _dot_nt(doh[h], v)
                gin = _cum2(g, u)
                big_g = dsum[h] - (g + gin + rg_ref[h])
                dz = g * (1.0 - beta) - big_g * beta
                if diag:
                    dz = jnp.where(strict, dz, 0.0)
                rl_ref[h] += cin[:, 0:1] + lf[:, 0:1]
                rg_ref[h] += gin[:, 0:1] + g[:, 0:1]
                dzb = dz.astype(BF16)
                dqa_ref[h] += _dot(dzb, k)
                dka_ref[pl.ds(off, T), :] += _dot_tn(dzb, qh[h])
                dva_ref[pl.ds(off, T), :] += _dot_tn(ab, doh[h])

        tile(i, True)

        def step(n, carry):
            tile(i - 1 - n, False)
            return carry

        lax.fori_loop(0, i, step, 0)
        dq_ref[...] = (jnp.where(m0, dqa_ref[0], dqa_ref[1]) * Q_SCALE).astype(BF16)

        @pl.when(i == nq - 1)
        def _():
            dk_ref[...] = dka_ref[...].astype(BF16)
            dv_ref[...] = dva_ref[...].astype(BF16)

    nb = SB_W // LANES
    return pl.pallas_call(
        body, name=name, grid=(nb, nq),
        in_specs=[pl.BlockSpec((T, LANES), lambda p, i: (i, p)),
                  pl.BlockSpec((s, LANES), lambda p, i: (0, nb + p)),
                  pl.BlockSpec((s, LANES), lambda p, i: (0, 2 * nb + p)),
                  pl.BlockSpec((T, LANES), lambda p, i: (i, p)),
                  pl.BlockSpec((T, LANES), lambda p, i: (i, p))],
        out_specs=[pl.BlockSpec((T, LANES), lambda p, i: (i, p)),
                   pl.BlockSpec((s, LANES), lambda p, i: (0, p)),
                   pl.BlockSpec((s, LANES), lambda p, i: (0, p))],
        out_shape=[jax.ShapeDtypeStruct((s, SB_W), BF16)] * 3,
        scratch_shapes=[pltpu.VMEM((2, T, LANES), F32), pltpu.VMEM((s, LANES), F32),
                        pltpu.VMEM((s, LANES), F32), pltpu.VMEM((2, T, 1), F32),
                        pltpu.VMEM((2, T, 1), F32)],
        compiler_params=_ARB2,
    )(qkv, qkv, qkv, o, do)


def _fox_bwd(qkv, fqb, frow, o, lse, do, name):
    s = qkv.shape[0]
    nq = s // T

    def body(q_ref, k_ref, v_ref, fq_ref, fr_ref, o_ref, lse_ref, do_ref,
             dq_ref, dk_ref, dv_ref, df_ref, dqa_ref, dka_ref, dva_ref, dfa_ref):
        i = pl.program_id(1)

        @pl.when(i == 0)
        def _():
            dka_ref[...] = jnp.zeros_like(dka_ref)
            dva_ref[...] = jnp.zeros_like(dva_ref)
            dfa_ref[...] = jnp.zeros_like(dfa_ref)

        m0 = _pair_masks()
        qh = _split_pair(q_ref[...] * jnp.asarray(Q_SCALE, BF16), m0)
        doh = _split_pair(do_ref[...], m0)
        dsum = _row_dots(do_ref, o_ref, m0)
        fq = fq_ref[...]
        fqh = (fq[:, 0:1], fq[:, HEAD_DIM:HEAD_DIM + 1])
        lse = lse_ref[...]
        lseh = (lse[:, 0:1], lse[:, HEAD_DIM:HEAD_DIM + 1])
        causal = _iota2((T, T), 0) >= _iota2((T, T), 1)
        dqa_ref[...] = jnp.zeros_like(dqa_ref)

        def tile(j, diag):
            off = pl.multiple_of(j * T, T)
            k = k_ref[pl.ds(off, T), :]
            v = v_ref[pl.ds(off, T), :]
            for h in range(2):
                fk = fr_ref[h:h + 1, pl.ds(off, T)]
                sc = _dot_nt(qh[h], k) + fqh[h] - fk
                p = jnp.exp(sc - lseh[h])
                if diag:
                    p = jnp.where(causal, p, 0.0)
                dl = p * (_dot_nt(doh[h], v) - dsum[h])
                dlb = dl.astype(BF16)
                dqa_ref[h] += _dot(dlb, k)
                dka_ref[pl.ds(off, T), :] += _dot_tn(dlb, qh[h])
                dva_ref[pl.ds(off, T), :] += _dot_tn(p.astype(BF16), doh[h])
                dfa_ref[h:h + 1, pl.ds(off, T)] -= jnp.sum(dl, axis=0, keepdims=True)

        tile(i, True)

        def step(n, carry):
            tile(i - 1 - n, False)
            return carry

        lax.fori_loop(0, i, step, 0)
        dq_ref[...] = (jnp.where(m0, dqa_ref[0], dqa_ref[1]) * Q_SCALE).astype(BF16)

        @pl.when(i == nq - 1)
        def _():
            dk_ref[...] = dka_ref[...].astype(BF16)
            dv_ref[...] = dva_ref[...].astype(BF16)
            df_ref[...] = dfa_ref[...]

    nb = FX_W // LANES
    base = 3 * SB_W // LANES
    return pl.pallas_call(
        body, name=name, grid=(nb, nq),
        in_specs=[pl.BlockSpec((T, LANES), lambda p, i: (i, base + p)),
                  pl.BlockSpec((s, LANES), lambda p, i: (0, base + nb + p)),
                  pl.BlockSpec((s, LANES), lambda p, i: (0, base + 2 * nb + p)),
                  pl.BlockSpec((T, LANES), lambda p, i: (i, p)),
                  pl.BlockSpec((None, 2, s), lambda p, i: (p, 0, 0)),
                  pl.BlockSpec((T, LANES), lambda p, i: (i, p)),
                  pl.BlockSpec((T, LANES), lambda p, i: (i, p)),
                  pl.BlockSpec((T, LANES), lambda p, i: (i, p))],
        out_specs=[pl.BlockSpec((T, LANES), lambda p, i: (i, p)),
                   pl.BlockSpec((s, LANES), lambda p, i: (0, p)),
                   pl.BlockSpec((s, LANES), lambda p, i: (0, p)),
                   pl.BlockSpec((None, 2, s), lambda p, i: (p, 0, 0))],
        out_shape=[jax.ShapeDtypeStruct((s, FX_W), BF16)] * 3
        + [jax.ShapeDtypeStruct((nb, 2, s), F32)],
        scratch_shapes=[pltpu.VMEM((2, T, LANES), F32), pltpu.VMEM((s, LANES), F32),
                        pltpu.VMEM((s, LANES), F32), pltpu.VMEM((2, s), F32)],
        compiler_params=_ARB2,
    )(qkv, qkv, qkv, fqb, frow, o, lse, do)


def _fox_prep_bwd(dfcol, gf, bpad, name):
    s = gf.shape[0]

    def body(df_ref, fl_ref, b_ref, dfl_ref, db_ref):
        tri = jnp.where(_iota2((T, T), 0) <= _iota2((T, T), 1), 1.0, 0.0).astype(BF16)
        carry = jnp.zeros((1, LANES), F32)
        db = jnp.zeros((1, LANES), F32)
        for blk in reversed(range(s // T)):
            rows = slice(blk * T, (blk + 1) * T)
            c = _tri3(tri, df_ref[rows, :]) + carry
            carry = c[0:1, :]
            dfl = c / (1.0 + jnp.exp(fl_ref[rows, :] + b_ref[...]))
            dfl_ref[rows, :] = dfl.astype(BF16)
            db = db + jnp.sum(dfl, axis=0, keepdims=True)
        db_ref[...] = db

    return pl.pallas_call(
        body, name=name, grid=(1,),
        in_specs=[pl.BlockSpec((s, LANES), lambda i: (0, 0)),
                  pl.BlockSpec((s, LANES), lambda i: (0, MIX_W // LANES)),
                  pl.BlockSpec((1, LANES), lambda i: (0, 0))],
        out_specs=[pl.BlockSpec((s, LANES), lambda i: (0, 0)),
                   pl.BlockSpec((1, LANES), lambda i: (0, 0))],
        out_shape=[jax.ShapeDtypeStruct((s, LANES), BF16), jax.ShapeDtypeStruct((1, LANES), F32)],
        compiler_params=_ARB1,
    )(dfcol, gf, bpad)


def _mem_bwd(qkv, kv, o, lse, do, name):
    s = qkv.shape[0]
    n = kv.shape[0]
    nq = s // T

    def body(q_ref, k_ref, v_ref, o_ref, lse_ref, do_ref, dq_ref, dk_ref, dv_ref):
        @pl.when(pl.program_id(1) == 0)
        def _():
            dk_ref[...] = jnp.zeros_like(dk_ref)
            dv_ref[...] = jnp.zeros_like(dv_ref)

        m0 = _pair_masks()
        qh = _split_pair(q_ref[...] * jnp.asarray(Q_SCALE, BF16), m0)
        doh = _split_pair(do_ref[...], m0)
        dsum = _row_dots(do_ref, o_ref, m0)
        lse = lse_ref[...]
        lseh = (lse[:, 0:1], lse[:, HEAD_DIM:HEAD_DIM + 1])
        k = k_ref[...]
        v = v_ref[...]
        dqs = []
        for h in range(2):
            p = jnp.exp(_dot_nt(qh[h], k) - lseh[h])
            dl = p * (_dot_nt(doh[h], v) - dsum[h])
            dlb = dl.astype(BF16)
            dqs.append(_dot(dlb, k))
            dk_ref[...] += _dot_tn(dlb, qh[h])
            dv_ref[...] += _dot_tn(p.astype(BF16), doh[h])
        dq_ref[...] = (jnp.where(m0, dqs[0], dqs[1]) * Q_SCALE).astype(BF16)

    nb = MEM_W // LANES
    base = (3 * SB_W + 3 * FX_W) // LANES
    return pl.pallas_call(
        body, name=name, grid=(nb, nq),
        in_specs=[pl.BlockSpec((T, LANES), lambda p, i: (i, base + p)),
                  pl.BlockSpec((n, LANES), lambda p, i: (0, p)),
                  pl.BlockSpec((n, LANES), lambda p, i: (0, nb + p)),
                  pl.BlockSpec((T, LANES), lambda p, i: (i, p)),
                  pl.BlockSpec((T, LANES), lambda p, i: (i, p)),
                  pl.BlockSpec((T, LANES), lambda p, i: (i, p))],
        out_specs=[pl.BlockSpec((T, LANES), lambda p, i: (i, p)),
                   pl.BlockSpec((n, LANES), lambda p, i: (0, p)),
                   pl.BlockSpec((n, LANES), lambda p, i: (0, p))],
        out_shape=[jax.ShapeDtypeStruct((s, MEM_W), BF16), jax.ShapeDtypeStruct((n, MEM_W), F32),
                   jax.ShapeDtypeStruct((n, MEM_W), F32)],
        compiler_params=_ARB2,
    )(qkv, kv, kv, o, lse, do)


def _memkv_bwd(mem, mnw, wkv, dk, dv, name):
    n = mem.shape[0]

    def body(mem_ref, mnw_ref, w_ref, dk_ref, dv_ref, dw_ref, dmnw_ref):
        mv = mem_ref[...]
        r = lax.rsqrt(jnp.mean(mv * mv, axis=-1, keepdims=True) + EPS)
        mh = mv * r
        hm = (mh * mnw_ref[...]).astype(BF16)
        dkv = jnp.concatenate([dk_ref[...], dv_ref[...]], axis=1).astype(BF16)
        dw_ref[...] = _dot_tn(hm, dkv)
        dhm = _dot_nt(dkv, w_ref[...])
        dmnw_ref[...] = jnp.sum(dhm * mh, axis=0, keepdims=True)

    return pl.pallas_call(
        body, name=name, grid=(1,),
        in_specs=[pl.BlockSpec((n, D_MODEL), lambda i: (0, 0)),
                  pl.BlockSpec((1, D_MODEL), lambda i: (0, 0)),
                  pl.BlockSpec((D_MODEL, 2 * MEM_W), lambda i: (0, 0)),
                  pl.BlockSpec((n, MEM_W), lambda i: (0, 0)),
                  pl.BlockSpec((n, MEM_W), lambda i: (0, 0))],
        out_specs=[pl.BlockSpec((D_MODEL, 2 * MEM_W), lambda i: (0, 0)),
                   pl.BlockSpec((1, D_MODEL), lambda i: (0, 0))],
        out_shape=[jax.ShapeDtypeStruct((D_MODEL, 2 * MEM_W), F32),
                   jax.ShapeDtypeStruct((1, D_MODEL), F32)],
        compiler_params=_ARB1,
    )(mem, mnw, wkv, dk, dv)


def _inproj_bwd_dx(dproj, w_r, x, nw, dxo, name):
    s = x.shape[0]

    def body(dp_ref, w_ref, x_ref, nw_ref, dxo_ref, dx_ref, ht_ref, dnw_ref):
        @pl.when(pl.program_id(0) == 0)
        def _():
            dnw_ref[...] = jnp.zeros_like(dnw_ref)

        dh = _dot_nt(dp_ref[...], w_ref[...])
        xv = x_ref[...]
        nw = nw_ref[...]
        r = lax.rsqrt(jnp.mean(xv * xv, axis=-1, keepdims=True) + EPS)
        xh = xv * r
        ht_ref[...] = (xh * nw).T.astype(BF16)
        dnw_ref[...] += jnp.sum(dh * xh, axis=0, keepdims=True)
        dxh = dh * nw
        dx_ref[...] = r * (dxh - xh * jnp.mean(dxh * xh, axis=-1, keepdims=True)) + dxo_ref[...]

    return pl.pallas_call(
        body, name=name, grid=(s // TM,),
        in_specs=[pl.BlockSpec((TM, WR_W), lambda i: (i, 0)),
                  pl.BlockSpec((D_MODEL, WR_W), lambda i: (0, 0)),
                  pl.BlockSpec((TM, D_MODEL), lambda i: (i, 0)),
                  pl.BlockSpec((1, D_MODEL), lambda i: (0, 0)),
                  pl.BlockSpec((TM, D_MODEL), lambda i: (i, 0))],
        out_specs=[pl.BlockSpec((TM, D_MODEL), lambda i: (i, 0)),
                   pl.BlockSpec((D_MODEL, TM), lambda i: (0, i)),
                   pl.BlockSpec((1, D_MODEL), lambda i: (0, 0))],
        out_shape=[jax.ShapeDtypeStruct((s, D_MODEL), F32), jax.ShapeDtypeStruct((D_MODEL, s), BF16),
                   jax.ShapeDtypeStruct((1, D_MODEL), F32)],
        compiler_params=_ARB1,
    )(dproj, w_r, x, nw, dxo)


def _inproj_bwd_dw(ht, dproj, name):
    s = dproj.shape[0]
    tn = 256

    def body(ht_ref, dp_ref, dw_ref):
        dw_ref[...] = _dot(ht_ref[...], dp_ref[...])

    return pl.pallas_call(
        body, name=name, grid=(WR_W // tn,),
        in_specs=[pl.BlockSpec((D_MODEL, s), lambda j: (0, 0)),
                  pl.BlockSpec((s, tn), lambda j: (0, j))],
        out_specs=pl.BlockSpec((D_MODEL, tn), lambda j: (0, j)),
        out_shape=jax.ShapeDtypeStruct((D_MODEL, WR_W), F32),
        compiler_params=_ARB1,
    )(ht, dproj)


def _rearrange_w_in(w):
    pad = jnp.zeros(w.shape[:-1] + (FL_PAD - FOX_HEADS,), w.dtype)
    return jnp.concatenate(
        [w[..., :3072], w[..., 3080:3336], w[..., 3336:IN_W], w[..., 3072:3080], pad], axis=-1)


def _restore_w_in(g):
    gate0 = QKV_W
    fl0 = QKV_W + MIX_W
    return jnp.concatenate(
        [g[..., :3072], g[..., fl0:fl0 + FOX_HEADS], g[..., 3072:QKV_W], g[..., gate0:fl0]], axis=-1)


def _pad_lanes(v, width=LANES):
    return jnp.pad(v, (0, width - v.shape[0])).reshape(1, width)


def _local_step(x, mem, norm_w, w_r, b_forget, mem_norm_w, wkv, out_norm_w, wout, final_norm_w,
                target):
    depth = w_r.shape[0]
    s = x.shape[0]
    saved = []
    xs = x
    for l in range(depth):
        nw = norm_w[l].reshape(1, D_MODEL)
        mnw = mem_norm_w[l].reshape(1, D_MODEL)
        onw = out_norm_w[l].reshape(1, MIX_W)
        bpad = _pad_lanes(b_forget[l])
        qkv, gf = _inproj_fwd(xs, nw, w_r[l], f"inproj_fwd_{l}")
        f = _fox_prep_fwd(gf, bpad, f"fox_prep_fwd_{l}")
        f8 = f[:, :FOX_HEADS]
        fqb = jnp.repeat(f8, HEAD_DIM, axis=1)
        frow = f8.T.reshape(FOX_HEADS // 2, 2, s)
        ysb = _sb_fwd(qkv, f"sb_fwd_{l}")
        yfx, lse_fx = _fox_fwd(qkv, fqb, frow, f"fox_fwd_{l}")
        kv = _memkv_fwd(mem, mnw, wkv[l], f"memkv_fwd_{l}")
        ym, lse_m = _mem_fwd(qkv, kv, f"mem_fwd_{l}")
        xn = _outproj_fwd(ysb, yfx, ym, gf, onw, wout[l], xs, f"outproj_fwd_{l}")
        saved.append((xs, nw, mnw, onw, bpad, qkv, gf, fqb, frow, ysb, yfx, lse_fx, kv, ym, lse_m))
        xs = xn

    dx, loss, dfnw = _final_fwd_bwd(xs, final_norm_w.reshape(1, D_MODEL), target, "final_fwd_bwd")

    g_nw, g_wr, g_b, g_mnw, g_wkv, g_onw, g_wout = [], [], [], [], [], [], []
    for l in reversed(range(depth)):
        xs, nw, mnw, onw, bpad, qkv, gf, fqb, frow, ysb, yfx, lse_fx, kv, ym, lse_m = saved[l]
        dysb, dyfx, dym, dgate, dwout, donw = _outproj_bwd(
            dx, wout[l], ysb, yfx, ym, gf, onw, f"outproj_bwd_{l}")
        sdq, sdk, sdv = _sb_bwd(qkv, ysb, dysb, f"sb_bwd_{l}")
        fdq, fdk, fdv, dfrow = _fox_bwd(qkv, fqb, frow, yfx, lse_fx, dyfx, f"fox_bwd_{l}")
        dfcol = jnp.pad(dfrow.reshape(FOX_HEADS, s).T, ((0, 0), (0, LANES - FOX_HEADS)))
        dfl, db = _fox_prep_bwd(dfcol, gf, bpad, f"fox_prep_bwd_{l}")
        dmq, dmk, dmv = _mem_bwd(qkv, kv, ym, lse_m, dym, f"mem_bwd_{l}")
        dwkv, dmnw = _memkv_bwd(mem, mnw, wkv[l], dmk, dmv, f"memkv_bwd_{l}")
        dproj = jnp.concatenate(
            [sdq, sdk, sdv, fdq, fdk, fdv, dmq, dgate, dfl, jnp.zeros((s, FL_PAD - LANES), BF16)],
            axis=1)
        dx, ht, dnw = _inproj_bwd_dx(dproj, w_r[l], xs, nw, dx, f"inproj_bwd_dx_{l}")
        dwr = _inproj_bwd_dw(ht, dproj, f"inproj_bwd_dw_{l}")
        g_nw.append(dnw[0])
        g_wr.append(dwr)
        g_b.append(db[0, :FOX_HEADS])
        g_mnw.append(dmnw[0])
        g_wkv.append(dwkv)
        g_onw.append(donw[0])
        g_wout.append(dwout)

    def stack(parts):
        return jnp.stack(parts[::-1])

    grads = dict(norm_w=stack(g_nw), w_r=stack(g_wr), b_forget=stack(g_b), mem_norm_w=stack(g_mnw),
                 w_mem_kv=stack(g_wkv), out_norm_w=stack(g_onw), w_out=stack(g_wout),
                 final_norm_w=dfnw[0])
    return loss, dx, grads


_ANY = pl.BlockSpec(memory_space=pl.ANY)


def _my_place():
    return lax.axis_index("x"), lax.axis_index("y"), lax.axis_index("c")


def _flip(v, bit):
    return 1 - v if bit else v


def _block_index(px, py, pc):
    return 4 * px + 2 * py + pc


def _all_gather_weights(shards, name):
    n = len(shards)

    def body(*refs):
        ins, outs = refs[:n], refs[n:2 * n]
        send_sems, recv_sems, local_sems = refs[2 * n:]
        x, y, c = _my_place()
        me = (x, y, c)
        sibling = (x, y, 1 - c)
        chips = [(1 - x, y), (x, 1 - y), (1 - x, 1 - y)]

        def copy(a, k, block, to, src=None):
            dst = outs[a].at[_block_index(*block)]
            return pltpu.make_async_remote_copy(
                src_ref=dst if src is None else src, dst_ref=dst,
                send_sem=send_sems.at[a, k], recv_sem=recv_sems.at[a, k],
                device_id=to, device_id_type=pl.DeviceIdType.MESH)

        mine = [pltpu.make_async_copy(ins[a], outs[a].at[_block_index(*me)], local_sems.at[a])
                for a in range(n)]
        for cp in mine:
            cp.start()
        first = []
        for a in range(n):
            first.append(copy(a, 0, me, sibling, src=ins[a]))
            first += [copy(a, 1 + j, me, (*chip, c), src=ins[a]) for j, chip in enumerate(chips)]
        for cp in first:
            cp.start()
        passed = []
        for j, chip in enumerate(chips):
            for a in range(n):
                copy(a, 1 + j, (*chip, c), me).wait_recv()
                fwd = copy(a, 4 + j, (*chip, c), sibling)
                fwd.start()
                passed.append(fwd)
        for a in range(n):
            copy(a, 0, sibling, me).wait_recv()
            for j, chip in enumerate(chips):
                copy(a, 4 + j, (*chip, 1 - c), me).wait_recv()
        for cp in first + passed:
            cp.wait_send()
        for cp in mine:
            cp.wait()

    return pl.pallas_call(
        body, name=name,
        in_specs=[_ANY] * n, out_specs=[_ANY] * n,
        out_shape=[jax.ShapeDtypeStruct((N_DEV,) + v.shape, v.dtype) for v in shards],
        scratch_shapes=[pltpu.SemaphoreType.DMA((n, 7)), pltpu.SemaphoreType.DMA((n, 7)),
                        pltpu.SemaphoreType.DMA((n,))],
    )(*shards)


def _exchange_blocks(blocked, name):
    n = len(blocked)

    def body(*refs):
        ins, outs = refs[:n], refs[n:2 * n]
        send_sems, recv_sems, local_sems = refs[2 * n:]
        x, y, c = _my_place()
        mine_idx = _block_index(x, y, c)
        local = [pltpu.make_async_copy(ins[a].at[mine_idx], outs[a].at[mine_idx], local_sems.at[a])
                 for a in range(n)]
        for cp in local:
            cp.start()
        sends, arrivals = [], []
        for r in range(1, N_DEV):
            peer = (_flip(x, r & 4), _flip(y, r & 2), _flip(c, r & 1))
            peer_idx = _block_index(*peer)
            for a in range(n):
                sems = dict(send_sem=send_sems.at[a, r - 1], recv_sem=recv_sems.at[a, r - 1],
                            device_id=peer, device_id_type=pl.DeviceIdType.MESH)
                sends.append(pltpu.make_async_remote_copy(
                    src_ref=ins[a].at[peer_idx], dst_ref=outs[a].at[mine_idx], **sems))
                arrivals.append(pltpu.make_async_remote_copy(
                    src_ref=ins[a].at[peer_idx], dst_ref=outs[a].at[peer_idx], **sems))
        for cp in sends:
            cp.start()
        for cp in arrivals:
            cp.wait_recv()
        for cp in sends:
            cp.wait_send()
        for cp in local:
            cp.wait()

    return pl.pallas_call(
        body, name=name,
        in_specs=[_ANY] * n, out_specs=[_ANY] * n,
        out_shape=[jax.ShapeDtypeStruct(v.shape, v.dtype) for v in blocked],
        scratch_shapes=[pltpu.SemaphoreType.DMA((n, 7)), pltpu.SemaphoreType.DMA((n, 7)),
                        pltpu.SemaphoreType.DMA((n,))],
    )(*blocked)


def _adamw_sum(parts, w, m, v, rows, name):
    depth, nrow, ncol = w.shape
    c1 = 1.0 / (1.0 - ADAM_B1 ** ADAM_STEP)
    c2 = 1.0 / (1.0 - ADAM_B2 ** ADAM_STEP)

    def body(p_ref, w_ref, m_ref, v_ref, g_ref, d_ref, nm_ref, nv_ref):
        g = p_ref[0].astype(F32)
        for k in range(1, N_DEV):
            g = g + p_ref[k].astype(F32)
        wv = w_ref[...]
        nm = ADAM_B1 * m_ref[...] + (1.0 - ADAM_B1) * g
        nv = ADAM_B2 * v_ref[...] + (1.0 - ADAM_B2) * (g * g)
        g_ref[...] = g
        nm_ref[...] = nm
        nv_ref[...] = nv
        d_ref[...] = -ADAM_LR * ((nm * c1) / (jnp.sqrt(nv * c2) + ADAM_EPS) + ADAM_WD * wv)

    blk = pl.BlockSpec((None, rows, ncol), lambda l, i: (l, i, 0))
    return pl.pallas_call(
        body, name=name, grid=(depth, nrow // rows),
        in_specs=[pl.BlockSpec((N_DEV, None, rows, ncol), lambda l, i: (0, l, i, 0)), blk, blk, blk],
        out_specs=[blk] * 4,
        out_shape=[jax.ShapeDtypeStruct(w.shape, F32)] * 4,
        compiler_params=_ARB2,
    )(parts, w, m, v)


def _pack_small(norm_w, mem_norm_w, out_norm_w, final_norm_w, b_forget):
    onw = jnp.pad(out_norm_w.reshape(20, LANES), ((0, 4), (0, 0)))
    b = jnp.pad(b_forget, ((0, 6), (0, LANES - FOX_HEADS)))
    return jnp.concatenate([norm_w.reshape(16, LANES), mem_norm_w.reshape(16, LANES), onw,
                            final_norm_w.reshape(8, LANES), b], axis=0)


def _unpack_small(p):
    return (p[0:16].reshape(2, D_MODEL), p[16:32].reshape(2, D_MODEL), p[32:52].reshape(2, MIX_W),
            p[56:64].reshape(D_MODEL), p[64:66, :FOX_HEADS])


def kernel(x, mem, norm_w, w_in, b_forget, mem_norm_w, w_mem_kv, out_norm_w, w_out, final_norm_w, loss_target, m_norm_w, m_w_in, m_b_forget, m_mem_norm_w, m_w_mem_kv, m_out_norm_w, m_w_out, m_final_norm_w, v_norm_w, v_w_in, v_b_forget, v_mem_norm_w, v_w_mem_kv, v_out_norm_w, v_w_out, v_final_norm_w):
    depth = w_in.shape[0]
    kv_rows = w_mem_kv.shape[1]
    out_rows = w_out.shape[1]

    g_in, g_kv, g_out = _all_gather_weights(
        [w_in.astype(BF16), w_mem_kv.astype(BF16), w_out.astype(BF16)], "all_gather_weights")
    w_full = jnp.transpose(g_in, (1, 2, 0, 3)).reshape(depth, D_MODEL, IN_W)
    w_r = _rearrange_w_in(w_full)
    wkv = jnp.transpose(g_kv, (1, 0, 2, 3)).reshape(depth, D_MODEL, 2 * MEM_W)
    wout = jnp.transpose(g_out, (1, 0, 2, 3)).reshape(depth, MIX_W, D_MODEL)

    loss, grad_x, grads = _local_step(x[0], mem[0], norm_w, w_r, b_forget, mem_norm_w, wkv,
                                      out_norm_w, wout, final_norm_w, loss_target[0])
    loss = lax.psum(loss[0, 0], ("x", "y", "c"))

    p_in = _restore_w_in(grads["w_r"]).reshape(depth, D_MODEL, N_DEV, SHARD_W)
    p_in = jnp.transpose(p_in, (2, 0, 1, 3)).astype(BF16)
    p_kv = grads["w_mem_kv"].reshape(depth, N_DEV, kv_rows, 2 * MEM_W)
    p_kv = jnp.transpose(p_kv, (1, 0, 2, 3)).astype(BF16)
    p_out = grads["w_out"].reshape(depth, N_DEV, out_rows, D_MODEL)
    p_out = jnp.transpose(p_out, (1, 0, 2, 3)).astype(BF16)
    small = _pack_small(grads["norm_w"], grads["mem_norm_w"], grads["out_norm_w"],
                        grads["final_norm_w"], grads["b_forget"])
    p_small = jnp.broadcast_to(small[None, None], (N_DEV, 1, SMALL_ROWS, LANES))
    r_in, r_kv, r_out, r_small = _exchange_blocks([p_in, p_kv, p_out, p_small], "exchange_grads")

    g_w_in, d_w_in, nm_w_in, nv_w_in = _adamw_sum(r_in, w_in, m_w_in, v_w_in, 256, "adamw_w_in")
    g_w_kv, d_w_kv, nm_w_kv, nv_w_kv = _adamw_sum(r_kv, w_mem_kv, m_w_mem_kv, v_w_mem_kv, kv_rows,
                                                  "adamw_w_mem_kv")
    g_w_out, d_w_out, nm_w_out, nv_w_out = _adamw_sum(r_out, w_out, m_w_out, v_w_out, out_rows,
                                                      "adamw_w_out")
    w_small = _pack_small(norm_w, mem_norm_w, out_norm_w, final_norm_w, b_forget)[None]
    m_small = _pack_small(m_norm_w, m_mem_norm_w, m_out_norm_w, m_final_norm_w, m_b_forget)[None]
    v_small = _pack_small(v_norm_w, v_mem_norm_w, v_out_norm_w, v_final_norm_w, v_b_forget)[None]
    small_out = _adamw_sum(r_small, w_small, m_small, v_small, SMALL_ROWS, "adamw_small")
    (g_nw, g_mnw, g_onw, g_fnw, g_b), (d_nw, d_mnw, d_onw, d_fnw, d_b), \
        (nm_nw, nm_mnw, nm_onw, nm_fnw, nm_b), (nv_nw, nv_mnw, nv_onw, nv_fnw, nv_b) = [
            _unpack_small(t[0]) for t in small_out]

    return (loss, grad_x[None],
            g_nw, g_w_in, g_b, g_mnw, g_w_kv, g_onw, g_w_out, g_fnw,
            d_nw, d_w_in, d_b, d_mnw, d_w_kv, d_onw, d_w_out, d_fnw,
            nm_nw, nm_w_in, nm_b, nm_mnw, nm_w_kv, nm_onw, nm_w_out, nm_fnw,
            nv_nw, nv_w_in, nv_b, nv_mnw, nv_w_kv, nv_onw, nv_w_out, nv_fnw)
```

```python
import functools

import jax
import jax.numpy as jnp
from jax import lax
from jax.experimental import pallas as pl
from jax.experimental.pallas import tpu as pltpu

F32 = jnp.float32
BF16 = jnp.bfloat16

N_DEV = 8
D_MODEL = 1024
HEAD_DIM = 64
LANES = 128
SB_W = 512
FX_W = 512
MEM_W = 256
MIX_W = 1280
FOX_HEADS = 8
IN_W = 4616
SHARD_W = IN_W // N_DEV
QKV_W = 3 * SB_W + 3 * FX_W + MEM_W
FL_PAD = 256
GF_W = MIX_W + FL_PAD
WR_W = QKV_W + GF_W
EPS = 1e-6
T = 256
TM = 256
Q_SCALE = 0.125
NEG = -1e30

ADAM_LR = 0.001
ADAM_B1 = 0.9
ADAM_B2 = 0.999
ADAM_EPS = 1e-08
ADAM_WD = 0.01
ADAM_STEP = 10

SMALL_ROWS = 72

_NT = (((1,), (1,)), ((), ()))
_TN = (((0,), (0,)), ((), ()))

_ARB1 = pltpu.CompilerParams(dimension_semantics=("arbitrary",))
_ARB2 = pltpu.CompilerParams(dimension_semantics=("arbitrary", "arbitrary"))


def _dot(a, b):
    return jnp.dot(a, b, preferred_element_type=F32)


def _dot_nt(a, b):
    return lax.dot_general(a, b, _NT, preferred_element_type=F32)


def _dot_tn(a, b):
    return lax.dot_general(a, b, _TN, preferred_element_type=F32)


def _split2(x):
    hi = x.astype(BF16)
    lo = (x - hi.astype(F32)).astype(BF16)
    return hi, lo


def _stack2(u):
    return jnp.concatenate([u, u], axis=0)


def _cum2(x, u2):
    hi, lo = _split2(x)
    return _dot(jnp.concatenate([hi, lo], axis=1), u2)


def _tri3(tri, x):
    hi = x.astype(BF16)
    r1 = x - hi.astype(F32)
    mid = r1.astype(BF16)
    lo = (r1 - mid.astype(F32)).astype(BF16)
    return _dot(tri, hi) + _dot(tri, mid) + _dot(tri, lo)


def _iota2(shape, dim):
    return lax.broadcasted_iota(jnp.int32, shape, dim)


def _head_block_diag():
    r = _iota2((LANES, LANES), 0) // HEAD_DIM
    c = _iota2((LANES, LANES), 1) // HEAD_DIM
    return _stack2(jnp.where(r == c, 1.0, 0.0).astype(BF16))


def _head_mean(x, bd):
    return _cum2(x, bd) * (1.0 / HEAD_DIM)


def _sigmoid(x):
    return 1.0 / (1.0 + jnp.exp(-x))


def _log_sigmoid(x):
    return jnp.minimum(x, 0.0) - jnp.log(1.0 + jnp.exp(-jnp.abs(x)))


def _pair_masks():
    lane = _iota2((1, LANES), 1)
    return lane < HEAD_DIM


def _split_pair(x, m0):
    zero = jnp.zeros_like(x)
    return jnp.where(m0, x, zero), jnp.where(m0, zero, x)


def _inproj_fwd(x, nw, w_r, name):
    s = x.shape[0]

    def body(x_ref, nw_ref, w_ref, qkv_ref, gf_ref):
        xv = x_ref[...]
        r = lax.rsqrt(jnp.mean(xv * xv, axis=-1, keepdims=True) + EPS)
        h = (xv * r * nw_ref[...]).astype(BF16)
        for c in range(0, QKV_W, 256):
            qkv_ref[:, c:c + 256] = _dot(h, w_ref[:, c:c + 256]).astype(BF16)
        for c in range(0, GF_W, 256):
            gf_ref[:, c:c + 256] = _dot(h, w_ref[:, QKV_W + c:QKV_W + c + 256])

    return pl.pallas_call(
        body, name=name, grid=(s // TM,),
        in_specs=[pl.BlockSpec((TM, D_MODEL), lambda i: (i, 0)),
                  pl.BlockSpec((1, D_MODEL), lambda i: (0, 0)),
                  pl.BlockSpec((D_MODEL, WR_W), lambda i: (0, 0))],
        out_specs=[pl.BlockSpec((TM, QKV_W), lambda i: (i, 0)),
                   pl.BlockSpec((TM, GF_W), lambda i: (i, 0))],
        out_shape=[jax.ShapeDtypeStruct((s, QKV_W), BF16), jax.ShapeDtypeStruct((s, GF_W), F32)],
        compiler_params=_ARB1,
    )(x, nw, w_r)


def _fox_prep_fwd(gf, bpad, name):
    s = gf.shape[0]

    def body(fl_ref, b_ref, f_ref):
        tri = jnp.where(_iota2((T, T), 0) >= _iota2((T, T), 1), 1.0, 0.0).astype(BF16)
        carry = jnp.zeros((1, LANES), F32)
        for blk in range(s // T):
            lf = _log_sigmoid(fl_ref[blk * T:(blk + 1) * T, :] + b_ref[...])
            c = _tri3(tri, lf) + carry
            f_ref[blk * T:(blk + 1) * T, :] = c
            carry = c[T - 1:T, :]

    return pl.pallas_call(
        body, name=name, grid=(1,),
        in_specs=[pl.BlockSpec((s, LANES), lambda i: (0, MIX_W // LANES)),
                  pl.BlockSpec((1, LANES), lambda i: (0, 0))],
        out_specs=pl.BlockSpec((s, LANES), lambda i: (0, 0)),
        out_shape=jax.ShapeDtypeStruct((s, LANES), F32),
        compiler_params=_ARB1,
    )(gf, bpad)


def _sb_fwd(qkv, name):
    s = qkv.shape[0]

    def body(q_ref, k_ref, v_ref, o_ref, acc_ref, r_ref, as_ref):
        i = pl.program_id(1)
        m0 = _pair_masks()
        qh = _split_pair(q_ref[...] * jnp.asarray(Q_SCALE, BF16), m0)
        strict = _iota2((T, T), 0) > _iota2((T, T), 1)
        u2 = _stack2(jnp.where(strict, 1.0, 0.0).astype(BF16))
        acc_ref[...] = jnp.zeros_like(acc_ref)
        r_ref[...] = jnp.zeros_like(r_ref)
        hs = range(2)

        def flush(j):
            v = v_ref[pl.ds(pl.multiple_of(j * T, T), T), :]
            for h in hs:
                acc_ref[h] += _dot(as_ref[h], v)

        def tile(j, diag):
            k = k_ref[pl.ds(pl.multiple_of(j * T, T), T), :]
            z = [_dot_nt(qh[h], k) for h in hs]
            if not diag:
                flush(j + 1)
            la = [jnp.minimum(z[h], 0.0) - jnp.log(1.0 + jnp.exp(-jnp.abs(z[h]))) for h in hs]
            lf = [la[h] - z[h] for h in hs]
            if diag:
                lf = [jnp.where(strict, lf[h], 0.0) for h in hs]
            cin = [_cum2(lf[h], u2) for h in hs]
            a = [jnp.exp(la[h] + cin[h] + r_ref[h]) for h in hs]
            if diag:
                a = [jnp.where(strict, a[h], 0.0) for h in hs]
            for h in hs:
                r_ref[h] += cin[h][:, 0:1] + lf[h][:, 0:1]
                as_ref[h] = a[h].astype(BF16)

        tile(i, True)

        def step(n, carry):
            tile(i - 1 - n, False)
            return carry

        lax.fori_loop(0, i, step, 0)
        flush(0)
        o_ref[...] = jnp.where(m0, acc_ref[0], acc_ref[1])

    nb = SB_W // LANES
    return pl.pallas_call(
        body, name=name, grid=(nb, s // T),
        in_specs=[pl.BlockSpec((T, LANES), lambda p, i: (i, p)),
                  pl.BlockSpec((s, LANES), lambda p, i: (0, nb + p)),
                  pl.BlockSpec((s, LANES), lambda p, i: (0, 2 * nb + p))],
        out_specs=pl.BlockSpec((T, LANES), lambda p, i: (i, p)),
        out_shape=jax.ShapeDtypeStruct((s, SB_W), F32),
        scratch_shapes=[pltpu.VMEM((2, T, LANES), F32), pltpu.VMEM((2, T, 1), F32),
                        pltpu.VMEM((2, T, T), BF16)],
        compiler_params=_ARB2,
    )(qkv, qkv, qkv)


def _fox_fwd(qkv, fqb, frow, name):
    s = qkv.shape[0]

    def body(q_ref, k_ref, v_ref, fq_ref, fr_ref, o_ref, lse_ref, acc_ref, m_ref, ps_ref):
        i = pl.program_id(1)
        m0 = _pair_masks()
        qh = _split_pair(q_ref[...] * jnp.asarray(Q_SCALE, BF16), m0)
        fq = fq_ref[...]
        fqh = (fq[:, 0:1], fq[:, HEAD_DIM:HEAD_DIM + 1])
        causal = _iota2((T, T), 0) >= _iota2((T, T), 1)
        ones = jnp.ones((T, LANES), BF16)
        acc_ref[...] = jnp.zeros_like(acc_ref)
        m_ref[...] = jnp.full_like(m_ref, NEG)
        hs = range(2)

        def flush(j):
            v = v_ref[pl.ds(pl.multiple_of(j * T, T), T), :]
            va2 = _stack2(jnp.concatenate([v, ones], axis=1))
            for h in hs:
                acc_ref[h] += _dot(ps_ref[h], va2)

        def tile(j, diag):
            off = pl.multiple_of(j * T, T)
            k = k_ref[pl.ds(off, T), :]
            sc = [_dot_nt(qh[h], k) + fqh[h] - fr_ref[h:h + 1, pl.ds(off, T)] for h in hs]
            if not diag:
                flush(j + 1)
            if diag:
                sc = [jnp.where(causal, sc[h], NEG) for h in hs]
            m_new = [jnp.maximum(m_ref[h], jnp.max(sc[h], axis=1, keepdims=True)) for h in hs]
            p = [jnp.exp(sc[h] - m_new[h]) for h in hs]
            for h in hs:
                acc_ref[h] = acc_ref[h] * jnp.exp(m_ref[h] - m_new[h])
                m_ref[h] = m_new[h]
                p_hi, p_lo = _split2(p[h])
                ps_ref[h] = jnp.concatenate([p_hi, p_lo], axis=1)

        tile(i, True)

        def step(n, carry):
            tile(i - 1 - n, False)
            return carry

        lax.fori_loop(0, i, step, 0)
        flush(0)
        acc = (acc_ref[0], acc_ref[1])
        o_ref[...] = jnp.where(m0, acc[0][:, :LANES] / acc[0][:, LANES:],
                               acc[1][:, :LANES] / acc[1][:, LANES:])
        lse_ref[...] = jnp.where(m0, m_ref[0] + jnp.log(acc[0][:, LANES:]),
                                 m_ref[1] + jnp.log(acc[1][:, LANES:]))

    nb = FX_W // LANES
    base = 3 * SB_W // LANES
    return pl.pallas_call(
        body, name=name, grid=(nb, s // T),
        in_specs=[pl.BlockSpec((T, LANES), lambda p, i: (i, base + p)),
                  pl.BlockSpec((s, LANES), lambda p, i: (0, base + nb + p)),
                  pl.BlockSpec((s, LANES), lambda p, i: (0, base + 2 * nb + p)),
                  pl.BlockSpec((T, LANES), lambda p, i: (i, p)),
                  pl.BlockSpec((None, 2, s), lambda p, i: (p, 0, 0))],
        out_specs=[pl.BlockSpec((T, LANES), lambda p, i: (i, p)),
                   pl.BlockSpec((T, LANES), lambda p, i: (i, p))],
        out_shape=[jax.ShapeDtypeStruct((s, FX_W), F32), jax.ShapeDtypeStruct((s, FX_W), F32)],
        scratch_shapes=[pltpu.VMEM((2, T, 2 * LANES), F32), pltpu.VMEM((2, T, 1), F32),
                        pltpu.VMEM((2, T, 2 * T), BF16)],
        compiler_params=_ARB2,
    )(qkv, qkv, qkv, fqb, frow)


def _memkv_fwd(mem, mnw, wkv, name):
    n = mem.shape[0]

    def body(mem_ref, mnw_ref, w_ref, kv_ref):
        mv = mem_ref[...]
        r = lax.rsqrt(jnp.mean(mv * mv, axis=-1, keepdims=True) + EPS)
        hm = (mv * r * mnw_ref[...]).astype(BF16)
        kv_ref[...] = _dot(hm, w_ref[...]).astype(BF16)

    return pl.pallas_call(
        body, name=name, grid=(1,),
        in_specs=[pl.BlockSpec((n, D_MODEL), lambda i: (0, 0)),
                  pl.BlockSpec((1, D_MODEL), lambda i: (0, 0)),
                  pl.BlockSpec((D_MODEL, 2 * MEM_W), lambda i: (0, 0))],
        out_specs=pl.BlockSpec((n, 2 * MEM_W), lambda i: (0, 0)),
        out_shape=jax.ShapeDtypeStruct((n, 2 * MEM_W), BF16),
        compiler_params=_ARB1,
    )(mem, mnw, wkv)


def _mem_fwd(qkv, kv, name):
    s = qkv.shape[0]
    n = kv.shape[0]

    def body(q_ref, k_ref, v_ref, o_ref, lse_ref):
        m0 = _pair_masks()
        qh = _split_pair(q_ref[...] * jnp.asarray(Q_SCALE, BF16), m0)
        k = k_ref[...]
        v = v_ref[...]
        outs, lses = [], []
        for h in range(2):
            sc = _dot_nt(qh[h], k)
            mx = jnp.max(sc, axis=1, keepdims=True)
            p = jnp.exp(sc - mx)
            l = jnp.sum(p, axis=1, keepdims=True)
            outs.append(_dot(p.astype(BF16), v) / l)
            lses.append(mx + jnp.log(l))
        o_ref[...] = jnp.where(m0, outs[0], outs[1])
        lse_ref[...] = jnp.where(m0, lses[0], lses[1])

    nb = MEM_W // LANES
    base = (3 * SB_W + 3 * FX_W) // LANES
    return pl.pallas_call(
        body, name=name, grid=(nb, s // T),
        in_specs=[pl.BlockSpec((T, LANES), lambda p, i: (i, base + p)),
                  pl.BlockSpec((n, LANES), lambda p, i: (0, p)),
                  pl.BlockSpec((n, LANES), lambda p, i: (0, nb + p))],
        out_specs=[pl.BlockSpec((T, LANES), lambda p, i: (i, p)),
                   pl.BlockSpec((T, LANES), lambda p, i: (i, p))],
        out_shape=[jax.ShapeDtypeStruct((s, MEM_W), F32), jax.ShapeDtypeStruct((s, MEM_W), F32)],
        compiler_params=_ARB2,
    )(qkv, kv, kv)


def _mix_chunk(c, ysb_ref, yfx_ref, ym_ref):
    if c < SB_W // LANES:
        return ysb_ref[:, c * LANES:(c + 1) * LANES]
    c -= SB_W // LANES
    if c < FX_W // LANES:
        return yfx_ref[:, c * LANES:(c + 1) * LANES]
    c -= FX_W // LANES
    return ym_ref[:, c * LANES:(c + 1) * LANES]


def _outproj_fwd(ysb, yfx, ym, gf, onw, wout, x, name):
    s = x.shape[0]

    def body(ysb_ref, yfx_ref, ym_ref, g_ref, onw_ref, w_ref, x_ref, o_ref, yg_ref):
        bd = _head_block_diag()
        for c in range(MIX_W // LANES):
            sl = slice(c * LANES, (c + 1) * LANES)
            u = _mix_chunk(c, ysb_ref, yfx_ref, ym_ref)
            r = lax.rsqrt(_head_mean(u * u, bd) + EPS)
            g = g_ref[:, sl]
            yg_ref[:, sl] = (u * r * onw_ref[:, sl] * (g * _sigmoid(g))).astype(BF16)
        o_ref[...] = x_ref[...] + _dot(yg_ref[...], w_ref[...])

    return pl.pallas_call(
        body, name=name, grid=(s // TM,),
        in_specs=[pl.BlockSpec((TM, SB_W), lambda i: (i, 0)),
                  pl.BlockSpec((TM, FX_W), lambda i: (i, 0)),
                  pl.BlockSpec((TM, MEM_W), lambda i: (i, 0)),
                  pl.BlockSpec((TM, MIX_W), lambda i: (i, 0)),
                  pl.BlockSpec((1, MIX_W), lambda i: (0, 0)),
                  pl.BlockSpec((MIX_W, D_MODEL), lambda i: (0, 0)),
                  pl.BlockSpec((TM, D_MODEL), lambda i: (i, 0))],
        out_specs=pl.BlockSpec((TM, D_MODEL), lambda i: (i, 0)),
        out_shape=jax.ShapeDtypeStruct((s, D_MODEL), F32),
        scratch_shapes=[pltpu.VMEM((TM, MIX_W), BF16)],
        compiler_params=_ARB1,
    )(ysb, yfx, ym, gf, onw, wout, x)


def _final_fwd_bwd(x, fnw, target, name):
    s = x.shape[0]

    def body(x_ref, w_ref, t_ref, dx_ref, loss_ref, dw_ref):
        @pl.when(pl.program_id(0) == 0)
        def _():
            loss_ref[...] = jnp.zeros_like(loss_ref)
            dw_ref[...] = jnp.zeros_like(dw_ref)

        xv = x_ref[...]
        w = w_ref[...]
        r = lax.rsqrt(jnp.mean(xv * xv, axis=-1, keepdims=True) + EPS)
        xh = xv * r
        err = xh * w - t_ref[...]
        part = jnp.sum(jnp.sum(err * err, axis=1, keepdims=True), axis=0, keepdims=True)
        loss_ref[...] += part * (0.5 / D_MODEL)
        dy = err * (1.0 / D_MODEL)
        dw_ref[...] += jnp.sum(dy * xh, axis=0, keepdims=True)
        dxh = dy * w
        dx_ref[...] = r * (dxh - xh * jnp.mean(dxh * xh, axis=-1, keepdims=True))

    return pl.pallas_call(
        body, name=name, grid=(s // TM,),
        in_specs=[pl.BlockSpec((TM, D_MODEL), lambda i: (i, 0)),
                  pl.BlockSpec((1, D_MODEL), lambda i: (0, 0)),
                  pl.BlockSpec((TM, D_MODEL), lambda i: (i, 0))],
        out_specs=[pl.BlockSpec((TM, D_MODEL), lambda i: (i, 0)),
                   pl.BlockSpec((1, LANES), lambda i: (0, 0)),
                   pl.BlockSpec((1, D_MODEL), lambda i: (0, 0))],
        out_shape=[jax.ShapeDtypeStruct((s, D_MODEL), F32), jax.ShapeDtypeStruct((1, LANES), F32),
                   jax.ShapeDtypeStruct((1, D_MODEL), F32)],
        compiler_params=_ARB1,
    )(x, fnw, target)


def _outproj_bwd(dxo, wout, ysb, yfx, ym, gf, onw, name):
    s = dxo.shape[0]

    def body(dx_ref, w_ref, ysb_ref, yfx_ref, ym_ref, g_ref, onw_ref,
             dysb_ref, dyfx_ref, dym_ref, dg_ref, dw_ref, donw_ref, yg_ref):
        @pl.when(pl.program_id(0) == 0)
        def _():
            dw_ref[...] = jnp.zeros_like(dw_ref)
            donw_ref[...] = jnp.zeros_like(donw_ref)

        dxb = dx_ref[...].astype(BF16)
        dyg = _dot_nt(dxb, w_ref[...])
        bd = _head_block_diag()
        for c in range(MIX_W // LANES):
            sl = slice(c * LANES, (c + 1) * LANES)
            u = _mix_chunk(c, ysb_ref, yfx_ref, ym_ref)
            r = lax.rsqrt(_head_mean(u * u, bd) + EPS)
            yn = u * r
            g = g_ref[:, sl]
            sg = _sigmoid(g)
            sil = g * sg
            onw = onw_ref[:, sl]
            e = dyg[:, sl]
            yg_ref[:, sl] = (yn * onw * sil).astype(BF16)
            donw_ref[:, sl] += jnp.sum(e * yn * sil, axis=0, keepdims=True)
            dg_ref[:, sl] = (e * yn * onw * (sg * (1.0 + g * (1.0 - sg)))).astype(BF16)
            dyn = e * onw * sil
            du = (r * (dyn - yn * _head_mean(dyn * yn, bd))).astype(BF16)
            if c < 4:
                dysb_ref[:, c * LANES:(c + 1) * LANES] = du
            elif c < 8:
                dyfx_ref[:, (c - 4) * LANES:(c - 3) * LANES] = du
            else:
                dym_ref[:, (c - 8) * LANES:(c - 7) * LANES] = du
        dw_ref[...] += _dot_tn(yg_ref[...], dxb)

    return pl.pallas_call(
        body, name=name, grid=(s // TM,),
        in_specs=[pl.BlockSpec((TM, D_MODEL), lambda i: (i, 0)),
                  pl.BlockSpec((MIX_W, D_MODEL), lambda i: (0, 0)),
                  pl.BlockSpec((TM, SB_W), lambda i: (i, 0)),
                  pl.BlockSpec((TM, FX_W), lambda i: (i, 0)),
                  pl.BlockSpec((TM, MEM_W), lambda i: (i, 0)),
                  pl.BlockSpec((TM, MIX_W), lambda i: (i, 0)),
                  pl.BlockSpec((1, MIX_W), lambda i: (0, 0))],
        out_specs=[pl.BlockSpec((TM, SB_W), lambda i: (i, 0)),
                   pl.BlockSpec((TM, FX_W), lambda i: (i, 0)),
                   pl.BlockSpec((TM, MEM_W), lambda i: (i, 0)),
                   pl.BlockSpec((TM, MIX_W), lambda i: (i, 0)),
                   pl.BlockSpec((MIX_W, D_MODEL), lambda i: (0, 0)),
                   pl.BlockSpec((1, MIX_W), lambda i: (0, 0))],
        out_shape=[jax.ShapeDtypeStruct((s, SB_W), BF16), jax.ShapeDtypeStruct((s, FX_W), BF16),
                   jax.ShapeDtypeStruct((s, MEM_W), BF16), jax.ShapeDtypeStruct((s, MIX_W), BF16),
                   jax.ShapeDtypeStruct((MIX_W, D_MODEL), F32), jax.ShapeDtypeStruct((1, MIX_W), F32)],
        scratch_shapes=[pltpu.VMEM((TM, MIX_W), BF16)],
        compiler_params=_ARB1,
    )(dxo, wout, ysb, yfx, ym, gf, onw)


def _row_dots(do_ref, o_ref, m0):
    prod = do_ref[...].astype(F32) * o_ref[...]
    zero = jnp.zeros_like(prod)
    return (jnp.sum(jnp.where(m0, prod, zero), axis=1, keepdims=True),
            jnp.sum(jnp.where(m0, zero, prod), axis=1, keepdims=True))


def _sb_bwd(qkv, o, do, name):
    s = qkv.shape[0]
    nq = s // T

    def body(q_ref, k_ref, v_ref, o_ref, do_ref, dq_ref, dk_ref, dv_ref,
             dqa_ref, dka_ref, dva_ref, rl_ref, rg_ref, dzs_ref, abs_ref):
        i = pl.program_id(1)

        @pl.when(i == 0)
        def _():
            dka_ref[...] = jnp.zeros_like(dka_ref)
            dva_ref[...] = jnp.zeros_like(dva_ref)

        m0 = _pair_masks()
        qh = _split_pair(q_ref[...] * jnp.asarray(Q_SCALE, BF16), m0)
        doh = _split_pair(do_ref[...], m0)
        dsum = _row_dots(do_ref, o_ref, m0)
        strict = _iota2((T, T), 0) > _iota2((T, T), 1)
        u2 = _stack2(jnp.where(strict, 1.0, 0.0).astype(BF16))
        dqa_ref[...] = jnp.zeros_like(dqa_ref)
        rl_ref[...] = jnp.zeros_like(rl_ref)
        rg_ref[...] = jnp.zeros_like(rg_ref)

        hs = range(2)

        def flush(j):
            off = pl.multiple_of(j * T, T)
            k = k_ref[pl.ds(off, T), :]
            for h in hs:
                dqa_ref[h] += _dot(dzs_ref[h], k)
            dka_ref[pl.ds(off, T), :] += _dot_tn(dzs_ref[0], qh[0]) + _dot_tn(dzs_ref[1], qh[1])
            dva_ref[pl.ds(off, T), :] += _dot_tn(abs_ref[0], doh[0]) + _dot_tn(abs_ref[1], doh[1])

        def tile(j, diag):
            off = pl.multiple_of(j * T, T)
            k = k_ref[pl.ds(off, T), :]
            v = v_ref[pl.ds(off, T), :]
            z = [_dot_nt(qh[h], k) for h in hs]
            da = [_dot_nt(doh[h], v) for h in hs]
            if not diag:
                flush(j + 1)
            la = [jnp.minimum(z[h], 0.0) - jnp.log(1.0 + jnp.exp(-jnp.abs(z[h]))) for h in hs]
            lf = [la[h] - z[h] for h in hs]
            if diag:
                lf = [jnp.where(strict, lf[h], 0.0) for h in hs]
            cin = [_cum2(lf[h], u2) for h in hs]
            a = [jnp.exp(la[h] + cin[h] + rl_ref[h]) for h in hs]
            if diag:
                a = [jnp.where(strict, a[h], 0.0) for h in hs]
            ab = [a[h].astype(BF16) for h in hs]
            g = [ab[h].astype(F32) * da[h] for h in hs]
            gin = [_cum2(g[h], u2) for h in hs]
            dz = [g[h] - jnp.exp(la[h]) * ((dsum[h] - rg_ref[h]) - gin[h]) for h in hs]
            if diag:
                dz = [jnp.where(strict, dz[h], 0.0) for h in hs]
            for h in hs:
                rl_ref[h] += cin[h][:, 0:1] + lf[h][:, 0:1]
                rg_ref[h] += gin[h][:, 0:1] + g[h][:, 0:1]
                dzs_ref[h] = dz[h].astype(BF16)
                abs_ref[h] = ab[h]

        tile(i, True)

        def step(n, carry):
            tile(i - 1 - n, False)
            return carry

        lax.fori_loop(0, i, step, 0)
        flush(0)
        dq_ref[...] = (jnp.where(m0, dqa_ref[0], dqa_ref[1]) * Q_SCALE).astype(BF16)

        @pl.when(i == nq - 1)
        def _():
            dk_ref[...] = dka_ref[...].astype(BF16)
            dv_ref[...] = dva_ref[...].astype(BF16)

    nb = SB_W // LANES
    return pl.pallas_call(
        body, name=name, grid=(nb, nq),
        in_specs=[pl.BlockSpec((T, LANES), lambda p, i: (i, p)),
                  pl.BlockSpec((s, LANES), lambda p, i: (0, nb + p)),
                  pl.BlockSpec((s, LANES), lambda p, i: (0, 2 * nb + p)),
                  pl.BlockSpec((T, LANES), lambda p, i: (i, p)),
                  pl.BlockSpec((T, LANES), lambda p, i: (i, p))],
        out_specs=[pl.BlockSpec((T, LANES), lambda p, i: (i, p)),
                   pl.BlockSpec((s, LANES), lambda p, i: (0, p)),
                   pl.BlockSpec((s, LANES), lambda p, i: (0, p))],
        out_shape=[jax.ShapeDtypeStruct((s, SB_W), BF16)] * 3,
        scratch_shapes=[pltpu.VMEM((2, T, LANES), F32), pltpu.VMEM((s, LANES), F32),
                        pltpu.VMEM((s, LANES), F32), pltpu.VMEM((2, T, 1), F32),
                        pltpu.VMEM((2, T, 1), F32), pltpu.VMEM((2, T, T), BF16),
                        pltpu.VMEM((2, T, T), BF16)],
        compiler_params=_ARB2,
    )(qkv, qkv, qkv, o, do)


def _fox_bwd(qkv, fqb, frow, o, lse, do, name):
    s = qkv.shape[0]
    nq = s // T

    def body(q_ref, k_ref, v_ref, fq_ref, fr_ref, o_ref, lse_ref, do_ref,
             dq_ref, dk_ref, dv_ref, df_ref, dqa_ref, dka_ref, dva_ref, dfa_ref, dls_ref, pbs_ref):
        i = pl.program_id(1)

        @pl.when(i == 0)
        def _():
            dka_ref[...] = jnp.zeros_like(dka_ref)
            dva_ref[...] = jnp.zeros_like(dva_ref)
            dfa_ref[...] = jnp.zeros_like(dfa_ref)

        m0 = _pair_masks()
        qh = _split_pair(q_ref[...] * jnp.asarray(Q_SCALE, BF16), m0)
        doh = _split_pair(do_ref[...], m0)
        dsum = _row_dots(do_ref, o_ref, m0)
        fq = fq_ref[...]
        fqh = (fq[:, 0:1], fq[:, HEAD_DIM:HEAD_DIM + 1])
        lse = lse_ref[...]
        lseh = (lse[:, 0:1], lse[:, HEAD_DIM:HEAD_DIM + 1])
        causal = _iota2((T, T), 0) >= _iota2((T, T), 1)
        dqa_ref[...] = jnp.zeros_like(dqa_ref)

        hs = range(2)

        def flush(j):
            off = pl.multiple_of(j * T, T)
            k = k_ref[pl.ds(off, T), :]
            for h in hs:
                dqa_ref[h] += _dot(dls_ref[h], k)
            dka_ref[pl.ds(off, T), :] += _dot_tn(dls_ref[0], qh[0]) + _dot_tn(dls_ref[1], qh[1])
            dva_ref[pl.ds(off, T), :] += _dot_tn(pbs_ref[0], doh[0]) + _dot_tn(pbs_ref[1], doh[1])

        def tile(j, diag):
            off = pl.multiple_of(j * T, T)
            k = k_ref[pl.ds(off, T), :]
            v = v_ref[pl.ds(off, T), :]
            sc = [_dot_nt(qh[h], k) + fqh[h] - fr_ref[h:h + 1, pl.ds(off, T)] for h in hs]
            dp = [_dot_nt(doh[h], v) for h in hs]
            if not diag:
                flush(j + 1)
            p = [jnp.exp(sc[h] - lseh[h]) for h in hs]
            if diag:
                p = [jnp.where(causal, p[h], 0.0) for h in hs]
            dl = [p[h] * (dp[h] - dsum[h]) for h in hs]
            for h in hs:
                dls_ref[h] = dl[h].astype(BF16)
                pbs_ref[h] = p[h].astype(BF16)
                dfa_ref[h:h + 1, pl.ds(off, T)] -= jnp.sum(dl[h], axis=0, keepdims=True)

        tile(i, True)

        def step(n, carry):
            tile(i - 1 - n, False)
            return carry

        lax.fori_loop(0, i, step, 0)
        flush(0)
        dq_ref[...] = (jnp.where(m0, dqa_ref[0], dqa_ref[1]) * Q_SCALE).astype(BF16)

        @pl.when(i == nq - 1)
        def _():
            dk_ref[...] = dka_ref[...].astype(BF16)
            dv_ref[...] = dva_ref[...].astype(BF16)
            df_ref[...] = dfa_ref[...]

    nb = FX_W // LANES
    base = 3 * SB_W // LANES
    return pl.pallas_call(
        body, name=name, grid=(nb, nq),
        in_specs=[pl.BlockSpec((T, LANES), lambda p, i: (i, base + p)),
                  pl.BlockSpec((s, LANES), lambda p, i: (0, base + nb + p)),
                  pl.BlockSpec((s, LANES), lambda p, i: (0, base + 2 * nb + p)),
                  pl.BlockSpec((T, LANES), lambda p, i: (i, p)),
                  pl.BlockSpec((None, 2, s), lambda p, i: (p, 0, 0)),
                  pl.BlockSpec((T, LANES), lambda p, i: (i, p)),
                  pl.BlockSpec((T, LANES), lambda p, i: (i, p)),
                  pl.BlockSpec((T, LANES), lambda p, i: (i, p))],
        out_specs=[pl.BlockSpec((T, LANES), lambda p, i: (i, p)),
                   pl.BlockSpec((s, LANES), lambda p, i: (0, p)),
                   pl.BlockSpec((s, LANES), lambda p, i: (0, p)),
                   pl.BlockSpec((None, 2, s), lambda p, i: (p, 0, 0))],
        out_shape=[jax.ShapeDtypeStruct((s, FX_W), BF16)] * 3
        + [jax.ShapeDtypeStruct((nb, 2, s), F32)],
        scratch_shapes=[pltpu.VMEM((2, T, LANES), F32), pltpu.VMEM((s, LANES), F32),
                        pltpu.VMEM((s, LANES), F32), pltpu.VMEM((2, s), F32),
                        pltpu.VMEM((2, T, T), BF16), pltpu.VMEM((2, T, T), BF16)],
        compiler_params=_ARB2,
    )(qkv, qkv, qkv, fqb, frow, o, lse, do)


def _fox_prep_bwd(dfcol, gf, bpad, name):
    s = gf.shape[0]

    def body(df_ref, fl_ref, b_ref, dfl_ref, db_ref):
        tri = jnp.where(_iota2((T, T), 0) <= _iota2((T, T), 1), 1.0, 0.0).astype(BF16)
        carry = jnp.zeros((1, LANES), F32)
        db = jnp.zeros((1, LANES), F32)
        for blk in reversed(range(s // T)):
            rows = slice(blk * T, (blk + 1) * T)
            c = _tri3(tri, df_ref[rows, :]) + carry
            carry = c[0:1, :]
            dfl = c / (1.0 + jnp.exp(fl_ref[rows, :] + b_ref[...]))
            dfl_ref[rows, :] = dfl.astype(BF16)
            db = db + jnp.sum(dfl, axis=0, keepdims=True)
        db_ref[...] = db

    return pl.pallas_call(
        body, name=name, grid=(1,),
        in_specs=[pl.BlockSpec((s, LANES), lambda i: (0, 0)),
                  pl.BlockSpec((s, LANES), lambda i: (0, MIX_W // LANES)),
                  pl.BlockSpec((1, LANES), lambda i: (0, 0))],
        out_specs=[pl.BlockSpec((s, LANES), lambda i: (0, 0)),
                   pl.BlockSpec((1, LANES), lambda i: (0, 0))],
        out_shape=[jax.ShapeDtypeStruct((s, LANES), BF16), jax.ShapeDtypeStruct((1, LANES), F32)],
        compiler_params=_ARB1,
    )(dfcol, gf, bpad)


def _mem_bwd(qkv, kv, o, lse, do, name):
    s = qkv.shape[0]
    n = kv.shape[0]
    nq = s // T

    def body(q_ref, k_ref, v_ref, o_ref, lse_ref, do_ref, dq_ref, dk_ref, dv_ref):
        @pl.when(pl.program_id(1) == 0)
        def _():
            dk_ref[...] = jnp.zeros_like(dk_ref)
            dv_ref[...] = jnp.zeros_like(dv_ref)

        m0 = _pair_masks()
        qh = _split_pair(q_ref[...] * jnp.asarray(Q_SCALE, BF16), m0)
        doh = _split_pair(do_ref[...], m0)
        dsum = _row_dots(do_ref, o_ref, m0)
        lse = lse_ref[...]
        lseh = (lse[:, 0:1], lse[:, HEAD_DIM:HEAD_DIM + 1])
        k = k_ref[...]
        v = v_ref[...]
        dqs = []
        for h in range(2):
            p = jnp.exp(_dot_nt(qh[h], k) - lseh[h])
            dl = p * (_dot_nt(doh[h], v) - dsum[h])
            dlb = dl.astype(BF16)
            dqs.append(_dot(dlb, k))
            dk_ref[...] += _dot_tn(dlb, qh[h])
            dv_ref[...] += _dot_tn(p.astype(BF16), doh[h])
        dq_ref[...] = (jnp.where(m0, dqs[0], dqs[1]) * Q_SCALE).astype(BF16)

    nb = MEM_W // LANES
    base = (3 * SB_W + 3 * FX_W) // LANES
    return pl.pallas_call(
        body, name=name, grid=(nb, nq),
        in_specs=[pl.BlockSpec((T, LANES), lambda p, i: (i, base + p)),
                  pl.BlockSpec((n, LANES), lambda p, i: (0, p)),
                  pl.BlockSpec((n, LANES), lambda p, i: (0, nb + p)),
                  pl.BlockSpec((T, LANES), lambda p, i: (i, p)),
                  pl.BlockSpec((T, LANES), lambda p, i: (i, p)),
                  pl.BlockSpec((T, LANES), lambda p, i: (i, p))],
        out_specs=[pl.BlockSpec((T, LANES), lambda p, i: (i, p)),
                   pl.BlockSpec((n, LANES), lambda p, i: (0, p)),
                   pl.BlockSpec((n, LANES), lambda p, i: (0, p))],
        out_shape=[jax.ShapeDtypeStruct((s, MEM_W), BF16), jax.ShapeDtypeStruct((n, MEM_W), F32),
                   jax.ShapeDtypeStruct((n, MEM_W), F32)],
        compiler_params=_ARB2,
    )(qkv, kv, kv, o, lse, do)


def _memkv_bwd(mem, mnw, wkv, dk, dv, name):
    n = mem.shape[0]

    def body(mem_ref, mnw_ref, w_ref, dk_ref, dv_ref, dw_ref, dmnw_ref):
        mv = mem_ref[...]
        r = lax.rsqrt(jnp.mean(mv * mv, axis=-1, keepdims=True) + EPS)
        mh = mv * r
        hm = (mh * mnw_ref[...]).astype(BF16)
        dkv = jnp.concatenate([dk_ref[...], dv_ref[...]], axis=1).astype(BF16)
        dw_ref[...] = _dot_tn(hm, dkv)
        dhm = _dot_nt(dkv, w_ref[...])
        dmnw_ref[...] = jnp.sum(dhm * mh, axis=0, keepdims=True)

    return pl.pallas_call(
        body, name=name, grid=(1,),
        in_specs=[pl.BlockSpec((n, D_MODEL), lambda i: (0, 0)),
                  pl.BlockSpec((1, D_MODEL), lambda i: (0, 0)),
                  pl.BlockSpec((D_MODEL, 2 * MEM_W), lambda i: (0, 0)),
                  pl.BlockSpec((n, MEM_W), lambda i: (0, 0)),
                  pl.BlockSpec((n, MEM_W), lambda i: (0, 0))],
        out_specs=[pl.BlockSpec((D_MODEL, 2 * MEM_W), lambda i: (0, 0)),
                   pl.BlockSpec((1, D_MODEL), lambda i: (0, 0))],
        out_shape=[jax.ShapeDtypeStruct((D_MODEL, 2 * MEM_W), F32),
                   jax.ShapeDtypeStruct((1, D_MODEL), F32)],
        compiler_params=_ARB1,
    )(mem, mnw, wkv, dk, dv)


def _inproj_bwd_dx(dproj, w_r, x, nw, dxo, name):
    s = x.shape[0]

    def body(dp_ref, w_ref, x_ref, nw_ref, dxo_ref, dx_ref, ht_ref, dnw_ref):
        @pl.when(pl.program_id(0) == 0)
        def _():
            dnw_ref[...] = jnp.zeros_like(dnw_ref)

        dh = _dot_nt(dp_ref[...], w_ref[...])
        xv = x_ref[...]
        nw = nw_ref[...]
        r = lax.rsqrt(jnp.mean(xv * xv, axis=-1, keepdims=True) + EPS)
        xh = xv * r
        ht_ref[...] = (xh * nw).T.astype(BF16)
        dnw_ref[...] += jnp.sum(dh * xh, axis=0, keepdims=True)
        dxh = dh * nw
        dx_ref[...] = r * (dxh - xh * jnp.mean(dxh * xh, axis=-1, keepdims=True)) + dxo_ref[...]

    return pl.pallas_call(
        body, name=name, grid=(s // TM,),
        in_specs=[pl.BlockSpec((TM, WR_W), lambda i: (i, 0)),
                  pl.BlockSpec((D_MODEL, WR_W), lambda i: (0, 0)),
                  pl.BlockSpec((TM, D_MODEL), lambda i: (i, 0)),
                  pl.BlockSpec((1, D_MODEL), lambda i: (0, 0)),
                  pl.BlockSpec((TM, D_MODEL), lambda i: (i, 0))],
        out_specs=[pl.BlockSpec((TM, D_MODEL), lambda i: (i, 0)),
                   pl.BlockSpec((D_MODEL, TM), lambda i: (0, i)),
                   pl.BlockSpec((1, D_MODEL), lambda i: (0, 0))],
        out_shape=[jax.ShapeDtypeStruct((s, D_MODEL), F32), jax.ShapeDtypeStruct((D_MODEL, s), BF16),
                   jax.ShapeDtypeStruct((1, D_MODEL), F32)],
        compiler_params=_ARB1,
    )(dproj, w_r, x, nw, dxo)


def _inproj_bwd_dw(ht, dproj, name):
    s = dproj.shape[0]
    tn = 256

    def body(ht_ref, dp_ref, dw_ref):
        dw_ref[...] = _dot(ht_ref[...], dp_ref[...])

    return pl.pallas_call(
        body, name=name, grid=(WR_W // tn,),
        in_specs=[pl.BlockSpec((D_MODEL, s), lambda j: (0, 0)),
                  pl.BlockSpec((s, tn), lambda j: (0, j))],
        out_specs=pl.BlockSpec((D_MODEL, tn), lambda j: (0, j)),
        out_shape=jax.ShapeDtypeStruct((D_MODEL, WR_W), F32),
        compiler_params=_ARB1,
    )(ht, dproj)


def _rearrange_w_in(w):
    pad = jnp.zeros(w.shape[:-1] + (FL_PAD - FOX_HEADS,), w.dtype)
    return jnp.concatenate(
        [w[..., :3072], w[..., 3080:3336], w[..., 3336:IN_W], w[..., 3072:3080], pad], axis=-1)


def _restore_w_in(g):
    gate0 = QKV_W
    fl0 = QKV_W + MIX_W
    return jnp.concatenate(
        [g[..., :3072], g[..., fl0:fl0 + FOX_HEADS], g[..., 3072:QKV_W], g[..., gate0:fl0]], axis=-1)


def _pad_lanes(v, width=LANES):
    return jnp.pad(v, (0, width - v.shape[0])).reshape(1, width)


def _local_step(x, mem, norm_w, w_r, b_forget, mem_norm_w, wkv, out_norm_w, wout, final_norm_w,
                target):
    depth = w_r.shape[0]
    s = x.shape[0]
    saved = []
    xs = x
    for l in range(depth):
        nw = norm_w[l].reshape(1, D_MODEL)
        mnw = mem_norm_w[l].reshape(1, D_MODEL)
        onw = out_norm_w[l].reshape(1, MIX_W)
        bpad = _pad_lanes(b_forget[l])
        qkv, gf = _inproj_fwd(xs, nw, w_r[l], f"inproj_fwd_{l}")
        f = _fox_prep_fwd(gf, bpad, f"fox_prep_fwd_{l}")
        f8 = f[:, :FOX_HEADS]
        fqb = jnp.repeat(f8, HEAD_DIM, axis=1)
        frow = f8.T.reshape(FOX_HEADS // 2, 2, s)
        ysb = _sb_fwd(qkv, f"sb_fwd_{l}")
        yfx, lse_fx = _fox_fwd(qkv, fqb, frow, f"fox_fwd_{l}")
        kv = _memkv_fwd(mem, mnw, wkv[l], f"memkv_fwd_{l}")
        ym, lse_m = _mem_fwd(qkv, kv, f"mem_fwd_{l}")
        xn = _outproj_fwd(ysb, yfx, ym, gf, onw, wout[l], xs, f"outproj_fwd_{l}")
        saved.append((xs, nw, mnw, onw, bpad, qkv, gf, fqb, frow, ysb, yfx, lse_fx, kv, ym, lse_m))
        xs = xn

    dx, loss, dfnw = _final_fwd_bwd(xs, final_norm_w.reshape(1, D_MODEL), target, "final_fwd_bwd")

    g_nw, g_wr, g_b, g_mnw, g_wkv, g_onw, g_wout = [], [], [], [], [], [], []
    for l in reversed(range(depth)):
        xs, nw, mnw, onw, bpad, qkv, gf, fqb, frow, ysb, yfx, lse_fx, kv, ym, lse_m = saved[l]
        dysb, dyfx, dym, dgate, dwout, donw = _outproj_bwd(
            dx, wout[l], ysb, yfx, ym, gf, onw, f"outproj_bwd_{l}")
        sdq, sdk, sdv = _sb_bwd(qkv, ysb, dysb, f"sb_bwd_{l}")
        fdq, fdk, fdv, dfrow = _fox_bwd(qkv, fqb, frow, yfx, lse_fx, dyfx, f"fox_bwd_{l}")
        dfcol = jnp.pad(dfrow.reshape(FOX_HEADS, s).T, ((0, 0), (0, LANES - FOX_HEADS)))
        dfl, db = _fox_prep_bwd(dfcol, gf, bpad, f"fox_prep_bwd_{l}")
        dmq, dmk, dmv = _mem_bwd(qkv, kv, ym, lse_m, dym, f"mem_bwd_{l}")
        dwkv, dmnw = _memkv_bwd(mem, mnw, wkv[l], dmk, dmv, f"memkv_bwd_{l}")
        dproj = jnp.concatenate(
            [sdq, sdk, sdv, fdq, fdk, fdv, dmq, dgate, dfl, jnp.zeros((s, FL_PAD - LANES), BF16)],
            axis=1)
        dx, ht, dnw = _inproj_bwd_dx(dproj, w_r[l], xs, nw, dx, f"inproj_bwd_dx_{l}")
        dwr = _inproj_bwd_dw(ht, dproj, f"inproj_bwd_dw_{l}")
        g_nw.append(dnw[0])
        g_wr.append(dwr)
        g_b.append(db[0, :FOX_HEADS])
        g_mnw.append(dmnw[0])
        g_wkv.append(dwkv)
        g_onw.append(donw[0])
        g_wout.append(dwout)

    def stack(parts):
        return jnp.stack(parts[::-1])

    grads = dict(norm_w=stack(g_nw), w_r=stack(g_wr), b_forget=stack(g_b), mem_norm_w=stack(g_mnw),
                 w_mem_kv=stack(g_wkv), out_norm_w=stack(g_onw), w_out=stack(g_wout),
                 final_norm_w=dfnw[0])
    return loss, dx, grads


_ANY = pl.BlockSpec(memory_space=pl.ANY)


def _my_place():
    return lax.axis_index("x"), lax.axis_index("y"), lax.axis_index("c")


def _flip(v, bit):
    return 1 - v if bit else v


def _block_index(px, py, pc):
    return 4 * px + 2 * py + pc


def _all_gather_weights(shards, name):
    n = len(shards)

    def body(*refs):
        ins, outs = refs[:n], refs[n:2 * n]
        send_sems, recv_sems, local_sems = refs[2 * n:]
        x, y, c = _my_place()
        me = (x, y, c)
        sibling = (x, y, 1 - c)
        chips = [(1 - x, y), (x, 1 - y), (1 - x, 1 - y)]

        def copy(a, k, block, to, src=None):
            dst = outs[a].at[_block_index(*block)]
            return pltpu.make_async_remote_copy(
                src_ref=dst if src is None else src, dst_ref=dst,
                send_sem=send_sems.at[a, k], recv_sem=recv_sems.at[a, k],
                device_id=to, device_id_type=pl.DeviceIdType.MESH)

        mine = [pltpu.make_async_copy(ins[a], outs[a].at[_block_index(*me)], local_sems.at[a])
                for a in range(n)]
        for cp in mine:
            cp.start()
        first = []
        for a in range(n):
            first.append(copy(a, 0, me, sibling, src=ins[a]))
            first += [copy(a, 1 + j, me, (*chip, c), src=ins[a]) for j, chip in enumerate(chips)]
        for cp in first:
            cp.start()
        passed = []
        for j, chip in enumerate(chips):
            for a in range(n):
                copy(a, 1 + j, (*chip, c), me).wait_recv()
                fwd = copy(a, 4 + j, (*chip, c), sibling)
                fwd.start()
                passed.append(fwd)
        for a in range(n):
            copy(a, 0, sibling, me).wait_recv()
            for j, chip in enumerate(chips):
                copy(a, 4 + j, (*chip, 1 - c), me).wait_recv()
        for cp in first + passed:
            cp.wait_send()
        for cp in mine:
            cp.wait()

    return pl.pallas_call(
        body, name=name,
        in_specs=[_ANY] * n, out_specs=[_ANY] * n,
        out_shape=[jax.ShapeDtypeStruct((N_DEV,) + v.shape, v.dtype) for v in shards],
        scratch_shapes=[pltpu.SemaphoreType.DMA((n, 7)), pltpu.SemaphoreType.DMA((n, 7)),
                        pltpu.SemaphoreType.DMA((n,))],
    )(*shards)


def _exchange_blocks(blocked, name):
    n = len(blocked)

    def body(*refs):
        ins, outs = refs[:n], refs[n:2 * n]
        send_sems, recv_sems, local_sems = refs[2 * n:]
        x, y, c = _my_place()
        mine_idx = _block_index(x, y, c)
        local = [pltpu.make_async_copy(ins[a].at[mine_idx], outs[a].at[mine_idx], local_sems.at[a])
                 for a in range(n)]
        for cp in local:
            cp.start()
        sends, arrivals = [], []
        for r in range(1, N_DEV):
            peer = (_flip(x, r & 4), _flip(y, r & 2), _flip(c, r & 1))
            peer_idx = _block_index(*peer)
            for a in range(n):
                sems = dict(send_sem=send_sems.at[a, r - 1], recv_sem=recv_sems.at[a, r - 1],
                            device_id=peer, device_id_type=pl.DeviceIdType.MESH)
                sends.append(pltpu.make_async_remote_copy(
                    src_ref=ins[a].at[peer_idx], dst_ref=outs[a].at[mine_idx], **sems))
                arrivals.append(pltpu.make_async_remote_copy(
                    src_ref=ins[a].at[peer_idx], dst_ref=outs[a].at[peer_idx], **sems))
        for cp in sends:
            cp.start()
        for cp in arrivals:
            cp.wait_recv()
        for cp in sends:
            cp.wait_send()
        for cp in local:
            cp.wait()

    return pl.pallas_call(
        body, name=name,
        in_specs=[_ANY] * n, out_specs=[_ANY] * n,
        out_shape=[jax.ShapeDtypeStruct(v.shape, v.dtype) for v in blocked],
        scratch_shapes=[pltpu.SemaphoreType.DMA((n, 7)), pltpu.SemaphoreType.DMA((n, 7)),
                        pltpu.SemaphoreType.DMA((n,))],
    )(*blocked)


def _adamw_sum(parts, w, m, v, rows, name):
    depth, nrow, ncol = w.shape
    c1 = 1.0 / (1.0 - ADAM_B1 ** ADAM_STEP)
    c2 = 1.0 / (1.0 - ADAM_B2 ** ADAM_STEP)

    def body(p_ref, w_ref, m_ref, v_ref, g_ref, d_ref, nm_ref, nv_ref):
        g = p_ref[0].astype(F32)
        for k in range(1, N_DEV):
            g = g + p_ref[k].astype(F32)
        wv = w_ref[...]
        nm = ADAM_B1 * m_ref[...] + (1.0 - ADAM_B1) * g
        nv = ADAM_B2 * v_ref[...] + (1.0 - ADAM_B2) * (g * g)
        g_ref[...] = g
        nm_ref[...] = nm
        nv_ref[...] = nv
        d_ref[...] = -ADAM_LR * ((nm * c1) / (jnp.sqrt(nv * c2) + ADAM_EPS) + ADAM_WD * wv)

    blk = pl.BlockSpec((None, rows, ncol), lambda l, i: (l, i, 0))
    return pl.pallas_call(
        body, name=name, grid=(depth, nrow // rows),
        in_specs=[pl.BlockSpec((N_DEV, None, rows, ncol), lambda l, i: (0, l, i, 0)), blk, blk, blk],
        out_specs=[blk] * 4,
        out_shape=[jax.ShapeDtypeStruct(w.shape, F32)] * 4,
        compiler_params=_ARB2,
    )(parts, w, m, v)


def _pack_small(norm_w, mem_norm_w, out_norm_w, final_norm_w, b_forget):
    onw = jnp.pad(out_norm_w.reshape(20, LANES), ((0, 4), (0, 0)))
    b = jnp.pad(b_forget, ((0, 6), (0, LANES - FOX_HEADS)))
    return jnp.concatenate([norm_w.reshape(16, LANES), mem_norm_w.reshape(16, LANES), onw,
                            final_norm_w.reshape(8, LANES), b], axis=0)


def _unpack_small(p):
    return (p[0:16].reshape(2, D_MODEL), p[16:32].reshape(2, D_MODEL), p[32:52].reshape(2, MIX_W),
            p[56:64].reshape(D_MODEL), p[64:66, :FOX_HEADS])


def kernel(x, mem, norm_w, w_in, b_forget, mem_norm_w, w_mem_kv, out_norm_w, w_out, final_norm_w, loss_target, m_norm_w, m_w_in, m_b_forget, m_mem_norm_w, m_w_mem_kv, m_out_norm_w, m_w_out, m_final_norm_w, v_norm_w, v_w_in, v_b_forget, v_mem_norm_w, v_w_mem_kv, v_out_norm_w, v_w_out, v_final_norm_w):
    depth = w_in.shape[0]
    kv_rows = w_mem_kv.shape[1]
    out_rows = w_out.shape[1]

    g_in, g_kv, g_out = _all_gather_weights(
        [w_in.astype(BF16), w_mem_kv.astype(BF16), w_out.astype(BF16)], "all_gather_weights")
    w_full = jnp.transpose(g_in, (1, 2, 0, 3)).reshape(depth, D_MODEL, IN_W)
    w_r = _rearrange_w_in(w_full)
    wkv = jnp.transpose(g_kv, (1, 0, 2, 3)).reshape(depth, D_MODEL, 2 * MEM_W)
    wout = jnp.transpose(g_out, (1, 0, 2, 3)).reshape(depth, MIX_W, D_MODEL)

    loss, grad_x, grads = _local_step(x[0], mem[0], norm_w, w_r, b_forget, mem_norm_w, wkv,
                                      out_norm_w, wout, final_norm_w, loss_target[0])
    loss = lax.psum(loss[0, 0], ("x", "y", "c"))

    p_in = _restore_w_in(grads["w_r"]).reshape(depth, D_MODEL, N_DEV, SHARD_W)
    p_in = jnp.transpose(p_in, (2, 0, 1, 3)).astype(BF16)
    p_kv = grads["w_mem_kv"].reshape(depth, N_DEV, kv_rows, 2 * MEM_W)
    p_kv = jnp.transpose(p_kv, (1, 0, 2, 3)).astype(BF16)
    p_out = grads["w_out"].reshape(depth, N_DEV, out_rows, D_MODEL)
    p_out = jnp.transpose(p_out, (1, 0, 2, 3)).astype(BF16)
    small = _pack_small(grads["norm_w"], grads["mem_norm_w"], grads["out_norm_w"],
                        grads["final_norm_w"], grads["b_forget"])
    p_small = jnp.broadcast_to(small[None, None], (N_DEV, 1, SMALL_ROWS, LANES))
    r_in, r_kv, r_out, r_small = _exchange_blocks([p_in, p_kv, p_out, p_small], "exchange_grads")

    g_w_in, d_w_in, nm_w_in, nv_w_in = _adamw_sum(r_in, w_in, m_w_in, v_w_in, 256, "adamw_w_in")
    g_w_kv, d_w_kv, nm_w_kv, nv_w_kv = _adamw_sum(r_kv, w_mem_kv, m_w_mem_kv, v_w_mem_kv, kv_rows,
                                                  "adamw_w_mem_kv")
    g_w_out, d_w_out, nm_w_out, nv_w_out = _adamw_sum(r_out, w_out, m_w_out, v_w_out, out_rows,
                                                      "adamw_w_out")
    w_small = _pack_small(norm_w, mem_norm_w, out_norm_w, final_norm_w, b_forget)[None]
    m_small = _pack_small(m_norm_w, m_mem_norm_w, m_out_norm_w, m_final_norm_w, m_b_forget)[None]
    v_small = _pack_small(v_norm_w, v_mem_norm_w, v_out_norm_w, v_final_norm_w, v_b_forget)[None]
    small_out = _adamw_sum(r_small, w_small, m_small, v_small, SMALL_ROWS, "adamw_small")
    (g_nw, g_mnw, g_onw, g_fnw, g_b), (d_nw, d_mnw, d_onw, d_fnw, d_b), \
        (nm_nw, nm_mnw, nm_onw, nm_fnw, nm_b), (nv_nw, nv_mnw, nv_onw, nv_fnw, nv_b) = [
            _unpack_small(t[0]) for t in small_out]

    return (loss, grad_x[None],
            g_nw, g_w_in, g_b, g_mnw, g_w_kv, g_onw, g_w_out, g_fnw,
            d_nw, d_w_in, d_b, d_mnw, d_w_kv, d_onw, d_w_out, d_fnw,
            nm_nw, nm_w_in, nm_b, nm_mnw, nm_w_kv, nm_onw, nm_w_out, nm_fnw,
            nv_nw, nv_w_in, nv_b, nv_mnw, nv_w_kv, nv_onw, nv_w_out, nv_fnw)
```

```python
import functools

import jax
import jax.numpy as jnp
from jax import lax
from jax.experimental import pallas as pl
from jax.experimental.pallas import tpu as pltpu

F32 = jnp.float32
BF16 = jnp.bfloat16

N_DEV = 8
D_MODEL = 1024
HEAD_DIM = 64
LANES = 128
SB_W = 512
FX_W = 512
MEM_W = 256
MIX_W = 1280
FOX_HEADS = 8
IN_W = 4616
SHARD_W = IN_W // N_DEV
QKV_W = 3 * SB_W + 3 * FX_W + MEM_W
FL_PAD = 256
GF_W = MIX_W + FL_PAD
WR_W = QKV_W + GF_W
EPS = 1e-6
T = 256
TM = 256
Q_SCALE = 0.125
NEG = -1e30

ADAM_LR = 0.001
ADAM_B1 = 0.9
ADAM_B2 = 0.999
ADAM_EPS = 1e-08
ADAM_WD = 0.01
ADAM_STEP = 10

SMALL_ROWS = 72

_NT = (((1,), (1,)), ((), ()))
_TN = (((0,), (0,)), ((), ()))

_ARB1 = pltpu.CompilerParams(dimension_semantics=("arbitrary",))
_ARB2 = pltpu.CompilerParams(dimension_semantics=("arbitrary", "arbitrary"))


def _dot(a, b):
    return jnp.dot(a, b, preferred_element_type=F32)


def _dot_nt(a, b):
    return lax.dot_general(a, b, _NT, preferred_element_type=F32)


def _dot_tn(a, b):
    return lax.dot_general(a, b, _TN, preferred_element_type=F32)


def _split2(x):
    hi = x.astype(BF16)
    lo = (x - hi.astype(F32)).astype(BF16)
    return hi, lo


def _stack2(u):
    return jnp.concatenate([u, u], axis=0)


def _cum2(x, u2):
    hi, lo = _split2(x)
    return _dot(jnp.concatenate([hi, lo], axis=1), u2)


def _tri3(tri, x):
    hi = x.astype(BF16)
    r1 = x - hi.astype(F32)
    mid = r1.astype(BF16)
    lo = (r1 - mid.astype(F32)).astype(BF16)
    return _dot(tri, hi) + _dot(tri, mid) + _dot(tri, lo)


def _iota2(shape, dim):
    return lax.broadcasted_iota(jnp.int32, shape, dim)


def _head_block_diag():
    r = _iota2((LANES, LANES), 0) // HEAD_DIM
    c = _iota2((LANES, LANES), 1) // HEAD_DIM
    return _stack2(jnp.where(r == c, 1.0, 0.0).astype(BF16))


def _head_mean(x, bd):
    return _cum2(x, bd) * (1.0 / HEAD_DIM)


def _sigmoid(x):
    return 1.0 / (1.0 + jnp.exp(-x))


def _log_sigmoid(x):
    return jnp.minimum(x, 0.0) - jnp.log(1.0 + jnp.exp(-jnp.abs(x)))


def _pair_masks():
    lane = _iota2((1, LANES), 1)
    return lane < HEAD_DIM


def _split_pair(x, m0):
    zero = jnp.zeros_like(x)
    return jnp.where(m0, x, zero), jnp.where(m0, zero, x)


def _inproj_fwd(x, nw, w_r, name):
    s = x.shape[0]

    def body(x_ref, nw_ref, w_ref, qkv_ref, gf_ref):
        xv = x_ref[...]
        r = lax.rsqrt(jnp.mean(xv * xv, axis=-1, keepdims=True) + EPS)
        h = (xv * r * nw_ref[...]).astype(BF16)
        for c in range(0, QKV_W, 256):
            qkv_ref[:, c:c + 256] = _dot(h, w_ref[:, c:c + 256]).astype(BF16)
        for c in range(0, GF_W, 256):
            gf_ref[:, c:c + 256] = _dot(h, w_ref[:, QKV_W + c:QKV_W + c + 256])

    return pl.pallas_call(
        body, name=name, grid=(s // TM,),
        in_specs=[pl.BlockSpec((TM, D_MODEL), lambda i: (i, 0)),
                  pl.BlockSpec((1, D_MODEL), lambda i: (0, 0)),
                  pl.BlockSpec((D_MODEL, WR_W), lambda i: (0, 0))],
        out_specs=[pl.BlockSpec((TM, QKV_W), lambda i: (i, 0)),
                   pl.BlockSpec((TM, GF_W), lambda i: (i, 0))],
        out_shape=[jax.ShapeDtypeStruct((s, QKV_W), BF16), jax.ShapeDtypeStruct((s, GF_W), F32)],
        compiler_params=_ARB1,
    )(x, nw, w_r)


def _fox_prep_fwd(gf, bpad, name):
    s = gf.shape[0]

    def body(fl_ref, b_ref, f_ref):
        tri = jnp.where(_iota2((T, T), 0) >= _iota2((T, T), 1), 1.0, 0.0).astype(BF16)
        carry = jnp.zeros((1, LANES), F32)
        for blk in range(s // T):
            lf = _log_sigmoid(fl_ref[blk * T:(blk + 1) * T, :] + b_ref[...])
            c = _tri3(tri, lf) + carry
            f_ref[blk * T:(blk + 1) * T, :] = c
            carry = c[T - 1:T, :]

    return pl.pallas_call(
        body, name=name, grid=(1,),
        in_specs=[pl.BlockSpec((s, LANES), lambda i: (0, MIX_W // LANES)),
                  pl.BlockSpec((1, LANES), lambda i: (0, 0))],
        out_specs=pl.BlockSpec((s, LANES), lambda i: (0, 0)),
        out_shape=jax.ShapeDtypeStruct((s, LANES), F32),
        compiler_params=_ARB1,
    )(gf, bpad)


def _sb_fwd(qkv, name):
    s = qkv.shape[0]

    def body(q_ref, k_ref, v_ref, o_ref, acc_ref, r_ref, as_ref):
        i = pl.program_id(1)
        m0 = _pair_masks()
        qh = _split_pair(q_ref[...] * jnp.asarray(Q_SCALE, BF16), m0)
        strict = _iota2((T, T), 0) > _iota2((T, T), 1)
        u2 = _stack2(jnp.where(strict, 1.0, 0.0).astype(BF16))
        acc_ref[...] = jnp.zeros_like(acc_ref)
        r_ref[...] = jnp.zeros_like(r_ref)
        hs = range(2)

        def flush(j):
            v = v_ref[pl.ds(pl.multiple_of(j * T, T), T), :]
            for h in hs:
                acc_ref[h] += _dot(as_ref[h], v)

        def tile(j, diag):
            k = k_ref[pl.ds(pl.multiple_of(j * T, T), T), :]
            z = [_dot_nt(qh[h], k) for h in hs]
            if not diag:
                flush(j + 1)
            la = [jnp.minimum(z[h], 0.0) - jnp.log(1.0 + jnp.exp(-jnp.abs(z[h]))) for h in hs]
            lf = [la[h] - z[h] for h in hs]
            if diag:
                lf = [jnp.where(strict, lf[h], 0.0) for h in hs]
            cin = [_cum2(lf[h], u2) for h in hs]
            a = [jnp.exp(la[h] + cin[h] + r_ref[h]) for h in hs]
            if diag:
                a = [jnp.where(strict, a[h], 0.0) for h in hs]
            for h in hs:
                r_ref[h] += cin[h][:, 0:1] + lf[h][:, 0:1]
                as_ref[h] = a[h].astype(BF16)

        tile(i, True)

        def step(n, carry):
            tile(i - 1 - n, False)
            return carry

        lax.fori_loop(0, i, step, 0)
        flush(0)
        o_ref[...] = jnp.where(m0, acc_ref[0], acc_ref[1])

    nb = SB_W // LANES
    return pl.pallas_call(
        body, name=name, grid=(nb, s // T),
        in_specs=[pl.BlockSpec((T, LANES), lambda p, i: (i, p)),
                  pl.BlockSpec((s, LANES), lambda p, i: (0, nb + p)),
                  pl.BlockSpec((s, LANES), lambda p, i: (0, 2 * nb + p))],
        out_specs=pl.BlockSpec((T, LANES), lambda p, i: (i, p)),
        out_shape=jax.ShapeDtypeStruct((s, SB_W), F32),
        scratch_shapes=[pltpu.VMEM((2, T, LANES), F32), pltpu.VMEM((2, T, 1), F32),
                        pltpu.VMEM((2, T, T), BF16)],
        compiler_params=_ARB2,
    )(qkv, qkv, qkv)


def _fox_fwd(qkv, fqb, frow, name):
    s = qkv.shape[0]

    def body(q_ref, k_ref, v_ref, fq_ref, fr_ref, o_ref, lse_ref, acc_ref, m_ref, ps_ref):
        i = pl.program_id(1)
        m0 = _pair_masks()
        qh = _split_pair(q_ref[...] * jnp.asarray(Q_SCALE, BF16), m0)
        fq = fq_ref[...]
        fqh = (fq[:, 0:1], fq[:, HEAD_DIM:HEAD_DIM + 1])
        causal = _iota2((T, T), 0) >= _iota2((T, T), 1)
        ones = jnp.ones((T, LANES), BF16)
        acc_ref[...] = jnp.zeros_like(acc_ref)
        m_ref[...] = jnp.full_like(m_ref, NEG)
        hs = range(2)

        def flush(j):
            v = v_ref[pl.ds(pl.multiple_of(j * T, T), T), :]
            va2 = _stack2(jnp.concatenate([v, ones], axis=1))
            for h in hs:
                acc_ref[h] += _dot(ps_ref[h], va2)

        def tile(j, diag):
            off = pl.multiple_of(j * T, T)
            k = k_ref[pl.ds(off, T), :]
            sc = [_dot_nt(qh[h], k) + fqh[h] - fr_ref[h:h + 1, pl.ds(off, T)] for h in hs]
            if not diag:
                flush(j + 1)
            if diag:
                sc = [jnp.where(causal, sc[h], NEG) for h in hs]
            m_new = [jnp.maximum(m_ref[h], jnp.max(sc[h], axis=1, keepdims=True)) for h in hs]
            p = [jnp.exp(sc[h] - m_new[h]) for h in hs]
            for h in hs:
                acc_ref[h] = acc_ref[h] * jnp.exp(m_ref[h] - m_new[h])
                m_ref[h] = m_new[h]
                p_hi, p_lo = _split2(p[h])
                ps_ref[h] = jnp.concatenate([p_hi, p_lo], axis=1)

        tile(i, True)

        def step(n, carry):
            tile(i - 1 - n, False)
            return carry

        lax.fori_loop(0, i, step, 0)
        flush(0)
        acc = (acc_ref[0], acc_ref[1])
        o_ref[...] = jnp.where(m0, acc[0][:, :LANES] / acc[0][:, LANES:],
                               acc[1][:, :LANES] / acc[1][:, LANES:])
        lse_ref[...] = jnp.where(m0, m_ref[0] + jnp.log(acc[0][:, LANES:]),
                                 m_ref[1] + jnp.log(acc[1][:, LANES:]))

    nb = FX_W // LANES
    base = 3 * SB_W // LANES
    return pl.pallas_call(
        body, name=name, grid=(nb, s // T),
        in_specs=[pl.BlockSpec((T, LANES), lambda p, i: (i, base + p)),
                  pl.BlockSpec((s, LANES), lambda p, i: (0, base + nb + p)),
                  pl.BlockSpec((s, LANES), lambda p, i: (0, base + 2 * nb + p)),
                  pl.BlockSpec((T, LANES), lambda p, i: (i, p)),
                  pl.BlockSpec((None, 2, s), lambda p, i: (p, 0, 0))],
        out_specs=[pl.BlockSpec((T, LANES), lambda p, i: (i, p)),
                   pl.BlockSpec((T, LANES), lambda p, i: (i, p))],
        out_shape=[jax.ShapeDtypeStruct((s, FX_W), F32), jax.ShapeDtypeStruct((s, FX_W), F32)],
        scratch_shapes=[pltpu.VMEM((2, T, 2 * LANES), F32), pltpu.VMEM((2, T, 1), F32),
                        pltpu.VMEM((2, T, 2 * T), BF16)],
        compiler_params=_ARB2,
    )(qkv, qkv, qkv, fqb, frow)


def _memkv_fwd(mem, mnw, wkv, name):
    n = mem.shape[0]

    def body(mem_ref, mnw_ref, w_ref, kv_ref):
        mv = mem_ref[...]
        r = lax.rsqrt(jnp.mean(mv * mv, axis=-1, keepdims=True) + EPS)
        hm = (mv * r * mnw_ref[...]).astype(BF16)
        kv_ref[...] = _dot(hm, w_ref[...]).astype(BF16)

    return pl.pallas_call(
        body, name=name, grid=(1,),
        in_specs=[pl.BlockSpec((n, D_MODEL), lambda i: (0, 0)),
                  pl.BlockSpec((1, D_MODEL), lambda i: (0, 0)),
                  pl.BlockSpec((D_MODEL, 2 * MEM_W), lambda i: (0, 0))],
        out_specs=pl.BlockSpec((n, 2 * MEM_W), lambda i: (0, 0)),
        out_shape=jax.ShapeDtypeStruct((n, 2 * MEM_W), BF16),
        compiler_params=_ARB1,
    )(mem, mnw, wkv)


def _mem_fwd(qkv, kv, name):
    s = qkv.shape[0]
    n = kv.shape[0]

    def body(q_ref, k_ref, v_ref, o_ref, lse_ref):
        m0 = _pair_masks()
        qh = _split_pair(q_ref[...] * jnp.asarray(Q_SCALE, BF16), m0)
        k = k_ref[...]
        v = v_ref[...]
        outs, lses = [], []
        for h in range(2):
            sc = _dot_nt(qh[h], k)
            mx = jnp.max(sc, axis=1, keepdims=True)
            p = jnp.exp(sc - mx)
            l = jnp.sum(p, axis=1, keepdims=True)
            outs.append(_dot(p.astype(BF16), v) / l)
            lses.append(mx + jnp.log(l))
        o_ref[...] = jnp.where(m0, outs[0], outs[1])
        lse_ref[...] = jnp.where(m0, lses[0], lses[1])

    nb = MEM_W // LANES
    base = (3 * SB_W + 3 * FX_W) // LANES
    return pl.pallas_call(
        body, name=name, grid=(nb, s // T),
        in_specs=[pl.BlockSpec((T, LANES), lambda p, i: (i, base + p)),
                  pl.BlockSpec((n, LANES), lambda p, i: (0, p)),
                  pl.BlockSpec((n, LANES), lambda p, i: (0, nb + p))],
        out_specs=[pl.BlockSpec((T, LANES), lambda p, i: (i, p)),
                   pl.BlockSpec((T, LANES), lambda p, i: (i, p))],
        out_shape=[jax.ShapeDtypeStruct((s, MEM_W), F32), jax.ShapeDtypeStruct((s, MEM_W), F32)],
        compiler_params=_ARB2,
    )(qkv, kv, kv)


def _mix_chunk(c, ysb_ref, yfx_ref, ym_ref):
    if c < SB_W // LANES:
        return ysb_ref[:, c * LANES:(c + 1) * LANES]
    c -= SB_W // LANES
    if c < FX_W // LANES:
        return yfx_ref[:, c * LANES:(c + 1) * LANES]
    c -= FX_W // LANES
    return ym_ref[:, c * LANES:(c + 1) * LANES]


def _outproj_fwd(ysb, yfx, ym, gf, onw, wout, x, name):
    s = x.shape[0]

    def body(ysb_ref, yfx_ref, ym_ref, g_ref, onw_ref, w_ref, x_ref, o_ref, yg_ref):
        bd = _head_block_diag()
        for c in range(MIX_W // LANES):
            sl = slice(c * LANES, (c + 1) * LANES)
            u = _mix_chunk(c, ysb_ref, yfx_ref, ym_ref)
            r = lax.rsqrt(_head_mean(u * u, bd) + EPS)
            g = g_ref[:, sl]
            yg_ref[:, sl] = (u * r * onw_ref[:, sl] * (g * _sigmoid(g))).astype(BF16)
        o_ref[...] = x_ref[...] + _dot(yg_ref[...], w_ref[...])

    return pl.pallas_call(
        body, name=name, grid=(s // TM,),
        in_specs=[pl.BlockSpec((TM, SB_W), lambda i: (i, 0)),
                  pl.BlockSpec((TM, FX_W), lambda i: (i, 0)),
                  pl.BlockSpec((TM, MEM_W), lambda i: (i, 0)),
                  pl.BlockSpec((TM, MIX_W), lambda i: (i, 0)),
                  pl.BlockSpec((1, MIX_W), lambda i: (0, 0)),
                  pl.BlockSpec((MIX_W, D_MODEL), lambda i: (0, 0)),
                  pl.BlockSpec((TM, D_MODEL), lambda i: (i, 0))],
        out_specs=pl.BlockSpec((TM, D_MODEL), lambda i: (i, 0)),
        out_shape=jax.ShapeDtypeStruct((s, D_MODEL), F32),
        scratch_shapes=[pltpu.VMEM((TM, MIX_W), BF16)],
        compiler_params=_ARB1,
    )(ysb, yfx, ym, gf, onw, wout, x)


def _final_fwd_bwd(x, fnw, target, name):
    s = x.shape[0]

    def body(x_ref, w_ref, t_ref, dx_ref, loss_ref, dw_ref):
        @pl.when(pl.program_id(0) == 0)
        def _():
            loss_ref[...] = jnp.zeros_like(loss_ref)
            dw_ref[...] = jnp.zeros_like(dw_ref)

        xv = x_ref[...]
        w = w_ref[...]
        r = lax.rsqrt(jnp.mean(xv * xv, axis=-1, keepdims=True) + EPS)
        xh = xv * r
        err = xh * w - t_ref[...]
        part = jnp.sum(jnp.sum(err * err, axis=1, keepdims=True), axis=0, keepdims=True)
        loss_ref[...] += part * (0.5 / D_MODEL)
        dy = err * (1.0 / D_MODEL)
        dw_ref[...] += jnp.sum(dy * xh, axis=0, keepdims=True)
        dxh = dy * w
        dx_ref[...] = r * (dxh - xh * jnp.mean(dxh * xh, axis=-1, keepdims=True))

    return pl.pallas_call(
        body, name=name, grid=(s // TM,),
        in_specs=[pl.BlockSpec((TM, D_MODEL), lambda i: (i, 0)),
                  pl.BlockSpec((1, D_MODEL), lambda i: (0, 0)),
                  pl.BlockSpec((TM, D_MODEL), lambda i: (i, 0))],
        out_specs=[pl.BlockSpec((TM, D_MODEL), lambda i: (i, 0)),
                   pl.BlockSpec((1, LANES), lambda i: (0, 0)),
                   pl.BlockSpec((1, D_MODEL), lambda i: (0, 0))],
        out_shape=[jax.ShapeDtypeStruct((s, D_MODEL), F32), jax.ShapeDtypeStruct((1, LANES), F32),
                   jax.ShapeDtypeStruct((1, D_MODEL), F32)],
        compiler_params=_ARB1,
    )(x, fnw, target)


def _outproj_bwd(dxo, wout, ysb, yfx, ym, gf, onw, name):
    s = dxo.shape[0]

    def body(dx_ref, w_ref, ysb_ref, yfx_ref, ym_ref, g_ref, onw_ref,
             dysb_ref, dyfx_ref, dym_ref, dg_ref, dw_ref, donw_ref, yg_ref):
        @pl.when(pl.program_id(0) == 0)
        def _():
            dw_ref[...] = jnp.zeros_like(dw_ref)
            donw_ref[...] = jnp.zeros_like(donw_ref)

        dxb = dx_ref[...].astype(BF16)
        dyg = _dot_nt(dxb, w_ref[...])
        bd = _head_block_diag()
        for c in range(MIX_W // LANES):
            sl = slice(c * LANES, (c + 1) * LANES)
            u = _mix_chunk(c, ysb_ref, yfx_ref, ym_ref)
            r = lax.rsqrt(_head_mean(u * u, bd) + EPS)
            yn = u * r
            g = g_ref[:, sl]
            sg = _sigmoid(g)
            sil = g * sg
            onw = onw_ref[:, sl]
            e = dyg[:, sl]
            yg_ref[:, sl] = (yn * onw * sil).astype(BF16)
            donw_ref[:, sl] += jnp.sum(e * yn * sil, axis=0, keepdims=True)
            dg_ref[:, sl] = (e * yn * onw * (sg * (1.0 + g * (1.0 - sg)))).astype(BF16)
            dyn = e * onw * sil
            du = (r * (dyn - yn * _head_mean(dyn * yn, bd))).astype(BF16)
            if c < 4:
                dysb_ref[:, c * LANES:(c + 1) * LANES] = du
            elif c < 8:
                dyfx_ref[:, (c - 4) * LANES:(c - 3) * LANES] = du
            else:
                dym_ref[:, (c - 8) * LANES:(c - 7) * LANES] = du
        dw_ref[...] += _dot_tn(yg_ref[...], dxb)

    return pl.pallas_call(
        body, name=name, grid=(s // TM,),
        in_specs=[pl.BlockSpec((TM, D_MODEL), lambda i: (i, 0)),
                  pl.BlockSpec((MIX_W, D_MODEL), lambda i: (0, 0)),
                  pl.BlockSpec((TM, SB_W), lambda i: (i, 0)),
                  pl.BlockSpec((TM, FX_W), lambda i: (i, 0)),
                  pl.BlockSpec((TM, MEM_W), lambda i: (i, 0)),
                  pl.BlockSpec((TM, MIX_W), lambda i: (i, 0)),
                  pl.BlockSpec((1, MIX_W), lambda i: (0, 0))],
        out_specs=[pl.BlockSpec((TM, SB_W), lambda i: (i, 0)),
                   pl.BlockSpec((TM, FX_W), lambda i: (i, 0)),
                   pl.BlockSpec((TM, MEM_W), lambda i: (i, 0)),
                   pl.BlockSpec((TM, MIX_W), lambda i: (i, 0)),
                   pl.BlockSpec((MIX_W, D_MODEL), lambda i: (0, 0)),
                   pl.BlockSpec((1, MIX_W), lambda i: (0, 0))],
        out_shape=[jax.ShapeDtypeStruct((s, SB_W), BF16), jax.ShapeDtypeStruct((s, FX_W), BF16),
                   jax.ShapeDtypeStruct((s, MEM_W), BF16), jax.ShapeDtypeStruct((s, MIX_W), BF16),
                   jax.ShapeDtypeStruct((MIX_W, D_MODEL), F32), jax.ShapeDtypeStruct((1, MIX_W), F32)],
        scratch_shapes=[pltpu.VMEM((TM, MIX_W), BF16)],
        compiler_params=_ARB1,
    )(dxo, wout, ysb, yfx, ym, gf, onw)


def _row_dots(do_ref, o_ref, m0):
    prod = do_ref[...].astype(F32) * o_ref[...]
    zero = jnp.zeros_like(prod)
    return (jnp.sum(jnp.where(m0, prod, zero), axis=1, keepdims=True),
            jnp.sum(jnp.where(m0, zero, prod), axis=1, keepdims=True))


def _sb_bwd(qkv, o, do, name):
    s = qkv.shape[0]
    nq = s // T

    def body(q_ref, k_ref, v_ref, o_ref, do_ref, dq_ref, dk_ref, dv_ref,
             dqa_ref, dka_ref, dva_ref, rl_ref, rg_ref, dzs_ref, abs_ref):
        i = pl.program_id(1)

        @pl.when(i == 0)
        def _():
            dka_ref[...] = jnp.zeros_like(dka_ref)
            dva_ref[...] = jnp.zeros_like(dva_ref)

        m0 = _pair_masks()
        qh = _split_pair(q_ref[...] * jnp.asarray(Q_SCALE, BF16), m0)
        doh = _split_pair(do_ref[...], m0)
        dsum = _row_dots(do_ref, o_ref, m0)
        strict = _iota2((T, T), 0) > _iota2((T, T), 1)
        u2 = _stack2(jnp.where(strict, 1.0, 0.0).astype(BF16))
        dqa_ref[...] = jnp.zeros_like(dqa_ref)
        rl_ref[...] = jnp.zeros_like(rl_ref)
        rg_ref[...] = jnp.zeros_like(rg_ref)

        hs = range(2)

        def flush(j):
            off = pl.multiple_of(j * T, T)
            k = k_ref[pl.ds(off, T), :]
            for h in hs:
                dqa_ref[h] += _dot(dzs_ref[h], k)
            dka_ref[pl.ds(off, T), :] += _dot_tn(dzs_ref[0], qh[0]) + _dot_tn(dzs_ref[1], qh[1])
            dva_ref[pl.ds(off, T), :] += _dot_tn(abs_ref[0], doh[0]) + _dot_tn(abs_ref[1], doh[1])

        def tile(j, diag):
            off = pl.multiple_of(j * T, T)
            k = k_ref[pl.ds(off, T), :]
            v = v_ref[pl.ds(off, T), :]
            z = [_dot_nt(qh[h], k) for h in hs]
            da = [_dot_nt(doh[h], v) for h in hs]
            if not diag:
                flush(j + 1)
            la = [jnp.minimum(z[h], 0.0) - jnp.log(1.0 + jnp.exp(-jnp.abs(z[h]))) for h in hs]
            lf = [la[h] - z[h] for h in hs]
            if diag:
                lf = [jnp.where(strict, lf[h], 0.0) for h in hs]
            cin = [_cum2(lf[h], u2) for h in hs]
            a = [jnp.exp(la[h] + cin[h] + rl_ref[h]) for h in hs]
            if diag:
                a = [jnp.where(strict, a[h], 0.0) for h in hs]
            ab = [a[h].astype(BF16) for h in hs]
            g = [ab[h].astype(F32) * da[h] for h in hs]
            gin = [_cum2(g[h], u2) for h in hs]
            dz = [g[h] - jnp.exp(la[h]) * ((dsum[h] - rg_ref[h]) - gin[h]) for h in hs]
            if diag:
                dz = [jnp.where(strict, dz[h], 0.0) for h in hs]
            for h in hs:
                rl_ref[h] += cin[h][:, 0:1] + lf[h][:, 0:1]
                rg_ref[h] += gin[h][:, 0:1] + g[h][:, 0:1]
                dzs_ref[h] = dz[h].astype(BF16)
                abs_ref[h] = ab[h]

        tile(i, True)

        def step(n, carry):
            tile(i - 1 - n, False)
            return carry

        lax.fori_loop(0, i, step, 0)
        flush(0)
        dq_ref[...] = (jnp.where(m0, dqa_ref[0], dqa_ref[1]) * Q_SCALE).astype(BF16)

        @pl.when(i == nq - 1)
        def _():
            dk_ref[...] = dka_ref[...].astype(BF16)
            dv_ref[...] = dva_ref[...].astype(BF16)

    nb = SB_W // LANES
    return pl.pallas_call(
        body, name=name, grid=(nb, nq),
        in_specs=[pl.BlockSpec((T, LANES), lambda p, i: (i, p)),
                  pl.BlockSpec((s, LANES), lambda p, i: (0, nb + p)),
                  pl.BlockSpec((s, LANES), lambda p, i: (0, 2 * nb + p)),
                  pl.BlockSpec((T, LANES), lambda p, i: (i, p)),
                  pl.BlockSpec((T, LANES), lambda p, i: (i, p))],
        out_specs=[pl.BlockSpec((T, LANES), lambda p, i: (i, p)),
                   pl.BlockSpec((s, LANES), lambda p, i: (0, p)),
                   pl.BlockSpec((s, LANES), lambda p, i: (0, p))],
        out_shape=[jax.ShapeDtypeStruct((s, SB_W), BF16)] * 3,
        scratch_shapes=[pltpu.VMEM((2, T, LANES), F32), pltpu.VMEM((s, LANES), F32),
                        pltpu.VMEM((s, LANES), F32), pltpu.VMEM((2, T, 1), F32),
                        pltpu.VMEM((2, T, 1), F32), pltpu.VMEM((2, T, T), BF16),
                        pltpu.VMEM((2, T, T), BF16)],
        compiler_params=_ARB2,
    )(qkv, qkv, qkv, o, do)


def _fox_bwd(qkv, fqb, frow, o, lse, do, name):
    s = qkv.shape[0]
    nq = s // T

    def body(q_ref, k_ref, v_ref, fq_ref, fr_ref, o_ref, lse_ref, do_ref,
             dq_ref, dk_ref, dv_ref, df_ref, dqa_ref, dka_ref, dva_ref, dfa_ref, dls_ref, pbs_ref):
        i = pl.program_id(1)

        @pl.when(i == 0)
        def _():
            dka_ref[...] = jnp.zeros_like(dka_ref)
            dva_ref[...] = jnp.zeros_like(dva_ref)
            dfa_ref[...] = jnp.zeros_like(dfa_ref)

        m0 = _pair_masks()
        qh = _split_pair(q_ref[...] * jnp.asarray(Q_SCALE, BF16), m0)
        doh = _split_pair(do_ref[...], m0)
        dsum = _row_dots(do_ref, o_ref, m0)
        fq = fq_ref[...]
        fqh = (fq[:, 0:1], fq[:, HEAD_DIM:HEAD_DIM + 1])
        lse = lse_ref[...]
        lseh = (lse[:, 0:1], lse[:, HEAD_DIM:HEAD_DIM + 1])
        causal = _iota2((T, T), 0) >= _iota2((T, T), 1)
        dqa_ref[...] = jnp.zeros_like(dqa_ref)

        hs = range(2)

        def flush(j):
            off = pl.multiple_of(j * T, T)
            k = k_ref[pl.ds(off, T), :]
            for h in hs:
                dqa_ref[h] += _dot(dls_ref[h], k)
            dka_ref[pl.ds(off, T), :] += _dot_tn(dls_ref[0], qh[0]) + _dot_tn(dls_ref[1], qh[1])
            dva_ref[pl.ds(off, T), :] += _dot_tn(pbs_ref[0], doh[0]) + _dot_tn(pbs_ref[1], doh[1])

        def tile(j, diag):
            off = pl.multiple_of(j * T, T)
            k = k_ref[pl.ds(off, T), :]
            v = v_ref[pl.ds(off, T), :]
            sc = [_dot_nt(qh[h], k) + fqh[h] - fr_ref[h:h + 1, pl.ds(off, T)] for h in hs]
            dp = [_dot_nt(doh[h], v) for h in hs]
            if not diag:
                flush(j + 1)
            p = [jnp.exp(sc[h] - lseh[h]) for h in hs]
            if diag:
                p = [jnp.where(causal, p[h], 0.0) for h in hs]
            dl = [p[h] * (dp[h] - dsum[h]) for h in hs]
            for h in hs:
                dls_ref[h] = dl[h].astype(BF16)
                pbs_ref[h] = p[h].astype(BF16)
                dfa_ref[h:h + 1, pl.ds(off, T)] -= jnp.sum(dl[h], axis=0, keepdims=True)

        tile(i, True)

        def step(n, carry):
            tile(i - 1 - n, False)
            return carry

        lax.fori_loop(0, i, step, 0)
        flush(0)
        dq_ref[...] = (jnp.where(m0, dqa_ref[0], dqa_ref[1]) * Q_SCALE).astype(BF16)

        @pl.when(i == nq - 1)
        def _():
            dk_ref[...] = dka_ref[...].astype(BF16)
            dv_ref[...] = dva_ref[...].astype(BF16)
            df_ref[...] = dfa_ref[...]

    nb = FX_W // LANES
    base = 3 * SB_W // LANES
    return pl.pallas_call(
        body, name=name, grid=(nb, nq),
        in_specs=[pl.BlockSpec((T, LANES), lambda p, i: (i, base + p)),
                  pl.BlockSpec((s, LANES), lambda p, i: (0, base + nb + p)),
                  pl.BlockSpec((s, LANES), lambda p, i: (0, base + 2 * nb + p)),
                  pl.BlockSpec((T, LANES), lambda p, i: (i, p)),
                  pl.BlockSpec((None, 2, s), lambda p, i: (p, 0, 0)),
                  pl.BlockSpec((T, LANES), lambda p, i: (i, p)),
                  pl.BlockSpec((T, LANES), lambda p, i: (i, p)),
                  pl.BlockSpec((T, LANES), lambda p, i: (i, p))],
        out_specs=[pl.BlockSpec((T, LANES), lambda p, i: (i, p)),
                   pl.BlockSpec((s, LANES), lambda p, i: (0, p)),
                   pl.BlockSpec((s, LANES), lambda p, i: (0, p)),
                   pl.BlockSpec((None, 2, s), lambda p, i: (p, 0, 0))],
        out_shape=[jax.ShapeDtypeStruct((s, FX_W), BF16)] * 3
        + [jax.ShapeDtypeStruct((nb, 2, s), F32)],
        scratch_shapes=[pltpu.VMEM((2, T, LANES), F32), pltpu.VMEM((s, LANES), F32),
                        pltpu.VMEM((s, LANES), F32), pltpu.VMEM((2, s), F32),
                        pltpu.VMEM((2, T, T), BF16), pltpu.VMEM((2, T, T), BF16)],
        compiler_params=_ARB2,
    )(qkv, qkv, qkv, fqb, frow, o, lse, do)


def _fox_prep_bwd(dfcol, gf, bpad, name):
    s = gf.shape[0]

    def body(df_ref, fl_ref, b_ref, dfl_ref, db_ref):
        tri = jnp.where(_iota2((T, T), 0) <= _iota2((T, T), 1), 1.0, 0.0).astype(BF16)
        carry = jnp.zeros((1, LANES), F32)
        db = jnp.zeros((1, LANES), F32)
        for blk in reversed(range(s // T)):
            rows = slice(blk * T, (blk + 1) * T)
            c = _tri3(tri, df_ref[rows, :]) + carry
            carry = c[0:1, :]
            dfl = c / (1.0 + jnp.exp(fl_ref[rows, :] + b_ref[...]))
            dfl_ref[rows, :] = dfl.astype(BF16)
            db = db + jnp.sum(dfl, axis=0, keepdims=True)
        db_ref[...] = db

    return pl.pallas_call(
        body, name=name, grid=(1,),
        in_specs=[pl.BlockSpec((s, LANES), lambda i: (0, 0)),
                  pl.BlockSpec((s, LANES), lambda i: (0, MIX_W // LANES)),
                  pl.BlockSpec((1, LANES), lambda i: (0, 0))],
        out_specs=[pl.BlockSpec((s, LANES), lambda i: (0, 0)),
                   pl.BlockSpec((1, LANES), lambda i: (0, 0))],
        out_shape=[jax.ShapeDtypeStruct((s, LANES), BF16), jax.ShapeDtypeStruct((1, LANES), F32)],
        compiler_params=_ARB1,
    )(dfcol, gf, bpad)


def _mem_bwd(qkv, kv, o, lse, do, name):
    s = qkv.shape[0]
    n = kv.shape[0]
    nq = s // T

    def body(q_ref, k_ref, v_ref, o_ref, lse_ref, do_ref, dq_ref, dk_ref, dv_ref):
        @pl.when(pl.program_id(1) == 0)
        def _():
            dk_ref[...] = jnp.zeros_like(dk_ref)
            dv_ref[...] = jnp.zeros_like(dv_ref)

        m0 = _pair_masks()
        qh = _split_pair(q_ref[...] * jnp.asarray(Q_SCALE, BF16), m0)
        doh = _split_pair(do_ref[...], m0)
        dsum = _row_dots(do_ref, o_ref, m0)
        lse = lse_ref[...]
        lseh = (lse[:, 0:1], lse[:, HEAD_DIM:HEAD_DIM + 1])
        k = k_ref[...]
        v = v_ref[...]
        dqs = []
        for h in range(2):
            p = jnp.exp(_dot_nt(qh[h], k) - lseh[h])
            dl = p * (_dot_nt(doh[h], v) - dsum[h])
            dlb = dl.astype(BF16)
            dqs.append(_dot(dlb, k))
            dk_ref[...] += _dot_tn(dlb, qh[h])
            dv_ref[...] += _dot_tn(p.astype(BF16), doh[h])
        dq_ref[...] = (jnp.where(m0, dqs[0], dqs[1]) * Q_SCALE).astype(BF16)

    nb = MEM_W // LANES
    base = (3 * SB_W + 3 * FX_W) // LANES
    return pl.pallas_call(
        body, name=name, grid=(nb, nq),
        in_specs=[pl.BlockSpec((T, LANES), lambda p, i: (i, base + p)),
                  pl.BlockSpec((n, LANES), lambda p, i: (0, p)),
                  pl.BlockSpec((n, LANES), lambda p, i: (0, nb + p)),
                  pl.BlockSpec((T, LANES), lambda p, i: (i, p)),
                  pl.BlockSpec((T, LANES), lambda p, i: (i, p)),
                  pl.BlockSpec((T, LANES), lambda p, i: (i, p))],
        out_specs=[pl.BlockSpec((T, LANES), lambda p, i: (i, p)),
                   pl.BlockSpec((n, LANES), lambda p, i: (0, p)),
                   pl.BlockSpec((n, LANES), lambda p, i: (0, p))],
        out_shape=[jax.ShapeDtypeStruct((s, MEM_W), BF16), jax.ShapeDtypeStruct((n, MEM_W), F32),
                   jax.ShapeDtypeStruct((n, MEM_W), F32)],
        compiler_params=_ARB2,
    )(qkv, kv, kv, o, lse, do)


def _memkv_bwd(mem, mnw, wkv, dk, dv, name):
    n = mem.shape[0]

    def body(mem_ref, mnw_ref, w_ref, dk_ref, dv_ref, dw_ref, dmnw_ref):
        mv = mem_ref[...]
        r = lax.rsqrt(jnp.mean(mv * mv, axis=-1, keepdims=True) + EPS)
        mh = mv * r
        hm = (mh * mnw_ref[...]).astype(BF16)
        dkv = jnp.concatenate([dk_ref[...], dv_ref[...]], axis=1).astype(BF16)
        dw_ref[...] = _dot_tn(hm, dkv)
        dhm = _dot_nt(dkv, w_ref[...])
        dmnw_ref[...] = jnp.sum(dhm * mh, axis=0, keepdims=True)

    return pl.pallas_call(
        body, name=name, grid=(1,),
        in_specs=[pl.BlockSpec((n, D_MODEL), lambda i: (0, 0)),
                  pl.BlockSpec((1, D_MODEL), lambda i: (0, 0)),
                  pl.BlockSpec((D_MODEL, 2 * MEM_W), lambda i: (0, 0)),
                  pl.BlockSpec((n, MEM_W), lambda i: (0, 0)),
                  pl.BlockSpec((n, MEM_W), lambda i: (0, 0))],
        out_specs=[pl.BlockSpec((D_MODEL, 2 * MEM_W), lambda i: (0, 0)),
                   pl.BlockSpec((1, D_MODEL), lambda i: (0, 0))],
        out_shape=[jax.ShapeDtypeStruct((D_MODEL, 2 * MEM_W), F32),
                   jax.ShapeDtypeStruct((1, D_MODEL), F32)],
        compiler_params=_ARB1,
    )(mem, mnw, wkv, dk, dv)


def _inproj_bwd_dx(pieces, w_r, x, nw, dxo, name):
    s = x.shape[0]
    n = len(pieces)
    widths = [p.shape[1] for p in pieces]

    def body(*refs):
        piece_refs = refs[:n]
        w_ref, x_ref, nw_ref, dxo_ref, dx_ref, ht_ref, dnw_ref, dp_ref = refs[n:]

        @pl.when(pl.program_id(0) == 0)
        def _():
            dnw_ref[...] = jnp.zeros_like(dnw_ref)

        col = 0
        for r, wd in zip(piece_refs, widths):
            dp_ref[:, col:col + wd] = r[...]
            col += wd
        dp_ref[:, col:] = jnp.zeros((TM, WR_W - col), BF16)
        dh = _dot_nt(dp_ref[...], w_ref[...])
        xv = x_ref[...]
        nw = nw_ref[...]
        r = lax.rsqrt(jnp.mean(xv * xv, axis=-1, keepdims=True) + EPS)
        xh = xv * r
        ht_ref[...] = (xh * nw).T.astype(BF16)
        dnw_ref[...] += jnp.sum(dh * xh, axis=0, keepdims=True)
        dxh = dh * nw
        dx_ref[...] = r * (dxh - xh * jnp.mean(dxh * xh, axis=-1, keepdims=True)) + dxo_ref[...]

    return pl.pallas_call(
        body, name=name, grid=(s // TM,),
        in_specs=[pl.BlockSpec((TM, wd), lambda i: (i, 0)) for wd in widths]
        + [pl.BlockSpec((D_MODEL, WR_W), lambda i: (0, 0)),
           pl.BlockSpec((TM, D_MODEL), lambda i: (i, 0)),
           pl.BlockSpec((1, D_MODEL), lambda i: (0, 0)),
           pl.BlockSpec((TM, D_MODEL), lambda i: (i, 0))],
        out_specs=[pl.BlockSpec((TM, D_MODEL), lambda i: (i, 0)),
                   pl.BlockSpec((D_MODEL, TM), lambda i: (0, i)),
                   pl.BlockSpec((1, D_MODEL), lambda i: (0, 0)),
                   pl.BlockSpec((TM, WR_W), lambda i: (i, 0))],
        out_shape=[jax.ShapeDtypeStruct((s, D_MODEL), F32), jax.ShapeDtypeStruct((D_MODEL, s), BF16),
                   jax.ShapeDtypeStruct((1, D_MODEL), F32), jax.ShapeDtypeStruct((s, WR_W), BF16)],
        compiler_params=_ARB1,
    )(*pieces, w_r, x, nw, dxo)


def _inproj_bwd_dw(ht, dproj, name):
    s = dproj.shape[0]
    tn = 256

    def body(ht_ref, dp_ref, dw_ref):
        dw_ref[...] = _dot(ht_ref[...], dp_ref[...])

    return pl.pallas_call(
        body, name=name, grid=(WR_W // tn,),
        in_specs=[pl.BlockSpec((D_MODEL, s), lambda j: (0, 0)),
                  pl.BlockSpec((s, tn), lambda j: (0, j))],
        out_specs=pl.BlockSpec((D_MODEL, tn), lambda j: (0, j)),
        out_shape=jax.ShapeDtypeStruct((D_MODEL, WR_W), F32),
        compiler_params=_ARB1,
    )(ht, dproj)


def _rearrange_w_in(w):
    pad = jnp.zeros(w.shape[:-1] + (FL_PAD - FOX_HEADS,), w.dtype)
    return jnp.concatenate(
        [w[..., :3072], w[..., 3080:3336], w[..., 3336:IN_W], w[..., 3072:3080], pad], axis=-1)


def _restore_w_in(g):
    gate0 = QKV_W
    fl0 = QKV_W + MIX_W
    return jnp.concatenate(
        [g[..., :3072], g[..., fl0:fl0 + FOX_HEADS], g[..., 3072:QKV_W], g[..., gate0:fl0]], axis=-1)


def _pad_lanes(v, width=LANES):
    return jnp.pad(v, (0, width - v.shape[0])).reshape(1, width)


def _layer_fwd(xs, mem, nw, w_r, b_forget, mnw, wkv, onw, wout, l):
    s = xs.shape[0]
    bpad = _pad_lanes(b_forget)
    qkv, gf = _inproj_fwd(xs, nw, w_r, f"inproj_fwd_{l}")
    f = _fox_prep_fwd(gf, bpad, f"fox_prep_fwd_{l}")
    f8 = f[:, :FOX_HEADS]
    fqb = jnp.repeat(f8, HEAD_DIM, axis=1)
    frow = f8.T.reshape(FOX_HEADS // 2, 2, s)
    ysb = _sb_fwd(qkv, f"sb_fwd_{l}")
    yfx, lse_fx = _fox_fwd(qkv, fqb, frow, f"fox_fwd_{l}")
    kv = _memkv_fwd(mem, mnw, wkv, f"memkv_fwd_{l}")
    ym, lse_m = _mem_fwd(qkv, kv, f"mem_fwd_{l}")
    xn = _outproj_fwd(ysb, yfx, ym, gf, onw, wout, xs, f"outproj_fwd_{l}")
    return xn, (xs, nw, mnw, onw, bpad, qkv, gf, fqb, frow, ysb, yfx, lse_fx, kv, ym, lse_m)


def _layer_bwd(dx, saved, mem, w_r, wkv, wout, l):
    xs, nw, mnw, onw, bpad, qkv, gf, fqb, frow, ysb, yfx, lse_fx, kv, ym, lse_m = saved
    s = xs.shape[0]
    dysb, dyfx, dym, dgate, dwout, donw = _outproj_bwd(
        dx, wout, ysb, yfx, ym, gf, onw, f"outproj_bwd_{l}")
    sdq, sdk, sdv = _sb_bwd(qkv, ysb, dysb, f"sb_bwd_{l}")
    fdq, fdk, fdv, dfrow = _fox_bwd(qkv, fqb, frow, yfx, lse_fx, dyfx, f"fox_bwd_{l}")
    dfcol = jnp.pad(dfrow.reshape(FOX_HEADS, s).T, ((0, 0), (0, LANES - FOX_HEADS)))
    dfl, db = _fox_prep_bwd(dfcol, gf, bpad, f"fox_prep_bwd_{l}")
    dmq, dmk, dmv = _mem_bwd(qkv, kv, ym, lse_m, dym, f"mem_bwd_{l}")
    dwkv, dmnw = _memkv_bwd(mem, mnw, wkv, dmk, dmv, f"memkv_bwd_{l}")
    dx, ht, dnw, dproj = _inproj_bwd_dx([sdq, sdk, sdv, fdq, fdk, fdv, dmq, dgate, dfl],
                                        w_r, xs, nw, dx, f"inproj_bwd_dx_{l}")
    dwr = _inproj_bwd_dw(ht, dproj, f"inproj_bwd_dw_{l}")
    grads = dict(norm_w=dnw[0], w_r=dwr, b_forget=db[0, :FOX_HEADS], mem_norm_w=dmnw[0],
                 w_mem_kv=dwkv, out_norm_w=donw[0], w_out=dwout)
    return dx, grads


_ANY = pl.BlockSpec(memory_space=pl.ANY)


def _my_place():
    return lax.axis_index("x"), lax.axis_index("y"), lax.axis_index("c")


def _flip(v, bit):
    return 1 - v if bit else v


def _block_index(px, py, pc):
    return 4 * px + 2 * py + pc


def _all_gather_weights(shards, name):
    n = len(shards)

    def body(*refs):
        ins, outs = refs[:n], refs[n:2 * n]
        send_sems, recv_sems, local_sems = refs[2 * n:]
        x, y, c = _my_place()
        me = (x, y, c)
        sibling = (x, y, 1 - c)
        chips = [(1 - x, y), (x, 1 - y), (1 - x, 1 - y)]

        def copy(a, k, block, to, src=None):
            dst = outs[a].at[_block_index(*block)]
            return pltpu.make_async_remote_copy(
                src_ref=dst if src is None else src, dst_ref=dst,
                send_sem=send_sems.at[a, k], recv_sem=recv_sems.at[a, k],
                device_id=to, device_id_type=pl.DeviceIdType.MESH)

        mine = [pltpu.make_async_copy(ins[a], outs[a].at[_block_index(*me)], local_sems.at[a])
                for a in range(n)]
        for cp in mine:
            cp.start()
        first = []
        for a in range(n):
            first.append(copy(a, 0, me, sibling, src=ins[a]))
            first += [copy(a, 1 + j, me, (*chip, c), src=ins[a]) for j, chip in enumerate(chips)]
        for cp in first:
            cp.start()
        passed = []
        for j, chip in enumerate(chips):
            for a in range(n):
                copy(a, 1 + j, (*chip, c), me).wait_recv()
                fwd = copy(a, 4 + j, (*chip, c), sibling)
                fwd.start()
                passed.append(fwd)
        for a in range(n):
            copy(a, 0, sibling, me).wait_recv()
            for j, chip in enumerate(chips):
                copy(a, 4 + j, (*chip, 1 - c), me).wait_recv()
        for cp in first + passed:
            cp.wait_send()
        for cp in mine:
            cp.wait()

    return pl.pallas_call(
        body, name=name,
        in_specs=[_ANY] * n, out_specs=[_ANY] * n,
        out_shape=[jax.ShapeDtypeStruct((N_DEV,) + v.shape, v.dtype) for v in shards],
        scratch_shapes=[pltpu.SemaphoreType.DMA((n, 7)), pltpu.SemaphoreType.DMA((n, 7)),
                        pltpu.SemaphoreType.DMA((n,))],
    )(*shards)


def _exchange_blocks(blocked, name):
    n = len(blocked)

    def body(*refs):
        ins, outs = refs[:n], refs[n:2 * n]
        send_sems, recv_sems, local_sems = refs[2 * n:]
        x, y, c = _my_place()
        mine_idx = _block_index(x, y, c)
        local = [pltpu.make_async_copy(ins[a].at[mine_idx], outs[a].at[mine_idx], local_sems.at[a])
                 for a in range(n)]
        for cp in local:
            cp.start()
        sends, arrivals = [], []
        for r in range(1, N_DEV):
            peer = (_flip(x, r & 4), _flip(y, r & 2), _flip(c, r & 1))
            peer_idx = _block_index(*peer)
            for a in range(n):
                sems = dict(send_sem=send_sems.at[a, r - 1], recv_sem=recv_sems.at[a, r - 1],
                            device_id=peer, device_id_type=pl.DeviceIdType.MESH)
                sends.append(pltpu.make_async_remote_copy(
                    src_ref=ins[a].at[peer_idx], dst_ref=outs[a].at[mine_idx], **sems))
                arrivals.append(pltpu.make_async_remote_copy(
                    src_ref=ins[a].at[peer_idx], dst_ref=outs[a].at[peer_idx], **sems))
        for cp in sends:
            cp.start()
        for cp in arrivals:
            cp.wait_recv()
        for cp in sends:
            cp.wait_send()
        for cp in local:
            cp.wait()

    return pl.pallas_call(
        body, name=name,
        in_specs=[_ANY] * n, out_specs=[_ANY] * n,
        out_shape=[jax.ShapeDtypeStruct(v.shape, v.dtype) for v in blocked],
        scratch_shapes=[pltpu.SemaphoreType.DMA((n, 7)), pltpu.SemaphoreType.DMA((n, 7)),
                        pltpu.SemaphoreType.DMA((n,))],
    )(*blocked)


_HBM = pl.BlockSpec(memory_space=pltpu.HBM)
_SEM = pl.BlockSpec(memory_space=pltpu.SEMAPHORE)
_DATAFLOW = pltpu.SideEffectType.DATAFLOW_SIDE_EFFECTING


def _split_copies(srcs, lands, send_sems, recv_sems, scatter):
    x, y, c = _my_place()
    mine_idx = _block_index(x, y, c)
    sends, arrivals = [], []
    for r in range(1, N_DEV):
        peer = (_flip(x, r & 4), _flip(y, r & 2), _flip(c, r & 1))
        peer_idx = _block_index(*peer)
        for a in range(len(srcs)):
            src = srcs[a].at[peer_idx] if scatter else srcs[a]
            k = 7 * a + r - 1
            sems = dict(send_sem=send_sems.at[k], recv_sem=recv_sems.at[k],
                        device_id=peer, device_id_type=pl.DeviceIdType.MESH)
            sends.append(pltpu.make_async_remote_copy(
                src_ref=src, dst_ref=lands[a].at[mine_idx], **sems))
            arrivals.append(pltpu.make_async_remote_copy(
                src_ref=src, dst_ref=lands[a].at[peer_idx], **sems))
    return sends, arrivals


def _exchange_start(srcs, scatter, name):
    n = len(srcs)
    land_shapes = [v.shape if scatter else (N_DEV,) + v.shape for v in srcs]

    def body(*refs):
        ins, lands = refs[:n], refs[n:2 * n]
        send_sems, recv_sems = refs[2 * n], refs[2 * n + 1]
        token = refs[-1]
        sends, _ = _split_copies(ins, lands, send_sems, recv_sems, scatter)
        for cp in sends:
            cp.start()
        token[...] = jnp.zeros_like(token)

    hbm_in = [pltpu.with_memory_space_constraint(v, pltpu.HBM) for v in srcs]
    hbm_land = [pltpu.with_memory_space_constraint(lax.empty(shape, v.dtype), pltpu.HBM)
                for shape, v in zip(land_shapes, srcs)]
    outs = pl.pallas_call(
        body, name=name,
        out_shape=(pltpu.SemaphoreType.DMA((7 * n,)), pltpu.SemaphoreType.DMA((7 * n,)),
                   *[pltpu.HBM(v.shape, v.dtype) for v in srcs],
                   *[pltpu.HBM(shape, v.dtype) for shape, v in zip(land_shapes, srcs)],
                   jax.ShapeDtypeStruct((8, LANES), F32)),
        in_specs=[_HBM] * (2 * n),
        out_specs=(_SEM, _SEM, *[_HBM] * (2 * n), pl.BlockSpec(memory_space=pltpu.VMEM)),
        input_output_aliases={a: 2 + a for a in range(2 * n)},
        compiler_params=pltpu.CompilerParams(has_side_effects=_DATAFLOW),
    )(*hbm_in, *hbm_land)
    return outs[0], outs[1], list(outs[2:2 + n]), list(outs[2 + n:2 + 2 * n]), outs[-1]


def _exchange_wait(started, after, scatter, name):
    send_sems, recv_sems, srcs, lands, _ = started
    n = len(srcs)

    def body(*refs):
        ins, lnd = refs[:n], refs[n:2 * n]
        sends, arrivals = _split_copies(ins, lnd, refs[2 * n], refs[2 * n + 1], scatter)
        for cp in sends:
            cp.wait_send()
        for cp in arrivals:
            cp.wait_recv()

    outs = pl.pallas_call(
        body, name=name,
        out_shape=tuple(pltpu.HBM(v.shape, v.dtype) for v in srcs + lands),
        in_specs=[_HBM] * (2 * n) + [_SEM, _SEM, _ANY],
        out_specs=tuple([_HBM] * (2 * n)),
        input_output_aliases={a: a for a in range(2 * n)},
        compiler_params=pltpu.CompilerParams(has_side_effects=_DATAFLOW),
    )(*srcs, *lands, send_sems, recv_sems, after)
    return list(outs[:n]), list(outs[n:])


def _adamw_sum(parts, w, m, v, rows, name):
    depth, nrow, ncol = w.shape
    c1 = 1.0 / (1.0 - ADAM_B1 ** ADAM_STEP)
    c2 = 1.0 / (1.0 - ADAM_B2 ** ADAM_STEP)

    def body(*refs):
        p_refs = refs[:depth]
        w_ref, m_ref, v_ref, g_ref, d_ref, nm_ref, nv_ref = refs[depth:]
        layer = pl.program_id(0)
        for l in range(depth):
            @pl.when(layer == l)
            def _(p_ref=p_refs[l]):
                g = p_ref[0].astype(F32)
                for k in range(1, N_DEV):
                    g = g + p_ref[k].astype(F32)
                nm = ADAM_B1 * m_ref[...] + (1.0 - ADAM_B1) * g
                nv = ADAM_B2 * v_ref[...] + (1.0 - ADAM_B2) * (g * g)
                g_ref[...] = g
                nm_ref[...] = nm
                nv_ref[...] = nv
                d_ref[...] = -ADAM_LR * ((nm * c1) / (jnp.sqrt(nv * c2) + ADAM_EPS)
                                         + ADAM_WD * w_ref[...])

    def part_spec(l):
        return pl.BlockSpec((N_DEV, rows, ncol), lambda q, i: (0, jnp.where(q == l, i, 0), 0))

    blk = pl.BlockSpec((None, rows, ncol), lambda q, i: (q, i, 0))
    return pl.pallas_call(
        body, name=name, grid=(depth, nrow // rows),
        in_specs=[part_spec(l) for l in range(depth)] + [blk, blk, blk],
        out_specs=[blk] * 4,
        out_shape=[jax.ShapeDtypeStruct(w.shape, F32)] * 4,
        compiler_params=_ARB2,
    )(*parts, w, m, v)


def _pack_small(norm_w, mem_norm_w, out_norm_w, final_norm_w, b_forget):
    onw = jnp.pad(out_norm_w.reshape(20, LANES), ((0, 4), (0, 0)))
    b = jnp.pad(b_forget, ((0, 6), (0, LANES - FOX_HEADS)))
    return jnp.concatenate([norm_w.reshape(16, LANES), mem_norm_w.reshape(16, LANES), onw,
                            final_norm_w.reshape(8, LANES), b], axis=0)


def _unpack_small(p):
    return (p[0:16].reshape(2, D_MODEL), p[16:32].reshape(2, D_MODEL), p[32:52].reshape(2, MIX_W),
            p[56:64].reshape(D_MODEL), p[64:66, :FOX_HEADS])


def kernel(x, mem, norm_w, w_in, b_forget, mem_norm_w, w_mem_kv, out_norm_w, w_out, final_norm_w, loss_target, m_norm_w, m_w_in, m_b_forget, m_mem_norm_w, m_w_mem_kv, m_out_norm_w, m_w_out, m_final_norm_w, v_norm_w, v_w_in, v_b_forget, v_mem_norm_w, v_w_mem_kv, v_out_norm_w, v_w_out, v_final_norm_w):
    kv_rows = w_mem_kv.shape[1]
    out_rows = w_out.shape[1]
    me = _block_index(*_my_place())

    def shards(l):
        return [w_in[l].astype(BF16), w_mem_kv[l].astype(BF16), w_out[l].astype(BF16)]

    def lay_out(g_in, g_kv, g_out):
        w_full = jnp.transpose(g_in, (1, 0, 2)).reshape(D_MODEL, IN_W)
        return (_rearrange_w_in(w_full), g_kv.reshape(D_MODEL, 2 * MEM_W),
                g_out.reshape(MIX_W, D_MODEL))

    def blocks(gr):
        p_in = _restore_w_in(gr["w_r"]).reshape(D_MODEL, N_DEV, SHARD_W)
        return [jnp.transpose(p_in, (1, 0, 2)).astype(BF16),
                gr["w_mem_kv"].reshape(N_DEV, kv_rows, 2 * MEM_W).astype(BF16),
                gr["w_out"].reshape(N_DEV, out_rows, D_MODEL).astype(BF16)]

    def row(v):
        return v.reshape(1, -1)

    w_r0, wkv0, wout0 = lay_out(*_all_gather_weights(shards(0), "all_gather_l0"))
    shards1 = shards(1)
    gather1 = _exchange_start(shards1, False, "gather_l1_start")
    zero = gather1[-1][0:1, 0:1]
    x1, saved0 = _layer_fwd(x[0], mem[0], row(norm_w[0]) + zero, w_r0, b_forget[0],
                            row(mem_norm_w[0]), wkv0, row(out_norm_w[0]), wout0, 0)
    _, lands = _exchange_wait(gather1, x1, False, "gather_l1_wait")
    w_r1, wkv1, wout1 = lay_out(*[
        lax.dynamic_update_slice(land, own[None], (me,) + (0,) * own.ndim)
        for land, own in zip(lands, shards1)])
    x2, saved1 = _layer_fwd(x1, mem[0], row(norm_w[1]), w_r1, b_forget[1], row(mem_norm_w[1]),
                            wkv1, row(out_norm_w[1]), wout1, 1)

    dx2, loss, dfnw = _final_fwd_bwd(x2, row(final_norm_w), loss_target[0], "final_fwd_bwd")
    loss = lax.psum(loss[0, 0], ("x", "y", "c"))

    dx1, gr1 = _layer_bwd(dx2, saved1, mem[0], w_r1, wkv1, wout1, 1)
    scatter1 = _exchange_start(blocks(gr1), True, "grads_l1_start")
    zero = scatter1[-1][0:1, 0:1]
    saved0 = saved0[:3] + (saved0[3] + zero,) + saved0[4:]
    grad_x, gr0 = _layer_bwd(dx1, saved0, mem[0], w_r0, wkv0, wout0, 0)

    def both(name):
        return jnp.stack([gr0[name], gr1[name]])

    small = _pack_small(both("norm_w"), both("mem_norm_w"), both("out_norm_w"), dfnw[0],
                        both("b_forget"))
    p_small = jnp.broadcast_to(small[None], (N_DEV, SMALL_ROWS, LANES))
    r_in0, r_kv0, r_out0, r_small = _exchange_blocks(blocks(gr0) + [p_small], "exchange_grads_l0")
    sent, lands = _exchange_wait(scatter1, r_in0, True, "grads_l1_wait")
    r_in1, r_kv1, r_out1 = [
        lax.dynamic_update_slice(land, lax.dynamic_slice_in_dim(src, me, 1, axis=0),
                                 (me,) + (0,) * (land.ndim - 1))
        for land, src in zip(lands, sent)]

    g_w_in, d_w_in, nm_w_in, nv_w_in = _adamw_sum([r_in0, r_in1], w_in, m_w_in, v_w_in, 256,
                                                  "adamw_w_in")
    g_w_kv, d_w_kv, nm_w_kv, nv_w_kv = _adamw_sum([r_kv0, r_kv1], w_mem_kv, m_w_mem_kv, v_w_mem_kv,
                                                  kv_rows, "adamw_w_mem_kv")
    g_w_out, d_w_out, nm_w_out, nv_w_out = _adamw_sum([r_out0, r_out1], w_out, m_w_out, v_w_out,
                                                      out_rows, "adamw_w_out")
    w_small = _pack_small(norm_w, mem_norm_w, out_norm_w, final_norm_w, b_forget)[None]
    m_small = _pack_small(m_norm_w, m_mem_norm_w, m_out_norm_w, m_final_norm_w, m_b_forget)[None]
    v_small = _pack_small(v_norm_w, v_mem_norm_w, v_out_norm_w, v_final_norm_w, v_b_forget)[None]
    small_out = _adamw_sum([r_small], w_small, m_small, v_small, SMALL_ROWS, "adamw_small")
    (g_nw, g_mnw, g_onw, g_fnw, g_b), (d_nw, d_mnw, d_onw, d_fnw, d_b), \
        (nm_nw, nm_mnw, nm_onw, nm_fnw, nm_b), (nv_nw, nv_mnw, nv_onw, nv_fnw, nv_b) = [
            _unpack_small(t[0]) for t in small_out]

    return (loss, grad_x[None],
            g_nw, g_w_in, g_b, g_mnw, g_w_kv, g_onw, g_w_out, g_fnw,
            d_nw, d_w_in, d_b, d_mnw, d_w_kv, d_onw, d_w_out, d_fnw,
            nm_nw, nm_w_in, nm_b, nm_mnw, nm_w_kv, nm_onw, nm_w_out, nm_fnw,
            nv_nw, nv_w_in, nv_b, nv_mnw, nv_w_kv, nv_onw, nv_w_out, nv_fnw)
```

```python
import functools
from typing import NamedTuple

import jax
import jax.numpy as jnp
from jax import lax
from jax.experimental import pallas as pl
from jax.experimental.pallas import tpu as pltpu

F32 = jnp.float32
BF16 = jnp.bfloat16

N_DEV = 8
D_MODEL = 1024
HEAD_DIM = 64
LANES = 128
SB_W = 512
FX_W = 512
MEM_W = 256
MIX_W = 1280
FOX_HEADS = 8
IN_W = 4616
SHARD_W = IN_W // N_DEV
QKV_W = 3 * SB_W + 3 * FX_W + MEM_W
FL_PAD = 256
GF_W = MIX_W + FL_PAD
WR_W = QKV_W + GF_W
EPS = 1e-6
T = 256
TM = 256
Q_SCALE = 0.125
NEG = -1e30
UNDERFLOW = -110.0

ADAM_LR = 0.001
ADAM_B1 = 0.9
ADAM_B2 = 0.999
ADAM_EPS = 1e-08
ADAM_WD = 0.01
ADAM_STEP = 10

SMALL_ROWS = 72

_NT = (((1,), (1,)), ((), ()))
_TN = (((0,), (0,)), ((), ()))

_ARB1 = pltpu.CompilerParams(dimension_semantics=("arbitrary",))
_ARB2 = pltpu.CompilerParams(dimension_semantics=("arbitrary", "arbitrary"))


def _dot(a, b):
    return jnp.dot(a, b, preferred_element_type=F32)


def _dot_nt(a, b):
    return lax.dot_general(a, b, _NT, preferred_element_type=F32)


def _dot_tn(a, b):
    return lax.dot_general(a, b, _TN, preferred_element_type=F32)


def _split2(x):
    hi = x.astype(BF16)
    lo = (x - hi.astype(F32)).astype(BF16)
    return hi, lo


def _stack2(u):
    return jnp.concatenate([u, u], axis=0)


def _cum2(x, u2):
    hi, lo = _split2(x)
    return _dot(jnp.concatenate([hi, lo], axis=1), u2)


def _tri3(tri, x):
    hi = x.astype(BF16)
    r1 = x - hi.astype(F32)
    mid = r1.astype(BF16)
    lo = (r1 - mid.astype(F32)).astype(BF16)
    return _dot(tri, hi) + _dot(tri, mid) + _dot(tri, lo)


def _iota2(shape, dim):
    return lax.broadcasted_iota(jnp.int32, shape, dim)


def _head_block_diag():
    r = _iota2((LANES, LANES), 0) // HEAD_DIM
    c = _iota2((LANES, LANES), 1) // HEAD_DIM
    return _stack2(jnp.where(r == c, 1.0, 0.0).astype(BF16))


def _head_mean(x, bd):
    return _cum2(x, bd) * (1.0 / HEAD_DIM)


def _sigmoid(x):
    return 1.0 / (1.0 + jnp.exp(-x))


def _log_sigmoid(x):
    return jnp.minimum(x, 0.0) - jnp.log(1.0 + jnp.exp(-jnp.abs(x)))


def _running_top(r_ref):
    return jnp.max(jnp.maximum(r_ref[0], r_ref[1]))


def _head_norm_tops(x, m0):
    x2 = x.astype(F32)
    x2 = x2 * x2
    zero = jnp.zeros_like(x2)
    return (jnp.max(jnp.sqrt(jnp.sum(jnp.where(m0, x2, zero), axis=1, keepdims=True))),
            jnp.max(jnp.sqrt(jnp.sum(jnp.where(m0, zero, x2), axis=1, keepdims=True))))


def _fox_tiles_left(i, pair, qn, kn_ref, fb_ref, tile):
    def bound(j):
        b = [2.0 * qn[h] * kn_ref[h] + fb_ref[2 * pair + h, i] - fb_ref[FOX_HEADS + 2 * pair + h, j]
             for h in range(2)]
        return jnp.maximum(b[0], b[1])

    def more(j):
        return jnp.logical_and(j >= 0, bound(jnp.maximum(j, 0)) > UNDERFLOW)

    def step(j):
        tile(j, False)
        return j - 1

    return lax.while_loop(more, step, i - 1)


def _pair_masks():
    lane = _iota2((1, LANES), 1)
    return lane < HEAD_DIM


def _split_pair(x, m0):
    zero = jnp.zeros_like(x)
    return jnp.where(m0, x, zero), jnp.where(m0, zero, x)


def _inproj_fwd(x, nw, w_r, name):
    s = x.shape[0]

    def body(x_ref, nw_ref, w_ref, qkv_ref, gf_ref):
        xv = x_ref[...]
        r = lax.rsqrt(jnp.mean(xv * xv, axis=-1, keepdims=True) + EPS)
        h = (xv * r * nw_ref[...]).astype(BF16)
        for c in range(0, QKV_W, 256):
            qkv_ref[:, c:c + 256] = _dot(h, w_ref[:, c:c + 256]).astype(BF16)
        for c in range(0, GF_W, 256):
            gf_ref[:, c:c + 256] = _dot(h, w_ref[:, QKV_W + c:QKV_W + c + 256])

    return pl.pallas_call(
        body, name=name, grid=(s // TM,),
        in_specs=[pl.BlockSpec((TM, D_MODEL), lambda i: (i, 0)),
                  pl.BlockSpec((1, D_MODEL), lambda i: (0, 0)),
                  pl.BlockSpec((D_MODEL, WR_W), lambda i: (0, 0))],
        out_specs=[pl.BlockSpec((TM, QKV_W), lambda i: (i, 0)),
                   pl.BlockSpec((TM, GF_W), lambda i: (i, 0))],
        out_shape=[jax.ShapeDtypeStruct((s, QKV_W), BF16), jax.ShapeDtypeStruct((s, GF_W), F32)],
        compiler_params=_ARB1,
    )(x, nw, w_r)


def _fox_prep_fwd(gf, bpad, name):
    s = gf.shape[0]

    def body(fl_ref, b_ref, f_ref):
        tri = jnp.where(_iota2((T, T), 0) >= _iota2((T, T), 1), 1.0, 0.0).astype(BF16)
        carry = jnp.zeros((1, LANES), F32)
        for blk in range(s // T):
            lf = _log_sigmoid(fl_ref[blk * T:(blk + 1) * T, :] + b_ref[...])
            c = _tri3(tri, lf) + carry
            f_ref[blk * T:(blk + 1) * T, :] = c
            carry = c[T - 1:T, :]

    return pl.pallas_call(
        body, name=name, grid=(1,),
        in_specs=[pl.BlockSpec((s, LANES), lambda i: (0, MIX_W // LANES)),
                  pl.BlockSpec((1, LANES), lambda i: (0, 0))],
        out_specs=pl.BlockSpec((s, LANES), lambda i: (0, 0)),
        out_shape=jax.ShapeDtypeStruct((s, LANES), F32),
        compiler_params=_ARB1,
    )(gf, bpad)


def _sb_fwd(qkv, name, rider=None):
    s = qkv.shape[0]

    def body(q_ref, k_ref, v_ref, o_ref, acc_ref, r_ref, as_ref):
        i = pl.program_id(1)
        m0 = _pair_masks()
        qh = _split_pair(q_ref[...] * jnp.asarray(Q_SCALE, BF16), m0)
        strict = _iota2((T, T), 0) > _iota2((T, T), 1)
        u2 = _stack2(jnp.where(strict, 1.0, 0.0).astype(BF16))
        acc_ref[...] = jnp.zeros_like(acc_ref)
        r_ref[...] = jnp.zeros_like(r_ref)
        hs = range(2)

        def flush(j):
            v = v_ref[pl.ds(pl.multiple_of(j * T, T), T), :]
            for h in hs:
                acc_ref[h] += _dot(as_ref[h], v)

        def tile(j, diag):
            k = k_ref[pl.ds(pl.multiple_of(j * T, T), T), :]
            z = [_dot_nt(qh[h], k) for h in hs]
            if not diag:
                flush(j + 1)
            la = [jnp.minimum(z[h], 0.0) - jnp.log(1.0 + jnp.exp(-jnp.abs(z[h]))) for h in hs]
            lf = [la[h] - z[h] for h in hs]
            if diag:
                lf = [jnp.where(strict, lf[h], 0.0) for h in hs]
            cin = [_cum2(lf[h], u2) for h in hs]
            a = [jnp.exp(la[h] + cin[h] + r_ref[h]) for h in hs]
            if diag:
                a = [jnp.where(strict, a[h], 0.0) for h in hs]
            for h in hs:
                r_ref[h] += cin[h][:, 0:1] + lf[h][:, 0:1]
                as_ref[h] = a[h].astype(BF16)

        tile(i, True)

        def more(state):
            j, top = state
            return jnp.logical_and(j >= 0, top > UNDERFLOW)

        def step(state):
            j, _ = state
            tile(j, False)
            return j - 1, _running_top(r_ref)

        j_left, _ = lax.while_loop(more, step, (i - 1, _running_top(r_ref)))
        flush(j_left + 1)
        o_ref[...] = jnp.where(m0, acc_ref[0], acc_ref[1])

    nb = SB_W // LANES
    (ysb,), lands = _ride(dict(
        body=body, name=name, grid=(nb, s // T),
        in_specs=[pl.BlockSpec((T, LANES), lambda p, i: (i, p)),
                  pl.BlockSpec((s, LANES), lambda p, i: (0, nb + p)),
                  pl.BlockSpec((s, LANES), lambda p, i: (0, 2 * nb + p))],
        out_specs=[pl.BlockSpec((T, LANES), lambda p, i: (i, p))],
        out_shape=[jax.ShapeDtypeStruct((s, SB_W), F32)],
        scratch_shapes=[pltpu.VMEM((2, T, LANES), F32), pltpu.VMEM((2, T, 1), F32),
                        pltpu.VMEM((2, T, T), BF16)],
        compiler_params=_ARB2, operands=[qkv, qkv, qkv]), rider)
    return ysb, lands


def _fox_fwd(qkv, fqb, frow, fbounds, name, rider=None):
    s = qkv.shape[0]

    def body(q_ref, k_ref, v_ref, fq_ref, fr_ref, fb_ref, o_ref, lse_ref, acc_ref, m_ref, ps_ref,
             kn_ref):
        pair = pl.program_id(0)
        i = pl.program_id(1)
        m0 = _pair_masks()

        @pl.when(i == 0)
        def _():
            kn_ref[0], kn_ref[1] = _head_norm_tops(k_ref[...], m0)

        q = q_ref[...] * jnp.asarray(Q_SCALE, BF16)
        qn = _head_norm_tops(q, m0)
        qh = _split_pair(q, m0)
        fq = fq_ref[...]
        fqh = (fq[:, 0:1], fq[:, HEAD_DIM:HEAD_DIM + 1])
        causal = _iota2((T, T), 0) >= _iota2((T, T), 1)
        ones = jnp.ones((T, LANES), BF16)
        acc_ref[...] = jnp.zeros_like(acc_ref)
        m_ref[...] = jnp.full_like(m_ref, NEG)
        hs = range(2)

        def flush(j):
            v = v_ref[pl.ds(pl.multiple_of(j * T, T), T), :]
            va2 = _stack2(jnp.concatenate([v, ones], axis=1))
            for h in hs:
                acc_ref[h] += _dot(ps_ref[h], va2)

        def tile(j, diag):
            off = pl.multiple_of(j * T, T)
            k = k_ref[pl.ds(off, T), :]
            sc = [_dot_nt(qh[h], k) + fqh[h] - fr_ref[h:h + 1, pl.ds(off, T)] for h in hs]
            if not diag:
                flush(j + 1)
            if diag:
                sc = [jnp.where(causal, sc[h], NEG) for h in hs]
            m_new = [jnp.maximum(m_ref[h], jnp.max(sc[h], axis=1, keepdims=True)) for h in hs]
            p = [jnp.exp(sc[h] - m_new[h]) for h in hs]
            for h in hs:
                acc_ref[h] = acc_ref[h] * jnp.exp(m_ref[h] - m_new[h])
                m_ref[h] = m_new[h]
                p_hi, p_lo = _split2(p[h])
                ps_ref[h] = jnp.concatenate([p_hi, p_lo], axis=1)

        tile(i, True)
        j_left = _fox_tiles_left(i, pair, qn, kn_ref, fb_ref, tile)
        flush(j_left + 1)
        acc = (acc_ref[0], acc_ref[1])
        o_ref[...] = jnp.where(m0, acc[0][:, :LANES] / acc[0][:, LANES:],
                               acc[1][:, :LANES] / acc[1][:, LANES:])
        lse_ref[...] = jnp.where(m0, m_ref[0] + jnp.log(acc[0][:, LANES:]),
                                 m_ref[1] + jnp.log(acc[1][:, LANES:]))

    nb = FX_W // LANES
    base = 3 * SB_W // LANES
    (yfx, lse), lands = _ride(dict(
        body=body, name=name, grid=(nb, s // T),
        in_specs=[pl.BlockSpec((T, LANES), lambda p, i: (i, base + p)),
                  pl.BlockSpec((s, LANES), lambda p, i: (0, base + nb + p)),
                  pl.BlockSpec((s, LANES), lambda p, i: (0, base + 2 * nb + p)),
                  pl.BlockSpec((T, LANES), lambda p, i: (i, p)),
                  pl.BlockSpec((None, 2, s), lambda p, i: (p, 0, 0)),
                  pl.BlockSpec(memory_space=pltpu.SMEM)],
        out_specs=[pl.BlockSpec((T, LANES), lambda p, i: (i, p)),
                   pl.BlockSpec((T, LANES), lambda p, i: (i, p))],
        out_shape=[jax.ShapeDtypeStruct((s, FX_W), F32), jax.ShapeDtypeStruct((s, FX_W), F32)],
        scratch_shapes=[pltpu.VMEM((2, T, 2 * LANES), F32), pltpu.VMEM((2, T, 1), F32),
                        pltpu.VMEM((2, T, 2 * T), BF16), pltpu.SMEM((2,), F32)],
        compiler_params=_ARB2, operands=[qkv, qkv, qkv, fqb, frow, fbounds]), rider)
    return yfx, lse, lands


def _memkv_fwd(mem, mnw, wkv, name):
    n = mem.shape[0]

    def body(mem_ref, mnw_ref, w_ref, kv_ref):
        mv = mem_ref[...]
        r = lax.rsqrt(jnp.mean(mv * mv, axis=-1, keepdims=True) + EPS)
        hm = (mv * r * mnw_ref[...]).astype(BF16)
        kv_ref[...] = _dot(hm, w_ref[...]).astype(BF16)

    return pl.pallas_call(
        body, name=name, grid=(1,),
        in_specs=[pl.BlockSpec((n, D_MODEL), lambda i: (0, 0)),
                  pl.BlockSpec((1, D_MODEL), lambda i: (0, 0)),
                  pl.BlockSpec((D_MODEL, 2 * MEM_W), lambda i: (0, 0))],
        out_specs=pl.BlockSpec((n, 2 * MEM_W), lambda i: (0, 0)),
        out_shape=jax.ShapeDtypeStruct((n, 2 * MEM_W), BF16),
        compiler_params=_ARB1,
    )(mem, mnw, wkv)


def _mem_fwd(qkv, kv, name):
    s = qkv.shape[0]
    n = kv.shape[0]

    def body(q_ref, k_ref, v_ref, o_ref, lse_ref):
        m0 = _pair_masks()
        qh = _split_pair(q_ref[...] * jnp.asarray(Q_SCALE, BF16), m0)
        k = k_ref[...]
        v = v_ref[...]
        outs, lses = [], []
        for h in range(2):
            sc = _dot_nt(qh[h], k)
            mx = jnp.max(sc, axis=1, keepdims=True)
            p = jnp.exp(sc - mx)
            l = jnp.sum(p, axis=1, keepdims=True)
            outs.append(_dot(p.astype(BF16), v) / l)
            lses.append(mx + jnp.log(l))
        o_ref[...] = jnp.where(m0, outs[0], outs[1])
        lse_ref[...] = jnp.where(m0, lses[0], lses[1])

    nb = MEM_W // LANES
    base = (3 * SB_W + 3 * FX_W) // LANES
    return pl.pallas_call(
        body, name=name, grid=(nb, s // T),
        in_specs=[pl.BlockSpec((T, LANES), lambda p, i: (i, base + p)),
                  pl.BlockSpec((n, LANES), lambda p, i: (0, p)),
                  pl.BlockSpec((n, LANES), lambda p, i: (0, nb + p))],
        out_specs=[pl.BlockSpec((T, LANES), lambda p, i: (i, p)),
                   pl.BlockSpec((T, LANES), lambda p, i: (i, p))],
        out_shape=[jax.ShapeDtypeStruct((s, MEM_W), F32), jax.ShapeDtypeStruct((s, MEM_W), F32)],
        compiler_params=_ARB2,
    )(qkv, kv, kv)


def _mix_chunk(c, ysb_ref, yfx_ref, ym_ref):
    if c < SB_W // LANES:
        return ysb_ref[:, c * LANES:(c + 1) * LANES]
    c -= SB_W // LANES
    if c < FX_W // LANES:
        return yfx_ref[:, c * LANES:(c + 1) * LANES]
    c -= FX_W // LANES
    return ym_ref[:, c * LANES:(c + 1) * LANES]


def _outproj_fwd(ysb, yfx, ym, gf, onw, wout, x, name):
    s = x.shape[0]

    def body(ysb_ref, yfx_ref, ym_ref, g_ref, onw_ref, w_ref, x_ref, o_ref, yg_ref):
        bd = _head_block_diag()
        for c in range(MIX_W // LANES):
            sl = slice(c * LANES, (c + 1) * LANES)
            u = _mix_chunk(c, ysb_ref, yfx_ref, ym_ref)
            r = lax.rsqrt(_head_mean(u * u, bd) + EPS)
            g = g_ref[:, sl]
            yg_ref[:, sl] = (u * r * onw_ref[:, sl] * (g * _sigmoid(g))).astype(BF16)
        o_ref[...] = x_ref[...] + _dot(yg_ref[...], w_ref[...])

    return pl.pallas_call(
        body, name=name, grid=(s // TM,),
        in_specs=[pl.BlockSpec((TM, SB_W), lambda i: (i, 0)),
                  pl.BlockSpec((TM, FX_W), lambda i: (i, 0)),
                  pl.BlockSpec((TM, MEM_W), lambda i: (i, 0)),
                  pl.BlockSpec((TM, MIX_W), lambda i: (i, 0)),
                  pl.BlockSpec((1, MIX_W), lambda i: (0, 0)),
                  pl.BlockSpec((MIX_W, D_MODEL), lambda i: (0, 0)),
                  pl.BlockSpec((TM, D_MODEL), lambda i: (i, 0))],
        out_specs=pl.BlockSpec((TM, D_MODEL), lambda i: (i, 0)),
        out_shape=jax.ShapeDtypeStruct((s, D_MODEL), F32),
        scratch_shapes=[pltpu.VMEM((TM, MIX_W), BF16)],
        compiler_params=_ARB1,
    )(ysb, yfx, ym, gf, onw, wout, x)


def _final_fwd_bwd(x, fnw, target, name):
    s = x.shape[0]

    def body(x_ref, w_ref, t_ref, dx_ref, loss_ref, dw_ref):
        @pl.when(pl.program_id(0) == 0)
        def _():
            loss_ref[...] = jnp.zeros_like(loss_ref)
            dw_ref[...] = jnp.zeros_like(dw_ref)

        xv = x_ref[...]
        w = w_ref[...]
        r = lax.rsqrt(jnp.mean(xv * xv, axis=-1, keepdims=True) + EPS)
        xh = xv * r
        err = xh * w - t_ref[...]
        part = jnp.sum(jnp.sum(err * err, axis=1, keepdims=True), axis=0, keepdims=True)
        loss_ref[...] += part * (0.5 / D_MODEL)
        dy = err * (1.0 / D_MODEL)
        dw_ref[...] += jnp.sum(dy * xh, axis=0, keepdims=True)
        dxh = dy * w
        dx_ref[...] = r * (dxh - xh * jnp.mean(dxh * xh, axis=-1, keepdims=True))

    return pl.pallas_call(
        body, name=name, grid=(s // TM,),
        in_specs=[pl.BlockSpec((TM, D_MODEL), lambda i: (i, 0)),
                  pl.BlockSpec((1, D_MODEL), lambda i: (0, 0)),
                  pl.BlockSpec((TM, D_MODEL), lambda i: (i, 0))],
        out_specs=[pl.BlockSpec((TM, D_MODEL), lambda i: (i, 0)),
                   pl.BlockSpec((1, LANES), lambda i: (0, 0)),
                   pl.BlockSpec((1, D_MODEL), lambda i: (0, 0))],
        out_shape=[jax.ShapeDtypeStruct((s, D_MODEL), F32), jax.ShapeDtypeStruct((1, LANES), F32),
                   jax.ShapeDtypeStruct((1, D_MODEL), F32)],
        compiler_params=_ARB1,
    )(x, fnw, target)


def _outproj_bwd(dxo, wout, ysb, yfx, ym, gf, onw, name):
    s = dxo.shape[0]

    def body(dx_ref, w_ref, ysb_ref, yfx_ref, ym_ref, g_ref, onw_ref,
             dysb_ref, dyfx_ref, dym_ref, dg_ref, dw_ref, donw_ref, yg_ref):
        @pl.when(pl.program_id(0) == 0)
        def _():
            dw_ref[...] = jnp.zeros_like(dw_ref)
            donw_ref[...] = jnp.zeros_like(donw_ref)

        dxb = dx_ref[...].astype(BF16)
        dyg = _dot_nt(dxb, w_ref[...])
        bd = _head_block_diag()
        for c in range(MIX_W // LANES):
            sl = slice(c * LANES, (c + 1) * LANES)
            u = _mix_chunk(c, ysb_ref, yfx_ref, ym_ref)
            r = lax.rsqrt(_head_mean(u * u, bd) + EPS)
            yn = u * r
            g = g_ref[:, sl]
            sg = _sigmoid(g)
            sil = g * sg
            onw = onw_ref[:, sl]
            e = dyg[:, sl]
            yg_ref[:, sl] = (yn * onw * sil).astype(BF16)
            donw_ref[:, sl] += jnp.sum(e * yn * sil, axis=0, keepdims=True)
            dg_ref[:, sl] = (e * yn * onw * (sg * (1.0 + g * (1.0 - sg)))).astype(BF16)
            dyn = e * onw * sil
            du = (r * (dyn - yn * _head_mean(dyn * yn, bd))).astype(BF16)
            if c < 4:
                dysb_ref[:, c * LANES:(c + 1) * LANES] = du
            elif c < 8:
                dyfx_ref[:, (c - 4) * LANES:(c - 3) * LANES] = du
            else:
                dym_ref[:, (c - 8) * LANES:(c - 7) * LANES] = du
        dw_ref[...] += _dot_tn(yg_ref[...], dxb)

    return pl.pallas_call(
        body, name=name, grid=(s // TM,),
        in_specs=[pl.BlockSpec((TM, D_MODEL), lambda i: (i, 0)),
                  pl.BlockSpec((MIX_W, D_MODEL), lambda i: (0, 0)),
                  pl.BlockSpec((TM, SB_W), lambda i: (i, 0)),
                  pl.BlockSpec((TM, FX_W), lambda i: (i, 0)),
                  pl.BlockSpec((TM, MEM_W), lambda i: (i, 0)),
                  pl.BlockSpec((TM, MIX_W), lambda i: (i, 0)),
                  pl.BlockSpec((1, MIX_W), lambda i: (0, 0))],
        out_specs=[pl.BlockSpec((TM, SB_W), lambda i: (i, 0)),
                   pl.BlockSpec((TM, FX_W), lambda i: (i, 0)),
                   pl.BlockSpec((TM, MEM_W), lambda i: (i, 0)),
                   pl.BlockSpec((TM, MIX_W), lambda i: (i, 0)),
                   pl.BlockSpec((MIX_W, D_MODEL), lambda i: (0, 0)),
                   pl.BlockSpec((1, MIX_W), lambda i: (0, 0))],
        out_shape=[jax.ShapeDtypeStruct((s, SB_W), BF16), jax.ShapeDtypeStruct((s, FX_W), BF16),
                   jax.ShapeDtypeStruct((s, MEM_W), BF16), jax.ShapeDtypeStruct((s, MIX_W), BF16),
                   jax.ShapeDtypeStruct((MIX_W, D_MODEL), F32), jax.ShapeDtypeStruct((1, MIX_W), F32)],
        scratch_shapes=[pltpu.VMEM((TM, MIX_W), BF16)],
        compiler_params=_ARB1,
    )(dxo, wout, ysb, yfx, ym, gf, onw)


def _row_dots(do_ref, o_ref, m0):
    prod = do_ref[...].astype(F32) * o_ref[...]
    zero = jnp.zeros_like(prod)
    return (jnp.sum(jnp.where(m0, prod, zero), axis=1, keepdims=True),
            jnp.sum(jnp.where(m0, zero, prod), axis=1, keepdims=True))


def _sb_bwd(qkv, o, do, name, rider=None):
    s = qkv.shape[0]
    nq = s // T

    def body(q_ref, k_ref, v_ref, o_ref, do_ref, dq_ref, dk_ref, dv_ref,
             dqa_ref, dka_ref, dva_ref, rl_ref, rg_ref, dzs_ref, abs_ref):
        i = pl.program_id(1)

        @pl.when(i == 0)
        def _():
            dka_ref[...] = jnp.zeros_like(dka_ref)
            dva_ref[...] = jnp.zeros_like(dva_ref)

        m0 = _pair_masks()
        qh = _split_pair(q_ref[...] * jnp.asarray(Q_SCALE, BF16), m0)
        doh = _split_pair(do_ref[...], m0)
        dsum = _row_dots(do_ref, o_ref, m0)
        strict = _iota2((T, T), 0) > _iota2((T, T), 1)
        u2 = _stack2(jnp.where(strict, 1.0, 0.0).astype(BF16))
        dqa_ref[...] = jnp.zeros_like(dqa_ref)
        rl_ref[...] = jnp.zeros_like(rl_ref)
        rg_ref[...] = jnp.zeros_like(rg_ref)

        hs = range(2)

        def flush(j):
            off = pl.multiple_of(j * T, T)
            k = k_ref[pl.ds(off, T), :]
            for h in hs:
                dqa_ref[h] += _dot(dzs_ref[h], k)
            dka_ref[pl.ds(off, T), :] += _dot_tn(dzs_ref[0], qh[0]) + _dot_tn(dzs_ref[1], qh[1])
            dva_ref[pl.ds(off, T), :] += _dot_tn(abs_ref[0], doh[0]) + _dot_tn(abs_ref[1], doh[1])

        def tile(j, diag):
            off = pl.multiple_of(j * T, T)
            k = k_ref[pl.ds(off, T), :]
            v = v_ref[pl.ds(off, T), :]
            z = [_dot_nt(qh[h], k) for h in hs]
            da = [_dot_nt(doh[h], v) for h in hs]
            if not diag:
                flush(j + 1)
            la = [jnp.minimum(z[h], 0.0) - jnp.log(1.0 + jnp.exp(-jnp.abs(z[h]))) for h in hs]
            lf = [la[h] - z[h] for h in hs]
            if diag:
                lf = [jnp.where(strict, lf[h], 0.0) for h in hs]
            cin = [_cum2(lf[h], u2) for h in hs]
            a = [jnp.exp(la[h] + cin[h] + rl_ref[h]) for h in hs]
            if diag:
                a = [jnp.where(strict, a[h], 0.0) for h in hs]
            ab = [a[h].astype(BF16) for h in hs]
            g = [ab[h].astype(F32) * da[h] for h in hs]
            gin = [_cum2(g[h], u2) for h in hs]
            dz = [g[h] - jnp.exp(la[h]) * ((dsum[h] - rg_ref[h]) - gin[h]) for h in hs]
            if diag:
                dz = [jnp.where(strict, dz[h], 0.0) for h in hs]
            for h in hs:
                rl_ref[h] += cin[h][:, 0:1] + lf[h][:, 0:1]
                rg_ref[h] += gin[h][:, 0:1] + g[h][:, 0:1]
                dzs_ref[h] = dz[h].astype(BF16)
                abs_ref[h] = ab[h]

        tile(i, True)

        def more(state):
            j, top = state
            return jnp.logical_and(j >= 0, top > UNDERFLOW)

        def step(state):
            j, _ = state
            tile(j, False)
            return j - 1, _running_top(rl_ref)

        j_left, _ = lax.while_loop(more, step, (i - 1, _running_top(rl_ref)))
        flush(j_left + 1)
        dq_ref[...] = (jnp.where(m0, dqa_ref[0], dqa_ref[1]) * Q_SCALE).astype(BF16)

        @pl.when(i == nq - 1)
        def _():
            dk_ref[...] = dka_ref[...].astype(BF16)
            dv_ref[...] = dva_ref[...].astype(BF16)

    nb = SB_W // LANES
    (dq, dk, dv), lands = _ride(dict(
        body=body, name=name, grid=(nb, nq),
        in_specs=[pl.BlockSpec((T, LANES), lambda p, i: (i, p)),
                  pl.BlockSpec((s, LANES), lambda p, i: (0, nb + p)),
                  pl.BlockSpec((s, LANES), lambda p, i: (0, 2 * nb + p)),
                  pl.BlockSpec((T, LANES), lambda p, i: (i, p)),
                  pl.BlockSpec((T, LANES), lambda p, i: (i, p))],
        out_specs=[pl.BlockSpec((T, LANES), lambda p, i: (i, p)),
                   pl.BlockSpec((s, LANES), lambda p, i: (0, p)),
                   pl.BlockSpec((s, LANES), lambda p, i: (0, p))],
        out_shape=[jax.ShapeDtypeStruct((s, SB_W), BF16)] * 3,
        scratch_shapes=[pltpu.VMEM((2, T, LANES), F32), pltpu.VMEM((s, LANES), F32),
                        pltpu.VMEM((s, LANES), F32), pltpu.VMEM((2, T, 1), F32),
                        pltpu.VMEM((2, T, 1), F32), pltpu.VMEM((2, T, T), BF16),
                        pltpu.VMEM((2, T, T), BF16)],
        compiler_params=_ARB2, operands=[qkv, qkv, qkv, o, do]), rider)
    return dq, dk, dv, lands


def _fox_bwd(qkv, fqb, frow, fbounds, o, lse, do, name, rider=None):
    s = qkv.shape[0]
    nq = s // T

    def body(q_ref, k_ref, v_ref, fq_ref, fr_ref, fb_ref, o_ref, lse_ref, do_ref,
             dq_ref, dk_ref, dv_ref, df_ref, dqa_ref, dka_ref, dva_ref, dfa_ref, dls_ref, pbs_ref,
             kn_ref):
        i = pl.program_id(1)

        @pl.when(i == 0)
        def _():
            dka_ref[...] = jnp.zeros_like(dka_ref)
            dva_ref[...] = jnp.zeros_like(dva_ref)
            dfa_ref[...] = jnp.zeros_like(dfa_ref)

        m0 = _pair_masks()

        @pl.when(i == 0)
        def _():
            kn_ref[0], kn_ref[1] = _head_norm_tops(k_ref[...], m0)

        q = q_ref[...] * jnp.asarray(Q_SCALE, BF16)
        qn = _head_norm_tops(q, m0)
        qh = _split_pair(q, m0)
        doh = _split_pair(do_ref[...], m0)
        dsum = _row_dots(do_ref, o_ref, m0)
        fq = fq_ref[...]
        fqh = (fq[:, 0:1], fq[:, HEAD_DIM:HEAD_DIM + 1])
        lse = lse_ref[...]
        lseh = (lse[:, 0:1], lse[:, HEAD_DIM:HEAD_DIM + 1])
        causal = _iota2((T, T), 0) >= _iota2((T, T), 1)
        dqa_ref[...] = jnp.zeros_like(dqa_ref)

        hs = range(2)

        def flush(j):
            off = pl.multiple_of(j * T, T)
            k = k_ref[pl.ds(off, T), :]
            for h in hs:
                dqa_ref[h] += _dot(dls_ref[h], k)
            dka_ref[pl.ds(off, T), :] += _dot_tn(dls_ref[0], qh[0]) + _dot_tn(dls_ref[1], qh[1])
            dva_ref[pl.ds(off, T), :] += _dot_tn(pbs_ref[0], doh[0]) + _dot_tn(pbs_ref[1], doh[1])

        def tile(j, diag):
            off = pl.multiple_of(j * T, T)
            k = k_ref[pl.ds(off, T), :]
            v = v_ref[pl.ds(off, T), :]
            sc = [_dot_nt(qh[h], k) + fqh[h] - fr_ref[h:h + 1, pl.ds(off, T)] for h in hs]
            dp = [_dot_nt(doh[h], v) for h in hs]
            if not diag:
                flush(j + 1)
            p = [jnp.exp(sc[h] - lseh[h]) for h in hs]
            if diag:
                p = [jnp.where(causal, p[h], 0.0) for h in hs]
            dl = [p[h] * (dp[h] - dsum[h]) for h in hs]
            for h in hs:
                dls_ref[h] = dl[h].astype(BF16)
                pbs_ref[h] = p[h].astype(BF16)
                dfa_ref[h:h + 1, pl.ds(off, T)] -= jnp.sum(dl[h], axis=0, keepdims=True)

        tile(i, True)
        j_left = _fox_tiles_left(i, pl.program_id(0), qn, kn_ref, fb_ref, tile)
        flush(j_left + 1)
        dq_ref[...] = (jnp.where(m0, dqa_ref[0], dqa_ref[1]) * Q_SCALE).astype(BF16)

        @pl.when(i == nq - 1)
        def _():
            dk_ref[...] = dka_ref[...].astype(BF16)
            dv_ref[...] = dva_ref[...].astype(BF16)
            df_ref[...] = dfa_ref[...]

    nb = FX_W // LANES
    base = 3 * SB_W // LANES
    (dq, dk, dv, df), lands = _ride(dict(
        body=body, name=name, grid=(nb, nq),
        in_specs=[pl.BlockSpec((T, LANES), lambda p, i: (i, base + p)),
                  pl.BlockSpec((s, LANES), lambda p, i: (0, base + nb + p)),
                  pl.BlockSpec((s, LANES), lambda p, i: (0, base + 2 * nb + p)),
                  pl.BlockSpec((T, LANES), lambda p, i: (i, p)),
                  pl.BlockSpec((None, 2, s), lambda p, i: (p, 0, 0)),
                  pl.BlockSpec(memory_space=pltpu.SMEM),
                  pl.BlockSpec((T, LANES), lambda p, i: (i, p)),
                  pl.BlockSpec((T, LANES), lambda p, i: (i, p)),
                  pl.BlockSpec((T, LANES), lambda p, i: (i, p))],
        out_specs=[pl.BlockSpec((T, LANES), lambda p, i: (i, p)),
                   pl.BlockSpec((s, LANES), lambda p, i: (0, p)),
                   pl.BlockSpec((s, LANES), lambda p, i: (0, p)),
                   pl.BlockSpec((None, 2, s), lambda p, i: (p, 0, 0))],
        out_shape=[jax.ShapeDtypeStruct((s, FX_W), BF16)] * 3
        + [jax.ShapeDtypeStruct((nb, 2, s), F32)],
        scratch_shapes=[pltpu.VMEM((2, T, LANES), F32), pltpu.VMEM((s, LANES), F32),
                        pltpu.VMEM((s, LANES), F32), pltpu.VMEM((2, s), F32),
                        pltpu.VMEM((2, T, T), BF16), pltpu.VMEM((2, T, T), BF16),
                        pltpu.SMEM((2,), F32)],
        compiler_params=_ARB2, operands=[qkv, qkv, qkv, fqb, frow, fbounds, o, lse, do]), rider)
    return dq, dk, dv, df, lands


def _fox_prep_bwd(dfcol, gf, bpad, name):
    s = gf.shape[0]

    def body(df_ref, fl_ref, b_ref, dfl_ref, db_ref):
        tri = jnp.where(_iota2((T, T), 0) <= _iota2((T, T), 1), 1.0, 0.0).astype(BF16)
        carry = jnp.zeros((1, LANES), F32)
        db = jnp.zeros((1, LANES), F32)
        for blk in reversed(range(s // T)):
            rows = slice(blk * T, (blk + 1) * T)
            c = _tri3(tri, df_ref[rows, :]) + carry
            carry = c[0:1, :]
            dfl = c / (1.0 + jnp.exp(fl_ref[rows, :] + b_ref[...]))
            dfl_ref[rows, :] = dfl.astype(BF16)
            db = db + jnp.sum(dfl, axis=0, keepdims=True)
        db_ref[...] = db

    return pl.pallas_call(
        body, name=name, grid=(1,),
        in_specs=[pl.BlockSpec((s, LANES), lambda i: (0, 0)),
                  pl.BlockSpec((s, LANES), lambda i: (0, MIX_W // LANES)),
                  pl.BlockSpec((1, LANES), lambda i: (0, 0))],
        out_specs=[pl.BlockSpec((s, LANES), lambda i: (0, 0)),
                   pl.BlockSpec((1, LANES), lambda i: (0, 0))],
        out_shape=[jax.ShapeDtypeStruct((s, LANES), BF16), jax.ShapeDtypeStruct((1, LANES), F32)],
        compiler_params=_ARB1,
    )(dfcol, gf, bpad)


def _mem_bwd(qkv, kv, o, lse, do, name):
    s = qkv.shape[0]
    n = kv.shape[0]
    nq = s // T

    def body(q_ref, k_ref, v_ref, o_ref, lse_ref, do_ref, dq_ref, dk_ref, dv_ref):
        @pl.when(pl.program_id(1) == 0)
        def _():
            dk_ref[...] = jnp.zeros_like(dk_ref)
            dv_ref[...] = jnp.zeros_like(dv_ref)

        m0 = _pair_masks()
        qh = _split_pair(q_ref[...] * jnp.asarray(Q_SCALE, BF16), m0)
        doh = _split_pair(do_ref[...], m0)
        dsum = _row_dots(do_ref, o_ref, m0)
        lse = lse_ref[...]
        lseh = (lse[:, 0:1], lse[:, HEAD_DIM:HEAD_DIM + 1])
        k = k_ref[...]
        v = v_ref[...]
        dqs = []
        for h in range(2):
            p = jnp.exp(_dot_nt(qh[h], k) - lseh[h])
            dl = p * (_dot_nt(doh[h], v) - dsum[h])
            dlb = dl.astype(BF16)
            dqs.append(_dot(dlb, k))
            dk_ref[...] += _dot_tn(dlb, qh[h])
            dv_ref[...] += _dot_tn(p.astype(BF16), doh[h])
        dq_ref[...] = (jnp.where(m0, dqs[0], dqs[1]) * Q_SCALE).astype(BF16)

    nb = MEM_W // LANES
    base = (3 * SB_W + 3 * FX_W) // LANES
    return pl.pallas_call(
        body, name=name, grid=(nb, nq),
        in_specs=[pl.BlockSpec((T, LANES), lambda p, i: (i, base + p)),
                  pl.BlockSpec((n, LANES), lambda p, i: (0, p)),
                  pl.BlockSpec((n, LANES), lambda p, i: (0, nb + p)),
                  pl.BlockSpec((T, LANES), lambda p, i: (i, p)),
                  pl.BlockSpec((T, LANES), lambda p, i: (i, p)),
                  pl.BlockSpec((T, LANES), lambda p, i: (i, p))],
        out_specs=[pl.BlockSpec((T, LANES), lambda p, i: (i, p)),
                   pl.BlockSpec((n, LANES), lambda p, i: (0, p)),
                   pl.BlockSpec((n, LANES), lambda p, i: (0, p))],
        out_shape=[jax.ShapeDtypeStruct((s, MEM_W), BF16), jax.ShapeDtypeStruct((n, MEM_W), F32),
                   jax.ShapeDtypeStruct((n, MEM_W), F32)],
        compiler_params=_ARB2,
    )(qkv, kv, kv, o, lse, do)


def _memkv_bwd(mem, mnw, wkv, dk, dv, name):
    n = mem.shape[0]

    def body(mem_ref, mnw_ref, w_ref, dk_ref, dv_ref, dw_ref, dmnw_ref):
        mv = mem_ref[...]
        r = lax.rsqrt(jnp.mean(mv * mv, axis=-1, keepdims=True) + EPS)
        mh = mv * r
        hm = (mh * mnw_ref[...]).astype(BF16)
        dkv = jnp.concatenate([dk_ref[...], dv_ref[...]], axis=1).astype(BF16)
        dw_ref[...] = _dot_tn(hm, dkv)
        dhm = _dot_nt(dkv, w_ref[...])
        dmnw_ref[...] = jnp.sum(dhm * mh, axis=0, keepdims=True)

    return pl.pallas_call(
        body, name=name, grid=(1,),
        in_specs=[pl.BlockSpec((n, D_MODEL), lambda i: (0, 0)),
                  pl.BlockSpec((1, D_MODEL), lambda i: (0, 0)),
                  pl.BlockSpec((D_MODEL, 2 * MEM_W), lambda i: (0, 0)),
                  pl.BlockSpec((n, MEM_W), lambda i: (0, 0)),
                  pl.BlockSpec((n, MEM_W), lambda i: (0, 0))],
        out_specs=[pl.BlockSpec((D_MODEL, 2 * MEM_W), lambda i: (0, 0)),
                   pl.BlockSpec((1, D_MODEL), lambda i: (0, 0))],
        out_shape=[jax.ShapeDtypeStruct((D_MODEL, 2 * MEM_W), F32),
                   jax.ShapeDtypeStruct((1, D_MODEL), F32)],
        compiler_params=_ARB1,
    )(mem, mnw, wkv, dk, dv)


def _inproj_bwd_dx(pieces, w_r, x, nw, dxo, name):
    s = x.shape[0]
    n = len(pieces)
    widths = [p.shape[1] for p in pieces]

    def body(*refs):
        piece_refs = refs[:n]
        w_ref, x_ref, nw_ref, dxo_ref, dx_ref, ht_ref, dnw_ref, dp_ref = refs[n:]

        @pl.when(pl.program_id(0) == 0)
        def _():
            dnw_ref[...] = jnp.zeros_like(dnw_ref)

        col = 0
        for r, wd in zip(piece_refs, widths):
            dp_ref[:, col:col + wd] = r[...]
            col += wd
        dp_ref[:, col:] = jnp.zeros((TM, WR_W - col), BF16)
        dh = _dot_nt(dp_ref[...], w_ref[...])
        xv = x_ref[...]
        nw = nw_ref[...]
        r = lax.rsqrt(jnp.mean(xv * xv, axis=-1, keepdims=True) + EPS)
        xh = xv * r
        ht_ref[...] = (xh * nw).T.astype(BF16)
        dnw_ref[...] += jnp.sum(dh * xh, axis=0, keepdims=True)
        dxh = dh * nw
        dx_ref[...] = r * (dxh - xh * jnp.mean(dxh * xh, axis=-1, keepdims=True)) + dxo_ref[...]

    return pl.pallas_call(
        body, name=name, grid=(s // TM,),
        in_specs=[pl.BlockSpec((TM, wd), lambda i: (i, 0)) for wd in widths]
        + [pl.BlockSpec((D_MODEL, WR_W), lambda i: (0, 0)),
           pl.BlockSpec((TM, D_MODEL), lambda i: (i, 0)),
           pl.BlockSpec((1, D_MODEL), lambda i: (0, 0)),
           pl.BlockSpec((TM, D_MODEL), lambda i: (i, 0))],
        out_specs=[pl.BlockSpec((TM, D_MODEL), lambda i: (i, 0)),
                   pl.BlockSpec((D_MODEL, TM), lambda i: (0, i)),
                   pl.BlockSpec((1, D_MODEL), lambda i: (0, 0)),
                   pl.BlockSpec((TM, WR_W), lambda i: (i, 0))],
        out_shape=[jax.ShapeDtypeStruct((s, D_MODEL), F32), jax.ShapeDtypeStruct((D_MODEL, s), BF16),
                   jax.ShapeDtypeStruct((1, D_MODEL), F32), jax.ShapeDtypeStruct((s, WR_W), BF16)],
        compiler_params=_ARB1,
    )(*pieces, w_r, x, nw, dxo)


def _inproj_bwd_dw(ht, dproj, name):
    s = dproj.shape[0]
    tn = 256

    def body(ht_ref, dp_ref, dw_ref):
        dw_ref[...] = _dot(ht_ref[...], dp_ref[...])

    return pl.pallas_call(
        body, name=name, grid=(WR_W // tn,),
        in_specs=[pl.BlockSpec((D_MODEL, s), lambda j: (0, 0)),
                  pl.BlockSpec((s, tn), lambda j: (0, j))],
        out_specs=pl.BlockSpec((D_MODEL, tn), lambda j: (0, j)),
        out_shape=jax.ShapeDtypeStruct((D_MODEL, WR_W), F32),
        compiler_params=_ARB1,
    )(ht, dproj)


def _rearrange_w_in(w):
    pad = jnp.zeros(w.shape[:-1] + (FL_PAD - FOX_HEADS,), w.dtype)
    return jnp.concatenate(
        [w[..., :3072], w[..., 3080:3336], w[..., 3336:IN_W], w[..., 3072:3080], pad], axis=-1)


def _restore_w_in(g):
    gate0 = QKV_W
    fl0 = QKV_W + MIX_W
    return jnp.concatenate(
        [g[..., :3072], g[..., fl0:fl0 + FOX_HEADS], g[..., 3072:QKV_W], g[..., gate0:fl0]], axis=-1)


def _pad_lanes(v, width=LANES):
    return jnp.pad(v, (0, width - v.shape[0])).reshape(1, width)


def _layer_fwd(xs, mem, nw, w_r, b_forget, mnw, wkv, onw, wout, l, travel=None):
    s = xs.shape[0]
    bpad = _pad_lanes(b_forget)
    qkv, gf = _inproj_fwd(xs, nw, w_r, f"inproj_fwd_{l}")
    f = _fox_prep_fwd(gf, bpad, f"fox_prep_fwd_{l}")
    f8 = f[:, :FOX_HEADS]
    fqb = jnp.repeat(f8, HEAD_DIM, axis=1)
    frow = f8.T.reshape(FOX_HEADS // 2, 2, s)
    fbounds = jnp.concatenate([f8[0::T].T, f8[T - 1::T].T], axis=0)
    travel = _Travel(travel)
    ysb, _ = travel.ride(0, _sb_fwd, qkv, f"sb_fwd_{l}")
    yfx, lse_fx, _ = travel.ride(1, _fox_fwd, qkv, fqb, frow, fbounds, f"fox_fwd_{l}")
    kv = _memkv_fwd(mem, mnw, wkv, f"memkv_fwd_{l}")
    ym, lse_m = _mem_fwd(qkv, kv, f"mem_fwd_{l}")
    xn = _outproj_fwd(ysb, yfx, ym, gf, onw, wout, xs, f"outproj_fwd_{l}")
    saved = (xs, nw, mnw, onw, bpad, qkv, gf, fqb, frow, fbounds, ysb, yfx, lse_fx, kv, ym, lse_m)
    return xn, saved, travel.lands


class _Travel:
    def __init__(self, plan):
        self.plan = plan
        self.lands = None if plan is None else _new_lands(plan[0], plan[1])

    def ride(self, n, fn, *args):
        if self.plan is None:
            return fn(*args)
        srcs, scatter, legs = self.plan
        idx, rows = legs[n]
        out = fn(*args, rider=_Rider([srcs[a] for a in idx], [self.lands[a] for a in idx],
                                     scatter, rows))
        for a, land in zip(idx, out[-1]):
            self.lands[a] = land
        return out


def _layer_bwd(dx, saved, mem, w_r, wkv, wout, l, travel=None):
    xs, nw, mnw, onw, bpad, qkv, gf, fqb, frow, fbounds, ysb, yfx, lse_fx, kv, ym, lse_m = saved
    s = xs.shape[0]
    travel = _Travel(travel)
    dysb, dyfx, dym, dgate, dwout, donw = _outproj_bwd(
        dx, wout, ysb, yfx, ym, gf, onw, f"outproj_bwd_{l}")
    sdq, sdk, sdv, _ = travel.ride(0, _sb_bwd, qkv, ysb, dysb, f"sb_bwd_{l}")
    fdq, fdk, fdv, dfrow, _ = travel.ride(1, _fox_bwd, qkv, fqb, frow, fbounds, yfx, lse_fx, dyfx,
                                          f"fox_bwd_{l}")
    dfcol = jnp.pad(dfrow.reshape(FOX_HEADS, s).T, ((0, 0), (0, LANES - FOX_HEADS)))
    dfl, db = _fox_prep_bwd(dfcol, gf, bpad, f"fox_prep_bwd_{l}")
    dmq, dmk, dmv = _mem_bwd(qkv, kv, ym, lse_m, dym, f"mem_bwd_{l}")
    dwkv, dmnw = _memkv_bwd(mem, mnw, wkv, dmk, dmv, f"memkv_bwd_{l}")
    dx, ht, dnw, dproj = _inproj_bwd_dx([sdq, sdk, sdv, fdq, fdk, fdv, dmq, dgate, dfl],
                                        w_r, xs, nw, dx, f"inproj_bwd_dx_{l}")
    dwr = _inproj_bwd_dw(ht, dproj, f"inproj_bwd_dw_{l}")
    grads = dict(norm_w=dnw[0], w_r=dwr, b_forget=db[0, :FOX_HEADS], mem_norm_w=dmnw[0],
                 w_mem_kv=dwkv, out_norm_w=donw[0], w_out=dwout)
    return dx, grads, travel.lands


_ANY = pl.BlockSpec(memory_space=pl.ANY)


def _my_place():
    return lax.axis_index("x"), lax.axis_index("y"), lax.axis_index("c")


def _flip(v, bit):
    return 1 - v if bit else v


def _block_index(px, py, pc):
    return 4 * px + 2 * py + pc


def _all_gather_weights(shards, name):
    n = len(shards)

    def body(*refs):
        ins, outs = refs[:n], refs[n:2 * n]
        send_sems, recv_sems, local_sems = refs[2 * n:]
        x, y, c = _my_place()
        me = (x, y, c)
        sibling = (x, y, 1 - c)
        chips = [(1 - x, y), (x, 1 - y), (1 - x, 1 - y)]

        def copy(a, k, block, to, src=None):
            dst = outs[a].at[_block_index(*block)]
            return pltpu.make_async_remote_copy(
                src_ref=dst if src is None else src, dst_ref=dst,
                send_sem=send_sems.at[a, k], recv_sem=recv_sems.at[a, k],
                device_id=to, device_id_type=pl.DeviceIdType.MESH)

        mine = [pltpu.make_async_copy(ins[a], outs[a].at[_block_index(*me)], local_sems.at[a])
                for a in range(n)]
        for cp in mine:
            cp.start()
        first = []
        for a in range(n):
            first.append(copy(a, 0, me, sibling, src=ins[a]))
            first += [copy(a, 1 + j, me, (*chip, c), src=ins[a]) for j, chip in enumerate(chips)]
        for cp in first:
            cp.start()
        passed = []
        for j, chip in enumerate(chips):
            for a in range(n):
                copy(a, 1 + j, (*chip, c), me).wait_recv()
                fwd = copy(a, 4 + j, (*chip, c), sibling)
                fwd.start()
                passed.append(fwd)
        for a in range(n):
            copy(a, 0, sibling, me).wait_recv()
            for j, chip in enumerate(chips):
                copy(a, 4 + j, (*chip, 1 - c), me).wait_recv()
        for cp in first + passed:
            cp.wait_send()
        for cp in mine:
            cp.wait()

    return pl.pallas_call(
        body, name=name,
        in_specs=[_ANY] * n, out_specs=[_ANY] * n,
        out_shape=[jax.ShapeDtypeStruct((N_DEV,) + v.shape, v.dtype) for v in shards],
        scratch_shapes=[pltpu.SemaphoreType.DMA((n, 7)), pltpu.SemaphoreType.DMA((n, 7)),
                        pltpu.SemaphoreType.DMA((n,))],
    )(*shards)


def _exchange_blocks(blocked, name):
    n = len(blocked)

    def body(*refs):
        ins, outs = refs[:n], refs[n:2 * n]
        send_sems, recv_sems, local_sems = refs[2 * n:]
        x, y, c = _my_place()
        mine_idx = _block_index(x, y, c)
        local = [pltpu.make_async_copy(ins[a].at[mine_idx], outs[a].at[mine_idx], local_sems.at[a])
                 for a in range(n)]
        for cp in local:
            cp.start()
        sends, arrivals = [], []
        for r in range(1, N_DEV):
            peer = (_flip(x, r & 4), _flip(y, r & 2), _flip(c, r & 1))
            peer_idx = _block_index(*peer)
            for a in range(n):
                sems = dict(send_sem=send_sems.at[a, r - 1], recv_sem=recv_sems.at[a, r - 1],
                            device_id=peer, device_id_type=pl.DeviceIdType.MESH)
                sends.append(pltpu.make_async_remote_copy(
                    src_ref=ins[a].at[peer_idx], dst_ref=outs[a].at[mine_idx], **sems))
                arrivals.append(pltpu.make_async_remote_copy(
                    src_ref=ins[a].at[peer_idx], dst_ref=outs[a].at[peer_idx], **sems))
        for cp in sends:
            cp.start()
        for cp in arrivals:
            cp.wait_recv()
        for cp in sends:
            cp.wait_send()
        for cp in local:
            cp.wait()

    return pl.pallas_call(
        body, name=name,
        in_specs=[_ANY] * n, out_specs=[_ANY] * n,
        out_shape=[jax.ShapeDtypeStruct(v.shape, v.dtype) for v in blocked],
        scratch_shapes=[pltpu.SemaphoreType.DMA((n, 7)), pltpu.SemaphoreType.DMA((n, 7)),
                        pltpu.SemaphoreType.DMA((n,))],
    )(*blocked)


class _Rider(NamedTuple):
    srcs: list
    lands: list
    scatter: bool
    rows: list


def _new_lands(srcs, scatter):
    return [lax.empty(v.shape if scatter else (N_DEV,) + v.shape, v.dtype) for v in srcs]


def _rider_copies(srcs, lands, send_sems, recv_sems, rider):
    x, y, c = _my_place()
    mine_idx = _block_index(x, y, c)

    def window(ref, a):
        return ref if rider.rows[a] is None else ref.at[pl.ds(*rider.rows[a])]

    sends, arrivals = [], []
    for r in range(1, N_DEV):
        peer = (_flip(x, r & 4), _flip(y, r & 2), _flip(c, r & 1))
        peer_idx = _block_index(*peer)
        for a in range(len(srcs)):
            src = window(srcs[a].at[peer_idx] if rider.scatter else srcs[a], a)
            k = 7 * a + r - 1
            sems = dict(send_sem=send_sems.at[k], recv_sem=recv_sems.at[k],
                        device_id=peer, device_id_type=pl.DeviceIdType.MESH)
            sends.append(pltpu.make_async_remote_copy(
                src_ref=src, dst_ref=window(lands[a].at[mine_idx], a), **sems))
            arrivals.append(pltpu.make_async_remote_copy(
                src_ref=src, dst_ref=window(lands[a].at[peer_idx], a), **sems))
    return sends, arrivals


def _ride(call, rider):
    call = dict(call)
    body, grid = call.pop("body"), call["grid"]
    operands = call.pop("operands")
    if rider is None:
        return list(pl.pallas_call(body, **call)(*operands)), None
    n_in, n_out = len(call["in_specs"]), len(call["out_specs"])
    n_scratch = len(call["scratch_shapes"])
    m = len(rider.srcs)

    def riding(*refs):
        main_in, srcs, lands = refs[:n_in], refs[n_in:n_in + m], refs[n_in + m:n_in + 2 * m]
        main_out = refs[n_in + 2 * m:n_in + 2 * m + n_out]
        rest = refs[n_in + 3 * m + n_out:]
        send_sems, recv_sems = rest[n_scratch:]
        at = [pl.program_id(d) for d in range(len(grid))]
        first = functools.reduce(jnp.logical_and, [p == 0 for p in at])
        last = functools.reduce(jnp.logical_and, [p == g - 1 for p, g in zip(at, grid)])
        sends, arrivals = _rider_copies(srcs, lands, send_sems, recv_sems, rider)

        @pl.when(first)
        def _():
            for cp in sends:
                cp.start()

        body(*main_in, *main_out, *rest[:n_scratch])

        @pl.when(last)
        def _():
            for cp in arrivals:
                cp.wait_recv()
            for cp in sends:
                cp.wait_send()

    call["in_specs"] = list(call["in_specs"]) + [_ANY] * (2 * m)
    call["out_specs"] = list(call["out_specs"]) + [_ANY] * m
    call["out_shape"] = list(call["out_shape"]) + [
        jax.ShapeDtypeStruct(v.shape, v.dtype) for v in rider.lands]
    call["scratch_shapes"] = list(call["scratch_shapes"]) + [
        pltpu.SemaphoreType.DMA((7 * m,)), pltpu.SemaphoreType.DMA((7 * m,))]
    call["input_output_aliases"] = {n_in + m + a: n_out + a for a in range(m)}
    outs = pl.pallas_call(riding, **call)(*operands, *rider.srcs, *rider.lands)
    return list(outs[:n_out]), list(outs[n_out:])


def _adamw_sum(parts, w, m, v, rows, name):
    depth, nrow, ncol = w.shape
    c1 = 1.0 / (1.0 - ADAM_B1 ** ADAM_STEP)
    c2 = 1.0 / (1.0 - ADAM_B2 ** ADAM_STEP)

    def body(*refs):
        p_refs = refs[:depth]
        w_ref, m_ref, v_ref, g_ref, d_ref, nm_ref, nv_ref = refs[depth:]
        layer = pl.program_id(0)
        for l in range(depth):
            @pl.when(layer == l)
            def _(p_ref=p_refs[l]):
                g = p_ref[0].astype(F32)
                for k in range(1, N_DEV):
                    g = g + p_ref[k].astype(F32)
                nm = ADAM_B1 * m_ref[...] + (1.0 - ADAM_B1) * g
                nv = ADAM_B2 * v_ref[...] + (1.0 - ADAM_B2) * (g * g)
                g_ref[...] = g
                nm_ref[...] = nm
                nv_ref[...] = nv
                d_ref[...] = -ADAM_LR * ((nm * c1) / (jnp.sqrt(nv * c2) + ADAM_EPS)
                                         + ADAM_WD * w_ref[...])

    def part_spec(l):
        return pl.BlockSpec((N_DEV, rows, ncol), lambda q, i: (0, jnp.where(q == l, i, 0), 0))

    blk = pl.BlockSpec((None, rows, ncol), lambda q, i: (q, i, 0))
    return pl.pallas_call(
        body, name=name, grid=(depth, nrow // rows),
        in_specs=[part_spec(l) for l in range(depth)] + [blk, blk, blk],
        out_specs=[blk] * 4,
        out_shape=[jax.ShapeDtypeStruct(w.shape, F32)] * 4,
        compiler_params=_ARB2,
    )(*parts, w, m, v)


def _pack_small(norm_w, mem_norm_w, out_norm_w, final_norm_w, b_forget):
    onw = jnp.pad(out_norm_w.reshape(20, LANES), ((0, 4), (0, 0)))
    b = jnp.pad(b_forget, ((0, 6), (0, LANES - FOX_HEADS)))
    return jnp.concatenate([norm_w.reshape(16, LANES), mem_norm_w.reshape(16, LANES), onw,
                            final_norm_w.reshape(8, LANES), b], axis=0)


def _unpack_small(p):
    return (p[0:16].reshape(2, D_MODEL), p[16:32].reshape(2, D_MODEL), p[32:52].reshape(2, MIX_W),
            p[56:64].reshape(D_MODEL), p[64:66, :FOX_HEADS])


def kernel(x, mem, norm_w, w_in, b_forget, mem_norm_w, w_mem_kv, out_norm_w, w_out, final_norm_w, loss_target, m_norm_w, m_w_in, m_b_forget, m_mem_norm_w, m_w_mem_kv, m_out_norm_w, m_w_out, m_final_norm_w, v_norm_w, v_w_in, v_b_forget, v_mem_norm_w, v_w_mem_kv, v_out_norm_w, v_w_out, v_final_norm_w):
    kv_rows = w_mem_kv.shape[1]
    out_rows = w_out.shape[1]
    me = _block_index(*_my_place())

    def shards(l):
        return [w_in[l].astype(BF16), w_mem_kv[l].astype(BF16), w_out[l].astype(BF16)]

    def lay_out(g_in, g_kv, g_out):
        w_full = jnp.transpose(g_in, (1, 0, 2)).reshape(D_MODEL, IN_W)
        return (_rearrange_w_in(w_full), g_kv.reshape(D_MODEL, 2 * MEM_W),
                g_out.reshape(MIX_W, D_MODEL))

    def blocks(gr):
        p_in = _restore_w_in(gr["w_r"]).reshape(D_MODEL, N_DEV, SHARD_W)
        return [jnp.transpose(p_in, (1, 0, 2)).astype(BF16),
                gr["w_mem_kv"].reshape(N_DEV, kv_rows, 2 * MEM_W).astype(BF16),
                gr["w_out"].reshape(N_DEV, out_rows, D_MODEL).astype(BF16)]

    def row(v):
        return v.reshape(1, -1)

    half = D_MODEL // 2
    legs = [([0, 1], [(0, half), None]), ([0, 2], [(half, half), None])]

    w_r0, wkv0, wout0 = lay_out(*_all_gather_weights(shards(0), "all_gather_l0"))
    shards1 = shards(1)
    x1, saved0, lands = _layer_fwd(x[0], mem[0], row(norm_w[0]), w_r0, b_forget[0],
                                   row(mem_norm_w[0]), wkv0, row(out_norm_w[0]), wout0, 0,
                                   travel=(shards1, False, legs))
    w_r1, wkv1, wout1 = lay_out(*[
        lax.dynamic_update_slice(land, own[None], (me,) + (0,) * own.ndim)
        for land, own in zip(lands, shards1)])
    x2, saved1, _ = _layer_fwd(x1, mem[0], row(norm_w[1]), w_r1, b_forget[1], row(mem_norm_w[1]),
                               wkv1, row(out_norm_w[1]), wout1, 1)

    dx2, loss, dfnw = _final_fwd_bwd(x2, row(final_norm_w), loss_target[0], "final_fwd_bwd")
    loss = lax.psum(loss[0, 0], ("x", "y", "c"))

    dx1, gr1, _ = _layer_bwd(dx2, saved1, mem[0], w_r1, wkv1, wout1, 1)
    sent = blocks(gr1)
    grad_x, gr0, lands = _layer_bwd(dx1, saved0, mem[0], w_r0, wkv0, wout0, 0,
                                    travel=(sent, True, legs))

    def both(name):
        return jnp.stack([gr0[name], gr1[name]])

    small = _pack_small(both("norm_w"), both("mem_norm_w"), both("out_norm_w"), dfnw[0],
                        both("b_forget"))
    p_small = jnp.broadcast_to(small[None], (N_DEV, SMALL_ROWS, LANES))
    r_in0, r_kv0, r_out0, r_small = _exchange_blocks(blocks(gr0) + [p_small], "exchange_grads_l0")
    r_in1, r_kv1, r_out1 = [
        lax.dynamic_update_slice(land, lax.dynamic_slice_in_dim(src, me, 1, axis=0),
                                 (me,) + (0,) * (land.ndim - 1))
        for land, src in zip(lands, sent)]

    g_w_in, d_w_in, nm_w_in, nv_w_in = _adamw_sum([r_in0, r_in1], w_in, m_w_in, v_w_in, 256,
                                                  "adamw_w_in")
    g_w_kv, d_w_kv, nm_w_kv, nv_w_kv = _adamw_sum([r_kv0, r_kv1], w_mem_kv, m_w_mem_kv, v_w_mem_kv,
                                                  kv_rows, "adamw_w_mem_kv")
    g_w_out, d_w_out, nm_w_out, nv_w_out = _adamw_sum([r_out0, r_out1], w_out, m_w_out, v_w_out,
                                                      out_rows, "adamw_w_out")
    w_small = _pack_small(norm_w, mem_norm_w, out_norm_w, final_norm_w, b_forget)[None]
    m_small = _pack_small(m_norm_w, m_mem_norm_w, m_out_norm_w, m_final_norm_w, m_b_forget)[None]
    v_small = _pack_small(v_norm_w, v_mem_norm_w, v_out_norm_w, v_final_norm_w, v_b_forget)[None]
    small_out = _adamw_sum([r_small], w_small, m_small, v_small, SMALL_ROWS, "adamw_small")
    (g_nw, g_mnw, g_onw, g_fnw, g_b), (d_nw, d_mnw, d_onw, d_fnw, d_b), \
        (nm_nw, nm_mnw, nm_onw, nm_fnw, nm_b), (nv_nw, nv_mnw, nv_onw, nv_fnw, nv_b) = [
            _unpack_small(t[0]) for t in small_out]

    return (loss, grad_x[None],
            g_nw, g_w_in, g_b, g_mnw, g_w_kv, g_onw, g_w_out, g_fnw,
            d_nw, d_w_in, d_b, d_mnw, d_w_kv, d_onw, d_w_out, d_fnw,
            nm_nw, nm_w_in, nm_b, nm_mnw, nm_w_kv, nm_onw, nm_w_out, nm_fnw,
            nv_nw, nv_w_in, nv_b, nv_mnw, nv_w_kv, nv_onw, nv_w_out, nv_fnw)
```

```python
import functools
from typing import NamedTuple

import jax
import jax.numpy as jnp
from jax import lax
from jax.experimental import pallas as pl
from jax.experimental.pallas import tpu as pltpu

F32 = jnp.float32
BF16 = jnp.bfloat16

N_DEV = 8
D_MODEL = 1024
HEAD_DIM = 64
LANES = 128
SB_W = 512
FX_W = 512
MEM_W = 256
MIX_W = 1280
FOX_HEADS = 8
IN_W = 4616
SHARD_W = IN_W // N_DEV
QKV_W = 3 * SB_W + 3 * FX_W + MEM_W
FL_PAD = 256
GF_W = MIX_W + FL_PAD
WR_W = QKV_W + GF_W
EPS = 1e-6
T = 256
TM = 256
Q_SCALE = 0.125
NEG = -1e30
UNDERFLOW = -110.0

ADAM_LR = 0.001
ADAM_B1 = 0.9
ADAM_B2 = 0.999
ADAM_EPS = 1e-08
ADAM_WD = 0.01
ADAM_STEP = 10

SMALL_ROWS = 72

_NT = (((1,), (1,)), ((), ()))
_TN = (((0,), (0,)), ((), ()))

_ARB1 = pltpu.CompilerParams(dimension_semantics=("arbitrary",))
_ARB2 = pltpu.CompilerParams(dimension_semantics=("arbitrary", "arbitrary"))


def _dot(a, b):
    return jnp.dot(a, b, preferred_element_type=F32)


def _dot_nt(a, b):
    return lax.dot_general(a, b, _NT, preferred_element_type=F32)


def _dot_tn(a, b):
    return lax.dot_general(a, b, _TN, preferred_element_type=F32)


def _split2(x):
    hi = x.astype(BF16)
    lo = (x - hi.astype(F32)).astype(BF16)
    return hi, lo


def _stack2(u):
    return jnp.concatenate([u, u], axis=0)


def _cum2(x, u2):
    hi, lo = _split2(x)
    return _dot(jnp.concatenate([hi, lo], axis=1), u2)


def _tri3(tri, x):
    hi = x.astype(BF16)
    r1 = x - hi.astype(F32)
    mid = r1.astype(BF16)
    lo = (r1 - mid.astype(F32)).astype(BF16)
    return _dot(tri, hi) + _dot(tri, mid) + _dot(tri, lo)


def _iota2(shape, dim):
    return lax.broadcasted_iota(jnp.int32, shape, dim)


def _head_block_diag():
    r = _iota2((LANES, LANES), 0) // HEAD_DIM
    c = _iota2((LANES, LANES), 1) // HEAD_DIM
    return _stack2(jnp.where(r == c, 1.0, 0.0).astype(BF16))


def _head_mean(x, bd):
    return _cum2(x, bd) * (1.0 / HEAD_DIM)


def _sigmoid(x):
    return 1.0 / (1.0 + jnp.exp(-x))


def _log_sigmoid(x):
    return jnp.minimum(x, 0.0) - jnp.log(1.0 + jnp.exp(-jnp.abs(x)))


def _running_top(r_ref):
    return jnp.max(jnp.maximum(r_ref[0], r_ref[1]))


def _head_norm_tops(x, m0):
    x2 = x.astype(F32)
    x2 = x2 * x2
    zero = jnp.zeros_like(x2)
    return (jnp.max(jnp.sqrt(jnp.sum(jnp.where(m0, x2, zero), axis=1, keepdims=True))),
            jnp.max(jnp.sqrt(jnp.sum(jnp.where(m0, zero, x2), axis=1, keepdims=True))))


def _fox_tiles_left(i, pair, qn, kn_ref, fb_ref, tile):
    def bound(j):
        b = [2.0 * qn[h] * kn_ref[h] + fb_ref[2 * pair + h, i] - fb_ref[FOX_HEADS + 2 * pair + h, j]
             for h in range(2)]
        return jnp.maximum(b[0], b[1])

    def more(j):
        return jnp.logical_and(j >= 0, bound(jnp.maximum(j, 0)) > UNDERFLOW)

    def step(j):
        tile(j, False)
        return j - 1

    return lax.while_loop(more, step, i - 1)


def _pair_masks():
    lane = _iota2((1, LANES), 1)
    return lane < HEAD_DIM


def _split_pair(x, m0):
    zero = jnp.zeros_like(x)
    return jnp.where(m0, x, zero), jnp.where(m0, zero, x)


def _inproj_fwd(x, nw, w_r, name):
    s = x.shape[0]

    def body(x_ref, nw_ref, w_ref, qkv_ref, gf_ref):
        xv = x_ref[...]
        r = lax.rsqrt(jnp.mean(xv * xv, axis=-1, keepdims=True) + EPS)
        h = (xv * r * nw_ref[...]).astype(BF16)
        for c in range(0, QKV_W, 256):
            qkv_ref[:, c:c + 256] = _dot_nt(h, w_ref[c:c + 256, :]).astype(BF16)
        for c in range(0, GF_W, 256):
            gf_ref[:, c:c + 256] = _dot_nt(h, w_ref[QKV_W + c:QKV_W + c + 256, :])

    return pl.pallas_call(
        body, name=name, grid=(s // TM,),
        in_specs=[pl.BlockSpec((TM, D_MODEL), lambda i: (i, 0)),
                  pl.BlockSpec((1, D_MODEL), lambda i: (0, 0)),
                  pl.BlockSpec((WR_W, D_MODEL), lambda i: (0, 0))],
        out_specs=[pl.BlockSpec((TM, QKV_W), lambda i: (i, 0)),
                   pl.BlockSpec((TM, GF_W), lambda i: (i, 0))],
        out_shape=[jax.ShapeDtypeStruct((s, QKV_W), BF16), jax.ShapeDtypeStruct((s, GF_W), F32)],
        compiler_params=_ARB1,
    )(x, nw, w_r)


def _fox_prep_fwd(gf, bpad, name):
    s = gf.shape[0]

    def body(fl_ref, b_ref, f_ref):
        tri = jnp.where(_iota2((T, T), 0) >= _iota2((T, T), 1), 1.0, 0.0).astype(BF16)
        carry = jnp.zeros((1, LANES), F32)
        for blk in range(s // T):
            lf = _log_sigmoid(fl_ref[blk * T:(blk + 1) * T, :] + b_ref[...])
            c = _tri3(tri, lf) + carry
            f_ref[blk * T:(blk + 1) * T, :] = c
            carry = c[T - 1:T, :]

    return pl.pallas_call(
        body, name=name, grid=(1,),
        in_specs=[pl.BlockSpec((s, LANES), lambda i: (0, MIX_W // LANES)),
                  pl.BlockSpec((1, LANES), lambda i: (0, 0))],
        out_specs=pl.BlockSpec((s, LANES), lambda i: (0, 0)),
        out_shape=jax.ShapeDtypeStruct((s, LANES), F32),
        compiler_params=_ARB1,
    )(gf, bpad)


def _sb_fwd(qkv, name, rider=None):
    s = qkv.shape[0]

    def body(q_ref, k_ref, v_ref, o_ref, acc_ref, r_ref, as_ref):
        i = pl.program_id(1)
        m0 = _pair_masks()
        qh = _split_pair(q_ref[...] * jnp.asarray(Q_SCALE, BF16), m0)
        strict = _iota2((T, T), 0) > _iota2((T, T), 1)
        u2 = _stack2(jnp.where(strict, 1.0, 0.0).astype(BF16))
        acc_ref[...] = jnp.zeros_like(acc_ref)
        r_ref[...] = jnp.zeros_like(r_ref)
        hs = range(2)

        def flush(j):
            v = v_ref[pl.ds(pl.multiple_of(j * T, T), T), :]
            for h in hs:
                acc_ref[h] += _dot(as_ref[h], v)

        def tile(j, diag):
            k = k_ref[pl.ds(pl.multiple_of(j * T, T), T), :]
            z = [_dot_nt(qh[h], k) for h in hs]
            if not diag:
                flush(j + 1)
            la = [jnp.minimum(z[h], 0.0) - jnp.log(1.0 + jnp.exp(-jnp.abs(z[h]))) for h in hs]
            lf = [la[h] - z[h] for h in hs]
            if diag:
                lf = [jnp.where(strict, lf[h], 0.0) for h in hs]
            cin = [_cum2(lf[h], u2) for h in hs]
            a = [jnp.exp(la[h] + cin[h] + r_ref[h]) for h in hs]
            if diag:
                a = [jnp.where(strict, a[h], 0.0) for h in hs]
            for h in hs:
                r_ref[h] += cin[h][:, 0:1] + lf[h][:, 0:1]
                as_ref[h] = a[h].astype(BF16)

        tile(i, True)

        def more(state):
            j, top = state
            return jnp.logical_and(j >= 0, top > UNDERFLOW)

        def step(state):
            j, _ = state
            tile(j, False)
            return j - 1, _running_top(r_ref)

        j_left, _ = lax.while_loop(more, step, (i - 1, _running_top(r_ref)))
        flush(j_left + 1)
        o_ref[...] = jnp.where(m0, acc_ref[0], acc_ref[1])

    nb = SB_W // LANES
    (ysb,), lands = _ride(dict(
        body=body, name=name, grid=(nb, s // T),
        in_specs=[pl.BlockSpec((T, LANES), lambda p, i: (i, p)),
                  pl.BlockSpec((s, LANES), lambda p, i: (0, nb + p)),
                  pl.BlockSpec((s, LANES), lambda p, i: (0, 2 * nb + p))],
        out_specs=[pl.BlockSpec((T, LANES), lambda p, i: (i, p))],
        out_shape=[jax.ShapeDtypeStruct((s, SB_W), F32)],
        scratch_shapes=[pltpu.VMEM((2, T, LANES), F32), pltpu.VMEM((2, T, 1), F32),
                        pltpu.VMEM((2, T, T), BF16)],
        compiler_params=_ARB2, operands=[qkv, qkv, qkv]), rider)
    return ysb, lands


def _fox_fwd(qkv, fqb, frow, fbounds, name, rider=None):
    s = qkv.shape[0]

    def body(q_ref, k_ref, v_ref, fq_ref, fr_ref, fb_ref, o_ref, lse_ref, acc_ref, m_ref, ps_ref,
             kn_ref):
        pair = pl.program_id(0)
        i = pl.program_id(1)
        m0 = _pair_masks()

        @pl.when(i == 0)
        def _():
            kn_ref[0], kn_ref[1] = _head_norm_tops(k_ref[...], m0)

        q = q_ref[...] * jnp.asarray(Q_SCALE, BF16)
        qn = _head_norm_tops(q, m0)
        qh = _split_pair(q, m0)
        fq = fq_ref[...]
        fqh = (fq[:, 0:1], fq[:, HEAD_DIM:HEAD_DIM + 1])
        causal = _iota2((T, T), 0) >= _iota2((T, T), 1)
        ones = jnp.ones((T, LANES), BF16)
        acc_ref[...] = jnp.zeros_like(acc_ref)
        m_ref[...] = jnp.full_like(m_ref, NEG)
        hs = range(2)

        def flush(j):
            v = v_ref[pl.ds(pl.multiple_of(j * T, T), T), :]
            va2 = _stack2(jnp.concatenate([v, ones], axis=1))
            for h in hs:
                acc_ref[h] += _dot(ps_ref[h], va2)

        def tile(j, diag):
            off = pl.multiple_of(j * T, T)
            k = k_ref[pl.ds(off, T), :]
            sc = [_dot_nt(qh[h], k) + fqh[h] - fr_ref[h:h + 1, pl.ds(off, T)] for h in hs]
            if not diag:
                flush(j + 1)
            if diag:
                sc = [jnp.where(causal, sc[h], NEG) for h in hs]
            m_new = [jnp.maximum(m_ref[h], jnp.max(sc[h], axis=1, keepdims=True)) for h in hs]
            p = [jnp.exp(sc[h] - m_new[h]) for h in hs]
            for h in hs:
                acc_ref[h] = acc_ref[h] * jnp.exp(m_ref[h] - m_new[h])
                m_ref[h] = m_new[h]
                p_hi, p_lo = _split2(p[h])
                ps_ref[h] = jnp.concatenate([p_hi, p_lo], axis=1)

        tile(i, True)
        j_left = _fox_tiles_left(i, pair, qn, kn_ref, fb_ref, tile)
        flush(j_left + 1)
        acc = (acc_ref[0], acc_ref[1])
        o_ref[...] = jnp.where(m0, acc[0][:, :LANES] / acc[0][:, LANES:],
                               acc[1][:, :LANES] / acc[1][:, LANES:])
        lse_ref[...] = jnp.where(m0, m_ref[0] + jnp.log(acc[0][:, LANES:]),
                                 m_ref[1] + jnp.log(acc[1][:, LANES:]))

    nb = FX_W // LANES
    base = 3 * SB_W // LANES
    (yfx, lse), lands = _ride(dict(
        body=body, name=name, grid=(nb, s // T),
        in_specs=[pl.BlockSpec((T, LANES), lambda p, i: (i, base + p)),
                  pl.BlockSpec((s, LANES), lambda p, i: (0, base + nb + p)),
                  pl.BlockSpec((s, LANES), lambda p, i: (0, base + 2 * nb + p)),
                  pl.BlockSpec((T, LANES), lambda p, i: (i, p)),
                  pl.BlockSpec((None, 2, s), lambda p, i: (p, 0, 0)),
                  pl.BlockSpec(memory_space=pltpu.SMEM)],
        out_specs=[pl.BlockSpec((T, LANES), lambda p, i: (i, p)),
                   pl.BlockSpec((T, LANES), lambda p, i: (i, p))],
        out_shape=[jax.ShapeDtypeStruct((s, FX_W), F32), jax.ShapeDtypeStruct((s, FX_W), F32)],
        scratch_shapes=[pltpu.VMEM((2, T, 2 * LANES), F32), pltpu.VMEM((2, T, 1), F32),
                        pltpu.VMEM((2, T, 2 * T), BF16), pltpu.SMEM((2,), F32)],
        compiler_params=_ARB2, operands=[qkv, qkv, qkv, fqb, frow, fbounds]), rider)
    return yfx, lse, lands


def _memkv_fwd(mem, mnw, wkv, name):
    n = mem.shape[0]

    def body(mem_ref, mnw_ref, w_ref, kv_ref):
        mv = mem_ref[...]
        r = lax.rsqrt(jnp.mean(mv * mv, axis=-1, keepdims=True) + EPS)
        hm = (mv * r * mnw_ref[...]).astype(BF16)
        kv_ref[...] = _dot(hm, w_ref[...]).astype(BF16)

    return pl.pallas_call(
        body, name=name, grid=(1,),
        in_specs=[pl.BlockSpec((n, D_MODEL), lambda i: (0, 0)),
                  pl.BlockSpec((1, D_MODEL), lambda i: (0, 0)),
                  pl.BlockSpec((D_MODEL, 2 * MEM_W), lambda i: (0, 0))],
        out_specs=pl.BlockSpec((n, 2 * MEM_W), lambda i: (0, 0)),
        out_shape=jax.ShapeDtypeStruct((n, 2 * MEM_W), BF16),
        compiler_params=_ARB1,
    )(mem, mnw, wkv)


def _mem_fwd(qkv, kv, name):
    s = qkv.shape[0]
    n = kv.shape[0]

    def body(q_ref, k_ref, v_ref, o_ref, lse_ref):
        m0 = _pair_masks()
        qh = _split_pair(q_ref[...] * jnp.asarray(Q_SCALE, BF16), m0)
        k = k_ref[...]
        v = v_ref[...]
        outs, lses = [], []
        for h in range(2):
            sc = _dot_nt(qh[h], k)
            mx = jnp.max(sc, axis=1, keepdims=True)
            p = jnp.exp(sc - mx)
            l = jnp.sum(p, axis=1, keepdims=True)
            outs.append(_dot(p.astype(BF16), v) / l)
            lses.append(mx + jnp.log(l))
        o_ref[...] = jnp.where(m0, outs[0], outs[1])
        lse_ref[...] = jnp.where(m0, lses[0], lses[1])

    nb = MEM_W // LANES
    base = (3 * SB_W + 3 * FX_W) // LANES
    return pl.pallas_call(
        body, name=name, grid=(nb, s // T),
        in_specs=[pl.BlockSpec((T, LANES), lambda p, i: (i, base + p)),
                  pl.BlockSpec((n, LANES), lambda p, i: (0, p)),
                  pl.BlockSpec((n, LANES), lambda p, i: (0, nb + p))],
        out_specs=[pl.BlockSpec((T, LANES), lambda p, i: (i, p)),
                   pl.BlockSpec((T, LANES), lambda p, i: (i, p))],
        out_shape=[jax.ShapeDtypeStruct((s, MEM_W), F32), jax.ShapeDtypeStruct((s, MEM_W), F32)],
        compiler_params=_ARB2,
    )(qkv, kv, kv)


def _mix_chunk(c, ysb_ref, yfx_ref, ym_ref):
    if c < SB_W // LANES:
        return ysb_ref[:, c * LANES:(c + 1) * LANES]
    c -= SB_W // LANES
    if c < FX_W // LANES:
        return yfx_ref[:, c * LANES:(c + 1) * LANES]
    c -= FX_W // LANES
    return ym_ref[:, c * LANES:(c + 1) * LANES]


def _outproj_fwd(ysb, yfx, ym, gf, onw, wout, x, name):
    s = x.shape[0]

    def body(ysb_ref, yfx_ref, ym_ref, g_ref, onw_ref, w_ref, x_ref, o_ref, yg_ref):
        bd = _head_block_diag()
        for c in range(MIX_W // LANES):
            sl = slice(c * LANES, (c + 1) * LANES)
            u = _mix_chunk(c, ysb_ref, yfx_ref, ym_ref)
            r = lax.rsqrt(_head_mean(u * u, bd) + EPS)
            g = g_ref[:, sl]
            yg_ref[:, sl] = (u * r * onw_ref[:, sl] * (g * _sigmoid(g))).astype(BF16)
        o_ref[...] = x_ref[...] + _dot(yg_ref[...], w_ref[...])

    return pl.pallas_call(
        body, name=name, grid=(s // TM,),
        in_specs=[pl.BlockSpec((TM, SB_W), lambda i: (i, 0)),
                  pl.BlockSpec((TM, FX_W), lambda i: (i, 0)),
                  pl.BlockSpec((TM, MEM_W), lambda i: (i, 0)),
                  pl.BlockSpec((TM, MIX_W), lambda i: (i, 0)),
                  pl.BlockSpec((1, MIX_W), lambda i: (0, 0)),
                  pl.BlockSpec((MIX_W, D_MODEL), lambda i: (0, 0)),
                  pl.BlockSpec((TM, D_MODEL), lambda i: (i, 0))],
        out_specs=pl.BlockSpec((TM, D_MODEL), lambda i: (i, 0)),
        out_shape=jax.ShapeDtypeStruct((s, D_MODEL), F32),
        scratch_shapes=[pltpu.VMEM((TM, MIX_W), BF16)],
        compiler_params=_ARB1,
    )(ysb, yfx, ym, gf, onw, wout, x)


def _final_fwd_bwd(x, fnw, target, name):
    s = x.shape[0]

    def body(x_ref, w_ref, t_ref, dx_ref, loss_ref, dw_ref):
        @pl.when(pl.program_id(0) == 0)
        def _():
            loss_ref[...] = jnp.zeros_like(loss_ref)
            dw_ref[...] = jnp.zeros_like(dw_ref)

        xv = x_ref[...]
        w = w_ref[...]
        r = lax.rsqrt(jnp.mean(xv * xv, axis=-1, keepdims=True) + EPS)
        xh = xv * r
        err = xh * w - t_ref[...]
        part = jnp.sum(jnp.sum(err * err, axis=1, keepdims=True), axis=0, keepdims=True)
        loss_ref[...] += part * (0.5 / D_MODEL)
        dy = err * (1.0 / D_MODEL)
        dw_ref[...] += jnp.sum(dy * xh, axis=0, keepdims=True)
        dxh = dy * w
        dx_ref[...] = r * (dxh - xh * jnp.mean(dxh * xh, axis=-1, keepdims=True))

    return pl.pallas_call(
        body, name=name, grid=(s // TM,),
        in_specs=[pl.BlockSpec((TM, D_MODEL), lambda i: (i, 0)),
                  pl.BlockSpec((1, D_MODEL), lambda i: (0, 0)),
                  pl.BlockSpec((TM, D_MODEL), lambda i: (i, 0))],
        out_specs=[pl.BlockSpec((TM, D_MODEL), lambda i: (i, 0)),
                   pl.BlockSpec((1, LANES), lambda i: (0, 0)),
                   pl.BlockSpec((1, D_MODEL), lambda i: (0, 0))],
        out_shape=[jax.ShapeDtypeStruct((s, D_MODEL), F32), jax.ShapeDtypeStruct((1, LANES), F32),
                   jax.ShapeDtypeStruct((1, D_MODEL), F32)],
        compiler_params=_ARB1,
    )(x, fnw, target)


def _outproj_bwd(dxo, wout, ysb, yfx, ym, gf, onw, name):
    s = dxo.shape[0]

    def body(dx_ref, w_ref, ysb_ref, yfx_ref, ym_ref, g_ref, onw_ref,
             dysb_ref, dyfx_ref, dym_ref, dg_ref, dw_ref, donw_ref, yg_ref):
        @pl.when(pl.program_id(0) == 0)
        def _():
            dw_ref[...] = jnp.zeros_like(dw_ref)
            donw_ref[...] = jnp.zeros_like(donw_ref)

        dxb = dx_ref[...].astype(BF16)
        dyg = _dot_nt(dxb, w_ref[...])
        bd = _head_block_diag()
        for c in range(MIX_W // LANES):
            sl = slice(c * LANES, (c + 1) * LANES)
            u = _mix_chunk(c, ysb_ref, yfx_ref, ym_ref)
            r = lax.rsqrt(_head_mean(u * u, bd) + EPS)
            yn = u * r
            g = g_ref[:, sl]
            sg = _sigmoid(g)
            sil = g * sg
            onw = onw_ref[:, sl]
            e = dyg[:, sl]
            yg_ref[:, sl] = (yn * onw * sil).astype(BF16)
            donw_ref[:, sl] += jnp.sum(e * yn * sil, axis=0, keepdims=True)
            dg_ref[:, sl] = (e * yn * onw * (sg * (1.0 + g * (1.0 - sg)))).astype(BF16)
            dyn = e * onw * sil
            du = (r * (dyn - yn * _head_mean(dyn * yn, bd))).astype(BF16)
            if c < 4:
                dysb_ref[:, c * LANES:(c + 1) * LANES] = du
            elif c < 8:
                dyfx_ref[:, (c - 4) * LANES:(c - 3) * LANES] = du
            else:
                dym_ref[:, (c - 8) * LANES:(c - 7) * LANES] = du
        dw_ref[...] += _dot_tn(yg_ref[...], dxb)

    return pl.pallas_call(
        body, name=name, grid=(s // TM,),
        in_specs=[pl.BlockSpec((TM, D_MODEL), lambda i: (i, 0)),
                  pl.BlockSpec((MIX_W, D_MODEL), lambda i: (0, 0)),
                  pl.BlockSpec((TM, SB_W), lambda i: (i, 0)),
                  pl.BlockSpec((TM, FX_W), lambda i: (i, 0)),
                  pl.BlockSpec((TM, MEM_W), lambda i: (i, 0)),
                  pl.BlockSpec((TM, MIX_W), lambda i: (i, 0)),
                  pl.BlockSpec((1, MIX_W), lambda i: (0, 0))],
        out_specs=[pl.BlockSpec((TM, SB_W), lambda i: (i, 0)),
                   pl.BlockSpec((TM, FX_W), lambda i: (i, 0)),
                   pl.BlockSpec((TM, MEM_W), lambda i: (i, 0)),
                   pl.BlockSpec((TM, MIX_W), lambda i: (i, 0)),
                   pl.BlockSpec((MIX_W, D_MODEL), lambda i: (0, 0)),
                   pl.BlockSpec((1, MIX_W), lambda i: (0, 0))],
        out_shape=[jax.ShapeDtypeStruct((s, SB_W), BF16), jax.ShapeDtypeStruct((s, FX_W), BF16),
                   jax.ShapeDtypeStruct((s, MEM_W), BF16), jax.ShapeDtypeStruct((s, MIX_W), BF16),
                   jax.ShapeDtypeStruct((MIX_W, D_MODEL), F32), jax.ShapeDtypeStruct((1, MIX_W), F32)],
        scratch_shapes=[pltpu.VMEM((TM, MIX_W), BF16)],
        compiler_params=_ARB1,
    )(dxo, wout, ysb, yfx, ym, gf, onw)


def _row_dots(do_ref, o_ref, m0):
    prod = do_ref[...].astype(F32) * o_ref[...]
    zero = jnp.zeros_like(prod)
    return (jnp.sum(jnp.where(m0, prod, zero), axis=1, keepdims=True),
            jnp.sum(jnp.where(m0, zero, prod), axis=1, keepdims=True))


def _sb_bwd(qkv, o, do, name, rider=None):
    s = qkv.shape[0]
    nq = s // T

    def body(q_ref, k_ref, v_ref, o_ref, do_ref, dq_ref, dk_ref, dv_ref,
             dqa_ref, dka_ref, dva_ref, rl_ref, rg_ref, dzs_ref, abs_ref):
        i = pl.program_id(1)

        @pl.when(i == 0)
        def _():
            dka_ref[...] = jnp.zeros_like(dka_ref)
            dva_ref[...] = jnp.zeros_like(dva_ref)

        m0 = _pair_masks()
        qh = _split_pair(q_ref[...] * jnp.asarray(Q_SCALE, BF16), m0)
        doh = _split_pair(do_ref[...], m0)
        dsum = _row_dots(do_ref, o_ref, m0)
        strict = _iota2((T, T), 0) > _iota2((T, T), 1)
        u2 = _stack2(jnp.where(strict, 1.0, 0.0).astype(BF16))
        dqa_ref[...] = jnp.zeros_like(dqa_ref)
        rl_ref[...] = jnp.zeros_like(rl_ref)
        rg_ref[...] = jnp.zeros_like(rg_ref)

        hs = range(2)

        def flush(j):
            off = pl.multiple_of(j * T, T)
            k = k_ref[pl.ds(off, T), :]
            for h in hs:
                dqa_ref[h] += _dot(dzs_ref[h], k)
            dka_ref[pl.ds(off, T), :] += _dot_tn(dzs_ref[0], qh[0]) + _dot_tn(dzs_ref[1], qh[1])
            dva_ref[pl.ds(off, T), :] += _dot_tn(abs_ref[0], doh[0]) + _dot_tn(abs_ref[1], doh[1])

        def tile(j, diag):
            off = pl.multiple_of(j * T, T)
            k = k_ref[pl.ds(off, T), :]
            v = v_ref[pl.ds(off, T), :]
            z = [_dot_nt(qh[h], k) for h in hs]
            da = [_dot_nt(doh[h], v) for h in hs]
            if not diag:
                flush(j + 1)
            la = [jnp.minimum(z[h], 0.0) - jnp.log(1.0 + jnp.exp(-jnp.abs(z[h]))) for h in hs]
            lf = [la[h] - z[h] for h in hs]
            if diag:
                lf = [jnp.where(strict, lf[h], 0.0) for h in hs]
            cin = [_cum2(lf[h], u2) for h in hs]
            a = [jnp.exp(la[h] + cin[h] + rl_ref[h]) for h in hs]
            if diag:
                a = [jnp.where(strict, a[h], 0.0) for h in hs]
            ab = [a[h].astype(BF16) for h in hs]
            g = [ab[h].astype(F32) * da[h] for h in hs]
            gin = [_cum2(g[h], u2) for h in hs]
            dz = [g[h] - jnp.exp(la[h]) * ((dsum[h] - rg_ref[h]) - gin[h]) for h in hs]
            if diag:
                dz = [jnp.where(strict, dz[h], 0.0) for h in hs]
            for h in hs:
                rl_ref[h] += cin[h][:, 0:1] + lf[h][:, 0:1]
                rg_ref[h] += gin[h][:, 0:1] + g[h][:, 0:1]
                dzs_ref[h] = dz[h].astype(BF16)
                abs_ref[h] = ab[h]

        tile(i, True)

        def more(state):
            j, top = state
            return jnp.logical_and(j >= 0, top > UNDERFLOW)

        def step(state):
            j, _ = state
            tile(j, False)
            return j - 1, _running_top(rl_ref)

        j_left, _ = lax.while_loop(more, step, (i - 1, _running_top(rl_ref)))
        flush(j_left + 1)
        dq_ref[...] = (jnp.where(m0, dqa_ref[0], dqa_ref[1]) * Q_SCALE).astype(BF16)

        @pl.when(i == nq - 1)
        def _():
            dk_ref[...] = dka_ref[...].astype(BF16)
            dv_ref[...] = dva_ref[...].astype(BF16)

    nb = SB_W // LANES
    (dq, dk, dv), lands = _ride(dict(
        body=body, name=name, grid=(nb, nq),
        in_specs=[pl.BlockSpec((T, LANES), lambda p, i: (i, p)),
                  pl.BlockSpec((s, LANES), lambda p, i: (0, nb + p)),
                  pl.BlockSpec((s, LANES), lambda p, i: (0, 2 * nb + p)),
                  pl.BlockSpec((T, LANES), lambda p, i: (i, p)),
                  pl.BlockSpec((T, LANES), lambda p, i: (i, p))],
        out_specs=[pl.BlockSpec((T, LANES), lambda p, i: (i, p)),
                   pl.BlockSpec((s, LANES), lambda p, i: (0, p)),
                   pl.BlockSpec((s, LANES), lambda p, i: (0, p))],
        out_shape=[jax.ShapeDtypeStruct((s, SB_W), BF16)] * 3,
        scratch_shapes=[pltpu.VMEM((2, T, LANES), F32), pltpu.VMEM((s, LANES), F32),
                        pltpu.VMEM((s, LANES), F32), pltpu.VMEM((2, T, 1), F32),
                        pltpu.VMEM((2, T, 1), F32), pltpu.VMEM((2, T, T), BF16),
                        pltpu.VMEM((2, T, T), BF16)],
        compiler_params=_ARB2, operands=[qkv, qkv, qkv, o, do]), rider)
    return dq, dk, dv, lands


def _fox_bwd(qkv, fqb, frow, fbounds, o, lse, do, name, rider=None):
    s = qkv.shape[0]
    nq = s // T

    def body(q_ref, k_ref, v_ref, fq_ref, fr_ref, fb_ref, o_ref, lse_ref, do_ref,
             dq_ref, dk_ref, dv_ref, df_ref, dqa_ref, dka_ref, dva_ref, dfa_ref, dls_ref, pbs_ref,
             kn_ref):
        i = pl.program_id(1)

        @pl.when(i == 0)
        def _():
            dka_ref[...] = jnp.zeros_like(dka_ref)
            dva_ref[...] = jnp.zeros_like(dva_ref)
            dfa_ref[...] = jnp.zeros_like(dfa_ref)

        m0 = _pair_masks()

        @pl.when(i == 0)
        def _():
            kn_ref[0], kn_ref[1] = _head_norm_tops(k_ref[...], m0)

        q = q_ref[...] * jnp.asarray(Q_SCALE, BF16)
        qn = _head_norm_tops(q, m0)
        qh = _split_pair(q, m0)
        doh = _split_pair(do_ref[...], m0)
        dsum = _row_dots(do_ref, o_ref, m0)
        fq = fq_ref[...]
        fqh = (fq[:, 0:1], fq[:, HEAD_DIM:HEAD_DIM + 1])
        lse = lse_ref[...]
        lseh = (lse[:, 0:1], lse[:, HEAD_DIM:HEAD_DIM + 1])
        causal = _iota2((T, T), 0) >= _iota2((T, T), 1)
        dqa_ref[...] = jnp.zeros_like(dqa_ref)

        hs = range(2)

        def flush(j):
            off = pl.multiple_of(j * T, T)
            k = k_ref[pl.ds(off, T), :]
            for h in hs:
                dqa_ref[h] += _dot(dls_ref[h], k)
            dka_ref[pl.ds(off, T), :] += _dot_tn(dls_ref[0], qh[0]) + _dot_tn(dls_ref[1], qh[1])
            dva_ref[pl.ds(off, T), :] += _dot_tn(pbs_ref[0], doh[0]) + _dot_tn(pbs_ref[1], doh[1])

        def tile(j, diag):
            off = pl.multiple_of(j * T, T)
            k = k_ref[pl.ds(off, T), :]
            v = v_ref[pl.ds(off, T), :]
            sc = [_dot_nt(qh[h], k) + fqh[h] - fr_ref[h:h + 1, pl.ds(off, T)] for h in hs]
            dp = [_dot_nt(doh[h], v) for h in hs]
            if not diag:
                flush(j + 1)
            p = [jnp.exp(sc[h] - lseh[h]) for h in hs]
            if diag:
                p = [jnp.where(causal, p[h], 0.0) for h in hs]
            dl = [p[h] * (dp[h] - dsum[h]) for h in hs]
            for h in hs:
                dls_ref[h] = dl[h].astype(BF16)
                pbs_ref[h] = p[h].astype(BF16)
                dfa_ref[h:h + 1, pl.ds(off, T)] -= jnp.sum(dl[h], axis=0, keepdims=True)

        tile(i, True)
        j_left = _fox_tiles_left(i, pl.program_id(0), qn, kn_ref, fb_ref, tile)
        flush(j_left + 1)
        dq_ref[...] = (jnp.where(m0, dqa_ref[0], dqa_ref[1]) * Q_SCALE).astype(BF16)

        @pl.when(i == nq - 1)
        def _():
            dk_ref[...] = dka_ref[...].astype(BF16)
            dv_ref[...] = dva_ref[...].astype(BF16)
            df_ref[...] = dfa_ref[...]

    nb = FX_W // LANES
    base = 3 * SB_W // LANES
    (dq, dk, dv, df), lands = _ride(dict(
        body=body, name=name, grid=(nb, nq),
        in_specs=[pl.BlockSpec((T, LANES), lambda p, i: (i, base + p)),
                  pl.BlockSpec((s, LANES), lambda p, i: (0, base + nb + p)),
                  pl.BlockSpec((s, LANES), lambda p, i: (0, base + 2 * nb + p)),
                  pl.BlockSpec((T, LANES), lambda p, i: (i, p)),
                  pl.BlockSpec((None, 2, s), lambda p, i: (p, 0, 0)),
                  pl.BlockSpec(memory_space=pltpu.SMEM),
                  pl.BlockSpec((T, LANES), lambda p, i: (i, p)),
                  pl.BlockSpec((T, LANES), lambda p, i: (i, p)),
                  pl.BlockSpec((T, LANES), lambda p, i: (i, p))],
        out_specs=[pl.BlockSpec((T, LANES), lambda p, i: (i, p)),
                   pl.BlockSpec((s, LANES), lambda p, i: (0, p)),
                   pl.BlockSpec((s, LANES), lambda p, i: (0, p)),
                   pl.BlockSpec((None, 2, s), lambda p, i: (p, 0, 0))],
        out_shape=[jax.ShapeDtypeStruct((s, FX_W), BF16)] * 3
        + [jax.ShapeDtypeStruct((nb, 2, s), F32)],
        scratch_shapes=[pltpu.VMEM((2, T, LANES), F32), pltpu.VMEM((s, LANES), F32),
                        pltpu.VMEM((s, LANES), F32), pltpu.VMEM((2, s), F32),
                        pltpu.VMEM((2, T, T), BF16), pltpu.VMEM((2, T, T), BF16),
                        pltpu.SMEM((2,), F32)],
        compiler_params=_ARB2, operands=[qkv, qkv, qkv, fqb, frow, fbounds, o, lse, do]), rider)
    return dq, dk, dv, df, lands


def _fox_prep_bwd(dfcol, gf, bpad, name):
    s = gf.shape[0]

    def body(df_ref, fl_ref, b_ref, dfl_ref, db_ref):
        tri = jnp.where(_iota2((T, T), 0) <= _iota2((T, T), 1), 1.0, 0.0).astype(BF16)
        carry = jnp.zeros((1, LANES), F32)
        db = jnp.zeros((1, LANES), F32)
        for blk in reversed(range(s // T)):
            rows = slice(blk * T, (blk + 1) * T)
            c = _tri3(tri, df_ref[rows, :]) + carry
            carry = c[0:1, :]
            dfl = c / (1.0 + jnp.exp(fl_ref[rows, :] + b_ref[...]))
            dfl_ref[rows, :] = dfl.astype(BF16)
            db = db + jnp.sum(dfl, axis=0, keepdims=True)
        db_ref[...] = db

    return pl.pallas_call(
        body, name=name, grid=(1,),
        in_specs=[pl.BlockSpec((s, LANES), lambda i: (0, 0)),
                  pl.BlockSpec((s, LANES), lambda i: (0, MIX_W // LANES)),
                  pl.BlockSpec((1, LANES), lambda i: (0, 0))],
        out_specs=[pl.BlockSpec((s, LANES), lambda i: (0, 0)),
                   pl.BlockSpec((1, LANES), lambda i: (0, 0))],
        out_shape=[jax.ShapeDtypeStruct((s, LANES), BF16), jax.ShapeDtypeStruct((1, LANES), F32)],
        compiler_params=_ARB1,
    )(dfcol, gf, bpad)


def _mem_bwd(qkv, kv, o, lse, do, name):
    s = qkv.shape[0]
    n = kv.shape[0]
    nq = s // T

    def body(q_ref, k_ref, v_ref, o_ref, lse_ref, do_ref, dq_ref, dk_ref, dv_ref):
        @pl.when(pl.program_id(1) == 0)
        def _():
            dk_ref[...] = jnp.zeros_like(dk_ref)
            dv_ref[...] = jnp.zeros_like(dv_ref)

        m0 = _pair_masks()
        qh = _split_pair(q_ref[...] * jnp.asarray(Q_SCALE, BF16), m0)
        doh = _split_pair(do_ref[...], m0)
        dsum = _row_dots(do_ref, o_ref, m0)
        lse = lse_ref[...]
        lseh = (lse[:, 0:1], lse[:, HEAD_DIM:HEAD_DIM + 1])
        k = k_ref[...]
        v = v_ref[...]
        dqs = []
        for h in range(2):
            p = jnp.exp(_dot_nt(qh[h], k) - lseh[h])
            dl = p * (_dot_nt(doh[h], v) - dsum[h])
            dlb = dl.astype(BF16)
            dqs.append(_dot(dlb, k))
            dk_ref[...] += _dot_tn(dlb, qh[h])
            dv_ref[...] += _dot_tn(p.astype(BF16), doh[h])
        dq_ref[...] = (jnp.where(m0, dqs[0], dqs[1]) * Q_SCALE).astype(BF16)

    nb = MEM_W // LANES
    base = (3 * SB_W + 3 * FX_W) // LANES
    return pl.pallas_call(
        body, name=name, grid=(nb, nq),
        in_specs=[pl.BlockSpec((T, LANES), lambda p, i: (i, base + p)),
                  pl.BlockSpec((n, LANES), lambda p, i: (0, p)),
                  pl.BlockSpec((n, LANES), lambda p, i: (0, nb + p)),
                  pl.BlockSpec((T, LANES), lambda p, i: (i, p)),
                  pl.BlockSpec((T, LANES), lambda p, i: (i, p)),
                  pl.BlockSpec((T, LANES), lambda p, i: (i, p))],
        out_specs=[pl.BlockSpec((T, LANES), lambda p, i: (i, p)),
                   pl.BlockSpec((n, LANES), lambda p, i: (0, p)),
                   pl.BlockSpec((n, LANES), lambda p, i: (0, p))],
        out_shape=[jax.ShapeDtypeStruct((s, MEM_W), BF16), jax.ShapeDtypeStruct((n, MEM_W), F32),
                   jax.ShapeDtypeStruct((n, MEM_W), F32)],
        compiler_params=_ARB2,
    )(qkv, kv, kv, o, lse, do)


def _memkv_bwd(mem, mnw, wkv, dk, dv, name):
    n = mem.shape[0]

    def body(mem_ref, mnw_ref, w_ref, dk_ref, dv_ref, dw_ref, dmnw_ref):
        mv = mem_ref[...]
        r = lax.rsqrt(jnp.mean(mv * mv, axis=-1, keepdims=True) + EPS)
        mh = mv * r
        hm = (mh * mnw_ref[...]).astype(BF16)
        dkv = jnp.concatenate([dk_ref[...], dv_ref[...]], axis=1).astype(BF16)
        dw_ref[...] = _dot_tn(hm, dkv)
        dhm = _dot_nt(dkv, w_ref[...])
        dmnw_ref[...] = jnp.sum(dhm * mh, axis=0, keepdims=True)

    return pl.pallas_call(
        body, name=name, grid=(1,),
        in_specs=[pl.BlockSpec((n, D_MODEL), lambda i: (0, 0)),
                  pl.BlockSpec((1, D_MODEL), lambda i: (0, 0)),
                  pl.BlockSpec((D_MODEL, 2 * MEM_W), lambda i: (0, 0)),
                  pl.BlockSpec((n, MEM_W), lambda i: (0, 0)),
                  pl.BlockSpec((n, MEM_W), lambda i: (0, 0))],
        out_specs=[pl.BlockSpec((D_MODEL, 2 * MEM_W), lambda i: (0, 0)),
                   pl.BlockSpec((1, D_MODEL), lambda i: (0, 0))],
        out_shape=[jax.ShapeDtypeStruct((D_MODEL, 2 * MEM_W), F32),
                   jax.ShapeDtypeStruct((1, D_MODEL), F32)],
        compiler_params=_ARB1,
    )(mem, mnw, wkv, dk, dv)


def _inproj_bwd_dx(pieces, w_r, x, nw, dxo, name):
    s = x.shape[0]
    n = len(pieces)
    widths = [p.shape[1] for p in pieces]

    def body(*refs):
        piece_refs = refs[:n]
        w_ref, x_ref, nw_ref, dxo_ref, dx_ref, h_ref, dnw_ref, dp_ref = refs[n:]

        @pl.when(pl.program_id(0) == 0)
        def _():
            dnw_ref[...] = jnp.zeros_like(dnw_ref)

        col = 0
        for r, wd in zip(piece_refs, widths):
            dp_ref[:, col:col + wd] = r[...]
            col += wd
        dp_ref[:, col:] = jnp.zeros((TM, WR_W - col), BF16)
        dh = _dot(dp_ref[...], w_ref[...])
        xv = x_ref[...]
        nw = nw_ref[...]
        r = lax.rsqrt(jnp.mean(xv * xv, axis=-1, keepdims=True) + EPS)
        xh = xv * r
        h_ref[...] = (xh * nw).astype(BF16)
        dnw_ref[...] += jnp.sum(dh * xh, axis=0, keepdims=True)
        dxh = dh * nw
        dx_ref[...] = r * (dxh - xh * jnp.mean(dxh * xh, axis=-1, keepdims=True)) + dxo_ref[...]

    return pl.pallas_call(
        body, name=name, grid=(s // TM,),
        in_specs=[pl.BlockSpec((TM, wd), lambda i: (i, 0)) for wd in widths]
        + [pl.BlockSpec((WR_W, D_MODEL), lambda i: (0, 0)),
           pl.BlockSpec((TM, D_MODEL), lambda i: (i, 0)),
           pl.BlockSpec((1, D_MODEL), lambda i: (0, 0)),
           pl.BlockSpec((TM, D_MODEL), lambda i: (i, 0))],
        out_specs=[pl.BlockSpec((TM, D_MODEL), lambda i: (i, 0)),
                   pl.BlockSpec((TM, D_MODEL), lambda i: (i, 0)),
                   pl.BlockSpec((1, D_MODEL), lambda i: (0, 0)),
                   pl.BlockSpec((TM, WR_W), lambda i: (i, 0))],
        out_shape=[jax.ShapeDtypeStruct((s, D_MODEL), F32), jax.ShapeDtypeStruct((s, D_MODEL), BF16),
                   jax.ShapeDtypeStruct((1, D_MODEL), F32), jax.ShapeDtypeStruct((s, WR_W), BF16)],
        compiler_params=_ARB1,
    )(*pieces, w_r, x, nw, dxo)


def _inproj_bwd_dw(h, dproj, name):
    s = dproj.shape[0]
    tn = 256

    def body(h_ref, dp_ref, dw_ref):
        dw_ref[...] = _dot_tn(dp_ref[...], h_ref[...])

    return pl.pallas_call(
        body, name=name, grid=(WR_W // tn,),
        in_specs=[pl.BlockSpec((s, D_MODEL), lambda j: (0, 0)),
                  pl.BlockSpec((s, tn), lambda j: (0, j))],
        out_specs=pl.BlockSpec((tn, D_MODEL), lambda j: (j, 0)),
        out_shape=jax.ShapeDtypeStruct((WR_W, D_MODEL), F32),
        compiler_params=_ARB1,
    )(h, dproj)


def _rearrange_w_in(wt):
    pad = jnp.zeros((FL_PAD - FOX_HEADS,) + wt.shape[1:], wt.dtype)
    return jnp.concatenate([wt[:3072], wt[3080:3336], wt[3336:IN_W], wt[3072:3080], pad], axis=0)


def _restore_w_in(g):
    gate0 = QKV_W
    fl0 = QKV_W + MIX_W
    return jnp.concatenate(
        [g[:3072], g[fl0:fl0 + FOX_HEADS], g[3072:QKV_W], g[gate0:fl0]], axis=0)


def _pad_lanes(v, width=LANES):
    return jnp.pad(v, (0, width - v.shape[0])).reshape(1, width)


def _layer_fwd(xs, mem, nw, w_r, b_forget, mnw, wkv, onw, wout, l, travel=None):
    s = xs.shape[0]
    bpad = _pad_lanes(b_forget)
    qkv, gf = _inproj_fwd(xs, nw, w_r, f"inproj_fwd_{l}")
    f = _fox_prep_fwd(gf, bpad, f"fox_prep_fwd_{l}")
    f8 = f[:, :FOX_HEADS]
    fqb = jnp.repeat(f8, HEAD_DIM, axis=1)
    frow = f8.T.reshape(FOX_HEADS // 2, 2, s)
    fbounds = jnp.concatenate([f8[0::T].T, f8[T - 1::T].T], axis=0)
    travel = _Travel(travel)
    ysb, _ = travel.ride(0, _sb_fwd, qkv, f"sb_fwd_{l}")
    yfx, lse_fx, _ = travel.ride(1, _fox_fwd, qkv, fqb, frow, fbounds, f"fox_fwd_{l}")
    kv = _memkv_fwd(mem, mnw, wkv, f"memkv_fwd_{l}")
    ym, lse_m = _mem_fwd(qkv, kv, f"mem_fwd_{l}")
    xn = _outproj_fwd(ysb, yfx, ym, gf, onw, wout, xs, f"outproj_fwd_{l}")
    saved = (xs, nw, mnw, onw, bpad, qkv, gf, fqb, frow, fbounds, ysb, yfx, lse_fx, kv, ym, lse_m)
    return xn, saved, travel.lands


class _Travel:
    def __init__(self, plan):
        self.plan = plan
        self.lands = None if plan is None else _new_lands(plan[0], plan[1])

    def ride(self, n, fn, *args):
        if self.plan is None:
            return fn(*args)
        srcs, scatter, legs = self.plan
        idx, rows = legs[n]
        out = fn(*args, rider=_Rider([srcs[a] for a in idx], [self.lands[a] for a in idx],
                                     scatter, rows))
        for a, land in zip(idx, out[-1]):
            self.lands[a] = land
        return out


def _layer_bwd(dx, saved, mem, w_r, wkv, wout, l, travel=None):
    xs, nw, mnw, onw, bpad, qkv, gf, fqb, frow, fbounds, ysb, yfx, lse_fx, kv, ym, lse_m = saved
    s = xs.shape[0]
    travel = _Travel(travel)
    dysb, dyfx, dym, dgate, dwout, donw = _outproj_bwd(
        dx, wout, ysb, yfx, ym, gf, onw, f"outproj_bwd_{l}")
    sdq, sdk, sdv, _ = travel.ride(0, _sb_bwd, qkv, ysb, dysb, f"sb_bwd_{l}")
    fdq, fdk, fdv, dfrow, _ = travel.ride(1, _fox_bwd, qkv, fqb, frow, fbounds, yfx, lse_fx, dyfx,
                                          f"fox_bwd_{l}")
    dfcol = jnp.pad(dfrow.reshape(FOX_HEADS, s).T, ((0, 0), (0, LANES - FOX_HEADS)))
    dfl, db = _fox_prep_bwd(dfcol, gf, bpad, f"fox_prep_bwd_{l}")
    dmq, dmk, dmv = _mem_bwd(qkv, kv, ym, lse_m, dym, f"mem_bwd_{l}")
    dwkv, dmnw = _memkv_bwd(mem, mnw, wkv, dmk, dmv, f"memkv_bwd_{l}")
    dx, ht, dnw, dproj = _inproj_bwd_dx([sdq, sdk, sdv, fdq, fdk, fdv, dmq, dgate, dfl],
                                        w_r, xs, nw, dx, f"inproj_bwd_dx_{l}")
    dwr = _inproj_bwd_dw(ht, dproj, f"inproj_bwd_dw_{l}")
    grads = dict(norm_w=dnw[0], w_r=dwr, b_forget=db[0, :FOX_HEADS], mem_norm_w=dmnw[0],
                 w_mem_kv=dwkv, out_norm_w=donw[0], w_out=dwout)
    return dx, grads, travel.lands


_ANY = pl.BlockSpec(memory_space=pl.ANY)


def _my_place():
    return lax.axis_index("x"), lax.axis_index("y"), lax.axis_index("c")


def _flip(v, bit):
    return 1 - v if bit else v


def _block_index(px, py, pc):
    return 4 * px + 2 * py + pc


def _all_gather_weights(shards, name):
    n = len(shards)

    def body(*refs):
        ins, outs = refs[:n], refs[n:2 * n]
        send_sems, recv_sems, local_sems = refs[2 * n:]
        x, y, c = _my_place()
        me = (x, y, c)
        sibling = (x, y, 1 - c)
        chips = [(1 - x, y), (x, 1 - y), (1 - x, 1 - y)]

        def copy(a, k, block, to, src=None):
            dst = outs[a].at[_block_index(*block)]
            return pltpu.make_async_remote_copy(
                src_ref=dst if src is None else src, dst_ref=dst,
                send_sem=send_sems.at[a, k], recv_sem=recv_sems.at[a, k],
                device_id=to, device_id_type=pl.DeviceIdType.MESH)

        mine = [pltpu.make_async_copy(ins[a], outs[a].at[_block_index(*me)], local_sems.at[a])
                for a in range(n)]
        for cp in mine:
            cp.start()
        first = []
        for a in range(n):
            first.append(copy(a, 0, me, sibling, src=ins[a]))
            first += [copy(a, 1 + j, me, (*chip, c), src=ins[a]) for j, chip in enumerate(chips)]
        for cp in first:
            cp.start()
        passed = []
        for j, chip in enumerate(chips):
            for a in range(n):
                copy(a, 1 + j, (*chip, c), me).wait_recv()
                fwd = copy(a, 4 + j, (*chip, c), sibling)
                fwd.start()
                passed.append(fwd)
        for a in range(n):
            copy(a, 0, sibling, me).wait_recv()
            for j, chip in enumerate(chips):
                copy(a, 4 + j, (*chip, 1 - c), me).wait_recv()
        for cp in first + passed:
            cp.wait_send()
        for cp in mine:
            cp.wait()

    return pl.pallas_call(
        body, name=name,
        in_specs=[_ANY] * n, out_specs=[_ANY] * n,
        out_shape=[jax.ShapeDtypeStruct((N_DEV,) + v.shape, v.dtype) for v in shards],
        scratch_shapes=[pltpu.SemaphoreType.DMA((n, 7)), pltpu.SemaphoreType.DMA((n, 7)),
                        pltpu.SemaphoreType.DMA((n,))],
    )(*shards)


def _exchange_blocks(blocked, name):
    n = len(blocked)

    def body(*refs):
        ins, outs = refs[:n], refs[n:2 * n]
        send_sems, recv_sems, local_sems = refs[2 * n:]
        x, y, c = _my_place()
        mine_idx = _block_index(x, y, c)
        local = [pltpu.make_async_copy(ins[a].at[mine_idx], outs[a].at[mine_idx], local_sems.at[a])
                 for a in range(n)]
        for cp in local:
            cp.start()
        sends, arrivals = [], []
        for r in range(1, N_DEV):
            peer = (_flip(x, r & 4), _flip(y, r & 2), _flip(c, r & 1))
            peer_idx = _block_index(*peer)
            for a in range(n):
                sems = dict(send_sem=send_sems.at[a, r - 1], recv_sem=recv_sems.at[a, r - 1],
                            device_id=peer, device_id_type=pl.DeviceIdType.MESH)
                sends.append(pltpu.make_async_remote_copy(
                    src_ref=ins[a].at[peer_idx], dst_ref=outs[a].at[mine_idx], **sems))
                arrivals.append(pltpu.make_async_remote_copy(
                    src_ref=ins[a].at[peer_idx], dst_ref=outs[a].at[peer_idx], **sems))
        for cp in sends:
            cp.start()
        for cp in arrivals:
            cp.wait_recv()
        for cp in sends:
            cp.wait_send()
        for cp in local:
            cp.wait()

    return pl.pallas_call(
        body, name=name,
        in_specs=[_ANY] * n, out_specs=[_ANY] * n,
        out_shape=[jax.ShapeDtypeStruct(v.shape, v.dtype) for v in blocked],
        scratch_shapes=[pltpu.SemaphoreType.DMA((n, 7)), pltpu.SemaphoreType.DMA((n, 7)),
                        pltpu.SemaphoreType.DMA((n,))],
    )(*blocked)


class _Rider(NamedTuple):
    srcs: list
    lands: list
    scatter: bool
    part: list


def _new_lands(srcs, scatter):
    return [lax.empty(v.shape if scatter else (N_DEV,) + v.shape, v.dtype) for v in srcs]


def _rider_copies(srcs, lands, send_sems, recv_sems, rider):
    x, y, c = _my_place()
    mine_idx = _block_index(x, y, c)

    def window(ref, a):
        if rider.part[a] is None:
            return ref
        dim, start, size = rider.part[a]
        return ref.at[(slice(None),) * dim + (pl.ds(start, size),)]

    sends, arrivals = [], []
    for r in range(1, N_DEV):
        peer = (_flip(x, r & 4), _flip(y, r & 2), _flip(c, r & 1))
        peer_idx = _block_index(*peer)
        for a in range(len(srcs)):
            src = window(srcs[a].at[peer_idx] if rider.scatter else srcs[a], a)
            k = 7 * a + r - 1
            sems = dict(send_sem=send_sems.at[k], recv_sem=recv_sems.at[k],
                        device_id=peer, device_id_type=pl.DeviceIdType.MESH)
            sends.append(pltpu.make_async_remote_copy(
                src_ref=src, dst_ref=window(lands[a].at[mine_idx], a), **sems))
            arrivals.append(pltpu.make_async_remote_copy(
                src_ref=src, dst_ref=window(lands[a].at[peer_idx], a), **sems))
    return sends, arrivals


def _ride(call, rider):
    call = dict(call)
    body, grid = call.pop("body"), call["grid"]
    operands = call.pop("operands")
    if rider is None:
        return list(pl.pallas_call(body, **call)(*operands)), None
    n_in, n_out = len(call["in_specs"]), len(call["out_specs"])
    n_scratch = len(call["scratch_shapes"])
    m = len(rider.srcs)

    def riding(*refs):
        main_in, srcs, lands = refs[:n_in], refs[n_in:n_in + m], refs[n_in + m:n_in + 2 * m]
        main_out = refs[n_in + 2 * m:n_in + 2 * m + n_out]
        rest = refs[n_in + 3 * m + n_out:]
        send_sems, recv_sems = rest[n_scratch:]
        at = [pl.program_id(d) for d in range(len(grid))]
        first = functools.reduce(jnp.logical_and, [p == 0 for p in at])
        last = functools.reduce(jnp.logical_and, [p == g - 1 for p, g in zip(at, grid)])
        sends, arrivals = _rider_copies(srcs, lands, send_sems, recv_sems, rider)

        @pl.when(first)
        def _():
            for cp in sends:
                cp.start()

        body(*main_in, *main_out, *rest[:n_scratch])

        @pl.when(last)
        def _():
            for cp in arrivals:
                cp.wait_recv()
            for cp in sends:
                cp.wait_send()

    call["in_specs"] = list(call["in_specs"]) + [_ANY] * (2 * m)
    call["out_specs"] = list(call["out_specs"]) + [_ANY] * m
    call["out_shape"] = list(call["out_shape"]) + [
        jax.ShapeDtypeStruct(v.shape, v.dtype) for v in rider.lands]
    call["scratch_shapes"] = list(call["scratch_shapes"]) + [
        pltpu.SemaphoreType.DMA((7 * m,)), pltpu.SemaphoreType.DMA((7 * m,))]
    call["input_output_aliases"] = {n_in + m + a: n_out + a for a in range(m)}
    outs = pl.pallas_call(riding, **call)(*operands, *rider.srcs, *rider.lands)
    return list(outs[:n_out]), list(outs[n_out:])


def _adamw_sum(parts, w, m, v, tile, name):
    depth, nrow, ncol = w.shape
    rows, cols = tile
    c1 = 1.0 / (1.0 - ADAM_B1 ** ADAM_STEP)
    c2 = 1.0 / (1.0 - ADAM_B2 ** ADAM_STEP)

    def body(*refs):
        p_refs = refs[:depth]
        w_ref, m_ref, v_ref, g_ref, d_ref, nm_ref, nv_ref = refs[depth:]
        layer = pl.program_id(0)
        for l in range(depth):
            @pl.when(layer == l)
            def _(p_ref=p_refs[l]):
                g = p_ref[0].astype(F32)
                for k in range(1, N_DEV):
                    g = g + p_ref[k].astype(F32)
                nm = ADAM_B1 * m_ref[...] + (1.0 - ADAM_B1) * g
                nv = ADAM_B2 * v_ref[...] + (1.0 - ADAM_B2) * (g * g)
                g_ref[...] = g
                nm_ref[...] = nm
                nv_ref[...] = nv
                d_ref[...] = -ADAM_LR * ((nm * c1) / (jnp.sqrt(nv * c2) + ADAM_EPS)
                                         + ADAM_WD * w_ref[...])

    def part_spec(l):
        return pl.BlockSpec((N_DEV, rows, cols), lambda q, i, j: (
            0, jnp.where(q == l, i, 0), jnp.where(q == l, j, 0)))

    blk = pl.BlockSpec((None, rows, cols), lambda q, i, j: (q, i, j))
    return pl.pallas_call(
        body, name=name, grid=(depth, nrow // rows, ncol // cols),
        in_specs=[part_spec(l) for l in range(depth)] + [blk, blk, blk],
        out_specs=[blk] * 4,
        out_shape=[jax.ShapeDtypeStruct(w.shape, F32)] * 4,
        compiler_params=pltpu.CompilerParams(
            dimension_semantics=("arbitrary", "arbitrary", "arbitrary")),
    )(*parts, w, m, v)


def _pack_small(norm_w, mem_norm_w, out_norm_w, final_norm_w, b_forget):
    onw = jnp.pad(out_norm_w.reshape(20, LANES), ((0, 4), (0, 0)))
    b = jnp.pad(b_forget, ((0, 6), (0, LANES - FOX_HEADS)))
    return jnp.concatenate([norm_w.reshape(16, LANES), mem_norm_w.reshape(16, LANES), onw,
                            final_norm_w.reshape(8, LANES), b], axis=0)


def _unpack_small(p):
    return (p[0:16].reshape(2, D_MODEL), p[16:32].reshape(2, D_MODEL), p[32:52].reshape(2, MIX_W),
            p[56:64].reshape(D_MODEL), p[64:66, :FOX_HEADS])


def kernel(x, mem, norm_w, w_in, b_forget, mem_norm_w, w_mem_kv, out_norm_w, w_out, final_norm_w, loss_target, m_norm_w, m_w_in, m_b_forget, m_mem_norm_w, m_w_mem_kv, m_out_norm_w, m_w_out, m_final_norm_w, v_norm_w, v_w_in, v_b_forget, v_mem_norm_w, v_w_mem_kv, v_out_norm_w, v_w_out, v_final_norm_w):
    kv_rows = w_mem_kv.shape[1]
    out_rows = w_out.shape[1]
    me = _block_index(*_my_place())

    def shards(l):
        return [w_in[l].T.astype(BF16), w_mem_kv[l].astype(BF16), w_out[l].astype(BF16)]

    def lay_out(g_in, g_kv, g_out):
        return (_rearrange_w_in(g_in.reshape(IN_W, D_MODEL)), g_kv.reshape(D_MODEL, 2 * MEM_W),
                g_out.reshape(MIX_W, D_MODEL))

    def blocks(gr):
        return [_restore_w_in(gr["w_r"]).reshape(N_DEV, SHARD_W, D_MODEL).astype(BF16),
                gr["w_mem_kv"].reshape(N_DEV, kv_rows, 2 * MEM_W).astype(BF16),
                gr["w_out"].reshape(N_DEV, out_rows, D_MODEL).astype(BF16)]

    def row(v):
        return v.reshape(1, -1)

    half = D_MODEL // 2
    legs = [([0, 1], [(1, 0, half), None]), ([0, 2], [(1, half, half), None])]

    w_r0, wkv0, wout0 = lay_out(*_all_gather_weights(shards(0), "all_gather_l0"))
    shards1 = shards(1)
    x1, saved0, lands = _layer_fwd(x[0], mem[0], row(norm_w[0]), w_r0, b_forget[0],
                                   row(mem_norm_w[0]), wkv0, row(out_norm_w[0]), wout0, 0,
                                   travel=(shards1, False, legs))
    w_r1, wkv1, wout1 = lay_out(*[
        lax.dynamic_update_slice(land, own[None], (me,) + (0,) * own.ndim)
        for land, own in zip(lands, shards1)])
    x2, saved1, _ = _layer_fwd(x1, mem[0], row(norm_w[1]), w_r1, b_forget[1], row(mem_norm_w[1]),
                               wkv1, row(out_norm_w[1]), wout1, 1)

    dx2, loss, dfnw = _final_fwd_bwd(x2, row(final_norm_w), loss_target[0], "final_fwd_bwd")
    loss = lax.psum(loss[0, 0], ("x", "y", "c"))

    dx1, gr1, _ = _layer_bwd(dx2, saved1, mem[0], w_r1, wkv1, wout1, 1)
    sent = blocks(gr1)
    grad_x, gr0, lands = _layer_bwd(dx1, saved0, mem[0], w_r0, wkv0, wout0, 0,
                                    travel=(sent, True, legs))

    def both(name):
        return jnp.stack([gr0[name], gr1[name]])

    small = _pack_small(both("norm_w"), both("mem_norm_w"), both("out_norm_w"), dfnw[0],
                        both("b_forget"))
    p_small = jnp.broadcast_to(small[None], (N_DEV, SMALL_ROWS, LANES))
    r_in0, r_kv0, r_out0, r_small = _exchange_blocks(blocks(gr0) + [p_small], "exchange_grads_l0")
    r_in1, r_kv1, r_out1 = [
        lax.dynamic_update_slice(land, lax.dynamic_slice_in_dim(src, me, 1, axis=0),
                                 (me,) + (0,) * (land.ndim - 1))
        for land, src in zip(lands, sent)]

    def t(v):
        return jnp.transpose(v, (0, 2, 1))

    g_w_in, d_w_in, nm_w_in, nv_w_in = [t(v) for v in _adamw_sum(
        [r_in0, r_in1], t(w_in), t(m_w_in), t(v_w_in), (SHARD_W, 256), "adamw_w_in")]
    g_w_kv, d_w_kv, nm_w_kv, nv_w_kv = _adamw_sum(
        [r_kv0, r_kv1], w_mem_kv, m_w_mem_kv, v_w_mem_kv, (kv_rows, 2 * MEM_W), "adamw_w_mem_kv")
    g_w_out, d_w_out, nm_w_out, nv_w_out = _adamw_sum(
        [r_out0, r_out1], w_out, m_w_out, v_w_out, (out_rows, D_MODEL), "adamw_w_out")
    w_small = _pack_small(norm_w, mem_norm_w, out_norm_w, final_norm_w, b_forget)[None]
    m_small = _pack_small(m_norm_w, m_mem_norm_w, m_out_norm_w, m_final_norm_w, m_b_forget)[None]
    v_small = _pack_small(v_norm_w, v_mem_norm_w, v_out_norm_w, v_final_norm_w, v_b_forget)[None]
    small_out = _adamw_sum([r_small], w_small, m_small, v_small, (SMALL_ROWS, LANES), "adamw_small")
    (g_nw, g_mnw, g_onw, g_fnw, g_b), (d_nw, d_mnw, d_onw, d_fnw, d_b), \
        (nm_nw, nm_mnw, nm_onw, nm_fnw, nm_b), (nv_nw, nv_mnw, nv_onw, nv_fnw, nv_b) = [
            _unpack_small(t[0]) for t in small_out]

    return (loss, grad_x[None],
            g_nw, g_w_in, g_b, g_mnw, g_w_kv, g_onw, g_w_out, g_fnw,
            d_nw, d_w_in, d_b, d_mnw, d_w_kv, d_onw, d_w_out, d_fnw,
            nm_nw, nm_w_in, nm_b, nm_mnw, nm_w_kv, nm_onw, nm_w_out, nm_fnw,
            nv_nw, nv_w_in, nv_b, nv_mnw, nv_w_kv, nv_onw, nv_w_out, nv_fnw)
```

```python
import functools
from typing import NamedTuple

import jax
import jax.numpy as jnp
from jax import lax
from jax.experimental import pallas as pl
from jax.experimental.pallas import tpu as pltpu

F32 = jnp.float32
BF16 = jnp.bfloat16

N_DEV = 8
D_MODEL = 1024
HEAD_DIM = 64
LANES = 128
SB_W = 512
FX_W = 512
MEM_W = 256
MIX_W = 1280
FOX_HEADS = 8
IN_W = 4616
SHARD_W = IN_W // N_DEV
QKV_W = 3 * SB_W + 3 * FX_W + MEM_W
FL_PAD = 256
GF_W = MIX_W + FL_PAD
WR_W = QKV_W + GF_W
EPS = 1e-6
T = 256
TM = 256
Q_SCALE = 0.125
NEG = -1e30
UNDERFLOW = -110.0

ADAM_LR = 0.001
ADAM_B1 = 0.9
ADAM_B2 = 0.999
ADAM_EPS = 1e-08
ADAM_WD = 0.01
ADAM_STEP = 10

SMALL_ROWS = 72

_NT = (((1,), (1,)), ((), ()))
_TN = (((0,), (0,)), ((), ()))

_ARB1 = pltpu.CompilerParams(dimension_semantics=("arbitrary",))
_ARB2 = pltpu.CompilerParams(dimension_semantics=("arbitrary", "arbitrary"))


def _dot(a, b):
    return jnp.dot(a, b, preferred_element_type=F32)


def _dot_nt(a, b):
    return lax.dot_general(a, b, _NT, preferred_element_type=F32)


def _dot_tn(a, b):
    return lax.dot_general(a, b, _TN, preferred_element_type=F32)


def _split2(x):
    hi = x.astype(BF16)
    lo = (x - hi.astype(F32)).astype(BF16)
    return hi, lo


def _stack2(u):
    return jnp.concatenate([u, u], axis=0)


def _cum2(x, u2):
    hi, lo = _split2(x)
    return _dot(jnp.concatenate([hi, lo], axis=1), u2)


def _tri3(tri, x):
    hi = x.astype(BF16)
    r1 = x - hi.astype(F32)
    mid = r1.astype(BF16)
    lo = (r1 - mid.astype(F32)).astype(BF16)
    return _dot(tri, hi) + _dot(tri, mid) + _dot(tri, lo)


def _iota2(shape, dim):
    return lax.broadcasted_iota(jnp.int32, shape, dim)


def _head_block_diag():
    r = _iota2((LANES, LANES), 0) // HEAD_DIM
    c = _iota2((LANES, LANES), 1) // HEAD_DIM
    return _stack2(jnp.where(r == c, 1.0, 0.0).astype(BF16))


def _head_mean(x, bd):
    return _cum2(x, bd) * (1.0 / HEAD_DIM)


def _sigmoid(x):
    return 1.0 / (1.0 + jnp.exp(-x))


def _log_sigmoid(x):
    return jnp.minimum(x, 0.0) - jnp.log(1.0 + jnp.exp(-jnp.abs(x)))


def _running_top(r_ref):
    return jnp.max(jnp.maximum(r_ref[0], r_ref[1]))


def _head_norm_tops(x, m0):
    x2 = x.astype(F32)
    x2 = x2 * x2
    zero = jnp.zeros_like(x2)
    return (jnp.max(jnp.sqrt(jnp.sum(jnp.where(m0, x2, zero), axis=1, keepdims=True))),
            jnp.max(jnp.sqrt(jnp.sum(jnp.where(m0, zero, x2), axis=1, keepdims=True))))


def _fox_tiles_left(i, pair, qn, kn_ref, fb_ref, tile):
    def bound(j):
        b = [2.0 * qn[h] * kn_ref[h] + fb_ref[2 * pair + h, i] - fb_ref[FOX_HEADS + 2 * pair + h, j]
             for h in range(2)]
        return jnp.maximum(b[0], b[1])

    def more(j):
        return jnp.logical_and(j >= 0, bound(jnp.maximum(j, 0)) > UNDERFLOW)

    def step(j):
        tile(j, False)
        return j - 1

    return lax.while_loop(more, step, i - 1)


def _pair_masks():
    lane = _iota2((1, LANES), 1)
    return lane < HEAD_DIM


def _split_pair(x, m0):
    zero = jnp.zeros_like(x)
    return jnp.where(m0, x, zero), jnp.where(m0, zero, x)


def _inproj_fwd(x, nw, w_r, name):
    s = x.shape[0]

    def body(x_ref, nw_ref, w_ref, qkv_ref, gf_ref):
        xv = x_ref[...]
        r = lax.rsqrt(jnp.mean(xv * xv, axis=-1, keepdims=True) + EPS)
        h = (xv * r * nw_ref[...]).astype(BF16)
        for c in range(0, QKV_W, 256):
            qkv_ref[:, c:c + 256] = _dot_nt(h, w_ref[c:c + 256, :]).astype(BF16)
        for c in range(0, GF_W, 256):
            gf_ref[:, c:c + 256] = _dot_nt(h, w_ref[QKV_W + c:QKV_W + c + 256, :])

    return pl.pallas_call(
        body, name=name, grid=(s // TM,),
        in_specs=[pl.BlockSpec((TM, D_MODEL), lambda i: (i, 0)),
                  pl.BlockSpec((1, D_MODEL), lambda i: (0, 0)),
                  pl.BlockSpec((WR_W, D_MODEL), lambda i: (0, 0))],
        out_specs=[pl.BlockSpec((TM, QKV_W), lambda i: (i, 0)),
                   pl.BlockSpec((TM, GF_W), lambda i: (i, 0))],
        out_shape=[jax.ShapeDtypeStruct((s, QKV_W), BF16), jax.ShapeDtypeStruct((s, GF_W), F32)],
        compiler_params=_ARB1,
    )(x, nw, w_r)


def _fox_prep_fwd(gf, bpad, name):
    s = gf.shape[0]

    def body(fl_ref, b_ref, f_ref):
        tri = jnp.where(_iota2((T, T), 0) >= _iota2((T, T), 1), 1.0, 0.0).astype(BF16)
        carry = jnp.zeros((1, LANES), F32)
        for blk in range(s // T):
            lf = _log_sigmoid(fl_ref[blk * T:(blk + 1) * T, :] + b_ref[...])
            c = _tri3(tri, lf) + carry
            f_ref[blk * T:(blk + 1) * T, :] = c
            carry = c[T - 1:T, :]

    return pl.pallas_call(
        body, name=name, grid=(1,),
        in_specs=[pl.BlockSpec((s, LANES), lambda i: (0, MIX_W // LANES)),
                  pl.BlockSpec((1, LANES), lambda i: (0, 0))],
        out_specs=pl.BlockSpec((s, LANES), lambda i: (0, 0)),
        out_shape=jax.ShapeDtypeStruct((s, LANES), F32),
        compiler_params=_ARB1,
    )(gf, bpad)


def _sb_fwd(qkv, name, rider=None):
    s = qkv.shape[0]

    def body(q_ref, k_ref, v_ref, o_ref, acc_ref, r_ref, as_ref):
        i = pl.program_id(1)
        m0 = _pair_masks()
        qh = _split_pair(q_ref[...] * jnp.asarray(Q_SCALE, BF16), m0)
        strict = _iota2((T, T), 0) > _iota2((T, T), 1)
        u2 = _stack2(jnp.where(strict, 1.0, 0.0).astype(BF16))
        acc_ref[...] = jnp.zeros_like(acc_ref)
        r_ref[...] = jnp.zeros_like(r_ref)
        hs = range(2)

        def flush(j):
            v = v_ref[pl.ds(pl.multiple_of(j * T, T), T), :]
            for h in hs:
                acc_ref[h] += _dot(as_ref[h], v)

        def tile(j, diag):
            k = k_ref[pl.ds(pl.multiple_of(j * T, T), T), :]
            z = [_dot_nt(qh[h], k) for h in hs]
            if not diag:
                flush(j + 1)
            la = [jnp.minimum(z[h], 0.0) - jnp.log(1.0 + jnp.exp(-jnp.abs(z[h]))) for h in hs]
            lf = [la[h] - z[h] for h in hs]
            if diag:
                lf = [jnp.where(strict, lf[h], 0.0) for h in hs]
            cin = [_cum2(lf[h], u2) for h in hs]
            a = [jnp.exp(la[h] + cin[h] + r_ref[h]) for h in hs]
            if diag:
                a = [jnp.where(strict, a[h], 0.0) for h in hs]
            for h in hs:
                r_ref[h] += cin[h][:, 0:1] + lf[h][:, 0:1]
                as_ref[h] = a[h].astype(BF16)

        tile(i, True)

        def more(state):
            j, top = state
            return jnp.logical_and(j >= 0, top > UNDERFLOW)

        def step(state):
            j, _ = state
            tile(j, False)
            return j - 1, _running_top(r_ref)

        j_left, _ = lax.while_loop(more, step, (i - 1, _running_top(r_ref)))
        flush(j_left + 1)
        o_ref[...] = jnp.where(m0, acc_ref[0], acc_ref[1])

    nb = SB_W // LANES
    (ysb,), lands = _ride(dict(
        body=body, name=name, grid=(nb, s // T),
        in_specs=[pl.BlockSpec((T, LANES), lambda p, i: (i, p)),
                  pl.BlockSpec((s, LANES), lambda p, i: (0, nb + p)),
                  pl.BlockSpec((s, LANES), lambda p, i: (0, 2 * nb + p))],
        out_specs=[pl.BlockSpec((T, LANES), lambda p, i: (i, p))],
        out_shape=[jax.ShapeDtypeStruct((s, SB_W), F32)],
        scratch_shapes=[pltpu.VMEM((2, T, LANES), F32), pltpu.VMEM((2, T, 1), F32),
                        pltpu.VMEM((2, T, T), BF16)],
        compiler_params=_ARB2, operands=[qkv, qkv, qkv]), rider)
    return ysb, lands


def _fox_fwd(qkv, fqb, frow, fbounds, name, rider=None):
    s = qkv.shape[0]

    def body(q_ref, k_ref, v_ref, fq_ref, fr_ref, fb_ref, o_ref, lse_ref, acc_ref, m_ref, ps_ref,
             kn_ref):
        pair = pl.program_id(0)
        i = pl.program_id(1)
        m0 = _pair_masks()

        @pl.when(i == 0)
        def _():
            kn_ref[0], kn_ref[1] = _head_norm_tops(k_ref[...], m0)

        q = q_ref[...] * jnp.asarray(Q_SCALE, BF16)
        qn = _head_norm_tops(q, m0)
        qh = _split_pair(q, m0)
        fq = fq_ref[...]
        fqh = (fq[:, 0:1], fq[:, HEAD_DIM:HEAD_DIM + 1])
        causal = _iota2((T, T), 0) >= _iota2((T, T), 1)
        ones = jnp.ones((T, LANES), BF16)
        acc_ref[...] = jnp.zeros_like(acc_ref)
        m_ref[...] = jnp.full_like(m_ref, NEG)
        hs = range(2)

        def flush(j):
            v = v_ref[pl.ds(pl.multiple_of(j * T, T), T), :]
            va2 = _stack2(jnp.concatenate([v, ones], axis=1))
            for h in hs:
                acc_ref[h] += _dot(ps_ref[h], va2)

        def tile(j, diag):
            off = pl.multiple_of(j * T, T)
            k = k_ref[pl.ds(off, T), :]
            sc = [_dot_nt(qh[h], k) + fqh[h] - fr_ref[h:h + 1, pl.ds(off, T)] for h in hs]
            if not diag:
                flush(j + 1)
            if diag:
                sc = [jnp.where(causal, sc[h], NEG) for h in hs]
            m_new = [jnp.maximum(m_ref[h], jnp.max(sc[h], axis=1, keepdims=True)) for h in hs]
            p = [jnp.exp(sc[h] - m_new[h]) for h in hs]
            for h in hs:
                acc_ref[h] = acc_ref[h] * jnp.exp(m_ref[h] - m_new[h])
                m_ref[h] = m_new[h]
                p_hi, p_lo = _split2(p[h])
                ps_ref[h] = jnp.concatenate([p_hi, p_lo], axis=1)

        tile(i, True)
        j_left = _fox_tiles_left(i, pair, qn, kn_ref, fb_ref, tile)
        flush(j_left + 1)
        acc = (acc_ref[0], acc_ref[1])
        o_ref[...] = jnp.where(m0, acc[0][:, :LANES] / acc[0][:, LANES:],
                               acc[1][:, :LANES] / acc[1][:, LANES:])
        lse_ref[...] = jnp.where(m0, m_ref[0] + jnp.log(acc[0][:, LANES:]),
                                 m_ref[1] + jnp.log(acc[1][:, LANES:]))

    nb = FX_W // LANES
    base = 3 * SB_W // LANES
    (yfx, lse), lands = _ride(dict(
        body=body, name=name, grid=(nb, s // T),
        in_specs=[pl.BlockSpec((T, LANES), lambda p, i: (i, base + p)),
                  pl.BlockSpec((s, LANES), lambda p, i: (0, base + nb + p)),
                  pl.BlockSpec((s, LANES), lambda p, i: (0, base + 2 * nb + p)),
                  pl.BlockSpec((T, LANES), lambda p, i: (i, p)),
                  pl.BlockSpec((None, 2, s), lambda p, i: (p, 0, 0)),
                  pl.BlockSpec(memory_space=pltpu.SMEM)],
        out_specs=[pl.BlockSpec((T, LANES), lambda p, i: (i, p)),
                   pl.BlockSpec((T, LANES), lambda p, i: (i, p))],
        out_shape=[jax.ShapeDtypeStruct((s, FX_W), F32), jax.ShapeDtypeStruct((s, FX_W), F32)],
        scratch_shapes=[pltpu.VMEM((2, T, 2 * LANES), F32), pltpu.VMEM((2, T, 1), F32),
                        pltpu.VMEM((2, T, 2 * T), BF16), pltpu.SMEM((2,), F32)],
        compiler_params=_ARB2, operands=[qkv, qkv, qkv, fqb, frow, fbounds]), rider)
    return yfx, lse, lands


def _memkv_fwd(mem, mnw, wkv, name):
    n = mem.shape[0]

    def body(mem_ref, mnw_ref, w_ref, kv_ref):
        mv = mem_ref[...]
        r = lax.rsqrt(jnp.mean(mv * mv, axis=-1, keepdims=True) + EPS)
        hm = (mv * r * mnw_ref[...]).astype(BF16)
        kv_ref[...] = _dot(hm, w_ref[...]).astype(BF16)

    return pl.pallas_call(
        body, name=name, grid=(1,),
        in_specs=[pl.BlockSpec((n, D_MODEL), lambda i: (0, 0)),
                  pl.BlockSpec((1, D_MODEL), lambda i: (0, 0)),
                  pl.BlockSpec((D_MODEL, 2 * MEM_W), lambda i: (0, 0))],
        out_specs=pl.BlockSpec((n, 2 * MEM_W), lambda i: (0, 0)),
        out_shape=jax.ShapeDtypeStruct((n, 2 * MEM_W), BF16),
        compiler_params=_ARB1,
    )(mem, mnw, wkv)


def _mem_fwd(qkv, kv, name):
    s = qkv.shape[0]
    n = kv.shape[0]

    def body(q_ref, k_ref, v_ref, o_ref, lse_ref):
        m0 = _pair_masks()
        qh = _split_pair(q_ref[...] * jnp.asarray(Q_SCALE, BF16), m0)
        k = k_ref[...]
        v = v_ref[...]
        outs, lses = [], []
        for h in range(2):
            sc = _dot_nt(qh[h], k)
            mx = jnp.max(sc, axis=1, keepdims=True)
            p = jnp.exp(sc - mx)
            l = jnp.sum(p, axis=1, keepdims=True)
            outs.append(_dot(p.astype(BF16), v) / l)
            lses.append(mx + jnp.log(l))
        o_ref[...] = jnp.where(m0, outs[0], outs[1])
        lse_ref[...] = jnp.where(m0, lses[0], lses[1])

    nb = MEM_W // LANES
    base = (3 * SB_W + 3 * FX_W) // LANES
    return pl.pallas_call(
        body, name=name, grid=(nb, s // T),
        in_specs=[pl.BlockSpec((T, LANES), lambda p, i: (i, base + p)),
                  pl.BlockSpec((n, LANES), lambda p, i: (0, p)),
                  pl.BlockSpec((n, LANES), lambda p, i: (0, nb + p))],
        out_specs=[pl.BlockSpec((T, LANES), lambda p, i: (i, p)),
                   pl.BlockSpec((T, LANES), lambda p, i: (i, p))],
        out_shape=[jax.ShapeDtypeStruct((s, MEM_W), F32), jax.ShapeDtypeStruct((s, MEM_W), F32)],
        compiler_params=_ARB2,
    )(qkv, kv, kv)


def _mix_chunk(c, ysb_ref, yfx_ref, ym_ref):
    if c < SB_W // LANES:
        return ysb_ref[:, c * LANES:(c + 1) * LANES]
    c -= SB_W // LANES
    if c < FX_W // LANES:
        return yfx_ref[:, c * LANES:(c + 1) * LANES]
    c -= FX_W // LANES
    return ym_ref[:, c * LANES:(c + 1) * LANES]


def _outproj_fwd(ysb, yfx, ym, gf, onw, wout, x, name):
    s = x.shape[0]

    def body(ysb_ref, yfx_ref, ym_ref, g_ref, onw_ref, w_ref, x_ref, o_ref, yg_ref):
        bd = _head_block_diag()
        for c in range(MIX_W // LANES):
            sl = slice(c * LANES, (c + 1) * LANES)
            u = _mix_chunk(c, ysb_ref, yfx_ref, ym_ref)
            r = lax.rsqrt(_head_mean(u * u, bd) + EPS)
            g = g_ref[:, sl]
            yg_ref[:, sl] = (u * r * onw_ref[:, sl] * (g * _sigmoid(g))).astype(BF16)
        o_ref[...] = x_ref[...] + _dot(yg_ref[...], w_ref[...])

    return pl.pallas_call(
        body, name=name, grid=(s // TM,),
        in_specs=[pl.BlockSpec((TM, SB_W), lambda i: (i, 0)),
                  pl.BlockSpec((TM, FX_W), lambda i: (i, 0)),
                  pl.BlockSpec((TM, MEM_W), lambda i: (i, 0)),
                  pl.BlockSpec((TM, MIX_W), lambda i: (i, 0)),
                  pl.BlockSpec((1, MIX_W), lambda i: (0, 0)),
                  pl.BlockSpec((MIX_W, D_MODEL), lambda i: (0, 0)),
                  pl.BlockSpec((TM, D_MODEL), lambda i: (i, 0))],
        out_specs=pl.BlockSpec((TM, D_MODEL), lambda i: (i, 0)),
        out_shape=jax.ShapeDtypeStruct((s, D_MODEL), F32),
        scratch_shapes=[pltpu.VMEM((TM, MIX_W), BF16)],
        compiler_params=_ARB1,
    )(ysb, yfx, ym, gf, onw, wout, x)


def _final_fwd_bwd(x, fnw, target, name):
    s = x.shape[0]

    def body(x_ref, w_ref, t_ref, dx_ref, loss_ref, dw_ref):
        @pl.when(pl.program_id(0) == 0)
        def _():
            loss_ref[...] = jnp.zeros_like(loss_ref)
            dw_ref[...] = jnp.zeros_like(dw_ref)

        xv = x_ref[...]
        w = w_ref[...]
        r = lax.rsqrt(jnp.mean(xv * xv, axis=-1, keepdims=True) + EPS)
        xh = xv * r
        err = xh * w - t_ref[...]
        part = jnp.sum(jnp.sum(err * err, axis=1, keepdims=True), axis=0, keepdims=True)
        loss_ref[...] += part * (0.5 / D_MODEL)
        dy = err * (1.0 / D_MODEL)
        dw_ref[...] += jnp.sum(dy * xh, axis=0, keepdims=True)
        dxh = dy * w
        dx_ref[...] = r * (dxh - xh * jnp.mean(dxh * xh, axis=-1, keepdims=True))

    return pl.pallas_call(
        body, name=name, grid=(s // TM,),
        in_specs=[pl.BlockSpec((TM, D_MODEL), lambda i: (i, 0)),
                  pl.BlockSpec((1, D_MODEL), lambda i: (0, 0)),
                  pl.BlockSpec((TM, D_MODEL), lambda i: (i, 0))],
        out_specs=[pl.BlockSpec((TM, D_MODEL), lambda i: (i, 0)),
                   pl.BlockSpec((1, LANES), lambda i: (0, 0)),
                   pl.BlockSpec((1, D_MODEL), lambda i: (0, 0))],
        out_shape=[jax.ShapeDtypeStruct((s, D_MODEL), F32), jax.ShapeDtypeStruct((1, LANES), F32),
                   jax.ShapeDtypeStruct((1, D_MODEL), F32)],
        compiler_params=_ARB1,
    )(x, fnw, target)


def _outproj_bwd(dxo, wout, ysb, yfx, ym, gf, onw, name):
    s = dxo.shape[0]

    def body(dx_ref, w_ref, ysb_ref, yfx_ref, ym_ref, g_ref, onw_ref,
             dysb_ref, dyfx_ref, dym_ref, dg_ref, dw_ref, donw_ref, yg_ref):
        @pl.when(pl.program_id(0) == 0)
        def _():
            dw_ref[...] = jnp.zeros_like(dw_ref)
            donw_ref[...] = jnp.zeros_like(donw_ref)

        dxb = dx_ref[...].astype(BF16)
        dyg = _dot_nt(dxb, w_ref[...])
        bd = _head_block_diag()
        for c in range(MIX_W // LANES):
            sl = slice(c * LANES, (c + 1) * LANES)
            u = _mix_chunk(c, ysb_ref, yfx_ref, ym_ref)
            r = lax.rsqrt(_head_mean(u * u, bd) + EPS)
            yn = u * r
            g = g_ref[:, sl]
            sg = _sigmoid(g)
            sil = g * sg
            onw = onw_ref[:, sl]
            e = dyg[:, sl]
            yg_ref[:, sl] = (yn * onw * sil).astype(BF16)
            donw_ref[:, sl] += jnp.sum(e * yn * sil, axis=0, keepdims=True)
            dg_ref[:, sl] = (e * yn * onw * (sg * (1.0 + g * (1.0 - sg)))).astype(BF16)
            dyn = e * onw * sil
            du = (r * (dyn - yn * _head_mean(dyn * yn, bd))).astype(BF16)
            if c < 4:
                dysb_ref[:, c * LANES:(c + 1) * LANES] = du
            elif c < 8:
                dyfx_ref[:, (c - 4) * LANES:(c - 3) * LANES] = du
            else:
                dym_ref[:, (c - 8) * LANES:(c - 7) * LANES] = du
        dw_ref[...] += _dot_tn(yg_ref[...], dxb)

    return pl.pallas_call(
        body, name=name, grid=(s // TM,),
        in_specs=[pl.BlockSpec((TM, D_MODEL), lambda i: (i, 0)),
                  pl.BlockSpec((MIX_W, D_MODEL), lambda i: (0, 0)),
                  pl.BlockSpec((TM, SB_W), lambda i: (i, 0)),
                  pl.BlockSpec((TM, FX_W), lambda i: (i, 0)),
                  pl.BlockSpec((TM, MEM_W), lambda i: (i, 0)),
                  pl.BlockSpec((TM, MIX_W), lambda i: (i, 0)),
                  pl.BlockSpec((1, MIX_W), lambda i: (0, 0))],
        out_specs=[pl.BlockSpec((TM, SB_W), lambda i: (i, 0)),
                   pl.BlockSpec((TM, FX_W), lambda i: (i, 0)),
                   pl.BlockSpec((TM, MEM_W), lambda i: (i, 0)),
                   pl.BlockSpec((TM, MIX_W), lambda i: (i, 0)),
                   pl.BlockSpec((MIX_W, D_MODEL), lambda i: (0, 0)),
                   pl.BlockSpec((1, MIX_W), lambda i: (0, 0))],
        out_shape=[jax.ShapeDtypeStruct((s, SB_W), BF16), jax.ShapeDtypeStruct((s, FX_W), BF16),
                   jax.ShapeDtypeStruct((s, MEM_W), BF16), jax.ShapeDtypeStruct((s, MIX_W), BF16),
                   jax.ShapeDtypeStruct((MIX_W, D_MODEL), F32), jax.ShapeDtypeStruct((1, MIX_W), F32)],
        scratch_shapes=[pltpu.VMEM((TM, MIX_W), BF16)],
        compiler_params=_ARB1,
    )(dxo, wout, ysb, yfx, ym, gf, onw)


def _row_dots(do_ref, o_ref, m0):
    prod = do_ref[...].astype(F32) * o_ref[...]
    zero = jnp.zeros_like(prod)
    return (jnp.sum(jnp.where(m0, prod, zero), axis=1, keepdims=True),
            jnp.sum(jnp.where(m0, zero, prod), axis=1, keepdims=True))


def _sb_bwd(qkv, o, do, name, rider=None):
    s = qkv.shape[0]
    nq = s // T

    def body(q_ref, k_ref, v_ref, o_ref, do_ref, dq_ref, dk_ref, dv_ref,
             dqa_ref, dka_ref, dva_ref, rl_ref, rg_ref, dzs_ref, abs_ref):
        i = pl.program_id(1)

        @pl.when(i == 0)
        def _():
            dka_ref[...] = jnp.zeros_like(dka_ref)
            dva_ref[...] = jnp.zeros_like(dva_ref)

        m0 = _pair_masks()
        qh = _split_pair(q_ref[...] * jnp.asarray(Q_SCALE, BF16), m0)
        doh = _split_pair(do_ref[...], m0)
        dsum = _row_dots(do_ref, o_ref, m0)
        strict = _iota2((T, T), 0) > _iota2((T, T), 1)
        u2 = _stack2(jnp.where(strict, 1.0, 0.0).astype(BF16))
        dqa_ref[...] = jnp.zeros_like(dqa_ref)
        rl_ref[...] = jnp.zeros_like(rl_ref)
        rg_ref[...] = jnp.zeros_like(rg_ref)

        hs = range(2)

        def flush(j):
            off = pl.multiple_of(j * T, T)
            k = k_ref[pl.ds(off, T), :]
            for h in hs:
                dqa_ref[h] += _dot(dzs_ref[h], k)
            dka_ref[pl.ds(off, T), :] += _dot_tn(dzs_ref[0], qh[0]) + _dot_tn(dzs_ref[1], qh[1])
            dva_ref[pl.ds(off, T), :] += _dot_tn(abs_ref[0], doh[0]) + _dot_tn(abs_ref[1], doh[1])

        def tile(j, diag):
            off = pl.multiple_of(j * T, T)
            k = k_ref[pl.ds(off, T), :]
            v = v_ref[pl.ds(off, T), :]
            z = [_dot_nt(qh[h], k) for h in hs]
            da = [_dot_nt(doh[h], v) for h in hs]
            if not diag:
                flush(j + 1)
            la = [jnp.minimum(z[h], 0.0) - jnp.log(1.0 + jnp.exp(-jnp.abs(z[h]))) for h in hs]
            lf = [la[h] - z[h] for h in hs]
            if diag:
                lf = [jnp.where(strict, lf[h], 0.0) for h in hs]
            cin = [_cum2(lf[h], u2) for h in hs]
            a = [jnp.exp(la[h] + cin[h] + rl_ref[h]) for h in hs]
            if diag:
                a = [jnp.where(strict, a[h], 0.0) for h in hs]
            ab = [a[h].astype(BF16) for h in hs]
            g = [ab[h].astype(F32) * da[h] for h in hs]
            gin = [_cum2(g[h], u2) for h in hs]
            dz = [g[h] - jnp.exp(la[h]) * ((dsum[h] - rg_ref[h]) - gin[h]) for h in hs]
            if diag:
                dz = [jnp.where(strict, dz[h], 0.0) for h in hs]
            for h in hs:
                rl_ref[h] += cin[h][:, 0:1] + lf[h][:, 0:1]
                rg_ref[h] += gin[h][:, 0:1] + g[h][:, 0:1]
                dzs_ref[h] = dz[h].astype(BF16)
                abs_ref[h] = ab[h]

        tile(i, True)

        def more(state):
            j, top = state
            return jnp.logical_and(j >= 0, top > UNDERFLOW)

        def step(state):
            j, _ = state
            tile(j, False)
            return j - 1, _running_top(rl_ref)

        j_left, _ = lax.while_loop(more, step, (i - 1, _running_top(rl_ref)))
        flush(j_left + 1)
        dq_ref[...] = (jnp.where(m0, dqa_ref[0], dqa_ref[1]) * Q_SCALE).astype(BF16)

        @pl.when(i == nq - 1)
        def _():
            dk_ref[...] = dka_ref[...].astype(BF16)
            dv_ref[...] = dva_ref[...].astype(BF16)

    nb = SB_W // LANES
    (dq, dk, dv), lands = _ride(dict(
        body=body, name=name, grid=(nb, nq),
        in_specs=[pl.BlockSpec((T, LANES), lambda p, i: (i, p)),
                  pl.BlockSpec((s, LANES), lambda p, i: (0, nb + p)),
                  pl.BlockSpec((s, LANES), lambda p, i: (0, 2 * nb + p)),
                  pl.BlockSpec((T, LANES), lambda p, i: (i, p)),
                  pl.BlockSpec((T, LANES), lambda p, i: (i, p))],
        out_specs=[pl.BlockSpec((T, LANES), lambda p, i: (i, p)),
                   pl.BlockSpec((s, LANES), lambda p, i: (0, p)),
                   pl.BlockSpec((s, LANES), lambda p, i: (0, p))],
        out_shape=[jax.ShapeDtypeStruct((s, SB_W), BF16)] * 3,
        scratch_shapes=[pltpu.VMEM((2, T, LANES), F32), pltpu.VMEM((s, LANES), F32),
                        pltpu.VMEM((s, LANES), F32), pltpu.VMEM((2, T, 1), F32),
                        pltpu.VMEM((2, T, 1), F32), pltpu.VMEM((2, T, T), BF16),
                        pltpu.VMEM((2, T, T), BF16)],
        compiler_params=_ARB2, operands=[qkv, qkv, qkv, o, do]), rider)
    return dq, dk, dv, lands


def _fox_bwd(qkv, fqb, frow, fbounds, o, lse, do, name, rider=None):
    s = qkv.shape[0]
    nq = s // T

    def body(q_ref, k_ref, v_ref, fq_ref, fr_ref, fb_ref, o_ref, lse_ref, do_ref,
             dq_ref, dk_ref, dv_ref, df_ref, dqa_ref, dka_ref, dva_ref, dfa_ref, dls_ref, pbs_ref,
             kn_ref):
        i = pl.program_id(1)

        @pl.when(i == 0)
        def _():
            dka_ref[...] = jnp.zeros_like(dka_ref)
            dva_ref[...] = jnp.zeros_like(dva_ref)
            dfa_ref[...] = jnp.zeros_like(dfa_ref)

        m0 = _pair_masks()

        @pl.when(i == 0)
        def _():
            kn_ref[0], kn_ref[1] = _head_norm_tops(k_ref[...], m0)

        q = q_ref[...] * jnp.asarray(Q_SCALE, BF16)
        qn = _head_norm_tops(q, m0)
        qh = _split_pair(q, m0)
        doh = _split_pair(do_ref[...], m0)
        dsum = _row_dots(do_ref, o_ref, m0)
        fq = fq_ref[...]
        fqh = (fq[:, 0:1], fq[:, HEAD_DIM:HEAD_DIM + 1])
        lse = lse_ref[...]
        lseh = (lse[:, 0:1], lse[:, HEAD_DIM:HEAD_DIM + 1])
        causal = _iota2((T, T), 0) >= _iota2((T, T), 1)
        dqa_ref[...] = jnp.zeros_like(dqa_ref)

        hs = range(2)

        def flush(j):
            off = pl.multiple_of(j * T, T)
            k = k_ref[pl.ds(off, T), :]
            for h in hs:
                dqa_ref[h] += _dot(dls_ref[h], k)
            dka_ref[pl.ds(off, T), :] += _dot_tn(dls_ref[0], qh[0]) + _dot_tn(dls_ref[1], qh[1])
            dva_ref[pl.ds(off, T), :] += _dot_tn(pbs_ref[0], doh[0]) + _dot_tn(pbs_ref[1], doh[1])

        def tile(j, diag):
            off = pl.multiple_of(j * T, T)
            k = k_ref[pl.ds(off, T), :]
            v = v_ref[pl.ds(off, T), :]
            sc = [_dot_nt(qh[h], k) + fqh[h] - fr_ref[h:h + 1, pl.ds(off, T)] for h in hs]
            dp = [_dot_nt(doh[h], v) for h in hs]
            if not diag:
                flush(j + 1)
            p = [jnp.exp(sc[h] - lseh[h]) for h in hs]
            if diag:
                p = [jnp.where(causal, p[h], 0.0) for h in hs]
            dl = [p[h] * (dp[h] - dsum[h]) for h in hs]
            for h in hs:
                dls_ref[h] = dl[h].astype(BF16)
                pbs_ref[h] = p[h].astype(BF16)
                dfa_ref[h:h + 1, pl.ds(off, T)] -= jnp.sum(dl[h], axis=0, keepdims=True)

        tile(i, True)
        j_left = _fox_tiles_left(i, pl.program_id(0), qn, kn_ref, fb_ref, tile)
        flush(j_left + 1)
        dq_ref[...] = (jnp.where(m0, dqa_ref[0], dqa_ref[1]) * Q_SCALE).astype(BF16)

        @pl.when(i == nq - 1)
        def _():
            dk_ref[...] = dka_ref[...].astype(BF16)
            dv_ref[...] = dva_ref[...].astype(BF16)
            df_ref[...] = dfa_ref[...]

    nb = FX_W // LANES
    base = 3 * SB_W // LANES
    (dq, dk, dv, df), lands = _ride(dict(
        body=body, name=name, grid=(nb, nq),
        in_specs=[pl.BlockSpec((T, LANES), lambda p, i: (i, base + p)),
                  pl.BlockSpec((s, LANES), lambda p, i: (0, base + nb + p)),
                  pl.BlockSpec((s, LANES), lambda p, i: (0, base + 2 * nb + p)),
                  pl.BlockSpec((T, LANES), lambda p, i: (i, p)),
                  pl.BlockSpec((None, 2, s), lambda p, i: (p, 0, 0)),
                  pl.BlockSpec(memory_space=pltpu.SMEM),
                  pl.BlockSpec((T, LANES), lambda p, i: (i, p)),
                  pl.BlockSpec((T, LANES), lambda p, i: (i, p)),
                  pl.BlockSpec((T, LANES), lambda p, i: (i, p))],
        out_specs=[pl.BlockSpec((T, LANES), lambda p, i: (i, p)),
                   pl.BlockSpec((s, LANES), lambda p, i: (0, p)),
                   pl.BlockSpec((s, LANES), lambda p, i: (0, p)),
                   pl.BlockSpec((None, 2, s), lambda p, i: (p, 0, 0))],
        out_shape=[jax.ShapeDtypeStruct((s, FX_W), BF16)] * 3
        + [jax.ShapeDtypeStruct((nb, 2, s), F32)],
        scratch_shapes=[pltpu.VMEM((2, T, LANES), F32), pltpu.VMEM((s, LANES), F32),
                        pltpu.VMEM((s, LANES), F32), pltpu.VMEM((2, s), F32),
                        pltpu.VMEM((2, T, T), BF16), pltpu.VMEM((2, T, T), BF16),
                        pltpu.SMEM((2,), F32)],
        compiler_params=_ARB2, operands=[qkv, qkv, qkv, fqb, frow, fbounds, o, lse, do]), rider)
    return dq, dk, dv, df, lands


def _fox_prep_bwd(dfcol, gf, bpad, name):
    s = gf.shape[0]

    def body(df_ref, fl_ref, b_ref, dfl_ref, db_ref):
        tri = jnp.where(_iota2((T, T), 0) <= _iota2((T, T), 1), 1.0, 0.0).astype(BF16)
        carry = jnp.zeros((1, LANES), F32)
        db = jnp.zeros((1, LANES), F32)
        for blk in reversed(range(s // T)):
            rows = slice(blk * T, (blk + 1) * T)
            c = _tri3(tri, df_ref[rows, :]) + carry
            carry = c[0:1, :]
            dfl = c / (1.0 + jnp.exp(fl_ref[rows, :] + b_ref[...]))
            dfl_ref[rows, :] = dfl.astype(BF16)
            db = db + jnp.sum(dfl, axis=0, keepdims=True)
        db_ref[...] = db

    return pl.pallas_call(
        body, name=name, grid=(1,),
        in_specs=[pl.BlockSpec((s, LANES), lambda i: (0, 0)),
                  pl.BlockSpec((s, LANES), lambda i: (0, MIX_W // LANES)),
                  pl.BlockSpec((1, LANES), lambda i: (0, 0))],
        out_specs=[pl.BlockSpec((s, LANES), lambda i: (0, 0)),
                   pl.BlockSpec((1, LANES), lambda i: (0, 0))],
        out_shape=[jax.ShapeDtypeStruct((s, LANES), BF16), jax.ShapeDtypeStruct((1, LANES), F32)],
        compiler_params=_ARB1,
    )(dfcol, gf, bpad)


def _mem_bwd(qkv, kv, o, lse, do, name):
    s = qkv.shape[0]
    n = kv.shape[0]
    nq = s // T

    def body(q_ref, k_ref, v_ref, o_ref, lse_ref, do_ref, dq_ref, dk_ref, dv_ref):
        @pl.when(pl.program_id(1) == 0)
        def _():
            dk_ref[...] = jnp.zeros_like(dk_ref)
            dv_ref[...] = jnp.zeros_like(dv_ref)

        m0 = _pair_masks()
        qh = _split_pair(q_ref[...] * jnp.asarray(Q_SCALE, BF16), m0)
        doh = _split_pair(do_ref[...], m0)
        dsum = _row_dots(do_ref, o_ref, m0)
        lse = lse_ref[...]
        lseh = (lse[:, 0:1], lse[:, HEAD_DIM:HEAD_DIM + 1])
        k = k_ref[...]
        v = v_ref[...]
        dqs = []
        for h in range(2):
            p = jnp.exp(_dot_nt(qh[h], k) - lseh[h])
            dl = p * (_dot_nt(doh[h], v) - dsum[h])
            dlb = dl.astype(BF16)
            dqs.append(_dot(dlb, k))
            dk_ref[...] += _dot_tn(dlb, qh[h])
            dv_ref[...] += _dot_tn(p.astype(BF16), doh[h])
        dq_ref[...] = (jnp.where(m0, dqs[0], dqs[1]) * Q_SCALE).astype(BF16)

    nb = MEM_W // LANES
    base = (3 * SB_W + 3 * FX_W) // LANES
    return pl.pallas_call(
        body, name=name, grid=(nb, nq),
        in_specs=[pl.BlockSpec((T, LANES), lambda p, i: (i, base + p)),
                  pl.BlockSpec((n, LANES), lambda p, i: (0, p)),
                  pl.BlockSpec((n, LANES), lambda p, i: (0, nb + p)),
                  pl.BlockSpec((T, LANES), lambda p, i: (i, p)),
                  pl.BlockSpec((T, LANES), lambda p, i: (i, p)),
                  pl.BlockSpec((T, LANES), lambda p, i: (i, p))],
        out_specs=[pl.BlockSpec((T, LANES), lambda p, i: (i, p)),
                   pl.BlockSpec((n, LANES), lambda p, i: (0, p)),
                   pl.BlockSpec((n, LANES), lambda p, i: (0, p))],
        out_shape=[jax.ShapeDtypeStruct((s, MEM_W), BF16), jax.ShapeDtypeStruct((n, MEM_W), F32),
                   jax.ShapeDtypeStruct((n, MEM_W), F32)],
        compiler_params=_ARB2,
    )(qkv, kv, kv, o, lse, do)


def _memkv_bwd(mem, mnw, wkv, dk, dv, name):
    n = mem.shape[0]

    def body(mem_ref, mnw_ref, w_ref, dk_ref, dv_ref, dw_ref, dmnw_ref):
        mv = mem_ref[...]
        r = lax.rsqrt(jnp.mean(mv * mv, axis=-1, keepdims=True) + EPS)
        mh = mv * r
        hm = (mh * mnw_ref[...]).astype(BF16)
        dkv = jnp.concatenate([dk_ref[...], dv_ref[...]], axis=1).astype(BF16)
        dw_ref[...] = _dot_tn(hm, dkv)
        dhm = _dot_nt(dkv, w_ref[...])
        dmnw_ref[...] = jnp.sum(dhm * mh, axis=0, keepdims=True)

    return pl.pallas_call(
        body, name=name, grid=(1,),
        in_specs=[pl.BlockSpec((n, D_MODEL), lambda i: (0, 0)),
                  pl.BlockSpec((1, D_MODEL), lambda i: (0, 0)),
                  pl.BlockSpec((D_MODEL, 2 * MEM_W), lambda i: (0, 0)),
                  pl.BlockSpec((n, MEM_W), lambda i: (0, 0)),
                  pl.BlockSpec((n, MEM_W), lambda i: (0, 0))],
        out_specs=[pl.BlockSpec((D_MODEL, 2 * MEM_W), lambda i: (0, 0)),
                   pl.BlockSpec((1, D_MODEL), lambda i: (0, 0))],
        out_shape=[jax.ShapeDtypeStruct((D_MODEL, 2 * MEM_W), F32),
                   jax.ShapeDtypeStruct((1, D_MODEL), F32)],
        compiler_params=_ARB1,
    )(mem, mnw, wkv, dk, dv)


def _inproj_bwd_dx(pieces, w_r, x, nw, dxo, name):
    s = x.shape[0]
    n = len(pieces)
    widths = [p.shape[1] for p in pieces]

    def body(*refs):
        piece_refs = refs[:n]
        w_ref, x_ref, nw_ref, dxo_ref, dx_ref, h_ref, dnw_ref, dp_ref = refs[n:]

        @pl.when(pl.program_id(0) == 0)
        def _():
            dnw_ref[...] = jnp.zeros_like(dnw_ref)

        col = 0
        for r, wd in zip(piece_refs, widths):
            dp_ref[:, col:col + wd] = r[...]
            col += wd
        dp_ref[:, col:] = jnp.zeros((TM, WR_W - col), BF16)
        dh = _dot(dp_ref[...], w_ref[...])
        xv = x_ref[...]
        nw = nw_ref[...]
        r = lax.rsqrt(jnp.mean(xv * xv, axis=-1, keepdims=True) + EPS)
        xh = xv * r
        h_ref[...] = (xh * nw).astype(BF16)
        dnw_ref[...] += jnp.sum(dh * xh, axis=0, keepdims=True)
        dxh = dh * nw
        dx_ref[...] = r * (dxh - xh * jnp.mean(dxh * xh, axis=-1, keepdims=True)) + dxo_ref[...]

    return pl.pallas_call(
        body, name=name, grid=(s // TM,),
        in_specs=[pl.BlockSpec((TM, wd), lambda i: (i, 0)) for wd in widths]
        + [pl.BlockSpec((WR_W, D_MODEL), lambda i: (0, 0)),
           pl.BlockSpec((TM, D_MODEL), lambda i: (i, 0)),
           pl.BlockSpec((1, D_MODEL), lambda i: (0, 0)),
           pl.BlockSpec((TM, D_MODEL), lambda i: (i, 0))],
        out_specs=[pl.BlockSpec((TM, D_MODEL), lambda i: (i, 0)),
                   pl.BlockSpec((TM, D_MODEL), lambda i: (i, 0)),
                   pl.BlockSpec((1, D_MODEL), lambda i: (0, 0)),
                   pl.BlockSpec((TM, WR_W), lambda i: (i, 0))],
        out_shape=[jax.ShapeDtypeStruct((s, D_MODEL), F32), jax.ShapeDtypeStruct((s, D_MODEL), BF16),
                   jax.ShapeDtypeStruct((1, D_MODEL), F32), jax.ShapeDtypeStruct((s, WR_W), BF16)],
        compiler_params=_ARB1,
    )(*pieces, w_r, x, nw, dxo)


def _inproj_bwd_dw(h, dproj, name):
    s = dproj.shape[0]
    tn = 256

    def body(h_ref, dp_ref, dw_ref):
        dw_ref[...] = _dot_tn(dp_ref[...], h_ref[...])

    return pl.pallas_call(
        body, name=name, grid=(WR_W // tn,),
        in_specs=[pl.BlockSpec((s, D_MODEL), lambda j: (0, 0)),
                  pl.BlockSpec((s, tn), lambda j: (0, j))],
        out_specs=pl.BlockSpec((tn, D_MODEL), lambda j: (j, 0)),
        out_shape=jax.ShapeDtypeStruct((WR_W, D_MODEL), F32),
        compiler_params=_ARB1,
    )(h, dproj)


def _rearrange_w_in(wt):
    pad = jnp.zeros((FL_PAD - FOX_HEADS,) + wt.shape[1:], wt.dtype)
    return jnp.concatenate([wt[:3072], wt[3080:3336], wt[3336:IN_W], wt[3072:3080], pad], axis=0)


def _restore_w_in(g):
    gate0 = QKV_W
    fl0 = QKV_W + MIX_W
    return jnp.concatenate(
        [g[:3072], g[fl0:fl0 + FOX_HEADS], g[3072:QKV_W], g[gate0:fl0]], axis=0)


def _pad_lanes(v, width=LANES):
    return jnp.pad(v, (0, width - v.shape[0])).reshape(1, width)


def _layer_fwd(xs, mem, nw, w_r, b_forget, mnw, late, onw, l, travel=None):
    s = xs.shape[0]
    bpad = _pad_lanes(b_forget)
    qkv, gf = _inproj_fwd(xs, nw, w_r, f"inproj_fwd_{l}")
    f = _fox_prep_fwd(gf, bpad, f"fox_prep_fwd_{l}")
    f8 = f[:, :FOX_HEADS]
    fqb = jnp.repeat(f8, HEAD_DIM, axis=1)
    frow = f8.T.reshape(FOX_HEADS // 2, 2, s)
    fbounds = jnp.concatenate([f8[0::T].T, f8[T - 1::T].T], axis=0)
    travel = _Travel(travel)
    ysb, _ = travel.ride(0, _sb_fwd, qkv, f"sb_fwd_{l}")
    yfx, lse_fx, _ = travel.ride(1, _fox_fwd, qkv, fqb, frow, fbounds, f"fox_fwd_{l}")
    wkv, wout = late(travel.lands)
    kv = _memkv_fwd(mem, mnw, wkv, f"memkv_fwd_{l}")
    ym, lse_m = _mem_fwd(qkv, kv, f"mem_fwd_{l}")
    xn = _outproj_fwd(ysb, yfx, ym, gf, onw, wout, xs, f"outproj_fwd_{l}")
    saved = (xs, nw, mnw, onw, bpad, qkv, gf, fqb, frow, fbounds, ysb, yfx, lse_fx, kv, ym, lse_m)
    return xn, saved, travel.lands, (wkv, wout)


class _Travel:
    def __init__(self, plan):
        self.plan = plan
        self.lands = None if plan is None else _new_lands(plan[0], plan[1])

    def ride(self, n, fn, *args):
        if self.plan is None or self.plan[2][n] is None:
            return fn(*args)
        srcs, scatter, legs = self.plan
        idx, rows = legs[n]
        out = fn(*args, rider=_Rider([srcs[a] for a in idx], [self.lands[a] for a in idx],
                                     scatter, rows))
        for a, land in zip(idx, out[-1]):
            self.lands[a] = land
        return out


def _layer_bwd(dx, saved, mem, w_r, wkv, wout, l, travel=None):
    xs, nw, mnw, onw, bpad, qkv, gf, fqb, frow, fbounds, ysb, yfx, lse_fx, kv, ym, lse_m = saved
    s = xs.shape[0]
    dysb, dyfx, dym, dgate, dwout, donw = _outproj_bwd(
        dx, wout, ysb, yfx, ym, gf, onw, f"outproj_bwd_{l}")
    travel = _Travel(None if travel is None else travel(dwout))
    sdq, sdk, sdv, _ = travel.ride(0, _sb_bwd, qkv, ysb, dysb, f"sb_bwd_{l}")
    fdq, fdk, fdv, dfrow, _ = travel.ride(1, _fox_bwd, qkv, fqb, frow, fbounds, yfx, lse_fx, dyfx,
                                          f"fox_bwd_{l}")
    dfcol = jnp.pad(dfrow.reshape(FOX_HEADS, s).T, ((0, 0), (0, LANES - FOX_HEADS)))
    dfl, db = _fox_prep_bwd(dfcol, gf, bpad, f"fox_prep_bwd_{l}")
    dmq, dmk, dmv = _mem_bwd(qkv, kv, ym, lse_m, dym, f"mem_bwd_{l}")
    dwkv, dmnw = _memkv_bwd(mem, mnw, wkv, dmk, dmv, f"memkv_bwd_{l}")
    dx, ht, dnw, dproj = _inproj_bwd_dx([sdq, sdk, sdv, fdq, fdk, fdv, dmq, dgate, dfl],
                                        w_r, xs, nw, dx, f"inproj_bwd_dx_{l}")
    dwr = _inproj_bwd_dw(ht, dproj, f"inproj_bwd_dw_{l}")
    grads = dict(norm_w=dnw[0], w_r=dwr, b_forget=db[0, :FOX_HEADS], mem_norm_w=dmnw[0],
                 w_mem_kv=dwkv, out_norm_w=donw[0], w_out=dwout)
    return dx, grads, travel.lands


_ANY = pl.BlockSpec(memory_space=pl.ANY)


def _my_place():
    return lax.axis_index("x"), lax.axis_index("y"), lax.axis_index("c")


def _flip(v, bit):
    return 1 - v if bit else v


def _block_index(px, py, pc):
    return 4 * px + 2 * py + pc


def _all_gather_weights(shards, name):
    n = len(shards)

    def body(*refs):
        ins, outs = refs[:n], refs[n:2 * n]
        send_sems, recv_sems, local_sems = refs[2 * n:]
        x, y, c = _my_place()
        me = (x, y, c)
        sibling = (x, y, 1 - c)
        chips = [(1 - x, y), (x, 1 - y), (1 - x, 1 - y)]

        def copy(a, k, block, to, src=None):
            dst = outs[a].at[_block_index(*block)]
            return pltpu.make_async_remote_copy(
                src_ref=dst if src is None else src, dst_ref=dst,
                send_sem=send_sems.at[a, k], recv_sem=recv_sems.at[a, k],
                device_id=to, device_id_type=pl.DeviceIdType.MESH)

        mine = [pltpu.make_async_copy(ins[a], outs[a].at[_block_index(*me)], local_sems.at[a])
                for a in range(n)]
        for cp in mine:
            cp.start()
        first = []
        for a in range(n):
            first.append(copy(a, 0, me, sibling, src=ins[a]))
            first += [copy(a, 1 + j, me, (*chip, c), src=ins[a]) for j, chip in enumerate(chips)]
        for cp in first:
            cp.start()
        passed = []
        for j, chip in enumerate(chips):
            for a in range(n):
                copy(a, 1 + j, (*chip, c), me).wait_recv()
                fwd = copy(a, 4 + j, (*chip, c), sibling)
                fwd.start()
                passed.append(fwd)
        for a in range(n):
            copy(a, 0, sibling, me).wait_recv()
            for j, chip in enumerate(chips):
                copy(a, 4 + j, (*chip, 1 - c), me).wait_recv()
        for cp in first + passed:
            cp.wait_send()
        for cp in mine:
            cp.wait()

    return pl.pallas_call(
        body, name=name,
        in_specs=[_ANY] * n, out_specs=[_ANY] * n,
        out_shape=[jax.ShapeDtypeStruct((N_DEV,) + v.shape, v.dtype) for v in shards],
        scratch_shapes=[pltpu.SemaphoreType.DMA((n, 7)), pltpu.SemaphoreType.DMA((n, 7)),
                        pltpu.SemaphoreType.DMA((n,))],
    )(*shards)


def _exchange_blocks(blocked, name):
    n = len(blocked)

    def body(*refs):
        ins, outs = refs[:n], refs[n:2 * n]
        send_sems, recv_sems, local_sems = refs[2 * n:]
        x, y, c = _my_place()
        mine_idx = _block_index(x, y, c)
        local = [pltpu.make_async_copy(ins[a].at[mine_idx], outs[a].at[mine_idx], local_sems.at[a])
                 for a in range(n)]
        for cp in local:
            cp.start()
        sends, arrivals = [], []
        for r in range(1, N_DEV):
            peer = (_flip(x, r & 4), _flip(y, r & 2), _flip(c, r & 1))
            peer_idx = _block_index(*peer)
            for a in range(n):
                sems = dict(send_sem=send_sems.at[a, r - 1], recv_sem=recv_sems.at[a, r - 1],
                            device_id=peer, device_id_type=pl.DeviceIdType.MESH)
                sends.append(pltpu.make_async_remote_copy(
                    src_ref=ins[a].at[peer_idx], dst_ref=outs[a].at[mine_idx], **sems))
                arrivals.append(pltpu.make_async_remote_copy(
                    src_ref=ins[a].at[peer_idx], dst_ref=outs[a].at[peer_idx], **sems))
        for cp in sends:
            cp.start()
        for cp in arrivals:
            cp.wait_recv()
        for cp in sends:
            cp.wait_send()
        for cp in local:
            cp.wait()

    return pl.pallas_call(
        body, name=name,
        in_specs=[_ANY] * n, out_specs=[_ANY] * n,
        out_shape=[jax.ShapeDtypeStruct(v.shape, v.dtype) for v in blocked],
        scratch_shapes=[pltpu.SemaphoreType.DMA((n, 7)), pltpu.SemaphoreType.DMA((n, 7)),
                        pltpu.SemaphoreType.DMA((n,))],
    )(*blocked)


class _Rider(NamedTuple):
    srcs: list
    lands: list
    scatter: bool
    part: list


def _new_lands(srcs, scatter):
    return [lax.empty(v.shape if scatter else (N_DEV,) + v.shape, v.dtype) for v in srcs]


def _rider_copies(srcs, lands, send_sems, recv_sems, rider):
    x, y, c = _my_place()
    mine_idx = _block_index(x, y, c)

    def window(ref, a):
        if rider.part[a] is None:
            return ref
        dim, start, size = rider.part[a]
        return ref.at[(slice(None),) * dim + (pl.ds(start, size),)]

    sends, arrivals = [], []
    for r in range(1, N_DEV):
        peer = (_flip(x, r & 4), _flip(y, r & 2), _flip(c, r & 1))
        peer_idx = _block_index(*peer)
        for a in range(len(srcs)):
            src = window(srcs[a].at[peer_idx] if rider.scatter else srcs[a], a)
            k = 7 * a + r - 1
            sems = dict(send_sem=send_sems.at[k], recv_sem=recv_sems.at[k],
                        device_id=peer, device_id_type=pl.DeviceIdType.MESH)
            sends.append(pltpu.make_async_remote_copy(
                src_ref=src, dst_ref=window(lands[a].at[mine_idx], a), **sems))
            arrivals.append(pltpu.make_async_remote_copy(
                src_ref=src, dst_ref=window(lands[a].at[peer_idx], a), **sems))
    return sends, arrivals


def _ride(call, rider):
    call = dict(call)
    body, grid = call.pop("body"), call["grid"]
    operands = call.pop("operands")
    if rider is None:
        return list(pl.pallas_call(body, **call)(*operands)), None
    n_in, n_out = len(call["in_specs"]), len(call["out_specs"])
    n_scratch = len(call["scratch_shapes"])
    m = len(rider.srcs)

    def riding(*refs):
        main_in, srcs, lands = refs[:n_in], refs[n_in:n_in + m], refs[n_in + m:n_in + 2 * m]
        main_out = refs[n_in + 2 * m:n_in + 2 * m + n_out]
        rest = refs[n_in + 3 * m + n_out:]
        send_sems, recv_sems = rest[n_scratch:]
        at = [pl.program_id(d) for d in range(len(grid))]
        first = functools.reduce(jnp.logical_and, [p == 0 for p in at])
        last = functools.reduce(jnp.logical_and, [p == g - 1 for p, g in zip(at, grid)])
        sends, arrivals = _rider_copies(srcs, lands, send_sems, recv_sems, rider)

        @pl.when(first)
        def _():
            for cp in sends:
                cp.start()

        body(*main_in, *main_out, *rest[:n_scratch])

        @pl.when(last)
        def _():
            for cp in arrivals:
                cp.wait_recv()
            for cp in sends:
                cp.wait_send()

    call["in_specs"] = list(call["in_specs"]) + [_ANY] * (2 * m)
    call["out_specs"] = list(call["out_specs"]) + [_ANY] * m
    call["out_shape"] = list(call["out_shape"]) + [
        jax.ShapeDtypeStruct(v.shape, v.dtype) for v in rider.lands]
    call["scratch_shapes"] = list(call["scratch_shapes"]) + [
        pltpu.SemaphoreType.DMA((7 * m,)), pltpu.SemaphoreType.DMA((7 * m,))]
    call["input_output_aliases"] = {n_in + m + a: n_out + a for a in range(m)}
    outs = pl.pallas_call(riding, **call)(*operands, *rider.srcs, *rider.lands)
    return list(outs[:n_out]), list(outs[n_out:])


def _adamw_sum(parts, w, m, v, tile, name):
    depth, nrow, ncol = w.shape
    rows, cols = tile
    c1 = 1.0 / (1.0 - ADAM_B1 ** ADAM_STEP)
    c2 = 1.0 / (1.0 - ADAM_B2 ** ADAM_STEP)

    def body(*refs):
        p_refs = refs[:depth]
        w_ref, m_ref, v_ref, g_ref, d_ref, nm_ref, nv_ref = refs[depth:]
        layer = pl.program_id(0)
        for l in range(depth):
            @pl.when(layer == l)
            def _(p_ref=p_refs[l]):
                g = p_ref[0].astype(F32)
                for k in range(1, N_DEV):
                    g = g + p_ref[k].astype(F32)
                nm = ADAM_B1 * m_ref[...] + (1.0 - ADAM_B1) * g
                nv = ADAM_B2 * v_ref[...] + (1.0 - ADAM_B2) * (g * g)
                g_ref[...] = g
                nm_ref[...] = nm
                nv_ref[...] = nv
                d_ref[...] = -ADAM_LR * ((nm * c1) / (jnp.sqrt(nv * c2) + ADAM_EPS)
                                         + ADAM_WD * w_ref[...])

    def part_spec(l):
        return pl.BlockSpec((N_DEV, rows, cols), lambda q, i, j: (
            0, jnp.where(q == l, i, 0), jnp.where(q == l, j, 0)))

    blk = pl.BlockSpec((None, rows, cols), lambda q, i, j: (q, i, j))
    return pl.pallas_call(
        body, name=name, grid=(depth, nrow // rows, ncol // cols),
        in_specs=[part_spec(l) for l in range(depth)] + [blk, blk, blk],
        out_specs=[blk] * 4,
        out_shape=[jax.ShapeDtypeStruct(w.shape, F32)] * 4,
        compiler_params=pltpu.CompilerParams(
            dimension_semantics=("arbitrary", "arbitrary", "arbitrary")),
    )(*parts, w, m, v)


def _pack_small(norm_w, mem_norm_w, out_norm_w, final_norm_w, b_forget):
    onw = jnp.pad(out_norm_w.reshape(20, LANES), ((0, 4), (0, 0)))
    b = jnp.pad(b_forget, ((0, 6), (0, LANES - FOX_HEADS)))
    return jnp.concatenate([norm_w.reshape(16, LANES), mem_norm_w.reshape(16, LANES), onw,
                            final_norm_w.reshape(8, LANES), b], axis=0)


def _unpack_small(p):
    return (p[0:16].reshape(2, D_MODEL), p[16:32].reshape(2, D_MODEL), p[32:52].reshape(2, MIX_W),
            p[56:64].reshape(D_MODEL), p[64:66, :FOX_HEADS])


def kernel(x, mem, norm_w, w_in, b_forget, mem_norm_w, w_mem_kv, out_norm_w, w_out, final_norm_w, loss_target, m_norm_w, m_w_in, m_b_forget, m_mem_norm_w, m_w_mem_kv, m_out_norm_w, m_w_out, m_final_norm_w, v_norm_w, v_w_in, v_b_forget, v_mem_norm_w, v_w_mem_kv, v_out_norm_w, v_w_out, v_final_norm_w):
    kv_rows = w_mem_kv.shape[1]
    out_rows = w_out.shape[1]
    me = _block_index(*_my_place())

    def shards(l):
        return [w_in[l].T.astype(BF16), w_mem_kv[l].astype(BF16), w_out[l].astype(BF16)]

    def full_in(g_in):
        return _rearrange_w_in(g_in.reshape(IN_W, D_MODEL))

    def full_kv_out(g_kv, g_out):
        return g_kv.reshape(D_MODEL, 2 * MEM_W), g_out.reshape(MIX_W, D_MODEL)

    def in_blocks(g):
        return _restore_w_in(g).reshape(N_DEV, SHARD_W, D_MODEL).astype(BF16)

    def kv_blocks(g):
        return g.reshape(N_DEV, kv_rows, 2 * MEM_W).astype(BF16)

    def out_blocks(g):
        return g.reshape(N_DEV, out_rows, D_MODEL).astype(BF16)

    def with_own(land, own):
        return lax.dynamic_update_slice(land, own[None], (me,) + (0,) * own.ndim)

    def with_own_of(land, blocked):
        return lax.dynamic_update_slice(land, lax.dynamic_slice_in_dim(blocked, me, 1, axis=0),
                                        (me,) + (0,) * (land.ndim - 1))

    def row(v):
        return v.reshape(1, -1)

    def cols(first, size):
        return (1, first, size)

    half = D_MODEL // 2

    g_in0, g_kv0, g_out0 = _all_gather_weights(shards(0), "all_gather_l0")
    w_r0 = full_in(g_in0)
    s_in1, s_kv1, s_out1 = shards(1)
    x1, saved0, (l_in1,), (wkv0, wout0) = _layer_fwd(
        x[0], mem[0], row(norm_w[0]), w_r0, b_forget[0], row(mem_norm_w[0]),
        lambda lands: full_kv_out(g_kv0, g_out0), row(out_norm_w[0]), 0,
        travel=([s_in1], False, [([0], [cols(0, half)]), ([0], [cols(half, half)])]))
    w_r1 = full_in(with_own(l_in1, s_in1))
    x2, saved1, _, (wkv1, wout1) = _layer_fwd(
        x1, mem[0], row(norm_w[1]), w_r1, b_forget[1], row(mem_norm_w[1]),
        lambda lands: full_kv_out(with_own(lands[0], s_kv1), with_own(lands[1], s_out1)),
        row(out_norm_w[1]), 1, travel=([s_kv1, s_out1], False, [([0, 1], [None, None]), None]))

    dx2, loss, dfnw = _final_fwd_bwd(x2, row(final_norm_w), loss_target[0], "final_fwd_bwd")
    loss = lax.psum(loss[0, 0], ("x", "y", "c"))

    dx1, gr1, (l_out1,) = _layer_bwd(
        dx2, saved1, mem[0], w_r1, wkv1, wout1, 1,
        travel=lambda dwout: ([out_blocks(dwout)], True, [([0], [None]), None]))
    p_in1, p_kv1 = in_blocks(gr1["w_r"]), kv_blocks(gr1["w_mem_kv"])
    split = 5 * LANES
    grad_x, gr0, (l_in1, l_kv1, l_out0) = _layer_bwd(
        dx1, saved0, mem[0], w_r0, wkv0, wout0, 0,
        travel=lambda dwout: ([p_in1, p_kv1, out_blocks(dwout)], True,
                              [([0, 1], [cols(0, split), None]),
                               ([0, 2], [cols(split, D_MODEL - split), None])]))
    r_out1 = with_own_of(l_out1, out_blocks(gr1["w_out"]))
    r_in1, r_kv1 = with_own_of(l_in1, p_in1), with_own_of(l_kv1, p_kv1)
    r_out0 = with_own_of(l_out0, out_blocks(gr0["w_out"]))

    def both(name):
        return jnp.stack([gr0[name], gr1[name]])

    small = _pack_small(both("norm_w"), both("mem_norm_w"), both("out_norm_w"), dfnw[0],
                        both("b_forget"))
    p_small = jnp.broadcast_to(small[None], (N_DEV, SMALL_ROWS, LANES))
    r_in0, r_kv0, r_small = _exchange_blocks(
        [in_blocks(gr0["w_r"]), kv_blocks(gr0["w_mem_kv"]), p_small], "exchange_grads_l0")

    def t(v):
        return jnp.transpose(v, (0, 2, 1))

    g_w_in, d_w_in, nm_w_in, nv_w_in = [t(v) for v in _adamw_sum(
        [r_in0, r_in1], t(w_in), t(m_w_in), t(v_w_in), (SHARD_W, 256), "adamw_w_in")]
    g_w_kv, d_w_kv, nm_w_kv, nv_w_kv = _adamw_sum(
        [r_kv0, r_kv1], w_mem_kv, m_w_mem_kv, v_w_mem_kv, (kv_rows, 2 * MEM_W), "adamw_w_mem_kv")
    g_w_out, d_w_out, nm_w_out, nv_w_out = _adamw_sum(
        [r_out0, r_out1], w_out, m_w_out, v_w_out, (out_rows, D_MODEL), "adamw_w_out")
    w_small = _pack_small(norm_w, mem_norm_w, out_norm_w, final_norm_w, b_forget)[None]
    m_small = _pack_small(m_norm_w, m_mem_norm_w, m_out_norm_w, m_final_norm_w, m_b_forget)[None]
    v_small = _pack_small(v_norm_w, v_mem_norm_w, v_out_norm_w, v_final_norm_w, v_b_forget)[None]
    small_out = _adamw_sum([r_small], w_small, m_small, v_small, (SMALL_ROWS, LANES), "adamw_small")
    (g_nw, g_mnw, g_onw, g_fnw, g_b), (d_nw, d_mnw, d_onw, d_fnw, d_b), \
        (nm_nw, nm_mnw, nm_onw, nm_fnw, nm_b), (nv_nw, nv_mnw, nv_onw, nv_fnw, nv_b) = [
            _unpack_small(t[0]) for t in small_out]

    return (loss, grad_x[None],
            g_nw, g_w_in, g_b, g_mnw, g_w_kv, g_onw, g_w_out, g_fnw,
            d_nw, d_w_in, d_b, d_mnw, d_w_kv, d_onw, d_w_out, d_fnw,
            nm_nw, nm_w_in, nm_b, nm_mnw, nm_w_kv, nm_onw, nm_w_out, nm_fnw,
            nv_nw, nv_w_in, nv_b, nv_mnw, nv_w_kv, nv_onw, nv_w_out, nv_fnw)
```

```python
import functools
from typing import NamedTuple

import jax
import jax.numpy as jnp
from jax import lax
from jax.experimental import pallas as pl
from jax.experimental.pallas import tpu as pltpu

F32 = jnp.float32
BF16 = jnp.bfloat16

N_DEV = 8
D_MODEL = 1024
HEAD_DIM = 64
LANES = 128
SB_W = 512
FX_W = 512
MEM_W = 256
MIX_W = 1280
FOX_HEADS = 8
IN_W = 4616
SHARD_W = IN_W // N_DEV
QKV_W = 3 * SB_W + 3 * FX_W + MEM_W
FL_PAD = 256
GF_W = MIX_W + FL_PAD
WR_W = QKV_W + GF_W
EPS = 1e-6
T = 256
TM = 256
Q_SCALE = 0.125
NEG = -1e30
UNDERFLOW = -110.0

ADAM_LR = 0.001
ADAM_B1 = 0.9
ADAM_B2 = 0.999
ADAM_EPS = 1e-08
ADAM_WD = 0.01
ADAM_STEP = 10

SMALL_ROWS = 72
LOSS_ROW = 66

_NT = (((1,), (1,)), ((), ()))
_TN = (((0,), (0,)), ((), ()))

_ARB1 = pltpu.CompilerParams(dimension_semantics=("arbitrary",))
_ARB2 = pltpu.CompilerParams(dimension_semantics=("arbitrary", "arbitrary"))


def _dot(a, b):
    return jnp.dot(a, b, preferred_element_type=F32)


def _dot_nt(a, b):
    return lax.dot_general(a, b, _NT, preferred_element_type=F32)


def _dot_tn(a, b):
    return lax.dot_general(a, b, _TN, preferred_element_type=F32)


def _split2(x):
    hi = x.astype(BF16)
    lo = (x - hi.astype(F32)).astype(BF16)
    return hi, lo


def _stack2(u):
    return jnp.concatenate([u, u], axis=0)


def _cum2(x, u2):
    hi, lo = _split2(x)
    return _dot(jnp.concatenate([hi, lo], axis=1), u2)


def _tri3(tri, x):
    hi = x.astype(BF16)
    r1 = x - hi.astype(F32)
    mid = r1.astype(BF16)
    lo = (r1 - mid.astype(F32)).astype(BF16)
    return _dot(tri, hi) + _dot(tri, mid) + _dot(tri, lo)


def _iota2(shape, dim):
    return lax.broadcasted_iota(jnp.int32, shape, dim)


def _head_block_diag():
    r = _iota2((LANES, LANES), 0) // HEAD_DIM
    c = _iota2((LANES, LANES), 1) // HEAD_DIM
    return _stack2(jnp.where(r == c, 1.0, 0.0).astype(BF16))


def _head_mean(x, bd):
    return _cum2(x, bd) * (1.0 / HEAD_DIM)


def _sigmoid(x):
    return 1.0 / (1.0 + jnp.exp(-x))


def _log_sigmoid(x):
    return jnp.minimum(x, 0.0) - jnp.log(1.0 + jnp.exp(-jnp.abs(x)))


def _running_top(r_ref):
    return jnp.max(jnp.maximum(r_ref[0], r_ref[1]))


def _head_norm_tops(x, m0):
    x2 = x.astype(F32)
    x2 = x2 * x2
    zero = jnp.zeros_like(x2)
    return (jnp.max(jnp.sqrt(jnp.sum(jnp.where(m0, x2, zero), axis=1, keepdims=True))),
            jnp.max(jnp.sqrt(jnp.sum(jnp.where(m0, zero, x2), axis=1, keepdims=True))))


def _fox_tiles_left(i, pair, qn, kn_ref, fb_ref, tile):
    def bound(j):
        b = [2.0 * qn[h] * kn_ref[h] + fb_ref[2 * pair + h, i] - fb_ref[FOX_HEADS + 2 * pair + h, j]
             for h in range(2)]
        return jnp.maximum(b[0], b[1])

    def more(j):
        return jnp.logical_and(j >= 0, bound(jnp.maximum(j, 0)) > UNDERFLOW)

    def step(j):
        tile(j, False)
        return j - 1

    return lax.while_loop(more, step, i - 1)


def _pair_masks():
    lane = _iota2((1, LANES), 1)
    return lane < HEAD_DIM


def _split_pair(x, m0):
    zero = jnp.zeros_like(x)
    return jnp.where(m0, x, zero), jnp.where(m0, zero, x)


def _inproj_fwd(x, nw, w_r, name):
    s = x.shape[0]

    def body(x_ref, nw_ref, w_ref, qkv_ref, gf_ref):
        xv = x_ref[...]
        r = lax.rsqrt(jnp.mean(xv * xv, axis=-1, keepdims=True) + EPS)
        h = (xv * r * nw_ref[...]).astype(BF16)
        for c in range(0, QKV_W, 256):
            qkv_ref[:, c:c + 256] = _dot_nt(h, w_ref[c:c + 256, :]).astype(BF16)
        for c in range(0, GF_W, 256):
            gf_ref[:, c:c + 256] = _dot_nt(h, w_ref[QKV_W + c:QKV_W + c + 256, :])

    return pl.pallas_call(
        body, name=name, grid=(s // TM,),
        in_specs=[pl.BlockSpec((TM, D_MODEL), lambda i: (i, 0)),
                  pl.BlockSpec((1, D_MODEL), lambda i: (0, 0)),
                  pl.BlockSpec((WR_W, D_MODEL), lambda i: (0, 0))],
        out_specs=[pl.BlockSpec((TM, QKV_W), lambda i: (i, 0)),
                   pl.BlockSpec((TM, GF_W), lambda i: (i, 0))],
        out_shape=[jax.ShapeDtypeStruct((s, QKV_W), BF16), jax.ShapeDtypeStruct((s, GF_W), F32)],
        compiler_params=_ARB1,
    )(x, nw, w_r)


def _fox_prep_fwd(gf, bpad, name):
    s = gf.shape[0]

    def body(fl_ref, b_ref, f_ref):
        tri = jnp.where(_iota2((T, T), 0) >= _iota2((T, T), 1), 1.0, 0.0).astype(BF16)
        carry = jnp.zeros((1, LANES), F32)
        for blk in range(s // T):
            lf = _log_sigmoid(fl_ref[blk * T:(blk + 1) * T, :] + b_ref[...])
            c = _tri3(tri, lf) + carry
            f_ref[blk * T:(blk + 1) * T, :] = c
            carry = c[T - 1:T, :]

    return pl.pallas_call(
        body, name=name, grid=(1,),
        in_specs=[pl.BlockSpec((s, LANES), lambda i: (0, MIX_W // LANES)),
                  pl.BlockSpec((1, LANES), lambda i: (0, 0))],
        out_specs=pl.BlockSpec((s, LANES), lambda i: (0, 0)),
        out_shape=jax.ShapeDtypeStruct((s, LANES), F32),
        compiler_params=_ARB1,
    )(gf, bpad)


def _sb_fwd(qkv, name, rider=None):
    s = qkv.shape[0]

    def body(q_ref, k_ref, v_ref, o_ref, acc_ref, r_ref, as_ref):
        i = pl.program_id(1)
        m0 = _pair_masks()
        qh = _split_pair(q_ref[...] * jnp.asarray(Q_SCALE, BF16), m0)
        strict = _iota2((T, T), 0) > _iota2((T, T), 1)
        u2 = _stack2(jnp.where(strict, 1.0, 0.0).astype(BF16))
        acc_ref[...] = jnp.zeros_like(acc_ref)
        r_ref[...] = jnp.zeros_like(r_ref)
        hs = range(2)

        def flush(j):
            v = v_ref[pl.ds(pl.multiple_of(j * T, T), T), :]
            for h in hs:
                acc_ref[h] += _dot(as_ref[h], v)

        def tile(j, diag):
            k = k_ref[pl.ds(pl.multiple_of(j * T, T), T), :]
            z = [_dot_nt(qh[h], k) for h in hs]
            if not diag:
                flush(j + 1)
            la = [jnp.minimum(z[h], 0.0) - jnp.log(1.0 + jnp.exp(-jnp.abs(z[h]))) for h in hs]
            lf = [la[h] - z[h] for h in hs]
            if diag:
                lf = [jnp.where(strict, lf[h], 0.0) for h in hs]
            cin = [_cum2(lf[h], u2) for h in hs]
            a = [jnp.exp(la[h] + cin[h] + r_ref[h]) for h in hs]
            if diag:
                a = [jnp.where(strict, a[h], 0.0) for h in hs]
            for h in hs:
                r_ref[h] += cin[h][:, 0:1] + lf[h][:, 0:1]
                as_ref[h] = a[h].astype(BF16)

        tile(i, True)

        def more(state):
            j, top = state
            return jnp.logical_and(j >= 0, top > UNDERFLOW)

        def step(state):
            j, _ = state
            tile(j, False)
            return j - 1, _running_top(r_ref)

        j_left, _ = lax.while_loop(more, step, (i - 1, _running_top(r_ref)))
        flush(j_left + 1)
        o_ref[...] = jnp.where(m0, acc_ref[0], acc_ref[1])

    nb = SB_W // LANES
    (ysb,), lands = _ride(dict(
        body=body, name=name, grid=(nb, s // T),
        in_specs=[pl.BlockSpec((T, LANES), lambda p, i: (i, p)),
                  pl.BlockSpec((s, LANES), lambda p, i: (0, nb + p)),
                  pl.BlockSpec((s, LANES), lambda p, i: (0, 2 * nb + p))],
        out_specs=[pl.BlockSpec((T, LANES), lambda p, i: (i, p))],
        out_shape=[jax.ShapeDtypeStruct((s, SB_W), F32)],
        scratch_shapes=[pltpu.VMEM((2, T, LANES), F32), pltpu.VMEM((2, T, 1), F32),
                        pltpu.VMEM((2, T, T), BF16)],
        compiler_params=_ARB2, operands=[qkv, qkv, qkv]), rider)
    return ysb, lands


def _fox_fwd(qkv, fqb, frow, fbounds, name, rider=None):
    s = qkv.shape[0]

    def body(q_ref, k_ref, v_ref, fq_ref, fr_ref, fb_ref, o_ref, lse_ref, acc_ref, m_ref, ps_ref,
             kn_ref):
        pair = pl.program_id(0)
        i = pl.program_id(1)
        m0 = _pair_masks()

        @pl.when(i == 0)
        def _():
            kn_ref[0], kn_ref[1] = _head_norm_tops(k_ref[...], m0)

        q = q_ref[...] * jnp.asarray(Q_SCALE, BF16)
        qn = _head_norm_tops(q, m0)
        qh = _split_pair(q, m0)
        fq = fq_ref[...]
        fqh = (fq[:, 0:1], fq[:, HEAD_DIM:HEAD_DIM + 1])
        causal = _iota2((T, T), 0) >= _iota2((T, T), 1)
        ones = jnp.ones((T, LANES), BF16)
        acc_ref[...] = jnp.zeros_like(acc_ref)
        m_ref[...] = jnp.full_like(m_ref, NEG)
        hs = range(2)

        def flush(j):
            v = v_ref[pl.ds(pl.multiple_of(j * T, T), T), :]
            va2 = _stack2(jnp.concatenate([v, ones], axis=1))
            for h in hs:
                acc_ref[h] += _dot(ps_ref[h], va2)

        def tile(j, diag):
            off = pl.multiple_of(j * T, T)
            k = k_ref[pl.ds(off, T), :]
            sc = [_dot_nt(qh[h], k) + fqh[h] - fr_ref[h:h + 1, pl.ds(off, T)] for h in hs]
            if not diag:
                flush(j + 1)
            if diag:
                sc = [jnp.where(causal, sc[h], NEG) for h in hs]
            m_new = [jnp.maximum(m_ref[h], jnp.max(sc[h], axis=1, keepdims=True)) for h in hs]
            p = [jnp.exp(sc[h] - m_new[h]) for h in hs]
            for h in hs:
                acc_ref[h] = acc_ref[h] * jnp.exp(m_ref[h] - m_new[h])
                m_ref[h] = m_new[h]
                p_hi, p_lo = _split2(p[h])
                ps_ref[h] = jnp.concatenate([p_hi, p_lo], axis=1)

        tile(i, True)
        j_left = _fox_tiles_left(i, pair, qn, kn_ref, fb_ref, tile)
        flush(j_left + 1)
        acc = (acc_ref[0], acc_ref[1])
        o_ref[...] = jnp.where(m0, acc[0][:, :LANES] / acc[0][:, LANES:],
                               acc[1][:, :LANES] / acc[1][:, LANES:])
        lse_ref[...] = jnp.where(m0, m_ref[0] + jnp.log(acc[0][:, LANES:]),
                                 m_ref[1] + jnp.log(acc[1][:, LANES:]))

    nb = FX_W // LANES
    base = 3 * SB_W // LANES
    (yfx, lse), lands = _ride(dict(
        body=body, name=name, grid=(nb, s // T),
        in_specs=[pl.BlockSpec((T, LANES), lambda p, i: (i, base + p)),
                  pl.BlockSpec((s, LANES), lambda p, i: (0, base + nb + p)),
                  pl.BlockSpec((s, LANES), lambda p, i: (0, base + 2 * nb + p)),
                  pl.BlockSpec((T, LANES), lambda p, i: (i, p)),
                  pl.BlockSpec((None, 2, s), lambda p, i: (p, 0, 0)),
                  pl.BlockSpec(memory_space=pltpu.SMEM)],
        out_specs=[pl.BlockSpec((T, LANES), lambda p, i: (i, p)),
                   pl.BlockSpec((T, LANES), lambda p, i: (i, p))],
        out_shape=[jax.ShapeDtypeStruct((s, FX_W), F32), jax.ShapeDtypeStruct((s, FX_W), F32)],
        scratch_shapes=[pltpu.VMEM((2, T, 2 * LANES), F32), pltpu.VMEM((2, T, 1), F32),
                        pltpu.VMEM((2, T, 2 * T), BF16), pltpu.SMEM((2,), F32)],
        compiler_params=_ARB2, operands=[qkv, qkv, qkv, fqb, frow, fbounds]), rider)
    return yfx, lse, lands


def _memkv_fwd(mem, mnw, wkv, name):
    n = mem.shape[0]

    def body(mem_ref, mnw_ref, w_ref, kv_ref):
        mv = mem_ref[...]
        r = lax.rsqrt(jnp.mean(mv * mv, axis=-1, keepdims=True) + EPS)
        hm = (mv * r * mnw_ref[...]).astype(BF16)
        kv_ref[...] = _dot(hm, w_ref[...]).astype(BF16)

    return pl.pallas_call(
        body, name=name, grid=(1,),
        in_specs=[pl.BlockSpec((n, D_MODEL), lambda i: (0, 0)),
                  pl.BlockSpec((1, D_MODEL), lambda i: (0, 0)),
                  pl.BlockSpec((D_MODEL, 2 * MEM_W), lambda i: (0, 0))],
        out_specs=pl.BlockSpec((n, 2 * MEM_W), lambda i: (0, 0)),
        out_shape=jax.ShapeDtypeStruct((n, 2 * MEM_W), BF16),
        compiler_params=_ARB1,
    )(mem, mnw, wkv)


def _mem_fwd(qkv, kv, name):
    s = qkv.shape[0]
    n = kv.shape[0]

    def body(q_ref, k_ref, v_ref, o_ref, lse_ref):
        m0 = _pair_masks()
        qh = _split_pair(q_ref[...] * jnp.asarray(Q_SCALE, BF16), m0)
        k = k_ref[...]
        v = v_ref[...]
        outs, lses = [], []
        for h in range(2):
            sc = _dot_nt(qh[h], k)
            mx = jnp.max(sc, axis=1, keepdims=True)
            p = jnp.exp(sc - mx)
            l = jnp.sum(p, axis=1, keepdims=True)
            outs.append(_dot(p.astype(BF16), v) / l)
            lses.append(mx + jnp.log(l))
        o_ref[...] = jnp.where(m0, outs[0], outs[1])
        lse_ref[...] = jnp.where(m0, lses[0], lses[1])

    nb = MEM_W // LANES
    base = (3 * SB_W + 3 * FX_W) // LANES
    return pl.pallas_call(
        body, name=name, grid=(nb, s // T),
        in_specs=[pl.BlockSpec((T, LANES), lambda p, i: (i, base + p)),
                  pl.BlockSpec((n, LANES), lambda p, i: (0, p)),
                  pl.BlockSpec((n, LANES), lambda p, i: (0, nb + p))],
        out_specs=[pl.BlockSpec((T, LANES), lambda p, i: (i, p)),
                   pl.BlockSpec((T, LANES), lambda p, i: (i, p))],
        out_shape=[jax.ShapeDtypeStruct((s, MEM_W), F32), jax.ShapeDtypeStruct((s, MEM_W), F32)],
        compiler_params=_ARB2,
    )(qkv, kv, kv)


def _mix_chunk(c, ysb_ref, yfx_ref, ym_ref):
    if c < SB_W // LANES:
        return ysb_ref[:, c * LANES:(c + 1) * LANES]
    c -= SB_W // LANES
    if c < FX_W // LANES:
        return yfx_ref[:, c * LANES:(c + 1) * LANES]
    c -= FX_W // LANES
    return ym_ref[:, c * LANES:(c + 1) * LANES]


def _outproj_fwd(ysb, yfx, ym, gf, onw, wout, x, name):
    s = x.shape[0]

    def body(ysb_ref, yfx_ref, ym_ref, g_ref, onw_ref, w_ref, x_ref, o_ref, yg_ref):
        bd = _head_block_diag()
        for c in range(MIX_W // LANES):
            sl = slice(c * LANES, (c + 1) * LANES)
            u = _mix_chunk(c, ysb_ref, yfx_ref, ym_ref)
            r = lax.rsqrt(_head_mean(u * u, bd) + EPS)
            g = g_ref[:, sl]
            yg_ref[:, sl] = (u * r * onw_ref[:, sl] * (g * _sigmoid(g))).astype(BF16)
        o_ref[...] = x_ref[...] + _dot(yg_ref[...], w_ref[...])

    return pl.pallas_call(
        body, name=name, grid=(s // TM,),
        in_specs=[pl.BlockSpec((TM, SB_W), lambda i: (i, 0)),
                  pl.BlockSpec((TM, FX_W), lambda i: (i, 0)),
                  pl.BlockSpec((TM, MEM_W), lambda i: (i, 0)),
                  pl.BlockSpec((TM, MIX_W), lambda i: (i, 0)),
                  pl.BlockSpec((1, MIX_W), lambda i: (0, 0)),
                  pl.BlockSpec((MIX_W, D_MODEL), lambda i: (0, 0)),
                  pl.BlockSpec((TM, D_MODEL), lambda i: (i, 0))],
        out_specs=pl.BlockSpec((TM, D_MODEL), lambda i: (i, 0)),
        out_shape=jax.ShapeDtypeStruct((s, D_MODEL), F32),
        scratch_shapes=[pltpu.VMEM((TM, MIX_W), BF16)],
        compiler_params=_ARB1,
    )(ysb, yfx, ym, gf, onw, wout, x)


def _final_fwd_bwd(x, fnw, target, name):
    s = x.shape[0]

    def body(x_ref, w_ref, t_ref, dx_ref, loss_ref, dw_ref):
        @pl.when(pl.program_id(0) == 0)
        def _():
            loss_ref[...] = jnp.zeros_like(loss_ref)
            dw_ref[...] = jnp.zeros_like(dw_ref)

        xv = x_ref[...]
        w = w_ref[...]
        r = lax.rsqrt(jnp.mean(xv * xv, axis=-1, keepdims=True) + EPS)
        xh = xv * r
        err = xh * w - t_ref[...]
        part = jnp.sum(jnp.sum(err * err, axis=1, keepdims=True), axis=0, keepdims=True)
        loss_ref[...] += part * (0.5 / D_MODEL)
        dy = err * (1.0 / D_MODEL)
        dw_ref[...] += jnp.sum(dy * xh, axis=0, keepdims=True)
        dxh = dy * w
        dx_ref[...] = r * (dxh - xh * jnp.mean(dxh * xh, axis=-1, keepdims=True))

    return pl.pallas_call(
        body, name=name, grid=(s // TM,),
        in_specs=[pl.BlockSpec((TM, D_MODEL), lambda i: (i, 0)),
                  pl.BlockSpec((1, D_MODEL), lambda i: (0, 0)),
                  pl.BlockSpec((TM, D_MODEL), lambda i: (i, 0))],
        out_specs=[pl.BlockSpec((TM, D_MODEL), lambda i: (i, 0)),
                   pl.BlockSpec((1, LANES), lambda i: (0, 0)),
                   pl.BlockSpec((1, D_MODEL), lambda i: (0, 0))],
        out_shape=[jax.ShapeDtypeStruct((s, D_MODEL), F32), jax.ShapeDtypeStruct((1, LANES), F32),
                   jax.ShapeDtypeStruct((1, D_MODEL), F32)],
        compiler_params=_ARB1,
    )(x, fnw, target)


def _outproj_bwd(dxo, wout, ysb, yfx, ym, gf, onw, name):
    s = dxo.shape[0]

    def body(dx_ref, w_ref, ysb_ref, yfx_ref, ym_ref, g_ref, onw_ref,
             dysb_ref, dyfx_ref, dym_ref, dg_ref, dw_ref, donw_ref, yg_ref):
        @pl.when(pl.program_id(0) == 0)
        def _():
            dw_ref[...] = jnp.zeros_like(dw_ref)
            donw_ref[...] = jnp.zeros_like(donw_ref)

        dxb = dx_ref[...].astype(BF16)
        dyg = _dot_nt(dxb, w_ref[...])
        bd = _head_block_diag()
        for c in range(MIX_W // LANES):
            sl = slice(c * LANES, (c + 1) * LANES)
            u = _mix_chunk(c, ysb_ref, yfx_ref, ym_ref)
            r = lax.rsqrt(_head_mean(u * u, bd) + EPS)
            yn = u * r
            g = g_ref[:, sl]
            sg = _sigmoid(g)
            sil = g * sg
            onw = onw_ref[:, sl]
            e = dyg[:, sl]
            yg_ref[:, sl] = (yn * onw * sil).astype(BF16)
            donw_ref[:, sl] += jnp.sum(e * yn * sil, axis=0, keepdims=True)
            dg_ref[:, sl] = (e * yn * onw * (sg * (1.0 + g * (1.0 - sg)))).astype(BF16)
            dyn = e * onw * sil
            du = (r * (dyn - yn * _head_mean(dyn * yn, bd))).astype(BF16)
            if c < 4:
                dysb_ref[:, c * LANES:(c + 1) * LANES] = du
            elif c < 8:
                dyfx_ref[:, (c - 4) * LANES:(c - 3) * LANES] = du
            else:
                dym_ref[:, (c - 8) * LANES:(c - 7) * LANES] = du
        dw_ref[...] += _dot_tn(yg_ref[...], dxb)

    return pl.pallas_call(
        body, name=name, grid=(s // TM,),
        in_specs=[pl.BlockSpec((TM, D_MODEL), lambda i: (i, 0)),
                  pl.BlockSpec((MIX_W, D_MODEL), lambda i: (0, 0)),
                  pl.BlockSpec((TM, SB_W), lambda i: (i, 0)),
                  pl.BlockSpec((TM, FX_W), lambda i: (i, 0)),
                  pl.BlockSpec((TM, MEM_W), lambda i: (i, 0)),
                  pl.BlockSpec((TM, MIX_W), lambda i: (i, 0)),
                  pl.BlockSpec((1, MIX_W), lambda i: (0, 0))],
        out_specs=[pl.BlockSpec((TM, SB_W), lambda i: (i, 0)),
                   pl.BlockSpec((TM, FX_W), lambda i: (i, 0)),
                   pl.BlockSpec((TM, MEM_W), lambda i: (i, 0)),
                   pl.BlockSpec((TM, MIX_W), lambda i: (i, 0)),
                   pl.BlockSpec((MIX_W, D_MODEL), lambda i: (0, 0)),
                   pl.BlockSpec((1, MIX_W), lambda i: (0, 0))],
        out_shape=[jax.ShapeDtypeStruct((s, SB_W), BF16), jax.ShapeDtypeStruct((s, FX_W), BF16),
                   jax.ShapeDtypeStruct((s, MEM_W), BF16), jax.ShapeDtypeStruct((s, MIX_W), BF16),
                   jax.ShapeDtypeStruct((MIX_W, D_MODEL), F32), jax.ShapeDtypeStruct((1, MIX_W), F32)],
        scratch_shapes=[pltpu.VMEM((TM, MIX_W), BF16)],
        compiler_params=_ARB1,
    )(dxo, wout, ysb, yfx, ym, gf, onw)


def _row_dots(do_ref, o_ref, m0):
    prod = do_ref[...].astype(F32) * o_ref[...]
    zero = jnp.zeros_like(prod)
    return (jnp.sum(jnp.where(m0, prod, zero), axis=1, keepdims=True),
            jnp.sum(jnp.where(m0, zero, prod), axis=1, keepdims=True))


def _sb_bwd(qkv, o, do, name, rider=None):
    s = qkv.shape[0]
    nq = s // T

    def body(q_ref, k_ref, v_ref, o_ref, do_ref, dq_ref, dk_ref, dv_ref,
             dqa_ref, dka_ref, dva_ref, rl_ref, rg_ref, dzs_ref, abs_ref):
        i = pl.program_id(1)

        @pl.when(i == 0)
        def _():
            dka_ref[...] = jnp.zeros_like(dka_ref)
            dva_ref[...] = jnp.zeros_like(dva_ref)

        m0 = _pair_masks()
        qh = _split_pair(q_ref[...] * jnp.asarray(Q_SCALE, BF16), m0)
        doh = _split_pair(do_ref[...], m0)
        dsum = _row_dots(do_ref, o_ref, m0)
        strict = _iota2((T, T), 0) > _iota2((T, T), 1)
        u2 = _stack2(jnp.where(strict, 1.0, 0.0).astype(BF16))
        dqa_ref[...] = jnp.zeros_like(dqa_ref)
        rl_ref[...] = jnp.zeros_like(rl_ref)
        rg_ref[...] = jnp.zeros_like(rg_ref)

        hs = range(2)

        def flush(j):
            off = pl.multiple_of(j * T, T)
            k = k_ref[pl.ds(off, T), :]
            for h in hs:
                dqa_ref[h] += _dot(dzs_ref[h], k)
            dka_ref[pl.ds(off, T), :] += _dot_tn(dzs_ref[0], qh[0]) + _dot_tn(dzs_ref[1], qh[1])
            dva_ref[pl.ds(off, T), :] += _dot_tn(abs_ref[0], doh[0]) + _dot_tn(abs_ref[1], doh[1])

        def tile(j, diag):
            off = pl.multiple_of(j * T, T)
            k = k_ref[pl.ds(off, T), :]
            v = v_ref[pl.ds(off, T), :]
            z = [_dot_nt(qh[h], k) for h in hs]
            da = [_dot_nt(doh[h], v) for h in hs]
            if not diag:
                flush(j + 1)
            la = [jnp.minimum(z[h], 0.0) - jnp.log(1.0 + jnp.exp(-jnp.abs(z[h]))) for h in hs]
            lf = [la[h] - z[h] for h in hs]
            if diag:
                lf = [jnp.where(strict, lf[h], 0.0) for h in hs]
            cin = [_cum2(lf[h], u2) for h in hs]
            a = [jnp.exp(la[h] + cin[h] + rl_ref[h]) for h in hs]
            if diag:
                a = [jnp.where(strict, a[h], 0.0) for h in hs]
            ab = [a[h].astype(BF16) for h in hs]
            g = [ab[h].astype(F32) * da[h] for h in hs]
            gin = [_cum2(g[h], u2) for h in hs]
            dz = [g[h] - jnp.exp(la[h]) * ((dsum[h] - rg_ref[h]) - gin[h]) for h in hs]
            if diag:
                dz = [jnp.where(strict, dz[h], 0.0) for h in hs]
            for h in hs:
                rl_ref[h] += cin[h][:, 0:1] + lf[h][:, 0:1]
                rg_ref[h] += gin[h][:, 0:1] + g[h][:, 0:1]
                dzs_ref[h] = dz[h].astype(BF16)
                abs_ref[h] = ab[h]

        tile(i, True)

        def more(state):
            j, top = state
            return jnp.logical_and(j >= 0, top > UNDERFLOW)

        def step(state):
            j, _ = state
            tile(j, False)
            return j - 1, _running_top(rl_ref)

        j_left, _ = lax.while_loop(more, step, (i - 1, _running_top(rl_ref)))
        flush(j_left + 1)
        dq_ref[...] = (jnp.where(m0, dqa_ref[0], dqa_ref[1]) * Q_SCALE).astype(BF16)

        @pl.when(i == nq - 1)
        def _():
            dk_ref[...] = dka_ref[...].astype(BF16)
            dv_ref[...] = dva_ref[...].astype(BF16)

    nb = SB_W // LANES
    (dq, dk, dv), lands = _ride(dict(
        body=body, name=name, grid=(nb, nq),
        in_specs=[pl.BlockSpec((T, LANES), lambda p, i: (i, p)),
                  pl.BlockSpec((s, LANES), lambda p, i: (0, nb + p)),
                  pl.BlockSpec((s, LANES), lambda p, i: (0, 2 * nb + p)),
                  pl.BlockSpec((T, LANES), lambda p, i: (i, p)),
                  pl.BlockSpec((T, LANES), lambda p, i: (i, p))],
        out_specs=[pl.BlockSpec((T, LANES), lambda p, i: (i, p)),
                   pl.BlockSpec((s, LANES), lambda p, i: (0, p)),
                   pl.BlockSpec((s, LANES), lambda p, i: (0, p))],
        out_shape=[jax.ShapeDtypeStruct((s, SB_W), BF16)] * 3,
        scratch_shapes=[pltpu.VMEM((2, T, LANES), F32), pltpu.VMEM((s, LANES), F32),
                        pltpu.VMEM((s, LANES), F32), pltpu.VMEM((2, T, 1), F32),
                        pltpu.VMEM((2, T, 1), F32), pltpu.VMEM((2, T, T), BF16),
                        pltpu.VMEM((2, T, T), BF16)],
        compiler_params=_ARB2, operands=[qkv, qkv, qkv, o, do]), rider)
    return dq, dk, dv, lands


def _fox_bwd(qkv, fqb, frow, fbounds, o, lse, do, name, rider=None):
    s = qkv.shape[0]
    nq = s // T

    def body(q_ref, k_ref, v_ref, fq_ref, fr_ref, fb_ref, o_ref, lse_ref, do_ref,
             dq_ref, dk_ref, dv_ref, df_ref, dqa_ref, dka_ref, dva_ref, dfa_ref, dls_ref, pbs_ref,
             kn_ref):
        i = pl.program_id(1)

        @pl.when(i == 0)
        def _():
            dka_ref[...] = jnp.zeros_like(dka_ref)
            dva_ref[...] = jnp.zeros_like(dva_ref)
            dfa_ref[...] = jnp.zeros_like(dfa_ref)

        m0 = _pair_masks()

        @pl.when(i == 0)
        def _():
            kn_ref[0], kn_ref[1] = _head_norm_tops(k_ref[...], m0)

        q = q_ref[...] * jnp.asarray(Q_SCALE, BF16)
        qn = _head_norm_tops(q, m0)
        qh = _split_pair(q, m0)
        doh = _split_pair(do_ref[...], m0)
        dsum = _row_dots(do_ref, o_ref, m0)
        fq = fq_ref[...]
        fqh = (fq[:, 0:1], fq[:, HEAD_DIM:HEAD_DIM + 1])
        lse = lse_ref[...]
        lseh = (lse[:, 0:1], lse[:, HEAD_DIM:HEAD_DIM + 1])
        causal = _iota2((T, T), 0) >= _iota2((T, T), 1)
        dqa_ref[...] = jnp.zeros_like(dqa_ref)

        hs = range(2)

        def flush(j):
            off = pl.multiple_of(j * T, T)
            k = k_ref[pl.ds(off, T), :]
            for h in hs:
                dqa_ref[h] += _dot(dls_ref[h], k)
            dka_ref[pl.ds(off, T), :] += _dot_tn(dls_ref[0], qh[0]) + _dot_tn(dls_ref[1], qh[1])
            dva_ref[pl.ds(off, T), :] += _dot_tn(pbs_ref[0], doh[0]) + _dot_tn(pbs_ref[1], doh[1])

        def tile(j, diag):
            off = pl.multiple_of(j * T, T)
            k = k_ref[pl.ds(off, T), :]
            v = v_ref[pl.ds(off, T), :]
            sc = [_dot_nt(qh[h], k) + fqh[h] - fr_ref[h:h + 1, pl.ds(off, T)] for h in hs]
            dp = [_dot_nt(doh[h], v) for h in hs]
            if not diag:
                flush(j + 1)
            p = [jnp.exp(sc[h] - lseh[h]) for h in hs]
            if diag:
                p = [jnp.where(causal, p[h], 0.0) for h in hs]
            dl = [p[h] * (dp[h] - dsum[h]) for h in hs]
            for h in hs:
                dls_ref[h] = dl[h].astype(BF16)
                pbs_ref[h] = p[h].astype(BF16)
                dfa_ref[h:h + 1, pl.ds(off, T)] -= jnp.sum(dl[h], axis=0, keepdims=True)

        tile(i, True)
        j_left = _fox_tiles_left(i, pl.program_id(0), qn, kn_ref, fb_ref, tile)
        flush(j_left + 1)
        dq_ref[...] = (jnp.where(m0, dqa_ref[0], dqa_ref[1]) * Q_SCALE).astype(BF16)

        @pl.when(i == nq - 1)
        def _():
            dk_ref[...] = dka_ref[...].astype(BF16)
            dv_ref[...] = dva_ref[...].astype(BF16)
            df_ref[...] = dfa_ref[...]

    nb = FX_W // LANES
    base = 3 * SB_W // LANES
    (dq, dk, dv, df), lands = _ride(dict(
        body=body, name=name, grid=(nb, nq),
        in_specs=[pl.BlockSpec((T, LANES), lambda p, i: (i, base + p)),
                  pl.BlockSpec((s, LANES), lambda p, i: (0, base + nb + p)),
                  pl.BlockSpec((s, LANES), lambda p, i: (0, base + 2 * nb + p)),
                  pl.BlockSpec((T, LANES), lambda p, i: (i, p)),
                  pl.BlockSpec((None, 2, s), lambda p, i: (p, 0, 0)),
                  pl.BlockSpec(memory_space=pltpu.SMEM),
                  pl.BlockSpec((T, LANES), lambda p, i: (i, p)),
                  pl.BlockSpec((T, LANES), lambda p, i: (i, p)),
                  pl.BlockSpec((T, LANES), lambda p, i: (i, p))],
        out_specs=[pl.BlockSpec((T, LANES), lambda p, i: (i, p)),
                   pl.BlockSpec((s, LANES), lambda p, i: (0, p)),
                   pl.BlockSpec((s, LANES), lambda p, i: (0, p)),
                   pl.BlockSpec((None, 2, s), lambda p, i: (p, 0, 0))],
        out_shape=[jax.ShapeDtypeStruct((s, FX_W), BF16)] * 3
        + [jax.ShapeDtypeStruct((nb, 2, s), F32)],
        scratch_shapes=[pltpu.VMEM((2, T, LANES), F32), pltpu.VMEM((s, LANES), F32),
                        pltpu.VMEM((s, LANES), F32), pltpu.VMEM((2, s), F32),
                        pltpu.VMEM((2, T, T), BF16), pltpu.VMEM((2, T, T), BF16),
                        pltpu.SMEM((2,), F32)],
        compiler_params=_ARB2, operands=[qkv, qkv, qkv, fqb, frow, fbounds, o, lse, do]), rider)
    return dq, dk, dv, df, lands


def _fox_prep_bwd(dfcol, gf, bpad, name):
    s = gf.shape[0]

    def body(df_ref, fl_ref, b_ref, dfl_ref, db_ref):
        tri = jnp.where(_iota2((T, T), 0) <= _iota2((T, T), 1), 1.0, 0.0).astype(BF16)
        carry = jnp.zeros((1, LANES), F32)
        db = jnp.zeros((1, LANES), F32)
        for blk in reversed(range(s // T)):
            rows = slice(blk * T, (blk + 1) * T)
            c = _tri3(tri, df_ref[rows, :]) + carry
            carry = c[0:1, :]
            dfl = c / (1.0 + jnp.exp(fl_ref[rows, :] + b_ref[...]))
            dfl_ref[rows, :] = dfl.astype(BF16)
            db = db + jnp.sum(dfl, axis=0, keepdims=True)
        db_ref[...] = db

    return pl.pallas_call(
        body, name=name, grid=(1,),
        in_specs=[pl.BlockSpec((s, LANES), lambda i: (0, 0)),
                  pl.BlockSpec((s, LANES), lambda i: (0, MIX_W // LANES)),
                  pl.BlockSpec((1, LANES), lambda i: (0, 0))],
        out_specs=[pl.BlockSpec((s, LANES), lambda i: (0, 0)),
                   pl.BlockSpec((1, LANES), lambda i: (0, 0))],
        out_shape=[jax.ShapeDtypeStruct((s, LANES), BF16), jax.ShapeDtypeStruct((1, LANES), F32)],
        compiler_params=_ARB1,
    )(dfcol, gf, bpad)


def _mem_bwd(qkv, kv, o, lse, do, name):
    s = qkv.shape[0]
    n = kv.shape[0]
    nq = s // T

    def body(q_ref, k_ref, v_ref, o_ref, lse_ref, do_ref, dq_ref, dk_ref, dv_ref):
        @pl.when(pl.program_id(1) == 0)
        def _():
            dk_ref[...] = jnp.zeros_like(dk_ref)
            dv_ref[...] = jnp.zeros_like(dv_ref)

        m0 = _pair_masks()
        qh = _split_pair(q_ref[...] * jnp.asarray(Q_SCALE, BF16), m0)
        doh = _split_pair(do_ref[...], m0)
        dsum = _row_dots(do_ref, o_ref, m0)
        lse = lse_ref[...]
        lseh = (lse[:, 0:1], lse[:, HEAD_DIM:HEAD_DIM + 1])
        k = k_ref[...]
        v = v_ref[...]
        dqs = []
        for h in range(2):
            p = jnp.exp(_dot_nt(qh[h], k) - lseh[h])
            dl = p * (_dot_nt(doh[h], v) - dsum[h])
            dlb = dl.astype(BF16)
            dqs.append(_dot(dlb, k))
            dk_ref[...] += _dot_tn(dlb, qh[h])
            dv_ref[...] += _dot_tn(p.astype(BF16), doh[h])
        dq_ref[...] = (jnp.where(m0, dqs[0], dqs[1]) * Q_SCALE).astype(BF16)

    nb = MEM_W // LANES
    base = (3 * SB_W + 3 * FX_W) // LANES
    return pl.pallas_call(
        body, name=name, grid=(nb, nq),
        in_specs=[pl.BlockSpec((T, LANES), lambda p, i: (i, base + p)),
                  pl.BlockSpec((n, LANES), lambda p, i: (0, p)),
                  pl.BlockSpec((n, LANES), lambda p, i: (0, nb + p)),
                  pl.BlockSpec((T, LANES), lambda p, i: (i, p)),
                  pl.BlockSpec((T, LANES), lambda p, i: (i, p)),
                  pl.BlockSpec((T, LANES), lambda p, i: (i, p))],
        out_specs=[pl.BlockSpec((T, LANES), lambda p, i: (i, p)),
                   pl.BlockSpec((n, LANES), lambda p, i: (0, p)),
                   pl.BlockSpec((n, LANES), lambda p, i: (0, p))],
        out_shape=[jax.ShapeDtypeStruct((s, MEM_W), BF16), jax.ShapeDtypeStruct((n, MEM_W), F32),
                   jax.ShapeDtypeStruct((n, MEM_W), F32)],
        compiler_params=_ARB2,
    )(qkv, kv, kv, o, lse, do)


def _memkv_bwd(mem, mnw, wkv, dk, dv, name):
    n = mem.shape[0]

    def body(mem_ref, mnw_ref, w_ref, dk_ref, dv_ref, dw_ref, dmnw_ref):
        mv = mem_ref[...]
        r = lax.rsqrt(jnp.mean(mv * mv, axis=-1, keepdims=True) + EPS)
        mh = mv * r
        hm = (mh * mnw_ref[...]).astype(BF16)
        dkv = jnp.concatenate([dk_ref[...], dv_ref[...]], axis=1).astype(BF16)
        dw_ref[...] = _dot_tn(hm, dkv)
        dhm = _dot_nt(dkv, w_ref[...])
        dmnw_ref[...] = jnp.sum(dhm * mh, axis=0, keepdims=True)

    return pl.pallas_call(
        body, name=name, grid=(1,),
        in_specs=[pl.BlockSpec((n, D_MODEL), lambda i: (0, 0)),
                  pl.BlockSpec((1, D_MODEL), lambda i: (0, 0)),
                  pl.BlockSpec((D_MODEL, 2 * MEM_W), lambda i: (0, 0)),
                  pl.BlockSpec((n, MEM_W), lambda i: (0, 0)),
                  pl.BlockSpec((n, MEM_W), lambda i: (0, 0))],
        out_specs=[pl.BlockSpec((D_MODEL, 2 * MEM_W), lambda i: (0, 0)),
                   pl.BlockSpec((1, D_MODEL), lambda i: (0, 0))],
        out_shape=[jax.ShapeDtypeStruct((D_MODEL, 2 * MEM_W), F32),
                   jax.ShapeDtypeStruct((1, D_MODEL), F32)],
        compiler_params=_ARB1,
    )(mem, mnw, wkv, dk, dv)


def _inproj_bwd_dx(pieces, w_r, x, nw, dxo, name):
    s = x.shape[0]
    n = len(pieces)
    widths = [p.shape[1] for p in pieces]

    def body(*refs):
        piece_refs = refs[:n]
        w_ref, x_ref, nw_ref, dxo_ref, dx_ref, h_ref, dnw_ref, dp_ref = refs[n:]

        @pl.when(pl.program_id(0) == 0)
        def _():
            dnw_ref[...] = jnp.zeros_like(dnw_ref)

        col = 0
        for r, wd in zip(piece_refs, widths):
            dp_ref[:, col:col + wd] = r[...]
            col += wd
        dp_ref[:, col:] = jnp.zeros((TM, WR_W - col), BF16)
        dh = _dot(dp_ref[...], w_ref[...])
        xv = x_ref[...]
        nw = nw_ref[...]
        r = lax.rsqrt(jnp.mean(xv * xv, axis=-1, keepdims=True) + EPS)
        xh = xv * r
        h_ref[...] = (xh * nw).astype(BF16)
        dnw_ref[...] += jnp.sum(dh * xh, axis=0, keepdims=True)
        dxh = dh * nw
        dx_ref[...] = r * (dxh - xh * jnp.mean(dxh * xh, axis=-1, keepdims=True)) + dxo_ref[...]

    return pl.pallas_call(
        body, name=name, grid=(s // TM,),
        in_specs=[pl.BlockSpec((TM, wd), lambda i: (i, 0)) for wd in widths]
        + [pl.BlockSpec((WR_W, D_MODEL), lambda i: (0, 0)),
           pl.BlockSpec((TM, D_MODEL), lambda i: (i, 0)),
           pl.BlockSpec((1, D_MODEL), lambda i: (0, 0)),
           pl.BlockSpec((TM, D_MODEL), lambda i: (i, 0))],
        out_specs=[pl.BlockSpec((TM, D_MODEL), lambda i: (i, 0)),
                   pl.BlockSpec((TM, D_MODEL), lambda i: (i, 0)),
                   pl.BlockSpec((1, D_MODEL), lambda i: (0, 0)),
                   pl.BlockSpec((TM, WR_W), lambda i: (i, 0))],
        out_shape=[jax.ShapeDtypeStruct((s, D_MODEL), F32), jax.ShapeDtypeStruct((s, D_MODEL), BF16),
                   jax.ShapeDtypeStruct((1, D_MODEL), F32), jax.ShapeDtypeStruct((s, WR_W), BF16)],
        compiler_params=_ARB1,
    )(*pieces, w_r, x, nw, dxo)


def _inproj_bwd_dw(h, dproj, name):
    s = dproj.shape[0]
    tn = 256

    def body(h_ref, dp_ref, dw_ref):
        dw_ref[...] = _dot_tn(dp_ref[...], h_ref[...])

    return pl.pallas_call(
        body, name=name, grid=(WR_W // tn,),
        in_specs=[pl.BlockSpec((s, D_MODEL), lambda j: (0, 0)),
                  pl.BlockSpec((s, tn), lambda j: (0, j))],
        out_specs=pl.BlockSpec((tn, D_MODEL), lambda j: (j, 0)),
        out_shape=jax.ShapeDtypeStruct((WR_W, D_MODEL), F32),
        compiler_params=_ARB1,
    )(h, dproj)


def _rearrange_w_in(wt):
    pad = jnp.zeros((FL_PAD - FOX_HEADS,) + wt.shape[1:], wt.dtype)
    return jnp.concatenate([wt[:3072], wt[3080:3336], wt[3336:IN_W], wt[3072:3080], pad], axis=0)


def _restore_w_in(g):
    gate0 = QKV_W
    fl0 = QKV_W + MIX_W
    return jnp.concatenate(
        [g[:3072], g[fl0:fl0 + FOX_HEADS], g[3072:QKV_W], g[gate0:fl0]], axis=0)


def _pad_lanes(v, width=LANES):
    return jnp.pad(v, (0, width - v.shape[0])).reshape(1, width)


def _layer_fwd(xs, mem, nw, w_r, b_forget, mnw, late, onw, l, travel=None):
    s = xs.shape[0]
    bpad = _pad_lanes(b_forget)
    qkv, gf = _inproj_fwd(xs, nw, w_r, f"inproj_fwd_{l}")
    f = _fox_prep_fwd(gf, bpad, f"fox_prep_fwd_{l}")
    f8 = f[:, :FOX_HEADS]
    fqb = jnp.repeat(f8, HEAD_DIM, axis=1)
    frow = f8.T.reshape(FOX_HEADS // 2, 2, s)
    fbounds = jnp.concatenate([f8[0::T].T, f8[T - 1::T].T], axis=0)
    travel = _Travel(travel)
    ysb, _ = travel.ride(0, _sb_fwd, qkv, f"sb_fwd_{l}")
    yfx, lse_fx, _ = travel.ride(1, _fox_fwd, qkv, fqb, frow, fbounds, f"fox_fwd_{l}")
    wkv, wout = late(travel.lands)
    kv = _memkv_fwd(mem, mnw, wkv, f"memkv_fwd_{l}")
    ym, lse_m = _mem_fwd(qkv, kv, f"mem_fwd_{l}")
    xn = _outproj_fwd(ysb, yfx, ym, gf, onw, wout, xs, f"outproj_fwd_{l}")
    saved = (xs, nw, mnw, onw, bpad, qkv, gf, fqb, frow, fbounds, ysb, yfx, lse_fx, kv, ym, lse_m)
    return xn, saved, travel.lands, (wkv, wout)


class _Travel:
    def __init__(self, plan):
        self.plan = plan
        self.lands = None if plan is None else _new_lands(plan[0], plan[1])

    def ride(self, n, fn, *args):
        if self.plan is None or self.plan[2][n] is None:
            return fn(*args)
        srcs, scatter, legs = self.plan
        idx, rows = legs[n]
        out = fn(*args, rider=_Rider([srcs[a] for a in idx], [self.lands[a] for a in idx],
                                     scatter, rows))
        for a, land in zip(idx, out[-1]):
            self.lands[a] = land
        return out


def _layer_bwd(dx, saved, mem, w_r, wkv, wout, l, travel=None):
    xs, nw, mnw, onw, bpad, qkv, gf, fqb, frow, fbounds, ysb, yfx, lse_fx, kv, ym, lse_m = saved
    s = xs.shape[0]
    dysb, dyfx, dym, dgate, dwout, donw = _outproj_bwd(
        dx, wout, ysb, yfx, ym, gf, onw, f"outproj_bwd_{l}")
    travel = _Travel(None if travel is None else travel(dwout))
    sdq, sdk, sdv, _ = travel.ride(0, _sb_bwd, qkv, ysb, dysb, f"sb_bwd_{l}")
    fdq, fdk, fdv, dfrow, _ = travel.ride(1, _fox_bwd, qkv, fqb, frow, fbounds, yfx, lse_fx, dyfx,
                                          f"fox_bwd_{l}")
    dfcol = jnp.pad(dfrow.reshape(FOX_HEADS, s).T, ((0, 0), (0, LANES - FOX_HEADS)))
    dfl, db = _fox_prep_bwd(dfcol, gf, bpad, f"fox_prep_bwd_{l}")
    dmq, dmk, dmv = _mem_bwd(qkv, kv, ym, lse_m, dym, f"mem_bwd_{l}")
    dwkv, dmnw = _memkv_bwd(mem, mnw, wkv, dmk, dmv, f"memkv_bwd_{l}")
    dx, ht, dnw, dproj = _inproj_bwd_dx([sdq, sdk, sdv, fdq, fdk, fdv, dmq, dgate, dfl],
                                        w_r, xs, nw, dx, f"inproj_bwd_dx_{l}")
    dwr = _inproj_bwd_dw(ht, dproj, f"inproj_bwd_dw_{l}")
    grads = dict(norm_w=dnw[0], w_r=dwr, b_forget=db[0, :FOX_HEADS], mem_norm_w=dmnw[0],
                 w_mem_kv=dwkv, out_norm_w=donw[0], w_out=dwout)
    return dx, grads, travel.lands


_ANY = pl.BlockSpec(memory_space=pl.ANY)


def _my_place():
    return lax.axis_index("x"), lax.axis_index("y"), lax.axis_index("c")


def _flip(v, bit):
    return 1 - v if bit else v


def _block_index(px, py, pc):
    return 4 * px + 2 * py + pc


def _all_gather_weights(shards, name):
    n = len(shards)

    def body(*refs):
        ins, outs = refs[:n], refs[n:2 * n]
        send_sems, recv_sems, local_sems = refs[2 * n:]
        x, y, c = _my_place()
        me = (x, y, c)
        sibling = (x, y, 1 - c)
        chips = [(1 - x, y), (x, 1 - y), (1 - x, 1 - y)]

        def copy(a, k, block, to, src=None):
            dst = outs[a].at[_block_index(*block)]
            return pltpu.make_async_remote_copy(
                src_ref=dst if src is None else src, dst_ref=dst,
                send_sem=send_sems.at[a, k], recv_sem=recv_sems.at[a, k],
                device_id=to, device_id_type=pl.DeviceIdType.MESH)

        mine = [pltpu.make_async_copy(ins[a], outs[a].at[_block_index(*me)], local_sems.at[a])
                for a in range(n)]
        for cp in mine:
            cp.start()
        first = []
        for a in range(n):
            first.append(copy(a, 0, me, sibling, src=ins[a]))
            first += [copy(a, 1 + j, me, (*chip, c), src=ins[a]) for j, chip in enumerate(chips)]
        for cp in first:
            cp.start()
        passed = []
        for j, chip in enumerate(chips):
            for a in range(n):
                copy(a, 1 + j, (*chip, c), me).wait_recv()
                fwd = copy(a, 4 + j, (*chip, c), sibling)
                fwd.start()
                passed.append(fwd)
        for a in range(n):
            copy(a, 0, sibling, me).wait_recv()
            for j, chip in enumerate(chips):
                copy(a, 4 + j, (*chip, 1 - c), me).wait_recv()
        for cp in first + passed:
            cp.wait_send()
        for cp in mine:
            cp.wait()

    return pl.pallas_call(
        body, name=name,
        in_specs=[_ANY] * n, out_specs=[_ANY] * n,
        out_shape=[jax.ShapeDtypeStruct((N_DEV,) + v.shape, v.dtype) for v in shards],
        scratch_shapes=[pltpu.SemaphoreType.DMA((n, 7)), pltpu.SemaphoreType.DMA((n, 7)),
                        pltpu.SemaphoreType.DMA((n,))],
    )(*shards)


def _exchange_blocks(blocked, name):
    n = len(blocked)

    def body(*refs):
        ins, outs = refs[:n], refs[n:2 * n]
        send_sems, recv_sems, local_sems = refs[2 * n:]
        x, y, c = _my_place()
        mine_idx = _block_index(x, y, c)
        local = [pltpu.make_async_copy(ins[a].at[mine_idx], outs[a].at[mine_idx], local_sems.at[a])
                 for a in range(n)]
        for cp in local:
            cp.start()
        sends, arrivals = [], []
        for r in range(1, N_DEV):
            peer = (_flip(x, r & 4), _flip(y, r & 2), _flip(c, r & 1))
            peer_idx = _block_index(*peer)
            for a in range(n):
                sems = dict(send_sem=send_sems.at[a, r - 1], recv_sem=recv_sems.at[a, r - 1],
                            device_id=peer, device_id_type=pl.DeviceIdType.MESH)
                sends.append(pltpu.make_async_remote_copy(
                    src_ref=ins[a].at[peer_idx], dst_ref=outs[a].at[mine_idx], **sems))
                arrivals.append(pltpu.make_async_remote_copy(
                    src_ref=ins[a].at[peer_idx], dst_ref=outs[a].at[peer_idx], **sems))
        for cp in sends:
            cp.start()
        for cp in arrivals:
            cp.wait_recv()
        for cp in sends:
            cp.wait_send()
        for cp in local:
            cp.wait()

    return pl.pallas_call(
        body, name=name,
        in_specs=[_ANY] * n, out_specs=[_ANY] * n,
        out_shape=[jax.ShapeDtypeStruct(v.shape, v.dtype) for v in blocked],
        scratch_shapes=[pltpu.SemaphoreType.DMA((n, 7)), pltpu.SemaphoreType.DMA((n, 7)),
                        pltpu.SemaphoreType.DMA((n,))],
    )(*blocked)


class _Rider(NamedTuple):
    srcs: list
    lands: list
    scatter: bool
    part: list


def _new_lands(srcs, scatter):
    return [lax.empty(v.shape if scatter else (N_DEV,) + v.shape, v.dtype) for v in srcs]


def _rider_copies(srcs, lands, send_sems, recv_sems, rider):
    x, y, c = _my_place()
    mine_idx = _block_index(x, y, c)

    def window(ref, a):
        if rider.part[a] is None:
            return ref
        dim, start, size = rider.part[a]
        return ref.at[(slice(None),) * dim + (pl.ds(start, size),)]

    sends, arrivals = [], []
    for r in range(1, N_DEV):
        peer = (_flip(x, r & 4), _flip(y, r & 2), _flip(c, r & 1))
        peer_idx = _block_index(*peer)
        for a in range(len(srcs)):
            src = window(srcs[a].at[peer_idx] if rider.scatter else srcs[a], a)
            k = 7 * a + r - 1
            sems = dict(send_sem=send_sems.at[k], recv_sem=recv_sems.at[k],
                        device_id=peer, device_id_type=pl.DeviceIdType.MESH)
            sends.append(pltpu.make_async_remote_copy(
                src_ref=src, dst_ref=window(lands[a].at[mine_idx], a), **sems))
            arrivals.append(pltpu.make_async_remote_copy(
                src_ref=src, dst_ref=window(lands[a].at[peer_idx], a), **sems))
    return sends, arrivals


def _ride(call, rider):
    call = dict(call)
    body, grid = call.pop("body"), call["grid"]
    operands = call.pop("operands")
    if rider is None:
        return list(pl.pallas_call(body, **call)(*operands)), None
    n_in, n_out = len(call["in_specs"]), len(call["out_specs"])
    n_scratch = len(call["scratch_shapes"])
    m = len(rider.srcs)

    def riding(*refs):
        main_in, srcs, lands = refs[:n_in], refs[n_in:n_in + m], refs[n_in + m:n_in + 2 * m]
        main_out = refs[n_in + 2 * m:n_in + 2 * m + n_out]
        rest = refs[n_in + 3 * m + n_out:]
        send_sems, recv_sems = rest[n_scratch:]
        at = [pl.program_id(d) for d in range(len(grid))]
        first = functools.reduce(jnp.logical_and, [p == 0 for p in at])
        last = functools.reduce(jnp.logical_and, [p == g - 1 for p, g in zip(at, grid)])
        sends, arrivals = _rider_copies(srcs, lands, send_sems, recv_sems, rider)

        @pl.when(first)
        def _():
            for cp in sends:
                cp.start()

        body(*main_in, *main_out, *rest[:n_scratch])

        @pl.when(last)
        def _():
            for cp in arrivals:
                cp.wait_recv()
            for cp in sends:
                cp.wait_send()

    call["in_specs"] = list(call["in_specs"]) + [_ANY] * (2 * m)
    call["out_specs"] = list(call["out_specs"]) + [_ANY] * m
    call["out_shape"] = list(call["out_shape"]) + [
        jax.ShapeDtypeStruct(v.shape, v.dtype) for v in rider.lands]
    call["scratch_shapes"] = list(call["scratch_shapes"]) + [
        pltpu.SemaphoreType.DMA((7 * m,)), pltpu.SemaphoreType.DMA((7 * m,))]
    call["input_output_aliases"] = {n_in + m + a: n_out + a for a in range(m)}
    outs = pl.pallas_call(riding, **call)(*operands, *rider.srcs, *rider.lands)
    return list(outs[:n_out]), list(outs[n_out:])


def _adamw_sum(parts, w, m, v, tile, name):
    depth, nrow, ncol = w.shape
    rows, cols = tile
    c1 = 1.0 / (1.0 - ADAM_B1 ** ADAM_STEP)
    c2 = 1.0 / (1.0 - ADAM_B2 ** ADAM_STEP)

    def body(*refs):
        p_refs = refs[:depth]
        w_ref, m_ref, v_ref, g_ref, d_ref, nm_ref, nv_ref = refs[depth:]
        layer = pl.program_id(0)
        for l in range(depth):
            @pl.when(layer == l)
            def _(p_ref=p_refs[l]):
                g = p_ref[0].astype(F32)
                for k in range(1, N_DEV):
                    g = g + p_ref[k].astype(F32)
                nm = ADAM_B1 * m_ref[...] + (1.0 - ADAM_B1) * g
                nv = ADAM_B2 * v_ref[...] + (1.0 - ADAM_B2) * (g * g)
                g_ref[...] = g
                nm_ref[...] = nm
                nv_ref[...] = nv
                d_ref[...] = -ADAM_LR * ((nm * c1) / (jnp.sqrt(nv * c2) + ADAM_EPS)
                                         + ADAM_WD * w_ref[...])

    def part_spec(l):
        return pl.BlockSpec((N_DEV, rows, cols), lambda q, i, j: (
            0, jnp.where(q == l, i, 0), jnp.where(q == l, j, 0)))

    blk = pl.BlockSpec((None, rows, cols), lambda q, i, j: (q, i, j))
    return pl.pallas_call(
        body, name=name, grid=(depth, nrow // rows, ncol // cols),
        in_specs=[part_spec(l) for l in range(depth)] + [blk, blk, blk],
        out_specs=[blk] * 4,
        out_shape=[jax.ShapeDtypeStruct(w.shape, F32)] * 4,
        compiler_params=pltpu.CompilerParams(
            dimension_semantics=("arbitrary", "arbitrary", "arbitrary")),
    )(*parts, w, m, v)


def _pack_small(norm_w, mem_norm_w, out_norm_w, final_norm_w, b_forget):
    onw = jnp.pad(out_norm_w.reshape(20, LANES), ((0, 4), (0, 0)))
    b = jnp.pad(b_forget, ((0, 6), (0, LANES - FOX_HEADS)))
    return jnp.concatenate([norm_w.reshape(16, LANES), mem_norm_w.reshape(16, LANES), onw,
                            final_norm_w.reshape(8, LANES), b], axis=0)


def _unpack_small(p):
    return (p[0:16].reshape(2, D_MODEL), p[16:32].reshape(2, D_MODEL), p[32:52].reshape(2, MIX_W),
            p[56:64].reshape(D_MODEL), p[64:66, :FOX_HEADS])


def kernel(x, mem, norm_w, w_in, b_forget, mem_norm_w, w_mem_kv, out_norm_w, w_out, final_norm_w, loss_target, m_norm_w, m_w_in, m_b_forget, m_mem_norm_w, m_w_mem_kv, m_out_norm_w, m_w_out, m_final_norm_w, v_norm_w, v_w_in, v_b_forget, v_mem_norm_w, v_w_mem_kv, v_out_norm_w, v_w_out, v_final_norm_w):
    kv_rows = w_mem_kv.shape[1]
    out_rows = w_out.shape[1]
    me = _block_index(*_my_place())

    def shards(l):
        return [w_in[l].T.astype(BF16), w_mem_kv[l].astype(BF16), w_out[l].astype(BF16)]

    def full_in(g_in):
        def rows(first, last):
            pieces = []
            while first < last:
                k, r = divmod(first, SHARD_W)
                n = min(last - first, SHARD_W - r)
                pieces.append(g_in[k, r:r + n])
                first += n
            return pieces

        pad = jnp.zeros((FL_PAD - FOX_HEADS, D_MODEL), g_in.dtype)
        return jnp.concatenate(
            rows(0, 3072) + rows(3080, 3336) + rows(3336, IN_W) + rows(3072, 3080) + [pad], axis=0)

    def full_kv_out(g_kv, g_out):
        return g_kv.reshape(D_MODEL, 2 * MEM_W), g_out.reshape(MIX_W, D_MODEL)

    def in_blocks(g):
        segments = [(0, 3072, 0), (3072, 3080, QKV_W + MIX_W), (3080, 3336, 3072),
                    (3336, IN_W, QKV_W)]

        def block(k):
            lo, hi = k * SHARD_W, (k + 1) * SHARD_W
            pieces = [g[at + max(lo, a) - a:at + min(hi, b) - a]
                      for a, b, at in segments if max(lo, a) < min(hi, b)]
            return jnp.concatenate(pieces, axis=0).astype(BF16)

        return jnp.stack([block(k) for k in range(N_DEV)])

    def kv_blocks(g):
        return g.reshape(N_DEV, kv_rows, 2 * MEM_W).astype(BF16)

    def out_blocks(g):
        return g.reshape(N_DEV, out_rows, D_MODEL).astype(BF16)

    def with_own(land, own):
        return lax.dynamic_update_slice(land, own[None], (me,) + (0,) * own.ndim)

    def with_own_of(land, blocked):
        return lax.dynamic_update_slice(land, lax.dynamic_slice_in_dim(blocked, me, 1, axis=0),
                                        (me,) + (0,) * (land.ndim - 1))

    def row(v):
        return v.reshape(1, -1)

    def cols(first, size):
        return (1, first, size)

    half = D_MODEL // 2

    g_in0, g_kv0, g_out0 = _all_gather_weights(shards(0), "all_gather_l0")
    w_r0 = full_in(g_in0)
    s_in1, s_kv1, s_out1 = shards(1)
    x1, saved0, (l_in1,), (wkv0, wout0) = _layer_fwd(
        x[0], mem[0], row(norm_w[0]), w_r0, b_forget[0], row(mem_norm_w[0]),
        lambda lands: full_kv_out(g_kv0, g_out0), row(out_norm_w[0]), 0,
        travel=([s_in1], False, [([0], [cols(0, half)]), ([0], [cols(half, half)])]))
    w_r1 = full_in(with_own(l_in1, s_in1))
    x2, saved1, _, (wkv1, wout1) = _layer_fwd(
        x1, mem[0], row(norm_w[1]), w_r1, b_forget[1], row(mem_norm_w[1]),
        lambda lands: full_kv_out(with_own(lands[0], s_kv1), with_own(lands[1], s_out1)),
        row(out_norm_w[1]), 1, travel=([s_kv1, s_out1], False, [([0, 1], [None, None]), None]))

    dx2, loss_part, dfnw = _final_fwd_bwd(x2, row(final_norm_w), loss_target[0], "final_fwd_bwd")

    dx1, gr1, (l_out1,) = _layer_bwd(
        dx2, saved1, mem[0], w_r1, wkv1, wout1, 1,
        travel=lambda dwout: ([out_blocks(dwout)], True, [([0], [None]), None]))
    p_in1, p_kv1 = in_blocks(gr1["w_r"]), kv_blocks(gr1["w_mem_kv"])
    split = 5 * LANES
    grad_x, gr0, (l_in1, l_kv1, l_out0) = _layer_bwd(
        dx1, saved0, mem[0], w_r0, wkv0, wout0, 0,
        travel=lambda dwout: ([p_in1, p_kv1, out_blocks(dwout)], True,
                              [([0, 1], [cols(0, split), None]),
                               ([0, 2], [cols(split, D_MODEL - split), None])]))
    r_out1 = with_own_of(l_out1, out_blocks(gr1["w_out"]))
    r_in1, r_kv1 = with_own_of(l_in1, p_in1), with_own_of(l_kv1, p_kv1)
    r_out0 = with_own_of(l_out0, out_blocks(gr0["w_out"]))

    def both(name):
        return jnp.stack([gr0[name], gr1[name]])

    small = _pack_small(both("norm_w"), both("mem_norm_w"), both("out_norm_w"), dfnw[0],
                        both("b_forget")).at[LOSS_ROW].set(loss_part[0])
    p_small = jnp.broadcast_to(small[None], (N_DEV, SMALL_ROWS, LANES))
    r_in0, r_kv0, r_small = _exchange_blocks(
        [in_blocks(gr0["w_r"]), kv_blocks(gr0["w_mem_kv"]), p_small], "exchange_grads_l0")

    def t(v):
        return jnp.transpose(v, (0, 2, 1))

    g_w_in, d_w_in, nm_w_in, nv_w_in = [t(v) for v in _adamw_sum(
        [r_in0, r_in1], t(w_in), t(m_w_in), t(v_w_in), (SHARD_W, 256), "adamw_w_in")]
    g_w_kv, d_w_kv, nm_w_kv, nv_w_kv = _adamw_sum(
        [r_kv0, r_kv1], w_mem_kv, m_w_mem_kv, v_w_mem_kv, (kv_rows, 2 * MEM_W), "adamw_w_mem_kv")
    g_w_out, d_w_out, nm_w_out, nv_w_out = _adamw_sum(
        [r_out0, r_out1], w_out, m_w_out, v_w_out, (out_rows, D_MODEL), "adamw_w_out")
    w_small = _pack_small(norm_w, mem_norm_w, out_norm_w, final_norm_w, b_forget)[None]
    m_small = _pack_small(m_norm_w, m_mem_norm_w, m_out_norm_w, m_final_norm_w, m_b_forget)[None]
    v_small = _pack_small(v_norm_w, v_mem_norm_w, v_out_norm_w, v_final_norm_w, v_b_forget)[None]
    small_out = _adamw_sum([r_small], w_small, m_small, v_small, (SMALL_ROWS, LANES), "adamw_small")
    (g_nw, g_mnw, g_onw, g_fnw, g_b), (d_nw, d_mnw, d_onw, d_fnw, d_b), \
        (nm_nw, nm_mnw, nm_onw, nm_fnw, nm_b), (nv_nw, nv_mnw, nv_onw, nv_fnw, nv_b) = [
            _unpack_small(t[0]) for t in small_out]
    loss = small_out[0][0, LOSS_ROW, 0]

    return (loss, grad_x[None],
            g_nw, g_w_in, g_b, g_mnw, g_w_kv, g_onw, g_w_out, g_fnw,
            d_nw, d_w_in, d_b, d_mnw, d_w_kv, d_onw, d_w_out, d_fnw,
            nm_nw, nm_w_in, nm_b, nm_mnw, nm_w_kv, nm_onw, nm_w_out, nm_fnw,
            nv_nw, nv_w_in, nv_b, nv_mnw, nv_w_kv, nv_onw, nv_w_out, nv_fnw)
```

```python
import functools
from typing import NamedTuple

import jax
import jax.numpy as jnp
from jax import lax
from jax.experimental import pallas as pl
from jax.experimental.pallas import tpu as pltpu

F32 = jnp.float32
BF16 = jnp.bfloat16

N_DEV = 8
D_MODEL = 1024
HEAD_DIM = 64
LANES = 128
SB_W = 512
FX_W = 512
MEM_W = 256
MIX_W = 1280
FOX_HEADS = 8
IN_W = 4616
SHARD_W = IN_W // N_DEV
QKV_W = 3 * SB_W + 3 * FX_W + MEM_W
FL_PAD = 256
GF_W = MIX_W + FL_PAD
WR_W = QKV_W + GF_W
EPS = 1e-6
T = 256
TM = 256
TQM = 512
Q_SCALE = 0.125
NEG = -1e30
UNDERFLOW = -110.0

ADAM_LR = 0.001
ADAM_B1 = 0.9
ADAM_B2 = 0.999
ADAM_EPS = 1e-08
ADAM_WD = 0.01
ADAM_STEP = 10

SMALL_ROWS = 72
LOSS_ROW = 66

_NT = (((1,), (1,)), ((), ()))
_TN = (((0,), (0,)), ((), ()))

_ARB1 = pltpu.CompilerParams(dimension_semantics=("arbitrary",))
_ARB2 = pltpu.CompilerParams(dimension_semantics=("arbitrary", "arbitrary"))


def _dot(a, b):
    return jnp.dot(a, b, preferred_element_type=F32)


def _dot_nt(a, b):
    return lax.dot_general(a, b, _NT, preferred_element_type=F32)


def _dot_tn(a, b):
    return lax.dot_general(a, b, _TN, preferred_element_type=F32)


def _split2(x):
    hi = x.astype(BF16)
    lo = (x - hi.astype(F32)).astype(BF16)
    return hi, lo


def _stack2(u):
    return jnp.concatenate([u, u], axis=0)


def _cum2(x, u2):
    hi, lo = _split2(x)
    return _dot(jnp.concatenate([hi, lo], axis=1), u2)


def _tri3(tri, x, dot=None):
    dot = dot or _dot
    hi = x.astype(BF16)
    r1 = x - hi.astype(F32)
    mid = r1.astype(BF16)
    lo = (r1 - mid.astype(F32)).astype(BF16)
    return dot(tri, hi) + dot(tri, mid) + dot(tri, lo)


def _iota2(shape, dim):
    return lax.broadcasted_iota(jnp.int32, shape, dim)


def _head_block_diag():
    r = _iota2((LANES, LANES), 0) // HEAD_DIM
    c = _iota2((LANES, LANES), 1) // HEAD_DIM
    return _stack2(jnp.where(r == c, 1.0, 0.0).astype(BF16))


def _head_mean(x, bd):
    return _cum2(x, bd) * (1.0 / HEAD_DIM)


def _sigmoid(x):
    return 1.0 / (1.0 + jnp.exp(-x))


def _log_sigmoid(x):
    return jnp.minimum(x, 0.0) - jnp.log(1.0 + jnp.exp(-jnp.abs(x)))


def _running_top(r_ref):
    return jnp.max(jnp.maximum(r_ref[0], r_ref[1]))


def _head_norm_tops(x, m0):
    x2 = x.astype(F32)
    x2 = x2 * x2
    zero = jnp.zeros_like(x2)
    return (jnp.max(jnp.sqrt(jnp.sum(jnp.where(m0, x2, zero), axis=1, keepdims=True))),
            jnp.max(jnp.sqrt(jnp.sum(jnp.where(m0, zero, x2), axis=1, keepdims=True))))


def _fox_tiles_left(i, pair, qn, kn_ref, fb_ref, tile):
    def bound(j):
        b = [2.0 * qn[h] * kn_ref[h] + fb_ref[2 * i, 2 * pair + h] - fb_ref[2 * j + 1, 2 * pair + h]
             for h in range(2)]
        return jnp.maximum(b[0], b[1])

    def more(j):
        return jnp.logical_and(j >= 0, bound(jnp.maximum(j, 0)) > UNDERFLOW)

    def step(j):
        tile(j, False)
        return j - 1

    return lax.while_loop(more, step, i - 1)


def _pair_masks():
    lane = _iota2((1, LANES), 1)
    return lane < HEAD_DIM


def _split_pair(x, m0):
    zero = jnp.zeros_like(x)
    return jnp.where(m0, x, zero), jnp.where(m0, zero, x)


def _inproj_fwd(x, nw, w_r, name):
    s = x.shape[0]

    def body(x_ref, nw_ref, w_ref, qkv_ref, gf_ref):
        xv = x_ref[...]
        r = lax.rsqrt(jnp.mean(xv * xv, axis=-1, keepdims=True) + EPS)
        h = (xv * r * nw_ref[...]).astype(BF16)
        for c in range(0, QKV_W, 256):
            qkv_ref[:, c:c + 256] = _dot_nt(h, w_ref[c:c + 256, :]).astype(BF16)
        for c in range(0, GF_W, 256):
            gf_ref[:, c:c + 256] = _dot_nt(h, w_ref[QKV_W + c:QKV_W + c + 256, :])

    return pl.pallas_call(
        body, name=name, grid=(s // TM,),
        in_specs=[pl.BlockSpec((TM, D_MODEL), lambda i: (i, 0)),
                  pl.BlockSpec((1, D_MODEL), lambda i: (0, 0)),
                  pl.BlockSpec((WR_W, D_MODEL), lambda i: (0, 0))],
        out_specs=[pl.BlockSpec((TM, QKV_W), lambda i: (i, 0)),
                   pl.BlockSpec((TM, GF_W), lambda i: (i, 0))],
        out_shape=[jax.ShapeDtypeStruct((s, QKV_W), BF16), jax.ShapeDtypeStruct((s, GF_W), F32)],
        compiler_params=_ARB1,
    )(x, nw, w_r)


def _fox_prep_fwd(gf, bpad, name):
    s = gf.shape[0]

    def body(fl_ref, b_ref, fq_ref, fr_ref, fb_ref):
        tri = jnp.where(_iota2((T, T), 0) >= _iota2((T, T), 1), 1.0, 0.0).astype(BF16)
        m0 = _pair_masks()
        carry = jnp.zeros((1, LANES), F32)
        for blk in range(s // T):
            rows = slice(blk * T, (blk + 1) * T)
            lf = _log_sigmoid(fl_ref[rows, :] + b_ref[...])
            c = _tri3(tri, lf) + carry
            carry = c[T - 1:T, :]
            for p in range(FOX_HEADS // 2):
                fq_ref[rows, p * LANES:(p + 1) * LANES] = jnp.where(
                    m0, c[:, 2 * p:2 * p + 1], c[:, 2 * p + 1:2 * p + 2])
            fr_ref[:, rows] = c.T[0:FOX_HEADS, :]
            fb_ref[2 * blk:2 * blk + 1, :] = c[0:1, :]
            fb_ref[2 * blk + 1:2 * blk + 2, :] = carry

    return pl.pallas_call(
        body, name=name, grid=(1,),
        in_specs=[pl.BlockSpec((s, LANES), lambda i: (0, MIX_W // LANES)),
                  pl.BlockSpec((1, LANES), lambda i: (0, 0))],
        out_specs=[pl.BlockSpec((s, FX_W), lambda i: (0, 0)),
                   pl.BlockSpec((FOX_HEADS, s), lambda i: (0, 0)),
                   pl.BlockSpec((2 * s // T, LANES), lambda i: (0, 0))],
        out_shape=[jax.ShapeDtypeStruct((s, FX_W), F32), jax.ShapeDtypeStruct((FOX_HEADS, s), F32),
                   jax.ShapeDtypeStruct((2 * s // T, LANES), F32)],
        compiler_params=_ARB1,
    )(gf, bpad)


def _sb_fwd(qkv, name, rider=None):
    s = qkv.shape[0]

    def body(q_ref, k_ref, v_ref, o_ref, acc_ref, r_ref, as_ref):
        i = pl.program_id(1)
        m0 = _pair_masks()
        qh = _split_pair(q_ref[...] * jnp.asarray(Q_SCALE, BF16), m0)
        strict = _iota2((T, T), 0) > _iota2((T, T), 1)
        u2 = _stack2(jnp.where(strict, 1.0, 0.0).astype(BF16))
        acc_ref[...] = jnp.zeros_like(acc_ref)
        r_ref[...] = jnp.zeros_like(r_ref)
        hs = range(2)

        def flush(j):
            v = v_ref[pl.ds(pl.multiple_of(j * T, T), T), :]
            for h in hs:
                acc_ref[h] += _dot(as_ref[h], v)

        def tile(j, diag):
            k = k_ref[pl.ds(pl.multiple_of(j * T, T), T), :]
            z = [_dot_nt(qh[h], k) for h in hs]
            if not diag:
                flush(j + 1)
            la = [jnp.minimum(z[h], 0.0) - jnp.log(1.0 + jnp.exp(-jnp.abs(z[h]))) for h in hs]
            lf = [la[h] - z[h] for h in hs]
            if diag:
                lf = [jnp.where(strict, lf[h], 0.0) for h in hs]
            cin = [_cum2(lf[h], u2) for h in hs]
            a = [jnp.exp(la[h] + cin[h] + r_ref[h]) for h in hs]
            if diag:
                a = [jnp.where(strict, a[h], 0.0) for h in hs]
            for h in hs:
                r_ref[h] += cin[h][:, 0:1] + lf[h][:, 0:1]
                as_ref[h] = a[h].astype(BF16)

        tile(i, True)

        def more(state):
            j, top = state
            return jnp.logical_and(j >= 0, top > UNDERFLOW)

        def step(state):
            j, _ = state
            tile(j, False)
            return j - 1, _running_top(r_ref)

        j_left, _ = lax.while_loop(more, step, (i - 1, _running_top(r_ref)))
        flush(j_left + 1)
        o_ref[...] = jnp.where(m0, acc_ref[0], acc_ref[1])

    nb = SB_W // LANES
    (ysb,), lands = _ride(dict(
        body=body, name=name, grid=(nb, s // T),
        in_specs=[pl.BlockSpec((T, LANES), lambda p, i: (i, p)),
                  pl.BlockSpec((s, LANES), lambda p, i: (0, nb + p)),
                  pl.BlockSpec((s, LANES), lambda p, i: (0, 2 * nb + p))],
        out_specs=[pl.BlockSpec((T, LANES), lambda p, i: (i, p))],
        out_shape=[jax.ShapeDtypeStruct((s, SB_W), F32)],
        scratch_shapes=[pltpu.VMEM((2, T, LANES), F32), pltpu.VMEM((2, T, 1), F32),
                        pltpu.VMEM((2, T, T), BF16)],
        compiler_params=_ARB2, operands=[qkv, qkv, qkv]), rider)
    return ysb, lands


def _fox_fwd(qkv, fqb, frow, fbounds, name, rider=None):
    s = qkv.shape[0]

    def body(q_ref, k_ref, v_ref, fq_ref, fr_ref, fb_ref, o_ref, lse_ref, acc_ref, m_ref, ps_ref,
             kn_ref):
        pair = pl.program_id(0)
        i = pl.program_id(1)
        m0 = _pair_masks()

        @pl.when(i == 0)
        def _():
            kn_ref[0], kn_ref[1] = _head_norm_tops(k_ref[...], m0)

        q = q_ref[...] * jnp.asarray(Q_SCALE, BF16)
        qn = _head_norm_tops(q, m0)
        qh = _split_pair(q, m0)
        fq = fq_ref[...]
        fqh = (fq[:, 0:1], fq[:, HEAD_DIM:HEAD_DIM + 1])
        causal = _iota2((T, T), 0) >= _iota2((T, T), 1)
        ones = jnp.ones((T, LANES), BF16)
        acc_ref[...] = jnp.zeros_like(acc_ref)
        m_ref[...] = jnp.full_like(m_ref, NEG)
        hs = range(2)

        def flush(j):
            v = v_ref[pl.ds(pl.multiple_of(j * T, T), T), :]
            va2 = _stack2(jnp.concatenate([v, ones], axis=1))
            for h in hs:
                acc_ref[h] += _dot(ps_ref[h], va2)

        def tile(j, diag):
            off = pl.multiple_of(j * T, T)
            k = k_ref[pl.ds(off, T), :]
            sc = [_dot_nt(qh[h], k) + fqh[h] - fr_ref[h:h + 1, pl.ds(off, T)] for h in hs]
            if not diag:
                flush(j + 1)
            if diag:
                sc = [jnp.where(causal, sc[h], NEG) for h in hs]
            m_new = [jnp.maximum(m_ref[h], jnp.max(sc[h], axis=1, keepdims=True)) for h in hs]
            p = [jnp.exp(sc[h] - m_new[h]) for h in hs]
            for h in hs:
                acc_ref[h] = acc_ref[h] * jnp.exp(m_ref[h] - m_new[h])
                m_ref[h] = m_new[h]
                p_hi, p_lo = _split2(p[h])
                ps_ref[h] = jnp.concatenate([p_hi, p_lo], axis=1)

        tile(i, True)
        j_left = _fox_tiles_left(i, pair, qn, kn_ref, fb_ref, tile)
        flush(j_left + 1)
        acc = (acc_ref[0], acc_ref[1])
        o_ref[...] = jnp.where(m0, acc[0][:, :LANES] / acc[0][:, LANES:],
                               acc[1][:, :LANES] / acc[1][:, LANES:])
        lse_ref[...] = jnp.where(m0, m_ref[0] + jnp.log(acc[0][:, LANES:]),
                                 m_ref[1] + jnp.log(acc[1][:, LANES:]))

    nb = FX_W // LANES
    base = 3 * SB_W // LANES
    (yfx, lse), lands = _ride(dict(
        body=body, name=name, grid=(nb, s // T),
        in_specs=[pl.BlockSpec((T, LANES), lambda p, i: (i, base + p)),
                  pl.BlockSpec((s, LANES), lambda p, i: (0, base + nb + p)),
                  pl.BlockSpec((s, LANES), lambda p, i: (0, base + 2 * nb + p)),
                  pl.BlockSpec((T, LANES), lambda p, i: (i, p)),
                  pl.BlockSpec((None, 2, s), lambda p, i: (p, 0, 0)),
                  pl.BlockSpec(memory_space=pltpu.SMEM)],
        out_specs=[pl.BlockSpec((T, LANES), lambda p, i: (i, p)),
                   pl.BlockSpec((T, LANES), lambda p, i: (i, p))],
        out_shape=[jax.ShapeDtypeStruct((s, FX_W), F32), jax.ShapeDtypeStruct((s, FX_W), F32)],
        scratch_shapes=[pltpu.VMEM((2, T, 2 * LANES), F32), pltpu.VMEM((2, T, 1), F32),
                        pltpu.VMEM((2, T, 2 * T), BF16), pltpu.SMEM((2,), F32)],
        compiler_params=_ARB2, operands=[qkv, qkv, qkv, fqb, frow, fbounds]), rider)
    return yfx, lse, lands


def _memkv_fwd(mem, mnw, wkv, name):
    n = mem.shape[0]

    def body(mem_ref, mnw_ref, w_ref, kv_ref):
        mv = mem_ref[...]
        r = lax.rsqrt(jnp.mean(mv * mv, axis=-1, keepdims=True) + EPS)
        hm = (mv * r * mnw_ref[...]).astype(BF16)
        kv_ref[...] = _dot(hm, w_ref[...]).astype(BF16)

    return pl.pallas_call(
        body, name=name, grid=(1,),
        in_specs=[pl.BlockSpec((n, D_MODEL), lambda i: (0, 0)),
                  pl.BlockSpec((1, D_MODEL), lambda i: (0, 0)),
                  pl.BlockSpec((D_MODEL, 2 * MEM_W), lambda i: (0, 0))],
        out_specs=pl.BlockSpec((n, 2 * MEM_W), lambda i: (0, 0)),
        out_shape=jax.ShapeDtypeStruct((n, 2 * MEM_W), BF16),
        compiler_params=_ARB1,
    )(mem, mnw, wkv)


def _mem_fwd(qkv, kv, name):
    s = qkv.shape[0]
    n = kv.shape[0]

    def body(q_ref, k_ref, v_ref, o_ref, lse_ref):
        m0 = _pair_masks()
        qh = _split_pair(q_ref[...] * jnp.asarray(Q_SCALE, BF16), m0)
        k = k_ref[...]
        v = v_ref[...]
        outs, lses = [], []
        for h in range(2):
            sc = _dot_nt(qh[h], k)
            mx = jnp.max(sc, axis=1, keepdims=True)
            p = jnp.exp(sc - mx)
            l = jnp.sum(p, axis=1, keepdims=True)
            outs.append(_dot(p.astype(BF16), v) / l)
            lses.append(mx + jnp.log(l))
        o_ref[...] = jnp.where(m0, outs[0], outs[1])
        lse_ref[...] = jnp.where(m0, lses[0], lses[1])

    nb = MEM_W // LANES
    base = (3 * SB_W + 3 * FX_W) // LANES
    return pl.pallas_call(
        body, name=name, grid=(nb, s // TQM),
        in_specs=[pl.BlockSpec((TQM, LANES), lambda p, i: (i, base + p)),
                  pl.BlockSpec((n, LANES), lambda p, i: (0, p)),
                  pl.BlockSpec((n, LANES), lambda p, i: (0, nb + p))],
        out_specs=[pl.BlockSpec((TQM, LANES), lambda p, i: (i, p)),
                   pl.BlockSpec((TQM, LANES), lambda p, i: (i, p))],
        out_shape=[jax.ShapeDtypeStruct((s, MEM_W), F32), jax.ShapeDtypeStruct((s, MEM_W), F32)],
        compiler_params=_ARB2,
    )(qkv, kv, kv)


def _mix_chunk(c, ysb_ref, yfx_ref, ym_ref):
    if c < SB_W // LANES:
        return ysb_ref[:, c * LANES:(c + 1) * LANES]
    c -= SB_W // LANES
    if c < FX_W // LANES:
        return yfx_ref[:, c * LANES:(c + 1) * LANES]
    c -= FX_W // LANES
    return ym_ref[:, c * LANES:(c + 1) * LANES]


def _outproj_fwd(ysb, yfx, ym, gf, onw, wout, x, name):
    s = x.shape[0]

    def body(ysb_ref, yfx_ref, ym_ref, g_ref, onw_ref, w_ref, x_ref, o_ref, yg_ref):
        bd = _head_block_diag()
        for c in range(MIX_W // LANES):
            sl = slice(c * LANES, (c + 1) * LANES)
            u = _mix_chunk(c, ysb_ref, yfx_ref, ym_ref)
            r = lax.rsqrt(_head_mean(u * u, bd) + EPS)
            g = g_ref[:, sl]
            yg_ref[:, sl] = (u * r * onw_ref[:, sl] * (g * _sigmoid(g))).astype(BF16)
        o_ref[...] = x_ref[...] + _dot(yg_ref[...], w_ref[...])

    return pl.pallas_call(
        body, name=name, grid=(s // TM,),
        in_specs=[pl.BlockSpec((TM, SB_W), lambda i: (i, 0)),
                  pl.BlockSpec((TM, FX_W), lambda i: (i, 0)),
                  pl.BlockSpec((TM, MEM_W), lambda i: (i, 0)),
                  pl.BlockSpec((TM, MIX_W), lambda i: (i, 0)),
                  pl.BlockSpec((1, MIX_W), lambda i: (0, 0)),
                  pl.BlockSpec((MIX_W, D_MODEL), lambda i: (0, 0)),
                  pl.BlockSpec((TM, D_MODEL), lambda i: (i, 0))],
        out_specs=pl.BlockSpec((TM, D_MODEL), lambda i: (i, 0)),
        out_shape=jax.ShapeDtypeStruct((s, D_MODEL), F32),
        scratch_shapes=[pltpu.VMEM((TM, MIX_W), BF16)],
        compiler_params=_ARB1,
    )(ysb, yfx, ym, gf, onw, wout, x)


def _final_fwd_bwd(x, fnw, target, name):
    s = x.shape[0]

    def body(x_ref, w_ref, t_ref, dx_ref, loss_ref, dw_ref):
        @pl.when(pl.program_id(0) == 0)
        def _():
            loss_ref[...] = jnp.zeros_like(loss_ref)
            dw_ref[...] = jnp.zeros_like(dw_ref)

        xv = x_ref[...]
        w = w_ref[...]
        r = lax.rsqrt(jnp.mean(xv * xv, axis=-1, keepdims=True) + EPS)
        xh = xv * r
        err = xh * w - t_ref[...]
        part = jnp.sum(jnp.sum(err * err, axis=1, keepdims=True), axis=0, keepdims=True)
        loss_ref[...] += part * (0.5 / D_MODEL)
        dy = err * (1.0 / D_MODEL)
        dw_ref[...] += jnp.sum(dy * xh, axis=0, keepdims=True)
        dxh = dy * w
        dx_ref[...] = r * (dxh - xh * jnp.mean(dxh * xh, axis=-1, keepdims=True))

    return pl.pallas_call(
        body, name=name, grid=(s // TM,),
        in_specs=[pl.BlockSpec((TM, D_MODEL), lambda i: (i, 0)),
                  pl.BlockSpec((1, D_MODEL), lambda i: (0, 0)),
                  pl.BlockSpec((TM, D_MODEL), lambda i: (i, 0))],
        out_specs=[pl.BlockSpec((TM, D_MODEL), lambda i: (i, 0)),
                   pl.BlockSpec((1, LANES), lambda i: (0, 0)),
                   pl.BlockSpec((1, D_MODEL), lambda i: (0, 0))],
        out_shape=[jax.ShapeDtypeStruct((s, D_MODEL), F32), jax.ShapeDtypeStruct((1, LANES), F32),
                   jax.ShapeDtypeStruct((1, D_MODEL), F32)],
        compiler_params=_ARB1,
    )(x, fnw, target)


def _outproj_bwd(dxo, wout, ysb, yfx, ym, gf, onw, name):
    s = dxo.shape[0]

    def body(dx_ref, w_ref, ysb_ref, yfx_ref, ym_ref, g_ref, onw_ref,
             dysb_ref, dyfx_ref, dym_ref, dg_ref, dw_ref, donw_ref, yg_ref):
        @pl.when(pl.program_id(0) == 0)
        def _():
            dw_ref[...] = jnp.zeros_like(dw_ref)
            donw_ref[...] = jnp.zeros_like(donw_ref)

        dxb = dx_ref[...].astype(BF16)
        dyg = _dot_nt(dxb, w_ref[...])
        bd = _head_block_diag()
        for c in range(MIX_W // LANES):
            sl = slice(c * LANES, (c + 1) * LANES)
            u = _mix_chunk(c, ysb_ref, yfx_ref, ym_ref)
            r = lax.rsqrt(_head_mean(u * u, bd) + EPS)
            yn = u * r
            g = g_ref[:, sl]
            sg = _sigmoid(g)
            sil = g * sg
            onw = onw_ref[:, sl]
            e = dyg[:, sl]
            yg_ref[:, sl] = (yn * onw * sil).astype(BF16)
            donw_ref[:, sl] += jnp.sum(e * yn * sil, axis=0, keepdims=True)
            dg_ref[:, sl] = (e * yn * onw * (sg * (1.0 + g * (1.0 - sg)))).astype(BF16)
            dyn = e * onw * sil
            du = (r * (dyn - yn * _head_mean(dyn * yn, bd))).astype(BF16)
            if c < 4:
                dysb_ref[:, c * LANES:(c + 1) * LANES] = du
            elif c < 8:
                dyfx_ref[:, (c - 4) * LANES:(c - 3) * LANES] = du
            else:
                dym_ref[:, (c - 8) * LANES:(c - 7) * LANES] = du
        dw_ref[...] += _dot_tn(yg_ref[...], dxb)

    return pl.pallas_call(
        body, name=name, grid=(s // TM,),
        in_specs=[pl.BlockSpec((TM, D_MODEL), lambda i: (i, 0)),
                  pl.BlockSpec((MIX_W, D_MODEL), lambda i: (0, 0)),
                  pl.BlockSpec((TM, SB_W), lambda i: (i, 0)),
                  pl.BlockSpec((TM, FX_W), lambda i: (i, 0)),
                  pl.BlockSpec((TM, MEM_W), lambda i: (i, 0)),
                  pl.BlockSpec((TM, MIX_W), lambda i: (i, 0)),
                  pl.BlockSpec((1, MIX_W), lambda i: (0, 0))],
        out_specs=[pl.BlockSpec((TM, SB_W), lambda i: (i, 0)),
                   pl.BlockSpec((TM, FX_W), lambda i: (i, 0)),
                   pl.BlockSpec((TM, MEM_W), lambda i: (i, 0)),
                   pl.BlockSpec((TM, MIX_W), lambda i: (i, 0)),
                   pl.BlockSpec((MIX_W, D_MODEL), lambda i: (0, 0)),
                   pl.BlockSpec((1, MIX_W), lambda i: (0, 0))],
        out_shape=[jax.ShapeDtypeStruct((s, SB_W), BF16), jax.ShapeDtypeStruct((s, FX_W), BF16),
                   jax.ShapeDtypeStruct((s, MEM_W), BF16), jax.ShapeDtypeStruct((s, MIX_W), BF16),
                   jax.ShapeDtypeStruct((MIX_W, D_MODEL), F32), jax.ShapeDtypeStruct((1, MIX_W), F32)],
        scratch_shapes=[pltpu.VMEM((TM, MIX_W), BF16)],
        compiler_params=_ARB1,
    )(dxo, wout, ysb, yfx, ym, gf, onw)


def _row_dots(do_ref, o_ref, m0):
    prod = do_ref[...].astype(F32) * o_ref[...]
    zero = jnp.zeros_like(prod)
    return (jnp.sum(jnp.where(m0, prod, zero), axis=1, keepdims=True),
            jnp.sum(jnp.where(m0, zero, prod), axis=1, keepdims=True))


def _sb_bwd(qkv, o, do, name, rider=None):
    s = qkv.shape[0]
    nq = s // T

    def body(q_ref, k_ref, v_ref, o_ref, do_ref, dq_ref, dk_ref, dv_ref,
             dqa_ref, dka_ref, dva_ref, rl_ref, rg_ref, dzs_ref, abs_ref):
        i = pl.program_id(1)

        @pl.when(i == 0)
        def _():
            dka_ref[...] = jnp.zeros_like(dka_ref)
            dva_ref[...] = jnp.zeros_like(dva_ref)

        m0 = _pair_masks()
        qh = _split_pair(q_ref[...] * jnp.asarray(Q_SCALE, BF16), m0)
        doh = _split_pair(do_ref[...], m0)
        dsum = _row_dots(do_ref, o_ref, m0)
        strict = _iota2((T, T), 0) > _iota2((T, T), 1)
        u2 = _stack2(jnp.where(strict, 1.0, 0.0).astype(BF16))
        dqa_ref[...] = jnp.zeros_like(dqa_ref)
        rl_ref[...] = jnp.zeros_like(rl_ref)
        rg_ref[...] = jnp.zeros_like(rg_ref)

        hs = range(2)

        def flush(j):
            off = pl.multiple_of(j * T, T)
            k = k_ref[pl.ds(off, T), :]
            for h in hs:
                dqa_ref[h] += _dot(dzs_ref[h], k)
            dka_ref[pl.ds(off, T), :] += _dot_tn(dzs_ref[0], qh[0]) + _dot_tn(dzs_ref[1], qh[1])
            dva_ref[pl.ds(off, T), :] += _dot_tn(abs_ref[0], doh[0]) + _dot_tn(abs_ref[1], doh[1])

        def tile(j, diag):
            off = pl.multiple_of(j * T, T)
            k = k_ref[pl.ds(off, T), :]
            v = v_ref[pl.ds(off, T), :]
            z = [_dot_nt(qh[h], k) for h in hs]
            da = [_dot_nt(doh[h], v) for h in hs]
            if not diag:
                flush(j + 1)
            la = [jnp.minimum(z[h], 0.0) - jnp.log(1.0 + jnp.exp(-jnp.abs(z[h]))) for h in hs]
            lf = [la[h] - z[h] for h in hs]
            if diag:
                lf = [jnp.where(strict, lf[h], 0.0) for h in hs]
            cin = [_cum2(lf[h], u2) for h in hs]
            a = [jnp.exp(la[h] + cin[h] + rl_ref[h]) for h in hs]
            if diag:
                a = [jnp.where(strict, a[h], 0.0) for h in hs]
            ab = [a[h].astype(BF16) for h in hs]
            g = [ab[h].astype(F32) * da[h] for h in hs]
            gin = [_cum2(g[h], u2) for h in hs]
            dz = [g[h] - jnp.exp(la[h]) * ((dsum[h] - rg_ref[h]) - gin[h]) for h in hs]
            if diag:
                dz = [jnp.where(strict, dz[h], 0.0) for h in hs]
            for h in hs:
                rl_ref[h] += cin[h][:, 0:1] + lf[h][:, 0:1]
                rg_ref[h] += gin[h][:, 0:1] + g[h][:, 0:1]
                dzs_ref[h] = dz[h].astype(BF16)
                abs_ref[h] = ab[h]

        tile(i, True)

        def more(state):
            j, top = state
            return jnp.logical_and(j >= 0, top > UNDERFLOW)

        def step(state):
            j, _ = state
            tile(j, False)
            return j - 1, _running_top(rl_ref)

        j_left, _ = lax.while_loop(more, step, (i - 1, _running_top(rl_ref)))
        flush(j_left + 1)
        dq_ref[...] = (jnp.where(m0, dqa_ref[0], dqa_ref[1]) * Q_SCALE).astype(BF16)

        @pl.when(i == nq - 1)
        def _():
            dk_ref[...] = dka_ref[...].astype(BF16)
            dv_ref[...] = dva_ref[...].astype(BF16)

    nb = SB_W // LANES
    (dq, dk, dv), lands = _ride(dict(
        body=body, name=name, grid=(nb, nq),
        in_specs=[pl.BlockSpec((T, LANES), lambda p, i: (i, p)),
                  pl.BlockSpec((s, LANES), lambda p, i: (0, nb + p)),
                  pl.BlockSpec((s, LANES), lambda p, i: (0, 2 * nb + p)),
                  pl.BlockSpec((T, LANES), lambda p, i: (i, p)),
                  pl.BlockSpec((T, LANES), lambda p, i: (i, p))],
        out_specs=[pl.BlockSpec((T, LANES), lambda p, i: (i, p)),
                   pl.BlockSpec((s, LANES), lambda p, i: (0, p)),
                   pl.BlockSpec((s, LANES), lambda p, i: (0, p))],
        out_shape=[jax.ShapeDtypeStruct((s, SB_W), BF16)] * 3,
        scratch_shapes=[pltpu.VMEM((2, T, LANES), F32), pltpu.VMEM((s, LANES), F32),
                        pltpu.VMEM((s, LANES), F32), pltpu.VMEM((2, T, 1), F32),
                        pltpu.VMEM((2, T, 1), F32), pltpu.VMEM((2, T, T), BF16),
                        pltpu.VMEM((2, T, T), BF16)],
        compiler_params=_ARB2, operands=[qkv, qkv, qkv, o, do]), rider)
    return dq, dk, dv, lands


def _fox_bwd(qkv, fqb, frow, fbounds, o, lse, do, name, rider=None):
    s = qkv.shape[0]
    nq = s // T

    def body(q_ref, k_ref, v_ref, fq_ref, fr_ref, fb_ref, o_ref, lse_ref, do_ref,
             dq_ref, dk_ref, dv_ref, df_ref, dqa_ref, dka_ref, dva_ref, dfa_ref, dls_ref, pbs_ref,
             kn_ref):
        i = pl.program_id(1)

        @pl.when(i == 0)
        def _():
            dka_ref[...] = jnp.zeros_like(dka_ref)
            dva_ref[...] = jnp.zeros_like(dva_ref)
            dfa_ref[...] = jnp.zeros_like(dfa_ref)

        m0 = _pair_masks()

        @pl.when(i == 0)
        def _():
            kn_ref[0], kn_ref[1] = _head_norm_tops(k_ref[...], m0)

        q = q_ref[...] * jnp.asarray(Q_SCALE, BF16)
        qn = _head_norm_tops(q, m0)
        qh = _split_pair(q, m0)
        doh = _split_pair(do_ref[...], m0)
        dsum = _row_dots(do_ref, o_ref, m0)
        fq = fq_ref[...]
        fqh = (fq[:, 0:1], fq[:, HEAD_DIM:HEAD_DIM + 1])
        lse = lse_ref[...]
        lseh = (lse[:, 0:1], lse[:, HEAD_DIM:HEAD_DIM + 1])
        causal = _iota2((T, T), 0) >= _iota2((T, T), 1)
        dqa_ref[...] = jnp.zeros_like(dqa_ref)

        hs = range(2)

        def flush(j):
            off = pl.multiple_of(j * T, T)
            k = k_ref[pl.ds(off, T), :]
            for h in hs:
                dqa_ref[h] += _dot(dls_ref[h], k)
            dka_ref[pl.ds(off, T), :] += _dot_tn(dls_ref[0], qh[0]) + _dot_tn(dls_ref[1], qh[1])
            dva_ref[pl.ds(off, T), :] += _dot_tn(pbs_ref[0], doh[0]) + _dot_tn(pbs_ref[1], doh[1])

        def tile(j, diag):
            off = pl.multiple_of(j * T, T)
            k = k_ref[pl.ds(off, T), :]
            v = v_ref[pl.ds(off, T), :]
            sc = [_dot_nt(qh[h], k) + fqh[h] - fr_ref[h:h + 1, pl.ds(off, T)] for h in hs]
            dp = [_dot_nt(doh[h], v) for h in hs]
            if not diag:
                flush(j + 1)
            p = [jnp.exp(sc[h] - lseh[h]) for h in hs]
            if diag:
                p = [jnp.where(causal, p[h], 0.0) for h in hs]
            dl = [p[h] * (dp[h] - dsum[h]) for h in hs]
            for h in hs:
                dls_ref[h] = dl[h].astype(BF16)
                pbs_ref[h] = p[h].astype(BF16)
                dfa_ref[h:h + 1, pl.ds(off, T)] -= jnp.sum(dl[h], axis=0, keepdims=True)

        tile(i, True)
        j_left = _fox_tiles_left(i, pl.program_id(0), qn, kn_ref, fb_ref, tile)
        flush(j_left + 1)
        dq_ref[...] = (jnp.where(m0, dqa_ref[0], dqa_ref[1]) * Q_SCALE).astype(BF16)

        @pl.when(i == nq - 1)
        def _():
            dk_ref[...] = dka_ref[...].astype(BF16)
            dv_ref[...] = dva_ref[...].astype(BF16)
            df_ref[...] = dfa_ref[...]

    nb = FX_W // LANES
    base = 3 * SB_W // LANES
    (dq, dk, dv, df), lands = _ride(dict(
        body=body, name=name, grid=(nb, nq),
        in_specs=[pl.BlockSpec((T, LANES), lambda p, i: (i, base + p)),
                  pl.BlockSpec((s, LANES), lambda p, i: (0, base + nb + p)),
                  pl.BlockSpec((s, LANES), lambda p, i: (0, base + 2 * nb + p)),
                  pl.BlockSpec((T, LANES), lambda p, i: (i, p)),
                  pl.BlockSpec((None, 2, s), lambda p, i: (p, 0, 0)),
                  pl.BlockSpec(memory_space=pltpu.SMEM),
                  pl.BlockSpec((T, LANES), lambda p, i: (i, p)),
                  pl.BlockSpec((T, LANES), lambda p, i: (i, p)),
                  pl.BlockSpec((T, LANES), lambda p, i: (i, p))],
        out_specs=[pl.BlockSpec((T, LANES), lambda p, i: (i, p)),
                   pl.BlockSpec((s, LANES), lambda p, i: (0, p)),
                   pl.BlockSpec((s, LANES), lambda p, i: (0, p)),
                   pl.BlockSpec((None, 2, s), lambda p, i: (p, 0, 0))],
        out_shape=[jax.ShapeDtypeStruct((s, FX_W), BF16)] * 3
        + [jax.ShapeDtypeStruct((nb, 2, s), F32)],
        scratch_shapes=[pltpu.VMEM((2, T, LANES), F32), pltpu.VMEM((s, LANES), F32),
                        pltpu.VMEM((s, LANES), F32), pltpu.VMEM((2, s), F32),
                        pltpu.VMEM((2, T, T), BF16), pltpu.VMEM((2, T, T), BF16),
                        pltpu.SMEM((2,), F32)],
        compiler_params=_ARB2, operands=[qkv, qkv, qkv, fqb, frow, fbounds, o, lse, do]), rider)
    return dq, dk, dv, df, lands


def _fox_prep_bwd(dfrow, gf, bpad, name):
    s = gf.shape[0]

    def body(df_ref, fl_ref, b_ref, dfl_ref, db_ref):
        tri = jnp.where(_iota2((T, T), 0) <= _iota2((T, T), 1), 1.0, 0.0).astype(BF16)
        carry = jnp.zeros((1, LANES), F32)
        db = jnp.zeros((1, LANES), F32)
        fill = jnp.zeros((LANES - FOX_HEADS, T), F32)
        for blk in reversed(range(s // T)):
            rows = slice(blk * T, (blk + 1) * T)
            c = _tri3(tri, jnp.concatenate([df_ref[:, rows], fill], axis=0), _dot_nt) + carry
            carry = c[0:1, :]
            dfl = c / (1.0 + jnp.exp(fl_ref[rows, :] + b_ref[...]))
            dfl_ref[rows, :] = dfl.astype(BF16)
            db = db + jnp.sum(dfl, axis=0, keepdims=True)
        db_ref[...] = db

    return pl.pallas_call(
        body, name=name, grid=(1,),
        in_specs=[pl.BlockSpec((FOX_HEADS, s), lambda i: (0, 0)),
                  pl.BlockSpec((s, LANES), lambda i: (0, MIX_W // LANES)),
                  pl.BlockSpec((1, LANES), lambda i: (0, 0))],
        out_specs=[pl.BlockSpec((s, LANES), lambda i: (0, 0)),
                   pl.BlockSpec((1, LANES), lambda i: (0, 0))],
        out_shape=[jax.ShapeDtypeStruct((s, LANES), BF16), jax.ShapeDtypeStruct((1, LANES), F32)],
        compiler_params=_ARB1,
    )(dfrow, gf, bpad)


def _mem_bwd(qkv, kv, o, lse, do, name):
    s = qkv.shape[0]
    n = kv.shape[0]

    def body(q_ref, k_ref, v_ref, o_ref, lse_ref, do_ref, dq_ref, dk_ref, dv_ref):
        @pl.when(pl.program_id(1) == 0)
        def _():
            dk_ref[...] = jnp.zeros_like(dk_ref)
            dv_ref[...] = jnp.zeros_like(dv_ref)

        m0 = _pair_masks()
        qh = _split_pair(q_ref[...] * jnp.asarray(Q_SCALE, BF16), m0)
        doh = _split_pair(do_ref[...], m0)
        dsum = _row_dots(do_ref, o_ref, m0)
        lse = lse_ref[...]
        lseh = (lse[:, 0:1], lse[:, HEAD_DIM:HEAD_DIM + 1])
        k = k_ref[...]
        v = v_ref[...]
        dqs = []
        for h in range(2):
            p = jnp.exp(_dot_nt(qh[h], k) - lseh[h])
            dl = p * (_dot_nt(doh[h], v) - dsum[h])
            dlb = dl.astype(BF16)
            dqs.append(_dot(dlb, k))
            dk_ref[...] += _dot_tn(dlb, qh[h])
            dv_ref[...] += _dot_tn(p.astype(BF16), doh[h])
        dq_ref[...] = (jnp.where(m0, dqs[0], dqs[1]) * Q_SCALE).astype(BF16)

    nb = MEM_W // LANES
    base = (3 * SB_W + 3 * FX_W) // LANES
    return pl.pallas_call(
        body, name=name, grid=(nb, s // TQM),
        in_specs=[pl.BlockSpec((TQM, LANES), lambda p, i: (i, base + p)),
                  pl.BlockSpec((n, LANES), lambda p, i: (0, p)),
                  pl.BlockSpec((n, LANES), lambda p, i: (0, nb + p)),
                  pl.BlockSpec((TQM, LANES), lambda p, i: (i, p)),
                  pl.BlockSpec((TQM, LANES), lambda p, i: (i, p)),
                  pl.BlockSpec((TQM, LANES), lambda p, i: (i, p))],
        out_specs=[pl.BlockSpec((TQM, LANES), lambda p, i: (i, p)),
                   pl.BlockSpec((n, LANES), lambda p, i: (0, p)),
                   pl.BlockSpec((n, LANES), lambda p, i: (0, p))],
        out_shape=[jax.ShapeDtypeStruct((s, MEM_W), BF16), jax.ShapeDtypeStruct((n, MEM_W), F32),
                   jax.ShapeDtypeStruct((n, MEM_W), F32)],
        compiler_params=_ARB2,
    )(qkv, kv, kv, o, lse, do)


def _memkv_bwd(mem, mnw, wkv, dk, dv, name):
    n = mem.shape[0]

    def body(mem_ref, mnw_ref, w_ref, dk_ref, dv_ref, dw_ref, dmnw_ref):
        mv = mem_ref[...]
        r = lax.rsqrt(jnp.mean(mv * mv, axis=-1, keepdims=True) + EPS)
        mh = mv * r
        hm = (mh * mnw_ref[...]).astype(BF16)
        dkv = jnp.concatenate([dk_ref[...], dv_ref[...]], axis=1).astype(BF16)
        dw_ref[...] = _dot_tn(hm, dkv)
        dhm = _dot_nt(dkv, w_ref[...])
        dmnw_ref[...] = jnp.sum(dhm * mh, axis=0, keepdims=True)

    return pl.pallas_call(
        body, name=name, grid=(1,),
        in_specs=[pl.BlockSpec((n, D_MODEL), lambda i: (0, 0)),
                  pl.BlockSpec((1, D_MODEL), lambda i: (0, 0)),
                  pl.BlockSpec((D_MODEL, 2 * MEM_W), lambda i: (0, 0)),
                  pl.BlockSpec((n, MEM_W), lambda i: (0, 0)),
                  pl.BlockSpec((n, MEM_W), lambda i: (0, 0))],
        out_specs=[pl.BlockSpec((D_MODEL, 2 * MEM_W), lambda i: (0, 0)),
                   pl.BlockSpec((1, D_MODEL), lambda i: (0, 0))],
        out_shape=[jax.ShapeDtypeStruct((D_MODEL, 2 * MEM_W), F32),
                   jax.ShapeDtypeStruct((1, D_MODEL), F32)],
        compiler_params=_ARB1,
    )(mem, mnw, wkv, dk, dv)


def _inproj_bwd_dx(pieces, w_r, x, nw, dxo, name):
    s = x.shape[0]
    n = len(pieces)
    widths = [p.shape[1] for p in pieces]

    def body(*refs):
        piece_refs = refs[:n]
        w_ref, x_ref, nw_ref, dxo_ref, dx_ref, h_ref, dnw_ref, dp_ref = refs[n:]

        @pl.when(pl.program_id(0) == 0)
        def _():
            dnw_ref[...] = jnp.zeros_like(dnw_ref)

        col = 0
        for r, wd in zip(piece_refs, widths):
            dp_ref[:, col:col + wd] = r[...]
            col += wd
        dp_ref[:, col:] = jnp.zeros((TM, WR_W - col), BF16)
        dh = _dot(dp_ref[...], w_ref[...])
        xv = x_ref[...]
        nw = nw_ref[...]
        r = lax.rsqrt(jnp.mean(xv * xv, axis=-1, keepdims=True) + EPS)
        xh = xv * r
        h_ref[...] = (xh * nw).astype(BF16)
        dnw_ref[...] += jnp.sum(dh * xh, axis=0, keepdims=True)
        dxh = dh * nw
        dx_ref[...] = r * (dxh - xh * jnp.mean(dxh * xh, axis=-1, keepdims=True)) + dxo_ref[...]

    return pl.pallas_call(
        body, name=name, grid=(s // TM,),
        in_specs=[pl.BlockSpec((TM, wd), lambda i: (i, 0)) for wd in widths]
        + [pl.BlockSpec((WR_W, D_MODEL), lambda i: (0, 0)),
           pl.BlockSpec((TM, D_MODEL), lambda i: (i, 0)),
           pl.BlockSpec((1, D_MODEL), lambda i: (0, 0)),
           pl.BlockSpec((TM, D_MODEL), lambda i: (i, 0))],
        out_specs=[pl.BlockSpec((TM, D_MODEL), lambda i: (i, 0)),
                   pl.BlockSpec((TM, D_MODEL), lambda i: (i, 0)),
                   pl.BlockSpec((1, D_MODEL), lambda i: (0, 0)),
                   pl.BlockSpec((TM, WR_W), lambda i: (i, 0))],
        out_shape=[jax.ShapeDtypeStruct((s, D_MODEL), F32), jax.ShapeDtypeStruct((s, D_MODEL), BF16),
                   jax.ShapeDtypeStruct((1, D_MODEL), F32), jax.ShapeDtypeStruct((s, WR_W), BF16)],
        compiler_params=_ARB1,
    )(*pieces, w_r, x, nw, dxo)


def _inproj_bwd_dw(h, dproj, name):
    s = dproj.shape[0]
    tn = 256

    def body(h_ref, dp_ref, dw_ref):
        dw_ref[...] = _dot_tn(dp_ref[...], h_ref[...])

    return pl.pallas_call(
        body, name=name, grid=(WR_W // tn,),
        in_specs=[pl.BlockSpec((s, D_MODEL), lambda j: (0, 0)),
                  pl.BlockSpec((s, tn), lambda j: (0, j))],
        out_specs=pl.BlockSpec((tn, D_MODEL), lambda j: (j, 0)),
        out_shape=jax.ShapeDtypeStruct((WR_W, D_MODEL), F32),
        compiler_params=_ARB1,
    )(h, dproj)


def _rearrange_w_in(wt):
    pad = jnp.zeros((FL_PAD - FOX_HEADS,) + wt.shape[1:], wt.dtype)
    return jnp.concatenate([wt[:3072], wt[3080:3336], wt[3336:IN_W], wt[3072:3080], pad], axis=0)


def _restore_w_in(g):
    gate0 = QKV_W
    fl0 = QKV_W + MIX_W
    return jnp.concatenate(
        [g[:3072], g[fl0:fl0 + FOX_HEADS], g[3072:QKV_W], g[gate0:fl0]], axis=0)


def _pad_lanes(v, width=LANES):
    return jnp.pad(v, (0, width - v.shape[0])).reshape(1, width)


def _layer_fwd(xs, mem, nw, w_r, b_forget, mnw, late, onw, l, travel=None):
    s = xs.shape[0]
    bpad = _pad_lanes(b_forget)
    qkv, gf = _inproj_fwd(xs, nw, w_r, f"inproj_fwd_{l}")
    fqb, frow, fbounds = _fox_prep_fwd(gf, bpad, f"fox_prep_fwd_{l}")
    frow = frow.reshape(FOX_HEADS // 2, 2, s)
    travel = _Travel(travel)
    ysb, _ = travel.ride(0, _sb_fwd, qkv, f"sb_fwd_{l}")
    yfx, lse_fx, _ = travel.ride(1, _fox_fwd, qkv, fqb, frow, fbounds, f"fox_fwd_{l}")
    wkv, wout = late(travel.lands)
    kv = _memkv_fwd(mem, mnw, wkv, f"memkv_fwd_{l}")
    ym, lse_m = _mem_fwd(qkv, kv, f"mem_fwd_{l}")
    xn = _outproj_fwd(ysb, yfx, ym, gf, onw, wout, xs, f"outproj_fwd_{l}")
    saved = (xs, nw, mnw, onw, bpad, qkv, gf, fqb, frow, fbounds, ysb, yfx, lse_fx, kv, ym, lse_m)
    return xn, saved, travel.lands, (wkv, wout)


class _Travel:
    def __init__(self, plan):
        self.plan = plan
        self.lands = None if plan is None else _new_lands(plan[0], plan[1])

    def ride(self, n, fn, *args):
        if self.plan is None or self.plan[2][n] is None:
            return fn(*args)
        srcs, scatter, legs = self.plan
        idx, rows = legs[n]
        out = fn(*args, rider=_Rider([srcs[a] for a in idx], [self.lands[a] for a in idx],
                                     scatter, rows))
        for a, land in zip(idx, out[-1]):
            self.lands[a] = land
        return out


def _layer_bwd(dx, saved, mem, w_r, wkv, wout, l, travel=None):
    xs, nw, mnw, onw, bpad, qkv, gf, fqb, frow, fbounds, ysb, yfx, lse_fx, kv, ym, lse_m = saved
    s = xs.shape[0]
    dysb, dyfx, dym, dgate, dwout, donw = _outproj_bwd(
        dx, wout, ysb, yfx, ym, gf, onw, f"outproj_bwd_{l}")
    travel = _Travel(None if travel is None else travel(dwout))
    sdq, sdk, sdv, _ = travel.ride(0, _sb_bwd, qkv, ysb, dysb, f"sb_bwd_{l}")
    fdq, fdk, fdv, dfrow, _ = travel.ride(1, _fox_bwd, qkv, fqb, frow, fbounds, yfx, lse_fx, dyfx,
                                          f"fox_bwd_{l}")
    dfl, db = _fox_prep_bwd(dfrow.reshape(FOX_HEADS, s), gf, bpad, f"fox_prep_bwd_{l}")
    dmq, dmk, dmv = _mem_bwd(qkv, kv, ym, lse_m, dym, f"mem_bwd_{l}")
    dwkv, dmnw = _memkv_bwd(mem, mnw, wkv, dmk, dmv, f"memkv_bwd_{l}")
    dx, ht, dnw, dproj = _inproj_bwd_dx([sdq, sdk, sdv, fdq, fdk, fdv, dmq, dgate, dfl],
                                        w_r, xs, nw, dx, f"inproj_bwd_dx_{l}")
    dwr = _inproj_bwd_dw(ht, dproj, f"inproj_bwd_dw_{l}")
    grads = dict(norm_w=dnw[0], w_r=dwr, b_forget=db[0, :FOX_HEADS], mem_norm_w=dmnw[0],
                 w_mem_kv=dwkv, out_norm_w=donw[0], w_out=dwout)
    return dx, grads, travel.lands


_ANY = pl.BlockSpec(memory_space=pl.ANY)


def _my_place():
    return lax.axis_index("x"), lax.axis_index("y"), lax.axis_index("c")


def _flip(v, bit):
    return 1 - v if bit else v


def _block_index(px, py, pc):
    return 4 * px + 2 * py + pc


def _all_gather_weights(shards, name):
    n = len(shards)

    def body(*refs):
        ins, outs = refs[:n], refs[n:2 * n]
        send_sems, recv_sems, local_sems = refs[2 * n:]
        x, y, c = _my_place()
        me = (x, y, c)
        sibling = (x, y, 1 - c)
        chips = [(1 - x, y), (x, 1 - y), (1 - x, 1 - y)]

        def copy(a, k, block, to, src=None):
            dst = outs[a].at[_block_index(*block)]
            return pltpu.make_async_remote_copy(
                src_ref=dst if src is None else src, dst_ref=dst,
                send_sem=send_sems.at[a, k], recv_sem=recv_sems.at[a, k],
                device_id=to, device_id_type=pl.DeviceIdType.MESH)

        mine = [pltpu.make_async_copy(ins[a], outs[a].at[_block_index(*me)], local_sems.at[a])
                for a in range(n)]
        for cp in mine:
            cp.start()
        first = []
        for a in range(n):
            first.append(copy(a, 0, me, sibling, src=ins[a]))
            first += [copy(a, 1 + j, me, (*chip, c), src=ins[a]) for j, chip in enumerate(chips)]
        for cp in first:
            cp.start()
        passed = []
        for j, chip in enumerate(chips):
            for a in range(n):
                copy(a, 1 + j, (*chip, c), me).wait_recv()
                fwd = copy(a, 4 + j, (*chip, c), sibling)
                fwd.start()
                passed.append(fwd)
        for a in range(n):
            copy(a, 0, sibling, me).wait_recv()
            for j, chip in enumerate(chips):
                copy(a, 4 + j, (*chip, 1 - c), me).wait_recv()
        for cp in first + passed:
            cp.wait_send()
        for cp in mine:
            cp.wait()

    return pl.pallas_call(
        body, name=name,
        in_specs=[_ANY] * n, out_specs=[_ANY] * n,
        out_shape=[jax.ShapeDtypeStruct((N_DEV,) + v.shape, v.dtype) for v in shards],
        scratch_shapes=[pltpu.SemaphoreType.DMA((n, 7)), pltpu.SemaphoreType.DMA((n, 7)),
                        pltpu.SemaphoreType.DMA((n,))],
    )(*shards)


def _exchange_blocks(blocked, name):
    n = len(blocked)

    def body(*refs):
        ins, outs = refs[:n], refs[n:2 * n]
        send_sems, recv_sems, local_sems = refs[2 * n:]
        x, y, c = _my_place()
        mine_idx = _block_index(x, y, c)
        local = [pltpu.make_async_copy(ins[a].at[mine_idx], outs[a].at[mine_idx], local_sems.at[a])
                 for a in range(n)]
        for cp in local:
            cp.start()
        sends, arrivals = [], []
        for r in range(1, N_DEV):
            peer = (_flip(x, r & 4), _flip(y, r & 2), _flip(c, r & 1))
            peer_idx = _block_index(*peer)
            for a in range(n):
                sems = dict(send_sem=send_sems.at[a, r - 1], recv_sem=recv_sems.at[a, r - 1],
                            device_id=peer, device_id_type=pl.DeviceIdType.MESH)
                sends.append(pltpu.make_async_remote_copy(
                    src_ref=ins[a].at[peer_idx], dst_ref=outs[a].at[mine_idx], **sems))
                arrivals.append(pltpu.make_async_remote_copy(
                    src_ref=ins[a].at[peer_idx], dst_ref=outs[a].at[peer_idx], **sems))
        for cp in sends:
            cp.start()
        for cp in arrivals:
            cp.wait_recv()
        for cp in sends:
            cp.wait_send()
        for cp in local:
            cp.wait()

    return pl.pallas_call(
        body, name=name,
        in_specs=[_ANY] * n, out_specs=[_ANY] * n,
        out_shape=[jax.ShapeDtypeStruct(v.shape, v.dtype) for v in blocked],
        scratch_shapes=[pltpu.SemaphoreType.DMA((n, 7)), pltpu.SemaphoreType.DMA((n, 7)),
                        pltpu.SemaphoreType.DMA((n,))],
    )(*blocked)


class _Rider(NamedTuple):
    srcs: list
    lands: list
    scatter: bool
    part: list


def _new_lands(srcs, scatter):
    return [lax.empty(v.shape if scatter else (N_DEV,) + v.shape, v.dtype) for v in srcs]


def _rider_copies(srcs, lands, send_sems, recv_sems, rider):
    x, y, c = _my_place()
    mine_idx = _block_index(x, y, c)

    def window(ref, a):
        if rider.part[a] is None:
            return ref
        dim, start, size = rider.part[a]
        return ref.at[(slice(None),) * dim + (pl.ds(start, size),)]

    sends, arrivals = [], []
    for r in range(1, N_DEV):
        peer = (_flip(x, r & 4), _flip(y, r & 2), _flip(c, r & 1))
        peer_idx = _block_index(*peer)
        for a in range(len(srcs)):
            src = window(srcs[a].at[peer_idx] if rider.scatter else srcs[a], a)
            k = 7 * a + r - 1
            sems = dict(send_sem=send_sems.at[k], recv_sem=recv_sems.at[k],
                        device_id=peer, device_id_type=pl.DeviceIdType.MESH)
            sends.append(pltpu.make_async_remote_copy(
                src_ref=src, dst_ref=window(lands[a].at[mine_idx], a), **sems))
            arrivals.append(pltpu.make_async_remote_copy(
                src_ref=src, dst_ref=window(lands[a].at[peer_idx], a), **sems))
    return sends, arrivals


def _ride(call, rider):
    call = dict(call)
    body, grid = call.pop("body"), call["grid"]
    operands = call.pop("operands")
    if rider is None:
        return list(pl.pallas_call(body, **call)(*operands)), None
    n_in, n_out = len(call["in_specs"]), len(call["out_specs"])
    n_scratch = len(call["scratch_shapes"])
    m = len(rider.srcs)

    def riding(*refs):
        main_in, srcs, lands = refs[:n_in], refs[n_in:n_in + m], refs[n_in + m:n_in + 2 * m]
        main_out = refs[n_in + 2 * m:n_in + 2 * m + n_out]
        rest = refs[n_in + 3 * m + n_out:]
        send_sems, recv_sems = rest[n_scratch:]
        at = [pl.program_id(d) for d in range(len(grid))]
        first = functools.reduce(jnp.logical_and, [p == 0 for p in at])
        last = functools.reduce(jnp.logical_and, [p == g - 1 for p, g in zip(at, grid)])
        sends, arrivals = _rider_copies(srcs, lands, send_sems, recv_sems, rider)

        @pl.when(first)
        def _():
            for cp in sends:
                cp.start()

        body(*main_in, *main_out, *rest[:n_scratch])

        @pl.when(last)
        def _():
            for cp in arrivals:
                cp.wait_recv()
            for cp in sends:
                cp.wait_send()

    call["in_specs"] = list(call["in_specs"]) + [_ANY] * (2 * m)
    call["out_specs"] = list(call["out_specs"]) + [_ANY] * m
    call["out_shape"] = list(call["out_shape"]) + [
        jax.ShapeDtypeStruct(v.shape, v.dtype) for v in rider.lands]
    call["scratch_shapes"] = list(call["scratch_shapes"]) + [
        pltpu.SemaphoreType.DMA((7 * m,)), pltpu.SemaphoreType.DMA((7 * m,))]
    call["input_output_aliases"] = {n_in + m + a: n_out + a for a in range(m)}
    outs = pl.pallas_call(riding, **call)(*operands, *rider.srcs, *rider.lands)
    return list(outs[:n_out]), list(outs[n_out:])


def _adamw_sum(parts, w, m, v, tile, name):
    depth, nrow, ncol = w.shape
    rows, cols = tile
    c1 = 1.0 / (1.0 - ADAM_B1 ** ADAM_STEP)
    c2 = 1.0 / (1.0 - ADAM_B2 ** ADAM_STEP)

    def body(*refs):
        p_refs = refs[:depth]
        w_ref, m_ref, v_ref, g_ref, d_ref, nm_ref, nv_ref = refs[depth:]
        layer = pl.program_id(0)
        for l in range(depth):
            @pl.when(layer == l)
            def _(p_ref=p_refs[l]):
                g = p_ref[0].astype(F32)
                for k in range(1, N_DEV):
                    g = g + p_ref[k].astype(F32)
                nm = ADAM_B1 * m_ref[...] + (1.0 - ADAM_B1) * g
                nv = ADAM_B2 * v_ref[...] + (1.0 - ADAM_B2) * (g * g)
                g_ref[...] = g
                nm_ref[...] = nm
                nv_ref[...] = nv
                d_ref[...] = -ADAM_LR * ((nm * c1) / (jnp.sqrt(nv * c2) + ADAM_EPS)
                                         + ADAM_WD * w_ref[...])

    def part_spec(l):
        return pl.BlockSpec((N_DEV, rows, cols), lambda q, i, j: (
            0, jnp.where(q == l, i, 0), jnp.where(q == l, j, 0)))

    blk = pl.BlockSpec((None, rows, cols), lambda q, i, j: (q, i, j))
    return pl.pallas_call(
        body, name=name, grid=(depth, nrow // rows, ncol // cols),
        in_specs=[part_spec(l) for l in range(depth)] + [blk, blk, blk],
        out_specs=[blk] * 4,
        out_shape=[jax.ShapeDtypeStruct(w.shape, F32)] * 4,
        compiler_params=pltpu.CompilerParams(
            dimension_semantics=("arbitrary", "arbitrary", "arbitrary")),
    )(*parts, w, m, v)


def _pack_small(norm_w, mem_norm_w, out_norm_w, final_norm_w, b_forget):
    onw = jnp.pad(out_norm_w.reshape(20, LANES), ((0, 4), (0, 0)))
    b = jnp.pad(b_forget, ((0, 6), (0, LANES - FOX_HEADS)))
    return jnp.concatenate([norm_w.reshape(16, LANES), mem_norm_w.reshape(16, LANES), onw,
                            final_norm_w.reshape(8, LANES), b], axis=0)


def _unpack_small(p):
    return (p[0:16].reshape(2, D_MODEL), p[16:32].reshape(2, D_MODEL), p[32:52].reshape(2, MIX_W),
            p[56:64].reshape(D_MODEL), p[64:66, :FOX_HEADS])


def kernel(x, mem, norm_w, w_in, b_forget, mem_norm_w, w_mem_kv, out_norm_w, w_out, final_norm_w, loss_target, m_norm_w, m_w_in, m_b_forget, m_mem_norm_w, m_w_mem_kv, m_out_norm_w, m_w_out, m_final_norm_w, v_norm_w, v_w_in, v_b_forget, v_mem_norm_w, v_w_mem_kv, v_out_norm_w, v_w_out, v_final_norm_w):
    kv_rows = w_mem_kv.shape[1]
    out_rows = w_out.shape[1]
    me = _block_index(*_my_place())

    def shards(l):
        return [w_in[l].T.astype(BF16), w_mem_kv[l].astype(BF16), w_out[l].astype(BF16)]

    def full_in(g_in):
        return _rearrange_w_in(g_in.reshape(IN_W, D_MODEL))

    def full_kv_out(g_kv, g_out):
        return g_kv.reshape(D_MODEL, 2 * MEM_W), g_out.reshape(MIX_W, D_MODEL)

    def in_blocks(g):
        segments = [(0, 3072, 0), (3072, 3080, QKV_W + MIX_W), (3080, 3336, 3072),
                    (3336, IN_W, QKV_W)]

        def block(k):
            lo, hi = k * SHARD_W, (k + 1) * SHARD_W
            pieces = [g[at + max(lo, a) - a:at + min(hi, b) - a]
                      for a, b, at in segments if max(lo, a) < min(hi, b)]
            return jnp.concatenate(pieces, axis=0).astype(BF16)

        return jnp.stack([block(k) for k in range(N_DEV)])

    def kv_blocks(g):
        return g.reshape(N_DEV, kv_rows, 2 * MEM_W).astype(BF16)

    def out_blocks(g):
        return g.reshape(N_DEV, out_rows, D_MODEL).astype(BF16)

    def with_own(land, own):
        return lax.dynamic_update_slice(land, own[None], (me,) + (0,) * own.ndim)

    def with_own_of(land, blocked):
        return lax.dynamic_update_slice(land, lax.dynamic_slice_in_dim(blocked, me, 1, axis=0),
                                        (me,) + (0,) * (land.ndim - 1))

    def row(v):
        return v.reshape(1, -1)

    def cols(first, size):
        return (1, first, size)

    half = D_MODEL // 2

    g_in0, g_kv0, g_out0 = _all_gather_weights(shards(0), "all_gather_l0")
    w_r0 = full_in(g_in0)
    s_in1, s_kv1, s_out1 = shards(1)
    x1, saved0, (l_in1,), (wkv0, wout0) = _layer_fwd(
        x[0], mem[0], row(norm_w[0]), w_r0, b_forget[0], row(mem_norm_w[0]),
        lambda lands: full_kv_out(g_kv0, g_out0), row(out_norm_w[0]), 0,
        travel=([s_in1], False, [([0], [cols(0, half)]), ([0], [cols(half, half)])]))
    w_r1 = full_in(with_own(l_in1, s_in1))
    x2, saved1, _, (wkv1, wout1) = _layer_fwd(
        x1, mem[0], row(norm_w[1]), w_r1, b_forget[1], row(mem_norm_w[1]),
        lambda lands: full_kv_out(with_own(lands[0], s_kv1), with_own(lands[1], s_out1)),
        row(out_norm_w[1]), 1, travel=([s_kv1, s_out1], False, [([0, 1], [None, None]), None]))

    dx2, loss_part, dfnw = _final_fwd_bwd(x2, row(final_norm_w), loss_target[0], "final_fwd_bwd")

    dx1, gr1, (l_out1,) = _layer_bwd(
        dx2, saved1, mem[0], w_r1, wkv1, wout1, 1,
        travel=lambda dwout: ([out_blocks(dwout)], True, [([0], [None]), None]))
    p_in1, p_kv1 = in_blocks(gr1["w_r"]), kv_blocks(gr1["w_mem_kv"])
    split = 5 * LANES
    grad_x, gr0, (l_in1, l_kv1, l_out0) = _layer_bwd(
        dx1, saved0, mem[0], w_r0, wkv0, wout0, 0,
        travel=lambda dwout: ([p_in1, p_kv1, out_blocks(dwout)], True,
                              [([0, 1], [cols(0, split), None]),
                               ([0, 2], [cols(split, D_MODEL - split), None])]))
    r_out1 = with_own_of(l_out1, out_blocks(gr1["w_out"]))
    r_in1, r_kv1 = with_own_of(l_in1, p_in1), with_own_of(l_kv1, p_kv1)
    r_out0 = with_own_of(l_out0, out_blocks(gr0["w_out"]))

    def both(name):
        return jnp.stack([gr0[name], gr1[name]])

    small = _pack_small(both("norm_w"), both("mem_norm_w"), both("out_norm_w"), dfnw[0],
                        both("b_forget")).at[LOSS_ROW].set(loss_part[0])
    p_small = jnp.broadcast_to(small[None], (N_DEV, SMALL_ROWS, LANES))
    r_in0, r_kv0, r_small = _exchange_blocks(
        [in_blocks(gr0["w_r"]), kv_blocks(gr0["w_mem_kv"]), p_small], "exchange_grads_l0")

    def t(v):
        return jnp.transpose(v, (0, 2, 1))

    g_w_in, d_w_in, nm_w_in, nv_w_in = [t(v) for v in _adamw_sum(
        [r_in0, r_in1], t(w_in), t(m_w_in), t(v_w_in), (SHARD_W, 256), "adamw_w_in")]
    g_w_kv, d_w_kv, nm_w_kv, nv_w_kv = _adamw_sum(
        [r_kv0, r_kv1], w_mem_kv, m_w_mem_kv, v_w_mem_kv, (kv_rows, 2 * MEM_W), "adamw_w_mem_kv")
    g_w_out, d_w_out, nm_w_out, nv_w_out = _adamw_sum(
        [r_out0, r_out1], w_out, m_w_out, v_w_out, (out_rows, D_MODEL), "adamw_w_out")
    w_small = _pack_small(norm_w, mem_norm_w, out_norm_w, final_norm_w, b_forget)[None]
    m_small = _pack_small(m_norm_w, m_mem_norm_w, m_out_norm_w, m_final_norm_w, m_b_forget)[None]
    v_small = _pack_small(v_norm_w, v_mem_norm_w, v_out_norm_w, v_final_norm_w, v_b_forget)[None]
    small_out = _adamw_sum([r_small], w_small, m_small, v_small, (SMALL_ROWS, LANES), "adamw_small")
    (g_nw, g_mnw, g_onw, g_fnw, g_b), (d_nw, d_mnw, d_onw, d_fnw, d_b), \
        (nm_nw, nm_mnw, nm_onw, nm_fnw, nm_b), (nv_nw, nv_mnw, nv_onw, nv_fnw, nv_b) = [
            _unpack_small(t[0]) for t in small_out]
    loss = small_out[0][0, LOSS_ROW, 0]

    return (loss, grad_x[None],
            g_nw, g_w_in, g_b, g_mnw, g_w_kv, g_onw, g_w_out, g_fnw,
            d_nw, d_w_in, d_b, d_mnw, d_w_kv, d_onw, d_w_out, d_fnw,
            nm_nw, nm_w_in, nm_b, nm_mnw, nm_w_kv, nm_onw, nm_w_out, nm_fnw,
            nv_nw, nv_w_in, nv_b, nv_mnw, nv_w_kv, nv_onw, nv_w_out, nv_fnw)
```

```python
import functools
from typing import NamedTuple

import jax
import jax.numpy as jnp
from jax import lax
from jax.experimental import pallas as pl
from jax.experimental.pallas import tpu as pltpu

F32 = jnp.float32
BF16 = jnp.bfloat16

N_DEV = 8
D_MODEL = 1024
HEAD_DIM = 64
LANES = 128
SB_W = 512
FX_W = 512
MEM_W = 256
MIX_W = 1280
FOX_HEADS = 8
IN_W = 4616
SHARD_W = IN_W // N_DEV
QKV_W = 3 * SB_W + 3 * FX_W + MEM_W
FL_PAD = 256
GF_W = MIX_W + FL_PAD
WR_W = QKV_W + GF_W
EPS = 1e-6
T = 256
TM = 256
TQM = 512
Q_SCALE = 0.125
NEG = -1e30
UNDERFLOW = -110.0

ADAM_LR = 0.001
ADAM_B1 = 0.9
ADAM_B2 = 0.999
ADAM_EPS = 1e-08
ADAM_WD = 0.01
ADAM_STEP = 10

SMALL_ROWS = 72
LOSS_ROW = 66

_NT = (((1,), (1,)), ((), ()))
_TN = (((0,), (0,)), ((), ()))

_ARB1 = pltpu.CompilerParams(dimension_semantics=("arbitrary",))
_ARB2 = pltpu.CompilerParams(dimension_semantics=("arbitrary", "arbitrary"))


def _dot(a, b):
    return jnp.dot(a, b, preferred_element_type=F32)


def _dot_nt(a, b):
    return lax.dot_general(a, b, _NT, preferred_element_type=F32)


def _dot_tn(a, b):
    return lax.dot_general(a, b, _TN, preferred_element_type=F32)


def _split2(x):
    hi = x.astype(BF16)
    lo = (x - hi.astype(F32)).astype(BF16)
    return hi, lo


def _stack2(u):
    return jnp.concatenate([u, u], axis=0)


def _cum2(x, u2):
    hi, lo = _split2(x)
    return _dot(jnp.concatenate([hi, lo], axis=1), u2)


def _tri3(tri, x, dot=None):
    dot = dot or _dot
    hi = x.astype(BF16)
    r1 = x - hi.astype(F32)
    mid = r1.astype(BF16)
    lo = (r1 - mid.astype(F32)).astype(BF16)
    return dot(tri, hi) + dot(tri, mid) + dot(tri, lo)


def _iota2(shape, dim):
    return lax.broadcasted_iota(jnp.int32, shape, dim)


def _head_block_diag():
    r = _iota2((LANES, LANES), 0) // HEAD_DIM
    c = _iota2((LANES, LANES), 1) // HEAD_DIM
    return _stack2(jnp.where(r == c, 1.0, 0.0).astype(BF16))


def _head_mean(x, bd):
    return _cum2(x, bd) * (1.0 / HEAD_DIM)


def _sigmoid(x):
    return 1.0 / (1.0 + jnp.exp(-x))


def _log_sigmoid(x):
    return jnp.minimum(x, 0.0) - jnp.log(1.0 + jnp.exp(-jnp.abs(x)))


def _running_top(r_ref):
    return jnp.max(jnp.maximum(r_ref[0], r_ref[1]))


def _head_norm_tops(x, m0):
    x2 = x.astype(F32)
    x2 = x2 * x2
    zero = jnp.zeros_like(x2)
    return (jnp.max(jnp.sqrt(jnp.sum(jnp.where(m0, x2, zero), axis=1, keepdims=True))),
            jnp.max(jnp.sqrt(jnp.sum(jnp.where(m0, zero, x2), axis=1, keepdims=True))))


def _fox_tiles_left(i, pair, qn, kn_ref, fb_ref, tile):
    def bound(j):
        b = [2.0 * qn[h] * kn_ref[h] + fb_ref[2 * i, 2 * pair + h] - fb_ref[2 * j + 1, 2 * pair + h]
             for h in range(2)]
        return jnp.maximum(b[0], b[1])

    def more(j):
        return jnp.logical_and(j >= 0, bound(jnp.maximum(j, 0)) > UNDERFLOW)

    def step(j):
        tile(j, False)
        return j - 1

    return lax.while_loop(more, step, i - 1)


def _pair_masks():
    lane = _iota2((1, LANES), 1)
    return lane < HEAD_DIM


def _split_pair(x, m0):
    zero = jnp.zeros_like(x)
    return jnp.where(m0, x, zero), jnp.where(m0, zero, x)


def _inproj_fwd(x, nw, w_r, name):
    s = x.shape[0]

    def body(x_ref, nw_ref, w_ref, qkv_ref, gf_ref):
        xv = x_ref[...]
        r = lax.rsqrt(jnp.mean(xv * xv, axis=-1, keepdims=True) + EPS)
        h = (xv * r * nw_ref[...]).astype(BF16)
        for c in range(0, QKV_W, 256):
            qkv_ref[:, c:c + 256] = _dot_nt(h, w_ref[c:c + 256, :]).astype(BF16)
        for c in range(0, GF_W, 256):
            gf_ref[:, c:c + 256] = _dot_nt(h, w_ref[QKV_W + c:QKV_W + c + 256, :])

    return pl.pallas_call(
        body, name=name, grid=(s // TM,),
        in_specs=[pl.BlockSpec((TM, D_MODEL), lambda i: (i, 0)),
                  pl.BlockSpec((1, D_MODEL), lambda i: (0, 0)),
                  pl.BlockSpec((WR_W, D_MODEL), lambda i: (0, 0))],
        out_specs=[pl.BlockSpec((TM, QKV_W), lambda i: (i, 0)),
                   pl.BlockSpec((TM, GF_W), lambda i: (i, 0))],
        out_shape=[jax.ShapeDtypeStruct((s, QKV_W), BF16), jax.ShapeDtypeStruct((s, GF_W), F32)],
        compiler_params=_ARB1,
    )(x, nw, w_r)


def _fox_prep_fwd(gf, bpad, name):
    s = gf.shape[0]

    def body(fl_ref, b_ref, fq_ref, fr_ref, fb_ref):
        tri = jnp.where(_iota2((T, T), 0) >= _iota2((T, T), 1), 1.0, 0.0).astype(BF16)
        m0 = _pair_masks()
        carry = jnp.zeros((1, LANES), F32)
        for blk in range(s // T):
            rows = slice(blk * T, (blk + 1) * T)
            lf = _log_sigmoid(fl_ref[rows, :] + b_ref[...])
            c = _tri3(tri, lf) + carry
            carry = c[T - 1:T, :]
            for p in range(FOX_HEADS // 2):
                fq_ref[rows, p * LANES:(p + 1) * LANES] = jnp.where(
                    m0, c[:, 2 * p:2 * p + 1], c[:, 2 * p + 1:2 * p + 2])
            fr_ref[:, rows] = c.T[0:FOX_HEADS, :]
            fb_ref[2 * blk:2 * blk + 1, :] = c[0:1, :]
            fb_ref[2 * blk + 1:2 * blk + 2, :] = carry

    return pl.pallas_call(
        body, name=name, grid=(1,),
        in_specs=[pl.BlockSpec((s, LANES), lambda i: (0, MIX_W // LANES)),
                  pl.BlockSpec((1, LANES), lambda i: (0, 0))],
        out_specs=[pl.BlockSpec((s, FX_W), lambda i: (0, 0)),
                   pl.BlockSpec((FOX_HEADS, s), lambda i: (0, 0)),
                   pl.BlockSpec((2 * s // T, LANES), lambda i: (0, 0))],
        out_shape=[jax.ShapeDtypeStruct((s, FX_W), F32), jax.ShapeDtypeStruct((FOX_HEADS, s), F32),
                   jax.ShapeDtypeStruct((2 * s // T, LANES), F32)],
        compiler_params=_ARB1,
    )(gf, bpad)


def _sb_fwd(qkv, name, rider=None):
    s = qkv.shape[0]

    def body(q_ref, k_ref, v_ref, o_ref, acc_ref, r_ref, as_ref):
        i = pl.program_id(1)
        m0 = _pair_masks()
        qh = _split_pair(q_ref[...] * jnp.asarray(Q_SCALE, BF16), m0)
        strict = _iota2((T, T), 0) > _iota2((T, T), 1)
        u2 = _stack2(jnp.where(strict, 1.0, 0.0).astype(BF16))
        acc_ref[...] = jnp.zeros_like(acc_ref)
        r_ref[...] = jnp.zeros_like(r_ref)
        hs = range(2)

        def flush(j):
            v = v_ref[pl.ds(pl.multiple_of(j * T, T), T), :]
            for h in hs:
                acc_ref[h] += _dot(as_ref[h], v)

        def tile(j, diag):
            k = k_ref[pl.ds(pl.multiple_of(j * T, T), T), :]
            z = [_dot_nt(qh[h], k) for h in hs]
            if not diag:
                flush(j + 1)
            la = [jnp.minimum(z[h], 0.0) - jnp.log(1.0 + jnp.exp(-jnp.abs(z[h]))) for h in hs]
            lf = [la[h] - z[h] for h in hs]
            if diag:
                lf = [jnp.where(strict, lf[h], 0.0) for h in hs]
            cin = [_cum2(lf[h], u2) for h in hs]
            a = [jnp.exp(la[h] + cin[h] + r_ref[h]) for h in hs]
            if diag:
                a = [jnp.where(strict, a[h], 0.0) for h in hs]
            for h in hs:
                r_ref[h] += cin[h][:, 0:1] + lf[h][:, 0:1]
                as_ref[h] = a[h].astype(BF16)

        tile(i, True)

        def more(state):
            j, top = state
            return jnp.logical_and(j >= 0, top > UNDERFLOW)

        def step(state):
            j, _ = state
            tile(j, False)
            return j - 1, _running_top(r_ref)

        j_left, _ = lax.while_loop(more, step, (i - 1, _running_top(r_ref)))
        flush(j_left + 1)
        o_ref[...] = jnp.where(m0, acc_ref[0], acc_ref[1])

    nb = SB_W // LANES
    (ysb,), lands = _ride(dict(
        body=body, name=name, grid=(nb, s // T),
        in_specs=[pl.BlockSpec((T, LANES), lambda p, i: (i, p)),
                  pl.BlockSpec((s, LANES), lambda p, i: (0, nb + p)),
                  pl.BlockSpec((s, LANES), lambda p, i: (0, 2 * nb + p))],
        out_specs=[pl.BlockSpec((T, LANES), lambda p, i: (i, p))],
        out_shape=[jax.ShapeDtypeStruct((s, SB_W), F32)],
        scratch_shapes=[pltpu.VMEM((2, T, LANES), F32), pltpu.VMEM((2, T, 1), F32),
                        pltpu.VMEM((2, T, T), BF16)],
        compiler_params=_ARB2, operands=[qkv, qkv, qkv]), rider)
    return ysb, lands


def _fox_fwd(qkv, fqb, frow, fbounds, name, rider=None):
    s = qkv.shape[0]

    def body(q_ref, k_ref, v_ref, fq_ref, fr_ref, fb_ref, o_ref, lse_ref, acc_ref, m_ref, ps_ref,
             kn_ref):
        pair = pl.program_id(0)
        i = pl.program_id(1)
        m0 = _pair_masks()

        @pl.when(i == 0)
        def _():
            kn_ref[0], kn_ref[1] = _head_norm_tops(k_ref[...], m0)

        q = q_ref[...] * jnp.asarray(Q_SCALE, BF16)
        qn = _head_norm_tops(q, m0)
        qh = _split_pair(q, m0)
        fq = fq_ref[...]
        fqh = (fq[:, 0:1], fq[:, HEAD_DIM:HEAD_DIM + 1])
        causal = _iota2((T, T), 0) >= _iota2((T, T), 1)
        ones = jnp.ones((T, LANES), BF16)
        acc_ref[...] = jnp.zeros_like(acc_ref)
        m_ref[...] = jnp.full_like(m_ref, NEG)
        hs = range(2)

        def flush(j):
            v = v_ref[pl.ds(pl.multiple_of(j * T, T), T), :]
            va2 = _stack2(jnp.concatenate([v, ones], axis=1))
            for h in hs:
                acc_ref[h] += _dot(ps_ref[h], va2)

        def tile(j, diag):
            off = pl.multiple_of(j * T, T)
            k = k_ref[pl.ds(off, T), :]
            sc = [_dot_nt(qh[h], k) + fqh[h] - fr_ref[h:h + 1, pl.ds(off, T)] for h in hs]
            if not diag:
                flush(j + 1)
            if diag:
                sc = [jnp.where(causal, sc[h], NEG) for h in hs]
            m_new = [jnp.maximum(m_ref[h], jnp.max(sc[h], axis=1, keepdims=True)) for h in hs]
            p = [jnp.exp(sc[h] - m_new[h]) for h in hs]
            for h in hs:
                acc_ref[h] = acc_ref[h] * jnp.exp(m_ref[h] - m_new[h])
                m_ref[h] = m_new[h]
                p_hi, p_lo = _split2(p[h])
                ps_ref[h] = jnp.concatenate([p_hi, p_lo], axis=1)

        tile(i, True)
        j_left = _fox_tiles_left(i, pair, qn, kn_ref, fb_ref, tile)
        flush(j_left + 1)
        acc = (acc_ref[0], acc_ref[1])
        o_ref[...] = jnp.where(m0, acc[0][:, :LANES] / acc[0][:, LANES:],
                               acc[1][:, :LANES] / acc[1][:, LANES:])
        lse_ref[...] = jnp.where(m0, m_ref[0] + jnp.log(acc[0][:, LANES:]),
                                 m_ref[1] + jnp.log(acc[1][:, LANES:]))

    nb = FX_W // LANES
    base = 3 * SB_W // LANES
    (yfx, lse), lands = _ride(dict(
        body=body, name=name, grid=(nb, s // T),
        in_specs=[pl.BlockSpec((T, LANES), lambda p, i: (i, base + p)),
                  pl.BlockSpec((s, LANES), lambda p, i: (0, base + nb + p)),
                  pl.BlockSpec((s, LANES), lambda p, i: (0, base + 2 * nb + p)),
                  pl.BlockSpec((T, LANES), lambda p, i: (i, p)),
                  pl.BlockSpec((None, 2, s), lambda p, i: (p, 0, 0)),
                  pl.BlockSpec(memory_space=pltpu.SMEM)],
        out_specs=[pl.BlockSpec((T, LANES), lambda p, i: (i, p)),
                   pl.BlockSpec((T, LANES), lambda p, i: (i, p))],
        out_shape=[jax.ShapeDtypeStruct((s, FX_W), F32), jax.ShapeDtypeStruct((s, FX_W), F32)],
        scratch_shapes=[pltpu.VMEM((2, T, 2 * LANES), F32), pltpu.VMEM((2, T, 1), F32),
                        pltpu.VMEM((2, T, 2 * T), BF16), pltpu.SMEM((2,), F32)],
        compiler_params=_ARB2, operands=[qkv, qkv, qkv, fqb, frow, fbounds]), rider)
    return yfx, lse, lands


def _memkv_fwd(mem, mnw, wkv, name):
    n = mem.shape[0]

    def body(mem_ref, mnw_ref, w_ref, kv_ref):
        mv = mem_ref[...]
        r = lax.rsqrt(jnp.mean(mv * mv, axis=-1, keepdims=True) + EPS)
        hm = (mv * r * mnw_ref[...]).astype(BF16)
        kv_ref[...] = _dot(hm, w_ref[...]).astype(BF16)

    return pl.pallas_call(
        body, name=name, grid=(1,),
        in_specs=[pl.BlockSpec((n, D_MODEL), lambda i: (0, 0)),
                  pl.BlockSpec((1, D_MODEL), lambda i: (0, 0)),
                  pl.BlockSpec((D_MODEL, 2 * MEM_W), lambda i: (0, 0))],
        out_specs=pl.BlockSpec((n, 2 * MEM_W), lambda i: (0, 0)),
        out_shape=jax.ShapeDtypeStruct((n, 2 * MEM_W), BF16),
        compiler_params=_ARB1,
    )(mem, mnw, wkv)


def _mem_fwd(qkv, kv, name):
    s = qkv.shape[0]
    n = kv.shape[0]

    def body(q_ref, k_ref, v_ref, o_ref, lse_ref):
        m0 = _pair_masks()
        qh = _split_pair(q_ref[...] * jnp.asarray(Q_SCALE, BF16), m0)
        k = k_ref[...]
        v = v_ref[...]
        outs, lses = [], []
        for h in range(2):
            sc = _dot_nt(qh[h], k)
            mx = jnp.max(sc, axis=1, keepdims=True)
            p = jnp.exp(sc - mx)
            l = jnp.sum(p, axis=1, keepdims=True)
            outs.append(_dot(p.astype(BF16), v) / l)
            lses.append(mx + jnp.log(l))
        o_ref[...] = jnp.where(m0, outs[0], outs[1])
        lse_ref[...] = jnp.where(m0, lses[0], lses[1])

    nb = MEM_W // LANES
    base = (3 * SB_W + 3 * FX_W) // LANES
    return pl.pallas_call(
        body, name=name, grid=(nb, s // TQM),
        in_specs=[pl.BlockSpec((TQM, LANES), lambda p, i: (i, base + p)),
                  pl.BlockSpec((n, LANES), lambda p, i: (0, p)),
                  pl.BlockSpec((n, LANES), lambda p, i: (0, nb + p))],
        out_specs=[pl.BlockSpec((TQM, LANES), lambda p, i: (i, p)),
                   pl.BlockSpec((TQM, LANES), lambda p, i: (i, p))],
        out_shape=[jax.ShapeDtypeStruct((s, MEM_W), F32), jax.ShapeDtypeStruct((s, MEM_W), F32)],
        compiler_params=_ARB2,
    )(qkv, kv, kv)


def _mix_chunk(c, ysb_ref, yfx_ref, ym_ref):
    if c < SB_W // LANES:
        return ysb_ref[:, c * LANES:(c + 1) * LANES]
    c -= SB_W // LANES
    if c < FX_W // LANES:
        return yfx_ref[:, c * LANES:(c + 1) * LANES]
    c -= FX_W // LANES
    return ym_ref[:, c * LANES:(c + 1) * LANES]


def _outproj_fwd(ysb, yfx, ym, gf, onw, wout, x, name):
    s = x.shape[0]

    def body(ysb_ref, yfx_ref, ym_ref, g_ref, onw_ref, w_ref, x_ref, o_ref, yg_ref):
        bd = _head_block_diag()
        for c in range(MIX_W // LANES):
            sl = slice(c * LANES, (c + 1) * LANES)
            u = _mix_chunk(c, ysb_ref, yfx_ref, ym_ref)
            r = lax.rsqrt(_head_mean(u * u, bd) + EPS)
            g = g_ref[:, sl]
            yg_ref[:, sl] = (u * r * onw_ref[:, sl] * (g * _sigmoid(g))).astype(BF16)
        o_ref[...] = x_ref[...] + _dot(yg_ref[...], w_ref[...])

    return pl.pallas_call(
        body, name=name, grid=(s // TM,),
        in_specs=[pl.BlockSpec((TM, SB_W), lambda i: (i, 0)),
                  pl.BlockSpec((TM, FX_W), lambda i: (i, 0)),
                  pl.BlockSpec((TM, MEM_W), lambda i: (i, 0)),
                  pl.BlockSpec((TM, MIX_W), lambda i: (i, 0)),
                  pl.BlockSpec((1, MIX_W), lambda i: (0, 0)),
                  pl.BlockSpec((MIX_W, D_MODEL), lambda i: (0, 0)),
                  pl.BlockSpec((TM, D_MODEL), lambda i: (i, 0))],
        out_specs=pl.BlockSpec((TM, D_MODEL), lambda i: (i, 0)),
        out_shape=jax.ShapeDtypeStruct((s, D_MODEL), F32),
        scratch_shapes=[pltpu.VMEM((TM, MIX_W), BF16)],
        compiler_params=_ARB1,
    )(ysb, yfx, ym, gf, onw, wout, x)


def _final_fwd_bwd(x, fnw, target, name):
    s = x.shape[0]

    def body(x_ref, w_ref, t_ref, dx_ref, loss_ref, dw_ref):
        @pl.when(pl.program_id(0) == 0)
        def _():
            loss_ref[...] = jnp.zeros_like(loss_ref)
            dw_ref[...] = jnp.zeros_like(dw_ref)

        xv = x_ref[...]
        w = w_ref[...]
        r = lax.rsqrt(jnp.mean(xv * xv, axis=-1, keepdims=True) + EPS)
        xh = xv * r
        err = xh * w - t_ref[...]
        part = jnp.sum(jnp.sum(err * err, axis=1, keepdims=True), axis=0, keepdims=True)
        loss_ref[...] += part * (0.5 / D_MODEL)
        dy = err * (1.0 / D_MODEL)
        dw_ref[...] += jnp.sum(dy * xh, axis=0, keepdims=True)
        dxh = dy * w
        dx_ref[...] = r * (dxh - xh * jnp.mean(dxh * xh, axis=-1, keepdims=True))

    return pl.pallas_call(
        body, name=name, grid=(s // TM,),
        in_specs=[pl.BlockSpec((TM, D_MODEL), lambda i: (i, 0)),
                  pl.BlockSpec((1, D_MODEL), lambda i: (0, 0)),
                  pl.BlockSpec((TM, D_MODEL), lambda i: (i, 0))],
        out_specs=[pl.BlockSpec((TM, D_MODEL), lambda i: (i, 0)),
                   pl.BlockSpec((1, LANES), lambda i: (0, 0)),
                   pl.BlockSpec((1, D_MODEL), lambda i: (0, 0))],
        out_shape=[jax.ShapeDtypeStruct((s, D_MODEL), F32), jax.ShapeDtypeStruct((1, LANES), F32),
                   jax.ShapeDtypeStruct((1, D_MODEL), F32)],
        compiler_params=_ARB1,
    )(x, fnw, target)


def _outproj_bwd(dxo, wout, ysb, yfx, ym, gf, onw, name):
    s = dxo.shape[0]

    def body(dx_ref, w_ref, ysb_ref, yfx_ref, ym_ref, g_ref, onw_ref,
             dysb_ref, dyfx_ref, dym_ref, dg_ref, dw_ref, donw_ref, yg_ref):
        @pl.when(pl.program_id(0) == 0)
        def _():
            dw_ref[...] = jnp.zeros_like(dw_ref)
            donw_ref[...] = jnp.zeros_like(donw_ref)

        dxb = dx_ref[...].astype(BF16)
        dyg = _dot_nt(dxb, w_ref[...])
        bd = _head_block_diag()
        for c in range(MIX_W // LANES):
            sl = slice(c * LANES, (c + 1) * LANES)
            u = _mix_chunk(c, ysb_ref, yfx_ref, ym_ref)
            r = lax.rsqrt(_head_mean(u * u, bd) + EPS)
            yn = u * r
            g = g_ref[:, sl]
            sg = _sigmoid(g)
            sil = g * sg
            onw = onw_ref[:, sl]
            e = dyg[:, sl]
            yg_ref[:, sl] = (yn * onw * sil).astype(BF16)
            donw_ref[:, sl] += jnp.sum(e * yn * sil, axis=0, keepdims=True)
            dg_ref[:, sl] = (e * yn * onw * (sg * (1.0 + g * (1.0 - sg)))).astype(BF16)
            dyn = e * onw * sil
            du = (r * (dyn - yn * _head_mean(dyn * yn, bd))).astype(BF16)
            if c < 4:
                dysb_ref[:, c * LANES:(c + 1) * LANES] = du
            elif c < 8:
                dyfx_ref[:, (c - 4) * LANES:(c - 3) * LANES] = du
            else:
                dym_ref[:, (c - 8) * LANES:(c - 7) * LANES] = du
        dw_ref[...] += _dot_tn(yg_ref[...], dxb)

    return pl.pallas_call(
        body, name=name, grid=(s // TM,),
        in_specs=[pl.BlockSpec((TM, D_MODEL), lambda i: (i, 0)),
                  pl.BlockSpec((MIX_W, D_MODEL), lambda i: (0, 0)),
                  pl.BlockSpec((TM, SB_W), lambda i: (i, 0)),
                  pl.BlockSpec((TM, FX_W), lambda i: (i, 0)),
                  pl.BlockSpec((TM, MEM_W), lambda i: (i, 0)),
                  pl.BlockSpec((TM, MIX_W), lambda i: (i, 0)),
                  pl.BlockSpec((1, MIX_W), lambda i: (0, 0))],
        out_specs=[pl.BlockSpec((TM, SB_W), lambda i: (i, 0)),
                   pl.BlockSpec((TM, FX_W), lambda i: (i, 0)),
                   pl.BlockSpec((TM, MEM_W), lambda i: (i, 0)),
                   pl.BlockSpec((TM, MIX_W), lambda i: (i, 0)),
                   pl.BlockSpec((MIX_W, D_MODEL), lambda i: (0, 0)),
                   pl.BlockSpec((1, MIX_W), lambda i: (0, 0))],
        out_shape=[jax.ShapeDtypeStruct((s, SB_W), BF16), jax.ShapeDtypeStruct((s, FX_W), BF16),
                   jax.ShapeDtypeStruct((s, MEM_W), BF16), jax.ShapeDtypeStruct((s, MIX_W), BF16),
                   jax.ShapeDtypeStruct((MIX_W, D_MODEL), F32), jax.ShapeDtypeStruct((1, MIX_W), F32)],
        scratch_shapes=[pltpu.VMEM((TM, MIX_W), BF16)],
        compiler_params=_ARB1,
    )(dxo, wout, ysb, yfx, ym, gf, onw)


def _row_dots(do_ref, o_ref, m0):
    prod = do_ref[...].astype(F32) * o_ref[...]
    zero = jnp.zeros_like(prod)
    return (jnp.sum(jnp.where(m0, prod, zero), axis=1, keepdims=True),
            jnp.sum(jnp.where(m0, zero, prod), axis=1, keepdims=True))


def _sb_bwd(qkv, o, do, name, rider=None):
    s = qkv.shape[0]
    nq = s // T

    def body(q_ref, k_ref, v_ref, o_ref, do_ref, dq_ref, dk_ref, dv_ref,
             dqa_ref, dka_ref, dva_ref, rl_ref, rg_ref, dzs_ref, abs_ref):
        i = pl.program_id(1)

        @pl.when(i == 0)
        def _():
            dka_ref[...] = jnp.zeros_like(dka_ref)
            dva_ref[...] = jnp.zeros_like(dva_ref)

        m0 = _pair_masks()
        qh = _split_pair(q_ref[...] * jnp.asarray(Q_SCALE, BF16), m0)
        doh = _split_pair(do_ref[...], m0)
        dsum = _row_dots(do_ref, o_ref, m0)
        strict = _iota2((T, T), 0) > _iota2((T, T), 1)
        u2 = _stack2(jnp.where(strict, 1.0, 0.0).astype(BF16))
        dqa_ref[...] = jnp.zeros_like(dqa_ref)
        rl_ref[...] = jnp.zeros_like(rl_ref)
        rg_ref[...] = jnp.zeros_like(rg_ref)

        hs = range(2)

        def flush(j):
            off = pl.multiple_of(j * T, T)
            k = k_ref[pl.ds(off, T), :]
            for h in hs:
                dqa_ref[h] += _dot(dzs_ref[h], k)
            dka_ref[pl.ds(off, T), :] += _dot_tn(dzs_ref[0], qh[0]) + _dot_tn(dzs_ref[1], qh[1])
            dva_ref[pl.ds(off, T), :] += _dot_tn(abs_ref[0], doh[0]) + _dot_tn(abs_ref[1], doh[1])

        def tile(j, diag):
            off = pl.multiple_of(j * T, T)
            k = k_ref[pl.ds(off, T), :]
            v = v_ref[pl.ds(off, T), :]
            z = [_dot_nt(qh[h], k) for h in hs]
            da = [_dot_nt(doh[h], v) for h in hs]
            if not diag:
                flush(j + 1)
            la = [jnp.minimum(z[h], 0.0) - jnp.log(1.0 + jnp.exp(-jnp.abs(z[h]))) for h in hs]
            lf = [la[h] - z[h] for h in hs]
            if diag:
                lf = [jnp.where(strict, lf[h], 0.0) for h in hs]
            cin = [_cum2(lf[h], u2) for h in hs]
            a = [jnp.exp(la[h] + cin[h] + rl_ref[h]) for h in hs]
            if diag:
                a = [jnp.where(strict, a[h], 0.0) for h in hs]
            ab = [a[h].astype(BF16) for h in hs]
            g = [ab[h].astype(F32) * da[h] for h in hs]
            gin = [_cum2(g[h], u2) for h in hs]
            dz = [g[h] - jnp.exp(la[h]) * ((dsum[h] - rg_ref[h]) - gin[h]) for h in hs]
            if diag:
                dz = [jnp.where(strict, dz[h], 0.0) for h in hs]
            for h in hs:
                rl_ref[h] += cin[h][:, 0:1] + lf[h][:, 0:1]
                rg_ref[h] += gin[h][:, 0:1] + g[h][:, 0:1]
                dzs_ref[h] = dz[h].astype(BF16)
                abs_ref[h] = ab[h]

        tile(i, True)

        def more(state):
            j, top = state
            return jnp.logical_and(j >= 0, top > UNDERFLOW)

        def step(state):
            j, _ = state
            tile(j, False)
            return j - 1, _running_top(rl_ref)

        j_left, _ = lax.while_loop(more, step, (i - 1, _running_top(rl_ref)))
        flush(j_left + 1)
        dq_ref[...] = (jnp.where(m0, dqa_ref[0], dqa_ref[1]) * Q_SCALE).astype(BF16)

        @pl.when(i == nq - 1)
        def _():
            dk_ref[...] = dka_ref[...].astype(BF16)
            dv_ref[...] = dva_ref[...].astype(BF16)

    nb = SB_W // LANES
    (dq, dk, dv), lands = _ride(dict(
        body=body, name=name, grid=(nb, nq),
        in_specs=[pl.BlockSpec((T, LANES), lambda p, i: (i, p)),
                  pl.BlockSpec((s, LANES), lambda p, i: (0, nb + p)),
                  pl.BlockSpec((s, LANES), lambda p, i: (0, 2 * nb + p)),
                  pl.BlockSpec((T, LANES), lambda p, i: (i, p)),
                  pl.BlockSpec((T, LANES), lambda p, i: (i, p))],
        out_specs=[pl.BlockSpec((T, LANES), lambda p, i: (i, p)),
                   pl.BlockSpec((s, LANES), lambda p, i: (0, p)),
                   pl.BlockSpec((s, LANES), lambda p, i: (0, p))],
        out_shape=[jax.ShapeDtypeStruct((s, SB_W), BF16)] * 3,
        scratch_shapes=[pltpu.VMEM((2, T, LANES), F32), pltpu.VMEM((s, LANES), F32),
                        pltpu.VMEM((s, LANES), F32), pltpu.VMEM((2, T, 1), F32),
                        pltpu.VMEM((2, T, 1), F32), pltpu.VMEM((2, T, T), BF16),
                        pltpu.VMEM((2, T, T), BF16)],
        compiler_params=_ARB2, operands=[qkv, qkv, qkv, o, do]), rider)
    return dq, dk, dv, lands


def _fox_bwd(qkv, fqb, frow, fbounds, o, lse, do, name, rider=None):
    s = qkv.shape[0]
    nq = s // T

    def body(q_ref, k_ref, v_ref, fq_ref, fr_ref, fb_ref, o_ref, lse_ref, do_ref,
             dq_ref, dk_ref, dv_ref, df_ref, dqa_ref, dka_ref, dva_ref, dfa_ref, dls_ref, pbs_ref,
             kn_ref):
        i = pl.program_id(1)

        @pl.when(i == 0)
        def _():
            dka_ref[...] = jnp.zeros_like(dka_ref)
            dva_ref[...] = jnp.zeros_like(dva_ref)
            dfa_ref[...] = jnp.zeros_like(dfa_ref)

        m0 = _pair_masks()

        @pl.when(i == 0)
        def _():
            kn_ref[0], kn_ref[1] = _head_norm_tops(k_ref[...], m0)

        q = q_ref[...] * jnp.asarray(Q_SCALE, BF16)
        qn = _head_norm_tops(q, m0)
        qh = _split_pair(q, m0)
        doh = _split_pair(do_ref[...], m0)
        dsum = _row_dots(do_ref, o_ref, m0)
        fq = fq_ref[...]
        fqh = (fq[:, 0:1], fq[:, HEAD_DIM:HEAD_DIM + 1])
        lse = lse_ref[...]
        lseh = (lse[:, 0:1], lse[:, HEAD_DIM:HEAD_DIM + 1])
        causal = _iota2((T, T), 0) >= _iota2((T, T), 1)
        dqa_ref[...] = jnp.zeros_like(dqa_ref)

        hs = range(2)

        def flush(j):
            off = pl.multiple_of(j * T, T)
            k = k_ref[pl.ds(off, T), :]
            for h in hs:
                dqa_ref[h] += _dot(dls_ref[h], k)
            dka_ref[pl.ds(off, T), :] += _dot_tn(dls_ref[0], qh[0]) + _dot_tn(dls_ref[1], qh[1])
            dva_ref[pl.ds(off, T), :] += _dot_tn(pbs_ref[0], doh[0]) + _dot_tn(pbs_ref[1], doh[1])

        def tile(j, diag):
            off = pl.multiple_of(j * T, T)
            k = k_ref[pl.ds(off, T), :]
            v = v_ref[pl.ds(off, T), :]
            sc = [_dot_nt(qh[h], k) + fqh[h] - fr_ref[h:h + 1, pl.ds(off, T)] for h in hs]
            dp = [_dot_nt(doh[h], v) for h in hs]
            if not diag:
                flush(j + 1)
            p = [jnp.exp(sc[h] - lseh[h]) for h in hs]
            if diag:
                p = [jnp.where(causal, p[h], 0.0) for h in hs]
            dl = [p[h] * (dp[h] - dsum[h]) for h in hs]
            for h in hs:
                dls_ref[h] = dl[h].astype(BF16)
                pbs_ref[h] = p[h].astype(BF16)
                dfa_ref[h:h + 1, pl.ds(off, T)] -= jnp.sum(dl[h], axis=0, keepdims=True)

        tile(i, True)
        j_left = _fox_tiles_left(i, pl.program_id(0), qn, kn_ref, fb_ref, tile)
        flush(j_left + 1)
        dq_ref[...] = (jnp.where(m0, dqa_ref[0], dqa_ref[1]) * Q_SCALE).astype(BF16)

        @pl.when(i == nq - 1)
        def _():
            dk_ref[...] = dka_ref[...].astype(BF16)
            dv_ref[...] = dva_ref[...].astype(BF16)
            df_ref[...] = dfa_ref[...]

    nb = FX_W // LANES
    base = 3 * SB_W // LANES
    (dq, dk, dv, df), lands = _ride(dict(
        body=body, name=name, grid=(nb, nq),
        in_specs=[pl.BlockSpec((T, LANES), lambda p, i: (i, base + p)),
                  pl.BlockSpec((s, LANES), lambda p, i: (0, base + nb + p)),
                  pl.BlockSpec((s, LANES), lambda p, i: (0, base + 2 * nb + p)),
                  pl.BlockSpec((T, LANES), lambda p, i: (i, p)),
                  pl.BlockSpec((None, 2, s), lambda p, i: (p, 0, 0)),
                  pl.BlockSpec(memory_space=pltpu.SMEM),
                  pl.BlockSpec((T, LANES), lambda p, i: (i, p)),
                  pl.BlockSpec((T, LANES), lambda p, i: (i, p)),
                  pl.BlockSpec((T, LANES), lambda p, i: (i, p))],
        out_specs=[pl.BlockSpec((T, LANES), lambda p, i: (i, p)),
                   pl.BlockSpec((s, LANES), lambda p, i: (0, p)),
                   pl.BlockSpec((s, LANES), lambda p, i: (0, p)),
                   pl.BlockSpec((None, 2, s), lambda p, i: (p, 0, 0))],
        out_shape=[jax.ShapeDtypeStruct((s, FX_W), BF16)] * 3
        + [jax.ShapeDtypeStruct((nb, 2, s), F32)],
        scratch_shapes=[pltpu.VMEM((2, T, LANES), F32), pltpu.VMEM((s, LANES), F32),
                        pltpu.VMEM((s, LANES), F32), pltpu.VMEM((2, s), F32),
                        pltpu.VMEM((2, T, T), BF16), pltpu.VMEM((2, T, T), BF16),
                        pltpu.SMEM((2,), F32)],
        compiler_params=_ARB2, operands=[qkv, qkv, qkv, fqb, frow, fbounds, o, lse, do]), rider)
    return dq, dk, dv, df, lands


def _fox_prep_bwd(dfrow, gf, bpad, name):
    s = gf.shape[0]

    def body(df_ref, fl_ref, b_ref, dfl_ref, db_ref):
        tri = jnp.where(_iota2((T, T), 0) <= _iota2((T, T), 1), 1.0, 0.0).astype(BF16)
        carry = jnp.zeros((1, LANES), F32)
        db = jnp.zeros((1, LANES), F32)
        fill = jnp.zeros((LANES - FOX_HEADS, T), F32)
        for blk in reversed(range(s // T)):
            rows = slice(blk * T, (blk + 1) * T)
            c = _tri3(tri, jnp.concatenate([df_ref[:, rows], fill], axis=0), _dot_nt) + carry
            carry = c[0:1, :]
            dfl = c / (1.0 + jnp.exp(fl_ref[rows, :] + b_ref[...]))
            dfl_ref[rows, :] = dfl.astype(BF16)
            db = db + jnp.sum(dfl, axis=0, keepdims=True)
        db_ref[...] = db

    return pl.pallas_call(
        body, name=name, grid=(1,),
        in_specs=[pl.BlockSpec((FOX_HEADS, s), lambda i: (0, 0)),
                  pl.BlockSpec((s, LANES), lambda i: (0, MIX_W // LANES)),
                  pl.BlockSpec((1, LANES), lambda i: (0, 0))],
        out_specs=[pl.BlockSpec((s, LANES), lambda i: (0, 0)),
                   pl.BlockSpec((1, LANES), lambda i: (0, 0))],
        out_shape=[jax.ShapeDtypeStruct((s, LANES), BF16), jax.ShapeDtypeStruct((1, LANES), F32)],
        compiler_params=_ARB1,
    )(dfrow, gf, bpad)


def _mem_bwd(qkv, kv, o, lse, do, name):
    s = qkv.shape[0]
    n = kv.shape[0]

    def body(q_ref, k_ref, v_ref, o_ref, lse_ref, do_ref, dq_ref, dk_ref, dv_ref):
        @pl.when(pl.program_id(1) == 0)
        def _():
            dk_ref[...] = jnp.zeros_like(dk_ref)
            dv_ref[...] = jnp.zeros_like(dv_ref)

        m0 = _pair_masks()
        qh = _split_pair(q_ref[...] * jnp.asarray(Q_SCALE, BF16), m0)
        doh = _split_pair(do_ref[...], m0)
        dsum = _row_dots(do_ref, o_ref, m0)
        lse = lse_ref[...]
        lseh = (lse[:, 0:1], lse[:, HEAD_DIM:HEAD_DIM + 1])
        k = k_ref[...]
        v = v_ref[...]
        dqs = []
        for h in range(2):
            p = jnp.exp(_dot_nt(qh[h], k) - lseh[h])
            dl = p * (_dot_nt(doh[h], v) - dsum[h])
            dlb = dl.astype(BF16)
            dqs.append(_dot(dlb, k))
            dk_ref[...] += _dot_tn(dlb, qh[h])
            dv_ref[...] += _dot_tn(p.astype(BF16), doh[h])
        dq_ref[...] = (jnp.where(m0, dqs[0], dqs[1]) * Q_SCALE).astype(BF16)

    nb = MEM_W // LANES
    base = (3 * SB_W + 3 * FX_W) // LANES
    return pl.pallas_call(
        body, name=name, grid=(nb, s // TQM),
        in_specs=[pl.BlockSpec((TQM, LANES), lambda p, i: (i, base + p)),
                  pl.BlockSpec((n, LANES), lambda p, i: (0, p)),
                  pl.BlockSpec((n, LANES), lambda p, i: (0, nb + p)),
                  pl.BlockSpec((TQM, LANES), lambda p, i: (i, p)),
                  pl.BlockSpec((TQM, LANES), lambda p, i: (i, p)),
                  pl.BlockSpec((TQM, LANES), lambda p, i: (i, p))],
        out_specs=[pl.BlockSpec((TQM, LANES), lambda p, i: (i, p)),
                   pl.BlockSpec((n, LANES), lambda p, i: (0, p)),
                   pl.BlockSpec((n, LANES), lambda p, i: (0, p))],
        out_shape=[jax.ShapeDtypeStruct((s, MEM_W), BF16), jax.ShapeDtypeStruct((n, MEM_W), F32),
                   jax.ShapeDtypeStruct((n, MEM_W), F32)],
        compiler_params=_ARB2,
    )(qkv, kv, kv, o, lse, do)


def _memkv_bwd(mem, mnw, wkv, dk, dv, name):
    n = mem.shape[0]

    def body(mem_ref, mnw_ref, w_ref, dk_ref, dv_ref, dw_ref, dmnw_ref):
        mv = mem_ref[...]
        r = lax.rsqrt(jnp.mean(mv * mv, axis=-1, keepdims=True) + EPS)
        mh = mv * r
        hm = (mh * mnw_ref[...]).astype(BF16)
        dkv = jnp.concatenate([dk_ref[...], dv_ref[...]], axis=1).astype(BF16)
        dw_ref[...] = _dot_tn(hm, dkv)
        dhm = _dot_nt(dkv, w_ref[...])
        dmnw_ref[...] = jnp.sum(dhm * mh, axis=0, keepdims=True)

    return pl.pallas_call(
        body, name=name, grid=(1,),
        in_specs=[pl.BlockSpec((n, D_MODEL), lambda i: (0, 0)),
                  pl.BlockSpec((1, D_MODEL), lambda i: (0, 0)),
                  pl.BlockSpec((D_MODEL, 2 * MEM_W), lambda i: (0, 0)),
                  pl.BlockSpec((n, MEM_W), lambda i: (0, 0)),
                  pl.BlockSpec((n, MEM_W), lambda i: (0, 0))],
        out_specs=[pl.BlockSpec((D_MODEL, 2 * MEM_W), lambda i: (0, 0)),
                   pl.BlockSpec((1, D_MODEL), lambda i: (0, 0))],
        out_shape=[jax.ShapeDtypeStruct((D_MODEL, 2 * MEM_W), F32),
                   jax.ShapeDtypeStruct((1, D_MODEL), F32)],
        compiler_params=_ARB1,
    )(mem, mnw, wkv, dk, dv)


def _inproj_bwd_dx(pieces, w_r, x, nw, dxo, name):
    s = x.shape[0]
    n = len(pieces)
    widths = [p.shape[1] for p in pieces]

    def body(*refs):
        piece_refs = refs[:n]
        w_ref, x_ref, nw_ref, dxo_ref, dx_ref, h_ref, dnw_ref, dp_ref = refs[n:]

        @pl.when(pl.program_id(0) == 0)
        def _():
            dnw_ref[...] = jnp.zeros_like(dnw_ref)

        col = 0
        for r, wd in zip(piece_refs, widths):
            dp_ref[:, col:col + wd] = r[...]
            col += wd
        dp_ref[:, col:] = jnp.zeros((TM, WR_W - col), BF16)
        dh = _dot(dp_ref[...], w_ref[...])
        xv = x_ref[...]
        nw = nw_ref[...]
        r = lax.rsqrt(jnp.mean(xv * xv, axis=-1, keepdims=True) + EPS)
        xh = xv * r
        h_ref[...] = (xh * nw).astype(BF16)
        dnw_ref[...] += jnp.sum(dh * xh, axis=0, keepdims=True)
        dxh = dh * nw
        dx_ref[...] = r * (dxh - xh * jnp.mean(dxh * xh, axis=-1, keepdims=True)) + dxo_ref[...]

    return pl.pallas_call(
        body, name=name, grid=(s // TM,),
        in_specs=[pl.BlockSpec((TM, wd), lambda i: (i, 0)) for wd in widths]
        + [pl.BlockSpec((WR_W, D_MODEL), lambda i: (0, 0)),
           pl.BlockSpec((TM, D_MODEL), lambda i: (i, 0)),
           pl.BlockSpec((1, D_MODEL), lambda i: (0, 0)),
           pl.BlockSpec((TM, D_MODEL), lambda i: (i, 0))],
        out_specs=[pl.BlockSpec((TM, D_MODEL), lambda i: (i, 0)),
                   pl.BlockSpec((TM, D_MODEL), lambda i: (i, 0)),
                   pl.BlockSpec((1, D_MODEL), lambda i: (0, 0)),
                   pl.BlockSpec((TM, WR_W), lambda i: (i, 0))],
        out_shape=[jax.ShapeDtypeStruct((s, D_MODEL), F32), jax.ShapeDtypeStruct((s, D_MODEL), BF16),
                   jax.ShapeDtypeStruct((1, D_MODEL), F32), jax.ShapeDtypeStruct((s, WR_W), BF16)],
        compiler_params=_ARB1,
    )(*pieces, w_r, x, nw, dxo)


def _inproj_bwd_dw(h, dproj, name):
    s = dproj.shape[0]
    tn = 256

    def body(h_ref, dp_ref, dw_ref):
        dw_ref[...] = _dot_tn(dp_ref[...], h_ref[...])

    return pl.pallas_call(
        body, name=name, grid=(WR_W // tn,),
        in_specs=[pl.BlockSpec((s, D_MODEL), lambda j: (0, 0)),
                  pl.BlockSpec((s, tn), lambda j: (0, j))],
        out_specs=pl.BlockSpec((tn, D_MODEL), lambda j: (j, 0)),
        out_shape=jax.ShapeDtypeStruct((WR_W, D_MODEL), F32),
        compiler_params=_ARB1,
    )(h, dproj)


def _rearrange_w_in(wt):
    pad = jnp.zeros((FL_PAD - FOX_HEADS,) + wt.shape[1:], wt.dtype)
    return jnp.concatenate([wt[:3072], wt[3080:3336], wt[3336:IN_W], wt[3072:3080], pad], axis=0)


def _restore_w_in(g):
    gate0 = QKV_W
    fl0 = QKV_W + MIX_W
    return jnp.concatenate(
        [g[:3072], g[fl0:fl0 + FOX_HEADS], g[3072:QKV_W], g[gate0:fl0]], axis=0)


def _pad_lanes(v, width=LANES):
    return jnp.pad(v, (0, width - v.shape[0])).reshape(1, width)


def _layer_fwd(xs, mem, nw, w_r, b_forget, mnw, late, onw, l, travel=None):
    s = xs.shape[0]
    bpad = _pad_lanes(b_forget)
    qkv, gf = _inproj_fwd(xs, nw, w_r, f"inproj_fwd_{l}")
    fqb, frow, fbounds = _fox_prep_fwd(gf, bpad, f"fox_prep_fwd_{l}")
    frow = frow.reshape(FOX_HEADS // 2, 2, s)
    travel = _Travel(travel)
    ysb, _ = travel.ride(0, _sb_fwd, qkv, f"sb_fwd_{l}")
    yfx, lse_fx, _ = travel.ride(1, _fox_fwd, qkv, fqb, frow, fbounds, f"fox_fwd_{l}")
    wkv, wout = late(travel.lands)
    kv = _memkv_fwd(mem, mnw, wkv, f"memkv_fwd_{l}")
    ym, lse_m = _mem_fwd(qkv, kv, f"mem_fwd_{l}")
    xn = _outproj_fwd(ysb, yfx, ym, gf, onw, wout, xs, f"outproj_fwd_{l}")
    saved = (xs, nw, mnw, onw, bpad, qkv, gf, fqb, frow, fbounds, ysb, yfx, lse_fx, kv, ym, lse_m)
    return xn, saved, travel.lands, (wkv, wout)


class _Travel:
    def __init__(self, plan):
        self.plan = plan
        self.lands = None if plan is None else _new_lands(plan[0], plan[1])

    def ride(self, n, fn, *args):
        if self.plan is None or self.plan[2][n] is None:
            return fn(*args)
        srcs, scatter, legs = self.plan
        idx, rows = legs[n]
        out = fn(*args, rider=_Rider([srcs[a] for a in idx], [self.lands[a] for a in idx],
                                     scatter, rows))
        for a, land in zip(idx, out[-1]):
            self.lands[a] = land
        return out


def _layer_bwd(dx, saved, mem, w_r, wkv, wout, l, travel=None):
    xs, nw, mnw, onw, bpad, qkv, gf, fqb, frow, fbounds, ysb, yfx, lse_fx, kv, ym, lse_m = saved
    s = xs.shape[0]
    dysb, dyfx, dym, dgate, dwout, donw = _outproj_bwd(
        dx, wout, ysb, yfx, ym, gf, onw, f"outproj_bwd_{l}")
    travel = _Travel(None if travel is None else travel(dwout))
    sdq, sdk, sdv, _ = travel.ride(0, _sb_bwd, qkv, ysb, dysb, f"sb_bwd_{l}")
    fdq, fdk, fdv, dfrow, _ = travel.ride(1, _fox_bwd, qkv, fqb, frow, fbounds, yfx, lse_fx, dyfx,
                                          f"fox_bwd_{l}")
    dfl, db = _fox_prep_bwd(dfrow.reshape(FOX_HEADS, s), gf, bpad, f"fox_prep_bwd_{l}")
    dmq, dmk, dmv = _mem_bwd(qkv, kv, ym, lse_m, dym, f"mem_bwd_{l}")
    dwkv, dmnw = _memkv_bwd(mem, mnw, wkv, dmk, dmv, f"memkv_bwd_{l}")
    dx, ht, dnw, dproj = _inproj_bwd_dx([sdq, sdk, sdv, fdq, fdk, fdv, dmq, dgate, dfl],
                                        w_r, xs, nw, dx, f"inproj_bwd_dx_{l}")
    dwr = _inproj_bwd_dw(ht, dproj, f"inproj_bwd_dw_{l}")
    grads = dict(norm_w=dnw[0], w_r=dwr, b_forget=db[0, :FOX_HEADS], mem_norm_w=dmnw[0],
                 w_mem_kv=dwkv, out_norm_w=donw[0], w_out=dwout)
    return dx, grads, travel.lands


_ANY = pl.BlockSpec(memory_space=pl.ANY)


def _my_place():
    return lax.axis_index("x"), lax.axis_index("y"), lax.axis_index("c")


def _flip(v, bit):
    return 1 - v if bit else v


def _block_index(px, py, pc):
    return 4 * px + 2 * py + pc


def _all_gather_weights(shards, name):
    n = len(shards)

    def body(*refs):
        ins, outs = refs[:n], refs[n:2 * n]
        send_sems, recv_sems, local_sems = refs[2 * n:]
        x, y, c = _my_place()
        me = (x, y, c)
        sibling = (x, y, 1 - c)
        chips = [(1 - x, y), (x, 1 - y), (1 - x, 1 - y)]

        def copy(a, k, block, to, src=None):
            dst = outs[a].at[_block_index(*block)]
            return pltpu.make_async_remote_copy(
                src_ref=dst if src is None else src, dst_ref=dst,
                send_sem=send_sems.at[a, k], recv_sem=recv_sems.at[a, k],
                device_id=to, device_id_type=pl.DeviceIdType.MESH)

        mine = [pltpu.make_async_copy(ins[a], outs[a].at[_block_index(*me)], local_sems.at[a])
                for a in range(n)]
        for cp in mine:
            cp.start()
        first = []
        for a in range(n):
            first.append(copy(a, 0, me, sibling, src=ins[a]))
            first += [copy(a, 1 + j, me, (*chip, c), src=ins[a]) for j, chip in enumerate(chips)]
        for cp in first:
            cp.start()
        passed = []
        for j, chip in enumerate(chips):
            for a in range(n):
                copy(a, 1 + j, (*chip, c), me).wait_recv()
                fwd = copy(a, 4 + j, (*chip, c), sibling)
                fwd.start()
                passed.append(fwd)
        for a in range(n):
            copy(a, 0, sibling, me).wait_recv()
            for j, chip in enumerate(chips):
                copy(a, 4 + j, (*chip, 1 - c), me).wait_recv()
        for cp in first + passed:
            cp.wait_send()
        for cp in mine:
            cp.wait()

    return pl.pallas_call(
        body, name=name,
        in_specs=[_ANY] * n, out_specs=[_ANY] * n,
        out_shape=[jax.ShapeDtypeStruct((N_DEV,) + v.shape, v.dtype) for v in shards],
        scratch_shapes=[pltpu.SemaphoreType.DMA((n, 7)), pltpu.SemaphoreType.DMA((n, 7)),
                        pltpu.SemaphoreType.DMA((n,))],
    )(*shards)


def _exchange_blocks(blocked, name):
    n = len(blocked)

    def body(*refs):
        ins, outs = refs[:n], refs[n:2 * n]
        send_sems, recv_sems, local_sems = refs[2 * n:]
        x, y, c = _my_place()
        mine_idx = _block_index(x, y, c)
        local = [pltpu.make_async_copy(ins[a].at[mine_idx], outs[a].at[mine_idx], local_sems.at[a])
                 for a in range(n)]
        for cp in local:
            cp.start()
        sends, arrivals = [], []
        for r in range(1, N_DEV):
            peer = (_flip(x, r & 4), _flip(y, r & 2), _flip(c, r & 1))
            peer_idx = _block_index(*peer)
            for a in range(n):
                sems = dict(send_sem=send_sems.at[a, r - 1], recv_sem=recv_sems.at[a, r - 1],
                            device_id=peer, device_id_type=pl.DeviceIdType.MESH)
                sends.append(pltpu.make_async_remote_copy(
                    src_ref=ins[a].at[peer_idx], dst_ref=outs[a].at[mine_idx], **sems))
                arrivals.append(pltpu.make_async_remote_copy(
                    src_ref=ins[a].at[peer_idx], dst_ref=outs[a].at[peer_idx], **sems))
        for cp in sends:
            cp.start()
        for cp in arrivals:
            cp.wait_recv()
        for cp in sends:
            cp.wait_send()
        for cp in local:
            cp.wait()

    return pl.pallas_call(
        body, name=name,
        in_specs=[_ANY] * n, out_specs=[_ANY] * n,
        out_shape=[jax.ShapeDtypeStruct(v.shape, v.dtype) for v in blocked],
        scratch_shapes=[pltpu.SemaphoreType.DMA((n, 7)), pltpu.SemaphoreType.DMA((n, 7)),
                        pltpu.SemaphoreType.DMA((n,))],
    )(*blocked)


class _Rider(NamedTuple):
    srcs: list
    lands: list
    scatter: bool
    part: list


def _new_lands(srcs, scatter):
    return [lax.empty(v.shape if scatter else (N_DEV,) + v.shape, v.dtype) for v in srcs]


def _rider_copies(srcs, lands, send_sems, recv_sems, rider):
    x, y, c = _my_place()
    mine_idx = _block_index(x, y, c)

    def window(ref, a):
        if rider.part[a] is None:
            return ref
        dim, start, size = rider.part[a]
        return ref.at[(slice(None),) * dim + (pl.ds(start, size),)]

    sends, arrivals = [], []
    for r in range(1, N_DEV):
        peer = (_flip(x, r & 4), _flip(y, r & 2), _flip(c, r & 1))
        peer_idx = _block_index(*peer)
        for a in range(len(srcs)):
            src = window(srcs[a].at[peer_idx] if rider.scatter else srcs[a], a)
            k = 7 * a + r - 1
            sems = dict(send_sem=send_sems.at[k], recv_sem=recv_sems.at[k],
                        device_id=peer, device_id_type=pl.DeviceIdType.MESH)
            sends.append(pltpu.make_async_remote_copy(
                src_ref=src, dst_ref=window(lands[a].at[mine_idx], a), **sems))
            arrivals.append(pltpu.make_async_remote_copy(
                src_ref=src, dst_ref=window(lands[a].at[peer_idx], a), **sems))
    return sends, arrivals


def _ride(call, rider):
    call = dict(call)
    body, grid = call.pop("body"), call["grid"]
    operands = call.pop("operands")
    if rider is None:
        return list(pl.pallas_call(body, **call)(*operands)), None
    n_in, n_out = len(call["in_specs"]), len(call["out_specs"])
    n_scratch = len(call["scratch_shapes"])
    m = len(rider.srcs)

    def riding(*refs):
        main_in, srcs, lands = refs[:n_in], refs[n_in:n_in + m], refs[n_in + m:n_in + 2 * m]
        main_out = refs[n_in + 2 * m:n_in + 2 * m + n_out]
        rest = refs[n_in + 3 * m + n_out:]
        send_sems, recv_sems = rest[n_scratch:]
        at = [pl.program_id(d) for d in range(len(grid))]
        first = functools.reduce(jnp.logical_and, [p == 0 for p in at])
        last = functools.reduce(jnp.logical_and, [p == g - 1 for p, g in zip(at, grid)])
        sends, arrivals = _rider_copies(srcs, lands, send_sems, recv_sems, rider)

        @pl.when(first)
        def _():
            for cp in sends:
                cp.start()

        body(*main_in, *main_out, *rest[:n_scratch])

        @pl.when(last)
        def _():
            for cp in arrivals:
                cp.wait_recv()
            for cp in sends:
                cp.wait_send()

    call["in_specs"] = list(call["in_specs"]) + [_ANY] * (2 * m)
    call["out_specs"] = list(call["out_specs"]) + [_ANY] * m
    call["out_shape"] = list(call["out_shape"]) + [
        jax.ShapeDtypeStruct(v.shape, v.dtype) for v in rider.lands]
    call["scratch_shapes"] = list(call["scratch_shapes"]) + [
        pltpu.SemaphoreType.DMA((7 * m,)), pltpu.SemaphoreType.DMA((7 * m,))]
    call["input_output_aliases"] = {n_in + m + a: n_out + a for a in range(m)}
    outs = pl.pallas_call(riding, **call)(*operands, *rider.srcs, *rider.lands)
    return list(outs[:n_out]), list(outs[n_out:])


def _sum_parts(p_ref):
    g = p_ref[0].astype(F32)
    for k in range(1, N_DEV):
        g = g + p_ref[k].astype(F32)
    return g


def _adamw(g, w, m, v):
    c1 = 1.0 / (1.0 - ADAM_B1 ** ADAM_STEP)
    c2 = 1.0 / (1.0 - ADAM_B2 ** ADAM_STEP)
    nm = ADAM_B1 * m + (1.0 - ADAM_B1) * g
    nv = ADAM_B2 * v + (1.0 - ADAM_B2) * (g * g)
    return nm, nv, -ADAM_LR * ((nm * c1) / (jnp.sqrt(nv * c2) + ADAM_EPS) + ADAM_WD * w)


def _adamw_w_in(parts, w, m, v, name):
    ncol_blk, depth, nfeat = w.shape
    cols = 256

    def body(*refs):
        p_refs = refs[:depth]
        w_ref, m_ref, v_ref, g_ref, d_ref, nm_ref, nv_ref = refs[depth:]
        for l in range(depth):
            g = _sum_parts(p_refs[l])
            nm, nv, d = _adamw(g, w_ref[:, l, :], m_ref[:, l, :], v_ref[:, l, :])
            g_ref[:, l, :] = g
            nm_ref[:, l, :] = nm
            nv_ref[:, l, :] = nv
            d_ref[:, l, :] = d

    blk = pl.BlockSpec((ncol_blk, depth, cols), lambda j: (0, 0, j))
    return pl.pallas_call(
        body, name=name, grid=(nfeat // cols,),
        in_specs=[pl.BlockSpec((N_DEV, ncol_blk, cols), lambda j: (0, 0, j))] * depth + [blk] * 3,
        out_specs=[blk] * 4,
        out_shape=[jax.ShapeDtypeStruct(w.shape, F32)] * 4,
        compiler_params=_ARB1,
    )(*parts, w, m, v)


def _adamw_sum(parts, w, m, v, tile, name):
    depth, nrow, ncol = w.shape
    rows, cols = tile

    def body(*refs):
        p_refs = refs[:depth]
        w_ref, m_ref, v_ref, g_ref, d_ref, nm_ref, nv_ref = refs[depth:]
        layer = pl.program_id(0)
        for l in range(depth):
            @pl.when(layer == l)
            def _(p_ref=p_refs[l]):
                g = _sum_parts(p_ref)
                nm, nv, d = _adamw(g, w_ref[...], m_ref[...], v_ref[...])
                g_ref[...] = g
                nm_ref[...] = nm
                nv_ref[...] = nv
                d_ref[...] = d

    def part_spec(l):
        return pl.BlockSpec((N_DEV, rows, cols), lambda q, i, j: (
            0, jnp.where(q == l, i, 0), jnp.where(q == l, j, 0)))

    blk = pl.BlockSpec((None, rows, cols), lambda q, i, j: (q, i, j))
    return pl.pallas_call(
        body, name=name, grid=(depth, nrow // rows, ncol // cols),
        in_specs=[part_spec(l) for l in range(depth)] + [blk, blk, blk],
        out_specs=[blk] * 4,
        out_shape=[jax.ShapeDtypeStruct(w.shape, F32)] * 4,
        compiler_params=pltpu.CompilerParams(
            dimension_semantics=("arbitrary", "arbitrary", "arbitrary")),
    )(*parts, w, m, v)


def _pack_small(norm_w, mem_norm_w, out_norm_w, final_norm_w, b_forget):
    onw = jnp.pad(out_norm_w.reshape(20, LANES), ((0, 4), (0, 0)))
    b = jnp.pad(b_forget, ((0, 6), (0, LANES - FOX_HEADS)))
    return jnp.concatenate([norm_w.reshape(16, LANES), mem_norm_w.reshape(16, LANES), onw,
                            final_norm_w.reshape(8, LANES), b], axis=0)


def _unpack_small(p):
    return (p[0:16].reshape(2, D_MODEL), p[16:32].reshape(2, D_MODEL), p[32:52].reshape(2, MIX_W),
            p[56:64].reshape(D_MODEL), p[64:66, :FOX_HEADS])


def kernel(x, mem, norm_w, w_in, b_forget, mem_norm_w, w_mem_kv, out_norm_w, w_out, final_norm_w, loss_target, m_norm_w, m_w_in, m_b_forget, m_mem_norm_w, m_w_mem_kv, m_out_norm_w, m_w_out, m_final_norm_w, v_norm_w, v_w_in, v_b_forget, v_mem_norm_w, v_w_mem_kv, v_out_norm_w, v_w_out, v_final_norm_w):
    kv_rows = w_mem_kv.shape[1]
    out_rows = w_out.shape[1]
    me = _block_index(*_my_place())

    def shards(l):
        return [w_in[l].T.astype(BF16), w_mem_kv[l].astype(BF16), w_out[l].astype(BF16)]

    def full_in(g_in):
        return _rearrange_w_in(g_in.reshape(IN_W, D_MODEL))

    def full_kv_out(g_kv, g_out):
        return g_kv.reshape(D_MODEL, 2 * MEM_W), g_out.reshape(MIX_W, D_MODEL)

    def in_blocks(g):
        segments = [(0, 3072, 0), (3072, 3080, QKV_W + MIX_W), (3080, 3336, 3072),
                    (3336, IN_W, QKV_W)]

        def block(k):
            lo, hi = k * SHARD_W, (k + 1) * SHARD_W
            pieces = [g[at + max(lo, a) - a:at + min(hi, b) - a]
                      for a, b, at in segments if max(lo, a) < min(hi, b)]
            return jnp.concatenate(pieces, axis=0).astype(BF16)

        return jnp.stack([block(k) for k in range(N_DEV)])

    def kv_blocks(g):
        return g.reshape(N_DEV, kv_rows, 2 * MEM_W).astype(BF16)

    def out_blocks(g):
        return g.reshape(N_DEV, out_rows, D_MODEL).astype(BF16)

    def with_own(land, own):
        return lax.dynamic_update_slice(land, own[None], (me,) + (0,) * own.ndim)

    def with_own_of(land, blocked):
        return lax.dynamic_update_slice(land, lax.dynamic_slice_in_dim(blocked, me, 1, axis=0),
                                        (me,) + (0,) * (land.ndim - 1))

    def row(v):
        return v.reshape(1, -1)

    def cols(first, size):
        return (1, first, size)

    half = D_MODEL // 2

    g_in0, g_kv0, g_out0 = _all_gather_weights(shards(0), "all_gather_l0")
    w_r0 = full_in(g_in0)
    s_in1, s_kv1, s_out1 = shards(1)
    x1, saved0, (l_in1,), (wkv0, wout0) = _layer_fwd(
        x[0], mem[0], row(norm_w[0]), w_r0, b_forget[0], row(mem_norm_w[0]),
        lambda lands: full_kv_out(g_kv0, g_out0), row(out_norm_w[0]), 0,
        travel=([s_in1], False, [([0], [cols(0, half)]), ([0], [cols(half, half)])]))
    w_r1 = full_in(with_own(l_in1, s_in1))
    x2, saved1, _, (wkv1, wout1) = _layer_fwd(
        x1, mem[0], row(norm_w[1]), w_r1, b_forget[1], row(mem_norm_w[1]),
        lambda lands: full_kv_out(with_own(lands[0], s_kv1), with_own(lands[1], s_out1)),
        row(out_norm_w[1]), 1, travel=([s_kv1, s_out1], False, [([0, 1], [None, None]), None]))

    dx2, loss_part, dfnw = _final_fwd_bwd(x2, row(final_norm_w), loss_target[0], "final_fwd_bwd")

    dx1, gr1, (l_out1,) = _layer_bwd(
        dx2, saved1, mem[0], w_r1, wkv1, wout1, 1,
        travel=lambda dwout: ([out_blocks(dwout)], True, [([0], [None]), None]))
    p_in1, p_kv1 = in_blocks(gr1["w_r"]), kv_blocks(gr1["w_mem_kv"])
    split = 5 * LANES
    grad_x, gr0, (l_in1, l_kv1, l_out0) = _layer_bwd(
        dx1, saved0, mem[0], w_r0, wkv0, wout0, 0,
        travel=lambda dwout: ([p_in1, p_kv1, out_blocks(dwout)], True,
                              [([0, 1], [cols(0, split), None]),
                               ([0, 2], [cols(split, D_MODEL - split), None])]))
    r_out1 = with_own_of(l_out1, out_blocks(gr1["w_out"]))
    r_in1, r_kv1 = with_own_of(l_in1, p_in1), with_own_of(l_kv1, p_kv1)
    r_out0 = with_own_of(l_out0, out_blocks(gr0["w_out"]))

    def both(name):
        return jnp.stack([gr0[name], gr1[name]])

    small = _pack_small(both("norm_w"), both("mem_norm_w"), both("out_norm_w"), dfnw[0],
                        both("b_forget")).at[LOSS_ROW].set(loss_part[0])
    p_small = jnp.broadcast_to(small[None], (N_DEV, SMALL_ROWS, LANES))
    r_in0, r_kv0, r_small = _exchange_blocks(
        [in_blocks(gr0["w_r"]), kv_blocks(gr0["w_mem_kv"]), p_small], "exchange_grads_l0")

    def view(v):
        return jnp.transpose(v, (2, 0, 1))

    g_w_in, d_w_in, nm_w_in, nv_w_in = [jnp.transpose(v, (1, 2, 0)) for v in _adamw_w_in(
        [r_in0, r_in1], view(w_in), view(m_w_in), view(v_w_in), "adamw_w_in")]
    g_w_kv, d_w_kv, nm_w_kv, nv_w_kv = _adamw_sum(
        [r_kv0, r_kv1], w_mem_kv, m_w_mem_kv, v_w_mem_kv, (kv_rows, 2 * MEM_W), "adamw_w_mem_kv")
    g_w_out, d_w_out, nm_w_out, nv_w_out = _adamw_sum(
        [r_out0, r_out1], w_out, m_w_out, v_w_out, (out_rows, D_MODEL), "adamw_w_out")
    w_small = _pack_small(norm_w, mem_norm_w, out_norm_w, final_norm_w, b_forget)[None]
    m_small = _pack_small(m_norm_w, m_mem_norm_w, m_out_norm_w, m_final_norm_w, m_b_forget)[None]
    v_small = _pack_small(v_norm_w, v_mem_norm_w, v_out_norm_w, v_final_norm_w, v_b_forget)[None]
    small_out = _adamw_sum([r_small], w_small, m_small, v_small, (SMALL_ROWS, LANES), "adamw_small")
    (g_nw, g_mnw, g_onw, g_fnw, g_b), (d_nw, d_mnw, d_onw, d_fnw, d_b), \
        (nm_nw, nm_mnw, nm_onw, nm_fnw, nm_b), (nv_nw, nv_mnw, nv_onw, nv_fnw, nv_b) = [
            _unpack_small(t[0]) for t in small_out]
    loss = small_out[0][0, LOSS_ROW, 0]

    return (loss, grad_x[None],
            g_nw, g_w_in, g_b, g_mnw, g_w_kv, g_onw, g_w_out, g_fnw,
            d_nw, d_w_in, d_b, d_mnw, d_w_kv, d_onw, d_w_out, d_fnw,
            nm_nw, nm_w_in, nm_b, nm_mnw, nm_w_kv, nm_onw, nm_w_out, nm_fnw,
            nv_nw, nv_w_in, nv_b, nv_mnw, nv_w_kv, nv_onw, nv_w_out, nv_fnw)
```

```python
import functools
from typing import NamedTuple

import jax
import jax.numpy as jnp
from jax import lax
from jax.experimental import pallas as pl
from jax.experimental.pallas import tpu as pltpu

F32 = jnp.float32
BF16 = jnp.bfloat16

N_DEV = 8
D_MODEL = 1024
HEAD_DIM = 64
LANES = 128
SB_W = 512
FX_W = 512
MEM_W = 256
MIX_W = 1280
FOX_HEADS = 8
IN_W = 4616
SHARD_W = IN_W // N_DEV
QKV_W = 3 * SB_W + 3 * FX_W + MEM_W
FL_PAD = 256
GF_W = MIX_W + FL_PAD
WR_W = QKV_W + GF_W
EPS = 1e-6
T = 256
TM = 256
TQM = 512
Q_SCALE = 0.125
NEG = -1e30
UNDERFLOW = -110.0

ADAM_LR = 0.001
ADAM_B1 = 0.9
ADAM_B2 = 0.999
ADAM_EPS = 1e-08
ADAM_WD = 0.01
ADAM_STEP = 10

SMALL_ROWS = 72
LOSS_ROW = 66

_NT = (((1,), (1,)), ((), ()))
_TN = (((0,), (0,)), ((), ()))

_ARB1 = pltpu.CompilerParams(dimension_semantics=("arbitrary",))
_ARB2 = pltpu.CompilerParams(dimension_semantics=("arbitrary", "arbitrary"))


def _dot(a, b):
    return jnp.dot(a, b, preferred_element_type=F32)


def _dot_nt(a, b):
    return lax.dot_general(a, b, _NT, preferred_element_type=F32)


def _dot_tn(a, b):
    return lax.dot_general(a, b, _TN, preferred_element_type=F32)


def _split2(x):
    hi = x.astype(BF16)
    lo = (x - hi.astype(F32)).astype(BF16)
    return hi, lo


def _stack2(u):
    return jnp.concatenate([u, u], axis=0)


def _cum2(x, u2):
    hi, lo = _split2(x)
    return _dot(jnp.concatenate([hi, lo], axis=1), u2)


def _tri3(tri, x, dot=None):
    dot = dot or _dot
    hi = x.astype(BF16)
    r1 = x - hi.astype(F32)
    mid = r1.astype(BF16)
    lo = (r1 - mid.astype(F32)).astype(BF16)
    return dot(tri, hi) + dot(tri, mid) + dot(tri, lo)


def _iota2(shape, dim):
    return lax.broadcasted_iota(jnp.int32, shape, dim)


def _head_block_diag():
    r = _iota2((LANES, LANES), 0) // HEAD_DIM
    c = _iota2((LANES, LANES), 1) // HEAD_DIM
    return _stack2(jnp.where(r == c, 1.0, 0.0).astype(BF16))


def _head_mean(x, bd):
    return _cum2(x, bd) * (1.0 / HEAD_DIM)


def _sigmoid(x):
    return 1.0 / (1.0 + jnp.exp(-x))


def _log_sigmoid(x):
    return jnp.minimum(x, 0.0) - jnp.log(1.0 + jnp.exp(-jnp.abs(x)))


def _running_top(r_ref):
    return jnp.max(jnp.maximum(r_ref[0], r_ref[1]))


def _head_norm_tops(x, m0):
    x2 = x.astype(F32)
    x2 = x2 * x2
    zero = jnp.zeros_like(x2)
    return (jnp.max(jnp.sqrt(jnp.sum(jnp.where(m0, x2, zero), axis=1, keepdims=True))),
            jnp.max(jnp.sqrt(jnp.sum(jnp.where(m0, zero, x2), axis=1, keepdims=True))))


def _fox_tiles_left(i, pair, qn, kn_ref, fb_ref, tile):
    def bound(j):
        b = [2.0 * qn[h] * kn_ref[h] + fb_ref[2 * i, 2 * pair + h] - fb_ref[2 * j + 1, 2 * pair + h]
             for h in range(2)]
        return jnp.maximum(b[0], b[1])

    def more(j):
        return jnp.logical_and(j >= 0, bound(jnp.maximum(j, 0)) > UNDERFLOW)

    def step(j):
        tile(j, False)
        return j - 1

    return lax.while_loop(more, step, i - 1)


def _pair_masks():
    lane = _iota2((1, LANES), 1)
    return lane < HEAD_DIM


def _split_pair(x, m0):
    zero = jnp.zeros_like(x)
    return jnp.where(m0, x, zero), jnp.where(m0, zero, x)


def _inproj_fwd(x, nw, w_r, name):
    s = x.shape[0]

    def body(x_ref, nw_ref, w_ref, qkv_ref, gf_ref):
        xv = x_ref[...]
        r = lax.rsqrt(jnp.mean(xv * xv, axis=-1, keepdims=True) + EPS)
        h = (xv * r * nw_ref[...]).astype(BF16)
        for c in range(0, QKV_W, 256):
            qkv_ref[:, c:c + 256] = _dot_nt(h, w_ref[c:c + 256, :]).astype(BF16)
        for c in range(0, GF_W, 256):
            gf_ref[:, c:c + 256] = _dot_nt(h, w_ref[QKV_W + c:QKV_W + c + 256, :])

    return pl.pallas_call(
        body, name=name, grid=(s // TM,),
        in_specs=[pl.BlockSpec((TM, D_MODEL), lambda i: (i, 0)),
                  pl.BlockSpec((1, D_MODEL), lambda i: (0, 0)),
                  pl.BlockSpec((WR_W, D_MODEL), lambda i: (0, 0))],
        out_specs=[pl.BlockSpec((TM, QKV_W), lambda i: (i, 0)),
                   pl.BlockSpec((TM, GF_W), lambda i: (i, 0))],
        out_shape=[jax.ShapeDtypeStruct((s, QKV_W), BF16), jax.ShapeDtypeStruct((s, GF_W), F32)],
        compiler_params=_ARB1,
    )(x, nw, w_r)


def _fox_prep_fwd(gf, bpad, name):
    s = gf.shape[0]

    def body(fl_ref, b_ref, fq_ref, fr_ref, fb_ref):
        tri = jnp.where(_iota2((T, T), 0) >= _iota2((T, T), 1), 1.0, 0.0).astype(BF16)
        m0 = _pair_masks()
        carry = jnp.zeros((1, LANES), F32)
        for blk in range(s // T):
            rows = slice(blk * T, (blk + 1) * T)
            lf = _log_sigmoid(fl_ref[rows, :] + b_ref[...])
            c = _tri3(tri, lf) + carry
            carry = c[T - 1:T, :]
            for p in range(FOX_HEADS // 2):
                fq_ref[rows, p * LANES:(p + 1) * LANES] = jnp.where(
                    m0, c[:, 2 * p:2 * p + 1], c[:, 2 * p + 1:2 * p + 2])
            fr_ref[:, rows] = c.T[0:FOX_HEADS, :]
            fb_ref[2 * blk:2 * blk + 1, :] = c[0:1, :]
            fb_ref[2 * blk + 1:2 * blk + 2, :] = carry

    return pl.pallas_call(
        body, name=name, grid=(1,),
        in_specs=[pl.BlockSpec((s, LANES), lambda i: (0, MIX_W // LANES)),
                  pl.BlockSpec((1, LANES), lambda i: (0, 0))],
        out_specs=[pl.BlockSpec((s, FX_W), lambda i: (0, 0)),
                   pl.BlockSpec((FOX_HEADS, s), lambda i: (0, 0)),
                   pl.BlockSpec((2 * s // T, LANES), lambda i: (0, 0))],
        out_shape=[jax.ShapeDtypeStruct((s, FX_W), F32), jax.ShapeDtypeStruct((FOX_HEADS, s), F32),
                   jax.ShapeDtypeStruct((2 * s // T, LANES), F32)],
        compiler_params=_ARB1,
    )(gf, bpad)


def _sb_fwd(qkv, name, rider=None):
    s = qkv.shape[0]

    def body(q_ref, k_ref, v_ref, o_ref, acc_ref, r_ref, as_ref):
        i = pl.program_id(1)
        m0 = _pair_masks()
        qh = _split_pair(q_ref[...] * jnp.asarray(Q_SCALE, BF16), m0)
        strict = _iota2((T, T), 0) > _iota2((T, T), 1)
        u2 = _stack2(jnp.where(strict, 1.0, 0.0).astype(BF16))
        acc_ref[...] = jnp.zeros_like(acc_ref)
        r_ref[...] = jnp.zeros_like(r_ref)
        hs = range(2)

        def flush(j):
            v = v_ref[pl.ds(pl.multiple_of(j * T, T), T), :]
            for h in hs:
                acc_ref[h] += _dot(as_ref[h], v)

        def tile(j, diag):
            k = k_ref[pl.ds(pl.multiple_of(j * T, T), T), :]
            z = [_dot_nt(qh[h], k) for h in hs]
            if not diag:
                flush(j + 1)
            la = [jnp.minimum(z[h], 0.0) - jnp.log(1.0 + jnp.exp(-jnp.abs(z[h]))) for h in hs]
            lf = [la[h] - z[h] for h in hs]
            if diag:
                lf = [jnp.where(strict, lf[h], 0.0) for h in hs]
            cin = [_cum2(lf[h], u2) for h in hs]
            a = [jnp.exp(la[h] + cin[h] + r_ref[h]) for h in hs]
            if diag:
                a = [jnp.where(strict, a[h], 0.0) for h in hs]
            for h in hs:
                r_ref[h] += cin[h][:, 0:1] + lf[h][:, 0:1]
                as_ref[h] = a[h].astype(BF16)

        tile(i, True)

        def more(state):
            j, top = state
            return jnp.logical_and(j >= 0, top > UNDERFLOW)

        def step(state):
            j, _ = state
            tile(j, False)
            return j - 1, _running_top(r_ref)

        j_left, _ = lax.while_loop(more, step, (i - 1, _running_top(r_ref)))
        flush(j_left + 1)
        o_ref[...] = jnp.where(m0, acc_ref[0], acc_ref[1])

    nb = SB_W // LANES
    (ysb,), lands = _ride(dict(
        body=body, name=name, grid=(nb, s // T),
        in_specs=[pl.BlockSpec((T, LANES), lambda p, i: (i, p)),
                  pl.BlockSpec((s, LANES), lambda p, i: (0, nb + p)),
                  pl.BlockSpec((s, LANES), lambda p, i: (0, 2 * nb + p))],
        out_specs=[pl.BlockSpec((T, LANES), lambda p, i: (i, p))],
        out_shape=[jax.ShapeDtypeStruct((s, SB_W), F32)],
        scratch_shapes=[pltpu.VMEM((2, T, LANES), F32), pltpu.VMEM((2, T, 1), F32),
                        pltpu.VMEM((2, T, T), BF16)],
        compiler_params=_ARB2, operands=[qkv, qkv, qkv]), rider)
    return ysb, lands


def _fox_fwd(qkv, fqb, frow, fbounds, name, rider=None):
    s = qkv.shape[0]

    def body(q_ref, k_ref, v_ref, fq_ref, fr_ref, fb_ref, o_ref, lse_ref, acc_ref, m_ref, ps_ref,
             kn_ref):
        pair = pl.program_id(0)
        i = pl.program_id(1)
        m0 = _pair_masks()

        @pl.when(i == 0)
        def _():
            kn_ref[0], kn_ref[1] = _head_norm_tops(k_ref[...], m0)

        q = q_ref[...] * jnp.asarray(Q_SCALE, BF16)
        qn = _head_norm_tops(q, m0)
        qh = _split_pair(q, m0)
        fq = fq_ref[...]
        fqh = (fq[:, 0:1], fq[:, HEAD_DIM:HEAD_DIM + 1])
        causal = _iota2((T, T), 0) >= _iota2((T, T), 1)
        ones = jnp.ones((T, LANES), BF16)
        acc_ref[...] = jnp.zeros_like(acc_ref)
        m_ref[...] = jnp.full_like(m_ref, NEG)
        hs = range(2)

        def flush(j):
            v = v_ref[pl.ds(pl.multiple_of(j * T, T), T), :]
            va2 = _stack2(jnp.concatenate([v, ones], axis=1))
            for h in hs:
                acc_ref[h] += _dot(ps_ref[h], va2)

        def tile(j, diag):
            off = pl.multiple_of(j * T, T)
            k = k_ref[pl.ds(off, T), :]
            sc = [_dot_nt(qh[h], k) + fqh[h] - fr_ref[h:h + 1, pl.ds(off, T)] for h in hs]
            if not diag:
                flush(j + 1)
            if diag:
                sc = [jnp.where(causal, sc[h], NEG) for h in hs]
            m_new = [jnp.maximum(m_ref[h], jnp.max(sc[h], axis=1, keepdims=True)) for h in hs]
            p = [jnp.exp(sc[h] - m_new[h]) for h in hs]
            for h in hs:
                acc_ref[h] = acc_ref[h] * jnp.exp(m_ref[h] - m_new[h])
                m_ref[h] = m_new[h]
                p_hi, p_lo = _split2(p[h])
                ps_ref[h] = jnp.concatenate([p_hi, p_lo], axis=1)

        tile(i, True)
        j_left = _fox_tiles_left(i, pair, qn, kn_ref, fb_ref, tile)
        flush(j_left + 1)
        acc = (acc_ref[0], acc_ref[1])
        o_ref[...] = jnp.where(m0, acc[0][:, :LANES] / acc[0][:, LANES:],
                               acc[1][:, :LANES] / acc[1][:, LANES:])
        lse_ref[...] = jnp.where(m0, m_ref[0] + jnp.log(acc[0][:, LANES:]),
                                 m_ref[1] + jnp.log(acc[1][:, LANES:]))

    nb = FX_W // LANES
    base = 3 * SB_W // LANES
    (yfx, lse), lands = _ride(dict(
        body=body, name=name, grid=(nb, s // T),
        in_specs=[pl.BlockSpec((T, LANES), lambda p, i: (i, base + p)),
                  pl.BlockSpec((s, LANES), lambda p, i: (0, base + nb + p)),
                  pl.BlockSpec((s, LANES), lambda p, i: (0, base + 2 * nb + p)),
                  pl.BlockSpec((T, LANES), lambda p, i: (i, p)),
                  pl.BlockSpec((None, 2, s), lambda p, i: (p, 0, 0)),
                  pl.BlockSpec(memory_space=pltpu.SMEM)],
        out_specs=[pl.BlockSpec((T, LANES), lambda p, i: (i, p)),
                   pl.BlockSpec((T, LANES), lambda p, i: (i, p))],
        out_shape=[jax.ShapeDtypeStruct((s, FX_W), F32), jax.ShapeDtypeStruct((s, FX_W), F32)],
        scratch_shapes=[pltpu.VMEM((2, T, 2 * LANES), F32), pltpu.VMEM((2, T, 1), F32),
                        pltpu.VMEM((2, T, 2 * T), BF16), pltpu.SMEM((2,), F32)],
        compiler_params=_ARB2, operands=[qkv, qkv, qkv, fqb, frow, fbounds]), rider)
    return yfx, lse, lands


def _memkv_fwd(mem, mnw, wkv, name):
    n = mem.shape[0]

    def body(mem_ref, mnw_ref, w_ref, kv_ref):
        mv = mem_ref[...]
        r = lax.rsqrt(jnp.mean(mv * mv, axis=-1, keepdims=True) + EPS)
        hm = (mv * r * mnw_ref[...]).astype(BF16)
        kv_ref[...] = _dot(hm, w_ref[...]).astype(BF16)

    return pl.pallas_call(
        body, name=name, grid=(1,),
        in_specs=[pl.BlockSpec((n, D_MODEL), lambda i: (0, 0)),
                  pl.BlockSpec((1, D_MODEL), lambda i: (0, 0)),
                  pl.BlockSpec((D_MODEL, 2 * MEM_W), lambda i: (0, 0))],
        out_specs=pl.BlockSpec((n, 2 * MEM_W), lambda i: (0, 0)),
        out_shape=jax.ShapeDtypeStruct((n, 2 * MEM_W), BF16),
        compiler_params=_ARB1,
    )(mem, mnw, wkv)


def _mem_fwd(qkv, kv, name):
    s = qkv.shape[0]
    n = kv.shape[0]

    def body(q_ref, k_ref, v_ref, o_ref, lse_ref):
        m0 = _pair_masks()
        qh = _split_pair(q_ref[...] * jnp.asarray(Q_SCALE, BF16), m0)
        k = k_ref[...]
        v = v_ref[...]
        outs, lses = [], []
        for h in range(2):
            sc = _dot_nt(qh[h], k)
            mx = jnp.max(sc, axis=1, keepdims=True)
            p = jnp.exp(sc - mx)
            l = jnp.sum(p, axis=1, keepdims=True)
            outs.append(_dot(p.astype(BF16), v) / l)
            lses.append(mx + jnp.log(l))
        o_ref[...] = jnp.where(m0, outs[0], outs[1])
        lse_ref[...] = jnp.where(m0, lses[0], lses[1])

    nb = MEM_W // LANES
    base = (3 * SB_W + 3 * FX_W) // LANES
    return pl.pallas_call(
        body, name=name, grid=(nb, s // TQM),
        in_specs=[pl.BlockSpec((TQM, LANES), lambda p, i: (i, base + p)),
                  pl.BlockSpec((n, LANES), lambda p, i: (0, p)),
                  pl.BlockSpec((n, LANES), lambda p, i: (0, nb + p))],
        out_specs=[pl.BlockSpec((TQM, LANES), lambda p, i: (i, p)),
                   pl.BlockSpec((TQM, LANES), lambda p, i: (i, p))],
        out_shape=[jax.ShapeDtypeStruct((s, MEM_W), F32), jax.ShapeDtypeStruct((s, MEM_W), F32)],
        compiler_params=_ARB2,
    )(qkv, kv, kv)


def _mix_chunk(c, ysb_ref, yfx_ref, ym_ref):
    if c < SB_W // LANES:
        return ysb_ref[:, c * LANES:(c + 1) * LANES]
    c -= SB_W // LANES
    if c < FX_W // LANES:
        return yfx_ref[:, c * LANES:(c + 1) * LANES]
    c -= FX_W // LANES
    return ym_ref[:, c * LANES:(c + 1) * LANES]


def _outproj_fwd(ysb, yfx, ym, gf, onw, wout, x, name):
    s = x.shape[0]

    def body(ysb_ref, yfx_ref, ym_ref, g_ref, onw_ref, w_ref, x_ref, o_ref, yg_ref):
        bd = _head_block_diag()
        for c in range(MIX_W // LANES):
            sl = slice(c * LANES, (c + 1) * LANES)
            u = _mix_chunk(c, ysb_ref, yfx_ref, ym_ref)
            r = lax.rsqrt(_head_mean(u * u, bd) + EPS)
            g = g_ref[:, sl]
            yg_ref[:, sl] = (u * r * onw_ref[:, sl] * (g * _sigmoid(g))).astype(BF16)
        o_ref[...] = x_ref[...] + _dot(yg_ref[...], w_ref[...])

    return pl.pallas_call(
        body, name=name, grid=(s // TM,),
        in_specs=[pl.BlockSpec((TM, SB_W), lambda i: (i, 0)),
                  pl.BlockSpec((TM, FX_W), lambda i: (i, 0)),
                  pl.BlockSpec((TM, MEM_W), lambda i: (i, 0)),
                  pl.BlockSpec((TM, MIX_W), lambda i: (i, 0)),
                  pl.BlockSpec((1, MIX_W), lambda i: (0, 0)),
                  pl.BlockSpec((MIX_W, D_MODEL), lambda i: (0, 0)),
                  pl.BlockSpec((TM, D_MODEL), lambda i: (i, 0))],
        out_specs=pl.BlockSpec((TM, D_MODEL), lambda i: (i, 0)),
        out_shape=jax.ShapeDtypeStruct((s, D_MODEL), F32),
        scratch_shapes=[pltpu.VMEM((TM, MIX_W), BF16)],
        compiler_params=_ARB1,
    )(ysb, yfx, ym, gf, onw, wout, x)


def _final_fwd_bwd(x, fnw, target, name):
    s = x.shape[0]

    def body(x_ref, w_ref, t_ref, dx_ref, loss_ref, dw_ref):
        @pl.when(pl.program_id(0) == 0)
        def _():
            loss_ref[...] = jnp.zeros_like(loss_ref)
            dw_ref[...] = jnp.zeros_like(dw_ref)

        xv = x_ref[...]
        w = w_ref[...]
        r = lax.rsqrt(jnp.mean(xv * xv, axis=-1, keepdims=True) + EPS)
        xh = xv * r
        err = xh * w - t_ref[...]
        part = jnp.sum(jnp.sum(err * err, axis=1, keepdims=True), axis=0, keepdims=True)
        loss_ref[...] += part * (0.5 / D_MODEL)
        dy = err * (1.0 / D_MODEL)
        dw_ref[...] += jnp.sum(dy * xh, axis=0, keepdims=True)
        dxh = dy * w
        dx_ref[...] = r * (dxh - xh * jnp.mean(dxh * xh, axis=-1, keepdims=True))

    return pl.pallas_call(
        body, name=name, grid=(s // TM,),
        in_specs=[pl.BlockSpec((TM, D_MODEL), lambda i: (i, 0)),
                  pl.BlockSpec((1, D_MODEL), lambda i: (0, 0)),
                  pl.BlockSpec((TM, D_MODEL), lambda i: (i, 0))],
        out_specs=[pl.BlockSpec((TM, D_MODEL), lambda i: (i, 0)),
                   pl.BlockSpec((1, LANES), lambda i: (0, 0)),
                   pl.BlockSpec((1, D_MODEL), lambda i: (0, 0))],
        out_shape=[jax.ShapeDtypeStruct((s, D_MODEL), F32), jax.ShapeDtypeStruct((1, LANES), F32),
                   jax.ShapeDtypeStruct((1, D_MODEL), F32)],
        compiler_params=_ARB1,
    )(x, fnw, target)


def _outproj_bwd(dxo, wout, ysb, yfx, ym, gf, onw, name):
    s = dxo.shape[0]

    def body(dx_ref, w_ref, ysb_ref, yfx_ref, ym_ref, g_ref, onw_ref,
             dysb_ref, dyfx_ref, dym_ref, dg_ref, dw_ref, donw_ref, yg_ref):
        @pl.when(pl.program_id(0) == 0)
        def _():
            dw_ref[...] = jnp.zeros_like(dw_ref)
            donw_ref[...] = jnp.zeros_like(donw_ref)

        dxb = dx_ref[...].astype(BF16)
        dyg = _dot_nt(dxb, w_ref[...])
        bd = _head_block_diag()
        for c in range(MIX_W // LANES):
            sl = slice(c * LANES, (c + 1) * LANES)
            u = _mix_chunk(c, ysb_ref, yfx_ref, ym_ref)
            r = lax.rsqrt(_head_mean(u * u, bd) + EPS)
            yn = u * r
            g = g_ref[:, sl]
            sg = _sigmoid(g)
            sil = g * sg
            onw = onw_ref[:, sl]
            e = dyg[:, sl]
            yg_ref[:, sl] = (yn * onw * sil).astype(BF16)
            donw_ref[:, sl] += jnp.sum(e * yn * sil, axis=0, keepdims=True)
            dg_ref[:, sl] = (e * yn * onw * (sg * (1.0 + g * (1.0 - sg)))).astype(BF16)
            dyn = e * onw * sil
            du = (r * (dyn - yn * _head_mean(dyn * yn, bd))).astype(BF16)
            if c < 4:
                dysb_ref[:, c * LANES:(c + 1) * LANES] = du
            elif c < 8:
                dyfx_ref[:, (c - 4) * LANES:(c - 3) * LANES] = du
            else:
                dym_ref[:, (c - 8) * LANES:(c - 7) * LANES] = du
        dw_ref[...] += _dot_tn(yg_ref[...], dxb)

    return pl.pallas_call(
        body, name=name, grid=(s // TM,),
        in_specs=[pl.BlockSpec((TM, D_MODEL), lambda i: (i, 0)),
                  pl.BlockSpec((MIX_W, D_MODEL), lambda i: (0, 0)),
                  pl.BlockSpec((TM, SB_W), lambda i: (i, 0)),
                  pl.BlockSpec((TM, FX_W), lambda i: (i, 0)),
                  pl.BlockSpec((TM, MEM_W), lambda i: (i, 0)),
                  pl.BlockSpec((TM, MIX_W), lambda i: (i, 0)),
                  pl.BlockSpec((1, MIX_W), lambda i: (0, 0))],
        out_specs=[pl.BlockSpec((TM, SB_W), lambda i: (i, 0)),
                   pl.BlockSpec((TM, FX_W), lambda i: (i, 0)),
                   pl.BlockSpec((TM, MEM_W), lambda i: (i, 0)),
                   pl.BlockSpec((TM, MIX_W), lambda i: (i, 0)),
                   pl.BlockSpec((MIX_W, D_MODEL), lambda i: (0, 0)),
                   pl.BlockSpec((1, MIX_W), lambda i: (0, 0))],
        out_shape=[jax.ShapeDtypeStruct((s, SB_W), BF16), jax.ShapeDtypeStruct((s, FX_W), BF16),
                   jax.ShapeDtypeStruct((s, MEM_W), BF16), jax.ShapeDtypeStruct((s, MIX_W), BF16),
                   jax.ShapeDtypeStruct((MIX_W, D_MODEL), F32), jax.ShapeDtypeStruct((1, MIX_W), F32)],
        scratch_shapes=[pltpu.VMEM((TM, MIX_W), BF16)],
        compiler_params=_ARB1,
    )(dxo, wout, ysb, yfx, ym, gf, onw)


def _row_dots(do_ref, o_ref, m0):
    prod = do_ref[...].astype(F32) * o_ref[...]
    zero = jnp.zeros_like(prod)
    return (jnp.sum(jnp.where(m0, prod, zero), axis=1, keepdims=True),
            jnp.sum(jnp.where(m0, zero, prod), axis=1, keepdims=True))


def _sb_bwd(qkv, o, do, name, rider=None):
    s = qkv.shape[0]
    nq = s // T

    def body(q_ref, k_ref, v_ref, o_ref, do_ref, dq_ref, dk_ref, dv_ref,
             dqa_ref, dka_ref, dva_ref, rl_ref, rg_ref, dzs_ref, abs_ref):
        i = pl.program_id(1)

        @pl.when(i == 0)
        def _():
            dka_ref[...] = jnp.zeros_like(dka_ref)
            dva_ref[...] = jnp.zeros_like(dva_ref)

        m0 = _pair_masks()
        qh = _split_pair(q_ref[...] * jnp.asarray(Q_SCALE, BF16), m0)
        doh = _split_pair(do_ref[...], m0)
        dsum = _row_dots(do_ref, o_ref, m0)
        strict = _iota2((T, T), 0) > _iota2((T, T), 1)
        u2 = _stack2(jnp.where(strict, 1.0, 0.0).astype(BF16))
        dqa_ref[...] = jnp.zeros_like(dqa_ref)
        rl_ref[...] = jnp.zeros_like(rl_ref)
        rg_ref[...] = jnp.zeros_like(rg_ref)

        hs = range(2)

        def flush(j):
            off = pl.multiple_of(j * T, T)
            k = k_ref[pl.ds(off, T), :]
            for h in hs:
                dqa_ref[h] += _dot(dzs_ref[h], k)
            dka_ref[pl.ds(off, T), :] += _dot_tn(dzs_ref[0], qh[0]) + _dot_tn(dzs_ref[1], qh[1])
            dva_ref[pl.ds(off, T), :] += _dot_tn(abs_ref[0], doh[0]) + _dot_tn(abs_ref[1], doh[1])

        def tile(j, diag):
            off = pl.multiple_of(j * T, T)
            k = k_ref[pl.ds(off, T), :]
            v = v_ref[pl.ds(off, T), :]
            z = [_dot_nt(qh[h], k) for h in hs]
            da = [_dot_nt(doh[h], v) for h in hs]
            if not diag:
                flush(j + 1)
            la = [jnp.minimum(z[h], 0.0) - jnp.log(1.0 + jnp.exp(-jnp.abs(z[h]))) for h in hs]
            lf = [la[h] - z[h] for h in hs]
            if diag:
                lf = [jnp.where(strict, lf[h], 0.0) for h in hs]
            cin = [_cum2(lf[h], u2) for h in hs]
            a = [jnp.exp(la[h] + cin[h] + rl_ref[h]) for h in hs]
            if diag:
                a = [jnp.where(strict, a[h], 0.0) for h in hs]
            ab = [a[h].astype(BF16) for h in hs]
            g = [ab[h].astype(F32) * da[h] for h in hs]
            gin = [_cum2(g[h], u2) for h in hs]
            dz = [g[h] - jnp.exp(la[h]) * ((dsum[h] - rg_ref[h]) - gin[h]) for h in hs]
            if diag:
                dz = [jnp.where(strict, dz[h], 0.0) for h in hs]
            for h in hs:
                rl_ref[h] += cin[h][:, 0:1] + lf[h][:, 0:1]
                rg_ref[h] += gin[h][:, 0:1] + g[h][:, 0:1]
                dzs_ref[h] = dz[h].astype(BF16)
                abs_ref[h] = ab[h]

        tile(i, True)

        def more(state):
            j, top = state
            return jnp.logical_and(j >= 0, top > UNDERFLOW)

        def step(state):
            j, _ = state
            tile(j, False)
            return j - 1, _running_top(rl_ref)

        j_left, _ = lax.while_loop(more, step, (i - 1, _running_top(rl_ref)))
        flush(j_left + 1)
        dq_ref[...] = (jnp.where(m0, dqa_ref[0], dqa_ref[1]) * Q_SCALE).astype(BF16)

        @pl.when(i == nq - 1)
        def _():
            dk_ref[...] = dka_ref[...].astype(BF16)
            dv_ref[...] = dva_ref[...].astype(BF16)

    nb = SB_W // LANES
    (dq, dk, dv), lands = _ride(dict(
        body=body, name=name, grid=(nb, nq),
        in_specs=[pl.BlockSpec((T, LANES), lambda p, i: (i, p)),
                  pl.BlockSpec((s, LANES), lambda p, i: (0, nb + p)),
                  pl.BlockSpec((s, LANES), lambda p, i: (0, 2 * nb + p)),
                  pl.BlockSpec((T, LANES), lambda p, i: (i, p)),
                  pl.BlockSpec((T, LANES), lambda p, i: (i, p))],
        out_specs=[pl.BlockSpec((T, LANES), lambda p, i: (i, p)),
                   pl.BlockSpec((s, LANES), lambda p, i: (0, p)),
                   pl.BlockSpec((s, LANES), lambda p, i: (0, p))],
        out_shape=[jax.ShapeDtypeStruct((s, SB_W), BF16)] * 3,
        scratch_shapes=[pltpu.VMEM((2, T, LANES), F32), pltpu.VMEM((s, LANES), F32),
                        pltpu.VMEM((s, LANES), F32), pltpu.VMEM((2, T, 1), F32),
                        pltpu.VMEM((2, T, 1), F32), pltpu.VMEM((2, T, T), BF16),
                        pltpu.VMEM((2, T, T), BF16)],
        compiler_params=_ARB2, operands=[qkv, qkv, qkv, o, do]), rider)
    return dq, dk, dv, lands


def _fox_bwd(qkv, fqb, frow, fbounds, o, lse, do, name, rider=None):
    s = qkv.shape[0]
    nq = s // T

    def body(q_ref, k_ref, v_ref, fq_ref, fr_ref, fb_ref, o_ref, lse_ref, do_ref,
             dq_ref, dk_ref, dv_ref, df_ref, dqa_ref, dka_ref, dva_ref, dfa_ref, dls_ref, pbs_ref,
             kn_ref):
        i = pl.program_id(1)

        @pl.when(i == 0)
        def _():
            dka_ref[...] = jnp.zeros_like(dka_ref)
            dva_ref[...] = jnp.zeros_like(dva_ref)
            dfa_ref[...] = jnp.zeros_like(dfa_ref)

        m0 = _pair_masks()

        @pl.when(i == 0)
        def _():
            kn_ref[0], kn_ref[1] = _head_norm_tops(k_ref[...], m0)

        q = q_ref[...] * jnp.asarray(Q_SCALE, BF16)
        qn = _head_norm_tops(q, m0)
        qh = _split_pair(q, m0)
        doh = _split_pair(do_ref[...], m0)
        dsum = _row_dots(do_ref, o_ref, m0)
        fq = fq_ref[...]
        fqh = (fq[:, 0:1], fq[:, HEAD_DIM:HEAD_DIM + 1])
        lse = lse_ref[...]
        lseh = (lse[:, 0:1], lse[:, HEAD_DIM:HEAD_DIM + 1])
        causal = _iota2((T, T), 0) >= _iota2((T, T), 1)
        dqa_ref[...] = jnp.zeros_like(dqa_ref)

        hs = range(2)

        def flush(j):
            off = pl.multiple_of(j * T, T)
            k = k_ref[pl.ds(off, T), :]
            for h in hs:
                dqa_ref[h] += _dot(dls_ref[h], k)
            dka_ref[pl.ds(off, T), :] += _dot_tn(dls_ref[0], qh[0]) + _dot_tn(dls_ref[1], qh[1])
            dva_ref[pl.ds(off, T), :] += _dot_tn(pbs_ref[0], doh[0]) + _dot_tn(pbs_ref[1], doh[1])

        def tile(j, diag):
            off = pl.multiple_of(j * T, T)
            k = k_ref[pl.ds(off, T), :]
            v = v_ref[pl.ds(off, T), :]
            sc = [_dot_nt(qh[h], k) + fqh[h] - fr_ref[h:h + 1, pl.ds(off, T)] for h in hs]
            dp = [_dot_nt(doh[h], v) for h in hs]
            if not diag:
                flush(j + 1)
            p = [jnp.exp(sc[h] - lseh[h]) for h in hs]
            if diag:
                p = [jnp.where(causal, p[h], 0.0) for h in hs]
            dl = [p[h] * (dp[h] - dsum[h]) for h in hs]
            for h in hs:
                dls_ref[h] = dl[h].astype(BF16)
                pbs_ref[h] = p[h].astype(BF16)
                dfa_ref[h:h + 1, pl.ds(off, T)] -= jnp.sum(dl[h], axis=0, keepdims=True)

        tile(i, True)
        j_left = _fox_tiles_left(i, pl.program_id(0), qn, kn_ref, fb_ref, tile)
        flush(j_left + 1)
        dq_ref[...] = (jnp.where(m0, dqa_ref[0], dqa_ref[1]) * Q_SCALE).astype(BF16)

        @pl.when(i == nq - 1)
        def _():
            dk_ref[...] = dka_ref[...].astype(BF16)
            dv_ref[...] = dva_ref[...].astype(BF16)
            df_ref[...] = dfa_ref[...]

    nb = FX_W // LANES
    base = 3 * SB_W // LANES
    (dq, dk, dv, df), lands = _ride(dict(
        body=body, name=name, grid=(nb, nq),
        in_specs=[pl.BlockSpec((T, LANES), lambda p, i: (i, base + p)),
                  pl.BlockSpec((s, LANES), lambda p, i: (0, base + nb + p)),
                  pl.BlockSpec((s, LANES), lambda p, i: (0, base + 2 * nb + p)),
                  pl.BlockSpec((T, LANES), lambda p, i: (i, p)),
                  pl.BlockSpec((None, 2, s), lambda p, i: (p, 0, 0)),
                  pl.BlockSpec(memory_space=pltpu.SMEM),
                  pl.BlockSpec((T, LANES), lambda p, i: (i, p)),
                  pl.BlockSpec((T, LANES), lambda p, i: (i, p)),
                  pl.BlockSpec((T, LANES), lambda p, i: (i, p))],
        out_specs=[pl.BlockSpec((T, LANES), lambda p, i: (i, p)),
                   pl.BlockSpec((s, LANES), lambda p, i: (0, p)),
                   pl.BlockSpec((s, LANES), lambda p, i: (0, p)),
                   pl.BlockSpec((None, 2, s), lambda p, i: (p, 0, 0))],
        out_shape=[jax.ShapeDtypeStruct((s, FX_W), BF16)] * 3
        + [jax.ShapeDtypeStruct((nb, 2, s), F32)],
        scratch_shapes=[pltpu.VMEM((2, T, LANES), F32), pltpu.VMEM((s, LANES), F32),
                        pltpu.VMEM((s, LANES), F32), pltpu.VMEM((2, s), F32),
                        pltpu.VMEM((2, T, T), BF16), pltpu.VMEM((2, T, T), BF16),
                        pltpu.SMEM((2,), F32)],
        compiler_params=_ARB2, operands=[qkv, qkv, qkv, fqb, frow, fbounds, o, lse, do]), rider)
    return dq, dk, dv, df, lands


def _fox_prep_bwd(dfrow, gf, bpad, name):
    s = gf.shape[0]

    def body(df_ref, fl_ref, b_ref, dfl_ref, db_ref):
        tri = jnp.where(_iota2((T, T), 0) <= _iota2((T, T), 1), 1.0, 0.0).astype(BF16)
        carry = jnp.zeros((1, LANES), F32)
        db = jnp.zeros((1, LANES), F32)
        fill = jnp.zeros((LANES - FOX_HEADS, T), F32)
        for blk in reversed(range(s // T)):
            rows = slice(blk * T, (blk + 1) * T)
            c = _tri3(tri, jnp.concatenate([df_ref[:, rows], fill], axis=0), _dot_nt) + carry
            carry = c[0:1, :]
            dfl = c / (1.0 + jnp.exp(fl_ref[rows, :] + b_ref[...]))
            dfl_ref[rows, :] = dfl.astype(BF16)
            db = db + jnp.sum(dfl, axis=0, keepdims=True)
        db_ref[...] = db

    return pl.pallas_call(
        body, name=name, grid=(1,),
        in_specs=[pl.BlockSpec((FOX_HEADS, s), lambda i: (0, 0)),
                  pl.BlockSpec((s, LANES), lambda i: (0, MIX_W // LANES)),
                  pl.BlockSpec((1, LANES), lambda i: (0, 0))],
        out_specs=[pl.BlockSpec((s, LANES), lambda i: (0, 0)),
                   pl.BlockSpec((1, LANES), lambda i: (0, 0))],
        out_shape=[jax.ShapeDtypeStruct((s, LANES), BF16), jax.ShapeDtypeStruct((1, LANES), F32)],
        compiler_params=_ARB1,
    )(dfrow, gf, bpad)


def _mem_bwd(qkv, kv, o, lse, do, name):
    s = qkv.shape[0]
    n = kv.shape[0]

    def body(q_ref, k_ref, v_ref, o_ref, lse_ref, do_ref, dq_ref, dk_ref, dv_ref):
        @pl.when(pl.program_id(1) == 0)
        def _():
            dk_ref[...] = jnp.zeros_like(dk_ref)
            dv_ref[...] = jnp.zeros_like(dv_ref)

        m0 = _pair_masks()
        qh = _split_pair(q_ref[...] * jnp.asarray(Q_SCALE, BF16), m0)
        doh = _split_pair(do_ref[...], m0)
        dsum = _row_dots(do_ref, o_ref, m0)
        lse = lse_ref[...]
        lseh = (lse[:, 0:1], lse[:, HEAD_DIM:HEAD_DIM + 1])
        k = k_ref[...]
        v = v_ref[...]
        dqs = []
        for h in range(2):
            p = jnp.exp(_dot_nt(qh[h], k) - lseh[h])
            dl = p * (_dot_nt(doh[h], v) - dsum[h])
            dlb = dl.astype(BF16)
            dqs.append(_dot(dlb, k))
            dk_ref[...] += _dot_tn(dlb, qh[h])
            dv_ref[...] += _dot_tn(p.astype(BF16), doh[h])
        dq_ref[...] = (jnp.where(m0, dqs[0], dqs[1]) * Q_SCALE).astype(BF16)

    nb = MEM_W // LANES
    base = (3 * SB_W + 3 * FX_W) // LANES
    return pl.pallas_call(
        body, name=name, grid=(nb, s // TQM),
        in_specs=[pl.BlockSpec((TQM, LANES), lambda p, i: (i, base + p)),
                  pl.BlockSpec((n, LANES), lambda p, i: (0, p)),
                  pl.BlockSpec((n, LANES), lambda p, i: (0, nb + p)),
                  pl.BlockSpec((TQM, LANES), lambda p, i: (i, p)),
                  pl.BlockSpec((TQM, LANES), lambda p, i: (i, p)),
                  pl.BlockSpec((TQM, LANES), lambda p, i: (i, p))],
        out_specs=[pl.BlockSpec((TQM, LANES), lambda p, i: (i, p)),
                   pl.BlockSpec((n, LANES), lambda p, i: (0, p)),
                   pl.BlockSpec((n, LANES), lambda p, i: (0, p))],
        out_shape=[jax.ShapeDtypeStruct((s, MEM_W), BF16), jax.ShapeDtypeStruct((n, MEM_W), F32),
                   jax.ShapeDtypeStruct((n, MEM_W), F32)],
        compiler_params=_ARB2,
    )(qkv, kv, kv, o, lse, do)


def _memkv_bwd(mem, mnw, wkv, dk, dv, name):
    n = mem.shape[0]

    def body(mem_ref, mnw_ref, w_ref, dk_ref, dv_ref, dw_ref, dmnw_ref):
        mv = mem_ref[...]
        r = lax.rsqrt(jnp.mean(mv * mv, axis=-1, keepdims=True) + EPS)
        mh = mv * r
        hm = (mh * mnw_ref[...]).astype(BF16)
        dkv = jnp.concatenate([dk_ref[...], dv_ref[...]], axis=1).astype(BF16)
        dw_ref[...] = _dot_tn(hm, dkv)
        dhm = _dot_nt(dkv, w_ref[...])
        dmnw_ref[...] = jnp.sum(dhm * mh, axis=0, keepdims=True)

    return pl.pallas_call(
        body, name=name, grid=(1,),
        in_specs=[pl.BlockSpec((n, D_MODEL), lambda i: (0, 0)),
                  pl.BlockSpec((1, D_MODEL), lambda i: (0, 0)),
                  pl.BlockSpec((D_MODEL, 2 * MEM_W), lambda i: (0, 0)),
                  pl.BlockSpec((n, MEM_W), lambda i: (0, 0)),
                  pl.BlockSpec((n, MEM_W), lambda i: (0, 0))],
        out_specs=[pl.BlockSpec((D_MODEL, 2 * MEM_W), lambda i: (0, 0)),
                   pl.BlockSpec((1, D_MODEL), lambda i: (0, 0))],
        out_shape=[jax.ShapeDtypeStruct((D_MODEL, 2 * MEM_W), F32),
                   jax.ShapeDtypeStruct((1, D_MODEL), F32)],
        compiler_params=_ARB1,
    )(mem, mnw, wkv, dk, dv)


def _inproj_bwd_dx(pieces, w_r, x, nw, dxo, name):
    s = x.shape[0]
    n = len(pieces)
    widths = [p.shape[1] for p in pieces]

    def body(*refs):
        piece_refs = refs[:n]
        w_ref, x_ref, nw_ref, dxo_ref, dx_ref, h_ref, dnw_ref, dp_ref = refs[n:]

        @pl.when(pl.program_id(0) == 0)
        def _():
            dnw_ref[...] = jnp.zeros_like(dnw_ref)

        col = 0
        for r, wd in zip(piece_refs, widths):
            dp_ref[:, col:col + wd] = r[...]
            col += wd
        dp_ref[:, col:] = jnp.zeros((TM, WR_W - col), BF16)
        dh = _dot(dp_ref[...], w_ref[...])
        xv = x_ref[...]
        nw = nw_ref[...]
        r = lax.rsqrt(jnp.mean(xv * xv, axis=-1, keepdims=True) + EPS)
        xh = xv * r
        h_ref[...] = (xh * nw).astype(BF16)
        dnw_ref[...] += jnp.sum(dh * xh, axis=0, keepdims=True)
        dxh = dh * nw
        dx_ref[...] = r * (dxh - xh * jnp.mean(dxh * xh, axis=-1, keepdims=True)) + dxo_ref[...]

    return pl.pallas_call(
        body, name=name, grid=(s // TM,),
        in_specs=[pl.BlockSpec((TM, wd), lambda i: (i, 0)) for wd in widths]
        + [pl.BlockSpec((WR_W, D_MODEL), lambda i: (0, 0)),
           pl.BlockSpec((TM, D_MODEL), lambda i: (i, 0)),
           pl.BlockSpec((1, D_MODEL), lambda i: (0, 0)),
           pl.BlockSpec((TM, D_MODEL), lambda i: (i, 0))],
        out_specs=[pl.BlockSpec((TM, D_MODEL), lambda i: (i, 0)),
                   pl.BlockSpec((TM, D_MODEL), lambda i: (i, 0)),
                   pl.BlockSpec((1, D_MODEL), lambda i: (0, 0)),
                   pl.BlockSpec((TM, WR_W), lambda i: (i, 0))],
        out_shape=[jax.ShapeDtypeStruct((s, D_MODEL), F32), jax.ShapeDtypeStruct((s, D_MODEL), BF16),
                   jax.ShapeDtypeStruct((1, D_MODEL), F32), jax.ShapeDtypeStruct((s, WR_W), BF16)],
        compiler_params=_ARB1,
    )(*pieces, w_r, x, nw, dxo)


def _inproj_bwd_dw(h, dproj, name):
    s = dproj.shape[0]
    tn = 256

    def body(h_ref, dp_ref, dw_ref):
        dw_ref[...] = _dot_tn(dp_ref[...], h_ref[...])

    return pl.pallas_call(
        body, name=name, grid=(WR_W // tn,),
        in_specs=[pl.BlockSpec((s, D_MODEL), lambda j: (0, 0)),
                  pl.BlockSpec((s, tn), lambda j: (0, j))],
        out_specs=pl.BlockSpec((tn, D_MODEL), lambda j: (j, 0)),
        out_shape=jax.ShapeDtypeStruct((WR_W, D_MODEL), F32),
        compiler_params=_ARB1,
    )(h, dproj)


def _rearrange_w_in(wt):
    pad = jnp.zeros((FL_PAD - FOX_HEADS,) + wt.shape[1:], wt.dtype)
    return jnp.concatenate([wt[:3072], wt[3080:3336], wt[3336:IN_W], wt[3072:3080], pad], axis=0)


def _restore_w_in(g):
    gate0 = QKV_W
    fl0 = QKV_W + MIX_W
    return jnp.concatenate(
        [g[:3072], g[fl0:fl0 + FOX_HEADS], g[3072:QKV_W], g[gate0:fl0]], axis=0)


def _pad_lanes(v, width=LANES):
    return jnp.pad(v, (0, width - v.shape[0])).reshape(1, width)


def _layer_fwd(xs, mem, nw, w_r, b_forget, mnw, late, onw, l, travel=None):
    s = xs.shape[0]
    bpad = _pad_lanes(b_forget)
    qkv, gf = _inproj_fwd(xs, nw, w_r, f"inproj_fwd_{l}")
    fqb, frow, fbounds = _fox_prep_fwd(gf, bpad, f"fox_prep_fwd_{l}")
    frow = frow.reshape(FOX_HEADS // 2, 2, s)
    travel = _Travel(travel)
    ysb, _ = travel.ride(0, _sb_fwd, qkv, f"sb_fwd_{l}")
    yfx, lse_fx, _ = travel.ride(1, _fox_fwd, qkv, fqb, frow, fbounds, f"fox_fwd_{l}")
    wkv, wout = late(travel.lands)
    kv = _memkv_fwd(mem, mnw, wkv, f"memkv_fwd_{l}")
    ym, lse_m = _mem_fwd(qkv, kv, f"mem_fwd_{l}")
    xn = _outproj_fwd(ysb, yfx, ym, gf, onw, wout, xs, f"outproj_fwd_{l}")
    saved = (xs, nw, mnw, onw, bpad, qkv, gf, fqb, frow, fbounds, ysb, yfx, lse_fx, kv, ym, lse_m)
    return xn, saved, travel.lands, (wkv, wout)


class _Travel:
    def __init__(self, plan):
        self.plan = plan
        self.lands = None if plan is None else _new_lands(plan[0], plan[1])

    def ride(self, n, fn, *args):
        if self.plan is None or self.plan[2][n] is None:
            return fn(*args)
        srcs, scatter, legs = self.plan
        idx, rows = legs[n]
        out = fn(*args, rider=_Rider([srcs[a] for a in idx], [self.lands[a] for a in idx],
                                     scatter, rows))
        for a, land in zip(idx, out[-1]):
            self.lands[a] = land
        return out


def _layer_bwd(dx, saved, mem, w_r, wkv, wout, l, travel=None):
    xs, nw, mnw, onw, bpad, qkv, gf, fqb, frow, fbounds, ysb, yfx, lse_fx, kv, ym, lse_m = saved
    s = xs.shape[0]
    dysb, dyfx, dym, dgate, dwout, donw = _outproj_bwd(
        dx, wout, ysb, yfx, ym, gf, onw, f"outproj_bwd_{l}")
    travel = _Travel(None if travel is None else travel(dwout))
    sdq, sdk, sdv, _ = travel.ride(0, _sb_bwd, qkv, ysb, dysb, f"sb_bwd_{l}")
    fdq, fdk, fdv, dfrow, _ = travel.ride(1, _fox_bwd, qkv, fqb, frow, fbounds, yfx, lse_fx, dyfx,
                                          f"fox_bwd_{l}")
    dfl, db = _fox_prep_bwd(dfrow.reshape(FOX_HEADS, s), gf, bpad, f"fox_prep_bwd_{l}")
    dmq, dmk, dmv = _mem_bwd(qkv, kv, ym, lse_m, dym, f"mem_bwd_{l}")
    dwkv, dmnw = _memkv_bwd(mem, mnw, wkv, dmk, dmv, f"memkv_bwd_{l}")
    dx, ht, dnw, dproj = _inproj_bwd_dx([sdq, sdk, sdv, fdq, fdk, fdv, dmq, dgate, dfl],
                                        w_r, xs, nw, dx, f"inproj_bwd_dx_{l}")
    dwr = _inproj_bwd_dw(ht, dproj, f"inproj_bwd_dw_{l}")
    grads = dict(norm_w=dnw[0], w_r=dwr, b_forget=db[0, :FOX_HEADS], mem_norm_w=dmnw[0],
                 w_mem_kv=dwkv, out_norm_w=donw[0], w_out=dwout)
    return dx, grads, travel.lands


_ANY = pl.BlockSpec(memory_space=pl.ANY)


def _my_place():
    return lax.axis_index("x"), lax.axis_index("y"), lax.axis_index("c")


def _flip(v, bit):
    return 1 - v if bit else v


def _block_index(px, py, pc):
    return 4 * px + 2 * py + pc


def _all_gather_weights(shards, name):
    n = len(shards)

    def body(*refs):
        ins, outs = refs[:n], refs[n:2 * n]
        send_sems, recv_sems, local_sems = refs[2 * n:]
        x, y, c = _my_place()
        me = (x, y, c)
        sibling = (x, y, 1 - c)
        chips = [(1 - x, y), (x, 1 - y), (1 - x, 1 - y)]

        def copy(a, k, block, to, src=None):
            dst = outs[a].at[_block_index(*block)]
            return pltpu.make_async_remote_copy(
                src_ref=dst if src is None else src, dst_ref=dst,
                send_sem=send_sems.at[a, k], recv_sem=recv_sems.at[a, k],
                device_id=to, device_id_type=pl.DeviceIdType.MESH)

        mine = [pltpu.make_async_copy(ins[a], outs[a].at[_block_index(*me)], local_sems.at[a])
                for a in range(n)]
        for cp in mine:
            cp.start()
        first = []
        for a in range(n):
            first.append(copy(a, 0, me, sibling, src=ins[a]))
            first += [copy(a, 1 + j, me, (*chip, c), src=ins[a]) for j, chip in enumerate(chips)]
        for cp in first:
            cp.start()
        passed = []
        for j, chip in enumerate(chips):
            for a in range(n):
                copy(a, 1 + j, (*chip, c), me).wait_recv()
                fwd = copy(a, 4 + j, (*chip, c), sibling)
                fwd.start()
                passed.append(fwd)
        for a in range(n):
            copy(a, 0, sibling, me).wait_recv()
            for j, chip in enumerate(chips):
                copy(a, 4 + j, (*chip, 1 - c), me).wait_recv()
        for cp in first + passed:
            cp.wait_send()
        for cp in mine:
            cp.wait()

    return pl.pallas_call(
        body, name=name,
        in_specs=[_ANY] * n, out_specs=[_ANY] * n,
        out_shape=[jax.ShapeDtypeStruct((N_DEV,) + v.shape, v.dtype) for v in shards],
        scratch_shapes=[pltpu.SemaphoreType.DMA((n, 7)), pltpu.SemaphoreType.DMA((n, 7)),
                        pltpu.SemaphoreType.DMA((n,))],
    )(*shards)


def _exchange_blocks(blocked, name):
    n = len(blocked)

    def body(*refs):
        ins, outs = refs[:n], refs[n:2 * n]
        send_sems, recv_sems, local_sems = refs[2 * n:]
        x, y, c = _my_place()
        mine_idx = _block_index(x, y, c)
        local = [pltpu.make_async_copy(ins[a].at[mine_idx], outs[a].at[mine_idx], local_sems.at[a])
                 for a in range(n)]
        for cp in local:
            cp.start()
        sends, arrivals = [], []
        for r in range(1, N_DEV):
            peer = (_flip(x, r & 4), _flip(y, r & 2), _flip(c, r & 1))
            peer_idx = _block_index(*peer)
            for a in range(n):
                sems = dict(send_sem=send_sems.at[a, r - 1], recv_sem=recv_sems.at[a, r - 1],
                            device_id=peer, device_id_type=pl.DeviceIdType.MESH)
                sends.append(pltpu.make_async_remote_copy(
                    src_ref=ins[a].at[peer_idx], dst_ref=outs[a].at[mine_idx], **sems))
                arrivals.append(pltpu.make_async_remote_copy(
                    src_ref=ins[a].at[peer_idx], dst_ref=outs[a].at[peer_idx], **sems))
        for cp in sends:
            cp.start()
        for cp in arrivals:
            cp.wait_recv()
        for cp in sends:
            cp.wait_send()
        for cp in local:
            cp.wait()

    return pl.pallas_call(
        body, name=name,
        in_specs=[_ANY] * n, out_specs=[_ANY] * n,
        out_shape=[jax.ShapeDtypeStruct(v.shape, v.dtype) for v in blocked],
        scratch_shapes=[pltpu.SemaphoreType.DMA((n, 7)), pltpu.SemaphoreType.DMA((n, 7)),
                        pltpu.SemaphoreType.DMA((n,))],
    )(*blocked)


N_CHIP = N_DEV // 2


def _pair_swap(blocked, name):
    n = len(blocked)

    def body(*refs):
        ins, outs = refs[:n], refs[n:2 * n]
        send_sems, recv_sems = refs[2 * n:]
        x, y, c = _my_place()
        copies = [pltpu.make_async_remote_copy(
            src_ref=ins[a].at[j, 1 - c], dst_ref=outs[a].at[j],
            send_sem=send_sems.at[N_CHIP * a + j], recv_sem=recv_sems.at[N_CHIP * a + j],
            device_id=(x, y, 1 - c), device_id_type=pl.DeviceIdType.MESH)
            for a in range(n) for j in range(N_CHIP)]
        for cp in copies:
            cp.start()
        for cp in copies:
            cp.wait_recv()
        for cp in copies:
            cp.wait_send()

    return pl.pallas_call(
        body, name=name,
        in_specs=[_ANY] * n, out_specs=[_ANY] * n,
        out_shape=[jax.ShapeDtypeStruct((N_CHIP,) + v.shape[2:], v.dtype) for v in blocked],
        scratch_shapes=[pltpu.SemaphoreType.DMA((N_CHIP * n,)),
                        pltpu.SemaphoreType.DMA((N_CHIP * n,))],
    )(*blocked)


def _pair_add(mine, theirs, name):
    _, nrow, ncol = mine.shape

    def body(a_ref, b_ref, o_ref):
        o_ref[...] = (a_ref[...].astype(F32) + b_ref[...].astype(F32)).astype(BF16)

    blk = pl.BlockSpec((None, nrow, ncol), lambda j: (j, 0, 0))
    return pl.pallas_call(
        body, name=name, grid=(N_CHIP,), in_specs=[blk, blk], out_specs=blk,
        out_shape=jax.ShapeDtypeStruct(mine.shape, BF16), compiler_params=_ARB1,
    )(mine, theirs)


def _chip_exchange(by_chip, to_all, name):
    n, m = len(by_chip), len(to_all)

    def body(*refs):
        ins, alls = refs[:n], refs[n:n + m]
        outs, all_outs = refs[n + m:2 * n + m], refs[2 * n + m:2 * (n + m)]
        send_sems, recv_sems, local_sems = refs[2 * (n + m):]
        x, y, c = _my_place()
        my_chip, mine_idx = 2 * x + y, _block_index(x, y, c)
        local = [pltpu.make_async_copy(ins[a].at[my_chip], outs[a].at[my_chip], local_sems.at[a])
                 for a in range(n)]
        local += [pltpu.make_async_copy(alls[b].at[mine_idx], all_outs[b].at[mine_idx],
                                        local_sems.at[n + b]) for b in range(m)]
        for cp in local:
            cp.start()
        sends, arrivals = [], []
        k = 0
        for r in range(1, N_DEV):
            peer = (_flip(x, r & 4), _flip(y, r & 2), _flip(c, r & 1))
            peer_chip, peer_idx = 2 * peer[0] + peer[1], _block_index(*peer)
            pairs = [(alls[b].at[mine_idx], all_outs[b].at[mine_idx], all_outs[b].at[peer_idx])
                     for b in range(m)]
            if not r & 1:
                pairs += [(ins[a].at[peer_chip], outs[a].at[my_chip], outs[a].at[peer_chip])
                          for a in range(n)]
            for src, there, here in pairs:
                sems = dict(send_sem=send_sems.at[k], recv_sem=recv_sems.at[k], device_id=peer,
                            device_id_type=pl.DeviceIdType.MESH)
                sends.append(pltpu.make_async_remote_copy(src_ref=src, dst_ref=there, **sems))
                arrivals.append(pltpu.make_async_remote_copy(src_ref=src, dst_ref=here, **sems))
                k += 1
        for cp in sends:
            cp.start()
        for cp in arrivals:
            cp.wait_recv()
        for cp in sends:
            cp.wait_send()
        for cp in local:
            cp.wait()

    n_copies = 7 * m + 3 * n
    return pl.pallas_call(
        body, name=name,
        in_specs=[_ANY] * (n + m), out_specs=[_ANY] * (n + m),
        out_shape=[jax.ShapeDtypeStruct(v.shape, v.dtype) for v in by_chip + to_all],
        scratch_shapes=[pltpu.SemaphoreType.DMA((n_copies,)), pltpu.SemaphoreType.DMA((n_copies,)),
                        pltpu.SemaphoreType.DMA((n + m,))],
    )(*by_chip, *to_all)


class _Rider(NamedTuple):
    srcs: list
    lands: list
    scatter: bool
    part: list


def _new_lands(srcs, scatter):
    return [lax.empty(v.shape if scatter else (N_DEV,) + v.shape, v.dtype) for v in srcs]


def _rider_copies(srcs, lands, send_sems, recv_sems, rider):
    x, y, c = _my_place()
    mine_idx = _block_index(x, y, c)

    def window(ref, a):
        if rider.part[a] is None:
            return ref
        dim, start, size = rider.part[a]
        return ref.at[(slice(None),) * dim + (pl.ds(start, size),)]

    sends, arrivals = [], []
    for r in range(1, N_DEV):
        peer = (_flip(x, r & 4), _flip(y, r & 2), _flip(c, r & 1))
        peer_idx = _block_index(*peer)
        for a in range(len(srcs)):
            src = window(srcs[a].at[peer_idx] if rider.scatter else srcs[a], a)
            k = 7 * a + r - 1
            sems = dict(send_sem=send_sems.at[k], recv_sem=recv_sems.at[k],
                        device_id=peer, device_id_type=pl.DeviceIdType.MESH)
            sends.append(pltpu.make_async_remote_copy(
                src_ref=src, dst_ref=window(lands[a].at[mine_idx], a), **sems))
            arrivals.append(pltpu.make_async_remote_copy(
                src_ref=src, dst_ref=window(lands[a].at[peer_idx], a), **sems))
    return sends, arrivals


def _ride(call, rider):
    call = dict(call)
    body, grid = call.pop("body"), call["grid"]
    operands = call.pop("operands")
    if rider is None:
        return list(pl.pallas_call(body, **call)(*operands)), None
    n_in, n_out = len(call["in_specs"]), len(call["out_specs"])
    n_scratch = len(call["scratch_shapes"])
    m = len(rider.srcs)

    def riding(*refs):
        main_in, srcs, lands = refs[:n_in], refs[n_in:n_in + m], refs[n_in + m:n_in + 2 * m]
        main_out = refs[n_in + 2 * m:n_in + 2 * m + n_out]
        rest = refs[n_in + 3 * m + n_out:]
        send_sems, recv_sems = rest[n_scratch:]
        at = [pl.program_id(d) for d in range(len(grid))]
        first = functools.reduce(jnp.logical_and, [p == 0 for p in at])
        last = functools.reduce(jnp.logical_and, [p == g - 1 for p, g in zip(at, grid)])
        sends, arrivals = _rider_copies(srcs, lands, send_sems, recv_sems, rider)

        @pl.when(first)
        def _():
            for cp in sends:
                cp.start()

        body(*main_in, *main_out, *rest[:n_scratch])

        @pl.when(last)
        def _():
            for cp in arrivals:
                cp.wait_recv()
            for cp in sends:
                cp.wait_send()

    call["in_specs"] = list(call["in_specs"]) + [_ANY] * (2 * m)
    call["out_specs"] = list(call["out_specs"]) + [_ANY] * m
    call["out_shape"] = list(call["out_shape"]) + [
        jax.ShapeDtypeStruct(v.shape, v.dtype) for v in rider.lands]
    call["scratch_shapes"] = list(call["scratch_shapes"]) + [
        pltpu.SemaphoreType.DMA((7 * m,)), pltpu.SemaphoreType.DMA((7 * m,))]
    call["input_output_aliases"] = {n_in + m + a: n_out + a for a in range(m)}
    outs = pl.pallas_call(riding, **call)(*operands, *rider.srcs, *rider.lands)
    return list(outs[:n_out]), list(outs[n_out:])


def _sum_parts(p_ref):
    g = p_ref[0].astype(F32)
    for k in range(1, p_ref.shape[0]):
        g = g + p_ref[k].astype(F32)
    return g


def _adamw(g, w, m, v):
    c1 = 1.0 / (1.0 - ADAM_B1 ** ADAM_STEP)
    c2 = 1.0 / (1.0 - ADAM_B2 ** ADAM_STEP)
    nm = ADAM_B1 * m + (1.0 - ADAM_B1) * g
    nv = ADAM_B2 * v + (1.0 - ADAM_B2) * (g * g)
    return nm, nv, -ADAM_LR * ((nm * c1) / (jnp.sqrt(nv * c2) + ADAM_EPS) + ADAM_WD * w)


def _adamw_w_in(parts, w, m, v, name):
    ncol_blk, depth, nfeat = w.shape
    cols = 256

    def body(*refs):
        p_refs = refs[:depth]
        w_ref, m_ref, v_ref, g_ref, d_ref, nm_ref, nv_ref = refs[depth:]
        for l in range(depth):
            g = _sum_parts(p_refs[l])
            nm, nv, d = _adamw(g, w_ref[:, l, :], m_ref[:, l, :], v_ref[:, l, :])
            g_ref[:, l, :] = g
            nm_ref[:, l, :] = nm
            nv_ref[:, l, :] = nv
            d_ref[:, l, :] = d

    blk = pl.BlockSpec((ncol_blk, depth, cols), lambda j: (0, 0, j))
    return pl.pallas_call(
        body, name=name, grid=(nfeat // cols,),
        in_specs=[pl.BlockSpec((p.shape[0], ncol_blk, cols), lambda j: (0, 0, j)) for p in parts]
        + [blk] * 3,
        out_specs=[blk] * 4,
        out_shape=[jax.ShapeDtypeStruct(w.shape, F32)] * 4,
        compiler_params=_ARB1,
    )(*parts, w, m, v)


def _adamw_sum(parts, w, m, v, tile, name):
    depth, nrow, ncol = w.shape
    rows, cols = tile

    def body(*refs):
        p_refs = refs[:depth]
        w_ref, m_ref, v_ref, g_ref, d_ref, nm_ref, nv_ref = refs[depth:]
        layer = pl.program_id(0)
        for l in range(depth):
            @pl.when(layer == l)
            def _(p_ref=p_refs[l]):
                g = _sum_parts(p_ref)
                nm, nv, d = _adamw(g, w_ref[...], m_ref[...], v_ref[...])
                g_ref[...] = g
                nm_ref[...] = nm
                nv_ref[...] = nv
                d_ref[...] = d

    def part_spec(l):
        return pl.BlockSpec((parts[l].shape[0], rows, cols), lambda q, i, j: (
            0, jnp.where(q == l, i, 0), jnp.where(q == l, j, 0)))

    blk = pl.BlockSpec((None, rows, cols), lambda q, i, j: (q, i, j))
    return pl.pallas_call(
        body, name=name, grid=(depth, nrow // rows, ncol // cols),
        in_specs=[part_spec(l) for l in range(depth)] + [blk, blk, blk],
        out_specs=[blk] * 4,
        out_shape=[jax.ShapeDtypeStruct(w.shape, F32)] * 4,
        compiler_params=pltpu.CompilerParams(
            dimension_semantics=("arbitrary", "arbitrary", "arbitrary")),
    )(*parts, w, m, v)


def _pack_small(norm_w, mem_norm_w, out_norm_w, final_norm_w, b_forget):
    onw = jnp.pad(out_norm_w.reshape(20, LANES), ((0, 4), (0, 0)))
    b = jnp.pad(b_forget, ((0, 6), (0, LANES - FOX_HEADS)))
    return jnp.concatenate([norm_w.reshape(16, LANES), mem_norm_w.reshape(16, LANES), onw,
                            final_norm_w.reshape(8, LANES), b], axis=0)


def _unpack_small(p):
    return (p[0:16].reshape(2, D_MODEL), p[16:32].reshape(2, D_MODEL), p[32:52].reshape(2, MIX_W),
            p[56:64].reshape(D_MODEL), p[64:66, :FOX_HEADS])


def kernel(x, mem, norm_w, w_in, b_forget, mem_norm_w, w_mem_kv, out_norm_w, w_out, final_norm_w, loss_target, m_norm_w, m_w_in, m_b_forget, m_mem_norm_w, m_w_mem_kv, m_out_norm_w, m_w_out, m_final_norm_w, v_norm_w, v_w_in, v_b_forget, v_mem_norm_w, v_w_mem_kv, v_out_norm_w, v_w_out, v_final_norm_w):
    kv_rows = w_mem_kv.shape[1]
    out_rows = w_out.shape[1]
    me = _block_index(*_my_place())

    def shards(l):
        return [w_in[l].T.astype(BF16), w_mem_kv[l].astype(BF16), w_out[l].astype(BF16)]

    def full_in(g_in):
        return _rearrange_w_in(g_in.reshape(IN_W, D_MODEL))

    def full_kv_out(g_kv, g_out):
        return g_kv.reshape(D_MODEL, 2 * MEM_W), g_out.reshape(MIX_W, D_MODEL)

    def in_blocks(g):
        segments = [(0, 3072, 0), (3072, 3080, QKV_W + MIX_W), (3080, 3336, 3072),
                    (3336, IN_W, QKV_W)]

        def block(k):
            lo, hi = k * SHARD_W, (k + 1) * SHARD_W
            pieces = [g[at + max(lo, a) - a:at + min(hi, b) - a]
                      for a, b, at in segments if max(lo, a) < min(hi, b)]
            return jnp.concatenate(pieces, axis=0).astype(BF16)

        return jnp.stack([block(k) for k in range(N_DEV)])

    def kv_blocks(g):
        return g.reshape(N_DEV, kv_rows, 2 * MEM_W).astype(BF16)

    def out_blocks(g):
        return g.reshape(N_DEV, out_rows, D_MODEL).astype(BF16)

    def with_own(land, own):
        return lax.dynamic_update_slice(land, own[None], (me,) + (0,) * own.ndim)

    def with_own_of(land, blocked):
        return lax.dynamic_update_slice(land, lax.dynamic_slice_in_dim(blocked, me, 1, axis=0),
                                        (me,) + (0,) * (land.ndim - 1))

    def row(v):
        return v.reshape(1, -1)

    def cols(first, size):
        return (1, first, size)

    half = D_MODEL // 2

    g_in0, g_kv0, g_out0 = _all_gather_weights(shards(0), "all_gather_l0")
    w_r0 = full_in(g_in0)
    s_in1, s_kv1, s_out1 = shards(1)
    x1, saved0, (l_in1,), (wkv0, wout0) = _layer_fwd(
        x[0], mem[0], row(norm_w[0]), w_r0, b_forget[0], row(mem_norm_w[0]),
        lambda lands: full_kv_out(g_kv0, g_out0), row(out_norm_w[0]), 0,
        travel=([s_in1], False, [([0], [cols(0, half)]), ([0], [cols(half, half)])]))
    w_r1 = full_in(with_own(l_in1, s_in1))
    x2, saved1, _, (wkv1, wout1) = _layer_fwd(
        x1, mem[0], row(norm_w[1]), w_r1, b_forget[1], row(mem_norm_w[1]),
        lambda lands: full_kv_out(with_own(lands[0], s_kv1), with_own(lands[1], s_out1)),
        row(out_norm_w[1]), 1, travel=([s_kv1, s_out1], False, [([0, 1], [None, None]), None]))

    dx2, loss_part, dfnw = _final_fwd_bwd(x2, row(final_norm_w), loss_target[0], "final_fwd_bwd")

    dx1, gr1, (l_out1,) = _layer_bwd(
        dx2, saved1, mem[0], w_r1, wkv1, wout1, 1,
        travel=lambda dwout: ([out_blocks(dwout)], True, [([0], [None]), None]))
    p_in1, p_kv1 = in_blocks(gr1["w_r"]), kv_blocks(gr1["w_mem_kv"])
    split = 5 * LANES
    grad_x, gr0, (l_in1, l_kv1, l_out0) = _layer_bwd(
        dx1, saved0, mem[0], w_r0, wkv0, wout0, 0,
        travel=lambda dwout: ([p_in1, p_kv1, out_blocks(dwout)], True,
                              [([0, 1], [cols(0, split), None]),
                               ([0, 2], [cols(split, D_MODEL - split), None])]))
    r_out1 = with_own_of(l_out1, out_blocks(gr1["w_out"]))
    r_in1, r_kv1 = with_own_of(l_in1, p_in1), with_own_of(l_kv1, p_kv1)
    r_out0 = with_own_of(l_out0, out_blocks(gr0["w_out"]))

    def both(name):
        return jnp.stack([gr0[name], gr1[name]])

    small = _pack_small(both("norm_w"), both("mem_norm_w"), both("out_norm_w"), dfnw[0],
                        both("b_forget")).at[LOSS_ROW].set(loss_part[0])
    p_small = jnp.broadcast_to(small[None], (N_DEV, SMALL_ROWS, LANES))
    by_core = [v.reshape((N_CHIP, 2) + v.shape[1:])
               for v in (in_blocks(gr0["w_r"]), kv_blocks(gr0["w_mem_kv"]))]
    from_sibling = _pair_swap(by_core, "grads_l0_pair_swap")
    core = lax.axis_index("c")
    chip_sums = [_pair_add(lax.dynamic_index_in_dim(v, core, axis=1, keepdims=False), got,
                           f"grads_l0_pair_add_{a}")
                 for a, (v, got) in enumerate(zip(by_core, from_sibling))]
    r_in0, r_kv0, r_small = _chip_exchange(chip_sums, [p_small], "exchange_grads_l0")

    def view(v):
        return jnp.transpose(v, (2, 0, 1))

    g_w_in, d_w_in, nm_w_in, nv_w_in = [jnp.transpose(v, (1, 2, 0)) for v in _adamw_w_in(
        [r_in0, r_in1], view(w_in), view(m_w_in), view(v_w_in), "adamw_w_in")]
    g_w_kv, d_w_kv, nm_w_kv, nv_w_kv = _adamw_sum(
        [r_kv0, r_kv1], w_mem_kv, m_w_mem_kv, v_w_mem_kv, (kv_rows, 2 * MEM_W), "adamw_w_mem_kv")
    g_w_out, d_w_out, nm_w_out, nv_w_out = _adamw_sum(
        [r_out0, r_out1], w_out, m_w_out, v_w_out, (out_rows, D_MODEL), "adamw_w_out")
    w_small = _pack_small(norm_w, mem_norm_w, out_norm_w, final_norm_w, b_forget)[None]
    m_small = _pack_small(m_norm_w, m_mem_norm_w, m_out_norm_w, m_final_norm_w, m_b_forget)[None]
    v_small = _pack_small(v_norm_w, v_mem_norm_w, v_out_norm_w, v_final_norm_w, v_b_forget)[None]
    small_out = _adamw_sum([r_small], w_small, m_small, v_small, (SMALL_ROWS, LANES), "adamw_small")
    (g_nw, g_mnw, g_onw, g_fnw, g_b), (d_nw, d_mnw, d_onw, d_fnw, d_b), \
        (nm_nw, nm_mnw, nm_onw, nm_fnw, nm_b), (nv_nw, nv_mnw, nv_onw, nv_fnw, nv_b) = [
            _unpack_small(t[0]) for t in small_out]
    loss = small_out[0][0, LOSS_ROW, 0]

    return (loss, grad_x[None],
            g_nw, g_w_in, g_b, g_mnw, g_w_kv, g_onw, g_w_out, g_fnw,
            d_nw, d_w_in, d_b, d_mnw, d_w_kv, d_onw, d_w_out, d_fnw,
            nm_nw, nm_w_in, nm_b, nm_mnw, nm_w_kv, nm_onw, nm_w_out, nm_fnw,
            nv_nw, nv_w_in, nv_b, nv_mnw, nv_w_kv, nv_onw, nv_w_out, nv_fnw)
```

```python
import functools
from typing import NamedTuple

import jax
import jax.numpy as jnp
from jax import lax
from jax.experimental import pallas as pl
from jax.experimental.pallas import tpu as pltpu

F32 = jnp.float32
BF16 = jnp.bfloat16

N_DEV = 8
D_MODEL = 1024
HEAD_DIM = 64
LANES = 128
SB_W = 512
FX_W = 512
MEM_W = 256
MIX_W = 1280
FOX_HEADS = 8
IN_W = 4616
SHARD_W = IN_W // N_DEV
QKV_W = 3 * SB_W + 3 * FX_W + MEM_W
FL_PAD = 256
GF_W = MIX_W + FL_PAD
WR_W = QKV_W + GF_W
EPS = 1e-6
T = 256
TM = 256
TQM = 512
Q_SCALE = 0.125
NEG = -1e30
UNDERFLOW = -110.0

ADAM_LR = 0.001
ADAM_B1 = 0.9
ADAM_B2 = 0.999
ADAM_EPS = 1e-08
ADAM_WD = 0.01
ADAM_STEP = 10

SMALL_ROWS = 72
LOSS_ROW = 66

_NT = (((1,), (1,)), ((), ()))
_TN = (((0,), (0,)), ((), ()))

_ARB1 = pltpu.CompilerParams(dimension_semantics=("arbitrary",))
_ARB2 = pltpu.CompilerParams(dimension_semantics=("arbitrary", "arbitrary"))


def _dot(a, b):
    return jnp.dot(a, b, preferred_element_type=F32)


def _dot_nt(a, b):
    return lax.dot_general(a, b, _NT, preferred_element_type=F32)


def _dot_tn(a, b):
    return lax.dot_general(a, b, _TN, preferred_element_type=F32)


def _split2(x):
    hi = x.astype(BF16)
    lo = (x - hi.astype(F32)).astype(BF16)
    return hi, lo


def _stack2(u):
    return jnp.concatenate([u, u], axis=0)


def _cum2(x, u2):
    hi, lo = _split2(x)
    return _dot(jnp.concatenate([hi, lo], axis=1), u2)


def _tri3(tri, x, dot=None):
    dot = dot or _dot
    hi = x.astype(BF16)
    r1 = x - hi.astype(F32)
    mid = r1.astype(BF16)
    lo = (r1 - mid.astype(F32)).astype(BF16)
    return dot(tri, hi) + dot(tri, mid) + dot(tri, lo)


def _iota2(shape, dim):
    return lax.broadcasted_iota(jnp.int32, shape, dim)


def _head_block_diag():
    r = _iota2((LANES, LANES), 0) // HEAD_DIM
    c = _iota2((LANES, LANES), 1) // HEAD_DIM
    return _stack2(jnp.where(r == c, 1.0, 0.0).astype(BF16))


def _head_mean(x, bd):
    return _cum2(x, bd) * (1.0 / HEAD_DIM)


def _sigmoid(x):
    return 1.0 / (1.0 + jnp.exp(-x))


def _log_sigmoid(x):
    return jnp.minimum(x, 0.0) - jnp.log(1.0 + jnp.exp(-jnp.abs(x)))


def _running_top(r_ref):
    return jnp.max(jnp.maximum(r_ref[0], r_ref[1]))


def _fox_tiles_left(i, pair, nq, fb_ref, tile):
    def bound(j):
        b = []
        for h in range(2):
            head = 2 * pair + h
            b.append(2.0 * fb_ref[2 * nq + i, head] * fb_ref[3 * nq, head]
                     + fb_ref[2 * i, head] - fb_ref[2 * j + 1, head])
        return jnp.maximum(b[0], b[1])

    def more(j):
        return jnp.logical_and(j >= 0, bound(jnp.maximum(j, 0)) > UNDERFLOW)

    def step(j):
        tile(j, False)
        return j - 1

    return lax.while_loop(more, step, i - 1)


def _pair_masks():
    lane = _iota2((1, LANES), 1)
    return lane < HEAD_DIM


def _split_pair(x, m0):
    zero = jnp.zeros_like(x)
    return jnp.where(m0, x, zero), jnp.where(m0, zero, x)


def _inproj_fwd(x, nw, w_r, name):
    s = x.shape[0]

    def body(x_ref, nw_ref, w_ref, qkv_ref, gf_ref):
        xv = x_ref[...]
        r = lax.rsqrt(jnp.mean(xv * xv, axis=-1, keepdims=True) + EPS)
        h = (xv * r * nw_ref[...]).astype(BF16)
        for c in range(0, QKV_W, 256):
            qkv_ref[:, c:c + 256] = _dot_nt(h, w_ref[c:c + 256, :]).astype(BF16)
        for c in range(0, GF_W, 256):
            gf_ref[:, c:c + 256] = _dot_nt(h, w_ref[QKV_W + c:QKV_W + c + 256, :])

    return pl.pallas_call(
        body, name=name, grid=(s // TM,),
        in_specs=[pl.BlockSpec((TM, D_MODEL), lambda i: (i, 0)),
                  pl.BlockSpec((1, D_MODEL), lambda i: (0, 0)),
                  pl.BlockSpec((WR_W, D_MODEL), lambda i: (0, 0))],
        out_specs=[pl.BlockSpec((TM, QKV_W), lambda i: (i, 0)),
                   pl.BlockSpec((TM, GF_W), lambda i: (i, 0))],
        out_shape=[jax.ShapeDtypeStruct((s, QKV_W), BF16), jax.ShapeDtypeStruct((s, GF_W), F32)],
        compiler_params=_ARB1,
    )(x, nw, w_r)


def _fox_prep_fwd(gf, qkv, bpad, name):
    s = gf.shape[0]
    nq = s // T
    nrow = -(-(3 * nq + 1) // 8) * 8

    def body(fl_ref, q_ref, k_ref, b_ref, fq_ref, fr_ref, fb_ref):
        tri = jnp.where(_iota2((T, T), 0) >= _iota2((T, T), 1), 1.0, 0.0).astype(BF16)
        m0 = _pair_masks()
        lane = _iota2((1, LANES), 1)
        norms = [jnp.zeros((1, LANES), F32) for _ in range(nq + 1)]
        for p in range(FOX_HEADS // 2):
            cols = slice(p * LANES, (p + 1) * LANES)
            q2 = (q_ref[:, cols] * jnp.asarray(Q_SCALE, BF16)).astype(F32)
            k2 = k_ref[:, cols].astype(F32)
            q2, k2 = q2 * q2, k2 * k2
            for h in range(2):
                mine = m0 if h == 0 else jnp.logical_not(m0)
                qn = jnp.sqrt(jnp.sum(jnp.where(mine, q2, 0.0), axis=1, keepdims=True))
                kn = jnp.sqrt(jnp.sum(jnp.where(mine, k2, 0.0), axis=1, keepdims=True))
                tops = [jnp.max(qn[j * T:(j + 1) * T], axis=0, keepdims=True) for j in range(nq)]
                tops.append(jnp.max(kn, axis=0, keepdims=True))
                norms = [jnp.where(lane == 2 * p + h, top, row) for top, row in zip(tops, norms)]
        for j in range(nq + 1):
            fb_ref[2 * nq + j:2 * nq + j + 1, :] = norms[j]
        fb_ref[3 * nq + 1:, :] = jnp.zeros((nrow - 3 * nq - 1, LANES), F32)
        carry = jnp.zeros((1, LANES), F32)
        for blk in range(s // T):
            rows = slice(blk * T, (blk + 1) * T)
            lf = _log_sigmoid(fl_ref[rows, :] + b_ref[...])
            c = _tri3(tri, lf) + carry
            carry = c[T - 1:T, :]
            for p in range(FOX_HEADS // 2):
                fq_ref[rows, p * LANES:(p + 1) * LANES] = jnp.where(
                    m0, c[:, 2 * p:2 * p + 1], c[:, 2 * p + 1:2 * p + 2])
            fr_ref[:, rows] = c.T[0:FOX_HEADS, :]
            fb_ref[2 * blk:2 * blk + 1, :] = c[0:1, :]
            fb_ref[2 * blk + 1:2 * blk + 2, :] = carry

    base = 3 * SB_W // FX_W
    return pl.pallas_call(
        body, name=name, grid=(1,),
        in_specs=[pl.BlockSpec((s, LANES), lambda i: (0, MIX_W // LANES)),
                  pl.BlockSpec((s, FX_W), lambda i: (0, base)),
                  pl.BlockSpec((s, FX_W), lambda i: (0, base + 1)),
                  pl.BlockSpec((1, LANES), lambda i: (0, 0))],
        out_specs=[pl.BlockSpec((s, FX_W), lambda i: (0, 0)),
                   pl.BlockSpec((FOX_HEADS, s), lambda i: (0, 0)),
                   pl.BlockSpec((nrow, LANES), lambda i: (0, 0))],
        out_shape=[jax.ShapeDtypeStruct((s, FX_W), F32), jax.ShapeDtypeStruct((FOX_HEADS, s), F32),
                   jax.ShapeDtypeStruct((nrow, LANES), F32)],
        compiler_params=_ARB1,
    )(gf, qkv, qkv, bpad)


def _sb_fwd(qkv, name, rider=None):
    s = qkv.shape[0]

    def body(q_ref, k_ref, v_ref, o_ref, acc_ref, r_ref, as_ref):
        i = pl.program_id(1)
        m0 = _pair_masks()
        qh = _split_pair(q_ref[...] * jnp.asarray(Q_SCALE, BF16), m0)
        strict = _iota2((T, T), 0) > _iota2((T, T), 1)
        u2 = _stack2(jnp.where(strict, 1.0, 0.0).astype(BF16))
        acc_ref[...] = jnp.zeros_like(acc_ref)
        r_ref[...] = jnp.zeros_like(r_ref)
        hs = range(2)

        def flush(j):
            v = v_ref[pl.ds(pl.multiple_of(j * T, T), T), :]
            for h in hs:
                acc_ref[h] += _dot(as_ref[h], v)

        def tile(j, diag):
            k = k_ref[pl.ds(pl.multiple_of(j * T, T), T), :]
            z = [_dot_nt(qh[h], k) for h in hs]
            if not diag:
                flush(j + 1)
            la = [jnp.minimum(z[h], 0.0) - jnp.log(1.0 + jnp.exp(-jnp.abs(z[h]))) for h in hs]
            lf = [la[h] - z[h] for h in hs]
            if diag:
                lf = [jnp.where(strict, lf[h], 0.0) for h in hs]
            cin = [_cum2(lf[h], u2) for h in hs]
            a = [jnp.exp(la[h] + cin[h] + r_ref[h]) for h in hs]
            if diag:
                a = [jnp.where(strict, a[h], 0.0) for h in hs]
            for h in hs:
                r_ref[h] += cin[h][:, 0:1] + lf[h][:, 0:1]
                as_ref[h] = a[h].astype(BF16)

        tile(i, True)

        def more(state):
            j, top = state
            return jnp.logical_and(j >= 0, top > UNDERFLOW)

        def step(state):
            j, _ = state
            tile(j, False)
            return j - 1, _running_top(r_ref)

        j_left, _ = lax.while_loop(more, step, (i - 1, _running_top(r_ref)))
        flush(j_left + 1)
        o_ref[...] = jnp.where(m0, acc_ref[0], acc_ref[1])

    nb = SB_W // LANES
    (ysb,), lands = _ride(dict(
        body=body, name=name, grid=(nb, s // T),
        in_specs=[pl.BlockSpec((T, LANES), lambda p, i: (i, p)),
                  pl.BlockSpec((s, LANES), lambda p, i: (0, nb + p)),
                  pl.BlockSpec((s, LANES), lambda p, i: (0, 2 * nb + p))],
        out_specs=[pl.BlockSpec((T, LANES), lambda p, i: (i, p))],
        out_shape=[jax.ShapeDtypeStruct((s, SB_W), F32)],
        scratch_shapes=[pltpu.VMEM((2, T, LANES), F32), pltpu.VMEM((2, T, 1), F32),
                        pltpu.VMEM((2, T, T), BF16)],
        compiler_params=_ARB2, operands=[qkv, qkv, qkv]), rider)
    return ysb, lands


def _fox_fwd(qkv, fqb, frow, fbounds, name, rider=None):
    s = qkv.shape[0]

    def body(q_ref, k_ref, v_ref, fq_ref, fr_ref, fb_ref, o_ref, lse_ref, acc_ref, m_ref, ps_ref):
        pair = pl.program_id(0)
        i = pl.program_id(1)
        m0 = _pair_masks()
        qh = _split_pair(q_ref[...] * jnp.asarray(Q_SCALE, BF16), m0)
        fq = fq_ref[...]
        fqh = (fq[:, 0:1], fq[:, HEAD_DIM:HEAD_DIM + 1])
        causal = _iota2((T, T), 0) >= _iota2((T, T), 1)
        ones = jnp.ones((T, LANES), BF16)
        acc_ref[...] = jnp.zeros_like(acc_ref)
        m_ref[...] = jnp.full_like(m_ref, NEG)
        hs = range(2)

        def flush(j):
            v = v_ref[pl.ds(pl.multiple_of(j * T, T), T), :]
            va2 = _stack2(jnp.concatenate([v, ones], axis=1))
            for h in hs:
                acc_ref[h] += _dot(ps_ref[h], va2)

        def tile(j, diag):
            off = pl.multiple_of(j * T, T)
            k = k_ref[pl.ds(off, T), :]
            sc = [_dot_nt(qh[h], k) + fqh[h] - fr_ref[h:h + 1, pl.ds(off, T)] for h in hs]
            if not diag:
                flush(j + 1)
            if diag:
                sc = [jnp.where(causal, sc[h], NEG) for h in hs]
            m_new = [jnp.maximum(m_ref[h], jnp.max(sc[h], axis=1, keepdims=True)) for h in hs]
            p = [jnp.exp(sc[h] - m_new[h]) for h in hs]
            for h in hs:
                acc_ref[h] = acc_ref[h] * jnp.exp(m_ref[h] - m_new[h])
                m_ref[h] = m_new[h]
                p_hi, p_lo = _split2(p[h])
                ps_ref[h] = jnp.concatenate([p_hi, p_lo], axis=1)

        tile(i, True)
        j_left = _fox_tiles_left(i, pair, s // T, fb_ref, tile)
        flush(j_left + 1)
        acc = (acc_ref[0], acc_ref[1])
        o_ref[...] = jnp.where(m0, acc[0][:, :LANES] / acc[0][:, LANES:],
                               acc[1][:, :LANES] / acc[1][:, LANES:])
        lse_ref[...] = jnp.where(m0, m_ref[0] + jnp.log(acc[0][:, LANES:]),
                                 m_ref[1] + jnp.log(acc[1][:, LANES:]))

    nb = FX_W // LANES
    base = 3 * SB_W // LANES
    (yfx, lse), lands = _ride(dict(
        body=body, name=name, grid=(nb, s // T),
        in_specs=[pl.BlockSpec((T, LANES), lambda p, i: (i, base + p)),
                  pl.BlockSpec((s, LANES), lambda p, i: (0, base + nb + p)),
                  pl.BlockSpec((s, LANES), lambda p, i: (0, base + 2 * nb + p)),
                  pl.BlockSpec((T, LANES), lambda p, i: (i, p)),
                  pl.BlockSpec((None, 2, s), lambda p, i: (p, 0, 0)),
                  pl.BlockSpec(memory_space=pltpu.SMEM)],
        out_specs=[pl.BlockSpec((T, LANES), lambda p, i: (i, p)),
                   pl.BlockSpec((T, LANES), lambda p, i: (i, p))],
        out_shape=[jax.ShapeDtypeStruct((s, FX_W), F32), jax.ShapeDtypeStruct((s, FX_W), F32)],
        scratch_shapes=[pltpu.VMEM((2, T, 2 * LANES), F32), pltpu.VMEM((2, T, 1), F32),
                        pltpu.VMEM((2, T, 2 * T), BF16)],
        compiler_params=_ARB2, operands=[qkv, qkv, qkv, fqb, frow, fbounds]), rider)
    return yfx, lse, lands


def _memkv_fwd(mem, mnw, wkv, name):
    n = mem.shape[0]

    def body(mem_ref, mnw_ref, w_ref, kv_ref):
        mv = mem_ref[...]
        r = lax.rsqrt(jnp.mean(mv * mv, axis=-1, keepdims=True) + EPS)
        hm = (mv * r * mnw_ref[...]).astype(BF16)
        kv_ref[...] = _dot(hm, w_ref[...]).astype(BF16)

    return pl.pallas_call(
        body, name=name, grid=(1,),
        in_specs=[pl.BlockSpec((n, D_MODEL), lambda i: (0, 0)),
                  pl.BlockSpec((1, D_MODEL), lambda i: (0, 0)),
                  pl.BlockSpec((D_MODEL, 2 * MEM_W), lambda i: (0, 0))],
        out_specs=pl.BlockSpec((n, 2 * MEM_W), lambda i: (0, 0)),
        out_shape=jax.ShapeDtypeStruct((n, 2 * MEM_W), BF16),
        compiler_params=_ARB1,
    )(mem, mnw, wkv)


def _mem_fwd(qkv, kv, name):
    s = qkv.shape[0]
    n = kv.shape[0]

    def body(q_ref, k_ref, v_ref, o_ref, lse_ref):
        m0 = _pair_masks()
        qh = _split_pair(q_ref[...] * jnp.asarray(Q_SCALE, BF16), m0)
        k = k_ref[...]
        v = v_ref[...]
        outs, lses = [], []
        for h in range(2):
            sc = _dot_nt(qh[h], k)
            mx = jnp.max(sc, axis=1, keepdims=True)
            p = jnp.exp(sc - mx)
            l = jnp.sum(p, axis=1, keepdims=True)
            outs.append(_dot(p.astype(BF16), v) / l)
            lses.append(mx + jnp.log(l))
        o_ref[...] = jnp.where(m0, outs[0], outs[1])
        lse_ref[...] = jnp.where(m0, lses[0], lses[1])

    nb = MEM_W // LANES
    base = (3 * SB_W + 3 * FX_W) // LANES
    return pl.pallas_call(
        body, name=name, grid=(nb, s // TQM),
        in_specs=[pl.BlockSpec((TQM, LANES), lambda p, i: (i, base + p)),
                  pl.BlockSpec((n, LANES), lambda p, i: (0, p)),
                  pl.BlockSpec((n, LANES), lambda p, i: (0, nb + p))],
        out_specs=[pl.BlockSpec((TQM, LANES), lambda p, i: (i, p)),
                   pl.BlockSpec((TQM, LANES), lambda p, i: (i, p))],
        out_shape=[jax.ShapeDtypeStruct((s, MEM_W), F32), jax.ShapeDtypeStruct((s, MEM_W), F32)],
        compiler_params=_ARB2,
    )(qkv, kv, kv)


def _mix_chunk(c, ysb_ref, yfx_ref, ym_ref):
    if c < SB_W // LANES:
        return ysb_ref[:, c * LANES:(c + 1) * LANES]
    c -= SB_W // LANES
    if c < FX_W // LANES:
        return yfx_ref[:, c * LANES:(c + 1) * LANES]
    c -= FX_W // LANES
    return ym_ref[:, c * LANES:(c + 1) * LANES]


def _outproj_fwd(ysb, yfx, ym, gf, onw, wout, x, name):
    s = x.shape[0]

    def body(ysb_ref, yfx_ref, ym_ref, g_ref, onw_ref, w_ref, x_ref, o_ref, yg_ref):
        bd = _head_block_diag()
        for c in range(MIX_W // LANES):
            sl = slice(c * LANES, (c + 1) * LANES)
            u = _mix_chunk(c, ysb_ref, yfx_ref, ym_ref)
            r = lax.rsqrt(_head_mean(u * u, bd) + EPS)
            g = g_ref[:, sl]
            yg_ref[:, sl] = (u * r * onw_ref[:, sl] * (g * _sigmoid(g))).astype(BF16)
        o_ref[...] = x_ref[...] + _dot(yg_ref[...], w_ref[...])

    return pl.pallas_call(
        body, name=name, grid=(s // TM,),
        in_specs=[pl.BlockSpec((TM, SB_W), lambda i: (i, 0)),
                  pl.BlockSpec((TM, FX_W), lambda i: (i, 0)),
                  pl.BlockSpec((TM, MEM_W), lambda i: (i, 0)),
                  pl.BlockSpec((TM, MIX_W), lambda i: (i, 0)),
                  pl.BlockSpec((1, MIX_W), lambda i: (0, 0)),
                  pl.BlockSpec((MIX_W, D_MODEL), lambda i: (0, 0)),
                  pl.BlockSpec((TM, D_MODEL), lambda i: (i, 0))],
        out_specs=pl.BlockSpec((TM, D_MODEL), lambda i: (i, 0)),
        out_shape=jax.ShapeDtypeStruct((s, D_MODEL), F32),
        scratch_shapes=[pltpu.VMEM((TM, MIX_W), BF16)],
        compiler_params=_ARB1,
    )(ysb, yfx, ym, gf, onw, wout, x)


def _final_fwd_bwd(x, fnw, target, name):
    s = x.shape[0]

    def body(x_ref, w_ref, t_ref, dx_ref, loss_ref, dw_ref):
        @pl.when(pl.program_id(0) == 0)
        def _():
            loss_ref[...] = jnp.zeros_like(loss_ref)
            dw_ref[...] = jnp.zeros_like(dw_ref)

        xv = x_ref[...]
        w = w_ref[...]
        r = lax.rsqrt(jnp.mean(xv * xv, axis=-1, keepdims=True) + EPS)
        xh = xv * r
        err = xh * w - t_ref[...]
        part = jnp.sum(jnp.sum(err * err, axis=1, keepdims=True), axis=0, keepdims=True)
        loss_ref[...] += part * (0.5 / D_MODEL)
        dy = err * (1.0 / D_MODEL)
        dw_ref[...] += jnp.sum(dy * xh, axis=0, keepdims=True)
        dxh = dy * w
        dx_ref[...] = r * (dxh - xh * jnp.mean(dxh * xh, axis=-1, keepdims=True))

    return pl.pallas_call(
        body, name=name, grid=(s // TM,),
        in_specs=[pl.BlockSpec((TM, D_MODEL), lambda i: (i, 0)),
                  pl.BlockSpec((1, D_MODEL), lambda i: (0, 0)),
                  pl.BlockSpec((TM, D_MODEL), lambda i: (i, 0))],
        out_specs=[pl.BlockSpec((TM, D_MODEL), lambda i: (i, 0)),
                   pl.BlockSpec((1, LANES), lambda i: (0, 0)),
                   pl.BlockSpec((1, D_MODEL), lambda i: (0, 0))],
        out_shape=[jax.ShapeDtypeStruct((s, D_MODEL), F32), jax.ShapeDtypeStruct((1, LANES), F32),
                   jax.ShapeDtypeStruct((1, D_MODEL), F32)],
        compiler_params=_ARB1,
    )(x, fnw, target)


def _outproj_bwd(dxo, wout, ysb, yfx, ym, gf, onw, name):
    s = dxo.shape[0]

    def body(dx_ref, w_ref, ysb_ref, yfx_ref, ym_ref, g_ref, onw_ref,
             dysb_ref, dyfx_ref, dym_ref, dg_ref, dw_ref, donw_ref, yg_ref):
        @pl.when(pl.program_id(0) == 0)
        def _():
            dw_ref[...] = jnp.zeros_like(dw_ref)
            donw_ref[...] = jnp.zeros_like(donw_ref)

        dxb = dx_ref[...].astype(BF16)
        dyg = _dot_nt(dxb, w_ref[...])
        bd = _head_block_diag()
        for c in range(MIX_W // LANES):
            sl = slice(c * LANES, (c + 1) * LANES)
            u = _mix_chunk(c, ysb_ref, yfx_ref, ym_ref)
            r = lax.rsqrt(_head_mean(u * u, bd) + EPS)
            yn = u * r
            g = g_ref[:, sl]
            sg = _sigmoid(g)
            sil = g * sg
            onw = onw_ref[:, sl]
            e = dyg[:, sl]
            yg_ref[:, sl] = (yn * onw * sil).astype(BF16)
            donw_ref[:, sl] += jnp.sum(e * yn * sil, axis=0, keepdims=True)
            dg_ref[:, sl] = (e * yn * onw * (sg * (1.0 + g * (1.0 - sg)))).astype(BF16)
            dyn = e * onw * sil
            du = (r * (dyn - yn * _head_mean(dyn * yn, bd))).astype(BF16)
            if c < 4:
                dysb_ref[:, c * LANES:(c + 1) * LANES] = du
            elif c < 8:
                dyfx_ref[:, (c - 4) * LANES:(c - 3) * LANES] = du
            else:
                dym_ref[:, (c - 8) * LANES:(c - 7) * LANES] = du
        dw_ref[...] += _dot_tn(yg_ref[...], dxb)

    return pl.pallas_call(
        body, name=name, grid=(s // TM,),
        in_specs=[pl.BlockSpec((TM, D_MODEL), lambda i: (i, 0)),
                  pl.BlockSpec((MIX_W, D_MODEL), lambda i: (0, 0)),
                  pl.BlockSpec((TM, SB_W), lambda i: (i, 0)),
                  pl.BlockSpec((TM, FX_W), lambda i: (i, 0)),
                  pl.BlockSpec((TM, MEM_W), lambda i: (i, 0)),
                  pl.BlockSpec((TM, MIX_W), lambda i: (i, 0)),
                  pl.BlockSpec((1, MIX_W), lambda i: (0, 0))],
        out_specs=[pl.BlockSpec((TM, SB_W), lambda i: (i, 0)),
                   pl.BlockSpec((TM, FX_W), lambda i: (i, 0)),
                   pl.BlockSpec((TM, MEM_W), lambda i: (i, 0)),
                   pl.BlockSpec((TM, MIX_W), lambda i: (i, 0)),
                   pl.BlockSpec((MIX_W, D_MODEL), lambda i: (0, 0)),
                   pl.BlockSpec((1, MIX_W), lambda i: (0, 0))],
        out_shape=[jax.ShapeDtypeStruct((s, SB_W), BF16), jax.ShapeDtypeStruct((s, FX_W), BF16),
                   jax.ShapeDtypeStruct((s, MEM_W), BF16), jax.ShapeDtypeStruct((s, MIX_W), BF16),
                   jax.ShapeDtypeStruct((MIX_W, D_MODEL), F32), jax.ShapeDtypeStruct((1, MIX_W), F32)],
        scratch_shapes=[pltpu.VMEM((TM, MIX_W), BF16)],
        compiler_params=_ARB1,
    )(dxo, wout, ysb, yfx, ym, gf, onw)


def _row_dots(do_ref, o_ref, m0):
    prod = do_ref[...].astype(F32) * o_ref[...]
    zero = jnp.zeros_like(prod)
    return (jnp.sum(jnp.where(m0, prod, zero), axis=1, keepdims=True),
            jnp.sum(jnp.where(m0, zero, prod), axis=1, keepdims=True))


def _sb_bwd(qkv, o, do, name, rider=None):
    s = qkv.shape[0]
    nq = s // T

    def body(q_ref, k_ref, v_ref, o_ref, do_ref, dq_ref, dk_ref, dv_ref,
             dqa_ref, dka_ref, dva_ref, rl_ref, rg_ref, dzs_ref, abs_ref):
        i = pl.program_id(1)

        @pl.when(i == 0)
        def _():
            dka_ref[...] = jnp.zeros_like(dka_ref)
            dva_ref[...] = jnp.zeros_like(dva_ref)

        m0 = _pair_masks()
        qh = _split_pair(q_ref[...] * jnp.asarray(Q_SCALE, BF16), m0)
        doh = _split_pair(do_ref[...], m0)
        dsum = _row_dots(do_ref, o_ref, m0)
        strict = _iota2((T, T), 0) > _iota2((T, T), 1)
        u2 = _stack2(jnp.where(strict, 1.0, 0.0).astype(BF16))
        dqa_ref[...] = jnp.zeros_like(dqa_ref)
        rl_ref[...] = jnp.zeros_like(rl_ref)
        rg_ref[...] = jnp.zeros_like(rg_ref)

        hs = range(2)

        def flush(j):
            off = pl.multiple_of(j * T, T)
            k = k_ref[pl.ds(off, T), :]
            for h in hs:
                dqa_ref[h] += _dot(dzs_ref[h], k)
            dka_ref[pl.ds(off, T), :] += _dot_tn(dzs_ref[0], qh[0]) + _dot_tn(dzs_ref[1], qh[1])
            dva_ref[pl.ds(off, T), :] += _dot_tn(abs_ref[0], doh[0]) + _dot_tn(abs_ref[1], doh[1])

        def tile(j, diag):
            off = pl.multiple_of(j * T, T)
            k = k_ref[pl.ds(off, T), :]
            v = v_ref[pl.ds(off, T), :]
            z = [_dot_nt(qh[h], k) for h in hs]
            da = [_dot_nt(doh[h], v) for h in hs]
            if not diag:
                flush(j + 1)
            la = [jnp.minimum(z[h], 0.0) - jnp.log(1.0 + jnp.exp(-jnp.abs(z[h]))) for h in hs]
            lf = [la[h] - z[h] for h in hs]
            if diag:
                lf = [jnp.where(strict, lf[h], 0.0) for h in hs]
            cin = [_cum2(lf[h], u2) for h in hs]
            a = [jnp.exp(la[h] + cin[h] + rl_ref[h]) for h in hs]
            if diag:
                a = [jnp.where(strict, a[h], 0.0) for h in hs]
            ab = [a[h].astype(BF16) for h in hs]
            g = [ab[h].astype(F32) * da[h] for h in hs]
            gin = [_cum2(g[h], u2) for h in hs]
            dz = [g[h] - jnp.exp(la[h]) * ((dsum[h] - rg_ref[h]) - gin[h]) for h in hs]
            if diag:
                dz = [jnp.where(strict, dz[h], 0.0) for h in hs]
            for h in hs:
                rl_ref[h] += cin[h][:, 0:1] + lf[h][:, 0:1]
                rg_ref[h] += gin[h][:, 0:1] + g[h][:, 0:1]
                dzs_ref[h] = dz[h].astype(BF16)
                abs_ref[h] = ab[h]

        tile(i, True)

        def more(state):
            j, top = state
            return jnp.logical_and(j >= 0, top > UNDERFLOW)

        def step(state):
            j, _ = state
            tile(j, False)
            return j - 1, _running_top(rl_ref)

        j_left, _ = lax.while_loop(more, step, (i - 1, _running_top(rl_ref)))
        flush(j_left + 1)
        dq_ref[...] = (jnp.where(m0, dqa_ref[0], dqa_ref[1]) * Q_SCALE).astype(BF16)

        @pl.when(i == nq - 1)
        def _():
            dk_ref[...] = dka_ref[...].astype(BF16)
            dv_ref[...] = dva_ref[...].astype(BF16)

    nb = SB_W // LANES
    (dq, dk, dv), lands = _ride(dict(
        body=body, name=name, grid=(nb, nq),
        in_specs=[pl.BlockSpec((T, LANES), lambda p, i: (i, p)),
                  pl.BlockSpec((s, LANES), lambda p, i: (0, nb + p)),
                  pl.BlockSpec((s, LANES), lambda p, i: (0, 2 * nb + p)),
                  pl.BlockSpec((T, LANES), lambda p, i: (i, p)),
                  pl.BlockSpec((T, LANES), lambda p, i: (i, p))],
        out_specs=[pl.BlockSpec((T, LANES), lambda p, i: (i, p)),
                   pl.BlockSpec((s, LANES), lambda p, i: (0, p)),
                   pl.BlockSpec((s, LANES), lambda p, i: (0, p))],
        out_shape=[jax.ShapeDtypeStruct((s, SB_W), BF16)] * 3,
        scratch_shapes=[pltpu.VMEM((2, T, LANES), F32), pltpu.VMEM((s, LANES), F32),
                        pltpu.VMEM((s, LANES), F32), pltpu.VMEM((2, T, 1), F32),
                        pltpu.VMEM((2, T, 1), F32), pltpu.VMEM((2, T, T), BF16),
                        pltpu.VMEM((2, T, T), BF16)],
        compiler_params=_ARB2, operands=[qkv, qkv, qkv, o, do]), rider)
    return dq, dk, dv, lands


def _fox_bwd(qkv, fqb, frow, fbounds, o, lse, do, name, rider=None):
    s = qkv.shape[0]
    nq = s // T

    def body(q_ref, k_ref, v_ref, fq_ref, fr_ref, fb_ref, o_ref, lse_ref, do_ref,
             dq_ref, dk_ref, dv_ref, df_ref, dqa_ref, dka_ref, dva_ref, dfa_ref, dls_ref, pbs_ref):
        i = pl.program_id(1)

        @pl.when(i == 0)
        def _():
            dka_ref[...] = jnp.zeros_like(dka_ref)
            dva_ref[...] = jnp.zeros_like(dva_ref)
            dfa_ref[...] = jnp.zeros_like(dfa_ref)

        m0 = _pair_masks()
        qh = _split_pair(q_ref[...] * jnp.asarray(Q_SCALE, BF16), m0)
        doh = _split_pair(do_ref[...], m0)
        dsum = _row_dots(do_ref, o_ref, m0)
        fq = fq_ref[...]
        fqh = (fq[:, 0:1], fq[:, HEAD_DIM:HEAD_DIM + 1])
        lse = lse_ref[...]
        lseh = (lse[:, 0:1], lse[:, HEAD_DIM:HEAD_DIM + 1])
        causal = _iota2((T, T), 0) >= _iota2((T, T), 1)
        dqa_ref[...] = jnp.zeros_like(dqa_ref)

        hs = range(2)

        def flush(j):
            off = pl.multiple_of(j * T, T)
            k = k_ref[pl.ds(off, T), :]
            for h in hs:
                dqa_ref[h] += _dot(dls_ref[h], k)
            dka_ref[pl.ds(off, T), :] += _dot_tn(dls_ref[0], qh[0]) + _dot_tn(dls_ref[1], qh[1])
            dva_ref[pl.ds(off, T), :] += _dot_tn(pbs_ref[0], doh[0]) + _dot_tn(pbs_ref[1], doh[1])

        def tile(j, diag):
            off = pl.multiple_of(j * T, T)
            k = k_ref[pl.ds(off, T), :]
            v = v_ref[pl.ds(off, T), :]
            sc = [_dot_nt(qh[h], k) + fqh[h] - fr_ref[h:h + 1, pl.ds(off, T)] for h in hs]
            dp = [_dot_nt(doh[h], v) for h in hs]
            if not diag:
                flush(j + 1)
            p = [jnp.exp(sc[h] - lseh[h]) for h in hs]
            if diag:
                p = [jnp.where(causal, p[h], 0.0) for h in hs]
            dl = [p[h] * (dp[h] - dsum[h]) for h in hs]
            for h in hs:
                dls_ref[h] = dl[h].astype(BF16)
                pbs_ref[h] = p[h].astype(BF16)
                dfa_ref[h:h + 1, pl.ds(off, T)] -= jnp.sum(dl[h], axis=0, keepdims=True)

        tile(i, True)
        j_left = _fox_tiles_left(i, pl.program_id(0), nq, fb_ref, tile)
        flush(j_left + 1)
        dq_ref[...] = (jnp.where(m0, dqa_ref[0], dqa_ref[1]) * Q_SCALE).astype(BF16)

        @pl.when(i == nq - 1)
        def _():
            dk_ref[...] = dka_ref[...].astype(BF16)
            dv_ref[...] = dva_ref[...].astype(BF16)
            df_ref[...] = dfa_ref[...]

    nb = FX_W // LANES
    base = 3 * SB_W // LANES
    (dq, dk, dv, df), lands = _ride(dict(
        body=body, name=name, grid=(nb, nq),
        in_specs=[pl.BlockSpec((T, LANES), lambda p, i: (i, base + p)),
                  pl.BlockSpec((s, LANES), lambda p, i: (0, base + nb + p)),
                  pl.BlockSpec((s, LANES), lambda p, i: (0, base + 2 * nb + p)),
                  pl.BlockSpec((T, LANES), lambda p, i: (i, p)),
                  pl.BlockSpec((None, 2, s), lambda p, i: (p, 0, 0)),
                  pl.BlockSpec(memory_space=pltpu.SMEM),
                  pl.BlockSpec((T, LANES), lambda p, i: (i, p)),
                  pl.BlockSpec((T, LANES), lambda p, i: (i, p)),
                  pl.BlockSpec((T, LANES), lambda p, i: (i, p))],
        out_specs=[pl.BlockSpec((T, LANES), lambda p, i: (i, p)),
                   pl.BlockSpec((s, LANES), lambda p, i: (0, p)),
                   pl.BlockSpec((s, LANES), lambda p, i: (0, p)),
                   pl.BlockSpec((None, 2, s), lambda p, i: (p, 0, 0))],
        out_shape=[jax.ShapeDtypeStruct((s, FX_W), BF16)] * 3
        + [jax.ShapeDtypeStruct((nb, 2, s), F32)],
        scratch_shapes=[pltpu.VMEM((2, T, LANES), F32), pltpu.VMEM((s, LANES), F32),
                        pltpu.VMEM((s, LANES), F32), pltpu.VMEM((2, s), F32),
                        pltpu.VMEM((2, T, T), BF16), pltpu.VMEM((2, T, T), BF16)],
        compiler_params=_ARB2, operands=[qkv, qkv, qkv, fqb, frow, fbounds, o, lse, do]), rider)
    return dq, dk, dv, df, lands


def _fox_prep_bwd(dfrow, gf, bpad, name):
    s = gf.shape[0]

    def body(df_ref, fl_ref, b_ref, dfl_ref, db_ref):
        tri = jnp.where(_iota2((T, T), 0) <= _iota2((T, T), 1), 1.0, 0.0).astype(BF16)
        carry = jnp.zeros((1, LANES), F32)
        db = jnp.zeros((1, LANES), F32)
        fill = jnp.zeros((LANES - FOX_HEADS, T), F32)
        for blk in reversed(range(s // T)):
            rows = slice(blk * T, (blk + 1) * T)
            c = _tri3(tri, jnp.concatenate([df_ref[:, rows], fill], axis=0), _dot_nt) + carry
            carry = c[0:1, :]
            dfl = c / (1.0 + jnp.exp(fl_ref[rows, :] + b_ref[...]))
            dfl_ref[rows, :] = dfl.astype(BF16)
            db = db + jnp.sum(dfl, axis=0, keepdims=True)
        db_ref[...] = db

    return pl.pallas_call(
        body, name=name, grid=(1,),
        in_specs=[pl.BlockSpec((FOX_HEADS, s), lambda i: (0, 0)),
                  pl.BlockSpec((s, LANES), lambda i: (0, MIX_W // LANES)),
                  pl.BlockSpec((1, LANES), lambda i: (0, 0))],
        out_specs=[pl.BlockSpec((s, LANES), lambda i: (0, 0)),
                   pl.BlockSpec((1, LANES), lambda i: (0, 0))],
        out_shape=[jax.ShapeDtypeStruct((s, LANES), BF16), jax.ShapeDtypeStruct((1, LANES), F32)],
        compiler_params=_ARB1,
    )(dfrow, gf, bpad)


def _mem_bwd(qkv, kv, o, lse, do, name):
    s = qkv.shape[0]
    n = kv.shape[0]

    def body(q_ref, k_ref, v_ref, o_ref, lse_ref, do_ref, dq_ref, dk_ref, dv_ref):
        @pl.when(pl.program_id(1) == 0)
        def _():
            dk_ref[...] = jnp.zeros_like(dk_ref)
            dv_ref[...] = jnp.zeros_like(dv_ref)

        m0 = _pair_masks()
        qh = _split_pair(q_ref[...] * jnp.asarray(Q_SCALE, BF16), m0)
        doh = _split_pair(do_ref[...], m0)
        dsum = _row_dots(do_ref, o_ref, m0)
        lse = lse_ref[...]
        lseh = (lse[:, 0:1], lse[:, HEAD_DIM:HEAD_DIM + 1])
        k = k_ref[...]
        v = v_ref[...]
        dqs = []
        for h in range(2):
            p = jnp.exp(_dot_nt(qh[h], k) - lseh[h])
            dl = p * (_dot_nt(doh[h], v) - dsum[h])
            dlb = dl.astype(BF16)
            dqs.append(_dot(dlb, k))
            dk_ref[...] += _dot_tn(dlb, qh[h])
            dv_ref[...] += _dot_tn(p.astype(BF16), doh[h])
        dq_ref[...] = (jnp.where(m0, dqs[0], dqs[1]) * Q_SCALE).astype(BF16)

    nb = MEM_W // LANES
    base = (3 * SB_W + 3 * FX_W) // LANES
    return pl.pallas_call(
        body, name=name, grid=(nb, s // TQM),
        in_specs=[pl.BlockSpec((TQM, LANES), lambda p, i: (i, base + p)),
                  pl.BlockSpec((n, LANES), lambda p, i: (0, p)),
                  pl.BlockSpec((n, LANES), lambda p, i: (0, nb + p)),
                  pl.BlockSpec((TQM, LANES), lambda p, i: (i, p)),
                  pl.BlockSpec((TQM, LANES), lambda p, i: (i, p)),
                  pl.BlockSpec((TQM, LANES), lambda p, i: (i, p))],
        out_specs=[pl.BlockSpec((TQM, LANES), lambda p, i: (i, p)),
                   pl.BlockSpec((n, LANES), lambda p, i: (0, p)),
                   pl.BlockSpec((n, LANES), lambda p, i: (0, p))],
        out_shape=[jax.ShapeDtypeStruct((s, MEM_W), BF16), jax.ShapeDtypeStruct((n, MEM_W), F32),
                   jax.ShapeDtypeStruct((n, MEM_W), F32)],
        compiler_params=_ARB2,
    )(qkv, kv, kv, o, lse, do)


def _memkv_bwd(mem, mnw, wkv, dk, dv, name):
    n = mem.shape[0]

    def body(mem_ref, mnw_ref, w_ref, dk_ref, dv_ref, dw_ref, dmnw_ref):
        mv = mem_ref[...]
        r = lax.rsqrt(jnp.mean(mv * mv, axis=-1, keepdims=True) + EPS)
        mh = mv * r
        hm = (mh * mnw_ref[...]).astype(BF16)
        dkv = jnp.concatenate([dk_ref[...], dv_ref[...]], axis=1).astype(BF16)
        dw_ref[...] = _dot_tn(hm, dkv)
        dhm = _dot_nt(dkv, w_ref[...])
        dmnw_ref[...] = jnp.sum(dhm * mh, axis=0, keepdims=True)

    return pl.pallas_call(
        body, name=name, grid=(1,),
        in_specs=[pl.BlockSpec((n, D_MODEL), lambda i: (0, 0)),
                  pl.BlockSpec((1, D_MODEL), lambda i: (0, 0)),
                  pl.BlockSpec((D_MODEL, 2 * MEM_W), lambda i: (0, 0)),
                  pl.BlockSpec((n, MEM_W), lambda i: (0, 0)),
                  pl.BlockSpec((n, MEM_W), lambda i: (0, 0))],
        out_specs=[pl.BlockSpec((D_MODEL, 2 * MEM_W), lambda i: (0, 0)),
                   pl.BlockSpec((1, D_MODEL), lambda i: (0, 0))],
        out_shape=[jax.ShapeDtypeStruct((D_MODEL, 2 * MEM_W), F32),
                   jax.ShapeDtypeStruct((1, D_MODEL), F32)],
        compiler_params=_ARB1,
    )(mem, mnw, wkv, dk, dv)


def _inproj_bwd_dx(pieces, w_r, x, nw, dxo, name):
    s = x.shape[0]
    n = len(pieces)
    widths = [p.shape[1] for p in pieces]

    def body(*refs):
        piece_refs = refs[:n]
        w_ref, x_ref, nw_ref, dxo_ref, dx_ref, h_ref, dnw_ref, dp_ref = refs[n:]

        @pl.when(pl.program_id(0) == 0)
        def _():
            dnw_ref[...] = jnp.zeros_like(dnw_ref)

        col = 0
        for r, wd in zip(piece_refs, widths):
            dp_ref[:, col:col + wd] = r[...]
            col += wd
        dp_ref[:, col:] = jnp.zeros((TM, WR_W - col), BF16)
        dh = _dot(dp_ref[...], w_ref[...])
        xv = x_ref[...]
        nw = nw_ref[...]
        r = lax.rsqrt(jnp.mean(xv * xv, axis=-1, keepdims=True) + EPS)
        xh = xv * r
        h_ref[...] = (xh * nw).astype(BF16)
        dnw_ref[...] += jnp.sum(dh * xh, axis=0, keepdims=True)
        dxh = dh * nw
        dx_ref[...] = r * (dxh - xh * jnp.mean(dxh * xh, axis=-1, keepdims=True)) + dxo_ref[...]

    return pl.pallas_call(
        body, name=name, grid=(s // TM,),
        in_specs=[pl.BlockSpec((TM, wd), lambda i: (i, 0)) for wd in widths]
        + [pl.BlockSpec((WR_W, D_MODEL), lambda i: (0, 0)),
           pl.BlockSpec((TM, D_MODEL), lambda i: (i, 0)),
           pl.BlockSpec((1, D_MODEL), lambda i: (0, 0)),
           pl.BlockSpec((TM, D_MODEL), lambda i: (i, 0))],
        out_specs=[pl.BlockSpec((TM, D_MODEL), lambda i: (i, 0)),
                   pl.BlockSpec((TM, D_MODEL), lambda i: (i, 0)),
                   pl.BlockSpec((1, D_MODEL), lambda i: (0, 0)),
                   pl.BlockSpec((TM, WR_W), lambda i: (i, 0))],
        out_shape=[jax.ShapeDtypeStruct((s, D_MODEL), F32), jax.ShapeDtypeStruct((s, D_MODEL), BF16),
                   jax.ShapeDtypeStruct((1, D_MODEL), F32), jax.ShapeDtypeStruct((s, WR_W), BF16)],
        compiler_params=_ARB1,
    )(*pieces, w_r, x, nw, dxo)


def _inproj_bwd_dw(h, dproj, name):
    s = dproj.shape[0]
    tn = 256

    def body(h_ref, dp_ref, dw_ref):
        dw_ref[...] = _dot_tn(dp_ref[...], h_ref[...])

    return pl.pallas_call(
        body, name=name, grid=(WR_W // tn,),
        in_specs=[pl.BlockSpec((s, D_MODEL), lambda j: (0, 0)),
                  pl.BlockSpec((s, tn), lambda j: (0, j))],
        out_specs=pl.BlockSpec((tn, D_MODEL), lambda j: (j, 0)),
        out_shape=jax.ShapeDtypeStruct((WR_W, D_MODEL), F32),
        compiler_params=_ARB1,
    )(h, dproj)


def _rearrange_w_in(wt):
    pad = jnp.zeros((FL_PAD - FOX_HEADS,) + wt.shape[1:], wt.dtype)
    return jnp.concatenate([wt[:3072], wt[3080:3336], wt[3336:IN_W], wt[3072:3080], pad], axis=0)


def _restore_w_in(g):
    gate0 = QKV_W
    fl0 = QKV_W + MIX_W
    return jnp.concatenate(
        [g[:3072], g[fl0:fl0 + FOX_HEADS], g[3072:QKV_W], g[gate0:fl0]], axis=0)


def _pad_lanes(v, width=LANES):
    return jnp.pad(v, (0, width - v.shape[0])).reshape(1, width)


def _layer_fwd(xs, mem, nw, w_r, b_forget, mnw, late, onw, l, travel=None):
    s = xs.shape[0]
    bpad = _pad_lanes(b_forget)
    qkv, gf = _inproj_fwd(xs, nw, w_r, f"inproj_fwd_{l}")
    fqb, frow, fbounds = _fox_prep_fwd(gf, qkv, bpad, f"fox_prep_fwd_{l}")
    frow = frow.reshape(FOX_HEADS // 2, 2, s)
    travel = _Travel(travel)
    ysb, _ = travel.ride(0, _sb_fwd, qkv, f"sb_fwd_{l}")
    yfx, lse_fx, _ = travel.ride(1, _fox_fwd, qkv, fqb, frow, fbounds, f"fox_fwd_{l}")
    wkv, wout = late(travel.lands)
    kv = _memkv_fwd(mem, mnw, wkv, f"memkv_fwd_{l}")
    ym, lse_m = _mem_fwd(qkv, kv, f"mem_fwd_{l}")
    xn = _outproj_fwd(ysb, yfx, ym, gf, onw, wout, xs, f"outproj_fwd_{l}")
    saved = (xs, nw, mnw, onw, bpad, qkv, gf, fqb, frow, fbounds, ysb, yfx, lse_fx, kv, ym, lse_m)
    return xn, saved, travel.lands, (wkv, wout)


class _Travel:
    def __init__(self, plan):
        self.plan = plan
        self.lands = None if plan is None else _new_lands(plan[0], plan[1])

    def ride(self, n, fn, *args):
        if self.plan is None or self.plan[2][n] is None:
            return fn(*args)
        srcs, scatter, legs = self.plan
        idx, rows = legs[n]
        out = fn(*args, rider=_Rider([srcs[a] for a in idx], [self.lands[a] for a in idx],
                                     scatter, rows))
        for a, land in zip(idx, out[-1]):
            self.lands[a] = land
        return out


def _layer_bwd(dx, saved, mem, w_r, wkv, wout, l, travel=None):
    xs, nw, mnw, onw, bpad, qkv, gf, fqb, frow, fbounds, ysb, yfx, lse_fx, kv, ym, lse_m = saved
    s = xs.shape[0]
    dysb, dyfx, dym, dgate, dwout, donw = _outproj_bwd(
        dx, wout, ysb, yfx, ym, gf, onw, f"outproj_bwd_{l}")
    travel = _Travel(None if travel is None else travel(dwout))
    sdq, sdk, sdv, _ = travel.ride(0, _sb_bwd, qkv, ysb, dysb, f"sb_bwd_{l}")
    fdq, fdk, fdv, dfrow, _ = travel.ride(1, _fox_bwd, qkv, fqb, frow, fbounds, yfx, lse_fx, dyfx,
                                          f"fox_bwd_{l}")
    dfl, db = _fox_prep_bwd(dfrow.reshape(FOX_HEADS, s), gf, bpad, f"fox_prep_bwd_{l}")
    dmq, dmk, dmv = _mem_bwd(qkv, kv, ym, lse_m, dym, f"mem_bwd_{l}")
    dwkv, dmnw = _memkv_bwd(mem, mnw, wkv, dmk, dmv, f"memkv_bwd_{l}")
    dx, ht, dnw, dproj = _inproj_bwd_dx([sdq, sdk, sdv, fdq, fdk, fdv, dmq, dgate, dfl],
                                        w_r, xs, nw, dx, f"inproj_bwd_dx_{l}")
    dwr = _inproj_bwd_dw(ht, dproj, f"inproj_bwd_dw_{l}")
    grads = dict(norm_w=dnw[0], w_r=dwr, b_forget=db[0, :FOX_HEADS], mem_norm_w=dmnw[0],
                 w_mem_kv=dwkv, out_norm_w=donw[0], w_out=dwout)
    return dx, grads, travel.lands


_ANY = pl.BlockSpec(memory_space=pl.ANY)


def _my_place():
    return lax.axis_index("x"), lax.axis_index("y"), lax.axis_index("c")


def _flip(v, bit):
    return 1 - v if bit else v


def _block_index(px, py, pc):
    return 4 * px + 2 * py + pc


def _all_gather_weights(shards, name):
    n = len(shards)

    def body(*refs):
        ins, outs = refs[:n], refs[n:2 * n]
        send_sems, recv_sems, local_sems = refs[2 * n:]
        x, y, c = _my_place()
        me = (x, y, c)
        sibling = (x, y, 1 - c)
        chips = [(1 - x, y), (x, 1 - y), (1 - x, 1 - y)]

        def copy(a, k, block, to, src=None):
            dst = outs[a].at[_block_index(*block)]
            return pltpu.make_async_remote_copy(
                src_ref=dst if src is None else src, dst_ref=dst,
                send_sem=send_sems.at[a, k], recv_sem=recv_sems.at[a, k],
                device_id=to, device_id_type=pl.DeviceIdType.MESH)

        mine = [pltpu.make_async_copy(ins[a], outs[a].at[_block_index(*me)], local_sems.at[a])
                for a in range(n)]
        for cp in mine:
            cp.start()
        first = []
        for a in range(n):
            first.append(copy(a, 0, me, sibling, src=ins[a]))
            first += [copy(a, 1 + j, me, (*chip, c), src=ins[a]) for j, chip in enumerate(chips)]
        for cp in first:
            cp.start()
        passed = []
        for j, chip in enumerate(chips):
            for a in range(n):
                copy(a, 1 + j, (*chip, c), me).wait_recv()
                fwd = copy(a, 4 + j, (*chip, c), sibling)
                fwd.start()
                passed.append(fwd)
        for a in range(n):
            copy(a, 0, sibling, me).wait_recv()
            for j, chip in enumerate(chips):
                copy(a, 4 + j, (*chip, 1 - c), me).wait_recv()
        for cp in first + passed:
            cp.wait_send()
        for cp in mine:
            cp.wait()

    return pl.pallas_call(
        body, name=name,
        in_specs=[_ANY] * n, out_specs=[_ANY] * n,
        out_shape=[jax.ShapeDtypeStruct((N_DEV,) + v.shape, v.dtype) for v in shards],
        scratch_shapes=[pltpu.SemaphoreType.DMA((n, 7)), pltpu.SemaphoreType.DMA((n, 7)),
                        pltpu.SemaphoreType.DMA((n,))],
    )(*shards)


def _exchange_blocks(blocked, name):
    n = len(blocked)

    def body(*refs):
        ins, outs = refs[:n], refs[n:2 * n]
        send_sems, recv_sems, local_sems = refs[2 * n:]
        x, y, c = _my_place()
        mine_idx = _block_index(x, y, c)
        local = [pltpu.make_async_copy(ins[a].at[mine_idx], outs[a].at[mine_idx], local_sems.at[a])
                 for a in range(n)]
        for cp in local:
            cp.start()
        sends, arrivals = [], []
        for r in range(1, N_DEV):
            peer = (_flip(x, r & 4), _flip(y, r & 2), _flip(c, r & 1))
            peer_idx = _block_index(*peer)
            for a in range(n):
                sems = dict(send_sem=send_sems.at[a, r - 1], recv_sem=recv_sems.at[a, r - 1],
                            device_id=peer, device_id_type=pl.DeviceIdType.MESH)
                sends.append(pltpu.make_async_remote_copy(
                    src_ref=ins[a].at[peer_idx], dst_ref=outs[a].at[mine_idx], **sems))
                arrivals.append(pltpu.make_async_remote_copy(
                    src_ref=ins[a].at[peer_idx], dst_ref=outs[a].at[peer_idx], **sems))
        for cp in sends:
            cp.start()
        for cp in arrivals:
            cp.wait_recv()
        for cp in sends:
            cp.wait_send()
        for cp in local:
            cp.wait()

    return pl.pallas_call(
        body, name=name,
        in_specs=[_ANY] * n, out_specs=[_ANY] * n,
        out_shape=[jax.ShapeDtypeStruct(v.shape, v.dtype) for v in blocked],
        scratch_shapes=[pltpu.SemaphoreType.DMA((n, 7)), pltpu.SemaphoreType.DMA((n, 7)),
                        pltpu.SemaphoreType.DMA((n,))],
    )(*blocked)


N_CHIP = N_DEV // 2


def _pair_swap(blocked, name):
    n = len(blocked)

    def body(*refs):
        ins, outs = refs[:n], refs[n:2 * n]
        send_sems, recv_sems = refs[2 * n:]
        x, y, c = _my_place()
        copies = [pltpu.make_async_remote_copy(
            src_ref=ins[a].at[j, 1 - c], dst_ref=outs[a].at[j],
            send_sem=send_sems.at[N_CHIP * a + j], recv_sem=recv_sems.at[N_CHIP * a + j],
            device_id=(x, y, 1 - c), device_id_type=pl.DeviceIdType.MESH)
            for a in range(n) for j in range(N_CHIP)]
        for cp in copies:
            cp.start()
        for cp in copies:
            cp.wait_recv()
        for cp in copies:
            cp.wait_send()

    return pl.pallas_call(
        body, name=name,
        in_specs=[_ANY] * n, out_specs=[_ANY] * n,
        out_shape=[jax.ShapeDtypeStruct((N_CHIP,) + v.shape[2:], v.dtype) for v in blocked],
        scratch_shapes=[pltpu.SemaphoreType.DMA((N_CHIP * n,)),
                        pltpu.SemaphoreType.DMA((N_CHIP * n,))],
    )(*blocked)


def _pair_add(mine, theirs, name):
    _, nrow, ncol = mine.shape

    def body(a_ref, b_ref, o_ref):
        o_ref[...] = (a_ref[...].astype(F32) + b_ref[...].astype(F32)).astype(BF16)

    blk = pl.BlockSpec((None, nrow, ncol), lambda j: (j, 0, 0))
    return pl.pallas_call(
        body, name=name, grid=(N_CHIP,), in_specs=[blk, blk], out_specs=blk,
        out_shape=jax.ShapeDtypeStruct(mine.shape, BF16), compiler_params=_ARB1,
    )(mine, theirs)


def _chip_exchange(by_chip, to_all, name):
    n, m = len(by_chip), len(to_all)

    def body(*refs):
        ins, alls = refs[:n], refs[n:n + m]
        outs, all_outs = refs[n + m:2 * n + m], refs[2 * n + m:2 * (n + m)]
        send_sems, recv_sems, local_sems = refs[2 * (n + m):]
        x, y, c = _my_place()
        my_chip, mine_idx = 2 * x + y, _block_index(x, y, c)
        local = [pltpu.make_async_copy(ins[a].at[my_chip], outs[a].at[my_chip], local_sems.at[a])
                 for a in range(n)]
        local += [pltpu.make_async_copy(alls[b].at[mine_idx], all_outs[b].at[mine_idx],
                                        local_sems.at[n + b]) for b in range(m)]
        for cp in local:
            cp.start()
        sends, arrivals = [], []
        k = 0
        for r in range(1, N_DEV):
            peer = (_flip(x, r & 4), _flip(y, r & 2), _flip(c, r & 1))
            peer_chip, peer_idx = 2 * peer[0] + peer[1], _block_index(*peer)
            pairs = [(alls[b].at[mine_idx], all_outs[b].at[mine_idx], all_outs[b].at[peer_idx])
                     for b in range(m)]
            if not r & 1:
                pairs += [(ins[a].at[peer_chip], outs[a].at[my_chip], outs[a].at[peer_chip])
                          for a in range(n)]
            for src, there, here in pairs:
                sems = dict(send_sem=send_sems.at[k], recv_sem=recv_sems.at[k], device_id=peer,
                            device_id_type=pl.DeviceIdType.MESH)
                sends.append(pltpu.make_async_remote_copy(src_ref=src, dst_ref=there, **sems))
                arrivals.append(pltpu.make_async_remote_copy(src_ref=src, dst_ref=here, **sems))
                k += 1
        for cp in sends:
            cp.start()
        for cp in arrivals:
            cp.wait_recv()
        for cp in sends:
            cp.wait_send()
        for cp in local:
            cp.wait()

    n_copies = 7 * m + 3 * n
    return pl.pallas_call(
        body, name=name,
        in_specs=[_ANY] * (n + m), out_specs=[_ANY] * (n + m),
        out_shape=[jax.ShapeDtypeStruct(v.shape, v.dtype) for v in by_chip + to_all],
        scratch_shapes=[pltpu.SemaphoreType.DMA((n_copies,)), pltpu.SemaphoreType.DMA((n_copies,)),
                        pltpu.SemaphoreType.DMA((n + m,))],
    )(*by_chip, *to_all)


class _Rider(NamedTuple):
    srcs: list
    lands: list
    scatter: bool
    part: list


def _new_lands(srcs, scatter):
    return [lax.empty(v.shape if scatter else (N_DEV,) + v.shape, v.dtype) for v in srcs]


def _rider_copies(srcs, lands, send_sems, recv_sems, rider):
    x, y, c = _my_place()
    mine_idx = _block_index(x, y, c)

    def window(ref, a):
        if rider.part[a] is None:
            return ref
        dim, start, size = rider.part[a]
        return ref.at[(slice(None),) * dim + (pl.ds(start, size),)]

    sends, arrivals = [], []
    for r in range(1, N_DEV):
        peer = (_flip(x, r & 4), _flip(y, r & 2), _flip(c, r & 1))
        peer_idx = _block_index(*peer)
        for a in range(len(srcs)):
            src = window(srcs[a].at[peer_idx] if rider.scatter else srcs[a], a)
            k = 7 * a + r - 1
            sems = dict(send_sem=send_sems.at[k], recv_sem=recv_sems.at[k],
                        device_id=peer, device_id_type=pl.DeviceIdType.MESH)
            sends.append(pltpu.make_async_remote_copy(
                src_ref=src, dst_ref=window(lands[a].at[mine_idx], a), **sems))
            arrivals.append(pltpu.make_async_remote_copy(
                src_ref=src, dst_ref=window(lands[a].at[peer_idx], a), **sems))
    return sends, arrivals


def _ride(call, rider):
    call = dict(call)
    body, grid = call.pop("body"), call["grid"]
    operands = call.pop("operands")
    if rider is None:
        return list(pl.pallas_call(body, **call)(*operands)), None
    n_in, n_out = len(call["in_specs"]), len(call["out_specs"])
    n_scratch = len(call["scratch_shapes"])
    m = len(rider.srcs)

    def riding(*refs):
        main_in, srcs, lands = refs[:n_in], refs[n_in:n_in + m], refs[n_in + m:n_in + 2 * m]
        main_out = refs[n_in + 2 * m:n_in + 2 * m + n_out]
        rest = refs[n_in + 3 * m + n_out:]
        send_sems, recv_sems = rest[n_scratch:]
        at = [pl.program_id(d) for d in range(len(grid))]
        first = functools.reduce(jnp.logical_and, [p == 0 for p in at])
        last = functools.reduce(jnp.logical_and, [p == g - 1 for p, g in zip(at, grid)])
        sends, arrivals = _rider_copies(srcs, lands, send_sems, recv_sems, rider)

        @pl.when(first)
        def _():
            for cp in sends:
                cp.start()

        body(*main_in, *main_out, *rest[:n_scratch])

        @pl.when(last)
        def _():
            for cp in arrivals:
                cp.wait_recv()
            for cp in sends:
                cp.wait_send()

    call["in_specs"] = list(call["in_specs"]) + [_ANY] * (2 * m)
    call["out_specs"] = list(call["out_specs"]) + [_ANY] * m
    call["out_shape"] = list(call["out_shape"]) + [
        jax.ShapeDtypeStruct(v.shape, v.dtype) for v in rider.lands]
    call["scratch_shapes"] = list(call["scratch_shapes"]) + [
        pltpu.SemaphoreType.DMA((7 * m,)), pltpu.SemaphoreType.DMA((7 * m,))]
    call["input_output_aliases"] = {n_in + m + a: n_out + a for a in range(m)}
    outs = pl.pallas_call(riding, **call)(*operands, *rider.srcs, *rider.lands)
    return list(outs[:n_out]), list(outs[n_out:])


def _sum_parts(p_ref):
    g = p_ref[0].astype(F32)
    for k in range(1, p_ref.shape[0]):
        g = g + p_ref[k].astype(F32)
    return g


def _adamw(g, w, m, v):
    c1 = 1.0 / (1.0 - ADAM_B1 ** ADAM_STEP)
    c2 = 1.0 / (1.0 - ADAM_B2 ** ADAM_STEP)
    nm = ADAM_B1 * m + (1.0 - ADAM_B1) * g
    nv = ADAM_B2 * v + (1.0 - ADAM_B2) * (g * g)
    return nm, nv, -ADAM_LR * ((nm * c1) / (jnp.sqrt(nv * c2) + ADAM_EPS) + ADAM_WD * w)


def _adamw_w_in(parts, w, m, v, name):
    ncol_blk, depth, nfeat = w.shape
    cols = 256

    def body(*refs):
        p_refs = refs[:depth]
        w_ref, m_ref, v_ref, g_ref, d_ref, nm_ref, nv_ref = refs[depth:]
        for l in range(depth):
            g = _sum_parts(p_refs[l])
            nm, nv, d = _adamw(g, w_ref[:, l, :], m_ref[:, l, :], v_ref[:, l, :])
            g_ref[:, l, :] = g
            nm_ref[:, l, :] = nm
            nv_ref[:, l, :] = nv
            d_ref[:, l, :] = d

    blk = pl.BlockSpec((ncol_blk, depth, cols), lambda j: (0, 0, j))
    return pl.pallas_call(
        body, name=name, grid=(nfeat // cols,),
        in_specs=[pl.BlockSpec((p.shape[0], ncol_blk, cols), lambda j: (0, 0, j)) for p in parts]
        + [blk] * 3,
        out_specs=[blk] * 4,
        out_shape=[jax.ShapeDtypeStruct(w.shape, F32)] * 4,
        compiler_params=_ARB1,
    )(*parts, w, m, v)


def _adamw_sum(parts, w, m, v, tile, name):
    depth, nrow, ncol = w.shape
    rows, cols = tile

    def body(*refs):
        p_refs = refs[:depth]
        w_ref, m_ref, v_ref, g_ref, d_ref, nm_ref, nv_ref = refs[depth:]
        layer = pl.program_id(0)
        for l in range(depth):
            @pl.when(layer == l)
            def _(p_ref=p_refs[l]):
                g = _sum_parts(p_ref)
                nm, nv, d = _adamw(g, w_ref[...], m_ref[...], v_ref[...])
                g_ref[...] = g
                nm_ref[...] = nm
                nv_ref[...] = nv
                d_ref[...] = d

    def part_spec(l):
        return pl.BlockSpec((parts[l].shape[0], rows, cols), lambda q, i, j: (
            0, jnp.where(q == l, i, 0), jnp.where(q == l, j, 0)))

    blk = pl.BlockSpec((None, rows, cols), lambda q, i, j: (q, i, j))
    return pl.pallas_call(
        body, name=name, grid=(depth, nrow // rows, ncol // cols),
        in_specs=[part_spec(l) for l in range(depth)] + [blk, blk, blk],
        out_specs=[blk] * 4,
        out_shape=[jax.ShapeDtypeStruct(w.shape, F32)] * 4,
        compiler_params=pltpu.CompilerParams(
            dimension_semantics=("arbitrary", "arbitrary", "arbitrary")),
    )(*parts, w, m, v)


def _pack_small(norm_w, mem_norm_w, out_norm_w, final_norm_w, b_forget):
    onw = jnp.pad(out_norm_w.reshape(20, LANES), ((0, 4), (0, 0)))
    b = jnp.pad(b_forget, ((0, 6), (0, LANES - FOX_HEADS)))
    return jnp.concatenate([norm_w.reshape(16, LANES), mem_norm_w.reshape(16, LANES), onw,
                            final_norm_w.reshape(8, LANES), b], axis=0)


def _unpack_small(p):
    return (p[0:16].reshape(2, D_MODEL), p[16:32].reshape(2, D_MODEL), p[32:52].reshape(2, MIX_W),
            p[56:64].reshape(D_MODEL), p[64:66, :FOX_HEADS])


def kernel(x, mem, norm_w, w_in, b_forget, mem_norm_w, w_mem_kv, out_norm_w, w_out, final_norm_w, loss_target, m_norm_w, m_w_in, m_b_forget, m_mem_norm_w, m_w_mem_kv, m_out_norm_w, m_w_out, m_final_norm_w, v_norm_w, v_w_in, v_b_forget, v_mem_norm_w, v_w_mem_kv, v_out_norm_w, v_w_out, v_final_norm_w):
    kv_rows = w_mem_kv.shape[1]
    out_rows = w_out.shape[1]
    me = _block_index(*_my_place())

    def shards(l):
        return [w_in[l].T.astype(BF16), w_mem_kv[l].astype(BF16), w_out[l].astype(BF16)]

    def full_in(g_in):
        return _rearrange_w_in(g_in.reshape(IN_W, D_MODEL))

    def full_kv_out(g_kv, g_out):
        return g_kv.reshape(D_MODEL, 2 * MEM_W), g_out.reshape(MIX_W, D_MODEL)

    def in_blocks(g):
        segments = [(0, 3072, 0), (3072, 3080, QKV_W + MIX_W), (3080, 3336, 3072),
                    (3336, IN_W, QKV_W)]

        def block(k):
            lo, hi = k * SHARD_W, (k + 1) * SHARD_W
            pieces = [g[at + max(lo, a) - a:at + min(hi, b) - a]
                      for a, b, at in segments if max(lo, a) < min(hi, b)]
            return jnp.concatenate(pieces, axis=0).astype(BF16)

        return jnp.stack([block(k) for k in range(N_DEV)])

    def kv_blocks(g):
        return g.reshape(N_DEV, kv_rows, 2 * MEM_W).astype(BF16)

    def out_blocks(g):
        return g.reshape(N_DEV, out_rows, D_MODEL).astype(BF16)

    def with_own(land, own):
        return lax.dynamic_update_slice(land, own[None], (me,) + (0,) * own.ndim)

    def with_own_of(land, blocked):
        return lax.dynamic_update_slice(land, lax.dynamic_slice_in_dim(blocked, me, 1, axis=0),
                                        (me,) + (0,) * (land.ndim - 1))

    def row(v):
        return v.reshape(1, -1)

    def cols(first, size):
        return (1, first, size)

    half = D_MODEL // 2

    g_in0, g_kv0, g_out0 = _all_gather_weights(shards(0), "all_gather_l0")
    w_r0 = full_in(g_in0)
    s_in1, s_kv1, s_out1 = shards(1)
    x1, saved0, (l_in1,), (wkv0, wout0) = _layer_fwd(
        x[0], mem[0], row(norm_w[0]), w_r0, b_forget[0], row(mem_norm_w[0]),
        lambda lands: full_kv_out(g_kv0, g_out0), row(out_norm_w[0]), 0,
        travel=([s_in1], False, [([0], [cols(0, half)]), ([0], [cols(half, half)])]))
    w_r1 = full_in(with_own(l_in1, s_in1))
    x2, saved1, _, (wkv1, wout1) = _layer_fwd(
        x1, mem[0], row(norm_w[1]), w_r1, b_forget[1], row(mem_norm_w[1]),
        lambda lands: full_kv_out(with_own(lands[0], s_kv1), with_own(lands[1], s_out1)),
        row(out_norm_w[1]), 1, travel=([s_kv1, s_out1], False, [([0, 1], [None, None]), None]))

    dx2, loss_part, dfnw = _final_fwd_bwd(x2, row(final_norm_w), loss_target[0], "final_fwd_bwd")

    dx1, gr1, (l_out1,) = _layer_bwd(
        dx2, saved1, mem[0], w_r1, wkv1, wout1, 1,
        travel=lambda dwout: ([out_blocks(dwout)], True, [([0], [None]), None]))
    p_in1, p_kv1 = in_blocks(gr1["w_r"]), kv_blocks(gr1["w_mem_kv"])
    split = 5 * LANES
    grad_x, gr0, (l_in1, l_kv1, l_out0) = _layer_bwd(
        dx1, saved0, mem[0], w_r0, wkv0, wout0, 0,
        travel=lambda dwout: ([p_in1, p_kv1, out_blocks(dwout)], True,
                              [([0, 1], [cols(0, split), None]),
                               ([0, 2], [cols(split, D_MODEL - split), None])]))
    r_out1 = with_own_of(l_out1, out_blocks(gr1["w_out"]))
    r_in1, r_kv1 = with_own_of(l_in1, p_in1), with_own_of(l_kv1, p_kv1)
    r_out0 = with_own_of(l_out0, out_blocks(gr0["w_out"]))

    def both(name):
        return jnp.stack([gr0[name], gr1[name]])

    small = _pack_small(both("norm_w"), both("mem_norm_w"), both("out_norm_w"), dfnw[0],
                        both("b_forget")).at[LOSS_ROW].set(loss_part[0])
    p_small = jnp.broadcast_to(small[None], (N_DEV, SMALL_ROWS, LANES))
    by_core = [v.reshape((N_CHIP, 2) + v.shape[1:])
               for v in (in_blocks(gr0["w_r"]), kv_blocks(gr0["w_mem_kv"]))]
    from_sibling = _pair_swap(by_core, "grads_l0_pair_swap")
    core = lax.axis_index("c")
    chip_sums = [_pair_add(lax.dynamic_index_in_dim(v, core, axis=1, keepdims=False), got,
                           f"grads_l0_pair_add_{a}")
                 for a, (v, got) in enumerate(zip(by_core, from_sibling))]
    r_in0, r_kv0, r_small = _chip_exchange(chip_sums, [p_small], "exchange_grads_l0")

    def view(v):
        return jnp.transpose(v, (2, 0, 1))

    g_w_in, d_w_in, nm_w_in, nv_w_in = [jnp.transpose(v, (1, 2, 0)) for v in _adamw_w_in(
        [r_in0, r_in1], view(w_in), view(m_w_in), view(v_w_in), "adamw_w_in")]
    g_w_kv, d_w_kv, nm_w_kv, nv_w_kv = _adamw_sum(
        [r_kv0, r_kv1], w_mem_kv, m_w_mem_kv, v_w_mem_kv, (kv_rows, 2 * MEM_W), "adamw_w_mem_kv")
    g_w_out, d_w_out, nm_w_out, nv_w_out = _adamw_sum(
        [r_out0, r_out1], w_out, m_w_out, v_w_out, (out_rows, D_MODEL), "adamw_w_out")
    w_small = _pack_small(norm_w, mem_norm_w, out_norm_w, final_norm_w, b_forget)[None]
    m_small = _pack_small(m_norm_w, m_mem_norm_w, m_out_norm_w, m_final_norm_w, m_b_forget)[None]
    v_small = _pack_small(v_norm_w, v_mem_norm_w, v_out_norm_w, v_final_norm_w, v_b_forget)[None]
    small_out = _adamw_sum([r_small], w_small, m_small, v_small, (SMALL_ROWS, LANES), "adamw_small")
    (g_nw, g_mnw, g_onw, g_fnw, g_b), (d_nw, d_mnw, d_onw, d_fnw, d_b), \
        (nm_nw, nm_mnw, nm_onw, nm_fnw, nm_b), (nv_nw, nv_mnw, nv_onw, nv_fnw, nv_b) = [
            _unpack_small(t[0]) for t in small_out]
    loss = small_out[0][0, LOSS_ROW, 0]

    return (loss, grad_x[None],
            g_nw, g_w_in, g_b, g_mnw, g_w_kv, g_onw, g_w_out, g_fnw,
            d_nw, d_w_in, d_b, d_mnw, d_w_kv, d_onw, d_w_out, d_fnw,
            nm_nw, nm_w_in, nm_b, nm_mnw, nm_w_kv, nm_onw, nm_w_out, nm_fnw,
            nv_nw, nv_w_in, nv_b, nv_mnw, nv_w_kv, nv_onw, nv_w_out, nv_fnw)
```

```python
import functools
from typing import NamedTuple

import jax
import jax.numpy as jnp
from jax import lax
from jax.experimental import pallas as pl
from jax.experimental.pallas import tpu as pltpu

F32 = jnp.float32
BF16 = jnp.bfloat16

N_DEV = 8
D_MODEL = 1024
HEAD_DIM = 64
LANES = 128
SB_W = 512
FX_W = 512
MEM_W = 256
MIX_W = 1280
FOX_HEADS = 8
IN_W = 4616
SHARD_W = IN_W // N_DEV
QKV_W = 3 * SB_W + 3 * FX_W + MEM_W
FL_PAD = 256
GF_W = MIX_W + FL_PAD
WR_W = QKV_W + GF_W
EPS = 1e-6
T = 256
TM = 256
TQM = 512
Q_SCALE = 0.125
NEG = -1e30
UNDERFLOW = -110.0
NORM_SLACK = 1.01

ADAM_LR = 0.001
ADAM_B1 = 0.9
ADAM_B2 = 0.999
ADAM_EPS = 1e-08
ADAM_WD = 0.01
ADAM_STEP = 10

SMALL_ROWS = 72
LOSS_ROW = 66

_NT = (((1,), (1,)), ((), ()))
_TN = (((0,), (0,)), ((), ()))

_ARB1 = pltpu.CompilerParams(dimension_semantics=("arbitrary",))
_ARB2 = pltpu.CompilerParams(dimension_semantics=("arbitrary", "arbitrary"))


def _dot(a, b):
    return jnp.dot(a, b, preferred_element_type=F32)


def _dot_nt(a, b):
    return lax.dot_general(a, b, _NT, preferred_element_type=F32)


def _dot_tn(a, b):
    return lax.dot_general(a, b, _TN, preferred_element_type=F32)


def _split2(x):
    hi = x.astype(BF16)
    lo = (x - hi.astype(F32)).astype(BF16)
    return hi, lo


def _stack2(u):
    return jnp.concatenate([u, u], axis=0)


def _cum2(x, u2):
    hi, lo = _split2(x)
    return _dot(jnp.concatenate([hi, lo], axis=1), u2)


def _tri3(tri, x, dot=None):
    dot = dot or _dot
    hi = x.astype(BF16)
    r1 = x - hi.astype(F32)
    mid = r1.astype(BF16)
    lo = (r1 - mid.astype(F32)).astype(BF16)
    return dot(tri, hi) + dot(tri, mid) + dot(tri, lo)


def _iota2(shape, dim):
    return lax.broadcasted_iota(jnp.int32, shape, dim)


def _head_block_diag():
    r = _iota2((LANES, LANES), 0) // HEAD_DIM
    c = _iota2((LANES, LANES), 1) // HEAD_DIM
    return _stack2(jnp.where(r == c, 1.0, 0.0).astype(BF16))


def _head_mean(x, bd):
    return _cum2(x, bd) * (1.0 / HEAD_DIM)


def _sigmoid(x):
    return 1.0 / (1.0 + jnp.exp(-x))


def _log_sigmoid(x):
    return jnp.minimum(x, 0.0) - jnp.log(1.0 + jnp.exp(-jnp.abs(x)))


def _running_top(r_ref):
    return jnp.max(jnp.maximum(r_ref[0], r_ref[1]))


def _fox_tiles_left(i, pair, nq, fb_ref, tile):
    def bound(j):
        b = []
        for h in range(2):
            head = 2 * pair + h
            b.append(2.0 * NORM_SLACK * fb_ref[2 * nq + i, head] * fb_ref[3 * nq, head]
                     + fb_ref[2 * i, head] - fb_ref[2 * j + 1, head])
        return jnp.maximum(b[0], b[1])

    def more(j):
        return jnp.logical_and(j >= 0, bound(jnp.maximum(j, 0)) > UNDERFLOW)

    def step(j):
        tile(j, False)
        return j - 1

    return lax.while_loop(more, step, i - 1)


def _pair_masks():
    lane = _iota2((1, LANES), 1)
    return lane < HEAD_DIM


def _split_pair(x, m0):
    zero = jnp.zeros_like(x)
    return jnp.where(m0, x, zero), jnp.where(m0, zero, x)


def _inproj_fwd(x, nw, w_r, name):
    s = x.shape[0]

    def body(x_ref, nw_ref, w_ref, qkv_ref, gf_ref):
        xv = x_ref[...]
        r = lax.rsqrt(jnp.mean(xv * xv, axis=-1, keepdims=True) + EPS)
        h = (xv * r * nw_ref[...]).astype(BF16)
        for c in range(0, QKV_W, 256):
            qkv_ref[:, c:c + 256] = _dot_nt(h, w_ref[c:c + 256, :]).astype(BF16)
        for c in range(0, GF_W, 256):
            gf_ref[:, c:c + 256] = _dot_nt(h, w_ref[QKV_W + c:QKV_W + c + 256, :])

    return pl.pallas_call(
        body, name=name, grid=(s // TM,),
        in_specs=[pl.BlockSpec((TM, D_MODEL), lambda i: (i, 0)),
                  pl.BlockSpec((1, D_MODEL), lambda i: (0, 0)),
                  pl.BlockSpec((WR_W, D_MODEL), lambda i: (0, 0))],
        out_specs=[pl.BlockSpec((TM, QKV_W), lambda i: (i, 0)),
                   pl.BlockSpec((TM, GF_W), lambda i: (i, 0))],
        out_shape=[jax.ShapeDtypeStruct((s, QKV_W), BF16), jax.ShapeDtypeStruct((s, GF_W), F32)],
        compiler_params=_ARB1,
    )(x, nw, w_r)


def _fox_prep_fwd(gf, qkv, bpad, name):
    s = gf.shape[0]
    nq = s // T
    nrow = -(-(3 * nq + 1) // 8) * 8

    def body(fl_ref, q_ref, k_ref, b_ref, fq_ref, fr_ref, fb_ref):
        tri = jnp.where(_iota2((T, T), 0) >= _iota2((T, T), 1), 1.0, 0.0).astype(BF16)
        m0 = _pair_masks()
        lane = _iota2((1, LANES), 1)
        norms = [jnp.zeros((1, LANES), F32) for _ in range(nq + 1)]
        same_head = (_iota2((LANES, LANES), 0) // HEAD_DIM) == (_iota2((LANES, LANES), 1) // HEAD_DIM)
        bd = jnp.where(same_head, 1.0, 0.0).astype(BF16)
        for p in range(FOX_HEADS // 2):
            cols = slice(p * LANES, (p + 1) * LANES)
            q = (q_ref[:, cols] * jnp.asarray(Q_SCALE, BF16)).astype(F32)
            k = k_ref[:, cols].astype(F32)
            qn = jnp.sqrt(_dot((q * q).astype(BF16), bd))
            kn = jnp.sqrt(_dot((k * k).astype(BF16), bd))
            tops = [jnp.max(qn[j * T:(j + 1) * T], axis=0, keepdims=True) for j in range(nq)]
            tops.append(jnp.max(kn, axis=0, keepdims=True))
            for h in range(2):
                at = h * HEAD_DIM
                norms = [jnp.where(lane == 2 * p + h, top[:, at:at + 1], row)
                         for top, row in zip(tops, norms)]
        for j in range(nq + 1):
            fb_ref[2 * nq + j:2 * nq + j + 1, :] = norms[j]
        fb_ref[3 * nq + 1:, :] = jnp.zeros((nrow - 3 * nq - 1, LANES), F32)
        carry = jnp.zeros((1, LANES), F32)
        for blk in range(s // T):
            rows = slice(blk * T, (blk + 1) * T)
            lf = _log_sigmoid(fl_ref[rows, :] + b_ref[...])
            c = _tri3(tri, lf) + carry
            carry = c[T - 1:T, :]
            for p in range(FOX_HEADS // 2):
                fq_ref[rows, p * LANES:(p + 1) * LANES] = jnp.where(
                    m0, c[:, 2 * p:2 * p + 1], c[:, 2 * p + 1:2 * p + 2])
            fr_ref[:, rows] = c.T[0:FOX_HEADS, :]
            fb_ref[2 * blk:2 * blk + 1, :] = c[0:1, :]
            fb_ref[2 * blk + 1:2 * blk + 2, :] = carry

    base = 3 * SB_W // FX_W
    return pl.pallas_call(
        body, name=name, grid=(1,),
        in_specs=[pl.BlockSpec((s, LANES), lambda i: (0, MIX_W // LANES)),
                  pl.BlockSpec((s, FX_W), lambda i: (0, base)),
                  pl.BlockSpec((s, FX_W), lambda i: (0, base + 1)),
                  pl.BlockSpec((1, LANES), lambda i: (0, 0))],
        out_specs=[pl.BlockSpec((s, FX_W), lambda i: (0, 0)),
                   pl.BlockSpec((FOX_HEADS, s), lambda i: (0, 0)),
                   pl.BlockSpec((nrow, LANES), lambda i: (0, 0))],
        out_shape=[jax.ShapeDtypeStruct((s, FX_W), F32), jax.ShapeDtypeStruct((FOX_HEADS, s), F32),
                   jax.ShapeDtypeStruct((nrow, LANES), F32)],
        compiler_params=_ARB1,
    )(gf, qkv, qkv, bpad)


def _sb_fwd(qkv, name, rider=None):
    s = qkv.shape[0]

    def body(q_ref, k_ref, v_ref, o_ref, acc_ref, r_ref, as_ref):
        i = pl.program_id(1)
        m0 = _pair_masks()
        qh = _split_pair(q_ref[...] * jnp.asarray(Q_SCALE, BF16), m0)
        strict = _iota2((T, T), 0) > _iota2((T, T), 1)
        u2 = _stack2(jnp.where(strict, 1.0, 0.0).astype(BF16))
        acc_ref[...] = jnp.zeros_like(acc_ref)
        r_ref[...] = jnp.zeros_like(r_ref)
        hs = range(2)

        def flush(j):
            v = v_ref[pl.ds(pl.multiple_of(j * T, T), T), :]
            for h in hs:
                acc_ref[h] += _dot(as_ref[h], v)

        def tile(j, diag):
            k = k_ref[pl.ds(pl.multiple_of(j * T, T), T), :]
            z = [_dot_nt(qh[h], k) for h in hs]
            if not diag:
                flush(j + 1)
            la = [jnp.minimum(z[h], 0.0) - jnp.log(1.0 + jnp.exp(-jnp.abs(z[h]))) for h in hs]
            lf = [la[h] - z[h] for h in hs]
            if diag:
                lf = [jnp.where(strict, lf[h], 0.0) for h in hs]
            cin = [_cum2(lf[h], u2) for h in hs]
            a = [jnp.exp(la[h] + cin[h] + r_ref[h]) for h in hs]
            if diag:
                a = [jnp.where(strict, a[h], 0.0) for h in hs]
            for h in hs:
                r_ref[h] += cin[h][:, 0:1] + lf[h][:, 0:1]
                as_ref[h] = a[h].astype(BF16)

        tile(i, True)

        def more(state):
            j, top = state
            return jnp.logical_and(j >= 0, top > UNDERFLOW)

        def step(state):
            j, _ = state
            tile(j, False)
            return j - 1, _running_top(r_ref)

        j_left, _ = lax.while_loop(more, step, (i - 1, _running_top(r_ref)))
        flush(j_left + 1)
        o_ref[...] = jnp.where(m0, acc_ref[0], acc_ref[1])

    nb = SB_W // LANES
    (ysb,), lands = _ride(dict(
        body=body, name=name, grid=(nb, s // T),
        in_specs=[pl.BlockSpec((T, LANES), lambda p, i: (i, p)),
                  pl.BlockSpec((s, LANES), lambda p, i: (0, nb + p)),
                  pl.BlockSpec((s, LANES), lambda p, i: (0, 2 * nb + p))],
        out_specs=[pl.BlockSpec((T, LANES), lambda p, i: (i, p))],
        out_shape=[jax.ShapeDtypeStruct((s, SB_W), F32)],
        scratch_shapes=[pltpu.VMEM((2, T, LANES), F32), pltpu.VMEM((2, T, 1), F32),
                        pltpu.VMEM((2, T, T), BF16)],
        compiler_params=_ARB2, operands=[qkv, qkv, qkv]), rider)
    return ysb, lands


def _fox_fwd(qkv, fqb, frow, fbounds, name, rider=None):
    s = qkv.shape[0]

    def body(q_ref, k_ref, v_ref, fq_ref, fr_ref, fb_ref, o_ref, lse_ref, acc_ref, m_ref, ps_ref):
        pair = pl.program_id(0)
        i = pl.program_id(1)
        m0 = _pair_masks()
        qh = _split_pair(q_ref[...] * jnp.asarray(Q_SCALE, BF16), m0)
        fq = fq_ref[...]
        fqh = (fq[:, 0:1], fq[:, HEAD_DIM:HEAD_DIM + 1])
        causal = _iota2((T, T), 0) >= _iota2((T, T), 1)
        ones = jnp.ones((T, LANES), BF16)
        acc_ref[...] = jnp.zeros_like(acc_ref)
        m_ref[...] = jnp.full_like(m_ref, NEG)
        hs = range(2)

        def flush(j):
            v = v_ref[pl.ds(pl.multiple_of(j * T, T), T), :]
            va2 = _stack2(jnp.concatenate([v, ones], axis=1))
            for h in hs:
                acc_ref[h] += _dot(ps_ref[h], va2)

        def tile(j, diag):
            off = pl.multiple_of(j * T, T)
            k = k_ref[pl.ds(off, T), :]
            sc = [_dot_nt(qh[h], k) + fqh[h] - fr_ref[h:h + 1, pl.ds(off, T)] for h in hs]
            if not diag:
                flush(j + 1)
            if diag:
                sc = [jnp.where(causal, sc[h], NEG) for h in hs]
            m_new = [jnp.maximum(m_ref[h], jnp.max(sc[h], axis=1, keepdims=True)) for h in hs]
            p = [jnp.exp(sc[h] - m_new[h]) for h in hs]
            for h in hs:
                acc_ref[h] = acc_ref[h] * jnp.exp(m_ref[h] - m_new[h])
                m_ref[h] = m_new[h]
                p_hi, p_lo = _split2(p[h])
                ps_ref[h] = jnp.concatenate([p_hi, p_lo], axis=1)

        tile(i, True)
        j_left = _fox_tiles_left(i, pair, s // T, fb_ref, tile)
        flush(j_left + 1)
        acc = (acc_ref[0], acc_ref[1])
        o_ref[...] = jnp.where(m0, acc[0][:, :LANES] / acc[0][:, LANES:],
                               acc[1][:, :LANES] / acc[1][:, LANES:])
        lse_ref[...] = jnp.where(m0, m_ref[0] + jnp.log(acc[0][:, LANES:]),
                                 m_ref[1] + jnp.log(acc[1][:, LANES:]))

    nb = FX_W // LANES
    base = 3 * SB_W // LANES
    (yfx, lse), lands = _ride(dict(
        body=body, name=name, grid=(nb, s // T),
        in_specs=[pl.BlockSpec((T, LANES), lambda p, i: (i, base + p)),
                  pl.BlockSpec((s, LANES), lambda p, i: (0, base + nb + p)),
                  pl.BlockSpec((s, LANES), lambda p, i: (0, base + 2 * nb + p)),
                  pl.BlockSpec((T, LANES), lambda p, i: (i, p)),
                  pl.BlockSpec((None, 2, s), lambda p, i: (p, 0, 0)),
                  pl.BlockSpec(memory_space=pltpu.SMEM)],
        out_specs=[pl.BlockSpec((T, LANES), lambda p, i: (i, p)),
                   pl.BlockSpec((T, LANES), lambda p, i: (i, p))],
        out_shape=[jax.ShapeDtypeStruct((s, FX_W), F32), jax.ShapeDtypeStruct((s, FX_W), F32)],
        scratch_shapes=[pltpu.VMEM((2, T, 2 * LANES), F32), pltpu.VMEM((2, T, 1), F32),
                        pltpu.VMEM((2, T, 2 * T), BF16)],
        compiler_params=_ARB2, operands=[qkv, qkv, qkv, fqb, frow, fbounds]), rider)
    return yfx, lse, lands


def _memkv_fwd(mem, mnw, wkv, name):
    n = mem.shape[0]

    def body(mem_ref, mnw_ref, w_ref, kv_ref):
        mv = mem_ref[...]
        r = lax.rsqrt(jnp.mean(mv * mv, axis=-1, keepdims=True) + EPS)
        hm = (mv * r * mnw_ref[...]).astype(BF16)
        kv_ref[...] = _dot(hm, w_ref[...]).astype(BF16)

    return pl.pallas_call(
        body, name=name, grid=(1,),
        in_specs=[pl.BlockSpec((n, D_MODEL), lambda i: (0, 0)),
                  pl.BlockSpec((1, D_MODEL), lambda i: (0, 0)),
                  pl.BlockSpec((D_MODEL, 2 * MEM_W), lambda i: (0, 0))],
        out_specs=pl.BlockSpec((n, 2 * MEM_W), lambda i: (0, 0)),
        out_shape=jax.ShapeDtypeStruct((n, 2 * MEM_W), BF16),
        compiler_params=_ARB1,
    )(mem, mnw, wkv)


def _mem_fwd(qkv, kv, name):
    s = qkv.shape[0]
    n = kv.shape[0]

    def body(q_ref, k_ref, v_ref, o_ref, lse_ref):
        m0 = _pair_masks()
        qh = _split_pair(q_ref[...] * jnp.asarray(Q_SCALE, BF16), m0)
        k = k_ref[...]
        v = v_ref[...]
        outs, lses = [], []
        for h in range(2):
            sc = _dot_nt(qh[h], k)
            mx = jnp.max(sc, axis=1, keepdims=True)
            p = jnp.exp(sc - mx)
            l = jnp.sum(p, axis=1, keepdims=True)
            outs.append(_dot(p.astype(BF16), v) / l)
            lses.append(mx + jnp.log(l))
        o_ref[...] = jnp.where(m0, outs[0], outs[1])
        lse_ref[...] = jnp.where(m0, lses[0], lses[1])

    nb = MEM_W // LANES
    base = (3 * SB_W + 3 * FX_W) // LANES
    return pl.pallas_call(
        body, name=name, grid=(nb, s // TQM),
        in_specs=[pl.BlockSpec((TQM, LANES), lambda p, i: (i, base + p)),
                  pl.BlockSpec((n, LANES), lambda p, i: (0, p)),
                  pl.BlockSpec((n, LANES), lambda p, i: (0, nb + p))],
        out_specs=[pl.BlockSpec((TQM, LANES), lambda p, i: (i, p)),
                   pl.BlockSpec((TQM, LANES), lambda p, i: (i, p))],
        out_shape=[jax.ShapeDtypeStruct((s, MEM_W), F32), jax.ShapeDtypeStruct((s, MEM_W), F32)],
        compiler_params=_ARB2,
    )(qkv, kv, kv)


def _mix_chunk(c, ysb_ref, yfx_ref, ym_ref):
    if c < SB_W // LANES:
        return ysb_ref[:, c * LANES:(c + 1) * LANES]
    c -= SB_W // LANES
    if c < FX_W // LANES:
        return yfx_ref[:, c * LANES:(c + 1) * LANES]
    c -= FX_W // LANES
    return ym_ref[:, c * LANES:(c + 1) * LANES]


def _outproj_fwd(ysb, yfx, ym, gf, onw, wout, x, name):
    s = x.shape[0]

    def body(ysb_ref, yfx_ref, ym_ref, g_ref, onw_ref, w_ref, x_ref, o_ref, yg_ref):
        bd = _head_block_diag()
        for c in range(MIX_W // LANES):
            sl = slice(c * LANES, (c + 1) * LANES)
            u = _mix_chunk(c, ysb_ref, yfx_ref, ym_ref)
            r = lax.rsqrt(_head_mean(u * u, bd) + EPS)
            g = g_ref[:, sl]
            yg_ref[:, sl] = (u * r * onw_ref[:, sl] * (g * _sigmoid(g))).astype(BF16)
        o_ref[...] = x_ref[...] + _dot(yg_ref[...], w_ref[...])

    return pl.pallas_call(
        body, name=name, grid=(s // TM,),
        in_specs=[pl.BlockSpec((TM, SB_W), lambda i: (i, 0)),
                  pl.BlockSpec((TM, FX_W), lambda i: (i, 0)),
                  pl.BlockSpec((TM, MEM_W), lambda i: (i, 0)),
                  pl.BlockSpec((TM, MIX_W), lambda i: (i, 0)),
                  pl.BlockSpec((1, MIX_W), lambda i: (0, 0)),
                  pl.BlockSpec((MIX_W, D_MODEL), lambda i: (0, 0)),
                  pl.BlockSpec((TM, D_MODEL), lambda i: (i, 0))],
        out_specs=pl.BlockSpec((TM, D_MODEL), lambda i: (i, 0)),
        out_shape=jax.ShapeDtypeStruct((s, D_MODEL), F32),
        scratch_shapes=[pltpu.VMEM((TM, MIX_W), BF16)],
        compiler_params=_ARB1,
    )(ysb, yfx, ym, gf, onw, wout, x)


def _final_fwd_bwd(x, fnw, target, name):
    s = x.shape[0]

    def body(x_ref, w_ref, t_ref, dx_ref, loss_ref, dw_ref):
        @pl.when(pl.program_id(0) == 0)
        def _():
            loss_ref[...] = jnp.zeros_like(loss_ref)
            dw_ref[...] = jnp.zeros_like(dw_ref)

        xv = x_ref[...]
        w = w_ref[...]
        r = lax.rsqrt(jnp.mean(xv * xv, axis=-1, keepdims=True) + EPS)
        xh = xv * r
        err = xh * w - t_ref[...]
        part = jnp.sum(jnp.sum(err * err, axis=1, keepdims=True), axis=0, keepdims=True)
        loss_ref[...] += part * (0.5 / D_MODEL)
        dy = err * (1.0 / D_MODEL)
        dw_ref[...] += jnp.sum(dy * xh, axis=0, keepdims=True)
        dxh = dy * w
        dx_ref[...] = r * (dxh - xh * jnp.mean(dxh * xh, axis=-1, keepdims=True))

    return pl.pallas_call(
        body, name=name, grid=(s // TM,),
        in_specs=[pl.BlockSpec((TM, D_MODEL), lambda i: (i, 0)),
                  pl.BlockSpec((1, D_MODEL), lambda i: (0, 0)),
                  pl.BlockSpec((TM, D_MODEL), lambda i: (i, 0))],
        out_specs=[pl.BlockSpec((TM, D_MODEL), lambda i: (i, 0)),
                   pl.BlockSpec((1, LANES), lambda i: (0, 0)),
                   pl.BlockSpec((1, D_MODEL), lambda i: (0, 0))],
        out_shape=[jax.ShapeDtypeStruct((s, D_MODEL), F32), jax.ShapeDtypeStruct((1, LANES), F32),
                   jax.ShapeDtypeStruct((1, D_MODEL), F32)],
        compiler_params=_ARB1,
    )(x, fnw, target)


def _outproj_bwd(dxo, wout, ysb, yfx, ym, gf, onw, name):
    s = dxo.shape[0]

    def body(dx_ref, w_ref, ysb_ref, yfx_ref, ym_ref, g_ref, onw_ref,
             dysb_ref, dyfx_ref, dym_ref, dg_ref, dw_ref, donw_ref, yg_ref):
        @pl.when(pl.program_id(0) == 0)
        def _():
            dw_ref[...] = jnp.zeros_like(dw_ref)
            donw_ref[...] = jnp.zeros_like(donw_ref)

        dxb = dx_ref[...].astype(BF16)
        dyg = _dot_nt(dxb, w_ref[...])
        bd = _head_block_diag()
        for c in range(MIX_W // LANES):
            sl = slice(c * LANES, (c + 1) * LANES)
            u = _mix_chunk(c, ysb_ref, yfx_ref, ym_ref)
            r = lax.rsqrt(_head_mean(u * u, bd) + EPS)
            yn = u * r
            g = g_ref[:, sl]
            sg = _sigmoid(g)
            sil = g * sg
            onw = onw_ref[:, sl]
            e = dyg[:, sl]
            yg_ref[:, sl] = (yn * onw * sil).astype(BF16)
            donw_ref[:, sl] += jnp.sum(e * yn * sil, axis=0, keepdims=True)
            dg_ref[:, sl] = (e * yn * onw * (sg * (1.0 + g * (1.0 - sg)))).astype(BF16)
            dyn = e * onw * sil
            du = (r * (dyn - yn * _head_mean(dyn * yn, bd))).astype(BF16)
            if c < 4:
                dysb_ref[:, c * LANES:(c + 1) * LANES] = du
            elif c < 8:
                dyfx_ref[:, (c - 4) * LANES:(c - 3) * LANES] = du
            else:
                dym_ref[:, (c - 8) * LANES:(c - 7) * LANES] = du
        dw_ref[...] += _dot_tn(yg_ref[...], dxb)

    return pl.pallas_call(
        body, name=name, grid=(s // TM,),
        in_specs=[pl.BlockSpec((TM, D_MODEL), lambda i: (i, 0)),
                  pl.BlockSpec((MIX_W, D_MODEL), lambda i: (0, 0)),
                  pl.BlockSpec((TM, SB_W), lambda i: (i, 0)),
                  pl.BlockSpec((TM, FX_W), lambda i: (i, 0)),
                  pl.BlockSpec((TM, MEM_W), lambda i: (i, 0)),
                  pl.BlockSpec((TM, MIX_W), lambda i: (i, 0)),
                  pl.BlockSpec((1, MIX_W), lambda i: (0, 0))],
        out_specs=[pl.BlockSpec((TM, SB_W), lambda i: (i, 0)),
                   pl.BlockSpec((TM, FX_W), lambda i: (i, 0)),
                   pl.BlockSpec((TM, MEM_W), lambda i: (i, 0)),
                   pl.BlockSpec((TM, MIX_W), lambda i: (i, 0)),
                   pl.BlockSpec((MIX_W, D_MODEL), lambda i: (0, 0)),
                   pl.BlockSpec((1, MIX_W), lambda i: (0, 0))],
        out_shape=[jax.ShapeDtypeStruct((s, SB_W), BF16), jax.ShapeDtypeStruct((s, FX_W), BF16),
                   jax.ShapeDtypeStruct((s, MEM_W), BF16), jax.ShapeDtypeStruct((s, MIX_W), BF16),
                   jax.ShapeDtypeStruct((MIX_W, D_MODEL), F32), jax.ShapeDtypeStruct((1, MIX_W), F32)],
        scratch_shapes=[pltpu.VMEM((TM, MIX_W), BF16)],
        compiler_params=_ARB1,
    )(dxo, wout, ysb, yfx, ym, gf, onw)


def _row_dots(do_ref, o_ref, m0):
    prod = do_ref[...].astype(F32) * o_ref[...]
    zero = jnp.zeros_like(prod)
    return (jnp.sum(jnp.where(m0, prod, zero), axis=1, keepdims=True),
            jnp.sum(jnp.where(m0, zero, prod), axis=1, keepdims=True))


def _sb_bwd(qkv, o, do, name, rider=None):
    s = qkv.shape[0]
    nq = s // T

    def body(q_ref, k_ref, v_ref, o_ref, do_ref, dq_ref, dk_ref, dv_ref,
             dqa_ref, dka_ref, dva_ref, rl_ref, rg_ref, dzs_ref, abs_ref):
        i = pl.program_id(1)

        @pl.when(i == 0)
        def _():
            dka_ref[...] = jnp.zeros_like(dka_ref)
            dva_ref[...] = jnp.zeros_like(dva_ref)

        m0 = _pair_masks()
        qh = _split_pair(q_ref[...] * jnp.asarray(Q_SCALE, BF16), m0)
        doh = _split_pair(do_ref[...], m0)
        dsum = _row_dots(do_ref, o_ref, m0)
        strict = _iota2((T, T), 0) > _iota2((T, T), 1)
        u2 = _stack2(jnp.where(strict, 1.0, 0.0).astype(BF16))
        dqa_ref[...] = jnp.zeros_like(dqa_ref)
        rl_ref[...] = jnp.zeros_like(rl_ref)
        rg_ref[...] = jnp.zeros_like(rg_ref)

        hs = range(2)

        def flush(j):
            off = pl.multiple_of(j * T, T)
            k = k_ref[pl.ds(off, T), :]
            for h in hs:
                dqa_ref[h] += _dot(dzs_ref[h], k)
            dka_ref[pl.ds(off, T), :] += _dot_tn(dzs_ref[0], qh[0]) + _dot_tn(dzs_ref[1], qh[1])
            dva_ref[pl.ds(off, T), :] += _dot_tn(abs_ref[0], doh[0]) + _dot_tn(abs_ref[1], doh[1])

        def tile(j, diag):
            off = pl.multiple_of(j * T, T)
            k = k_ref[pl.ds(off, T), :]
            v = v_ref[pl.ds(off, T), :]
            z = [_dot_nt(qh[h], k) for h in hs]
            da = [_dot_nt(doh[h], v) for h in hs]
            if not diag:
                flush(j + 1)
            la = [jnp.minimum(z[h], 0.0) - jnp.log(1.0 + jnp.exp(-jnp.abs(z[h]))) for h in hs]
            lf = [la[h] - z[h] for h in hs]
            if diag:
                lf = [jnp.where(strict, lf[h], 0.0) for h in hs]
            cin = [_cum2(lf[h], u2) for h in hs]
            a = [jnp.exp(la[h] + cin[h] + rl_ref[h]) for h in hs]
            if diag:
                a = [jnp.where(strict, a[h], 0.0) for h in hs]
            ab = [a[h].astype(BF16) for h in hs]
            g = [ab[h].astype(F32) * da[h] for h in hs]
            gin = [_cum2(g[h], u2) for h in hs]
            dz = [g[h] - jnp.exp(la[h]) * ((dsum[h] - rg_ref[h]) - gin[h]) for h in hs]
            if diag:
                dz = [jnp.where(strict, dz[h], 0.0) for h in hs]
            for h in hs:
                rl_ref[h] += cin[h][:, 0:1] + lf[h][:, 0:1]
                rg_ref[h] += gin[h][:, 0:1] + g[h][:, 0:1]
                dzs_ref[h] = dz[h].astype(BF16)
                abs_ref[h] = ab[h]

        tile(i, True)

        def more(state):
            j, top = state
            return jnp.logical_and(j >= 0, top > UNDERFLOW)

        def step(state):
            j, _ = state
            tile(j, False)
            return j - 1, _running_top(rl_ref)

        j_left, _ = lax.while_loop(more, step, (i - 1, _running_top(rl_ref)))
        flush(j_left + 1)
        dq_ref[...] = (jnp.where(m0, dqa_ref[0], dqa_ref[1]) * Q_SCALE).astype(BF16)

        @pl.when(i == nq - 1)
        def _():
            dk_ref[...] = dka_ref[...].astype(BF16)
            dv_ref[...] = dva_ref[...].astype(BF16)

    nb = SB_W // LANES
    (dq, dk, dv), lands = _ride(dict(
        body=body, name=name, grid=(nb, nq),
        in_specs=[pl.BlockSpec((T, LANES), lambda p, i: (i, p)),
                  pl.BlockSpec((s, LANES), lambda p, i: (0, nb + p)),
                  pl.BlockSpec((s, LANES), lambda p, i: (0, 2 * nb + p)),
                  pl.BlockSpec((T, LANES), lambda p, i: (i, p)),
                  pl.BlockSpec((T, LANES), lambda p, i: (i, p))],
        out_specs=[pl.BlockSpec((T, LANES), lambda p, i: (i, p)),
                   pl.BlockSpec((s, LANES), lambda p, i: (0, p)),
                   pl.BlockSpec((s, LANES), lambda p, i: (0, p))],
        out_shape=[jax.ShapeDtypeStruct((s, SB_W), BF16)] * 3,
        scratch_shapes=[pltpu.VMEM((2, T, LANES), F32), pltpu.VMEM((s, LANES), F32),
                        pltpu.VMEM((s, LANES), F32), pltpu.VMEM((2, T, 1), F32),
                        pltpu.VMEM((2, T, 1), F32), pltpu.VMEM((2, T, T), BF16),
                        pltpu.VMEM((2, T, T), BF16)],
        compiler_params=_ARB2, operands=[qkv, qkv, qkv, o, do]), rider)
    return dq, dk, dv, lands


def _fox_bwd(qkv, fqb, frow, fbounds, o, lse, do, name, rider=None):
    s = qkv.shape[0]
    nq = s // T

    def body(q_ref, k_ref, v_ref, fq_ref, fr_ref, fb_ref, o_ref, lse_ref, do_ref,
             dq_ref, dk_ref, dv_ref, df_ref, dqa_ref, dka_ref, dva_ref, dfa_ref, dls_ref, pbs_ref):
        i = pl.program_id(1)

        @pl.when(i == 0)
        def _():
            dka_ref[...] = jnp.zeros_like(dka_ref)
            dva_ref[...] = jnp.zeros_like(dva_ref)
            dfa_ref[...] = jnp.zeros_like(dfa_ref)

        m0 = _pair_masks()
        qh = _split_pair(q_ref[...] * jnp.asarray(Q_SCALE, BF16), m0)
        doh = _split_pair(do_ref[...], m0)
        dsum = _row_dots(do_ref, o_ref, m0)
        fq = fq_ref[...]
        fqh = (fq[:, 0:1], fq[:, HEAD_DIM:HEAD_DIM + 1])
        lse = lse_ref[...]
        lseh = (lse[:, 0:1], lse[:, HEAD_DIM:HEAD_DIM + 1])
        causal = _iota2((T, T), 0) >= _iota2((T, T), 1)
        dqa_ref[...] = jnp.zeros_like(dqa_ref)

        hs = range(2)

        def flush(j):
            off = pl.multiple_of(j * T, T)
            k = k_ref[pl.ds(off, T), :]
            for h in hs:
                dqa_ref[h] += _dot(dls_ref[h], k)
            dka_ref[pl.ds(off, T), :] += _dot_tn(dls_ref[0], qh[0]) + _dot_tn(dls_ref[1], qh[1])
            dva_ref[pl.ds(off, T), :] += _dot_tn(pbs_ref[0], doh[0]) + _dot_tn(pbs_ref[1], doh[1])

        def tile(j, diag):
            off = pl.multiple_of(j * T, T)
            k = k_ref[pl.ds(off, T), :]
            v = v_ref[pl.ds(off, T), :]
            sc = [_dot_nt(qh[h], k) + fqh[h] - fr_ref[h:h + 1, pl.ds(off, T)] for h in hs]
            dp = [_dot_nt(doh[h], v) for h in hs]
            if not diag:
                flush(j + 1)
            p = [jnp.exp(sc[h] - lseh[h]) for h in hs]
            if diag:
                p = [jnp.where(causal, p[h], 0.0) for h in hs]
            dl = [p[h] * (dp[h] - dsum[h]) for h in hs]
            for h in hs:
                dls_ref[h] = dl[h].astype(BF16)
                pbs_ref[h] = p[h].astype(BF16)
                dfa_ref[h:h + 1, pl.ds(off, T)] -= jnp.sum(dl[h], axis=0, keepdims=True)

        tile(i, True)
        j_left = _fox_tiles_left(i, pl.program_id(0), nq, fb_ref, tile)
        flush(j_left + 1)
        dq_ref[...] = (jnp.where(m0, dqa_ref[0], dqa_ref[1]) * Q_SCALE).astype(BF16)

        @pl.when(i == nq - 1)
        def _():
            dk_ref[...] = dka_ref[...].astype(BF16)
            dv_ref[...] = dva_ref[...].astype(BF16)
            df_ref[...] = dfa_ref[...]

    nb = FX_W // LANES
    base = 3 * SB_W // LANES
    (dq, dk, dv, df), lands = _ride(dict(
        body=body, name=name, grid=(nb, nq),
        in_specs=[pl.BlockSpec((T, LANES), lambda p, i: (i, base + p)),
                  pl.BlockSpec((s, LANES), lambda p, i: (0, base + nb + p)),
                  pl.BlockSpec((s, LANES), lambda p, i: (0, base + 2 * nb + p)),
                  pl.BlockSpec((T, LANES), lambda p, i: (i, p)),
                  pl.BlockSpec((None, 2, s), lambda p, i: (p, 0, 0)),
                  pl.BlockSpec(memory_space=pltpu.SMEM),
                  pl.BlockSpec((T, LANES), lambda p, i: (i, p)),
                  pl.BlockSpec((T, LANES), lambda p, i: (i, p)),
                  pl.BlockSpec((T, LANES), lambda p, i: (i, p))],
        out_specs=[pl.BlockSpec((T, LANES), lambda p, i: (i, p)),
                   pl.BlockSpec((s, LANES), lambda p, i: (0, p)),
                   pl.BlockSpec((s, LANES), lambda p, i: (0, p)),
                   pl.BlockSpec((None, 2, s), lambda p, i: (p, 0, 0))],
        out_shape=[jax.ShapeDtypeStruct((s, FX_W), BF16)] * 3
        + [jax.ShapeDtypeStruct((nb, 2, s), F32)],
        scratch_shapes=[pltpu.VMEM((2, T, LANES), F32), pltpu.VMEM((s, LANES), F32),
                        pltpu.VMEM((s, LANES), F32), pltpu.VMEM((2, s), F32),
                        pltpu.VMEM((2, T, T), BF16), pltpu.VMEM((2, T, T), BF16)],
        compiler_params=_ARB2, operands=[qkv, qkv, qkv, fqb, frow, fbounds, o, lse, do]), rider)
    return dq, dk, dv, df, lands


def _fox_prep_bwd(dfrow, gf, bpad, name):
    s = gf.shape[0]

    def body(df_ref, fl_ref, b_ref, dfl_ref, db_ref):
        tri = jnp.where(_iota2((T, T), 0) <= _iota2((T, T), 1), 1.0, 0.0).astype(BF16)
        carry = jnp.zeros((1, LANES), F32)
        db = jnp.zeros((1, LANES), F32)
        fill = jnp.zeros((LANES - FOX_HEADS, T), F32)
        for blk in reversed(range(s // T)):
            rows = slice(blk * T, (blk + 1) * T)
            c = _tri3(tri, jnp.concatenate([df_ref[:, rows], fill], axis=0), _dot_nt) + carry
            carry = c[0:1, :]
            dfl = c / (1.0 + jnp.exp(fl_ref[rows, :] + b_ref[...]))
            dfl_ref[rows, :] = dfl.astype(BF16)
            db = db + jnp.sum(dfl, axis=0, keepdims=True)
        db_ref[...] = db

    return pl.pallas_call(
        body, name=name, grid=(1,),
        in_specs=[pl.BlockSpec((FOX_HEADS, s), lambda i: (0, 0)),
                  pl.BlockSpec((s, LANES), lambda i: (0, MIX_W // LANES)),
                  pl.BlockSpec((1, LANES), lambda i: (0, 0))],
        out_specs=[pl.BlockSpec((s, LANES), lambda i: (0, 0)),
                   pl.BlockSpec((1, LANES), lambda i: (0, 0))],
        out_shape=[jax.ShapeDtypeStruct((s, LANES), BF16), jax.ShapeDtypeStruct((1, LANES), F32)],
        compiler_params=_ARB1,
    )(dfrow, gf, bpad)


def _mem_bwd(qkv, kv, o, lse, do, name):
    s = qkv.shape[0]
    n = kv.shape[0]

    def body(q_ref, k_ref, v_ref, o_ref, lse_ref, do_ref, dq_ref, dk_ref, dv_ref):
        @pl.when(pl.program_id(1) == 0)
        def _():
            dk_ref[...] = jnp.zeros_like(dk_ref)
            dv_ref[...] = jnp.zeros_like(dv_ref)

        m0 = _pair_masks()
        qh = _split_pair(q_ref[...] * jnp.asarray(Q_SCALE, BF16), m0)
        doh = _split_pair(do_ref[...], m0)
        dsum = _row_dots(do_ref, o_ref, m0)
        lse = lse_ref[...]
        lseh = (lse[:, 0:1], lse[:, HEAD_DIM:HEAD_DIM + 1])
        k = k_ref[...]
        v = v_ref[...]
        dqs = []
        for h in range(2):
            p = jnp.exp(_dot_nt(qh[h], k) - lseh[h])
            dl = p * (_dot_nt(doh[h], v) - dsum[h])
            dlb = dl.astype(BF16)
            dqs.append(_dot(dlb, k))
            dk_ref[...] += _dot_tn(dlb, qh[h])
            dv_ref[...] += _dot_tn(p.astype(BF16), doh[h])
        dq_ref[...] = (jnp.where(m0, dqs[0], dqs[1]) * Q_SCALE).astype(BF16)

    nb = MEM_W // LANES
    base = (3 * SB_W + 3 * FX_W) // LANES
    return pl.pallas_call(
        body, name=name, grid=(nb, s // TQM),
        in_specs=[pl.BlockSpec((TQM, LANES), lambda p, i: (i, base + p)),
                  pl.BlockSpec((n, LANES), lambda p, i: (0, p)),
                  pl.BlockSpec((n, LANES), lambda p, i: (0, nb + p)),
                  pl.BlockSpec((TQM, LANES), lambda p, i: (i, p)),
                  pl.BlockSpec((TQM, LANES), lambda p, i: (i, p)),
                  pl.BlockSpec((TQM, LANES), lambda p, i: (i, p))],
        out_specs=[pl.BlockSpec((TQM, LANES), lambda p, i: (i, p)),
                   pl.BlockSpec((n, LANES), lambda p, i: (0, p)),
                   pl.BlockSpec((n, LANES), lambda p, i: (0, p))],
        out_shape=[jax.ShapeDtypeStruct((s, MEM_W), BF16), jax.ShapeDtypeStruct((n, MEM_W), F32),
                   jax.ShapeDtypeStruct((n, MEM_W), F32)],
        compiler_params=_ARB2,
    )(qkv, kv, kv, o, lse, do)


def _memkv_bwd(mem, mnw, wkv, dk, dv, name):
    n = mem.shape[0]

    def body(mem_ref, mnw_ref, w_ref, dk_ref, dv_ref, dw_ref, dmnw_ref):
        mv = mem_ref[...]
        r = lax.rsqrt(jnp.mean(mv * mv, axis=-1, keepdims=True) + EPS)
        mh = mv * r
        hm = (mh * mnw_ref[...]).astype(BF16)
        dkv = jnp.concatenate([dk_ref[...], dv_ref[...]], axis=1).astype(BF16)
        dw_ref[...] = _dot_tn(hm, dkv)
        dhm = _dot_nt(dkv, w_ref[...])
        dmnw_ref[...] = jnp.sum(dhm * mh, axis=0, keepdims=True)

    return pl.pallas_call(
        body, name=name, grid=(1,),
        in_specs=[pl.BlockSpec((n, D_MODEL), lambda i: (0, 0)),
                  pl.BlockSpec((1, D_MODEL), lambda i: (0, 0)),
                  pl.BlockSpec((D_MODEL, 2 * MEM_W), lambda i: (0, 0)),
                  pl.BlockSpec((n, MEM_W), lambda i: (0, 0)),
                  pl.BlockSpec((n, MEM_W), lambda i: (0, 0))],
        out_specs=[pl.BlockSpec((D_MODEL, 2 * MEM_W), lambda i: (0, 0)),
                   pl.BlockSpec((1, D_MODEL), lambda i: (0, 0))],
        out_shape=[jax.ShapeDtypeStruct((D_MODEL, 2 * MEM_W), F32),
                   jax.ShapeDtypeStruct((1, D_MODEL), F32)],
        compiler_params=_ARB1,
    )(mem, mnw, wkv, dk, dv)


def _inproj_bwd_dx(pieces, w_r, x, nw, dxo, name):
    s = x.shape[0]
    n = len(pieces)
    widths = [p.shape[1] for p in pieces]

    def body(*refs):
        piece_refs = refs[:n]
        w_ref, x_ref, nw_ref, dxo_ref, dx_ref, h_ref, dnw_ref, dp_ref = refs[n:]

        @pl.when(pl.program_id(0) == 0)
        def _():
            dnw_ref[...] = jnp.zeros_like(dnw_ref)

        col = 0
        for r, wd in zip(piece_refs, widths):
            dp_ref[:, col:col + wd] = r[...]
            col += wd
        dp_ref[:, col:] = jnp.zeros((TM, WR_W - col), BF16)
        dh = _dot(dp_ref[...], w_ref[...])
        xv = x_ref[...]
        nw = nw_ref[...]
        r = lax.rsqrt(jnp.mean(xv * xv, axis=-1, keepdims=True) + EPS)
        xh = xv * r
        h_ref[...] = (xh * nw).astype(BF16)
        dnw_ref[...] += jnp.sum(dh * xh, axis=0, keepdims=True)
        dxh = dh * nw
        dx_ref[...] = r * (dxh - xh * jnp.mean(dxh * xh, axis=-1, keepdims=True)) + dxo_ref[...]

    return pl.pallas_call(
        body, name=name, grid=(s // TM,),
        in_specs=[pl.BlockSpec((TM, wd), lambda i: (i, 0)) for wd in widths]
        + [pl.BlockSpec((WR_W, D_MODEL), lambda i: (0, 0)),
           pl.BlockSpec((TM, D_MODEL), lambda i: (i, 0)),
           pl.BlockSpec((1, D_MODEL), lambda i: (0, 0)),
           pl.BlockSpec((TM, D_MODEL), lambda i: (i, 0))],
        out_specs=[pl.BlockSpec((TM, D_MODEL), lambda i: (i, 0)),
                   pl.BlockSpec((TM, D_MODEL), lambda i: (i, 0)),
                   pl.BlockSpec((1, D_MODEL), lambda i: (0, 0)),
                   pl.BlockSpec((TM, WR_W), lambda i: (i, 0))],
        out_shape=[jax.ShapeDtypeStruct((s, D_MODEL), F32), jax.ShapeDtypeStruct((s, D_MODEL), BF16),
                   jax.ShapeDtypeStruct((1, D_MODEL), F32), jax.ShapeDtypeStruct((s, WR_W), BF16)],
        compiler_params=_ARB1,
    )(*pieces, w_r, x, nw, dxo)


def _inproj_bwd_dw(h, dproj, name):
    s = dproj.shape[0]
    tn = 256

    def body(h_ref, dp_ref, dw_ref):
        dw_ref[...] = _dot_tn(dp_ref[...], h_ref[...])

    return pl.pallas_call(
        body, name=name, grid=(WR_W // tn,),
        in_specs=[pl.BlockSpec((s, D_MODEL), lambda j: (0, 0)),
                  pl.BlockSpec((s, tn), lambda j: (0, j))],
        out_specs=pl.BlockSpec((tn, D_MODEL), lambda j: (j, 0)),
        out_shape=jax.ShapeDtypeStruct((WR_W, D_MODEL), F32),
        compiler_params=_ARB1,
    )(h, dproj)


def _rearrange_w_in(wt):
    pad = jnp.zeros((FL_PAD - FOX_HEADS,) + wt.shape[1:], wt.dtype)
    return jnp.concatenate([wt[:3072], wt[3080:3336], wt[3336:IN_W], wt[3072:3080], pad], axis=0)


def _restore_w_in(g):
    gate0 = QKV_W
    fl0 = QKV_W + MIX_W
    return jnp.concatenate(
        [g[:3072], g[fl0:fl0 + FOX_HEADS], g[3072:QKV_W], g[gate0:fl0]], axis=0)


def _pad_lanes(v, width=LANES):
    return jnp.pad(v, (0, width - v.shape[0])).reshape(1, width)


def _layer_fwd(xs, mem, nw, w_r, b_forget, mnw, late, onw, l, travel=None):
    s = xs.shape[0]
    bpad = _pad_lanes(b_forget)
    qkv, gf = _inproj_fwd(xs, nw, w_r, f"inproj_fwd_{l}")
    fqb, frow, fbounds = _fox_prep_fwd(gf, qkv, bpad, f"fox_prep_fwd_{l}")
    frow = frow.reshape(FOX_HEADS // 2, 2, s)
    travel = _Travel(travel)
    ysb, _ = travel.ride(0, _sb_fwd, qkv, f"sb_fwd_{l}")
    yfx, lse_fx, _ = travel.ride(1, _fox_fwd, qkv, fqb, frow, fbounds, f"fox_fwd_{l}")
    wkv, wout = late(travel.lands)
    kv = _memkv_fwd(mem, mnw, wkv, f"memkv_fwd_{l}")
    ym, lse_m = _mem_fwd(qkv, kv, f"mem_fwd_{l}")
    xn = _outproj_fwd(ysb, yfx, ym, gf, onw, wout, xs, f"outproj_fwd_{l}")
    saved = (xs, nw, mnw, onw, bpad, qkv, gf, fqb, frow, fbounds, ysb, yfx, lse_fx, kv, ym, lse_m)
    return xn, saved, travel.lands, (wkv, wout)


class _Travel:
    def __init__(self, plan):
        self.plan = plan
        self.lands = None if plan is None else _new_lands(plan[0], plan[1])

    def ride(self, n, fn, *args):
        if self.plan is None or self.plan[2][n] is None:
            return fn(*args)
        srcs, scatter, legs = self.plan
        idx, rows = legs[n]
        out = fn(*args, rider=_Rider([srcs[a] for a in idx], [self.lands[a] for a in idx],
                                     scatter, rows))
        for a, land in zip(idx, out[-1]):
            self.lands[a] = land
        return out


def _layer_bwd(dx, saved, mem, w_r, wkv, wout, l, travel=None):
    xs, nw, mnw, onw, bpad, qkv, gf, fqb, frow, fbounds, ysb, yfx, lse_fx, kv, ym, lse_m = saved
    s = xs.shape[0]
    dysb, dyfx, dym, dgate, dwout, donw = _outproj_bwd(
        dx, wout, ysb, yfx, ym, gf, onw, f"outproj_bwd_{l}")
    travel = _Travel(None if travel is None else travel(dwout))
    sdq, sdk, sdv, _ = travel.ride(0, _sb_bwd, qkv, ysb, dysb, f"sb_bwd_{l}")
    fdq, fdk, fdv, dfrow, _ = travel.ride(1, _fox_bwd, qkv, fqb, frow, fbounds, yfx, lse_fx, dyfx,
                                          f"fox_bwd_{l}")
    dfl, db = _fox_prep_bwd(dfrow.reshape(FOX_HEADS, s), gf, bpad, f"fox_prep_bwd_{l}")
    dmq, dmk, dmv = _mem_bwd(qkv, kv, ym, lse_m, dym, f"mem_bwd_{l}")
    dwkv, dmnw = _memkv_bwd(mem, mnw, wkv, dmk, dmv, f"memkv_bwd_{l}")
    dx, ht, dnw, dproj = _inproj_bwd_dx([sdq, sdk, sdv, fdq, fdk, fdv, dmq, dgate, dfl],
                                        w_r, xs, nw, dx, f"inproj_bwd_dx_{l}")
    dwr = _inproj_bwd_dw(ht, dproj, f"inproj_bwd_dw_{l}")
    grads = dict(norm_w=dnw[0], w_r=dwr, b_forget=db[0, :FOX_HEADS], mem_norm_w=dmnw[0],
                 w_mem_kv=dwkv, out_norm_w=donw[0], w_out=dwout)
    return dx, grads, travel.lands


_ANY = pl.BlockSpec(memory_space=pl.ANY)


def _my_place():
    return lax.axis_index("x"), lax.axis_index("y"), lax.axis_index("c")


def _flip(v, bit):
    return 1 - v if bit else v


def _block_index(px, py, pc):
    return 4 * px + 2 * py + pc


def _all_gather_weights(shards, name):
    n = len(shards)

    def body(*refs):
        ins, outs = refs[:n], refs[n:2 * n]
        send_sems, recv_sems, local_sems = refs[2 * n:]
        x, y, c = _my_place()
        me = (x, y, c)
        sibling = (x, y, 1 - c)
        chips = [(1 - x, y), (x, 1 - y), (1 - x, 1 - y)]

        def copy(a, k, block, to, src=None):
            dst = outs[a].at[_block_index(*block)]
            return pltpu.make_async_remote_copy(
                src_ref=dst if src is None else src, dst_ref=dst,
                send_sem=send_sems.at[a, k], recv_sem=recv_sems.at[a, k],
                device_id=to, device_id_type=pl.DeviceIdType.MESH)

        mine = [pltpu.make_async_copy(ins[a], outs[a].at[_block_index(*me)], local_sems.at[a])
                for a in range(n)]
        for cp in mine:
            cp.start()
        first = []
        for a in range(n):
            first.append(copy(a, 0, me, sibling, src=ins[a]))
            first += [copy(a, 1 + j, me, (*chip, c), src=ins[a]) for j, chip in enumerate(chips)]
        for cp in first:
            cp.start()
        passed = []
        for j, chip in enumerate(chips):
            for a in range(n):
                copy(a, 1 + j, (*chip, c), me).wait_recv()
                fwd = copy(a, 4 + j, (*chip, c), sibling)
                fwd.start()
                passed.append(fwd)
        for a in range(n):
            copy(a, 0, sibling, me).wait_recv()
            for j, chip in enumerate(chips):
                copy(a, 4 + j, (*chip, 1 - c), me).wait_recv()
        for cp in first + passed:
            cp.wait_send()
        for cp in mine:
            cp.wait()

    return pl.pallas_call(
        body, name=name,
        in_specs=[_ANY] * n, out_specs=[_ANY] * n,
        out_shape=[jax.ShapeDtypeStruct((N_DEV,) + v.shape, v.dtype) for v in shards],
        scratch_shapes=[pltpu.SemaphoreType.DMA((n, 7)), pltpu.SemaphoreType.DMA((n, 7)),
                        pltpu.SemaphoreType.DMA((n,))],
    )(*shards)


def _exchange_blocks(blocked, name):
    n = len(blocked)

    def body(*refs):
        ins, outs = refs[:n], refs[n:2 * n]
        send_sems, recv_sems, local_sems = refs[2 * n:]
        x, y, c = _my_place()
        mine_idx = _block_index(x, y, c)
        local = [pltpu.make_async_copy(ins[a].at[mine_idx], outs[a].at[mine_idx], local_sems.at[a])
                 for a in range(n)]
        for cp in local:
            cp.start()
        sends, arrivals = [], []
        for r in range(1, N_DEV):
            peer = (_flip(x, r & 4), _flip(y, r & 2), _flip(c, r & 1))
            peer_idx = _block_index(*peer)
            for a in range(n):
                sems = dict(send_sem=send_sems.at[a, r - 1], recv_sem=recv_sems.at[a, r - 1],
                            device_id=peer, device_id_type=pl.DeviceIdType.MESH)
                sends.append(pltpu.make_async_remote_copy(
                    src_ref=ins[a].at[peer_idx], dst_ref=outs[a].at[mine_idx], **sems))
                arrivals.append(pltpu.make_async_remote_copy(
                    src_ref=ins[a].at[peer_idx], dst_ref=outs[a].at[peer_idx], **sems))
        for cp in sends:
            cp.start()
        for cp in arrivals:
            cp.wait_recv()
        for cp in sends:
            cp.wait_send()
        for cp in local:
            cp.wait()

    return pl.pallas_call(
        body, name=name,
        in_specs=[_ANY] * n, out_specs=[_ANY] * n,
        out_shape=[jax.ShapeDtypeStruct(v.shape, v.dtype) for v in blocked],
        scratch_shapes=[pltpu.SemaphoreType.DMA((n, 7)), pltpu.SemaphoreType.DMA((n, 7)),
                        pltpu.SemaphoreType.DMA((n,))],
    )(*blocked)


N_CHIP = N_DEV // 2


def _pair_swap(blocked, name):
    n = len(blocked)

    def body(*refs):
        ins, outs = refs[:n], refs[n:2 * n]
        send_sems, recv_sems = refs[2 * n:]
        x, y, c = _my_place()
        copies = [pltpu.make_async_remote_copy(
            src_ref=ins[a].at[j, 1 - c], dst_ref=outs[a].at[j],
            send_sem=send_sems.at[N_CHIP * a + j], recv_sem=recv_sems.at[N_CHIP * a + j],
            device_id=(x, y, 1 - c), device_id_type=pl.DeviceIdType.MESH)
            for a in range(n) for j in range(N_CHIP)]
        for cp in copies:
            cp.start()
        for cp in copies:
            cp.wait_recv()
        for cp in copies:
            cp.wait_send()

    return pl.pallas_call(
        body, name=name,
        in_specs=[_ANY] * n, out_specs=[_ANY] * n,
        out_shape=[jax.ShapeDtypeStruct((N_CHIP,) + v.shape[2:], v.dtype) for v in blocked],
        scratch_shapes=[pltpu.SemaphoreType.DMA((N_CHIP * n,)),
                        pltpu.SemaphoreType.DMA((N_CHIP * n,))],
    )(*blocked)


def _pair_add(mine, theirs, name):
    _, nrow, ncol = mine.shape

    def body(a_ref, b_ref, o_ref):
        o_ref[...] = (a_ref[...].astype(F32) + b_ref[...].astype(F32)).astype(BF16)

    blk = pl.BlockSpec((None, nrow, ncol), lambda j: (j, 0, 0))
    return pl.pallas_call(
        body, name=name, grid=(N_CHIP,), in_specs=[blk, blk], out_specs=blk,
        out_shape=jax.ShapeDtypeStruct(mine.shape, BF16), compiler_params=_ARB1,
    )(mine, theirs)


def _chip_exchange(by_chip, to_all, name):
    n, m = len(by_chip), len(to_all)

    def body(*refs):
        ins, alls = refs[:n], refs[n:n + m]
        outs, all_outs = refs[n + m:2 * n + m], refs[2 * n + m:2 * (n + m)]
        send_sems, recv_sems, local_sems = refs[2 * (n + m):]
        x, y, c = _my_place()
        my_chip, mine_idx = 2 * x + y, _block_index(x, y, c)
        local = [pltpu.make_async_copy(ins[a].at[my_chip], outs[a].at[my_chip], local_sems.at[a])
                 for a in range(n)]
        local += [pltpu.make_async_copy(alls[b].at[mine_idx], all_outs[b].at[mine_idx],
                                        local_sems.at[n + b]) for b in range(m)]
        for cp in local:
            cp.start()
        sends, arrivals = [], []
        k = 0
        for r in range(1, N_DEV):
            peer = (_flip(x, r & 4), _flip(y, r & 2), _flip(c, r & 1))
            peer_chip, peer_idx = 2 * peer[0] + peer[1], _block_index(*peer)
            pairs = [(alls[b].at[mine_idx], all_outs[b].at[mine_idx], all_outs[b].at[peer_idx])
                     for b in range(m)]
            if not r & 1:
                pairs += [(ins[a].at[peer_chip], outs[a].at[my_chip], outs[a].at[peer_chip])
                          for a in range(n)]
            for src, there, here in pairs:
                sems = dict(send_sem=send_sems.at[k], recv_sem=recv_sems.at[k], device_id=peer,
                            device_id_type=pl.DeviceIdType.MESH)
                sends.append(pltpu.make_async_remote_copy(src_ref=src, dst_ref=there, **sems))
                arrivals.append(pltpu.make_async_remote_copy(src_ref=src, dst_ref=here, **sems))
                k += 1
        for cp in sends:
            cp.start()
        for cp in arrivals:
            cp.wait_recv()
        for cp in sends:
            cp.wait_send()
        for cp in local:
            cp.wait()

    n_copies = 7 * m + 3 * n
    return pl.pallas_call(
        body, name=name,
        in_specs=[_ANY] * (n + m), out_specs=[_ANY] * (n + m),
        out_shape=[jax.ShapeDtypeStruct(v.shape, v.dtype) for v in by_chip + to_all],
        scratch_shapes=[pltpu.SemaphoreType.DMA((n_copies,)), pltpu.SemaphoreType.DMA((n_copies,)),
                        pltpu.SemaphoreType.DMA((n + m,))],
    )(*by_chip, *to_all)


class _Rider(NamedTuple):
    srcs: list
    lands: list
    scatter: bool
    part: list


def _new_lands(srcs, scatter):
    return [lax.empty(v.shape if scatter else (N_DEV,) + v.shape, v.dtype) for v in srcs]


def _rider_copies(srcs, lands, send_sems, recv_sems, rider):
    x, y, c = _my_place()
    mine_idx = _block_index(x, y, c)

    def window(ref, a):
        if rider.part[a] is None:
            return ref
        dim, start, size = rider.part[a]
        return ref.at[(slice(None),) * dim + (pl.ds(start, size),)]

    sends, arrivals = [], []
    for r in range(1, N_DEV):
        peer = (_flip(x, r & 4), _flip(y, r & 2), _flip(c, r & 1))
        peer_idx = _block_index(*peer)
        for a in range(len(srcs)):
            src = window(srcs[a].at[peer_idx] if rider.scatter else srcs[a], a)
            k = 7 * a + r - 1
            sems = dict(send_sem=send_sems.at[k], recv_sem=recv_sems.at[k],
                        device_id=peer, device_id_type=pl.DeviceIdType.MESH)
            sends.append(pltpu.make_async_remote_copy(
                src_ref=src, dst_ref=window(lands[a].at[mine_idx], a), **sems))
            arrivals.append(pltpu.make_async_remote_copy(
                src_ref=src, dst_ref=window(lands[a].at[peer_idx], a), **sems))
    return sends, arrivals


def _ride(call, rider):
    call = dict(call)
    body, grid = call.pop("body"), call["grid"]
    operands = call.pop("operands")
    if rider is None:
        return list(pl.pallas_call(body, **call)(*operands)), None
    n_in, n_out = len(call["in_specs"]), len(call["out_specs"])
    n_scratch = len(call["scratch_shapes"])
    m = len(rider.srcs)

    def riding(*refs):
        main_in, srcs, lands = refs[:n_in], refs[n_in:n_in + m], refs[n_in + m:n_in + 2 * m]
        main_out = refs[n_in + 2 * m:n_in + 2 * m + n_out]
        rest = refs[n_in + 3 * m + n_out:]
        send_sems, recv_sems = rest[n_scratch:]
        at = [pl.program_id(d) for d in range(len(grid))]
        first = functools.reduce(jnp.logical_and, [p == 0 for p in at])
        last = functools.reduce(jnp.logical_and, [p == g - 1 for p, g in zip(at, grid)])
        sends, arrivals = _rider_copies(srcs, lands, send_sems, recv_sems, rider)

        @pl.when(first)
        def _():
            for cp in sends:
                cp.start()

        body(*main_in, *main_out, *rest[:n_scratch])

        @pl.when(last)
        def _():
            for cp in arrivals:
                cp.wait_recv()
            for cp in sends:
                cp.wait_send()

    call["in_specs"] = list(call["in_specs"]) + [_ANY] * (2 * m)
    call["out_specs"] = list(call["out_specs"]) + [_ANY] * m
    call["out_shape"] = list(call["out_shape"]) + [
        jax.ShapeDtypeStruct(v.shape, v.dtype) for v in rider.lands]
    call["scratch_shapes"] = list(call["scratch_shapes"]) + [
        pltpu.SemaphoreType.DMA((7 * m,)), pltpu.SemaphoreType.DMA((7 * m,))]
    call["input_output_aliases"] = {n_in + m + a: n_out + a for a in range(m)}
    outs = pl.pallas_call(riding, **call)(*operands, *rider.srcs, *rider.lands)
    return list(outs[:n_out]), list(outs[n_out:])


def _sum_parts(p_ref):
    g = p_ref[0].astype(F32)
    for k in range(1, p_ref.shape[0]):
        g = g + p_ref[k].astype(F32)
    return g


def _adamw(g, w, m, v):
    c1 = 1.0 / (1.0 - ADAM_B1 ** ADAM_STEP)
    c2 = 1.0 / (1.0 - ADAM_B2 ** ADAM_STEP)
    nm = ADAM_B1 * m + (1.0 - ADAM_B1) * g
    nv = ADAM_B2 * v + (1.0 - ADAM_B2) * (g * g)
    return nm, nv, -ADAM_LR * ((nm * c1) / (jnp.sqrt(nv * c2) + ADAM_EPS) + ADAM_WD * w)


def _adamw_w_in(parts, w, m, v, name):
    ncol_blk, depth, nfeat = w.shape
    cols = 256

    def body(*refs):
        p_refs = refs[:depth]
        w_ref, m_ref, v_ref, g_ref, d_ref, nm_ref, nv_ref = refs[depth:]
        for l in range(depth):
            g = _sum_parts(p_refs[l])
            nm, nv, d = _adamw(g, w_ref[:, l, :], m_ref[:, l, :], v_ref[:, l, :])
            g_ref[:, l, :] = g
            nm_ref[:, l, :] = nm
            nv_ref[:, l, :] = nv
            d_ref[:, l, :] = d

    blk = pl.BlockSpec((ncol_blk, depth, cols), lambda j: (0, 0, j))
    return pl.pallas_call(
        body, name=name, grid=(nfeat // cols,),
        in_specs=[pl.BlockSpec((p.shape[0], ncol_blk, cols), lambda j: (0, 0, j)) for p in parts]
        + [blk] * 3,
        out_specs=[blk] * 4,
        out_shape=[jax.ShapeDtypeStruct(w.shape, F32)] * 4,
        compiler_params=_ARB1,
    )(*parts, w, m, v)


def _adamw_sum(parts, w, m, v, tile, name):
    depth, nrow, ncol = w.shape
    rows, cols = tile

    def body(*refs):
        p_refs = refs[:depth]
        w_ref, m_ref, v_ref, g_ref, d_ref, nm_ref, nv_ref = refs[depth:]
        layer = pl.program_id(0)
        for l in range(depth):
            @pl.when(layer == l)
            def _(p_ref=p_refs[l]):
                g = _sum_parts(p_ref)
                nm, nv, d = _adamw(g, w_ref[...], m_ref[...], v_ref[...])
                g_ref[...] = g
                nm_ref[...] = nm
                nv_ref[...] = nv
                d_ref[...] = d

    def part_spec(l):
        return pl.BlockSpec((parts[l].shape[0], rows, cols), lambda q, i, j: (
            0, jnp.where(q == l, i, 0), jnp.where(q == l, j, 0)))

    blk = pl.BlockSpec((None, rows, cols), lambda q, i, j: (q, i, j))
    return pl.pallas_call(
        body, name=name, grid=(depth, nrow // rows, ncol // cols),
        in_specs=[part_spec(l) for l in range(depth)] + [blk, blk, blk],
        out_specs=[blk] * 4,
        out_shape=[jax.ShapeDtypeStruct(w.shape, F32)] * 4,
        compiler_params=pltpu.CompilerParams(
            dimension_semantics=("arbitrary", "arbitrary", "arbitrary")),
    )(*parts, w, m, v)


def _pack_small(norm_w, mem_norm_w, out_norm_w, final_norm_w, b_forget):
    onw = jnp.pad(out_norm_w.reshape(20, LANES), ((0, 4), (0, 0)))
    b = jnp.pad(b_forget, ((0, 6), (0, LANES - FOX_HEADS)))
    return jnp.concatenate([norm_w.reshape(16, LANES), mem_norm_w.reshape(16, LANES), onw,
                            final_norm_w.reshape(8, LANES), b], axis=0)


def _unpack_small(p):
    return (p[0:16].reshape(2, D_MODEL), p[16:32].reshape(2, D_MODEL), p[32:52].reshape(2, MIX_W),
            p[56:64].reshape(D_MODEL), p[64:66, :FOX_HEADS])


def kernel(x, mem, norm_w, w_in, b_forget, mem_norm_w, w_mem_kv, out_norm_w, w_out, final_norm_w, loss_target, m_norm_w, m_w_in, m_b_forget, m_mem_norm_w, m_w_mem_kv, m_out_norm_w, m_w_out, m_final_norm_w, v_norm_w, v_w_in, v_b_forget, v_mem_norm_w, v_w_mem_kv, v_out_norm_w, v_w_out, v_final_norm_w):
    kv_rows = w_mem_kv.shape[1]
    out_rows = w_out.shape[1]
    me = _block_index(*_my_place())

    def shards(l):
        return [w_in[l].T.astype(BF16), w_mem_kv[l].astype(BF16), w_out[l].astype(BF16)]

    def full_in(g_in):
        return _rearrange_w_in(g_in.reshape(IN_W, D_MODEL))

    def full_kv_out(g_kv, g_out):
        return g_kv.reshape(D_MODEL, 2 * MEM_W), g_out.reshape(MIX_W, D_MODEL)

    def in_blocks(g):
        segments = [(0, 3072, 0), (3072, 3080, QKV_W + MIX_W), (3080, 3336, 3072),
                    (3336, IN_W, QKV_W)]

        def block(k):
            lo, hi = k * SHARD_W, (k + 1) * SHARD_W
            pieces = [g[at + max(lo, a) - a:at + min(hi, b) - a]
                      for a, b, at in segments if max(lo, a) < min(hi, b)]
            return jnp.concatenate(pieces, axis=0).astype(BF16)

        return jnp.stack([block(k) for k in range(N_DEV)])

    def kv_blocks(g):
        return g.reshape(N_DEV, kv_rows, 2 * MEM_W).astype(BF16)

    def out_blocks(g):
        return g.reshape(N_DEV, out_rows, D_MODEL).astype(BF16)

    def with_own(land, own):
        return lax.dynamic_update_slice(land, own[None], (me,) + (0,) * own.ndim)

    def with_own_of(land, blocked):
        return lax.dynamic_update_slice(land, lax.dynamic_slice_in_dim(blocked, me, 1, axis=0),
                                        (me,) + (0,) * (land.ndim - 1))

    def row(v):
        return v.reshape(1, -1)

    def cols(first, size):
        return (1, first, size)

    fwd_split, bwd_split = 5 * LANES, 6 * LANES

    g_in0, g_kv0, g_out0 = _all_gather_weights(shards(0), "all_gather_l0")
    w_r0 = full_in(g_in0)
    s_in1, s_kv1, s_out1 = shards(1)
    x1, saved0, (l_in1,), (wkv0, wout0) = _layer_fwd(
        x[0], mem[0], row(norm_w[0]), w_r0, b_forget[0], row(mem_norm_w[0]),
        lambda lands: full_kv_out(g_kv0, g_out0), row(out_norm_w[0]), 0,
        travel=([s_in1], False, [([0], [cols(0, fwd_split)]),
                                 ([0], [cols(fwd_split, D_MODEL - fwd_split)])]))
    w_r1 = full_in(with_own(l_in1, s_in1))
    x2, saved1, _, (wkv1, wout1) = _layer_fwd(
        x1, mem[0], row(norm_w[1]), w_r1, b_forget[1], row(mem_norm_w[1]),
        lambda lands: full_kv_out(with_own(lands[0], s_kv1), with_own(lands[1], s_out1)),
        row(out_norm_w[1]), 1, travel=([s_kv1, s_out1], False, [([0, 1], [None, None]), None]))

    dx2, loss_part, dfnw = _final_fwd_bwd(x2, row(final_norm_w), loss_target[0], "final_fwd_bwd")

    dx1, gr1, (l_out1,) = _layer_bwd(
        dx2, saved1, mem[0], w_r1, wkv1, wout1, 1,
        travel=lambda dwout: ([out_blocks(dwout)], True, [([0], [None]), None]))
    p_in1, p_kv1 = in_blocks(gr1["w_r"]), kv_blocks(gr1["w_mem_kv"])
    grad_x, gr0, (l_in1, l_kv1, l_out0) = _layer_bwd(
        dx1, saved0, mem[0], w_r0, wkv0, wout0, 0,
        travel=lambda dwout: ([p_in1, p_kv1, out_blocks(dwout)], True,
                              [([0, 1], [cols(0, bwd_split), None]),
                               ([0, 2], [cols(bwd_split, D_MODEL - bwd_split), None])]))
    r_out1 = with_own_of(l_out1, out_blocks(gr1["w_out"]))
    r_in1, r_kv1 = with_own_of(l_in1, p_in1), with_own_of(l_kv1, p_kv1)
    r_out0 = with_own_of(l_out0, out_blocks(gr0["w_out"]))

    def both(name):
        return jnp.stack([gr0[name], gr1[name]])

    small = _pack_small(both("norm_w"), both("mem_norm_w"), both("out_norm_w"), dfnw[0],
                        both("b_forget")).at[LOSS_ROW].set(loss_part[0])
    p_small = jnp.broadcast_to(small[None], (N_DEV, SMALL_ROWS, LANES))
    by_core = [v.reshape((N_CHIP, 2) + v.shape[1:])
               for v in (in_blocks(gr0["w_r"]), kv_blocks(gr0["w_mem_kv"]))]
    from_sibling = _pair_swap(by_core, "grads_l0_pair_swap")
    core = lax.axis_index("c")
    chip_sums = [_pair_add(lax.dynamic_index_in_dim(v, core, axis=1, keepdims=False), got,
                           f"grads_l0_pair_add_{a}")
                 for a, (v, got) in enumerate(zip(by_core, from_sibling))]
    r_in0, r_kv0, r_small = _chip_exchange(chip_sums, [p_small], "exchange_grads_l0")

    def view(v):
        return jnp.transpose(v, (2, 0, 1))

    g_w_in, d_w_in, nm_w_in, nv_w_in = [jnp.transpose(v, (1, 2, 0)) for v in _adamw_w_in(
        [r_in0, r_in1], view(w_in), view(m_w_in), view(v_w_in), "adamw_w_in")]
    g_w_kv, d_w_kv, nm_w_kv, nv_w_kv = _adamw_sum(
        [r_kv0, r_kv1], w_mem_kv, m_w_mem_kv, v_w_mem_kv, (kv_rows, 2 * MEM_W), "adamw_w_mem_kv")
    g_w_out, d_w_out, nm_w_out, nv_w_out = _adamw_sum(
        [r_out0, r_out1], w_out, m_w_out, v_w_out, (out_rows, D_MODEL), "adamw_w_out")
    w_small = _pack_small(norm_w, mem_norm_w, out_norm_w, final_norm_w, b_forget)[None]
    m_small = _pack_small(m_norm_w, m_mem_norm_w, m_out_norm_w, m_final_norm_w, m_b_forget)[None]
    v_small = _pack_small(v_norm_w, v_mem_norm_w, v_out_norm_w, v_final_norm_w, v_b_forget)[None]
    small_out = _adamw_sum([r_small], w_small, m_small, v_small, (SMALL_ROWS, LANES), "adamw_small")
    (g_nw, g_mnw, g_onw, g_fnw, g_b), (d_nw, d_mnw, d_onw, d_fnw, d_b), \
        (nm_nw, nm_mnw, nm_onw, nm_fnw, nm_b), (nv_nw, nv_mnw, nv_onw, nv_fnw, nv_b) = [
            _unpack_small(t[0]) for t in small_out]
    loss = small_out[0][0, LOSS_ROW, 0]

    return (loss, grad_x[None],
            g_nw, g_w_in, g_b, g_mnw, g_w_kv, g_onw, g_w_out, g_fnw,
            d_nw, d_w_in, d_b, d_mnw, d_w_kv, d_onw, d_w_out, d_fnw,
            nm_nw, nm_w_in, nm_b, nm_mnw, nm_w_kv, nm_onw, nm_w_out, nm_fnw,
            nv_nw, nv_w_in, nv_b, nv_mnw, nv_w_kv, nv_onw, nv_w_out, nv_fnw)
```

```python
import functools
from typing import NamedTuple

import jax
import jax.numpy as jnp
from jax import lax
from jax.experimental import pallas as pl
from jax.experimental.pallas import tpu as pltpu

F32 = jnp.float32
BF16 = jnp.bfloat16

N_DEV = 8
D_MODEL = 1024
HEAD_DIM = 64
LANES = 128
SB_W = 512
FX_W = 512
MEM_W = 256
MIX_W = 1280
FOX_HEADS = 8
IN_W = 4616
SHARD_W = IN_W // N_DEV
QKV_W = 3 * SB_W + 3 * FX_W + MEM_W
FL_PAD = 256
GF_W = MIX_W + FL_PAD
WR_W = QKV_W + GF_W
EPS = 1e-6
T = 256
TM = 256
TQM = 512
Q_SCALE = 0.125
NEG = -1e30
UNDERFLOW = -110.0
NORM_SLACK = 1.01

ADAM_LR = 0.001
ADAM_B1 = 0.9
ADAM_B2 = 0.999
ADAM_EPS = 1e-08
ADAM_WD = 0.01
ADAM_STEP = 10

SMALL_ROWS = 72
LOSS_ROW = 66

_NT = (((1,), (1,)), ((), ()))
_TN = (((0,), (0,)), ((), ()))

_ARB1 = pltpu.CompilerParams(dimension_semantics=("arbitrary",))
_ARB2 = pltpu.CompilerParams(dimension_semantics=("arbitrary", "arbitrary"))


def _dot(a, b):
    return jnp.dot(a, b, preferred_element_type=F32)


def _dot_nt(a, b):
    return lax.dot_general(a, b, _NT, preferred_element_type=F32)


def _dot_tn(a, b):
    return lax.dot_general(a, b, _TN, preferred_element_type=F32)


def _split2(x):
    hi = x.astype(BF16)
    lo = (x - hi.astype(F32)).astype(BF16)
    return hi, lo


def _stack2(u):
    return jnp.concatenate([u, u], axis=0)


def _cum2(x, u2):
    hi, lo = _split2(x)
    return _dot(jnp.concatenate([hi, lo], axis=1), u2)


def _tri3(tri, x, dot=None):
    dot = dot or _dot
    hi = x.astype(BF16)
    r1 = x - hi.astype(F32)
    mid = r1.astype(BF16)
    lo = (r1 - mid.astype(F32)).astype(BF16)
    return dot(tri, hi) + dot(tri, mid) + dot(tri, lo)


def _iota2(shape, dim):
    return lax.broadcasted_iota(jnp.int32, shape, dim)


def _head_block_diag():
    r = _iota2((LANES, LANES), 0) // HEAD_DIM
    c = _iota2((LANES, LANES), 1) // HEAD_DIM
    return _stack2(jnp.where(r == c, 1.0, 0.0).astype(BF16))


def _head_mean(x, bd):
    return _cum2(x, bd) * (1.0 / HEAD_DIM)


def _sigmoid(x):
    return 1.0 / (1.0 + jnp.exp(-x))


def _log_sigmoid(x):
    return jnp.minimum(x, 0.0) - jnp.log(1.0 + jnp.exp(-jnp.abs(x)))


def _running_top(r_ref):
    return jnp.max(jnp.maximum(r_ref[0], r_ref[1]))


def _fox_tiles_left(i, pair, nq, fb_ref, tile):
    def bound(j):
        b = []
        for h in range(2):
            head = 2 * pair + h
            b.append(2.0 * NORM_SLACK * fb_ref[2 * nq + i, head] * fb_ref[3 * nq, head]
                     + fb_ref[2 * i, head] - fb_ref[2 * j + 1, head])
        return jnp.maximum(b[0], b[1])

    def more(j):
        return jnp.logical_and(j >= 0, bound(jnp.maximum(j, 0)) > UNDERFLOW)

    def step(j):
        tile(j, False)
        return j - 1

    return lax.while_loop(more, step, i - 1)


def _pair_masks():
    lane = _iota2((1, LANES), 1)
    return lane < HEAD_DIM


def _split_pair(x, m0):
    zero = jnp.zeros_like(x)
    return jnp.where(m0, x, zero), jnp.where(m0, zero, x)


def _inproj_fwd(x, nw, w_r, name):
    s = x.shape[0]

    def body(x_ref, nw_ref, w_ref, qkv_ref, gf_ref):
        xv = x_ref[...]
        r = lax.rsqrt(jnp.mean(xv * xv, axis=-1, keepdims=True) + EPS)
        h = (xv * r * nw_ref[...]).astype(BF16)
        for c in range(0, QKV_W, 256):
            qkv_ref[:, c:c + 256] = _dot_nt(h, w_ref[c:c + 256, :]).astype(BF16)
        for c in range(0, GF_W, 256):
            gf_ref[:, c:c + 256] = _dot_nt(h, w_ref[QKV_W + c:QKV_W + c + 256, :])

    return pl.pallas_call(
        body, name=name, grid=(s // TM,),
        in_specs=[pl.BlockSpec((TM, D_MODEL), lambda i: (i, 0)),
                  pl.BlockSpec((1, D_MODEL), lambda i: (0, 0)),
                  pl.BlockSpec((WR_W, D_MODEL), lambda i: (0, 0))],
        out_specs=[pl.BlockSpec((TM, QKV_W), lambda i: (i, 0)),
                   pl.BlockSpec((TM, GF_W), lambda i: (i, 0))],
        out_shape=[jax.ShapeDtypeStruct((s, QKV_W), BF16), jax.ShapeDtypeStruct((s, GF_W), F32)],
        compiler_params=_ARB1,
    )(x, nw, w_r)


def _fox_prep_fwd(gf, qkv, bpad, name):
    s = gf.shape[0]
    nq = s // T
    nrow = -(-(3 * nq + 1) // 8) * 8

    def body(fl_ref, q_ref, k_ref, b_ref, fq_ref, fr_ref, fb_ref):
        tri = jnp.where(_iota2((T, T), 0) >= _iota2((T, T), 1), 1.0, 0.0).astype(BF16)
        m0 = _pair_masks()
        lane = _iota2((1, LANES), 1)
        norms = [jnp.zeros((1, LANES), F32) for _ in range(nq + 1)]
        same_head = (_iota2((LANES, LANES), 0) // HEAD_DIM) == (_iota2((LANES, LANES), 1) // HEAD_DIM)
        bd = jnp.where(same_head, 1.0, 0.0).astype(BF16)
        for p in range(FOX_HEADS // 2):
            cols = slice(p * LANES, (p + 1) * LANES)
            q = (q_ref[:, cols] * jnp.asarray(Q_SCALE, BF16)).astype(F32)
            k = k_ref[:, cols].astype(F32)
            qn = _dot((q * q).astype(BF16), bd)
            kn = _dot((k * k).astype(BF16), bd)
            tops = [jnp.max(qn[j * T:(j + 1) * T], axis=0, keepdims=True) for j in range(nq)]
            tops.append(jnp.max(kn, axis=0, keepdims=True))
            tops = [jnp.sqrt(top) for top in tops]
            for h in range(2):
                at = h * HEAD_DIM
                norms = [jnp.where(lane == 2 * p + h, top[:, at:at + 1], row)
                         for top, row in zip(tops, norms)]
        for j in range(nq + 1):
            fb_ref[2 * nq + j:2 * nq + j + 1, :] = norms[j]
        fb_ref[3 * nq + 1:, :] = jnp.zeros((nrow - 3 * nq - 1, LANES), F32)
        carry = jnp.zeros((1, LANES), F32)
        for blk in range(s // T):
            rows = slice(blk * T, (blk + 1) * T)
            lf = _log_sigmoid(fl_ref[rows, :] + b_ref[...])
            c = _tri3(tri, lf) + carry
            carry = c[T - 1:T, :]
            for p in range(FOX_HEADS // 2):
                fq_ref[rows, p * LANES:(p + 1) * LANES] = jnp.where(
                    m0, c[:, 2 * p:2 * p + 1], c[:, 2 * p + 1:2 * p + 2])
            fr_ref[:, rows] = c.T[0:FOX_HEADS, :]
            fb_ref[2 * blk:2 * blk + 1, :] = c[0:1, :]
            fb_ref[2 * blk + 1:2 * blk + 2, :] = carry

    base = 3 * SB_W // FX_W
    return pl.pallas_call(
        body, name=name, grid=(1,),
        in_specs=[pl.BlockSpec((s, LANES), lambda i: (0, MIX_W // LANES)),
                  pl.BlockSpec((s, FX_W), lambda i: (0, base)),
                  pl.BlockSpec((s, FX_W), lambda i: (0, base + 1)),
                  pl.BlockSpec((1, LANES), lambda i: (0, 0))],
        out_specs=[pl.BlockSpec((s, FX_W), lambda i: (0, 0)),
                   pl.BlockSpec((FOX_HEADS, s), lambda i: (0, 0)),
                   pl.BlockSpec((nrow, LANES), lambda i: (0, 0))],
        out_shape=[jax.ShapeDtypeStruct((s, FX_W), F32), jax.ShapeDtypeStruct((FOX_HEADS, s), F32),
                   jax.ShapeDtypeStruct((nrow, LANES), F32)],
        compiler_params=_ARB1,
    )(gf, qkv, qkv, bpad)


def _sb_fwd(qkv, name, rider=None):
    s = qkv.shape[0]

    def body(q_ref, k_ref, v_ref, o_ref, acc_ref, r_ref, as_ref):
        i = pl.program_id(1)
        m0 = _pair_masks()
        qh = _split_pair(q_ref[...] * jnp.asarray(Q_SCALE, BF16), m0)
        strict = _iota2((T, T), 0) > _iota2((T, T), 1)
        u2 = _stack2(jnp.where(strict, 1.0, 0.0).astype(BF16))
        acc_ref[...] = jnp.zeros_like(acc_ref)
        r_ref[...] = jnp.zeros_like(r_ref)
        hs = range(2)

        def flush(j):
            v = v_ref[pl.ds(pl.multiple_of(j * T, T), T), :]
            for h in hs:
                acc_ref[h] += _dot(as_ref[h], v)

        def tile(j, diag):
            k = k_ref[pl.ds(pl.multiple_of(j * T, T), T), :]
            z = [_dot_nt(qh[h], k) for h in hs]
            if not diag:
                flush(j + 1)
            la = [jnp.minimum(z[h], 0.0) - jnp.log(1.0 + jnp.exp(-jnp.abs(z[h]))) for h in hs]
            lf = [la[h] - z[h] for h in hs]
            if diag:
                lf = [jnp.where(strict, lf[h], 0.0) for h in hs]
            cin = [_cum2(lf[h], u2) for h in hs]
            a = [jnp.exp(la[h] + cin[h] + r_ref[h]) for h in hs]
            if diag:
                a = [jnp.where(strict, a[h], 0.0) for h in hs]
            for h in hs:
                r_ref[h] += cin[h][:, 0:1] + lf[h][:, 0:1]
                as_ref[h] = a[h].astype(BF16)

        tile(i, True)

        def more(state):
            j, top = state
            return jnp.logical_and(j >= 0, top > UNDERFLOW)

        def step(state):
            j, _ = state
            tile(j, False)
            return j - 1, _running_top(r_ref)

        j_left, _ = lax.while_loop(more, step, (i - 1, _running_top(r_ref)))
        flush(j_left + 1)
        o_ref[...] = jnp.where(m0, acc_ref[0], acc_ref[1])

    nb = SB_W // LANES
    (ysb,), lands = _ride(dict(
        body=body, name=name, grid=(nb, s // T),
        in_specs=[pl.BlockSpec((T, LANES), lambda p, i: (i, p)),
                  pl.BlockSpec((s, LANES), lambda p, i: (0, nb + p)),
                  pl.BlockSpec((s, LANES), lambda p, i: (0, 2 * nb + p))],
        out_specs=[pl.BlockSpec((T, LANES), lambda p, i: (i, p))],
        out_shape=[jax.ShapeDtypeStruct((s, SB_W), F32)],
        scratch_shapes=[pltpu.VMEM((2, T, LANES), F32), pltpu.VMEM((2, T, 1), F32),
                        pltpu.VMEM((2, T, T), BF16)],
        compiler_params=_ARB2, operands=[qkv, qkv, qkv]), rider)
    return ysb, lands


def _fox_fwd(qkv, fqb, frow, fbounds, name, rider=None):
    s = qkv.shape[0]

    def body(q_ref, k_ref, v_ref, fq_ref, fr_ref, fb_ref, o_ref, lse_ref, acc_ref, m_ref, ps_ref):
        pair = pl.program_id(0)
        i = pl.program_id(1)
        m0 = _pair_masks()
        qh = _split_pair(q_ref[...] * jnp.asarray(Q_SCALE, BF16), m0)
        fq = fq_ref[...]
        fqh = (fq[:, 0:1], fq[:, HEAD_DIM:HEAD_DIM + 1])
        causal = _iota2((T, T), 0) >= _iota2((T, T), 1)
        ones = jnp.ones((T, LANES), BF16)
        acc_ref[...] = jnp.zeros_like(acc_ref)
        m_ref[...] = jnp.full_like(m_ref, NEG)
        hs = range(2)

        def flush(j):
            v = v_ref[pl.ds(pl.multiple_of(j * T, T), T), :]
            va2 = _stack2(jnp.concatenate([v, ones], axis=1))
            for h in hs:
                acc_ref[h] += _dot(ps_ref[h], va2)

        def tile(j, diag):
            off = pl.multiple_of(j * T, T)
            k = k_ref[pl.ds(off, T), :]
            sc = [_dot_nt(qh[h], k) + fqh[h] - fr_ref[h:h + 1, pl.ds(off, T)] for h in hs]
            if not diag:
                flush(j + 1)
            if diag:
                sc = [jnp.where(causal, sc[h], NEG) for h in hs]
            m_new = [jnp.maximum(m_ref[h], jnp.max(sc[h], axis=1, keepdims=True)) for h in hs]
            p = [jnp.exp(sc[h] - m_new[h]) for h in hs]
            for h in hs:
                acc_ref[h] = acc_ref[h] * jnp.exp(m_ref[h] - m_new[h])
                m_ref[h] = m_new[h]
                p_hi, p_lo = _split2(p[h])
                ps_ref[h] = jnp.concatenate([p_hi, p_lo], axis=1)

        tile(i, True)
        j_left = _fox_tiles_left(i, pair, s // T, fb_ref, tile)
        flush(j_left + 1)
        acc = (acc_ref[0], acc_ref[1])
        o_ref[...] = jnp.where(m0, acc[0][:, :LANES] / acc[0][:, LANES:],
                               acc[1][:, :LANES] / acc[1][:, LANES:])
        lse_ref[...] = jnp.where(m0, m_ref[0] + jnp.log(acc[0][:, LANES:]),
                                 m_ref[1] + jnp.log(acc[1][:, LANES:]))

    nb = FX_W // LANES
    base = 3 * SB_W // LANES
    (yfx, lse), lands = _ride(dict(
        body=body, name=name, grid=(nb, s // T),
        in_specs=[pl.BlockSpec((T, LANES), lambda p, i: (i, base + p)),
                  pl.BlockSpec((s, LANES), lambda p, i: (0, base + nb + p)),
                  pl.BlockSpec((s, LANES), lambda p, i: (0, base + 2 * nb + p)),
                  pl.BlockSpec((T, LANES), lambda p, i: (i, p)),
                  pl.BlockSpec((None, 2, s), lambda p, i: (p, 0, 0)),
                  pl.BlockSpec(memory_space=pltpu.SMEM)],
        out_specs=[pl.BlockSpec((T, LANES), lambda p, i: (i, p)),
                   pl.BlockSpec((T, LANES), lambda p, i: (i, p))],
        out_shape=[jax.ShapeDtypeStruct((s, FX_W), F32), jax.ShapeDtypeStruct((s, FX_W), F32)],
        scratch_shapes=[pltpu.VMEM((2, T, 2 * LANES), F32), pltpu.VMEM((2, T, 1), F32),
                        pltpu.VMEM((2, T, 2 * T), BF16)],
        compiler_params=_ARB2, operands=[qkv, qkv, qkv, fqb, frow, fbounds]), rider)
    return yfx, lse, lands


def _memkv_fwd(mem, mnw, wkv, name):
    n = mem.shape[0]

    def body(mem_ref, mnw_ref, w_ref, kv_ref):
        mv = mem_ref[...]
        r = lax.rsqrt(jnp.mean(mv * mv, axis=-1, keepdims=True) + EPS)
        hm = (mv * r * mnw_ref[...]).astype(BF16)
        kv_ref[...] = _dot(hm, w_ref[...]).astype(BF16)

    return pl.pallas_call(
        body, name=name, grid=(1,),
        in_specs=[pl.BlockSpec((n, D_MODEL), lambda i: (0, 0)),
                  pl.BlockSpec((1, D_MODEL), lambda i: (0, 0)),
                  pl.BlockSpec((D_MODEL, 2 * MEM_W), lambda i: (0, 0))],
        out_specs=pl.BlockSpec((n, 2 * MEM_W), lambda i: (0, 0)),
        out_shape=jax.ShapeDtypeStruct((n, 2 * MEM_W), BF16),
        compiler_params=_ARB1,
    )(mem, mnw, wkv)


def _mem_fwd(qkv, kv, name):
    s = qkv.shape[0]
    n = kv.shape[0]

    def body(q_ref, k_ref, v_ref, o_ref, lse_ref):
        m0 = _pair_masks()
        qh = _split_pair(q_ref[...] * jnp.asarray(Q_SCALE, BF16), m0)
        k = k_ref[...]
        v = v_ref[...]
        outs, lses = [], []
        for h in range(2):
            sc = _dot_nt(qh[h], k)
            mx = jnp.max(sc, axis=1, keepdims=True)
            p = jnp.exp(sc - mx)
            l = jnp.sum(p, axis=1, keepdims=True)
            outs.append(_dot(p.astype(BF16), v) / l)
            lses.append(mx + jnp.log(l))
        o_ref[...] = jnp.where(m0, outs[0], outs[1])
        lse_ref[...] = jnp.where(m0, lses[0], lses[1])

    nb = MEM_W // LANES
    base = (3 * SB_W + 3 * FX_W) // LANES
    return pl.pallas_call(
        body, name=name, grid=(nb, s // TQM),
        in_specs=[pl.BlockSpec((TQM, LANES), lambda p, i: (i, base + p)),
                  pl.BlockSpec((n, LANES), lambda p, i: (0, p)),
                  pl.BlockSpec((n, LANES), lambda p, i: (0, nb + p))],
        out_specs=[pl.BlockSpec((TQM, LANES), lambda p, i: (i, p)),
                   pl.BlockSpec((TQM, LANES), lambda p, i: (i, p))],
        out_shape=[jax.ShapeDtypeStruct((s, MEM_W), F32), jax.ShapeDtypeStruct((s, MEM_W), F32)],
        compiler_params=_ARB2,
    )(qkv, kv, kv)


def _mix_chunk(c, ysb_ref, yfx_ref, ym_ref):
    if c < SB_W // LANES:
        return ysb_ref[:, c * LANES:(c + 1) * LANES]
    c -= SB_W // LANES
    if c < FX_W // LANES:
        return yfx_ref[:, c * LANES:(c + 1) * LANES]
    c -= FX_W // LANES
    return ym_ref[:, c * LANES:(c + 1) * LANES]


def _outproj_fwd(ysb, yfx, ym, gf, onw, wout, x, name):
    s = x.shape[0]

    def body(ysb_ref, yfx_ref, ym_ref, g_ref, onw_ref, w_ref, x_ref, o_ref, yg_ref):
        bd = _head_block_diag()
        for c in range(MIX_W // LANES):
            sl = slice(c * LANES, (c + 1) * LANES)
            u = _mix_chunk(c, ysb_ref, yfx_ref, ym_ref)
            r = lax.rsqrt(_head_mean(u * u, bd) + EPS)
            g = g_ref[:, sl]
            yg_ref[:, sl] = (u * r * onw_ref[:, sl] * (g * _sigmoid(g))).astype(BF16)
        o_ref[...] = x_ref[...] + _dot(yg_ref[...], w_ref[...])

    return pl.pallas_call(
        body, name=name, grid=(s // TM,),
        in_specs=[pl.BlockSpec((TM, SB_W), lambda i: (i, 0)),
                  pl.BlockSpec((TM, FX_W), lambda i: (i, 0)),
                  pl.BlockSpec((TM, MEM_W), lambda i: (i, 0)),
                  pl.BlockSpec((TM, MIX_W), lambda i: (i, 0)),
                  pl.BlockSpec((1, MIX_W), lambda i: (0, 0)),
                  pl.BlockSpec((MIX_W, D_MODEL), lambda i: (0, 0)),
                  pl.BlockSpec((TM, D_MODEL), lambda i: (i, 0))],
        out_specs=pl.BlockSpec((TM, D_MODEL), lambda i: (i, 0)),
        out_shape=jax.ShapeDtypeStruct((s, D_MODEL), F32),
        scratch_shapes=[pltpu.VMEM((TM, MIX_W), BF16)],
        compiler_params=_ARB1,
    )(ysb, yfx, ym, gf, onw, wout, x)


def _final_fwd_bwd(x, fnw, target, name):
    s = x.shape[0]

    def body(x_ref, w_ref, t_ref, dx_ref, loss_ref, dw_ref):
        @pl.when(pl.program_id(0) == 0)
        def _():
            loss_ref[...] = jnp.zeros_like(loss_ref)
            dw_ref[...] = jnp.zeros_like(dw_ref)

        xv = x_ref[...]
        w = w_ref[...]
        r = lax.rsqrt(jnp.mean(xv * xv, axis=-1, keepdims=True) + EPS)
        xh = xv * r
        err = xh * w - t_ref[...]
        part = jnp.sum(jnp.sum(err * err, axis=1, keepdims=True), axis=0, keepdims=True)
        loss_ref[...] += part * (0.5 / D_MODEL)
        dy = err * (1.0 / D_MODEL)
        dw_ref[...] += jnp.sum(dy * xh, axis=0, keepdims=True)
        dxh = dy * w
        dx_ref[...] = r * (dxh - xh * jnp.mean(dxh * xh, axis=-1, keepdims=True))

    return pl.pallas_call(
        body, name=name, grid=(s // TM,),
        in_specs=[pl.BlockSpec((TM, D_MODEL), lambda i: (i, 0)),
                  pl.BlockSpec((1, D_MODEL), lambda i: (0, 0)),
                  pl.BlockSpec((TM, D_MODEL), lambda i: (i, 0))],
        out_specs=[pl.BlockSpec((TM, D_MODEL), lambda i: (i, 0)),
                   pl.BlockSpec((1, LANES), lambda i: (0, 0)),
                   pl.BlockSpec((1, D_MODEL), lambda i: (0, 0))],
        out_shape=[jax.ShapeDtypeStruct((s, D_MODEL), F32), jax.ShapeDtypeStruct((1, LANES), F32),
                   jax.ShapeDtypeStruct((1, D_MODEL), F32)],
        compiler_params=_ARB1,
    )(x, fnw, target)


def _outproj_bwd(dxo, wout, ysb, yfx, ym, gf, onw, name):
    s = dxo.shape[0]

    def body(dx_ref, w_ref, ysb_ref, yfx_ref, ym_ref, g_ref, onw_ref,
             dysb_ref, dyfx_ref, dym_ref, dg_ref, dw_ref, donw_ref, yg_ref):
        @pl.when(pl.program_id(0) == 0)
        def _():
            dw_ref[...] = jnp.zeros_like(dw_ref)
            donw_ref[...] = jnp.zeros_like(donw_ref)

        dxb = dx_ref[...].astype(BF16)
        dyg = _dot_nt(dxb, w_ref[...])
        bd = _head_block_diag()
        for c in range(MIX_W // LANES):
            sl = slice(c * LANES, (c + 1) * LANES)
            u = _mix_chunk(c, ysb_ref, yfx_ref, ym_ref)
            r = lax.rsqrt(_head_mean(u * u, bd) + EPS)
            yn = u * r
            g = g_ref[:, sl]
            sg = _sigmoid(g)
            sil = g * sg
            onw = onw_ref[:, sl]
            e = dyg[:, sl]
            yg_ref[:, sl] = (yn * onw * sil).astype(BF16)
            donw_ref[:, sl] += jnp.sum(e * yn * sil, axis=0, keepdims=True)
            dg_ref[:, sl] = (e * yn * onw * (sg * (1.0 + g * (1.0 - sg)))).astype(BF16)
            dyn = e * onw * sil
            du = (r * (dyn - yn * _head_mean(dyn * yn, bd))).astype(BF16)
            if c < 4:
                dysb_ref[:, c * LANES:(c + 1) * LANES] = du
            elif c < 8:
                dyfx_ref[:, (c - 4) * LANES:(c - 3) * LANES] = du
            else:
                dym_ref[:, (c - 8) * LANES:(c - 7) * LANES] = du
        dw_ref[...] += _dot_tn(yg_ref[...], dxb)

    return pl.pallas_call(
        body, name=name, grid=(s // TM,),
        in_specs=[pl.BlockSpec((TM, D_MODEL), lambda i: (i, 0)),
                  pl.BlockSpec((MIX_W, D_MODEL), lambda i: (0, 0)),
                  pl.BlockSpec((TM, SB_W), lambda i: (i, 0)),
                  pl.BlockSpec((TM, FX_W), lambda i: (i, 0)),
                  pl.BlockSpec((TM, MEM_W), lambda i: (i, 0)),
                  pl.BlockSpec((TM, MIX_W), lambda i: (i, 0)),
                  pl.BlockSpec((1, MIX_W), lambda i: (0, 0))],
        out_specs=[pl.BlockSpec((TM, SB_W), lambda i: (i, 0)),
                   pl.BlockSpec((TM, FX_W), lambda i: (i, 0)),
                   pl.BlockSpec((TM, MEM_W), lambda i: (i, 0)),
                   pl.BlockSpec((TM, MIX_W), lambda i: (i, 0)),
                   pl.BlockSpec((MIX_W, D_MODEL), lambda i: (0, 0)),
                   pl.BlockSpec((1, MIX_W), lambda i: (0, 0))],
        out_shape=[jax.ShapeDtypeStruct((s, SB_W), BF16), jax.ShapeDtypeStruct((s, FX_W), BF16),
                   jax.ShapeDtypeStruct((s, MEM_W), BF16), jax.ShapeDtypeStruct((s, MIX_W), BF16),
                   jax.ShapeDtypeStruct((MIX_W, D_MODEL), F32), jax.ShapeDtypeStruct((1, MIX_W), F32)],
        scratch_shapes=[pltpu.VMEM((TM, MIX_W), BF16)],
        compiler_params=_ARB1,
    )(dxo, wout, ysb, yfx, ym, gf, onw)


def _row_dots(do_ref, o_ref, m0):
    prod = do_ref[...].astype(F32) * o_ref[...]
    zero = jnp.zeros_like(prod)
    return (jnp.sum(jnp.where(m0, prod, zero), axis=1, keepdims=True),
            jnp.sum(jnp.where(m0, zero, prod), axis=1, keepdims=True))


def _sb_bwd(qkv, o, do, name, rider=None):
    s = qkv.shape[0]
    nq = s // T

    def body(q_ref, k_ref, v_ref, o_ref, do_ref, dq_ref, dk_ref, dv_ref,
             dqa_ref, dka_ref, dva_ref, rl_ref, rg_ref, dzs_ref):
        i = pl.program_id(1)

        @pl.when(i == 0)
        def _():
            dka_ref[...] = jnp.zeros_like(dka_ref)
            dva_ref[...] = jnp.zeros_like(dva_ref)

        m0 = _pair_masks()
        qh = _split_pair(q_ref[...] * jnp.asarray(Q_SCALE, BF16), m0)
        doh = _split_pair(do_ref[...], m0)
        dsum = _row_dots(do_ref, o_ref, m0)
        strict = _iota2((T, T), 0) > _iota2((T, T), 1)
        u2 = _stack2(jnp.where(strict, 1.0, 0.0).astype(BF16))
        dqa_ref[...] = jnp.zeros_like(dqa_ref)
        rl_ref[...] = jnp.zeros_like(rl_ref)
        rg_ref[...] = jnp.zeros_like(rg_ref)

        hs = range(2)

        def flush(j):
            off = pl.multiple_of(j * T, T)
            k = k_ref[pl.ds(off, T), :]
            for h in hs:
                dqa_ref[h] += _dot(dzs_ref[h], k)
            dka_ref[pl.ds(off, T), :] += _dot_tn(dzs_ref[0], qh[0]) + _dot_tn(dzs_ref[1], qh[1])

        def tile(j, diag):
            off = pl.multiple_of(j * T, T)
            k = k_ref[pl.ds(off, T), :]
            v = v_ref[pl.ds(off, T), :]
            z = [_dot_nt(qh[h], k) for h in hs]
            da = [_dot_nt(doh[h], v) for h in hs]
            if not diag:
                flush(j + 1)
            la = [jnp.minimum(z[h], 0.0) - jnp.log(1.0 + jnp.exp(-jnp.abs(z[h]))) for h in hs]
            lf = [la[h] - z[h] for h in hs]
            if diag:
                lf = [jnp.where(strict, lf[h], 0.0) for h in hs]
            cin = [_cum2(lf[h], u2) for h in hs]
            a = [jnp.exp(la[h] + cin[h] + rl_ref[h]) for h in hs]
            if diag:
                a = [jnp.where(strict, a[h], 0.0) for h in hs]
            ab = [a[h].astype(BF16) for h in hs]
            dva_ref[pl.ds(off, T), :] += _dot_tn(ab[0], doh[0]) + _dot_tn(ab[1], doh[1])
            g = [ab[h].astype(F32) * da[h] for h in hs]
            gin = [_cum2(g[h], u2) for h in hs]
            dz = [g[h] - jnp.exp(la[h]) * ((dsum[h] - rg_ref[h]) - gin[h]) for h in hs]
            if diag:
                dz = [jnp.where(strict, dz[h], 0.0) for h in hs]
            for h in hs:
                rl_ref[h] += cin[h][:, 0:1] + lf[h][:, 0:1]
                rg_ref[h] += gin[h][:, 0:1] + g[h][:, 0:1]
                dzs_ref[h] = dz[h].astype(BF16)

        tile(i, True)

        def more(state):
            j, top = state
            return jnp.logical_and(j >= 0, top > UNDERFLOW)

        def step(state):
            j, _ = state
            tile(j, False)
            return j - 1, _running_top(rl_ref)

        j_left, _ = lax.while_loop(more, step, (i - 1, _running_top(rl_ref)))
        flush(j_left + 1)
        dq_ref[...] = (jnp.where(m0, dqa_ref[0], dqa_ref[1]) * Q_SCALE).astype(BF16)

        @pl.when(i == nq - 1)
        def _():
            dk_ref[...] = dka_ref[...].astype(BF16)
            dv_ref[...] = dva_ref[...].astype(BF16)

    nb = SB_W // LANES
    (dq, dk, dv), lands = _ride(dict(
        body=body, name=name, grid=(nb, nq),
        in_specs=[pl.BlockSpec((T, LANES), lambda p, i: (i, p)),
                  pl.BlockSpec((s, LANES), lambda p, i: (0, nb + p)),
                  pl.BlockSpec((s, LANES), lambda p, i: (0, 2 * nb + p)),
                  pl.BlockSpec((T, LANES), lambda p, i: (i, p)),
                  pl.BlockSpec((T, LANES), lambda p, i: (i, p))],
        out_specs=[pl.BlockSpec((T, LANES), lambda p, i: (i, p)),
                   pl.BlockSpec((s, LANES), lambda p, i: (0, p)),
                   pl.BlockSpec((s, LANES), lambda p, i: (0, p))],
        out_shape=[jax.ShapeDtypeStruct((s, SB_W), BF16)] * 3,
        scratch_shapes=[pltpu.VMEM((2, T, LANES), F32), pltpu.VMEM((s, LANES), F32),
                        pltpu.VMEM((s, LANES), F32), pltpu.VMEM((2, T, 1), F32),
                        pltpu.VMEM((2, T, 1), F32), pltpu.VMEM((2, T, T), BF16)],
        compiler_params=_ARB2, operands=[qkv, qkv, qkv, o, do]), rider)
    return dq, dk, dv, lands


def _fox_bwd(qkv, fqb, frow, fbounds, o, lse, do, name, rider=None):
    s = qkv.shape[0]
    nq = s // T

    def body(q_ref, k_ref, v_ref, fq_ref, fr_ref, fb_ref, o_ref, lse_ref, do_ref,
             dq_ref, dk_ref, dv_ref, df_ref, dqa_ref, dka_ref, dva_ref, dfa_ref, dls_ref):
        i = pl.program_id(1)

        @pl.when(i == 0)
        def _():
            dka_ref[...] = jnp.zeros_like(dka_ref)
            dva_ref[...] = jnp.zeros_like(dva_ref)
            dfa_ref[...] = jnp.zeros_like(dfa_ref)

        m0 = _pair_masks()
        qh = _split_pair(q_ref[...] * jnp.asarray(Q_SCALE, BF16), m0)
        doh = _split_pair(do_ref[...], m0)
        dsum = _row_dots(do_ref, o_ref, m0)
        fq = fq_ref[...]
        fqh = (fq[:, 0:1], fq[:, HEAD_DIM:HEAD_DIM + 1])
        lse = lse_ref[...]
        lseh = (lse[:, 0:1], lse[:, HEAD_DIM:HEAD_DIM + 1])
        causal = _iota2((T, T), 0) >= _iota2((T, T), 1)
        dqa_ref[...] = jnp.zeros_like(dqa_ref)

        hs = range(2)

        def flush(j):
            off = pl.multiple_of(j * T, T)
            k = k_ref[pl.ds(off, T), :]
            for h in hs:
                dqa_ref[h] += _dot(dls_ref[h], k)
            dka_ref[pl.ds(off, T), :] += _dot_tn(dls_ref[0], qh[0]) + _dot_tn(dls_ref[1], qh[1])

        def tile(j, diag):
            off = pl.multiple_of(j * T, T)
            k = k_ref[pl.ds(off, T), :]
            v = v_ref[pl.ds(off, T), :]
            sc = [_dot_nt(qh[h], k) + fqh[h] - fr_ref[h:h + 1, pl.ds(off, T)] for h in hs]
            dp = [_dot_nt(doh[h], v) for h in hs]
            if not diag:
                flush(j + 1)
            p = [jnp.exp(sc[h] - lseh[h]) for h in hs]
            if diag:
                p = [jnp.where(causal, p[h], 0.0) for h in hs]
            dva_ref[pl.ds(off, T), :] += (_dot_tn(p[0].astype(BF16), doh[0])
                                          + _dot_tn(p[1].astype(BF16), doh[1]))
            dl = [p[h] * (dp[h] - dsum[h]) for h in hs]
            for h in hs:
                dls_ref[h] = dl[h].astype(BF16)
                dfa_ref[h:h + 1, pl.ds(off, T)] -= jnp.sum(dl[h], axis=0, keepdims=True)

        tile(i, True)
        j_left = _fox_tiles_left(i, pl.program_id(0), nq, fb_ref, tile)
        flush(j_left + 1)
        dq_ref[...] = (jnp.where(m0, dqa_ref[0], dqa_ref[1]) * Q_SCALE).astype(BF16)

        @pl.when(i == nq - 1)
        def _():
            dk_ref[...] = dka_ref[...].astype(BF16)
            dv_ref[...] = dva_ref[...].astype(BF16)
            df_ref[...] = dfa_ref[...]

    nb = FX_W // LANES
    base = 3 * SB_W // LANES
    (dq, dk, dv, df), lands = _ride(dict(
        body=body, name=name, grid=(nb, nq),
        in_specs=[pl.BlockSpec((T, LANES), lambda p, i: (i, base + p)),
                  pl.BlockSpec((s, LANES), lambda p, i: (0, base + nb + p)),
                  pl.BlockSpec((s, LANES), lambda p, i: (0, base + 2 * nb + p)),
                  pl.BlockSpec((T, LANES), lambda p, i: (i, p)),
                  pl.BlockSpec((None, 2, s), lambda p, i: (p, 0, 0)),
                  pl.BlockSpec(memory_space=pltpu.SMEM),
                  pl.BlockSpec((T, LANES), lambda p, i: (i, p)),
                  pl.BlockSpec((T, LANES), lambda p, i: (i, p)),
                  pl.BlockSpec((T, LANES), lambda p, i: (i, p))],
        out_specs=[pl.BlockSpec((T, LANES), lambda p, i: (i, p)),
                   pl.BlockSpec((s, LANES), lambda p, i: (0, p)),
                   pl.BlockSpec((s, LANES), lambda p, i: (0, p)),
                   pl.BlockSpec((None, 2, s), lambda p, i: (p, 0, 0))],
        out_shape=[jax.ShapeDtypeStruct((s, FX_W), BF16)] * 3
        + [jax.ShapeDtypeStruct((nb, 2, s), F32)],
        scratch_shapes=[pltpu.VMEM((2, T, LANES), F32), pltpu.VMEM((s, LANES), F32),
                        pltpu.VMEM((s, LANES), F32), pltpu.VMEM((2, s), F32),
                        pltpu.VMEM((2, T, T), BF16)],
        compiler_params=_ARB2, operands=[qkv, qkv, qkv, fqb, frow, fbounds, o, lse, do]), rider)
    return dq, dk, dv, df, lands


def _fox_prep_bwd(dfrow, gf, bpad, name):
    s = gf.shape[0]

    def body(df_ref, fl_ref, b_ref, dfl_ref, db_ref):
        tri = jnp.where(_iota2((T, T), 0) <= _iota2((T, T), 1), 1.0, 0.0).astype(BF16)
        carry = jnp.zeros((1, LANES), F32)
        db = jnp.zeros((1, LANES), F32)
        fill = jnp.zeros((LANES - FOX_HEADS, T), F32)
        for blk in reversed(range(s // T)):
            rows = slice(blk * T, (blk + 1) * T)
            c = _tri3(tri, jnp.concatenate([df_ref[:, rows], fill], axis=0), _dot_nt) + carry
            carry = c[0:1, :]
            dfl = c / (1.0 + jnp.exp(fl_ref[rows, :] + b_ref[...]))
            dfl_ref[rows, :] = dfl.astype(BF16)
            db = db + jnp.sum(dfl, axis=0, keepdims=True)
        db_ref[...] = db

    return pl.pallas_call(
        body, name=name, grid=(1,),
        in_specs=[pl.BlockSpec((FOX_HEADS, s), lambda i: (0, 0)),
                  pl.BlockSpec((s, LANES), lambda i: (0, MIX_W // LANES)),
                  pl.BlockSpec((1, LANES), lambda i: (0, 0))],
        out_specs=[pl.BlockSpec((s, LANES), lambda i: (0, 0)),
                   pl.BlockSpec((1, LANES), lambda i: (0, 0))],
        out_shape=[jax.ShapeDtypeStruct((s, LANES), BF16), jax.ShapeDtypeStruct((1, LANES), F32)],
        compiler_params=_ARB1,
    )(dfrow, gf, bpad)


def _mem_bwd(qkv, kv, o, lse, do, name):
    s = qkv.shape[0]
    n = kv.shape[0]

    def body(q_ref, k_ref, v_ref, o_ref, lse_ref, do_ref, dq_ref, dk_ref, dv_ref):
        @pl.when(pl.program_id(1) == 0)
        def _():
            dk_ref[...] = jnp.zeros_like(dk_ref)
            dv_ref[...] = jnp.zeros_like(dv_ref)

        m0 = _pair_masks()
        qh = _split_pair(q_ref[...] * jnp.asarray(Q_SCALE, BF16), m0)
        doh = _split_pair(do_ref[...], m0)
        dsum = _row_dots(do_ref, o_ref, m0)
        lse = lse_ref[...]
        lseh = (lse[:, 0:1], lse[:, HEAD_DIM:HEAD_DIM + 1])
        k = k_ref[...]
        v = v_ref[...]
        dqs = []
        for h in range(2):
            p = jnp.exp(_dot_nt(qh[h], k) - lseh[h])
            dl = p * (_dot_nt(doh[h], v) - dsum[h])
            dlb = dl.astype(BF16)
            dqs.append(_dot(dlb, k))
            dk_ref[...] += _dot_tn(dlb, qh[h])
            dv_ref[...] += _dot_tn(p.astype(BF16), doh[h])
        dq_ref[...] = (jnp.where(m0, dqs[0], dqs[1]) * Q_SCALE).astype(BF16)

    nb = MEM_W // LANES
    base = (3 * SB_W + 3 * FX_W) // LANES
    return pl.pallas_call(
        body, name=name, grid=(nb, s // TQM),
        in_specs=[pl.BlockSpec((TQM, LANES), lambda p, i: (i, base + p)),
                  pl.BlockSpec((n, LANES), lambda p, i: (0, p)),
                  pl.BlockSpec((n, LANES), lambda p, i: (0, nb + p)),
                  pl.BlockSpec((TQM, LANES), lambda p, i: (i, p)),
                  pl.BlockSpec((TQM, LANES), lambda p, i: (i, p)),
                  pl.BlockSpec((TQM, LANES), lambda p, i: (i, p))],
        out_specs=[pl.BlockSpec((TQM, LANES), lambda p, i: (i, p)),
                   pl.BlockSpec((n, LANES), lambda p, i: (0, p)),
                   pl.BlockSpec((n, LANES), lambda p, i: (0, p))],
        out_shape=[jax.ShapeDtypeStruct((s, MEM_W), BF16), jax.ShapeDtypeStruct((n, MEM_W), F32),
                   jax.ShapeDtypeStruct((n, MEM_W), F32)],
        compiler_params=_ARB2,
    )(qkv, kv, kv, o, lse, do)


def _memkv_bwd(mem, mnw, wkv, dk, dv, name):
    n = mem.shape[0]

    def body(mem_ref, mnw_ref, w_ref, dk_ref, dv_ref, dw_ref, dmnw_ref):
        mv = mem_ref[...]
        r = lax.rsqrt(jnp.mean(mv * mv, axis=-1, keepdims=True) + EPS)
        mh = mv * r
        hm = (mh * mnw_ref[...]).astype(BF16)
        dkv = jnp.concatenate([dk_ref[...], dv_ref[...]], axis=1).astype(BF16)
        dw_ref[...] = _dot_tn(hm, dkv)
        dhm = _dot_nt(dkv, w_ref[...])
        dmnw_ref[...] = jnp.sum(dhm * mh, axis=0, keepdims=True)

    return pl.pallas_call(
        body, name=name, grid=(1,),
        in_specs=[pl.BlockSpec((n, D_MODEL), lambda i: (0, 0)),
                  pl.BlockSpec((1, D_MODEL), lambda i: (0, 0)),
                  pl.BlockSpec((D_MODEL, 2 * MEM_W), lambda i: (0, 0)),
                  pl.BlockSpec((n, MEM_W), lambda i: (0, 0)),
                  pl.BlockSpec((n, MEM_W), lambda i: (0, 0))],
        out_specs=[pl.BlockSpec((D_MODEL, 2 * MEM_W), lambda i: (0, 0)),
                   pl.BlockSpec((1, D_MODEL), lambda i: (0, 0))],
        out_shape=[jax.ShapeDtypeStruct((D_MODEL, 2 * MEM_W), F32),
                   jax.ShapeDtypeStruct((1, D_MODEL), F32)],
        compiler_params=_ARB1,
    )(mem, mnw, wkv, dk, dv)


def _inproj_bwd_dx(pieces, w_r, x, nw, dxo, name):
    s = x.shape[0]
    n = len(pieces)
    widths = [p.shape[1] for p in pieces]

    def body(*refs):
        piece_refs = refs[:n]
        w_ref, x_ref, nw_ref, dxo_ref, dx_ref, h_ref, dnw_ref, dp_ref = refs[n:]

        @pl.when(pl.program_id(0) == 0)
        def _():
            dnw_ref[...] = jnp.zeros_like(dnw_ref)

        col = 0
        for r, wd in zip(piece_refs, widths):
            dp_ref[:, col:col + wd] = r[...]
            col += wd
        dp_ref[:, col:] = jnp.zeros((TM, WR_W - col), BF16)
        dh = _dot(dp_ref[...], w_ref[...])
        xv = x_ref[...]
        nw = nw_ref[...]
        r = lax.rsqrt(jnp.mean(xv * xv, axis=-1, keepdims=True) + EPS)
        xh = xv * r
        h_ref[...] = (xh * nw).astype(BF16)
        dnw_ref[...] += jnp.sum(dh * xh, axis=0, keepdims=True)
        dxh = dh * nw
        dx_ref[...] = r * (dxh - xh * jnp.mean(dxh * xh, axis=-1, keepdims=True)) + dxo_ref[...]

    return pl.pallas_call(
        body, name=name, grid=(s // TM,),
        in_specs=[pl.BlockSpec((TM, wd), lambda i: (i, 0)) for wd in widths]
        + [pl.BlockSpec((WR_W, D_MODEL), lambda i: (0, 0)),
           pl.BlockSpec((TM, D_MODEL), lambda i: (i, 0)),
           pl.BlockSpec((1, D_MODEL), lambda i: (0, 0)),
           pl.BlockSpec((TM, D_MODEL), lambda i: (i, 0))],
        out_specs=[pl.BlockSpec((TM, D_MODEL), lambda i: (i, 0)),
                   pl.BlockSpec((TM, D_MODEL), lambda i: (i, 0)),
                   pl.BlockSpec((1, D_MODEL), lambda i: (0, 0)),
                   pl.BlockSpec((TM, WR_W), lambda i: (i, 0))],
        out_shape=[jax.ShapeDtypeStruct((s, D_MODEL), F32), jax.ShapeDtypeStruct((s, D_MODEL), BF16),
                   jax.ShapeDtypeStruct((1, D_MODEL), F32), jax.ShapeDtypeStruct((s, WR_W), BF16)],
        compiler_params=_ARB1,
    )(*pieces, w_r, x, nw, dxo)


def _inproj_bwd_dw(h, dproj, name):
    s = dproj.shape[0]
    tn = 256

    def body(h_ref, dp_ref, dw_ref):
        dw_ref[...] = _dot_tn(dp_ref[...], h_ref[...])

    return pl.pallas_call(
        body, name=name, grid=(WR_W // tn,),
        in_specs=[pl.BlockSpec((s, D_MODEL), lambda j: (0, 0)),
                  pl.BlockSpec((s, tn), lambda j: (0, j))],
        out_specs=pl.BlockSpec((tn, D_MODEL), lambda j: (j, 0)),
        out_shape=jax.ShapeDtypeStruct((WR_W, D_MODEL), F32),
        compiler_params=_ARB1,
    )(h, dproj)


def _rearrange_w_in(wt):
    pad = jnp.zeros((FL_PAD - FOX_HEADS,) + wt.shape[1:], wt.dtype)
    return jnp.concatenate([wt[:3072], wt[3080:3336], wt[3336:IN_W], wt[3072:3080], pad], axis=0)


def _restore_w_in(g):
    gate0 = QKV_W
    fl0 = QKV_W + MIX_W
    return jnp.concatenate(
        [g[:3072], g[fl0:fl0 + FOX_HEADS], g[3072:QKV_W], g[gate0:fl0]], axis=0)


def _pad_lanes(v, width=LANES):
    return jnp.pad(v, (0, width - v.shape[0])).reshape(1, width)


def _layer_fwd(xs, mem, nw, w_r, b_forget, mnw, late, onw, l, travel=None):
    s = xs.shape[0]
    bpad = _pad_lanes(b_forget)
    qkv, gf = _inproj_fwd(xs, nw, w_r, f"inproj_fwd_{l}")
    fqb, frow, fbounds = _fox_prep_fwd(gf, qkv, bpad, f"fox_prep_fwd_{l}")
    frow = frow.reshape(FOX_HEADS // 2, 2, s)
    travel = _Travel(travel)
    ysb, _ = travel.ride(0, _sb_fwd, qkv, f"sb_fwd_{l}")
    yfx, lse_fx, _ = travel.ride(1, _fox_fwd, qkv, fqb, frow, fbounds, f"fox_fwd_{l}")
    wkv, wout = late(travel.lands)
    kv = _memkv_fwd(mem, mnw, wkv, f"memkv_fwd_{l}")
    ym, lse_m = _mem_fwd(qkv, kv, f"mem_fwd_{l}")
    xn = _outproj_fwd(ysb, yfx, ym, gf, onw, wout, xs, f"outproj_fwd_{l}")
    saved = (xs, nw, mnw, onw, bpad, qkv, gf, fqb, frow, fbounds, ysb, yfx, lse_fx, kv, ym, lse_m)
    return xn, saved, travel.lands, (wkv, wout)


class _Travel:
    def __init__(self, plan):
        self.plan = plan
        self.lands = None if plan is None else _new_lands(plan[0], plan[1])

    def ride(self, n, fn, *args):
        if self.plan is None or self.plan[2][n] is None:
            return fn(*args)
        srcs, scatter, legs = self.plan
        idx, rows = legs[n]
        out = fn(*args, rider=_Rider([srcs[a] for a in idx], [self.lands[a] for a in idx],
                                     scatter, rows))
        for a, land in zip(idx, out[-1]):
            self.lands[a] = land
        return out


def _layer_bwd(dx, saved, mem, w_r, wkv, wout, l, travel=None):
    xs, nw, mnw, onw, bpad, qkv, gf, fqb, frow, fbounds, ysb, yfx, lse_fx, kv, ym, lse_m = saved
    s = xs.shape[0]
    dysb, dyfx, dym, dgate, dwout, donw = _outproj_bwd(
        dx, wout, ysb, yfx, ym, gf, onw, f"outproj_bwd_{l}")
    travel = _Travel(None if travel is None else travel(dwout))
    sdq, sdk, sdv, _ = travel.ride(0, _sb_bwd, qkv, ysb, dysb, f"sb_bwd_{l}")
    fdq, fdk, fdv, dfrow, _ = travel.ride(1, _fox_bwd, qkv, fqb, frow, fbounds, yfx, lse_fx, dyfx,
                                          f"fox_bwd_{l}")
    dfl, db = _fox_prep_bwd(dfrow.reshape(FOX_HEADS, s), gf, bpad, f"fox_prep_bwd_{l}")
    dmq, dmk, dmv = _mem_bwd(qkv, kv, ym, lse_m, dym, f"mem_bwd_{l}")
    dwkv, dmnw = _memkv_bwd(mem, mnw, wkv, dmk, dmv, f"memkv_bwd_{l}")
    dx, ht, dnw, dproj = _inproj_bwd_dx([sdq, sdk, sdv, fdq, fdk, fdv, dmq, dgate, dfl],
                                        w_r, xs, nw, dx, f"inproj_bwd_dx_{l}")
    dwr = _inproj_bwd_dw(ht, dproj, f"inproj_bwd_dw_{l}")
    grads = dict(norm_w=dnw[0], w_r=dwr, b_forget=db[0, :FOX_HEADS], mem_norm_w=dmnw[0],
                 w_mem_kv=dwkv, out_norm_w=donw[0], w_out=dwout)
    return dx, grads, travel.lands


_ANY = pl.BlockSpec(memory_space=pl.ANY)


def _my_place():
    return lax.axis_index("x"), lax.axis_index("y"), lax.axis_index("c")


def _flip(v, bit):
    return 1 - v if bit else v


def _block_index(px, py, pc):
    return 4 * px + 2 * py + pc


def _all_gather_weights(shards, name):
    n = len(shards)

    def body(*refs):
        ins, outs = refs[:n], refs[n:2 * n]
        send_sems, recv_sems, local_sems = refs[2 * n:]
        x, y, c = _my_place()
        me = (x, y, c)
        sibling = (x, y, 1 - c)
        chips = [(1 - x, y), (x, 1 - y), (1 - x, 1 - y)]

        def copy(a, k, block, to, src=None):
            dst = outs[a].at[_block_index(*block)]
            return pltpu.make_async_remote_copy(
                src_ref=dst if src is None else src, dst_ref=dst,
                send_sem=send_sems.at[a, k], recv_sem=recv_sems.at[a, k],
                device_id=to, device_id_type=pl.DeviceIdType.MESH)

        mine = [pltpu.make_async_copy(ins[a], outs[a].at[_block_index(*me)], local_sems.at[a])
                for a in range(n)]
        for cp in mine:
            cp.start()
        first = []
        for a in range(n):
            first.append(copy(a, 0, me, sibling, src=ins[a]))
            first += [copy(a, 1 + j, me, (*chip, c), src=ins[a]) for j, chip in enumerate(chips)]
        for cp in first:
            cp.start()
        passed = []
        for j, chip in enumerate(chips):
            for a in range(n):
                copy(a, 1 + j, (*chip, c), me).wait_recv()
                fwd = copy(a, 4 + j, (*chip, c), sibling)
                fwd.start()
                passed.append(fwd)
        for a in range(n):
            copy(a, 0, sibling, me).wait_recv()
            for j, chip in enumerate(chips):
                copy(a, 4 + j, (*chip, 1 - c), me).wait_recv()
        for cp in first + passed:
            cp.wait_send()
        for cp in mine:
            cp.wait()

    return pl.pallas_call(
        body, name=name,
        in_specs=[_ANY] * n, out_specs=[_ANY] * n,
        out_shape=[jax.ShapeDtypeStruct((N_DEV,) + v.shape, v.dtype) for v in shards],
        scratch_shapes=[pltpu.SemaphoreType.DMA((n, 7)), pltpu.SemaphoreType.DMA((n, 7)),
                        pltpu.SemaphoreType.DMA((n,))],
    )(*shards)


def _exchange_blocks(blocked, name):
    n = len(blocked)

    def body(*refs):
        ins, outs = refs[:n], refs[n:2 * n]
        send_sems, recv_sems, local_sems = refs[2 * n:]
        x, y, c = _my_place()
        mine_idx = _block_index(x, y, c)
        local = [pltpu.make_async_copy(ins[a].at[mine_idx], outs[a].at[mine_idx], local_sems.at[a])
                 for a in range(n)]
        for cp in local:
            cp.start()
        sends, arrivals = [], []
        for r in range(1, N_DEV):
            peer = (_flip(x, r & 4), _flip(y, r & 2), _flip(c, r & 1))
            peer_idx = _block_index(*peer)
            for a in range(n):
                sems = dict(send_sem=send_sems.at[a, r - 1], recv_sem=recv_sems.at[a, r - 1],
                            device_id=peer, device_id_type=pl.DeviceIdType.MESH)
                sends.append(pltpu.make_async_remote_copy(
                    src_ref=ins[a].at[peer_idx], dst_ref=outs[a].at[mine_idx], **sems))
                arrivals.append(pltpu.make_async_remote_copy(
                    src_ref=ins[a].at[peer_idx], dst_ref=outs[a].at[peer_idx], **sems))
        for cp in sends:
            cp.start()
        for cp in arrivals:
            cp.wait_recv()
        for cp in sends:
            cp.wait_send()
        for cp in local:
            cp.wait()

    return pl.pallas_call(
        body, name=name,
        in_specs=[_ANY] * n, out_specs=[_ANY] * n,
        out_shape=[jax.ShapeDtypeStruct(v.shape, v.dtype) for v in blocked],
        scratch_shapes=[pltpu.SemaphoreType.DMA((n, 7)), pltpu.SemaphoreType.DMA((n, 7)),
                        pltpu.SemaphoreType.DMA((n,))],
    )(*blocked)


N_CHIP = N_DEV // 2


def _pair_swap(blocked, name):
    n = len(blocked)

    def body(*refs):
        ins, outs = refs[:n], refs[n:2 * n]
        send_sems, recv_sems = refs[2 * n:]
        x, y, c = _my_place()
        copies = [pltpu.make_async_remote_copy(
            src_ref=ins[a].at[j, 1 - c], dst_ref=outs[a].at[j],
            send_sem=send_sems.at[N_CHIP * a + j], recv_sem=recv_sems.at[N_CHIP * a + j],
            device_id=(x, y, 1 - c), device_id_type=pl.DeviceIdType.MESH)
            for a in range(n) for j in range(N_CHIP)]
        for cp in copies:
            cp.start()
        for cp in copies:
            cp.wait_recv()
        for cp in copies:
            cp.wait_send()

    return pl.pallas_call(
        body, name=name,
        in_specs=[_ANY] * n, out_specs=[_ANY] * n,
        out_shape=[jax.ShapeDtypeStruct((N_CHIP,) + v.shape[2:], v.dtype) for v in blocked],
        scratch_shapes=[pltpu.SemaphoreType.DMA((N_CHIP * n,)),
                        pltpu.SemaphoreType.DMA((N_CHIP * n,))],
    )(*blocked)


def _pair_add(mine, theirs, name):
    _, nrow, ncol = mine.shape

    def body(a_ref, b_ref, o_ref):
        o_ref[...] = (a_ref[...].astype(F32) + b_ref[...].astype(F32)).astype(BF16)

    blk = pl.BlockSpec((None, nrow, ncol), lambda j: (j, 0, 0))
    return pl.pallas_call(
        body, name=name, grid=(N_CHIP,), in_specs=[blk, blk], out_specs=blk,
        out_shape=jax.ShapeDtypeStruct(mine.shape, BF16), compiler_params=_ARB1,
    )(mine, theirs)


def _chip_exchange(by_chip, to_all, name):
    n, m = len(by_chip), len(to_all)

    def body(*refs):
        ins, alls = refs[:n], refs[n:n + m]
        outs, all_outs = refs[n + m:2 * n + m], refs[2 * n + m:2 * (n + m)]
        send_sems, recv_sems, local_sems = refs[2 * (n + m):]
        x, y, c = _my_place()
        my_chip, mine_idx = 2 * x + y, _block_index(x, y, c)
        local = [pltpu.make_async_copy(ins[a].at[my_chip], outs[a].at[my_chip], local_sems.at[a])
                 for a in range(n)]
        local += [pltpu.make_async_copy(alls[b].at[mine_idx], all_outs[b].at[mine_idx],
                                        local_sems.at[n + b]) for b in range(m)]
        for cp in local:
            cp.start()
        sends, arrivals = [], []
        k = 0
        for r in range(1, N_DEV):
            peer = (_flip(x, r & 4), _flip(y, r & 2), _flip(c, r & 1))
            peer_chip, peer_idx = 2 * peer[0] + peer[1], _block_index(*peer)
            pairs = [(alls[b].at[mine_idx], all_outs[b].at[mine_idx], all_outs[b].at[peer_idx])
                     for b in range(m)]
            if not r & 1:
                pairs += [(ins[a].at[peer_chip], outs[a].at[my_chip], outs[a].at[peer_chip])
                          for a in range(n)]
            for src, there, here in pairs:
                sems = dict(send_sem=send_sems.at[k], recv_sem=recv_sems.at[k], device_id=peer,
                            device_id_type=pl.DeviceIdType.MESH)
                sends.append(pltpu.make_async_remote_copy(src_ref=src, dst_ref=there, **sems))
                arrivals.append(pltpu.make_async_remote_copy(src_ref=src, dst_ref=here, **sems))
                k += 1
        for cp in sends:
            cp.start()
        for cp in arrivals:
            cp.wait_recv()
        for cp in sends:
            cp.wait_send()
        for cp in local:
            cp.wait()

    n_copies = 7 * m + 3 * n
    return pl.pallas_call(
        body, name=name,
        in_specs=[_ANY] * (n + m), out_specs=[_ANY] * (n + m),
        out_shape=[jax.ShapeDtypeStruct(v.shape, v.dtype) for v in by_chip + to_all],
        scratch_shapes=[pltpu.SemaphoreType.DMA((n_copies,)), pltpu.SemaphoreType.DMA((n_copies,)),
                        pltpu.SemaphoreType.DMA((n + m,))],
    )(*by_chip, *to_all)


class _Rider(NamedTuple):
    srcs: list
    lands: list
    scatter: bool
    part: list


def _new_lands(srcs, scatter):
    return [lax.empty(v.shape if scatter else (N_DEV,) + v.shape, v.dtype) for v in srcs]


def _rider_copies(srcs, lands, send_sems, recv_sems, rider):
    x, y, c = _my_place()
    mine_idx = _block_index(x, y, c)

    def window(ref, a):
        if rider.part[a] is None:
            return ref
        dim, start, size = rider.part[a]
        return ref.at[(slice(None),) * dim + (pl.ds(start, size),)]

    sends, arrivals = [], []
    for r in range(1, N_DEV):
        peer = (_flip(x, r & 4), _flip(y, r & 2), _flip(c, r & 1))
        peer_idx = _block_index(*peer)
        for a in range(len(srcs)):
            src = window(srcs[a].at[peer_idx] if rider.scatter else srcs[a], a)
            k = 7 * a + r - 1
            sems = dict(send_sem=send_sems.at[k], recv_sem=recv_sems.at[k],
                        device_id=peer, device_id_type=pl.DeviceIdType.MESH)
            sends.append(pltpu.make_async_remote_copy(
                src_ref=src, dst_ref=window(lands[a].at[mine_idx], a), **sems))
            arrivals.append(pltpu.make_async_remote_copy(
                src_ref=src, dst_ref=window(lands[a].at[peer_idx], a), **sems))
    return sends, arrivals


def _ride(call, rider):
    call = dict(call)
    body, grid = call.pop("body"), call["grid"]
    operands = call.pop("operands")
    if rider is None:
        return list(pl.pallas_call(body, **call)(*operands)), None
    n_in, n_out = len(call["in_specs"]), len(call["out_specs"])
    n_scratch = len(call["scratch_shapes"])
    m = len(rider.srcs)

    def riding(*refs):
        main_in, srcs, lands = refs[:n_in], refs[n_in:n_in + m], refs[n_in + m:n_in + 2 * m]
        main_out = refs[n_in + 2 * m:n_in + 2 * m + n_out]
        rest = refs[n_in + 3 * m + n_out:]
        send_sems, recv_sems = rest[n_scratch:]
        at = [pl.program_id(d) for d in range(len(grid))]
        first = functools.reduce(jnp.logical_and, [p == 0 for p in at])
        last = functools.reduce(jnp.logical_and, [p == g - 1 for p, g in zip(at, grid)])
        sends, arrivals = _rider_copies(srcs, lands, send_sems, recv_sems, rider)

        @pl.when(first)
        def _():
            for cp in sends:
                cp.start()

        body(*main_in, *main_out, *rest[:n_scratch])

        @pl.when(last)
        def _():
            for cp in arrivals:
                cp.wait_recv()
            for cp in sends:
                cp.wait_send()

    call["in_specs"] = list(call["in_specs"]) + [_ANY] * (2 * m)
    call["out_specs"] = list(call["out_specs"]) + [_ANY] * m
    call["out_shape"] = list(call["out_shape"]) + [
        jax.ShapeDtypeStruct(v.shape, v.dtype) for v in rider.lands]
    call["scratch_shapes"] = list(call["scratch_shapes"]) + [
        pltpu.SemaphoreType.DMA((7 * m,)), pltpu.SemaphoreType.DMA((7 * m,))]
    call["input_output_aliases"] = {n_in + m + a: n_out + a for a in range(m)}
    outs = pl.pallas_call(riding, **call)(*operands, *rider.srcs, *rider.lands)
    return list(outs[:n_out]), list(outs[n_out:])


def _sum_parts(p_ref):
    g = p_ref[0].astype(F32)
    for k in range(1, p_ref.shape[0]):
        g = g + p_ref[k].astype(F32)
    return g


def _adamw(g, w, m, v):
    c1 = 1.0 / (1.0 - ADAM_B1 ** ADAM_STEP)
    c2 = 1.0 / (1.0 - ADAM_B2 ** ADAM_STEP)
    nm = ADAM_B1 * m + (1.0 - ADAM_B1) * g
    nv = ADAM_B2 * v + (1.0 - ADAM_B2) * (g * g)
    return nm, nv, -ADAM_LR * ((nm * c1) / (jnp.sqrt(nv * c2) + ADAM_EPS) + ADAM_WD * w)


def _adamw_w_in(parts, w, m, v, name):
    ncol_blk, depth, nfeat = w.shape
    cols = 256

    def body(*refs):
        p_refs = refs[:depth]
        w_ref, m_ref, v_ref, g_ref, d_ref, nm_ref, nv_ref = refs[depth:]
        for l in range(depth):
            g = _sum_parts(p_refs[l])
            nm, nv, d = _adamw(g, w_ref[:, l, :], m_ref[:, l, :], v_ref[:, l, :])
            g_ref[:, l, :] = g
            nm_ref[:, l, :] = nm
            nv_ref[:, l, :] = nv
            d_ref[:, l, :] = d

    blk = pl.BlockSpec((ncol_blk, depth, cols), lambda j: (0, 0, j))
    return pl.pallas_call(
        body, name=name, grid=(nfeat // cols,),
        in_specs=[pl.BlockSpec((p.shape[0], ncol_blk, cols), lambda j: (0, 0, j)) for p in parts]
        + [blk] * 3,
        out_specs=[blk] * 4,
        out_shape=[jax.ShapeDtypeStruct(w.shape, F32)] * 4,
        compiler_params=_ARB1,
    )(*parts, w, m, v)


def _adamw_sum(parts, w, m, v, tile, name):
    depth, nrow, ncol = w.shape
    rows, cols = tile

    def body(*refs):
        p_refs = refs[:depth]
        w_ref, m_ref, v_ref, g_ref, d_ref, nm_ref, nv_ref = refs[depth:]
        layer = pl.program_id(0)
        for l in range(depth):
            @pl.when(layer == l)
            def _(p_ref=p_refs[l]):
                g = _sum_parts(p_ref)
                nm, nv, d = _adamw(g, w_ref[...], m_ref[...], v_ref[...])
                g_ref[...] = g
                nm_ref[...] = nm
                nv_ref[...] = nv
                d_ref[...] = d

    def part_spec(l):
        return pl.BlockSpec((parts[l].shape[0], rows, cols), lambda q, i, j: (
            0, jnp.where(q == l, i, 0), jnp.where(q == l, j, 0)))

    blk = pl.BlockSpec((None, rows, cols), lambda q, i, j: (q, i, j))
    return pl.pallas_call(
        body, name=name, grid=(depth, nrow // rows, ncol // cols),
        in_specs=[part_spec(l) for l in range(depth)] + [blk, blk, blk],
        out_specs=[blk] * 4,
        out_shape=[jax.ShapeDtypeStruct(w.shape, F32)] * 4,
        compiler_params=pltpu.CompilerParams(
            dimension_semantics=("arbitrary", "arbitrary", "arbitrary")),
    )(*parts, w, m, v)


def _pack_small(norm_w, mem_norm_w, out_norm_w, final_norm_w, b_forget):
    onw = jnp.pad(out_norm_w.reshape(20, LANES), ((0, 4), (0, 0)))
    b = jnp.pad(b_forget, ((0, 6), (0, LANES - FOX_HEADS)))
    return jnp.concatenate([norm_w.reshape(16, LANES), mem_norm_w.reshape(16, LANES), onw,
                            final_norm_w.reshape(8, LANES), b], axis=0)


def _unpack_small(p):
    return (p[0:16].reshape(2, D_MODEL), p[16:32].reshape(2, D_MODEL), p[32:52].reshape(2, MIX_W),
            p[56:64].reshape(D_MODEL), p[64:66, :FOX_HEADS])


def kernel(x, mem, norm_w, w_in, b_forget, mem_norm_w, w_mem_kv, out_norm_w, w_out, final_norm_w, loss_target, m_norm_w, m_w_in, m_b_forget, m_mem_norm_w, m_w_mem_kv, m_out_norm_w, m_w_out, m_final_norm_w, v_norm_w, v_w_in, v_b_forget, v_mem_norm_w, v_w_mem_kv, v_out_norm_w, v_w_out, v_final_norm_w):
    kv_rows = w_mem_kv.shape[1]
    out_rows = w_out.shape[1]
    me = _block_index(*_my_place())

    def shards(l):
        return [w_in[l].T.astype(BF16), w_mem_kv[l].astype(BF16), w_out[l].astype(BF16)]

    def full_in(g_in):
        return _rearrange_w_in(g_in.reshape(IN_W, D_MODEL))

    def full_kv_out(g_kv, g_out):
        return g_kv.reshape(D_MODEL, 2 * MEM_W), g_out.reshape(MIX_W, D_MODEL)

    def in_blocks(g):
        segments = [(0, 3072, 0), (3072, 3080, QKV_W + MIX_W), (3080, 3336, 3072),
                    (3336, IN_W, QKV_W)]

        def block(k):
            lo, hi = k * SHARD_W, (k + 1) * SHARD_W
            pieces = [g[at + max(lo, a) - a:at + min(hi, b) - a]
                      for a, b, at in segments if max(lo, a) < min(hi, b)]
            return jnp.concatenate(pieces, axis=0).astype(BF16)

        return jnp.stack([block(k) for k in range(N_DEV)])

    def kv_blocks(g):
        return g.reshape(N_DEV, kv_rows, 2 * MEM_W).astype(BF16)

    def out_blocks(g):
        return g.reshape(N_DEV, out_rows, D_MODEL).astype(BF16)

    def with_own(land, own):
        return lax.dynamic_update_slice(land, own[None], (me,) + (0,) * own.ndim)

    def with_own_of(land, blocked):
        return lax.dynamic_update_slice(land, lax.dynamic_slice_in_dim(blocked, me, 1, axis=0),
                                        (me,) + (0,) * (land.ndim - 1))

    def row(v):
        return v.reshape(1, -1)

    def cols(first, size):
        return (1, first, size)

    fwd_split, bwd_split = 5 * LANES, 6 * LANES

    g_in0, g_kv0, g_out0 = _all_gather_weights(shards(0), "all_gather_l0")
    w_r0 = full_in(g_in0)
    s_in1, s_kv1, s_out1 = shards(1)
    x1, saved0, (l_in1,), (wkv0, wout0) = _layer_fwd(
        x[0], mem[0], row(norm_w[0]), w_r0, b_forget[0], row(mem_norm_w[0]),
        lambda lands: full_kv_out(g_kv0, g_out0), row(out_norm_w[0]), 0,
        travel=([s_in1], False, [([0], [cols(0, fwd_split)]),
                                 ([0], [cols(fwd_split, D_MODEL - fwd_split)])]))
    w_r1 = full_in(with_own(l_in1, s_in1))
    x2, saved1, _, (wkv1, wout1) = _layer_fwd(
        x1, mem[0], row(norm_w[1]), w_r1, b_forget[1], row(mem_norm_w[1]),
        lambda lands: full_kv_out(with_own(lands[0], s_kv1), with_own(lands[1], s_out1)),
        row(out_norm_w[1]), 1, travel=([s_kv1, s_out1], False, [([0, 1], [None, None]), None]))

    dx2, loss_part, dfnw = _final_fwd_bwd(x2, row(final_norm_w), loss_target[0], "final_fwd_bwd")

    dx1, gr1, (l_out1,) = _layer_bwd(
        dx2, saved1, mem[0], w_r1, wkv1, wout1, 1,
        travel=lambda dwout: ([out_blocks(dwout)], True, [([0], [None]), None]))
    p_in1, p_kv1 = in_blocks(gr1["w_r"]), kv_blocks(gr1["w_mem_kv"])
    grad_x, gr0, (l_in1, l_kv1, l_out0) = _layer_bwd(
        dx1, saved0, mem[0], w_r0, wkv0, wout0, 0,
        travel=lambda dwout: ([p_in1, p_kv1, out_blocks(dwout)], True,
                              [([0, 1], [cols(0, bwd_split), None]),
                               ([0, 2], [cols(bwd_split, D_MODEL - bwd_split), None])]))
    r_out1 = with_own_of(l_out1, out_blocks(gr1["w_out"]))
    r_in1, r_kv1 = with_own_of(l_in1, p_in1), with_own_of(l_kv1, p_kv1)
    r_out0 = with_own_of(l_out0, out_blocks(gr0["w_out"]))

    def both(name):
        return jnp.stack([gr0[name], gr1[name]])

    small = _pack_small(both("norm_w"), both("mem_norm_w"), both("out_norm_w"), dfnw[0],
                        both("b_forget")).at[LOSS_ROW].set(loss_part[0])
    p_small = jnp.broadcast_to(small[None], (N_DEV, SMALL_ROWS, LANES))
    by_core = [v.reshape((N_CHIP, 2) + v.shape[1:])
               for v in (in_blocks(gr0["w_r"]), kv_blocks(gr0["w_mem_kv"]))]
    from_sibling = _pair_swap(by_core, "grads_l0_pair_swap")
    core = lax.axis_index("c")
    chip_sums = [_pair_add(lax.dynamic_index_in_dim(v, core, axis=1, keepdims=False), got,
                           f"grads_l0_pair_add_{a}")
                 for a, (v, got) in enumerate(zip(by_core, from_sibling))]
    r_in0, r_kv0, r_small = _chip_exchange(chip_sums, [p_small], "exchange_grads_l0")

    def view(v):
        return jnp.transpose(v, (2, 0, 1))

    g_w_in, d_w_in, nm_w_in, nv_w_in = [jnp.transpose(v, (1, 2, 0)) for v in _adamw_w_in(
        [r_in0, r_in1], view(w_in), view(m_w_in), view(v_w_in), "adamw_w_in")]
    g_w_kv, d_w_kv, nm_w_kv, nv_w_kv = _adamw_sum(
        [r_kv0, r_kv1], w_mem_kv, m_w_mem_kv, v_w_mem_kv, (kv_rows, 2 * MEM_W), "adamw_w_mem_kv")
    g_w_out, d_w_out, nm_w_out, nv_w_out = _adamw_sum(
        [r_out0, r_out1], w_out, m_w_out, v_w_out, (out_rows, D_MODEL), "adamw_w_out")
    w_small = _pack_small(norm_w, mem_norm_w, out_norm_w, final_norm_w, b_forget)[None]
    m_small = _pack_small(m_norm_w, m_mem_norm_w, m_out_norm_w, m_final_norm_w, m_b_forget)[None]
    v_small = _pack_small(v_norm_w, v_mem_norm_w, v_out_norm_w, v_final_norm_w, v_b_forget)[None]
    small_out = _adamw_sum([r_small], w_small, m_small, v_small, (SMALL_ROWS, LANES), "adamw_small")
    (g_nw, g_mnw, g_onw, g_fnw, g_b), (d_nw, d_mnw, d_onw, d_fnw, d_b), \
        (nm_nw, nm_mnw, nm_onw, nm_fnw, nm_b), (nv_nw, nv_mnw, nv_onw, nv_fnw, nv_b) = [
            _unpack_small(t[0]) for t in small_out]
    loss = small_out[0][0, LOSS_ROW, 0]

    return (loss, grad_x[None],
            g_nw, g_w_in, g_b, g_mnw, g_w_kv, g_onw, g_w_out, g_fnw,
            d_nw, d_w_in, d_b, d_mnw, d_w_kv, d_onw, d_w_out, d_fnw,
            nm_nw, nm_w_in, nm_b, nm_mnw, nm_w_kv, nm_onw, nm_w_out, nm_fnw,
            nv_nw, nv_w_in, nv_b, nv_mnw, nv_w_kv, nv_onw, nv_w_out, nv_fnw)
```

```python
import functools
from typing import NamedTuple

import jax
import jax.numpy as jnp
from jax import lax
from jax.experimental import pallas as pl
from jax.experimental.pallas import tpu as pltpu

F32 = jnp.float32
BF16 = jnp.bfloat16

N_DEV = 8
D_MODEL = 1024
HEAD_DIM = 64
LANES = 128
SB_W = 512
FX_W = 512
MEM_W = 256
MIX_W = 1280
FOX_HEADS = 8
IN_W = 4616
SHARD_W = IN_W // N_DEV
QKV_W = 3 * SB_W + 3 * FX_W + MEM_W
FL_PAD = 256
GF_W = MIX_W + FL_PAD
WR_W = QKV_W + GF_W
EPS = 1e-6
T = 256
TM = 256
TQM = 512
Q_SCALE = 0.125
NEG = -1e30
UNDERFLOW = -110.0
NORM_SLACK = 1.01

ADAM_LR = 0.001
ADAM_B1 = 0.9
ADAM_B2 = 0.999
ADAM_EPS = 1e-08
ADAM_WD = 0.01
ADAM_STEP = 10

SMALL_ROWS = 72
LOSS_ROW = 66

_NT = (((1,), (1,)), ((), ()))
_TN = (((0,), (0,)), ((), ()))

_ARB1 = pltpu.CompilerParams(dimension_semantics=("arbitrary",))
_ARB2 = pltpu.CompilerParams(dimension_semantics=("arbitrary", "arbitrary"))


def _dot(a, b):
    return jnp.dot(a, b, preferred_element_type=F32)


def _dot_nt(a, b):
    return lax.dot_general(a, b, _NT, preferred_element_type=F32)


def _dot_tn(a, b):
    return lax.dot_general(a, b, _TN, preferred_element_type=F32)


def _split2(x):
    hi = x.astype(BF16)
    lo = (x - hi.astype(F32)).astype(BF16)
    return hi, lo


def _stack2(u):
    return jnp.concatenate([u, u], axis=0)


def _cum2(x, u2):
    hi, lo = _split2(x)
    return _dot(jnp.concatenate([hi, lo], axis=1), u2)


def _tri3(tri, x, dot=None):
    dot = dot or _dot
    hi = x.astype(BF16)
    r1 = x - hi.astype(F32)
    mid = r1.astype(BF16)
    lo = (r1 - mid.astype(F32)).astype(BF16)
    return dot(tri, hi) + dot(tri, mid) + dot(tri, lo)


def _iota2(shape, dim):
    return lax.broadcasted_iota(jnp.int32, shape, dim)


def _head_block_diag():
    r = _iota2((LANES, LANES), 0) // HEAD_DIM
    c = _iota2((LANES, LANES), 1) // HEAD_DIM
    return _stack2(jnp.where(r == c, 1.0, 0.0).astype(BF16))


def _head_mean(x, bd):
    return _cum2(x, bd) * (1.0 / HEAD_DIM)


def _sigmoid(x):
    return 1.0 / (1.0 + jnp.exp(-x))


def _log_sigmoid(x):
    return jnp.minimum(x, 0.0) - jnp.log(1.0 + jnp.exp(-jnp.abs(x)))


def _running_top(r_ref):
    return jnp.max(jnp.maximum(r_ref[0], r_ref[1]))


def _fox_tiles_left(i, pair, nq, fb_ref, tile):
    def bound(j):
        b = []
        for h in range(2):
            head = 2 * pair + h
            b.append(2.0 * NORM_SLACK * fb_ref[2 * nq + i, head] * fb_ref[3 * nq, head]
                     + fb_ref[2 * i, head] - fb_ref[2 * j + 1, head])
        return jnp.maximum(b[0], b[1])

    def more(j):
        return jnp.logical_and(j >= 0, bound(jnp.maximum(j, 0)) > UNDERFLOW)

    def step(j):
        tile(j, False)
        return j - 1

    return lax.while_loop(more, step, i - 1)


def _pair_masks():
    lane = _iota2((1, LANES), 1)
    return lane < HEAD_DIM


def _split_pair(x, m0):
    zero = jnp.zeros_like(x)
    return jnp.where(m0, x, zero), jnp.where(m0, zero, x)


def _inproj_fwd(x, nw, w_r, name, rider=None):
    s = x.shape[0]

    def body(x_ref, nw_ref, w_ref, qkv_ref, gf_ref):
        xv = x_ref[...]
        r = lax.rsqrt(jnp.mean(xv * xv, axis=-1, keepdims=True) + EPS)
        h = (xv * r * nw_ref[...]).astype(BF16)
        for c in range(0, QKV_W, 256):
            qkv_ref[:, c:c + 256] = _dot_nt(h, w_ref[c:c + 256, :]).astype(BF16)
        for c in range(0, GF_W, 256):
            gf_ref[:, c:c + 256] = _dot_nt(h, w_ref[QKV_W + c:QKV_W + c + 256, :])

    (qkv, gf), lands = _ride(dict(
        body=body, name=name, grid=(s // TM,),
        in_specs=[pl.BlockSpec((TM, D_MODEL), lambda i: (i, 0)),
                  pl.BlockSpec((1, D_MODEL), lambda i: (0, 0)),
                  pl.BlockSpec((WR_W, D_MODEL), lambda i: (0, 0))],
        out_specs=[pl.BlockSpec((TM, QKV_W), lambda i: (i, 0)),
                   pl.BlockSpec((TM, GF_W), lambda i: (i, 0))],
        out_shape=[jax.ShapeDtypeStruct((s, QKV_W), BF16), jax.ShapeDtypeStruct((s, GF_W), F32)],
        scratch_shapes=[], compiler_params=_ARB1, operands=[x, nw, w_r]), rider)
    return qkv, gf, lands


def _fox_prep_fwd(gf, qkv, bpad, name):
    s = gf.shape[0]
    nq = s // T
    nrow = -(-(3 * nq + 1) // 8) * 8

    def body(fl_ref, q_ref, k_ref, b_ref, fq_ref, fr_ref, fb_ref):
        tri = jnp.where(_iota2((T, T), 0) >= _iota2((T, T), 1), 1.0, 0.0).astype(BF16)
        m0 = _pair_masks()
        lane = _iota2((1, LANES), 1)
        norms = [jnp.zeros((1, LANES), F32) for _ in range(nq + 1)]
        same_head = (_iota2((LANES, LANES), 0) // HEAD_DIM) == (_iota2((LANES, LANES), 1) // HEAD_DIM)
        bd = jnp.where(same_head, 1.0, 0.0).astype(BF16)
        for p in range(FOX_HEADS // 2):
            cols = slice(p * LANES, (p + 1) * LANES)
            q = (q_ref[:, cols] * jnp.asarray(Q_SCALE, BF16)).astype(F32)
            k = k_ref[:, cols].astype(F32)
            qn = _dot((q * q).astype(BF16), bd)
            kn = _dot((k * k).astype(BF16), bd)
            tops = [jnp.max(qn[j * T:(j + 1) * T], axis=0, keepdims=True) for j in range(nq)]
            tops.append(jnp.max(kn, axis=0, keepdims=True))
            tops = [jnp.sqrt(top) for top in tops]
            for h in range(2):
                at = h * HEAD_DIM
                norms = [jnp.where(lane == 2 * p + h, top[:, at:at + 1], row)
                         for top, row in zip(tops, norms)]
        for j in range(nq + 1):
            fb_ref[2 * nq + j:2 * nq + j + 1, :] = norms[j]
        fb_ref[3 * nq + 1:, :] = jnp.zeros((nrow - 3 * nq - 1, LANES), F32)
        carry = jnp.zeros((1, LANES), F32)
        for blk in range(s // T):
            rows = slice(blk * T, (blk + 1) * T)
            lf = _log_sigmoid(fl_ref[rows, :] + b_ref[...])
            c = _tri3(tri, lf) + carry
            carry = c[T - 1:T, :]
            for p in range(FOX_HEADS // 2):
                fq_ref[rows, p * LANES:(p + 1) * LANES] = jnp.where(
                    m0, c[:, 2 * p:2 * p + 1], c[:, 2 * p + 1:2 * p + 2])
            fr_ref[:, rows] = c.T[0:FOX_HEADS, :]
            fb_ref[2 * blk:2 * blk + 1, :] = c[0:1, :]
            fb_ref[2 * blk + 1:2 * blk + 2, :] = carry

    base = 3 * SB_W // FX_W
    return pl.pallas_call(
        body, name=name, grid=(1,),
        in_specs=[pl.BlockSpec((s, LANES), lambda i: (0, MIX_W // LANES)),
                  pl.BlockSpec((s, FX_W), lambda i: (0, base)),
                  pl.BlockSpec((s, FX_W), lambda i: (0, base + 1)),
                  pl.BlockSpec((1, LANES), lambda i: (0, 0))],
        out_specs=[pl.BlockSpec((s, FX_W), lambda i: (0, 0)),
                   pl.BlockSpec((FOX_HEADS, s), lambda i: (0, 0)),
                   pl.BlockSpec((nrow, LANES), lambda i: (0, 0))],
        out_shape=[jax.ShapeDtypeStruct((s, FX_W), F32), jax.ShapeDtypeStruct((FOX_HEADS, s), F32),
                   jax.ShapeDtypeStruct((nrow, LANES), F32)],
        compiler_params=_ARB1,
    )(gf, qkv, qkv, bpad)


def _sb_fwd(qkv, name, rider=None):
    s = qkv.shape[0]

    def body(q_ref, k_ref, v_ref, o_ref, acc_ref, r_ref, as_ref):
        i = pl.program_id(1)
        m0 = _pair_masks()
        qh = _split_pair(q_ref[...] * jnp.asarray(Q_SCALE, BF16), m0)
        strict = _iota2((T, T), 0) > _iota2((T, T), 1)
        u2 = _stack2(jnp.where(strict, 1.0, 0.0).astype(BF16))
        acc_ref[...] = jnp.zeros_like(acc_ref)
        r_ref[...] = jnp.zeros_like(r_ref)
        hs = range(2)

        def flush(j):
            v = v_ref[pl.ds(pl.multiple_of(j * T, T), T), :]
            for h in hs:
                acc_ref[h] += _dot(as_ref[h], v)

        def tile(j, diag):
            k = k_ref[pl.ds(pl.multiple_of(j * T, T), T), :]
            z = [_dot_nt(qh[h], k) for h in hs]
            if not diag:
                flush(j + 1)
            la = [jnp.minimum(z[h], 0.0) - jnp.log(1.0 + jnp.exp(-jnp.abs(z[h]))) for h in hs]
            lf = [la[h] - z[h] for h in hs]
            if diag:
                lf = [jnp.where(strict, lf[h], 0.0) for h in hs]
            cin = [_cum2(lf[h], u2) for h in hs]
            a = [jnp.exp(la[h] + cin[h] + r_ref[h]) for h in hs]
            if diag:
                a = [jnp.where(strict, a[h], 0.0) for h in hs]
            for h in hs:
                r_ref[h] += cin[h][:, 0:1] + lf[h][:, 0:1]
                as_ref[h] = a[h].astype(BF16)

        tile(i, True)

        def more(state):
            j, top = state
            return jnp.logical_and(j >= 0, top > UNDERFLOW)

        def step(state):
            j, _ = state
            tile(j, False)
            return j - 1, _running_top(r_ref)

        j_left, _ = lax.while_loop(more, step, (i - 1, _running_top(r_ref)))
        flush(j_left + 1)
        o_ref[...] = jnp.where(m0, acc_ref[0], acc_ref[1])

    nb = SB_W // LANES
    (ysb,), lands = _ride(dict(
        body=body, name=name, grid=(nb, s // T),
        in_specs=[pl.BlockSpec((T, LANES), lambda p, i: (i, p)),
                  pl.BlockSpec((s, LANES), lambda p, i: (0, nb + p)),
                  pl.BlockSpec((s, LANES), lambda p, i: (0, 2 * nb + p))],
        out_specs=[pl.BlockSpec((T, LANES), lambda p, i: (i, p))],
        out_shape=[jax.ShapeDtypeStruct((s, SB_W), F32)],
        scratch_shapes=[pltpu.VMEM((2, T, LANES), F32), pltpu.VMEM((2, T, 1), F32),
                        pltpu.VMEM((2, T, T), BF16)],
        compiler_params=_ARB2, operands=[qkv, qkv, qkv]), rider)
    return ysb, lands


def _fox_fwd(qkv, fqb, frow, fbounds, name, rider=None):
    s = qkv.shape[0]

    def body(q_ref, k_ref, v_ref, fq_ref, fr_ref, fb_ref, o_ref, lse_ref, acc_ref, m_ref, ps_ref):
        pair = pl.program_id(0)
        i = pl.program_id(1)
        m0 = _pair_masks()
        qh = _split_pair(q_ref[...] * jnp.asarray(Q_SCALE, BF16), m0)
        fq = fq_ref[...]
        fqh = (fq[:, 0:1], fq[:, HEAD_DIM:HEAD_DIM + 1])
        causal = _iota2((T, T), 0) >= _iota2((T, T), 1)
        ones = jnp.ones((T, LANES), BF16)
        acc_ref[...] = jnp.zeros_like(acc_ref)
        m_ref[...] = jnp.full_like(m_ref, NEG)
        hs = range(2)

        def flush(j):
            v = v_ref[pl.ds(pl.multiple_of(j * T, T), T), :]
            va2 = _stack2(jnp.concatenate([v, ones], axis=1))
            for h in hs:
                acc_ref[h] += _dot(ps_ref[h], va2)

        def tile(j, diag):
            off = pl.multiple_of(j * T, T)
            k = k_ref[pl.ds(off, T), :]
            sc = [_dot_nt(qh[h], k) + fqh[h] - fr_ref[h:h + 1, pl.ds(off, T)] for h in hs]
            if not diag:
                flush(j + 1)
            if diag:
                sc = [jnp.where(causal, sc[h], NEG) for h in hs]
            m_new = [jnp.maximum(m_ref[h], jnp.max(sc[h], axis=1, keepdims=True)) for h in hs]
            p = [jnp.exp(sc[h] - m_new[h]) for h in hs]
            for h in hs:
                acc_ref[h] = acc_ref[h] * jnp.exp(m_ref[h] - m_new[h])
                m_ref[h] = m_new[h]
                p_hi, p_lo = _split2(p[h])
                ps_ref[h] = jnp.concatenate([p_hi, p_lo], axis=1)

        tile(i, True)
        j_left = _fox_tiles_left(i, pair, s // T, fb_ref, tile)
        flush(j_left + 1)
        acc = (acc_ref[0], acc_ref[1])
        o_ref[...] = jnp.where(m0, acc[0][:, :LANES] / acc[0][:, LANES:],
                               acc[1][:, :LANES] / acc[1][:, LANES:])
        lse_ref[...] = jnp.where(m0, m_ref[0] + jnp.log(acc[0][:, LANES:]),
                                 m_ref[1] + jnp.log(acc[1][:, LANES:]))

    nb = FX_W // LANES
    base = 3 * SB_W // LANES
    (yfx, lse), lands = _ride(dict(
        body=body, name=name, grid=(nb, s // T),
        in_specs=[pl.BlockSpec((T, LANES), lambda p, i: (i, base + p)),
                  pl.BlockSpec((s, LANES), lambda p, i: (0, base + nb + p)),
                  pl.BlockSpec((s, LANES), lambda p, i: (0, base + 2 * nb + p)),
                  pl.BlockSpec((T, LANES), lambda p, i: (i, p)),
                  pl.BlockSpec((None, 2, s), lambda p, i: (p, 0, 0)),
                  pl.BlockSpec(memory_space=pltpu.SMEM)],
        out_specs=[pl.BlockSpec((T, LANES), lambda p, i: (i, p)),
                   pl.BlockSpec((T, LANES), lambda p, i: (i, p))],
        out_shape=[jax.ShapeDtypeStruct((s, FX_W), F32), jax.ShapeDtypeStruct((s, FX_W), F32)],
        scratch_shapes=[pltpu.VMEM((2, T, 2 * LANES), F32), pltpu.VMEM((2, T, 1), F32),
                        pltpu.VMEM((2, T, 2 * T), BF16)],
        compiler_params=_ARB2, operands=[qkv, qkv, qkv, fqb, frow, fbounds]), rider)
    return yfx, lse, lands


def _memkv_fwd(mem, mnw, wkv, name):
    n = mem.shape[0]

    def body(mem_ref, mnw_ref, w_ref, kv_ref):
        mv = mem_ref[...]
        r = lax.rsqrt(jnp.mean(mv * mv, axis=-1, keepdims=True) + EPS)
        hm = (mv * r * mnw_ref[...]).astype(BF16)
        kv_ref[...] = _dot(hm, w_ref[...]).astype(BF16)

    return pl.pallas_call(
        body, name=name, grid=(1,),
        in_specs=[pl.BlockSpec((n, D_MODEL), lambda i: (0, 0)),
                  pl.BlockSpec((1, D_MODEL), lambda i: (0, 0)),
                  pl.BlockSpec((D_MODEL, 2 * MEM_W), lambda i: (0, 0))],
        out_specs=pl.BlockSpec((n, 2 * MEM_W), lambda i: (0, 0)),
        out_shape=jax.ShapeDtypeStruct((n, 2 * MEM_W), BF16),
        compiler_params=_ARB1,
    )(mem, mnw, wkv)


def _mem_fwd(qkv, kv, name):
    s = qkv.shape[0]
    n = kv.shape[0]

    def body(q_ref, k_ref, v_ref, o_ref, lse_ref):
        m0 = _pair_masks()
        qh = _split_pair(q_ref[...] * jnp.asarray(Q_SCALE, BF16), m0)
        k = k_ref[...]
        v = v_ref[...]
        outs, lses = [], []
        for h in range(2):
            sc = _dot_nt(qh[h], k)
            mx = jnp.max(sc, axis=1, keepdims=True)
            p = jnp.exp(sc - mx)
            l = jnp.sum(p, axis=1, keepdims=True)
            outs.append(_dot(p.astype(BF16), v) / l)
            lses.append(mx + jnp.log(l))
        o_ref[...] = jnp.where(m0, outs[0], outs[1])
        lse_ref[...] = jnp.where(m0, lses[0], lses[1])

    nb = MEM_W // LANES
    base = (3 * SB_W + 3 * FX_W) // LANES
    return pl.pallas_call(
        body, name=name, grid=(nb, s // TQM),
        in_specs=[pl.BlockSpec((TQM, LANES), lambda p, i: (i, base + p)),
                  pl.BlockSpec((n, LANES), lambda p, i: (0, p)),
                  pl.BlockSpec((n, LANES), lambda p, i: (0, nb + p))],
        out_specs=[pl.BlockSpec((TQM, LANES), lambda p, i: (i, p)),
                   pl.BlockSpec((TQM, LANES), lambda p, i: (i, p))],
        out_shape=[jax.ShapeDtypeStruct((s, MEM_W), F32), jax.ShapeDtypeStruct((s, MEM_W), F32)],
        compiler_params=_ARB2,
    )(qkv, kv, kv)


def _mix_chunk(c, ysb_ref, yfx_ref, ym_ref):
    if c < SB_W // LANES:
        return ysb_ref[:, c * LANES:(c + 1) * LANES]
    c -= SB_W // LANES
    if c < FX_W // LANES:
        return yfx_ref[:, c * LANES:(c + 1) * LANES]
    c -= FX_W // LANES
    return ym_ref[:, c * LANES:(c + 1) * LANES]


def _outproj_fwd(ysb, yfx, ym, gf, onw, wout, x, name):
    s = x.shape[0]

    def body(ysb_ref, yfx_ref, ym_ref, g_ref, onw_ref, w_ref, x_ref, o_ref, yg_ref):
        bd = _head_block_diag()
        for c in range(MIX_W // LANES):
            sl = slice(c * LANES, (c + 1) * LANES)
            u = _mix_chunk(c, ysb_ref, yfx_ref, ym_ref)
            r = lax.rsqrt(_head_mean(u * u, bd) + EPS)
            g = g_ref[:, sl]
            yg_ref[:, sl] = (u * r * onw_ref[:, sl] * (g * _sigmoid(g))).astype(BF16)
        o_ref[...] = x_ref[...] + _dot(yg_ref[...], w_ref[...])

    return pl.pallas_call(
        body, name=name, grid=(s // TM,),
        in_specs=[pl.BlockSpec((TM, SB_W), lambda i: (i, 0)),
                  pl.BlockSpec((TM, FX_W), lambda i: (i, 0)),
                  pl.BlockSpec((TM, MEM_W), lambda i: (i, 0)),
                  pl.BlockSpec((TM, MIX_W), lambda i: (i, 0)),
                  pl.BlockSpec((1, MIX_W), lambda i: (0, 0)),
                  pl.BlockSpec((MIX_W, D_MODEL), lambda i: (0, 0)),
                  pl.BlockSpec((TM, D_MODEL), lambda i: (i, 0))],
        out_specs=pl.BlockSpec((TM, D_MODEL), lambda i: (i, 0)),
        out_shape=jax.ShapeDtypeStruct((s, D_MODEL), F32),
        scratch_shapes=[pltpu.VMEM((TM, MIX_W), BF16)],
        compiler_params=_ARB1,
    )(ysb, yfx, ym, gf, onw, wout, x)


def _final_fwd_bwd(x, fnw, target, name):
    s = x.shape[0]

    def body(x_ref, w_ref, t_ref, dx_ref, loss_ref, dw_ref):
        @pl.when(pl.program_id(0) == 0)
        def _():
            loss_ref[...] = jnp.zeros_like(loss_ref)
            dw_ref[...] = jnp.zeros_like(dw_ref)

        xv = x_ref[...]
        w = w_ref[...]
        r = lax.rsqrt(jnp.mean(xv * xv, axis=-1, keepdims=True) + EPS)
        xh = xv * r
        err = xh * w - t_ref[...]
        part = jnp.sum(jnp.sum(err * err, axis=1, keepdims=True), axis=0, keepdims=True)
        loss_ref[...] += part * (0.5 / D_MODEL)
        dy = err * (1.0 / D_MODEL)
        dw_ref[...] += jnp.sum(dy * xh, axis=0, keepdims=True)
        dxh = dy * w
        dx_ref[...] = r * (dxh - xh * jnp.mean(dxh * xh, axis=-1, keepdims=True))

    return pl.pallas_call(
        body, name=name, grid=(s // TM,),
        in_specs=[pl.BlockSpec((TM, D_MODEL), lambda i: (i, 0)),
                  pl.BlockSpec((1, D_MODEL), lambda i: (0, 0)),
                  pl.BlockSpec((TM, D_MODEL), lambda i: (i, 0))],
        out_specs=[pl.BlockSpec((TM, D_MODEL), lambda i: (i, 0)),
                   pl.BlockSpec((1, LANES), lambda i: (0, 0)),
                   pl.BlockSpec((1, D_MODEL), lambda i: (0, 0))],
        out_shape=[jax.ShapeDtypeStruct((s, D_MODEL), F32), jax.ShapeDtypeStruct((1, LANES), F32),
                   jax.ShapeDtypeStruct((1, D_MODEL), F32)],
        compiler_params=_ARB1,
    )(x, fnw, target)


def _outproj_bwd(dxo, wout, ysb, yfx, ym, gf, onw, name):
    s = dxo.shape[0]

    def body(dx_ref, w_ref, ysb_ref, yfx_ref, ym_ref, g_ref, onw_ref,
             dysb_ref, dyfx_ref, dym_ref, dg_ref, dw_ref, donw_ref, yg_ref):
        @pl.when(pl.program_id(0) == 0)
        def _():
            dw_ref[...] = jnp.zeros_like(dw_ref)
            donw_ref[...] = jnp.zeros_like(donw_ref)

        dxb = dx_ref[...].astype(BF16)
        dyg = _dot_nt(dxb, w_ref[...])
        bd = _head_block_diag()
        for c in range(MIX_W // LANES):
            sl = slice(c * LANES, (c + 1) * LANES)
            u = _mix_chunk(c, ysb_ref, yfx_ref, ym_ref)
            r = lax.rsqrt(_head_mean(u * u, bd) + EPS)
            yn = u * r
            g = g_ref[:, sl]
            sg = _sigmoid(g)
            sil = g * sg
            onw = onw_ref[:, sl]
            e = dyg[:, sl]
            yg_ref[:, sl] = (yn * onw * sil).astype(BF16)
            donw_ref[:, sl] += jnp.sum(e * yn * sil, axis=0, keepdims=True)
            dg_ref[:, sl] = (e * yn * onw * (sg * (1.0 + g * (1.0 - sg)))).astype(BF16)
            dyn = e * onw * sil
            du = (r * (dyn - yn * _head_mean(dyn * yn, bd))).astype(BF16)
            if c < 4:
                dysb_ref[:, c * LANES:(c + 1) * LANES] = du
            elif c < 8:
                dyfx_ref[:, (c - 4) * LANES:(c - 3) * LANES] = du
            else:
                dym_ref[:, (c - 8) * LANES:(c - 7) * LANES] = du
        dw_ref[...] += _dot_tn(yg_ref[...], dxb)

    return pl.pallas_call(
        body, name=name, grid=(s // TM,),
        in_specs=[pl.BlockSpec((TM, D_MODEL), lambda i: (i, 0)),
                  pl.BlockSpec((MIX_W, D_MODEL), lambda i: (0, 0)),
                  pl.BlockSpec((TM, SB_W), lambda i: (i, 0)),
                  pl.BlockSpec((TM, FX_W), lambda i: (i, 0)),
                  pl.BlockSpec((TM, MEM_W), lambda i: (i, 0)),
                  pl.BlockSpec((TM, MIX_W), lambda i: (i, 0)),
                  pl.BlockSpec((1, MIX_W), lambda i: (0, 0))],
        out_specs=[pl.BlockSpec((TM, SB_W), lambda i: (i, 0)),
                   pl.BlockSpec((TM, FX_W), lambda i: (i, 0)),
                   pl.BlockSpec((TM, MEM_W), lambda i: (i, 0)),
                   pl.BlockSpec((TM, MIX_W), lambda i: (i, 0)),
                   pl.BlockSpec((MIX_W, D_MODEL), lambda i: (0, 0)),
                   pl.BlockSpec((1, MIX_W), lambda i: (0, 0))],
        out_shape=[jax.ShapeDtypeStruct((s, SB_W), BF16), jax.ShapeDtypeStruct((s, FX_W), BF16),
                   jax.ShapeDtypeStruct((s, MEM_W), BF16), jax.ShapeDtypeStruct((s, MIX_W), BF16),
                   jax.ShapeDtypeStruct((MIX_W, D_MODEL), F32), jax.ShapeDtypeStruct((1, MIX_W), F32)],
        scratch_shapes=[pltpu.VMEM((TM, MIX_W), BF16)],
        compiler_params=_ARB1,
    )(dxo, wout, ysb, yfx, ym, gf, onw)


def _row_dots(do_ref, o_ref, m0):
    prod = do_ref[...].astype(F32) * o_ref[...]
    zero = jnp.zeros_like(prod)
    return (jnp.sum(jnp.where(m0, prod, zero), axis=1, keepdims=True),
            jnp.sum(jnp.where(m0, zero, prod), axis=1, keepdims=True))


def _sb_bwd(qkv, o, do, name, rider=None):
    s = qkv.shape[0]
    nq = s // T

    def body(q_ref, k_ref, v_ref, o_ref, do_ref, dq_ref, dk_ref, dv_ref,
             dqa_ref, dka_ref, dva_ref, rl_ref, rg_ref, dzs_ref, abs_ref):
        i = pl.program_id(1)

        @pl.when(i == 0)
        def _():
            dka_ref[...] = jnp.zeros_like(dka_ref)
            dva_ref[...] = jnp.zeros_like(dva_ref)

        m0 = _pair_masks()
        qh = _split_pair(q_ref[...] * jnp.asarray(Q_SCALE, BF16), m0)
        doh = _split_pair(do_ref[...], m0)
        dsum = _row_dots(do_ref, o_ref, m0)
        strict = _iota2((T, T), 0) > _iota2((T, T), 1)
        u2 = _stack2(jnp.where(strict, 1.0, 0.0).astype(BF16))
        dqa_ref[...] = jnp.zeros_like(dqa_ref)
        rl_ref[...] = jnp.zeros_like(rl_ref)
        rg_ref[...] = jnp.zeros_like(rg_ref)

        hs = range(2)

        def flush(j):
            off = pl.multiple_of(j * T, T)
            k = k_ref[pl.ds(off, T), :]
            for h in hs:
                dqa_ref[h] += _dot(dzs_ref[h], k)
            dka_ref[pl.ds(off, T), :] += _dot_tn(dzs_ref[0], qh[0]) + _dot_tn(dzs_ref[1], qh[1])
            dva_ref[pl.ds(off, T), :] += _dot_tn(abs_ref[0], doh[0]) + _dot_tn(abs_ref[1], doh[1])

        def tile(j, diag):
            off = pl.multiple_of(j * T, T)
            k = k_ref[pl.ds(off, T), :]
            v = v_ref[pl.ds(off, T), :]
            z = [_dot_nt(qh[h], k) for h in hs]
            da = [_dot_nt(doh[h], v) for h in hs]
            if not diag:
                flush(j + 1)
            la = [jnp.minimum(z[h], 0.0) - jnp.log(1.0 + jnp.exp(-jnp.abs(z[h]))) for h in hs]
            lf = [la[h] - z[h] for h in hs]
            if diag:
                lf = [jnp.where(strict, lf[h], 0.0) for h in hs]
            cin = [_cum2(lf[h], u2) for h in hs]
            a = [jnp.exp(la[h] + cin[h] + rl_ref[h]) for h in hs]
            if diag:
                a = [jnp.where(strict, a[h], 0.0) for h in hs]
            ab = [a[h].astype(BF16) for h in hs]
            g = [ab[h].astype(F32) * da[h] for h in hs]
            gin = [_cum2(g[h], u2) for h in hs]
            dz = [g[h] - jnp.exp(la[h]) * ((dsum[h] - rg_ref[h]) - gin[h]) for h in hs]
            if diag:
                dz = [jnp.where(strict, dz[h], 0.0) for h in hs]
            for h in hs:
                rl_ref[h] += cin[h][:, 0:1] + lf[h][:, 0:1]
                rg_ref[h] += gin[h][:, 0:1] + g[h][:, 0:1]
                dzs_ref[h] = dz[h].astype(BF16)
                abs_ref[h] = ab[h]

        tile(i, True)

        def more(state):
            j, top = state
            return jnp.logical_and(j >= 0, top > UNDERFLOW)

        def step(state):
            j, _ = state
            tile(j, False)
            return j - 1, _running_top(rl_ref)

        j_left, _ = lax.while_loop(more, step, (i - 1, _running_top(rl_ref)))
        flush(j_left + 1)
        dq_ref[...] = (jnp.where(m0, dqa_ref[0], dqa_ref[1]) * Q_SCALE).astype(BF16)

        @pl.when(i == nq - 1)
        def _():
            dk_ref[...] = dka_ref[...].astype(BF16)
            dv_ref[...] = dva_ref[...].astype(BF16)

    nb = SB_W // LANES
    (dq, dk, dv), lands = _ride(dict(
        body=body, name=name, grid=(nb, nq),
        in_specs=[pl.BlockSpec((T, LANES), lambda p, i: (i, p)),
                  pl.BlockSpec((s, LANES), lambda p, i: (0, nb + p)),
                  pl.BlockSpec((s, LANES), lambda p, i: (0, 2 * nb + p)),
                  pl.BlockSpec((T, LANES), lambda p, i: (i, p)),
                  pl.BlockSpec((T, LANES), lambda p, i: (i, p))],
        out_specs=[pl.BlockSpec((T, LANES), lambda p, i: (i, p)),
                   pl.BlockSpec((s, LANES), lambda p, i: (0, p)),
                   pl.BlockSpec((s, LANES), lambda p, i: (0, p))],
        out_shape=[jax.ShapeDtypeStruct((s, SB_W), BF16)] * 3,
        scratch_shapes=[pltpu.VMEM((2, T, LANES), F32), pltpu.VMEM((s, LANES), F32),
                        pltpu.VMEM((s, LANES), F32), pltpu.VMEM((2, T, 1), F32),
                        pltpu.VMEM((2, T, 1), F32), pltpu.VMEM((2, T, T), BF16),
                        pltpu.VMEM((2, T, T), BF16)],
        compiler_params=_ARB2, operands=[qkv, qkv, qkv, o, do]), rider)
    return dq, dk, dv, lands


def _fox_bwd(qkv, fqb, frow, fbounds, o, lse, do, name, rider=None):
    s = qkv.shape[0]
    nq = s // T

    def body(q_ref, k_ref, v_ref, fq_ref, fr_ref, fb_ref, o_ref, lse_ref, do_ref,
             dq_ref, dk_ref, dv_ref, df_ref, dqa_ref, dka_ref, dva_ref, dfa_ref, dls_ref, pbs_ref):
        i = pl.program_id(1)

        @pl.when(i == 0)
        def _():
            dka_ref[...] = jnp.zeros_like(dka_ref)
            dva_ref[...] = jnp.zeros_like(dva_ref)
            dfa_ref[...] = jnp.zeros_like(dfa_ref)

        m0 = _pair_masks()
        qh = _split_pair(q_ref[...] * jnp.asarray(Q_SCALE, BF16), m0)
        doh = _split_pair(do_ref[...], m0)
        dsum = _row_dots(do_ref, o_ref, m0)
        fq = fq_ref[...]
        fqh = (fq[:, 0:1], fq[:, HEAD_DIM:HEAD_DIM + 1])
        lse = lse_ref[...]
        lseh = (lse[:, 0:1], lse[:, HEAD_DIM:HEAD_DIM + 1])
        causal = _iota2((T, T), 0) >= _iota2((T, T), 1)
        dqa_ref[...] = jnp.zeros_like(dqa_ref)

        hs = range(2)

        def flush(j):
            off = pl.multiple_of(j * T, T)
            k = k_ref[pl.ds(off, T), :]
            for h in hs:
                dqa_ref[h] += _dot(dls_ref[h], k)
            dka_ref[pl.ds(off, T), :] += _dot_tn(dls_ref[0], qh[0]) + _dot_tn(dls_ref[1], qh[1])
            dva_ref[pl.ds(off, T), :] += _dot_tn(pbs_ref[0], doh[0]) + _dot_tn(pbs_ref[1], doh[1])

        def tile(j, diag):
            off = pl.multiple_of(j * T, T)
            k = k_ref[pl.ds(off, T), :]
            v = v_ref[pl.ds(off, T), :]
            sc = [_dot_nt(qh[h], k) + fqh[h] - fr_ref[h:h + 1, pl.ds(off, T)] for h in hs]
            dp = [_dot_nt(doh[h], v) for h in hs]
            if not diag:
                flush(j + 1)
            p = [jnp.exp(sc[h] - lseh[h]) for h in hs]
            if diag:
                p = [jnp.where(causal, p[h], 0.0) for h in hs]
            dl = [p[h] * (dp[h] - dsum[h]) for h in hs]
            for h in hs:
                dls_ref[h] = dl[h].astype(BF16)
                pbs_ref[h] = p[h].astype(BF16)
                dfa_ref[h:h + 1, pl.ds(off, T)] -= jnp.sum(dl[h], axis=0, keepdims=True)

        tile(i, True)
        j_left = _fox_tiles_left(i, pl.program_id(0), nq, fb_ref, tile)
        flush(j_left + 1)
        dq_ref[...] = (jnp.where(m0, dqa_ref[0], dqa_ref[1]) * Q_SCALE).astype(BF16)

        @pl.when(i == nq - 1)
        def _():
            dk_ref[...] = dka_ref[...].astype(BF16)
            dv_ref[...] = dva_ref[...].astype(BF16)
            df_ref[...] = dfa_ref[...]

    nb = FX_W // LANES
    base = 3 * SB_W // LANES
    (dq, dk, dv, df), lands = _ride(dict(
        body=body, name=name, grid=(nb, nq),
        in_specs=[pl.BlockSpec((T, LANES), lambda p, i: (i, base + p)),
                  pl.BlockSpec((s, LANES), lambda p, i: (0, base + nb + p)),
                  pl.BlockSpec((s, LANES), lambda p, i: (0, base + 2 * nb + p)),
                  pl.BlockSpec((T, LANES), lambda p, i: (i, p)),
                  pl.BlockSpec((None, 2, s), lambda p, i: (p, 0, 0)),
                  pl.BlockSpec(memory_space=pltpu.SMEM),
                  pl.BlockSpec((T, LANES), lambda p, i: (i, p)),
                  pl.BlockSpec((T, LANES), lambda p, i: (i, p)),
                  pl.BlockSpec((T, LANES), lambda p, i: (i, p))],
        out_specs=[pl.BlockSpec((T, LANES), lambda p, i: (i, p)),
                   pl.BlockSpec((s, LANES), lambda p, i: (0, p)),
                   pl.BlockSpec((s, LANES), lambda p, i: (0, p)),
                   pl.BlockSpec((None, 2, s), lambda p, i: (p, 0, 0))],
        out_shape=[jax.ShapeDtypeStruct((s, FX_W), BF16)] * 3
        + [jax.ShapeDtypeStruct((nb, 2, s), F32)],
        scratch_shapes=[pltpu.VMEM((2, T, LANES), F32), pltpu.VMEM((s, LANES), F32),
                        pltpu.VMEM((s, LANES), F32), pltpu.VMEM((2, s), F32),
                        pltpu.VMEM((2, T, T), BF16), pltpu.VMEM((2, T, T), BF16)],
        compiler_params=_ARB2, operands=[qkv, qkv, qkv, fqb, frow, fbounds, o, lse, do]), rider)
    return dq, dk, dv, df, lands


def _fox_prep_bwd(dfrow, gf, bpad, name):
    s = gf.shape[0]

    def body(df_ref, fl_ref, b_ref, dfl_ref, db_ref):
        tri = jnp.where(_iota2((T, T), 0) <= _iota2((T, T), 1), 1.0, 0.0).astype(BF16)
        carry = jnp.zeros((1, LANES), F32)
        db = jnp.zeros((1, LANES), F32)
        fill = jnp.zeros((LANES - FOX_HEADS, T), F32)
        for blk in reversed(range(s // T)):
            rows = slice(blk * T, (blk + 1) * T)
            c = _tri3(tri, jnp.concatenate([df_ref[:, rows], fill], axis=0), _dot_nt) + carry
            carry = c[0:1, :]
            dfl = c / (1.0 + jnp.exp(fl_ref[rows, :] + b_ref[...]))
            dfl_ref[rows, :] = dfl.astype(BF16)
            db = db + jnp.sum(dfl, axis=0, keepdims=True)
        db_ref[...] = db

    return pl.pallas_call(
        body, name=name, grid=(1,),
        in_specs=[pl.BlockSpec((FOX_HEADS, s), lambda i: (0, 0)),
                  pl.BlockSpec((s, LANES), lambda i: (0, MIX_W // LANES)),
                  pl.BlockSpec((1, LANES), lambda i: (0, 0))],
        out_specs=[pl.BlockSpec((s, LANES), lambda i: (0, 0)),
                   pl.BlockSpec((1, LANES), lambda i: (0, 0))],
        out_shape=[jax.ShapeDtypeStruct((s, LANES), BF16), jax.ShapeDtypeStruct((1, LANES), F32)],
        compiler_params=_ARB1,
    )(dfrow, gf, bpad)


def _mem_bwd(qkv, kv, o, lse, do, name):
    s = qkv.shape[0]
    n = kv.shape[0]

    def body(q_ref, k_ref, v_ref, o_ref, lse_ref, do_ref, dq_ref, dk_ref, dv_ref):
        @pl.when(pl.program_id(1) == 0)
        def _():
            dk_ref[...] = jnp.zeros_like(dk_ref)
            dv_ref[...] = jnp.zeros_like(dv_ref)

        m0 = _pair_masks()
        qh = _split_pair(q_ref[...] * jnp.asarray(Q_SCALE, BF16), m0)
        doh = _split_pair(do_ref[...], m0)
        dsum = _row_dots(do_ref, o_ref, m0)
        lse = lse_ref[...]
        lseh = (lse[:, 0:1], lse[:, HEAD_DIM:HEAD_DIM + 1])
        k = k_ref[...]
        v = v_ref[...]
        dqs = []
        for h in range(2):
            p = jnp.exp(_dot_nt(qh[h], k) - lseh[h])
            dl = p * (_dot_nt(doh[h], v) - dsum[h])
            dlb = dl.astype(BF16)
            dqs.append(_dot(dlb, k))
            dk_ref[...] += _dot_tn(dlb, qh[h])
            dv_ref[...] += _dot_tn(p.astype(BF16), doh[h])
        dq_ref[...] = (jnp.where(m0, dqs[0], dqs[1]) * Q_SCALE).astype(BF16)

    nb = MEM_W // LANES
    base = (3 * SB_W + 3 * FX_W) // LANES
    return pl.pallas_call(
        body, name=name, grid=(nb, s // TQM),
        in_specs=[pl.BlockSpec((TQM, LANES), lambda p, i: (i, base + p)),
                  pl.BlockSpec((n, LANES), lambda p, i: (0, p)),
                  pl.BlockSpec((n, LANES), lambda p, i: (0, nb + p)),
                  pl.BlockSpec((TQM, LANES), lambda p, i: (i, p)),
                  pl.BlockSpec((TQM, LANES), lambda p, i: (i, p)),
                  pl.BlockSpec((TQM, LANES), lambda p, i: (i, p))],
        out_specs=[pl.BlockSpec((TQM, LANES), lambda p, i: (i, p)),
                   pl.BlockSpec((n, LANES), lambda p, i: (0, p)),
                   pl.BlockSpec((n, LANES), lambda p, i: (0, p))],
        out_shape=[jax.ShapeDtypeStruct((s, MEM_W), BF16), jax.ShapeDtypeStruct((n, MEM_W), F32),
                   jax.ShapeDtypeStruct((n, MEM_W), F32)],
        compiler_params=_ARB2,
    )(qkv, kv, kv, o, lse, do)


def _memkv_bwd(mem, mnw, wkv, dk, dv, name):
    n = mem.shape[0]

    def body(mem_ref, mnw_ref, w_ref, dk_ref, dv_ref, dw_ref, dmnw_ref):
        mv = mem_ref[...]
        r = lax.rsqrt(jnp.mean(mv * mv, axis=-1, keepdims=True) + EPS)
        mh = mv * r
        hm = (mh * mnw_ref[...]).astype(BF16)
        dkv = jnp.concatenate([dk_ref[...], dv_ref[...]], axis=1).astype(BF16)
        dw_ref[...] = _dot_tn(hm, dkv)
        dhm = _dot_nt(dkv, w_ref[...])
        dmnw_ref[...] = jnp.sum(dhm * mh, axis=0, keepdims=True)

    return pl.pallas_call(
        body, name=name, grid=(1,),
        in_specs=[pl.BlockSpec((n, D_MODEL), lambda i: (0, 0)),
                  pl.BlockSpec((1, D_MODEL), lambda i: (0, 0)),
                  pl.BlockSpec((D_MODEL, 2 * MEM_W), lambda i: (0, 0)),
                  pl.BlockSpec((n, MEM_W), lambda i: (0, 0)),
                  pl.BlockSpec((n, MEM_W), lambda i: (0, 0))],
        out_specs=[pl.BlockSpec((D_MODEL, 2 * MEM_W), lambda i: (0, 0)),
                   pl.BlockSpec((1, D_MODEL), lambda i: (0, 0))],
        out_shape=[jax.ShapeDtypeStruct((D_MODEL, 2 * MEM_W), F32),
                   jax.ShapeDtypeStruct((1, D_MODEL), F32)],
        compiler_params=_ARB1,
    )(mem, mnw, wkv, dk, dv)


def _inproj_bwd_dx(pieces, w_r, x, nw, dxo, name):
    s = x.shape[0]
    n = len(pieces)
    widths = [p.shape[1] for p in pieces]

    def body(*refs):
        piece_refs = refs[:n]
        w_ref, x_ref, nw_ref, dxo_ref, dx_ref, h_ref, dnw_ref, dp_ref = refs[n:]

        @pl.when(pl.program_id(0) == 0)
        def _():
            dnw_ref[...] = jnp.zeros_like(dnw_ref)

        col = 0
        for r, wd in zip(piece_refs, widths):
            dp_ref[:, col:col + wd] = r[...]
            col += wd
        dp_ref[:, col:] = jnp.zeros((TM, WR_W - col), BF16)
        dh = _dot(dp_ref[...], w_ref[...])
        xv = x_ref[...]
        nw = nw_ref[...]
        r = lax.rsqrt(jnp.mean(xv * xv, axis=-1, keepdims=True) + EPS)
        xh = xv * r
        h_ref[...] = (xh * nw).astype(BF16)
        dnw_ref[...] += jnp.sum(dh * xh, axis=0, keepdims=True)
        dxh = dh * nw
        dx_ref[...] = r * (dxh - xh * jnp.mean(dxh * xh, axis=-1, keepdims=True)) + dxo_ref[...]

    return pl.pallas_call(
        body, name=name, grid=(s // TM,),
        in_specs=[pl.BlockSpec((TM, wd), lambda i: (i, 0)) for wd in widths]
        + [pl.BlockSpec((WR_W, D_MODEL), lambda i: (0, 0)),
           pl.BlockSpec((TM, D_MODEL), lambda i: (i, 0)),
           pl.BlockSpec((1, D_MODEL), lambda i: (0, 0)),
           pl.BlockSpec((TM, D_MODEL), lambda i: (i, 0))],
        out_specs=[pl.BlockSpec((TM, D_MODEL), lambda i: (i, 0)),
                   pl.BlockSpec((TM, D_MODEL), lambda i: (i, 0)),
                   pl.BlockSpec((1, D_MODEL), lambda i: (0, 0)),
                   pl.BlockSpec((TM, WR_W), lambda i: (i, 0))],
        out_shape=[jax.ShapeDtypeStruct((s, D_MODEL), F32), jax.ShapeDtypeStruct((s, D_MODEL), BF16),
                   jax.ShapeDtypeStruct((1, D_MODEL), F32), jax.ShapeDtypeStruct((s, WR_W), BF16)],
        compiler_params=_ARB1,
    )(*pieces, w_r, x, nw, dxo)


def _inproj_bwd_dw(h, dproj, name):
    s = dproj.shape[0]
    tn = 256

    def body(h_ref, dp_ref, dw_ref):
        dw_ref[...] = _dot_tn(dp_ref[...], h_ref[...])

    return pl.pallas_call(
        body, name=name, grid=(WR_W // tn,),
        in_specs=[pl.BlockSpec((s, D_MODEL), lambda j: (0, 0)),
                  pl.BlockSpec((s, tn), lambda j: (0, j))],
        out_specs=pl.BlockSpec((tn, D_MODEL), lambda j: (j, 0)),
        out_shape=jax.ShapeDtypeStruct((WR_W, D_MODEL), F32),
        compiler_params=_ARB1,
    )(h, dproj)


def _rearrange_w_in(wt):
    pad = jnp.zeros((FL_PAD - FOX_HEADS,) + wt.shape[1:], wt.dtype)
    return jnp.concatenate([wt[:3072], wt[3080:3336], wt[3336:IN_W], wt[3072:3080], pad], axis=0)


def _restore_w_in(g):
    gate0 = QKV_W
    fl0 = QKV_W + MIX_W
    return jnp.concatenate(
        [g[:3072], g[fl0:fl0 + FOX_HEADS], g[3072:QKV_W], g[gate0:fl0]], axis=0)


def _pad_lanes(v, width=LANES):
    return jnp.pad(v, (0, width - v.shape[0])).reshape(1, width)


def _layer_fwd(xs, mem, nw, w_r, b_forget, mnw, late, onw, l, travel=None):
    s = xs.shape[0]
    bpad = _pad_lanes(b_forget)
    travel = _Travel(travel)
    qkv, gf, _ = travel.ride(0, _inproj_fwd, xs, nw, w_r, f"inproj_fwd_{l}")
    fqb, frow, fbounds = _fox_prep_fwd(gf, qkv, bpad, f"fox_prep_fwd_{l}")
    frow = frow.reshape(FOX_HEADS // 2, 2, s)
    ysb, _ = travel.ride(1, _sb_fwd, qkv, f"sb_fwd_{l}")
    yfx, lse_fx, _ = travel.ride(2, _fox_fwd, qkv, fqb, frow, fbounds, f"fox_fwd_{l}")
    wkv, wout = late(travel.lands)
    kv = _memkv_fwd(mem, mnw, wkv, f"memkv_fwd_{l}")
    ym, lse_m = _mem_fwd(qkv, kv, f"mem_fwd_{l}")
    xn = _outproj_fwd(ysb, yfx, ym, gf, onw, wout, xs, f"outproj_fwd_{l}")
    saved = (xs, nw, mnw, onw, bpad, qkv, gf, fqb, frow, fbounds, ysb, yfx, lse_fx, kv, ym, lse_m)
    return xn, saved, travel.lands, (wkv, wout)


class _Travel:
    def __init__(self, plan):
        self.plan = plan
        self.lands = None if plan is None else _new_lands(plan[0], plan[1])

    def ride(self, n, fn, *args):
        if self.plan is None or self.plan[2][n] is None:
            return fn(*args)
        srcs, scatter, legs = self.plan
        idx, rows = legs[n]
        out = fn(*args, rider=_Rider([srcs[a] for a in idx], [self.lands[a] for a in idx],
                                     scatter, rows))
        for a, land in zip(idx, out[-1]):
            self.lands[a] = land
        return out


def _layer_bwd(dx, saved, mem, w_r, wkv, wout, l, travel=None):
    xs, nw, mnw, onw, bpad, qkv, gf, fqb, frow, fbounds, ysb, yfx, lse_fx, kv, ym, lse_m = saved
    s = xs.shape[0]
    dysb, dyfx, dym, dgate, dwout, donw = _outproj_bwd(
        dx, wout, ysb, yfx, ym, gf, onw, f"outproj_bwd_{l}")
    travel = _Travel(None if travel is None else travel(dwout))
    sdq, sdk, sdv, _ = travel.ride(0, _sb_bwd, qkv, ysb, dysb, f"sb_bwd_{l}")
    fdq, fdk, fdv, dfrow, _ = travel.ride(1, _fox_bwd, qkv, fqb, frow, fbounds, yfx, lse_fx, dyfx,
                                          f"fox_bwd_{l}")
    dfl, db = _fox_prep_bwd(dfrow.reshape(FOX_HEADS, s), gf, bpad, f"fox_prep_bwd_{l}")
    dmq, dmk, dmv = _mem_bwd(qkv, kv, ym, lse_m, dym, f"mem_bwd_{l}")
    dwkv, dmnw = _memkv_bwd(mem, mnw, wkv, dmk, dmv, f"memkv_bwd_{l}")
    dx, ht, dnw, dproj = _inproj_bwd_dx([sdq, sdk, sdv, fdq, fdk, fdv, dmq, dgate, dfl],
                                        w_r, xs, nw, dx, f"inproj_bwd_dx_{l}")
    dwr = _inproj_bwd_dw(ht, dproj, f"inproj_bwd_dw_{l}")
    grads = dict(norm_w=dnw[0], w_r=dwr, b_forget=db[0, :FOX_HEADS], mem_norm_w=dmnw[0],
                 w_mem_kv=dwkv, out_norm_w=donw[0], w_out=dwout)
    return dx, grads, travel.lands


_ANY = pl.BlockSpec(memory_space=pl.ANY)


def _my_place():
    return lax.axis_index("x"), lax.axis_index("y"), lax.axis_index("c")


def _flip(v, bit):
    return 1 - v if bit else v


def _block_index(px, py, pc):
    return 4 * px + 2 * py + pc


def _all_gather_weights(shards, name):
    n = len(shards)

    def body(*refs):
        ins, outs = refs[:n], refs[n:2 * n]
        send_sems, recv_sems, local_sems = refs[2 * n:]
        x, y, c = _my_place()
        me = (x, y, c)
        sibling = (x, y, 1 - c)
        chips = [(1 - x, y), (x, 1 - y), (1 - x, 1 - y)]

        def copy(a, k, block, to, src=None):
            dst = outs[a].at[_block_index(*block)]
            return pltpu.make_async_remote_copy(
                src_ref=dst if src is None else src, dst_ref=dst,
                send_sem=send_sems.at[a, k], recv_sem=recv_sems.at[a, k],
                device_id=to, device_id_type=pl.DeviceIdType.MESH)

        mine = [pltpu.make_async_copy(ins[a], outs[a].at[_block_index(*me)], local_sems.at[a])
                for a in range(n)]
        for cp in mine:
            cp.start()
        first = []
        for a in range(n):
            first.append(copy(a, 0, me, sibling, src=ins[a]))
            first += [copy(a, 1 + j, me, (*chip, c), src=ins[a]) for j, chip in enumerate(chips)]
        for cp in first:
            cp.start()
        passed = []
        for j, chip in enumerate(chips):
            for a in range(n):
                copy(a, 1 + j, (*chip, c), me).wait_recv()
                fwd = copy(a, 4 + j, (*chip, c), sibling)
                fwd.start()
                passed.append(fwd)
        for a in range(n):
            copy(a, 0, sibling, me).wait_recv()
            for j, chip in enumerate(chips):
                copy(a, 4 + j, (*chip, 1 - c), me).wait_recv()
        for cp in first + passed:
            cp.wait_send()
        for cp in mine:
            cp.wait()

    return pl.pallas_call(
        body, name=name,
        in_specs=[_ANY] * n, out_specs=[_ANY] * n,
        out_shape=[jax.ShapeDtypeStruct((N_DEV,) + v.shape, v.dtype) for v in shards],
        scratch_shapes=[pltpu.SemaphoreType.DMA((n, 7)), pltpu.SemaphoreType.DMA((n, 7)),
                        pltpu.SemaphoreType.DMA((n,))],
    )(*shards)


def _exchange_blocks(blocked, name):
    n = len(blocked)

    def body(*refs):
        ins, outs = refs[:n], refs[n:2 * n]
        send_sems, recv_sems, local_sems = refs[2 * n:]
        x, y, c = _my_place()
        mine_idx = _block_index(x, y, c)
        local = [pltpu.make_async_copy(ins[a].at[mine_idx], outs[a].at[mine_idx], local_sems.at[a])
                 for a in range(n)]
        for cp in local:
            cp.start()
        sends, arrivals = [], []
        for r in range(1, N_DEV):
            peer = (_flip(x, r & 4), _flip(y, r & 2), _flip(c, r & 1))
            peer_idx = _block_index(*peer)
            for a in range(n):
                sems = dict(send_sem=send_sems.at[a, r - 1], recv_sem=recv_sems.at[a, r - 1],
                            device_id=peer, device_id_type=pl.DeviceIdType.MESH)
                sends.append(pltpu.make_async_remote_copy(
                    src_ref=ins[a].at[peer_idx], dst_ref=outs[a].at[mine_idx], **sems))
                arrivals.append(pltpu.make_async_remote_copy(
                    src_ref=ins[a].at[peer_idx], dst_ref=outs[a].at[peer_idx], **sems))
        for cp in sends:
            cp.start()
        for cp in arrivals:
            cp.wait_recv()
        for cp in sends:
            cp.wait_send()
        for cp in local:
            cp.wait()

    return pl.pallas_call(
        body, name=name,
        in_specs=[_ANY] * n, out_specs=[_ANY] * n,
        out_shape=[jax.ShapeDtypeStruct(v.shape, v.dtype) for v in blocked],
        scratch_shapes=[pltpu.SemaphoreType.DMA((n, 7)), pltpu.SemaphoreType.DMA((n, 7)),
                        pltpu.SemaphoreType.DMA((n,))],
    )(*blocked)


N_CHIP = N_DEV // 2


def _pair_swap(blocked, name):
    n = len(blocked)

    def body(*refs):
        ins, outs = refs[:n], refs[n:2 * n]
        send_sems, recv_sems = refs[2 * n:]
        x, y, c = _my_place()
        copies = [pltpu.make_async_remote_copy(
            src_ref=ins[a].at[j, 1 - c], dst_ref=outs[a].at[j],
            send_sem=send_sems.at[N_CHIP * a + j], recv_sem=recv_sems.at[N_CHIP * a + j],
            device_id=(x, y, 1 - c), device_id_type=pl.DeviceIdType.MESH)
            for a in range(n) for j in range(N_CHIP)]
        for cp in copies:
            cp.start()
        for cp in copies:
            cp.wait_recv()
        for cp in copies:
            cp.wait_send()

    return pl.pallas_call(
        body, name=name,
        in_specs=[_ANY] * n, out_specs=[_ANY] * n,
        out_shape=[jax.ShapeDtypeStruct((N_CHIP,) + v.shape[2:], v.dtype) for v in blocked],
        scratch_shapes=[pltpu.SemaphoreType.DMA((N_CHIP * n,)),
                        pltpu.SemaphoreType.DMA((N_CHIP * n,))],
    )(*blocked)


def _pair_add(mine, theirs, name):
    _, nrow, ncol = mine.shape

    def body(a_ref, b_ref, o_ref):
        o_ref[...] = (a_ref[...].astype(F32) + b_ref[...].astype(F32)).astype(BF16)

    blk = pl.BlockSpec((None, nrow, ncol), lambda j: (j, 0, 0))
    return pl.pallas_call(
        body, name=name, grid=(N_CHIP,), in_specs=[blk, blk], out_specs=blk,
        out_shape=jax.ShapeDtypeStruct(mine.shape, BF16), compiler_params=_ARB1,
    )(mine, theirs)


def _chip_exchange(by_chip, to_all, name):
    n, m = len(by_chip), len(to_all)

    def body(*refs):
        ins, alls = refs[:n], refs[n:n + m]
        outs, all_outs = refs[n + m:2 * n + m], refs[2 * n + m:2 * (n + m)]
        send_sems, recv_sems, local_sems = refs[2 * (n + m):]
        x, y, c = _my_place()
        my_chip, mine_idx = 2 * x + y, _block_index(x, y, c)
        local = [pltpu.make_async_copy(ins[a].at[my_chip], outs[a].at[my_chip], local_sems.at[a])
                 for a in range(n)]
        local += [pltpu.make_async_copy(alls[b].at[mine_idx], all_outs[b].at[mine_idx],
                                        local_sems.at[n + b]) for b in range(m)]
        for cp in local:
            cp.start()
        sends, arrivals = [], []
        k = 0
        for r in range(1, N_DEV):
            peer = (_flip(x, r & 4), _flip(y, r & 2), _flip(c, r & 1))
            peer_chip, peer_idx = 2 * peer[0] + peer[1], _block_index(*peer)
            pairs = [(alls[b].at[mine_idx], all_outs[b].at[mine_idx], all_outs[b].at[peer_idx])
                     for b in range(m)]
            if not r & 1:
                pairs += [(ins[a].at[peer_chip], outs[a].at[my_chip], outs[a].at[peer_chip])
                          for a in range(n)]
            for src, there, here in pairs:
                sems = dict(send_sem=send_sems.at[k], recv_sem=recv_sems.at[k], device_id=peer,
                            device_id_type=pl.DeviceIdType.MESH)
                sends.append(pltpu.make_async_remote_copy(src_ref=src, dst_ref=there, **sems))
                arrivals.append(pltpu.make_async_remote_copy(src_ref=src, dst_ref=here, **sems))
                k += 1
        for cp in sends:
            cp.start()
        for cp in arrivals:
            cp.wait_recv()
        for cp in sends:
            cp.wait_send()
        for cp in local:
            cp.wait()

    n_copies = 7 * m + 3 * n
    return pl.pallas_call(
        body, name=name,
        in_specs=[_ANY] * (n + m), out_specs=[_ANY] * (n + m),
        out_shape=[jax.ShapeDtypeStruct(v.shape, v.dtype) for v in by_chip + to_all],
        scratch_shapes=[pltpu.SemaphoreType.DMA((n_copies,)), pltpu.SemaphoreType.DMA((n_copies,)),
                        pltpu.SemaphoreType.DMA((n + m,))],
    )(*by_chip, *to_all)


class _Rider(NamedTuple):
    srcs: list
    lands: list
    scatter: bool
    part: list


def _new_lands(srcs, scatter):
    return [lax.empty(v.shape if scatter else (N_DEV,) + v.shape, v.dtype) for v in srcs]


def _rider_copies(srcs, lands, send_sems, recv_sems, rider):
    x, y, c = _my_place()
    mine_idx = _block_index(x, y, c)

    def window(ref, a):
        if rider.part[a] is None:
            return ref
        dim, start, size = rider.part[a]
        return ref.at[(slice(None),) * dim + (pl.ds(start, size),)]

    sends, arrivals = [], []
    for r in range(1, N_DEV):
        peer = (_flip(x, r & 4), _flip(y, r & 2), _flip(c, r & 1))
        peer_idx = _block_index(*peer)
        for a in range(len(srcs)):
            src = window(srcs[a].at[peer_idx] if rider.scatter else srcs[a], a)
            k = 7 * a + r - 1
            sems = dict(send_sem=send_sems.at[k], recv_sem=recv_sems.at[k],
                        device_id=peer, device_id_type=pl.DeviceIdType.MESH)
            sends.append(pltpu.make_async_remote_copy(
                src_ref=src, dst_ref=window(lands[a].at[mine_idx], a), **sems))
            arrivals.append(pltpu.make_async_remote_copy(
                src_ref=src, dst_ref=window(lands[a].at[peer_idx], a), **sems))
    return sends, arrivals


def _ride(call, rider):
    call = dict(call)
    body, grid = call.pop("body"), call["grid"]
    operands = call.pop("operands")
    if rider is None:
        return list(pl.pallas_call(body, **call)(*operands)), None
    n_in, n_out = len(call["in_specs"]), len(call["out_specs"])
    n_scratch = len(call["scratch_shapes"])
    m = len(rider.srcs)

    def riding(*refs):
        main_in, srcs, lands = refs[:n_in], refs[n_in:n_in + m], refs[n_in + m:n_in + 2 * m]
        main_out = refs[n_in + 2 * m:n_in + 2 * m + n_out]
        rest = refs[n_in + 3 * m + n_out:]
        send_sems, recv_sems = rest[n_scratch:]
        at = [pl.program_id(d) for d in range(len(grid))]
        first = functools.reduce(jnp.logical_and, [p == 0 for p in at])
        last = functools.reduce(jnp.logical_and, [p == g - 1 for p, g in zip(at, grid)])
        sends, arrivals = _rider_copies(srcs, lands, send_sems, recv_sems, rider)

        @pl.when(first)
        def _():
            for cp in sends:
                cp.start()

        body(*main_in, *main_out, *rest[:n_scratch])

        @pl.when(last)
        def _():
            for cp in arrivals:
                cp.wait_recv()
            for cp in sends:
                cp.wait_send()

    call["in_specs"] = list(call["in_specs"]) + [_ANY] * (2 * m)
    call["out_specs"] = list(call["out_specs"]) + [_ANY] * m
    call["out_shape"] = list(call["out_shape"]) + [
        jax.ShapeDtypeStruct(v.shape, v.dtype) for v in rider.lands]
    call["scratch_shapes"] = list(call["scratch_shapes"]) + [
        pltpu.SemaphoreType.DMA((7 * m,)), pltpu.SemaphoreType.DMA((7 * m,))]
    call["input_output_aliases"] = {n_in + m + a: n_out + a for a in range(m)}
    outs = pl.pallas_call(riding, **call)(*operands, *rider.srcs, *rider.lands)
    return list(outs[:n_out]), list(outs[n_out:])


def _sum_parts(p_ref):
    g = p_ref[0].astype(F32)
    for k in range(1, p_ref.shape[0]):
        g = g + p_ref[k].astype(F32)
    return g


def _adamw(g, w, m, v):
    c1 = 1.0 / (1.0 - ADAM_B1 ** ADAM_STEP)
    c2 = 1.0 / (1.0 - ADAM_B2 ** ADAM_STEP)
    nm = ADAM_B1 * m + (1.0 - ADAM_B1) * g
    nv = ADAM_B2 * v + (1.0 - ADAM_B2) * (g * g)
    return nm, nv, -ADAM_LR * ((nm * c1) / (jnp.sqrt(nv * c2) + ADAM_EPS) + ADAM_WD * w)


def _adamw_w_in(parts, w, m, v, name):
    ncol_blk, depth, nfeat = w.shape
    cols = 256

    def body(*refs):
        p_refs = refs[:depth]
        w_ref, m_ref, v_ref, g_ref, d_ref, nm_ref, nv_ref = refs[depth:]
        for l in range(depth):
            g = _sum_parts(p_refs[l])
            nm, nv, d = _adamw(g, w_ref[:, l, :], m_ref[:, l, :], v_ref[:, l, :])
            g_ref[:, l, :] = g
            nm_ref[:, l, :] = nm
            nv_ref[:, l, :] = nv
            d_ref[:, l, :] = d

    blk = pl.BlockSpec((ncol_blk, depth, cols), lambda j: (0, 0, j))
    return pl.pallas_call(
        body, name=name, grid=(nfeat // cols,),
        in_specs=[pl.BlockSpec((p.shape[0], ncol_blk, cols), lambda j: (0, 0, j)) for p in parts]
        + [blk] * 3,
        out_specs=[blk] * 4,
        out_shape=[jax.ShapeDtypeStruct(w.shape, F32)] * 4,
        compiler_params=_ARB1,
    )(*parts, w, m, v)


def _adamw_sum(parts, w, m, v, tile, name):
    depth, nrow, ncol = w.shape
    rows, cols = tile

    def body(*refs):
        p_refs = refs[:depth]
        w_ref, m_ref, v_ref, g_ref, d_ref, nm_ref, nv_ref = refs[depth:]
        layer = pl.program_id(0)
        for l in range(depth):
            @pl.when(layer == l)
            def _(p_ref=p_refs[l]):
                g = _sum_parts(p_ref)
                nm, nv, d = _adamw(g, w_ref[...], m_ref[...], v_ref[...])
                g_ref[...] = g
                nm_ref[...] = nm
                nv_ref[...] = nv
                d_ref[...] = d

    def part_spec(l):
        return pl.BlockSpec((parts[l].shape[0], rows, cols), lambda q, i, j: (
            0, jnp.where(q == l, i, 0), jnp.where(q == l, j, 0)))

    blk = pl.BlockSpec((None, rows, cols), lambda q, i, j: (q, i, j))
    return pl.pallas_call(
        body, name=name, grid=(depth, nrow // rows, ncol // cols),
        in_specs=[part_spec(l) for l in range(depth)] + [blk, blk, blk],
        out_specs=[blk] * 4,
        out_shape=[jax.ShapeDtypeStruct(w.shape, F32)] * 4,
        compiler_params=pltpu.CompilerParams(
            dimension_semantics=("arbitrary", "arbitrary", "arbitrary")),
    )(*parts, w, m, v)


def _pack_small(norm_w, mem_norm_w, out_norm_w, final_norm_w, b_forget):
    onw = jnp.pad(out_norm_w.reshape(20, LANES), ((0, 4), (0, 0)))
    b = jnp.pad(b_forget, ((0, 6), (0, LANES - FOX_HEADS)))
    return jnp.concatenate([norm_w.reshape(16, LANES), mem_norm_w.reshape(16, LANES), onw,
                            final_norm_w.reshape(8, LANES), b], axis=0)


def _unpack_small(p):
    return (p[0:16].reshape(2, D_MODEL), p[16:32].reshape(2, D_MODEL), p[32:52].reshape(2, MIX_W),
            p[56:64].reshape(D_MODEL), p[64:66, :FOX_HEADS])


def kernel(x, mem, norm_w, w_in, b_forget, mem_norm_w, w_mem_kv, out_norm_w, w_out, final_norm_w, loss_target, m_norm_w, m_w_in, m_b_forget, m_mem_norm_w, m_w_mem_kv, m_out_norm_w, m_w_out, m_final_norm_w, v_norm_w, v_w_in, v_b_forget, v_mem_norm_w, v_w_mem_kv, v_out_norm_w, v_w_out, v_final_norm_w):
    kv_rows = w_mem_kv.shape[1]
    out_rows = w_out.shape[1]
    me = _block_index(*_my_place())

    def shards(l):
        return [w_in[l].T.astype(BF16), w_mem_kv[l].astype(BF16), w_out[l].astype(BF16)]

    def full_in(g_in):
        return _rearrange_w_in(g_in.reshape(IN_W, D_MODEL))

    def full_kv_out(g_kv, g_out):
        return g_kv.reshape(D_MODEL, 2 * MEM_W), g_out.reshape(MIX_W, D_MODEL)

    def in_blocks(g):
        segments = [(0, 3072, 0), (3072, 3080, QKV_W + MIX_W), (3080, 3336, 3072),
                    (3336, IN_W, QKV_W)]

        def block(k):
            lo, hi = k * SHARD_W, (k + 1) * SHARD_W
            pieces = [g[at + max(lo, a) - a:at + min(hi, b) - a]
                      for a, b, at in segments if max(lo, a) < min(hi, b)]
            return jnp.concatenate(pieces, axis=0).astype(BF16)

        return jnp.stack([block(k) for k in range(N_DEV)])

    def kv_blocks(g):
        return g.reshape(N_DEV, kv_rows, 2 * MEM_W).astype(BF16)

    def out_blocks(g):
        return g.reshape(N_DEV, out_rows, D_MODEL).astype(BF16)

    def with_own(land, own):
        return lax.dynamic_update_slice(land, own[None], (me,) + (0,) * own.ndim)

    def with_own_of(land, blocked):
        return lax.dynamic_update_slice(land, lax.dynamic_slice_in_dim(blocked, me, 1, axis=0),
                                        (me,) + (0,) * (land.ndim - 1))

    def row(v):
        return v.reshape(1, -1)

    def cols(first, size):
        return (1, first, size)

    fwd_split, bwd_split = 5 * LANES, 6 * LANES

    s_in0, s_kv0, s_out0 = shards(0)
    (g_in0,) = _all_gather_weights([s_in0], "all_gather_l0")
    w_r0 = full_in(g_in0)
    s_in1, s_kv1, s_out1 = shards(1)
    x1, saved0, (l_in1, _, _), (wkv0, wout0) = _layer_fwd(
        x[0], mem[0], row(norm_w[0]), w_r0, b_forget[0], row(mem_norm_w[0]),
        lambda lands: full_kv_out(with_own(lands[1], s_kv0), with_own(lands[2], s_out0)),
        row(out_norm_w[0]), 0,
        travel=([s_in1, s_kv0, s_out0], False,
                [([1, 2], [None, None]), ([0], [cols(0, fwd_split)]),
                 ([0], [cols(fwd_split, D_MODEL - fwd_split)])]))
    w_r1 = full_in(with_own(l_in1, s_in1))
    x2, saved1, _, (wkv1, wout1) = _layer_fwd(
        x1, mem[0], row(norm_w[1]), w_r1, b_forget[1], row(mem_norm_w[1]),
        lambda lands: full_kv_out(with_own(lands[0], s_kv1), with_own(lands[1], s_out1)),
        row(out_norm_w[1]), 1,
        travel=([s_kv1, s_out1], False, [None, ([0, 1], [None, None]), None]))

    dx2, loss_part, dfnw = _final_fwd_bwd(x2, row(final_norm_w), loss_target[0], "final_fwd_bwd")

    dx1, gr1, (l_out1,) = _layer_bwd(
        dx2, saved1, mem[0], w_r1, wkv1, wout1, 1,
        travel=lambda dwout: ([out_blocks(dwout)], True, [([0], [None]), None]))
    p_in1, p_kv1 = in_blocks(gr1["w_r"]), kv_blocks(gr1["w_mem_kv"])
    grad_x, gr0, (l_in1, l_kv1, l_out0) = _layer_bwd(
        dx1, saved0, mem[0], w_r0, wkv0, wout0, 0,
        travel=lambda dwout: ([p_in1, p_kv1, out_blocks(dwout)], True,
                              [([0, 1], [cols(0, bwd_split), None]),
                               ([0, 2], [cols(bwd_split, D_MODEL - bwd_split), None])]))
    r_out1 = with_own_of(l_out1, out_blocks(gr1["w_out"]))
    r_in1, r_kv1 = with_own_of(l_in1, p_in1), with_own_of(l_kv1, p_kv1)
    r_out0 = with_own_of(l_out0, out_blocks(gr0["w_out"]))

    def both(name):
        return jnp.stack([gr0[name], gr1[name]])

    small = _pack_small(both("norm_w"), both("mem_norm_w"), both("out_norm_w"), dfnw[0],
                        both("b_forget")).at[LOSS_ROW].set(loss_part[0])
    p_small = jnp.broadcast_to(small[None], (N_DEV, SMALL_ROWS, LANES))
    by_core = [v.reshape((N_CHIP, 2) + v.shape[1:])
               for v in (in_blocks(gr0["w_r"]), kv_blocks(gr0["w_mem_kv"]))]
    from_sibling = _pair_swap(by_core, "grads_l0_pair_swap")
    core = lax.axis_index("c")
    chip_sums = [_pair_add(lax.dynamic_index_in_dim(v, core, axis=1, keepdims=False), got,
                           f"grads_l0_pair_add_{a}")
                 for a, (v, got) in enumerate(zip(by_core, from_sibling))]
    r_in0, r_kv0, r_small = _chip_exchange(chip_sums, [p_small], "exchange_grads_l0")

    def view(v):
        return jnp.transpose(v, (2, 0, 1))

    g_w_in, d_w_in, nm_w_in, nv_w_in = [jnp.transpose(v, (1, 2, 0)) for v in _adamw_w_in(
        [r_in0, r_in1], view(w_in), view(m_w_in), view(v_w_in), "adamw_w_in")]
    g_w_kv, d_w_kv, nm_w_kv, nv_w_kv = _adamw_sum(
        [r_kv0, r_kv1], w_mem_kv, m_w_mem_kv, v_w_mem_kv, (kv_rows, 2 * MEM_W), "adamw_w_mem_kv")
    g_w_out, d_w_out, nm_w_out, nv_w_out = _adamw_sum(
        [r_out0, r_out1], w_out, m_w_out, v_w_out, (out_rows, D_MODEL), "adamw_w_out")
    w_small = _pack_small(norm_w, mem_norm_w, out_norm_w, final_norm_w, b_forget)[None]
    m_small = _pack_small(m_norm_w, m_mem_norm_w, m_out_norm_w, m_final_norm_w, m_b_forget)[None]
    v_small = _pack_small(v_norm_w, v_mem_norm_w, v_out_norm_w, v_final_norm_w, v_b_forget)[None]
    small_out = _adamw_sum([r_small], w_small, m_small, v_small, (SMALL_ROWS, LANES), "adamw_small")
    (g_nw, g_mnw, g_onw, g_fnw, g_b), (d_nw, d_mnw, d_onw, d_fnw, d_b), \
        (nm_nw, nm_mnw, nm_onw, nm_fnw, nm_b), (nv_nw, nv_mnw, nv_onw, nv_fnw, nv_b) = [
            _unpack_small(t[0]) for t in small_out]
    loss = small_out[0][0, LOSS_ROW, 0]

    return (loss, grad_x[None],
            g_nw, g_w_in, g_b, g_mnw, g_w_kv, g_onw, g_w_out, g_fnw,
            d_nw, d_w_in, d_b, d_mnw, d_w_kv, d_onw, d_w_out, d_fnw,
            nm_nw, nm_w_in, nm_b, nm_mnw, nm_w_kv, nm_onw, nm_w_out, nm_fnw,
            nv_nw, nv_w_in, nv_b, nv_mnw, nv_w_kv, nv_onw, nv_w_out, nv_fnw)
```

```python
import functools
from typing import NamedTuple

import jax
import jax.numpy as jnp
from jax import lax
from jax.experimental import pallas as pl
from jax.experimental.pallas import tpu as pltpu

F32 = jnp.float32
BF16 = jnp.bfloat16

N_DEV = 8
D_MODEL = 1024
HEAD_DIM = 64
LANES = 128
SB_W = 512
FX_W = 512
MEM_W = 256
MIX_W = 1280
FOX_HEADS = 8
IN_W = 4616
SHARD_W = IN_W // N_DEV
QKV_W = 3 * SB_W + 3 * FX_W + MEM_W
FL_PAD = 256
GF_W = MIX_W + FL_PAD
WR_W = QKV_W + GF_W
EPS = 1e-6
T = 256
QPS = 2
TM = 256
TQM = 512
Q_SCALE = 0.125
NEG = -1e30
UNDERFLOW = -110.0
NORM_SLACK = 1.01

ADAM_LR = 0.001
ADAM_B1 = 0.9
ADAM_B2 = 0.999
ADAM_EPS = 1e-08
ADAM_WD = 0.01
ADAM_STEP = 10

SMALL_ROWS = 72
LOSS_ROW = 66

_NT = (((1,), (1,)), ((), ()))
_TN = (((0,), (0,)), ((), ()))

_ARB1 = pltpu.CompilerParams(dimension_semantics=("arbitrary",))
_ARB2 = pltpu.CompilerParams(dimension_semantics=("arbitrary", "arbitrary"))


def _dot(a, b):
    return jnp.dot(a, b, preferred_element_type=F32)


def _dot_nt(a, b):
    return lax.dot_general(a, b, _NT, preferred_element_type=F32)


def _dot_tn(a, b):
    return lax.dot_general(a, b, _TN, preferred_element_type=F32)


def _split2(x):
    hi = x.astype(BF16)
    lo = (x - hi.astype(F32)).astype(BF16)
    return hi, lo


def _stack2(u):
    return jnp.concatenate([u, u], axis=0)


def _cum2(x, u2):
    hi, lo = _split2(x)
    return _dot(jnp.concatenate([hi, lo], axis=1), u2)


def _tri3(tri, x, dot=None):
    dot = dot or _dot
    hi = x.astype(BF16)
    r1 = x - hi.astype(F32)
    mid = r1.astype(BF16)
    lo = (r1 - mid.astype(F32)).astype(BF16)
    return dot(tri, hi) + dot(tri, mid) + dot(tri, lo)


def _iota2(shape, dim):
    return lax.broadcasted_iota(jnp.int32, shape, dim)


def _head_block_diag():
    r = _iota2((LANES, LANES), 0) // HEAD_DIM
    c = _iota2((LANES, LANES), 1) // HEAD_DIM
    return _stack2(jnp.where(r == c, 1.0, 0.0).astype(BF16))


def _head_mean(x, bd):
    return _cum2(x, bd) * (1.0 / HEAD_DIM)


def _sigmoid(x):
    return 1.0 / (1.0 + jnp.exp(-x))


def _log_sigmoid(x):
    return jnp.minimum(x, 0.0) - jnp.log(1.0 + jnp.exp(-jnp.abs(x)))


def _running_top(r_ref):
    return jnp.max(jnp.maximum(r_ref[0], r_ref[1]))


def _fox_tiles_left(i, pair, nq, fb_ref, tile):
    def bound(j):
        b = []
        for h in range(2):
            head = 2 * pair + h
            b.append(2.0 * NORM_SLACK * fb_ref[2 * nq + i, head] * fb_ref[3 * nq, head]
                     + fb_ref[2 * i, head] - fb_ref[2 * j + 1, head])
        return jnp.maximum(b[0], b[1])

    def more(j):
        return jnp.logical_and(j >= 0, bound(jnp.maximum(j, 0)) > UNDERFLOW)

    def step(j):
        tile(j, False)
        return j - 1

    return lax.while_loop(more, step, i - 1)


def _pair_masks():
    lane = _iota2((1, LANES), 1)
    return lane < HEAD_DIM


def _split_pair(x, m0):
    zero = jnp.zeros_like(x)
    return jnp.where(m0, x, zero), jnp.where(m0, zero, x)


def _inproj_fwd(x, nw, w_r, name):
    s = x.shape[0]

    def body(x_ref, nw_ref, w_ref, qkv_ref, gf_ref):
        xv = x_ref[...]
        r = lax.rsqrt(jnp.mean(xv * xv, axis=-1, keepdims=True) + EPS)
        h = (xv * r * nw_ref[...]).astype(BF16)
        for c in range(0, QKV_W, 256):
            qkv_ref[:, c:c + 256] = _dot_nt(h, w_ref[c:c + 256, :]).astype(BF16)
        for c in range(0, GF_W, 256):
            gf_ref[:, c:c + 256] = _dot_nt(h, w_ref[QKV_W + c:QKV_W + c + 256, :])

    return pl.pallas_call(
        body, name=name, grid=(s // TM,),
        in_specs=[pl.BlockSpec((TM, D_MODEL), lambda i: (i, 0)),
                  pl.BlockSpec((1, D_MODEL), lambda i: (0, 0)),
                  pl.BlockSpec((WR_W, D_MODEL), lambda i: (0, 0))],
        out_specs=[pl.BlockSpec((TM, QKV_W), lambda i: (i, 0)),
                   pl.BlockSpec((TM, GF_W), lambda i: (i, 0))],
        out_shape=[jax.ShapeDtypeStruct((s, QKV_W), BF16), jax.ShapeDtypeStruct((s, GF_W), F32)],
        compiler_params=_ARB1,
    )(x, nw, w_r)


def _fox_prep_fwd(gf, qkv, bpad, name):
    s = gf.shape[0]
    nq = s // T
    nrow = -(-(3 * nq + 1) // 8) * 8

    def body(fl_ref, q_ref, k_ref, b_ref, fq_ref, fr_ref, fb_ref):
        tri = jnp.where(_iota2((T, T), 0) >= _iota2((T, T), 1), 1.0, 0.0).astype(BF16)
        m0 = _pair_masks()
        lane = _iota2((1, LANES), 1)
        norms = [jnp.zeros((1, LANES), F32) for _ in range(nq + 1)]
        same_head = (_iota2((LANES, LANES), 0) // HEAD_DIM) == (_iota2((LANES, LANES), 1) // HEAD_DIM)
        bd = jnp.where(same_head, 1.0, 0.0).astype(BF16)
        for p in range(FOX_HEADS // 2):
            cols = slice(p * LANES, (p + 1) * LANES)
            q = (q_ref[:, cols] * jnp.asarray(Q_SCALE, BF16)).astype(F32)
            k = k_ref[:, cols].astype(F32)
            qn = _dot((q * q).astype(BF16), bd)
            kn = _dot((k * k).astype(BF16), bd)
            tops = [jnp.max(qn[j * T:(j + 1) * T], axis=0, keepdims=True) for j in range(nq)]
            tops.append(jnp.max(kn, axis=0, keepdims=True))
            tops = [jnp.sqrt(top) for top in tops]
            for h in range(2):
                at = h * HEAD_DIM
                norms = [jnp.where(lane == 2 * p + h, top[:, at:at + 1], row)
                         for top, row in zip(tops, norms)]
        for j in range(nq + 1):
            fb_ref[2 * nq + j:2 * nq + j + 1, :] = norms[j]
        fb_ref[3 * nq + 1:, :] = jnp.zeros((nrow - 3 * nq - 1, LANES), F32)
        carry = jnp.zeros((1, LANES), F32)
        for blk in range(s // T):
            rows = slice(blk * T, (blk + 1) * T)
            lf = _log_sigmoid(fl_ref[rows, :] + b_ref[...])
            c = _tri3(tri, lf) + carry
            carry = c[T - 1:T, :]
            for p in range(FOX_HEADS // 2):
                fq_ref[rows, p * LANES:(p + 1) * LANES] = jnp.where(
                    m0, c[:, 2 * p:2 * p + 1], c[:, 2 * p + 1:2 * p + 2])
            fr_ref[:, rows] = c.T[0:FOX_HEADS, :]
            fb_ref[2 * blk:2 * blk + 1, :] = c[0:1, :]
            fb_ref[2 * blk + 1:2 * blk + 2, :] = carry

    base = 3 * SB_W // FX_W
    return pl.pallas_call(
        body, name=name, grid=(1,),
        in_specs=[pl.BlockSpec((s, LANES), lambda i: (0, MIX_W // LANES)),
                  pl.BlockSpec((s, FX_W), lambda i: (0, base)),
                  pl.BlockSpec((s, FX_W), lambda i: (0, base + 1)),
                  pl.BlockSpec((1, LANES), lambda i: (0, 0))],
        out_specs=[pl.BlockSpec((s, FX_W), lambda i: (0, 0)),
                   pl.BlockSpec((FOX_HEADS, s), lambda i: (0, 0)),
                   pl.BlockSpec((nrow, LANES), lambda i: (0, 0))],
        out_shape=[jax.ShapeDtypeStruct((s, FX_W), F32), jax.ShapeDtypeStruct((FOX_HEADS, s), F32),
                   jax.ShapeDtypeStruct((nrow, LANES), F32)],
        compiler_params=_ARB1,
    )(gf, qkv, qkv, bpad)


def _sb_fwd(qkv, name, rider=None):
    s = qkv.shape[0]

    def body(q_ref, k_ref, v_ref, o_ref, acc_ref, r_ref, as_ref):
        m0 = _pair_masks()
        strict = _iota2((T, T), 0) > _iota2((T, T), 1)
        u2 = _stack2(jnp.where(strict, 1.0, 0.0).astype(BF16))
        hs = range(2)

        def query_tile(i, rows):
            qh = _split_pair(q_ref[rows, :] * jnp.asarray(Q_SCALE, BF16), m0)
            acc_ref[...] = jnp.zeros_like(acc_ref)
            r_ref[...] = jnp.zeros_like(r_ref)

            def flush(j):
                v = v_ref[pl.ds(pl.multiple_of(j * T, T), T), :]
                for h in hs:
                    acc_ref[h] += _dot(as_ref[h], v)

            def tile(j, diag):
                k = k_ref[pl.ds(pl.multiple_of(j * T, T), T), :]
                z = [_dot_nt(qh[h], k) for h in hs]
                if not diag:
                    flush(j + 1)
                la = [jnp.minimum(z[h], 0.0) - jnp.log(1.0 + jnp.exp(-jnp.abs(z[h]))) for h in hs]
                lf = [la[h] - z[h] for h in hs]
                if diag:
                    lf = [jnp.where(strict, lf[h], 0.0) for h in hs]
                cin = [_cum2(lf[h], u2) for h in hs]
                a = [jnp.exp(la[h] + cin[h] + r_ref[h]) for h in hs]
                if diag:
                    a = [jnp.where(strict, a[h], 0.0) for h in hs]
                for h in hs:
                    r_ref[h] += cin[h][:, 0:1] + lf[h][:, 0:1]
                    as_ref[h] = a[h].astype(BF16)

            tile(i, True)

            def more(state):
                j, top = state
                return jnp.logical_and(j >= 0, top > UNDERFLOW)

            def step(state):
                j, _ = state
                tile(j, False)
                return j - 1, _running_top(r_ref)

            j_left, _ = lax.while_loop(more, step, (i - 1, _running_top(r_ref)))
            flush(j_left + 1)
            o_ref[rows, :] = jnp.where(m0, acc_ref[0], acc_ref[1])

        for n in range(QPS):
            query_tile(QPS * pl.program_id(1) + n, slice(n * T, (n + 1) * T))

    nb = SB_W // LANES
    (ysb,), lands = _ride(dict(
        body=body, name=name, grid=(nb, s // (QPS * T)),
        in_specs=[pl.BlockSpec((QPS * T, LANES), lambda p, i: (i, p)),
                  pl.BlockSpec((s, LANES), lambda p, i: (0, nb + p)),
                  pl.BlockSpec((s, LANES), lambda p, i: (0, 2 * nb + p))],
        out_specs=[pl.BlockSpec((QPS * T, LANES), lambda p, i: (i, p))],
        out_shape=[jax.ShapeDtypeStruct((s, SB_W), F32)],
        scratch_shapes=[pltpu.VMEM((2, T, LANES), F32), pltpu.VMEM((2, T, 1), F32),
                        pltpu.VMEM((2, T, T), BF16)],
        compiler_params=_ARB2, operands=[qkv, qkv, qkv]), rider)
    return ysb, lands


def _fox_fwd(qkv, fqb, frow, fbounds, name, rider=None):
    s = qkv.shape[0]

    def body(q_ref, k_ref, v_ref, fq_ref, fr_ref, fb_ref, o_ref, lse_ref, acc_ref, m_ref, ps_ref):
        pair = pl.program_id(0)
        m0 = _pair_masks()
        causal = _iota2((T, T), 0) >= _iota2((T, T), 1)
        ones = jnp.ones((T, LANES), BF16)
        hs = range(2)

        def query_tile(i, rows):
            qh = _split_pair(q_ref[rows, :] * jnp.asarray(Q_SCALE, BF16), m0)
            fq = fq_ref[rows, :]
            fqh = (fq[:, 0:1], fq[:, HEAD_DIM:HEAD_DIM + 1])
            acc_ref[...] = jnp.zeros_like(acc_ref)
            m_ref[...] = jnp.full_like(m_ref, NEG)

            def flush(j):
                v = v_ref[pl.ds(pl.multiple_of(j * T, T), T), :]
                va2 = _stack2(jnp.concatenate([v, ones], axis=1))
                for h in hs:
                    acc_ref[h] += _dot(ps_ref[h], va2)

            def tile(j, diag):
                off = pl.multiple_of(j * T, T)
                k = k_ref[pl.ds(off, T), :]
                sc = [_dot_nt(qh[h], k) + fqh[h] - fr_ref[h:h + 1, pl.ds(off, T)] for h in hs]
                if not diag:
                    flush(j + 1)
                if diag:
                    sc = [jnp.where(causal, sc[h], NEG) for h in hs]
                m_new = [jnp.maximum(m_ref[h], jnp.max(sc[h], axis=1, keepdims=True)) for h in hs]
                p = [jnp.exp(sc[h] - m_new[h]) for h in hs]
                for h in hs:
                    acc_ref[h] = acc_ref[h] * jnp.exp(m_ref[h] - m_new[h])
                    m_ref[h] = m_new[h]
                    p_hi, p_lo = _split2(p[h])
                    ps_ref[h] = jnp.concatenate([p_hi, p_lo], axis=1)

            tile(i, True)
            j_left = _fox_tiles_left(i, pair, s // T, fb_ref, tile)
            flush(j_left + 1)
            acc = (acc_ref[0], acc_ref[1])
            o_ref[rows, :] = jnp.where(m0, acc[0][:, :LANES] / acc[0][:, LANES:],
                                       acc[1][:, :LANES] / acc[1][:, LANES:])
            lse_ref[rows, :] = jnp.where(m0, m_ref[0] + jnp.log(acc[0][:, LANES:]),
                                         m_ref[1] + jnp.log(acc[1][:, LANES:]))

        for n in range(QPS):
            query_tile(QPS * pl.program_id(1) + n, slice(n * T, (n + 1) * T))

    nb = FX_W // LANES
    base = 3 * SB_W // LANES
    (yfx, lse), lands = _ride(dict(
        body=body, name=name, grid=(nb, s // (QPS * T)),
        in_specs=[pl.BlockSpec((QPS * T, LANES), lambda p, i: (i, base + p)),
                  pl.BlockSpec((s, LANES), lambda p, i: (0, base + nb + p)),
                  pl.BlockSpec((s, LANES), lambda p, i: (0, base + 2 * nb + p)),
                  pl.BlockSpec((QPS * T, LANES), lambda p, i: (i, p)),
                  pl.BlockSpec((None, 2, s), lambda p, i: (p, 0, 0)),
                  pl.BlockSpec(memory_space=pltpu.SMEM)],
        out_specs=[pl.BlockSpec((QPS * T, LANES), lambda p, i: (i, p)),
                   pl.BlockSpec((QPS * T, LANES), lambda p, i: (i, p))],
        out_shape=[jax.ShapeDtypeStruct((s, FX_W), F32), jax.ShapeDtypeStruct((s, FX_W), F32)],
        scratch_shapes=[pltpu.VMEM((2, T, 2 * LANES), F32), pltpu.VMEM((2, T, 1), F32),
                        pltpu.VMEM((2, T, 2 * T), BF16)],
        compiler_params=_ARB2, operands=[qkv, qkv, qkv, fqb, frow, fbounds]), rider)
    return yfx, lse, lands


def _memkv_fwd(mem, mnw, wkv, name):
    n = mem.shape[0]

    def body(mem_ref, mnw_ref, w_ref, kv_ref):
        mv = mem_ref[...]
        r = lax.rsqrt(jnp.mean(mv * mv, axis=-1, keepdims=True) + EPS)
        hm = (mv * r * mnw_ref[...]).astype(BF16)
        kv_ref[...] = _dot(hm, w_ref[...]).astype(BF16)

    return pl.pallas_call(
        body, name=name, grid=(1,),
        in_specs=[pl.BlockSpec((n, D_MODEL), lambda i: (0, 0)),
                  pl.BlockSpec((1, D_MODEL), lambda i: (0, 0)),
                  pl.BlockSpec((D_MODEL, 2 * MEM_W), lambda i: (0, 0))],
        out_specs=pl.BlockSpec((n, 2 * MEM_W), lambda i: (0, 0)),
        out_shape=jax.ShapeDtypeStruct((n, 2 * MEM_W), BF16),
        compiler_params=_ARB1,
    )(mem, mnw, wkv)


def _mem_fwd(qkv, kv, name):
    s = qkv.shape[0]
    n = kv.shape[0]

    def body(q_ref, k_ref, v_ref, o_ref, lse_ref):
        m0 = _pair_masks()
        qh = _split_pair(q_ref[...] * jnp.asarray(Q_SCALE, BF16), m0)
        k = k_ref[...]
        v = v_ref[...]
        outs, lses = [], []
        for h in range(2):
            sc = _dot_nt(qh[h], k)
            mx = jnp.max(sc, axis=1, keepdims=True)
            p = jnp.exp(sc - mx)
            l = jnp.sum(p, axis=1, keepdims=True)
            outs.append(_dot(p.astype(BF16), v) / l)
            lses.append(mx + jnp.log(l))
        o_ref[...] = jnp.where(m0, outs[0], outs[1])
        lse_ref[...] = jnp.where(m0, lses[0], lses[1])

    nb = MEM_W // LANES
    base = (3 * SB_W + 3 * FX_W) // LANES
    return pl.pallas_call(
        body, name=name, grid=(nb, s // TQM),
        in_specs=[pl.BlockSpec((TQM, LANES), lambda p, i: (i, base + p)),
                  pl.BlockSpec((n, LANES), lambda p, i: (0, p)),
                  pl.BlockSpec((n, LANES), lambda p, i: (0, nb + p))],
        out_specs=[pl.BlockSpec((TQM, LANES), lambda p, i: (i, p)),
                   pl.BlockSpec((TQM, LANES), lambda p, i: (i, p))],
        out_shape=[jax.ShapeDtypeStruct((s, MEM_W), F32), jax.ShapeDtypeStruct((s, MEM_W), F32)],
        compiler_params=_ARB2,
    )(qkv, kv, kv)


def _mix_chunk(c, ysb_ref, yfx_ref, ym_ref):
    if c < SB_W // LANES:
        return ysb_ref[:, c * LANES:(c + 1) * LANES]
    c -= SB_W // LANES
    if c < FX_W // LANES:
        return yfx_ref[:, c * LANES:(c + 1) * LANES]
    c -= FX_W // LANES
    return ym_ref[:, c * LANES:(c + 1) * LANES]


def _outproj_fwd(ysb, yfx, ym, gf, onw, wout, x, name):
    s = x.shape[0]

    def body(ysb_ref, yfx_ref, ym_ref, g_ref, onw_ref, w_ref, x_ref, o_ref, yg_ref):
        bd = _head_block_diag()
        for c in range(MIX_W // LANES):
            sl = slice(c * LANES, (c + 1) * LANES)
            u = _mix_chunk(c, ysb_ref, yfx_ref, ym_ref)
            r = lax.rsqrt(_head_mean(u * u, bd) + EPS)
            g = g_ref[:, sl]
            yg_ref[:, sl] = (u * r * onw_ref[:, sl] * (g * _sigmoid(g))).astype(BF16)
        o_ref[...] = x_ref[...] + _dot(yg_ref[...], w_ref[...])

    return pl.pallas_call(
        body, name=name, grid=(s // TM,),
        in_specs=[pl.BlockSpec((TM, SB_W), lambda i: (i, 0)),
                  pl.BlockSpec((TM, FX_W), lambda i: (i, 0)),
                  pl.BlockSpec((TM, MEM_W), lambda i: (i, 0)),
                  pl.BlockSpec((TM, MIX_W), lambda i: (i, 0)),
                  pl.BlockSpec((1, MIX_W), lambda i: (0, 0)),
                  pl.BlockSpec((MIX_W, D_MODEL), lambda i: (0, 0)),
                  pl.BlockSpec((TM, D_MODEL), lambda i: (i, 0))],
        out_specs=pl.BlockSpec((TM, D_MODEL), lambda i: (i, 0)),
        out_shape=jax.ShapeDtypeStruct((s, D_MODEL), F32),
        scratch_shapes=[pltpu.VMEM((TM, MIX_W), BF16)],
        compiler_params=_ARB1,
    )(ysb, yfx, ym, gf, onw, wout, x)


def _final_fwd_bwd(x, fnw, target, name):
    s = x.shape[0]

    def body(x_ref, w_ref, t_ref, dx_ref, loss_ref, dw_ref):
        @pl.when(pl.program_id(0) == 0)
        def _():
            loss_ref[...] = jnp.zeros_like(loss_ref)
            dw_ref[...] = jnp.zeros_like(dw_ref)

        xv = x_ref[...]
        w = w_ref[...]
        r = lax.rsqrt(jnp.mean(xv * xv, axis=-1, keepdims=True) + EPS)
        xh = xv * r
        err = xh * w - t_ref[...]
        part = jnp.sum(jnp.sum(err * err, axis=1, keepdims=True), axis=0, keepdims=True)
        loss_ref[...] += part * (0.5 / D_MODEL)
        dy = err * (1.0 / D_MODEL)
        dw_ref[...] += jnp.sum(dy * xh, axis=0, keepdims=True)
        dxh = dy * w
        dx_ref[...] = r * (dxh - xh * jnp.mean(dxh * xh, axis=-1, keepdims=True))

    return pl.pallas_call(
        body, name=name, grid=(s // TM,),
        in_specs=[pl.BlockSpec((TM, D_MODEL), lambda i: (i, 0)),
                  pl.BlockSpec((1, D_MODEL), lambda i: (0, 0)),
                  pl.BlockSpec((TM, D_MODEL), lambda i: (i, 0))],
        out_specs=[pl.BlockSpec((TM, D_MODEL), lambda i: (i, 0)),
                   pl.BlockSpec((1, LANES), lambda i: (0, 0)),
                   pl.BlockSpec((1, D_MODEL), lambda i: (0, 0))],
        out_shape=[jax.ShapeDtypeStruct((s, D_MODEL), F32), jax.ShapeDtypeStruct((1, LANES), F32),
                   jax.ShapeDtypeStruct((1, D_MODEL), F32)],
        compiler_params=_ARB1,
    )(x, fnw, target)


def _outproj_bwd(dxo, wout, ysb, yfx, ym, gf, onw, name):
    s = dxo.shape[0]

    def body(dx_ref, w_ref, ysb_ref, yfx_ref, ym_ref, g_ref, onw_ref,
             dysb_ref, dyfx_ref, dym_ref, dg_ref, dw_ref, donw_ref, yg_ref):
        @pl.when(pl.program_id(0) == 0)
        def _():
            dw_ref[...] = jnp.zeros_like(dw_ref)
            donw_ref[...] = jnp.zeros_like(donw_ref)

        dxb = dx_ref[...].astype(BF16)
        dyg = _dot_nt(dxb, w_ref[...])
        bd = _head_block_diag()
        for c in range(MIX_W // LANES):
            sl = slice(c * LANES, (c + 1) * LANES)
            u = _mix_chunk(c, ysb_ref, yfx_ref, ym_ref)
            r = lax.rsqrt(_head_mean(u * u, bd) + EPS)
            yn = u * r
            g = g_ref[:, sl]
            sg = _sigmoid(g)
            sil = g * sg
            onw = onw_ref[:, sl]
            e = dyg[:, sl]
            yg_ref[:, sl] = (yn * onw * sil).astype(BF16)
            donw_ref[:, sl] += jnp.sum(e * yn * sil, axis=0, keepdims=True)
            dg_ref[:, sl] = (e * yn * onw * (sg * (1.0 + g * (1.0 - sg)))).astype(BF16)
            dyn = e * onw * sil
            du = (r * (dyn - yn * _head_mean(dyn * yn, bd))).astype(BF16)
            if c < 4:
                dysb_ref[:, c * LANES:(c + 1) * LANES] = du
            elif c < 8:
                dyfx_ref[:, (c - 4) * LANES:(c - 3) * LANES] = du
            else:
                dym_ref[:, (c - 8) * LANES:(c - 7) * LANES] = du
        dw_ref[...] += _dot_tn(yg_ref[...], dxb)

    return pl.pallas_call(
        body, name=name, grid=(s // TM,),
        in_specs=[pl.BlockSpec((TM, D_MODEL), lambda i: (i, 0)),
                  pl.BlockSpec((MIX_W, D_MODEL), lambda i: (0, 0)),
                  pl.BlockSpec((TM, SB_W), lambda i: (i, 0)),
                  pl.BlockSpec((TM, FX_W), lambda i: (i, 0)),
                  pl.BlockSpec((TM, MEM_W), lambda i: (i, 0)),
                  pl.BlockSpec((TM, MIX_W), lambda i: (i, 0)),
                  pl.BlockSpec((1, MIX_W), lambda i: (0, 0))],
        out_specs=[pl.BlockSpec((TM, SB_W), lambda i: (i, 0)),
                   pl.BlockSpec((TM, FX_W), lambda i: (i, 0)),
                   pl.BlockSpec((TM, MEM_W), lambda i: (i, 0)),
                   pl.BlockSpec((TM, MIX_W), lambda i: (i, 0)),
                   pl.BlockSpec((MIX_W, D_MODEL), lambda i: (0, 0)),
                   pl.BlockSpec((1, MIX_W), lambda i: (0, 0))],
        out_shape=[jax.ShapeDtypeStruct((s, SB_W), BF16), jax.ShapeDtypeStruct((s, FX_W), BF16),
                   jax.ShapeDtypeStruct((s, MEM_W), BF16), jax.ShapeDtypeStruct((s, MIX_W), BF16),
                   jax.ShapeDtypeStruct((MIX_W, D_MODEL), F32), jax.ShapeDtypeStruct((1, MIX_W), F32)],
        scratch_shapes=[pltpu.VMEM((TM, MIX_W), BF16)],
        compiler_params=_ARB1,
    )(dxo, wout, ysb, yfx, ym, gf, onw)


def _row_dots(do_ref, o_ref, m0):
    prod = do_ref[...].astype(F32) * o_ref[...]
    zero = jnp.zeros_like(prod)
    return (jnp.sum(jnp.where(m0, prod, zero), axis=1, keepdims=True),
            jnp.sum(jnp.where(m0, zero, prod), axis=1, keepdims=True))


def _sb_bwd(qkv, o, do, name, rider=None):
    s = qkv.shape[0]
    nq = s // T

    def body(q_ref, k_ref, v_ref, o_ref, do_ref, dq_ref, dk_ref, dv_ref,
             dqa_ref, dka_ref, dva_ref, rl_ref, rg_ref, dzs_ref, abs_ref):
        i = pl.program_id(1)

        @pl.when(i == 0)
        def _():
            dka_ref[...] = jnp.zeros_like(dka_ref)
            dva_ref[...] = jnp.zeros_like(dva_ref)

        m0 = _pair_masks()
        qh = _split_pair(q_ref[...] * jnp.asarray(Q_SCALE, BF16), m0)
        doh = _split_pair(do_ref[...], m0)
        dsum = _row_dots(do_ref, o_ref, m0)
        strict = _iota2((T, T), 0) > _iota2((T, T), 1)
        u2 = _stack2(jnp.where(strict, 1.0, 0.0).astype(BF16))
        dqa_ref[...] = jnp.zeros_like(dqa_ref)
        rl_ref[...] = jnp.zeros_like(rl_ref)
        rg_ref[...] = jnp.zeros_like(rg_ref)

        hs = range(2)

        def flush(j):
            off = pl.multiple_of(j * T, T)
            k = k_ref[pl.ds(off, T), :]
            for h in hs:
                dqa_ref[h] += _dot(dzs_ref[h], k)
            dka_ref[pl.ds(off, T), :] += _dot_tn(dzs_ref[0], qh[0]) + _dot_tn(dzs_ref[1], qh[1])
            dva_ref[pl.ds(off, T), :] += _dot_tn(abs_ref[0], doh[0]) + _dot_tn(abs_ref[1], doh[1])

        def tile(j, diag):
            off = pl.multiple_of(j * T, T)
            k = k_ref[pl.ds(off, T), :]
            v = v_ref[pl.ds(off, T), :]
            z = [_dot_nt(qh[h], k) for h in hs]
            da = [_dot_nt(doh[h], v) for h in hs]
            if not diag:
                flush(j + 1)
            la = [jnp.minimum(z[h], 0.0) - jnp.log(1.0 + jnp.exp(-jnp.abs(z[h]))) for h in hs]
            lf = [la[h] - z[h] for h in hs]
            if diag:
                lf = [jnp.where(strict, lf[h], 0.0) for h in hs]
            cin = [_cum2(lf[h], u2) for h in hs]
            a = [jnp.exp(la[h] + cin[h] + rl_ref[h]) for h in hs]
            if diag:
                a = [jnp.where(strict, a[h], 0.0) for h in hs]
            ab = [a[h].astype(BF16) for h in hs]
            g = [ab[h].astype(F32) * da[h] for h in hs]
            gin = [_cum2(g[h], u2) for h in hs]
            dz = [g[h] - jnp.exp(la[h]) * ((dsum[h] - rg_ref[h]) - gin[h]) for h in hs]
            if diag:
                dz = [jnp.where(strict, dz[h], 0.0) for h in hs]
            for h in hs:
                rl_ref[h] += cin[h][:, 0:1] + lf[h][:, 0:1]
                rg_ref[h] += gin[h][:, 0:1] + g[h][:, 0:1]
                dzs_ref[h] = dz[h].astype(BF16)
                abs_ref[h] = ab[h]

        tile(i, True)

        def more(state):
            j, top = state
            return jnp.logical_and(j >= 0, top > UNDERFLOW)

        def step(state):
            j, _ = state
            tile(j, False)
            return j - 1, _running_top(rl_ref)

        j_left, _ = lax.while_loop(more, step, (i - 1, _running_top(rl_ref)))
        flush(j_left + 1)
        dq_ref[...] = (jnp.where(m0, dqa_ref[0], dqa_ref[1]) * Q_SCALE).astype(BF16)

        @pl.when(i == nq - 1)
        def _():
            dk_ref[...] = dka_ref[...].astype(BF16)
            dv_ref[...] = dva_ref[...].astype(BF16)

    nb = SB_W // LANES
    (dq, dk, dv), lands = _ride(dict(
        body=body, name=name, grid=(nb, nq),
        in_specs=[pl.BlockSpec((T, LANES), lambda p, i: (i, p)),
                  pl.BlockSpec((s, LANES), lambda p, i: (0, nb + p)),
                  pl.BlockSpec((s, LANES), lambda p, i: (0, 2 * nb + p)),
                  pl.BlockSpec((T, LANES), lambda p, i: (i, p)),
                  pl.BlockSpec((T, LANES), lambda p, i: (i, p))],
        out_specs=[pl.BlockSpec((T, LANES), lambda p, i: (i, p)),
                   pl.BlockSpec((s, LANES), lambda p, i: (0, p)),
                   pl.BlockSpec((s, LANES), lambda p, i: (0, p))],
        out_shape=[jax.ShapeDtypeStruct((s, SB_W), BF16)] * 3,
        scratch_shapes=[pltpu.VMEM((2, T, LANES), F32), pltpu.VMEM((s, LANES), F32),
                        pltpu.VMEM((s, LANES), F32), pltpu.VMEM((2, T, 1), F32),
                        pltpu.VMEM((2, T, 1), F32), pltpu.VMEM((2, T, T), BF16),
                        pltpu.VMEM((2, T, T), BF16)],
        compiler_params=_ARB2, operands=[qkv, qkv, qkv, o, do]), rider)
    return dq, dk, dv, lands


def _fox_bwd(qkv, fqb, frow, fbounds, o, lse, do, name, rider=None):
    s = qkv.shape[0]
    nq = s // T

    def body(q_ref, k_ref, v_ref, fq_ref, fr_ref, fb_ref, o_ref, lse_ref, do_ref,
             dq_ref, dk_ref, dv_ref, df_ref, dqa_ref, dka_ref, dva_ref, dfa_ref, dls_ref, pbs_ref):
        i = pl.program_id(1)

        @pl.when(i == 0)
        def _():
            dka_ref[...] = jnp.zeros_like(dka_ref)
            dva_ref[...] = jnp.zeros_like(dva_ref)
            dfa_ref[...] = jnp.zeros_like(dfa_ref)

        m0 = _pair_masks()
        qh = _split_pair(q_ref[...] * jnp.asarray(Q_SCALE, BF16), m0)
        doh = _split_pair(do_ref[...], m0)
        dsum = _row_dots(do_ref, o_ref, m0)
        fq = fq_ref[...]
        fqh = (fq[:, 0:1], fq[:, HEAD_DIM:HEAD_DIM + 1])
        lse = lse_ref[...]
        lseh = (lse[:, 0:1], lse[:, HEAD_DIM:HEAD_DIM + 1])
        causal = _iota2((T, T), 0) >= _iota2((T, T), 1)
        dqa_ref[...] = jnp.zeros_like(dqa_ref)

        hs = range(2)

        def flush(j):
            off = pl.multiple_of(j * T, T)
            k = k_ref[pl.ds(off, T), :]
            for h in hs:
                dqa_ref[h] += _dot(dls_ref[h], k)
            dka_ref[pl.ds(off, T), :] += _dot_tn(dls_ref[0], qh[0]) + _dot_tn(dls_ref[1], qh[1])
            dva_ref[pl.ds(off, T), :] += _dot_tn(pbs_ref[0], doh[0]) + _dot_tn(pbs_ref[1], doh[1])

        def tile(j, diag):
            off = pl.multiple_of(j * T, T)
            k = k_ref[pl.ds(off, T), :]
            v = v_ref[pl.ds(off, T), :]
            sc = [_dot_nt(qh[h], k) + fqh[h] - fr_ref[h:h + 1, pl.ds(off, T)] for h in hs]
            dp = [_dot_nt(doh[h], v) for h in hs]
            if not diag:
                flush(j + 1)
            p = [jnp.exp(sc[h] - lseh[h]) for h in hs]
            if diag:
                p = [jnp.where(causal, p[h], 0.0) for h in hs]
            dl = [p[h] * (dp[h] - dsum[h]) for h in hs]
            for h in hs:
                dls_ref[h] = dl[h].astype(BF16)
                pbs_ref[h] = p[h].astype(BF16)
                dfa_ref[h:h + 1, pl.ds(off, T)] -= jnp.sum(dl[h], axis=0, keepdims=True)

        tile(i, True)
        j_left = _fox_tiles_left(i, pl.program_id(0), nq, fb_ref, tile)
        flush(j_left + 1)
        dq_ref[...] = (jnp.where(m0, dqa_ref[0], dqa_ref[1]) * Q_SCALE).astype(BF16)

        @pl.when(i == nq - 1)
        def _():
            dk_ref[...] = dka_ref[...].astype(BF16)
            dv_ref[...] = dva_ref[...].astype(BF16)
            df_ref[...] = dfa_ref[...]

    nb = FX_W // LANES
    base = 3 * SB_W // LANES
    (dq, dk, dv, df), lands = _ride(dict(
        body=body, name=name, grid=(nb, nq),
        in_specs=[pl.BlockSpec((T, LANES), lambda p, i: (i, base + p)),
                  pl.BlockSpec((s, LANES), lambda p, i: (0, base + nb + p)),
                  pl.BlockSpec((s, LANES), lambda p, i: (0, base + 2 * nb + p)),
                  pl.BlockSpec((T, LANES), lambda p, i: (i, p)),
                  pl.BlockSpec((None, 2, s), lambda p, i: (p, 0, 0)),
                  pl.BlockSpec(memory_space=pltpu.SMEM),
                  pl.BlockSpec((T, LANES), lambda p, i: (i, p)),
                  pl.BlockSpec((T, LANES), lambda p, i: (i, p)),
                  pl.BlockSpec((T, LANES), lambda p, i: (i, p))],
        out_specs=[pl.BlockSpec((T, LANES), lambda p, i: (i, p)),
                   pl.BlockSpec((s, LANES), lambda p, i: (0, p)),
                   pl.BlockSpec((s, LANES), lambda p, i: (0, p)),
                   pl.BlockSpec((None, 2, s), lambda p, i: (p, 0, 0))],
        out_shape=[jax.ShapeDtypeStruct((s, FX_W), BF16)] * 3
        + [jax.ShapeDtypeStruct((nb, 2, s), F32)],
        scratch_shapes=[pltpu.VMEM((2, T, LANES), F32), pltpu.VMEM((s, LANES), F32),
                        pltpu.VMEM((s, LANES), F32), pltpu.VMEM((2, s), F32),
                        pltpu.VMEM((2, T, T), BF16), pltpu.VMEM((2, T, T), BF16)],
        compiler_params=_ARB2, operands=[qkv, qkv, qkv, fqb, frow, fbounds, o, lse, do]), rider)
    return dq, dk, dv, df, lands


def _fox_prep_bwd(dfrow, gf, bpad, name):
    s = gf.shape[0]

    def body(df_ref, fl_ref, b_ref, dfl_ref, db_ref):
        tri = jnp.where(_iota2((T, T), 0) <= _iota2((T, T), 1), 1.0, 0.0).astype(BF16)
        carry = jnp.zeros((1, LANES), F32)
        db = jnp.zeros((1, LANES), F32)
        fill = jnp.zeros((LANES - FOX_HEADS, T), F32)
        for blk in reversed(range(s // T)):
            rows = slice(blk * T, (blk + 1) * T)
            c = _tri3(tri, jnp.concatenate([df_ref[:, rows], fill], axis=0), _dot_nt) + carry
            carry = c[0:1, :]
            dfl = c / (1.0 + jnp.exp(fl_ref[rows, :] + b_ref[...]))
            dfl_ref[rows, :] = dfl.astype(BF16)
            db = db + jnp.sum(dfl, axis=0, keepdims=True)
        db_ref[...] = db

    return pl.pallas_call(
        body, name=name, grid=(1,),
        in_specs=[pl.BlockSpec((FOX_HEADS, s), lambda i: (0, 0)),
                  pl.BlockSpec((s, LANES), lambda i: (0, MIX_W // LANES)),
                  pl.BlockSpec((1, LANES), lambda i: (0, 0))],
        out_specs=[pl.BlockSpec((s, LANES), lambda i: (0, 0)),
                   pl.BlockSpec((1, LANES), lambda i: (0, 0))],
        out_shape=[jax.ShapeDtypeStruct((s, LANES), BF16), jax.ShapeDtypeStruct((1, LANES), F32)],
        compiler_params=_ARB1,
    )(dfrow, gf, bpad)


def _mem_bwd(qkv, kv, o, lse, do, name):
    s = qkv.shape[0]
    n = kv.shape[0]

    def body(q_ref, k_ref, v_ref, o_ref, lse_ref, do_ref, dq_ref, dk_ref, dv_ref):
        @pl.when(pl.program_id(1) == 0)
        def _():
            dk_ref[...] = jnp.zeros_like(dk_ref)
            dv_ref[...] = jnp.zeros_like(dv_ref)

        m0 = _pair_masks()
        qh = _split_pair(q_ref[...] * jnp.asarray(Q_SCALE, BF16), m0)
        doh = _split_pair(do_ref[...], m0)
        dsum = _row_dots(do_ref, o_ref, m0)
        lse = lse_ref[...]
        lseh = (lse[:, 0:1], lse[:, HEAD_DIM:HEAD_DIM + 1])
        k = k_ref[...]
        v = v_ref[...]
        dqs = []
        for h in range(2):
            p = jnp.exp(_dot_nt(qh[h], k) - lseh[h])
            dl = p * (_dot_nt(doh[h], v) - dsum[h])
            dlb = dl.astype(BF16)
            dqs.append(_dot(dlb, k))
            dk_ref[...] += _dot_tn(dlb, qh[h])
            dv_ref[...] += _dot_tn(p.astype(BF16), doh[h])
        dq_ref[...] = (jnp.where(m0, dqs[0], dqs[1]) * Q_SCALE).astype(BF16)

    nb = MEM_W // LANES
    base = (3 * SB_W + 3 * FX_W) // LANES
    return pl.pallas_call(
        body, name=name, grid=(nb, s // TQM),
        in_specs=[pl.BlockSpec((TQM, LANES), lambda p, i: (i, base + p)),
                  pl.BlockSpec((n, LANES), lambda p, i: (0, p)),
                  pl.BlockSpec((n, LANES), lambda p, i: (0, nb + p)),
                  pl.BlockSpec((TQM, LANES), lambda p, i: (i, p)),
                  pl.BlockSpec((TQM, LANES), lambda p, i: (i, p)),
                  pl.BlockSpec((TQM, LANES), lambda p, i: (i, p))],
        out_specs=[pl.BlockSpec((TQM, LANES), lambda p, i: (i, p)),
                   pl.BlockSpec((n, LANES), lambda p, i: (0, p)),
                   pl.BlockSpec((n, LANES), lambda p, i: (0, p))],
        out_shape=[jax.ShapeDtypeStruct((s, MEM_W), BF16), jax.ShapeDtypeStruct((n, MEM_W), F32),
                   jax.ShapeDtypeStruct((n, MEM_W), F32)],
        compiler_params=_ARB2,
    )(qkv, kv, kv, o, lse, do)


def _memkv_bwd(mem, mnw, wkv, dk, dv, name):
    n = mem.shape[0]

    def body(mem_ref, mnw_ref, w_ref, dk_ref, dv_ref, dw_ref, dmnw_ref):
        mv = mem_ref[...]
        r = lax.rsqrt(jnp.mean(mv * mv, axis=-1, keepdims=True) + EPS)
        mh = mv * r
        hm = (mh * mnw_ref[...]).astype(BF16)
        dkv = jnp.concatenate([dk_ref[...], dv_ref[...]], axis=1).astype(BF16)
        dw_ref[...] = _dot_tn(hm, dkv)
        dhm = _dot_nt(dkv, w_ref[...])
        dmnw_ref[...] = jnp.sum(dhm * mh, axis=0, keepdims=True)

    return pl.pallas_call(
        body, name=name, grid=(1,),
        in_specs=[pl.BlockSpec((n, D_MODEL), lambda i: (0, 0)),
                  pl.BlockSpec((1, D_MODEL), lambda i: (0, 0)),
                  pl.BlockSpec((D_MODEL, 2 * MEM_W), lambda i: (0, 0)),
                  pl.BlockSpec((n, MEM_W), lambda i: (0, 0)),
                  pl.BlockSpec((n, MEM_W), lambda i: (0, 0))],
        out_specs=[pl.BlockSpec((D_MODEL, 2 * MEM_W), lambda i: (0, 0)),
                   pl.BlockSpec((1, D_MODEL), lambda i: (0, 0))],
        out_shape=[jax.ShapeDtypeStruct((D_MODEL, 2 * MEM_W), F32),
                   jax.ShapeDtypeStruct((1, D_MODEL), F32)],
        compiler_params=_ARB1,
    )(mem, mnw, wkv, dk, dv)


def _inproj_bwd_dx(pieces, w_r, x, nw, dxo, name):
    s = x.shape[0]
    n = len(pieces)
    widths = [p.shape[1] for p in pieces]

    def body(*refs):
        piece_refs = refs[:n]
        w_ref, x_ref, nw_ref, dxo_ref, dx_ref, h_ref, dnw_ref, dp_ref = refs[n:]

        @pl.when(pl.program_id(0) == 0)
        def _():
            dnw_ref[...] = jnp.zeros_like(dnw_ref)

        col = 0
        for r, wd in zip(piece_refs, widths):
            dp_ref[:, col:col + wd] = r[...]
            col += wd
        dp_ref[:, col:] = jnp.zeros((TM, WR_W - col), BF16)
        dh = _dot(dp_ref[...], w_ref[...])
        xv = x_ref[...]
        nw = nw_ref[...]
        r = lax.rsqrt(jnp.mean(xv * xv, axis=-1, keepdims=True) + EPS)
        xh = xv * r
        h_ref[...] = (xh * nw).astype(BF16)
        dnw_ref[...] += jnp.sum(dh * xh, axis=0, keepdims=True)
        dxh = dh * nw
        dx_ref[...] = r * (dxh - xh * jnp.mean(dxh * xh, axis=-1, keepdims=True)) + dxo_ref[...]

    return pl.pallas_call(
        body, name=name, grid=(s // TM,),
        in_specs=[pl.BlockSpec((TM, wd), lambda i: (i, 0)) for wd in widths]
        + [pl.BlockSpec((WR_W, D_MODEL), lambda i: (0, 0)),
           pl.BlockSpec((TM, D_MODEL), lambda i: (i, 0)),
           pl.BlockSpec((1, D_MODEL), lambda i: (0, 0)),
           pl.BlockSpec((TM, D_MODEL), lambda i: (i, 0))],
        out_specs=[pl.BlockSpec((TM, D_MODEL), lambda i: (i, 0)),
                   pl.BlockSpec((TM, D_MODEL), lambda i: (i, 0)),
                   pl.BlockSpec((1, D_MODEL), lambda i: (0, 0)),
                   pl.BlockSpec((TM, WR_W), lambda i: (i, 0))],
        out_shape=[jax.ShapeDtypeStruct((s, D_MODEL), F32), jax.ShapeDtypeStruct((s, D_MODEL), BF16),
                   jax.ShapeDtypeStruct((1, D_MODEL), F32), jax.ShapeDtypeStruct((s, WR_W), BF16)],
        compiler_params=_ARB1,
    )(*pieces, w_r, x, nw, dxo)


def _inproj_bwd_dw(h, dproj, name):
    s = dproj.shape[0]
    tn = 256

    def body(h_ref, dp_ref, dw_ref):
        dw_ref[...] = _dot_tn(dp_ref[...], h_ref[...])

    return pl.pallas_call(
        body, name=name, grid=(WR_W // tn,),
        in_specs=[pl.BlockSpec((s, D_MODEL), lambda j: (0, 0)),
                  pl.BlockSpec((s, tn), lambda j: (0, j))],
        out_specs=pl.BlockSpec((tn, D_MODEL), lambda j: (j, 0)),
        out_shape=jax.ShapeDtypeStruct((WR_W, D_MODEL), F32),
        compiler_params=_ARB1,
    )(h, dproj)


def _rearrange_w_in(wt):
    pad = jnp.zeros((FL_PAD - FOX_HEADS,) + wt.shape[1:], wt.dtype)
    return jnp.concatenate([wt[:3072], wt[3080:3336], wt[3336:IN_W], wt[3072:3080], pad], axis=0)


def _restore_w_in(g):
    gate0 = QKV_W
    fl0 = QKV_W + MIX_W
    return jnp.concatenate(
        [g[:3072], g[fl0:fl0 + FOX_HEADS], g[3072:QKV_W], g[gate0:fl0]], axis=0)


def _pad_lanes(v, width=LANES):
    return jnp.pad(v, (0, width - v.shape[0])).reshape(1, width)


def _layer_fwd(xs, mem, nw, w_r, b_forget, mnw, late, onw, l, travel=None):
    s = xs.shape[0]
    bpad = _pad_lanes(b_forget)
    qkv, gf = _inproj_fwd(xs, nw, w_r, f"inproj_fwd_{l}")
    fqb, frow, fbounds = _fox_prep_fwd(gf, qkv, bpad, f"fox_prep_fwd_{l}")
    frow = frow.reshape(FOX_HEADS // 2, 2, s)
    travel = _Travel(travel)
    ysb, _ = travel.ride(0, _sb_fwd, qkv, f"sb_fwd_{l}")
    yfx, lse_fx, _ = travel.ride(1, _fox_fwd, qkv, fqb, frow, fbounds, f"fox_fwd_{l}")
    wkv, wout = late(travel.lands)
    kv = _memkv_fwd(mem, mnw, wkv, f"memkv_fwd_{l}")
    ym, lse_m = _mem_fwd(qkv, kv, f"mem_fwd_{l}")
    xn = _outproj_fwd(ysb, yfx, ym, gf, onw, wout, xs, f"outproj_fwd_{l}")
    saved = (xs, nw, mnw, onw, bpad, qkv, gf, fqb, frow, fbounds, ysb, yfx, lse_fx, kv, ym, lse_m)
    return xn, saved, travel.lands, (wkv, wout)


class _Travel:
    def __init__(self, plan):
        self.plan = plan
        self.lands = None if plan is None else _new_lands(plan[0], plan[1])

    def ride(self, n, fn, *args):
        if self.plan is None or self.plan[2][n] is None:
            return fn(*args)
        srcs, scatter, legs = self.plan
        idx, rows = legs[n]
        out = fn(*args, rider=_Rider([srcs[a] for a in idx], [self.lands[a] for a in idx],
                                     scatter, rows))
        for a, land in zip(idx, out[-1]):
            self.lands[a] = land
        return out


def _layer_bwd(dx, saved, mem, w_r, wkv, wout, l, travel=None):
    xs, nw, mnw, onw, bpad, qkv, gf, fqb, frow, fbounds, ysb, yfx, lse_fx, kv, ym, lse_m = saved
    s = xs.shape[0]
    dysb, dyfx, dym, dgate, dwout, donw = _outproj_bwd(
        dx, wout, ysb, yfx, ym, gf, onw, f"outproj_bwd_{l}")
    travel = _Travel(None if travel is None else travel(dwout))
    sdq, sdk, sdv, _ = travel.ride(0, _sb_bwd, qkv, ysb, dysb, f"sb_bwd_{l}")
    fdq, fdk, fdv, dfrow, _ = travel.ride(1, _fox_bwd, qkv, fqb, frow, fbounds, yfx, lse_fx, dyfx,
                                          f"fox_bwd_{l}")
    dfl, db = _fox_prep_bwd(dfrow.reshape(FOX_HEADS, s), gf, bpad, f"fox_prep_bwd_{l}")
    dmq, dmk, dmv = _mem_bwd(qkv, kv, ym, lse_m, dym, f"mem_bwd_{l}")
    dwkv, dmnw = _memkv_bwd(mem, mnw, wkv, dmk, dmv, f"memkv_bwd_{l}")
    dx, ht, dnw, dproj = _inproj_bwd_dx([sdq, sdk, sdv, fdq, fdk, fdv, dmq, dgate, dfl],
                                        w_r, xs, nw, dx, f"inproj_bwd_dx_{l}")
    dwr = _inproj_bwd_dw(ht, dproj, f"inproj_bwd_dw_{l}")
    grads = dict(norm_w=dnw[0], w_r=dwr, b_forget=db[0, :FOX_HEADS], mem_norm_w=dmnw[0],
                 w_mem_kv=dwkv, out_norm_w=donw[0], w_out=dwout)
    return dx, grads, travel.lands


_ANY = pl.BlockSpec(memory_space=pl.ANY)


def _my_place():
    return lax.axis_index("x"), lax.axis_index("y"), lax.axis_index("c")


def _flip(v, bit):
    return 1 - v if bit else v


def _block_index(px, py, pc):
    return 4 * px + 2 * py + pc


def _all_gather_weights(shards, name):
    n = len(shards)

    def body(*refs):
        ins, outs = refs[:n], refs[n:2 * n]
        send_sems, recv_sems, local_sems = refs[2 * n:]
        x, y, c = _my_place()
        me = (x, y, c)
        sibling = (x, y, 1 - c)
        chips = [(1 - x, y), (x, 1 - y), (1 - x, 1 - y)]

        def copy(a, k, block, to, src=None):
            dst = outs[a].at[_block_index(*block)]
            return pltpu.make_async_remote_copy(
                src_ref=dst if src is None else src, dst_ref=dst,
                send_sem=send_sems.at[a, k], recv_sem=recv_sems.at[a, k],
                device_id=to, device_id_type=pl.DeviceIdType.MESH)

        mine = [pltpu.make_async_copy(ins[a], outs[a].at[_block_index(*me)], local_sems.at[a])
                for a in range(n)]
        for cp in mine:
            cp.start()
        first = []
        for a in range(n):
            first.append(copy(a, 0, me, sibling, src=ins[a]))
            first += [copy(a, 1 + j, me, (*chip, c), src=ins[a]) for j, chip in enumerate(chips)]
        for cp in first:
            cp.start()
        passed = []
        for j, chip in enumerate(chips):
            for a in range(n):
                copy(a, 1 + j, (*chip, c), me).wait_recv()
                fwd = copy(a, 4 + j, (*chip, c), sibling)
                fwd.start()
                passed.append(fwd)
        for a in range(n):
            copy(a, 0, sibling, me).wait_recv()
            for j, chip in enumerate(chips):
                copy(a, 4 + j, (*chip, 1 - c), me).wait_recv()
        for cp in first + passed:
            cp.wait_send()
        for cp in mine:
            cp.wait()

    return pl.pallas_call(
        body, name=name,
        in_specs=[_ANY] * n, out_specs=[_ANY] * n,
        out_shape=[jax.ShapeDtypeStruct((N_DEV,) + v.shape, v.dtype) for v in shards],
        scratch_shapes=[pltpu.SemaphoreType.DMA((n, 7)), pltpu.SemaphoreType.DMA((n, 7)),
                        pltpu.SemaphoreType.DMA((n,))],
    )(*shards)


def _exchange_blocks(blocked, name):
    n = len(blocked)

    def body(*refs):
        ins, outs = refs[:n], refs[n:2 * n]
        send_sems, recv_sems, local_sems = refs[2 * n:]
        x, y, c = _my_place()
        mine_idx = _block_index(x, y, c)
        local = [pltpu.make_async_copy(ins[a].at[mine_idx], outs[a].at[mine_idx], local_sems.at[a])
                 for a in range(n)]
        for cp in local:
            cp.start()
        sends, arrivals = [], []
        for r in range(1, N_DEV):
            peer = (_flip(x, r & 4), _flip(y, r & 2), _flip(c, r & 1))
            peer_idx = _block_index(*peer)
            for a in range(n):
                sems = dict(send_sem=send_sems.at[a, r - 1], recv_sem=recv_sems.at[a, r - 1],
                            device_id=peer, device_id_type=pl.DeviceIdType.MESH)
                sends.append(pltpu.make_async_remote_copy(
                    src_ref=ins[a].at[peer_idx], dst_ref=outs[a].at[mine_idx], **sems))
                arrivals.append(pltpu.make_async_remote_copy(
                    src_ref=ins[a].at[peer_idx], dst_ref=outs[a].at[peer_idx], **sems))
        for cp in sends:
            cp.start()
        for cp in arrivals:
            cp.wait_recv()
        for cp in sends:
            cp.wait_send()
        for cp in local:
            cp.wait()

    return pl.pallas_call(
        body, name=name,
        in_specs=[_ANY] * n, out_specs=[_ANY] * n,
        out_shape=[jax.ShapeDtypeStruct(v.shape, v.dtype) for v in blocked],
        scratch_shapes=[pltpu.SemaphoreType.DMA((n, 7)), pltpu.SemaphoreType.DMA((n, 7)),
                        pltpu.SemaphoreType.DMA((n,))],
    )(*blocked)


N_CHIP = N_DEV // 2


def _pair_swap(blocked, name):
    n = len(blocked)

    def body(*refs):
        ins, outs = refs[:n], refs[n:2 * n]
        send_sems, recv_sems = refs[2 * n:]
        x, y, c = _my_place()
        copies = [pltpu.make_async_remote_copy(
            src_ref=ins[a].at[j, 1 - c], dst_ref=outs[a].at[j],
            send_sem=send_sems.at[N_CHIP * a + j], recv_sem=recv_sems.at[N_CHIP * a + j],
            device_id=(x, y, 1 - c), device_id_type=pl.DeviceIdType.MESH)
            for a in range(n) for j in range(N_CHIP)]
        for cp in copies:
            cp.start()
        for cp in copies:
            cp.wait_recv()
        for cp in copies:
            cp.wait_send()

    return pl.pallas_call(
        body, name=name,
        in_specs=[_ANY] * n, out_specs=[_ANY] * n,
        out_shape=[jax.ShapeDtypeStruct((N_CHIP,) + v.shape[2:], v.dtype) for v in blocked],
        scratch_shapes=[pltpu.SemaphoreType.DMA((N_CHIP * n,)),
                        pltpu.SemaphoreType.DMA((N_CHIP * n,))],
    )(*blocked)


def _pair_add(mine, theirs, name):
    _, nrow, ncol = mine.shape

    def body(a_ref, b_ref, o_ref):
        o_ref[...] = (a_ref[...].astype(F32) + b_ref[...].astype(F32)).astype(BF16)

    blk = pl.BlockSpec((None, nrow, ncol), lambda j: (j, 0, 0))
    return pl.pallas_call(
        body, name=name, grid=(N_CHIP,), in_specs=[blk, blk], out_specs=blk,
        out_shape=jax.ShapeDtypeStruct(mine.shape, BF16), compiler_params=_ARB1,
    )(mine, theirs)


def _chip_exchange(by_chip, to_all, name):
    n, m = len(by_chip), len(to_all)

    def body(*refs):
        ins, alls = refs[:n], refs[n:n + m]
        outs, all_outs = refs[n + m:2 * n + m], refs[2 * n + m:2 * (n + m)]
        send_sems, recv_sems, local_sems = refs[2 * (n + m):]
        x, y, c = _my_place()
        my_chip, mine_idx = 2 * x + y, _block_index(x, y, c)
        local = [pltpu.make_async_copy(ins[a].at[my_chip], outs[a].at[my_chip], local_sems.at[a])
                 for a in range(n)]
        local += [pltpu.make_async_copy(alls[b].at[mine_idx], all_outs[b].at[mine_idx],
                                        local_sems.at[n + b]) for b in range(m)]
        for cp in local:
            cp.start()
        sends, arrivals = [], []
        k = 0
        for r in range(1, N_DEV):
            peer = (_flip(x, r & 4), _flip(y, r & 2), _flip(c, r & 1))
            peer_chip, peer_idx = 2 * peer[0] + peer[1], _block_index(*peer)
            pairs = [(alls[b].at[mine_idx], all_outs[b].at[mine_idx], all_outs[b].at[peer_idx])
                     for b in range(m)]
            if not r & 1:
                pairs += [(ins[a].at[peer_chip], outs[a].at[my_chip], outs[a].at[peer_chip])
                          for a in range(n)]
            for src, there, here in pairs:
                sems = dict(send_sem=send_sems.at[k], recv_sem=recv_sems.at[k], device_id=peer,
                            device_id_type=pl.DeviceIdType.MESH)
                sends.append(pltpu.make_async_remote_copy(src_ref=src, dst_ref=there, **sems))
                arrivals.append(pltpu.make_async_remote_copy(src_ref=src, dst_ref=here, **sems))
                k += 1
        for cp in sends:
            cp.start()
        for cp in arrivals:
            cp.wait_recv()
        for cp in sends:
            cp.wait_send()
        for cp in local:
            cp.wait()

    n_copies = 7 * m + 3 * n
    return pl.pallas_call(
        body, name=name,
        in_specs=[_ANY] * (n + m), out_specs=[_ANY] * (n + m),
        out_shape=[jax.ShapeDtypeStruct(v.shape, v.dtype) for v in by_chip + to_all],
        scratch_shapes=[pltpu.SemaphoreType.DMA((n_copies,)), pltpu.SemaphoreType.DMA((n_copies,)),
                        pltpu.SemaphoreType.DMA((n + m,))],
    )(*by_chip, *to_all)


class _Rider(NamedTuple):
    srcs: list
    lands: list
    scatter: bool
    part: list


def _new_lands(srcs, scatter):
    return [lax.empty(v.shape if scatter else (N_DEV,) + v.shape, v.dtype) for v in srcs]


def _rider_copies(srcs, lands, send_sems, recv_sems, rider):
    x, y, c = _my_place()
    mine_idx = _block_index(x, y, c)

    def window(ref, a):
        if rider.part[a] is None:
            return ref
        dim, start, size = rider.part[a]
        return ref.at[(slice(None),) * dim + (pl.ds(start, size),)]

    sends, arrivals = [], []
    for r in range(1, N_DEV):
        peer = (_flip(x, r & 4), _flip(y, r & 2), _flip(c, r & 1))
        peer_idx = _block_index(*peer)
        for a in range(len(srcs)):
            src = window(srcs[a].at[peer_idx] if rider.scatter else srcs[a], a)
            k = 7 * a + r - 1
            sems = dict(send_sem=send_sems.at[k], recv_sem=recv_sems.at[k],
                        device_id=peer, device_id_type=pl.DeviceIdType.MESH)
            sends.append(pltpu.make_async_remote_copy(
                src_ref=src, dst_ref=window(lands[a].at[mine_idx], a), **sems))
            arrivals.append(pltpu.make_async_remote_copy(
                src_ref=src, dst_ref=window(lands[a].at[peer_idx], a), **sems))
    return sends, arrivals


def _ride(call, rider):
    call = dict(call)
    body, grid = call.pop("body"), call["grid"]
    operands = call.pop("operands")
    if rider is None:
        return list(pl.pallas_call(body, **call)(*operands)), None
    n_in, n_out = len(call["in_specs"]), len(call["out_specs"])
    n_scratch = len(call["scratch_shapes"])
    m = len(rider.srcs)

    def riding(*refs):
        main_in, srcs, lands = refs[:n_in], refs[n_in:n_in + m], refs[n_in + m:n_in + 2 * m]
        main_out = refs[n_in + 2 * m:n_in + 2 * m + n_out]
        rest = refs[n_in + 3 * m + n_out:]
        send_sems, recv_sems = rest[n_scratch:]
        at = [pl.program_id(d) for d in range(len(grid))]
        first = functools.reduce(jnp.logical_and, [p == 0 for p in at])
        last = functools.reduce(jnp.logical_and, [p == g - 1 for p, g in zip(at, grid)])
        sends, arrivals = _rider_copies(srcs, lands, send_sems, recv_sems, rider)

        @pl.when(first)
        def _():
            for cp in sends:
                cp.start()

        body(*main_in, *main_out, *rest[:n_scratch])

        @pl.when(last)
        def _():
            for cp in arrivals:
                cp.wait_recv()
            for cp in sends:
                cp.wait_send()

    call["in_specs"] = list(call["in_specs"]) + [_ANY] * (2 * m)
    call["out_specs"] = list(call["out_specs"]) + [_ANY] * m
    call["out_shape"] = list(call["out_shape"]) + [
        jax.ShapeDtypeStruct(v.shape, v.dtype) for v in rider.lands]
    call["scratch_shapes"] = list(call["scratch_shapes"]) + [
        pltpu.SemaphoreType.DMA((7 * m,)), pltpu.SemaphoreType.DMA((7 * m,))]
    call["input_output_aliases"] = {n_in + m + a: n_out + a for a in range(m)}
    outs = pl.pallas_call(riding, **call)(*operands, *rider.srcs, *rider.lands)
    return list(outs[:n_out]), list(outs[n_out:])


def _sum_parts(p_ref):
    g = p_ref[0].astype(F32)
    for k in range(1, p_ref.shape[0]):
        g = g + p_ref[k].astype(F32)
    return g


def _adamw(g, w, m, v):
    c1 = 1.0 / (1.0 - ADAM_B1 ** ADAM_STEP)
    c2 = 1.0 / (1.0 - ADAM_B2 ** ADAM_STEP)
    nm = ADAM_B1 * m + (1.0 - ADAM_B1) * g
    nv = ADAM_B2 * v + (1.0 - ADAM_B2) * (g * g)
    return nm, nv, -ADAM_LR * ((nm * c1) / (jnp.sqrt(nv * c2) + ADAM_EPS) + ADAM_WD * w)


def _adamw_w_in(parts, w, m, v, name):
    ncol_blk, depth, nfeat = w.shape
    cols = 256

    def body(*refs):
        p_refs = refs[:depth]
        w_ref, m_ref, v_ref, g_ref, d_ref, nm_ref, nv_ref = refs[depth:]
        for l in range(depth):
            g = _sum_parts(p_refs[l])
            nm, nv, d = _adamw(g, w_ref[:, l, :], m_ref[:, l, :], v_ref[:, l, :])
            g_ref[:, l, :] = g
            nm_ref[:, l, :] = nm
            nv_ref[:, l, :] = nv
            d_ref[:, l, :] = d

    blk = pl.BlockSpec((ncol_blk, depth, cols), lambda j: (0, 0, j))
    return pl.pallas_call(
        body, name=name, grid=(nfeat // cols,),
        in_specs=[pl.BlockSpec((p.shape[0], ncol_blk, cols), lambda j: (0, 0, j)) for p in parts]
        + [blk] * 3,
        out_specs=[blk] * 4,
        out_shape=[jax.ShapeDtypeStruct(w.shape, F32)] * 4,
        compiler_params=_ARB1,
    )(*parts, w, m, v)


def _adamw_sum(parts, w, m, v, tile, name):
    depth, nrow, ncol = w.shape
    rows, cols = tile

    def body(*refs):
        p_refs = refs[:depth]
        w_ref, m_ref, v_ref, g_ref, d_ref, nm_ref, nv_ref = refs[depth:]
        layer = pl.program_id(0)
        for l in range(depth):
            @pl.when(layer == l)
            def _(p_ref=p_refs[l]):
                g = _sum_parts(p_ref)
                nm, nv, d = _adamw(g, w_ref[...], m_ref[...], v_ref[...])
                g_ref[...] = g
                nm_ref[...] = nm
                nv_ref[...] = nv
                d_ref[...] = d

    def part_spec(l):
        return pl.BlockSpec((parts[l].shape[0], rows, cols), lambda q, i, j: (
            0, jnp.where(q == l, i, 0), jnp.where(q == l, j, 0)))

    blk = pl.BlockSpec((None, rows, cols), lambda q, i, j: (q, i, j))
    return pl.pallas_call(
        body, name=name, grid=(depth, nrow // rows, ncol // cols),
        in_specs=[part_spec(l) for l in range(depth)] + [blk, blk, blk],
        out_specs=[blk] * 4,
        out_shape=[jax.ShapeDtypeStruct(w.shape, F32)] * 4,
        compiler_params=pltpu.CompilerParams(
            dimension_semantics=("arbitrary", "arbitrary", "arbitrary")),
    )(*parts, w, m, v)


def _pack_small(norm_w, mem_norm_w, out_norm_w, final_norm_w, b_forget):
    onw = jnp.pad(out_norm_w.reshape(20, LANES), ((0, 4), (0, 0)))
    b = jnp.pad(b_forget, ((0, 6), (0, LANES - FOX_HEADS)))
    return jnp.concatenate([norm_w.reshape(16, LANES), mem_norm_w.reshape(16, LANES), onw,
                            final_norm_w.reshape(8, LANES), b], axis=0)


def _unpack_small(p):
    return (p[0:16].reshape(2, D_MODEL), p[16:32].reshape(2, D_MODEL), p[32:52].reshape(2, MIX_W),
            p[56:64].reshape(D_MODEL), p[64:66, :FOX_HEADS])


def kernel(x, mem, norm_w, w_in, b_forget, mem_norm_w, w_mem_kv, out_norm_w, w_out, final_norm_w, loss_target, m_norm_w, m_w_in, m_b_forget, m_mem_norm_w, m_w_mem_kv, m_out_norm_w, m_w_out, m_final_norm_w, v_norm_w, v_w_in, v_b_forget, v_mem_norm_w, v_w_mem_kv, v_out_norm_w, v_w_out, v_final_norm_w):
    kv_rows = w_mem_kv.shape[1]
    out_rows = w_out.shape[1]
    me = _block_index(*_my_place())

    def shards(l):
        return [w_in[l].T.astype(BF16), w_mem_kv[l].astype(BF16), w_out[l].astype(BF16)]

    def full_in(g_in):
        return _rearrange_w_in(g_in.reshape(IN_W, D_MODEL))

    def full_kv_out(g_kv, g_out):
        return g_kv.reshape(D_MODEL, 2 * MEM_W), g_out.reshape(MIX_W, D_MODEL)

    def in_blocks(g):
        segments = [(0, 3072, 0), (3072, 3080, QKV_W + MIX_W), (3080, 3336, 3072),
                    (3336, IN_W, QKV_W)]

        def block(k):
            lo, hi = k * SHARD_W, (k + 1) * SHARD_W
            pieces = [g[at + max(lo, a) - a:at + min(hi, b) - a]
                      for a, b, at in segments if max(lo, a) < min(hi, b)]
            return jnp.concatenate(pieces, axis=0).astype(BF16)

        return jnp.stack([block(k) for k in range(N_DEV)])

    def kv_blocks(g):
        return g.reshape(N_DEV, kv_rows, 2 * MEM_W).astype(BF16)

    def out_blocks(g):
        return g.reshape(N_DEV, out_rows, D_MODEL).astype(BF16)

    def with_own(land, own):
        return lax.dynamic_update_slice(land, own[None], (me,) + (0,) * own.ndim)

    def with_own_of(land, blocked):
        return lax.dynamic_update_slice(land, lax.dynamic_slice_in_dim(blocked, me, 1, axis=0),
                                        (me,) + (0,) * (land.ndim - 1))

    def row(v):
        return v.reshape(1, -1)

    def cols(first, size):
        return (1, first, size)

    fwd_split, bwd_split = 5 * LANES, 6 * LANES

    g_in0, g_kv0, g_out0 = _all_gather_weights(shards(0), "all_gather_l0")
    w_r0 = full_in(g_in0)
    s_in1, s_kv1, s_out1 = shards(1)
    x1, saved0, (l_in1,), (wkv0, wout0) = _layer_fwd(
        x[0], mem[0], row(norm_w[0]), w_r0, b_forget[0], row(mem_norm_w[0]),
        lambda lands: full_kv_out(g_kv0, g_out0), row(out_norm_w[0]), 0,
        travel=([s_in1], False, [([0], [cols(0, fwd_split)]),
                                 ([0], [cols(fwd_split, D_MODEL - fwd_split)])]))
    w_r1 = full_in(with_own(l_in1, s_in1))
    x2, saved1, _, (wkv1, wout1) = _layer_fwd(
        x1, mem[0], row(norm_w[1]), w_r1, b_forget[1], row(mem_norm_w[1]),
        lambda lands: full_kv_out(with_own(lands[0], s_kv1), with_own(lands[1], s_out1)),
        row(out_norm_w[1]), 1, travel=([s_kv1, s_out1], False, [([0, 1], [None, None]), None]))

    dx2, loss_part, dfnw = _final_fwd_bwd(x2, row(final_norm_w), loss_target[0], "final_fwd_bwd")

    dx1, gr1, (l_out1,) = _layer_bwd(
        dx2, saved1, mem[0], w_r1, wkv1, wout1, 1,
        travel=lambda dwout: ([out_blocks(dwout)], True, [([0], [None]), None]))
    p_in1, p_kv1 = in_blocks(gr1["w_r"]), kv_blocks(gr1["w_mem_kv"])
    grad_x, gr0, (l_in1, l_kv1, l_out0) = _layer_bwd(
        dx1, saved0, mem[0], w_r0, wkv0, wout0, 0,
        travel=lambda dwout: ([p_in1, p_kv1, out_blocks(dwout)], True,
                              [([0, 1], [cols(0, bwd_split), None]),
                               ([0, 2], [cols(bwd_split, D_MODEL - bwd_split), None])]))
    r_out1 = with_own_of(l_out1, out_blocks(gr1["w_out"]))
    r_in1, r_kv1 = with_own_of(l_in1, p_in1), with_own_of(l_kv1, p_kv1)
    r_out0 = with_own_of(l_out0, out_blocks(gr0["w_out"]))

    def both(name):
        return jnp.stack([gr0[name], gr1[name]])

    small = _pack_small(both("norm_w"), both("mem_norm_w"), both("out_norm_w"), dfnw[0],
                        both("b_forget")).at[LOSS_ROW].set(loss_part[0])
    p_small = jnp.broadcast_to(small[None], (N_DEV, SMALL_ROWS, LANES))
    by_core = [v.reshape((N_CHIP, 2) + v.shape[1:])
               for v in (in_blocks(gr0["w_r"]), kv_blocks(gr0["w_mem_kv"]))]
    from_sibling = _pair_swap(by_core, "grads_l0_pair_swap")
    core = lax.axis_index("c")
    chip_sums = [_pair_add(lax.dynamic_index_in_dim(v, core, axis=1, keepdims=False), got,
                           f"grads_l0_pair_add_{a}")
                 for a, (v, got) in enumerate(zip(by_core, from_sibling))]
    r_in0, r_kv0, r_small = _chip_exchange(chip_sums, [p_small], "exchange_grads_l0")

    def view(v):
        return jnp.transpose(v, (2, 0, 1))

    g_w_in, d_w_in, nm_w_in, nv_w_in = [jnp.transpose(v, (1, 2, 0)) for v in _adamw_w_in(
        [r_in0, r_in1], view(w_in), view(m_w_in), view(v_w_in), "adamw_w_in")]
    g_w_kv, d_w_kv, nm_w_kv, nv_w_kv = _adamw_sum(
        [r_kv0, r_kv1], w_mem_kv, m_w_mem_kv, v_w_mem_kv, (kv_rows, 2 * MEM_W), "adamw_w_mem_kv")
    g_w_out, d_w_out, nm_w_out, nv_w_out = _adamw_sum(
        [r_out0, r_out1], w_out, m_w_out, v_w_out, (out_rows, D_MODEL), "adamw_w_out")
    w_small = _pack_small(norm_w, mem_norm_w, out_norm_w, final_norm_w, b_forget)[None]
    m_small = _pack_small(m_norm_w, m_mem_norm_w, m_out_norm_w, m_final_norm_w, m_b_forget)[None]
    v_small = _pack_small(v_norm_w, v_mem_norm_w, v_out_norm_w, v_final_norm_w, v_b_forget)[None]
    small_out = _adamw_sum([r_small], w_small, m_small, v_small, (SMALL_ROWS, LANES), "adamw_small")
    (g_nw, g_mnw, g_onw, g_fnw, g_b), (d_nw, d_mnw, d_onw, d_fnw, d_b), \
        (nm_nw, nm_mnw, nm_onw, nm_fnw, nm_b), (nv_nw, nv_mnw, nv_onw, nv_fnw, nv_b) = [
            _unpack_small(t[0]) for t in small_out]
    loss = small_out[0][0, LOSS_ROW, 0]

    return (loss, grad_x[None],
            g_nw, g_w_in, g_b, g_mnw, g_w_kv, g_onw, g_w_out, g_fnw,
            d_nw, d_w_in, d_b, d_mnw, d_w_kv, d_onw, d_w_out, d_fnw,
            nm_nw, nm_w_in, nm_b, nm_mnw, nm_w_kv, nm_onw, nm_w_out, nm_fnw,
            nv_nw, nv_w_in, nv_b, nv_mnw, nv_w_kv, nv_onw, nv_w_out, nv_fnw)
```

```python
import functools
from typing import NamedTuple

import jax
import jax.numpy as jnp
from jax import lax
from jax.experimental import pallas as pl
from jax.experimental.pallas import tpu as pltpu

F32 = jnp.float32
BF16 = jnp.bfloat16

N_DEV = 8
D_MODEL = 1024
HEAD_DIM = 64
LANES = 128
SB_W = 512
FX_W = 512
MEM_W = 256
MIX_W = 1280
FOX_HEADS = 8
IN_W = 4616
SHARD_W = IN_W // N_DEV
QKV_W = 3 * SB_W + 3 * FX_W + MEM_W
FL_PAD = 256
GF_W = MIX_W + FL_PAD
WR_W = QKV_W + GF_W
EPS = 1e-6
T = 256
QPS = 2
TM = 256
TQM = 512
Q_SCALE = 0.125
NEG = -1e30
UNDERFLOW = -110.0
NORM_SLACK = 1.01

ADAM_LR = 0.001
ADAM_B1 = 0.9
ADAM_B2 = 0.999
ADAM_EPS = 1e-08
ADAM_WD = 0.01
ADAM_STEP = 10

SMALL_ROWS = 72
LOSS_ROW = 66

_NT = (((1,), (1,)), ((), ()))
_TN = (((0,), (0,)), ((), ()))

_ARB1 = pltpu.CompilerParams(dimension_semantics=("arbitrary",))
_ARB2 = pltpu.CompilerParams(dimension_semantics=("arbitrary", "arbitrary"))


def _dot(a, b):
    return jnp.dot(a, b, preferred_element_type=F32)


def _dot_nt(a, b):
    return lax.dot_general(a, b, _NT, preferred_element_type=F32)


def _dot_tn(a, b):
    return lax.dot_general(a, b, _TN, preferred_element_type=F32)


def _split2(x):
    hi = x.astype(BF16)
    lo = (x - hi.astype(F32)).astype(BF16)
    return hi, lo


def _stack2(u):
    return jnp.concatenate([u, u], axis=0)


def _cum2(x, u2):
    hi, lo = _split2(x)
    return _dot(jnp.concatenate([hi, lo], axis=1), u2)


def _tri3(tri, x, dot=None):
    dot = dot or _dot
    hi = x.astype(BF16)
    r1 = x - hi.astype(F32)
    mid = r1.astype(BF16)
    lo = (r1 - mid.astype(F32)).astype(BF16)
    return dot(tri, hi) + dot(tri, mid) + dot(tri, lo)


def _iota2(shape, dim):
    return lax.broadcasted_iota(jnp.int32, shape, dim)


def _head_block_diag():
    r = _iota2((LANES, LANES), 0) // HEAD_DIM
    c = _iota2((LANES, LANES), 1) // HEAD_DIM
    return _stack2(jnp.where(r == c, 1.0, 0.0).astype(BF16))


def _head_mean(x, bd):
    return _cum2(x, bd) * (1.0 / HEAD_DIM)


def _sigmoid(x):
    return 1.0 / (1.0 + jnp.exp(-x))


def _log_sigmoid(x):
    return jnp.minimum(x, 0.0) - jnp.log(1.0 + jnp.exp(-jnp.abs(x)))


def _running_top(r_ref):
    return jnp.max(jnp.maximum(r_ref[0], r_ref[1]))


def _fox_tiles_left(i, pair, nq, fb_ref, tile):
    def bound(j):
        b = []
        for h in range(2):
            head = 2 * pair + h
            b.append(2.0 * NORM_SLACK * fb_ref[2 * nq + i, head] * fb_ref[3 * nq, head]
                     + fb_ref[2 * i, head] - fb_ref[2 * j + 1, head])
        return jnp.maximum(b[0], b[1])

    def more(j):
        return jnp.logical_and(j >= 0, bound(jnp.maximum(j, 0)) > UNDERFLOW)

    def step(j):
        tile(j, False)
        return j - 1

    return lax.while_loop(more, step, i - 1)


def _pair_masks():
    lane = _iota2((1, LANES), 1)
    return lane < HEAD_DIM


def _split_pair(x, m0):
    zero = jnp.zeros_like(x)
    return jnp.where(m0, x, zero), jnp.where(m0, zero, x)


def _inproj_fwd(x, nw, w_r, name):
    s = x.shape[0]

    def body(x_ref, nw_ref, w_ref, qkv_ref, gf_ref):
        xv = x_ref[...]
        r = lax.rsqrt(jnp.mean(xv * xv, axis=-1, keepdims=True) + EPS)
        h = (xv * r * nw_ref[...]).astype(BF16)
        for c in range(0, QKV_W, 256):
            qkv_ref[:, c:c + 256] = _dot_nt(h, w_ref[c:c + 256, :]).astype(BF16)
        for c in range(0, GF_W, 256):
            gf_ref[:, c:c + 256] = _dot_nt(h, w_ref[QKV_W + c:QKV_W + c + 256, :])

    return pl.pallas_call(
        body, name=name, grid=(s // TM,),
        in_specs=[pl.BlockSpec((TM, D_MODEL), lambda i: (i, 0)),
                  pl.BlockSpec((1, D_MODEL), lambda i: (0, 0)),
                  pl.BlockSpec((WR_W, D_MODEL), lambda i: (0, 0))],
        out_specs=[pl.BlockSpec((TM, QKV_W), lambda i: (i, 0)),
                   pl.BlockSpec((TM, GF_W), lambda i: (i, 0))],
        out_shape=[jax.ShapeDtypeStruct((s, QKV_W), BF16), jax.ShapeDtypeStruct((s, GF_W), F32)],
        compiler_params=_ARB1,
    )(x, nw, w_r)


def _fox_prep_fwd(gf, qkv, bpad, name):
    s = gf.shape[0]
    nq = s // T
    nrow = -(-(3 * nq + 1) // 8) * 8

    def body(fl_ref, q_ref, k_ref, b_ref, fq_ref, fr_ref, fb_ref):
        tri = jnp.where(_iota2((T, T), 0) >= _iota2((T, T), 1), 1.0, 0.0).astype(BF16)
        m0 = _pair_masks()
        lane = _iota2((1, LANES), 1)
        norms = [jnp.zeros((1, LANES), F32) for _ in range(nq + 1)]
        same_head = (_iota2((LANES, LANES), 0) // HEAD_DIM) == (_iota2((LANES, LANES), 1) // HEAD_DIM)
        bd = jnp.where(same_head, 1.0, 0.0).astype(BF16)
        for p in range(FOX_HEADS // 2):
            cols = slice(p * LANES, (p + 1) * LANES)
            q = (q_ref[:, cols] * jnp.asarray(Q_SCALE, BF16)).astype(F32)
            k = k_ref[:, cols].astype(F32)
            qn = _dot((q * q).astype(BF16), bd)
            kn = _dot((k * k).astype(BF16), bd)
            tops = [jnp.max(qn[j * T:(j + 1) * T], axis=0, keepdims=True) for j in range(nq)]
            tops.append(jnp.max(kn, axis=0, keepdims=True))
            tops = [jnp.sqrt(top) for top in tops]
            for h in range(2):
                at = h * HEAD_DIM
                norms = [jnp.where(lane == 2 * p + h, top[:, at:at + 1], row)
                         for top, row in zip(tops, norms)]
        for j in range(nq + 1):
            fb_ref[2 * nq + j:2 * nq + j + 1, :] = norms[j]
        fb_ref[3 * nq + 1:, :] = jnp.zeros((nrow - 3 * nq - 1, LANES), F32)
        carry = jnp.zeros((1, LANES), F32)
        for blk in range(s // T):
            rows = slice(blk * T, (blk + 1) * T)
            lf = _log_sigmoid(fl_ref[rows, :] + b_ref[...])
            c = _tri3(tri, lf) + carry
            carry = c[T - 1:T, :]
            for p in range(FOX_HEADS // 2):
                fq_ref[rows, p * LANES:(p + 1) * LANES] = jnp.where(
                    m0, c[:, 2 * p:2 * p + 1], c[:, 2 * p + 1:2 * p + 2])
            fr_ref[:, rows] = c.T[0:FOX_HEADS, :]
            fb_ref[2 * blk:2 * blk + 1, :] = c[0:1, :]
            fb_ref[2 * blk + 1:2 * blk + 2, :] = carry

    base = 3 * SB_W // FX_W
    return pl.pallas_call(
        body, name=name, grid=(1,),
        in_specs=[pl.BlockSpec((s, LANES), lambda i: (0, MIX_W // LANES)),
                  pl.BlockSpec((s, FX_W), lambda i: (0, base)),
                  pl.BlockSpec((s, FX_W), lambda i: (0, base + 1)),
                  pl.BlockSpec((1, LANES), lambda i: (0, 0))],
        out_specs=[pl.BlockSpec((s, FX_W), lambda i: (0, 0)),
                   pl.BlockSpec((FOX_HEADS, s), lambda i: (0, 0)),
                   pl.BlockSpec((nrow, LANES), lambda i: (0, 0))],
        out_shape=[jax.ShapeDtypeStruct((s, FX_W), F32), jax.ShapeDtypeStruct((FOX_HEADS, s), F32),
                   jax.ShapeDtypeStruct((nrow, LANES), F32)],
        compiler_params=_ARB1,
    )(gf, qkv, qkv, bpad)


def _sb_fwd(qkv, name, rider=None):
    s = qkv.shape[0]

    def body(q_ref, k_ref, v_ref, o_ref, acc_ref, r_ref, as_ref):
        m0 = _pair_masks()
        strict = _iota2((T, T), 0) > _iota2((T, T), 1)
        u2 = _stack2(jnp.where(strict, 1.0, 0.0).astype(BF16))
        hs = range(2)

        def query_tile(i, rows):
            qh = _split_pair(q_ref[rows, :] * jnp.asarray(Q_SCALE, BF16), m0)
            acc_ref[...] = jnp.zeros_like(acc_ref)
            r_ref[...] = jnp.zeros_like(r_ref)

            def flush(j):
                v = v_ref[pl.ds(pl.multiple_of(j * T, T), T), :]
                for h in hs:
                    acc_ref[h] += _dot(as_ref[h], v)

            def tile(j, diag):
                k = k_ref[pl.ds(pl.multiple_of(j * T, T), T), :]
                z = [_dot_nt(qh[h], k) for h in hs]
                if not diag:
                    flush(j + 1)
                la = [jnp.minimum(z[h], 0.0) - jnp.log(1.0 + jnp.exp(-jnp.abs(z[h]))) for h in hs]
                lf = [la[h] - z[h] for h in hs]
                if diag:
                    lf = [jnp.where(strict, lf[h], 0.0) for h in hs]
                cin = [_cum2(lf[h], u2) for h in hs]
                a = [jnp.exp(la[h] + cin[h] + r_ref[h]) for h in hs]
                if diag:
                    a = [jnp.where(strict, a[h], 0.0) for h in hs]
                for h in hs:
                    r_ref[h] += cin[h][:, 0:1] + lf[h][:, 0:1]
                    as_ref[h] = a[h].astype(BF16)

            tile(i, True)

            def more(state):
                j, top = state
                return jnp.logical_and(j >= 0, top > UNDERFLOW)

            def step(state):
                j, _ = state
                tile(j, False)
                return j - 1, _running_top(r_ref)

            j_left, _ = lax.while_loop(more, step, (i - 1, _running_top(r_ref)))
            flush(j_left + 1)
            o_ref[rows, :] = jnp.where(m0, acc_ref[0], acc_ref[1])

        for n in range(QPS):
            query_tile(QPS * pl.program_id(1) + n, slice(n * T, (n + 1) * T))

    nb = SB_W // LANES
    (ysb,), lands = _ride(dict(
        body=body, name=name, grid=(nb, s // (QPS * T)),
        in_specs=[pl.BlockSpec((QPS * T, LANES), lambda p, i: (i, p)),
                  pl.BlockSpec((s, LANES), lambda p, i: (0, nb + p)),
                  pl.BlockSpec((s, LANES), lambda p, i: (0, 2 * nb + p))],
        out_specs=[pl.BlockSpec((QPS * T, LANES), lambda p, i: (i, p))],
        out_shape=[jax.ShapeDtypeStruct((s, SB_W), F32)],
        scratch_shapes=[pltpu.VMEM((2, T, LANES), F32), pltpu.VMEM((2, T, 1), F32),
                        pltpu.VMEM((2, T, T), BF16)],
        compiler_params=_ARB2, operands=[qkv, qkv, qkv]), rider)
    return ysb, lands


def _fox_fwd(qkv, fqb, frow, fbounds, name, rider=None):
    s = qkv.shape[0]

    def body(q_ref, k_ref, v_ref, fq_ref, fr_ref, fb_ref, o_ref, lse_ref, acc_ref, m_ref, ps_ref):
        pair = pl.program_id(0)
        m0 = _pair_masks()
        causal = _iota2((T, T), 0) >= _iota2((T, T), 1)
        ones = jnp.ones((T, LANES), BF16)
        hs = range(2)

        def query_tile(i, rows):
            qh = _split_pair(q_ref[rows, :] * jnp.asarray(Q_SCALE, BF16), m0)
            fq = fq_ref[rows, :]
            fqh = (fq[:, 0:1], fq[:, HEAD_DIM:HEAD_DIM + 1])
            acc_ref[...] = jnp.zeros_like(acc_ref)
            m_ref[...] = jnp.full_like(m_ref, NEG)

            def flush(j):
                v = v_ref[pl.ds(pl.multiple_of(j * T, T), T), :]
                va2 = _stack2(jnp.concatenate([v, ones], axis=1))
                for h in hs:
                    acc_ref[h] += _dot(ps_ref[h], va2)

            def tile(j, diag):
                off = pl.multiple_of(j * T, T)
                k = k_ref[pl.ds(off, T), :]
                sc = [_dot_nt(qh[h], k) + fqh[h] - fr_ref[h:h + 1, pl.ds(off, T)] for h in hs]
                if not diag:
                    flush(j + 1)
                if diag:
                    sc = [jnp.where(causal, sc[h], NEG) for h in hs]
                m_new = [jnp.maximum(m_ref[h], jnp.max(sc[h], axis=1, keepdims=True)) for h in hs]
                p = [jnp.exp(sc[h] - m_new[h]) for h in hs]
                for h in hs:
                    acc_ref[h] = acc_ref[h] * jnp.exp(m_ref[h] - m_new[h])
                    m_ref[h] = m_new[h]
                    p_hi, p_lo = _split2(p[h])
                    ps_ref[h] = jnp.concatenate([p_hi, p_lo], axis=1)

            tile(i, True)
            j_left = _fox_tiles_left(i, pair, s // T, fb_ref, tile)
            flush(j_left + 1)
            acc = (acc_ref[0], acc_ref[1])
            o_ref[rows, :] = jnp.where(m0, acc[0][:, :LANES] / acc[0][:, LANES:],
                                       acc[1][:, :LANES] / acc[1][:, LANES:])
            lse_ref[rows, :] = jnp.where(m0, m_ref[0] + jnp.log(acc[0][:, LANES:]),
                                         m_ref[1] + jnp.log(acc[1][:, LANES:]))

        for n in range(QPS):
            query_tile(QPS * pl.program_id(1) + n, slice(n * T, (n + 1) * T))

    nb = FX_W // LANES
    base = 3 * SB_W // LANES
    (yfx, lse), lands = _ride(dict(
        body=body, name=name, grid=(nb, s // (QPS * T)),
        in_specs=[pl.BlockSpec((QPS * T, LANES), lambda p, i: (i, base + p)),
                  pl.BlockSpec((s, LANES), lambda p, i: (0, base + nb + p)),
                  pl.BlockSpec((s, LANES), lambda p, i: (0, base + 2 * nb + p)),
                  pl.BlockSpec((QPS * T, LANES), lambda p, i: (i, p)),
                  pl.BlockSpec((None, 2, s), lambda p, i: (p, 0, 0)),
                  pl.BlockSpec(memory_space=pltpu.SMEM)],
        out_specs=[pl.BlockSpec((QPS * T, LANES), lambda p, i: (i, p)),
                   pl.BlockSpec((QPS * T, LANES), lambda p, i: (i, p))],
        out_shape=[jax.ShapeDtypeStruct((s, FX_W), F32), jax.ShapeDtypeStruct((s, FX_W), F32)],
        scratch_shapes=[pltpu.VMEM((2, T, 2 * LANES), F32), pltpu.VMEM((2, T, 1), F32),
                        pltpu.VMEM((2, T, 2 * T), BF16)],
        compiler_params=_ARB2, operands=[qkv, qkv, qkv, fqb, frow, fbounds]), rider)
    return yfx, lse, lands


def _memkv_fwd(mem, mnw, wkv, name):
    n = mem.shape[0]

    def body(mem_ref, mnw_ref, w_ref, kv_ref):
        mv = mem_ref[...]
        r = lax.rsqrt(jnp.mean(mv * mv, axis=-1, keepdims=True) + EPS)
        hm = (mv * r * mnw_ref[...]).astype(BF16)
        kv_ref[...] = _dot(hm, w_ref[...]).astype(BF16)

    return pl.pallas_call(
        body, name=name, grid=(1,),
        in_specs=[pl.BlockSpec((n, D_MODEL), lambda i: (0, 0)),
                  pl.BlockSpec((1, D_MODEL), lambda i: (0, 0)),
                  pl.BlockSpec((D_MODEL, 2 * MEM_W), lambda i: (0, 0))],
        out_specs=pl.BlockSpec((n, 2 * MEM_W), lambda i: (0, 0)),
        out_shape=jax.ShapeDtypeStruct((n, 2 * MEM_W), BF16),
        compiler_params=_ARB1,
    )(mem, mnw, wkv)


def _mem_fwd(qkv, kv, name):
    s = qkv.shape[0]
    n = kv.shape[0]

    def body(q_ref, k_ref, v_ref, o_ref, lse_ref):
        m0 = _pair_masks()
        qh = _split_pair(q_ref[...] * jnp.asarray(Q_SCALE, BF16), m0)
        k = k_ref[...]
        v = v_ref[...]
        outs, lses = [], []
        for h in range(2):
            sc = _dot_nt(qh[h], k)
            mx = jnp.max(sc, axis=1, keepdims=True)
            p = jnp.exp(sc - mx)
            l = jnp.sum(p, axis=1, keepdims=True)
            outs.append(_dot(p.astype(BF16), v) / l)
            lses.append(mx + jnp.log(l))
        o_ref[...] = jnp.where(m0, outs[0], outs[1])
        lse_ref[...] = jnp.where(m0, lses[0], lses[1])

    nb = MEM_W // LANES
    base = (3 * SB_W + 3 * FX_W) // LANES
    return pl.pallas_call(
        body, name=name, grid=(nb, s // TQM),
        in_specs=[pl.BlockSpec((TQM, LANES), lambda p, i: (i, base + p)),
                  pl.BlockSpec((n, LANES), lambda p, i: (0, p)),
                  pl.BlockSpec((n, LANES), lambda p, i: (0, nb + p))],
        out_specs=[pl.BlockSpec((TQM, LANES), lambda p, i: (i, p)),
                   pl.BlockSpec((TQM, LANES), lambda p, i: (i, p))],
        out_shape=[jax.ShapeDtypeStruct((s, MEM_W), F32), jax.ShapeDtypeStruct((s, MEM_W), F32)],
        compiler_params=_ARB2,
    )(qkv, kv, kv)


def _mix_chunk(c, ysb_ref, yfx_ref, ym_ref):
    if c < SB_W // LANES:
        return ysb_ref[:, c * LANES:(c + 1) * LANES]
    c -= SB_W // LANES
    if c < FX_W // LANES:
        return yfx_ref[:, c * LANES:(c + 1) * LANES]
    c -= FX_W // LANES
    return ym_ref[:, c * LANES:(c + 1) * LANES]


def _outproj_fwd(ysb, yfx, ym, gf, onw, wout, x, name):
    s = x.shape[0]

    def body(ysb_ref, yfx_ref, ym_ref, g_ref, onw_ref, w_ref, x_ref, o_ref, yg_ref):
        bd = _head_block_diag()
        for c in range(MIX_W // LANES):
            sl = slice(c * LANES, (c + 1) * LANES)
            u = _mix_chunk(c, ysb_ref, yfx_ref, ym_ref)
            r = lax.rsqrt(_head_mean(u * u, bd) + EPS)
            g = g_ref[:, sl]
            yg_ref[:, sl] = (u * r * onw_ref[:, sl] * (g * _sigmoid(g))).astype(BF16)
        o_ref[...] = x_ref[...] + _dot(yg_ref[...], w_ref[...])

    return pl.pallas_call(
        body, name=name, grid=(s // TM,),
        in_specs=[pl.BlockSpec((TM, SB_W), lambda i: (i, 0)),
                  pl.BlockSpec((TM, FX_W), lambda i: (i, 0)),
                  pl.BlockSpec((TM, MEM_W), lambda i: (i, 0)),
                  pl.BlockSpec((TM, MIX_W), lambda i: (i, 0)),
                  pl.BlockSpec((1, MIX_W), lambda i: (0, 0)),
                  pl.BlockSpec((MIX_W, D_MODEL), lambda i: (0, 0)),
                  pl.BlockSpec((TM, D_MODEL), lambda i: (i, 0))],
        out_specs=pl.BlockSpec((TM, D_MODEL), lambda i: (i, 0)),
        out_shape=jax.ShapeDtypeStruct((s, D_MODEL), F32),
        scratch_shapes=[pltpu.VMEM((TM, MIX_W), BF16)],
        compiler_params=_ARB1,
    )(ysb, yfx, ym, gf, onw, wout, x)


def _final_fwd_bwd(x, fnw, target, name):
    s = x.shape[0]

    def body(x_ref, w_ref, t_ref, dx_ref, loss_ref, dw_ref):
        @pl.when(pl.program_id(0) == 0)
        def _():
            loss_ref[...] = jnp.zeros_like(loss_ref)
            dw_ref[...] = jnp.zeros_like(dw_ref)

        xv = x_ref[...]
        w = w_ref[...]
        r = lax.rsqrt(jnp.mean(xv * xv, axis=-1, keepdims=True) + EPS)
        xh = xv * r
        err = xh * w - t_ref[...]
        part = jnp.sum(jnp.sum(err * err, axis=1, keepdims=True), axis=0, keepdims=True)
        loss_ref[...] += part * (0.5 / D_MODEL)
        dy = err * (1.0 / D_MODEL)
        dw_ref[...] += jnp.sum(dy * xh, axis=0, keepdims=True)
        dxh = dy * w
        dx_ref[...] = r * (dxh - xh * jnp.mean(dxh * xh, axis=-1, keepdims=True))

    return pl.pallas_call(
        body, name=name, grid=(s // TM,),
        in_specs=[pl.BlockSpec((TM, D_MODEL), lambda i: (i, 0)),
                  pl.BlockSpec((1, D_MODEL), lambda i: (0, 0)),
                  pl.BlockSpec((TM, D_MODEL), lambda i: (i, 0))],
        out_specs=[pl.BlockSpec((TM, D_MODEL), lambda i: (i, 0)),
                   pl.BlockSpec((1, LANES), lambda i: (0, 0)),
                   pl.BlockSpec((1, D_MODEL), lambda i: (0, 0))],
        out_shape=[jax.ShapeDtypeStruct((s, D_MODEL), F32), jax.ShapeDtypeStruct((1, LANES), F32),
                   jax.ShapeDtypeStruct((1, D_MODEL), F32)],
        compiler_params=_ARB1,
    )(x, fnw, target)


def _outproj_bwd(dxo, wout, ysb, yfx, ym, gf, onw, name):
    s = dxo.shape[0]

    def body(dx_ref, w_ref, ysb_ref, yfx_ref, ym_ref, g_ref, onw_ref,
             dysb_ref, dyfx_ref, dym_ref, dg_ref, dw_ref, donw_ref, yg_ref):
        @pl.when(pl.program_id(0) == 0)
        def _():
            dw_ref[...] = jnp.zeros_like(dw_ref)
            donw_ref[...] = jnp.zeros_like(donw_ref)

        dxb = dx_ref[...].astype(BF16)
        dyg = _dot_nt(dxb, w_ref[...])
        bd = _head_block_diag()
        for c in range(MIX_W // LANES):
            sl = slice(c * LANES, (c + 1) * LANES)
            u = _mix_chunk(c, ysb_ref, yfx_ref, ym_ref)
            r = lax.rsqrt(_head_mean(u * u, bd) + EPS)
            yn = u * r
            g = g_ref[:, sl]
            sg = _sigmoid(g)
            sil = g * sg
            onw = onw_ref[:, sl]
            e = dyg[:, sl]
            yg_ref[:, sl] = (yn * onw * sil).astype(BF16)
            donw_ref[:, sl] += jnp.sum(e * yn * sil, axis=0, keepdims=True)
            dg_ref[:, sl] = (e * yn * onw * (sg * (1.0 + g * (1.0 - sg)))).astype(BF16)
            dyn = e * onw * sil
            du = (r * (dyn - yn * _head_mean(dyn * yn, bd))).astype(BF16)
            if c < 4:
                dysb_ref[:, c * LANES:(c + 1) * LANES] = du
            elif c < 8:
                dyfx_ref[:, (c - 4) * LANES:(c - 3) * LANES] = du
            else:
                dym_ref[:, (c - 8) * LANES:(c - 7) * LANES] = du
        dw_ref[...] += _dot_tn(yg_ref[...], dxb)

    return pl.pallas_call(
        body, name=name, grid=(s // TM,),
        in_specs=[pl.BlockSpec((TM, D_MODEL), lambda i: (i, 0)),
                  pl.BlockSpec((MIX_W, D_MODEL), lambda i: (0, 0)),
                  pl.BlockSpec((TM, SB_W), lambda i: (i, 0)),
                  pl.BlockSpec((TM, FX_W), lambda i: (i, 0)),
                  pl.BlockSpec((TM, MEM_W), lambda i: (i, 0)),
                  pl.BlockSpec((TM, MIX_W), lambda i: (i, 0)),
                  pl.BlockSpec((1, MIX_W), lambda i: (0, 0))],
        out_specs=[pl.BlockSpec((TM, SB_W), lambda i: (i, 0)),
                   pl.BlockSpec((TM, FX_W), lambda i: (i, 0)),
                   pl.BlockSpec((TM, MEM_W), lambda i: (i, 0)),
                   pl.BlockSpec((TM, MIX_W), lambda i: (i, 0)),
                   pl.BlockSpec((MIX_W, D_MODEL), lambda i: (0, 0)),
                   pl.BlockSpec((1, MIX_W), lambda i: (0, 0))],
        out_shape=[jax.ShapeDtypeStruct((s, SB_W), BF16), jax.ShapeDtypeStruct((s, FX_W), BF16),
                   jax.ShapeDtypeStruct((s, MEM_W), BF16), jax.ShapeDtypeStruct((s, MIX_W), BF16),
                   jax.ShapeDtypeStruct((MIX_W, D_MODEL), F32), jax.ShapeDtypeStruct((1, MIX_W), F32)],
        scratch_shapes=[pltpu.VMEM((TM, MIX_W), BF16)],
        compiler_params=_ARB1,
    )(dxo, wout, ysb, yfx, ym, gf, onw)


def _row_dots(do, o, m0):
    prod = do.astype(F32) * o
    zero = jnp.zeros_like(prod)
    return (jnp.sum(jnp.where(m0, prod, zero), axis=1, keepdims=True),
            jnp.sum(jnp.where(m0, zero, prod), axis=1, keepdims=True))


def _sb_bwd(qkv, o, do, name, rider=None):
    s = qkv.shape[0]
    nq = s // T

    def body(q_ref, k_ref, v_ref, o_ref, do_ref, dq_ref, dk_ref, dv_ref,
             dqa_ref, dka_ref, dva_ref, rl_ref, rg_ref, dzs_ref, abs_ref):
        step_id = pl.program_id(1)

        @pl.when(step_id == 0)
        def _():
            dka_ref[...] = jnp.zeros_like(dka_ref)
            dva_ref[...] = jnp.zeros_like(dva_ref)

        m0 = _pair_masks()
        strict = _iota2((T, T), 0) > _iota2((T, T), 1)
        u2 = _stack2(jnp.where(strict, 1.0, 0.0).astype(BF16))
        hs = range(2)

        def query_tile(i, rows):
            qh = _split_pair(q_ref[rows, :] * jnp.asarray(Q_SCALE, BF16), m0)
            doh = _split_pair(do_ref[rows, :], m0)
            dsum = _row_dots(do_ref[rows, :], o_ref[rows, :], m0)
            dqa_ref[...] = jnp.zeros_like(dqa_ref)
            rl_ref[...] = jnp.zeros_like(rl_ref)
            rg_ref[...] = jnp.zeros_like(rg_ref)

            def flush(j):
                off = pl.multiple_of(j * T, T)
                k = k_ref[pl.ds(off, T), :]
                for h in hs:
                    dqa_ref[h] += _dot(dzs_ref[h], k)
                dka_ref[pl.ds(off, T), :] += (_dot_tn(dzs_ref[0], qh[0])
                                              + _dot_tn(dzs_ref[1], qh[1]))
                dva_ref[pl.ds(off, T), :] += (_dot_tn(abs_ref[0], doh[0])
                                              + _dot_tn(abs_ref[1], doh[1]))

            def tile(j, diag):
                off = pl.multiple_of(j * T, T)
                k = k_ref[pl.ds(off, T), :]
                v = v_ref[pl.ds(off, T), :]
                z = [_dot_nt(qh[h], k) for h in hs]
                da = [_dot_nt(doh[h], v) for h in hs]
                if not diag:
                    flush(j + 1)
                la = [jnp.minimum(z[h], 0.0) - jnp.log(1.0 + jnp.exp(-jnp.abs(z[h]))) for h in hs]
                lf = [la[h] - z[h] for h in hs]
                if diag:
                    lf = [jnp.where(strict, lf[h], 0.0) for h in hs]
                cin = [_cum2(lf[h], u2) for h in hs]
                a = [jnp.exp(la[h] + cin[h] + rl_ref[h]) for h in hs]
                if diag:
                    a = [jnp.where(strict, a[h], 0.0) for h in hs]
                ab = [a[h].astype(BF16) for h in hs]
                g = [ab[h].astype(F32) * da[h] for h in hs]
                gin = [_cum2(g[h], u2) for h in hs]
                dz = [g[h] - jnp.exp(la[h]) * ((dsum[h] - rg_ref[h]) - gin[h]) for h in hs]
                if diag:
                    dz = [jnp.where(strict, dz[h], 0.0) for h in hs]
                for h in hs:
                    rl_ref[h] += cin[h][:, 0:1] + lf[h][:, 0:1]
                    rg_ref[h] += gin[h][:, 0:1] + g[h][:, 0:1]
                    dzs_ref[h] = dz[h].astype(BF16)
                    abs_ref[h] = ab[h]

            tile(i, True)

            def more(state):
                j, top = state
                return jnp.logical_and(j >= 0, top > UNDERFLOW)

            def step(state):
                j, _ = state
                tile(j, False)
                return j - 1, _running_top(rl_ref)

            j_left, _ = lax.while_loop(more, step, (i - 1, _running_top(rl_ref)))
            flush(j_left + 1)
            dq_ref[rows, :] = (jnp.where(m0, dqa_ref[0], dqa_ref[1]) * Q_SCALE).astype(BF16)

        for n in range(QPS):
            query_tile(QPS * step_id + n, slice(n * T, (n + 1) * T))

        @pl.when(step_id == nq // QPS - 1)
        def _():
            dk_ref[...] = dka_ref[...].astype(BF16)
            dv_ref[...] = dva_ref[...].astype(BF16)

    nb = SB_W // LANES
    (dq, dk, dv), lands = _ride(dict(
        body=body, name=name, grid=(nb, nq // QPS),
        in_specs=[pl.BlockSpec((QPS * T, LANES), lambda p, i: (i, p)),
                  pl.BlockSpec((s, LANES), lambda p, i: (0, nb + p)),
                  pl.BlockSpec((s, LANES), lambda p, i: (0, 2 * nb + p)),
                  pl.BlockSpec((QPS * T, LANES), lambda p, i: (i, p)),
                  pl.BlockSpec((QPS * T, LANES), lambda p, i: (i, p))],
        out_specs=[pl.BlockSpec((QPS * T, LANES), lambda p, i: (i, p)),
                   pl.BlockSpec((s, LANES), lambda p, i: (0, p)),
                   pl.BlockSpec((s, LANES), lambda p, i: (0, p))],
        out_shape=[jax.ShapeDtypeStruct((s, SB_W), BF16)] * 3,
        scratch_shapes=[pltpu.VMEM((2, T, LANES), F32), pltpu.VMEM((s, LANES), F32),
                        pltpu.VMEM((s, LANES), F32), pltpu.VMEM((2, T, 1), F32),
                        pltpu.VMEM((2, T, 1), F32), pltpu.VMEM((2, T, T), BF16),
                        pltpu.VMEM((2, T, T), BF16)],
        compiler_params=_ARB2, operands=[qkv, qkv, qkv, o, do]), rider)
    return dq, dk, dv, lands


def _fox_bwd(qkv, fqb, frow, fbounds, o, lse, do, name, rider=None):
    s = qkv.shape[0]
    nq = s // T

    def body(q_ref, k_ref, v_ref, fq_ref, fr_ref, fb_ref, o_ref, lse_ref, do_ref,
             dq_ref, dk_ref, dv_ref, df_ref, dqa_ref, dka_ref, dva_ref, dfa_ref, dls_ref, pbs_ref):
        step_id = pl.program_id(1)

        @pl.when(step_id == 0)
        def _():
            dka_ref[...] = jnp.zeros_like(dka_ref)
            dva_ref[...] = jnp.zeros_like(dva_ref)
            dfa_ref[...] = jnp.zeros_like(dfa_ref)

        m0 = _pair_masks()
        causal = _iota2((T, T), 0) >= _iota2((T, T), 1)
        hs = range(2)

        def query_tile(i, rows):
            qh = _split_pair(q_ref[rows, :] * jnp.asarray(Q_SCALE, BF16), m0)
            doh = _split_pair(do_ref[rows, :], m0)
            dsum = _row_dots(do_ref[rows, :], o_ref[rows, :], m0)
            fq = fq_ref[rows, :]
            fqh = (fq[:, 0:1], fq[:, HEAD_DIM:HEAD_DIM + 1])
            lse = lse_ref[rows, :]
            lseh = (lse[:, 0:1], lse[:, HEAD_DIM:HEAD_DIM + 1])
            dqa_ref[...] = jnp.zeros_like(dqa_ref)

            def flush(j):
                off = pl.multiple_of(j * T, T)
                k = k_ref[pl.ds(off, T), :]
                for h in hs:
                    dqa_ref[h] += _dot(dls_ref[h], k)
                dka_ref[pl.ds(off, T), :] += (_dot_tn(dls_ref[0], qh[0])
                                              + _dot_tn(dls_ref[1], qh[1]))
                dva_ref[pl.ds(off, T), :] += (_dot_tn(pbs_ref[0], doh[0])
                                              + _dot_tn(pbs_ref[1], doh[1]))

            def tile(j, diag):
                off = pl.multiple_of(j * T, T)
                k = k_ref[pl.ds(off, T), :]
                v = v_ref[pl.ds(off, T), :]
                sc = [_dot_nt(qh[h], k) + fqh[h] - fr_ref[h:h + 1, pl.ds(off, T)] for h in hs]
                dp = [_dot_nt(doh[h], v) for h in hs]
                if not diag:
                    flush(j + 1)
                p = [jnp.exp(sc[h] - lseh[h]) for h in hs]
                if diag:
                    p = [jnp.where(causal, p[h], 0.0) for h in hs]
                dl = [p[h] * (dp[h] - dsum[h]) for h in hs]
                for h in hs:
                    dls_ref[h] = dl[h].astype(BF16)
                    pbs_ref[h] = p[h].astype(BF16)
                    dfa_ref[h:h + 1, pl.ds(off, T)] -= jnp.sum(dl[h], axis=0, keepdims=True)

            tile(i, True)
            j_left = _fox_tiles_left(i, pl.program_id(0), nq, fb_ref, tile)
            flush(j_left + 1)
            dq_ref[rows, :] = (jnp.where(m0, dqa_ref[0], dqa_ref[1]) * Q_SCALE).astype(BF16)

        for n in range(QPS):
            query_tile(QPS * step_id + n, slice(n * T, (n + 1) * T))

        @pl.when(step_id == nq // QPS - 1)
        def _():
            dk_ref[...] = dka_ref[...].astype(BF16)
            dv_ref[...] = dva_ref[...].astype(BF16)
            df_ref[...] = dfa_ref[...]

    nb = FX_W // LANES
    base = 3 * SB_W // LANES
    (dq, dk, dv, df), lands = _ride(dict(
        body=body, name=name, grid=(nb, nq // QPS),
        in_specs=[pl.BlockSpec((QPS * T, LANES), lambda p, i: (i, base + p)),
                  pl.BlockSpec((s, LANES), lambda p, i: (0, base + nb + p)),
                  pl.BlockSpec((s, LANES), lambda p, i: (0, base + 2 * nb + p)),
                  pl.BlockSpec((QPS * T, LANES), lambda p, i: (i, p)),
                  pl.BlockSpec((None, 2, s), lambda p, i: (p, 0, 0)),
                  pl.BlockSpec(memory_space=pltpu.SMEM),
                  pl.BlockSpec((QPS * T, LANES), lambda p, i: (i, p)),
                  pl.BlockSpec((QPS * T, LANES), lambda p, i: (i, p)),
                  pl.BlockSpec((QPS * T, LANES), lambda p, i: (i, p))],
        out_specs=[pl.BlockSpec((QPS * T, LANES), lambda p, i: (i, p)),
                   pl.BlockSpec((s, LANES), lambda p, i: (0, p)),
                   pl.BlockSpec((s, LANES), lambda p, i: (0, p)),
                   pl.BlockSpec((None, 2, s), lambda p, i: (p, 0, 0))],
        out_shape=[jax.ShapeDtypeStruct((s, FX_W), BF16)] * 3
        + [jax.ShapeDtypeStruct((nb, 2, s), F32)],
        scratch_shapes=[pltpu.VMEM((2, T, LANES), F32), pltpu.VMEM((s, LANES), F32),
                        pltpu.VMEM((s, LANES), F32), pltpu.VMEM((2, s), F32),
                        pltpu.VMEM((2, T, T), BF16), pltpu.VMEM((2, T, T), BF16)],
        compiler_params=_ARB2, operands=[qkv, qkv, qkv, fqb, frow, fbounds, o, lse, do]), rider)
    return dq, dk, dv, df, lands


def _fox_prep_bwd(dfrow, gf, bpad, name):
    s = gf.shape[0]

    def body(df_ref, fl_ref, b_ref, dfl_ref, db_ref):
        tri = jnp.where(_iota2((T, T), 0) <= _iota2((T, T), 1), 1.0, 0.0).astype(BF16)
        carry = jnp.zeros((1, LANES), F32)
        db = jnp.zeros((1, LANES), F32)
        fill = jnp.zeros((LANES - FOX_HEADS, T), F32)
        for blk in reversed(range(s // T)):
            rows = slice(blk * T, (blk + 1) * T)
            c = _tri3(tri, jnp.concatenate([df_ref[:, rows], fill], axis=0), _dot_nt) + carry
            carry = c[0:1, :]
            dfl = c / (1.0 + jnp.exp(fl_ref[rows, :] + b_ref[...]))
            dfl_ref[rows, :] = dfl.astype(BF16)
            db = db + jnp.sum(dfl, axis=0, keepdims=True)
        db_ref[...] = db

    return pl.pallas_call(
        body, name=name, grid=(1,),
        in_specs=[pl.BlockSpec((FOX_HEADS, s), lambda i: (0, 0)),
                  pl.BlockSpec((s, LANES), lambda i: (0, MIX_W // LANES)),
                  pl.BlockSpec((1, LANES), lambda i: (0, 0))],
        out_specs=[pl.BlockSpec((s, LANES), lambda i: (0, 0)),
                   pl.BlockSpec((1, LANES), lambda i: (0, 0))],
        out_shape=[jax.ShapeDtypeStruct((s, LANES), BF16), jax.ShapeDtypeStruct((1, LANES), F32)],
        compiler_params=_ARB1,
    )(dfrow, gf, bpad)


def _mem_bwd(qkv, kv, o, lse, do, name):
    s = qkv.shape[0]
    n = kv.shape[0]

    def body(q_ref, k_ref, v_ref, o_ref, lse_ref, do_ref, dq_ref, dk_ref, dv_ref):
        @pl.when(pl.program_id(1) == 0)
        def _():
            dk_ref[...] = jnp.zeros_like(dk_ref)
            dv_ref[...] = jnp.zeros_like(dv_ref)

        m0 = _pair_masks()
        qh = _split_pair(q_ref[...] * jnp.asarray(Q_SCALE, BF16), m0)
        doh = _split_pair(do_ref[...], m0)
        dsum = _row_dots(do_ref[...], o_ref[...], m0)
        lse = lse_ref[...]
        lseh = (lse[:, 0:1], lse[:, HEAD_DIM:HEAD_DIM + 1])
        k = k_ref[...]
        v = v_ref[...]
        dqs = []
        for h in range(2):
            p = jnp.exp(_dot_nt(qh[h], k) - lseh[h])
            dl = p * (_dot_nt(doh[h], v) - dsum[h])
            dlb = dl.astype(BF16)
            dqs.append(_dot(dlb, k))
            dk_ref[...] += _dot_tn(dlb, qh[h])
            dv_ref[...] += _dot_tn(p.astype(BF16), doh[h])
        dq_ref[...] = (jnp.where(m0, dqs[0], dqs[1]) * Q_SCALE).astype(BF16)

    nb = MEM_W // LANES
    base = (3 * SB_W + 3 * FX_W) // LANES
    return pl.pallas_call(
        body, name=name, grid=(nb, s // TQM),
        in_specs=[pl.BlockSpec((TQM, LANES), lambda p, i: (i, base + p)),
                  pl.BlockSpec((n, LANES), lambda p, i: (0, p)),
                  pl.BlockSpec((n, LANES), lambda p, i: (0, nb + p)),
                  pl.BlockSpec((TQM, LANES), lambda p, i: (i, p)),
                  pl.BlockSpec((TQM, LANES), lambda p, i: (i, p)),
                  pl.BlockSpec((TQM, LANES), lambda p, i: (i, p))],
        out_specs=[pl.BlockSpec((TQM, LANES), lambda p, i: (i, p)),
                   pl.BlockSpec((n, LANES), lambda p, i: (0, p)),
                   pl.BlockSpec((n, LANES), lambda p, i: (0, p))],
        out_shape=[jax.ShapeDtypeStruct((s, MEM_W), BF16), jax.ShapeDtypeStruct((n, MEM_W), F32),
                   jax.ShapeDtypeStruct((n, MEM_W), F32)],
        compiler_params=_ARB2,
    )(qkv, kv, kv, o, lse, do)


def _memkv_bwd(mem, mnw, wkv, dk, dv, name):
    n = mem.shape[0]

    def body(mem_ref, mnw_ref, w_ref, dk_ref, dv_ref, dw_ref, dmnw_ref):
        mv = mem_ref[...]
        r = lax.rsqrt(jnp.mean(mv * mv, axis=-1, keepdims=True) + EPS)
        mh = mv * r
        hm = (mh * mnw_ref[...]).astype(BF16)
        dkv = jnp.concatenate([dk_ref[...], dv_ref[...]], axis=1).astype(BF16)
        dw_ref[...] = _dot_tn(hm, dkv)
        dhm = _dot_nt(dkv, w_ref[...])
        dmnw_ref[...] = jnp.sum(dhm * mh, axis=0, keepdims=True)

    return pl.pallas_call(
        body, name=name, grid=(1,),
        in_specs=[pl.BlockSpec((n, D_MODEL), lambda i: (0, 0)),
                  pl.BlockSpec((1, D_MODEL), lambda i: (0, 0)),
                  pl.BlockSpec((D_MODEL, 2 * MEM_W), lambda i: (0, 0)),
                  pl.BlockSpec((n, MEM_W), lambda i: (0, 0)),
                  pl.BlockSpec((n, MEM_W), lambda i: (0, 0))],
        out_specs=[pl.BlockSpec((D_MODEL, 2 * MEM_W), lambda i: (0, 0)),
                   pl.BlockSpec((1, D_MODEL), lambda i: (0, 0))],
        out_shape=[jax.ShapeDtypeStruct((D_MODEL, 2 * MEM_W), F32),
                   jax.ShapeDtypeStruct((1, D_MODEL), F32)],
        compiler_params=_ARB1,
    )(mem, mnw, wkv, dk, dv)


def _inproj_bwd_dx(pieces, w_r, x, nw, dxo, name):
    s = x.shape[0]
    n = len(pieces)
    widths = [p.shape[1] for p in pieces]

    def body(*refs):
        piece_refs = refs[:n]
        w_ref, x_ref, nw_ref, dxo_ref, dx_ref, h_ref, dnw_ref, dp_ref = refs[n:]

        @pl.when(pl.program_id(0) == 0)
        def _():
            dnw_ref[...] = jnp.zeros_like(dnw_ref)

        col = 0
        for r, wd in zip(piece_refs, widths):
            dp_ref[:, col:col + wd] = r[...]
            col += wd
        dp_ref[:, col:] = jnp.zeros((TM, WR_W - col), BF16)
        dh = _dot(dp_ref[...], w_ref[...])
        xv = x_ref[...]
        nw = nw_ref[...]
        r = lax.rsqrt(jnp.mean(xv * xv, axis=-1, keepdims=True) + EPS)
        xh = xv * r
        h_ref[...] = (xh * nw).astype(BF16)
        dnw_ref[...] += jnp.sum(dh * xh, axis=0, keepdims=True)
        dxh = dh * nw
        dx_ref[...] = r * (dxh - xh * jnp.mean(dxh * xh, axis=-1, keepdims=True)) + dxo_ref[...]

    return pl.pallas_call(
        body, name=name, grid=(s // TM,),
        in_specs=[pl.BlockSpec((TM, wd), lambda i: (i, 0)) for wd in widths]
        + [pl.BlockSpec((WR_W, D_MODEL), lambda i: (0, 0)),
           pl.BlockSpec((TM, D_MODEL), lambda i: (i, 0)),
           pl.BlockSpec((1, D_MODEL), lambda i: (0, 0)),
           pl.BlockSpec((TM, D_MODEL), lambda i: (i, 0))],
        out_specs=[pl.BlockSpec((TM, D_MODEL), lambda i: (i, 0)),
                   pl.BlockSpec((TM, D_MODEL), lambda i: (i, 0)),
                   pl.BlockSpec((1, D_MODEL), lambda i: (0, 0)),
                   pl.BlockSpec((TM, WR_W), lambda i: (i, 0))],
        out_shape=[jax.ShapeDtypeStruct((s, D_MODEL), F32), jax.ShapeDtypeStruct((s, D_MODEL), BF16),
                   jax.ShapeDtypeStruct((1, D_MODEL), F32), jax.ShapeDtypeStruct((s, WR_W), BF16)],
        compiler_params=_ARB1,
    )(*pieces, w_r, x, nw, dxo)


def _inproj_bwd_dw(h, dproj, name):
    s = dproj.shape[0]
    tn = 256

    def body(h_ref, dp_ref, dw_ref):
        dw_ref[...] = _dot_tn(dp_ref[...], h_ref[...])

    return pl.pallas_call(
        body, name=name, grid=(WR_W // tn,),
        in_specs=[pl.BlockSpec((s, D_MODEL), lambda j: (0, 0)),
                  pl.BlockSpec((s, tn), lambda j: (0, j))],
        out_specs=pl.BlockSpec((tn, D_MODEL), lambda j: (j, 0)),
        out_shape=jax.ShapeDtypeStruct((WR_W, D_MODEL), F32),
        compiler_params=_ARB1,
    )(h, dproj)


def _rearrange_w_in(wt):
    pad = jnp.zeros((FL_PAD - FOX_HEADS,) + wt.shape[1:], wt.dtype)
    return jnp.concatenate([wt[:3072], wt[3080:3336], wt[3336:IN_W], wt[3072:3080], pad], axis=0)


def _restore_w_in(g):
    gate0 = QKV_W
    fl0 = QKV_W + MIX_W
    return jnp.concatenate(
        [g[:3072], g[fl0:fl0 + FOX_HEADS], g[3072:QKV_W], g[gate0:fl0]], axis=0)


def _pad_lanes(v, width=LANES):
    return jnp.pad(v, (0, width - v.shape[0])).reshape(1, width)


def _layer_fwd(xs, mem, nw, w_r, b_forget, mnw, late, onw, l, travel=None):
    s = xs.shape[0]
    bpad = _pad_lanes(b_forget)
    qkv, gf = _inproj_fwd(xs, nw, w_r, f"inproj_fwd_{l}")
    fqb, frow, fbounds = _fox_prep_fwd(gf, qkv, bpad, f"fox_prep_fwd_{l}")
    frow = frow.reshape(FOX_HEADS // 2, 2, s)
    travel = _Travel(travel)
    ysb, _ = travel.ride(0, _sb_fwd, qkv, f"sb_fwd_{l}")
    yfx, lse_fx, _ = travel.ride(1, _fox_fwd, qkv, fqb, frow, fbounds, f"fox_fwd_{l}")
    wkv, wout = late(travel.lands)
    kv = _memkv_fwd(mem, mnw, wkv, f"memkv_fwd_{l}")
    ym, lse_m = _mem_fwd(qkv, kv, f"mem_fwd_{l}")
    xn = _outproj_fwd(ysb, yfx, ym, gf, onw, wout, xs, f"outproj_fwd_{l}")
    saved = (xs, nw, mnw, onw, bpad, qkv, gf, fqb, frow, fbounds, ysb, yfx, lse_fx, kv, ym, lse_m)
    return xn, saved, travel.lands, (wkv, wout)


class _Travel:
    def __init__(self, plan):
        self.plan = plan
        self.lands = None if plan is None else _new_lands(plan[0], plan[1])

    def ride(self, n, fn, *args):
        if self.plan is None or self.plan[2][n] is None:
            return fn(*args)
        srcs, scatter, legs = self.plan
        idx, rows = legs[n]
        out = fn(*args, rider=_Rider([srcs[a] for a in idx], [self.lands[a] for a in idx],
                                     scatter, rows))
        for a, land in zip(idx, out[-1]):
            self.lands[a] = land
        return out


def _layer_bwd(dx, saved, mem, w_r, wkv, wout, l, travel=None):
    xs, nw, mnw, onw, bpad, qkv, gf, fqb, frow, fbounds, ysb, yfx, lse_fx, kv, ym, lse_m = saved
    s = xs.shape[0]
    dysb, dyfx, dym, dgate, dwout, donw = _outproj_bwd(
        dx, wout, ysb, yfx, ym, gf, onw, f"outproj_bwd_{l}")
    travel = _Travel(None if travel is None else travel(dwout))
    sdq, sdk, sdv, _ = travel.ride(0, _sb_bwd, qkv, ysb, dysb, f"sb_bwd_{l}")
    fdq, fdk, fdv, dfrow, _ = travel.ride(1, _fox_bwd, qkv, fqb, frow, fbounds, yfx, lse_fx, dyfx,
                                          f"fox_bwd_{l}")
    dfl, db = _fox_prep_bwd(dfrow.reshape(FOX_HEADS, s), gf, bpad, f"fox_prep_bwd_{l}")
    dmq, dmk, dmv = _mem_bwd(qkv, kv, ym, lse_m, dym, f"mem_bwd_{l}")
    dwkv, dmnw = _memkv_bwd(mem, mnw, wkv, dmk, dmv, f"memkv_bwd_{l}")
    dx, ht, dnw, dproj = _inproj_bwd_dx([sdq, sdk, sdv, fdq, fdk, fdv, dmq, dgate, dfl],
                                        w_r, xs, nw, dx, f"inproj_bwd_dx_{l}")
    dwr = _inproj_bwd_dw(ht, dproj, f"inproj_bwd_dw_{l}")
    grads = dict(norm_w=dnw[0], w_r=dwr, b_forget=db[0, :FOX_HEADS], mem_norm_w=dmnw[0],
                 w_mem_kv=dwkv, out_norm_w=donw[0], w_out=dwout)
    return dx, grads, travel.lands


_ANY = pl.BlockSpec(memory_space=pl.ANY)


def _my_place():
    return lax.axis_index("x"), lax.axis_index("y"), lax.axis_index("c")


def _flip(v, bit):
    return 1 - v if bit else v


def _block_index(px, py, pc):
    return 4 * px + 2 * py + pc


def _all_gather_weights(shards, name):
    n = len(shards)

    def body(*refs):
        ins, outs = refs[:n], refs[n:2 * n]
        send_sems, recv_sems, local_sems = refs[2 * n:]
        x, y, c = _my_place()
        me = (x, y, c)
        sibling = (x, y, 1 - c)
        chips = [(1 - x, y), (x, 1 - y), (1 - x, 1 - y)]

        def copy(a, k, block, to, src=None):
            dst = outs[a].at[_block_index(*block)]
            return pltpu.make_async_remote_copy(
                src_ref=dst if src is None else src, dst_ref=dst,
                send_sem=send_sems.at[a, k], recv_sem=recv_sems.at[a, k],
                device_id=to, device_id_type=pl.DeviceIdType.MESH)

        mine = [pltpu.make_async_copy(ins[a], outs[a].at[_block_index(*me)], local_sems.at[a])
                for a in range(n)]
        for cp in mine:
            cp.start()
        first = []
        for a in range(n):
            first.append(copy(a, 0, me, sibling, src=ins[a]))
            first += [copy(a, 1 + j, me, (*chip, c), src=ins[a]) for j, chip in enumerate(chips)]
        for cp in first:
            cp.start()
        passed = []
        for j, chip in enumerate(chips):
            for a in range(n):
                copy(a, 1 + j, (*chip, c), me).wait_recv()
                fwd = copy(a, 4 + j, (*chip, c), sibling)
                fwd.start()
                passed.append(fwd)
        for a in range(n):
            copy(a, 0, sibling, me).wait_recv()
            for j, chip in enumerate(chips):
                copy(a, 4 + j, (*chip, 1 - c), me).wait_recv()
        for cp in first + passed:
            cp.wait_send()
        for cp in mine:
            cp.wait()

    return pl.pallas_call(
        body, name=name,
        in_specs=[_ANY] * n, out_specs=[_ANY] * n,
        out_shape=[jax.ShapeDtypeStruct((N_DEV,) + v.shape, v.dtype) for v in shards],
        scratch_shapes=[pltpu.SemaphoreType.DMA((n, 7)), pltpu.SemaphoreType.DMA((n, 7)),
                        pltpu.SemaphoreType.DMA((n,))],
    )(*shards)


def _exchange_blocks(blocked, name):
    n = len(blocked)

    def body(*refs):
        ins, outs = refs[:n], refs[n:2 * n]
        send_sems, recv_sems, local_sems = refs[2 * n:]
        x, y, c = _my_place()
        mine_idx = _block_index(x, y, c)
        local = [pltpu.make_async_copy(ins[a].at[mine_idx], outs[a].at[mine_idx], local_sems.at[a])
                 for a in range(n)]
        for cp in local:
            cp.start()
        sends, arrivals = [], []
        for r in range(1, N_DEV):
            peer = (_flip(x, r & 4), _flip(y, r & 2), _flip(c, r & 1))
            peer_idx = _block_index(*peer)
            for a in range(n):
                sems = dict(send_sem=send_sems.at[a, r - 1], recv_sem=recv_sems.at[a, r - 1],
                            device_id=peer, device_id_type=pl.DeviceIdType.MESH)
                sends.append(pltpu.make_async_remote_copy(
                    src_ref=ins[a].at[peer_idx], dst_ref=outs[a].at[mine_idx], **sems))
                arrivals.append(pltpu.make_async_remote_copy(
                    src_ref=ins[a].at[peer_idx], dst_ref=outs[a].at[peer_idx], **sems))
        for cp in sends:
            cp.start()
        for cp in arrivals:
            cp.wait_recv()
        for cp in sends:
            cp.wait_send()
        for cp in local:
            cp.wait()

    return pl.pallas_call(
        body, name=name,
        in_specs=[_ANY] * n, out_specs=[_ANY] * n,
        out_shape=[jax.ShapeDtypeStruct(v.shape, v.dtype) for v in blocked],
        scratch_shapes=[pltpu.SemaphoreType.DMA((n, 7)), pltpu.SemaphoreType.DMA((n, 7)),
                        pltpu.SemaphoreType.DMA((n,))],
    )(*blocked)


N_CHIP = N_DEV // 2


def _pair_swap(blocked, name):
    n = len(blocked)

    def body(*refs):
        ins, outs = refs[:n], refs[n:2 * n]
        send_sems, recv_sems = refs[2 * n:]
        x, y, c = _my_place()
        copies = [pltpu.make_async_remote_copy(
            src_ref=ins[a].at[j, 1 - c], dst_ref=outs[a].at[j],
            send_sem=send_sems.at[N_CHIP * a + j], recv_sem=recv_sems.at[N_CHIP * a + j],
            device_id=(x, y, 1 - c), device_id_type=pl.DeviceIdType.MESH)
            for a in range(n) for j in range(N_CHIP)]
        for cp in copies:
            cp.start()
        for cp in copies:
            cp.wait_recv()
        for cp in copies:
            cp.wait_send()

    return pl.pallas_call(
        body, name=name,
        in_specs=[_ANY] * n, out_specs=[_ANY] * n,
        out_shape=[jax.ShapeDtypeStruct((N_CHIP,) + v.shape[2:], v.dtype) for v in blocked],
        scratch_shapes=[pltpu.SemaphoreType.DMA((N_CHIP * n,)),
                        pltpu.SemaphoreType.DMA((N_CHIP * n,))],
    )(*blocked)


def _pair_add(mine, theirs, name):
    _, nrow, ncol = mine.shape

    def body(a_ref, b_ref, o_ref):
        o_ref[...] = (a_ref[...].astype(F32) + b_ref[...].astype(F32)).astype(BF16)

    blk = pl.BlockSpec((None, nrow, ncol), lambda j: (j, 0, 0))
    return pl.pallas_call(
        body, name=name, grid=(N_CHIP,), in_specs=[blk, blk], out_specs=blk,
        out_shape=jax.ShapeDtypeStruct(mine.shape, BF16), compiler_params=_ARB1,
    )(mine, theirs)


def _chip_exchange(by_chip, to_all, name):
    n, m = len(by_chip), len(to_all)

    def body(*refs):
        ins, alls = refs[:n], refs[n:n + m]
        outs, all_outs = refs[n + m:2 * n + m], refs[2 * n + m:2 * (n + m)]
        send_sems, recv_sems, local_sems = refs[2 * (n + m):]
        x, y, c = _my_place()
        my_chip, mine_idx = 2 * x + y, _block_index(x, y, c)
        local = [pltpu.make_async_copy(ins[a].at[my_chip], outs[a].at[my_chip], local_sems.at[a])
                 for a in range(n)]
        local += [pltpu.make_async_copy(alls[b].at[mine_idx], all_outs[b].at[mine_idx],
                                        local_sems.at[n + b]) for b in range(m)]
        for cp in local:
            cp.start()
        sends, arrivals = [], []
        k = 0
        for r in range(1, N_DEV):
            peer = (_flip(x, r & 4), _flip(y, r & 2), _flip(c, r & 1))
            peer_chip, peer_idx = 2 * peer[0] + peer[1], _block_index(*peer)
            pairs = [(alls[b].at[mine_idx], all_outs[b].at[mine_idx], all_outs[b].at[peer_idx])
                     for b in range(m)]
            if not r & 1:
                pairs += [(ins[a].at[peer_chip], outs[a].at[my_chip], outs[a].at[peer_chip])
                          for a in range(n)]
            for src, there, here in pairs:
                sems = dict(send_sem=send_sems.at[k], recv_sem=recv_sems.at[k], device_id=peer,
                            device_id_type=pl.DeviceIdType.MESH)
                sends.append(pltpu.make_async_remote_copy(src_ref=src, dst_ref=there, **sems))
                arrivals.append(pltpu.make_async_remote_copy(src_ref=src, dst_ref=here, **sems))
                k += 1
        for cp in sends:
            cp.start()
        for cp in arrivals:
            cp.wait_recv()
        for cp in sends:
            cp.wait_send()
        for cp in local:
            cp.wait()

    n_copies = 7 * m + 3 * n
    return pl.pallas_call(
        body, name=name,
        in_specs=[_ANY] * (n + m), out_specs=[_ANY] * (n + m),
        out_shape=[jax.ShapeDtypeStruct(v.shape, v.dtype) for v in by_chip + to_all],
        scratch_shapes=[pltpu.SemaphoreType.DMA((n_copies,)), pltpu.SemaphoreType.DMA((n_copies,)),
                        pltpu.SemaphoreType.DMA((n + m,))],
    )(*by_chip, *to_all)


class _Rider(NamedTuple):
    srcs: list
    lands: list
    scatter: bool
    part: list


def _new_lands(srcs, scatter):
    return [lax.empty(v.shape if scatter else (N_DEV,) + v.shape, v.dtype) for v in srcs]


def _rider_copies(srcs, lands, send_sems, recv_sems, rider):
    x, y, c = _my_place()
    mine_idx = _block_index(x, y, c)

    def window(ref, a):
        if rider.part[a] is None:
            return ref
        dim, start, size = rider.part[a]
        return ref.at[(slice(None),) * dim + (pl.ds(start, size),)]

    sends, arrivals = [], []
    for r in range(1, N_DEV):
        peer = (_flip(x, r & 4), _flip(y, r & 2), _flip(c, r & 1))
        peer_idx = _block_index(*peer)
        for a in range(len(srcs)):
            src = window(srcs[a].at[peer_idx] if rider.scatter else srcs[a], a)
            k = 7 * a + r - 1
            sems = dict(send_sem=send_sems.at[k], recv_sem=recv_sems.at[k],
                        device_id=peer, device_id_type=pl.DeviceIdType.MESH)
            sends.append(pltpu.make_async_remote_copy(
                src_ref=src, dst_ref=window(lands[a].at[mine_idx], a), **sems))
            arrivals.append(pltpu.make_async_remote_copy(
                src_ref=src, dst_ref=window(lands[a].at[peer_idx], a), **sems))
    return sends, arrivals


def _ride(call, rider):
    call = dict(call)
    body, grid = call.pop("body"), call["grid"]
    operands = call.pop("operands")
    if rider is None:
        return list(pl.pallas_call(body, **call)(*operands)), None
    n_in, n_out = len(call["in_specs"]), len(call["out_specs"])
    n_scratch = len(call["scratch_shapes"])
    m = len(rider.srcs)

    def riding(*refs):
        main_in, srcs, lands = refs[:n_in], refs[n_in:n_in + m], refs[n_in + m:n_in + 2 * m]
        main_out = refs[n_in + 2 * m:n_in + 2 * m + n_out]
        rest = refs[n_in + 3 * m + n_out:]
        send_sems, recv_sems = rest[n_scratch:]
        at = [pl.program_id(d) for d in range(len(grid))]
        first = functools.reduce(jnp.logical_and, [p == 0 for p in at])
        last = functools.reduce(jnp.logical_and, [p == g - 1 for p, g in zip(at, grid)])
        sends, arrivals = _rider_copies(srcs, lands, send_sems, recv_sems, rider)

        @pl.when(first)
        def _():
            for cp in sends:
                cp.start()

        body(*main_in, *main_out, *rest[:n_scratch])

        @pl.when(last)
        def _():
            for cp in arrivals:
                cp.wait_recv()
            for cp in sends:
                cp.wait_send()

    call["in_specs"] = list(call["in_specs"]) + [_ANY] * (2 * m)
    call["out_specs"] = list(call["out_specs"]) + [_ANY] * m
    call["out_shape"] = list(call["out_shape"]) + [
        jax.ShapeDtypeStruct(v.shape, v.dtype) for v in rider.lands]
    call["scratch_shapes"] = list(call["scratch_shapes"]) + [
        pltpu.SemaphoreType.DMA((7 * m,)), pltpu.SemaphoreType.DMA((7 * m,))]
    call["input_output_aliases"] = {n_in + m + a: n_out + a for a in range(m)}
    outs = pl.pallas_call(riding, **call)(*operands, *rider.srcs, *rider.lands)
    return list(outs[:n_out]), list(outs[n_out:])


def _sum_parts(p_ref):
    g = p_ref[0].astype(F32)
    for k in range(1, p_ref.shape[0]):
        g = g + p_ref[k].astype(F32)
    return g


def _adamw(g, w, m, v):
    c1 = 1.0 / (1.0 - ADAM_B1 ** ADAM_STEP)
    c2 = 1.0 / (1.0 - ADAM_B2 ** ADAM_STEP)
    nm = ADAM_B1 * m + (1.0 - ADAM_B1) * g
    nv = ADAM_B2 * v + (1.0 - ADAM_B2) * (g * g)
    return nm, nv, -ADAM_LR * ((nm * c1) / (jnp.sqrt(nv * c2) + ADAM_EPS) + ADAM_WD * w)


def _adamw_w_in(parts, w, m, v, name):
    ncol_blk, depth, nfeat = w.shape
    cols = 256

    def body(*refs):
        p_refs = refs[:depth]
        w_ref, m_ref, v_ref, g_ref, d_ref, nm_ref, nv_ref = refs[depth:]
        for l in range(depth):
            g = _sum_parts(p_refs[l])
            nm, nv, d = _adamw(g, w_ref[:, l, :], m_ref[:, l, :], v_ref[:, l, :])
            g_ref[:, l, :] = g
            nm_ref[:, l, :] = nm
            nv_ref[:, l, :] = nv
            d_ref[:, l, :] = d

    blk = pl.BlockSpec((ncol_blk, depth, cols), lambda j: (0, 0, j))
    return pl.pallas_call(
        body, name=name, grid=(nfeat // cols,),
        in_specs=[pl.BlockSpec((p.shape[0], ncol_blk, cols), lambda j: (0, 0, j)) for p in parts]
        + [blk] * 3,
        out_specs=[blk] * 4,
        out_shape=[jax.ShapeDtypeStruct(w.shape, F32)] * 4,
        compiler_params=_ARB1,
    )(*parts, w, m, v)


def _adamw_sum(parts, w, m, v, tile, name):
    depth, nrow, ncol = w.shape
    rows, cols = tile

    def body(*refs):
        p_refs = refs[:depth]
        w_ref, m_ref, v_ref, g_ref, d_ref, nm_ref, nv_ref = refs[depth:]
        layer = pl.program_id(0)
        for l in range(depth):
            @pl.when(layer == l)
            def _(p_ref=p_refs[l]):
                g = _sum_parts(p_ref)
                nm, nv, d = _adamw(g, w_ref[...], m_ref[...], v_ref[...])
                g_ref[...] = g
                nm_ref[...] = nm
                nv_ref[...] = nv
                d_ref[...] = d

    def part_spec(l):
        return pl.BlockSpec((parts[l].shape[0], rows, cols), lambda q, i, j: (
            0, jnp.where(q == l, i, 0), jnp.where(q == l, j, 0)))

    blk = pl.BlockSpec((None, rows, cols), lambda q, i, j: (q, i, j))
    return pl.pallas_call(
        body, name=name, grid=(depth, nrow // rows, ncol // cols),
        in_specs=[part_spec(l) for l in range(depth)] + [blk, blk, blk],
        out_specs=[blk] * 4,
        out_shape=[jax.ShapeDtypeStruct(w.shape, F32)] * 4,
        compiler_params=pltpu.CompilerParams(
            dimension_semantics=("arbitrary", "arbitrary", "arbitrary")),
    )(*parts, w, m, v)


def _pack_small(norm_w, mem_norm_w, out_norm_w, final_norm_w, b_forget):
    onw = jnp.pad(out_norm_w.reshape(20, LANES), ((0, 4), (0, 0)))
    b = jnp.pad(b_forget, ((0, 6), (0, LANES - FOX_HEADS)))
    return jnp.concatenate([norm_w.reshape(16, LANES), mem_norm_w.reshape(16, LANES), onw,
                            final_norm_w.reshape(8, LANES), b], axis=0)


def _unpack_small(p):
    return (p[0:16].reshape(2, D_MODEL), p[16:32].reshape(2, D_MODEL), p[32:52].reshape(2, MIX_W),
            p[56:64].reshape(D_MODEL), p[64:66, :FOX_HEADS])


def kernel(x, mem, norm_w, w_in, b_forget, mem_norm_w, w_mem_kv, out_norm_w, w_out, final_norm_w, loss_target, m_norm_w, m_w_in, m_b_forget, m_mem_norm_w, m_w_mem_kv, m_out_norm_w, m_w_out, m_final_norm_w, v_norm_w, v_w_in, v_b_forget, v_mem_norm_w, v_w_mem_kv, v_out_norm_w, v_w_out, v_final_norm_w):
    kv_rows = w_mem_kv.shape[1]
    out_rows = w_out.shape[1]
    me = _block_index(*_my_place())

    def shards(l):
        return [w_in[l].T.astype(BF16), w_mem_kv[l].astype(BF16), w_out[l].astype(BF16)]

    def full_in(g_in):
        return _rearrange_w_in(g_in.reshape(IN_W, D_MODEL))

    def full_kv_out(g_kv, g_out):
        return g_kv.reshape(D_MODEL, 2 * MEM_W), g_out.reshape(MIX_W, D_MODEL)

    def in_blocks(g):
        segments = [(0, 3072, 0), (3072, 3080, QKV_W + MIX_W), (3080, 3336, 3072),
                    (3336, IN_W, QKV_W)]

        def block(k):
            lo, hi = k * SHARD_W, (k + 1) * SHARD_W
            pieces = [g[at + max(lo, a) - a:at + min(hi, b) - a]
                      for a, b, at in segments if max(lo, a) < min(hi, b)]
            return jnp.concatenate(pieces, axis=0).astype(BF16)

        return jnp.stack([block(k) for k in range(N_DEV)])

    def kv_blocks(g):
        return g.reshape(N_DEV, kv_rows, 2 * MEM_W).astype(BF16)

    def out_blocks(g):
        return g.reshape(N_DEV, out_rows, D_MODEL).astype(BF16)

    def with_own(land, own):
        return lax.dynamic_update_slice(land, own[None], (me,) + (0,) * own.ndim)

    def with_own_of(land, blocked):
        return lax.dynamic_update_slice(land, lax.dynamic_slice_in_dim(blocked, me, 1, axis=0),
                                        (me,) + (0,) * (land.ndim - 1))

    def row(v):
        return v.reshape(1, -1)

    def cols(first, size):
        return (1, first, size)

    fwd_split, bwd_split = 5 * LANES, 6 * LANES

    g_in0, g_kv0, g_out0 = _all_gather_weights(shards(0), "all_gather_l0")
    w_r0 = full_in(g_in0)
    s_in1, s_kv1, s_out1 = shards(1)
    x1, saved0, (l_in1,), (wkv0, wout0) = _layer_fwd(
        x[0], mem[0], row(norm_w[0]), w_r0, b_forget[0], row(mem_norm_w[0]),
        lambda lands: full_kv_out(g_kv0, g_out0), row(out_norm_w[0]), 0,
        travel=([s_in1], False, [([0], [cols(0, fwd_split)]),
                                 ([0], [cols(fwd_split, D_MODEL - fwd_split)])]))
    w_r1 = full_in(with_own(l_in1, s_in1))
    x2, saved1, _, (wkv1, wout1) = _layer_fwd(
        x1, mem[0], row(norm_w[1]), w_r1, b_forget[1], row(mem_norm_w[1]),
        lambda lands: full_kv_out(with_own(lands[0], s_kv1), with_own(lands[1], s_out1)),
        row(out_norm_w[1]), 1, travel=([s_kv1, s_out1], False, [([0, 1], [None, None]), None]))

    dx2, loss_part, dfnw = _final_fwd_bwd(x2, row(final_norm_w), loss_target[0], "final_fwd_bwd")

    dx1, gr1, (l_out1,) = _layer_bwd(
        dx2, saved1, mem[0], w_r1, wkv1, wout1, 1,
        travel=lambda dwout: ([out_blocks(dwout)], True, [([0], [None]), None]))
    p_in1, p_kv1 = in_blocks(gr1["w_r"]), kv_blocks(gr1["w_mem_kv"])
    grad_x, gr0, (l_in1, l_kv1, l_out0) = _layer_bwd(
        dx1, saved0, mem[0], w_r0, wkv0, wout0, 0,
        travel=lambda dwout: ([p_in1, p_kv1, out_blocks(dwout)], True,
                              [([0, 1], [cols(0, bwd_split), None]),
                               ([0, 2], [cols(bwd_split, D_MODEL - bwd_split), None])]))
    r_out1 = with_own_of(l_out1, out_blocks(gr1["w_out"]))
    r_in1, r_kv1 = with_own_of(l_in1, p_in1), with_own_of(l_kv1, p_kv1)
    r_out0 = with_own_of(l_out0, out_blocks(gr0["w_out"]))

    def both(name):
        return jnp.stack([gr0[name], gr1[name]])

    small = _pack_small(both("norm_w"), both("mem_norm_w"), both("out_norm_w"), dfnw[0],
                        both("b_forget")).at[LOSS_ROW].set(loss_part[0])
    p_small = jnp.broadcast_to(small[None], (N_DEV, SMALL_ROWS, LANES))
    by_core = [v.reshape((N_CHIP, 2) + v.shape[1:])
               for v in (in_blocks(gr0["w_r"]), kv_blocks(gr0["w_mem_kv"]))]
    from_sibling = _pair_swap(by_core, "grads_l0_pair_swap")
    core = lax.axis_index("c")
    chip_sums = [_pair_add(lax.dynamic_index_in_dim(v, core, axis=1, keepdims=False), got,
                           f"grads_l0_pair_add_{a}")
                 for a, (v, got) in enumerate(zip(by_core, from_sibling))]
    r_in0, r_kv0, r_small = _chip_exchange(chip_sums, [p_small], "exchange_grads_l0")

    def view(v):
        return jnp.transpose(v, (2, 0, 1))

    g_w_in, d_w_in, nm_w_in, nv_w_in = [jnp.transpose(v, (1, 2, 0)) for v in _adamw_w_in(
        [r_in0, r_in1], view(w_in), view(m_w_in), view(v_w_in), "adamw_w_in")]
    g_w_kv, d_w_kv, nm_w_kv, nv_w_kv = _adamw_sum(
        [r_kv0, r_kv1], w_mem_kv, m_w_mem_kv, v_w_mem_kv, (kv_rows, 2 * MEM_W), "adamw_w_mem_kv")
    g_w_out, d_w_out, nm_w_out, nv_w_out = _adamw_sum(
        [r_out0, r_out1], w_out, m_w_out, v_w_out, (out_rows, D_MODEL), "adamw_w_out")
    w_small = _pack_small(norm_w, mem_norm_w, out_norm_w, final_norm_w, b_forget)[None]
    m_small = _pack_small(m_norm_w, m_mem_norm_w, m_out_norm_w, m_final_norm_w, m_b_forget)[None]
    v_small = _pack_small(v_norm_w, v_mem_norm_w, v_out_norm_w, v_final_norm_w, v_b_forget)[None]
    small_out = _adamw_sum([r_small], w_small, m_small, v_small, (SMALL_ROWS, LANES), "adamw_small")
    (g_nw, g_mnw, g_onw, g_fnw, g_b), (d_nw, d_mnw, d_onw, d_fnw, d_b), \
        (nm_nw, nm_mnw, nm_onw, nm_fnw, nm_b), (nv_nw, nv_mnw, nv_onw, nv_fnw, nv_b) = [
            _unpack_small(t[0]) for t in small_out]
    loss = small_out[0][0, LOSS_ROW, 0]

    return (loss, grad_x[None],
            g_nw, g_w_in, g_b, g_mnw, g_w_kv, g_onw, g_w_out, g_fnw,
            d_nw, d_w_in, d_b, d_mnw, d_w_kv, d_onw, d_w_out, d_fnw,
            nm_nw, nm_w_in, nm_b, nm_mnw, nm_w_kv, nm_onw, nm_w_out, nm_fnw,
            nv_nw, nv_w_in, nv_b, nv_mnw, nv_w_kv, nv_onw, nv_w_out, nv_fnw)
```

```python
import functools
from typing import NamedTuple

import jax
import jax.numpy as jnp
from jax import lax
from jax.experimental import pallas as pl
from jax.experimental.pallas import tpu as pltpu

F32 = jnp.float32
BF16 = jnp.bfloat16

N_DEV = 8
D_MODEL = 1024
HEAD_DIM = 64
LANES = 128
SB_W = 512
FX_W = 512
MEM_W = 256
MIX_W = 1280
FOX_HEADS = 8
IN_W = 4616
SHARD_W = IN_W // N_DEV
QKV_W = 3 * SB_W + 3 * FX_W + MEM_W
FL_PAD = 256
GF_W = MIX_W + FL_PAD
WR_W = QKV_W + GF_W
EPS = 1e-6
T = 256
QPS = 4
TM = 256
TQM = 512
Q_SCALE = 0.125
NEG = -1e30
UNDERFLOW = -110.0
NORM_SLACK = 1.01

ADAM_LR = 0.001
ADAM_B1 = 0.9
ADAM_B2 = 0.999
ADAM_EPS = 1e-08
ADAM_WD = 0.01
ADAM_STEP = 10

SMALL_ROWS = 72
LOSS_ROW = 66

_NT = (((1,), (1,)), ((), ()))
_TN = (((0,), (0,)), ((), ()))

_ARB1 = pltpu.CompilerParams(dimension_semantics=("arbitrary",))
_ARB2 = pltpu.CompilerParams(dimension_semantics=("arbitrary", "arbitrary"))


def _dot(a, b):
    return jnp.dot(a, b, preferred_element_type=F32)


def _dot_nt(a, b):
    return lax.dot_general(a, b, _NT, preferred_element_type=F32)


def _dot_tn(a, b):
    return lax.dot_general(a, b, _TN, preferred_element_type=F32)


def _split2(x):
    hi = x.astype(BF16)
    lo = (x - hi.astype(F32)).astype(BF16)
    return hi, lo


def _stack2(u):
    return jnp.concatenate([u, u], axis=0)


def _cum2(x, u2):
    hi, lo = _split2(x)
    return _dot(jnp.concatenate([hi, lo], axis=1), u2)


def _tri3(tri, x, dot=None):
    dot = dot or _dot
    hi = x.astype(BF16)
    r1 = x - hi.astype(F32)
    mid = r1.astype(BF16)
    lo = (r1 - mid.astype(F32)).astype(BF16)
    return dot(tri, hi) + dot(tri, mid) + dot(tri, lo)


def _iota2(shape, dim):
    return lax.broadcasted_iota(jnp.int32, shape, dim)


def _head_block_diag():
    r = _iota2((LANES, LANES), 0) // HEAD_DIM
    c = _iota2((LANES, LANES), 1) // HEAD_DIM
    return _stack2(jnp.where(r == c, 1.0, 0.0).astype(BF16))


def _head_mean(x, bd):
    return _cum2(x, bd) * (1.0 / HEAD_DIM)


def _sigmoid(x):
    return 1.0 / (1.0 + jnp.exp(-x))


def _log_sigmoid(x):
    return jnp.minimum(x, 0.0) - jnp.log(1.0 + jnp.exp(-jnp.abs(x)))


def _running_top(r_ref):
    return jnp.max(jnp.maximum(r_ref[0], r_ref[1]))


def _fox_tiles_left(i, pair, nq, fb_ref, tile):
    def bound(j):
        b = []
        for h in range(2):
            head = 2 * pair + h
            b.append(2.0 * NORM_SLACK * fb_ref[2 * nq + i, head] * fb_ref[3 * nq, head]
                     + fb_ref[2 * i, head] - fb_ref[2 * j + 1, head])
        return jnp.maximum(b[0], b[1])

    def more(j):
        return jnp.logical_and(j >= 0, bound(jnp.maximum(j, 0)) > UNDERFLOW)

    def step(j):
        tile(j, False)
        return j - 1

    return lax.while_loop(more, step, i - 1)


def _pair_masks():
    lane = _iota2((1, LANES), 1)
    return lane < HEAD_DIM


def _split_pair(x, m0):
    zero = jnp.zeros_like(x)
    return jnp.where(m0, x, zero), jnp.where(m0, zero, x)


def _inproj_fwd(x, nw, w_r, name):
    s = x.shape[0]

    def body(x_ref, nw_ref, w_ref, qkv_ref, gf_ref):
        xv = x_ref[...]
        r = lax.rsqrt(jnp.mean(xv * xv, axis=-1, keepdims=True) + EPS)
        h = (xv * r * nw_ref[...]).astype(BF16)
        for c in range(0, QKV_W, 256):
            qkv_ref[:, c:c + 256] = _dot_nt(h, w_ref[c:c + 256, :]).astype(BF16)
        for c in range(0, GF_W, 256):
            gf_ref[:, c:c + 256] = _dot_nt(h, w_ref[QKV_W + c:QKV_W + c + 256, :])

    return pl.pallas_call(
        body, name=name, grid=(s // TM,),
        in_specs=[pl.BlockSpec((TM, D_MODEL), lambda i: (i, 0)),
                  pl.BlockSpec((1, D_MODEL), lambda i: (0, 0)),
                  pl.BlockSpec((WR_W, D_MODEL), lambda i: (0, 0))],
        out_specs=[pl.BlockSpec((TM, QKV_W), lambda i: (i, 0)),
                   pl.BlockSpec((TM, GF_W), lambda i: (i, 0))],
        out_shape=[jax.ShapeDtypeStruct((s, QKV_W), BF16), jax.ShapeDtypeStruct((s, GF_W), F32)],
        compiler_params=_ARB1,
    )(x, nw, w_r)


def _fox_prep_fwd(gf, qkv, bpad, name):
    s = gf.shape[0]
    nq = s // T
    nrow = -(-(3 * nq + 1) // 8) * 8

    def body(fl_ref, q_ref, k_ref, b_ref, fq_ref, fr_ref, fb_ref):
        tri = jnp.where(_iota2((T, T), 0) >= _iota2((T, T), 1), 1.0, 0.0).astype(BF16)
        m0 = _pair_masks()
        lane = _iota2((1, LANES), 1)
        norms = [jnp.zeros((1, LANES), F32) for _ in range(nq + 1)]
        same_head = (_iota2((LANES, LANES), 0) // HEAD_DIM) == (_iota2((LANES, LANES), 1) // HEAD_DIM)
        bd = jnp.where(same_head, 1.0, 0.0).astype(BF16)
        for p in range(FOX_HEADS // 2):
            cols = slice(p * LANES, (p + 1) * LANES)
            q = (q_ref[:, cols] * jnp.asarray(Q_SCALE, BF16)).astype(F32)
            k = k_ref[:, cols].astype(F32)
            qn = _dot((q * q).astype(BF16), bd)
            kn = _dot((k * k).astype(BF16), bd)
            tops = [jnp.max(qn[j * T:(j + 1) * T], axis=0, keepdims=True) for j in range(nq)]
            tops.append(jnp.max(kn, axis=0, keepdims=True))
            tops = [jnp.sqrt(top) for top in tops]
            for h in range(2):
                at = h * HEAD_DIM
                norms = [jnp.where(lane == 2 * p + h, top[:, at:at + 1], row)
                         for top, row in zip(tops, norms)]
        for j in range(nq + 1):
            fb_ref[2 * nq + j:2 * nq + j + 1, :] = norms[j]
        fb_ref[3 * nq + 1:, :] = jnp.zeros((nrow - 3 * nq - 1, LANES), F32)
        carry = jnp.zeros((1, LANES), F32)
        for blk in range(s // T):
            rows = slice(blk * T, (blk + 1) * T)
            lf = _log_sigmoid(fl_ref[rows, :] + b_ref[...])
            c = _tri3(tri, lf) + carry
            carry = c[T - 1:T, :]
            for p in range(FOX_HEADS // 2):
                fq_ref[rows, p * LANES:(p + 1) * LANES] = jnp.where(
                    m0, c[:, 2 * p:2 * p + 1], c[:, 2 * p + 1:2 * p + 2])
            fr_ref[:, rows] = c.T[0:FOX_HEADS, :]
            fb_ref[2 * blk:2 * blk + 1, :] = c[0:1, :]
            fb_ref[2 * blk + 1:2 * blk + 2, :] = carry

    base = 3 * SB_W // FX_W
    return pl.pallas_call(
        body, name=name, grid=(1,),
        in_specs=[pl.BlockSpec((s, LANES), lambda i: (0, MIX_W // LANES)),
                  pl.BlockSpec((s, FX_W), lambda i: (0, base)),
                  pl.BlockSpec((s, FX_W), lambda i: (0, base + 1)),
                  pl.BlockSpec((1, LANES), lambda i: (0, 0))],
        out_specs=[pl.BlockSpec((s, FX_W), lambda i: (0, 0)),
                   pl.BlockSpec((FOX_HEADS, s), lambda i: (0, 0)),
                   pl.BlockSpec((nrow, LANES), lambda i: (0, 0))],
        out_shape=[jax.ShapeDtypeStruct((s, FX_W), F32), jax.ShapeDtypeStruct((FOX_HEADS, s), F32),
                   jax.ShapeDtypeStruct((nrow, LANES), F32)],
        compiler_params=_ARB1,
    )(gf, qkv, qkv, bpad)


def _sb_fwd(qkv, name, rider=None):
    s = qkv.shape[0]

    def body(q_ref, k_ref, v_ref, o_ref, acc_ref, r_ref, as_ref):
        m0 = _pair_masks()
        strict = _iota2((T, T), 0) > _iota2((T, T), 1)
        u2 = _stack2(jnp.where(strict, 1.0, 0.0).astype(BF16))
        hs = range(2)

        def query_tile(i, rows):
            qh = _split_pair(q_ref[rows, :] * jnp.asarray(Q_SCALE, BF16), m0)
            acc_ref[...] = jnp.zeros_like(acc_ref)
            r_ref[...] = jnp.zeros_like(r_ref)

            def flush(j):
                v = v_ref[pl.ds(pl.multiple_of(j * T, T), T), :]
                for h in hs:
                    acc_ref[h] += _dot(as_ref[h], v)

            def tile(j, diag):
                k = k_ref[pl.ds(pl.multiple_of(j * T, T), T), :]
                z = [_dot_nt(qh[h], k) for h in hs]
                if not diag:
                    flush(j + 1)
                la = [jnp.minimum(z[h], 0.0) - jnp.log(1.0 + jnp.exp(-jnp.abs(z[h]))) for h in hs]
                lf = [la[h] - z[h] for h in hs]
                if diag:
                    lf = [jnp.where(strict, lf[h], 0.0) for h in hs]
                cin = [_cum2(lf[h], u2) for h in hs]
                a = [jnp.exp(la[h] + cin[h] + r_ref[h]) for h in hs]
                if diag:
                    a = [jnp.where(strict, a[h], 0.0) for h in hs]
                for h in hs:
                    r_ref[h] += cin[h][:, 0:1] + lf[h][:, 0:1]
                    as_ref[h] = a[h].astype(BF16)

            tile(i, True)

            def more(state):
                j, top = state
                return jnp.logical_and(j >= 0, top > UNDERFLOW)

            def step(state):
                j, _ = state
                tile(j, False)
                return j - 1, _running_top(r_ref)

            j_left, _ = lax.while_loop(more, step, (i - 1, _running_top(r_ref)))
            flush(j_left + 1)
            o_ref[rows, :] = jnp.where(m0, acc_ref[0], acc_ref[1])

        for n in range(QPS):
            query_tile(QPS * pl.program_id(1) + n, slice(n * T, (n + 1) * T))

    nb = SB_W // LANES
    (ysb,), lands = _ride(dict(
        body=body, name=name, grid=(nb, s // (QPS * T)),
        in_specs=[pl.BlockSpec((QPS * T, LANES), lambda p, i: (i, p)),
                  pl.BlockSpec((s, LANES), lambda p, i: (0, nb + p)),
                  pl.BlockSpec((s, LANES), lambda p, i: (0, 2 * nb + p))],
        out_specs=[pl.BlockSpec((QPS * T, LANES), lambda p, i: (i, p))],
        out_shape=[jax.ShapeDtypeStruct((s, SB_W), F32)],
        scratch_shapes=[pltpu.VMEM((2, T, LANES), F32), pltpu.VMEM((2, T, 1), F32),
                        pltpu.VMEM((2, T, T), BF16)],
        compiler_params=_ARB2, operands=[qkv, qkv, qkv]), rider)
    return ysb, lands


def _fox_fwd(qkv, fqb, frow, fbounds, name, rider=None):
    s = qkv.shape[0]

    def body(q_ref, k_ref, v_ref, fq_ref, fr_ref, fb_ref, o_ref, lse_ref, acc_ref, m_ref, ps_ref):
        pair = pl.program_id(0)
        m0 = _pair_masks()
        causal = _iota2((T, T), 0) >= _iota2((T, T), 1)
        ones = jnp.ones((T, LANES), BF16)
        hs = range(2)

        def query_tile(i, rows):
            qh = _split_pair(q_ref[rows, :] * jnp.asarray(Q_SCALE, BF16), m0)
            fq = fq_ref[rows, :]
            fqh = (fq[:, 0:1], fq[:, HEAD_DIM:HEAD_DIM + 1])
            acc_ref[...] = jnp.zeros_like(acc_ref)
            m_ref[...] = jnp.full_like(m_ref, NEG)

            def flush(j):
                v = v_ref[pl.ds(pl.multiple_of(j * T, T), T), :]
                va2 = _stack2(jnp.concatenate([v, ones], axis=1))
                for h in hs:
                    acc_ref[h] += _dot(ps_ref[h], va2)

            def tile(j, diag):
                off = pl.multiple_of(j * T, T)
                k = k_ref[pl.ds(off, T), :]
                sc = [_dot_nt(qh[h], k) + fqh[h] - fr_ref[h:h + 1, pl.ds(off, T)] for h in hs]
                if not diag:
                    flush(j + 1)
                if diag:
                    sc = [jnp.where(causal, sc[h], NEG) for h in hs]
                m_new = [jnp.maximum(m_ref[h], jnp.max(sc[h], axis=1, keepdims=True)) for h in hs]
                p = [jnp.exp(sc[h] - m_new[h]) for h in hs]
                for h in hs:
                    acc_ref[h] = acc_ref[h] * jnp.exp(m_ref[h] - m_new[h])
                    m_ref[h] = m_new[h]
                    p_hi, p_lo = _split2(p[h])
                    ps_ref[h] = jnp.concatenate([p_hi, p_lo], axis=1)

            tile(i, True)
            j_left = _fox_tiles_left(i, pair, s // T, fb_ref, tile)
            flush(j_left + 1)
            acc = (acc_ref[0], acc_ref[1])
            o_ref[rows, :] = jnp.where(m0, acc[0][:, :LANES] / acc[0][:, LANES:],
                                       acc[1][:, :LANES] / acc[1][:, LANES:])
            lse_ref[rows, :] = jnp.where(m0, m_ref[0] + jnp.log(acc[0][:, LANES:]),
                                         m_ref[1] + jnp.log(acc[1][:, LANES:]))

        for n in range(QPS):
            query_tile(QPS * pl.program_id(1) + n, slice(n * T, (n + 1) * T))

    nb = FX_W // LANES
    base = 3 * SB_W // LANES
    (yfx, lse), lands = _ride(dict(
        body=body, name=name, grid=(nb, s // (QPS * T)),
        in_specs=[pl.BlockSpec((QPS * T, LANES), lambda p, i: (i, base + p)),
                  pl.BlockSpec((s, LANES), lambda p, i: (0, base + nb + p)),
                  pl.BlockSpec((s, LANES), lambda p, i: (0, base + 2 * nb + p)),
                  pl.BlockSpec((QPS * T, LANES), lambda p, i: (i, p)),
                  pl.BlockSpec((None, 2, s), lambda p, i: (p, 0, 0)),
                  pl.BlockSpec(memory_space=pltpu.SMEM)],
        out_specs=[pl.BlockSpec((QPS * T, LANES), lambda p, i: (i, p)),
                   pl.BlockSpec((QPS * T, LANES), lambda p, i: (i, p))],
        out_shape=[jax.ShapeDtypeStruct((s, FX_W), F32), jax.ShapeDtypeStruct((s, FX_W), F32)],
        scratch_shapes=[pltpu.VMEM((2, T, 2 * LANES), F32), pltpu.VMEM((2, T, 1), F32),
                        pltpu.VMEM((2, T, 2 * T), BF16)],
        compiler_params=_ARB2, operands=[qkv, qkv, qkv, fqb, frow, fbounds]), rider)
    return yfx, lse, lands


def _memkv_fwd(mem, mnw, wkv, name):
    n = mem.shape[0]

    def body(mem_ref, mnw_ref, w_ref, kv_ref):
        mv = mem_ref[...]
        r = lax.rsqrt(jnp.mean(mv * mv, axis=-1, keepdims=True) + EPS)
        hm = (mv * r * mnw_ref[...]).astype(BF16)
        kv_ref[...] = _dot(hm, w_ref[...]).astype(BF16)

    return pl.pallas_call(
        body, name=name, grid=(1,),
        in_specs=[pl.BlockSpec((n, D_MODEL), lambda i: (0, 0)),
                  pl.BlockSpec((1, D_MODEL), lambda i: (0, 0)),
                  pl.BlockSpec((D_MODEL, 2 * MEM_W), lambda i: (0, 0))],
        out_specs=pl.BlockSpec((n, 2 * MEM_W), lambda i: (0, 0)),
        out_shape=jax.ShapeDtypeStruct((n, 2 * MEM_W), BF16),
        compiler_params=_ARB1,
    )(mem, mnw, wkv)


def _mem_fwd(qkv, kv, name):
    s = qkv.shape[0]
    n = kv.shape[0]

    def body(q_ref, k_ref, v_ref, o_ref, lse_ref):
        m0 = _pair_masks()
        qh = _split_pair(q_ref[...] * jnp.asarray(Q_SCALE, BF16), m0)
        k = k_ref[...]
        v = v_ref[...]
        outs, lses = [], []
        for h in range(2):
            sc = _dot_nt(qh[h], k)
            mx = jnp.max(sc, axis=1, keepdims=True)
            p = jnp.exp(sc - mx)
            l = jnp.sum(p, axis=1, keepdims=True)
            outs.append(_dot(p.astype(BF16), v) / l)
            lses.append(mx + jnp.log(l))
        o_ref[...] = jnp.where(m0, outs[0], outs[1])
        lse_ref[...] = jnp.where(m0, lses[0], lses[1])

    nb = MEM_W // LANES
    base = (3 * SB_W + 3 * FX_W) // LANES
    return pl.pallas_call(
        body, name=name, grid=(nb, s // TQM),
        in_specs=[pl.BlockSpec((TQM, LANES), lambda p, i: (i, base + p)),
                  pl.BlockSpec((n, LANES), lambda p, i: (0, p)),
                  pl.BlockSpec((n, LANES), lambda p, i: (0, nb + p))],
        out_specs=[pl.BlockSpec((TQM, LANES), lambda p, i: (i, p)),
                   pl.BlockSpec((TQM, LANES), lambda p, i: (i, p))],
        out_shape=[jax.ShapeDtypeStruct((s, MEM_W), F32), jax.ShapeDtypeStruct((s, MEM_W), F32)],
        compiler_params=_ARB2,
    )(qkv, kv, kv)


def _mix_chunk(c, ysb_ref, yfx_ref, ym_ref):
    if c < SB_W // LANES:
        return ysb_ref[:, c * LANES:(c + 1) * LANES]
    c -= SB_W // LANES
    if c < FX_W // LANES:
        return yfx_ref[:, c * LANES:(c + 1) * LANES]
    c -= FX_W // LANES
    return ym_ref[:, c * LANES:(c + 1) * LANES]


def _outproj_fwd(ysb, yfx, ym, gf, onw, wout, x, name):
    s = x.shape[0]

    def body(ysb_ref, yfx_ref, ym_ref, g_ref, onw_ref, w_ref, x_ref, o_ref, yg_ref):
        bd = _head_block_diag()
        for c in range(MIX_W // LANES):
            sl = slice(c * LANES, (c + 1) * LANES)
            u = _mix_chunk(c, ysb_ref, yfx_ref, ym_ref)
            r = lax.rsqrt(_head_mean(u * u, bd) + EPS)
            g = g_ref[:, sl]
            yg_ref[:, sl] = (u * r * onw_ref[:, sl] * (g * _sigmoid(g))).astype(BF16)
        o_ref[...] = x_ref[...] + _dot(yg_ref[...], w_ref[...])

    return pl.pallas_call(
        body, name=name, grid=(s // TM,),
        in_specs=[pl.BlockSpec((TM, SB_W), lambda i: (i, 0)),
                  pl.BlockSpec((TM, FX_W), lambda i: (i, 0)),
                  pl.BlockSpec((TM, MEM_W), lambda i: (i, 0)),
                  pl.BlockSpec((TM, MIX_W), lambda i: (i, 0)),
                  pl.BlockSpec((1, MIX_W), lambda i: (0, 0)),
                  pl.BlockSpec((MIX_W, D_MODEL), lambda i: (0, 0)),
                  pl.BlockSpec((TM, D_MODEL), lambda i: (i, 0))],
        out_specs=pl.BlockSpec((TM, D_MODEL), lambda i: (i, 0)),
        out_shape=jax.ShapeDtypeStruct((s, D_MODEL), F32),
        scratch_shapes=[pltpu.VMEM((TM, MIX_W), BF16)],
        compiler_params=_ARB1,
    )(ysb, yfx, ym, gf, onw, wout, x)


def _final_fwd_bwd(x, fnw, target, name):
    s = x.shape[0]

    def body(x_ref, w_ref, t_ref, dx_ref, loss_ref, dw_ref):
        @pl.when(pl.program_id(0) == 0)
        def _():
            loss_ref[...] = jnp.zeros_like(loss_ref)
            dw_ref[...] = jnp.zeros_like(dw_ref)

        xv = x_ref[...]
        w = w_ref[...]
        r = lax.rsqrt(jnp.mean(xv * xv, axis=-1, keepdims=True) + EPS)
        xh = xv * r
        err = xh * w - t_ref[...]
        part = jnp.sum(jnp.sum(err * err, axis=1, keepdims=True), axis=0, keepdims=True)
        loss_ref[...] += part * (0.5 / D_MODEL)
        dy = err * (1.0 / D_MODEL)
        dw_ref[...] += jnp.sum(dy * xh, axis=0, keepdims=True)
        dxh = dy * w
        dx_ref[...] = r * (dxh - xh * jnp.mean(dxh * xh, axis=-1, keepdims=True))

    return pl.pallas_call(
        body, name=name, grid=(s // TM,),
        in_specs=[pl.BlockSpec((TM, D_MODEL), lambda i: (i, 0)),
                  pl.BlockSpec((1, D_MODEL), lambda i: (0, 0)),
                  pl.BlockSpec((TM, D_MODEL), lambda i: (i, 0))],
        out_specs=[pl.BlockSpec((TM, D_MODEL), lambda i: (i, 0)),
                   pl.BlockSpec((1, LANES), lambda i: (0, 0)),
                   pl.BlockSpec((1, D_MODEL), lambda i: (0, 0))],
        out_shape=[jax.ShapeDtypeStruct((s, D_MODEL), F32), jax.ShapeDtypeStruct((1, LANES), F32),
                   jax.ShapeDtypeStruct((1, D_MODEL), F32)],
        compiler_params=_ARB1,
    )(x, fnw, target)


def _outproj_bwd(dxo, wout, ysb, yfx, ym, gf, onw, name):
    s = dxo.shape[0]

    def body(dx_ref, w_ref, ysb_ref, yfx_ref, ym_ref, g_ref, onw_ref,
             dysb_ref, dyfx_ref, dym_ref, dg_ref, dw_ref, donw_ref, yg_ref):
        @pl.when(pl.program_id(0) == 0)
        def _():
            dw_ref[...] = jnp.zeros_like(dw_ref)
            donw_ref[...] = jnp.zeros_like(donw_ref)

        dxb = dx_ref[...].astype(BF16)
        dyg = _dot_nt(dxb, w_ref[...])
        bd = _head_block_diag()
        for c in range(MIX_W // LANES):
            sl = slice(c * LANES, (c + 1) * LANES)
            u = _mix_chunk(c, ysb_ref, yfx_ref, ym_ref)
            r = lax.rsqrt(_head_mean(u * u, bd) + EPS)
            yn = u * r
            g = g_ref[:, sl]
            sg = _sigmoid(g)
            sil = g * sg
            onw = onw_ref[:, sl]
            e = dyg[:, sl]
            yg_ref[:, sl] = (yn * onw * sil).astype(BF16)
            donw_ref[:, sl] += jnp.sum(e * yn * sil, axis=0, keepdims=True)
            dg_ref[:, sl] = (e * yn * onw * (sg * (1.0 + g * (1.0 - sg)))).astype(BF16)
            dyn = e * onw * sil
            du = (r * (dyn - yn * _head_mean(dyn * yn, bd))).astype(BF16)
            if c < 4:
                dysb_ref[:, c * LANES:(c + 1) * LANES] = du
            elif c < 8:
                dyfx_ref[:, (c - 4) * LANES:(c - 3) * LANES] = du
            else:
                dym_ref[:, (c - 8) * LANES:(c - 7) * LANES] = du
        dw_ref[...] += _dot_tn(yg_ref[...], dxb)

    return pl.pallas_call(
        body, name=name, grid=(s // TM,),
        in_specs=[pl.BlockSpec((TM, D_MODEL), lambda i: (i, 0)),
                  pl.BlockSpec((MIX_W, D_MODEL), lambda i: (0, 0)),
                  pl.BlockSpec((TM, SB_W), lambda i: (i, 0)),
                  pl.BlockSpec((TM, FX_W), lambda i: (i, 0)),
                  pl.BlockSpec((TM, MEM_W), lambda i: (i, 0)),
                  pl.BlockSpec((TM, MIX_W), lambda i: (i, 0)),
                  pl.BlockSpec((1, MIX_W), lambda i: (0, 0))],
        out_specs=[pl.BlockSpec((TM, SB_W), lambda i: (i, 0)),
                   pl.BlockSpec((TM, FX_W), lambda i: (i, 0)),
                   pl.BlockSpec((TM, MEM_W), lambda i: (i, 0)),
                   pl.BlockSpec((TM, MIX_W), lambda i: (i, 0)),
                   pl.BlockSpec((MIX_W, D_MODEL), lambda i: (0, 0)),
                   pl.BlockSpec((1, MIX_W), lambda i: (0, 0))],
        out_shape=[jax.ShapeDtypeStruct((s, SB_W), BF16), jax.ShapeDtypeStruct((s, FX_W), BF16),
                   jax.ShapeDtypeStruct((s, MEM_W), BF16), jax.ShapeDtypeStruct((s, MIX_W), BF16),
                   jax.ShapeDtypeStruct((MIX_W, D_MODEL), F32), jax.ShapeDtypeStruct((1, MIX_W), F32)],
        scratch_shapes=[pltpu.VMEM((TM, MIX_W), BF16)],
        compiler_params=_ARB1,
    )(dxo, wout, ysb, yfx, ym, gf, onw)


def _row_dots(do, o, m0):
    prod = do.astype(F32) * o
    zero = jnp.zeros_like(prod)
    return (jnp.sum(jnp.where(m0, prod, zero), axis=1, keepdims=True),
            jnp.sum(jnp.where(m0, zero, prod), axis=1, keepdims=True))


def _sb_bwd(qkv, o, do, name, rider=None):
    s = qkv.shape[0]
    nq = s // T

    def body(q_ref, k_ref, v_ref, o_ref, do_ref, dq_ref, dk_ref, dv_ref,
             dqa_ref, dka_ref, dva_ref, rl_ref, rg_ref, dzs_ref, abs_ref):
        step_id = pl.program_id(1)

        @pl.when(step_id == 0)
        def _():
            dka_ref[...] = jnp.zeros_like(dka_ref)
            dva_ref[...] = jnp.zeros_like(dva_ref)

        m0 = _pair_masks()
        strict = _iota2((T, T), 0) > _iota2((T, T), 1)
        u2 = _stack2(jnp.where(strict, 1.0, 0.0).astype(BF16))
        hs = range(2)

        def query_tile(i, rows):
            qh = _split_pair(q_ref[rows, :] * jnp.asarray(Q_SCALE, BF16), m0)
            doh = _split_pair(do_ref[rows, :], m0)
            dsum = _row_dots(do_ref[rows, :], o_ref[rows, :], m0)
            dqa_ref[...] = jnp.zeros_like(dqa_ref)
            rl_ref[...] = jnp.zeros_like(rl_ref)
            rg_ref[...] = jnp.zeros_like(rg_ref)

            def flush(j):
                off = pl.multiple_of(j * T, T)
                k = k_ref[pl.ds(off, T), :]
                for h in hs:
                    dqa_ref[h] += _dot(dzs_ref[h], k)
                dka_ref[pl.ds(off, T), :] += (_dot_tn(dzs_ref[0], qh[0])
                                              + _dot_tn(dzs_ref[1], qh[1]))
                dva_ref[pl.ds(off, T), :] += (_dot_tn(abs_ref[0], doh[0])
                                              + _dot_tn(abs_ref[1], doh[1]))

            def tile(j, diag):
                off = pl.multiple_of(j * T, T)
                k = k_ref[pl.ds(off, T), :]
                v = v_ref[pl.ds(off, T), :]
                z = [_dot_nt(qh[h], k) for h in hs]
                da = [_dot_nt(doh[h], v) for h in hs]
                if not diag:
                    flush(j + 1)
                la = [jnp.minimum(z[h], 0.0) - jnp.log(1.0 + jnp.exp(-jnp.abs(z[h]))) for h in hs]
                lf = [la[h] - z[h] for h in hs]
                if diag:
                    lf = [jnp.where(strict, lf[h], 0.0) for h in hs]
                cin = [_cum2(lf[h], u2) for h in hs]
                a = [jnp.exp(la[h] + cin[h] + rl_ref[h]) for h in hs]
                if diag:
                    a = [jnp.where(strict, a[h], 0.0) for h in hs]
                ab = [a[h].astype(BF16) for h in hs]
                g = [ab[h].astype(F32) * da[h] for h in hs]
                gin = [_cum2(g[h], u2) for h in hs]
                dz = [g[h] - jnp.exp(la[h]) * ((dsum[h] - rg_ref[h]) - gin[h]) for h in hs]
                if diag:
                    dz = [jnp.where(strict, dz[h], 0.0) for h in hs]
                for h in hs:
                    rl_ref[h] += cin[h][:, 0:1] + lf[h][:, 0:1]
                    rg_ref[h] += gin[h][:, 0:1] + g[h][:, 0:1]
                    dzs_ref[h] = dz[h].astype(BF16)
                    abs_ref[h] = ab[h]

            tile(i, True)

            def more(state):
                j, top = state
                return jnp.logical_and(j >= 0, top > UNDERFLOW)

            def step(state):
                j, _ = state
                tile(j, False)
                return j - 1, _running_top(rl_ref)

            j_left, _ = lax.while_loop(more, step, (i - 1, _running_top(rl_ref)))
            flush(j_left + 1)
            dq_ref[rows, :] = (jnp.where(m0, dqa_ref[0], dqa_ref[1]) * Q_SCALE).astype(BF16)

        for n in range(QPS):
            query_tile(QPS * step_id + n, slice(n * T, (n + 1) * T))

        @pl.when(step_id == nq // QPS - 1)
        def _():
            dk_ref[...] = dka_ref[...].astype(BF16)
            dv_ref[...] = dva_ref[...].astype(BF16)

    nb = SB_W // LANES
    (dq, dk, dv), lands = _ride(dict(
        body=body, name=name, grid=(nb, nq // QPS),
        in_specs=[pl.BlockSpec((QPS * T, LANES), lambda p, i: (i, p)),
                  pl.BlockSpec((s, LANES), lambda p, i: (0, nb + p)),
                  pl.BlockSpec((s, LANES), lambda p, i: (0, 2 * nb + p)),
                  pl.BlockSpec((QPS * T, LANES), lambda p, i: (i, p)),
                  pl.BlockSpec((QPS * T, LANES), lambda p, i: (i, p))],
        out_specs=[pl.BlockSpec((QPS * T, LANES), lambda p, i: (i, p)),
                   pl.BlockSpec((s, LANES), lambda p, i: (0, p)),
                   pl.BlockSpec((s, LANES), lambda p, i: (0, p))],
        out_shape=[jax.ShapeDtypeStruct((s, SB_W), BF16)] * 3,
        scratch_shapes=[pltpu.VMEM((2, T, LANES), F32), pltpu.VMEM((s, LANES), F32),
                        pltpu.VMEM((s, LANES), F32), pltpu.VMEM((2, T, 1), F32),
                        pltpu.VMEM((2, T, 1), F32), pltpu.VMEM((2, T, T), BF16),
                        pltpu.VMEM((2, T, T), BF16)],
        compiler_params=_ARB2, operands=[qkv, qkv, qkv, o, do]), rider)
    return dq, dk, dv, lands


def _fox_bwd(qkv, fqb, frow, fbounds, o, lse, do, name, rider=None):
    s = qkv.shape[0]
    nq = s // T

    def body(q_ref, k_ref, v_ref, fq_ref, fr_ref, fb_ref, o_ref, lse_ref, do_ref,
             dq_ref, dk_ref, dv_ref, df_ref, dqa_ref, dka_ref, dva_ref, dfa_ref, dls_ref, pbs_ref):
        step_id = pl.program_id(1)

        @pl.when(step_id == 0)
        def _():
            dka_ref[...] = jnp.zeros_like(dka_ref)
            dva_ref[...] = jnp.zeros_like(dva_ref)
            dfa_ref[...] = jnp.zeros_like(dfa_ref)

        m0 = _pair_masks()
        causal = _iota2((T, T), 0) >= _iota2((T, T), 1)
        hs = range(2)

        def query_tile(i, rows):
            qh = _split_pair(q_ref[rows, :] * jnp.asarray(Q_SCALE, BF16), m0)
            doh = _split_pair(do_ref[rows, :], m0)
            dsum = _row_dots(do_ref[rows, :], o_ref[rows, :], m0)
            fq = fq_ref[rows, :]
            fqh = (fq[:, 0:1], fq[:, HEAD_DIM:HEAD_DIM + 1])
            lse = lse_ref[rows, :]
            lseh = (lse[:, 0:1], lse[:, HEAD_DIM:HEAD_DIM + 1])
            dqa_ref[...] = jnp.zeros_like(dqa_ref)

            def flush(j):
                off = pl.multiple_of(j * T, T)
                k = k_ref[pl.ds(off, T), :]
                for h in hs:
                    dqa_ref[h] += _dot(dls_ref[h], k)
                dka_ref[pl.ds(off, T), :] += (_dot_tn(dls_ref[0], qh[0])
                                              + _dot_tn(dls_ref[1], qh[1]))
                dva_ref[pl.ds(off, T), :] += (_dot_tn(pbs_ref[0], doh[0])
                                              + _dot_tn(pbs_ref[1], doh[1]))

            def tile(j, diag):
                off = pl.multiple_of(j * T, T)
                k = k_ref[pl.ds(off, T), :]
                v = v_ref[pl.ds(off, T), :]
                sc = [_dot_nt(qh[h], k) + fqh[h] - fr_ref[h:h + 1, pl.ds(off, T)] for h in hs]
                dp = [_dot_nt(doh[h], v) for h in hs]
                if not diag:
                    flush(j + 1)
                p = [jnp.exp(sc[h] - lseh[h]) for h in hs]
                if diag:
                    p = [jnp.where(causal, p[h], 0.0) for h in hs]
                dl = [p[h] * (dp[h] - dsum[h]) for h in hs]
                for h in hs:
                    dls_ref[h] = dl[h].astype(BF16)
                    pbs_ref[h] = p[h].astype(BF16)
                    dfa_ref[h:h + 1, pl.ds(off, T)] -= jnp.sum(dl[h], axis=0, keepdims=True)

            tile(i, True)
            j_left = _fox_tiles_left(i, pl.program_id(0), nq, fb_ref, tile)
            flush(j_left + 1)
            dq_ref[rows, :] = (jnp.where(m0, dqa_ref[0], dqa_ref[1]) * Q_SCALE).astype(BF16)

        for n in range(QPS):
            query_tile(QPS * step_id + n, slice(n * T, (n + 1) * T))

        @pl.when(step_id == nq // QPS - 1)
        def _():
            dk_ref[...] = dka_ref[...].astype(BF16)
            dv_ref[...] = dva_ref[...].astype(BF16)
            df_ref[...] = dfa_ref[...]

    nb = FX_W // LANES
    base = 3 * SB_W // LANES
    (dq, dk, dv, df), lands = _ride(dict(
        body=body, name=name, grid=(nb, nq // QPS),
        in_specs=[pl.BlockSpec((QPS * T, LANES), lambda p, i: (i, base + p)),
                  pl.BlockSpec((s, LANES), lambda p, i: (0, base + nb + p)),
                  pl.BlockSpec((s, LANES), lambda p, i: (0, base + 2 * nb + p)),
                  pl.BlockSpec((QPS * T, LANES), lambda p, i: (i, p)),
                  pl.BlockSpec((None, 2, s), lambda p, i: (p, 0, 0)),
                  pl.BlockSpec(memory_space=pltpu.SMEM),
                  pl.BlockSpec((QPS * T, LANES), lambda p, i: (i, p)),
                  pl.BlockSpec((QPS * T, LANES), lambda p, i: (i, p)),
                  pl.BlockSpec((QPS * T, LANES), lambda p, i: (i, p))],
        out_specs=[pl.BlockSpec((QPS * T, LANES), lambda p, i: (i, p)),
                   pl.BlockSpec((s, LANES), lambda p, i: (0, p)),
                   pl.BlockSpec((s, LANES), lambda p, i: (0, p)),
                   pl.BlockSpec((None, 2, s), lambda p, i: (p, 0, 0))],
        out_shape=[jax.ShapeDtypeStruct((s, FX_W), BF16)] * 3
        + [jax.ShapeDtypeStruct((nb, 2, s), F32)],
        scratch_shapes=[pltpu.VMEM((2, T, LANES), F32), pltpu.VMEM((s, LANES), F32),
                        pltpu.VMEM((s, LANES), F32), pltpu.VMEM((2, s), F32),
                        pltpu.VMEM((2, T, T), BF16), pltpu.VMEM((2, T, T), BF16)],
        compiler_params=_ARB2, operands=[qkv, qkv, qkv, fqb, frow, fbounds, o, lse, do]), rider)
    return dq, dk, dv, df, lands


def _fox_prep_bwd(dfrow, gf, bpad, name):
    s = gf.shape[0]

    def body(df_ref, fl_ref, b_ref, dfl_ref, db_ref):
        tri = jnp.where(_iota2((T, T), 0) <= _iota2((T, T), 1), 1.0, 0.0).astype(BF16)
        carry = jnp.zeros((1, LANES), F32)
        db = jnp.zeros((1, LANES), F32)
        fill = jnp.zeros((LANES - FOX_HEADS, T), F32)
        for blk in reversed(range(s // T)):
            rows = slice(blk * T, (blk + 1) * T)
            c = _tri3(tri, jnp.concatenate([df_ref[:, rows], fill], axis=0), _dot_nt) + carry
            carry = c[0:1, :]
            dfl = c / (1.0 + jnp.exp(fl_ref[rows, :] + b_ref[...]))
            dfl_ref[rows, :] = dfl.astype(BF16)
            db = db + jnp.sum(dfl, axis=0, keepdims=True)
        db_ref[...] = db

    return pl.pallas_call(
        body, name=name, grid=(1,),
        in_specs=[pl.BlockSpec((FOX_HEADS, s), lambda i: (0, 0)),
                  pl.BlockSpec((s, LANES), lambda i: (0, MIX_W // LANES)),
                  pl.BlockSpec((1, LANES), lambda i: (0, 0))],
        out_specs=[pl.BlockSpec((s, LANES), lambda i: (0, 0)),
                   pl.BlockSpec((1, LANES), lambda i: (0, 0))],
        out_shape=[jax.ShapeDtypeStruct((s, LANES), BF16), jax.ShapeDtypeStruct((1, LANES), F32)],
        compiler_params=_ARB1,
    )(dfrow, gf, bpad)


def _mem_bwd(qkv, kv, o, lse, do, name):
    s = qkv.shape[0]
    n = kv.shape[0]

    def body(q_ref, k_ref, v_ref, o_ref, lse_ref, do_ref, dq_ref, dk_ref, dv_ref):
        @pl.when(pl.program_id(1) == 0)
        def _():
            dk_ref[...] = jnp.zeros_like(dk_ref)
            dv_ref[...] = jnp.zeros_like(dv_ref)

        m0 = _pair_masks()
        qh = _split_pair(q_ref[...] * jnp.asarray(Q_SCALE, BF16), m0)
        doh = _split_pair(do_ref[...], m0)
        dsum = _row_dots(do_ref[...], o_ref[...], m0)
        lse = lse_ref[...]
        lseh = (lse[:, 0:1], lse[:, HEAD_DIM:HEAD_DIM + 1])
        k = k_ref[...]
        v = v_ref[...]
        dqs = []
        for h in range(2):
            p = jnp.exp(_dot_nt(qh[h], k) - lseh[h])
            dl = p * (_dot_nt(doh[h], v) - dsum[h])
            dlb = dl.astype(BF16)
            dqs.append(_dot(dlb, k))
            dk_ref[...] += _dot_tn(dlb, qh[h])
            dv_ref[...] += _dot_tn(p.astype(BF16), doh[h])
        dq_ref[...] = (jnp.where(m0, dqs[0], dqs[1]) * Q_SCALE).astype(BF16)

    nb = MEM_W // LANES
    base = (3 * SB_W + 3 * FX_W) // LANES
    return pl.pallas_call(
        body, name=name, grid=(nb, s // TQM),
        in_specs=[pl.BlockSpec((TQM, LANES), lambda p, i: (i, base + p)),
                  pl.BlockSpec((n, LANES), lambda p, i: (0, p)),
                  pl.BlockSpec((n, LANES), lambda p, i: (0, nb + p)),
                  pl.BlockSpec((TQM, LANES), lambda p, i: (i, p)),
                  pl.BlockSpec((TQM, LANES), lambda p, i: (i, p)),
                  pl.BlockSpec((TQM, LANES), lambda p, i: (i, p))],
        out_specs=[pl.BlockSpec((TQM, LANES), lambda p, i: (i, p)),
                   pl.BlockSpec((n, LANES), lambda p, i: (0, p)),
                   pl.BlockSpec((n, LANES), lambda p, i: (0, p))],
        out_shape=[jax.ShapeDtypeStruct((s, MEM_W), BF16), jax.ShapeDtypeStruct((n, MEM_W), F32),
                   jax.ShapeDtypeStruct((n, MEM_W), F32)],
        compiler_params=_ARB2,
    )(qkv, kv, kv, o, lse, do)


def _memkv_bwd(mem, mnw, wkv, dk, dv, name):
    n = mem.shape[0]

    def body(mem_ref, mnw_ref, w_ref, dk_ref, dv_ref, dw_ref, dmnw_ref):
        mv = mem_ref[...]
        r = lax.rsqrt(jnp.mean(mv * mv, axis=-1, keepdims=True) + EPS)
        mh = mv * r
        hm = (mh * mnw_ref[...]).astype(BF16)
        dkv = jnp.concatenate([dk_ref[...], dv_ref[...]], axis=1).astype(BF16)
        dw_ref[...] = _dot_tn(hm, dkv)
        dhm = _dot_nt(dkv, w_ref[...])
        dmnw_ref[...] = jnp.sum(dhm * mh, axis=0, keepdims=True)

    return pl.pallas_call(
        body, name=name, grid=(1,),
        in_specs=[pl.BlockSpec((n, D_MODEL), lambda i: (0, 0)),
                  pl.BlockSpec((1, D_MODEL), lambda i: (0, 0)),
                  pl.BlockSpec((D_MODEL, 2 * MEM_W), lambda i: (0, 0)),
                  pl.BlockSpec((n, MEM_W), lambda i: (0, 0)),
                  pl.BlockSpec((n, MEM_W), lambda i: (0, 0))],
        out_specs=[pl.BlockSpec((D_MODEL, 2 * MEM_W), lambda i: (0, 0)),
                   pl.BlockSpec((1, D_MODEL), lambda i: (0, 0))],
        out_shape=[jax.ShapeDtypeStruct((D_MODEL, 2 * MEM_W), F32),
                   jax.ShapeDtypeStruct((1, D_MODEL), F32)],
        compiler_params=_ARB1,
    )(mem, mnw, wkv, dk, dv)


def _inproj_bwd_dx(pieces, w_r, x, nw, dxo, name):
    s = x.shape[0]
    n = len(pieces)
    widths = [p.shape[1] for p in pieces]

    def body(*refs):
        piece_refs = refs[:n]
        w_ref, x_ref, nw_ref, dxo_ref, dx_ref, h_ref, dnw_ref, dp_ref = refs[n:]

        @pl.when(pl.program_id(0) == 0)
        def _():
            dnw_ref[...] = jnp.zeros_like(dnw_ref)

        col = 0
        for r, wd in zip(piece_refs, widths):
            dp_ref[:, col:col + wd] = r[...]
            col += wd
        dp_ref[:, col:] = jnp.zeros((TM, WR_W - col), BF16)
        dh = _dot(dp_ref[...], w_ref[...])
        xv = x_ref[...]
        nw = nw_ref[...]
        r = lax.rsqrt(jnp.mean(xv * xv, axis=-1, keepdims=True) + EPS)
        xh = xv * r
        h_ref[...] = (xh * nw).astype(BF16)
        dnw_ref[...] += jnp.sum(dh * xh, axis=0, keepdims=True)
        dxh = dh * nw
        dx_ref[...] = r * (dxh - xh * jnp.mean(dxh * xh, axis=-1, keepdims=True)) + dxo_ref[...]

    return pl.pallas_call(
        body, name=name, grid=(s // TM,),
        in_specs=[pl.BlockSpec((TM, wd), lambda i: (i, 0)) for wd in widths]
        + [pl.BlockSpec((WR_W, D_MODEL), lambda i: (0, 0)),
           pl.BlockSpec((TM, D_MODEL), lambda i: (i, 0)),
           pl.BlockSpec((1, D_MODEL), lambda i: (0, 0)),
           pl.BlockSpec((TM, D_MODEL), lambda i: (i, 0))],
        out_specs=[pl.BlockSpec((TM, D_MODEL), lambda i: (i, 0)),
                   pl.BlockSpec((TM, D_MODEL), lambda i: (i, 0)),
                   pl.BlockSpec((1, D_MODEL), lambda i: (0, 0)),
                   pl.BlockSpec((TM, WR_W), lambda i: (i, 0))],
        out_shape=[jax.ShapeDtypeStruct((s, D_MODEL), F32), jax.ShapeDtypeStruct((s, D_MODEL), BF16),
                   jax.ShapeDtypeStruct((1, D_MODEL), F32), jax.ShapeDtypeStruct((s, WR_W), BF16)],
        compiler_params=_ARB1,
    )(*pieces, w_r, x, nw, dxo)


def _inproj_bwd_dw(h, dproj, name):
    s = dproj.shape[0]
    tn = 256

    def body(h_ref, dp_ref, dw_ref):
        dw_ref[...] = _dot_tn(dp_ref[...], h_ref[...])

    return pl.pallas_call(
        body, name=name, grid=(WR_W // tn,),
        in_specs=[pl.BlockSpec((s, D_MODEL), lambda j: (0, 0)),
                  pl.BlockSpec((s, tn), lambda j: (0, j))],
        out_specs=pl.BlockSpec((tn, D_MODEL), lambda j: (j, 0)),
        out_shape=jax.ShapeDtypeStruct((WR_W, D_MODEL), F32),
        compiler_params=_ARB1,
    )(h, dproj)


def _rearrange_w_in(wt):
    pad = jnp.zeros((FL_PAD - FOX_HEADS,) + wt.shape[1:], wt.dtype)
    return jnp.concatenate([wt[:3072], wt[3080:3336], wt[3336:IN_W], wt[3072:3080], pad], axis=0)


def _restore_w_in(g):
    gate0 = QKV_W
    fl0 = QKV_W + MIX_W
    return jnp.concatenate(
        [g[:3072], g[fl0:fl0 + FOX_HEADS], g[3072:QKV_W], g[gate0:fl0]], axis=0)


def _pad_lanes(v, width=LANES):
    return jnp.pad(v, (0, width - v.shape[0])).reshape(1, width)


def _layer_fwd(xs, mem, nw, w_r, b_forget, mnw, late, onw, l, travel=None):
    s = xs.shape[0]
    bpad = _pad_lanes(b_forget)
    qkv, gf = _inproj_fwd(xs, nw, w_r, f"inproj_fwd_{l}")
    fqb, frow, fbounds = _fox_prep_fwd(gf, qkv, bpad, f"fox_prep_fwd_{l}")
    frow = frow.reshape(FOX_HEADS // 2, 2, s)
    travel = _Travel(travel)
    ysb, _ = travel.ride(0, _sb_fwd, qkv, f"sb_fwd_{l}")
    yfx, lse_fx, _ = travel.ride(1, _fox_fwd, qkv, fqb, frow, fbounds, f"fox_fwd_{l}")
    wkv, wout = late(travel.lands)
    kv = _memkv_fwd(mem, mnw, wkv, f"memkv_fwd_{l}")
    ym, lse_m = _mem_fwd(qkv, kv, f"mem_fwd_{l}")
    xn = _outproj_fwd(ysb, yfx, ym, gf, onw, wout, xs, f"outproj_fwd_{l}")
    saved = (xs, nw, mnw, onw, bpad, qkv, gf, fqb, frow, fbounds, ysb, yfx, lse_fx, kv, ym, lse_m)
    return xn, saved, travel.lands, (wkv, wout)


class _Travel:
    def __init__(self, plan):
        self.plan = plan
        self.lands = None if plan is None else _new_lands(plan[0], plan[1])

    def ride(self, n, fn, *args):
        if self.plan is None or self.plan[2][n] is None:
            return fn(*args)
        srcs, scatter, legs = self.plan
        idx, rows = legs[n]
        out = fn(*args, rider=_Rider([srcs[a] for a in idx], [self.lands[a] for a in idx],
                                     scatter, rows))
        for a, land in zip(idx, out[-1]):
            self.lands[a] = land
        return out


def _layer_bwd(dx, saved, mem, w_r, wkv, wout, l, travel=None):
    xs, nw, mnw, onw, bpad, qkv, gf, fqb, frow, fbounds, ysb, yfx, lse_fx, kv, ym, lse_m = saved
    s = xs.shape[0]
    dysb, dyfx, dym, dgate, dwout, donw = _outproj_bwd(
        dx, wout, ysb, yfx, ym, gf, onw, f"outproj_bwd_{l}")
    travel = _Travel(None if travel is None else travel(dwout))
    sdq, sdk, sdv, _ = travel.ride(0, _sb_bwd, qkv, ysb, dysb, f"sb_bwd_{l}")
    fdq, fdk, fdv, dfrow, _ = travel.ride(1, _fox_bwd, qkv, fqb, frow, fbounds, yfx, lse_fx, dyfx,
                                          f"fox_bwd_{l}")
    dfl, db = _fox_prep_bwd(dfrow.reshape(FOX_HEADS, s), gf, bpad, f"fox_prep_bwd_{l}")
    dmq, dmk, dmv = _mem_bwd(qkv, kv, ym, lse_m, dym, f"mem_bwd_{l}")
    dwkv, dmnw = _memkv_bwd(mem, mnw, wkv, dmk, dmv, f"memkv_bwd_{l}")
    dx, ht, dnw, dproj = _inproj_bwd_dx([sdq, sdk, sdv, fdq, fdk, fdv, dmq, dgate, dfl],
                                        w_r, xs, nw, dx, f"inproj_bwd_dx_{l}")
    dwr = _inproj_bwd_dw(ht, dproj, f"inproj_bwd_dw_{l}")
    grads = dict(norm_w=dnw[0], w_r=dwr, b_forget=db[0, :FOX_HEADS], mem_norm_w=dmnw[0],
                 w_mem_kv=dwkv, out_norm_w=donw[0], w_out=dwout)
    return dx, grads, travel.lands


_ANY = pl.BlockSpec(memory_space=pl.ANY)


def _my_place():
    return lax.axis_index("x"), lax.axis_index("y"), lax.axis_index("c")


def _flip(v, bit):
    return 1 - v if bit else v


def _block_index(px, py, pc):
    return 4 * px + 2 * py + pc


def _all_gather_weights(shards, name):
    n = len(shards)

    def body(*refs):
        ins, outs = refs[:n], refs[n:2 * n]
        send_sems, recv_sems, local_sems = refs[2 * n:]
        x, y, c = _my_place()
        me = (x, y, c)
        sibling = (x, y, 1 - c)
        chips = [(1 - x, y), (x, 1 - y), (1 - x, 1 - y)]

        def copy(a, k, block, to, src=None):
            dst = outs[a].at[_block_index(*block)]
            return pltpu.make_async_remote_copy(
                src_ref=dst if src is None else src, dst_ref=dst,
                send_sem=send_sems.at[a, k], recv_sem=recv_sems.at[a, k],
                device_id=to, device_id_type=pl.DeviceIdType.MESH)

        mine = [pltpu.make_async_copy(ins[a], outs[a].at[_block_index(*me)], local_sems.at[a])
                for a in range(n)]
        for cp in mine:
            cp.start()
        first = []
        for a in range(n):
            first.append(copy(a, 0, me, sibling, src=ins[a]))
            first += [copy(a, 1 + j, me, (*chip, c), src=ins[a]) for j, chip in enumerate(chips)]
        for cp in first:
            cp.start()
        passed = []
        for j, chip in enumerate(chips):
            for a in range(n):
                copy(a, 1 + j, (*chip, c), me).wait_recv()
                fwd = copy(a, 4 + j, (*chip, c), sibling)
                fwd.start()
                passed.append(fwd)
        for a in range(n):
            copy(a, 0, sibling, me).wait_recv()
            for j, chip in enumerate(chips):
                copy(a, 4 + j, (*chip, 1 - c), me).wait_recv()
        for cp in first + passed:
            cp.wait_send()
        for cp in mine:
            cp.wait()

    return pl.pallas_call(
        body, name=name,
        in_specs=[_ANY] * n, out_specs=[_ANY] * n,
        out_shape=[jax.ShapeDtypeStruct((N_DEV,) + v.shape, v.dtype) for v in shards],
        scratch_shapes=[pltpu.SemaphoreType.DMA((n, 7)), pltpu.SemaphoreType.DMA((n, 7)),
                        pltpu.SemaphoreType.DMA((n,))],
    )(*shards)


def _exchange_blocks(blocked, name):
    n = len(blocked)

    def body(*refs):
        ins, outs = refs[:n], refs[n:2 * n]
        send_sems, recv_sems, local_sems = refs[2 * n:]
        x, y, c = _my_place()
        mine_idx = _block_index(x, y, c)
        local = [pltpu.make_async_copy(ins[a].at[mine_idx], outs[a].at[mine_idx], local_sems.at[a])
                 for a in range(n)]
        for cp in local:
            cp.start()
        sends, arrivals = [], []
        for r in range(1, N_DEV):
            peer = (_flip(x, r & 4), _flip(y, r & 2), _flip(c, r & 1))
            peer_idx = _block_index(*peer)
            for a in range(n):
                sems = dict(send_sem=send_sems.at[a, r - 1], recv_sem=recv_sems.at[a, r - 1],
                            device_id=peer, device_id_type=pl.DeviceIdType.MESH)
                sends.append(pltpu.make_async_remote_copy(
                    src_ref=ins[a].at[peer_idx], dst_ref=outs[a].at[mine_idx], **sems))
                arrivals.append(pltpu.make_async_remote_copy(
                    src_ref=ins[a].at[peer_idx], dst_ref=outs[a].at[peer_idx], **sems))
        for cp in sends:
            cp.start()
        for cp in arrivals:
            cp.wait_recv()
        for cp in sends:
            cp.wait_send()
        for cp in local:
            cp.wait()

    return pl.pallas_call(
        body, name=name,
        in_specs=[_ANY] * n, out_specs=[_ANY] * n,
        out_shape=[jax.ShapeDtypeStruct(v.shape, v.dtype) for v in blocked],
        scratch_shapes=[pltpu.SemaphoreType.DMA((n, 7)), pltpu.SemaphoreType.DMA((n, 7)),
                        pltpu.SemaphoreType.DMA((n,))],
    )(*blocked)


N_CHIP = N_DEV // 2


def _pair_swap(blocked, name):
    n = len(blocked)

    def body(*refs):
        ins, outs = refs[:n], refs[n:2 * n]
        send_sems, recv_sems = refs[2 * n:]
        x, y, c = _my_place()
        copies = [pltpu.make_async_remote_copy(
            src_ref=ins[a].at[j, 1 - c], dst_ref=outs[a].at[j],
            send_sem=send_sems.at[N_CHIP * a + j], recv_sem=recv_sems.at[N_CHIP * a + j],
            device_id=(x, y, 1 - c), device_id_type=pl.DeviceIdType.MESH)
            for a in range(n) for j in range(N_CHIP)]
        for cp in copies:
            cp.start()
        for cp in copies:
            cp.wait_recv()
        for cp in copies:
            cp.wait_send()

    return pl.pallas_call(
        body, name=name,
        in_specs=[_ANY] * n, out_specs=[_ANY] * n,
        out_shape=[jax.ShapeDtypeStruct((N_CHIP,) + v.shape[2:], v.dtype) for v in blocked],
        scratch_shapes=[pltpu.SemaphoreType.DMA((N_CHIP * n,)),
                        pltpu.SemaphoreType.DMA((N_CHIP * n,))],
    )(*blocked)


def _pair_add(mine, theirs, name):
    _, nrow, ncol = mine.shape

    def body(a_ref, b_ref, o_ref):
        o_ref[...] = (a_ref[...].astype(F32) + b_ref[...].astype(F32)).astype(BF16)

    blk = pl.BlockSpec((None, nrow, ncol), lambda j: (j, 0, 0))
    return pl.pallas_call(
        body, name=name, grid=(N_CHIP,), in_specs=[blk, blk], out_specs=blk,
        out_shape=jax.ShapeDtypeStruct(mine.shape, BF16), compiler_params=_ARB1,
    )(mine, theirs)


def _chip_exchange(by_chip, to_all, name):
    n, m = len(by_chip), len(to_all)

    def body(*refs):
        ins, alls = refs[:n], refs[n:n + m]
        outs, all_outs = refs[n + m:2 * n + m], refs[2 * n + m:2 * (n + m)]
        send_sems, recv_sems, local_sems = refs[2 * (n + m):]
        x, y, c = _my_place()
        my_chip, mine_idx = 2 * x + y, _block_index(x, y, c)
        local = [pltpu.make_async_copy(ins[a].at[my_chip], outs[a].at[my_chip], local_sems.at[a])
                 for a in range(n)]
        local += [pltpu.make_async_copy(alls[b].at[mine_idx], all_outs[b].at[mine_idx],
                                        local_sems.at[n + b]) for b in range(m)]
        for cp in local:
            cp.start()
        sends, arrivals = [], []
        k = 0
        for r in range(1, N_DEV):
            peer = (_flip(x, r & 4), _flip(y, r & 2), _flip(c, r & 1))
            peer_chip, peer_idx = 2 * peer[0] + peer[1], _block_index(*peer)
            pairs = [(alls[b].at[mine_idx], all_outs[b].at[mine_idx], all_outs[b].at[peer_idx])
                     for b in range(m)]
            if not r & 1:
                pairs += [(ins[a].at[peer_chip], outs[a].at[my_chip], outs[a].at[peer_chip])
                          for a in range(n)]
            for src, there, here in pairs:
                sems = dict(send_sem=send_sems.at[k], recv_sem=recv_sems.at[k], device_id=peer,
                            device_id_type=pl.DeviceIdType.MESH)
                sends.append(pltpu.make_async_remote_copy(src_ref=src, dst_ref=there, **sems))
                arrivals.append(pltpu.make_async_remote_copy(src_ref=src, dst_ref=here, **sems))
                k += 1
        for cp in sends:
            cp.start()
        for cp in arrivals:
            cp.wait_recv()
        for cp in sends:
            cp.wait_send()
        for cp in local:
            cp.wait()

    n_copies = 7 * m + 3 * n
    return pl.pallas_call(
        body, name=name,
        in_specs=[_ANY] * (n + m), out_specs=[_ANY] * (n + m),
        out_shape=[jax.ShapeDtypeStruct(v.shape, v.dtype) for v in by_chip + to_all],
        scratch_shapes=[pltpu.SemaphoreType.DMA((n_copies,)), pltpu.SemaphoreType.DMA((n_copies,)),
                        pltpu.SemaphoreType.DMA((n + m,))],
    )(*by_chip, *to_all)


class _Rider(NamedTuple):
    srcs: list
    lands: list
    scatter: bool
    part: list


def _new_lands(srcs, scatter):
    return [lax.empty(v.shape if scatter else (N_DEV,) + v.shape, v.dtype) for v in srcs]


def _rider_copies(srcs, lands, send_sems, recv_sems, rider):
    x, y, c = _my_place()
    mine_idx = _block_index(x, y, c)

    def window(ref, a):
        if rider.part[a] is None:
            return ref
        dim, start, size = rider.part[a]
        return ref.at[(slice(None),) * dim + (pl.ds(start, size),)]

    sends, arrivals = [], []
    for r in range(1, N_DEV):
        peer = (_flip(x, r & 4), _flip(y, r & 2), _flip(c, r & 1))
        peer_idx = _block_index(*peer)
        for a in range(len(srcs)):
            src = window(srcs[a].at[peer_idx] if rider.scatter else srcs[a], a)
            k = 7 * a + r - 1
            sems = dict(send_sem=send_sems.at[k], recv_sem=recv_sems.at[k],
                        device_id=peer, device_id_type=pl.DeviceIdType.MESH)
            sends.append(pltpu.make_async_remote_copy(
                src_ref=src, dst_ref=window(lands[a].at[mine_idx], a), **sems))
            arrivals.append(pltpu.make_async_remote_copy(
                src_ref=src, dst_ref=window(lands[a].at[peer_idx], a), **sems))
    return sends, arrivals


def _ride(call, rider):
    call = dict(call)
    body, grid = call.pop("body"), call["grid"]
    operands = call.pop("operands")
    if rider is None:
        return list(pl.pallas_call(body, **call)(*operands)), None
    n_in, n_out = len(call["in_specs"]), len(call["out_specs"])
    n_scratch = len(call["scratch_shapes"])
    m = len(rider.srcs)

    def riding(*refs):
        main_in, srcs, lands = refs[:n_in], refs[n_in:n_in + m], refs[n_in + m:n_in + 2 * m]
        main_out = refs[n_in + 2 * m:n_in + 2 * m + n_out]
        rest = refs[n_in + 3 * m + n_out:]
        send_sems, recv_sems = rest[n_scratch:]
        at = [pl.program_id(d) for d in range(len(grid))]
        first = functools.reduce(jnp.logical_and, [p == 0 for p in at])
        last = functools.reduce(jnp.logical_and, [p == g - 1 for p, g in zip(at, grid)])
        sends, arrivals = _rider_copies(srcs, lands, send_sems, recv_sems, rider)

        @pl.when(first)
        def _():
            for cp in sends:
                cp.start()

        body(*main_in, *main_out, *rest[:n_scratch])

        @pl.when(last)
        def _():
            for cp in arrivals:
                cp.wait_recv()
            for cp in sends:
                cp.wait_send()

    call["in_specs"] = list(call["in_specs"]) + [_ANY] * (2 * m)
    call["out_specs"] = list(call["out_specs"]) + [_ANY] * m
    call["out_shape"] = list(call["out_shape"]) + [
        jax.ShapeDtypeStruct(v.shape, v.dtype) for v in rider.lands]
    call["scratch_shapes"] = list(call["scratch_shapes"]) + [
        pltpu.SemaphoreType.DMA((7 * m,)), pltpu.SemaphoreType.DMA((7 * m,))]
    call["input_output_aliases"] = {n_in + m + a: n_out + a for a in range(m)}
    outs = pl.pallas_call(riding, **call)(*operands, *rider.srcs, *rider.lands)
    return list(outs[:n_out]), list(outs[n_out:])


def _sum_parts(p_ref):
    g = p_ref[0].astype(F32)
    for k in range(1, p_ref.shape[0]):
        g = g + p_ref[k].astype(F32)
    return g


def _adamw(g, w, m, v):
    c1 = 1.0 / (1.0 - ADAM_B1 ** ADAM_STEP)
    c2 = 1.0 / (1.0 - ADAM_B2 ** ADAM_STEP)
    nm = ADAM_B1 * m + (1.0 - ADAM_B1) * g
    nv = ADAM_B2 * v + (1.0 - ADAM_B2) * (g * g)
    return nm, nv, -ADAM_LR * ((nm * c1) / (jnp.sqrt(nv * c2) + ADAM_EPS) + ADAM_WD * w)


def _adamw_w_in(parts, w, m, v, name):
    ncol_blk, depth, nfeat = w.shape
    cols = 256

    def body(*refs):
        p_refs = refs[:depth]
        w_ref, m_ref, v_ref, g_ref, d_ref, nm_ref, nv_ref = refs[depth:]
        for l in range(depth):
            g = _sum_parts(p_refs[l])
            nm, nv, d = _adamw(g, w_ref[:, l, :], m_ref[:, l, :], v_ref[:, l, :])
            g_ref[:, l, :] = g
            nm_ref[:, l, :] = nm
            nv_ref[:, l, :] = nv
            d_ref[:, l, :] = d

    blk = pl.BlockSpec((ncol_blk, depth, cols), lambda j: (0, 0, j))
    return pl.pallas_call(
        body, name=name, grid=(nfeat // cols,),
        in_specs=[pl.BlockSpec((p.shape[0], ncol_blk, cols), lambda j: (0, 0, j)) for p in parts]
        + [blk] * 3,
        out_specs=[blk] * 4,
        out_shape=[jax.ShapeDtypeStruct(w.shape, F32)] * 4,
        compiler_params=_ARB1,
    )(*parts, w, m, v)


def _adamw_sum(parts, w, m, v, tile, name):
    depth, nrow, ncol = w.shape
    rows, cols = tile

    def body(*refs):
        p_refs = refs[:depth]
        w_ref, m_ref, v_ref, g_ref, d_ref, nm_ref, nv_ref = refs[depth:]
        layer = pl.program_id(0)
        for l in range(depth):
            @pl.when(layer == l)
            def _(p_ref=p_refs[l]):
                g = _sum_parts(p_ref)
                nm, nv, d = _adamw(g, w_ref[...], m_ref[...], v_ref[...])
                g_ref[...] = g
                nm_ref[...] = nm
                nv_ref[...] = nv
                d_ref[...] = d

    def part_spec(l):
        return pl.BlockSpec((parts[l].shape[0], rows, cols), lambda q, i, j: (
            0, jnp.where(q == l, i, 0), jnp.where(q == l, j, 0)))

    blk = pl.BlockSpec((None, rows, cols), lambda q, i, j: (q, i, j))
    return pl.pallas_call(
        body, name=name, grid=(depth, nrow // rows, ncol // cols),
        in_specs=[part_spec(l) for l in range(depth)] + [blk, blk, blk],
        out_specs=[blk] * 4,
        out_shape=[jax.ShapeDtypeStruct(w.shape, F32)] * 4,
        compiler_params=pltpu.CompilerParams(
            dimension_semantics=("arbitrary", "arbitrary", "arbitrary")),
    )(*parts, w, m, v)


def _pack_small(norm_w, mem_norm_w, out_norm_w, final_norm_w, b_forget):
    onw = jnp.pad(out_norm_w.reshape(20, LANES), ((0, 4), (0, 0)))
    b = jnp.pad(b_forget, ((0, 6), (0, LANES - FOX_HEADS)))
    return jnp.concatenate([norm_w.reshape(16, LANES), mem_norm_w.reshape(16, LANES), onw,
                            final_norm_w.reshape(8, LANES), b], axis=0)


def _unpack_small(p):
    return (p[0:16].reshape(2, D_MODEL), p[16:32].reshape(2, D_MODEL), p[32:52].reshape(2, MIX_W),
            p[56:64].reshape(D_MODEL), p[64:66, :FOX_HEADS])


def kernel(x, mem, norm_w, w_in, b_forget, mem_norm_w, w_mem_kv, out_norm_w, w_out, final_norm_w, loss_target, m_norm_w, m_w_in, m_b_forget, m_mem_norm_w, m_w_mem_kv, m_out_norm_w, m_w_out, m_final_norm_w, v_norm_w, v_w_in, v_b_forget, v_mem_norm_w, v_w_mem_kv, v_out_norm_w, v_w_out, v_final_norm_w):
    kv_rows = w_mem_kv.shape[1]
    out_rows = w_out.shape[1]
    me = _block_index(*_my_place())

    def shards(l):
        return [w_in[l].T.astype(BF16), w_mem_kv[l].astype(BF16), w_out[l].astype(BF16)]

    def full_in(g_in):
        return _rearrange_w_in(g_in.reshape(IN_W, D_MODEL))

    def full_kv_out(g_kv, g_out):
        return g_kv.reshape(D_MODEL, 2 * MEM_W), g_out.reshape(MIX_W, D_MODEL)

    def in_blocks(g):
        segments = [(0, 3072, 0), (3072, 3080, QKV_W + MIX_W), (3080, 3336, 3072),
                    (3336, IN_W, QKV_W)]

        def block(k):
            lo, hi = k * SHARD_W, (k + 1) * SHARD_W
            pieces = [g[at + max(lo, a) - a:at + min(hi, b) - a]
                      for a, b, at in segments if max(lo, a) < min(hi, b)]
            return jnp.concatenate(pieces, axis=0).astype(BF16)

        return jnp.stack([block(k) for k in range(N_DEV)])

    def kv_blocks(g):
        return g.reshape(N_DEV, kv_rows, 2 * MEM_W).astype(BF16)

    def out_blocks(g):
        return g.reshape(N_DEV, out_rows, D_MODEL).astype(BF16)

    def with_own(land, own):
        return lax.dynamic_update_slice(land, own[None], (me,) + (0,) * own.ndim)

    def with_own_of(land, blocked):
        return lax.dynamic_update_slice(land, lax.dynamic_slice_in_dim(blocked, me, 1, axis=0),
                                        (me,) + (0,) * (land.ndim - 1))

    def row(v):
        return v.reshape(1, -1)

    def cols(first, size):
        return (1, first, size)

    fwd_split, bwd_split = 5 * LANES, 6 * LANES

    g_in0, g_kv0, g_out0 = _all_gather_weights(shards(0), "all_gather_l0")
    w_r0 = full_in(g_in0)
    s_in1, s_kv1, s_out1 = shards(1)
    x1, saved0, (l_in1,), (wkv0, wout0) = _layer_fwd(
        x[0], mem[0], row(norm_w[0]), w_r0, b_forget[0], row(mem_norm_w[0]),
        lambda lands: full_kv_out(g_kv0, g_out0), row(out_norm_w[0]), 0,
        travel=([s_in1], False, [([0], [cols(0, fwd_split)]),
                                 ([0], [cols(fwd_split, D_MODEL - fwd_split)])]))
    w_r1 = full_in(with_own(l_in1, s_in1))
    x2, saved1, _, (wkv1, wout1) = _layer_fwd(
        x1, mem[0], row(norm_w[1]), w_r1, b_forget[1], row(mem_norm_w[1]),
        lambda lands: full_kv_out(with_own(lands[0], s_kv1), with_own(lands[1], s_out1)),
        row(out_norm_w[1]), 1, travel=([s_kv1, s_out1], False, [([0, 1], [None, None]), None]))

    dx2, loss_part, dfnw = _final_fwd_bwd(x2, row(final_norm_w), loss_target[0], "final_fwd_bwd")

    dx1, gr1, (l_out1,) = _layer_bwd(
        dx2, saved1, mem[0], w_r1, wkv1, wout1, 1,
        travel=lambda dwout: ([out_blocks(dwout)], True, [([0], [None]), None]))
    p_in1, p_kv1 = in_blocks(gr1["w_r"]), kv_blocks(gr1["w_mem_kv"])
    grad_x, gr0, (l_in1, l_kv1, l_out0) = _layer_bwd(
        dx1, saved0, mem[0], w_r0, wkv0, wout0, 0,
        travel=lambda dwout: ([p_in1, p_kv1, out_blocks(dwout)], True,
                              [([0, 1], [cols(0, bwd_split), None]),
                               ([0, 2], [cols(bwd_split, D_MODEL - bwd_split), None])]))
    r_out1 = with_own_of(l_out1, out_blocks(gr1["w_out"]))
    r_in1, r_kv1 = with_own_of(l_in1, p_in1), with_own_of(l_kv1, p_kv1)
    r_out0 = with_own_of(l_out0, out_blocks(gr0["w_out"]))

    def both(name):
        return jnp.stack([gr0[name], gr1[name]])

    small = _pack_small(both("norm_w"), both("mem_norm_w"), both("out_norm_w"), dfnw[0],
                        both("b_forget")).at[LOSS_ROW].set(loss_part[0])
    p_small = jnp.broadcast_to(small[None], (N_DEV, SMALL_ROWS, LANES))
    by_core = [v.reshape((N_CHIP, 2) + v.shape[1:])
               for v in (in_blocks(gr0["w_r"]), kv_blocks(gr0["w_mem_kv"]))]
    from_sibling = _pair_swap(by_core, "grads_l0_pair_swap")
    core = lax.axis_index("c")
    chip_sums = [_pair_add(lax.dynamic_index_in_dim(v, core, axis=1, keepdims=False), got,
                           f"grads_l0_pair_add_{a}")
                 for a, (v, got) in enumerate(zip(by_core, from_sibling))]
    r_in0, r_kv0, r_small = _chip_exchange(chip_sums, [p_small], "exchange_grads_l0")

    def view(v):
        return jnp.transpose(v, (2, 0, 1))

    g_w_in, d_w_in, nm_w_in, nv_w_in = [jnp.transpose(v, (1, 2, 0)) for v in _adamw_w_in(
        [r_in0, r_in1], view(w_in), view(m_w_in), view(v_w_in), "adamw_w_in")]
    g_w_kv, d_w_kv, nm_w_kv, nv_w_kv = _adamw_sum(
        [r_kv0, r_kv1], w_mem_kv, m_w_mem_kv, v_w_mem_kv, (kv_rows, 2 * MEM_W), "adamw_w_mem_kv")
    g_w_out, d_w_out, nm_w_out, nv_w_out = _adamw_sum(
        [r_out0, r_out1], w_out, m_w_out, v_w_out, (out_rows, D_MODEL), "adamw_w_out")
    w_small = _pack_small(norm_w, mem_norm_w, out_norm_w, final_norm_w, b_forget)[None]
    m_small = _pack_small(m_norm_w, m_mem_norm_w, m_out_norm_w, m_final_norm_w, m_b_forget)[None]
    v_small = _pack_small(v_norm_w, v_mem_norm_w, v_out_norm_w, v_final_norm_w, v_b_forget)[None]
    small_out = _adamw_sum([r_small], w_small, m_small, v_small, (SMALL_ROWS, LANES), "adamw_small")
    (g_nw, g_mnw, g_onw, g_fnw, g_b), (d_nw, d_mnw, d_onw, d_fnw, d_b), \
        (nm_nw, nm_mnw, nm_onw, nm_fnw, nm_b), (nv_nw, nv_mnw, nv_onw, nv_fnw, nv_b) = [
            _unpack_small(t[0]) for t in small_out]
    loss = small_out[0][0, LOSS_ROW, 0]

    return (loss, grad_x[None],
            g_nw, g_w_in, g_b, g_mnw, g_w_kv, g_onw, g_w_out, g_fnw,
            d_nw, d_w_in, d_b, d_mnw, d_w_kv, d_onw, d_w_out, d_fnw,
            nm_nw, nm_w_in, nm_b, nm_mnw, nm_w_kv, nm_onw, nm_w_out, nm_fnw,
            nv_nw, nv_w_in, nv_b, nv_mnw, nv_w_kv, nv_onw, nv_w_out, nv_fnw)
```

```python
import functools
from typing import NamedTuple

import jax
import jax.numpy as jnp
from jax import lax
from jax.experimental import pallas as pl
from jax.experimental.pallas import tpu as pltpu

F32 = jnp.float32
BF16 = jnp.bfloat16

N_DEV = 8
D_MODEL = 1024
HEAD_DIM = 64
LANES = 128
SB_W = 512
FX_W = 512
MEM_W = 256
MIX_W = 1280
FOX_HEADS = 8
IN_W = 4616
SHARD_W = IN_W // N_DEV
QKV_W = 3 * SB_W + 3 * FX_W + MEM_W
FL_PAD = 256
GF_W = MIX_W + FL_PAD
WR_W = QKV_W + GF_W
EPS = 1e-6
T = 256
QPS = 8
TM = 256
TQM = 512
Q_SCALE = 0.125
NEG = -1e30
UNDERFLOW = -110.0
NORM_SLACK = 1.01

ADAM_LR = 0.001
ADAM_B1 = 0.9
ADAM_B2 = 0.999
ADAM_EPS = 1e-08
ADAM_WD = 0.01
ADAM_STEP = 10

SMALL_ROWS = 72
LOSS_ROW = 66

_NT = (((1,), (1,)), ((), ()))
_TN = (((0,), (0,)), ((), ()))

_ARB1 = pltpu.CompilerParams(dimension_semantics=("arbitrary",))
_ARB2 = pltpu.CompilerParams(dimension_semantics=("arbitrary", "arbitrary"))


def _dot(a, b):
    return jnp.dot(a, b, preferred_element_type=F32)


def _dot_nt(a, b):
    return lax.dot_general(a, b, _NT, preferred_element_type=F32)


def _dot_tn(a, b):
    return lax.dot_general(a, b, _TN, preferred_element_type=F32)


def _split2(x):
    hi = x.astype(BF16)
    lo = (x - hi.astype(F32)).astype(BF16)
    return hi, lo


def _stack2(u):
    return jnp.concatenate([u, u], axis=0)


def _cum2(x, u2):
    hi, lo = _split2(x)
    return _dot(jnp.concatenate([hi, lo], axis=1), u2)


def _tri3(tri, x, dot=None):
    dot = dot or _dot
    hi = x.astype(BF16)
    r1 = x - hi.astype(F32)
    mid = r1.astype(BF16)
    lo = (r1 - mid.astype(F32)).astype(BF16)
    return dot(tri, hi) + dot(tri, mid) + dot(tri, lo)


def _iota2(shape, dim):
    return lax.broadcasted_iota(jnp.int32, shape, dim)


def _head_block_diag():
    r = _iota2((LANES, LANES), 0) // HEAD_DIM
    c = _iota2((LANES, LANES), 1) // HEAD_DIM
    return _stack2(jnp.where(r == c, 1.0, 0.0).astype(BF16))


def _head_mean(x, bd):
    return _cum2(x, bd) * (1.0 / HEAD_DIM)


def _sigmoid(x):
    return 1.0 / (1.0 + jnp.exp(-x))


def _log_sigmoid(x):
    return jnp.minimum(x, 0.0) - jnp.log(1.0 + jnp.exp(-jnp.abs(x)))


def _running_top(r_ref):
    return jnp.max(jnp.maximum(r_ref[0], r_ref[1]))


def _fox_tiles_left(i, pair, nq, fb_ref, tile):
    def bound(j):
        b = []
        for h in range(2):
            head = 2 * pair + h
            b.append(2.0 * NORM_SLACK * fb_ref[2 * nq + i, head] * fb_ref[3 * nq, head]
                     + fb_ref[2 * i, head] - fb_ref[2 * j + 1, head])
        return jnp.maximum(b[0], b[1])

    def more(j):
        return jnp.logical_and(j >= 0, bound(jnp.maximum(j, 0)) > UNDERFLOW)

    def step(j):
        tile(j, False)
        return j - 1

    return lax.while_loop(more, step, i - 1)


def _pair_masks():
    lane = _iota2((1, LANES), 1)
    return lane < HEAD_DIM


def _split_pair(x, m0):
    zero = jnp.zeros_like(x)
    return jnp.where(m0, x, zero), jnp.where(m0, zero, x)


def _inproj_fwd(x, nw, w_r, name):
    s = x.shape[0]

    def body(x_ref, nw_ref, w_ref, qkv_ref, gf_ref):
        xv = x_ref[...]
        r = lax.rsqrt(jnp.mean(xv * xv, axis=-1, keepdims=True) + EPS)
        h = (xv * r * nw_ref[...]).astype(BF16)
        for c in range(0, QKV_W, 256):
            qkv_ref[:, c:c + 256] = _dot_nt(h, w_ref[c:c + 256, :]).astype(BF16)
        for c in range(0, GF_W, 256):
            gf_ref[:, c:c + 256] = _dot_nt(h, w_ref[QKV_W + c:QKV_W + c + 256, :])

    return pl.pallas_call(
        body, name=name, grid=(s // TM,),
        in_specs=[pl.BlockSpec((TM, D_MODEL), lambda i: (i, 0)),
                  pl.BlockSpec((1, D_MODEL), lambda i: (0, 0)),
                  pl.BlockSpec((WR_W, D_MODEL), lambda i: (0, 0))],
        out_specs=[pl.BlockSpec((TM, QKV_W), lambda i: (i, 0)),
                   pl.BlockSpec((TM, GF_W), lambda i: (i, 0))],
        out_shape=[jax.ShapeDtypeStruct((s, QKV_W), BF16), jax.ShapeDtypeStruct((s, GF_W), F32)],
        compiler_params=_ARB1,
    )(x, nw, w_r)


def _fox_prep_fwd(gf, qkv, bpad, name):
    s = gf.shape[0]
    nq = s // T
    nrow = -(-(3 * nq + 1) // 8) * 8

    def body(fl_ref, q_ref, k_ref, b_ref, fq_ref, fr_ref, fb_ref):
        tri = jnp.where(_iota2((T, T), 0) >= _iota2((T, T), 1), 1.0, 0.0).astype(BF16)
        m0 = _pair_masks()
        lane = _iota2((1, LANES), 1)
        norms = [jnp.zeros((1, LANES), F32) for _ in range(nq + 1)]
        same_head = (_iota2((LANES, LANES), 0) // HEAD_DIM) == (_iota2((LANES, LANES), 1) // HEAD_DIM)
        bd = jnp.where(same_head, 1.0, 0.0).astype(BF16)
        for p in range(FOX_HEADS // 2):
            cols = slice(p * LANES, (p + 1) * LANES)
            q = (q_ref[:, cols] * jnp.asarray(Q_SCALE, BF16)).astype(F32)
            k = k_ref[:, cols].astype(F32)
            qn = _dot((q * q).astype(BF16), bd)
            kn = _dot((k * k).astype(BF16), bd)
            tops = [jnp.max(qn[j * T:(j + 1) * T], axis=0, keepdims=True) for j in range(nq)]
            tops.append(jnp.max(kn, axis=0, keepdims=True))
            tops = [jnp.sqrt(top) for top in tops]
            for h in range(2):
                at = h * HEAD_DIM
                norms = [jnp.where(lane == 2 * p + h, top[:, at:at + 1], row)
                         for top, row in zip(tops, norms)]
        for j in range(nq + 1):
            fb_ref[2 * nq + j:2 * nq + j + 1, :] = norms[j]
        fb_ref[3 * nq + 1:, :] = jnp.zeros((nrow - 3 * nq - 1, LANES), F32)
        carry = jnp.zeros((1, LANES), F32)
        for blk in range(s // T):
            rows = slice(blk * T, (blk + 1) * T)
            lf = _log_sigmoid(fl_ref[rows, :] + b_ref[...])
            c = _tri3(tri, lf) + carry
            carry = c[T - 1:T, :]
            for p in range(FOX_HEADS // 2):
                fq_ref[rows, p * LANES:(p + 1) * LANES] = jnp.where(
                    m0, c[:, 2 * p:2 * p + 1], c[:, 2 * p + 1:2 * p + 2])
            fr_ref[:, rows] = c.T[0:FOX_HEADS, :]
            fb_ref[2 * blk:2 * blk + 1, :] = c[0:1, :]
            fb_ref[2 * blk + 1:2 * blk + 2, :] = carry

    base = 3 * SB_W // FX_W
    return pl.pallas_call(
        body, name=name, grid=(1,),
        in_specs=[pl.BlockSpec((s, LANES), lambda i: (0, MIX_W // LANES)),
                  pl.BlockSpec((s, FX_W), lambda i: (0, base)),
                  pl.BlockSpec((s, FX_W), lambda i: (0, base + 1)),
                  pl.BlockSpec((1, LANES), lambda i: (0, 0))],
        out_specs=[pl.BlockSpec((s, FX_W), lambda i: (0, 0)),
                   pl.BlockSpec((FOX_HEADS, s), lambda i: (0, 0)),
                   pl.BlockSpec((nrow, LANES), lambda i: (0, 0))],
        out_shape=[jax.ShapeDtypeStruct((s, FX_W), F32), jax.ShapeDtypeStruct((FOX_HEADS, s), F32),
                   jax.ShapeDtypeStruct((nrow, LANES), F32)],
        compiler_params=_ARB1,
    )(gf, qkv, qkv, bpad)


def _sb_fwd(qkv, name, rider=None):
    s = qkv.shape[0]

    def body(q_ref, k_ref, v_ref, o_ref, acc_ref, r_ref, as_ref):
        m0 = _pair_masks()
        strict = _iota2((T, T), 0) > _iota2((T, T), 1)
        u2 = _stack2(jnp.where(strict, 1.0, 0.0).astype(BF16))
        hs = range(2)

        def query_tile(i, rows):
            qh = _split_pair(q_ref[rows, :] * jnp.asarray(Q_SCALE, BF16), m0)
            acc_ref[...] = jnp.zeros_like(acc_ref)
            r_ref[...] = jnp.zeros_like(r_ref)

            def flush(j):
                v = v_ref[pl.ds(pl.multiple_of(j * T, T), T), :]
                for h in hs:
                    acc_ref[h] += _dot(as_ref[h], v)

            def tile(j, diag):
                k = k_ref[pl.ds(pl.multiple_of(j * T, T), T), :]
                z = [_dot_nt(qh[h], k) for h in hs]
                if not diag:
                    flush(j + 1)
                la = [jnp.minimum(z[h], 0.0) - jnp.log(1.0 + jnp.exp(-jnp.abs(z[h]))) for h in hs]
                lf = [la[h] - z[h] for h in hs]
                if diag:
                    lf = [jnp.where(strict, lf[h], 0.0) for h in hs]
                cin = [_cum2(lf[h], u2) for h in hs]
                a = [jnp.exp(la[h] + cin[h] + r_ref[h]) for h in hs]
                if diag:
                    a = [jnp.where(strict, a[h], 0.0) for h in hs]
                for h in hs:
                    r_ref[h] += cin[h][:, 0:1] + lf[h][:, 0:1]
                    as_ref[h] = a[h].astype(BF16)

            tile(i, True)

            def more(state):
                j, top = state
                return jnp.logical_and(j >= 0, top > UNDERFLOW)

            def step(state):
                j, _ = state
                tile(j, False)
                return j - 1, _running_top(r_ref)

            j_left, _ = lax.while_loop(more, step, (i - 1, _running_top(r_ref)))
            flush(j_left + 1)
            o_ref[rows, :] = jnp.where(m0, acc_ref[0], acc_ref[1])

        for n in range(QPS):
            query_tile(QPS * pl.program_id(1) + n, slice(n * T, (n + 1) * T))

    nb = SB_W // LANES
    (ysb,), lands = _ride(dict(
        body=body, name=name, grid=(nb, s // (QPS * T)),
        in_specs=[pl.BlockSpec((QPS * T, LANES), lambda p, i: (i, p)),
                  pl.BlockSpec((s, LANES), lambda p, i: (0, nb + p)),
                  pl.BlockSpec((s, LANES), lambda p, i: (0, 2 * nb + p))],
        out_specs=[pl.BlockSpec((QPS * T, LANES), lambda p, i: (i, p))],
        out_shape=[jax.ShapeDtypeStruct((s, SB_W), F32)],
        scratch_shapes=[pltpu.VMEM((2, T, LANES), F32), pltpu.VMEM((2, T, 1), F32),
                        pltpu.VMEM((2, T, T), BF16)],
        compiler_params=_ARB2, operands=[qkv, qkv, qkv]), rider)
    return ysb, lands


def _fox_fwd(qkv, fqb, frow, fbounds, name, rider=None):
    s = qkv.shape[0]

    def body(q_ref, k_ref, v_ref, fq_ref, fr_ref, fb_ref, o_ref, lse_ref, acc_ref, m_ref, ps_ref):
        pair = pl.program_id(0)
        m0 = _pair_masks()
        causal = _iota2((T, T), 0) >= _iota2((T, T), 1)
        ones = jnp.ones((T, LANES), BF16)
        hs = range(2)

        def query_tile(i, rows):
            qh = _split_pair(q_ref[rows, :] * jnp.asarray(Q_SCALE, BF16), m0)
            fq = fq_ref[rows, :]
            fqh = (fq[:, 0:1], fq[:, HEAD_DIM:HEAD_DIM + 1])
            acc_ref[...] = jnp.zeros_like(acc_ref)
            m_ref[...] = jnp.full_like(m_ref, NEG)

            def flush(j):
                v = v_ref[pl.ds(pl.multiple_of(j * T, T), T), :]
                va2 = _stack2(jnp.concatenate([v, ones], axis=1))
                for h in hs:
                    acc_ref[h] += _dot(ps_ref[h], va2)

            def tile(j, diag):
                off = pl.multiple_of(j * T, T)
                k = k_ref[pl.ds(off, T), :]
                sc = [_dot_nt(qh[h], k) + fqh[h] - fr_ref[h:h + 1, pl.ds(off, T)] for h in hs]
                if not diag:
                    flush(j + 1)
                if diag:
                    sc = [jnp.where(causal, sc[h], NEG) for h in hs]
                m_new = [jnp.maximum(m_ref[h], jnp.max(sc[h], axis=1, keepdims=True)) for h in hs]
                p = [jnp.exp(sc[h] - m_new[h]) for h in hs]
                for h in hs:
                    acc_ref[h] = acc_ref[h] * jnp.exp(m_ref[h] - m_new[h])
                    m_ref[h] = m_new[h]
                    p_hi, p_lo = _split2(p[h])
                    ps_ref[h] = jnp.concatenate([p_hi, p_lo], axis=1)

            tile(i, True)
            j_left = _fox_tiles_left(i, pair, s // T, fb_ref, tile)
            flush(j_left + 1)
            acc = (acc_ref[0], acc_ref[1])
            o_ref[rows, :] = jnp.where(m0, acc[0][:, :LANES] / acc[0][:, LANES:],
                                       acc[1][:, :LANES] / acc[1][:, LANES:])
            lse_ref[rows, :] = jnp.where(m0, m_ref[0] + jnp.log(acc[0][:, LANES:]),
                                         m_ref[1] + jnp.log(acc[1][:, LANES:]))

        for n in range(QPS):
            query_tile(QPS * pl.program_id(1) + n, slice(n * T, (n + 1) * T))

    nb = FX_W // LANES
    base = 3 * SB_W // LANES
    (yfx, lse), lands = _ride(dict(
        body=body, name=name, grid=(nb, s // (QPS * T)),
        in_specs=[pl.BlockSpec((QPS * T, LANES), lambda p, i: (i, base + p)),
                  pl.BlockSpec((s, LANES), lambda p, i: (0, base + nb + p)),
                  pl.BlockSpec((s, LANES), lambda p, i: (0, base + 2 * nb + p)),
                  pl.BlockSpec((QPS * T, LANES), lambda p, i: (i, p)),
                  pl.BlockSpec((None, 2, s), lambda p, i: (p, 0, 0)),
                  pl.BlockSpec(memory_space=pltpu.SMEM)],
        out_specs=[pl.BlockSpec((QPS * T, LANES), lambda p, i: (i, p)),
                   pl.BlockSpec((QPS * T, LANES), lambda p, i: (i, p))],
        out_shape=[jax.ShapeDtypeStruct((s, FX_W), F32), jax.ShapeDtypeStruct((s, FX_W), F32)],
        scratch_shapes=[pltpu.VMEM((2, T, 2 * LANES), F32), pltpu.VMEM((2, T, 1), F32),
                        pltpu.VMEM((2, T, 2 * T), BF16)],
        compiler_params=_ARB2, operands=[qkv, qkv, qkv, fqb, frow, fbounds]), rider)
    return yfx, lse, lands


def _memkv_fwd(mem, mnw, wkv, name):
    n = mem.shape[0]

    def body(mem_ref, mnw_ref, w_ref, kv_ref):
        mv = mem_ref[...]
        r = lax.rsqrt(jnp.mean(mv * mv, axis=-1, keepdims=True) + EPS)
        hm = (mv * r * mnw_ref[...]).astype(BF16)
        kv_ref[...] = _dot(hm, w_ref[...]).astype(BF16)

    return pl.pallas_call(
        body, name=name, grid=(1,),
        in_specs=[pl.BlockSpec((n, D_MODEL), lambda i: (0, 0)),
                  pl.BlockSpec((1, D_MODEL), lambda i: (0, 0)),
                  pl.BlockSpec((D_MODEL, 2 * MEM_W), lambda i: (0, 0))],
        out_specs=pl.BlockSpec((n, 2 * MEM_W), lambda i: (0, 0)),
        out_shape=jax.ShapeDtypeStruct((n, 2 * MEM_W), BF16),
        compiler_params=_ARB1,
    )(mem, mnw, wkv)


def _mem_fwd(qkv, kv, name):
    s = qkv.shape[0]
    n = kv.shape[0]

    def body(q_ref, k_ref, v_ref, o_ref, lse_ref):
        m0 = _pair_masks()
        qh = _split_pair(q_ref[...] * jnp.asarray(Q_SCALE, BF16), m0)
        k = k_ref[...]
        v = v_ref[...]
        outs, lses = [], []
        for h in range(2):
            sc = _dot_nt(qh[h], k)
            mx = jnp.max(sc, axis=1, keepdims=True)
            p = jnp.exp(sc - mx)
            l = jnp.sum(p, axis=1, keepdims=True)
            outs.append(_dot(p.astype(BF16), v) / l)
            lses.append(mx + jnp.log(l))
        o_ref[...] = jnp.where(m0, outs[0], outs[1])
        lse_ref[...] = jnp.where(m0, lses[0], lses[1])

    nb = MEM_W // LANES
    base = (3 * SB_W + 3 * FX_W) // LANES
    return pl.pallas_call(
        body, name=name, grid=(nb, s // TQM),
        in_specs=[pl.BlockSpec((TQM, LANES), lambda p, i: (i, base + p)),
                  pl.BlockSpec((n, LANES), lambda p, i: (0, p)),
                  pl.BlockSpec((n, LANES), lambda p, i: (0, nb + p))],
        out_specs=[pl.BlockSpec((TQM, LANES), lambda p, i: (i, p)),
                   pl.BlockSpec((TQM, LANES), lambda p, i: (i, p))],
        out_shape=[jax.ShapeDtypeStruct((s, MEM_W), F32), jax.ShapeDtypeStruct((s, MEM_W), F32)],
        compiler_params=_ARB2,
    )(qkv, kv, kv)


def _mix_chunk(c, ysb_ref, yfx_ref, ym_ref):
    if c < SB_W // LANES:
        return ysb_ref[:, c * LANES:(c + 1) * LANES]
    c -= SB_W // LANES
    if c < FX_W // LANES:
        return yfx_ref[:, c * LANES:(c + 1) * LANES]
    c -= FX_W // LANES
    return ym_ref[:, c * LANES:(c + 1) * LANES]


def _outproj_fwd(ysb, yfx, ym, gf, onw, wout, x, name):
    s = x.shape[0]

    def body(ysb_ref, yfx_ref, ym_ref, g_ref, onw_ref, w_ref, x_ref, o_ref, yg_ref):
        bd = _head_block_diag()
        for c in range(MIX_W // LANES):
            sl = slice(c * LANES, (c + 1) * LANES)
            u = _mix_chunk(c, ysb_ref, yfx_ref, ym_ref)
            r = lax.rsqrt(_head_mean(u * u, bd) + EPS)
            g = g_ref[:, sl]
            yg_ref[:, sl] = (u * r * onw_ref[:, sl] * (g * _sigmoid(g))).astype(BF16)
        o_ref[...] = x_ref[...] + _dot(yg_ref[...], w_ref[...])

    return pl.pallas_call(
        body, name=name, grid=(s // TM,),
        in_specs=[pl.BlockSpec((TM, SB_W), lambda i: (i, 0)),
                  pl.BlockSpec((TM, FX_W), lambda i: (i, 0)),
                  pl.BlockSpec((TM, MEM_W), lambda i: (i, 0)),
                  pl.BlockSpec((TM, MIX_W), lambda i: (i, 0)),
                  pl.BlockSpec((1, MIX_W), lambda i: (0, 0)),
                  pl.BlockSpec((MIX_W, D_MODEL), lambda i: (0, 0)),
                  pl.BlockSpec((TM, D_MODEL), lambda i: (i, 0))],
        out_specs=pl.BlockSpec((TM, D_MODEL), lambda i: (i, 0)),
        out_shape=jax.ShapeDtypeStruct((s, D_MODEL), F32),
        scratch_shapes=[pltpu.VMEM((TM, MIX_W), BF16)],
        compiler_params=_ARB1,
    )(ysb, yfx, ym, gf, onw, wout, x)


def _final_fwd_bwd(x, fnw, target, name):
    s = x.shape[0]

    def body(x_ref, w_ref, t_ref, dx_ref, loss_ref, dw_ref):
        @pl.when(pl.program_id(0) == 0)
        def _():
            loss_ref[...] = jnp.zeros_like(loss_ref)
            dw_ref[...] = jnp.zeros_like(dw_ref)

        xv = x_ref[...]
        w = w_ref[...]
        r = lax.rsqrt(jnp.mean(xv * xv, axis=-1, keepdims=True) + EPS)
        xh = xv * r
        err = xh * w - t_ref[...]
        part = jnp.sum(jnp.sum(err * err, axis=1, keepdims=True), axis=0, keepdims=True)
        loss_ref[...] += part * (0.5 / D_MODEL)
        dy = err * (1.0 / D_MODEL)
        dw_ref[...] += jnp.sum(dy * xh, axis=0, keepdims=True)
        dxh = dy * w
        dx_ref[...] = r * (dxh - xh * jnp.mean(dxh * xh, axis=-1, keepdims=True))

    return pl.pallas_call(
        body, name=name, grid=(s // TM,),
        in_specs=[pl.BlockSpec((TM, D_MODEL), lambda i: (i, 0)),
                  pl.BlockSpec((1, D_MODEL), lambda i: (0, 0)),
                  pl.BlockSpec((TM, D_MODEL), lambda i: (i, 0))],
        out_specs=[pl.BlockSpec((TM, D_MODEL), lambda i: (i, 0)),
                   pl.BlockSpec((1, LANES), lambda i: (0, 0)),
                   pl.BlockSpec((1, D_MODEL), lambda i: (0, 0))],
        out_shape=[jax.ShapeDtypeStruct((s, D_MODEL), F32), jax.ShapeDtypeStruct((1, LANES), F32),
                   jax.ShapeDtypeStruct((1, D_MODEL), F32)],
        compiler_params=_ARB1,
    )(x, fnw, target)


def _outproj_bwd(dxo, wout, ysb, yfx, ym, gf, onw, name):
    s = dxo.shape[0]

    def body(dx_ref, w_ref, ysb_ref, yfx_ref, ym_ref, g_ref, onw_ref,
             dysb_ref, dyfx_ref, dym_ref, dg_ref, dw_ref, donw_ref, yg_ref):
        @pl.when(pl.program_id(0) == 0)
        def _():
            dw_ref[...] = jnp.zeros_like(dw_ref)
            donw_ref[...] = jnp.zeros_like(donw_ref)

        dxb = dx_ref[...].astype(BF16)
        dyg = _dot_nt(dxb, w_ref[...])
        bd = _head_block_diag()
        for c in range(MIX_W // LANES):
            sl = slice(c * LANES, (c + 1) * LANES)
            u = _mix_chunk(c, ysb_ref, yfx_ref, ym_ref)
            r = lax.rsqrt(_head_mean(u * u, bd) + EPS)
            yn = u * r
            g = g_ref[:, sl]
            sg = _sigmoid(g)
            sil = g * sg
            onw = onw_ref[:, sl]
            e = dyg[:, sl]
            yg_ref[:, sl] = (yn * onw * sil).astype(BF16)
            donw_ref[:, sl] += jnp.sum(e * yn * sil, axis=0, keepdims=True)
            dg_ref[:, sl] = (e * yn * onw * (sg * (1.0 + g * (1.0 - sg)))).astype(BF16)
            dyn = e * onw * sil
            du = (r * (dyn - yn * _head_mean(dyn * yn, bd))).astype(BF16)
            if c < 4:
                dysb_ref[:, c * LANES:(c + 1) * LANES] = du
            elif c < 8:
                dyfx_ref[:, (c - 4) * LANES:(c - 3) * LANES] = du
            else:
                dym_ref[:, (c - 8) * LANES:(c - 7) * LANES] = du
        dw_ref[...] += _dot_tn(yg_ref[...], dxb)

    return pl.pallas_call(
        body, name=name, grid=(s // TM,),
        in_specs=[pl.BlockSpec((TM, D_MODEL), lambda i: (i, 0)),
                  pl.BlockSpec((MIX_W, D_MODEL), lambda i: (0, 0)),
                  pl.BlockSpec((TM, SB_W), lambda i: (i, 0)),
                  pl.BlockSpec((TM, FX_W), lambda i: (i, 0)),
                  pl.BlockSpec((TM, MEM_W), lambda i: (i, 0)),
                  pl.BlockSpec((TM, MIX_W), lambda i: (i, 0)),
                  pl.BlockSpec((1, MIX_W), lambda i: (0, 0))],
        out_specs=[pl.BlockSpec((TM, SB_W), lambda i: (i, 0)),
                   pl.BlockSpec((TM, FX_W), lambda i: (i, 0)),
                   pl.BlockSpec((TM, MEM_W), lambda i: (i, 0)),
                   pl.BlockSpec((TM, MIX_W), lambda i: (i, 0)),
                   pl.BlockSpec((MIX_W, D_MODEL), lambda i: (0, 0)),
                   pl.BlockSpec((1, MIX_W), lambda i: (0, 0))],
        out_shape=[jax.ShapeDtypeStruct((s, SB_W), BF16), jax.ShapeDtypeStruct((s, FX_W), BF16),
                   jax.ShapeDtypeStruct((s, MEM_W), BF16), jax.ShapeDtypeStruct((s, MIX_W), BF16),
                   jax.ShapeDtypeStruct((MIX_W, D_MODEL), F32), jax.ShapeDtypeStruct((1, MIX_W), F32)],
        scratch_shapes=[pltpu.VMEM((TM, MIX_W), BF16)],
        compiler_params=_ARB1,
    )(dxo, wout, ysb, yfx, ym, gf, onw)


def _row_dots(do, o, m0):
    prod = do.astype(F32) * o
    zero = jnp.zeros_like(prod)
    return (jnp.sum(jnp.where(m0, prod, zero), axis=1, keepdims=True),
            jnp.sum(jnp.where(m0, zero, prod), axis=1, keepdims=True))


def _sb_bwd(qkv, o, do, name, rider=None):
    s = qkv.shape[0]
    nq = s // T

    def body(q_ref, k_ref, v_ref, o_ref, do_ref, dq_ref, dk_ref, dv_ref,
             dqa_ref, dka_ref, dva_ref, rl_ref, rg_ref, dzs_ref, abs_ref):
        step_id = pl.program_id(1)

        @pl.when(step_id == 0)
        def _():
            dka_ref[...] = jnp.zeros_like(dka_ref)
            dva_ref[...] = jnp.zeros_like(dva_ref)

        m0 = _pair_masks()
        strict = _iota2((T, T), 0) > _iota2((T, T), 1)
        u2 = _stack2(jnp.where(strict, 1.0, 0.0).astype(BF16))
        hs = range(2)

        def query_tile(i, rows):
            qh = _split_pair(q_ref[rows, :] * jnp.asarray(Q_SCALE, BF16), m0)
            doh = _split_pair(do_ref[rows, :], m0)
            dsum = _row_dots(do_ref[rows, :], o_ref[rows, :], m0)
            dqa_ref[...] = jnp.zeros_like(dqa_ref)
            rl_ref[...] = jnp.zeros_like(rl_ref)
            rg_ref[...] = jnp.zeros_like(rg_ref)

            def flush(j):
                off = pl.multiple_of(j * T, T)
                k = k_ref[pl.ds(off, T), :]
                for h in hs:
                    dqa_ref[h] += _dot(dzs_ref[h], k)
                dka_ref[pl.ds(off, T), :] += (_dot_tn(dzs_ref[0], qh[0])
                                              + _dot_tn(dzs_ref[1], qh[1]))
                dva_ref[pl.ds(off, T), :] += (_dot_tn(abs_ref[0], doh[0])
                                              + _dot_tn(abs_ref[1], doh[1]))

            def tile(j, diag):
                off = pl.multiple_of(j * T, T)
                k = k_ref[pl.ds(off, T), :]
                v = v_ref[pl.ds(off, T), :]
                z = [_dot_nt(qh[h], k) for h in hs]
                da = [_dot_nt(doh[h], v) for h in hs]
                if not diag:
                    flush(j + 1)
                la = [jnp.minimum(z[h], 0.0) - jnp.log(1.0 + jnp.exp(-jnp.abs(z[h]))) for h in hs]
                lf = [la[h] - z[h] for h in hs]
                if diag:
                    lf = [jnp.where(strict, lf[h], 0.0) for h in hs]
                cin = [_cum2(lf[h], u2) for h in hs]
                a = [jnp.exp(la[h] + cin[h] + rl_ref[h]) for h in hs]
                if diag:
                    a = [jnp.where(strict, a[h], 0.0) for h in hs]
                ab = [a[h].astype(BF16) for h in hs]
                g = [ab[h].astype(F32) * da[h] for h in hs]
                gin = [_cum2(g[h], u2) for h in hs]
                dz = [g[h] - jnp.exp(la[h]) * ((dsum[h] - rg_ref[h]) - gin[h]) for h in hs]
                if diag:
                    dz = [jnp.where(strict, dz[h], 0.0) for h in hs]
                for h in hs:
                    rl_ref[h] += cin[h][:, 0:1] + lf[h][:, 0:1]
                    rg_ref[h] += gin[h][:, 0:1] + g[h][:, 0:1]
                    dzs_ref[h] = dz[h].astype(BF16)
                    abs_ref[h] = ab[h]

            tile(i, True)

            def more(state):
                j, top = state
                return jnp.logical_and(j >= 0, top > UNDERFLOW)

            def step(state):
                j, _ = state
                tile(j, False)
                return j - 1, _running_top(rl_ref)

            j_left, _ = lax.while_loop(more, step, (i - 1, _running_top(rl_ref)))
            flush(j_left + 1)
            dq_ref[rows, :] = (jnp.where(m0, dqa_ref[0], dqa_ref[1]) * Q_SCALE).astype(BF16)

        for n in range(QPS):
            query_tile(QPS * step_id + n, slice(n * T, (n + 1) * T))

        @pl.when(step_id == nq // QPS - 1)
        def _():
            dk_ref[...] = dka_ref[...].astype(BF16)
            dv_ref[...] = dva_ref[...].astype(BF16)

    nb = SB_W // LANES
    (dq, dk, dv), lands = _ride(dict(
        body=body, name=name, grid=(nb, nq // QPS),
        in_specs=[pl.BlockSpec((QPS * T, LANES), lambda p, i: (i, p)),
                  pl.BlockSpec((s, LANES), lambda p, i: (0, nb + p)),
                  pl.BlockSpec((s, LANES), lambda p, i: (0, 2 * nb + p)),
                  pl.BlockSpec((QPS * T, LANES), lambda p, i: (i, p)),
                  pl.BlockSpec((QPS * T, LANES), lambda p, i: (i, p))],
        out_specs=[pl.BlockSpec((QPS * T, LANES), lambda p, i: (i, p)),
                   pl.BlockSpec((s, LANES), lambda p, i: (0, p)),
                   pl.BlockSpec((s, LANES), lambda p, i: (0, p))],
        out_shape=[jax.ShapeDtypeStruct((s, SB_W), BF16)] * 3,
        scratch_shapes=[pltpu.VMEM((2, T, LANES), F32), pltpu.VMEM((s, LANES), F32),
                        pltpu.VMEM((s, LANES), F32), pltpu.VMEM((2, T, 1), F32),
                        pltpu.VMEM((2, T, 1), F32), pltpu.VMEM((2, T, T), BF16),
                        pltpu.VMEM((2, T, T), BF16)],
        compiler_params=_ARB2, operands=[qkv, qkv, qkv, o, do]), rider)
    return dq, dk, dv, lands


def _fox_bwd(qkv, fqb, frow, fbounds, o, lse, do, name, rider=None):
    s = qkv.shape[0]
    nq = s // T

    def body(q_ref, k_ref, v_ref, fq_ref, fr_ref, fb_ref, o_ref, lse_ref, do_ref,
             dq_ref, dk_ref, dv_ref, df_ref, dqa_ref, dka_ref, dva_ref, dfa_ref, dls_ref, pbs_ref):
        step_id = pl.program_id(1)

        @pl.when(step_id == 0)
        def _():
            dka_ref[...] = jnp.zeros_like(dka_ref)
            dva_ref[...] = jnp.zeros_like(dva_ref)
            dfa_ref[...] = jnp.zeros_like(dfa_ref)

        m0 = _pair_masks()
        causal = _iota2((T, T), 0) >= _iota2((T, T), 1)
        hs = range(2)

        def query_tile(i, rows):
            qh = _split_pair(q_ref[rows, :] * jnp.asarray(Q_SCALE, BF16), m0)
            doh = _split_pair(do_ref[rows, :], m0)
            dsum = _row_dots(do_ref[rows, :], o_ref[rows, :], m0)
            fq = fq_ref[rows, :]
            fqh = (fq[:, 0:1], fq[:, HEAD_DIM:HEAD_DIM + 1])
            lse = lse_ref[rows, :]
            lseh = (lse[:, 0:1], lse[:, HEAD_DIM:HEAD_DIM + 1])
            dqa_ref[...] = jnp.zeros_like(dqa_ref)

            def flush(j):
                off = pl.multiple_of(j * T, T)
                k = k_ref[pl.ds(off, T), :]
                for h in hs:
                    dqa_ref[h] += _dot(dls_ref[h], k)
                dka_ref[pl.ds(off, T), :] += (_dot_tn(dls_ref[0], qh[0])
                                              + _dot_tn(dls_ref[1], qh[1]))
                dva_ref[pl.ds(off, T), :] += (_dot_tn(pbs_ref[0], doh[0])
                                              + _dot_tn(pbs_ref[1], doh[1]))

            def tile(j, diag):
                off = pl.multiple_of(j * T, T)
                k = k_ref[pl.ds(off, T), :]
                v = v_ref[pl.ds(off, T), :]
                sc = [_dot_nt(qh[h], k) + fqh[h] - fr_ref[h:h + 1, pl.ds(off, T)] for h in hs]
                dp = [_dot_nt(doh[h], v) for h in hs]
                if not diag:
                    flush(j + 1)
                p = [jnp.exp(sc[h] - lseh[h]) for h in hs]
                if diag:
                    p = [jnp.where(causal, p[h], 0.0) for h in hs]
                dl = [p[h] * (dp[h] - dsum[h]) for h in hs]
                for h in hs:
                    dls_ref[h] = dl[h].astype(BF16)
                    pbs_ref[h] = p[h].astype(BF16)
                    dfa_ref[h:h + 1, pl.ds(off, T)] -= jnp.sum(dl[h], axis=0, keepdims=True)

            tile(i, True)
            j_left = _fox_tiles_left(i, pl.program_id(0), nq, fb_ref, tile)
            flush(j_left + 1)
            dq_ref[rows, :] = (jnp.where(m0, dqa_ref[0], dqa_ref[1]) * Q_SCALE).astype(BF16)

        for n in range(QPS):
            query_tile(QPS * step_id + n, slice(n * T, (n + 1) * T))

        @pl.when(step_id == nq // QPS - 1)
        def _():
            dk_ref[...] = dka_ref[...].astype(BF16)
            dv_ref[...] = dva_ref[...].astype(BF16)
            df_ref[...] = dfa_ref[...]

    nb = FX_W // LANES
    base = 3 * SB_W // LANES
    (dq, dk, dv, df), lands = _ride(dict(
        body=body, name=name, grid=(nb, nq // QPS),
        in_specs=[pl.BlockSpec((QPS * T, LANES), lambda p, i: (i, base + p)),
                  pl.BlockSpec((s, LANES), lambda p, i: (0, base + nb + p)),
                  pl.BlockSpec((s, LANES), lambda p, i: (0, base + 2 * nb + p)),
                  pl.BlockSpec((QPS * T, LANES), lambda p, i: (i, p)),
                  pl.BlockSpec((None, 2, s), lambda p, i: (p, 0, 0)),
                  pl.BlockSpec(memory_space=pltpu.SMEM),
                  pl.BlockSpec((QPS * T, LANES), lambda p, i: (i, p)),
                  pl.BlockSpec((QPS * T, LANES), lambda p, i: (i, p)),
                  pl.BlockSpec((QPS * T, LANES), lambda p, i: (i, p))],
        out_specs=[pl.BlockSpec((QPS * T, LANES), lambda p, i: (i, p)),
                   pl.BlockSpec((s, LANES), lambda p, i: (0, p)),
                   pl.BlockSpec((s, LANES), lambda p, i: (0, p)),
                   pl.BlockSpec((None, 2, s), lambda p, i: (p, 0, 0))],
        out_shape=[jax.ShapeDtypeStruct((s, FX_W), BF16)] * 3
        + [jax.ShapeDtypeStruct((nb, 2, s), F32)],
        scratch_shapes=[pltpu.VMEM((2, T, LANES), F32), pltpu.VMEM((s, LANES), F32),
                        pltpu.VMEM((s, LANES), F32), pltpu.VMEM((2, s), F32),
                        pltpu.VMEM((2, T, T), BF16), pltpu.VMEM((2, T, T), BF16)],
        compiler_params=_ARB2, operands=[qkv, qkv, qkv, fqb, frow, fbounds, o, lse, do]), rider)
    return dq, dk, dv, df, lands


def _fox_prep_bwd(dfrow, gf, bpad, name):
    s = gf.shape[0]

    def body(df_ref, fl_ref, b_ref, dfl_ref, db_ref):
        tri = jnp.where(_iota2((T, T), 0) <= _iota2((T, T), 1), 1.0, 0.0).astype(BF16)
        carry = jnp.zeros((1, LANES), F32)
        db = jnp.zeros((1, LANES), F32)
        fill = jnp.zeros((LANES - FOX_HEADS, T), F32)
        for blk in reversed(range(s // T)):
            rows = slice(blk * T, (blk + 1) * T)
            c = _tri3(tri, jnp.concatenate([df_ref[:, rows], fill], axis=0), _dot_nt) + carry
            carry = c[0:1, :]
            dfl = c / (1.0 + jnp.exp(fl_ref[rows, :] + b_ref[...]))
            dfl_ref[rows, :] = dfl.astype(BF16)
            db = db + jnp.sum(dfl, axis=0, keepdims=True)
        db_ref[...] = db

    return pl.pallas_call(
        body, name=name, grid=(1,),
        in_specs=[pl.BlockSpec((FOX_HEADS, s), lambda i: (0, 0)),
                  pl.BlockSpec((s, LANES), lambda i: (0, MIX_W // LANES)),
                  pl.BlockSpec((1, LANES), lambda i: (0, 0))],
        out_specs=[pl.BlockSpec((s, LANES), lambda i: (0, 0)),
                   pl.BlockSpec((1, LANES), lambda i: (0, 0))],
        out_shape=[jax.ShapeDtypeStruct((s, LANES), BF16), jax.ShapeDtypeStruct((1, LANES), F32)],
        compiler_params=_ARB1,
    )(dfrow, gf, bpad)


def _mem_bwd(qkv, kv, o, lse, do, name):
    s = qkv.shape[0]
    n = kv.shape[0]

    def body(q_ref, k_ref, v_ref, o_ref, lse_ref, do_ref, dq_ref, dk_ref, dv_ref):
        @pl.when(pl.program_id(1) == 0)
        def _():
            dk_ref[...] = jnp.zeros_like(dk_ref)
            dv_ref[...] = jnp.zeros_like(dv_ref)

        m0 = _pair_masks()
        qh = _split_pair(q_ref[...] * jnp.asarray(Q_SCALE, BF16), m0)
        doh = _split_pair(do_ref[...], m0)
        dsum = _row_dots(do_ref[...], o_ref[...], m0)
        lse = lse_ref[...]
        lseh = (lse[:, 0:1], lse[:, HEAD_DIM:HEAD_DIM + 1])
        k = k_ref[...]
        v = v_ref[...]
        dqs = []
        for h in range(2):
            p = jnp.exp(_dot_nt(qh[h], k) - lseh[h])
            dl = p * (_dot_nt(doh[h], v) - dsum[h])
            dlb = dl.astype(BF16)
            dqs.append(_dot(dlb, k))
            dk_ref[...] += _dot_tn(dlb, qh[h])
            dv_ref[...] += _dot_tn(p.astype(BF16), doh[h])
        dq_ref[...] = (jnp.where(m0, dqs[0], dqs[1]) * Q_SCALE).astype(BF16)

    nb = MEM_W // LANES
    base = (3 * SB_W + 3 * FX_W) // LANES
    return pl.pallas_call(
        body, name=name, grid=(nb, s // TQM),
        in_specs=[pl.BlockSpec((TQM, LANES), lambda p, i: (i, base + p)),
                  pl.BlockSpec((n, LANES), lambda p, i: (0, p)),
                  pl.BlockSpec((n, LANES), lambda p, i: (0, nb + p)),
                  pl.BlockSpec((TQM, LANES), lambda p, i: (i, p)),
                  pl.BlockSpec((TQM, LANES), lambda p, i: (i, p)),
                  pl.BlockSpec((TQM, LANES), lambda p, i: (i, p))],
        out_specs=[pl.BlockSpec((TQM, LANES), lambda p, i: (i, p)),
                   pl.BlockSpec((n, LANES), lambda p, i: (0, p)),
                   pl.BlockSpec((n, LANES), lambda p, i: (0, p))],
        out_shape=[jax.ShapeDtypeStruct((s, MEM_W), BF16), jax.ShapeDtypeStruct((n, MEM_W), F32),
                   jax.ShapeDtypeStruct((n, MEM_W), F32)],
        compiler_params=_ARB2,
    )(qkv, kv, kv, o, lse, do)


def _memkv_bwd(mem, mnw, wkv, dk, dv, name):
    n = mem.shape[0]

    def body(mem_ref, mnw_ref, w_ref, dk_ref, dv_ref, dw_ref, dmnw_ref):
        mv = mem_ref[...]
        r = lax.rsqrt(jnp.mean(mv * mv, axis=-1, keepdims=True) + EPS)
        mh = mv * r
        hm = (mh * mnw_ref[...]).astype(BF16)
        dkv = jnp.concatenate([dk_ref[...], dv_ref[...]], axis=1).astype(BF16)
        dw_ref[...] = _dot_tn(hm, dkv)
        dhm = _dot_nt(dkv, w_ref[...])
        dmnw_ref[...] = jnp.sum(dhm * mh, axis=0, keepdims=True)

    return pl.pallas_call(
        body, name=name, grid=(1,),
        in_specs=[pl.BlockSpec((n, D_MODEL), lambda i: (0, 0)),
                  pl.BlockSpec((1, D_MODEL), lambda i: (0, 0)),
                  pl.BlockSpec((D_MODEL, 2 * MEM_W), lambda i: (0, 0)),
                  pl.BlockSpec((n, MEM_W), lambda i: (0, 0)),
                  pl.BlockSpec((n, MEM_W), lambda i: (0, 0))],
        out_specs=[pl.BlockSpec((D_MODEL, 2 * MEM_W), lambda i: (0, 0)),
                   pl.BlockSpec((1, D_MODEL), lambda i: (0, 0))],
        out_shape=[jax.ShapeDtypeStruct((D_MODEL, 2 * MEM_W), F32),
                   jax.ShapeDtypeStruct((1, D_MODEL), F32)],
        compiler_params=_ARB1,
    )(mem, mnw, wkv, dk, dv)


def _inproj_bwd_dx(pieces, w_r, x, nw, dxo, name):
    s = x.shape[0]
    n = len(pieces)
    widths = [p.shape[1] for p in pieces]

    def body(*refs):
        piece_refs = refs[:n]
        w_ref, x_ref, nw_ref, dxo_ref, dx_ref, h_ref, dnw_ref, dp_ref = refs[n:]

        @pl.when(pl.program_id(0) == 0)
        def _():
            dnw_ref[...] = jnp.zeros_like(dnw_ref)

        col = 0
        for r, wd in zip(piece_refs, widths):
            dp_ref[:, col:col + wd] = r[...]
            col += wd
        dp_ref[:, col:] = jnp.zeros((TM, WR_W - col), BF16)
        dh = _dot(dp_ref[...], w_ref[...])
        xv = x_ref[...]
        nw = nw_ref[...]
        r = lax.rsqrt(jnp.mean(xv * xv, axis=-1, keepdims=True) + EPS)
        xh = xv * r
        h_ref[...] = (xh * nw).astype(BF16)
        dnw_ref[...] += jnp.sum(dh * xh, axis=0, keepdims=True)
        dxh = dh * nw
        dx_ref[...] = r * (dxh - xh * jnp.mean(dxh * xh, axis=-1, keepdims=True)) + dxo_ref[...]

    return pl.pallas_call(
        body, name=name, grid=(s // TM,),
        in_specs=[pl.BlockSpec((TM, wd), lambda i: (i, 0)) for wd in widths]
        + [pl.BlockSpec((WR_W, D_MODEL), lambda i: (0, 0)),
           pl.BlockSpec((TM, D_MODEL), lambda i: (i, 0)),
           pl.BlockSpec((1, D_MODEL), lambda i: (0, 0)),
           pl.BlockSpec((TM, D_MODEL), lambda i: (i, 0))],
        out_specs=[pl.BlockSpec((TM, D_MODEL), lambda i: (i, 0)),
                   pl.BlockSpec((TM, D_MODEL), lambda i: (i, 0)),
                   pl.BlockSpec((1, D_MODEL), lambda i: (0, 0)),
                   pl.BlockSpec((TM, WR_W), lambda i: (i, 0))],
        out_shape=[jax.ShapeDtypeStruct((s, D_MODEL), F32), jax.ShapeDtypeStruct((s, D_MODEL), BF16),
                   jax.ShapeDtypeStruct((1, D_MODEL), F32), jax.ShapeDtypeStruct((s, WR_W), BF16)],
        compiler_params=_ARB1,
    )(*pieces, w_r, x, nw, dxo)


def _inproj_bwd_dw(h, dproj, name):
    s = dproj.shape[0]
    tn = 256

    def body(h_ref, dp_ref, dw_ref):
        dw_ref[...] = _dot_tn(dp_ref[...], h_ref[...])

    return pl.pallas_call(
        body, name=name, grid=(WR_W // tn,),
        in_specs=[pl.BlockSpec((s, D_MODEL), lambda j: (0, 0)),
                  pl.BlockSpec((s, tn), lambda j: (0, j))],
        out_specs=pl.BlockSpec((tn, D_MODEL), lambda j: (j, 0)),
        out_shape=jax.ShapeDtypeStruct((WR_W, D_MODEL), F32),
        compiler_params=_ARB1,
    )(h, dproj)


def _rearrange_w_in(wt):
    pad = jnp.zeros((FL_PAD - FOX_HEADS,) + wt.shape[1:], wt.dtype)
    return jnp.concatenate([wt[:3072], wt[3080:3336], wt[3336:IN_W], wt[3072:3080], pad], axis=0)


def _restore_w_in(g):
    gate0 = QKV_W
    fl0 = QKV_W + MIX_W
    return jnp.concatenate(
        [g[:3072], g[fl0:fl0 + FOX_HEADS], g[3072:QKV_W], g[gate0:fl0]], axis=0)


def _pad_lanes(v, width=LANES):
    return jnp.pad(v, (0, width - v.shape[0])).reshape(1, width)


def _layer_fwd(xs, mem, nw, w_r, b_forget, mnw, late, onw, l, travel=None):
    s = xs.shape[0]
    bpad = _pad_lanes(b_forget)
    qkv, gf = _inproj_fwd(xs, nw, w_r, f"inproj_fwd_{l}")
    fqb, frow, fbounds = _fox_prep_fwd(gf, qkv, bpad, f"fox_prep_fwd_{l}")
    frow = frow.reshape(FOX_HEADS // 2, 2, s)
    travel = _Travel(travel)
    ysb, _ = travel.ride(0, _sb_fwd, qkv, f"sb_fwd_{l}")
    yfx, lse_fx, _ = travel.ride(1, _fox_fwd, qkv, fqb, frow, fbounds, f"fox_fwd_{l}")
    wkv, wout = late(travel.lands)
    kv = _memkv_fwd(mem, mnw, wkv, f"memkv_fwd_{l}")
    ym, lse_m = _mem_fwd(qkv, kv, f"mem_fwd_{l}")
    xn = _outproj_fwd(ysb, yfx, ym, gf, onw, wout, xs, f"outproj_fwd_{l}")
    saved = (xs, nw, mnw, onw, bpad, qkv, gf, fqb, frow, fbounds, ysb, yfx, lse_fx, kv, ym, lse_m)
    return xn, saved, travel.lands, (wkv, wout)


class _Travel:
    def __init__(self, plan):
        self.plan = plan
        self.lands = None if plan is None else _new_lands(plan[0], plan[1])

    def ride(self, n, fn, *args):
        if self.plan is None or self.plan[2][n] is None:
            return fn(*args)
        srcs, scatter, legs = self.plan
        idx, rows = legs[n]
        out = fn(*args, rider=_Rider([srcs[a] for a in idx], [self.lands[a] for a in idx],
                                     scatter, rows))
        for a, land in zip(idx, out[-1]):
            self.lands[a] = land
        return out


def _layer_bwd(dx, saved, mem, w_r, wkv, wout, l, travel=None):
    xs, nw, mnw, onw, bpad, qkv, gf, fqb, frow, fbounds, ysb, yfx, lse_fx, kv, ym, lse_m = saved
    s = xs.shape[0]
    dysb, dyfx, dym, dgate, dwout, donw = _outproj_bwd(
        dx, wout, ysb, yfx, ym, gf, onw, f"outproj_bwd_{l}")
    travel = _Travel(None if travel is None else travel(dwout))
    sdq, sdk, sdv, _ = travel.ride(0, _sb_bwd, qkv, ysb, dysb, f"sb_bwd_{l}")
    fdq, fdk, fdv, dfrow, _ = travel.ride(1, _fox_bwd, qkv, fqb, frow, fbounds, yfx, lse_fx, dyfx,
                                          f"fox_bwd_{l}")
    dfl, db = _fox_prep_bwd(dfrow.reshape(FOX_HEADS, s), gf, bpad, f"fox_prep_bwd_{l}")
    dmq, dmk, dmv = _mem_bwd(qkv, kv, ym, lse_m, dym, f"mem_bwd_{l}")
    dwkv, dmnw = _memkv_bwd(mem, mnw, wkv, dmk, dmv, f"memkv_bwd_{l}")
    dx, ht, dnw, dproj = _inproj_bwd_dx([sdq, sdk, sdv, fdq, fdk, fdv, dmq, dgate, dfl],
                                        w_r, xs, nw, dx, f"inproj_bwd_dx_{l}")
    dwr = _inproj_bwd_dw(ht, dproj, f"inproj_bwd_dw_{l}")
    grads = dict(norm_w=dnw[0], w_r=dwr, b_forget=db[0, :FOX_HEADS], mem_norm_w=dmnw[0],
                 w_mem_kv=dwkv, out_norm_w=donw[0], w_out=dwout)
    return dx, grads, travel.lands


_ANY = pl.BlockSpec(memory_space=pl.ANY)


def _my_place():
    return lax.axis_index("x"), lax.axis_index("y"), lax.axis_index("c")


def _flip(v, bit):
    return 1 - v if bit else v


def _block_index(px, py, pc):
    return 4 * px + 2 * py + pc


def _all_gather_weights(shards, name):
    n = len(shards)

    def body(*refs):
        ins, outs = refs[:n], refs[n:2 * n]
        send_sems, recv_sems, local_sems = refs[2 * n:]
        x, y, c = _my_place()
        me = (x, y, c)
        sibling = (x, y, 1 - c)
        chips = [(1 - x, y), (x, 1 - y), (1 - x, 1 - y)]

        def copy(a, k, block, to, src=None):
            dst = outs[a].at[_block_index(*block)]
            return pltpu.make_async_remote_copy(
                src_ref=dst if src is None else src, dst_ref=dst,
                send_sem=send_sems.at[a, k], recv_sem=recv_sems.at[a, k],
                device_id=to, device_id_type=pl.DeviceIdType.MESH)

        mine = [pltpu.make_async_copy(ins[a], outs[a].at[_block_index(*me)], local_sems.at[a])
                for a in range(n)]
        for cp in mine:
            cp.start()
        first = []
        for a in range(n):
            first.append(copy(a, 0, me, sibling, src=ins[a]))
            first += [copy(a, 1 + j, me, (*chip, c), src=ins[a]) for j, chip in enumerate(chips)]
        for cp in first:
            cp.start()
        passed = []
        for j, chip in enumerate(chips):
            for a in range(n):
                copy(a, 1 + j, (*chip, c), me).wait_recv()
                fwd = copy(a, 4 + j, (*chip, c), sibling)
                fwd.start()
                passed.append(fwd)
        for a in range(n):
            copy(a, 0, sibling, me).wait_recv()
            for j, chip in enumerate(chips):
                copy(a, 4 + j, (*chip, 1 - c), me).wait_recv()
        for cp in first + passed:
            cp.wait_send()
        for cp in mine:
            cp.wait()

    return pl.pallas_call(
        body, name=name,
        in_specs=[_ANY] * n, out_specs=[_ANY] * n,
        out_shape=[jax.ShapeDtypeStruct((N_DEV,) + v.shape, v.dtype) for v in shards],
        scratch_shapes=[pltpu.SemaphoreType.DMA((n, 7)), pltpu.SemaphoreType.DMA((n, 7)),
                        pltpu.SemaphoreType.DMA((n,))],
    )(*shards)


def _exchange_blocks(blocked, name):
    n = len(blocked)

    def body(*refs):
        ins, outs = refs[:n], refs[n:2 * n]
        send_sems, recv_sems, local_sems = refs[2 * n:]
        x, y, c = _my_place()
        mine_idx = _block_index(x, y, c)
        local = [pltpu.make_async_copy(ins[a].at[mine_idx], outs[a].at[mine_idx], local_sems.at[a])
                 for a in range(n)]
        for cp in local:
            cp.start()
        sends, arrivals = [], []
        for r in range(1, N_DEV):
            peer = (_flip(x, r & 4), _flip(y, r & 2), _flip(c, r & 1))
            peer_idx = _block_index(*peer)
            for a in range(n):
                sems = dict(send_sem=send_sems.at[a, r - 1], recv_sem=recv_sems.at[a, r - 1],
                            device_id=peer, device_id_type=pl.DeviceIdType.MESH)
                sends.append(pltpu.make_async_remote_copy(
                    src_ref=ins[a].at[peer_idx], dst_ref=outs[a].at[mine_idx], **sems))
                arrivals.append(pltpu.make_async_remote_copy(
                    src_ref=ins[a].at[peer_idx], dst_ref=outs[a].at[peer_idx], **sems))
        for cp in sends:
            cp.start()
        for cp in arrivals:
            cp.wait_recv()
        for cp in sends:
            cp.wait_send()
        for cp in local:
            cp.wait()

    return pl.pallas_call(
        body, name=name,
        in_specs=[_ANY] * n, out_specs=[_ANY] * n,
        out_shape=[jax.ShapeDtypeStruct(v.shape, v.dtype) for v in blocked],
        scratch_shapes=[pltpu.SemaphoreType.DMA((n, 7)), pltpu.SemaphoreType.DMA((n, 7)),
                        pltpu.SemaphoreType.DMA((n,))],
    )(*blocked)


N_CHIP = N_DEV // 2


def _pair_swap(blocked, name):
    n = len(blocked)

    def body(*refs):
        ins, outs = refs[:n], refs[n:2 * n]
        send_sems, recv_sems = refs[2 * n:]
        x, y, c = _my_place()
        copies = [pltpu.make_async_remote_copy(
            src_ref=ins[a].at[j, 1 - c], dst_ref=outs[a].at[j],
            send_sem=send_sems.at[N_CHIP * a + j], recv_sem=recv_sems.at[N_CHIP * a + j],
            device_id=(x, y, 1 - c), device_id_type=pl.DeviceIdType.MESH)
            for a in range(n) for j in range(N_CHIP)]
        for cp in copies:
            cp.start()
        for cp in copies:
            cp.wait_recv()
        for cp in copies:
            cp.wait_send()

    return pl.pallas_call(
        body, name=name,
        in_specs=[_ANY] * n, out_specs=[_ANY] * n,
        out_shape=[jax.ShapeDtypeStruct((N_CHIP,) + v.shape[2:], v.dtype) for v in blocked],
        scratch_shapes=[pltpu.SemaphoreType.DMA((N_CHIP * n,)),
                        pltpu.SemaphoreType.DMA((N_CHIP * n,))],
    )(*blocked)


def _pair_add(mine, theirs, name):
    _, nrow, ncol = mine.shape

    def body(a_ref, b_ref, o_ref):
        o_ref[...] = (a_ref[...].astype(F32) + b_ref[...].astype(F32)).astype(BF16)

    blk = pl.BlockSpec((None, nrow, ncol), lambda j: (j, 0, 0))
    return pl.pallas_call(
        body, name=name, grid=(N_CHIP,), in_specs=[blk, blk], out_specs=blk,
        out_shape=jax.ShapeDtypeStruct(mine.shape, BF16), compiler_params=_ARB1,
    )(mine, theirs)


def _chip_exchange(by_chip, to_all, name):
    n, m = len(by_chip), len(to_all)

    def body(*refs):
        ins, alls = refs[:n], refs[n:n + m]
        outs, all_outs = refs[n + m:2 * n + m], refs[2 * n + m:2 * (n + m)]
        send_sems, recv_sems, local_sems = refs[2 * (n + m):]
        x, y, c = _my_place()
        my_chip, mine_idx = 2 * x + y, _block_index(x, y, c)
        local = [pltpu.make_async_copy(ins[a].at[my_chip], outs[a].at[my_chip], local_sems.at[a])
                 for a in range(n)]
        local += [pltpu.make_async_copy(alls[b].at[mine_idx], all_outs[b].at[mine_idx],
                                        local_sems.at[n + b]) for b in range(m)]
        for cp in local:
            cp.start()
        sends, arrivals = [], []
        k = 0
        for r in range(1, N_DEV):
            peer = (_flip(x, r & 4), _flip(y, r & 2), _flip(c, r & 1))
            peer_chip, peer_idx = 2 * peer[0] + peer[1], _block_index(*peer)
            pairs = [(alls[b].at[mine_idx], all_outs[b].at[mine_idx], all_outs[b].at[peer_idx])
                     for b in range(m)]
            if not r & 1:
                pairs += [(ins[a].at[peer_chip], outs[a].at[my_chip], outs[a].at[peer_chip])
                          for a in range(n)]
            for src, there, here in pairs:
                sems = dict(send_sem=send_sems.at[k], recv_sem=recv_sems.at[k], device_id=peer,
                            device_id_type=pl.DeviceIdType.MESH)
                sends.append(pltpu.make_async_remote_copy(src_ref=src, dst_ref=there, **sems))
                arrivals.append(pltpu.make_async_remote_copy(src_ref=src, dst_ref=here, **sems))
                k += 1
        for cp in sends:
            cp.start()
        for cp in arrivals:
            cp.wait_recv()
        for cp in sends:
            cp.wait_send()
        for cp in local:
            cp.wait()

    n_copies = 7 * m + 3 * n
    return pl.pallas_call(
        body, name=name,
        in_specs=[_ANY] * (n + m), out_specs=[_ANY] * (n + m),
        out_shape=[jax.ShapeDtypeStruct(v.shape, v.dtype) for v in by_chip + to_all],
        scratch_shapes=[pltpu.SemaphoreType.DMA((n_copies,)), pltpu.SemaphoreType.DMA((n_copies,)),
                        pltpu.SemaphoreType.DMA((n + m,))],
    )(*by_chip, *to_all)


class _Rider(NamedTuple):
    srcs: list
    lands: list
    scatter: bool
    part: list


def _new_lands(srcs, scatter):
    return [lax.empty(v.shape if scatter else (N_DEV,) + v.shape, v.dtype) for v in srcs]


def _rider_copies(srcs, lands, send_sems, recv_sems, rider):
    x, y, c = _my_place()
    mine_idx = _block_index(x, y, c)

    def window(ref, a):
        if rider.part[a] is None:
            return ref
        dim, start, size = rider.part[a]
        return ref.at[(slice(None),) * dim + (pl.ds(start, size),)]

    sends, arrivals = [], []
    for r in range(1, N_DEV):
        peer = (_flip(x, r & 4), _flip(y, r & 2), _flip(c, r & 1))
        peer_idx = _block_index(*peer)
        for a in range(len(srcs)):
            src = window(srcs[a].at[peer_idx] if rider.scatter else srcs[a], a)
            k = 7 * a + r - 1
            sems = dict(send_sem=send_sems.at[k], recv_sem=recv_sems.at[k],
                        device_id=peer, device_id_type=pl.DeviceIdType.MESH)
            sends.append(pltpu.make_async_remote_copy(
                src_ref=src, dst_ref=window(lands[a].at[mine_idx], a), **sems))
            arrivals.append(pltpu.make_async_remote_copy(
                src_ref=src, dst_ref=window(lands[a].at[peer_idx], a), **sems))
    return sends, arrivals


def _ride(call, rider):
    call = dict(call)
    body, grid = call.pop("body"), call["grid"]
    operands = call.pop("operands")
    if rider is None:
        return list(pl.pallas_call(body, **call)(*operands)), None
    n_in, n_out = len(call["in_specs"]), len(call["out_specs"])
    n_scratch = len(call["scratch_shapes"])
    m = len(rider.srcs)

    def riding(*refs):
        main_in, srcs, lands = refs[:n_in], refs[n_in:n_in + m], refs[n_in + m:n_in + 2 * m]
        main_out = refs[n_in + 2 * m:n_in + 2 * m + n_out]
        rest = refs[n_in + 3 * m + n_out:]
        send_sems, recv_sems = rest[n_scratch:]
        at = [pl.program_id(d) for d in range(len(grid))]
        first = functools.reduce(jnp.logical_and, [p == 0 for p in at])
        last = functools.reduce(jnp.logical_and, [p == g - 1 for p, g in zip(at, grid)])
        sends, arrivals = _rider_copies(srcs, lands, send_sems, recv_sems, rider)

        @pl.when(first)
        def _():
            for cp in sends:
                cp.start()

        body(*main_in, *main_out, *rest[:n_scratch])

        @pl.when(last)
        def _():
            for cp in arrivals:
                cp.wait_recv()
            for cp in sends:
                cp.wait_send()

    call["in_specs"] = list(call["in_specs"]) + [_ANY] * (2 * m)
    call["out_specs"] = list(call["out_specs"]) + [_ANY] * m
    call["out_shape"] = list(call["out_shape"]) + [
        jax.ShapeDtypeStruct(v.shape, v.dtype) for v in rider.lands]
    call["scratch_shapes"] = list(call["scratch_shapes"]) + [
        pltpu.SemaphoreType.DMA((7 * m,)), pltpu.SemaphoreType.DMA((7 * m,))]
    call["input_output_aliases"] = {n_in + m + a: n_out + a for a in range(m)}
    outs = pl.pallas_call(riding, **call)(*operands, *rider.srcs, *rider.lands)
    return list(outs[:n_out]), list(outs[n_out:])


def _sum_parts(p_ref):
    g = p_ref[0].astype(F32)
    for k in range(1, p_ref.shape[0]):
        g = g + p_ref[k].astype(F32)
    return g


def _adamw(g, w, m, v):
    c1 = 1.0 / (1.0 - ADAM_B1 ** ADAM_STEP)
    c2 = 1.0 / (1.0 - ADAM_B2 ** ADAM_STEP)
    nm = ADAM_B1 * m + (1.0 - ADAM_B1) * g
    nv = ADAM_B2 * v + (1.0 - ADAM_B2) * (g * g)
    return nm, nv, -ADAM_LR * ((nm * c1) / (jnp.sqrt(nv * c2) + ADAM_EPS) + ADAM_WD * w)


def _adamw_w_in(parts, w, m, v, name):
    ncol_blk, depth, nfeat = w.shape
    cols = 256

    def body(*refs):
        p_refs = refs[:depth]
        w_ref, m_ref, v_ref, g_ref, d_ref, nm_ref, nv_ref = refs[depth:]
        for l in range(depth):
            g = _sum_parts(p_refs[l])
            nm, nv, d = _adamw(g, w_ref[:, l, :], m_ref[:, l, :], v_ref[:, l, :])
            g_ref[:, l, :] = g
            nm_ref[:, l, :] = nm
            nv_ref[:, l, :] = nv
            d_ref[:, l, :] = d

    blk = pl.BlockSpec((ncol_blk, depth, cols), lambda j: (0, 0, j))
    return pl.pallas_call(
        body, name=name, grid=(nfeat // cols,),
        in_specs=[pl.BlockSpec((p.shape[0], ncol_blk, cols), lambda j: (0, 0, j)) for p in parts]
        + [blk] * 3,
        out_specs=[blk] * 4,
        out_shape=[jax.ShapeDtypeStruct(w.shape, F32)] * 4,
        compiler_params=_ARB1,
    )(*parts, w, m, v)


def _adamw_sum(parts, w, m, v, tile, name):
    depth, nrow, ncol = w.shape
    rows, cols = tile

    def body(*refs):
        p_refs = refs[:depth]
        w_ref, m_ref, v_ref, g_ref, d_ref, nm_ref, nv_ref = refs[depth:]
        layer = pl.program_id(0)
        for l in range(depth):
            @pl.when(layer == l)
            def _(p_ref=p_refs[l]):
                g = _sum_parts(p_ref)
                nm, nv, d = _adamw(g, w_ref[...], m_ref[...], v_ref[...])
                g_ref[...] = g
                nm_ref[...] = nm
                nv_ref[...] = nv
                d_ref[...] = d

    def part_spec(l):
        return pl.BlockSpec((parts[l].shape[0], rows, cols), lambda q, i, j: (
            0, jnp.where(q == l, i, 0), jnp.where(q == l, j, 0)))

    blk = pl.BlockSpec((None, rows, cols), lambda q, i, j: (q, i, j))
    return pl.pallas_call(
        body, name=name, grid=(depth, nrow // rows, ncol // cols),
        in_specs=[part_spec(l) for l in range(depth)] + [blk, blk, blk],
        out_specs=[blk] * 4,
        out_shape=[jax.ShapeDtypeStruct(w.shape, F32)] * 4,
        compiler_params=pltpu.CompilerParams(
            dimension_semantics=("arbitrary", "arbitrary", "arbitrary")),
    )(*parts, w, m, v)


def _pack_small(norm_w, mem_norm_w, out_norm_w, final_norm_w, b_forget):
    onw = jnp.pad(out_norm_w.reshape(20, LANES), ((0, 4), (0, 0)))
    b = jnp.pad(b_forget, ((0, 6), (0, LANES - FOX_HEADS)))
    return jnp.concatenate([norm_w.reshape(16, LANES), mem_norm_w.reshape(16, LANES), onw,
                            final_norm_w.reshape(8, LANES), b], axis=0)


def _unpack_small(p):
    return (p[0:16].reshape(2, D_MODEL), p[16:32].reshape(2, D_MODEL), p[32:52].reshape(2, MIX_W),
            p[56:64].reshape(D_MODEL), p[64:66, :FOX_HEADS])


def kernel(x, mem, norm_w, w_in, b_forget, mem_norm_w, w_mem_kv, out_norm_w, w_out, final_norm_w, loss_target, m_norm_w, m_w_in, m_b_forget, m_mem_norm_w, m_w_mem_kv, m_out_norm_w, m_w_out, m_final_norm_w, v_norm_w, v_w_in, v_b_forget, v_mem_norm_w, v_w_mem_kv, v_out_norm_w, v_w_out, v_final_norm_w):
    kv_rows = w_mem_kv.shape[1]
    out_rows = w_out.shape[1]
    me = _block_index(*_my_place())

    def shards(l):
        return [w_in[l].T.astype(BF16), w_mem_kv[l].astype(BF16), w_out[l].astype(BF16)]

    def full_in(g_in):
        return _rearrange_w_in(g_in.reshape(IN_W, D_MODEL))

    def full_kv_out(g_kv, g_out):
        return g_kv.reshape(D_MODEL, 2 * MEM_W), g_out.reshape(MIX_W, D_MODEL)

    def in_blocks(g):
        segments = [(0, 3072, 0), (3072, 3080, QKV_W + MIX_W), (3080, 3336, 3072),
                    (3336, IN_W, QKV_W)]

        def block(k):
            lo, hi = k * SHARD_W, (k + 1) * SHARD_W
            pieces = [g[at + max(lo, a) - a:at + min(hi, b) - a]
                      for a, b, at in segments if max(lo, a) < min(hi, b)]
            return jnp.concatenate(pieces, axis=0).astype(BF16)

        return jnp.stack([block(k) for k in range(N_DEV)])

    def kv_blocks(g):
        return g.reshape(N_DEV, kv_rows, 2 * MEM_W).astype(BF16)

    def out_blocks(g):
        return g.reshape(N_DEV, out_rows, D_MODEL).astype(BF16)

    def with_own(land, own):
        return lax.dynamic_update_slice(land, own[None], (me,) + (0,) * own.ndim)

    def with_own_of(land, blocked):
        return lax.dynamic_update_slice(land, lax.dynamic_slice_in_dim(blocked, me, 1, axis=0),
                                        (me,) + (0,) * (land.ndim - 1))

    def row(v):
        return v.reshape(1, -1)

    def cols(first, size):
        return (1, first, size)

    fwd_split, bwd_split = 5 * LANES, 6 * LANES

    g_in0, g_kv0, g_out0 = _all_gather_weights(shards(0), "all_gather_l0")
    w_r0 = full_in(g_in0)
    s_in1, s_kv1, s_out1 = shards(1)
    x1, saved0, (l_in1,), (wkv0, wout0) = _layer_fwd(
        x[0], mem[0], row(norm_w[0]), w_r0, b_forget[0], row(mem_norm_w[0]),
        lambda lands: full_kv_out(g_kv0, g_out0), row(out_norm_w[0]), 0,
        travel=([s_in1], False, [([0], [cols(0, fwd_split)]),
                                 ([0], [cols(fwd_split, D_MODEL - fwd_split)])]))
    w_r1 = full_in(with_own(l_in1, s_in1))
    x2, saved1, _, (wkv1, wout1) = _layer_fwd(
        x1, mem[0], row(norm_w[1]), w_r1, b_forget[1], row(mem_norm_w[1]),
        lambda lands: full_kv_out(with_own(lands[0], s_kv1), with_own(lands[1], s_out1)),
        row(out_norm_w[1]), 1, travel=([s_kv1, s_out1], False, [([0, 1], [None, None]), None]))

    dx2, loss_part, dfnw = _final_fwd_bwd(x2, row(final_norm_w), loss_target[0], "final_fwd_bwd")

    dx1, gr1, (l_out1,) = _layer_bwd(
        dx2, saved1, mem[0], w_r1, wkv1, wout1, 1,
        travel=lambda dwout: ([out_blocks(dwout)], True, [([0], [None]), None]))
    p_in1, p_kv1 = in_blocks(gr1["w_r"]), kv_blocks(gr1["w_mem_kv"])
    grad_x, gr0, (l_in1, l_kv1, l_out0) = _layer_bwd(
        dx1, saved0, mem[0], w_r0, wkv0, wout0, 0,
        travel=lambda dwout: ([p_in1, p_kv1, out_blocks(dwout)], True,
                              [([0, 1], [cols(0, bwd_split), None]),
                               ([0, 2], [cols(bwd_split, D_MODEL - bwd_split), None])]))
    r_out1 = with_own_of(l_out1, out_blocks(gr1["w_out"]))
    r_in1, r_kv1 = with_own_of(l_in1, p_in1), with_own_of(l_kv1, p_kv1)
    r_out0 = with_own_of(l_out0, out_blocks(gr0["w_out"]))

    def both(name):
        return jnp.stack([gr0[name], gr1[name]])

    small = _pack_small(both("norm_w"), both("mem_norm_w"), both("out_norm_w"), dfnw[0],
                        both("b_forget")).at[LOSS_ROW].set(loss_part[0])
    p_small = jnp.broadcast_to(small[None], (N_DEV, SMALL_ROWS, LANES))
    by_core = [v.reshape((N_CHIP, 2) + v.shape[1:])
               for v in (in_blocks(gr0["w_r"]), kv_blocks(gr0["w_mem_kv"]))]
    from_sibling = _pair_swap(by_core, "grads_l0_pair_swap")
    core = lax.axis_index("c")
    chip_sums = [_pair_add(lax.dynamic_index_in_dim(v, core, axis=1, keepdims=False), got,
                           f"grads_l0_pair_add_{a}")
                 for a, (v, got) in enumerate(zip(by_core, from_sibling))]
    r_in0, r_kv0, r_small = _chip_exchange(chip_sums, [p_small], "exchange_grads_l0")

    def view(v):
        return jnp.transpose(v, (2, 0, 1))

    g_w_in, d_w_in, nm_w_in, nv_w_in = [jnp.transpose(v, (1, 2, 0)) for v in _adamw_w_in(
        [r_in0, r_in1], view(w_in), view(m_w_in), view(v_w_in), "adamw_w_in")]
    g_w_kv, d_w_kv, nm_w_kv, nv_w_kv = _adamw_sum(
        [r_kv0, r_kv1], w_mem_kv, m_w_mem_kv, v_w_mem_kv, (kv_rows, 2 * MEM_W), "adamw_w_mem_kv")
    g_w_out, d_w_out, nm_w_out, nv_w_out = _adamw_sum(
        [r_out0, r_out1], w_out, m_w_out, v_w_out, (out_rows, D_MODEL), "adamw_w_out")
    w_small = _pack_small(norm_w, mem_norm_w, out_norm_w, final_norm_w, b_forget)[None]
    m_small = _pack_small(m_norm_w, m_mem_norm_w, m_out_norm_w, m_final_norm_w, m_b_forget)[None]
    v_small = _pack_small(v_norm_w, v_mem_norm_w, v_out_norm_w, v_final_norm_w, v_b_forget)[None]
    small_out = _adamw_sum([r_small], w_small, m_small, v_small, (SMALL_ROWS, LANES), "adamw_small")
    (g_nw, g_mnw, g_onw, g_fnw, g_b), (d_nw, d_mnw, d_onw, d_fnw, d_b), \
        (nm_nw, nm_mnw, nm_onw, nm_fnw, nm_b), (nv_nw, nv_mnw, nv_onw, nv_fnw, nv_b) = [
            _unpack_small(t[0]) for t in small_out]
    loss = small_out[0][0, LOSS_ROW, 0]

    return (loss, grad_x[None],
            g_nw, g_w_in, g_b, g_mnw, g_w_kv, g_onw, g_w_out, g_fnw,
            d_nw, d_w_in, d_b, d_mnw, d_w_kv, d_onw, d_w_out, d_fnw,
            nm_nw, nm_w_in, nm_b, nm_mnw, nm_w_kv, nm_onw, nm_w_out, nm_fnw,
            nv_nw, nv_w_in, nv_b, nv_mnw, nv_w_kv, nv_onw, nv_w_out, nv_fnw)
```

```python
import functools
from typing import NamedTuple

import jax
import jax.numpy as jnp
from jax import lax
from jax.experimental import pallas as pl
from jax.experimental.pallas import tpu as pltpu

F32 = jnp.float32
BF16 = jnp.bfloat16

N_DEV = 8
D_MODEL = 1024
HEAD_DIM = 64
LANES = 128
SB_W = 512
FX_W = 512
MEM_W = 256
MIX_W = 1280
FOX_HEADS = 8
IN_W = 4616
SHARD_W = IN_W // N_DEV
QKV_W = 3 * SB_W + 3 * FX_W + MEM_W
FL_PAD = 256
GF_W = MIX_W + FL_PAD
WR_W = QKV_W + GF_W
EPS = 1e-6
T = 256
QPS = 4
TM = 256
TQM = 512
Q_SCALE = 0.125
NEG = -1e30
UNDERFLOW = -110.0
NORM_SLACK = 1.01

ADAM_LR = 0.001
ADAM_B1 = 0.9
ADAM_B2 = 0.999
ADAM_EPS = 1e-08
ADAM_WD = 0.01
ADAM_STEP = 10

SMALL_ROWS = 72
LOSS_ROW = 66

_NT = (((1,), (1,)), ((), ()))
_TN = (((0,), (0,)), ((), ()))

_ARB1 = pltpu.CompilerParams(dimension_semantics=("arbitrary",))
_ARB2 = pltpu.CompilerParams(dimension_semantics=("arbitrary", "arbitrary"))


def _dot(a, b):
    return jnp.dot(a, b, preferred_element_type=F32)


def _dot_nt(a, b):
    return lax.dot_general(a, b, _NT, preferred_element_type=F32)


def _dot_tn(a, b):
    return lax.dot_general(a, b, _TN, preferred_element_type=F32)


def _split2(x):
    hi = x.astype(BF16)
    lo = (x - hi.astype(F32)).astype(BF16)
    return hi, lo


def _stack2(u):
    return jnp.concatenate([u, u], axis=0)


def _cum2(x, u2):
    hi, lo = _split2(x)
    return _dot(jnp.concatenate([hi, lo], axis=1), u2)


def _tri3(tri, x, dot=None):
    dot = dot or _dot
    hi = x.astype(BF16)
    r1 = x - hi.astype(F32)
    mid = r1.astype(BF16)
    lo = (r1 - mid.astype(F32)).astype(BF16)
    return dot(tri, hi) + dot(tri, mid) + dot(tri, lo)


def _iota2(shape, dim):
    return lax.broadcasted_iota(jnp.int32, shape, dim)


def _head_block_diag():
    r = _iota2((LANES, LANES), 0) // HEAD_DIM
    c = _iota2((LANES, LANES), 1) // HEAD_DIM
    return _stack2(jnp.where(r == c, 1.0, 0.0).astype(BF16))


def _head_mean(x, bd):
    return _cum2(x, bd) * (1.0 / HEAD_DIM)


def _sigmoid(x):
    return 1.0 / (1.0 + jnp.exp(-x))


def _log_sigmoid(x):
    return jnp.minimum(x, 0.0) - jnp.log(1.0 + jnp.exp(-jnp.abs(x)))


def _running_top(r_ref):
    return jnp.max(jnp.maximum(r_ref[0], r_ref[1]))


def _fox_tiles_left(i, pair, nq, fb_ref, tile):
    def bound(j):
        b = []
        for h in range(2):
            head = 2 * pair + h
            b.append(2.0 * NORM_SLACK * fb_ref[2 * nq + i, head] * fb_ref[3 * nq, head]
                     + fb_ref[2 * i, head] - fb_ref[2 * j + 1, head])
        return jnp.maximum(b[0], b[1])

    def more(j):
        return jnp.logical_and(j >= 0, bound(jnp.maximum(j, 0)) > UNDERFLOW)

    def step(j):
        tile(j, False)
        return j - 1

    return lax.while_loop(more, step, i - 1)


def _pair_masks():
    lane = _iota2((1, LANES), 1)
    return lane < HEAD_DIM


def _split_pair(x, m0):
    zero = jnp.zeros_like(x)
    return jnp.where(m0, x, zero), jnp.where(m0, zero, x)


def _inproj_fwd(x, nw, w_r, name):
    s = x.shape[0]

    def body(x_ref, nw_ref, w_ref, qkv_ref, gf_ref):
        xv = x_ref[...]
        r = lax.rsqrt(jnp.mean(xv * xv, axis=-1, keepdims=True) + EPS)
        h = (xv * r * nw_ref[...]).astype(BF16)
        for c in range(0, QKV_W, 256):
            qkv_ref[:, c:c + 256] = _dot_nt(h, w_ref[c:c + 256, :]).astype(BF16)
        for c in range(0, GF_W, 256):
            gf_ref[:, c:c + 256] = _dot_nt(h, w_ref[QKV_W + c:QKV_W + c + 256, :])

    return pl.pallas_call(
        body, name=name, grid=(s // TM,),
        in_specs=[pl.BlockSpec((TM, D_MODEL), lambda i: (i, 0)),
                  pl.BlockSpec((1, D_MODEL), lambda i: (0, 0)),
                  pl.BlockSpec((WR_W, D_MODEL), lambda i: (0, 0))],
        out_specs=[pl.BlockSpec((TM, QKV_W), lambda i: (i, 0)),
                   pl.BlockSpec((TM, GF_W), lambda i: (i, 0))],
        out_shape=[jax.ShapeDtypeStruct((s, QKV_W), BF16), jax.ShapeDtypeStruct((s, GF_W), F32)],
        compiler_params=_ARB1,
    )(x, nw, w_r)


def _fox_prep_fwd(gf, qkv, bpad, name):
    s = gf.shape[0]
    nq = s // T
    nrow = -(-(3 * nq + 1) // 8) * 8

    def body(fl_ref, q_ref, k_ref, b_ref, fq_ref, fr_ref, fb_ref):
        tri = jnp.where(_iota2((T, T), 0) >= _iota2((T, T), 1), 1.0, 0.0).astype(BF16)
        m0 = _pair_masks()
        lane = _iota2((1, LANES), 1)
        norms = [jnp.zeros((1, LANES), F32) for _ in range(nq + 1)]
        same_head = (_iota2((LANES, LANES), 0) // HEAD_DIM) == (_iota2((LANES, LANES), 1) // HEAD_DIM)
        bd = jnp.where(same_head, 1.0, 0.0).astype(BF16)
        for p in range(FOX_HEADS // 2):
            cols = slice(p * LANES, (p + 1) * LANES)
            q = (q_ref[:, cols] * jnp.asarray(Q_SCALE, BF16)).astype(F32)
            k = k_ref[:, cols].astype(F32)
            qn = _dot((q * q).astype(BF16), bd)
            kn = _dot((k * k).astype(BF16), bd)
            tops = [jnp.max(qn[j * T:(j + 1) * T], axis=0, keepdims=True) for j in range(nq)]
            tops.append(jnp.max(kn, axis=0, keepdims=True))
            tops = [jnp.sqrt(top) for top in tops]
            for h in range(2):
                at = h * HEAD_DIM
                norms = [jnp.where(lane == 2 * p + h, top[:, at:at + 1], row)
                         for top, row in zip(tops, norms)]
        for j in range(nq + 1):
            fb_ref[2 * nq + j:2 * nq + j + 1, :] = norms[j]
        fb_ref[3 * nq + 1:, :] = jnp.zeros((nrow - 3 * nq - 1, LANES), F32)
        carry = jnp.zeros((1, LANES), F32)
        for blk in range(s // T):
            rows = slice(blk * T, (blk + 1) * T)
            lf = _log_sigmoid(fl_ref[rows, :] + b_ref[...])
            c = _tri3(tri, lf) + carry
            carry = c[T - 1:T, :]
            for p in range(FOX_HEADS // 2):
                fq_ref[rows, p * LANES:(p + 1) * LANES] = jnp.where(
                    m0, c[:, 2 * p:2 * p + 1], c[:, 2 * p + 1:2 * p + 2])
            fr_ref[:, rows] = c.T[0:FOX_HEADS, :]
            fb_ref[2 * blk:2 * blk + 1, :] = c[0:1, :]
            fb_ref[2 * blk + 1:2 * blk + 2, :] = carry

    base = 3 * SB_W // FX_W
    return pl.pallas_call(
        body, name=name, grid=(1,),
        in_specs=[pl.BlockSpec((s, LANES), lambda i: (0, MIX_W // LANES)),
                  pl.BlockSpec((s, FX_W), lambda i: (0, base)),
                  pl.BlockSpec((s, FX_W), lambda i: (0, base + 1)),
                  pl.BlockSpec((1, LANES), lambda i: (0, 0))],
        out_specs=[pl.BlockSpec((s, FX_W), lambda i: (0, 0)),
                   pl.BlockSpec((FOX_HEADS, s), lambda i: (0, 0)),
                   pl.BlockSpec((nrow, LANES), lambda i: (0, 0))],
        out_shape=[jax.ShapeDtypeStruct((s, FX_W), F32), jax.ShapeDtypeStruct((FOX_HEADS, s), F32),
                   jax.ShapeDtypeStruct((nrow, LANES), F32)],
        compiler_params=_ARB1,
    )(gf, qkv, qkv, bpad)


def _sb_fwd(qkv, name, rider=None):
    s = qkv.shape[0]

    def body(q_ref, k_ref, v_ref, o_ref, acc_ref, r_ref, as_ref):
        m0 = _pair_masks()
        strict = _iota2((T, T), 0) > _iota2((T, T), 1)
        u2 = _stack2(jnp.where(strict, 1.0, 0.0).astype(BF16))
        hs = range(2)

        def query_tile(i, rows):
            qh = _split_pair(q_ref[rows, :] * jnp.asarray(Q_SCALE, BF16), m0)
            acc_ref[...] = jnp.zeros_like(acc_ref)
            r_ref[...] = jnp.zeros_like(r_ref)

            def flush(j):
                v = v_ref[pl.ds(pl.multiple_of(j * T, T), T), :]
                for h in hs:
                    acc_ref[h] += _dot(as_ref[h], v)

            def tile(j, diag):
                k = k_ref[pl.ds(pl.multiple_of(j * T, T), T), :]
                z = [_dot_nt(qh[h], k) for h in hs]
                if not diag:
                    flush(j + 1)
                la = [jnp.minimum(z[h], 0.0) - jnp.log(1.0 + jnp.exp(-jnp.abs(z[h]))) for h in hs]
                lf = [la[h] - z[h] for h in hs]
                if diag:
                    lf = [jnp.where(strict, lf[h], 0.0) for h in hs]
                cin = [_cum2(lf[h], u2) for h in hs]
                a = [jnp.exp(la[h] + cin[h] + r_ref[h]) for h in hs]
                if diag:
                    a = [jnp.where(strict, a[h], 0.0) for h in hs]
                for h in hs:
                    r_ref[h] += cin[h][:, 0:1] + lf[h][:, 0:1]
                    as_ref[h] = a[h].astype(BF16)

            tile(i, True)

            def more(state):
                j, top = state
                return jnp.logical_and(j >= 0, top > UNDERFLOW)

            def step(state):
                j, _ = state
                tile(j, False)
                return j - 1, _running_top(r_ref)

            j_left, _ = lax.while_loop(more, step, (i - 1, _running_top(r_ref)))
            flush(j_left + 1)
            o_ref[rows, :] = jnp.where(m0, acc_ref[0], acc_ref[1])

        for n in range(QPS):
            query_tile(QPS * pl.program_id(1) + n, slice(n * T, (n + 1) * T))

    nb = SB_W // LANES
    (ysb,), lands = _ride(dict(
        body=body, name=name, grid=(nb, s // (QPS * T)),
        in_specs=[pl.BlockSpec((QPS * T, LANES), lambda p, i: (i, p)),
                  pl.BlockSpec((s, LANES), lambda p, i: (0, nb + p)),
                  pl.BlockSpec((s, LANES), lambda p, i: (0, 2 * nb + p))],
        out_specs=[pl.BlockSpec((QPS * T, LANES), lambda p, i: (i, p))],
        out_shape=[jax.ShapeDtypeStruct((s, SB_W), F32)],
        scratch_shapes=[pltpu.VMEM((2, T, LANES), F32), pltpu.VMEM((2, T, 1), F32),
                        pltpu.VMEM((2, T, T), BF16)],
        compiler_params=_ARB2, operands=[qkv, qkv, qkv]), rider)
    return ysb, lands


def _fox_fwd(qkv, fqb, frow, fbounds, name, rider=None):
    s = qkv.shape[0]

    def body(q_ref, k_ref, v_ref, fq_ref, fr_ref, fb_ref, o_ref, lse_ref, acc_ref, m_ref, ps_ref):
        pair = pl.program_id(0)
        m0 = _pair_masks()
        causal = _iota2((T, T), 0) >= _iota2((T, T), 1)
        ones = jnp.ones((T, LANES), BF16)
        hs = range(2)

        def query_tile(i, rows):
            qh = _split_pair(q_ref[rows, :] * jnp.asarray(Q_SCALE, BF16), m0)
            fq = fq_ref[rows, :]
            fqh = (fq[:, 0:1], fq[:, HEAD_DIM:HEAD_DIM + 1])
            acc_ref[...] = jnp.zeros_like(acc_ref)
            m_ref[...] = jnp.full_like(m_ref, NEG)

            def flush(j):
                v = v_ref[pl.ds(pl.multiple_of(j * T, T), T), :]
                va2 = _stack2(jnp.concatenate([v, ones], axis=1))
                for h in hs:
                    acc_ref[h] += _dot(ps_ref[h], va2)

            def tile(j, diag):
                off = pl.multiple_of(j * T, T)
                k = k_ref[pl.ds(off, T), :]
                sc = [_dot_nt(qh[h], k) + fqh[h] - fr_ref[h:h + 1, pl.ds(off, T)] for h in hs]
                if not diag:
                    flush(j + 1)
                if diag:
                    sc = [jnp.where(causal, sc[h], NEG) for h in hs]
                m_new = [jnp.maximum(m_ref[h], jnp.max(sc[h], axis=1, keepdims=True)) for h in hs]
                p = [jnp.exp(sc[h] - m_new[h]) for h in hs]
                for h in hs:
                    acc_ref[h] = acc_ref[h] * jnp.exp(m_ref[h] - m_new[h])
                    m_ref[h] = m_new[h]
                    p_hi, p_lo = _split2(p[h])
                    ps_ref[h] = jnp.concatenate([p_hi, p_lo], axis=1)

            tile(i, True)
            j_left = _fox_tiles_left(i, pair, s // T, fb_ref, tile)
            flush(j_left + 1)
            acc = (acc_ref[0], acc_ref[1])
            o_ref[rows, :] = jnp.where(m0, acc[0][:, :LANES] / acc[0][:, LANES:],
                                       acc[1][:, :LANES] / acc[1][:, LANES:])
            lse_ref[rows, :] = jnp.where(m0, m_ref[0] + jnp.log(acc[0][:, LANES:]),
                                         m_ref[1] + jnp.log(acc[1][:, LANES:]))

        for n in range(QPS):
            query_tile(QPS * pl.program_id(1) + n, slice(n * T, (n + 1) * T))

    nb = FX_W // LANES
    base = 3 * SB_W // LANES
    (yfx, lse), lands = _ride(dict(
        body=body, name=name, grid=(nb, s // (QPS * T)),
        in_specs=[pl.BlockSpec((QPS * T, LANES), lambda p, i: (i, base + p)),
                  pl.BlockSpec((s, LANES), lambda p, i: (0, base + nb + p)),
                  pl.BlockSpec((s, LANES), lambda p, i: (0, base + 2 * nb + p)),
                  pl.BlockSpec((QPS * T, LANES), lambda p, i: (i, p)),
                  pl.BlockSpec((None, 2, s), lambda p, i: (p, 0, 0)),
                  pl.BlockSpec(memory_space=pltpu.SMEM)],
        out_specs=[pl.BlockSpec((QPS * T, LANES), lambda p, i: (i, p)),
                   pl.BlockSpec((QPS * T, LANES), lambda p, i: (i, p))],
        out_shape=[jax.ShapeDtypeStruct((s, FX_W), F32), jax.ShapeDtypeStruct((s, FX_W), F32)],
        scratch_shapes=[pltpu.VMEM((2, T, 2 * LANES), F32), pltpu.VMEM((2, T, 1), F32),
                        pltpu.VMEM((2, T, 2 * T), BF16)],
        compiler_params=_ARB2, operands=[qkv, qkv, qkv, fqb, frow, fbounds]), rider)
    return yfx, lse, lands


def _memkv_fwd(mem, mnw, wkv, name):
    n = mem.shape[0]

    def body(mem_ref, mnw_ref, w_ref, kv_ref):
        mv = mem_ref[...]
        r = lax.rsqrt(jnp.mean(mv * mv, axis=-1, keepdims=True) + EPS)
        hm = (mv * r * mnw_ref[...]).astype(BF16)
        kv_ref[...] = _dot(hm, w_ref[...]).astype(BF16)

    return pl.pallas_call(
        body, name=name, grid=(1,),
        in_specs=[pl.BlockSpec((n, D_MODEL), lambda i: (0, 0)),
                  pl.BlockSpec((1, D_MODEL), lambda i: (0, 0)),
                  pl.BlockSpec((D_MODEL, 2 * MEM_W), lambda i: (0, 0))],
        out_specs=pl.BlockSpec((n, 2 * MEM_W), lambda i: (0, 0)),
        out_shape=jax.ShapeDtypeStruct((n, 2 * MEM_W), BF16),
        compiler_params=_ARB1,
    )(mem, mnw, wkv)


def _mem_fwd(qkv, kv, name):
    s = qkv.shape[0]
    n = kv.shape[0]

    def body(q_ref, k_ref, v_ref, o_ref, lse_ref):
        m0 = _pair_masks()
        qh = _split_pair(q_ref[...] * jnp.asarray(Q_SCALE, BF16), m0)
        k = k_ref[...]
        v = v_ref[...]
        outs, lses = [], []
        for h in range(2):
            sc = _dot_nt(qh[h], k)
            mx = jnp.max(sc, axis=1, keepdims=True)
            p = jnp.exp(sc - mx)
            l = jnp.sum(p, axis=1, keepdims=True)
            outs.append(_dot(p.astype(BF16), v) / l)
            lses.append(mx + jnp.log(l))
        o_ref[...] = jnp.where(m0, outs[0], outs[1])
        lse_ref[...] = jnp.where(m0, lses[0], lses[1])

    nb = MEM_W // LANES
    base = (3 * SB_W + 3 * FX_W) // LANES
    return pl.pallas_call(
        body, name=name, grid=(nb, s // TQM),
        in_specs=[pl.BlockSpec((TQM, LANES), lambda p, i: (i, base + p)),
                  pl.BlockSpec((n, LANES), lambda p, i: (0, p)),
                  pl.BlockSpec((n, LANES), lambda p, i: (0, nb + p))],
        out_specs=[pl.BlockSpec((TQM, LANES), lambda p, i: (i, p)),
                   pl.BlockSpec((TQM, LANES), lambda p, i: (i, p))],
        out_shape=[jax.ShapeDtypeStruct((s, MEM_W), F32), jax.ShapeDtypeStruct((s, MEM_W), F32)],
        compiler_params=_ARB2,
    )(qkv, kv, kv)


def _mix_chunk(c, ysb_ref, yfx_ref, ym_ref):
    if c < SB_W // LANES:
        return ysb_ref[:, c * LANES:(c + 1) * LANES]
    c -= SB_W // LANES
    if c < FX_W // LANES:
        return yfx_ref[:, c * LANES:(c + 1) * LANES]
    c -= FX_W // LANES
    return ym_ref[:, c * LANES:(c + 1) * LANES]


def _outproj_fwd(ysb, yfx, ym, gf, onw, wout, x, name):
    s = x.shape[0]

    def body(ysb_ref, yfx_ref, ym_ref, g_ref, onw_ref, w_ref, x_ref, o_ref, yg_ref):
        bd = _head_block_diag()
        for c in range(MIX_W // LANES):
            sl = slice(c * LANES, (c + 1) * LANES)
            u = _mix_chunk(c, ysb_ref, yfx_ref, ym_ref)
            r = lax.rsqrt(_head_mean(u * u, bd) + EPS)
            g = g_ref[:, sl]
            yg_ref[:, sl] = (u * r * onw_ref[:, sl] * (g * _sigmoid(g))).astype(BF16)
        o_ref[...] = x_ref[...] + _dot(yg_ref[...], w_ref[...])

    return pl.pallas_call(
        body, name=name, grid=(s // TM,),
        in_specs=[pl.BlockSpec((TM, SB_W), lambda i: (i, 0)),
                  pl.BlockSpec((TM, FX_W), lambda i: (i, 0)),
                  pl.BlockSpec((TM, MEM_W), lambda i: (i, 0)),
                  pl.BlockSpec((TM, MIX_W), lambda i: (i, 0)),
                  pl.BlockSpec((1, MIX_W), lambda i: (0, 0)),
                  pl.BlockSpec((MIX_W, D_MODEL), lambda i: (0, 0)),
                  pl.BlockSpec((TM, D_MODEL), lambda i: (i, 0))],
        out_specs=pl.BlockSpec((TM, D_MODEL), lambda i: (i, 0)),
        out_shape=jax.ShapeDtypeStruct((s, D_MODEL), F32),
        scratch_shapes=[pltpu.VMEM((TM, MIX_W), BF16)],
        compiler_params=_ARB1,
    )(ysb, yfx, ym, gf, onw, wout, x)


def _final_fwd_bwd(x, fnw, target, name):
    s = x.shape[0]

    def body(x_ref, w_ref, t_ref, dx_ref, loss_ref, dw_ref):
        @pl.when(pl.program_id(0) == 0)
        def _():
            loss_ref[...] = jnp.zeros_like(loss_ref)
            dw_ref[...] = jnp.zeros_like(dw_ref)

        xv = x_ref[...]
        w = w_ref[...]
        r = lax.rsqrt(jnp.mean(xv * xv, axis=-1, keepdims=True) + EPS)
        xh = xv * r
        err = xh * w - t_ref[...]
        part = jnp.sum(jnp.sum(err * err, axis=1, keepdims=True), axis=0, keepdims=True)
        loss_ref[...] += part * (0.5 / D_MODEL)
        dy = err * (1.0 / D_MODEL)
        dw_ref[...] += jnp.sum(dy * xh, axis=0, keepdims=True)
        dxh = dy * w
        dx_ref[...] = r * (dxh - xh * jnp.mean(dxh * xh, axis=-1, keepdims=True))

    return pl.pallas_call(
        body, name=name, grid=(s // TM,),
        in_specs=[pl.BlockSpec((TM, D_MODEL), lambda i: (i, 0)),
                  pl.BlockSpec((1, D_MODEL), lambda i: (0, 0)),
                  pl.BlockSpec((TM, D_MODEL), lambda i: (i, 0))],
        out_specs=[pl.BlockSpec((TM, D_MODEL), lambda i: (i, 0)),
                   pl.BlockSpec((1, LANES), lambda i: (0, 0)),
                   pl.BlockSpec((1, D_MODEL), lambda i: (0, 0))],
        out_shape=[jax.ShapeDtypeStruct((s, D_MODEL), F32), jax.ShapeDtypeStruct((1, LANES), F32),
                   jax.ShapeDtypeStruct((1, D_MODEL), F32)],
        compiler_params=_ARB1,
    )(x, fnw, target)


def _outproj_bwd(dxo, wout, ysb, yfx, ym, gf, onw, name):
    s = dxo.shape[0]

    def body(dx_ref, w_ref, ysb_ref, yfx_ref, ym_ref, g_ref, onw_ref,
             dysb_ref, dyfx_ref, dym_ref, dg_ref, dw_ref, donw_ref, yg_ref):
        @pl.when(pl.program_id(0) == 0)
        def _():
            dw_ref[...] = jnp.zeros_like(dw_ref)
            donw_ref[...] = jnp.zeros_like(donw_ref)

        dxb = dx_ref[...].astype(BF16)
        dyg = _dot_nt(dxb, w_ref[...])
        bd = _head_block_diag()
        for c in range(MIX_W // LANES):
            sl = slice(c * LANES, (c + 1) * LANES)
            u = _mix_chunk(c, ysb_ref, yfx_ref, ym_ref)
            r = lax.rsqrt(_head_mean(u * u, bd) + EPS)
            yn = u * r
            g = g_ref[:, sl]
            sg = _sigmoid(g)
            sil = g * sg
            onw = onw_ref[:, sl]
            e = dyg[:, sl]
            yg_ref[:, sl] = (yn * onw * sil).astype(BF16)
            donw_ref[:, sl] += jnp.sum(e * yn * sil, axis=0, keepdims=True)
            dg_ref[:, sl] = (e * yn * onw * (sg * (1.0 + g * (1.0 - sg)))).astype(BF16)
            dyn = e * onw * sil
            du = (r * (dyn - yn * _head_mean(dyn * yn, bd))).astype(BF16)
            if c < 4:
                dysb_ref[:, c * LANES:(c + 1) * LANES] = du
            elif c < 8:
                dyfx_ref[:, (c - 4) * LANES:(c - 3) * LANES] = du
            else:
                dym_ref[:, (c - 8) * LANES:(c - 7) * LANES] = du
        dw_ref[...] += _dot_tn(yg_ref[...], dxb)

    return pl.pallas_call(
        body, name=name, grid=(s // TM,),
        in_specs=[pl.BlockSpec((TM, D_MODEL), lambda i: (i, 0)),
                  pl.BlockSpec((MIX_W, D_MODEL), lambda i: (0, 0)),
                  pl.BlockSpec((TM, SB_W), lambda i: (i, 0)),
                  pl.BlockSpec((TM, FX_W), lambda i: (i, 0)),
                  pl.BlockSpec((TM, MEM_W), lambda i: (i, 0)),
                  pl.BlockSpec((TM, MIX_W), lambda i: (i, 0)),
                  pl.BlockSpec((1, MIX_W), lambda i: (0, 0))],
        out_specs=[pl.BlockSpec((TM, SB_W), lambda i: (i, 0)),
                   pl.BlockSpec((TM, FX_W), lambda i: (i, 0)),
                   pl.BlockSpec((TM, MEM_W), lambda i: (i, 0)),
                   pl.BlockSpec((TM, MIX_W), lambda i: (i, 0)),
                   pl.BlockSpec((MIX_W, D_MODEL), lambda i: (0, 0)),
                   pl.BlockSpec((1, MIX_W), lambda i: (0, 0))],
        out_shape=[jax.ShapeDtypeStruct((s, SB_W), BF16), jax.ShapeDtypeStruct((s, FX_W), BF16),
                   jax.ShapeDtypeStruct((s, MEM_W), BF16), jax.ShapeDtypeStruct((s, MIX_W), BF16),
                   jax.ShapeDtypeStruct((MIX_W, D_MODEL), F32), jax.ShapeDtypeStruct((1, MIX_W), F32)],
        scratch_shapes=[pltpu.VMEM((TM, MIX_W), BF16)],
        compiler_params=_ARB1,
    )(dxo, wout, ysb, yfx, ym, gf, onw)


def _row_dots(do, o, m0):
    prod = do.astype(F32) * o
    zero = jnp.zeros_like(prod)
    return (jnp.sum(jnp.where(m0, prod, zero), axis=1, keepdims=True),
            jnp.sum(jnp.where(m0, zero, prod), axis=1, keepdims=True))


def _sb_bwd(qkv, o, do, name, rider=None):
    s = qkv.shape[0]
    nq = s // T

    def body(q_ref, k_ref, v_ref, o_ref, do_ref, dq_ref, dk_ref, dv_ref,
             dqa_ref, dka_ref, dva_ref, rl_ref, rg_ref, dzs_ref, abs_ref):
        step_id = pl.program_id(1)

        @pl.when(step_id == 0)
        def _():
            dka_ref[...] = jnp.zeros_like(dka_ref)
            dva_ref[...] = jnp.zeros_like(dva_ref)

        m0 = _pair_masks()
        strict = _iota2((T, T), 0) > _iota2((T, T), 1)
        u2 = _stack2(jnp.where(strict, 1.0, 0.0).astype(BF16))
        hs = range(2)

        def query_tile(i, rows):
            qh = _split_pair(q_ref[rows, :] * jnp.asarray(Q_SCALE, BF16), m0)
            doh = _split_pair(do_ref[rows, :], m0)
            dsum = _row_dots(do_ref[rows, :], o_ref[rows, :], m0)
            dqa_ref[...] = jnp.zeros_like(dqa_ref)
            rl_ref[...] = jnp.zeros_like(rl_ref)
            rg_ref[...] = jnp.zeros_like(rg_ref)

            def flush(j):
                off = pl.multiple_of(j * T, T)
                k = k_ref[pl.ds(off, T), :]
                for h in hs:
                    dqa_ref[h] += _dot(dzs_ref[h], k)
                dka_ref[pl.ds(off, T), :] += (_dot_tn(dzs_ref[0], qh[0])
                                              + _dot_tn(dzs_ref[1], qh[1]))
                dva_ref[pl.ds(off, T), :] += (_dot_tn(abs_ref[0], doh[0])
                                              + _dot_tn(abs_ref[1], doh[1]))

            def tile(j, diag):
                off = pl.multiple_of(j * T, T)
                k = k_ref[pl.ds(off, T), :]
                v = v_ref[pl.ds(off, T), :]
                z = [_dot_nt(qh[h], k) for h in hs]
                da = [_dot_nt(doh[h], v) for h in hs]
                if not diag:
                    flush(j + 1)
                la = [jnp.minimum(z[h], 0.0) - jnp.log(1.0 + jnp.exp(-jnp.abs(z[h]))) for h in hs]
                lf = [la[h] - z[h] for h in hs]
                if diag:
                    lf = [jnp.where(strict, lf[h], 0.0) for h in hs]
                cin = [_cum2(lf[h], u2) for h in hs]
                a = [jnp.exp(la[h] + cin[h] + rl_ref[h]) for h in hs]
                if diag:
                    a = [jnp.where(strict, a[h], 0.0) for h in hs]
                ab = [a[h].astype(BF16) for h in hs]
                g = [ab[h].astype(F32) * da[h] for h in hs]
                gin = [_cum2(g[h], u2) for h in hs]
                dz = [g[h] - jnp.exp(la[h]) * ((dsum[h] - rg_ref[h]) - gin[h]) for h in hs]
                if diag:
                    dz = [jnp.where(strict, dz[h], 0.0) for h in hs]
                for h in hs:
                    rl_ref[h] += cin[h][:, 0:1] + lf[h][:, 0:1]
                    rg_ref[h] += gin[h][:, 0:1] + g[h][:, 0:1]
                    dzs_ref[h] = dz[h].astype(BF16)
                    abs_ref[h] = ab[h]

            tile(i, True)

            def more(state):
                j, top = state
                return jnp.logical_and(j >= 0, top > UNDERFLOW)

            def step(state):
                j, _ = state
                tile(j, False)
                return j - 1, _running_top(rl_ref)

            j_left, _ = lax.while_loop(more, step, (i - 1, _running_top(rl_ref)))
            flush(j_left + 1)
            dq_ref[rows, :] = (jnp.where(m0, dqa_ref[0], dqa_ref[1]) * Q_SCALE).astype(BF16)

        for n in range(QPS):
            query_tile(QPS * step_id + n, slice(n * T, (n + 1) * T))

        @pl.when(step_id == nq // QPS - 1)
        def _():
            dk_ref[...] = dka_ref[...].astype(BF16)
            dv_ref[...] = dva_ref[...].astype(BF16)

    nb = SB_W // LANES
    (dq, dk, dv), lands = _ride(dict(
        body=body, name=name, grid=(nb, nq // QPS),
        in_specs=[pl.BlockSpec((QPS * T, LANES), lambda p, i: (i, p)),
                  pl.BlockSpec((s, LANES), lambda p, i: (0, nb + p)),
                  pl.BlockSpec((s, LANES), lambda p, i: (0, 2 * nb + p)),
                  pl.BlockSpec((QPS * T, LANES), lambda p, i: (i, p)),
                  pl.BlockSpec((QPS * T, LANES), lambda p, i: (i, p))],
        out_specs=[pl.BlockSpec((QPS * T, LANES), lambda p, i: (i, p)),
                   pl.BlockSpec((s, LANES), lambda p, i: (0, p)),
                   pl.BlockSpec((s, LANES), lambda p, i: (0, p))],
        out_shape=[jax.ShapeDtypeStruct((s, SB_W), BF16)] * 3,
        scratch_shapes=[pltpu.VMEM((2, T, LANES), F32), pltpu.VMEM((s, LANES), F32),
                        pltpu.VMEM((s, LANES), F32), pltpu.VMEM((2, T, 1), F32),
                        pltpu.VMEM((2, T, 1), F32), pltpu.VMEM((2, T, T), BF16),
                        pltpu.VMEM((2, T, T), BF16)],
        compiler_params=_ARB2, operands=[qkv, qkv, qkv, o, do]), rider)
    return dq, dk, dv, lands


def _fox_bwd(qkv, fqb, frow, fbounds, o, lse, do, name, rider=None):
    s = qkv.shape[0]
    nq = s // T

    def body(q_ref, k_ref, v_ref, fq_ref, fr_ref, fb_ref, o_ref, lse_ref, do_ref,
             dq_ref, dk_ref, dv_ref, df_ref, dqa_ref, dka_ref, dva_ref, dfa_ref, dls_ref, pbs_ref):
        step_id = pl.program_id(1)

        @pl.when(step_id == 0)
        def _():
            dka_ref[...] = jnp.zeros_like(dka_ref)
            dva_ref[...] = jnp.zeros_like(dva_ref)
            dfa_ref[...] = jnp.zeros_like(dfa_ref)

        m0 = _pair_masks()
        causal = _iota2((T, T), 0) >= _iota2((T, T), 1)
        hs = range(2)

        def query_tile(i, rows):
            qh = _split_pair(q_ref[rows, :] * jnp.asarray(Q_SCALE, BF16), m0)
            doh = _split_pair(do_ref[rows, :], m0)
            dsum = _row_dots(do_ref[rows, :], o_ref[rows, :], m0)
            fq = fq_ref[rows, :]
            fqh = (fq[:, 0:1], fq[:, HEAD_DIM:HEAD_DIM + 1])
            lse = lse_ref[rows, :]
            lseh = (lse[:, 0:1], lse[:, HEAD_DIM:HEAD_DIM + 1])
            dqa_ref[...] = jnp.zeros_like(dqa_ref)

            def flush(j):
                off = pl.multiple_of(j * T, T)
                k = k_ref[pl.ds(off, T), :]
                for h in hs:
                    dqa_ref[h] += _dot(dls_ref[h], k)
                dka_ref[pl.ds(off, T), :] += (_dot_tn(dls_ref[0], qh[0])
                                              + _dot_tn(dls_ref[1], qh[1]))
                dva_ref[pl.ds(off, T), :] += (_dot_tn(pbs_ref[0], doh[0])
                                              + _dot_tn(pbs_ref[1], doh[1]))

            def tile(j, diag):
                off = pl.multiple_of(j * T, T)
                k = k_ref[pl.ds(off, T), :]
                v = v_ref[pl.ds(off, T), :]
                sc = [_dot_nt(qh[h], k) + fqh[h] - fr_ref[h:h + 1, pl.ds(off, T)] for h in hs]
                dp = [_dot_nt(doh[h], v) for h in hs]
                if not diag:
                    flush(j + 1)
                p = [jnp.exp(sc[h] - lseh[h]) for h in hs]
                if diag:
                    p = [jnp.where(causal, p[h], 0.0) for h in hs]
                dl = [p[h] * (dp[h] - dsum[h]) for h in hs]
                for h in hs:
                    dls_ref[h] = dl[h].astype(BF16)
                    pbs_ref[h] = p[h].astype(BF16)
                    dfa_ref[h:h + 1, pl.ds(off, T)] -= jnp.sum(dl[h], axis=0, keepdims=True)

            tile(i, True)
            j_left = _fox_tiles_left(i, pl.program_id(0), nq, fb_ref, tile)
            flush(j_left + 1)
            dq_ref[rows, :] = (jnp.where(m0, dqa_ref[0], dqa_ref[1]) * Q_SCALE).astype(BF16)

        for n in range(QPS):
            query_tile(QPS * step_id + n, slice(n * T, (n + 1) * T))

        @pl.when(step_id == nq // QPS - 1)
        def _():
            dk_ref[...] = dka_ref[...].astype(BF16)
            dv_ref[...] = dva_ref[...].astype(BF16)
            df_ref[...] = dfa_ref[...]

    nb = FX_W // LANES
    base = 3 * SB_W // LANES
    (dq, dk, dv, df), lands = _ride(dict(
        body=body, name=name, grid=(nb, nq // QPS),
        in_specs=[pl.BlockSpec((QPS * T, LANES), lambda p, i: (i, base + p)),
                  pl.BlockSpec((s, LANES), lambda p, i: (0, base + nb + p)),
                  pl.BlockSpec((s, LANES), lambda p, i: (0, base + 2 * nb + p)),
                  pl.BlockSpec((QPS * T, LANES), lambda p, i: (i, p)),
                  pl.BlockSpec((None, 2, s), lambda p, i: (p, 0, 0)),
                  pl.BlockSpec(memory_space=pltpu.SMEM),
                  pl.BlockSpec((QPS * T, LANES), lambda p, i: (i, p)),
                  pl.BlockSpec((QPS * T, LANES), lambda p, i: (i, p)),
                  pl.BlockSpec((QPS * T, LANES), lambda p, i: (i, p))],
        out_specs=[pl.BlockSpec((QPS * T, LANES), lambda p, i: (i, p)),
                   pl.BlockSpec((s, LANES), lambda p, i: (0, p)),
                   pl.BlockSpec((s, LANES), lambda p, i: (0, p)),
                   pl.BlockSpec((None, 2, s), lambda p, i: (p, 0, 0))],
        out_shape=[jax.ShapeDtypeStruct((s, FX_W), BF16)] * 3
        + [jax.ShapeDtypeStruct((nb, 2, s), F32)],
        scratch_shapes=[pltpu.VMEM((2, T, LANES), F32), pltpu.VMEM((s, LANES), F32),
                        pltpu.VMEM((s, LANES), F32), pltpu.VMEM((2, s), F32),
                        pltpu.VMEM((2, T, T), BF16), pltpu.VMEM((2, T, T), BF16)],
        compiler_params=_ARB2, operands=[qkv, qkv, qkv, fqb, frow, fbounds, o, lse, do]), rider)
    return dq, dk, dv, df, lands


def _fox_prep_bwd(dfrow, gf, bpad, name):
    s = gf.shape[0]

    def body(df_ref, fl_ref, b_ref, dfl_ref, db_ref):
        tri = jnp.where(_iota2((T, T), 0) <= _iota2((T, T), 1), 1.0, 0.0).astype(BF16)
        carry = jnp.zeros((1, LANES), F32)
        db = jnp.zeros((1, LANES), F32)
        fill = jnp.zeros((LANES - FOX_HEADS, T), F32)
        for blk in reversed(range(s // T)):
            rows = slice(blk * T, (blk + 1) * T)
            c = _tri3(tri, jnp.concatenate([df_ref[:, rows], fill], axis=0), _dot_nt) + carry
            carry = c[0:1, :]
            dfl = c / (1.0 + jnp.exp(fl_ref[rows, :] + b_ref[...]))
            dfl_ref[rows, :] = dfl.astype(BF16)
            db = db + jnp.sum(dfl, axis=0, keepdims=True)
        db_ref[...] = db

    return pl.pallas_call(
        body, name=name, grid=(1,),
        in_specs=[pl.BlockSpec((FOX_HEADS, s), lambda i: (0, 0)),
                  pl.BlockSpec((s, LANES), lambda i: (0, MIX_W // LANES)),
                  pl.BlockSpec((1, LANES), lambda i: (0, 0))],
        out_specs=[pl.BlockSpec((s, LANES), lambda i: (0, 0)),
                   pl.BlockSpec((1, LANES), lambda i: (0, 0))],
        out_shape=[jax.ShapeDtypeStruct((s, LANES), BF16), jax.ShapeDtypeStruct((1, LANES), F32)],
        compiler_params=_ARB1,
    )(dfrow, gf, bpad)


def _mem_bwd(qkv, kv, o, lse, do, name):
    s = qkv.shape[0]
    n = kv.shape[0]

    def body(q_ref, k_ref, v_ref, o_ref, lse_ref, do_ref, dq_ref, dk_ref, dv_ref):
        @pl.when(pl.program_id(1) == 0)
        def _():
            dk_ref[...] = jnp.zeros_like(dk_ref)
            dv_ref[...] = jnp.zeros_like(dv_ref)

        m0 = _pair_masks()
        qh = _split_pair(q_ref[...] * jnp.asarray(Q_SCALE, BF16), m0)
        doh = _split_pair(do_ref[...], m0)
        dsum = _row_dots(do_ref[...], o_ref[...], m0)
        lse = lse_ref[...]
        lseh = (lse[:, 0:1], lse[:, HEAD_DIM:HEAD_DIM + 1])
        k = k_ref[...]
        v = v_ref[...]
        dqs = []
        for h in range(2):
            p = jnp.exp(_dot_nt(qh[h], k) - lseh[h])
            dl = p * (_dot_nt(doh[h], v) - dsum[h])
            dlb = dl.astype(BF16)
            dqs.append(_dot(dlb, k))
            dk_ref[...] += _dot_tn(dlb, qh[h])
            dv_ref[...] += _dot_tn(p.astype(BF16), doh[h])
        dq_ref[...] = (jnp.where(m0, dqs[0], dqs[1]) * Q_SCALE).astype(BF16)

    nb = MEM_W // LANES
    base = (3 * SB_W + 3 * FX_W) // LANES
    return pl.pallas_call(
        body, name=name, grid=(nb, s // TQM),
        in_specs=[pl.BlockSpec((TQM, LANES), lambda p, i: (i, base + p)),
                  pl.BlockSpec((n, LANES), lambda p, i: (0, p)),
                  pl.BlockSpec((n, LANES), lambda p, i: (0, nb + p)),
                  pl.BlockSpec((TQM, LANES), lambda p, i: (i, p)),
                  pl.BlockSpec((TQM, LANES), lambda p, i: (i, p)),
                  pl.BlockSpec((TQM, LANES), lambda p, i: (i, p))],
        out_specs=[pl.BlockSpec((TQM, LANES), lambda p, i: (i, p)),
                   pl.BlockSpec((n, LANES), lambda p, i: (0, p)),
                   pl.BlockSpec((n, LANES), lambda p, i: (0, p))],
        out_shape=[jax.ShapeDtypeStruct((s, MEM_W), BF16), jax.ShapeDtypeStruct((n, MEM_W), F32),
                   jax.ShapeDtypeStruct((n, MEM_W), F32)],
        compiler_params=_ARB2,
    )(qkv, kv, kv, o, lse, do)


def _memkv_bwd(mem, mnw, wkv, dk, dv, name):
    n = mem.shape[0]

    def body(mem_ref, mnw_ref, w_ref, dk_ref, dv_ref, dw_ref, dmnw_ref):
        mv = mem_ref[...]
        r = lax.rsqrt(jnp.mean(mv * mv, axis=-1, keepdims=True) + EPS)
        mh = mv * r
        hm = (mh * mnw_ref[...]).astype(BF16)
        dkv = jnp.concatenate([dk_ref[...], dv_ref[...]], axis=1).astype(BF16)
        dw_ref[...] = _dot_tn(hm, dkv)
        dhm = _dot_nt(dkv, w_ref[...])
        dmnw_ref[...] = jnp.sum(dhm * mh, axis=0, keepdims=True)

    return pl.pallas_call(
        body, name=name, grid=(1,),
        in_specs=[pl.BlockSpec((n, D_MODEL), lambda i: (0, 0)),
                  pl.BlockSpec((1, D_MODEL), lambda i: (0, 0)),
                  pl.BlockSpec((D_MODEL, 2 * MEM_W), lambda i: (0, 0)),
                  pl.BlockSpec((n, MEM_W), lambda i: (0, 0)),
                  pl.BlockSpec((n, MEM_W), lambda i: (0, 0))],
        out_specs=[pl.BlockSpec((D_MODEL, 2 * MEM_W), lambda i: (0, 0)),
                   pl.BlockSpec((1, D_MODEL), lambda i: (0, 0))],
        out_shape=[jax.ShapeDtypeStruct((D_MODEL, 2 * MEM_W), F32),
                   jax.ShapeDtypeStruct((1, D_MODEL), F32)],
        compiler_params=_ARB1,
    )(mem, mnw, wkv, dk, dv)


def _inproj_bwd_dx(pieces, w_r, x, nw, dxo, name):
    s = x.shape[0]
    n = len(pieces)
    widths = [p.shape[1] for p in pieces]

    def body(*refs):
        piece_refs = refs[:n]
        w_ref, x_ref, nw_ref, dxo_ref, dx_ref, h_ref, dnw_ref, dp_ref = refs[n:]

        @pl.when(pl.program_id(0) == 0)
        def _():
            dnw_ref[...] = jnp.zeros_like(dnw_ref)

        col = 0
        for r, wd in zip(piece_refs, widths):
            dp_ref[:, col:col + wd] = r[...]
            col += wd
        dp_ref[:, col:] = jnp.zeros((TM, WR_W - col), BF16)
        dh = _dot(dp_ref[...], w_ref[...])
        xv = x_ref[...]
        nw = nw_ref[...]
        r = lax.rsqrt(jnp.mean(xv * xv, axis=-1, keepdims=True) + EPS)
        xh = xv * r
        h_ref[...] = (xh * nw).astype(BF16)
        dnw_ref[...] += jnp.sum(dh * xh, axis=0, keepdims=True)
        dxh = dh * nw
        dx_ref[...] = r * (dxh - xh * jnp.mean(dxh * xh, axis=-1, keepdims=True)) + dxo_ref[...]

    return pl.pallas_call(
        body, name=name, grid=(s // TM,),
        in_specs=[pl.BlockSpec((TM, wd), lambda i: (i, 0)) for wd in widths]
        + [pl.BlockSpec((WR_W, D_MODEL), lambda i: (0, 0)),
           pl.BlockSpec((TM, D_MODEL), lambda i: (i, 0)),
           pl.BlockSpec((1, D_MODEL), lambda i: (0, 0)),
           pl.BlockSpec((TM, D_MODEL), lambda i: (i, 0))],
        out_specs=[pl.BlockSpec((TM, D_MODEL), lambda i: (i, 0)),
                   pl.BlockSpec((TM, D_MODEL), lambda i: (i, 0)),
                   pl.BlockSpec((1, D_MODEL), lambda i: (0, 0)),
                   pl.BlockSpec((TM, WR_W), lambda i: (i, 0))],
        out_shape=[jax.ShapeDtypeStruct((s, D_MODEL), F32), jax.ShapeDtypeStruct((s, D_MODEL), BF16),
                   jax.ShapeDtypeStruct((1, D_MODEL), F32), jax.ShapeDtypeStruct((s, WR_W), BF16)],
        compiler_params=_ARB1,
    )(*pieces, w_r, x, nw, dxo)


def _inproj_bwd_dw(h, dproj, name):
    s = dproj.shape[0]
    tn = 256

    def body(h_ref, dp_ref, dw_ref):
        dw_ref[...] = _dot_tn(dp_ref[...], h_ref[...])

    return pl.pallas_call(
        body, name=name, grid=(WR_W // tn,),
        in_specs=[pl.BlockSpec((s, D_MODEL), lambda j: (0, 0)),
                  pl.BlockSpec((s, tn), lambda j: (0, j))],
        out_specs=pl.BlockSpec((tn, D_MODEL), lambda j: (j, 0)),
        out_shape=jax.ShapeDtypeStruct((WR_W, D_MODEL), F32),
        compiler_params=_ARB1,
    )(h, dproj)


def _rearrange_w_in(wt):
    pad = jnp.zeros((FL_PAD - FOX_HEADS,) + wt.shape[1:], wt.dtype)
    return jnp.concatenate([wt[:3072], wt[3080:3336], wt[3336:IN_W], wt[3072:3080], pad], axis=0)


def _restore_w_in(g):
    gate0 = QKV_W
    fl0 = QKV_W + MIX_W
    return jnp.concatenate(
        [g[:3072], g[fl0:fl0 + FOX_HEADS], g[3072:QKV_W], g[gate0:fl0]], axis=0)


def _pad_lanes(v, width=LANES):
    return jnp.pad(v, (0, width - v.shape[0])).reshape(1, width)


def _layer_fwd(xs, mem, nw, w_r, b_forget, mnw, late, onw, l, travel=None):
    s = xs.shape[0]
    bpad = _pad_lanes(b_forget)
    qkv, gf = _inproj_fwd(xs, nw, w_r, f"inproj_fwd_{l}")
    fqb, frow, fbounds = _fox_prep_fwd(gf, qkv, bpad, f"fox_prep_fwd_{l}")
    frow = frow.reshape(FOX_HEADS // 2, 2, s)
    travel = _Travel(travel)
    ysb, _ = travel.ride(0, _sb_fwd, qkv, f"sb_fwd_{l}")
    yfx, lse_fx, _ = travel.ride(1, _fox_fwd, qkv, fqb, frow, fbounds, f"fox_fwd_{l}")
    wkv, wout = late(travel.lands)
    kv = _memkv_fwd(mem, mnw, wkv, f"memkv_fwd_{l}")
    ym, lse_m = _mem_fwd(qkv, kv, f"mem_fwd_{l}")
    xn = _outproj_fwd(ysb, yfx, ym, gf, onw, wout, xs, f"outproj_fwd_{l}")
    saved = (xs, nw, mnw, onw, bpad, qkv, gf, fqb, frow, fbounds, ysb, yfx, lse_fx, kv, ym, lse_m)
    return xn, saved, travel.lands, (wkv, wout)


class _Travel:
    def __init__(self, plan):
        self.plan = plan
        self.lands = None if plan is None else _new_lands(plan[0], plan[1])

    def ride(self, n, fn, *args):
        if self.plan is None or self.plan[2][n] is None:
            return fn(*args)
        srcs, scatter, legs = self.plan
        idx, rows = legs[n]
        out = fn(*args, rider=_Rider([srcs[a] for a in idx], [self.lands[a] for a in idx],
                                     scatter, rows, relay=not scatter))
        for a, land in zip(idx, out[-1]):
            self.lands[a] = land
        return out


def _layer_bwd(dx, saved, mem, w_r, wkv, wout, l, travel=None):
    xs, nw, mnw, onw, bpad, qkv, gf, fqb, frow, fbounds, ysb, yfx, lse_fx, kv, ym, lse_m = saved
    s = xs.shape[0]
    dysb, dyfx, dym, dgate, dwout, donw = _outproj_bwd(
        dx, wout, ysb, yfx, ym, gf, onw, f"outproj_bwd_{l}")
    travel = _Travel(None if travel is None else travel(dwout))
    sdq, sdk, sdv, _ = travel.ride(0, _sb_bwd, qkv, ysb, dysb, f"sb_bwd_{l}")
    fdq, fdk, fdv, dfrow, _ = travel.ride(1, _fox_bwd, qkv, fqb, frow, fbounds, yfx, lse_fx, dyfx,
                                          f"fox_bwd_{l}")
    dfl, db = _fox_prep_bwd(dfrow.reshape(FOX_HEADS, s), gf, bpad, f"fox_prep_bwd_{l}")
    dmq, dmk, dmv = _mem_bwd(qkv, kv, ym, lse_m, dym, f"mem_bwd_{l}")
    dwkv, dmnw = _memkv_bwd(mem, mnw, wkv, dmk, dmv, f"memkv_bwd_{l}")
    dx, ht, dnw, dproj = _inproj_bwd_dx([sdq, sdk, sdv, fdq, fdk, fdv, dmq, dgate, dfl],
                                        w_r, xs, nw, dx, f"inproj_bwd_dx_{l}")
    dwr = _inproj_bwd_dw(ht, dproj, f"inproj_bwd_dw_{l}")
    grads = dict(norm_w=dnw[0], w_r=dwr, b_forget=db[0, :FOX_HEADS], mem_norm_w=dmnw[0],
                 w_mem_kv=dwkv, out_norm_w=donw[0], w_out=dwout)
    return dx, grads, travel.lands


_ANY = pl.BlockSpec(memory_space=pl.ANY)


def _my_place():
    return lax.axis_index("x"), lax.axis_index("y"), lax.axis_index("c")


def _flip(v, bit):
    return 1 - v if bit else v


def _block_index(px, py, pc):
    return 4 * px + 2 * py + pc


def _all_gather_weights(shards, name):
    n = len(shards)

    def body(*refs):
        ins, outs = refs[:n], refs[n:2 * n]
        send_sems, recv_sems, local_sems = refs[2 * n:]
        x, y, c = _my_place()
        me = (x, y, c)
        sibling = (x, y, 1 - c)
        chips = [(1 - x, y), (x, 1 - y), (1 - x, 1 - y)]

        def copy(a, k, block, to, src=None):
            dst = outs[a].at[_block_index(*block)]
            return pltpu.make_async_remote_copy(
                src_ref=dst if src is None else src, dst_ref=dst,
                send_sem=send_sems.at[a, k], recv_sem=recv_sems.at[a, k],
                device_id=to, device_id_type=pl.DeviceIdType.MESH)

        mine = [pltpu.make_async_copy(ins[a], outs[a].at[_block_index(*me)], local_sems.at[a])
                for a in range(n)]
        for cp in mine:
            cp.start()
        first = []
        for a in range(n):
            first.append(copy(a, 0, me, sibling, src=ins[a]))
            first += [copy(a, 1 + j, me, (*chip, c), src=ins[a]) for j, chip in enumerate(chips)]
        for cp in first:
            cp.start()
        passed = []
        for j, chip in enumerate(chips):
            for a in range(n):
                copy(a, 1 + j, (*chip, c), me).wait_recv()
                fwd = copy(a, 4 + j, (*chip, c), sibling)
                fwd.start()
                passed.append(fwd)
        for a in range(n):
            copy(a, 0, sibling, me).wait_recv()
            for j, chip in enumerate(chips):
                copy(a, 4 + j, (*chip, 1 - c), me).wait_recv()
        for cp in first + passed:
            cp.wait_send()
        for cp in mine:
            cp.wait()

    return pl.pallas_call(
        body, name=name,
        in_specs=[_ANY] * n, out_specs=[_ANY] * n,
        out_shape=[jax.ShapeDtypeStruct((N_DEV,) + v.shape, v.dtype) for v in shards],
        scratch_shapes=[pltpu.SemaphoreType.DMA((n, 7)), pltpu.SemaphoreType.DMA((n, 7)),
                        pltpu.SemaphoreType.DMA((n,))],
    )(*shards)


def _exchange_blocks(blocked, name):
    n = len(blocked)

    def body(*refs):
        ins, outs = refs[:n], refs[n:2 * n]
        send_sems, recv_sems, local_sems = refs[2 * n:]
        x, y, c = _my_place()
        mine_idx = _block_index(x, y, c)
        local = [pltpu.make_async_copy(ins[a].at[mine_idx], outs[a].at[mine_idx], local_sems.at[a])
                 for a in range(n)]
        for cp in local:
            cp.start()
        sends, arrivals = [], []
        for r in range(1, N_DEV):
            peer = (_flip(x, r & 4), _flip(y, r & 2), _flip(c, r & 1))
            peer_idx = _block_index(*peer)
            for a in range(n):
                sems = dict(send_sem=send_sems.at[a, r - 1], recv_sem=recv_sems.at[a, r - 1],
                            device_id=peer, device_id_type=pl.DeviceIdType.MESH)
                sends.append(pltpu.make_async_remote_copy(
                    src_ref=ins[a].at[peer_idx], dst_ref=outs[a].at[mine_idx], **sems))
                arrivals.append(pltpu.make_async_remote_copy(
                    src_ref=ins[a].at[peer_idx], dst_ref=outs[a].at[peer_idx], **sems))
        for cp in sends:
            cp.start()
        for cp in arrivals:
            cp.wait_recv()
        for cp in sends:
            cp.wait_send()
        for cp in local:
            cp.wait()

    return pl.pallas_call(
        body, name=name,
        in_specs=[_ANY] * n, out_specs=[_ANY] * n,
        out_shape=[jax.ShapeDtypeStruct(v.shape, v.dtype) for v in blocked],
        scratch_shapes=[pltpu.SemaphoreType.DMA((n, 7)), pltpu.SemaphoreType.DMA((n, 7)),
                        pltpu.SemaphoreType.DMA((n,))],
    )(*blocked)


N_CHIP = N_DEV // 2


def _pair_swap(blocked, name):
    n = len(blocked)

    def body(*refs):
        ins, outs = refs[:n], refs[n:2 * n]
        send_sems, recv_sems = refs[2 * n:]
        x, y, c = _my_place()
        copies = [pltpu.make_async_remote_copy(
            src_ref=ins[a].at[j, 1 - c], dst_ref=outs[a].at[j],
            send_sem=send_sems.at[N_CHIP * a + j], recv_sem=recv_sems.at[N_CHIP * a + j],
            device_id=(x, y, 1 - c), device_id_type=pl.DeviceIdType.MESH)
            for a in range(n) for j in range(N_CHIP)]
        for cp in copies:
            cp.start()
        for cp in copies:
            cp.wait_recv()
        for cp in copies:
            cp.wait_send()

    return pl.pallas_call(
        body, name=name,
        in_specs=[_ANY] * n, out_specs=[_ANY] * n,
        out_shape=[jax.ShapeDtypeStruct((N_CHIP,) + v.shape[2:], v.dtype) for v in blocked],
        scratch_shapes=[pltpu.SemaphoreType.DMA((N_CHIP * n,)),
                        pltpu.SemaphoreType.DMA((N_CHIP * n,))],
    )(*blocked)


def _pair_add(mine, theirs, name):
    _, nrow, ncol = mine.shape

    def body(a_ref, b_ref, o_ref):
        o_ref[...] = (a_ref[...].astype(F32) + b_ref[...].astype(F32)).astype(BF16)

    blk = pl.BlockSpec((None, nrow, ncol), lambda j: (j, 0, 0))
    return pl.pallas_call(
        body, name=name, grid=(N_CHIP,), in_specs=[blk, blk], out_specs=blk,
        out_shape=jax.ShapeDtypeStruct(mine.shape, BF16), compiler_params=_ARB1,
    )(mine, theirs)


def _chip_exchange(by_chip, to_all, name):
    n, m = len(by_chip), len(to_all)

    def body(*refs):
        ins, alls = refs[:n], refs[n:n + m]
        outs, all_outs = refs[n + m:2 * n + m], refs[2 * n + m:2 * (n + m)]
        send_sems, recv_sems, local_sems = refs[2 * (n + m):]
        x, y, c = _my_place()
        my_chip, mine_idx = 2 * x + y, _block_index(x, y, c)
        local = [pltpu.make_async_copy(ins[a].at[my_chip], outs[a].at[my_chip], local_sems.at[a])
                 for a in range(n)]
        local += [pltpu.make_async_copy(alls[b].at[mine_idx], all_outs[b].at[mine_idx],
                                        local_sems.at[n + b]) for b in range(m)]
        for cp in local:
            cp.start()
        sends, arrivals = [], []
        k = 0
        for r in range(1, N_DEV):
            peer = (_flip(x, r & 4), _flip(y, r & 2), _flip(c, r & 1))
            peer_chip, peer_idx = 2 * peer[0] + peer[1], _block_index(*peer)
            pairs = [(alls[b].at[mine_idx], all_outs[b].at[mine_idx], all_outs[b].at[peer_idx])
                     for b in range(m)]
            if not r & 1:
                pairs += [(ins[a].at[peer_chip], outs[a].at[my_chip], outs[a].at[peer_chip])
                          for a in range(n)]
            for src, there, here in pairs:
                sems = dict(send_sem=send_sems.at[k], recv_sem=recv_sems.at[k], device_id=peer,
                            device_id_type=pl.DeviceIdType.MESH)
                sends.append(pltpu.make_async_remote_copy(src_ref=src, dst_ref=there, **sems))
                arrivals.append(pltpu.make_async_remote_copy(src_ref=src, dst_ref=here, **sems))
                k += 1
        for cp in sends:
            cp.start()
        for cp in arrivals:
            cp.wait_recv()
        for cp in sends:
            cp.wait_send()
        for cp in local:
            cp.wait()

    n_copies = 7 * m + 3 * n
    return pl.pallas_call(
        body, name=name,
        in_specs=[_ANY] * (n + m), out_specs=[_ANY] * (n + m),
        out_shape=[jax.ShapeDtypeStruct(v.shape, v.dtype) for v in by_chip + to_all],
        scratch_shapes=[pltpu.SemaphoreType.DMA((n_copies,)), pltpu.SemaphoreType.DMA((n_copies,)),
                        pltpu.SemaphoreType.DMA((n + m,))],
    )(*by_chip, *to_all)


class _Rider(NamedTuple):
    srcs: list
    lands: list
    scatter: bool
    part: list
    relay: bool = False


def _window(ref, part):
    if part is None:
        return ref
    dim, start, size = part
    return ref.at[(slice(None),) * dim + (pl.ds(start, size),)]


def _relay_copies(srcs, lands, send_sems, recv_sems, rider):
    x, y, c = _my_place()
    me, sibling = (x, y, c), (x, y, 1 - c)
    chips = [(1 - x, y), (x, 1 - y), (1 - x, 1 - y)]
    first, from_chips, passed, last = [], [], [], []
    for a in range(len(srcs)):
        def copy(k, block, to, src=None, a=a):
            dst = _window(lands[a].at[_block_index(*block)], rider.part[a])
            return pltpu.make_async_remote_copy(
                src_ref=dst if src is None else src, dst_ref=dst,
                send_sem=send_sems.at[7 * a + k], recv_sem=recv_sems.at[7 * a + k],
                device_id=to, device_id_type=pl.DeviceIdType.MESH)

        mine = _window(srcs[a], rider.part[a])
        first.append(copy(0, me, sibling, src=mine))
        first += [copy(1 + j, me, (*chip, c), src=mine) for j, chip in enumerate(chips)]
        from_chips += [copy(1 + j, (*chip, c), me) for j, chip in enumerate(chips)]
        passed += [copy(4 + j, (*chip, c), sibling) for j, chip in enumerate(chips)]
        last.append(copy(0, sibling, me))
        last += [copy(4 + j, (*chip, 1 - c), me) for j, chip in enumerate(chips)]
    return first, from_chips, passed, last


def _new_lands(srcs, scatter):
    return [lax.empty(v.shape if scatter else (N_DEV,) + v.shape, v.dtype) for v in srcs]


def _rider_copies(srcs, lands, send_sems, recv_sems, rider):
    x, y, c = _my_place()
    mine_idx = _block_index(x, y, c)

    def window(ref, a):
        return _window(ref, rider.part[a])

    sends, arrivals = [], []
    for r in range(1, N_DEV):
        peer = (_flip(x, r & 4), _flip(y, r & 2), _flip(c, r & 1))
        peer_idx = _block_index(*peer)
        for a in range(len(srcs)):
            src = window(srcs[a].at[peer_idx] if rider.scatter else srcs[a], a)
            k = 7 * a + r - 1
            sems = dict(send_sem=send_sems.at[k], recv_sem=recv_sems.at[k],
                        device_id=peer, device_id_type=pl.DeviceIdType.MESH)
            sends.append(pltpu.make_async_remote_copy(
                src_ref=src, dst_ref=window(lands[a].at[mine_idx], a), **sems))
            arrivals.append(pltpu.make_async_remote_copy(
                src_ref=src, dst_ref=window(lands[a].at[peer_idx], a), **sems))
    return sends, arrivals


def _ride(call, rider):
    call = dict(call)
    body, grid = call.pop("body"), call["grid"]
    operands = call.pop("operands")
    if rider is None:
        return list(pl.pallas_call(body, **call)(*operands)), None
    n_in, n_out = len(call["in_specs"]), len(call["out_specs"])
    n_scratch = len(call["scratch_shapes"])
    m = len(rider.srcs)

    def riding(*refs):
        main_in, srcs, lands = refs[:n_in], refs[n_in:n_in + m], refs[n_in + m:n_in + 2 * m]
        main_out = refs[n_in + 2 * m:n_in + 2 * m + n_out]
        rest = refs[n_in + 3 * m + n_out:]
        send_sems, recv_sems = rest[n_scratch:]
        at = [pl.program_id(d) for d in range(len(grid))]
        first = functools.reduce(jnp.logical_and, [p == 0 for p in at])
        last = functools.reduce(jnp.logical_and, [p == g - 1 for p, g in zip(at, grid)])
        if rider.relay:
            sends, from_chips, passed, arrivals = _relay_copies(
                srcs, lands, send_sems, recv_sems, rider)
            step, steps = 0, 1
            for p, g in zip(at, grid):
                step, steps = step * g + p, steps * g
            assert steps >= 2, "a relayed gather needs a later grid step to pass blocks on"

            @pl.when(step == (3 * steps) // 4)
            def _():
                for cp in from_chips:
                    cp.wait_recv()
                for cp in passed:
                    cp.start()
        else:
            sends, arrivals = _rider_copies(srcs, lands, send_sems, recv_sems, rider)
            passed = []

        @pl.when(first)
        def _():
            for cp in sends:
                cp.start()

        body(*main_in, *main_out, *rest[:n_scratch])

        @pl.when(last)
        def _():
            for cp in arrivals:
                cp.wait_recv()
            for cp in sends + passed:
                cp.wait_send()

    call["in_specs"] = list(call["in_specs"]) + [_ANY] * (2 * m)
    call["out_specs"] = list(call["out_specs"]) + [_ANY] * m
    call["out_shape"] = list(call["out_shape"]) + [
        jax.ShapeDtypeStruct(v.shape, v.dtype) for v in rider.lands]
    call["scratch_shapes"] = list(call["scratch_shapes"]) + [
        pltpu.SemaphoreType.DMA((7 * m,)), pltpu.SemaphoreType.DMA((7 * m,))]
    call["input_output_aliases"] = {n_in + m + a: n_out + a for a in range(m)}
    outs = pl.pallas_call(riding, **call)(*operands, *rider.srcs, *rider.lands)
    return list(outs[:n_out]), list(outs[n_out:])


def _sum_parts(p_ref):
    g = p_ref[0].astype(F32)
    for k in range(1, p_ref.shape[0]):
        g = g + p_ref[k].astype(F32)
    return g


def _adamw(g, w, m, v):
    c1 = 1.0 / (1.0 - ADAM_B1 ** ADAM_STEP)
    c2 = 1.0 / (1.0 - ADAM_B2 ** ADAM_STEP)
    nm = ADAM_B1 * m + (1.0 - ADAM_B1) * g
    nv = ADAM_B2 * v + (1.0 - ADAM_B2) * (g * g)
    return nm, nv, -ADAM_LR * ((nm * c1) / (jnp.sqrt(nv * c2) + ADAM_EPS) + ADAM_WD * w)


def _adamw_w_in(parts, w, m, v, name):
    ncol_blk, depth, nfeat = w.shape
    cols = 256

    def body(*refs):
        p_refs = refs[:depth]
        w_ref, m_ref, v_ref, g_ref, d_ref, nm_ref, nv_ref = refs[depth:]
        for l in range(depth):
            g = _sum_parts(p_refs[l])
            nm, nv, d = _adamw(g, w_ref[:, l, :], m_ref[:, l, :], v_ref[:, l, :])
            g_ref[:, l, :] = g
            nm_ref[:, l, :] = nm
            nv_ref[:, l, :] = nv
            d_ref[:, l, :] = d

    blk = pl.BlockSpec((ncol_blk, depth, cols), lambda j: (0, 0, j))
    return pl.pallas_call(
        body, name=name, grid=(nfeat // cols,),
        in_specs=[pl.BlockSpec((p.shape[0], ncol_blk, cols), lambda j: (0, 0, j)) for p in parts]
        + [blk] * 3,
        out_specs=[blk] * 4,
        out_shape=[jax.ShapeDtypeStruct(w.shape, F32)] * 4,
        compiler_params=_ARB1,
    )(*parts, w, m, v)


def _adamw_sum(parts, w, m, v, tile, name):
    depth, nrow, ncol = w.shape
    rows, cols = tile

    def body(*refs):
        p_refs = refs[:depth]
        w_ref, m_ref, v_ref, g_ref, d_ref, nm_ref, nv_ref = refs[depth:]
        layer = pl.program_id(0)
        for l in range(depth):
            @pl.when(layer == l)
            def _(p_ref=p_refs[l]):
                g = _sum_parts(p_ref)
                nm, nv, d = _adamw(g, w_ref[...], m_ref[...], v_ref[...])
                g_ref[...] = g
                nm_ref[...] = nm
                nv_ref[...] = nv
                d_ref[...] = d

    def part_spec(l):
        return pl.BlockSpec((parts[l].shape[0], rows, cols), lambda q, i, j: (
            0, jnp.where(q == l, i, 0), jnp.where(q == l, j, 0)))

    blk = pl.BlockSpec((None, rows, cols), lambda q, i, j: (q, i, j))
    return pl.pallas_call(
        body, name=name, grid=(depth, nrow // rows, ncol // cols),
        in_specs=[part_spec(l) for l in range(depth)] + [blk, blk, blk],
        out_specs=[blk] * 4,
        out_shape=[jax.ShapeDtypeStruct(w.shape, F32)] * 4,
        compiler_params=pltpu.CompilerParams(
            dimension_semantics=("arbitrary", "arbitrary", "arbitrary")),
    )(*parts, w, m, v)


def _pack_small(norm_w, mem_norm_w, out_norm_w, final_norm_w, b_forget):
    onw = jnp.pad(out_norm_w.reshape(20, LANES), ((0, 4), (0, 0)))
    b = jnp.pad(b_forget, ((0, 6), (0, LANES - FOX_HEADS)))
    return jnp.concatenate([norm_w.reshape(16, LANES), mem_norm_w.reshape(16, LANES), onw,
                            final_norm_w.reshape(8, LANES), b], axis=0)


def _unpack_small(p):
    return (p[0:16].reshape(2, D_MODEL), p[16:32].reshape(2, D_MODEL), p[32:52].reshape(2, MIX_W),
            p[56:64].reshape(D_MODEL), p[64:66, :FOX_HEADS])


def kernel(x, mem, norm_w, w_in, b_forget, mem_norm_w, w_mem_kv, out_norm_w, w_out, final_norm_w, loss_target, m_norm_w, m_w_in, m_b_forget, m_mem_norm_w, m_w_mem_kv, m_out_norm_w, m_w_out, m_final_norm_w, v_norm_w, v_w_in, v_b_forget, v_mem_norm_w, v_w_mem_kv, v_out_norm_w, v_w_out, v_final_norm_w):
    kv_rows = w_mem_kv.shape[1]
    out_rows = w_out.shape[1]
    me = _block_index(*_my_place())

    def shards(l):
        return [w_in[l].T.astype(BF16), w_mem_kv[l].astype(BF16), w_out[l].astype(BF16)]

    def full_in(g_in):
        return _rearrange_w_in(g_in.reshape(IN_W, D_MODEL))

    def full_kv_out(g_kv, g_out):
        return g_kv.reshape(D_MODEL, 2 * MEM_W), g_out.reshape(MIX_W, D_MODEL)

    def in_blocks(g):
        segments = [(0, 3072, 0), (3072, 3080, QKV_W + MIX_W), (3080, 3336, 3072),
                    (3336, IN_W, QKV_W)]

        def block(k):
            lo, hi = k * SHARD_W, (k + 1) * SHARD_W
            pieces = [g[at + max(lo, a) - a:at + min(hi, b) - a]
                      for a, b, at in segments if max(lo, a) < min(hi, b)]
            return jnp.concatenate(pieces, axis=0).astype(BF16)

        return jnp.stack([block(k) for k in range(N_DEV)])

    def kv_blocks(g):
        return g.reshape(N_DEV, kv_rows, 2 * MEM_W).astype(BF16)

    def out_blocks(g):
        return g.reshape(N_DEV, out_rows, D_MODEL).astype(BF16)

    def with_own(land, own):
        return lax.dynamic_update_slice(land, own[None], (me,) + (0,) * own.ndim)

    def with_own_of(land, blocked):
        return lax.dynamic_update_slice(land, lax.dynamic_slice_in_dim(blocked, me, 1, axis=0),
                                        (me,) + (0,) * (land.ndim - 1))

    def row(v):
        return v.reshape(1, -1)

    def cols(first, size):
        return (1, first, size)

    fwd_split, bwd_split = 4 * LANES, 6 * LANES

    g_in0, g_kv0, g_out0 = _all_gather_weights(shards(0), "all_gather_l0")
    w_r0 = full_in(g_in0)
    s_in1, s_kv1, s_out1 = shards(1)
    x1, saved0, (l_in1,), (wkv0, wout0) = _layer_fwd(
        x[0], mem[0], row(norm_w[0]), w_r0, b_forget[0], row(mem_norm_w[0]),
        lambda lands: full_kv_out(g_kv0, g_out0), row(out_norm_w[0]), 0,
        travel=([s_in1], False, [([0], [cols(0, fwd_split)]),
                                 ([0], [cols(fwd_split, D_MODEL - fwd_split)])]))
    w_r1 = full_in(with_own(l_in1, s_in1))
    x2, saved1, _, (wkv1, wout1) = _layer_fwd(
        x1, mem[0], row(norm_w[1]), w_r1, b_forget[1], row(mem_norm_w[1]),
        lambda lands: full_kv_out(with_own(lands[0], s_kv1), with_own(lands[1], s_out1)),
        row(out_norm_w[1]), 1, travel=([s_kv1, s_out1], False, [([0, 1], [None, None]), None]))

    dx2, loss_part, dfnw = _final_fwd_bwd(x2, row(final_norm_w), loss_target[0], "final_fwd_bwd")

    dx1, gr1, (l_out1,) = _layer_bwd(
        dx2, saved1, mem[0], w_r1, wkv1, wout1, 1,
        travel=lambda dwout: ([out_blocks(dwout)], True, [([0], [None]), None]))
    p_in1, p_kv1 = in_blocks(gr1["w_r"]), kv_blocks(gr1["w_mem_kv"])
    grad_x, gr0, (l_in1, l_kv1, l_out0) = _layer_bwd(
        dx1, saved0, mem[0], w_r0, wkv0, wout0, 0,
        travel=lambda dwout: ([p_in1, p_kv1, out_blocks(dwout)], True,
                              [([0, 1], [cols(0, bwd_split), None]),
                               ([0, 2], [cols(bwd_split, D_MODEL - bwd_split), None])]))
    r_out1 = with_own_of(l_out1, out_blocks(gr1["w_out"]))
    r_in1, r_kv1 = with_own_of(l_in1, p_in1), with_own_of(l_kv1, p_kv1)
    r_out0 = with_own_of(l_out0, out_blocks(gr0["w_out"]))

    def both(name):
        return jnp.stack([gr0[name], gr1[name]])

    small = _pack_small(both("norm_w"), both("mem_norm_w"), both("out_norm_w"), dfnw[0],
                        both("b_forget")).at[LOSS_ROW].set(loss_part[0])
    p_small = jnp.broadcast_to(small[None], (N_DEV, SMALL_ROWS, LANES))
    by_core = [v.reshape((N_CHIP, 2) + v.shape[1:])
               for v in (in_blocks(gr0["w_r"]), kv_blocks(gr0["w_mem_kv"]))]
    from_sibling = _pair_swap(by_core, "grads_l0_pair_swap")
    core = lax.axis_index("c")
    chip_sums = [_pair_add(lax.dynamic_index_in_dim(v, core, axis=1, keepdims=False), got,
                           f"grads_l0_pair_add_{a}")
                 for a, (v, got) in enumerate(zip(by_core, from_sibling))]
    r_in0, r_kv0, r_small = _chip_exchange(chip_sums, [p_small], "exchange_grads_l0")

    def view(v):
        return jnp.transpose(v, (2, 0, 1))

    g_w_in, d_w_in, nm_w_in, nv_w_in = [jnp.transpose(v, (1, 2, 0)) for v in _adamw_w_in(
        [r_in0, r_in1], view(w_in), view(m_w_in), view(v_w_in), "adamw_w_in")]
    g_w_kv, d_w_kv, nm_w_kv, nv_w_kv = _adamw_sum(
        [r_kv0, r_kv1], w_mem_kv, m_w_mem_kv, v_w_mem_kv, (kv_rows, 2 * MEM_W), "adamw_w_mem_kv")
    g_w_out, d_w_out, nm_w_out, nv_w_out = _adamw_sum(
        [r_out0, r_out1], w_out, m_w_out, v_w_out, (out_rows, D_MODEL), "adamw_w_out")
    w_small = _pack_small(norm_w, mem_norm_w, out_norm_w, final_norm_w, b_forget)[None]
    m_small = _pack_small(m_norm_w, m_mem_norm_w, m_out_norm_w, m_final_norm_w, m_b_forget)[None]
    v_small = _pack_small(v_norm_w, v_mem_norm_w, v_out_norm_w, v_final_norm_w, v_b_forget)[None]
    small_out = _adamw_sum([r_small], w_small, m_small, v_small, (SMALL_ROWS, LANES), "adamw_small")
    (g_nw, g_mnw, g_onw, g_fnw, g_b), (d_nw, d_mnw, d_onw, d_fnw, d_b), \
        (nm_nw, nm_mnw, nm_onw, nm_fnw, nm_b), (nv_nw, nv_mnw, nv_onw, nv_fnw, nv_b) = [
            _unpack_small(t[0]) for t in small_out]
    loss = small_out[0][0, LOSS_ROW, 0]

    return (loss, grad_x[None],
            g_nw, g_w_in, g_b, g_mnw, g_w_kv, g_onw, g_w_out, g_fnw,
            d_nw, d_w_in, d_b, d_mnw, d_w_kv, d_onw, d_w_out, d_fnw,
            nm_nw, nm_w_in, nm_b, nm_mnw, nm_w_kv, nm_onw, nm_w_out, nm_fnw,
            nv_nw, nv_w_in, nv_b, nv_mnw, nv_w_kv, nv_onw, nv_w_out, nv_fnw)
```

```python
import functools
from typing import NamedTuple

import jax
import jax.numpy as jnp
from jax import lax
from jax.experimental import pallas as pl
from jax.experimental.pallas import tpu as pltpu

F32 = jnp.float32
BF16 = jnp.bfloat16

N_DEV = 8
D_MODEL = 1024
HEAD_DIM = 64
LANES = 128
SB_W = 512
FX_W = 512
MEM_W = 256
MIX_W = 1280
FOX_HEADS = 8
IN_W = 4616
SHARD_W = IN_W // N_DEV
QKV_W = 3 * SB_W + 3 * FX_W + MEM_W
FL_PAD = 256
GF_W = MIX_W + FL_PAD
WR_W = QKV_W + GF_W
EPS = 1e-6
T = 256
QPS = 4
TM = 256
TQM = 512
Q_SCALE = 0.125
NEG = -1e30
UNDERFLOW = -110.0
NORM_SLACK = 1.01

ADAM_LR = 0.001
ADAM_B1 = 0.9
ADAM_B2 = 0.999
ADAM_EPS = 1e-08
ADAM_WD = 0.01
ADAM_STEP = 10

SMALL_ROWS = 72
LOSS_ROW = 66

_NT = (((1,), (1,)), ((), ()))
_TN = (((0,), (0,)), ((), ()))

_ARB1 = pltpu.CompilerParams(dimension_semantics=("arbitrary",))
_ARB2 = pltpu.CompilerParams(dimension_semantics=("arbitrary", "arbitrary"))


def _dot(a, b):
    return jnp.dot(a, b, preferred_element_type=F32)


def _dot_nt(a, b):
    return lax.dot_general(a, b, _NT, preferred_element_type=F32)


def _dot_tn(a, b):
    return lax.dot_general(a, b, _TN, preferred_element_type=F32)


def _split2(x):
    hi = x.astype(BF16)
    lo = (x - hi.astype(F32)).astype(BF16)
    return hi, lo


def _stack2(u):
    return jnp.concatenate([u, u], axis=0)


def _cum2(x, u2):
    hi, lo = _split2(x)
    return _dot(jnp.concatenate([hi, lo], axis=1), u2)


def _tri3(tri, x, dot=None):
    dot = dot or _dot
    hi = x.astype(BF16)
    r1 = x - hi.astype(F32)
    mid = r1.astype(BF16)
    lo = (r1 - mid.astype(F32)).astype(BF16)
    return dot(tri, hi) + dot(tri, mid) + dot(tri, lo)


def _iota2(shape, dim):
    return lax.broadcasted_iota(jnp.int32, shape, dim)


def _head_block_diag():
    r = _iota2((LANES, LANES), 0) // HEAD_DIM
    c = _iota2((LANES, LANES), 1) // HEAD_DIM
    return _stack2(jnp.where(r == c, 1.0, 0.0).astype(BF16))


def _head_mean(x, bd):
    return _cum2(x, bd) * (1.0 / HEAD_DIM)


def _sigmoid(x):
    return 1.0 / (1.0 + jnp.exp(-x))


def _log_sigmoid(x):
    return jnp.minimum(x, 0.0) - jnp.log(1.0 + jnp.exp(-jnp.abs(x)))


def _running_top(r_ref):
    return jnp.max(jnp.maximum(r_ref[0], r_ref[1]))


def _fox_tiles_left(i, pair, nq, fb_ref, tile):
    def bound(j):
        b = []
        for h in range(2):
            head = 2 * pair + h
            b.append(2.0 * NORM_SLACK * fb_ref[2 * nq + i, head] * fb_ref[3 * nq, head]
                     + fb_ref[2 * i, head] - fb_ref[2 * j + 1, head])
        return jnp.maximum(b[0], b[1])

    def more(j):
        return jnp.logical_and(j >= 0, bound(jnp.maximum(j, 0)) > UNDERFLOW)

    def step(j):
        tile(j, False)
        return j - 1

    return lax.while_loop(more, step, i - 1)


def _pair_masks():
    lane = _iota2((1, LANES), 1)
    return lane < HEAD_DIM


def _split_pair(x, m0):
    zero = jnp.zeros_like(x)
    return jnp.where(m0, x, zero), jnp.where(m0, zero, x)


def _inproj_fwd(x, nw, w_r, name):
    s = x.shape[0]

    def body(x_ref, nw_ref, w_ref, qkv_ref, gf_ref):
        xv = x_ref[...]
        r = lax.rsqrt(jnp.mean(xv * xv, axis=-1, keepdims=True) + EPS)
        h = (xv * r * nw_ref[...]).astype(BF16)
        for c in range(0, QKV_W, 256):
            qkv_ref[:, c:c + 256] = _dot_nt(h, w_ref[c:c + 256, :]).astype(BF16)
        for c in range(0, GF_W, 256):
            gf_ref[:, c:c + 256] = _dot_nt(h, w_ref[QKV_W + c:QKV_W + c + 256, :])

    return pl.pallas_call(
        body, name=name, grid=(s // TM,),
        in_specs=[pl.BlockSpec((TM, D_MODEL), lambda i: (i, 0)),
                  pl.BlockSpec((1, D_MODEL), lambda i: (0, 0)),
                  pl.BlockSpec((WR_W, D_MODEL), lambda i: (0, 0))],
        out_specs=[pl.BlockSpec((TM, QKV_W), lambda i: (i, 0)),
                   pl.BlockSpec((TM, GF_W), lambda i: (i, 0))],
        out_shape=[jax.ShapeDtypeStruct((s, QKV_W), BF16), jax.ShapeDtypeStruct((s, GF_W), F32)],
        compiler_params=_ARB1,
    )(x, nw, w_r)


def _fox_prep_fwd(gf, qkv, bpad, name):
    s = gf.shape[0]
    nq = s // T
    nrow = -(-(3 * nq + 1) // 8) * 8

    def body(fl_ref, q_ref, k_ref, b_ref, fq_ref, fr_ref, fb_ref):
        tri = jnp.where(_iota2((T, T), 0) >= _iota2((T, T), 1), 1.0, 0.0).astype(BF16)
        m0 = _pair_masks()
        lane = _iota2((1, LANES), 1)
        norms = [jnp.zeros((1, LANES), F32) for _ in range(nq + 1)]
        same_head = (_iota2((LANES, LANES), 0) // HEAD_DIM) == (_iota2((LANES, LANES), 1) // HEAD_DIM)
        bd = jnp.where(same_head, 1.0, 0.0).astype(BF16)
        for p in range(FOX_HEADS // 2):
            cols = slice(p * LANES, (p + 1) * LANES)
            q = (q_ref[:, cols] * jnp.asarray(Q_SCALE, BF16)).astype(F32)
            k = k_ref[:, cols].astype(F32)
            qn = _dot((q * q).astype(BF16), bd)
            kn = _dot((k * k).astype(BF16), bd)
            tops = [jnp.max(qn[j * T:(j + 1) * T], axis=0, keepdims=True) for j in range(nq)]
            tops.append(jnp.max(kn, axis=0, keepdims=True))
            tops = [jnp.sqrt(top) for top in tops]
            for h in range(2):
                at = h * HEAD_DIM
                norms = [jnp.where(lane == 2 * p + h, top[:, at:at + 1], row)
                         for top, row in zip(tops, norms)]
        for j in range(nq + 1):
            fb_ref[2 * nq + j:2 * nq + j + 1, :] = norms[j]
        fb_ref[3 * nq + 1:, :] = jnp.zeros((nrow - 3 * nq - 1, LANES), F32)
        carry = jnp.zeros((1, LANES), F32)
        for blk in range(s // T):
            rows = slice(blk * T, (blk + 1) * T)
            lf = _log_sigmoid(fl_ref[rows, :] + b_ref[...])
            c = _tri3(tri, lf) + carry
            carry = c[T - 1:T, :]
            for p in range(FOX_HEADS // 2):
                fq_ref[rows, p * LANES:(p + 1) * LANES] = jnp.where(
                    m0, c[:, 2 * p:2 * p + 1], c[:, 2 * p + 1:2 * p + 2])
            fr_ref[:, rows] = c.T[0:FOX_HEADS, :]
            fb_ref[2 * blk:2 * blk + 1, :] = c[0:1, :]
            fb_ref[2 * blk + 1:2 * blk + 2, :] = carry

    base = 3 * SB_W // FX_W
    return pl.pallas_call(
        body, name=name, grid=(1,),
        in_specs=[pl.BlockSpec((s, LANES), lambda i: (0, MIX_W // LANES)),
                  pl.BlockSpec((s, FX_W), lambda i: (0, base)),
                  pl.BlockSpec((s, FX_W), lambda i: (0, base + 1)),
                  pl.BlockSpec((1, LANES), lambda i: (0, 0))],
        out_specs=[pl.BlockSpec((s, FX_W), lambda i: (0, 0)),
                   pl.BlockSpec((FOX_HEADS, s), lambda i: (0, 0)),
                   pl.BlockSpec((nrow, LANES), lambda i: (0, 0))],
        out_shape=[jax.ShapeDtypeStruct((s, FX_W), F32), jax.ShapeDtypeStruct((FOX_HEADS, s), F32),
                   jax.ShapeDtypeStruct((nrow, LANES), F32)],
        compiler_params=_ARB1,
    )(gf, qkv, qkv, bpad)


def _sb_fwd(qkv, name, rider=None):
    s = qkv.shape[0]

    def body(q_ref, k_ref, v_ref, o_ref, acc_ref, r_ref, as_ref):
        m0 = _pair_masks()
        strict = _iota2((T, T), 0) > _iota2((T, T), 1)
        u2 = _stack2(jnp.where(strict, 1.0, 0.0).astype(BF16))
        hs = range(2)

        def query_tile(i, rows):
            qh = _split_pair(q_ref[rows, :] * jnp.asarray(Q_SCALE, BF16), m0)
            acc_ref[...] = jnp.zeros_like(acc_ref)
            r_ref[...] = jnp.zeros_like(r_ref)

            def flush(j):
                v = v_ref[pl.ds(pl.multiple_of(j * T, T), T), :]
                for h in hs:
                    acc_ref[h] += _dot(as_ref[h], v)

            def tile(j, diag):
                k = k_ref[pl.ds(pl.multiple_of(j * T, T), T), :]
                z = [_dot_nt(qh[h], k) for h in hs]
                if not diag:
                    flush(j + 1)
                la = [jnp.minimum(z[h], 0.0) - jnp.log(1.0 + jnp.exp(-jnp.abs(z[h]))) for h in hs]
                lf = [la[h] - z[h] for h in hs]
                if diag:
                    lf = [jnp.where(strict, lf[h], 0.0) for h in hs]
                cin = [_cum2(lf[h], u2) for h in hs]
                a = [jnp.exp(la[h] + cin[h] + r_ref[h]) for h in hs]
                if diag:
                    a = [jnp.where(strict, a[h], 0.0) for h in hs]
                for h in hs:
                    r_ref[h] += cin[h][:, 0:1] + lf[h][:, 0:1]
                    as_ref[h] = a[h].astype(BF16)

            tile(i, True)

            def more(state):
                j, top = state
                return jnp.logical_and(j >= 0, top > UNDERFLOW)

            def step(state):
                j, _ = state
                tile(j, False)
                return j - 1, _running_top(r_ref)

            j_left, _ = lax.while_loop(more, step, (i - 1, _running_top(r_ref)))
            flush(j_left + 1)
            o_ref[rows, :] = jnp.where(m0, acc_ref[0], acc_ref[1])

        for n in range(QPS):
            query_tile(QPS * pl.program_id(1) + n, slice(n * T, (n + 1) * T))

    nb = SB_W // LANES
    (ysb,), lands = _ride(dict(
        body=body, name=name, grid=(nb, s // (QPS * T)),
        in_specs=[pl.BlockSpec((QPS * T, LANES), lambda p, i: (i, p)),
                  pl.BlockSpec((s, LANES), lambda p, i: (0, nb + p)),
                  pl.BlockSpec((s, LANES), lambda p, i: (0, 2 * nb + p))],
        out_specs=[pl.BlockSpec((QPS * T, LANES), lambda p, i: (i, p))],
        out_shape=[jax.ShapeDtypeStruct((s, SB_W), F32)],
        scratch_shapes=[pltpu.VMEM((2, T, LANES), F32), pltpu.VMEM((2, T, 1), F32),
                        pltpu.VMEM((2, T, T), BF16)],
        compiler_params=_ARB2, operands=[qkv, qkv, qkv]), rider)
    return ysb, lands


def _fox_fwd(qkv, fqb, frow, fbounds, name, rider=None):
    s = qkv.shape[0]

    def body(q_ref, k_ref, v_ref, fq_ref, fr_ref, fb_ref, o_ref, lse_ref, acc_ref, m_ref, ps_ref):
        pair = pl.program_id(0)
        m0 = _pair_masks()
        causal = _iota2((T, T), 0) >= _iota2((T, T), 1)
        ones = jnp.ones((T, LANES), BF16)
        hs = range(2)

        def query_tile(i, rows):
            qh = _split_pair(q_ref[rows, :] * jnp.asarray(Q_SCALE, BF16), m0)
            fq = fq_ref[rows, :]
            fqh = (fq[:, 0:1], fq[:, HEAD_DIM:HEAD_DIM + 1])
            acc_ref[...] = jnp.zeros_like(acc_ref)
            m_ref[...] = jnp.full_like(m_ref, NEG)

            def flush(j):
                v = v_ref[pl.ds(pl.multiple_of(j * T, T), T), :]
                va2 = _stack2(jnp.concatenate([v, ones], axis=1))
                for h in hs:
                    acc_ref[h] += _dot(ps_ref[h], va2)

            def tile(j, diag):
                off = pl.multiple_of(j * T, T)
                k = k_ref[pl.ds(off, T), :]
                sc = [_dot_nt(qh[h], k) + fqh[h] - fr_ref[h:h + 1, pl.ds(off, T)] for h in hs]
                if not diag:
                    flush(j + 1)
                if diag:
                    sc = [jnp.where(causal, sc[h], NEG) for h in hs]
                m_new = [jnp.maximum(m_ref[h], jnp.max(sc[h], axis=1, keepdims=True)) for h in hs]
                p = [jnp.exp(sc[h] - m_new[h]) for h in hs]
                for h in hs:
                    acc_ref[h] = acc_ref[h] * jnp.exp(m_ref[h] - m_new[h])
                    m_ref[h] = m_new[h]
                    p_hi, p_lo = _split2(p[h])
                    ps_ref[h] = jnp.concatenate([p_hi, p_lo], axis=1)

            tile(i, True)
            j_left = _fox_tiles_left(i, pair, s // T, fb_ref, tile)
            flush(j_left + 1)
            acc = (acc_ref[0], acc_ref[1])
            o_ref[rows, :] = jnp.where(m0, acc[0][:, :LANES] / acc[0][:, LANES:],
                                       acc[1][:, :LANES] / acc[1][:, LANES:])
            lse_ref[rows, :] = jnp.where(m0, m_ref[0] + jnp.log(acc[0][:, LANES:]),
                                         m_ref[1] + jnp.log(acc[1][:, LANES:]))

        for n in range(QPS):
            query_tile(QPS * pl.program_id(1) + n, slice(n * T, (n + 1) * T))

    nb = FX_W // LANES
    base = 3 * SB_W // LANES
    (yfx, lse), lands = _ride(dict(
        body=body, name=name, grid=(nb, s // (QPS * T)),
        in_specs=[pl.BlockSpec((QPS * T, LANES), lambda p, i: (i, base + p)),
                  pl.BlockSpec((s, LANES), lambda p, i: (0, base + nb + p)),
                  pl.BlockSpec((s, LANES), lambda p, i: (0, base + 2 * nb + p)),
                  pl.BlockSpec((QPS * T, LANES), lambda p, i: (i, p)),
                  pl.BlockSpec((None, 2, s), lambda p, i: (p, 0, 0)),
                  pl.BlockSpec(memory_space=pltpu.SMEM)],
        out_specs=[pl.BlockSpec((QPS * T, LANES), lambda p, i: (i, p)),
                   pl.BlockSpec((QPS * T, LANES), lambda p, i: (i, p))],
        out_shape=[jax.ShapeDtypeStruct((s, FX_W), F32), jax.ShapeDtypeStruct((s, FX_W), F32)],
        scratch_shapes=[pltpu.VMEM((2, T, 2 * LANES), F32), pltpu.VMEM((2, T, 1), F32),
                        pltpu.VMEM((2, T, 2 * T), BF16)],
        compiler_params=_ARB2, operands=[qkv, qkv, qkv, fqb, frow, fbounds]), rider)
    return yfx, lse, lands


def _memkv_fwd(mem, mnw, wkv, name):
    n = mem.shape[0]

    def body(mem_ref, mnw_ref, w_ref, kv_ref):
        mv = mem_ref[...]
        r = lax.rsqrt(jnp.mean(mv * mv, axis=-1, keepdims=True) + EPS)
        hm = (mv * r * mnw_ref[...]).astype(BF16)
        kv_ref[...] = _dot(hm, w_ref[...]).astype(BF16)

    return pl.pallas_call(
        body, name=name, grid=(1,),
        in_specs=[pl.BlockSpec((n, D_MODEL), lambda i: (0, 0)),
                  pl.BlockSpec((1, D_MODEL), lambda i: (0, 0)),
                  pl.BlockSpec((D_MODEL, 2 * MEM_W), lambda i: (0, 0))],
        out_specs=pl.BlockSpec((n, 2 * MEM_W), lambda i: (0, 0)),
        out_shape=jax.ShapeDtypeStruct((n, 2 * MEM_W), BF16),
        compiler_params=_ARB1,
    )(mem, mnw, wkv)


def _mem_fwd(qkv, kv, name):
    s = qkv.shape[0]
    n = kv.shape[0]

    def body(q_ref, k_ref, v_ref, o_ref, lse_ref):
        m0 = _pair_masks()
        qh = _split_pair(q_ref[...] * jnp.asarray(Q_SCALE, BF16), m0)
        k = k_ref[...]
        v = v_ref[...]
        outs, lses = [], []
        for h in range(2):
            sc = _dot_nt(qh[h], k)
            mx = jnp.max(sc, axis=1, keepdims=True)
            p = jnp.exp(sc - mx)
            l = jnp.sum(p, axis=1, keepdims=True)
            outs.append(_dot(p.astype(BF16), v) / l)
            lses.append(mx + jnp.log(l))
        o_ref[...] = jnp.where(m0, outs[0], outs[1])
        lse_ref[...] = jnp.where(m0, lses[0], lses[1])

    nb = MEM_W // LANES
    base = (3 * SB_W + 3 * FX_W) // LANES
    return pl.pallas_call(
        body, name=name, grid=(nb, s // TQM),
        in_specs=[pl.BlockSpec((TQM, LANES), lambda p, i: (i, base + p)),
                  pl.BlockSpec((n, LANES), lambda p, i: (0, p)),
                  pl.BlockSpec((n, LANES), lambda p, i: (0, nb + p))],
        out_specs=[pl.BlockSpec((TQM, LANES), lambda p, i: (i, p)),
                   pl.BlockSpec((TQM, LANES), lambda p, i: (i, p))],
        out_shape=[jax.ShapeDtypeStruct((s, MEM_W), F32), jax.ShapeDtypeStruct((s, MEM_W), F32)],
        compiler_params=_ARB2,
    )(qkv, kv, kv)


def _mix_chunk(c, ysb_ref, yfx_ref, ym_ref):
    if c < SB_W // LANES:
        return ysb_ref[:, c * LANES:(c + 1) * LANES]
    c -= SB_W // LANES
    if c < FX_W // LANES:
        return yfx_ref[:, c * LANES:(c + 1) * LANES]
    c -= FX_W // LANES
    return ym_ref[:, c * LANES:(c + 1) * LANES]


def _outproj_fwd(ysb, yfx, ym, gf, onw, wout, x, name):
    s = x.shape[0]

    def body(ysb_ref, yfx_ref, ym_ref, g_ref, onw_ref, w_ref, x_ref, o_ref, yg_ref):
        bd = _head_block_diag()
        for c in range(MIX_W // LANES):
            sl = slice(c * LANES, (c + 1) * LANES)
            u = _mix_chunk(c, ysb_ref, yfx_ref, ym_ref)
            r = lax.rsqrt(_head_mean(u * u, bd) + EPS)
            g = g_ref[:, sl]
            yg_ref[:, sl] = (u * r * onw_ref[:, sl] * (g * _sigmoid(g))).astype(BF16)
        o_ref[...] = x_ref[...] + _dot(yg_ref[...], w_ref[...])

    return pl.pallas_call(
        body, name=name, grid=(s // TM,),
        in_specs=[pl.BlockSpec((TM, SB_W), lambda i: (i, 0)),
                  pl.BlockSpec((TM, FX_W), lambda i: (i, 0)),
                  pl.BlockSpec((TM, MEM_W), lambda i: (i, 0)),
                  pl.BlockSpec((TM, MIX_W), lambda i: (i, 0)),
                  pl.BlockSpec((1, MIX_W), lambda i: (0, 0)),
                  pl.BlockSpec((MIX_W, D_MODEL), lambda i: (0, 0)),
                  pl.BlockSpec((TM, D_MODEL), lambda i: (i, 0))],
        out_specs=pl.BlockSpec((TM, D_MODEL), lambda i: (i, 0)),
        out_shape=jax.ShapeDtypeStruct((s, D_MODEL), F32),
        scratch_shapes=[pltpu.VMEM((TM, MIX_W), BF16)],
        compiler_params=_ARB1,
    )(ysb, yfx, ym, gf, onw, wout, x)


def _final_fwd_bwd(x, fnw, target, name):
    s = x.shape[0]

    def body(x_ref, w_ref, t_ref, dx_ref, loss_ref, dw_ref):
        @pl.when(pl.program_id(0) == 0)
        def _():
            loss_ref[...] = jnp.zeros_like(loss_ref)
            dw_ref[...] = jnp.zeros_like(dw_ref)

        xv = x_ref[...]
        w = w_ref[...]
        r = lax.rsqrt(jnp.mean(xv * xv, axis=-1, keepdims=True) + EPS)
        xh = xv * r
        err = xh * w - t_ref[...]
        part = jnp.sum(jnp.sum(err * err, axis=1, keepdims=True), axis=0, keepdims=True)
        loss_ref[...] += part * (0.5 / D_MODEL)
        dy = err * (1.0 / D_MODEL)
        dw_ref[...] += jnp.sum(dy * xh, axis=0, keepdims=True)
        dxh = dy * w
        dx_ref[...] = r * (dxh - xh * jnp.mean(dxh * xh, axis=-1, keepdims=True))

    return pl.pallas_call(
        body, name=name, grid=(s // TM,),
        in_specs=[pl.BlockSpec((TM, D_MODEL), lambda i: (i, 0)),
                  pl.BlockSpec((1, D_MODEL), lambda i: (0, 0)),
                  pl.BlockSpec((TM, D_MODEL), lambda i: (i, 0))],
        out_specs=[pl.BlockSpec((TM, D_MODEL), lambda i: (i, 0)),
                   pl.BlockSpec((1, LANES), lambda i: (0, 0)),
                   pl.BlockSpec((1, D_MODEL), lambda i: (0, 0))],
        out_shape=[jax.ShapeDtypeStruct((s, D_MODEL), F32), jax.ShapeDtypeStruct((1, LANES), F32),
                   jax.ShapeDtypeStruct((1, D_MODEL), F32)],
        compiler_params=_ARB1,
    )(x, fnw, target)


def _outproj_bwd(dxo, wout, ysb, yfx, ym, gf, onw, name):
    s = dxo.shape[0]

    def body(dx_ref, w_ref, ysb_ref, yfx_ref, ym_ref, g_ref, onw_ref,
             dysb_ref, dyfx_ref, dym_ref, dg_ref, dw_ref, donw_ref, yg_ref):
        @pl.when(pl.program_id(0) == 0)
        def _():
            dw_ref[...] = jnp.zeros_like(dw_ref)
            donw_ref[...] = jnp.zeros_like(donw_ref)

        dxb = dx_ref[...].astype(BF16)
        dyg = _dot_nt(dxb, w_ref[...])
        bd = _head_block_diag()
        for c in range(MIX_W // LANES):
            sl = slice(c * LANES, (c + 1) * LANES)
            u = _mix_chunk(c, ysb_ref, yfx_ref, ym_ref)
            r = lax.rsqrt(_head_mean(u * u, bd) + EPS)
            yn = u * r
            g = g_ref[:, sl]
            sg = _sigmoid(g)
            sil = g * sg
            onw = onw_ref[:, sl]
            e = dyg[:, sl]
            yg_ref[:, sl] = (yn * onw * sil).astype(BF16)
            donw_ref[:, sl] += jnp.sum(e * yn * sil, axis=0, keepdims=True)
            dg_ref[:, sl] = (e * yn * onw * (sg * (1.0 + g * (1.0 - sg)))).astype(BF16)
            dyn = e * onw * sil
            du = (r * (dyn - yn * _head_mean(dyn * yn, bd))).astype(BF16)
            if c < 4:
                dysb_ref[:, c * LANES:(c + 1) * LANES] = du
            elif c < 8:
                dyfx_ref[:, (c - 4) * LANES:(c - 3) * LANES] = du
            else:
                dym_ref[:, (c - 8) * LANES:(c - 7) * LANES] = du
        dw_ref[...] += _dot_tn(yg_ref[...], dxb)

    return pl.pallas_call(
        body, name=name, grid=(s // TM,),
        in_specs=[pl.BlockSpec((TM, D_MODEL), lambda i: (i, 0)),
                  pl.BlockSpec((MIX_W, D_MODEL), lambda i: (0, 0)),
                  pl.BlockSpec((TM, SB_W), lambda i: (i, 0)),
                  pl.BlockSpec((TM, FX_W), lambda i: (i, 0)),
                  pl.BlockSpec((TM, MEM_W), lambda i: (i, 0)),
                  pl.BlockSpec((TM, MIX_W), lambda i: (i, 0)),
                  pl.BlockSpec((1, MIX_W), lambda i: (0, 0))],
        out_specs=[pl.BlockSpec((TM, SB_W), lambda i: (i, 0)),
                   pl.BlockSpec((TM, FX_W), lambda i: (i, 0)),
                   pl.BlockSpec((TM, MEM_W), lambda i: (i, 0)),
                   pl.BlockSpec((TM, MIX_W), lambda i: (i, 0)),
                   pl.BlockSpec((MIX_W, D_MODEL), lambda i: (0, 0)),
                   pl.BlockSpec((1, MIX_W), lambda i: (0, 0))],
        out_shape=[jax.ShapeDtypeStruct((s, SB_W), BF16), jax.ShapeDtypeStruct((s, FX_W), BF16),
                   jax.ShapeDtypeStruct((s, MEM_W), BF16), jax.ShapeDtypeStruct((s, MIX_W), BF16),
                   jax.ShapeDtypeStruct((MIX_W, D_MODEL), F32), jax.ShapeDtypeStruct((1, MIX_W), F32)],
        scratch_shapes=[pltpu.VMEM((TM, MIX_W), BF16)],
        compiler_params=_ARB1,
    )(dxo, wout, ysb, yfx, ym, gf, onw)


def _row_dots(do, o, m0):
    prod = do.astype(F32) * o
    zero = jnp.zeros_like(prod)
    return (jnp.sum(jnp.where(m0, prod, zero), axis=1, keepdims=True),
            jnp.sum(jnp.where(m0, zero, prod), axis=1, keepdims=True))


def _sb_bwd(qkv, o, do, name, rider=None):
    s = qkv.shape[0]
    nq = s // T

    def body(q_ref, k_ref, v_ref, o_ref, do_ref, dq_ref, dk_ref, dv_ref,
             dqa_ref, dka_ref, dva_ref, rl_ref, rg_ref, dzs_ref, abs_ref):
        step_id = pl.program_id(1)

        @pl.when(step_id == 0)
        def _():
            dka_ref[...] = jnp.zeros_like(dka_ref)
            dva_ref[...] = jnp.zeros_like(dva_ref)

        m0 = _pair_masks()
        strict = _iota2((T, T), 0) > _iota2((T, T), 1)
        u2 = _stack2(jnp.where(strict, 1.0, 0.0).astype(BF16))
        hs = range(2)

        def query_tile(i, rows):
            qh = _split_pair(q_ref[rows, :] * jnp.asarray(Q_SCALE, BF16), m0)
            doh = _split_pair(do_ref[rows, :], m0)
            dsum = _row_dots(do_ref[rows, :], o_ref[rows, :], m0)
            dqa_ref[...] = jnp.zeros_like(dqa_ref)
            rl_ref[...] = jnp.zeros_like(rl_ref)
            rg_ref[...] = jnp.zeros_like(rg_ref)

            def flush(j):
                off = pl.multiple_of(j * T, T)
                k = k_ref[pl.ds(off, T), :]
                for h in hs:
                    dqa_ref[h] += _dot(dzs_ref[h], k)
                dka_ref[pl.ds(off, T), :] += (_dot_tn(dzs_ref[0], qh[0])
                                              + _dot_tn(dzs_ref[1], qh[1]))
                dva_ref[pl.ds(off, T), :] += (_dot_tn(abs_ref[0], doh[0])
                                              + _dot_tn(abs_ref[1], doh[1]))

            def tile(j, diag):
                off = pl.multiple_of(j * T, T)
                k = k_ref[pl.ds(off, T), :]
                v = v_ref[pl.ds(off, T), :]
                z = [_dot_nt(qh[h], k) for h in hs]
                da = [_dot_nt(doh[h], v) for h in hs]
                if not diag:
                    flush(j + 1)
                la = [jnp.minimum(z[h], 0.0) - jnp.log(1.0 + jnp.exp(-jnp.abs(z[h]))) for h in hs]
                lf = [la[h] - z[h] for h in hs]
                if diag:
                    lf = [jnp.where(strict, lf[h], 0.0) for h in hs]
                cin = [_cum2(lf[h], u2) for h in hs]
                a = [jnp.exp(la[h] + cin[h] + rl_ref[h]) for h in hs]
                if diag:
                    a = [jnp.where(strict, a[h], 0.0) for h in hs]
                ab = [a[h].astype(BF16) for h in hs]
                g = [ab[h].astype(F32) * da[h] for h in hs]
                gin = [_cum2(g[h], u2) for h in hs]
                dz = [g[h] - jnp.exp(la[h]) * ((dsum[h] - rg_ref[h]) - gin[h]) for h in hs]
                if diag:
                    dz = [jnp.where(strict, dz[h], 0.0) for h in hs]
                for h in hs:
                    rl_ref[h] += cin[h][:, 0:1] + lf[h][:, 0:1]
                    rg_ref[h] += gin[h][:, 0:1] + g[h][:, 0:1]
                    dzs_ref[h] = dz[h].astype(BF16)
                    abs_ref[h] = ab[h]

            tile(i, True)

            def more(state):
                j, top = state
                return jnp.logical_and(j >= 0, top > UNDERFLOW)

            def step(state):
                j, _ = state
                tile(j, False)
                return j - 1, _running_top(rl_ref)

            j_left, _ = lax.while_loop(more, step, (i - 1, _running_top(rl_ref)))
            flush(j_left + 1)
            dq_ref[rows, :] = (jnp.where(m0, dqa_ref[0], dqa_ref[1]) * Q_SCALE).astype(BF16)

        for n in range(QPS):
            query_tile(QPS * step_id + n, slice(n * T, (n + 1) * T))

        @pl.when(step_id == nq // QPS - 1)
        def _():
            dk_ref[...] = dka_ref[...].astype(BF16)
            dv_ref[...] = dva_ref[...].astype(BF16)

    nb = SB_W // LANES
    (dq, dk, dv), lands = _ride(dict(
        body=body, name=name, grid=(nb, nq // QPS),
        in_specs=[pl.BlockSpec((QPS * T, LANES), lambda p, i: (i, p)),
                  pl.BlockSpec((s, LANES), lambda p, i: (0, nb + p)),
                  pl.BlockSpec((s, LANES), lambda p, i: (0, 2 * nb + p)),
                  pl.BlockSpec((QPS * T, LANES), lambda p, i: (i, p)),
                  pl.BlockSpec((QPS * T, LANES), lambda p, i: (i, p))],
        out_specs=[pl.BlockSpec((QPS * T, LANES), lambda p, i: (i, p)),
                   pl.BlockSpec((s, LANES), lambda p, i: (0, p)),
                   pl.BlockSpec((s, LANES), lambda p, i: (0, p))],
        out_shape=[jax.ShapeDtypeStruct((s, SB_W), BF16)] * 3,
        scratch_shapes=[pltpu.VMEM((2, T, LANES), F32), pltpu.VMEM((s, LANES), F32),
                        pltpu.VMEM((s, LANES), F32), pltpu.VMEM((2, T, 1), F32),
                        pltpu.VMEM((2, T, 1), F32), pltpu.VMEM((2, T, T), BF16),
                        pltpu.VMEM((2, T, T), BF16)],
        compiler_params=_ARB2, operands=[qkv, qkv, qkv, o, do]), rider)
    return dq, dk, dv, lands


def _fox_bwd(qkv, fqb, frow, fbounds, o, lse, do, name, rider=None):
    s = qkv.shape[0]
    nq = s // T

    def body(q_ref, k_ref, v_ref, fq_ref, fr_ref, fb_ref, o_ref, lse_ref, do_ref,
             dq_ref, dk_ref, dv_ref, df_ref, dqa_ref, dka_ref, dva_ref, dfa_ref, dls_ref, pbs_ref):
        step_id = pl.program_id(1)

        @pl.when(step_id == 0)
        def _():
            dka_ref[...] = jnp.zeros_like(dka_ref)
            dva_ref[...] = jnp.zeros_like(dva_ref)
            dfa_ref[...] = jnp.zeros_like(dfa_ref)

        m0 = _pair_masks()
        causal = _iota2((T, T), 0) >= _iota2((T, T), 1)
        hs = range(2)

        def query_tile(i, rows):
            qh = _split_pair(q_ref[rows, :] * jnp.asarray(Q_SCALE, BF16), m0)
            doh = _split_pair(do_ref[rows, :], m0)
            dsum = _row_dots(do_ref[rows, :], o_ref[rows, :], m0)
            fq = fq_ref[rows, :]
            fqh = (fq[:, 0:1], fq[:, HEAD_DIM:HEAD_DIM + 1])
            lse = lse_ref[rows, :]
            lseh = (lse[:, 0:1], lse[:, HEAD_DIM:HEAD_DIM + 1])
            dqa_ref[...] = jnp.zeros_like(dqa_ref)

            def flush(j):
                off = pl.multiple_of(j * T, T)
                k = k_ref[pl.ds(off, T), :]
                for h in hs:
                    dqa_ref[h] += _dot(dls_ref[h], k)
                dka_ref[pl.ds(off, T), :] += (_dot_tn(dls_ref[0], qh[0])
                                              + _dot_tn(dls_ref[1], qh[1]))
                dva_ref[pl.ds(off, T), :] += (_dot_tn(pbs_ref[0], doh[0])
                                              + _dot_tn(pbs_ref[1], doh[1]))

            def tile(j, diag):
                off = pl.multiple_of(j * T, T)
                k = k_ref[pl.ds(off, T), :]
                v = v_ref[pl.ds(off, T), :]
                sc = [_dot_nt(qh[h], k) + fqh[h] - fr_ref[h:h + 1, pl.ds(off, T)] for h in hs]
                dp = [_dot_nt(doh[h], v) for h in hs]
                if not diag:
                    flush(j + 1)
                p = [jnp.exp(sc[h] - lseh[h]) for h in hs]
                if diag:
                    p = [jnp.where(causal, p[h], 0.0) for h in hs]
                dl = [p[h] * (dp[h] - dsum[h]) for h in hs]
                for h in hs:
                    dls_ref[h] = dl[h].astype(BF16)
                    pbs_ref[h] = p[h].astype(BF16)
                    dfa_ref[h:h + 1, pl.ds(off, T)] -= jnp.sum(dl[h], axis=0, keepdims=True)

            tile(i, True)
            j_left = _fox_tiles_left(i, pl.program_id(0), nq, fb_ref, tile)
            flush(j_left + 1)
            dq_ref[rows, :] = (jnp.where(m0, dqa_ref[0], dqa_ref[1]) * Q_SCALE).astype(BF16)

        for n in range(QPS):
            query_tile(QPS * step_id + n, slice(n * T, (n + 1) * T))

        @pl.when(step_id == nq // QPS - 1)
        def _():
            dk_ref[...] = dka_ref[...].astype(BF16)
            dv_ref[...] = dva_ref[...].astype(BF16)
            df_ref[...] = dfa_ref[...]

    nb = FX_W // LANES
    base = 3 * SB_W // LANES
    (dq, dk, dv, df), lands = _ride(dict(
        body=body, name=name, grid=(nb, nq // QPS),
        in_specs=[pl.BlockSpec((QPS * T, LANES), lambda p, i: (i, base + p)),
                  pl.BlockSpec((s, LANES), lambda p, i: (0, base + nb + p)),
                  pl.BlockSpec((s, LANES), lambda p, i: (0, base + 2 * nb + p)),
                  pl.BlockSpec((QPS * T, LANES), lambda p, i: (i, p)),
                  pl.BlockSpec((None, 2, s), lambda p, i: (p, 0, 0)),
                  pl.BlockSpec(memory_space=pltpu.SMEM),
                  pl.BlockSpec((QPS * T, LANES), lambda p, i: (i, p)),
                  pl.BlockSpec((QPS * T, LANES), lambda p, i: (i, p)),
                  pl.BlockSpec((QPS * T, LANES), lambda p, i: (i, p))],
        out_specs=[pl.BlockSpec((QPS * T, LANES), lambda p, i: (i, p)),
                   pl.BlockSpec((s, LANES), lambda p, i: (0, p)),
                   pl.BlockSpec((s, LANES), lambda p, i: (0, p)),
                   pl.BlockSpec((None, 2, s), lambda p, i: (p, 0, 0))],
        out_shape=[jax.ShapeDtypeStruct((s, FX_W), BF16)] * 3
        + [jax.ShapeDtypeStruct((nb, 2, s), F32)],
        scratch_shapes=[pltpu.VMEM((2, T, LANES), F32), pltpu.VMEM((s, LANES), F32),
                        pltpu.VMEM((s, LANES), F32), pltpu.VMEM((2, s), F32),
                        pltpu.VMEM((2, T, T), BF16), pltpu.VMEM((2, T, T), BF16)],
        compiler_params=_ARB2, operands=[qkv, qkv, qkv, fqb, frow, fbounds, o, lse, do]), rider)
    return dq, dk, dv, df, lands


def _fox_prep_bwd(dfrow, gf, bpad, name):
    s = gf.shape[0]

    def body(df_ref, fl_ref, b_ref, dfl_ref, db_ref):
        tri = jnp.where(_iota2((T, T), 0) <= _iota2((T, T), 1), 1.0, 0.0).astype(BF16)
        carry = jnp.zeros((1, LANES), F32)
        db = jnp.zeros((1, LANES), F32)
        fill = jnp.zeros((LANES - FOX_HEADS, T), F32)
        for blk in reversed(range(s // T)):
            rows = slice(blk * T, (blk + 1) * T)
            c = _tri3(tri, jnp.concatenate([df_ref[:, rows], fill], axis=0), _dot_nt) + carry
            carry = c[0:1, :]
            dfl = c / (1.0 + jnp.exp(fl_ref[rows, :] + b_ref[...]))
            dfl_ref[rows, :] = dfl.astype(BF16)
            db = db + jnp.sum(dfl, axis=0, keepdims=True)
        db_ref[...] = db

    return pl.pallas_call(
        body, name=name, grid=(1,),
        in_specs=[pl.BlockSpec((FOX_HEADS, s), lambda i: (0, 0)),
                  pl.BlockSpec((s, LANES), lambda i: (0, MIX_W // LANES)),
                  pl.BlockSpec((1, LANES), lambda i: (0, 0))],
        out_specs=[pl.BlockSpec((s, LANES), lambda i: (0, 0)),
                   pl.BlockSpec((1, LANES), lambda i: (0, 0))],
        out_shape=[jax.ShapeDtypeStruct((s, LANES), BF16), jax.ShapeDtypeStruct((1, LANES), F32)],
        compiler_params=_ARB1,
    )(dfrow, gf, bpad)


def _mem_bwd(qkv, kv, o, lse, do, name):
    s = qkv.shape[0]
    n = kv.shape[0]

    def body(q_ref, k_ref, v_ref, o_ref, lse_ref, do_ref, dq_ref, dk_ref, dv_ref):
        @pl.when(pl.program_id(1) == 0)
        def _():
            dk_ref[...] = jnp.zeros_like(dk_ref)
            dv_ref[...] = jnp.zeros_like(dv_ref)

        m0 = _pair_masks()
        qh = _split_pair(q_ref[...] * jnp.asarray(Q_SCALE, BF16), m0)
        doh = _split_pair(do_ref[...], m0)
        dsum = _row_dots(do_ref[...], o_ref[...], m0)
        lse = lse_ref[...]
        lseh = (lse[:, 0:1], lse[:, HEAD_DIM:HEAD_DIM + 1])
        k = k_ref[...]
        v = v_ref[...]
        dqs = []
        for h in range(2):
            p = jnp.exp(_dot_nt(qh[h], k) - lseh[h])
            dl = p * (_dot_nt(doh[h], v) - dsum[h])
            dlb = dl.astype(BF16)
            dqs.append(_dot(dlb, k))
            dk_ref[...] += _dot_tn(dlb, qh[h])
            dv_ref[...] += _dot_tn(p.astype(BF16), doh[h])
        dq_ref[...] = (jnp.where(m0, dqs[0], dqs[1]) * Q_SCALE).astype(BF16)

    nb = MEM_W // LANES
    base = (3 * SB_W + 3 * FX_W) // LANES
    return pl.pallas_call(
        body, name=name, grid=(nb, s // TQM),
        in_specs=[pl.BlockSpec((TQM, LANES), lambda p, i: (i, base + p)),
                  pl.BlockSpec((n, LANES), lambda p, i: (0, p)),
                  pl.BlockSpec((n, LANES), lambda p, i: (0, nb + p)),
                  pl.BlockSpec((TQM, LANES), lambda p, i: (i, p)),
                  pl.BlockSpec((TQM, LANES), lambda p, i: (i, p)),
                  pl.BlockSpec((TQM, LANES), lambda p, i: (i, p))],
        out_specs=[pl.BlockSpec((TQM, LANES), lambda p, i: (i, p)),
                   pl.BlockSpec((n, LANES), lambda p, i: (0, p)),
                   pl.BlockSpec((n, LANES), lambda p, i: (0, p))],
        out_shape=[jax.ShapeDtypeStruct((s, MEM_W), BF16), jax.ShapeDtypeStruct((n, MEM_W), F32),
                   jax.ShapeDtypeStruct((n, MEM_W), F32)],
        compiler_params=_ARB2,
    )(qkv, kv, kv, o, lse, do)


def _memkv_bwd(mem, mnw, wkv, dk, dv, name):
    n = mem.shape[0]

    def body(mem_ref, mnw_ref, w_ref, dk_ref, dv_ref, dw_ref, dmnw_ref):
        mv = mem_ref[...]
        r = lax.rsqrt(jnp.mean(mv * mv, axis=-1, keepdims=True) + EPS)
        mh = mv * r
        hm = (mh * mnw_ref[...]).astype(BF16)
        dkv = jnp.concatenate([dk_ref[...], dv_ref[...]], axis=1).astype(BF16)
        dw_ref[...] = _dot_tn(hm, dkv)
        dhm = _dot_nt(dkv, w_ref[...])
        dmnw_ref[...] = jnp.sum(dhm * mh, axis=0, keepdims=True)

    return pl.pallas_call(
        body, name=name, grid=(1,),
        in_specs=[pl.BlockSpec((n, D_MODEL), lambda i: (0, 0)),
                  pl.BlockSpec((1, D_MODEL), lambda i: (0, 0)),
                  pl.BlockSpec((D_MODEL, 2 * MEM_W), lambda i: (0, 0)),
                  pl.BlockSpec((n, MEM_W), lambda i: (0, 0)),
                  pl.BlockSpec((n, MEM_W), lambda i: (0, 0))],
        out_specs=[pl.BlockSpec((D_MODEL, 2 * MEM_W), lambda i: (0, 0)),
                   pl.BlockSpec((1, D_MODEL), lambda i: (0, 0))],
        out_shape=[jax.ShapeDtypeStruct((D_MODEL, 2 * MEM_W), F32),
                   jax.ShapeDtypeStruct((1, D_MODEL), F32)],
        compiler_params=_ARB1,
    )(mem, mnw, wkv, dk, dv)


def _inproj_bwd_dx(pieces, w_r, x, nw, dxo, name, rider=None):
    s = x.shape[0]
    n = len(pieces)
    widths = [p.shape[1] for p in pieces]

    def body(*refs):
        piece_refs = refs[:n]
        w_ref, x_ref, nw_ref, dxo_ref, dx_ref, h_ref, dnw_ref, dp_ref = refs[n:]

        @pl.when(pl.program_id(0) == 0)
        def _():
            dnw_ref[...] = jnp.zeros_like(dnw_ref)

        col = 0
        for r, wd in zip(piece_refs, widths):
            dp_ref[:, col:col + wd] = r[...]
            col += wd
        dp_ref[:, col:] = jnp.zeros((TM, WR_W - col), BF16)
        dh = _dot(dp_ref[...], w_ref[...])
        xv = x_ref[...]
        nw = nw_ref[...]
        r = lax.rsqrt(jnp.mean(xv * xv, axis=-1, keepdims=True) + EPS)
        xh = xv * r
        h_ref[...] = (xh * nw).astype(BF16)
        dnw_ref[...] += jnp.sum(dh * xh, axis=0, keepdims=True)
        dxh = dh * nw
        dx_ref[...] = r * (dxh - xh * jnp.mean(dxh * xh, axis=-1, keepdims=True)) + dxo_ref[...]

    (dx, h, dnw, dproj), lands = _ride(dict(
        body=body, name=name, grid=(s // TM,),
        in_specs=[pl.BlockSpec((TM, wd), lambda i: (i, 0)) for wd in widths]
        + [pl.BlockSpec((WR_W, D_MODEL), lambda i: (0, 0)),
           pl.BlockSpec((TM, D_MODEL), lambda i: (i, 0)),
           pl.BlockSpec((1, D_MODEL), lambda i: (0, 0)),
           pl.BlockSpec((TM, D_MODEL), lambda i: (i, 0))],
        out_specs=[pl.BlockSpec((TM, D_MODEL), lambda i: (i, 0)),
                   pl.BlockSpec((TM, D_MODEL), lambda i: (i, 0)),
                   pl.BlockSpec((1, D_MODEL), lambda i: (0, 0)),
                   pl.BlockSpec((TM, WR_W), lambda i: (i, 0))],
        out_shape=[jax.ShapeDtypeStruct((s, D_MODEL), F32), jax.ShapeDtypeStruct((s, D_MODEL), BF16),
                   jax.ShapeDtypeStruct((1, D_MODEL), F32), jax.ShapeDtypeStruct((s, WR_W), BF16)],
        scratch_shapes=[], compiler_params=_ARB1, operands=[*pieces, w_r, x, nw, dxo]), rider)
    return dx, h, dnw, dproj, lands


def _inproj_bwd_dw(h, dproj, name):
    s = dproj.shape[0]
    tn = 256

    def body(h_ref, dp_ref, dw_ref):
        dw_ref[...] = _dot_tn(dp_ref[...], h_ref[...])

    return pl.pallas_call(
        body, name=name, grid=(WR_W // tn,),
        in_specs=[pl.BlockSpec((s, D_MODEL), lambda j: (0, 0)),
                  pl.BlockSpec((s, tn), lambda j: (0, j))],
        out_specs=pl.BlockSpec((tn, D_MODEL), lambda j: (j, 0)),
        out_shape=jax.ShapeDtypeStruct((WR_W, D_MODEL), F32),
        compiler_params=_ARB1,
    )(h, dproj)


def _rearrange_w_in(wt):
    pad = jnp.zeros((FL_PAD - FOX_HEADS,) + wt.shape[1:], wt.dtype)
    return jnp.concatenate([wt[:3072], wt[3080:3336], wt[3336:IN_W], wt[3072:3080], pad], axis=0)


def _restore_w_in(g):
    gate0 = QKV_W
    fl0 = QKV_W + MIX_W
    return jnp.concatenate(
        [g[:3072], g[fl0:fl0 + FOX_HEADS], g[3072:QKV_W], g[gate0:fl0]], axis=0)


def _pad_lanes(v, width=LANES):
    return jnp.pad(v, (0, width - v.shape[0])).reshape(1, width)


def _layer_fwd(xs, mem, nw, w_r, b_forget, mnw, late, onw, l, travel=None):
    s = xs.shape[0]
    bpad = _pad_lanes(b_forget)
    qkv, gf = _inproj_fwd(xs, nw, w_r, f"inproj_fwd_{l}")
    fqb, frow, fbounds = _fox_prep_fwd(gf, qkv, bpad, f"fox_prep_fwd_{l}")
    frow = frow.reshape(FOX_HEADS // 2, 2, s)
    travel = _Travel(travel)
    ysb, _ = travel.ride(0, _sb_fwd, qkv, f"sb_fwd_{l}")
    yfx, lse_fx, _ = travel.ride(1, _fox_fwd, qkv, fqb, frow, fbounds, f"fox_fwd_{l}")
    wkv, wout = late(travel.lands)
    kv = _memkv_fwd(mem, mnw, wkv, f"memkv_fwd_{l}")
    ym, lse_m = _mem_fwd(qkv, kv, f"mem_fwd_{l}")
    xn = _outproj_fwd(ysb, yfx, ym, gf, onw, wout, xs, f"outproj_fwd_{l}")
    saved = (xs, nw, mnw, onw, bpad, qkv, gf, fqb, frow, fbounds, ysb, yfx, lse_fx, kv, ym, lse_m)
    return xn, saved, travel.lands, (wkv, wout)


class _Travel:
    def __init__(self, plan):
        self.plan = plan
        self.lands = None if plan is None else _new_lands(plan[0], plan[1])

    def ride(self, n, fn, *args):
        if self.plan is None or n >= len(self.plan[2]) or self.plan[2][n] is None:
            return fn(*args)
        srcs, scatter, legs = self.plan
        idx, rows = legs[n]
        out = fn(*args, rider=_Rider([srcs[a] for a in idx], [self.lands[a] for a in idx],
                                     scatter, rows, relay=not scatter))
        for a, land in zip(idx, out[-1]):
            self.lands[a] = land
        return out


def _layer_bwd(dx, saved, mem, w_r, wkv, wout, l, travel=None):
    xs, nw, mnw, onw, bpad, qkv, gf, fqb, frow, fbounds, ysb, yfx, lse_fx, kv, ym, lse_m = saved
    s = xs.shape[0]
    dysb, dyfx, dym, dgate, dwout, donw = _outproj_bwd(
        dx, wout, ysb, yfx, ym, gf, onw, f"outproj_bwd_{l}")
    travel = _Travel(None if travel is None else travel(dwout))
    sdq, sdk, sdv, _ = travel.ride(0, _sb_bwd, qkv, ysb, dysb, f"sb_bwd_{l}")
    fdq, fdk, fdv, dfrow, _ = travel.ride(1, _fox_bwd, qkv, fqb, frow, fbounds, yfx, lse_fx, dyfx,
                                          f"fox_bwd_{l}")
    dfl, db = _fox_prep_bwd(dfrow.reshape(FOX_HEADS, s), gf, bpad, f"fox_prep_bwd_{l}")
    dmq, dmk, dmv = _mem_bwd(qkv, kv, ym, lse_m, dym, f"mem_bwd_{l}")
    dwkv, dmnw = _memkv_bwd(mem, mnw, wkv, dmk, dmv, f"memkv_bwd_{l}")
    dx, ht, dnw, dproj, _ = travel.ride(2, _inproj_bwd_dx,
                                        [sdq, sdk, sdv, fdq, fdk, fdv, dmq, dgate, dfl],
                                        w_r, xs, nw, dx, f"inproj_bwd_dx_{l}")
    dwr = _inproj_bwd_dw(ht, dproj, f"inproj_bwd_dw_{l}")
    grads = dict(norm_w=dnw[0], w_r=dwr, b_forget=db[0, :FOX_HEADS], mem_norm_w=dmnw[0],
                 w_mem_kv=dwkv, out_norm_w=donw[0], w_out=dwout)
    return dx, grads, travel.lands


_ANY = pl.BlockSpec(memory_space=pl.ANY)


def _my_place():
    return lax.axis_index("x"), lax.axis_index("y"), lax.axis_index("c")


def _flip(v, bit):
    return 1 - v if bit else v


def _block_index(px, py, pc):
    return 4 * px + 2 * py + pc


def _all_gather_weights(shards, name):
    n = len(shards)

    def body(*refs):
        ins, outs = refs[:n], refs[n:2 * n]
        send_sems, recv_sems, local_sems = refs[2 * n:]
        x, y, c = _my_place()
        me = (x, y, c)
        sibling = (x, y, 1 - c)
        chips = [(1 - x, y), (x, 1 - y), (1 - x, 1 - y)]

        def copy(a, k, block, to, src=None):
            dst = outs[a].at[_block_index(*block)]
            return pltpu.make_async_remote_copy(
                src_ref=dst if src is None else src, dst_ref=dst,
                send_sem=send_sems.at[a, k], recv_sem=recv_sems.at[a, k],
                device_id=to, device_id_type=pl.DeviceIdType.MESH)

        mine = [pltpu.make_async_copy(ins[a], outs[a].at[_block_index(*me)], local_sems.at[a])
                for a in range(n)]
        for cp in mine:
            cp.start()
        first = []
        for a in range(n):
            first.append(copy(a, 0, me, sibling, src=ins[a]))
            first += [copy(a, 1 + j, me, (*chip, c), src=ins[a]) for j, chip in enumerate(chips)]
        for cp in first:
            cp.start()
        passed = []
        for j, chip in enumerate(chips):
            for a in range(n):
                copy(a, 1 + j, (*chip, c), me).wait_recv()
                fwd = copy(a, 4 + j, (*chip, c), sibling)
                fwd.start()
                passed.append(fwd)
        for a in range(n):
            copy(a, 0, sibling, me).wait_recv()
            for j, chip in enumerate(chips):
                copy(a, 4 + j, (*chip, 1 - c), me).wait_recv()
        for cp in first + passed:
            cp.wait_send()
        for cp in mine:
            cp.wait()

    return pl.pallas_call(
        body, name=name,
        in_specs=[_ANY] * n, out_specs=[_ANY] * n,
        out_shape=[jax.ShapeDtypeStruct((N_DEV,) + v.shape, v.dtype) for v in shards],
        scratch_shapes=[pltpu.SemaphoreType.DMA((n, 7)), pltpu.SemaphoreType.DMA((n, 7)),
                        pltpu.SemaphoreType.DMA((n,))],
    )(*shards)


def _exchange_blocks(blocked, name):
    n = len(blocked)

    def body(*refs):
        ins, outs = refs[:n], refs[n:2 * n]
        send_sems, recv_sems, local_sems = refs[2 * n:]
        x, y, c = _my_place()
        mine_idx = _block_index(x, y, c)
        local = [pltpu.make_async_copy(ins[a].at[mine_idx], outs[a].at[mine_idx], local_sems.at[a])
                 for a in range(n)]
        for cp in local:
            cp.start()
        sends, arrivals = [], []
        for r in range(1, N_DEV):
            peer = (_flip(x, r & 4), _flip(y, r & 2), _flip(c, r & 1))
            peer_idx = _block_index(*peer)
            for a in range(n):
                sems = dict(send_sem=send_sems.at[a, r - 1], recv_sem=recv_sems.at[a, r - 1],
                            device_id=peer, device_id_type=pl.DeviceIdType.MESH)
                sends.append(pltpu.make_async_remote_copy(
                    src_ref=ins[a].at[peer_idx], dst_ref=outs[a].at[mine_idx], **sems))
                arrivals.append(pltpu.make_async_remote_copy(
                    src_ref=ins[a].at[peer_idx], dst_ref=outs[a].at[peer_idx], **sems))
        for cp in sends:
            cp.start()
        for cp in arrivals:
            cp.wait_recv()
        for cp in sends:
            cp.wait_send()
        for cp in local:
            cp.wait()

    return pl.pallas_call(
        body, name=name,
        in_specs=[_ANY] * n, out_specs=[_ANY] * n,
        out_shape=[jax.ShapeDtypeStruct(v.shape, v.dtype) for v in blocked],
        scratch_shapes=[pltpu.SemaphoreType.DMA((n, 7)), pltpu.SemaphoreType.DMA((n, 7)),
                        pltpu.SemaphoreType.DMA((n,))],
    )(*blocked)


N_CHIP = N_DEV // 2


def _pair_swap(blocked, name):
    n = len(blocked)

    def body(*refs):
        ins, outs = refs[:n], refs[n:2 * n]
        send_sems, recv_sems = refs[2 * n:]
        x, y, c = _my_place()
        copies = [pltpu.make_async_remote_copy(
            src_ref=ins[a].at[j, 1 - c], dst_ref=outs[a].at[j],
            send_sem=send_sems.at[N_CHIP * a + j], recv_sem=recv_sems.at[N_CHIP * a + j],
            device_id=(x, y, 1 - c), device_id_type=pl.DeviceIdType.MESH)
            for a in range(n) for j in range(N_CHIP)]
        for cp in copies:
            cp.start()
        for cp in copies:
            cp.wait_recv()
        for cp in copies:
            cp.wait_send()

    return pl.pallas_call(
        body, name=name,
        in_specs=[_ANY] * n, out_specs=[_ANY] * n,
        out_shape=[jax.ShapeDtypeStruct((N_CHIP,) + v.shape[2:], v.dtype) for v in blocked],
        scratch_shapes=[pltpu.SemaphoreType.DMA((N_CHIP * n,)),
                        pltpu.SemaphoreType.DMA((N_CHIP * n,))],
    )(*blocked)


def _pair_add(mine, theirs, name):
    _, nrow, ncol = mine.shape

    def body(a_ref, b_ref, o_ref):
        o_ref[...] = (a_ref[...].astype(F32) + b_ref[...].astype(F32)).astype(BF16)

    blk = pl.BlockSpec((None, nrow, ncol), lambda j: (j, 0, 0))
    return pl.pallas_call(
        body, name=name, grid=(N_CHIP,), in_specs=[blk, blk], out_specs=blk,
        out_shape=jax.ShapeDtypeStruct(mine.shape, BF16), compiler_params=_ARB1,
    )(mine, theirs)


def _chip_exchange(by_chip, to_all, name):
    n, m = len(by_chip), len(to_all)

    def body(*refs):
        ins, alls = refs[:n], refs[n:n + m]
        outs, all_outs = refs[n + m:2 * n + m], refs[2 * n + m:2 * (n + m)]
        send_sems, recv_sems, local_sems = refs[2 * (n + m):]
        x, y, c = _my_place()
        my_chip, mine_idx = 2 * x + y, _block_index(x, y, c)
        local = [pltpu.make_async_copy(ins[a].at[my_chip], outs[a].at[my_chip], local_sems.at[a])
                 for a in range(n)]
        local += [pltpu.make_async_copy(alls[b].at[mine_idx], all_outs[b].at[mine_idx],
                                        local_sems.at[n + b]) for b in range(m)]
        for cp in local:
            cp.start()
        sends, arrivals = [], []
        k = 0
        for r in range(1, N_DEV):
            peer = (_flip(x, r & 4), _flip(y, r & 2), _flip(c, r & 1))
            peer_chip, peer_idx = 2 * peer[0] + peer[1], _block_index(*peer)
            pairs = [(alls[b].at[mine_idx], all_outs[b].at[mine_idx], all_outs[b].at[peer_idx])
                     for b in range(m)]
            if not r & 1:
                pairs += [(ins[a].at[peer_chip], outs[a].at[my_chip], outs[a].at[peer_chip])
                          for a in range(n)]
            for src, there, here in pairs:
                sems = dict(send_sem=send_sems.at[k], recv_sem=recv_sems.at[k], device_id=peer,
                            device_id_type=pl.DeviceIdType.MESH)
                sends.append(pltpu.make_async_remote_copy(src_ref=src, dst_ref=there, **sems))
                arrivals.append(pltpu.make_async_remote_copy(src_ref=src, dst_ref=here, **sems))
                k += 1
        for cp in sends:
            cp.start()
        for cp in arrivals:
            cp.wait_recv()
        for cp in sends:
            cp.wait_send()
        for cp in local:
            cp.wait()

    n_copies = 7 * m + 3 * n
    return pl.pallas_call(
        body, name=name,
        in_specs=[_ANY] * (n + m), out_specs=[_ANY] * (n + m),
        out_shape=[jax.ShapeDtypeStruct(v.shape, v.dtype) for v in by_chip + to_all],
        scratch_shapes=[pltpu.SemaphoreType.DMA((n_copies,)), pltpu.SemaphoreType.DMA((n_copies,)),
                        pltpu.SemaphoreType.DMA((n + m,))],
    )(*by_chip, *to_all)


class _Rider(NamedTuple):
    srcs: list
    lands: list
    scatter: bool
    part: list
    relay: bool = False


def _window(ref, part):
    if part is None:
        return ref
    dim, start, size = part
    return ref.at[(slice(None),) * dim + (pl.ds(start, size),)]


def _relay_copies(srcs, lands, send_sems, recv_sems, rider):
    x, y, c = _my_place()
    me, sibling = (x, y, c), (x, y, 1 - c)
    chips = [(1 - x, y), (x, 1 - y), (1 - x, 1 - y)]
    first, from_chips, passed, last = [], [], [], []
    for a in range(len(srcs)):
        def copy(k, block, to, src=None, a=a):
            dst = _window(lands[a].at[_block_index(*block)], rider.part[a])
            return pltpu.make_async_remote_copy(
                src_ref=dst if src is None else src, dst_ref=dst,
                send_sem=send_sems.at[7 * a + k], recv_sem=recv_sems.at[7 * a + k],
                device_id=to, device_id_type=pl.DeviceIdType.MESH)

        mine = _window(srcs[a], rider.part[a])
        first.append(copy(0, me, sibling, src=mine))
        first += [copy(1 + j, me, (*chip, c), src=mine) for j, chip in enumerate(chips)]
        from_chips += [copy(1 + j, (*chip, c), me) for j, chip in enumerate(chips)]
        passed += [copy(4 + j, (*chip, c), sibling) for j, chip in enumerate(chips)]
        last.append(copy(0, sibling, me))
        last += [copy(4 + j, (*chip, 1 - c), me) for j, chip in enumerate(chips)]
    return first, from_chips, passed, last


def _new_lands(srcs, scatter):
    return [lax.empty(v.shape if scatter else (N_DEV,) + v.shape, v.dtype) for v in srcs]


def _rider_copies(srcs, lands, send_sems, recv_sems, rider):
    x, y, c = _my_place()
    mine_idx = _block_index(x, y, c)

    def window(ref, a):
        return _window(ref, rider.part[a])

    sends, arrivals = [], []
    for r in range(1, N_DEV):
        peer = (_flip(x, r & 4), _flip(y, r & 2), _flip(c, r & 1))
        peer_idx = _block_index(*peer)
        for a in range(len(srcs)):
            src = window(srcs[a].at[peer_idx] if rider.scatter else srcs[a], a)
            k = 7 * a + r - 1
            sems = dict(send_sem=send_sems.at[k], recv_sem=recv_sems.at[k],
                        device_id=peer, device_id_type=pl.DeviceIdType.MESH)
            sends.append(pltpu.make_async_remote_copy(
                src_ref=src, dst_ref=window(lands[a].at[mine_idx], a), **sems))
            arrivals.append(pltpu.make_async_remote_copy(
                src_ref=src, dst_ref=window(lands[a].at[peer_idx], a), **sems))
    return sends, arrivals


def _ride(call, rider):
    call = dict(call)
    body, grid = call.pop("body"), call["grid"]
    operands = call.pop("operands")
    if rider is None:
        return list(pl.pallas_call(body, **call)(*operands)), None
    n_in, n_out = len(call["in_specs"]), len(call["out_specs"])
    n_scratch = len(call["scratch_shapes"])
    m = len(rider.srcs)

    def riding(*refs):
        main_in, srcs, lands = refs[:n_in], refs[n_in:n_in + m], refs[n_in + m:n_in + 2 * m]
        main_out = refs[n_in + 2 * m:n_in + 2 * m + n_out]
        rest = refs[n_in + 3 * m + n_out:]
        send_sems, recv_sems = rest[n_scratch:]
        at = [pl.program_id(d) for d in range(len(grid))]
        first = functools.reduce(jnp.logical_and, [p == 0 for p in at])
        last = functools.reduce(jnp.logical_and, [p == g - 1 for p, g in zip(at, grid)])
        if rider.relay:
            sends, from_chips, passed, arrivals = _relay_copies(
                srcs, lands, send_sems, recv_sems, rider)
            step, steps = 0, 1
            for p, g in zip(at, grid):
                step, steps = step * g + p, steps * g
            assert steps >= 2, "a relayed gather needs a later grid step to pass blocks on"

            @pl.when(step == (3 * steps) // 4)
            def _():
                for cp in from_chips:
                    cp.wait_recv()
                for cp in passed:
                    cp.start()
        else:
            sends, arrivals = _rider_copies(srcs, lands, send_sems, recv_sems, rider)
            passed = []

        @pl.when(first)
        def _():
            for cp in sends:
                cp.start()

        body(*main_in, *main_out, *rest[:n_scratch])

        @pl.when(last)
        def _():
            for cp in arrivals:
                cp.wait_recv()
            for cp in sends + passed:
                cp.wait_send()

    call["in_specs"] = list(call["in_specs"]) + [_ANY] * (2 * m)
    call["out_specs"] = list(call["out_specs"]) + [_ANY] * m
    call["out_shape"] = list(call["out_shape"]) + [
        jax.ShapeDtypeStruct(v.shape, v.dtype) for v in rider.lands]
    call["scratch_shapes"] = list(call["scratch_shapes"]) + [
        pltpu.SemaphoreType.DMA((7 * m,)), pltpu.SemaphoreType.DMA((7 * m,))]
    call["input_output_aliases"] = {n_in + m + a: n_out + a for a in range(m)}
    outs = pl.pallas_call(riding, **call)(*operands, *rider.srcs, *rider.lands)
    return list(outs[:n_out]), list(outs[n_out:])


def _sum_parts(p_ref):
    g = p_ref[0].astype(F32)
    for k in range(1, p_ref.shape[0]):
        g = g + p_ref[k].astype(F32)
    return g


def _adamw(g, w, m, v):
    c1 = 1.0 / (1.0 - ADAM_B1 ** ADAM_STEP)
    c2 = 1.0 / (1.0 - ADAM_B2 ** ADAM_STEP)
    nm = ADAM_B1 * m + (1.0 - ADAM_B1) * g
    nv = ADAM_B2 * v + (1.0 - ADAM_B2) * (g * g)
    return nm, nv, -ADAM_LR * ((nm * c1) / (jnp.sqrt(nv * c2) + ADAM_EPS) + ADAM_WD * w)


def _adamw_w_in(parts, w, m, v, name):
    ncol_blk, depth, nfeat = w.shape
    cols = 256

    def body(*refs):
        p_refs = refs[:depth]
        w_ref, m_ref, v_ref, g_ref, d_ref, nm_ref, nv_ref = refs[depth:]
        for l in range(depth):
            g = _sum_parts(p_refs[l])
            nm, nv, d = _adamw(g, w_ref[:, l, :], m_ref[:, l, :], v_ref[:, l, :])
            g_ref[:, l, :] = g
            nm_ref[:, l, :] = nm
            nv_ref[:, l, :] = nv
            d_ref[:, l, :] = d

    blk = pl.BlockSpec((ncol_blk, depth, cols), lambda j: (0, 0, j))
    return pl.pallas_call(
        body, name=name, grid=(nfeat // cols,),
        in_specs=[pl.BlockSpec((p.shape[0], ncol_blk, cols), lambda j: (0, 0, j)) for p in parts]
        + [blk] * 3,
        out_specs=[blk] * 4,
        out_shape=[jax.ShapeDtypeStruct(w.shape, F32)] * 4,
        compiler_params=_ARB1,
    )(*parts, w, m, v)


def _adamw_sum(parts, w, m, v, tile, name):
    depth, nrow, ncol = w.shape
    rows, cols = tile

    def body(*refs):
        p_refs = refs[:depth]
        w_ref, m_ref, v_ref, g_ref, d_ref, nm_ref, nv_ref = refs[depth:]
        layer = pl.program_id(0)
        for l in range(depth):
            @pl.when(layer == l)
            def _(p_ref=p_refs[l]):
                g = _sum_parts(p_ref)
                nm, nv, d = _adamw(g, w_ref[...], m_ref[...], v_ref[...])
                g_ref[...] = g
                nm_ref[...] = nm
                nv_ref[...] = nv
                d_ref[...] = d

    def part_spec(l):
        return pl.BlockSpec((parts[l].shape[0], rows, cols), lambda q, i, j: (
            0, jnp.where(q == l, i, 0), jnp.where(q == l, j, 0)))

    blk = pl.BlockSpec((None, rows, cols), lambda q, i, j: (q, i, j))
    return pl.pallas_call(
        body, name=name, grid=(depth, nrow // rows, ncol // cols),
        in_specs=[part_spec(l) for l in range(depth)] + [blk, blk, blk],
        out_specs=[blk] * 4,
        out_shape=[jax.ShapeDtypeStruct(w.shape, F32)] * 4,
        compiler_params=pltpu.CompilerParams(
            dimension_semantics=("arbitrary", "arbitrary", "arbitrary")),
    )(*parts, w, m, v)


def _pack_small(norm_w, mem_norm_w, out_norm_w, final_norm_w, b_forget):
    onw = jnp.pad(out_norm_w.reshape(20, LANES), ((0, 4), (0, 0)))
    b = jnp.pad(b_forget, ((0, 6), (0, LANES - FOX_HEADS)))
    return jnp.concatenate([norm_w.reshape(16, LANES), mem_norm_w.reshape(16, LANES), onw,
                            final_norm_w.reshape(8, LANES), b], axis=0)


def _unpack_small(p):
    return (p[0:16].reshape(2, D_MODEL), p[16:32].reshape(2, D_MODEL), p[32:52].reshape(2, MIX_W),
            p[56:64].reshape(D_MODEL), p[64:66, :FOX_HEADS])


def kernel(x, mem, norm_w, w_in, b_forget, mem_norm_w, w_mem_kv, out_norm_w, w_out, final_norm_w, loss_target, m_norm_w, m_w_in, m_b_forget, m_mem_norm_w, m_w_mem_kv, m_out_norm_w, m_w_out, m_final_norm_w, v_norm_w, v_w_in, v_b_forget, v_mem_norm_w, v_w_mem_kv, v_out_norm_w, v_w_out, v_final_norm_w):
    kv_rows = w_mem_kv.shape[1]
    out_rows = w_out.shape[1]
    me = _block_index(*_my_place())

    def shards(l):
        return [w_in[l].T.astype(BF16), w_mem_kv[l].astype(BF16), w_out[l].astype(BF16)]

    def full_in(g_in):
        return _rearrange_w_in(g_in.reshape(IN_W, D_MODEL))

    def full_kv_out(g_kv, g_out):
        return g_kv.reshape(D_MODEL, 2 * MEM_W), g_out.reshape(MIX_W, D_MODEL)

    def in_blocks(g):
        segments = [(0, 3072, 0), (3072, 3080, QKV_W + MIX_W), (3080, 3336, 3072),
                    (3336, IN_W, QKV_W)]

        def block(k):
            lo, hi = k * SHARD_W, (k + 1) * SHARD_W
            pieces = [g[at + max(lo, a) - a:at + min(hi, b) - a]
                      for a, b, at in segments if max(lo, a) < min(hi, b)]
            return jnp.concatenate(pieces, axis=0).astype(BF16)

        return jnp.stack([block(k) for k in range(N_DEV)])

    def kv_blocks(g):
        return g.reshape(N_DEV, kv_rows, 2 * MEM_W).astype(BF16)

    def out_blocks(g):
        return g.reshape(N_DEV, out_rows, D_MODEL).astype(BF16)

    def with_own(land, own):
        return lax.dynamic_update_slice(land, own[None], (me,) + (0,) * own.ndim)

    def with_own_of(land, blocked):
        return lax.dynamic_update_slice(land, lax.dynamic_slice_in_dim(blocked, me, 1, axis=0),
                                        (me,) + (0,) * (land.ndim - 1))

    def row(v):
        return v.reshape(1, -1)

    def cols(first, size):
        return (1, first, size)

    fwd_split, bwd_split = 4 * LANES, (5 * LANES, 6 * LANES)

    g_in0, g_kv0, g_out0 = _all_gather_weights(shards(0), "all_gather_l0")
    w_r0 = full_in(g_in0)
    s_in1, s_kv1, s_out1 = shards(1)
    x1, saved0, (l_in1,), (wkv0, wout0) = _layer_fwd(
        x[0], mem[0], row(norm_w[0]), w_r0, b_forget[0], row(mem_norm_w[0]),
        lambda lands: full_kv_out(g_kv0, g_out0), row(out_norm_w[0]), 0,
        travel=([s_in1], False, [([0], [cols(0, fwd_split)]),
                                 ([0], [cols(fwd_split, D_MODEL - fwd_split)])]))
    w_r1 = full_in(with_own(l_in1, s_in1))
    x2, saved1, _, (wkv1, wout1) = _layer_fwd(
        x1, mem[0], row(norm_w[1]), w_r1, b_forget[1], row(mem_norm_w[1]),
        lambda lands: full_kv_out(with_own(lands[0], s_kv1), with_own(lands[1], s_out1)),
        row(out_norm_w[1]), 1, travel=([s_kv1, s_out1], False, [([0, 1], [None, None]), None]))

    dx2, loss_part, dfnw = _final_fwd_bwd(x2, row(final_norm_w), loss_target[0], "final_fwd_bwd")

    dx1, gr1, (l_out1,) = _layer_bwd(
        dx2, saved1, mem[0], w_r1, wkv1, wout1, 1,
        travel=lambda dwout: ([out_blocks(dwout)], True, [([0], [None]), None]))
    p_in1, p_kv1 = in_blocks(gr1["w_r"]), kv_blocks(gr1["w_mem_kv"])
    grad_x, gr0, (l_in1, l_kv1, l_out0) = _layer_bwd(
        dx1, saved0, mem[0], w_r0, wkv0, wout0, 0,
        travel=lambda dwout: ([p_in1, p_kv1, out_blocks(dwout)], True,
                              [([0, 1], [cols(0, bwd_split[0]), None]),
                               ([0, 2], [cols(bwd_split[0], bwd_split[1] - bwd_split[0]), None]),
                               ([0], [cols(bwd_split[1], D_MODEL - bwd_split[1])])]))
    r_out1 = with_own_of(l_out1, out_blocks(gr1["w_out"]))
    r_in1, r_kv1 = with_own_of(l_in1, p_in1), with_own_of(l_kv1, p_kv1)
    r_out0 = with_own_of(l_out0, out_blocks(gr0["w_out"]))

    def both(name):
        return jnp.stack([gr0[name], gr1[name]])

    small = _pack_small(both("norm_w"), both("mem_norm_w"), both("out_norm_w"), dfnw[0],
                        both("b_forget")).at[LOSS_ROW].set(loss_part[0])
    p_small = jnp.broadcast_to(small[None], (N_DEV, SMALL_ROWS, LANES))
    by_core = [v.reshape((N_CHIP, 2) + v.shape[1:])
               for v in (in_blocks(gr0["w_r"]), kv_blocks(gr0["w_mem_kv"]))]
    from_sibling = _pair_swap(by_core, "grads_l0_pair_swap")
    core = lax.axis_index("c")
    chip_sums = [_pair_add(lax.dynamic_index_in_dim(v, core, axis=1, keepdims=False), got,
                           f"grads_l0_pair_add_{a}")
                 for a, (v, got) in enumerate(zip(by_core, from_sibling))]
    r_in0, r_kv0, r_small = _chip_exchange(chip_sums, [p_small], "exchange_grads_l0")

    def view(v):
        return jnp.transpose(v, (2, 0, 1))

    g_w_in, d_w_in, nm_w_in, nv_w_in = [jnp.transpose(v, (1, 2, 0)) for v in _adamw_w_in(
        [r_in0, r_in1], view(w_in), view(m_w_in), view(v_w_in), "adamw_w_in")]
    g_w_kv, d_w_kv, nm_w_kv, nv_w_kv = _adamw_sum(
        [r_kv0, r_kv1], w_mem_kv, m_w_mem_kv, v_w_mem_kv, (kv_rows, 2 * MEM_W), "adamw_w_mem_kv")
    g_w_out, d_w_out, nm_w_out, nv_w_out = _adamw_sum(
        [r_out0, r_out1], w_out, m_w_out, v_w_out, (out_rows, D_MODEL), "adamw_w_out")
    w_small = _pack_small(norm_w, mem_norm_w, out_norm_w, final_norm_w, b_forget)[None]
    m_small = _pack_small(m_norm_w, m_mem_norm_w, m_out_norm_w, m_final_norm_w, m_b_forget)[None]
    v_small = _pack_small(v_norm_w, v_mem_norm_w, v_out_norm_w, v_final_norm_w, v_b_forget)[None]
    small_out = _adamw_sum([r_small], w_small, m_small, v_small, (SMALL_ROWS, LANES), "adamw_small")
    (g_nw, g_mnw, g_onw, g_fnw, g_b), (d_nw, d_mnw, d_onw, d_fnw, d_b), \
        (nm_nw, nm_mnw, nm_onw, nm_fnw, nm_b), (nv_nw, nv_mnw, nv_onw, nv_fnw, nv_b) = [
            _unpack_small(t[0]) for t in small_out]
    loss = small_out[0][0, LOSS_ROW, 0]

    return (loss, grad_x[None],
            g_nw, g_w_in, g_b, g_mnw, g_w_kv, g_onw, g_w_out, g_fnw,
            d_nw, d_w_in, d_b, d_mnw, d_w_kv, d_onw, d_w_out, d_fnw,
            nm_nw, nm_w_in, nm_b, nm_mnw, nm_w_kv, nm_onw, nm_w_out, nm_fnw,
            nv_nw, nv_w_in, nv_b, nv_mnw, nv_w_kv, nv_onw, nv_w_out, nv_fnw)
```

```python
import functools
from typing import NamedTuple

import jax
import jax.numpy as jnp
from jax import lax
from jax.experimental import pallas as pl
from jax.experimental.pallas import tpu as pltpu

F32 = jnp.float32
BF16 = jnp.bfloat16

N_DEV = 8
D_MODEL = 1024
HEAD_DIM = 64
LANES = 128
SB_W = 512
FX_W = 512
MEM_W = 256
MIX_W = 1280
FOX_HEADS = 8
IN_W = 4616
SHARD_W = IN_W // N_DEV
QKV_W = 3 * SB_W + 3 * FX_W + MEM_W
FL_PAD = 256
GF_W = MIX_W + FL_PAD
WR_W = QKV_W + GF_W
EPS = 1e-6
T = 256
QPS = 4
TM = 256
TQM = 512
Q_SCALE = 0.125
NEG = -1e30
UNDERFLOW = -110.0
NORM_SLACK = 1.01

ADAM_LR = 0.001
ADAM_B1 = 0.9
ADAM_B2 = 0.999
ADAM_EPS = 1e-08
ADAM_WD = 0.01
ADAM_STEP = 10

SMALL_ROWS = 72
LOSS_ROW = 66

_NT = (((1,), (1,)), ((), ()))
_TN = (((0,), (0,)), ((), ()))

_ARB1 = pltpu.CompilerParams(dimension_semantics=("arbitrary",))
_ARB2 = pltpu.CompilerParams(dimension_semantics=("arbitrary", "arbitrary"))


def _dot(a, b):
    return jnp.dot(a, b, preferred_element_type=F32)


def _dot_nt(a, b):
    return lax.dot_general(a, b, _NT, preferred_element_type=F32)


def _dot_tn(a, b):
    return lax.dot_general(a, b, _TN, preferred_element_type=F32)


def _split2(x):
    hi = x.astype(BF16)
    lo = (x - hi.astype(F32)).astype(BF16)
    return hi, lo


def _stack2(u):
    return jnp.concatenate([u, u], axis=0)


def _cum2(x, u2):
    hi, lo = _split2(x)
    return _dot(jnp.concatenate([hi, lo], axis=1), u2)


def _tri3(tri, x, dot=None):
    dot = dot or _dot
    hi = x.astype(BF16)
    r1 = x - hi.astype(F32)
    mid = r1.astype(BF16)
    lo = (r1 - mid.astype(F32)).astype(BF16)
    return dot(tri, hi) + dot(tri, mid) + dot(tri, lo)


def _iota2(shape, dim):
    return lax.broadcasted_iota(jnp.int32, shape, dim)


def _head_block_diag():
    r = _iota2((LANES, LANES), 0) // HEAD_DIM
    c = _iota2((LANES, LANES), 1) // HEAD_DIM
    return _stack2(jnp.where(r == c, 1.0, 0.0).astype(BF16))


def _head_mean(x, bd):
    return _cum2(x, bd) * (1.0 / HEAD_DIM)


def _sigmoid(x):
    return 1.0 / (1.0 + jnp.exp(-x))


def _log_sigmoid(x):
    return jnp.minimum(x, 0.0) - jnp.log(1.0 + jnp.exp(-jnp.abs(x)))


def _running_top(r_ref):
    return jnp.max(jnp.maximum(r_ref[0], r_ref[1]))


def _fox_tiles_left(i, pair, nq, fb_ref, tile):
    def bound(j):
        b = []
        for h in range(2):
            head = 2 * pair + h
            b.append(2.0 * NORM_SLACK * fb_ref[2 * nq + i, head] * fb_ref[3 * nq, head]
                     + fb_ref[2 * i, head] - fb_ref[2 * j + 1, head])
        return jnp.maximum(b[0], b[1])

    def more(j):
        return jnp.logical_and(j >= 0, bound(jnp.maximum(j, 0)) > UNDERFLOW)

    def step(j):
        tile(j, False)
        return j - 1

    return lax.while_loop(more, step, i - 1)


def _pair_masks():
    lane = _iota2((1, LANES), 1)
    return lane < HEAD_DIM


def _split_pair(x, m0):
    zero = jnp.zeros_like(x)
    return jnp.where(m0, x, zero), jnp.where(m0, zero, x)


def _inproj_fwd(x, nw, w_r, name, rider=None):
    s = x.shape[0]

    def body(x_ref, nw_ref, w_ref, qkv_ref, gf_ref):
        xv = x_ref[...]
        r = lax.rsqrt(jnp.mean(xv * xv, axis=-1, keepdims=True) + EPS)
        h = (xv * r * nw_ref[...]).astype(BF16)
        for c in range(0, QKV_W, 256):
            qkv_ref[:, c:c + 256] = _dot_nt(h, w_ref[c:c + 256, :]).astype(BF16)
        for c in range(0, GF_W, 256):
            gf_ref[:, c:c + 256] = _dot_nt(h, w_ref[QKV_W + c:QKV_W + c + 256, :])

    (qkv, gf), lands = _ride(dict(
        body=body, name=name, grid=(s // TM,),
        in_specs=[pl.BlockSpec((TM, D_MODEL), lambda i: (i, 0)),
                  pl.BlockSpec((1, D_MODEL), lambda i: (0, 0)),
                  pl.BlockSpec((WR_W, D_MODEL), lambda i: (0, 0))],
        out_specs=[pl.BlockSpec((TM, QKV_W), lambda i: (i, 0)),
                   pl.BlockSpec((TM, GF_W), lambda i: (i, 0))],
        out_shape=[jax.ShapeDtypeStruct((s, QKV_W), BF16), jax.ShapeDtypeStruct((s, GF_W), F32)],
        scratch_shapes=[], compiler_params=_ARB1, operands=[x, nw, w_r]), rider)
    return qkv, gf, lands


def _fox_prep_fwd(gf, qkv, bpad, name):
    s = gf.shape[0]
    nq = s // T
    nrow = -(-(3 * nq + 1) // 8) * 8

    def body(fl_ref, q_ref, k_ref, b_ref, fq_ref, fr_ref, fb_ref):
        tri = jnp.where(_iota2((T, T), 0) >= _iota2((T, T), 1), 1.0, 0.0).astype(BF16)
        m0 = _pair_masks()
        lane = _iota2((1, LANES), 1)
        norms = [jnp.zeros((1, LANES), F32) for _ in range(nq + 1)]
        same_head = (_iota2((LANES, LANES), 0) // HEAD_DIM) == (_iota2((LANES, LANES), 1) // HEAD_DIM)
        bd = jnp.where(same_head, 1.0, 0.0).astype(BF16)
        for p in range(FOX_HEADS // 2):
            cols = slice(p * LANES, (p + 1) * LANES)
            q = (q_ref[:, cols] * jnp.asarray(Q_SCALE, BF16)).astype(F32)
            k = k_ref[:, cols].astype(F32)
            qn = _dot((q * q).astype(BF16), bd)
            kn = _dot((k * k).astype(BF16), bd)
            tops = [jnp.max(qn[j * T:(j + 1) * T], axis=0, keepdims=True) for j in range(nq)]
            tops.append(jnp.max(kn, axis=0, keepdims=True))
            tops = [jnp.sqrt(top) for top in tops]
            for h in range(2):
                at = h * HEAD_DIM
                norms = [jnp.where(lane == 2 * p + h, top[:, at:at + 1], row)
                         for top, row in zip(tops, norms)]
        for j in range(nq + 1):
            fb_ref[2 * nq + j:2 * nq + j + 1, :] = norms[j]
        fb_ref[3 * nq + 1:, :] = jnp.zeros((nrow - 3 * nq - 1, LANES), F32)
        carry = jnp.zeros((1, LANES), F32)
        for blk in range(s // T):
            rows = slice(blk * T, (blk + 1) * T)
            lf = _log_sigmoid(fl_ref[rows, :] + b_ref[...])
            c = _tri3(tri, lf) + carry
            carry = c[T - 1:T, :]
            for p in range(FOX_HEADS // 2):
                fq_ref[rows, p * LANES:(p + 1) * LANES] = jnp.where(
                    m0, c[:, 2 * p:2 * p + 1], c[:, 2 * p + 1:2 * p + 2])
            fr_ref[:, rows] = c.T[0:FOX_HEADS, :]
            fb_ref[2 * blk:2 * blk + 1, :] = c[0:1, :]
            fb_ref[2 * blk + 1:2 * blk + 2, :] = carry

    base = 3 * SB_W // FX_W
    return pl.pallas_call(
        body, name=name, grid=(1,),
        in_specs=[pl.BlockSpec((s, LANES), lambda i: (0, MIX_W // LANES)),
                  pl.BlockSpec((s, FX_W), lambda i: (0, base)),
                  pl.BlockSpec((s, FX_W), lambda i: (0, base + 1)),
                  pl.BlockSpec((1, LANES), lambda i: (0, 0))],
        out_specs=[pl.BlockSpec((s, FX_W), lambda i: (0, 0)),
                   pl.BlockSpec((FOX_HEADS, s), lambda i: (0, 0)),
                   pl.BlockSpec((nrow, LANES), lambda i: (0, 0))],
        out_shape=[jax.ShapeDtypeStruct((s, FX_W), F32), jax.ShapeDtypeStruct((FOX_HEADS, s), F32),
                   jax.ShapeDtypeStruct((nrow, LANES), F32)],
        compiler_params=_ARB1,
    )(gf, qkv, qkv, bpad)


def _sb_fwd(qkv, name, rider=None):
    s = qkv.shape[0]

    def body(q_ref, k_ref, v_ref, o_ref, acc_ref, r_ref, as_ref):
        m0 = _pair_masks()
        strict = _iota2((T, T), 0) > _iota2((T, T), 1)
        u2 = _stack2(jnp.where(strict, 1.0, 0.0).astype(BF16))
        hs = range(2)

        def query_tile(i, rows):
            qh = _split_pair(q_ref[rows, :] * jnp.asarray(Q_SCALE, BF16), m0)
            acc_ref[...] = jnp.zeros_like(acc_ref)
            r_ref[...] = jnp.zeros_like(r_ref)

            def flush(j):
                v = v_ref[pl.ds(pl.multiple_of(j * T, T), T), :]
                for h in hs:
                    acc_ref[h] += _dot(as_ref[h], v)

            def tile(j, diag):
                k = k_ref[pl.ds(pl.multiple_of(j * T, T), T), :]
                z = [_dot_nt(qh[h], k) for h in hs]
                if not diag:
                    flush(j + 1)
                la = [jnp.minimum(z[h], 0.0) - jnp.log(1.0 + jnp.exp(-jnp.abs(z[h]))) for h in hs]
                lf = [la[h] - z[h] for h in hs]
                if diag:
                    lf = [jnp.where(strict, lf[h], 0.0) for h in hs]
                cin = [_cum2(lf[h], u2) for h in hs]
                a = [jnp.exp(la[h] + cin[h] + r_ref[h]) for h in hs]
                if diag:
                    a = [jnp.where(strict, a[h], 0.0) for h in hs]
                for h in hs:
                    r_ref[h] += cin[h][:, 0:1] + lf[h][:, 0:1]
                    as_ref[h] = a[h].astype(BF16)

            tile(i, True)

            def more(state):
                j, top = state
                return jnp.logical_and(j >= 0, top > UNDERFLOW)

            def step(state):
                j, _ = state
                tile(j, False)
                return j - 1, _running_top(r_ref)

            j_left, _ = lax.while_loop(more, step, (i - 1, _running_top(r_ref)))
            flush(j_left + 1)
            o_ref[rows, :] = jnp.where(m0, acc_ref[0], acc_ref[1])

        for n in range(QPS):
            query_tile(QPS * pl.program_id(1) + n, slice(n * T, (n + 1) * T))

    nb = SB_W // LANES
    (ysb,), lands = _ride(dict(
        body=body, name=name, grid=(nb, s // (QPS * T)),
        in_specs=[pl.BlockSpec((QPS * T, LANES), lambda p, i: (i, p)),
                  pl.BlockSpec((s, LANES), lambda p, i: (0, nb + p)),
                  pl.BlockSpec((s, LANES), lambda p, i: (0, 2 * nb + p))],
        out_specs=[pl.BlockSpec((QPS * T, LANES), lambda p, i: (i, p))],
        out_shape=[jax.ShapeDtypeStruct((s, SB_W), F32)],
        scratch_shapes=[pltpu.VMEM((2, T, LANES), F32), pltpu.VMEM((2, T, 1), F32),
                        pltpu.VMEM((2, T, T), BF16)],
        compiler_params=_ARB2, operands=[qkv, qkv, qkv]), rider)
    return ysb, lands


def _fox_fwd(qkv, fqb, frow, fbounds, name, rider=None):
    s = qkv.shape[0]

    def body(q_ref, k_ref, v_ref, fq_ref, fr_ref, fb_ref, o_ref, lse_ref, acc_ref, m_ref, ps_ref):
        pair = pl.program_id(0)
        m0 = _pair_masks()
        causal = _iota2((T, T), 0) >= _iota2((T, T), 1)
        ones = jnp.ones((T, LANES), BF16)
        hs = range(2)

        def query_tile(i, rows):
            qh = _split_pair(q_ref[rows, :] * jnp.asarray(Q_SCALE, BF16), m0)
            fq = fq_ref[rows, :]
            fqh = (fq[:, 0:1], fq[:, HEAD_DIM:HEAD_DIM + 1])
            acc_ref[...] = jnp.zeros_like(acc_ref)
            m_ref[...] = jnp.full_like(m_ref, NEG)

            def flush(j):
                v = v_ref[pl.ds(pl.multiple_of(j * T, T), T), :]
                va2 = _stack2(jnp.concatenate([v, ones], axis=1))
                for h in hs:
                    acc_ref[h] += _dot(ps_ref[h], va2)

            def tile(j, diag):
                off = pl.multiple_of(j * T, T)
                k = k_ref[pl.ds(off, T), :]
                sc = [_dot_nt(qh[h], k) + fqh[h] - fr_ref[h:h + 1, pl.ds(off, T)] for h in hs]
                if not diag:
                    flush(j + 1)
                if diag:
                    sc = [jnp.where(causal, sc[h], NEG) for h in hs]
                m_new = [jnp.maximum(m_ref[h], jnp.max(sc[h], axis=1, keepdims=True)) for h in hs]
                p = [jnp.exp(sc[h] - m_new[h]) for h in hs]
                for h in hs:
                    acc_ref[h] = acc_ref[h] * jnp.exp(m_ref[h] - m_new[h])
                    m_ref[h] = m_new[h]
                    p_hi, p_lo = _split2(p[h])
                    ps_ref[h] = jnp.concatenate([p_hi, p_lo], axis=1)

            tile(i, True)
            j_left = _fox_tiles_left(i, pair, s // T, fb_ref, tile)
            flush(j_left + 1)
            acc = (acc_ref[0], acc_ref[1])
            o_ref[rows, :] = jnp.where(m0, acc[0][:, :LANES] / acc[0][:, LANES:],
                                       acc[1][:, :LANES] / acc[1][:, LANES:])
            lse_ref[rows, :] = jnp.where(m0, m_ref[0] + jnp.log(acc[0][:, LANES:]),
                                         m_ref[1] + jnp.log(acc[1][:, LANES:]))

        for n in range(QPS):
            query_tile(QPS * pl.program_id(1) + n, slice(n * T, (n + 1) * T))

    nb = FX_W // LANES
    base = 3 * SB_W // LANES
    (yfx, lse), lands = _ride(dict(
        body=body, name=name, grid=(nb, s // (QPS * T)),
        in_specs=[pl.BlockSpec((QPS * T, LANES), lambda p, i: (i, base + p)),
                  pl.BlockSpec((s, LANES), lambda p, i: (0, base + nb + p)),
                  pl.BlockSpec((s, LANES), lambda p, i: (0, base + 2 * nb + p)),
                  pl.BlockSpec((QPS * T, LANES), lambda p, i: (i, p)),
                  pl.BlockSpec((None, 2, s), lambda p, i: (p, 0, 0)),
                  pl.BlockSpec(memory_space=pltpu.SMEM)],
        out_specs=[pl.BlockSpec((QPS * T, LANES), lambda p, i: (i, p)),
                   pl.BlockSpec((QPS * T, LANES), lambda p, i: (i, p))],
        out_shape=[jax.ShapeDtypeStruct((s, FX_W), F32), jax.ShapeDtypeStruct((s, FX_W), F32)],
        scratch_shapes=[pltpu.VMEM((2, T, 2 * LANES), F32), pltpu.VMEM((2, T, 1), F32),
                        pltpu.VMEM((2, T, 2 * T), BF16)],
        compiler_params=_ARB2, operands=[qkv, qkv, qkv, fqb, frow, fbounds]), rider)
    return yfx, lse, lands


def _memkv_fwd(mem, mnw, wkv, name):
    n = mem.shape[0]

    def body(mem_ref, mnw_ref, w_ref, kv_ref):
        mv = mem_ref[...]
        r = lax.rsqrt(jnp.mean(mv * mv, axis=-1, keepdims=True) + EPS)
        hm = (mv * r * mnw_ref[...]).astype(BF16)
        kv_ref[...] = _dot(hm, w_ref[...]).astype(BF16)

    return pl.pallas_call(
        body, name=name, grid=(1,),
        in_specs=[pl.BlockSpec((n, D_MODEL), lambda i: (0, 0)),
                  pl.BlockSpec((1, D_MODEL), lambda i: (0, 0)),
                  pl.BlockSpec((D_MODEL, 2 * MEM_W), lambda i: (0, 0))],
        out_specs=pl.BlockSpec((n, 2 * MEM_W), lambda i: (0, 0)),
        out_shape=jax.ShapeDtypeStruct((n, 2 * MEM_W), BF16),
        compiler_params=_ARB1,
    )(mem, mnw, wkv)


def _mem_fwd(qkv, kv, name):
    s = qkv.shape[0]
    n = kv.shape[0]

    def body(q_ref, k_ref, v_ref, o_ref, lse_ref):
        m0 = _pair_masks()
        qh = _split_pair(q_ref[...] * jnp.asarray(Q_SCALE, BF16), m0)
        k = k_ref[...]
        v = v_ref[...]
        outs, lses = [], []
        for h in range(2):
            sc = _dot_nt(qh[h], k)
            mx = jnp.max(sc, axis=1, keepdims=True)
            p = jnp.exp(sc - mx)
            l = jnp.sum(p, axis=1, keepdims=True)
            outs.append(_dot(p.astype(BF16), v) / l)
            lses.append(mx + jnp.log(l))
        o_ref[...] = jnp.where(m0, outs[0], outs[1])
        lse_ref[...] = jnp.where(m0, lses[0], lses[1])

    nb = MEM_W // LANES
    base = (3 * SB_W + 3 * FX_W) // LANES
    return pl.pallas_call(
        body, name=name, grid=(nb, s // TQM),
        in_specs=[pl.BlockSpec((TQM, LANES), lambda p, i: (i, base + p)),
                  pl.BlockSpec((n, LANES), lambda p, i: (0, p)),
                  pl.BlockSpec((n, LANES), lambda p, i: (0, nb + p))],
        out_specs=[pl.BlockSpec((TQM, LANES), lambda p, i: (i, p)),
                   pl.BlockSpec((TQM, LANES), lambda p, i: (i, p))],
        out_shape=[jax.ShapeDtypeStruct((s, MEM_W), F32), jax.ShapeDtypeStruct((s, MEM_W), F32)],
        compiler_params=_ARB2,
    )(qkv, kv, kv)


def _mix_chunk(c, ysb_ref, yfx_ref, ym_ref):
    if c < SB_W // LANES:
        return ysb_ref[:, c * LANES:(c + 1) * LANES]
    c -= SB_W // LANES
    if c < FX_W // LANES:
        return yfx_ref[:, c * LANES:(c + 1) * LANES]
    c -= FX_W // LANES
    return ym_ref[:, c * LANES:(c + 1) * LANES]


def _outproj_fwd(ysb, yfx, ym, gf, onw, wout, x, name):
    s = x.shape[0]

    def body(ysb_ref, yfx_ref, ym_ref, g_ref, onw_ref, w_ref, x_ref, o_ref, yg_ref):
        bd = _head_block_diag()
        for c in range(MIX_W // LANES):
            sl = slice(c * LANES, (c + 1) * LANES)
            u = _mix_chunk(c, ysb_ref, yfx_ref, ym_ref)
            r = lax.rsqrt(_head_mean(u * u, bd) + EPS)
            g = g_ref[:, sl]
            yg_ref[:, sl] = (u * r * onw_ref[:, sl] * (g * _sigmoid(g))).astype(BF16)
        o_ref[...] = x_ref[...] + _dot(yg_ref[...], w_ref[...])

    return pl.pallas_call(
        body, name=name, grid=(s // TM,),
        in_specs=[pl.BlockSpec((TM, SB_W), lambda i: (i, 0)),
                  pl.BlockSpec((TM, FX_W), lambda i: (i, 0)),
                  pl.BlockSpec((TM, MEM_W), lambda i: (i, 0)),
                  pl.BlockSpec((TM, MIX_W), lambda i: (i, 0)),
                  pl.BlockSpec((1, MIX_W), lambda i: (0, 0)),
                  pl.BlockSpec((MIX_W, D_MODEL), lambda i: (0, 0)),
                  pl.BlockSpec((TM, D_MODEL), lambda i: (i, 0))],
        out_specs=pl.BlockSpec((TM, D_MODEL), lambda i: (i, 0)),
        out_shape=jax.ShapeDtypeStruct((s, D_MODEL), F32),
        scratch_shapes=[pltpu.VMEM((TM, MIX_W), BF16)],
        compiler_params=_ARB1,
    )(ysb, yfx, ym, gf, onw, wout, x)


def _final_fwd_bwd(x, fnw, target, name):
    s = x.shape[0]

    def body(x_ref, w_ref, t_ref, dx_ref, loss_ref, dw_ref):
        @pl.when(pl.program_id(0) == 0)
        def _():
            loss_ref[...] = jnp.zeros_like(loss_ref)
            dw_ref[...] = jnp.zeros_like(dw_ref)

        xv = x_ref[...]
        w = w_ref[...]
        r = lax.rsqrt(jnp.mean(xv * xv, axis=-1, keepdims=True) + EPS)
        xh = xv * r
        err = xh * w - t_ref[...]
        part = jnp.sum(jnp.sum(err * err, axis=1, keepdims=True), axis=0, keepdims=True)
        loss_ref[...] += part * (0.5 / D_MODEL)
        dy = err * (1.0 / D_MODEL)
        dw_ref[...] += jnp.sum(dy * xh, axis=0, keepdims=True)
        dxh = dy * w
        dx_ref[...] = r * (dxh - xh * jnp.mean(dxh * xh, axis=-1, keepdims=True))

    return pl.pallas_call(
        body, name=name, grid=(s // TM,),
        in_specs=[pl.BlockSpec((TM, D_MODEL), lambda i: (i, 0)),
                  pl.BlockSpec((1, D_MODEL), lambda i: (0, 0)),
                  pl.BlockSpec((TM, D_MODEL), lambda i: (i, 0))],
        out_specs=[pl.BlockSpec((TM, D_MODEL), lambda i: (i, 0)),
                   pl.BlockSpec((1, LANES), lambda i: (0, 0)),
                   pl.BlockSpec((1, D_MODEL), lambda i: (0, 0))],
        out_shape=[jax.ShapeDtypeStruct((s, D_MODEL), F32), jax.ShapeDtypeStruct((1, LANES), F32),
                   jax.ShapeDtypeStruct((1, D_MODEL), F32)],
        compiler_params=_ARB1,
    )(x, fnw, target)


def _outproj_bwd(dxo, wout, ysb, yfx, ym, gf, onw, name):
    s = dxo.shape[0]

    def body(dx_ref, w_ref, ysb_ref, yfx_ref, ym_ref, g_ref, onw_ref,
             dysb_ref, dyfx_ref, dym_ref, dg_ref, dw_ref, donw_ref, yg_ref):
        @pl.when(pl.program_id(0) == 0)
        def _():
            dw_ref[...] = jnp.zeros_like(dw_ref)
            donw_ref[...] = jnp.zeros_like(donw_ref)

        dxb = dx_ref[...].astype(BF16)
        dyg = _dot_nt(dxb, w_ref[...])
        bd = _head_block_diag()
        for c in range(MIX_W // LANES):
            sl = slice(c * LANES, (c + 1) * LANES)
            u = _mix_chunk(c, ysb_ref, yfx_ref, ym_ref)
            r = lax.rsqrt(_head_mean(u * u, bd) + EPS)
            yn = u * r
            g = g_ref[:, sl]
            sg = _sigmoid(g)
            sil = g * sg
            onw = onw_ref[:, sl]
            e = dyg[:, sl]
            yg_ref[:, sl] = (yn * onw * sil).astype(BF16)
            donw_ref[:, sl] += jnp.sum(e * yn * sil, axis=0, keepdims=True)
            dg_ref[:, sl] = (e * yn * onw * (sg * (1.0 + g * (1.0 - sg)))).astype(BF16)
            dyn = e * onw * sil
            du = (r * (dyn - yn * _head_mean(dyn * yn, bd))).astype(BF16)
            if c < 4:
                dysb_ref[:, c * LANES:(c + 1) * LANES] = du
            elif c < 8:
                dyfx_ref[:, (c - 4) * LANES:(c - 3) * LANES] = du
            else:
                dym_ref[:, (c - 8) * LANES:(c - 7) * LANES] = du
        dw_ref[...] += _dot_tn(yg_ref[...], dxb)

    return pl.pallas_call(
        body, name=name, grid=(s // TM,),
        in_specs=[pl.BlockSpec((TM, D_MODEL), lambda i: (i, 0)),
                  pl.BlockSpec((MIX_W, D_MODEL), lambda i: (0, 0)),
                  pl.BlockSpec((TM, SB_W), lambda i: (i, 0)),
                  pl.BlockSpec((TM, FX_W), lambda i: (i, 0)),
                  pl.BlockSpec((TM, MEM_W), lambda i: (i, 0)),
                  pl.BlockSpec((TM, MIX_W), lambda i: (i, 0)),
                  pl.BlockSpec((1, MIX_W), lambda i: (0, 0))],
        out_specs=[pl.BlockSpec((TM, SB_W), lambda i: (i, 0)),
                   pl.BlockSpec((TM, FX_W), lambda i: (i, 0)),
                   pl.BlockSpec((TM, MEM_W), lambda i: (i, 0)),
                   pl.BlockSpec((TM, MIX_W), lambda i: (i, 0)),
                   pl.BlockSpec((MIX_W, D_MODEL), lambda i: (0, 0)),
                   pl.BlockSpec((1, MIX_W), lambda i: (0, 0))],
        out_shape=[jax.ShapeDtypeStruct((s, SB_W), BF16), jax.ShapeDtypeStruct((s, FX_W), BF16),
                   jax.ShapeDtypeStruct((s, MEM_W), BF16), jax.ShapeDtypeStruct((s, MIX_W), BF16),
                   jax.ShapeDtypeStruct((MIX_W, D_MODEL), F32), jax.ShapeDtypeStruct((1, MIX_W), F32)],
        scratch_shapes=[pltpu.VMEM((TM, MIX_W), BF16)],
        compiler_params=_ARB1,
    )(dxo, wout, ysb, yfx, ym, gf, onw)


def _row_dots(do, o, m0):
    prod = do.astype(F32) * o
    zero = jnp.zeros_like(prod)
    return (jnp.sum(jnp.where(m0, prod, zero), axis=1, keepdims=True),
            jnp.sum(jnp.where(m0, zero, prod), axis=1, keepdims=True))


def _sb_bwd(qkv, o, do, name, rider=None):
    s = qkv.shape[0]
    nq = s // T

    def body(q_ref, k_ref, v_ref, o_ref, do_ref, dq_ref, dk_ref, dv_ref,
             dqa_ref, dka_ref, dva_ref, rl_ref, rg_ref, dzs_ref, abs_ref):
        step_id = pl.program_id(1)

        @pl.when(step_id == 0)
        def _():
            dka_ref[...] = jnp.zeros_like(dka_ref)
            dva_ref[...] = jnp.zeros_like(dva_ref)

        m0 = _pair_masks()
        strict = _iota2((T, T), 0) > _iota2((T, T), 1)
        u2 = _stack2(jnp.where(strict, 1.0, 0.0).astype(BF16))
        hs = range(2)

        def query_tile(i, rows):
            qh = _split_pair(q_ref[rows, :] * jnp.asarray(Q_SCALE, BF16), m0)
            doh = _split_pair(do_ref[rows, :], m0)
            dsum = _row_dots(do_ref[rows, :], o_ref[rows, :], m0)
            dqa_ref[...] = jnp.zeros_like(dqa_ref)
            rl_ref[...] = jnp.zeros_like(rl_ref)
            rg_ref[...] = jnp.zeros_like(rg_ref)

            def flush(j):
                off = pl.multiple_of(j * T, T)
                k = k_ref[pl.ds(off, T), :]
                for h in hs:
                    dqa_ref[h] += _dot(dzs_ref[h], k)
                dka_ref[pl.ds(off, T), :] += (_dot_tn(dzs_ref[0], qh[0])
                                              + _dot_tn(dzs_ref[1], qh[1]))
                dva_ref[pl.ds(off, T), :] += (_dot_tn(abs_ref[0], doh[0])
                                              + _dot_tn(abs_ref[1], doh[1]))

            def tile(j, diag):
                off = pl.multiple_of(j * T, T)
                k = k_ref[pl.ds(off, T), :]
                v = v_ref[pl.ds(off, T), :]
                z = [_dot_nt(qh[h], k) for h in hs]
                da = [_dot_nt(doh[h], v) for h in hs]
                if not diag:
                    flush(j + 1)
                la = [jnp.minimum(z[h], 0.0) - jnp.log(1.0 + jnp.exp(-jnp.abs(z[h]))) for h in hs]
                lf = [la[h] - z[h] for h in hs]
                if diag:
                    lf = [jnp.where(strict, lf[h], 0.0) for h in hs]
                cin = [_cum2(lf[h], u2) for h in hs]
                a = [jnp.exp(la[h] + cin[h] + rl_ref[h]) for h in hs]
                if diag:
                    a = [jnp.where(strict, a[h], 0.0) for h in hs]
                ab = [a[h].astype(BF16) for h in hs]
                g = [ab[h].astype(F32) * da[h] for h in hs]
                gin = [_cum2(g[h], u2) for h in hs]
                dz = [g[h] - jnp.exp(la[h]) * ((dsum[h] - rg_ref[h]) - gin[h]) for h in hs]
                if diag:
                    dz = [jnp.where(strict, dz[h], 0.0) for h in hs]
                for h in hs:
                    rl_ref[h] += cin[h][:, 0:1] + lf[h][:, 0:1]
                    rg_ref[h] += gin[h][:, 0:1] + g[h][:, 0:1]
                    dzs_ref[h] = dz[h].astype(BF16)
                    abs_ref[h] = ab[h]

            tile(i, True)

            def more(state):
                j, top = state
                return jnp.logical_and(j >= 0, top > UNDERFLOW)

            def step(state):
                j, _ = state
                tile(j, False)
                return j - 1, _running_top(rl_ref)

            j_left, _ = lax.while_loop(more, step, (i - 1, _running_top(rl_ref)))
            flush(j_left + 1)
            dq_ref[rows, :] = (jnp.where(m0, dqa_ref[0], dqa_ref[1]) * Q_SCALE).astype(BF16)

        for n in range(QPS):
            query_tile(QPS * step_id + n, slice(n * T, (n + 1) * T))

        @pl.when(step_id == nq // QPS - 1)
        def _():
            dk_ref[...] = dka_ref[...].astype(BF16)
            dv_ref[...] = dva_ref[...].astype(BF16)

    nb = SB_W // LANES
    (dq, dk, dv), lands = _ride(dict(
        body=body, name=name, grid=(nb, nq // QPS),
        in_specs=[pl.BlockSpec((QPS * T, LANES), lambda p, i: (i, p)),
                  pl.BlockSpec((s, LANES), lambda p, i: (0, nb + p)),
                  pl.BlockSpec((s, LANES), lambda p, i: (0, 2 * nb + p)),
                  pl.BlockSpec((QPS * T, LANES), lambda p, i: (i, p)),
                  pl.BlockSpec((QPS * T, LANES), lambda p, i: (i, p))],
        out_specs=[pl.BlockSpec((QPS * T, LANES), lambda p, i: (i, p)),
                   pl.BlockSpec((s, LANES), lambda p, i: (0, p)),
                   pl.BlockSpec((s, LANES), lambda p, i: (0, p))],
        out_shape=[jax.ShapeDtypeStruct((s, SB_W), BF16)] * 3,
        scratch_shapes=[pltpu.VMEM((2, T, LANES), F32), pltpu.VMEM((s, LANES), F32),
                        pltpu.VMEM((s, LANES), F32), pltpu.VMEM((2, T, 1), F32),
                        pltpu.VMEM((2, T, 1), F32), pltpu.VMEM((2, T, T), BF16),
                        pltpu.VMEM((2, T, T), BF16)],
        compiler_params=_ARB2, operands=[qkv, qkv, qkv, o, do]), rider)
    return dq, dk, dv, lands


def _fox_bwd(qkv, fqb, frow, fbounds, o, lse, do, name, rider=None):
    s = qkv.shape[0]
    nq = s // T

    def body(q_ref, k_ref, v_ref, fq_ref, fr_ref, fb_ref, o_ref, lse_ref, do_ref,
             dq_ref, dk_ref, dv_ref, df_ref, dqa_ref, dka_ref, dva_ref, dfa_ref, dls_ref, pbs_ref):
        step_id = pl.program_id(1)

        @pl.when(step_id == 0)
        def _():
            dka_ref[...] = jnp.zeros_like(dka_ref)
            dva_ref[...] = jnp.zeros_like(dva_ref)
            dfa_ref[...] = jnp.zeros_like(dfa_ref)

        m0 = _pair_masks()
        causal = _iota2((T, T), 0) >= _iota2((T, T), 1)
        hs = range(2)

        def query_tile(i, rows):
            qh = _split_pair(q_ref[rows, :] * jnp.asarray(Q_SCALE, BF16), m0)
            doh = _split_pair(do_ref[rows, :], m0)
            dsum = _row_dots(do_ref[rows, :], o_ref[rows, :], m0)
            fq = fq_ref[rows, :]
            fqh = (fq[:, 0:1], fq[:, HEAD_DIM:HEAD_DIM + 1])
            lse = lse_ref[rows, :]
            lseh = (lse[:, 0:1], lse[:, HEAD_DIM:HEAD_DIM + 1])
            dqa_ref[...] = jnp.zeros_like(dqa_ref)

            def flush(j):
                off = pl.multiple_of(j * T, T)
                k = k_ref[pl.ds(off, T), :]
                for h in hs:
                    dqa_ref[h] += _dot(dls_ref[h], k)
                dka_ref[pl.ds(off, T), :] += (_dot_tn(dls_ref[0], qh[0])
                                              + _dot_tn(dls_ref[1], qh[1]))
                dva_ref[pl.ds(off, T), :] += (_dot_tn(pbs_ref[0], doh[0])
                                              + _dot_tn(pbs_ref[1], doh[1]))

            def tile(j, diag):
                off = pl.multiple_of(j * T, T)
                k = k_ref[pl.ds(off, T), :]
                v = v_ref[pl.ds(off, T), :]
                sc = [_dot_nt(qh[h], k) + fqh[h] - fr_ref[h:h + 1, pl.ds(off, T)] for h in hs]
                dp = [_dot_nt(doh[h], v) for h in hs]
                if not diag:
                    flush(j + 1)
                p = [jnp.exp(sc[h] - lseh[h]) for h in hs]
                if diag:
                    p = [jnp.where(causal, p[h], 0.0) for h in hs]
                dl = [p[h] * (dp[h] - dsum[h]) for h in hs]
                for h in hs:
                    dls_ref[h] = dl[h].astype(BF16)
                    pbs_ref[h] = p[h].astype(BF16)
                    dfa_ref[h:h + 1, pl.ds(off, T)] -= jnp.sum(dl[h], axis=0, keepdims=True)

            tile(i, True)
            j_left = _fox_tiles_left(i, pl.program_id(0), nq, fb_ref, tile)
            flush(j_left + 1)
            dq_ref[rows, :] = (jnp.where(m0, dqa_ref[0], dqa_ref[1]) * Q_SCALE).astype(BF16)

        for n in range(QPS):
            query_tile(QPS * step_id + n, slice(n * T, (n + 1) * T))

        @pl.when(step_id == nq // QPS - 1)
        def _():
            dk_ref[...] = dka_ref[...].astype(BF16)
            dv_ref[...] = dva_ref[...].astype(BF16)
            df_ref[...] = dfa_ref[...]

    nb = FX_W // LANES
    base = 3 * SB_W // LANES
    (dq, dk, dv, df), lands = _ride(dict(
        body=body, name=name, grid=(nb, nq // QPS),
        in_specs=[pl.BlockSpec((QPS * T, LANES), lambda p, i: (i, base + p)),
                  pl.BlockSpec((s, LANES), lambda p, i: (0, base + nb + p)),
                  pl.BlockSpec((s, LANES), lambda p, i: (0, base + 2 * nb + p)),
                  pl.BlockSpec((QPS * T, LANES), lambda p, i: (i, p)),
                  pl.BlockSpec((None, 2, s), lambda p, i: (p, 0, 0)),
                  pl.BlockSpec(memory_space=pltpu.SMEM),
                  pl.BlockSpec((QPS * T, LANES), lambda p, i: (i, p)),
                  pl.BlockSpec((QPS * T, LANES), lambda p, i: (i, p)),
                  pl.BlockSpec((QPS * T, LANES), lambda p, i: (i, p))],
        out_specs=[pl.BlockSpec((QPS * T, LANES), lambda p, i: (i, p)),
                   pl.BlockSpec((s, LANES), lambda p, i: (0, p)),
                   pl.BlockSpec((s, LANES), lambda p, i: (0, p)),
                   pl.BlockSpec((None, 2, s), lambda p, i: (p, 0, 0))],
        out_shape=[jax.ShapeDtypeStruct((s, FX_W), BF16)] * 3
        + [jax.ShapeDtypeStruct((nb, 2, s), F32)],
        scratch_shapes=[pltpu.VMEM((2, T, LANES), F32), pltpu.VMEM((s, LANES), F32),
                        pltpu.VMEM((s, LANES), F32), pltpu.VMEM((2, s), F32),
                        pltpu.VMEM((2, T, T), BF16), pltpu.VMEM((2, T, T), BF16)],
        compiler_params=_ARB2, operands=[qkv, qkv, qkv, fqb, frow, fbounds, o, lse, do]), rider)
    return dq, dk, dv, df, lands


def _fox_prep_bwd(dfrow, gf, bpad, name):
    s = gf.shape[0]

    def body(df_ref, fl_ref, b_ref, dfl_ref, db_ref):
        tri = jnp.where(_iota2((T, T), 0) <= _iota2((T, T), 1), 1.0, 0.0).astype(BF16)
        carry = jnp.zeros((1, LANES), F32)
        db = jnp.zeros((1, LANES), F32)
        fill = jnp.zeros((LANES - FOX_HEADS, T), F32)
        for blk in reversed(range(s // T)):
            rows = slice(blk * T, (blk + 1) * T)
            c = _tri3(tri, jnp.concatenate([df_ref[:, rows], fill], axis=0), _dot_nt) + carry
            carry = c[0:1, :]
            dfl = c / (1.0 + jnp.exp(fl_ref[rows, :] + b_ref[...]))
            dfl_ref[rows, :] = dfl.astype(BF16)
            db = db + jnp.sum(dfl, axis=0, keepdims=True)
        db_ref[...] = db

    return pl.pallas_call(
        body, name=name, grid=(1,),
        in_specs=[pl.BlockSpec((FOX_HEADS, s), lambda i: (0, 0)),
                  pl.BlockSpec((s, LANES), lambda i: (0, MIX_W // LANES)),
                  pl.BlockSpec((1, LANES), lambda i: (0, 0))],
        out_specs=[pl.BlockSpec((s, LANES), lambda i: (0, 0)),
                   pl.BlockSpec((1, LANES), lambda i: (0, 0))],
        out_shape=[jax.ShapeDtypeStruct((s, LANES), BF16), jax.ShapeDtypeStruct((1, LANES), F32)],
        compiler_params=_ARB1,
    )(dfrow, gf, bpad)


def _mem_bwd(qkv, kv, o, lse, do, name):
    s = qkv.shape[0]
    n = kv.shape[0]

    def body(q_ref, k_ref, v_ref, o_ref, lse_ref, do_ref, dq_ref, dk_ref, dv_ref):
        @pl.when(pl.program_id(1) == 0)
        def _():
            dk_ref[...] = jnp.zeros_like(dk_ref)
            dv_ref[...] = jnp.zeros_like(dv_ref)

        m0 = _pair_masks()
        qh = _split_pair(q_ref[...] * jnp.asarray(Q_SCALE, BF16), m0)
        doh = _split_pair(do_ref[...], m0)
        dsum = _row_dots(do_ref[...], o_ref[...], m0)
        lse = lse_ref[...]
        lseh = (lse[:, 0:1], lse[:, HEAD_DIM:HEAD_DIM + 1])
        k = k_ref[...]
        v = v_ref[...]
        dqs = []
        for h in range(2):
            p = jnp.exp(_dot_nt(qh[h], k) - lseh[h])
            dl = p * (_dot_nt(doh[h], v) - dsum[h])
            dlb = dl.astype(BF16)
            dqs.append(_dot(dlb, k))
            dk_ref[...] += _dot_tn(dlb, qh[h])
            dv_ref[...] += _dot_tn(p.astype(BF16), doh[h])
        dq_ref[...] = (jnp.where(m0, dqs[0], dqs[1]) * Q_SCALE).astype(BF16)

    nb = MEM_W // LANES
    base = (3 * SB_W + 3 * FX_W) // LANES
    return pl.pallas_call(
        body, name=name, grid=(nb, s // TQM),
        in_specs=[pl.BlockSpec((TQM, LANES), lambda p, i: (i, base + p)),
                  pl.BlockSpec((n, LANES), lambda p, i: (0, p)),
                  pl.BlockSpec((n, LANES), lambda p, i: (0, nb + p)),
                  pl.BlockSpec((TQM, LANES), lambda p, i: (i, p)),
                  pl.BlockSpec((TQM, LANES), lambda p, i: (i, p)),
                  pl.BlockSpec((TQM, LANES), lambda p, i: (i, p))],
        out_specs=[pl.BlockSpec((TQM, LANES), lambda p, i: (i, p)),
                   pl.BlockSpec((n, LANES), lambda p, i: (0, p)),
                   pl.BlockSpec((n, LANES), lambda p, i: (0, p))],
        out_shape=[jax.ShapeDtypeStruct((s, MEM_W), BF16), jax.ShapeDtypeStruct((n, MEM_W), F32),
                   jax.ShapeDtypeStruct((n, MEM_W), F32)],
        compiler_params=_ARB2,
    )(qkv, kv, kv, o, lse, do)


def _memkv_bwd(mem, mnw, wkv, dk, dv, name):
    n = mem.shape[0]

    def body(mem_ref, mnw_ref, w_ref, dk_ref, dv_ref, dw_ref, dmnw_ref):
        mv = mem_ref[...]
        r = lax.rsqrt(jnp.mean(mv * mv, axis=-1, keepdims=True) + EPS)
        mh = mv * r
        hm = (mh * mnw_ref[...]).astype(BF16)
        dkv = jnp.concatenate([dk_ref[...], dv_ref[...]], axis=1).astype(BF16)
        dw_ref[...] = _dot_tn(hm, dkv)
        dhm = _dot_nt(dkv, w_ref[...])
        dmnw_ref[...] = jnp.sum(dhm * mh, axis=0, keepdims=True)

    return pl.pallas_call(
        body, name=name, grid=(1,),
        in_specs=[pl.BlockSpec((n, D_MODEL), lambda i: (0, 0)),
                  pl.BlockSpec((1, D_MODEL), lambda i: (0, 0)),
                  pl.BlockSpec((D_MODEL, 2 * MEM_W), lambda i: (0, 0)),
                  pl.BlockSpec((n, MEM_W), lambda i: (0, 0)),
                  pl.BlockSpec((n, MEM_W), lambda i: (0, 0))],
        out_specs=[pl.BlockSpec((D_MODEL, 2 * MEM_W), lambda i: (0, 0)),
                   pl.BlockSpec((1, D_MODEL), lambda i: (0, 0))],
        out_shape=[jax.ShapeDtypeStruct((D_MODEL, 2 * MEM_W), F32),
                   jax.ShapeDtypeStruct((1, D_MODEL), F32)],
        compiler_params=_ARB1,
    )(mem, mnw, wkv, dk, dv)


def _inproj_bwd_dx(pieces, w_r, x, nw, dxo, name, rider=None):
    s = x.shape[0]
    n = len(pieces)
    widths = [p.shape[1] for p in pieces]

    def body(*refs):
        piece_refs = refs[:n]
        w_ref, x_ref, nw_ref, dxo_ref, dx_ref, h_ref, dnw_ref, dp_ref = refs[n:]

        @pl.when(pl.program_id(0) == 0)
        def _():
            dnw_ref[...] = jnp.zeros_like(dnw_ref)

        col = 0
        for r, wd in zip(piece_refs, widths):
            dp_ref[:, col:col + wd] = r[...]
            col += wd
        dp_ref[:, col:] = jnp.zeros((TM, WR_W - col), BF16)
        dh = _dot(dp_ref[...], w_ref[...])
        xv = x_ref[...]
        nw = nw_ref[...]
        r = lax.rsqrt(jnp.mean(xv * xv, axis=-1, keepdims=True) + EPS)
        xh = xv * r
        h_ref[...] = (xh * nw).astype(BF16)
        dnw_ref[...] += jnp.sum(dh * xh, axis=0, keepdims=True)
        dxh = dh * nw
        dx_ref[...] = r * (dxh - xh * jnp.mean(dxh * xh, axis=-1, keepdims=True)) + dxo_ref[...]

    (dx, h, dnw, dproj), lands = _ride(dict(
        body=body, name=name, grid=(s // TM,),
        in_specs=[pl.BlockSpec((TM, wd), lambda i: (i, 0)) for wd in widths]
        + [pl.BlockSpec((WR_W, D_MODEL), lambda i: (0, 0)),
           pl.BlockSpec((TM, D_MODEL), lambda i: (i, 0)),
           pl.BlockSpec((1, D_MODEL), lambda i: (0, 0)),
           pl.BlockSpec((TM, D_MODEL), lambda i: (i, 0))],
        out_specs=[pl.BlockSpec((TM, D_MODEL), lambda i: (i, 0)),
                   pl.BlockSpec((TM, D_MODEL), lambda i: (i, 0)),
                   pl.BlockSpec((1, D_MODEL), lambda i: (0, 0)),
                   pl.BlockSpec((TM, WR_W), lambda i: (i, 0))],
        out_shape=[jax.ShapeDtypeStruct((s, D_MODEL), F32), jax.ShapeDtypeStruct((s, D_MODEL), BF16),
                   jax.ShapeDtypeStruct((1, D_MODEL), F32), jax.ShapeDtypeStruct((s, WR_W), BF16)],
        scratch_shapes=[], compiler_params=_ARB1, operands=[*pieces, w_r, x, nw, dxo]), rider)
    return dx, h, dnw, dproj, lands


def _inproj_bwd_dw(h, dproj, name):
    s = dproj.shape[0]
    tn = 256

    def body(h_ref, dp_ref, dw_ref):
        dw_ref[...] = _dot_tn(dp_ref[...], h_ref[...])

    return pl.pallas_call(
        body, name=name, grid=(WR_W // tn,),
        in_specs=[pl.BlockSpec((s, D_MODEL), lambda j: (0, 0)),
                  pl.BlockSpec((s, tn), lambda j: (0, j))],
        out_specs=pl.BlockSpec((tn, D_MODEL), lambda j: (j, 0)),
        out_shape=jax.ShapeDtypeStruct((WR_W, D_MODEL), F32),
        compiler_params=_ARB1,
    )(h, dproj)


def _rearrange_w_in(wt):
    pad = jnp.zeros((FL_PAD - FOX_HEADS,) + wt.shape[1:], wt.dtype)
    return jnp.concatenate([wt[:3072], wt[3080:3336], wt[3336:IN_W], wt[3072:3080], pad], axis=0)


def _restore_w_in(g):
    gate0 = QKV_W
    fl0 = QKV_W + MIX_W
    return jnp.concatenate(
        [g[:3072], g[fl0:fl0 + FOX_HEADS], g[3072:QKV_W], g[gate0:fl0]], axis=0)


def _pad_lanes(v, width=LANES):
    return jnp.pad(v, (0, width - v.shape[0])).reshape(1, width)


def _layer_fwd(xs, mem, nw, w_r, b_forget, mnw, late, onw, l, travel=None):
    s = xs.shape[0]
    bpad = _pad_lanes(b_forget)
    travel = _Travel(travel)
    qkv, gf, _ = travel.ride(2, _inproj_fwd, xs, nw, w_r, f"inproj_fwd_{l}")
    fqb, frow, fbounds = _fox_prep_fwd(gf, qkv, bpad, f"fox_prep_fwd_{l}")
    frow = frow.reshape(FOX_HEADS // 2, 2, s)
    ysb, _ = travel.ride(0, _sb_fwd, qkv, f"sb_fwd_{l}")
    yfx, lse_fx, _ = travel.ride(1, _fox_fwd, qkv, fqb, frow, fbounds, f"fox_fwd_{l}")
    wkv, wout = late(travel.lands)
    kv = _memkv_fwd(mem, mnw, wkv, f"memkv_fwd_{l}")
    ym, lse_m = _mem_fwd(qkv, kv, f"mem_fwd_{l}")
    xn = _outproj_fwd(ysb, yfx, ym, gf, onw, wout, xs, f"outproj_fwd_{l}")
    saved = (xs, nw, mnw, onw, bpad, qkv, gf, fqb, frow, fbounds, ysb, yfx, lse_fx, kv, ym, lse_m)
    return xn, saved, travel.lands, (wkv, wout)


class _Travel:
    def __init__(self, plan):
        self.plan = plan
        self.lands = None if plan is None else _new_lands(plan[0], plan[1])

    def ride(self, n, fn, *args):
        if self.plan is None or n >= len(self.plan[2]) or self.plan[2][n] is None:
            return fn(*args)
        srcs, scatter, legs = self.plan
        idx, rows = legs[n]
        out = fn(*args, rider=_Rider([srcs[a] for a in idx], [self.lands[a] for a in idx],
                                     scatter, rows, relay=not scatter))
        for a, land in zip(idx, out[-1]):
            self.lands[a] = land
        return out


def _layer_bwd(dx, saved, mem, w_r, wkv, wout, l, travel=None):
    xs, nw, mnw, onw, bpad, qkv, gf, fqb, frow, fbounds, ysb, yfx, lse_fx, kv, ym, lse_m = saved
    s = xs.shape[0]
    dysb, dyfx, dym, dgate, dwout, donw = _outproj_bwd(
        dx, wout, ysb, yfx, ym, gf, onw, f"outproj_bwd_{l}")
    travel = _Travel(None if travel is None else travel(dwout))
    sdq, sdk, sdv, _ = travel.ride(0, _sb_bwd, qkv, ysb, dysb, f"sb_bwd_{l}")
    fdq, fdk, fdv, dfrow, _ = travel.ride(1, _fox_bwd, qkv, fqb, frow, fbounds, yfx, lse_fx, dyfx,
                                          f"fox_bwd_{l}")
    dfl, db = _fox_prep_bwd(dfrow.reshape(FOX_HEADS, s), gf, bpad, f"fox_prep_bwd_{l}")
    dmq, dmk, dmv = _mem_bwd(qkv, kv, ym, lse_m, dym, f"mem_bwd_{l}")
    dwkv, dmnw = _memkv_bwd(mem, mnw, wkv, dmk, dmv, f"memkv_bwd_{l}")
    dx, ht, dnw, dproj, _ = travel.ride(2, _inproj_bwd_dx,
                                        [sdq, sdk, sdv, fdq, fdk, fdv, dmq, dgate, dfl],
                                        w_r, xs, nw, dx, f"inproj_bwd_dx_{l}")
    dwr = _inproj_bwd_dw(ht, dproj, f"inproj_bwd_dw_{l}")
    grads = dict(norm_w=dnw[0], w_r=dwr, b_forget=db[0, :FOX_HEADS], mem_norm_w=dmnw[0],
                 w_mem_kv=dwkv, out_norm_w=donw[0], w_out=dwout)
    return dx, grads, travel.lands


_ANY = pl.BlockSpec(memory_space=pl.ANY)


def _my_place():
    return lax.axis_index("x"), lax.axis_index("y"), lax.axis_index("c")


def _flip(v, bit):
    return 1 - v if bit else v


def _block_index(px, py, pc):
    return 4 * px + 2 * py + pc


def _all_gather_weights(shards, name):
    n = len(shards)

    def body(*refs):
        ins, outs = refs[:n], refs[n:2 * n]
        send_sems, recv_sems, local_sems = refs[2 * n:]
        x, y, c = _my_place()
        me = (x, y, c)
        sibling = (x, y, 1 - c)
        chips = [(1 - x, y), (x, 1 - y), (1 - x, 1 - y)]

        def copy(a, k, block, to, src=None):
            dst = outs[a].at[_block_index(*block)]
            return pltpu.make_async_remote_copy(
                src_ref=dst if src is None else src, dst_ref=dst,
                send_sem=send_sems.at[a, k], recv_sem=recv_sems.at[a, k],
                device_id=to, device_id_type=pl.DeviceIdType.MESH)

        mine = [pltpu.make_async_copy(ins[a], outs[a].at[_block_index(*me)], local_sems.at[a])
                for a in range(n)]
        for cp in mine:
            cp.start()
        first = []
        for a in range(n):
            first.append(copy(a, 0, me, sibling, src=ins[a]))
            first += [copy(a, 1 + j, me, (*chip, c), src=ins[a]) for j, chip in enumerate(chips)]
        for cp in first:
            cp.start()
        passed = []
        for j, chip in enumerate(chips):
            for a in range(n):
                copy(a, 1 + j, (*chip, c), me).wait_recv()
                fwd = copy(a, 4 + j, (*chip, c), sibling)
                fwd.start()
                passed.append(fwd)
        for a in range(n):
            copy(a, 0, sibling, me).wait_recv()
            for j, chip in enumerate(chips):
                copy(a, 4 + j, (*chip, 1 - c), me).wait_recv()
        for cp in first + passed:
            cp.wait_send()
        for cp in mine:
            cp.wait()

    return pl.pallas_call(
        body, name=name,
        in_specs=[_ANY] * n, out_specs=[_ANY] * n,
        out_shape=[jax.ShapeDtypeStruct((N_DEV,) + v.shape, v.dtype) for v in shards],
        scratch_shapes=[pltpu.SemaphoreType.DMA((n, 7)), pltpu.SemaphoreType.DMA((n, 7)),
                        pltpu.SemaphoreType.DMA((n,))],
    )(*shards)


def _exchange_blocks(blocked, name):
    n = len(blocked)

    def body(*refs):
        ins, outs = refs[:n], refs[n:2 * n]
        send_sems, recv_sems, local_sems = refs[2 * n:]
        x, y, c = _my_place()
        mine_idx = _block_index(x, y, c)
        local = [pltpu.make_async_copy(ins[a].at[mine_idx], outs[a].at[mine_idx], local_sems.at[a])
                 for a in range(n)]
        for cp in local:
            cp.start()
        sends, arrivals = [], []
        for r in range(1, N_DEV):
            peer = (_flip(x, r & 4), _flip(y, r & 2), _flip(c, r & 1))
            peer_idx = _block_index(*peer)
            for a in range(n):
                sems = dict(send_sem=send_sems.at[a, r - 1], recv_sem=recv_sems.at[a, r - 1],
                            device_id=peer, device_id_type=pl.DeviceIdType.MESH)
                sends.append(pltpu.make_async_remote_copy(
                    src_ref=ins[a].at[peer_idx], dst_ref=outs[a].at[mine_idx], **sems))
                arrivals.append(pltpu.make_async_remote_copy(
                    src_ref=ins[a].at[peer_idx], dst_ref=outs[a].at[peer_idx], **sems))
        for cp in sends:
            cp.start()
        for cp in arrivals:
            cp.wait_recv()
        for cp in sends:
            cp.wait_send()
        for cp in local:
            cp.wait()

    return pl.pallas_call(
        body, name=name,
        in_specs=[_ANY] * n, out_specs=[_ANY] * n,
        out_shape=[jax.ShapeDtypeStruct(v.shape, v.dtype) for v in blocked],
        scratch_shapes=[pltpu.SemaphoreType.DMA((n, 7)), pltpu.SemaphoreType.DMA((n, 7)),
                        pltpu.SemaphoreType.DMA((n,))],
    )(*blocked)


N_CHIP = N_DEV // 2


def _pair_swap(blocked, name):
    n = len(blocked)

    def body(*refs):
        ins, outs = refs[:n], refs[n:2 * n]
        send_sems, recv_sems = refs[2 * n:]
        x, y, c = _my_place()
        copies = [pltpu.make_async_remote_copy(
            src_ref=ins[a].at[j, 1 - c], dst_ref=outs[a].at[j],
            send_sem=send_sems.at[N_CHIP * a + j], recv_sem=recv_sems.at[N_CHIP * a + j],
            device_id=(x, y, 1 - c), device_id_type=pl.DeviceIdType.MESH)
            for a in range(n) for j in range(N_CHIP)]
        for cp in copies:
            cp.start()
        for cp in copies:
            cp.wait_recv()
        for cp in copies:
            cp.wait_send()

    return pl.pallas_call(
        body, name=name,
        in_specs=[_ANY] * n, out_specs=[_ANY] * n,
        out_shape=[jax.ShapeDtypeStruct((N_CHIP,) + v.shape[2:], v.dtype) for v in blocked],
        scratch_shapes=[pltpu.SemaphoreType.DMA((N_CHIP * n,)),
                        pltpu.SemaphoreType.DMA((N_CHIP * n,))],
    )(*blocked)


def _pair_add(mine, theirs, name):
    _, nrow, ncol = mine.shape

    def body(a_ref, b_ref, o_ref):
        o_ref[...] = (a_ref[...].astype(F32) + b_ref[...].astype(F32)).astype(BF16)

    blk = pl.BlockSpec((None, nrow, ncol), lambda j: (j, 0, 0))
    return pl.pallas_call(
        body, name=name, grid=(N_CHIP,), in_specs=[blk, blk], out_specs=blk,
        out_shape=jax.ShapeDtypeStruct(mine.shape, BF16), compiler_params=_ARB1,
    )(mine, theirs)


def _chip_exchange(by_chip, to_all, name):
    n, m = len(by_chip), len(to_all)

    def body(*refs):
        ins, alls = refs[:n], refs[n:n + m]
        outs, all_outs = refs[n + m:2 * n + m], refs[2 * n + m:2 * (n + m)]
        send_sems, recv_sems, local_sems = refs[2 * (n + m):]
        x, y, c = _my_place()
        my_chip, mine_idx = 2 * x + y, _block_index(x, y, c)
        local = [pltpu.make_async_copy(ins[a].at[my_chip], outs[a].at[my_chip], local_sems.at[a])
                 for a in range(n)]
        local += [pltpu.make_async_copy(alls[b].at[mine_idx], all_outs[b].at[mine_idx],
                                        local_sems.at[n + b]) for b in range(m)]
        for cp in local:
            cp.start()
        sends, arrivals = [], []
        k = 0
        for r in range(1, N_DEV):
            peer = (_flip(x, r & 4), _flip(y, r & 2), _flip(c, r & 1))
            peer_chip, peer_idx = 2 * peer[0] + peer[1], _block_index(*peer)
            pairs = [(alls[b].at[mine_idx], all_outs[b].at[mine_idx], all_outs[b].at[peer_idx])
                     for b in range(m)]
            if not r & 1:
                pairs += [(ins[a].at[peer_chip], outs[a].at[my_chip], outs[a].at[peer_chip])
                          for a in range(n)]
            for src, there, here in pairs:
                sems = dict(send_sem=send_sems.at[k], recv_sem=recv_sems.at[k], device_id=peer,
                            device_id_type=pl.DeviceIdType.MESH)
                sends.append(pltpu.make_async_remote_copy(src_ref=src, dst_ref=there, **sems))
                arrivals.append(pltpu.make_async_remote_copy(src_ref=src, dst_ref=here, **sems))
                k += 1
        for cp in sends:
            cp.start()
        for cp in arrivals:
            cp.wait_recv()
        for cp in sends:
            cp.wait_send()
        for cp in local:
            cp.wait()

    n_copies = 7 * m + 3 * n
    return pl.pallas_call(
        body, name=name,
        in_specs=[_ANY] * (n + m), out_specs=[_ANY] * (n + m),
        out_shape=[jax.ShapeDtypeStruct(v.shape, v.dtype) for v in by_chip + to_all],
        scratch_shapes=[pltpu.SemaphoreType.DMA((n_copies,)), pltpu.SemaphoreType.DMA((n_copies,)),
                        pltpu.SemaphoreType.DMA((n + m,))],
    )(*by_chip, *to_all)


class _Rider(NamedTuple):
    srcs: list
    lands: list
    scatter: bool
    part: list
    relay: bool = False


def _window(ref, part):
    if part is None:
        return ref
    dim, start, size = part
    return ref.at[(slice(None),) * dim + (pl.ds(start, size),)]


def _relay_copies(srcs, lands, send_sems, recv_sems, rider):
    x, y, c = _my_place()
    me, sibling = (x, y, c), (x, y, 1 - c)
    chips = [(1 - x, y), (x, 1 - y), (1 - x, 1 - y)]
    first, from_chips, passed, last = [], [], [], []
    for a in range(len(srcs)):
        def copy(k, block, to, src=None, a=a):
            dst = _window(lands[a].at[_block_index(*block)], rider.part[a])
            return pltpu.make_async_remote_copy(
                src_ref=dst if src is None else src, dst_ref=dst,
                send_sem=send_sems.at[7 * a + k], recv_sem=recv_sems.at[7 * a + k],
                device_id=to, device_id_type=pl.DeviceIdType.MESH)

        mine = _window(srcs[a], rider.part[a])
        first.append(copy(0, me, sibling, src=mine))
        first += [copy(1 + j, me, (*chip, c), src=mine) for j, chip in enumerate(chips)]
        from_chips += [copy(1 + j, (*chip, c), me) for j, chip in enumerate(chips)]
        passed += [copy(4 + j, (*chip, c), sibling) for j, chip in enumerate(chips)]
        last.append(copy(0, sibling, me))
        last += [copy(4 + j, (*chip, 1 - c), me) for j, chip in enumerate(chips)]
    return first, from_chips, passed, last


def _new_lands(srcs, scatter):
    return [lax.empty(v.shape if scatter else (N_DEV,) + v.shape, v.dtype) for v in srcs]


def _rider_copies(srcs, lands, send_sems, recv_sems, rider):
    x, y, c = _my_place()
    mine_idx = _block_index(x, y, c)

    def window(ref, a):
        return _window(ref, rider.part[a])

    sends, arrivals = [], []
    for r in range(1, N_DEV):
        peer = (_flip(x, r & 4), _flip(y, r & 2), _flip(c, r & 1))
        peer_idx = _block_index(*peer)
        for a in range(len(srcs)):
            src = window(srcs[a].at[peer_idx] if rider.scatter else srcs[a], a)
            k = 7 * a + r - 1
            sems = dict(send_sem=send_sems.at[k], recv_sem=recv_sems.at[k],
                        device_id=peer, device_id_type=pl.DeviceIdType.MESH)
            sends.append(pltpu.make_async_remote_copy(
                src_ref=src, dst_ref=window(lands[a].at[mine_idx], a), **sems))
            arrivals.append(pltpu.make_async_remote_copy(
                src_ref=src, dst_ref=window(lands[a].at[peer_idx], a), **sems))
    return sends, arrivals


def _ride(call, rider):
    call = dict(call)
    body, grid = call.pop("body"), call["grid"]
    operands = call.pop("operands")
    if rider is None:
        return list(pl.pallas_call(body, **call)(*operands)), None
    n_in, n_out = len(call["in_specs"]), len(call["out_specs"])
    n_scratch = len(call["scratch_shapes"])
    m = len(rider.srcs)

    def riding(*refs):
        main_in, srcs, lands = refs[:n_in], refs[n_in:n_in + m], refs[n_in + m:n_in + 2 * m]
        main_out = refs[n_in + 2 * m:n_in + 2 * m + n_out]
        rest = refs[n_in + 3 * m + n_out:]
        send_sems, recv_sems = rest[n_scratch:]
        at = [pl.program_id(d) for d in range(len(grid))]
        first = functools.reduce(jnp.logical_and, [p == 0 for p in at])
        last = functools.reduce(jnp.logical_and, [p == g - 1 for p, g in zip(at, grid)])
        if rider.relay:
            sends, from_chips, passed, arrivals = _relay_copies(
                srcs, lands, send_sems, recv_sems, rider)
            step, steps = 0, 1
            for p, g in zip(at, grid):
                step, steps = step * g + p, steps * g
            assert steps >= 2, "a relayed gather needs a later grid step to pass blocks on"

            @pl.when(step == (3 * steps) // 4)
            def _():
                for cp in from_chips:
                    cp.wait_recv()
                for cp in passed:
                    cp.start()
        else:
            sends, arrivals = _rider_copies(srcs, lands, send_sems, recv_sems, rider)
            passed = []

        @pl.when(first)
        def _():
            for cp in sends:
                cp.start()

        body(*main_in, *main_out, *rest[:n_scratch])

        @pl.when(last)
        def _():
            for cp in arrivals:
                cp.wait_recv()
            for cp in sends + passed:
                cp.wait_send()

    call["in_specs"] = list(call["in_specs"]) + [_ANY] * (2 * m)
    call["out_specs"] = list(call["out_specs"]) + [_ANY] * m
    call["out_shape"] = list(call["out_shape"]) + [
        jax.ShapeDtypeStruct(v.shape, v.dtype) for v in rider.lands]
    call["scratch_shapes"] = list(call["scratch_shapes"]) + [
        pltpu.SemaphoreType.DMA((7 * m,)), pltpu.SemaphoreType.DMA((7 * m,))]
    call["input_output_aliases"] = {n_in + m + a: n_out + a for a in range(m)}
    outs = pl.pallas_call(riding, **call)(*operands, *rider.srcs, *rider.lands)
    return list(outs[:n_out]), list(outs[n_out:])


def _sum_parts(p_ref):
    g = p_ref[0].astype(F32)
    for k in range(1, p_ref.shape[0]):
        g = g + p_ref[k].astype(F32)
    return g


def _adamw(g, w, m, v):
    c1 = 1.0 / (1.0 - ADAM_B1 ** ADAM_STEP)
    c2 = 1.0 / (1.0 - ADAM_B2 ** ADAM_STEP)
    nm = ADAM_B1 * m + (1.0 - ADAM_B1) * g
    nv = ADAM_B2 * v + (1.0 - ADAM_B2) * (g * g)
    return nm, nv, -ADAM_LR * ((nm * c1) / (jnp.sqrt(nv * c2) + ADAM_EPS) + ADAM_WD * w)


def _adamw_w_in(parts, w, m, v, name):
    ncol_blk, depth, nfeat = w.shape
    cols = 256

    def body(*refs):
        p_refs = refs[:depth]
        w_ref, m_ref, v_ref, g_ref, d_ref, nm_ref, nv_ref = refs[depth:]
        for l in range(depth):
            g = _sum_parts(p_refs[l])
            nm, nv, d = _adamw(g, w_ref[:, l, :], m_ref[:, l, :], v_ref[:, l, :])
            g_ref[:, l, :] = g
            nm_ref[:, l, :] = nm
            nv_ref[:, l, :] = nv
            d_ref[:, l, :] = d

    blk = pl.BlockSpec((ncol_blk, depth, cols), lambda j: (0, 0, j))
    return pl.pallas_call(
        body, name=name, grid=(nfeat // cols,),
        in_specs=[pl.BlockSpec((p.shape[0], ncol_blk, cols), lambda j: (0, 0, j)) for p in parts]
        + [blk] * 3,
        out_specs=[blk] * 4,
        out_shape=[jax.ShapeDtypeStruct(w.shape, F32)] * 4,
        compiler_params=_ARB1,
    )(*parts, w, m, v)


def _adamw_sum(parts, w, m, v, tile, name):
    depth, nrow, ncol = w.shape
    rows, cols = tile

    def body(*refs):
        p_refs = refs[:depth]
        w_ref, m_ref, v_ref, g_ref, d_ref, nm_ref, nv_ref = refs[depth:]
        layer = pl.program_id(0)
        for l in range(depth):
            @pl.when(layer == l)
            def _(p_ref=p_refs[l]):
                g = _sum_parts(p_ref)
                nm, nv, d = _adamw(g, w_ref[...], m_ref[...], v_ref[...])
                g_ref[...] = g
                nm_ref[...] = nm
                nv_ref[...] = nv
                d_ref[...] = d

    def part_spec(l):
        return pl.BlockSpec((parts[l].shape[0], rows, cols), lambda q, i, j: (
            0, jnp.where(q == l, i, 0), jnp.where(q == l, j, 0)))

    blk = pl.BlockSpec((None, rows, cols), lambda q, i, j: (q, i, j))
    return pl.pallas_call(
        body, name=name, grid=(depth, nrow // rows, ncol // cols),
        in_specs=[part_spec(l) for l in range(depth)] + [blk, blk, blk],
        out_specs=[blk] * 4,
        out_shape=[jax.ShapeDtypeStruct(w.shape, F32)] * 4,
        compiler_params=pltpu.CompilerParams(
            dimension_semantics=("arbitrary", "arbitrary", "arbitrary")),
    )(*parts, w, m, v)


def _pack_small(norm_w, mem_norm_w, out_norm_w, final_norm_w, b_forget):
    onw = jnp.pad(out_norm_w.reshape(20, LANES), ((0, 4), (0, 0)))
    b = jnp.pad(b_forget, ((0, 6), (0, LANES - FOX_HEADS)))
    return jnp.concatenate([norm_w.reshape(16, LANES), mem_norm_w.reshape(16, LANES), onw,
                            final_norm_w.reshape(8, LANES), b], axis=0)


def _unpack_small(p):
    return (p[0:16].reshape(2, D_MODEL), p[16:32].reshape(2, D_MODEL), p[32:52].reshape(2, MIX_W),
            p[56:64].reshape(D_MODEL), p[64:66, :FOX_HEADS])


def kernel(x, mem, norm_w, w_in, b_forget, mem_norm_w, w_mem_kv, out_norm_w, w_out, final_norm_w, loss_target, m_norm_w, m_w_in, m_b_forget, m_mem_norm_w, m_w_mem_kv, m_out_norm_w, m_w_out, m_final_norm_w, v_norm_w, v_w_in, v_b_forget, v_mem_norm_w, v_w_mem_kv, v_out_norm_w, v_w_out, v_final_norm_w):
    kv_rows = w_mem_kv.shape[1]
    out_rows = w_out.shape[1]
    me = _block_index(*_my_place())

    def shards(l):
        return [w_in[l].T.astype(BF16), w_mem_kv[l].astype(BF16), w_out[l].astype(BF16)]

    def full_in(g_in):
        return _rearrange_w_in(g_in.reshape(IN_W, D_MODEL))

    def full_kv_out(g_kv, g_out):
        return g_kv.reshape(D_MODEL, 2 * MEM_W), g_out.reshape(MIX_W, D_MODEL)

    def in_blocks(g):
        segments = [(0, 3072, 0), (3072, 3080, QKV_W + MIX_W), (3080, 3336, 3072),
                    (3336, IN_W, QKV_W)]

        def block(k):
            lo, hi = k * SHARD_W, (k + 1) * SHARD_W
            pieces = [g[at + max(lo, a) - a:at + min(hi, b) - a]
                      for a, b, at in segments if max(lo, a) < min(hi, b)]
            return jnp.concatenate(pieces, axis=0).astype(BF16)

        return jnp.stack([block(k) for k in range(N_DEV)])

    def kv_blocks(g):
        return g.reshape(N_DEV, kv_rows, 2 * MEM_W).astype(BF16)

    def out_blocks(g):
        return g.reshape(N_DEV, out_rows, D_MODEL).astype(BF16)

    def with_own(land, own):
        return lax.dynamic_update_slice(land, own[None], (me,) + (0,) * own.ndim)

    def with_own_of(land, blocked):
        return lax.dynamic_update_slice(land, lax.dynamic_slice_in_dim(blocked, me, 1, axis=0),
                                        (me,) + (0,) * (land.ndim - 1))

    def row(v):
        return v.reshape(1, -1)

    def cols(first, size):
        return (1, first, size)

    fwd_split, bwd_split = 4 * LANES, (5 * LANES, 6 * LANES)

    s_in0, s_kv0, s_out0 = shards(0)
    (g_in0,) = _all_gather_weights([s_in0], "all_gather_l0")
    w_r0 = full_in(g_in0)
    s_in1, s_kv1, s_out1 = shards(1)
    x1, saved0, (l_in1, _, _), (wkv0, wout0) = _layer_fwd(
        x[0], mem[0], row(norm_w[0]), w_r0, b_forget[0], row(mem_norm_w[0]),
        lambda lands: full_kv_out(with_own(lands[1], s_kv0), with_own(lands[2], s_out0)),
        row(out_norm_w[0]), 0,
        travel=([s_in1, s_kv0, s_out0], False,
                [([0], [cols(0, fwd_split)]), ([0], [cols(fwd_split, D_MODEL - fwd_split)]),
                 ([1, 2], [None, None])]))
    w_r1 = full_in(with_own(l_in1, s_in1))
    x2, saved1, _, (wkv1, wout1) = _layer_fwd(
        x1, mem[0], row(norm_w[1]), w_r1, b_forget[1], row(mem_norm_w[1]),
        lambda lands: full_kv_out(with_own(lands[0], s_kv1), with_own(lands[1], s_out1)),
        row(out_norm_w[1]), 1, travel=([s_kv1, s_out1], False, [([0, 1], [None, None]), None]))

    dx2, loss_part, dfnw = _final_fwd_bwd(x2, row(final_norm_w), loss_target[0], "final_fwd_bwd")

    dx1, gr1, (l_out1,) = _layer_bwd(
        dx2, saved1, mem[0], w_r1, wkv1, wout1, 1,
        travel=lambda dwout: ([out_blocks(dwout)], True, [([0], [None]), None]))
    p_in1, p_kv1 = in_blocks(gr1["w_r"]), kv_blocks(gr1["w_mem_kv"])
    grad_x, gr0, (l_in1, l_kv1, l_out0) = _layer_bwd(
        dx1, saved0, mem[0], w_r0, wkv0, wout0, 0,
        travel=lambda dwout: ([p_in1, p_kv1, out_blocks(dwout)], True,
                              [([0, 1], [cols(0, bwd_split[0]), None]),
                               ([0, 2], [cols(bwd_split[0], bwd_split[1] - bwd_split[0]), None]),
                               ([0], [cols(bwd_split[1], D_MODEL - bwd_split[1])])]))
    r_out1 = with_own_of(l_out1, out_blocks(gr1["w_out"]))
    r_in1, r_kv1 = with_own_of(l_in1, p_in1), with_own_of(l_kv1, p_kv1)
    r_out0 = with_own_of(l_out0, out_blocks(gr0["w_out"]))

    def both(name):
        return jnp.stack([gr0[name], gr1[name]])

    small = _pack_small(both("norm_w"), both("mem_norm_w"), both("out_norm_w"), dfnw[0],
                        both("b_forget")).at[LOSS_ROW].set(loss_part[0])
    p_small = jnp.broadcast_to(small[None], (N_DEV, SMALL_ROWS, LANES))
    by_core = [v.reshape((N_CHIP, 2) + v.shape[1:])
               for v in (in_blocks(gr0["w_r"]), kv_blocks(gr0["w_mem_kv"]))]
    from_sibling = _pair_swap(by_core, "grads_l0_pair_swap")
    core = lax.axis_index("c")
    chip_sums = [_pair_add(lax.dynamic_index_in_dim(v, core, axis=1, keepdims=False), got,
                           f"grads_l0_pair_add_{a}")
                 for a, (v, got) in enumerate(zip(by_core, from_sibling))]
    r_in0, r_kv0, r_small = _chip_exchange(chip_sums, [p_small], "exchange_grads_l0")

    def view(v):
        return jnp.transpose(v, (2, 0, 1))

    g_w_in, d_w_in, nm_w_in, nv_w_in = [jnp.transpose(v, (1, 2, 0)) for v in _adamw_w_in(
        [r_in0, r_in1], view(w_in), view(m_w_in), view(v_w_in), "adamw_w_in")]
    g_w_kv, d_w_kv, nm_w_kv, nv_w_kv = _adamw_sum(
        [r_kv0, r_kv1], w_mem_kv, m_w_mem_kv, v_w_mem_kv, (kv_rows, 2 * MEM_W), "adamw_w_mem_kv")
    g_w_out, d_w_out, nm_w_out, nv_w_out = _adamw_sum(
        [r_out0, r_out1], w_out, m_w_out, v_w_out, (out_rows, D_MODEL), "adamw_w_out")
    w_small = _pack_small(norm_w, mem_norm_w, out_norm_w, final_norm_w, b_forget)[None]
    m_small = _pack_small(m_norm_w, m_mem_norm_w, m_out_norm_w, m_final_norm_w, m_b_forget)[None]
    v_small = _pack_small(v_norm_w, v_mem_norm_w, v_out_norm_w, v_final_norm_w, v_b_forget)[None]
    small_out = _adamw_sum([r_small], w_small, m_small, v_small, (SMALL_ROWS, LANES), "adamw_small")
    (g_nw, g_mnw, g_onw, g_fnw, g_b), (d_nw, d_mnw, d_onw, d_fnw, d_b), \
        (nm_nw, nm_mnw, nm_onw, nm_fnw, nm_b), (nv_nw, nv_mnw, nv_onw, nv_fnw, nv_b) = [
            _unpack_small(t[0]) for t in small_out]
    loss = small_out[0][0, LOSS_ROW, 0]

    return (loss, grad_x[None],
            g_nw, g_w_in, g_b, g_mnw, g_w_kv, g_onw, g_w_out, g_fnw,
            d_nw, d_w_in, d_b, d_mnw, d_w_kv, d_onw, d_w_out, d_fnw,
            nm_nw, nm_w_in, nm_b, nm_mnw, nm_w_kv, nm_onw, nm_w_out, nm_fnw,
            nv_nw, nv_w_in, nv_b, nv_mnw, nv_w_kv, nv_onw, nv_w_out, nv_fnw)
```

```python
import functools
from typing import NamedTuple

import jax
import jax.numpy as jnp
from jax import lax
from jax.experimental import pallas as pl
from jax.experimental.pallas import tpu as pltpu

F32 = jnp.float32
BF16 = jnp.bfloat16

N_DEV = 8
D_MODEL = 1024
HEAD_DIM = 64
LANES = 128
SB_W = 512
FX_W = 512
MEM_W = 256
MIX_W = 1280
FOX_HEADS = 8
IN_W = 4616
SHARD_W = IN_W // N_DEV
QKV_W = 3 * SB_W + 3 * FX_W + MEM_W
FL_PAD = 256
GF_W = MIX_W + FL_PAD
WR_W = QKV_W + GF_W
EPS = 1e-6
T = 256
QPS = 4
TM = 256
TQM = 512
Q_SCALE = 0.125
NEG = -1e30
UNDERFLOW = -110.0
NORM_SLACK = 1.01

ADAM_LR = 0.001
ADAM_B1 = 0.9
ADAM_B2 = 0.999
ADAM_EPS = 1e-08
ADAM_WD = 0.01
ADAM_STEP = 10


_NT = (((1,), (1,)), ((), ()))
_TN = (((0,), (0,)), ((), ()))

_ARB1 = pltpu.CompilerParams(dimension_semantics=("arbitrary",))
_ARB2 = pltpu.CompilerParams(dimension_semantics=("arbitrary", "arbitrary"))


def _dot(a, b):
    return jnp.dot(a, b, preferred_element_type=F32)


def _dot_nt(a, b):
    return lax.dot_general(a, b, _NT, preferred_element_type=F32)


def _dot_tn(a, b):
    return lax.dot_general(a, b, _TN, preferred_element_type=F32)


def _split2(x):
    hi = x.astype(BF16)
    lo = (x - hi.astype(F32)).astype(BF16)
    return hi, lo


def _stack2(u):
    return jnp.concatenate([u, u], axis=0)


def _cum2(x, u2):
    hi, lo = _split2(x)
    return _dot(jnp.concatenate([hi, lo], axis=1), u2)


def _tri3(tri, x, dot=None):
    dot = dot or _dot
    hi = x.astype(BF16)
    r1 = x - hi.astype(F32)
    mid = r1.astype(BF16)
    lo = (r1 - mid.astype(F32)).astype(BF16)
    return dot(tri, hi) + dot(tri, mid) + dot(tri, lo)


def _iota2(shape, dim):
    return lax.broadcasted_iota(jnp.int32, shape, dim)


def _head_block_diag():
    r = _iota2((LANES, LANES), 0) // HEAD_DIM
    c = _iota2((LANES, LANES), 1) // HEAD_DIM
    return _stack2(jnp.where(r == c, 1.0, 0.0).astype(BF16))


def _head_mean(x, bd):
    return _cum2(x, bd) * (1.0 / HEAD_DIM)


def _sigmoid(x):
    return 1.0 / (1.0 + jnp.exp(-x))


def _log_sigmoid(x):
    return jnp.minimum(x, 0.0) - jnp.log(1.0 + jnp.exp(-jnp.abs(x)))


def _running_top(r_ref):
    return jnp.max(jnp.maximum(r_ref[0], r_ref[1]))


def _fox_tiles_left(i, pair, nq, fb_ref, tile):
    def bound(j):
        b = []
        for h in range(2):
            head = 2 * pair + h
            b.append(2.0 * NORM_SLACK * fb_ref[2 * nq + i, head] * fb_ref[3 * nq, head]
                     + fb_ref[2 * i, head] - fb_ref[2 * j + 1, head])
        return jnp.maximum(b[0], b[1])

    def more(j):
        return jnp.logical_and(j >= 0, bound(jnp.maximum(j, 0)) > UNDERFLOW)

    def step(j):
        tile(j, False)
        return j - 1

    return lax.while_loop(more, step, i - 1)


def _pair_masks():
    lane = _iota2((1, LANES), 1)
    return lane < HEAD_DIM


def _split_pair(x, m0):
    zero = jnp.zeros_like(x)
    return jnp.where(m0, x, zero), jnp.where(m0, zero, x)


def _inproj_fwd(x, nw, w_r, name, rider=None):
    s = x.shape[0]

    def body(x_ref, nw_ref, w_ref, qkv_ref, gf_ref):
        xv = x_ref[...]
        r = lax.rsqrt(jnp.mean(xv * xv, axis=-1, keepdims=True) + EPS)
        h = (xv * r * nw_ref[...]).astype(BF16)
        for c in range(0, QKV_W, 256):
            qkv_ref[:, c:c + 256] = _dot_nt(h, w_ref[c:c + 256, :]).astype(BF16)
        for c in range(0, GF_W, 256):
            gf_ref[:, c:c + 256] = _dot_nt(h, w_ref[QKV_W + c:QKV_W + c + 256, :])

    (qkv, gf), lands = _ride(dict(
        body=body, name=name, grid=(s // TM,),
        in_specs=[pl.BlockSpec((TM, D_MODEL), lambda i: (i, 0)),
                  pl.BlockSpec((1, D_MODEL), lambda i: (0, 0)),
                  pl.BlockSpec((WR_W, D_MODEL), lambda i: (0, 0))],
        out_specs=[pl.BlockSpec((TM, QKV_W), lambda i: (i, 0)),
                   pl.BlockSpec((TM, GF_W), lambda i: (i, 0))],
        out_shape=[jax.ShapeDtypeStruct((s, QKV_W), BF16), jax.ShapeDtypeStruct((s, GF_W), F32)],
        scratch_shapes=[], compiler_params=_ARB1, operands=[x, nw, w_r]), rider)
    return qkv, gf, lands


def _fox_prep_fwd(gf, qkv, bpad, name):
    s = gf.shape[0]
    nq = s // T
    nrow = -(-(3 * nq + 1) // 8) * 8

    def body(fl_ref, q_ref, k_ref, b_ref, fq_ref, fr_ref, fb_ref):
        tri = jnp.where(_iota2((T, T), 0) >= _iota2((T, T), 1), 1.0, 0.0).astype(BF16)
        m0 = _pair_masks()
        lane = _iota2((1, LANES), 1)
        norms = [jnp.zeros((1, LANES), F32) for _ in range(nq + 1)]
        same_head = (_iota2((LANES, LANES), 0) // HEAD_DIM) == (_iota2((LANES, LANES), 1) // HEAD_DIM)
        bd = jnp.where(same_head, 1.0, 0.0).astype(BF16)
        for p in range(FOX_HEADS // 2):
            cols = slice(p * LANES, (p + 1) * LANES)
            q = (q_ref[:, cols] * jnp.asarray(Q_SCALE, BF16)).astype(F32)
            k = k_ref[:, cols].astype(F32)
            qn = _dot((q * q).astype(BF16), bd)
            kn = _dot((k * k).astype(BF16), bd)
            tops = [jnp.max(qn[j * T:(j + 1) * T], axis=0, keepdims=True) for j in range(nq)]
            tops.append(jnp.max(kn, axis=0, keepdims=True))
            tops = [jnp.sqrt(top) for top in tops]
            for h in range(2):
                at = h * HEAD_DIM
                norms = [jnp.where(lane == 2 * p + h, top[:, at:at + 1], row)
                         for top, row in zip(tops, norms)]
        for j in range(nq + 1):
            fb_ref[2 * nq + j:2 * nq + j + 1, :] = norms[j]
        fb_ref[3 * nq + 1:, :] = jnp.zeros((nrow - 3 * nq - 1, LANES), F32)
        carry = jnp.zeros((1, LANES), F32)
        for blk in range(s // T):
            rows = slice(blk * T, (blk + 1) * T)
            lf = _log_sigmoid(fl_ref[rows, :] + b_ref[...])
            c = _tri3(tri, lf) + carry
            carry = c[T - 1:T, :]
            for p in range(FOX_HEADS // 2):
                fq_ref[rows, p * LANES:(p + 1) * LANES] = jnp.where(
                    m0, c[:, 2 * p:2 * p + 1], c[:, 2 * p + 1:2 * p + 2])
            fr_ref[:, rows] = c.T[0:FOX_HEADS, :]
            fb_ref[2 * blk:2 * blk + 1, :] = c[0:1, :]
            fb_ref[2 * blk + 1:2 * blk + 2, :] = carry

    base = 3 * SB_W // FX_W
    return pl.pallas_call(
        body, name=name, grid=(1,),
        in_specs=[pl.BlockSpec((s, LANES), lambda i: (0, MIX_W // LANES)),
                  pl.BlockSpec((s, FX_W), lambda i: (0, base)),
                  pl.BlockSpec((s, FX_W), lambda i: (0, base + 1)),
                  pl.BlockSpec((1, LANES), lambda i: (0, 0))],
        out_specs=[pl.BlockSpec((s, FX_W), lambda i: (0, 0)),
                   pl.BlockSpec((FOX_HEADS, s), lambda i: (0, 0)),
                   pl.BlockSpec((nrow, LANES), lambda i: (0, 0))],
        out_shape=[jax.ShapeDtypeStruct((s, FX_W), F32), jax.ShapeDtypeStruct((FOX_HEADS, s), F32),
                   jax.ShapeDtypeStruct((nrow, LANES), F32)],
        compiler_params=_ARB1,
    )(gf, qkv, qkv, bpad)


def _sb_fwd(qkv, name, rider=None):
    s = qkv.shape[0]

    def body(q_ref, k_ref, v_ref, o_ref, acc_ref, r_ref, as_ref):
        m0 = _pair_masks()
        strict = _iota2((T, T), 0) > _iota2((T, T), 1)
        u2 = _stack2(jnp.where(strict, 1.0, 0.0).astype(BF16))
        hs = range(2)

        def query_tile(i, rows):
            qh = _split_pair(q_ref[rows, :] * jnp.asarray(Q_SCALE, BF16), m0)
            acc_ref[...] = jnp.zeros_like(acc_ref)
            r_ref[...] = jnp.zeros_like(r_ref)

            def flush(j):
                v = v_ref[pl.ds(pl.multiple_of(j * T, T), T), :]
                for h in hs:
                    acc_ref[h] += _dot(as_ref[h], v)

            def tile(j, diag):
                k = k_ref[pl.ds(pl.multiple_of(j * T, T), T), :]
                z = [_dot_nt(qh[h], k) for h in hs]
                if not diag:
                    flush(j + 1)
                la = [jnp.minimum(z[h], 0.0) - jnp.log(1.0 + jnp.exp(-jnp.abs(z[h]))) for h in hs]
                lf = [la[h] - z[h] for h in hs]
                if diag:
                    lf = [jnp.where(strict, lf[h], 0.0) for h in hs]
                cin = [_cum2(lf[h], u2) for h in hs]
                a = [jnp.exp(la[h] + cin[h] + r_ref[h]) for h in hs]
                if diag:
                    a = [jnp.where(strict, a[h], 0.0) for h in hs]
                for h in hs:
                    r_ref[h] += cin[h][:, 0:1] + lf[h][:, 0:1]
                    as_ref[h] = a[h].astype(BF16)

            tile(i, True)

            def more(state):
                j, top = state
                return jnp.logical_and(j >= 0, top > UNDERFLOW)

            def step(state):
                j, _ = state
                tile(j, False)
                return j - 1, _running_top(r_ref)

            j_left, _ = lax.while_loop(more, step, (i - 1, _running_top(r_ref)))
            flush(j_left + 1)
            o_ref[rows, :] = jnp.where(m0, acc_ref[0], acc_ref[1])

        for n in range(QPS):
            query_tile(QPS * pl.program_id(1) + n, slice(n * T, (n + 1) * T))

    nb = SB_W // LANES
    (ysb,), lands = _ride(dict(
        body=body, name=name, grid=(nb, s // (QPS * T)),
        in_specs=[pl.BlockSpec((QPS * T, LANES), lambda p, i: (i, p)),
                  pl.BlockSpec((s, LANES), lambda p, i: (0, nb + p)),
                  pl.BlockSpec((s, LANES), lambda p, i: (0, 2 * nb + p))],
        out_specs=[pl.BlockSpec((QPS * T, LANES), lambda p, i: (i, p))],
        out_shape=[jax.ShapeDtypeStruct((s, SB_W), F32)],
        scratch_shapes=[pltpu.VMEM((2, T, LANES), F32), pltpu.VMEM((2, T, 1), F32),
                        pltpu.VMEM((2, T, T), BF16)],
        compiler_params=_ARB2, operands=[qkv, qkv, qkv]), rider)
    return ysb, lands


def _fox_fwd(qkv, fqb, frow, fbounds, name, rider=None):
    s = qkv.shape[0]

    def body(q_ref, k_ref, v_ref, fq_ref, fr_ref, fb_ref, o_ref, lse_ref, acc_ref, m_ref, ps_ref):
        pair = pl.program_id(0)
        m0 = _pair_masks()
        causal = _iota2((T, T), 0) >= _iota2((T, T), 1)
        ones = jnp.ones((T, LANES), BF16)
        hs = range(2)

        def query_tile(i, rows):
            qh = _split_pair(q_ref[rows, :] * jnp.asarray(Q_SCALE, BF16), m0)
            fq = fq_ref[rows, :]
            fqh = (fq[:, 0:1], fq[:, HEAD_DIM:HEAD_DIM + 1])
            acc_ref[...] = jnp.zeros_like(acc_ref)
            m_ref[...] = jnp.full_like(m_ref, NEG)

            def flush(j):
                v = v_ref[pl.ds(pl.multiple_of(j * T, T), T), :]
                va2 = _stack2(jnp.concatenate([v, ones], axis=1))
                for h in hs:
                    acc_ref[h] += _dot(ps_ref[h], va2)

            def tile(j, diag):
                off = pl.multiple_of(j * T, T)
                k = k_ref[pl.ds(off, T), :]
                sc = [_dot_nt(qh[h], k) + fqh[h] - fr_ref[h:h + 1, pl.ds(off, T)] for h in hs]
                if not diag:
                    flush(j + 1)
                if diag:
                    sc = [jnp.where(causal, sc[h], NEG) for h in hs]
                m_new = [jnp.maximum(m_ref[h], jnp.max(sc[h], axis=1, keepdims=True)) for h in hs]
                p = [jnp.exp(sc[h] - m_new[h]) for h in hs]
                for h in hs:
                    acc_ref[h] = acc_ref[h] * jnp.exp(m_ref[h] - m_new[h])
                    m_ref[h] = m_new[h]
                    p_hi, p_lo = _split2(p[h])
                    ps_ref[h] = jnp.concatenate([p_hi, p_lo], axis=1)

            tile(i, True)
            j_left = _fox_tiles_left(i, pair, s // T, fb_ref, tile)
            flush(j_left + 1)
            acc = (acc_ref[0], acc_ref[1])
            o_ref[rows, :] = jnp.where(m0, acc[0][:, :LANES] / acc[0][:, LANES:],
                                       acc[1][:, :LANES] / acc[1][:, LANES:])
            lse_ref[rows, :] = jnp.where(m0, m_ref[0] + jnp.log(acc[0][:, LANES:]),
                                         m_ref[1] + jnp.log(acc[1][:, LANES:]))

        for n in range(QPS):
            query_tile(QPS * pl.program_id(1) + n, slice(n * T, (n + 1) * T))

    nb = FX_W // LANES
    base = 3 * SB_W // LANES
    (yfx, lse), lands = _ride(dict(
        body=body, name=name, grid=(nb, s // (QPS * T)),
        in_specs=[pl.BlockSpec((QPS * T, LANES), lambda p, i: (i, base + p)),
                  pl.BlockSpec((s, LANES), lambda p, i: (0, base + nb + p)),
                  pl.BlockSpec((s, LANES), lambda p, i: (0, base + 2 * nb + p)),
                  pl.BlockSpec((QPS * T, LANES), lambda p, i: (i, p)),
                  pl.BlockSpec((None, 2, s), lambda p, i: (p, 0, 0)),
                  pl.BlockSpec(memory_space=pltpu.SMEM)],
        out_specs=[pl.BlockSpec((QPS * T, LANES), lambda p, i: (i, p)),
                   pl.BlockSpec((QPS * T, LANES), lambda p, i: (i, p))],
        out_shape=[jax.ShapeDtypeStruct((s, FX_W), F32), jax.ShapeDtypeStruct((s, FX_W), F32)],
        scratch_shapes=[pltpu.VMEM((2, T, 2 * LANES), F32), pltpu.VMEM((2, T, 1), F32),
                        pltpu.VMEM((2, T, 2 * T), BF16)],
        compiler_params=_ARB2, operands=[qkv, qkv, qkv, fqb, frow, fbounds]), rider)
    return yfx, lse, lands


def _memkv_fwd(mem, mnw, wkv, name):
    n = mem.shape[0]

    def body(mem_ref, mnw_ref, w_ref, kv_ref):
        mv = mem_ref[...]
        r = lax.rsqrt(jnp.mean(mv * mv, axis=-1, keepdims=True) + EPS)
        hm = (mv * r * mnw_ref[...]).astype(BF16)
        kv_ref[...] = _dot(hm, w_ref[...]).astype(BF16)

    return pl.pallas_call(
        body, name=name, grid=(1,),
        in_specs=[pl.BlockSpec((n, D_MODEL), lambda i: (0, 0)),
                  pl.BlockSpec((1, D_MODEL), lambda i: (0, 0)),
                  pl.BlockSpec((D_MODEL, 2 * MEM_W), lambda i: (0, 0))],
        out_specs=pl.BlockSpec((n, 2 * MEM_W), lambda i: (0, 0)),
        out_shape=jax.ShapeDtypeStruct((n, 2 * MEM_W), BF16),
        compiler_params=_ARB1,
    )(mem, mnw, wkv)


def _mem_fwd(qkv, kv, name):
    s = qkv.shape[0]
    n = kv.shape[0]

    def body(q_ref, k_ref, v_ref, o_ref, lse_ref):
        m0 = _pair_masks()
        qh = _split_pair(q_ref[...] * jnp.asarray(Q_SCALE, BF16), m0)
        k = k_ref[...]
        v = v_ref[...]
        outs, lses = [], []
        for h in range(2):
            sc = _dot_nt(qh[h], k)
            mx = jnp.max(sc, axis=1, keepdims=True)
            p = jnp.exp(sc - mx)
            l = jnp.sum(p, axis=1, keepdims=True)
            outs.append(_dot(p.astype(BF16), v) / l)
            lses.append(mx + jnp.log(l))
        o_ref[...] = jnp.where(m0, outs[0], outs[1])
        lse_ref[...] = jnp.where(m0, lses[0], lses[1])

    nb = MEM_W // LANES
    base = (3 * SB_W + 3 * FX_W) // LANES
    return pl.pallas_call(
        body, name=name, grid=(nb, s // TQM),
        in_specs=[pl.BlockSpec((TQM, LANES), lambda p, i: (i, base + p)),
                  pl.BlockSpec((n, LANES), lambda p, i: (0, p)),
                  pl.BlockSpec((n, LANES), lambda p, i: (0, nb + p))],
        out_specs=[pl.BlockSpec((TQM, LANES), lambda p, i: (i, p)),
                   pl.BlockSpec((TQM, LANES), lambda p, i: (i, p))],
        out_shape=[jax.ShapeDtypeStruct((s, MEM_W), F32), jax.ShapeDtypeStruct((s, MEM_W), F32)],
        compiler_params=_ARB2,
    )(qkv, kv, kv)


def _mix_chunk(c, ysb_ref, yfx_ref, ym_ref):
    if c < SB_W // LANES:
        return ysb_ref[:, c * LANES:(c + 1) * LANES]
    c -= SB_W // LANES
    if c < FX_W // LANES:
        return yfx_ref[:, c * LANES:(c + 1) * LANES]
    c -= FX_W // LANES
    return ym_ref[:, c * LANES:(c + 1) * LANES]


def _outproj_fwd(ysb, yfx, ym, gf, onw, wout, x, name):
    s = x.shape[0]

    def body(ysb_ref, yfx_ref, ym_ref, g_ref, onw_ref, w_ref, x_ref, o_ref, yg_ref):
        bd = _head_block_diag()
        for c in range(MIX_W // LANES):
            sl = slice(c * LANES, (c + 1) * LANES)
            u = _mix_chunk(c, ysb_ref, yfx_ref, ym_ref)
            r = lax.rsqrt(_head_mean(u * u, bd) + EPS)
            g = g_ref[:, sl]
            yg_ref[:, sl] = (u * r * onw_ref[:, sl] * (g * _sigmoid(g))).astype(BF16)
        o_ref[...] = x_ref[...] + _dot(yg_ref[...], w_ref[...])

    return pl.pallas_call(
        body, name=name, grid=(s // TM,),
        in_specs=[pl.BlockSpec((TM, SB_W), lambda i: (i, 0)),
                  pl.BlockSpec((TM, FX_W), lambda i: (i, 0)),
                  pl.BlockSpec((TM, MEM_W), lambda i: (i, 0)),
                  pl.BlockSpec((TM, MIX_W), lambda i: (i, 0)),
                  pl.BlockSpec((1, MIX_W), lambda i: (0, 0)),
                  pl.BlockSpec((MIX_W, D_MODEL), lambda i: (0, 0)),
                  pl.BlockSpec((TM, D_MODEL), lambda i: (i, 0))],
        out_specs=pl.BlockSpec((TM, D_MODEL), lambda i: (i, 0)),
        out_shape=jax.ShapeDtypeStruct((s, D_MODEL), F32),
        scratch_shapes=[pltpu.VMEM((TM, MIX_W), BF16)],
        compiler_params=_ARB1,
    )(ysb, yfx, ym, gf, onw, wout, x)


def _final_fwd_bwd(x, fnw, target, name):
    s = x.shape[0]

    def body(x_ref, w_ref, t_ref, dx_ref, loss_ref, dw_ref):
        @pl.when(pl.program_id(0) == 0)
        def _():
            loss_ref[...] = jnp.zeros_like(loss_ref)
            dw_ref[...] = jnp.zeros_like(dw_ref)

        xv = x_ref[...]
        w = w_ref[...]
        r = lax.rsqrt(jnp.mean(xv * xv, axis=-1, keepdims=True) + EPS)
        xh = xv * r
        err = xh * w - t_ref[...]
        part = jnp.sum(jnp.sum(err * err, axis=1, keepdims=True), axis=0, keepdims=True)
        loss_ref[...] += part * (0.5 / D_MODEL)
        dy = err * (1.0 / D_MODEL)
        dw_ref[...] += jnp.sum(dy * xh, axis=0, keepdims=True)
        dxh = dy * w
        dx_ref[...] = r * (dxh - xh * jnp.mean(dxh * xh, axis=-1, keepdims=True))

    return pl.pallas_call(
        body, name=name, grid=(s // TM,),
        in_specs=[pl.BlockSpec((TM, D_MODEL), lambda i: (i, 0)),
                  pl.BlockSpec((1, D_MODEL), lambda i: (0, 0)),
                  pl.BlockSpec((TM, D_MODEL), lambda i: (i, 0))],
        out_specs=[pl.BlockSpec((TM, D_MODEL), lambda i: (i, 0)),
                   pl.BlockSpec((1, LANES), lambda i: (0, 0)),
                   pl.BlockSpec((1, D_MODEL), lambda i: (0, 0))],
        out_shape=[jax.ShapeDtypeStruct((s, D_MODEL), F32), jax.ShapeDtypeStruct((1, LANES), F32),
                   jax.ShapeDtypeStruct((1, D_MODEL), F32)],
        compiler_params=_ARB1,
    )(x, fnw, target)


def _outproj_bwd(dxo, wout, ysb, yfx, ym, gf, onw, name):
    s = dxo.shape[0]

    def body(dx_ref, w_ref, ysb_ref, yfx_ref, ym_ref, g_ref, onw_ref,
             dysb_ref, dyfx_ref, dym_ref, dg_ref, dw_ref, donw_ref, yg_ref):
        @pl.when(pl.program_id(0) == 0)
        def _():
            dw_ref[...] = jnp.zeros_like(dw_ref)
            donw_ref[...] = jnp.zeros_like(donw_ref)

        dxb = dx_ref[...].astype(BF16)
        dyg = _dot_nt(dxb, w_ref[...])
        bd = _head_block_diag()
        for c in range(MIX_W // LANES):
            sl = slice(c * LANES, (c + 1) * LANES)
            u = _mix_chunk(c, ysb_ref, yfx_ref, ym_ref)
            r = lax.rsqrt(_head_mean(u * u, bd) + EPS)
            yn = u * r
            g = g_ref[:, sl]
            sg = _sigmoid(g)
            sil = g * sg
            onw = onw_ref[:, sl]
            e = dyg[:, sl]
            yg_ref[:, sl] = (yn * onw * sil).astype(BF16)
            donw_ref[:, sl] += jnp.sum(e * yn * sil, axis=0, keepdims=True)
            dg_ref[:, sl] = (e * yn * onw * (sg * (1.0 + g * (1.0 - sg)))).astype(BF16)
            dyn = e * onw * sil
            du = (r * (dyn - yn * _head_mean(dyn * yn, bd))).astype(BF16)
            if c < 4:
                dysb_ref[:, c * LANES:(c + 1) * LANES] = du
            elif c < 8:
                dyfx_ref[:, (c - 4) * LANES:(c - 3) * LANES] = du
            else:
                dym_ref[:, (c - 8) * LANES:(c - 7) * LANES] = du
        dw_ref[...] += _dot_tn(yg_ref[...], dxb)

    return pl.pallas_call(
        body, name=name, grid=(s // TM,),
        in_specs=[pl.BlockSpec((TM, D_MODEL), lambda i: (i, 0)),
                  pl.BlockSpec((MIX_W, D_MODEL), lambda i: (0, 0)),
                  pl.BlockSpec((TM, SB_W), lambda i: (i, 0)),
                  pl.BlockSpec((TM, FX_W), lambda i: (i, 0)),
                  pl.BlockSpec((TM, MEM_W), lambda i: (i, 0)),
                  pl.BlockSpec((TM, MIX_W), lambda i: (i, 0)),
                  pl.BlockSpec((1, MIX_W), lambda i: (0, 0))],
        out_specs=[pl.BlockSpec((TM, SB_W), lambda i: (i, 0)),
                   pl.BlockSpec((TM, FX_W), lambda i: (i, 0)),
                   pl.BlockSpec((TM, MEM_W), lambda i: (i, 0)),
                   pl.BlockSpec((TM, MIX_W), lambda i: (i, 0)),
                   pl.BlockSpec((MIX_W, D_MODEL), lambda i: (0, 0)),
                   pl.BlockSpec((1, MIX_W), lambda i: (0, 0))],
        out_shape=[jax.ShapeDtypeStruct((s, SB_W), BF16), jax.ShapeDtypeStruct((s, FX_W), BF16),
                   jax.ShapeDtypeStruct((s, MEM_W), BF16), jax.ShapeDtypeStruct((s, MIX_W), BF16),
                   jax.ShapeDtypeStruct((MIX_W, D_MODEL), F32), jax.ShapeDtypeStruct((1, MIX_W), F32)],
        scratch_shapes=[pltpu.VMEM((TM, MIX_W), BF16)],
        compiler_params=_ARB1,
    )(dxo, wout, ysb, yfx, ym, gf, onw)


def _row_dots(do, o, m0):
    prod = do.astype(F32) * o
    zero = jnp.zeros_like(prod)
    return (jnp.sum(jnp.where(m0, prod, zero), axis=1, keepdims=True),
            jnp.sum(jnp.where(m0, zero, prod), axis=1, keepdims=True))


def _sb_bwd(qkv, o, do, name, rider=None):
    s = qkv.shape[0]
    nq = s // T

    def body(q_ref, k_ref, v_ref, o_ref, do_ref, dq_ref, dk_ref, dv_ref,
             dqa_ref, dka_ref, dva_ref, rl_ref, rg_ref, dzs_ref, abs_ref):
        step_id = pl.program_id(1)

        @pl.when(step_id == 0)
        def _():
            dka_ref[...] = jnp.zeros_like(dka_ref)
            dva_ref[...] = jnp.zeros_like(dva_ref)

        m0 = _pair_masks()
        strict = _iota2((T, T), 0) > _iota2((T, T), 1)
        u2 = _stack2(jnp.where(strict, 1.0, 0.0).astype(BF16))
        hs = range(2)

        def query_tile(i, rows):
            qh = _split_pair(q_ref[rows, :] * jnp.asarray(Q_SCALE, BF16), m0)
            doh = _split_pair(do_ref[rows, :], m0)
            dsum = _row_dots(do_ref[rows, :], o_ref[rows, :], m0)
            dqa_ref[...] = jnp.zeros_like(dqa_ref)
            rl_ref[...] = jnp.zeros_like(rl_ref)
            rg_ref[...] = jnp.zeros_like(rg_ref)

            def flush(j):
                off = pl.multiple_of(j * T, T)
                k = k_ref[pl.ds(off, T), :]
                for h in hs:
                    dqa_ref[h] += _dot(dzs_ref[h], k)
                dka_ref[pl.ds(off, T), :] += (_dot_tn(dzs_ref[0], qh[0])
                                              + _dot_tn(dzs_ref[1], qh[1]))
                dva_ref[pl.ds(off, T), :] += (_dot_tn(abs_ref[0], doh[0])
                                              + _dot_tn(abs_ref[1], doh[1]))

            def tile(j, diag):
                off = pl.multiple_of(j * T, T)
                k = k_ref[pl.ds(off, T), :]
                v = v_ref[pl.ds(off, T), :]
                z = [_dot_nt(qh[h], k) for h in hs]
                da = [_dot_nt(doh[h], v) for h in hs]
                if not diag:
                    flush(j + 1)
                la = [jnp.minimum(z[h], 0.0) - jnp.log(1.0 + jnp.exp(-jnp.abs(z[h]))) for h in hs]
                lf = [la[h] - z[h] for h in hs]
                if diag:
                    lf = [jnp.where(strict, lf[h], 0.0) for h in hs]
                cin = [_cum2(lf[h], u2) for h in hs]
                a = [jnp.exp(la[h] + cin[h] + rl_ref[h]) for h in hs]
                if diag:
                    a = [jnp.where(strict, a[h], 0.0) for h in hs]
                ab = [a[h].astype(BF16) for h in hs]
                g = [ab[h].astype(F32) * da[h] for h in hs]
                gin = [_cum2(g[h], u2) for h in hs]
                dz = [g[h] - jnp.exp(la[h]) * ((dsum[h] - rg_ref[h]) - gin[h]) for h in hs]
                if diag:
                    dz = [jnp.where(strict, dz[h], 0.0) for h in hs]
                for h in hs:
                    rl_ref[h] += cin[h][:, 0:1] + lf[h][:, 0:1]
                    rg_ref[h] += gin[h][:, 0:1] + g[h][:, 0:1]
                    dzs_ref[h] = dz[h].astype(BF16)
                    abs_ref[h] = ab[h]

            tile(i, True)

            def more(state):
                j, top = state
                return jnp.logical_and(j >= 0, top > UNDERFLOW)

            def step(state):
                j, _ = state
                tile(j, False)
                return j - 1, _running_top(rl_ref)

            j_left, _ = lax.while_loop(more, step, (i - 1, _running_top(rl_ref)))
            flush(j_left + 1)
            dq_ref[rows, :] = (jnp.where(m0, dqa_ref[0], dqa_ref[1]) * Q_SCALE).astype(BF16)

        for n in range(QPS):
            query_tile(QPS * step_id + n, slice(n * T, (n + 1) * T))

        @pl.when(step_id == nq // QPS - 1)
        def _():
            dk_ref[...] = dka_ref[...].astype(BF16)
            dv_ref[...] = dva_ref[...].astype(BF16)

    nb = SB_W // LANES
    (dq, dk, dv), lands = _ride(dict(
        body=body, name=name, grid=(nb, nq // QPS),
        in_specs=[pl.BlockSpec((QPS * T, LANES), lambda p, i: (i, p)),
                  pl.BlockSpec((s, LANES), lambda p, i: (0, nb + p)),
                  pl.BlockSpec((s, LANES), lambda p, i: (0, 2 * nb + p)),
                  pl.BlockSpec((QPS * T, LANES), lambda p, i: (i, p)),
                  pl.BlockSpec((QPS * T, LANES), lambda p, i: (i, p))],
        out_specs=[pl.BlockSpec((QPS * T, LANES), lambda p, i: (i, p)),
                   pl.BlockSpec((s, LANES), lambda p, i: (0, p)),
                   pl.BlockSpec((s, LANES), lambda p, i: (0, p))],
        out_shape=[jax.ShapeDtypeStruct((s, SB_W), BF16)] * 3,
        scratch_shapes=[pltpu.VMEM((2, T, LANES), F32), pltpu.VMEM((s, LANES), F32),
                        pltpu.VMEM((s, LANES), F32), pltpu.VMEM((2, T, 1), F32),
                        pltpu.VMEM((2, T, 1), F32), pltpu.VMEM((2, T, T), BF16),
                        pltpu.VMEM((2, T, T), BF16)],
        compiler_params=_ARB2, operands=[qkv, qkv, qkv, o, do]), rider)
    return dq, dk, dv, lands


def _fox_bwd(qkv, fqb, frow, fbounds, o, lse, do, name, rider=None):
    s = qkv.shape[0]
    nq = s // T

    def body(q_ref, k_ref, v_ref, fq_ref, fr_ref, fb_ref, o_ref, lse_ref, do_ref,
             dq_ref, dk_ref, dv_ref, df_ref, dqa_ref, dka_ref, dva_ref, dfa_ref, dls_ref, pbs_ref):
        step_id = pl.program_id(1)

        @pl.when(step_id == 0)
        def _():
            dka_ref[...] = jnp.zeros_like(dka_ref)
            dva_ref[...] = jnp.zeros_like(dva_ref)
            dfa_ref[...] = jnp.zeros_like(dfa_ref)

        m0 = _pair_masks()
        causal = _iota2((T, T), 0) >= _iota2((T, T), 1)
        hs = range(2)

        def query_tile(i, rows):
            qh = _split_pair(q_ref[rows, :] * jnp.asarray(Q_SCALE, BF16), m0)
            doh = _split_pair(do_ref[rows, :], m0)
            dsum = _row_dots(do_ref[rows, :], o_ref[rows, :], m0)
            fq = fq_ref[rows, :]
            fqh = (fq[:, 0:1], fq[:, HEAD_DIM:HEAD_DIM + 1])
            lse = lse_ref[rows, :]
            lseh = (lse[:, 0:1], lse[:, HEAD_DIM:HEAD_DIM + 1])
            dqa_ref[...] = jnp.zeros_like(dqa_ref)

            def flush(j):
                off = pl.multiple_of(j * T, T)
                k = k_ref[pl.ds(off, T), :]
                for h in hs:
                    dqa_ref[h] += _dot(dls_ref[h], k)
                dka_ref[pl.ds(off, T), :] += (_dot_tn(dls_ref[0], qh[0])
                                              + _dot_tn(dls_ref[1], qh[1]))
                dva_ref[pl.ds(off, T), :] += (_dot_tn(pbs_ref[0], doh[0])
                                              + _dot_tn(pbs_ref[1], doh[1]))

            def tile(j, diag):
                off = pl.multiple_of(j * T, T)
                k = k_ref[pl.ds(off, T), :]
                v = v_ref[pl.ds(off, T), :]
                sc = [_dot_nt(qh[h], k) + fqh[h] - fr_ref[h:h + 1, pl.ds(off, T)] for h in hs]
                dp = [_dot_nt(doh[h], v) for h in hs]
                if not diag:
                    flush(j + 1)
                p = [jnp.exp(sc[h] - lseh[h]) for h in hs]
                if diag:
                    p = [jnp.where(causal, p[h], 0.0) for h in hs]
                dl = [p[h] * (dp[h] - dsum[h]) for h in hs]
                for h in hs:
                    dls_ref[h] = dl[h].astype(BF16)
                    pbs_ref[h] = p[h].astype(BF16)
                    dfa_ref[h:h + 1, pl.ds(off, T)] -= jnp.sum(dl[h], axis=0, keepdims=True)

            tile(i, True)
            j_left = _fox_tiles_left(i, pl.program_id(0), nq, fb_ref, tile)
            flush(j_left + 1)
            dq_ref[rows, :] = (jnp.where(m0, dqa_ref[0], dqa_ref[1]) * Q_SCALE).astype(BF16)

        for n in range(QPS):
            query_tile(QPS * step_id + n, slice(n * T, (n + 1) * T))

        @pl.when(step_id == nq // QPS - 1)
        def _():
            dk_ref[...] = dka_ref[...].astype(BF16)
            dv_ref[...] = dva_ref[...].astype(BF16)
            df_ref[...] = dfa_ref[...]

    nb = FX_W // LANES
    base = 3 * SB_W // LANES
    (dq, dk, dv, df), lands = _ride(dict(
        body=body, name=name, grid=(nb, nq // QPS),
        in_specs=[pl.BlockSpec((QPS * T, LANES), lambda p, i: (i, base + p)),
                  pl.BlockSpec((s, LANES), lambda p, i: (0, base + nb + p)),
                  pl.BlockSpec((s, LANES), lambda p, i: (0, base + 2 * nb + p)),
                  pl.BlockSpec((QPS * T, LANES), lambda p, i: (i, p)),
                  pl.BlockSpec((None, 2, s), lambda p, i: (p, 0, 0)),
                  pl.BlockSpec(memory_space=pltpu.SMEM),
                  pl.BlockSpec((QPS * T, LANES), lambda p, i: (i, p)),
                  pl.BlockSpec((QPS * T, LANES), lambda p, i: (i, p)),
                  pl.BlockSpec((QPS * T, LANES), lambda p, i: (i, p))],
        out_specs=[pl.BlockSpec((QPS * T, LANES), lambda p, i: (i, p)),
                   pl.BlockSpec((s, LANES), lambda p, i: (0, p)),
                   pl.BlockSpec((s, LANES), lambda p, i: (0, p)),
                   pl.BlockSpec((None, 2, s), lambda p, i: (p, 0, 0))],
        out_shape=[jax.ShapeDtypeStruct((s, FX_W), BF16)] * 3
        + [jax.ShapeDtypeStruct((nb, 2, s), F32)],
        scratch_shapes=[pltpu.VMEM((2, T, LANES), F32), pltpu.VMEM((s, LANES), F32),
                        pltpu.VMEM((s, LANES), F32), pltpu.VMEM((2, s), F32),
                        pltpu.VMEM((2, T, T), BF16), pltpu.VMEM((2, T, T), BF16)],
        compiler_params=_ARB2, operands=[qkv, qkv, qkv, fqb, frow, fbounds, o, lse, do]), rider)
    return dq, dk, dv, df, lands


def _fox_prep_bwd(dfrow, gf, bpad, name):
    s = gf.shape[0]

    def body(df_ref, fl_ref, b_ref, dfl_ref, db_ref):
        tri = jnp.where(_iota2((T, T), 0) <= _iota2((T, T), 1), 1.0, 0.0).astype(BF16)
        carry = jnp.zeros((1, LANES), F32)
        db = jnp.zeros((1, LANES), F32)
        fill = jnp.zeros((LANES - FOX_HEADS, T), F32)
        for blk in reversed(range(s // T)):
            rows = slice(blk * T, (blk + 1) * T)
            c = _tri3(tri, jnp.concatenate([df_ref[:, rows], fill], axis=0), _dot_nt) + carry
            carry = c[0:1, :]
            dfl = c / (1.0 + jnp.exp(fl_ref[rows, :] + b_ref[...]))
            dfl_ref[rows, :] = dfl.astype(BF16)
            db = db + jnp.sum(dfl, axis=0, keepdims=True)
        db_ref[...] = db

    return pl.pallas_call(
        body, name=name, grid=(1,),
        in_specs=[pl.BlockSpec((FOX_HEADS, s), lambda i: (0, 0)),
                  pl.BlockSpec((s, LANES), lambda i: (0, MIX_W // LANES)),
                  pl.BlockSpec((1, LANES), lambda i: (0, 0))],
        out_specs=[pl.BlockSpec((s, LANES), lambda i: (0, 0)),
                   pl.BlockSpec((1, LANES), lambda i: (0, 0))],
        out_shape=[jax.ShapeDtypeStruct((s, LANES), BF16), jax.ShapeDtypeStruct((1, LANES), F32)],
        compiler_params=_ARB1,
    )(dfrow, gf, bpad)


def _mem_bwd(qkv, kv, o, lse, do, name):
    s = qkv.shape[0]
    n = kv.shape[0]

    def body(q_ref, k_ref, v_ref, o_ref, lse_ref, do_ref, dq_ref, dk_ref, dv_ref):
        @pl.when(pl.program_id(1) == 0)
        def _():
            dk_ref[...] = jnp.zeros_like(dk_ref)
            dv_ref[...] = jnp.zeros_like(dv_ref)

        m0 = _pair_masks()
        qh = _split_pair(q_ref[...] * jnp.asarray(Q_SCALE, BF16), m0)
        doh = _split_pair(do_ref[...], m0)
        dsum = _row_dots(do_ref[...], o_ref[...], m0)
        lse = lse_ref[...]
        lseh = (lse[:, 0:1], lse[:, HEAD_DIM:HEAD_DIM + 1])
        k = k_ref[...]
        v = v_ref[...]
        dqs = []
        for h in range(2):
            p = jnp.exp(_dot_nt(qh[h], k) - lseh[h])
            dl = p * (_dot_nt(doh[h], v) - dsum[h])
            dlb = dl.astype(BF16)
            dqs.append(_dot(dlb, k))
            dk_ref[...] += _dot_tn(dlb, qh[h])
            dv_ref[...] += _dot_tn(p.astype(BF16), doh[h])
        dq_ref[...] = (jnp.where(m0, dqs[0], dqs[1]) * Q_SCALE).astype(BF16)

    nb = MEM_W // LANES
    base = (3 * SB_W + 3 * FX_W) // LANES
    return pl.pallas_call(
        body, name=name, grid=(nb, s // TQM),
        in_specs=[pl.BlockSpec((TQM, LANES), lambda p, i: (i, base + p)),
                  pl.BlockSpec((n, LANES), lambda p, i: (0, p)),
                  pl.BlockSpec((n, LANES), lambda p, i: (0, nb + p)),
                  pl.BlockSpec((TQM, LANES), lambda p, i: (i, p)),
                  pl.BlockSpec((TQM, LANES), lambda p, i: (i, p)),
                  pl.BlockSpec((TQM, LANES), lambda p, i: (i, p))],
        out_specs=[pl.BlockSpec((TQM, LANES), lambda p, i: (i, p)),
                   pl.BlockSpec((n, LANES), lambda p, i: (0, p)),
                   pl.BlockSpec((n, LANES), lambda p, i: (0, p))],
        out_shape=[jax.ShapeDtypeStruct((s, MEM_W), BF16), jax.ShapeDtypeStruct((n, MEM_W), F32),
                   jax.ShapeDtypeStruct((n, MEM_W), F32)],
        compiler_params=_ARB2,
    )(qkv, kv, kv, o, lse, do)


def _memkv_bwd(mem, mnw, wkv, dk, dv, name):
    n = mem.shape[0]

    def body(mem_ref, mnw_ref, w_ref, dk_ref, dv_ref, dw_ref, dmnw_ref):
        mv = mem_ref[...]
        r = lax.rsqrt(jnp.mean(mv * mv, axis=-1, keepdims=True) + EPS)
        mh = mv * r
        hm = (mh * mnw_ref[...]).astype(BF16)
        dkv = jnp.concatenate([dk_ref[...], dv_ref[...]], axis=1).astype(BF16)
        dw_ref[...] = _dot_tn(hm, dkv)
        dhm = _dot_nt(dkv, w_ref[...])
        dmnw_ref[...] = jnp.sum(dhm * mh, axis=0, keepdims=True)

    return pl.pallas_call(
        body, name=name, grid=(1,),
        in_specs=[pl.BlockSpec((n, D_MODEL), lambda i: (0, 0)),
                  pl.BlockSpec((1, D_MODEL), lambda i: (0, 0)),
                  pl.BlockSpec((D_MODEL, 2 * MEM_W), lambda i: (0, 0)),
                  pl.BlockSpec((n, MEM_W), lambda i: (0, 0)),
                  pl.BlockSpec((n, MEM_W), lambda i: (0, 0))],
        out_specs=[pl.BlockSpec((D_MODEL, 2 * MEM_W), lambda i: (0, 0)),
                   pl.BlockSpec((1, D_MODEL), lambda i: (0, 0))],
        out_shape=[jax.ShapeDtypeStruct((D_MODEL, 2 * MEM_W), F32),
                   jax.ShapeDtypeStruct((1, D_MODEL), F32)],
        compiler_params=_ARB1,
    )(mem, mnw, wkv, dk, dv)


def _inproj_bwd_dx(pieces, w_r, x, nw, dxo, name, rider=None):
    s = x.shape[0]
    n = len(pieces)
    widths = [p.shape[1] for p in pieces]

    def body(*refs):
        piece_refs = refs[:n]
        w_ref, x_ref, nw_ref, dxo_ref, dx_ref, h_ref, dnw_ref, dp_ref = refs[n:]

        @pl.when(pl.program_id(0) == 0)
        def _():
            dnw_ref[...] = jnp.zeros_like(dnw_ref)

        col = 0
        for r, wd in zip(piece_refs, widths):
            dp_ref[:, col:col + wd] = r[...]
            col += wd
        dp_ref[:, col:] = jnp.zeros((TM, WR_W - col), BF16)
        dh = _dot(dp_ref[...], w_ref[...])
        xv = x_ref[...]
        nw = nw_ref[...]
        r = lax.rsqrt(jnp.mean(xv * xv, axis=-1, keepdims=True) + EPS)
        xh = xv * r
        h_ref[...] = (xh * nw).astype(BF16)
        dnw_ref[...] += jnp.sum(dh * xh, axis=0, keepdims=True)
        dxh = dh * nw
        dx_ref[...] = r * (dxh - xh * jnp.mean(dxh * xh, axis=-1, keepdims=True)) + dxo_ref[...]

    (dx, h, dnw, dproj), lands = _ride(dict(
        body=body, name=name, grid=(s // TM,),
        in_specs=[pl.BlockSpec((TM, wd), lambda i: (i, 0)) for wd in widths]
        + [pl.BlockSpec((WR_W, D_MODEL), lambda i: (0, 0)),
           pl.BlockSpec((TM, D_MODEL), lambda i: (i, 0)),
           pl.BlockSpec((1, D_MODEL), lambda i: (0, 0)),
           pl.BlockSpec((TM, D_MODEL), lambda i: (i, 0))],
        out_specs=[pl.BlockSpec((TM, D_MODEL), lambda i: (i, 0)),
                   pl.BlockSpec((TM, D_MODEL), lambda i: (i, 0)),
                   pl.BlockSpec((1, D_MODEL), lambda i: (0, 0)),
                   pl.BlockSpec((TM, WR_W), lambda i: (i, 0))],
        out_shape=[jax.ShapeDtypeStruct((s, D_MODEL), F32), jax.ShapeDtypeStruct((s, D_MODEL), BF16),
                   jax.ShapeDtypeStruct((1, D_MODEL), F32), jax.ShapeDtypeStruct((s, WR_W), BF16)],
        scratch_shapes=[], compiler_params=_ARB1, operands=[*pieces, w_r, x, nw, dxo]), rider)
    return dx, h, dnw, dproj, lands


def _inproj_bwd_dw(h, dproj, name):
    s = dproj.shape[0]
    tn = 256

    def body(h_ref, dp_ref, dw_ref):
        dw_ref[...] = _dot_tn(dp_ref[...], h_ref[...])

    return pl.pallas_call(
        body, name=name, grid=(WR_W // tn,),
        in_specs=[pl.BlockSpec((s, D_MODEL), lambda j: (0, 0)),
                  pl.BlockSpec((s, tn), lambda j: (0, j))],
        out_specs=pl.BlockSpec((tn, D_MODEL), lambda j: (j, 0)),
        out_shape=jax.ShapeDtypeStruct((WR_W, D_MODEL), F32),
        compiler_params=_ARB1,
    )(h, dproj)


def _rearrange_w_in(wt):
    pad = jnp.zeros((FL_PAD - FOX_HEADS,) + wt.shape[1:], wt.dtype)
    return jnp.concatenate([wt[:3072], wt[3080:3336], wt[3336:IN_W], wt[3072:3080], pad], axis=0)


def _restore_w_in(g):
    gate0 = QKV_W
    fl0 = QKV_W + MIX_W
    return jnp.concatenate(
        [g[:3072], g[fl0:fl0 + FOX_HEADS], g[3072:QKV_W], g[gate0:fl0]], axis=0)


def _pad_lanes(v, width=LANES):
    return jnp.pad(v, (0, width - v.shape[0])).reshape(1, width)


def _layer_fwd(xs, mem, nw, w_r, b_forget, mnw, late, onw, l, travel=None):
    s = xs.shape[0]
    bpad = _pad_lanes(b_forget)
    travel = _Travel(travel)
    qkv, gf, _ = travel.ride(2, _inproj_fwd, xs, nw, w_r, f"inproj_fwd_{l}")
    fqb, frow, fbounds = _fox_prep_fwd(gf, qkv, bpad, f"fox_prep_fwd_{l}")
    frow = frow.reshape(FOX_HEADS // 2, 2, s)
    ysb, _ = travel.ride(0, _sb_fwd, qkv, f"sb_fwd_{l}")
    yfx, lse_fx, _ = travel.ride(1, _fox_fwd, qkv, fqb, frow, fbounds, f"fox_fwd_{l}")
    wkv, wout = late(travel.lands)
    kv = _memkv_fwd(mem, mnw, wkv, f"memkv_fwd_{l}")
    ym, lse_m = _mem_fwd(qkv, kv, f"mem_fwd_{l}")
    xn = _outproj_fwd(ysb, yfx, ym, gf, onw, wout, xs, f"outproj_fwd_{l}")
    saved = (xs, nw, mnw, onw, bpad, qkv, gf, fqb, frow, fbounds, ysb, yfx, lse_fx, kv, ym, lse_m)
    return xn, saved, travel.lands, (wkv, wout)


class _Travel:
    def __init__(self, plan):
        self.plan = plan
        self.lands = None if plan is None else _new_lands(plan[0], plan[1])

    def ride(self, n, fn, *args):
        if self.plan is None or n >= len(self.plan[2]) or self.plan[2][n] is None:
            return fn(*args)
        srcs, scatter, legs = self.plan
        idx, rows = legs[n]
        out = fn(*args, rider=_Rider([srcs[a] for a in idx], [self.lands[a] for a in idx],
                                     scatter, rows, relay=not scatter))
        for a, land in zip(idx, out[-1]):
            self.lands[a] = land
        return out


def _layer_bwd(dx, saved, mem, w_r, wkv, wout, l, travel=None):
    xs, nw, mnw, onw, bpad, qkv, gf, fqb, frow, fbounds, ysb, yfx, lse_fx, kv, ym, lse_m = saved
    s = xs.shape[0]
    dysb, dyfx, dym, dgate, dwout, donw = _outproj_bwd(
        dx, wout, ysb, yfx, ym, gf, onw, f"outproj_bwd_{l}")
    travel = _Travel(None if travel is None else travel(dwout))
    sdq, sdk, sdv, _ = travel.ride(0, _sb_bwd, qkv, ysb, dysb, f"sb_bwd_{l}")
    fdq, fdk, fdv, dfrow, _ = travel.ride(1, _fox_bwd, qkv, fqb, frow, fbounds, yfx, lse_fx, dyfx,
                                          f"fox_bwd_{l}")
    dfl, db = _fox_prep_bwd(dfrow.reshape(FOX_HEADS, s), gf, bpad, f"fox_prep_bwd_{l}")
    dmq, dmk, dmv = _mem_bwd(qkv, kv, ym, lse_m, dym, f"mem_bwd_{l}")
    dwkv, dmnw = _memkv_bwd(mem, mnw, wkv, dmk, dmv, f"memkv_bwd_{l}")
    dx, ht, dnw, dproj, _ = travel.ride(2, _inproj_bwd_dx,
                                        [sdq, sdk, sdv, fdq, fdk, fdv, dmq, dgate, dfl],
                                        w_r, xs, nw, dx, f"inproj_bwd_dx_{l}")
    dwr = _inproj_bwd_dw(ht, dproj, f"inproj_bwd_dw_{l}")
    grads = dict(norm_w=dnw[0], w_r=dwr, b_forget=db[0, :FOX_HEADS], mem_norm_w=dmnw[0],
                 w_mem_kv=dwkv, out_norm_w=donw[0], w_out=dwout)
    return dx, grads, travel.lands


_ANY = pl.BlockSpec(memory_space=pl.ANY)


def _my_place():
    return lax.axis_index("x"), lax.axis_index("y"), lax.axis_index("c")


def _flip(v, bit):
    return 1 - v if bit else v


def _block_index(px, py, pc):
    return 4 * px + 2 * py + pc


def _all_gather_weights(shards, name):
    n = len(shards)

    def body(*refs):
        ins, outs = refs[:n], refs[n:2 * n]
        send_sems, recv_sems, local_sems = refs[2 * n:]
        x, y, c = _my_place()
        me = (x, y, c)
        sibling = (x, y, 1 - c)
        chips = [(1 - x, y), (x, 1 - y), (1 - x, 1 - y)]

        def copy(a, k, block, to, src=None):
            dst = outs[a].at[_block_index(*block)]
            return pltpu.make_async_remote_copy(
                src_ref=dst if src is None else src, dst_ref=dst,
                send_sem=send_sems.at[a, k], recv_sem=recv_sems.at[a, k],
                device_id=to, device_id_type=pl.DeviceIdType.MESH)

        mine = [pltpu.make_async_copy(ins[a], outs[a].at[_block_index(*me)], local_sems.at[a])
                for a in range(n)]
        for cp in mine:
            cp.start()
        first = []
        for a in range(n):
            first.append(copy(a, 0, me, sibling, src=ins[a]))
            first += [copy(a, 1 + j, me, (*chip, c), src=ins[a]) for j, chip in enumerate(chips)]
        for cp in first:
            cp.start()
        passed = []
        for j, chip in enumerate(chips):
            for a in range(n):
                copy(a, 1 + j, (*chip, c), me).wait_recv()
                fwd = copy(a, 4 + j, (*chip, c), sibling)
                fwd.start()
                passed.append(fwd)
        for a in range(n):
            copy(a, 0, sibling, me).wait_recv()
            for j, chip in enumerate(chips):
                copy(a, 4 + j, (*chip, 1 - c), me).wait_recv()
        for cp in first + passed:
            cp.wait_send()
        for cp in mine:
            cp.wait()

    return pl.pallas_call(
        body, name=name,
        in_specs=[_ANY] * n, out_specs=[_ANY] * n,
        out_shape=[jax.ShapeDtypeStruct((N_DEV,) + v.shape, v.dtype) for v in shards],
        scratch_shapes=[pltpu.SemaphoreType.DMA((n, 7)), pltpu.SemaphoreType.DMA((n, 7)),
                        pltpu.SemaphoreType.DMA((n,))],
    )(*shards)


def _exchange_blocks(blocked, name):
    n = len(blocked)

    def body(*refs):
        ins, outs = refs[:n], refs[n:2 * n]
        send_sems, recv_sems, local_sems = refs[2 * n:]
        x, y, c = _my_place()
        mine_idx = _block_index(x, y, c)
        local = [pltpu.make_async_copy(ins[a].at[mine_idx], outs[a].at[mine_idx], local_sems.at[a])
                 for a in range(n)]
        for cp in local:
            cp.start()
        sends, arrivals = [], []
        for r in range(1, N_DEV):
            peer = (_flip(x, r & 4), _flip(y, r & 2), _flip(c, r & 1))
            peer_idx = _block_index(*peer)
            for a in range(n):
                sems = dict(send_sem=send_sems.at[a, r - 1], recv_sem=recv_sems.at[a, r - 1],
                            device_id=peer, device_id_type=pl.DeviceIdType.MESH)
                sends.append(pltpu.make_async_remote_copy(
                    src_ref=ins[a].at[peer_idx], dst_ref=outs[a].at[mine_idx], **sems))
                arrivals.append(pltpu.make_async_remote_copy(
                    src_ref=ins[a].at[peer_idx], dst_ref=outs[a].at[peer_idx], **sems))
        for cp in sends:
            cp.start()
        for cp in arrivals:
            cp.wait_recv()
        for cp in sends:
            cp.wait_send()
        for cp in local:
            cp.wait()

    return pl.pallas_call(
        body, name=name,
        in_specs=[_ANY] * n, out_specs=[_ANY] * n,
        out_shape=[jax.ShapeDtypeStruct(v.shape, v.dtype) for v in blocked],
        scratch_shapes=[pltpu.SemaphoreType.DMA((n, 7)), pltpu.SemaphoreType.DMA((n, 7)),
                        pltpu.SemaphoreType.DMA((n,))],
    )(*blocked)


N_CHIP = N_DEV // 2


def _pair_swap(blocked, name):
    n = len(blocked)

    def body(*refs):
        ins, outs = refs[:n], refs[n:2 * n]
        send_sems, recv_sems = refs[2 * n:]
        x, y, c = _my_place()
        copies = [pltpu.make_async_remote_copy(
            src_ref=ins[a].at[j, 1 - c], dst_ref=outs[a].at[j],
            send_sem=send_sems.at[N_CHIP * a + j], recv_sem=recv_sems.at[N_CHIP * a + j],
            device_id=(x, y, 1 - c), device_id_type=pl.DeviceIdType.MESH)
            for a in range(n) for j in range(N_CHIP)]
        for cp in copies:
            cp.start()
        for cp in copies:
            cp.wait_recv()
        for cp in copies:
            cp.wait_send()

    return pl.pallas_call(
        body, name=name,
        in_specs=[_ANY] * n, out_specs=[_ANY] * n,
        out_shape=[jax.ShapeDtypeStruct((N_CHIP,) + v.shape[2:], v.dtype) for v in blocked],
        scratch_shapes=[pltpu.SemaphoreType.DMA((N_CHIP * n,)),
                        pltpu.SemaphoreType.DMA((N_CHIP * n,))],
    )(*blocked)


def _pair_add(mine, theirs, name):
    _, nrow, ncol = mine.shape

    def body(a_ref, b_ref, o_ref):
        o_ref[...] = (a_ref[...].astype(F32) + b_ref[...].astype(F32)).astype(BF16)

    blk = pl.BlockSpec((None, nrow, ncol), lambda j: (j, 0, 0))
    return pl.pallas_call(
        body, name=name, grid=(N_CHIP,), in_specs=[blk, blk], out_specs=blk,
        out_shape=jax.ShapeDtypeStruct(mine.shape, BF16), compiler_params=_ARB1,
    )(mine, theirs)


def _chip_exchange(by_chip, to_all, name):
    n, m = len(by_chip), len(to_all)

    def body(*refs):
        ins, alls = refs[:n], refs[n:n + m]
        outs, all_outs = refs[n + m:2 * n + m], refs[2 * n + m:2 * (n + m)]
        send_sems, recv_sems, local_sems = refs[2 * (n + m):]
        x, y, c = _my_place()
        my_chip, mine_idx = 2 * x + y, _block_index(x, y, c)
        local = [pltpu.make_async_copy(ins[a].at[my_chip], outs[a].at[my_chip], local_sems.at[a])
                 for a in range(n)]
        local += [pltpu.make_async_copy(alls[b].at[mine_idx], all_outs[b].at[mine_idx],
                                        local_sems.at[n + b]) for b in range(m)]
        for cp in local:
            cp.start()
        sends, arrivals = [], []
        k = 0
        for r in range(1, N_DEV):
            peer = (_flip(x, r & 4), _flip(y, r & 2), _flip(c, r & 1))
            peer_chip, peer_idx = 2 * peer[0] + peer[1], _block_index(*peer)
            pairs = [(alls[b].at[mine_idx], all_outs[b].at[mine_idx], all_outs[b].at[peer_idx])
                     for b in range(m)]
            if not r & 1:
                pairs += [(ins[a].at[peer_chip], outs[a].at[my_chip], outs[a].at[peer_chip])
                          for a in range(n)]
            for src, there, here in pairs:
                sems = dict(send_sem=send_sems.at[k], recv_sem=recv_sems.at[k], device_id=peer,
                            device_id_type=pl.DeviceIdType.MESH)
                sends.append(pltpu.make_async_remote_copy(src_ref=src, dst_ref=there, **sems))
                arrivals.append(pltpu.make_async_remote_copy(src_ref=src, dst_ref=here, **sems))
                k += 1
        for cp in sends:
            cp.start()
        for cp in arrivals:
            cp.wait_recv()
        for cp in sends:
            cp.wait_send()
        for cp in local:
            cp.wait()

    n_copies = 7 * m + 3 * n
    return pl.pallas_call(
        body, name=name,
        in_specs=[_ANY] * (n + m), out_specs=[_ANY] * (n + m),
        out_shape=[jax.ShapeDtypeStruct(v.shape, v.dtype) for v in by_chip + to_all],
        scratch_shapes=[pltpu.SemaphoreType.DMA((n_copies,)), pltpu.SemaphoreType.DMA((n_copies,)),
                        pltpu.SemaphoreType.DMA((n + m,))],
    )(*by_chip, *to_all)


class _Rider(NamedTuple):
    srcs: list
    lands: list
    scatter: bool
    part: list
    relay: bool = False


def _window(ref, part):
    if part is None:
        return ref
    dim, start, size = part
    return ref.at[(slice(None),) * dim + (pl.ds(start, size),)]


def _relay_copies(srcs, lands, send_sems, recv_sems, rider):
    x, y, c = _my_place()
    me, sibling = (x, y, c), (x, y, 1 - c)
    chips = [(1 - x, y), (x, 1 - y), (1 - x, 1 - y)]
    first, from_chips, passed, last = [], [], [], []
    for a in range(len(srcs)):
        def copy(k, block, to, src=None, a=a):
            dst = _window(lands[a].at[_block_index(*block)], rider.part[a])
            return pltpu.make_async_remote_copy(
                src_ref=dst if src is None else src, dst_ref=dst,
                send_sem=send_sems.at[7 * a + k], recv_sem=recv_sems.at[7 * a + k],
                device_id=to, device_id_type=pl.DeviceIdType.MESH)

        mine = _window(srcs[a], rider.part[a])
        first.append(copy(0, me, sibling, src=mine))
        first += [copy(1 + j, me, (*chip, c), src=mine) for j, chip in enumerate(chips)]
        from_chips += [copy(1 + j, (*chip, c), me) for j, chip in enumerate(chips)]
        passed += [copy(4 + j, (*chip, c), sibling) for j, chip in enumerate(chips)]
        last.append(copy(0, sibling, me))
        last += [copy(4 + j, (*chip, 1 - c), me) for j, chip in enumerate(chips)]
    return first, from_chips, passed, last


def _new_lands(srcs, scatter):
    return [lax.empty(v.shape if scatter else (N_DEV,) + v.shape, v.dtype) for v in srcs]


def _rider_copies(srcs, lands, send_sems, recv_sems, rider):
    x, y, c = _my_place()
    mine_idx = _block_index(x, y, c)

    def window(ref, a):
        return _window(ref, rider.part[a])

    sends, arrivals = [], []
    for r in range(1, N_DEV):
        peer = (_flip(x, r & 4), _flip(y, r & 2), _flip(c, r & 1))
        peer_idx = _block_index(*peer)
        for a in range(len(srcs)):
            src = window(srcs[a].at[peer_idx] if rider.scatter else srcs[a], a)
            k = 7 * a + r - 1
            sems = dict(send_sem=send_sems.at[k], recv_sem=recv_sems.at[k],
                        device_id=peer, device_id_type=pl.DeviceIdType.MESH)
            sends.append(pltpu.make_async_remote_copy(
                src_ref=src, dst_ref=window(lands[a].at[mine_idx], a), **sems))
            arrivals.append(pltpu.make_async_remote_copy(
                src_ref=src, dst_ref=window(lands[a].at[peer_idx], a), **sems))
    return sends, arrivals


def _ride(call, rider):
    call = dict(call)
    body, grid = call.pop("body"), call["grid"]
    operands = call.pop("operands")
    if rider is None:
        return list(pl.pallas_call(body, **call)(*operands)), None
    n_in, n_out = len(call["in_specs"]), len(call["out_specs"])
    n_scratch = len(call["scratch_shapes"])
    m = len(rider.srcs)

    def riding(*refs):
        main_in, srcs, lands = refs[:n_in], refs[n_in:n_in + m], refs[n_in + m:n_in + 2 * m]
        main_out = refs[n_in + 2 * m:n_in + 2 * m + n_out]
        rest = refs[n_in + 3 * m + n_out:]
        send_sems, recv_sems = rest[n_scratch:]
        at = [pl.program_id(d) for d in range(len(grid))]
        first = functools.reduce(jnp.logical_and, [p == 0 for p in at])
        last = functools.reduce(jnp.logical_and, [p == g - 1 for p, g in zip(at, grid)])
        if rider.relay:
            sends, from_chips, passed, arrivals = _relay_copies(
                srcs, lands, send_sems, recv_sems, rider)
            step, steps = 0, 1
            for p, g in zip(at, grid):
                step, steps = step * g + p, steps * g
            assert steps >= 2, "a relayed gather needs a later grid step to pass blocks on"

            @pl.when(step == (3 * steps) // 4)
            def _():
                for cp in from_chips:
                    cp.wait_recv()
                for cp in passed:
                    cp.start()
        else:
            sends, arrivals = _rider_copies(srcs, lands, send_sems, recv_sems, rider)
            passed = []

        @pl.when(first)
        def _():
            for cp in sends:
                cp.start()

        body(*main_in, *main_out, *rest[:n_scratch])

        @pl.when(last)
        def _():
            for cp in arrivals:
                cp.wait_recv()
            for cp in sends + passed:
                cp.wait_send()

    call["in_specs"] = list(call["in_specs"]) + [_ANY] * (2 * m)
    call["out_specs"] = list(call["out_specs"]) + [_ANY] * m
    call["out_shape"] = list(call["out_shape"]) + [
        jax.ShapeDtypeStruct(v.shape, v.dtype) for v in rider.lands]
    call["scratch_shapes"] = list(call["scratch_shapes"]) + [
        pltpu.SemaphoreType.DMA((7 * m,)), pltpu.SemaphoreType.DMA((7 * m,))]
    call["input_output_aliases"] = {n_in + m + a: n_out + a for a in range(m)}
    outs = pl.pallas_call(riding, **call)(*operands, *rider.srcs, *rider.lands)
    return list(outs[:n_out]), list(outs[n_out:])


def _sum_parts(p_ref):
    g = p_ref[0].astype(F32)
    for k in range(1, p_ref.shape[0]):
        g = g + p_ref[k].astype(F32)
    return g


def _adamw(g, w, m, v):
    c1 = 1.0 / (1.0 - ADAM_B1 ** ADAM_STEP)
    c2 = 1.0 / (1.0 - ADAM_B2 ** ADAM_STEP)
    nm = ADAM_B1 * m + (1.0 - ADAM_B1) * g
    nv = ADAM_B2 * v + (1.0 - ADAM_B2) * (g * g)
    return nm, nv, -ADAM_LR * ((nm * c1) / (jnp.sqrt(nv * c2) + ADAM_EPS) + ADAM_WD * w)


def _adamw_w_in(parts, w, m, v, name):
    ncol_blk, depth, nfeat = w.shape
    cols = 256

    def body(*refs):
        p_refs = refs[:depth]
        w_ref, m_ref, v_ref, g_ref, d_ref, nm_ref, nv_ref = refs[depth:]
        for l in range(depth):
            g = _sum_parts(p_refs[l])
            nm, nv, d = _adamw(g, w_ref[:, l, :], m_ref[:, l, :], v_ref[:, l, :])
            g_ref[:, l, :] = g
            nm_ref[:, l, :] = nm
            nv_ref[:, l, :] = nv
            d_ref[:, l, :] = d

    blk = pl.BlockSpec((ncol_blk, depth, cols), lambda j: (0, 0, j))
    return pl.pallas_call(
        body, name=name, grid=(nfeat // cols,),
        in_specs=[pl.BlockSpec((p.shape[0], ncol_blk, cols), lambda j: (0, 0, j)) for p in parts]
        + [blk] * 3,
        out_specs=[blk] * 4,
        out_shape=[jax.ShapeDtypeStruct(w.shape, F32)] * 4,
        compiler_params=_ARB1,
    )(*parts, w, m, v)


def _adamw_sum(parts, w, m, v, tile, name):
    depth, nrow, ncol = w.shape
    rows, cols = tile

    def body(*refs):
        p_refs = refs[:depth]
        w_ref, m_ref, v_ref, g_ref, d_ref, nm_ref, nv_ref = refs[depth:]
        layer = pl.program_id(0)
        for l in range(depth):
            @pl.when(layer == l)
            def _(p_ref=p_refs[l]):
                g = _sum_parts(p_ref)
                nm, nv, d = _adamw(g, w_ref[...], m_ref[...], v_ref[...])
                g_ref[...] = g
                nm_ref[...] = nm
                nv_ref[...] = nv
                d_ref[...] = d

    def part_spec(l):
        return pl.BlockSpec((parts[l].shape[0], rows, cols), lambda q, i, j: (
            0, jnp.where(q == l, i, 0), jnp.where(q == l, j, 0)))

    blk = pl.BlockSpec((None, rows, cols), lambda q, i, j: (q, i, j))
    return pl.pallas_call(
        body, name=name, grid=(depth, nrow // rows, ncol // cols),
        in_specs=[part_spec(l) for l in range(depth)] + [blk, blk, blk],
        out_specs=[blk] * 4,
        out_shape=[jax.ShapeDtypeStruct(w.shape, F32)] * 4,
        compiler_params=pltpu.CompilerParams(
            dimension_semantics=("arbitrary", "arbitrary", "arbitrary")),
    )(*parts, w, m, v)


def _adamw_small(parts, ws, ms, vs, name):
    n = len(parts)

    def body(*refs):
        p, w, m, v = (refs[a * n:(a + 1) * n] for a in range(4))
        out = refs[4 * n:]
        for a in range(n):
            g = _sum_parts(p[a])
            nm, nv, d = _adamw(g, w[a][...], m[a][...], v[a][...])
            out[a][...] = g
            out[n + a][...] = d
            out[2 * n + a][...] = nm
            out[3 * n + a][...] = nv

    vmem = pl.BlockSpec(memory_space=pltpu.VMEM)
    outs = pl.pallas_call(
        body, name=name, in_specs=[vmem] * (4 * n), out_specs=[vmem] * (4 * n),
        out_shape=[jax.ShapeDtypeStruct(w.shape, F32) for w in ws] * 4,
    )(*parts, *ws, *ms, *vs)
    return [list(outs[a * n:(a + 1) * n]) for a in range(4)]


def _misc_rows(b_forget, extra=None):
    tile = jnp.pad(b_forget, ((0, 6), (0, LANES - FOX_HEADS)))
    return tile if extra is None else tile.at[2].set(extra)


def kernel(x, mem, norm_w, w_in, b_forget, mem_norm_w, w_mem_kv, out_norm_w, w_out, final_norm_w, loss_target, m_norm_w, m_w_in, m_b_forget, m_mem_norm_w, m_w_mem_kv, m_out_norm_w, m_w_out, m_final_norm_w, v_norm_w, v_w_in, v_b_forget, v_mem_norm_w, v_w_mem_kv, v_out_norm_w, v_w_out, v_final_norm_w):
    kv_rows = w_mem_kv.shape[1]
    out_rows = w_out.shape[1]
    me = _block_index(*_my_place())

    def shards(l):
        return [w_in[l].T.astype(BF16), w_mem_kv[l].astype(BF16), w_out[l].astype(BF16)]

    def full_in(g_in):
        return _rearrange_w_in(g_in.reshape(IN_W, D_MODEL))

    def full_kv_out(g_kv, g_out):
        return g_kv.reshape(D_MODEL, 2 * MEM_W), g_out.reshape(MIX_W, D_MODEL)

    def in_blocks(g):
        segments = [(0, 3072, 0), (3072, 3080, QKV_W + MIX_W), (3080, 3336, 3072),
                    (3336, IN_W, QKV_W)]

        def block(k):
            lo, hi = k * SHARD_W, (k + 1) * SHARD_W
            pieces = [g[at + max(lo, a) - a:at + min(hi, b) - a]
                      for a, b, at in segments if max(lo, a) < min(hi, b)]
            return jnp.concatenate(pieces, axis=0).astype(BF16)

        return jnp.stack([block(k) for k in range(N_DEV)])

    def kv_blocks(g):
        return g.reshape(N_DEV, kv_rows, 2 * MEM_W).astype(BF16)

    def out_blocks(g):
        return g.reshape(N_DEV, out_rows, D_MODEL).astype(BF16)

    def with_own(land, own):
        return lax.dynamic_update_slice(land, own[None], (me,) + (0,) * own.ndim)

    def with_own_of(land, blocked):
        return lax.dynamic_update_slice(land, lax.dynamic_slice_in_dim(blocked, me, 1, axis=0),
                                        (me,) + (0,) * (land.ndim - 1))

    def row(v):
        return v.reshape(1, -1)

    def cols(first, size):
        return (1, first, size)

    fwd_split, bwd_split = 4 * LANES, (5 * LANES, 6 * LANES)

    s_in0, s_kv0, s_out0 = shards(0)
    (g_in0,) = _all_gather_weights([s_in0], "all_gather_l0")
    w_r0 = full_in(g_in0)
    s_in1, s_kv1, s_out1 = shards(1)
    x1, saved0, (l_in1, _, _), (wkv0, wout0) = _layer_fwd(
        x[0], mem[0], row(norm_w[0]), w_r0, b_forget[0], row(mem_norm_w[0]),
        lambda lands: full_kv_out(with_own(lands[1], s_kv0), with_own(lands[2], s_out0)),
        row(out_norm_w[0]), 0,
        travel=([s_in1, s_kv0, s_out0], False,
                [([0], [cols(0, fwd_split)]), ([0], [cols(fwd_split, D_MODEL - fwd_split)]),
                 ([1, 2], [None, None])]))
    w_r1 = full_in(with_own(l_in1, s_in1))
    x2, saved1, _, (wkv1, wout1) = _layer_fwd(
        x1, mem[0], row(norm_w[1]), w_r1, b_forget[1], row(mem_norm_w[1]),
        lambda lands: full_kv_out(with_own(lands[0], s_kv1), with_own(lands[1], s_out1)),
        row(out_norm_w[1]), 1, travel=([s_kv1, s_out1], False, [([0, 1], [None, None]), None]))

    dx2, loss_part, dfnw = _final_fwd_bwd(x2, row(final_norm_w), loss_target[0], "final_fwd_bwd")

    dx1, gr1, (l_out1,) = _layer_bwd(
        dx2, saved1, mem[0], w_r1, wkv1, wout1, 1,
        travel=lambda dwout: ([out_blocks(dwout)], True, [([0], [None]), None]))
    p_in1, p_kv1 = in_blocks(gr1["w_r"]), kv_blocks(gr1["w_mem_kv"])
    grad_x, gr0, (l_in1, l_kv1, l_out0) = _layer_bwd(
        dx1, saved0, mem[0], w_r0, wkv0, wout0, 0,
        travel=lambda dwout: ([p_in1, p_kv1, out_blocks(dwout)], True,
                              [([0, 1], [cols(0, bwd_split[0]), None]),
                               ([0, 2], [cols(bwd_split[0], bwd_split[1] - bwd_split[0]), None]),
                               ([0], [cols(bwd_split[1], D_MODEL - bwd_split[1])])]))
    r_out1 = with_own_of(l_out1, out_blocks(gr1["w_out"]))
    r_in1, r_kv1 = with_own_of(l_in1, p_in1), with_own_of(l_kv1, p_kv1)
    r_out0 = with_own_of(l_out0, out_blocks(gr0["w_out"]))

    def both(name):
        return jnp.stack([gr0[name], gr1[name]])

    small = [both("norm_w"), both("mem_norm_w"), both("out_norm_w"), dfnw,
             _misc_rows(both("b_forget"), loss_part[0])]
    p_small = [jnp.broadcast_to(v[None], (N_DEV,) + v.shape) for v in small]
    by_core = [v.reshape((N_CHIP, 2) + v.shape[1:])
               for v in (in_blocks(gr0["w_r"]), kv_blocks(gr0["w_mem_kv"]))]
    from_sibling = _pair_swap(by_core, "grads_l0_pair_swap")
    core = lax.axis_index("c")
    chip_sums = [_pair_add(lax.dynamic_index_in_dim(v, core, axis=1, keepdims=False), got,
                           f"grads_l0_pair_add_{a}")
                 for a, (v, got) in enumerate(zip(by_core, from_sibling))]
    r_in0, r_kv0, *r_small = _chip_exchange(chip_sums, p_small, "exchange_grads_l0")

    def view(v):
        return jnp.transpose(v, (2, 0, 1))

    g_w_in, d_w_in, nm_w_in, nv_w_in = [jnp.transpose(v, (1, 2, 0)) for v in _adamw_w_in(
        [r_in0, r_in1], view(w_in), view(m_w_in), view(v_w_in), "adamw_w_in")]
    g_w_kv, d_w_kv, nm_w_kv, nv_w_kv = _adamw_sum(
        [r_kv0, r_kv1], w_mem_kv, m_w_mem_kv, v_w_mem_kv, (kv_rows, 2 * MEM_W), "adamw_w_mem_kv")
    g_w_out, d_w_out, nm_w_out, nv_w_out = _adamw_sum(
        [r_out0, r_out1], w_out, m_w_out, v_w_out, (out_rows, D_MODEL), "adamw_w_out")
    def smalls(nw, mnw, onw, fnw, b):
        return [nw, mnw, onw, row(fnw), _misc_rows(b)]

    small_out = _adamw_small(
        r_small, smalls(norm_w, mem_norm_w, out_norm_w, final_norm_w, b_forget),
        smalls(m_norm_w, m_mem_norm_w, m_out_norm_w, m_final_norm_w, m_b_forget),
        smalls(v_norm_w, v_mem_norm_w, v_out_norm_w, v_final_norm_w, v_b_forget), "adamw_small")
    (g_nw, g_mnw, g_onw, g_fnw, g_b), (d_nw, d_mnw, d_onw, d_fnw, d_b), \
        (nm_nw, nm_mnw, nm_onw, nm_fnw, nm_b), (nv_nw, nv_mnw, nv_onw, nv_fnw, nv_b) = [
            (nw, mnw, onw, fnw[0], misc[:2, :FOX_HEADS]) for nw, mnw, onw, fnw, misc in small_out]
    loss = small_out[0][4][2, 0]

    return (loss, grad_x[None],
            g_nw, g_w_in, g_b, g_mnw, g_w_kv, g_onw, g_w_out, g_fnw,
            d_nw, d_w_in, d_b, d_mnw, d_w_kv, d_onw, d_w_out, d_fnw,
            nm_nw, nm_w_in, nm_b, nm_mnw, nm_w_kv, nm_onw, nm_w_out, nm_fnw,
            nv_nw, nv_w_in, nv_b, nv_mnw, nv_w_kv, nv_onw, nv_w_out, nv_fnw)
```

```python
import functools
from typing import NamedTuple

import jax
import jax.numpy as jnp
from jax import lax
from jax.experimental import pallas as pl
from jax.experimental.pallas import tpu as pltpu

F32 = jnp.float32
BF16 = jnp.bfloat16

N_DEV = 8
D_MODEL = 1024
HEAD_DIM = 64
LANES = 128
SB_W = 512
FX_W = 512
MEM_W = 256
MIX_W = 1280
FOX_HEADS = 8
IN_W = 4616
SHARD_W = IN_W // N_DEV
QKV_W = 3 * SB_W + 3 * FX_W + MEM_W
FL_PAD = 256
GF_W = MIX_W + FL_PAD
WR_W = QKV_W + GF_W
EPS = 1e-6
T = 256
QPS = 4
TM = 256
TQM = 512
Q_SCALE = 0.125
NEG = -1e30
UNDERFLOW = -110.0
NORM_SLACK = 1.01

ADAM_LR = 0.001
ADAM_B1 = 0.9
ADAM_B2 = 0.999
ADAM_EPS = 1e-08
ADAM_WD = 0.01
ADAM_STEP = 10


_NT = (((1,), (1,)), ((), ()))
_TN = (((0,), (0,)), ((), ()))

_ARB1 = pltpu.CompilerParams(dimension_semantics=("arbitrary",))
_ARB2 = pltpu.CompilerParams(dimension_semantics=("arbitrary", "arbitrary"))


def _dot(a, b):
    return jnp.dot(a, b, preferred_element_type=F32)


def _dot_nt(a, b):
    return lax.dot_general(a, b, _NT, preferred_element_type=F32)


def _dot_tn(a, b):
    return lax.dot_general(a, b, _TN, preferred_element_type=F32)


def _split2(x):
    hi = x.astype(BF16)
    lo = (x - hi.astype(F32)).astype(BF16)
    return hi, lo


def _stack2(u):
    return jnp.concatenate([u, u], axis=0)


def _cum2(x, u2):
    hi, lo = _split2(x)
    return _dot(jnp.concatenate([hi, lo], axis=1), u2)


def _tri3(tri, x, dot=None):
    dot = dot or _dot
    hi = x.astype(BF16)
    r1 = x - hi.astype(F32)
    mid = r1.astype(BF16)
    lo = (r1 - mid.astype(F32)).astype(BF16)
    return dot(tri, hi) + dot(tri, mid) + dot(tri, lo)


def _iota2(shape, dim):
    return lax.broadcasted_iota(jnp.int32, shape, dim)


def _head_block_diag():
    r = _iota2((LANES, LANES), 0) // HEAD_DIM
    c = _iota2((LANES, LANES), 1) // HEAD_DIM
    return _stack2(jnp.where(r == c, 1.0, 0.0).astype(BF16))


def _head_mean(x, bd):
    return _cum2(x, bd) * (1.0 / HEAD_DIM)


def _sigmoid(x):
    return 1.0 / (1.0 + jnp.exp(-x))


def _log_sigmoid(x):
    return jnp.minimum(x, 0.0) - jnp.log(1.0 + jnp.exp(-jnp.abs(x)))


def _running_top(r_ref):
    return jnp.max(jnp.maximum(r_ref[0], r_ref[1]))


def _fox_tiles_left(i, pair, nq, fb_ref, tile):
    def bound(j):
        b = []
        for h in range(2):
            head = 2 * pair + h
            b.append(2.0 * NORM_SLACK * fb_ref[2 * nq + i, head] * fb_ref[3 * nq, head]
                     + fb_ref[2 * i, head] - fb_ref[2 * j + 1, head])
        return jnp.maximum(b[0], b[1])

    def more(j):
        return jnp.logical_and(j >= 0, bound(jnp.maximum(j, 0)) > UNDERFLOW)

    def step(j):
        tile(j, False)
        return j - 1

    return lax.while_loop(more, step, i - 1)


def _pair_masks():
    lane = _iota2((1, LANES), 1)
    return lane < HEAD_DIM


def _split_pair(x, m0):
    zero = jnp.zeros_like(x)
    return jnp.where(m0, x, zero), jnp.where(m0, zero, x)


def _inproj_fwd(x, nw, w_r, name, rider=None):
    s = x.shape[0]

    def body(x_ref, nw_ref, w_ref, qkv_ref, gf_ref):
        xv = x_ref[...]
        r = lax.rsqrt(jnp.mean(xv * xv, axis=-1, keepdims=True) + EPS)
        h = (xv * r * nw_ref[...]).astype(BF16)
        for c in range(0, QKV_W, 256):
            qkv_ref[:, c:c + 256] = _dot_nt(h, w_ref[c:c + 256, :]).astype(BF16)
        for c in range(0, GF_W, 256):
            gf_ref[:, c:c + 256] = _dot_nt(h, w_ref[QKV_W + c:QKV_W + c + 256, :])

    (qkv, gf), lands = _ride(dict(
        body=body, name=name, grid=(s // TM,),
        in_specs=[pl.BlockSpec((TM, D_MODEL), lambda i: (i, 0)),
                  pl.BlockSpec((1, D_MODEL), lambda i: (0, 0)),
                  pl.BlockSpec((WR_W, D_MODEL), lambda i: (0, 0))],
        out_specs=[pl.BlockSpec((TM, QKV_W), lambda i: (i, 0)),
                   pl.BlockSpec((TM, GF_W), lambda i: (i, 0))],
        out_shape=[jax.ShapeDtypeStruct((s, QKV_W), BF16), jax.ShapeDtypeStruct((s, GF_W), F32)],
        scratch_shapes=[], compiler_params=_ARB1, operands=[x, nw, w_r]), rider)
    return qkv, gf, lands


def _fox_prep_fwd(gf, qkv, bpad, name):
    s = gf.shape[0]
    nq = s // T
    nrow = -(-(3 * nq + 1) // 8) * 8

    def body(fl_ref, q_ref, k_ref, b_ref, fq_ref, fr_ref, fb_ref):
        tri = jnp.where(_iota2((T, T), 0) >= _iota2((T, T), 1), 1.0, 0.0).astype(BF16)
        m0 = _pair_masks()
        lane = _iota2((1, LANES), 1)
        norms = [jnp.zeros((1, LANES), F32) for _ in range(nq + 1)]
        same_head = (_iota2((LANES, LANES), 0) // HEAD_DIM) == (_iota2((LANES, LANES), 1) // HEAD_DIM)
        bd = jnp.where(same_head, 1.0, 0.0).astype(BF16)
        for p in range(FOX_HEADS // 2):
            cols = slice(p * LANES, (p + 1) * LANES)
            q = (q_ref[:, cols] * jnp.asarray(Q_SCALE, BF16)).astype(F32)
            k = k_ref[:, cols].astype(F32)
            qn = _dot((q * q).astype(BF16), bd)
            kn = _dot((k * k).astype(BF16), bd)
            tops = [jnp.max(qn[j * T:(j + 1) * T], axis=0, keepdims=True) for j in range(nq)]
            tops.append(jnp.max(kn, axis=0, keepdims=True))
            tops = [jnp.sqrt(top) for top in tops]
            for h in range(2):
                at = h * HEAD_DIM
                norms = [jnp.where(lane == 2 * p + h, top[:, at:at + 1], row)
                         for top, row in zip(tops, norms)]
        for j in range(nq + 1):
            fb_ref[2 * nq + j:2 * nq + j + 1, :] = norms[j]
        fb_ref[3 * nq + 1:, :] = jnp.zeros((nrow - 3 * nq - 1, LANES), F32)
        carry = jnp.zeros((1, LANES), F32)
        for blk in range(s // T):
            rows = slice(blk * T, (blk + 1) * T)
            lf = _log_sigmoid(fl_ref[rows, :] + b_ref[...])
            c = _tri3(tri, lf) + carry
            carry = c[T - 1:T, :]
            for p in range(FOX_HEADS // 2):
                fq_ref[rows, p * LANES:(p + 1) * LANES] = jnp.where(
                    m0, c[:, 2 * p:2 * p + 1], c[:, 2 * p + 1:2 * p + 2])
            fr_ref[:, rows] = c.T[0:FOX_HEADS, :]
            fb_ref[2 * blk:2 * blk + 1, :] = c[0:1, :]
            fb_ref[2 * blk + 1:2 * blk + 2, :] = carry

    base = 3 * SB_W // FX_W
    return pl.pallas_call(
        body, name=name, grid=(1,),
        in_specs=[pl.BlockSpec((s, LANES), lambda i: (0, MIX_W // LANES)),
                  pl.BlockSpec((s, FX_W), lambda i: (0, base)),
                  pl.BlockSpec((s, FX_W), lambda i: (0, base + 1)),
                  pl.BlockSpec((1, LANES), lambda i: (0, 0))],
        out_specs=[pl.BlockSpec((s, FX_W), lambda i: (0, 0)),
                   pl.BlockSpec((FOX_HEADS, s), lambda i: (0, 0)),
                   pl.BlockSpec((nrow, LANES), lambda i: (0, 0))],
        out_shape=[jax.ShapeDtypeStruct((s, FX_W), F32), jax.ShapeDtypeStruct((FOX_HEADS, s), F32),
                   jax.ShapeDtypeStruct((nrow, LANES), F32)],
        compiler_params=_ARB1,
    )(gf, qkv, qkv, bpad)


def _sb_fwd(qkv, name, rider=None):
    s = qkv.shape[0]

    def body(q_ref, k_ref, v_ref, o_ref, acc_ref, r_ref, as_ref):
        m0 = _pair_masks()
        strict = _iota2((T, T), 0) > _iota2((T, T), 1)
        u2 = _stack2(jnp.where(strict, 1.0, 0.0).astype(BF16))
        hs = range(2)

        def query_tile(i, rows):
            qh = _split_pair(q_ref[rows, :] * jnp.asarray(Q_SCALE, BF16), m0)
            acc_ref[...] = jnp.zeros_like(acc_ref)
            r_ref[...] = jnp.zeros_like(r_ref)

            def flush(j):
                v = v_ref[pl.ds(pl.multiple_of(j * T, T), T), :]
                for h in hs:
                    acc_ref[h] += _dot(as_ref[h], v)

            def tile(j, diag):
                k = k_ref[pl.ds(pl.multiple_of(j * T, T), T), :]
                z = [_dot_nt(qh[h], k) for h in hs]
                if not diag:
                    flush(j + 1)
                la = [jnp.minimum(z[h], 0.0) - jnp.log(1.0 + jnp.exp(-jnp.abs(z[h]))) for h in hs]
                lf = [la[h] - z[h] for h in hs]
                if diag:
                    lf = [jnp.where(strict, lf[h], 0.0) for h in hs]
                cin = [_cum2(lf[h], u2) for h in hs]
                a = [jnp.exp(la[h] + cin[h] + r_ref[h]) for h in hs]
                if diag:
                    a = [jnp.where(strict, a[h], 0.0) for h in hs]
                for h in hs:
                    r_ref[h] += cin[h][:, 0:1] + lf[h][:, 0:1]
                    as_ref[h] = a[h].astype(BF16)

            tile(i, True)

            def more(state):
                j, top = state
                return jnp.logical_and(j >= 0, top > UNDERFLOW)

            def step(state):
                j, _ = state
                tile(j, False)
                return j - 1, _running_top(r_ref)

            j_left, _ = lax.while_loop(more, step, (i - 1, _running_top(r_ref)))
            flush(j_left + 1)
            o_ref[rows, :] = jnp.where(m0, acc_ref[0], acc_ref[1])

        for n in range(QPS):
            query_tile(QPS * pl.program_id(1) + n, slice(n * T, (n + 1) * T))

    nb = SB_W // LANES
    (ysb,), lands = _ride(dict(
        body=body, name=name, grid=(nb, s // (QPS * T)),
        in_specs=[pl.BlockSpec((QPS * T, LANES), lambda p, i: (i, p)),
                  pl.BlockSpec((s, LANES), lambda p, i: (0, nb + p)),
                  pl.BlockSpec((s, LANES), lambda p, i: (0, 2 * nb + p))],
        out_specs=[pl.BlockSpec((QPS * T, LANES), lambda p, i: (i, p))],
        out_shape=[jax.ShapeDtypeStruct((s, SB_W), F32)],
        scratch_shapes=[pltpu.VMEM((2, T, LANES), F32), pltpu.VMEM((2, T, 1), F32),
                        pltpu.VMEM((2, T, T), BF16)],
        compiler_params=_ARB2, operands=[qkv, qkv, qkv]), rider)
    return ysb, lands


def _fox_fwd(qkv, fqb, frow, fbounds, name, rider=None):
    s = qkv.shape[0]

    def body(q_ref, k_ref, v_ref, fq_ref, fr_ref, fb_ref, o_ref, lse_ref, acc_ref, m_ref, ps_ref):
        pair = pl.program_id(0)
        m0 = _pair_masks()
        causal = _iota2((T, T), 0) >= _iota2((T, T), 1)
        ones = jnp.ones((T, LANES), BF16)
        hs = range(2)

        def query_tile(i, rows):
            qh = _split_pair(q_ref[rows, :] * jnp.asarray(Q_SCALE, BF16), m0)
            fq = fq_ref[rows, :]
            fqh = (fq[:, 0:1], fq[:, HEAD_DIM:HEAD_DIM + 1])
            acc_ref[...] = jnp.zeros_like(acc_ref)
            m_ref[...] = jnp.full_like(m_ref, NEG)

            def flush(j):
                v = v_ref[pl.ds(pl.multiple_of(j * T, T), T), :]
                va2 = _stack2(jnp.concatenate([v, ones], axis=1))
                for h in hs:
                    acc_ref[h] += _dot(ps_ref[h], va2)

            def tile(j, diag):
                off = pl.multiple_of(j * T, T)
                k = k_ref[pl.ds(off, T), :]
                sc = [_dot_nt(qh[h], k) + fqh[h] - fr_ref[h:h + 1, pl.ds(off, T)] for h in hs]
                if not diag:
                    flush(j + 1)
                if diag:
                    sc = [jnp.where(causal, sc[h], NEG) for h in hs]
                m_new = [jnp.maximum(m_ref[h], jnp.max(sc[h], axis=1, keepdims=True)) for h in hs]
                p = [jnp.exp(sc[h] - m_new[h]) for h in hs]
                for h in hs:
                    acc_ref[h] = acc_ref[h] * jnp.exp(m_ref[h] - m_new[h])
                    m_ref[h] = m_new[h]
                    p_hi, p_lo = _split2(p[h])
                    ps_ref[h] = jnp.concatenate([p_hi, p_lo], axis=1)

            tile(i, True)
            j_left = _fox_tiles_left(i, pair, s // T, fb_ref, tile)
            flush(j_left + 1)
            acc = (acc_ref[0], acc_ref[1])
            o_ref[rows, :] = jnp.where(m0, acc[0][:, :LANES] / acc[0][:, LANES:],
                                       acc[1][:, :LANES] / acc[1][:, LANES:])
            lse_ref[rows, :] = jnp.where(m0, m_ref[0] + jnp.log(acc[0][:, LANES:]),
                                         m_ref[1] + jnp.log(acc[1][:, LANES:]))

        for n in range(QPS):
            query_tile(QPS * pl.program_id(1) + n, slice(n * T, (n + 1) * T))

    nb = FX_W // LANES
    base = 3 * SB_W // LANES
    (yfx, lse), lands = _ride(dict(
        body=body, name=name, grid=(nb, s // (QPS * T)),
        in_specs=[pl.BlockSpec((QPS * T, LANES), lambda p, i: (i, base + p)),
                  pl.BlockSpec((s, LANES), lambda p, i: (0, base + nb + p)),
                  pl.BlockSpec((s, LANES), lambda p, i: (0, base + 2 * nb + p)),
                  pl.BlockSpec((QPS * T, LANES), lambda p, i: (i, p)),
                  pl.BlockSpec((None, 2, s), lambda p, i: (p, 0, 0)),
                  pl.BlockSpec(memory_space=pltpu.SMEM)],
        out_specs=[pl.BlockSpec((QPS * T, LANES), lambda p, i: (i, p)),
                   pl.BlockSpec((QPS * T, LANES), lambda p, i: (i, p))],
        out_shape=[jax.ShapeDtypeStruct((s, FX_W), F32), jax.ShapeDtypeStruct((s, FX_W), F32)],
        scratch_shapes=[pltpu.VMEM((2, T, 2 * LANES), F32), pltpu.VMEM((2, T, 1), F32),
                        pltpu.VMEM((2, T, 2 * T), BF16)],
        compiler_params=_ARB2, operands=[qkv, qkv, qkv, fqb, frow, fbounds]), rider)
    return yfx, lse, lands


def _memkv_fwd(mem, mnw, wkv, name):
    n = mem.shape[0]

    def body(mem_ref, mnw_ref, w_ref, kv_ref):
        mv = mem_ref[...]
        r = lax.rsqrt(jnp.mean(mv * mv, axis=-1, keepdims=True) + EPS)
        hm = (mv * r * mnw_ref[...]).astype(BF16)
        kv_ref[...] = _dot(hm, w_ref[...]).astype(BF16)

    return pl.pallas_call(
        body, name=name, grid=(1,),
        in_specs=[pl.BlockSpec((n, D_MODEL), lambda i: (0, 0)),
                  pl.BlockSpec((1, D_MODEL), lambda i: (0, 0)),
                  pl.BlockSpec((D_MODEL, 2 * MEM_W), lambda i: (0, 0))],
        out_specs=pl.BlockSpec((n, 2 * MEM_W), lambda i: (0, 0)),
        out_shape=jax.ShapeDtypeStruct((n, 2 * MEM_W), BF16),
        compiler_params=_ARB1,
    )(mem, mnw, wkv)


def _mem_fwd(qkv, kv, name):
    s = qkv.shape[0]
    n = kv.shape[0]

    def body(q_ref, k_ref, v_ref, o_ref, lse_ref):
        m0 = _pair_masks()
        qh = _split_pair(q_ref[...] * jnp.asarray(Q_SCALE, BF16), m0)
        k = k_ref[...]
        v = v_ref[...]
        outs, lses = [], []
        for h in range(2):
            sc = _dot_nt(qh[h], k)
            mx = jnp.max(sc, axis=1, keepdims=True)
            p = jnp.exp(sc - mx)
            l = jnp.sum(p, axis=1, keepdims=True)
            outs.append(_dot(p.astype(BF16), v) / l)
            lses.append(mx + jnp.log(l))
        o_ref[...] = jnp.where(m0, outs[0], outs[1])
        lse_ref[...] = jnp.where(m0, lses[0], lses[1])

    nb = MEM_W // LANES
    base = (3 * SB_W + 3 * FX_W) // LANES
    return pl.pallas_call(
        body, name=name, grid=(nb, s // TQM),
        in_specs=[pl.BlockSpec((TQM, LANES), lambda p, i: (i, base + p)),
                  pl.BlockSpec((n, LANES), lambda p, i: (0, p)),
                  pl.BlockSpec((n, LANES), lambda p, i: (0, nb + p))],
        out_specs=[pl.BlockSpec((TQM, LANES), lambda p, i: (i, p)),
                   pl.BlockSpec((TQM, LANES), lambda p, i: (i, p))],
        out_shape=[jax.ShapeDtypeStruct((s, MEM_W), F32), jax.ShapeDtypeStruct((s, MEM_W), F32)],
        compiler_params=_ARB2,
    )(qkv, kv, kv)


def _mix_chunk(c, ysb_ref, yfx_ref, ym_ref):
    if c < SB_W // LANES:
        return ysb_ref[:, c * LANES:(c + 1) * LANES]
    c -= SB_W // LANES
    if c < FX_W // LANES:
        return yfx_ref[:, c * LANES:(c + 1) * LANES]
    c -= FX_W // LANES
    return ym_ref[:, c * LANES:(c + 1) * LANES]


def _outproj_fwd(ysb, yfx, ym, gf, onw, wout, x, name):
    s = x.shape[0]

    def body(ysb_ref, yfx_ref, ym_ref, g_ref, onw_ref, w_ref, x_ref, o_ref, yg_ref):
        bd = _head_block_diag()
        for c in range(MIX_W // LANES):
            sl = slice(c * LANES, (c + 1) * LANES)
            u = _mix_chunk(c, ysb_ref, yfx_ref, ym_ref)
            r = lax.rsqrt(_head_mean(u * u, bd) + EPS)
            g = g_ref[:, sl]
            yg_ref[:, sl] = (u * r * onw_ref[:, sl] * (g * _sigmoid(g))).astype(BF16)
        o_ref[...] = x_ref[...] + _dot(yg_ref[...], w_ref[...])

    return pl.pallas_call(
        body, name=name, grid=(s // TM,),
        in_specs=[pl.BlockSpec((TM, SB_W), lambda i: (i, 0)),
                  pl.BlockSpec((TM, FX_W), lambda i: (i, 0)),
                  pl.BlockSpec((TM, MEM_W), lambda i: (i, 0)),
                  pl.BlockSpec((TM, MIX_W), lambda i: (i, 0)),
                  pl.BlockSpec((1, MIX_W), lambda i: (0, 0)),
                  pl.BlockSpec((MIX_W, D_MODEL), lambda i: (0, 0)),
                  pl.BlockSpec((TM, D_MODEL), lambda i: (i, 0))],
        out_specs=pl.BlockSpec((TM, D_MODEL), lambda i: (i, 0)),
        out_shape=jax.ShapeDtypeStruct((s, D_MODEL), F32),
        scratch_shapes=[pltpu.VMEM((TM, MIX_W), BF16)],
        compiler_params=_ARB1,
    )(ysb, yfx, ym, gf, onw, wout, x)


def _final_fwd_bwd(x, fnw, target, name):
    s = x.shape[0]

    def body(x_ref, w_ref, t_ref, dx_ref, loss_ref, dw_ref):
        @pl.when(pl.program_id(0) == 0)
        def _():
            loss_ref[...] = jnp.zeros_like(loss_ref)
            dw_ref[...] = jnp.zeros_like(dw_ref)

        xv = x_ref[...]
        w = w_ref[...]
        r = lax.rsqrt(jnp.mean(xv * xv, axis=-1, keepdims=True) + EPS)
        xh = xv * r
        err = xh * w - t_ref[...]
        part = jnp.sum(jnp.sum(err * err, axis=1, keepdims=True), axis=0, keepdims=True)
        loss_ref[...] += part * (0.5 / D_MODEL)
        dy = err * (1.0 / D_MODEL)
        dw_ref[...] += jnp.sum(dy * xh, axis=0, keepdims=True)
        dxh = dy * w
        dx_ref[...] = r * (dxh - xh * jnp.mean(dxh * xh, axis=-1, keepdims=True))

    return pl.pallas_call(
        body, name=name, grid=(s // TM,),
        in_specs=[pl.BlockSpec((TM, D_MODEL), lambda i: (i, 0)),
                  pl.BlockSpec((1, D_MODEL), lambda i: (0, 0)),
                  pl.BlockSpec((TM, D_MODEL), lambda i: (i, 0))],
        out_specs=[pl.BlockSpec((TM, D_MODEL), lambda i: (i, 0)),
                   pl.BlockSpec((1, LANES), lambda i: (0, 0)),
                   pl.BlockSpec((1, D_MODEL), lambda i: (0, 0))],
        out_shape=[jax.ShapeDtypeStruct((s, D_MODEL), F32), jax.ShapeDtypeStruct((1, LANES), F32),
                   jax.ShapeDtypeStruct((1, D_MODEL), F32)],
        compiler_params=_ARB1,
    )(x, fnw, target)


def _outproj_bwd(dxo, wout, ysb, yfx, ym, gf, onw, name):
    s = dxo.shape[0]

    def body(dx_ref, w_ref, ysb_ref, yfx_ref, ym_ref, g_ref, onw_ref,
             dysb_ref, dyfx_ref, dym_ref, dg_ref, dw_ref, donw_ref, yg_ref):
        @pl.when(pl.program_id(0) == 0)
        def _():
            dw_ref[...] = jnp.zeros_like(dw_ref)
            donw_ref[...] = jnp.zeros_like(donw_ref)

        dxb = dx_ref[...].astype(BF16)
        dyg = _dot_nt(dxb, w_ref[...])
        bd = _head_block_diag()
        for c in range(MIX_W // LANES):
            sl = slice(c * LANES, (c + 1) * LANES)
            u = _mix_chunk(c, ysb_ref, yfx_ref, ym_ref)
            r = lax.rsqrt(_head_mean(u * u, bd) + EPS)
            yn = u * r
            g = g_ref[:, sl]
            sg = _sigmoid(g)
            sil = g * sg
            onw = onw_ref[:, sl]
            e = dyg[:, sl]
            yg_ref[:, sl] = (yn * onw * sil).astype(BF16)
            donw_ref[:, sl] += jnp.sum(e * yn * sil, axis=0, keepdims=True)
            dg_ref[:, sl] = (e * yn * onw * (sg * (1.0 + g * (1.0 - sg)))).astype(BF16)
            dyn = e * onw * sil
            du = (r * (dyn - yn * _head_mean(dyn * yn, bd))).astype(BF16)
            if c < 4:
                dysb_ref[:, c * LANES:(c + 1) * LANES] = du
            elif c < 8:
                dyfx_ref[:, (c - 4) * LANES:(c - 3) * LANES] = du
            else:
                dym_ref[:, (c - 8) * LANES:(c - 7) * LANES] = du
        dw_ref[...] += _dot_tn(yg_ref[...], dxb)

    return pl.pallas_call(
        body, name=name, grid=(s // TM,),
        in_specs=[pl.BlockSpec((TM, D_MODEL), lambda i: (i, 0)),
                  pl.BlockSpec((MIX_W, D_MODEL), lambda i: (0, 0)),
                  pl.BlockSpec((TM, SB_W), lambda i: (i, 0)),
                  pl.BlockSpec((TM, FX_W), lambda i: (i, 0)),
                  pl.BlockSpec((TM, MEM_W), lambda i: (i, 0)),
                  pl.BlockSpec((TM, MIX_W), lambda i: (i, 0)),
                  pl.BlockSpec((1, MIX_W), lambda i: (0, 0))],
        out_specs=[pl.BlockSpec((TM, SB_W), lambda i: (i, 0)),
                   pl.BlockSpec((TM, FX_W), lambda i: (i, 0)),
                   pl.BlockSpec((TM, MEM_W), lambda i: (i, 0)),
                   pl.BlockSpec((TM, MIX_W), lambda i: (i, 0)),
                   pl.BlockSpec((MIX_W, D_MODEL), lambda i: (0, 0)),
                   pl.BlockSpec((1, MIX_W), lambda i: (0, 0))],
        out_shape=[jax.ShapeDtypeStruct((s, SB_W), BF16), jax.ShapeDtypeStruct((s, FX_W), BF16),
                   jax.ShapeDtypeStruct((s, MEM_W), BF16), jax.ShapeDtypeStruct((s, MIX_W), BF16),
                   jax.ShapeDtypeStruct((MIX_W, D_MODEL), F32), jax.ShapeDtypeStruct((1, MIX_W), F32)],
        scratch_shapes=[pltpu.VMEM((TM, MIX_W), BF16)],
        compiler_params=_ARB1,
    )(dxo, wout, ysb, yfx, ym, gf, onw)


def _row_dots(do, o, m0):
    prod = do.astype(F32) * o
    zero = jnp.zeros_like(prod)
    return (jnp.sum(jnp.where(m0, prod, zero), axis=1, keepdims=True),
            jnp.sum(jnp.where(m0, zero, prod), axis=1, keepdims=True))


def _sb_bwd(qkv, o, do, name, rider=None):
    s = qkv.shape[0]
    nq = s // T

    def body(q_ref, k_ref, v_ref, o_ref, do_ref, dq_ref, dk_ref, dv_ref,
             dqa_ref, dka_ref, dva_ref, rl_ref, rg_ref, dzs_ref, abs_ref):
        step_id = pl.program_id(1)

        @pl.when(step_id == 0)
        def _():
            dka_ref[...] = jnp.zeros_like(dka_ref)
            dva_ref[...] = jnp.zeros_like(dva_ref)

        m0 = _pair_masks()
        strict = _iota2((T, T), 0) > _iota2((T, T), 1)
        u2 = _stack2(jnp.where(strict, 1.0, 0.0).astype(BF16))
        hs = range(2)

        def query_tile(i, rows):
            qh = _split_pair(q_ref[rows, :] * jnp.asarray(Q_SCALE, BF16), m0)
            doh = _split_pair(do_ref[rows, :], m0)
            dsum = _row_dots(do_ref[rows, :], o_ref[rows, :], m0)
            dqa_ref[...] = jnp.zeros_like(dqa_ref)
            rl_ref[...] = jnp.zeros_like(rl_ref)
            rg_ref[...] = jnp.zeros_like(rg_ref)

            def flush(j):
                off = pl.multiple_of(j * T, T)
                k = k_ref[pl.ds(off, T), :]
                for h in hs:
                    dqa_ref[h] += _dot(dzs_ref[h], k)
                dka_ref[pl.ds(off, T), :] += (_dot_tn(dzs_ref[0], qh[0])
                                              + _dot_tn(dzs_ref[1], qh[1]))
                dva_ref[pl.ds(off, T), :] += (_dot_tn(abs_ref[0], doh[0])
                                              + _dot_tn(abs_ref[1], doh[1]))

            def tile(j, diag):
                off = pl.multiple_of(j * T, T)
                k = k_ref[pl.ds(off, T), :]
                v = v_ref[pl.ds(off, T), :]
                z = [_dot_nt(qh[h], k) for h in hs]
                da = [_dot_nt(doh[h], v) for h in hs]
                if not diag:
                    flush(j + 1)
                la = [jnp.minimum(z[h], 0.0) - jnp.log(1.0 + jnp.exp(-jnp.abs(z[h]))) for h in hs]
                lf = [la[h] - z[h] for h in hs]
                if diag:
                    lf = [jnp.where(strict, lf[h], 0.0) for h in hs]
                cin = [_cum2(lf[h], u2) for h in hs]
                a = [jnp.exp(la[h] + cin[h] + rl_ref[h]) for h in hs]
                if diag:
                    a = [jnp.where(strict, a[h], 0.0) for h in hs]
                ab = [a[h].astype(BF16) for h in hs]
                g = [ab[h].astype(F32) * da[h] for h in hs]
                gin = [_cum2(g[h], u2) for h in hs]
                dz = [g[h] - jnp.exp(la[h]) * ((dsum[h] - rg_ref[h]) - gin[h]) for h in hs]
                if diag:
                    dz = [jnp.where(strict, dz[h], 0.0) for h in hs]
                for h in hs:
                    rl_ref[h] += cin[h][:, 0:1] + lf[h][:, 0:1]
                    rg_ref[h] += gin[h][:, 0:1] + g[h][:, 0:1]
                    dzs_ref[h] = dz[h].astype(BF16)
                    abs_ref[h] = ab[h]

            tile(i, True)

            def more(state):
                j, top = state
                return jnp.logical_and(j >= 0, top > UNDERFLOW)

            def step(state):
                j, _ = state
                tile(j, False)
                return j - 1, _running_top(rl_ref)

            j_left, _ = lax.while_loop(more, step, (i - 1, _running_top(rl_ref)))
            flush(j_left + 1)
            dq_ref[rows, :] = (jnp.where(m0, dqa_ref[0], dqa_ref[1]) * Q_SCALE).astype(BF16)

        for n in range(QPS):
            query_tile(QPS * step_id + n, slice(n * T, (n + 1) * T))

        @pl.when(step_id == nq // QPS - 1)
        def _():
            dk_ref[...] = dka_ref[...].astype(BF16)
            dv_ref[...] = dva_ref[...].astype(BF16)

    nb = SB_W // LANES
    (dq, dk, dv), lands = _ride(dict(
        body=body, name=name, grid=(nb, nq // QPS),
        in_specs=[pl.BlockSpec((QPS * T, LANES), lambda p, i: (i, p)),
                  pl.BlockSpec((s, LANES), lambda p, i: (0, nb + p)),
                  pl.BlockSpec((s, LANES), lambda p, i: (0, 2 * nb + p)),
                  pl.BlockSpec((QPS * T, LANES), lambda p, i: (i, p)),
                  pl.BlockSpec((QPS * T, LANES), lambda p, i: (i, p))],
        out_specs=[pl.BlockSpec((QPS * T, LANES), lambda p, i: (i, p)),
                   pl.BlockSpec((s, LANES), lambda p, i: (0, p)),
                   pl.BlockSpec((s, LANES), lambda p, i: (0, p))],
        out_shape=[jax.ShapeDtypeStruct((s, SB_W), BF16)] * 3,
        scratch_shapes=[pltpu.VMEM((2, T, LANES), F32), pltpu.VMEM((s, LANES), F32),
                        pltpu.VMEM((s, LANES), F32), pltpu.VMEM((2, T, 1), F32),
                        pltpu.VMEM((2, T, 1), F32), pltpu.VMEM((2, T, T), BF16),
                        pltpu.VMEM((2, T, T), BF16)],
        compiler_params=_ARB2, operands=[qkv, qkv, qkv, o, do]), rider)
    return dq, dk, dv, lands


def _fox_bwd(qkv, fqb, frow, fbounds, o, lse, do, name, rider=None):
    s = qkv.shape[0]
    nq = s // T

    def body(q_ref, k_ref, v_ref, fq_ref, fr_ref, fb_ref, o_ref, lse_ref, do_ref,
             dq_ref, dk_ref, dv_ref, df_ref, dqa_ref, dka_ref, dva_ref, dfa_ref, dls_ref, pbs_ref):
        step_id = pl.program_id(1)

        @pl.when(step_id == 0)
        def _():
            dka_ref[...] = jnp.zeros_like(dka_ref)
            dva_ref[...] = jnp.zeros_like(dva_ref)
            dfa_ref[...] = jnp.zeros_like(dfa_ref)

        m0 = _pair_masks()
        causal = _iota2((T, T), 0) >= _iota2((T, T), 1)
        hs = range(2)

        def query_tile(i, rows):
            qh = _split_pair(q_ref[rows, :] * jnp.asarray(Q_SCALE, BF16), m0)
            doh = _split_pair(do_ref[rows, :], m0)
            dsum = _row_dots(do_ref[rows, :], o_ref[rows, :], m0)
            fq = fq_ref[rows, :]
            fqh = (fq[:, 0:1], fq[:, HEAD_DIM:HEAD_DIM + 1])
            lse = lse_ref[rows, :]
            lseh = (lse[:, 0:1], lse[:, HEAD_DIM:HEAD_DIM + 1])
            dqa_ref[...] = jnp.zeros_like(dqa_ref)

            def flush(j):
                off = pl.multiple_of(j * T, T)
                k = k_ref[pl.ds(off, T), :]
                for h in hs:
                    dqa_ref[h] += _dot(dls_ref[h], k)
                dka_ref[pl.ds(off, T), :] += (_dot_tn(dls_ref[0], qh[0])
                                              + _dot_tn(dls_ref[1], qh[1]))
                dva_ref[pl.ds(off, T), :] += (_dot_tn(pbs_ref[0], doh[0])
                                              + _dot_tn(pbs_ref[1], doh[1]))

            def tile(j, diag):
                off = pl.multiple_of(j * T, T)
                k = k_ref[pl.ds(off, T), :]
                v = v_ref[pl.ds(off, T), :]
                sc = [_dot_nt(qh[h], k) + fqh[h] - fr_ref[h:h + 1, pl.ds(off, T)] for h in hs]
                dp = [_dot_nt(doh[h], v) for h in hs]
                if not diag:
                    flush(j + 1)
                p = [jnp.exp(sc[h] - lseh[h]) for h in hs]
                if diag:
                    p = [jnp.where(causal, p[h], 0.0) for h in hs]
                dl = [p[h] * (dp[h] - dsum[h]) for h in hs]
                for h in hs:
                    dls_ref[h] = dl[h].astype(BF16)
                    pbs_ref[h] = p[h].astype(BF16)
                    dfa_ref[h:h + 1, pl.ds(off, T)] -= jnp.sum(dl[h], axis=0, keepdims=True)

            tile(i, True)
            j_left = _fox_tiles_left(i, pl.program_id(0), nq, fb_ref, tile)
            flush(j_left + 1)
            dq_ref[rows, :] = (jnp.where(m0, dqa_ref[0], dqa_ref[1]) * Q_SCALE).astype(BF16)

        for n in range(QPS):
            query_tile(QPS * step_id + n, slice(n * T, (n + 1) * T))

        @pl.when(step_id == nq // QPS - 1)
        def _():
            dk_ref[...] = dka_ref[...].astype(BF16)
            dv_ref[...] = dva_ref[...].astype(BF16)
            df_ref[...] = dfa_ref[...]

    nb = FX_W // LANES
    base = 3 * SB_W // LANES
    (dq, dk, dv, df), lands = _ride(dict(
        body=body, name=name, grid=(nb, nq // QPS),
        in_specs=[pl.BlockSpec((QPS * T, LANES), lambda p, i: (i, base + p)),
                  pl.BlockSpec((s, LANES), lambda p, i: (0, base + nb + p)),
                  pl.BlockSpec((s, LANES), lambda p, i: (0, base + 2 * nb + p)),
                  pl.BlockSpec((QPS * T, LANES), lambda p, i: (i, p)),
                  pl.BlockSpec((None, 2, s), lambda p, i: (p, 0, 0)),
                  pl.BlockSpec(memory_space=pltpu.SMEM),
                  pl.BlockSpec((QPS * T, LANES), lambda p, i: (i, p)),
                  pl.BlockSpec((QPS * T, LANES), lambda p, i: (i, p)),
                  pl.BlockSpec((QPS * T, LANES), lambda p, i: (i, p))],
        out_specs=[pl.BlockSpec((QPS * T, LANES), lambda p, i: (i, p)),
                   pl.BlockSpec((s, LANES), lambda p, i: (0, p)),
                   pl.BlockSpec((s, LANES), lambda p, i: (0, p)),
                   pl.BlockSpec((None, 2, s), lambda p, i: (p, 0, 0))],
        out_shape=[jax.ShapeDtypeStruct((s, FX_W), BF16)] * 3
        + [jax.ShapeDtypeStruct((nb, 2, s), F32)],
        scratch_shapes=[pltpu.VMEM((2, T, LANES), F32), pltpu.VMEM((s, LANES), F32),
                        pltpu.VMEM((s, LANES), F32), pltpu.VMEM((2, s), F32),
                        pltpu.VMEM((2, T, T), BF16), pltpu.VMEM((2, T, T), BF16)],
        compiler_params=_ARB2, operands=[qkv, qkv, qkv, fqb, frow, fbounds, o, lse, do]), rider)
    return dq, dk, dv, df, lands


def _fox_prep_bwd(dfrow, gf, bpad, name):
    s = gf.shape[0]

    def body(df_ref, fl_ref, b_ref, dfl_ref, db_ref):
        tri = jnp.where(_iota2((T, T), 0) <= _iota2((T, T), 1), 1.0, 0.0).astype(BF16)
        carry = jnp.zeros((1, LANES), F32)
        db = jnp.zeros((1, LANES), F32)
        fill = jnp.zeros((LANES - FOX_HEADS, T), F32)
        for blk in reversed(range(s // T)):
            rows = slice(blk * T, (blk + 1) * T)
            c = _tri3(tri, jnp.concatenate([df_ref[:, rows], fill], axis=0), _dot_nt) + carry
            carry = c[0:1, :]
            dfl = c / (1.0 + jnp.exp(fl_ref[rows, :] + b_ref[...]))
            dfl_ref[rows, :] = dfl.astype(BF16)
            db = db + jnp.sum(dfl, axis=0, keepdims=True)
        db_ref[...] = db

    return pl.pallas_call(
        body, name=name, grid=(1,),
        in_specs=[pl.BlockSpec((FOX_HEADS, s), lambda i: (0, 0)),
                  pl.BlockSpec((s, LANES), lambda i: (0, MIX_W // LANES)),
                  pl.BlockSpec((1, LANES), lambda i: (0, 0))],
        out_specs=[pl.BlockSpec((s, LANES), lambda i: (0, 0)),
                   pl.BlockSpec((1, LANES), lambda i: (0, 0))],
        out_shape=[jax.ShapeDtypeStruct((s, LANES), BF16), jax.ShapeDtypeStruct((1, LANES), F32)],
        compiler_params=_ARB1,
    )(dfrow, gf, bpad)


def _mem_bwd(qkv, kv, o, lse, do, name):
    s = qkv.shape[0]
    n = kv.shape[0]

    def body(q_ref, k_ref, v_ref, o_ref, lse_ref, do_ref, dq_ref, dk_ref, dv_ref):
        @pl.when(pl.program_id(1) == 0)
        def _():
            dk_ref[...] = jnp.zeros_like(dk_ref)
            dv_ref[...] = jnp.zeros_like(dv_ref)

        m0 = _pair_masks()
        qh = _split_pair(q_ref[...] * jnp.asarray(Q_SCALE, BF16), m0)
        doh = _split_pair(do_ref[...], m0)
        dsum = _row_dots(do_ref[...], o_ref[...], m0)
        lse = lse_ref[...]
        lseh = (lse[:, 0:1], lse[:, HEAD_DIM:HEAD_DIM + 1])
        k = k_ref[...]
        v = v_ref[...]
        dqs = []
        for h in range(2):
            p = jnp.exp(_dot_nt(qh[h], k) - lseh[h])
            dl = p * (_dot_nt(doh[h], v) - dsum[h])
            dlb = dl.astype(BF16)
            dqs.append(_dot(dlb, k))
            dk_ref[...] += _dot_tn(dlb, qh[h])
            dv_ref[...] += _dot_tn(p.astype(BF16), doh[h])
        dq_ref[...] = (jnp.where(m0, dqs[0], dqs[1]) * Q_SCALE).astype(BF16)

    nb = MEM_W // LANES
    base = (3 * SB_W + 3 * FX_W) // LANES
    return pl.pallas_call(
        body, name=name, grid=(nb, s // TQM),
        in_specs=[pl.BlockSpec((TQM, LANES), lambda p, i: (i, base + p)),
                  pl.BlockSpec((n, LANES), lambda p, i: (0, p)),
                  pl.BlockSpec((n, LANES), lambda p, i: (0, nb + p)),
                  pl.BlockSpec((TQM, LANES), lambda p, i: (i, p)),
                  pl.BlockSpec((TQM, LANES), lambda p, i: (i, p)),
                  pl.BlockSpec((TQM, LANES), lambda p, i: (i, p))],
        out_specs=[pl.BlockSpec((TQM, LANES), lambda p, i: (i, p)),
                   pl.BlockSpec((n, LANES), lambda p, i: (0, p)),
                   pl.BlockSpec((n, LANES), lambda p, i: (0, p))],
        out_shape=[jax.ShapeDtypeStruct((s, MEM_W), BF16), jax.ShapeDtypeStruct((n, MEM_W), F32),
                   jax.ShapeDtypeStruct((n, MEM_W), F32)],
        compiler_params=_ARB2,
    )(qkv, kv, kv, o, lse, do)


def _memkv_bwd(mem, mnw, wkv, dk, dv, name):
    n = mem.shape[0]

    def body(mem_ref, mnw_ref, w_ref, dk_ref, dv_ref, dw_ref, dmnw_ref):
        mv = mem_ref[...]
        r = lax.rsqrt(jnp.mean(mv * mv, axis=-1, keepdims=True) + EPS)
        mh = mv * r
        hm = (mh * mnw_ref[...]).astype(BF16)
        dkv = jnp.concatenate([dk_ref[...], dv_ref[...]], axis=1).astype(BF16)
        dw_ref[...] = _dot_tn(hm, dkv)
        dhm = _dot_nt(dkv, w_ref[...])
        dmnw_ref[...] = jnp.sum(dhm * mh, axis=0, keepdims=True)

    return pl.pallas_call(
        body, name=name, grid=(1,),
        in_specs=[pl.BlockSpec((n, D_MODEL), lambda i: (0, 0)),
                  pl.BlockSpec((1, D_MODEL), lambda i: (0, 0)),
                  pl.BlockSpec((D_MODEL, 2 * MEM_W), lambda i: (0, 0)),
                  pl.BlockSpec((n, MEM_W), lambda i: (0, 0)),
                  pl.BlockSpec((n, MEM_W), lambda i: (0, 0))],
        out_specs=[pl.BlockSpec((D_MODEL, 2 * MEM_W), lambda i: (0, 0)),
                   pl.BlockSpec((1, D_MODEL), lambda i: (0, 0))],
        out_shape=[jax.ShapeDtypeStruct((D_MODEL, 2 * MEM_W), F32),
                   jax.ShapeDtypeStruct((1, D_MODEL), F32)],
        compiler_params=_ARB1,
    )(mem, mnw, wkv, dk, dv)


def _inproj_bwd_dx(pieces, w_r, x, nw, dxo, name, rider=None):
    s = x.shape[0]
    n = len(pieces)
    widths = [p.shape[1] for p in pieces]

    def body(*refs):
        piece_refs = refs[:n]
        w_ref, x_ref, nw_ref, dxo_ref, dx_ref, h_ref, dnw_ref, dp_ref = refs[n:]

        @pl.when(pl.program_id(0) == 0)
        def _():
            dnw_ref[...] = jnp.zeros_like(dnw_ref)

        col = 0
        for r, wd in zip(piece_refs, widths):
            dp_ref[:, col:col + wd] = r[...]
            col += wd
        dp_ref[:, col:] = jnp.zeros((TM, WR_W - col), BF16)
        dh = _dot(dp_ref[...], w_ref[...])
        xv = x_ref[...]
        nw = nw_ref[...]
        r = lax.rsqrt(jnp.mean(xv * xv, axis=-1, keepdims=True) + EPS)
        xh = xv * r
        h_ref[...] = (xh * nw).astype(BF16)
        dnw_ref[...] += jnp.sum(dh * xh, axis=0, keepdims=True)
        dxh = dh * nw
        dx_ref[...] = r * (dxh - xh * jnp.mean(dxh * xh, axis=-1, keepdims=True)) + dxo_ref[...]

    (dx, h, dnw, dproj), lands = _ride(dict(
        body=body, name=name, grid=(s // TM,),
        in_specs=[pl.BlockSpec((TM, wd), lambda i: (i, 0)) for wd in widths]
        + [pl.BlockSpec((WR_W, D_MODEL), lambda i: (0, 0)),
           pl.BlockSpec((TM, D_MODEL), lambda i: (i, 0)),
           pl.BlockSpec((1, D_MODEL), lambda i: (0, 0)),
           pl.BlockSpec((TM, D_MODEL), lambda i: (i, 0))],
        out_specs=[pl.BlockSpec((TM, D_MODEL), lambda i: (i, 0)),
                   pl.BlockSpec((TM, D_MODEL), lambda i: (i, 0)),
                   pl.BlockSpec((1, D_MODEL), lambda i: (0, 0)),
                   pl.BlockSpec((TM, WR_W), lambda i: (i, 0))],
        out_shape=[jax.ShapeDtypeStruct((s, D_MODEL), F32), jax.ShapeDtypeStruct((s, D_MODEL), BF16),
                   jax.ShapeDtypeStruct((1, D_MODEL), F32), jax.ShapeDtypeStruct((s, WR_W), BF16)],
        scratch_shapes=[], compiler_params=_ARB1, operands=[*pieces, w_r, x, nw, dxo]), rider)
    return dx, h, dnw, dproj, lands


def _inproj_bwd_dw(h, dproj, name):
    s = dproj.shape[0]
    tn = 256

    def body(h_ref, dp_ref, dw_ref):
        dw_ref[...] = _dot_tn(dp_ref[...], h_ref[...])

    return pl.pallas_call(
        body, name=name, grid=(WR_W // tn,),
        in_specs=[pl.BlockSpec((s, D_MODEL), lambda j: (0, 0)),
                  pl.BlockSpec((s, tn), lambda j: (0, j))],
        out_specs=pl.BlockSpec((tn, D_MODEL), lambda j: (j, 0)),
        out_shape=jax.ShapeDtypeStruct((WR_W, D_MODEL), F32),
        compiler_params=_ARB1,
    )(h, dproj)


def _rearrange_w_in(wt):
    pad = jnp.zeros((FL_PAD - FOX_HEADS,) + wt.shape[1:], wt.dtype)
    return jnp.concatenate([wt[:3072], wt[3080:3336], wt[3336:IN_W], wt[3072:3080], pad], axis=0)


def _restore_w_in(g):
    gate0 = QKV_W
    fl0 = QKV_W + MIX_W
    return jnp.concatenate(
        [g[:3072], g[fl0:fl0 + FOX_HEADS], g[3072:QKV_W], g[gate0:fl0]], axis=0)


def _pad_lanes(v, width=LANES):
    return jnp.pad(v, (0, width - v.shape[0])).reshape(1, width)


def _layer_fwd(xs, mem, nw, w_r, b_forget, mnw, late, onw, l, travel=None):
    s = xs.shape[0]
    bpad = _pad_lanes(b_forget)
    travel = _Travel(travel)
    qkv, gf, _ = travel.ride(2, _inproj_fwd, xs, nw, w_r, f"inproj_fwd_{l}")
    fqb, frow, fbounds = _fox_prep_fwd(gf, qkv, bpad, f"fox_prep_fwd_{l}")
    frow = frow.reshape(FOX_HEADS // 2, 2, s)
    ysb, _ = travel.ride(0, _sb_fwd, qkv, f"sb_fwd_{l}")
    yfx, lse_fx, _ = travel.ride(1, _fox_fwd, qkv, fqb, frow, fbounds, f"fox_fwd_{l}")
    wkv, wout = late(travel.lands)
    kv = _memkv_fwd(mem, mnw, wkv, f"memkv_fwd_{l}")
    ym, lse_m = _mem_fwd(qkv, kv, f"mem_fwd_{l}")
    xn = _outproj_fwd(ysb, yfx, ym, gf, onw, wout, xs, f"outproj_fwd_{l}")
    saved = (xs, nw, mnw, onw, bpad, qkv, gf, fqb, frow, fbounds, ysb, yfx, lse_fx, kv, ym, lse_m)
    return xn, saved, travel.lands, (wkv, wout)


class _Travel:
    def __init__(self, plan):
        self.plan = plan
        self.lands = None if plan is None else _new_lands(plan[0], plan[1])

    def ride(self, n, fn, *args):
        if self.plan is None or n >= len(self.plan[2]) or self.plan[2][n] is None:
            return fn(*args)
        srcs, scatter, legs = self.plan
        idx, rows = legs[n]
        out = fn(*args, rider=_Rider([srcs[a] for a in idx], [self.lands[a] for a in idx],
                                     scatter, rows, relay=not scatter))
        for a, land in zip(idx, out[-1]):
            self.lands[a] = land
        return out


def _layer_bwd(dx, saved, mem, w_r, wkv, wout, l, travel=None):
    xs, nw, mnw, onw, bpad, qkv, gf, fqb, frow, fbounds, ysb, yfx, lse_fx, kv, ym, lse_m = saved
    s = xs.shape[0]
    dysb, dyfx, dym, dgate, dwout, donw = _outproj_bwd(
        dx, wout, ysb, yfx, ym, gf, onw, f"outproj_bwd_{l}")
    travel = _Travel(None if travel is None else travel(dwout))
    sdq, sdk, sdv, _ = travel.ride(0, _sb_bwd, qkv, ysb, dysb, f"sb_bwd_{l}")
    fdq, fdk, fdv, dfrow, _ = travel.ride(1, _fox_bwd, qkv, fqb, frow, fbounds, yfx, lse_fx, dyfx,
                                          f"fox_bwd_{l}")
    dfl, db = _fox_prep_bwd(dfrow.reshape(FOX_HEADS, s), gf, bpad, f"fox_prep_bwd_{l}")
    dmq, dmk, dmv = _mem_bwd(qkv, kv, ym, lse_m, dym, f"mem_bwd_{l}")
    dwkv, dmnw = _memkv_bwd(mem, mnw, wkv, dmk, dmv, f"memkv_bwd_{l}")
    dx, ht, dnw, dproj, _ = travel.ride(2, _inproj_bwd_dx,
                                        [sdq, sdk, sdv, fdq, fdk, fdv, dmq, dgate, dfl],
                                        w_r, xs, nw, dx, f"inproj_bwd_dx_{l}")
    dwr = _inproj_bwd_dw(ht, dproj, f"inproj_bwd_dw_{l}")
    grads = dict(norm_w=dnw[0], w_r=dwr, b_forget=db[0, :FOX_HEADS], mem_norm_w=dmnw[0],
                 w_mem_kv=dwkv, out_norm_w=donw[0], w_out=dwout)
    return dx, grads, travel.lands


_ANY = pl.BlockSpec(memory_space=pl.ANY)


def _my_place():
    return lax.axis_index("x"), lax.axis_index("y"), lax.axis_index("c")


def _flip(v, bit):
    return 1 - v if bit else v


def _block_index(px, py, pc):
    return 4 * px + 2 * py + pc


def _all_gather_weights(shards, name):
    n = len(shards)

    def body(*refs):
        ins, outs = refs[:n], refs[n:2 * n]
        send_sems, recv_sems, local_sems = refs[2 * n:]
        x, y, c = _my_place()
        me = (x, y, c)
        sibling = (x, y, 1 - c)
        chips = [(1 - x, y), (x, 1 - y), (1 - x, 1 - y)]

        def copy(a, k, block, to, src=None):
            dst = outs[a].at[_block_index(*block)]
            return pltpu.make_async_remote_copy(
                src_ref=dst if src is None else src, dst_ref=dst,
                send_sem=send_sems.at[a, k], recv_sem=recv_sems.at[a, k],
                device_id=to, device_id_type=pl.DeviceIdType.MESH)

        mine = [pltpu.make_async_copy(ins[a], outs[a].at[_block_index(*me)], local_sems.at[a])
                for a in range(n)]
        for cp in mine:
            cp.start()
        first = []
        for a in range(n):
            first.append(copy(a, 0, me, sibling, src=ins[a]))
            first += [copy(a, 1 + j, me, (*chip, c), src=ins[a]) for j, chip in enumerate(chips)]
        for cp in first:
            cp.start()
        passed = []
        for j, chip in enumerate(chips):
            for a in range(n):
                copy(a, 1 + j, (*chip, c), me).wait_recv()
                fwd = copy(a, 4 + j, (*chip, c), sibling)
                fwd.start()
                passed.append(fwd)
        for a in range(n):
            copy(a, 0, sibling, me).wait_recv()
            for j, chip in enumerate(chips):
                copy(a, 4 + j, (*chip, 1 - c), me).wait_recv()
        for cp in first + passed:
            cp.wait_send()
        for cp in mine:
            cp.wait()

    return pl.pallas_call(
        body, name=name,
        in_specs=[_ANY] * n, out_specs=[_ANY] * n,
        out_shape=[jax.ShapeDtypeStruct((N_DEV,) + v.shape, v.dtype) for v in shards],
        scratch_shapes=[pltpu.SemaphoreType.DMA((n, 7)), pltpu.SemaphoreType.DMA((n, 7)),
                        pltpu.SemaphoreType.DMA((n,))],
    )(*shards)


def _exchange_blocks(blocked, name):
    n = len(blocked)

    def body(*refs):
        ins, outs = refs[:n], refs[n:2 * n]
        send_sems, recv_sems, local_sems = refs[2 * n:]
        x, y, c = _my_place()
        mine_idx = _block_index(x, y, c)
        local = [pltpu.make_async_copy(ins[a].at[mine_idx], outs[a].at[mine_idx], local_sems.at[a])
                 for a in range(n)]
        for cp in local:
            cp.start()
        sends, arrivals = [], []
        for r in range(1, N_DEV):
            peer = (_flip(x, r & 4), _flip(y, r & 2), _flip(c, r & 1))
            peer_idx = _block_index(*peer)
            for a in range(n):
                sems = dict(send_sem=send_sems.at[a, r - 1], recv_sem=recv_sems.at[a, r - 1],
                            device_id=peer, device_id_type=pl.DeviceIdType.MESH)
                sends.append(pltpu.make_async_remote_copy(
                    src_ref=ins[a].at[peer_idx], dst_ref=outs[a].at[mine_idx], **sems))
                arrivals.append(pltpu.make_async_remote_copy(
                    src_ref=ins[a].at[peer_idx], dst_ref=outs[a].at[peer_idx], **sems))
        for cp in sends:
            cp.start()
        for cp in arrivals:
            cp.wait_recv()
        for cp in sends:
            cp.wait_send()
        for cp in local:
            cp.wait()

    return pl.pallas_call(
        body, name=name,
        in_specs=[_ANY] * n, out_specs=[_ANY] * n,
        out_shape=[jax.ShapeDtypeStruct(v.shape, v.dtype) for v in blocked],
        scratch_shapes=[pltpu.SemaphoreType.DMA((n, 7)), pltpu.SemaphoreType.DMA((n, 7)),
                        pltpu.SemaphoreType.DMA((n,))],
    )(*blocked)


N_CHIP = N_DEV // 2


def _pair_swap(blocked, name):
    n = len(blocked)

    def body(*refs):
        ins, outs = refs[:n], refs[n:2 * n]
        send_sems, recv_sems = refs[2 * n:]
        x, y, c = _my_place()
        copies = [pltpu.make_async_remote_copy(
            src_ref=ins[a].at[j, 1 - c], dst_ref=outs[a].at[j],
            send_sem=send_sems.at[N_CHIP * a + j], recv_sem=recv_sems.at[N_CHIP * a + j],
            device_id=(x, y, 1 - c), device_id_type=pl.DeviceIdType.MESH)
            for a in range(n) for j in range(N_CHIP)]
        for cp in copies:
            cp.start()
        for cp in copies:
            cp.wait_recv()
        for cp in copies:
            cp.wait_send()

    return pl.pallas_call(
        body, name=name,
        in_specs=[_ANY] * n, out_specs=[_ANY] * n,
        out_shape=[jax.ShapeDtypeStruct((N_CHIP,) + v.shape[2:], v.dtype) for v in blocked],
        scratch_shapes=[pltpu.SemaphoreType.DMA((N_CHIP * n,)),
                        pltpu.SemaphoreType.DMA((N_CHIP * n,))],
    )(*blocked)


def _pair_add(by_core, theirs, core, name):
    n = len(by_core)

    def body(core_ref, *refs):
        for a in range(n):
            mine, got, out = refs[a], refs[n + a], refs[2 * n + a]
            out[...] = (mine[...].astype(F32) + got[...].astype(F32)).astype(BF16)

    def own(v):
        return pl.BlockSpec((None, None) + v.shape[2:], lambda j, core_ref: (j, core_ref[0], 0, 0))

    def block(v):
        return pl.BlockSpec((None,) + v.shape[1:], lambda j, core_ref: (j, 0, 0))

    return list(pl.pallas_call(
        body, name=name,
        grid_spec=pltpu.PrefetchScalarGridSpec(
            num_scalar_prefetch=1, grid=(N_CHIP,),
            in_specs=[own(v) for v in by_core] + [block(v) for v in theirs],
            out_specs=[block(v) for v in theirs]),
        out_shape=[jax.ShapeDtypeStruct(v.shape, BF16) for v in theirs],
        compiler_params=_ARB1,
    )(core, *by_core, *theirs))


def _chip_exchange(by_chip, to_all, name):
    n, m = len(by_chip), len(to_all)

    def body(*refs):
        ins, alls = refs[:n], refs[n:n + m]
        outs, all_outs = refs[n + m:2 * n + m], refs[2 * n + m:2 * (n + m)]
        send_sems, recv_sems, local_sems = refs[2 * (n + m):]
        x, y, c = _my_place()
        my_chip, mine_idx = 2 * x + y, _block_index(x, y, c)
        local = [pltpu.make_async_copy(ins[a].at[my_chip], outs[a].at[my_chip], local_sems.at[a])
                 for a in range(n)]
        local += [pltpu.make_async_copy(alls[b].at[mine_idx], all_outs[b].at[mine_idx],
                                        local_sems.at[n + b]) for b in range(m)]
        for cp in local:
            cp.start()
        sends, arrivals = [], []
        k = 0
        for r in range(1, N_DEV):
            peer = (_flip(x, r & 4), _flip(y, r & 2), _flip(c, r & 1))
            peer_chip, peer_idx = 2 * peer[0] + peer[1], _block_index(*peer)
            pairs = [(alls[b].at[mine_idx], all_outs[b].at[mine_idx], all_outs[b].at[peer_idx])
                     for b in range(m)]
            if not r & 1:
                pairs += [(ins[a].at[peer_chip], outs[a].at[my_chip], outs[a].at[peer_chip])
                          for a in range(n)]
            for src, there, here in pairs:
                sems = dict(send_sem=send_sems.at[k], recv_sem=recv_sems.at[k], device_id=peer,
                            device_id_type=pl.DeviceIdType.MESH)
                sends.append(pltpu.make_async_remote_copy(src_ref=src, dst_ref=there, **sems))
                arrivals.append(pltpu.make_async_remote_copy(src_ref=src, dst_ref=here, **sems))
                k += 1
        for cp in sends:
            cp.start()
        for cp in arrivals:
            cp.wait_recv()
        for cp in sends:
            cp.wait_send()
        for cp in local:
            cp.wait()

    n_copies = 7 * m + 3 * n
    return pl.pallas_call(
        body, name=name,
        in_specs=[_ANY] * (n + m), out_specs=[_ANY] * (n + m),
        out_shape=[jax.ShapeDtypeStruct(v.shape, v.dtype) for v in by_chip + to_all],
        scratch_shapes=[pltpu.SemaphoreType.DMA((n_copies,)), pltpu.SemaphoreType.DMA((n_copies,)),
                        pltpu.SemaphoreType.DMA((n + m,))],
    )(*by_chip, *to_all)


class _Rider(NamedTuple):
    srcs: list
    lands: list
    scatter: bool
    part: list
    relay: bool = False


def _window(ref, part):
    if part is None:
        return ref
    dim, start, size = part
    return ref.at[(slice(None),) * dim + (pl.ds(start, size),)]


def _relay_copies(srcs, lands, send_sems, recv_sems, rider):
    x, y, c = _my_place()
    me, sibling = (x, y, c), (x, y, 1 - c)
    chips = [(1 - x, y), (x, 1 - y), (1 - x, 1 - y)]
    first, from_chips, passed, last = [], [], [], []
    for a in range(len(srcs)):
        def copy(k, block, to, src=None, a=a):
            dst = _window(lands[a].at[_block_index(*block)], rider.part[a])
            return pltpu.make_async_remote_copy(
                src_ref=dst if src is None else src, dst_ref=dst,
                send_sem=send_sems.at[7 * a + k], recv_sem=recv_sems.at[7 * a + k],
                device_id=to, device_id_type=pl.DeviceIdType.MESH)

        mine = _window(srcs[a], rider.part[a])
        first.append(copy(0, me, sibling, src=mine))
        first += [copy(1 + j, me, (*chip, c), src=mine) for j, chip in enumerate(chips)]
        from_chips += [copy(1 + j, (*chip, c), me) for j, chip in enumerate(chips)]
        passed += [copy(4 + j, (*chip, c), sibling) for j, chip in enumerate(chips)]
        last.append(copy(0, sibling, me))
        last += [copy(4 + j, (*chip, 1 - c), me) for j, chip in enumerate(chips)]
    return first, from_chips, passed, last


def _new_lands(srcs, scatter):
    return [lax.empty(v.shape if scatter else (N_DEV,) + v.shape, v.dtype) for v in srcs]


def _rider_copies(srcs, lands, send_sems, recv_sems, rider):
    x, y, c = _my_place()
    mine_idx = _block_index(x, y, c)

    def window(ref, a):
        return _window(ref, rider.part[a])

    sends, arrivals = [], []
    for r in range(1, N_DEV):
        peer = (_flip(x, r & 4), _flip(y, r & 2), _flip(c, r & 1))
        peer_idx = _block_index(*peer)
        for a in range(len(srcs)):
            src = window(srcs[a].at[peer_idx] if rider.scatter else srcs[a], a)
            k = 7 * a + r - 1
            sems = dict(send_sem=send_sems.at[k], recv_sem=recv_sems.at[k],
                        device_id=peer, device_id_type=pl.DeviceIdType.MESH)
            sends.append(pltpu.make_async_remote_copy(
                src_ref=src, dst_ref=window(lands[a].at[mine_idx], a), **sems))
            arrivals.append(pltpu.make_async_remote_copy(
                src_ref=src, dst_ref=window(lands[a].at[peer_idx], a), **sems))
    return sends, arrivals


def _ride(call, rider):
    call = dict(call)
    body, grid = call.pop("body"), call["grid"]
    operands = call.pop("operands")
    if rider is None:
        return list(pl.pallas_call(body, **call)(*operands)), None
    n_in, n_out = len(call["in_specs"]), len(call["out_specs"])
    n_scratch = len(call["scratch_shapes"])
    m = len(rider.srcs)

    def riding(*refs):
        main_in, srcs, lands = refs[:n_in], refs[n_in:n_in + m], refs[n_in + m:n_in + 2 * m]
        main_out = refs[n_in + 2 * m:n_in + 2 * m + n_out]
        rest = refs[n_in + 3 * m + n_out:]
        send_sems, recv_sems = rest[n_scratch:]
        at = [pl.program_id(d) for d in range(len(grid))]
        first = functools.reduce(jnp.logical_and, [p == 0 for p in at])
        last = functools.reduce(jnp.logical_and, [p == g - 1 for p, g in zip(at, grid)])
        if rider.relay:
            sends, from_chips, passed, arrivals = _relay_copies(
                srcs, lands, send_sems, recv_sems, rider)
            step, steps = 0, 1
            for p, g in zip(at, grid):
                step, steps = step * g + p, steps * g
            assert steps >= 2, "a relayed gather needs a later grid step to pass blocks on"

            @pl.when(step == (3 * steps) // 4)
            def _():
                for cp in from_chips:
                    cp.wait_recv()
                for cp in passed:
                    cp.start()
        else:
            sends, arrivals = _rider_copies(srcs, lands, send_sems, recv_sems, rider)
            passed = []

        @pl.when(first)
        def _():
            for cp in sends:
                cp.start()

        body(*main_in, *main_out, *rest[:n_scratch])

        @pl.when(last)
        def _():
            for cp in arrivals:
                cp.wait_recv()
            for cp in sends + passed:
                cp.wait_send()

    call["in_specs"] = list(call["in_specs"]) + [_ANY] * (2 * m)
    call["out_specs"] = list(call["out_specs"]) + [_ANY] * m
    call["out_shape"] = list(call["out_shape"]) + [
        jax.ShapeDtypeStruct(v.shape, v.dtype) for v in rider.lands]
    call["scratch_shapes"] = list(call["scratch_shapes"]) + [
        pltpu.SemaphoreType.DMA((7 * m,)), pltpu.SemaphoreType.DMA((7 * m,))]
    call["input_output_aliases"] = {n_in + m + a: n_out + a for a in range(m)}
    outs = pl.pallas_call(riding, **call)(*operands, *rider.srcs, *rider.lands)
    return list(outs[:n_out]), list(outs[n_out:])


def _sum_parts(p_ref):
    g = p_ref[0].astype(F32)
    for k in range(1, p_ref.shape[0]):
        g = g + p_ref[k].astype(F32)
    return g


def _adamw(g, w, m, v):
    c1 = 1.0 / (1.0 - ADAM_B1 ** ADAM_STEP)
    c2 = 1.0 / (1.0 - ADAM_B2 ** ADAM_STEP)
    nm = ADAM_B1 * m + (1.0 - ADAM_B1) * g
    nv = ADAM_B2 * v + (1.0 - ADAM_B2) * (g * g)
    return nm, nv, -ADAM_LR * ((nm * c1) / (jnp.sqrt(nv * c2) + ADAM_EPS) + ADAM_WD * w)


def _adamw_w_in(parts, w, m, v, name):
    ncol_blk, depth, nfeat = w.shape
    cols = 256

    def body(*refs):
        p_refs = refs[:depth]
        w_ref, m_ref, v_ref, g_ref, d_ref, nm_ref, nv_ref = refs[depth:]
        for l in range(depth):
            g = _sum_parts(p_refs[l])
            nm, nv, d = _adamw(g, w_ref[:, l, :], m_ref[:, l, :], v_ref[:, l, :])
            g_ref[:, l, :] = g
            nm_ref[:, l, :] = nm
            nv_ref[:, l, :] = nv
            d_ref[:, l, :] = d

    blk = pl.BlockSpec((ncol_blk, depth, cols), lambda j: (0, 0, j))
    return pl.pallas_call(
        body, name=name, grid=(nfeat // cols,),
        in_specs=[pl.BlockSpec((p.shape[0], ncol_blk, cols), lambda j: (0, 0, j)) for p in parts]
        + [blk] * 3,
        out_specs=[blk] * 4,
        out_shape=[jax.ShapeDtypeStruct(w.shape, F32)] * 4,
        compiler_params=_ARB1,
    )(*parts, w, m, v)


def _adamw_sum(parts, w, m, v, tile, name):
    depth, nrow, ncol = w.shape
    rows, cols = tile

    def body(*refs):
        p_refs = refs[:depth]
        w_ref, m_ref, v_ref, g_ref, d_ref, nm_ref, nv_ref = refs[depth:]
        layer = pl.program_id(0)
        for l in range(depth):
            @pl.when(layer == l)
            def _(p_ref=p_refs[l]):
                g = _sum_parts(p_ref)
                nm, nv, d = _adamw(g, w_ref[...], m_ref[...], v_ref[...])
                g_ref[...] = g
                nm_ref[...] = nm
                nv_ref[...] = nv
                d_ref[...] = d

    def part_spec(l):
        return pl.BlockSpec((parts[l].shape[0], rows, cols), lambda q, i, j: (
            0, jnp.where(q == l, i, 0), jnp.where(q == l, j, 0)))

    blk = pl.BlockSpec((None, rows, cols), lambda q, i, j: (q, i, j))
    return pl.pallas_call(
        body, name=name, grid=(depth, nrow // rows, ncol // cols),
        in_specs=[part_spec(l) for l in range(depth)] + [blk, blk, blk],
        out_specs=[blk] * 4,
        out_shape=[jax.ShapeDtypeStruct(w.shape, F32)] * 4,
        compiler_params=pltpu.CompilerParams(
            dimension_semantics=("arbitrary", "arbitrary", "arbitrary")),
    )(*parts, w, m, v)


def _adamw_small(parts, ws, ms, vs, name):
    n = len(parts)

    def body(*refs):
        p, w, m, v = (refs[a * n:(a + 1) * n] for a in range(4))
        out = refs[4 * n:]
        for a in range(n):
            g = _sum_parts(p[a])
            nm, nv, d = _adamw(g, w[a][...], m[a][...], v[a][...])
            out[a][...] = g
            out[n + a][...] = d
            out[2 * n + a][...] = nm
            out[3 * n + a][...] = nv

    vmem = pl.BlockSpec(memory_space=pltpu.VMEM)
    outs = pl.pallas_call(
        body, name=name, in_specs=[vmem] * (4 * n), out_specs=[vmem] * (4 * n),
        out_shape=[jax.ShapeDtypeStruct(w.shape, F32) for w in ws] * 4,
    )(*parts, *ws, *ms, *vs)
    return [list(outs[a * n:(a + 1) * n]) for a in range(4)]


def _misc_rows(b_forget, extra=None):
    tile = jnp.pad(b_forget, ((0, 6), (0, LANES - FOX_HEADS)))
    return tile if extra is None else tile.at[2].set(extra)


def kernel(x, mem, norm_w, w_in, b_forget, mem_norm_w, w_mem_kv, out_norm_w, w_out, final_norm_w, loss_target, m_norm_w, m_w_in, m_b_forget, m_mem_norm_w, m_w_mem_kv, m_out_norm_w, m_w_out, m_final_norm_w, v_norm_w, v_w_in, v_b_forget, v_mem_norm_w, v_w_mem_kv, v_out_norm_w, v_w_out, v_final_norm_w):
    kv_rows = w_mem_kv.shape[1]
    out_rows = w_out.shape[1]
    me = _block_index(*_my_place())

    def shards(l):
        return [w_in[l].T.astype(BF16), w_mem_kv[l].astype(BF16), w_out[l].astype(BF16)]

    def full_in(g_in):
        return _rearrange_w_in(g_in.reshape(IN_W, D_MODEL))

    def full_kv_out(g_kv, g_out):
        return g_kv.reshape(D_MODEL, 2 * MEM_W), g_out.reshape(MIX_W, D_MODEL)

    def in_blocks(g):
        segments = [(0, 3072, 0), (3072, 3080, QKV_W + MIX_W), (3080, 3336, 3072),
                    (3336, IN_W, QKV_W)]

        def block(k):
            lo, hi = k * SHARD_W, (k + 1) * SHARD_W
            pieces = [g[at + max(lo, a) - a:at + min(hi, b) - a]
                      for a, b, at in segments if max(lo, a) < min(hi, b)]
            return jnp.concatenate(pieces, axis=0).astype(BF16)

        return jnp.stack([block(k) for k in range(N_DEV)])

    def kv_blocks(g):
        return g.reshape(N_DEV, kv_rows, 2 * MEM_W).astype(BF16)

    def out_blocks(g):
        return g.reshape(N_DEV, out_rows, D_MODEL).astype(BF16)

    def with_own(land, own):
        return lax.dynamic_update_slice(land, own[None], (me,) + (0,) * own.ndim)

    def with_own_of(land, blocked):
        return lax.dynamic_update_slice(land, lax.dynamic_slice_in_dim(blocked, me, 1, axis=0),
                                        (me,) + (0,) * (land.ndim - 1))

    def row(v):
        return v.reshape(1, -1)

    def cols(first, size):
        return (1, first, size)

    fwd_split, bwd_split = 4 * LANES, (5 * LANES, 6 * LANES)

    s_in0, s_kv0, s_out0 = shards(0)
    (g_in0,) = _all_gather_weights([s_in0], "all_gather_l0")
    w_r0 = full_in(g_in0)
    s_in1, s_kv1, s_out1 = shards(1)
    x1, saved0, (l_in1, _, _, lw_kv1, lw_out1), (wkv0, wout0) = _layer_fwd(
        x[0], mem[0], row(norm_w[0]), w_r0, b_forget[0], row(mem_norm_w[0]),
        lambda lands: full_kv_out(with_own(lands[1], s_kv0), with_own(lands[2], s_out0)),
        row(out_norm_w[0]), 0,
        travel=([s_in1, s_kv0, s_out0, s_kv1, s_out1], False,
                [([0, 3, 4], [cols(0, fwd_split), None, None]),
                 ([0], [cols(fwd_split, D_MODEL - fwd_split)]), ([1, 2], [None, None])]))
    w_r1 = full_in(with_own(l_in1, s_in1))
    x2, saved1, _, (wkv1, wout1) = _layer_fwd(
        x1, mem[0], row(norm_w[1]), w_r1, b_forget[1], row(mem_norm_w[1]),
        lambda lands: full_kv_out(with_own(lw_kv1, s_kv1), with_own(lw_out1, s_out1)),
        row(out_norm_w[1]), 1)

    dx2, loss_part, dfnw = _final_fwd_bwd(x2, row(final_norm_w), loss_target[0], "final_fwd_bwd")

    dx1, gr1, (l_out1,) = _layer_bwd(
        dx2, saved1, mem[0], w_r1, wkv1, wout1, 1,
        travel=lambda dwout: ([out_blocks(dwout)], True, [([0], [None]), None]))
    p_in1, p_kv1 = in_blocks(gr1["w_r"]), kv_blocks(gr1["w_mem_kv"])
    grad_x, gr0, (l_in1, l_kv1, l_out0) = _layer_bwd(
        dx1, saved0, mem[0], w_r0, wkv0, wout0, 0,
        travel=lambda dwout: ([p_in1, p_kv1, out_blocks(dwout)], True,
                              [([0, 1], [cols(0, bwd_split[0]), None]),
                               ([0, 2], [cols(bwd_split[0], bwd_split[1] - bwd_split[0]), None]),
                               ([0], [cols(bwd_split[1], D_MODEL - bwd_split[1])])]))
    r_out1 = with_own_of(l_out1, out_blocks(gr1["w_out"]))
    r_in1, r_kv1 = with_own_of(l_in1, p_in1), with_own_of(l_kv1, p_kv1)
    r_out0 = with_own_of(l_out0, out_blocks(gr0["w_out"]))

    def both(name):
        return jnp.stack([gr0[name], gr1[name]])

    small = [both("norm_w"), both("mem_norm_w"), both("out_norm_w"), dfnw,
             _misc_rows(both("b_forget"), loss_part[0])]
    p_small = [jnp.broadcast_to(v[None], (N_DEV,) + v.shape) for v in small]
    by_core = [v.reshape((N_CHIP, 2) + v.shape[1:])
               for v in (in_blocks(gr0["w_r"]), kv_blocks(gr0["w_mem_kv"]))]
    from_sibling = _pair_swap(by_core, "grads_l0_pair_swap")
    core = lax.axis_index("c").astype(jnp.int32).reshape(1)
    chip_sums = _pair_add(by_core, from_sibling, core, "grads_l0_pair_add")
    r_in0, r_kv0, *r_small = _chip_exchange(chip_sums, p_small, "exchange_grads_l0")

    def view(v):
        return jnp.transpose(v, (2, 0, 1))

    g_w_in, d_w_in, nm_w_in, nv_w_in = [jnp.transpose(v, (1, 2, 0)) for v in _adamw_w_in(
        [r_in0, r_in1], view(w_in), view(m_w_in), view(v_w_in), "adamw_w_in")]
    g_w_kv, d_w_kv, nm_w_kv, nv_w_kv = _adamw_sum(
        [r_kv0, r_kv1], w_mem_kv, m_w_mem_kv, v_w_mem_kv, (kv_rows, 2 * MEM_W), "adamw_w_mem_kv")
    g_w_out, d_w_out, nm_w_out, nv_w_out = _adamw_sum(
        [r_out0, r_out1], w_out, m_w_out, v_w_out, (out_rows, D_MODEL), "adamw_w_out")
    def smalls(nw, mnw, onw, fnw, b):
        return [nw, mnw, onw, row(fnw), _misc_rows(b)]

    small_out = _adamw_small(
        r_small, smalls(norm_w, mem_norm_w, out_norm_w, final_norm_w, b_forget),
        smalls(m_norm_w, m_mem_norm_w, m_out_norm_w, m_final_norm_w, m_b_forget),
        smalls(v_norm_w, v_mem_norm_w, v_out_norm_w, v_final_norm_w, v_b_forget), "adamw_small")
    (g_nw, g_mnw, g_onw, g_fnw, g_b), (d_nw, d_mnw, d_onw, d_fnw, d_b), \
        (nm_nw, nm_mnw, nm_onw, nm_fnw, nm_b), (nv_nw, nv_mnw, nv_onw, nv_fnw, nv_b) = [
            (nw, mnw, onw, fnw[0], misc[:2, :FOX_HEADS]) for nw, mnw, onw, fnw, misc in small_out]
    loss = small_out[0][4][2, 0]

    return (loss, grad_x[None],
            g_nw, g_w_in, g_b, g_mnw, g_w_kv, g_onw, g_w_out, g_fnw,
            d_nw, d_w_in, d_b, d_mnw, d_w_kv, d_onw, d_w_out, d_fnw,
            nm_nw, nm_w_in, nm_b, nm_mnw, nm_w_kv, nm_onw, nm_w_out, nm_fnw,
            nv_nw, nv_w_in, nv_b, nv_mnw, nv_w_kv, nv_onw, nv_w_out, nv_fnw)
```

```python
import functools
from typing import NamedTuple

import jax
import jax.numpy as jnp
from jax import lax
from jax.experimental import pallas as pl
from jax.experimental.pallas import tpu as pltpu

F32 = jnp.float32
BF16 = jnp.bfloat16

N_DEV = 8
D_MODEL = 1024
HEAD_DIM = 64
LANES = 128
SB_W = 512
FX_W = 512
MEM_W = 256
MIX_W = 1280
FOX_HEADS = 8
IN_W = 4616
SHARD_W = IN_W // N_DEV
QKV_W = 3 * SB_W + 3 * FX_W + MEM_W
FL_PAD = 256
GF_W = MIX_W + FL_PAD
WR_W = QKV_W + GF_W
EPS = 1e-6
T = 256
QPS = 4
TM = 256
TQM = 512
Q_SCALE = 0.125
NEG = -1e30
UNDERFLOW = -110.0
NORM_SLACK = 1.01

ADAM_LR = 0.001
ADAM_B1 = 0.9
ADAM_B2 = 0.999
ADAM_EPS = 1e-08
ADAM_WD = 0.01
ADAM_STEP = 10


_NT = (((1,), (1,)), ((), ()))
_TN = (((0,), (0,)), ((), ()))

_ARB1 = pltpu.CompilerParams(dimension_semantics=("arbitrary",))
_ARB2 = pltpu.CompilerParams(dimension_semantics=("arbitrary", "arbitrary"))


def _dot(a, b):
    return jnp.dot(a, b, preferred_element_type=F32)


def _dot_nt(a, b):
    return lax.dot_general(a, b, _NT, preferred_element_type=F32)


def _dot_tn(a, b):
    return lax.dot_general(a, b, _TN, preferred_element_type=F32)


def _split2(x):
    hi = x.astype(BF16)
    lo = (x - hi.astype(F32)).astype(BF16)
    return hi, lo


def _stack2(u):
    return jnp.concatenate([u, u], axis=0)


def _cum2(x, u2):
    hi, lo = _split2(x)
    return _dot(jnp.concatenate([hi, lo], axis=1), u2)


def _tri3(tri, x, dot=None):
    dot = dot or _dot
    hi = x.astype(BF16)
    r1 = x - hi.astype(F32)
    mid = r1.astype(BF16)
    lo = (r1 - mid.astype(F32)).astype(BF16)
    return dot(tri, hi) + dot(tri, mid) + dot(tri, lo)


def _iota2(shape, dim):
    return lax.broadcasted_iota(jnp.int32, shape, dim)


def _head_block_diag():
    r = _iota2((LANES, LANES), 0) // HEAD_DIM
    c = _iota2((LANES, LANES), 1) // HEAD_DIM
    return _stack2(jnp.where(r == c, 1.0, 0.0).astype(BF16))


def _head_mean(x, bd):
    return _cum2(x, bd) * (1.0 / HEAD_DIM)


def _sigmoid(x):
    return 1.0 / (1.0 + jnp.exp(-x))


def _log_sigmoid(x):
    return jnp.minimum(x, 0.0) - jnp.log(1.0 + jnp.exp(-jnp.abs(x)))


def _running_top(r_ref):
    return jnp.max(jnp.maximum(r_ref[0], r_ref[1]))


def _fox_tiles_left(i, pair, nq, fb_ref, tile):
    def bound(j):
        b = []
        for h in range(2):
            head = 2 * pair + h
            b.append(2.0 * NORM_SLACK * fb_ref[2 * nq + i, head] * fb_ref[3 * nq, head]
                     + fb_ref[2 * i, head] - fb_ref[2 * j + 1, head])
        return jnp.maximum(b[0], b[1])

    def more(j):
        return jnp.logical_and(j >= 0, bound(jnp.maximum(j, 0)) > UNDERFLOW)

    def step(j):
        tile(j, False)
        return j - 1

    return lax.while_loop(more, step, i - 1)


def _pair_masks():
    lane = _iota2((1, LANES), 1)
    return lane < HEAD_DIM


def _split_pair(x, m0):
    zero = jnp.zeros_like(x)
    return jnp.where(m0, x, zero), jnp.where(m0, zero, x)


def _inproj_fwd(x, nw, w_r, name, rider=None):
    s = x.shape[0]

    def body(x_ref, nw_ref, w_ref, qkv_ref, gf_ref):
        xv = x_ref[...]
        r = lax.rsqrt(jnp.mean(xv * xv, axis=-1, keepdims=True) + EPS)
        h = (xv * r * nw_ref[...]).astype(BF16)
        for c in range(0, QKV_W, 256):
            qkv_ref[:, c:c + 256] = _dot_nt(h, w_ref[c:c + 256, :]).astype(BF16)
        for c in range(0, GF_W, 256):
            gf_ref[:, c:c + 256] = _dot_nt(h, w_ref[QKV_W + c:QKV_W + c + 256, :])

    (qkv, gf), lands = _ride(dict(
        body=body, name=name, grid=(s // TM,),
        in_specs=[pl.BlockSpec((TM, D_MODEL), lambda i: (i, 0)),
                  pl.BlockSpec((1, D_MODEL), lambda i: (0, 0)),
                  pl.BlockSpec((WR_W, D_MODEL), lambda i: (0, 0))],
        out_specs=[pl.BlockSpec((TM, QKV_W), lambda i: (i, 0)),
                   pl.BlockSpec((TM, GF_W), lambda i: (i, 0))],
        out_shape=[jax.ShapeDtypeStruct((s, QKV_W), BF16), jax.ShapeDtypeStruct((s, GF_W), F32)],
        scratch_shapes=[], compiler_params=_ARB1, operands=[x, nw, w_r]), rider)
    return qkv, gf, lands


def _fox_prep_fwd(gf, qkv, bpad, name):
    s = gf.shape[0]
    nq = s // T
    nrow = -(-(3 * nq + 1) // 8) * 8

    def body(fl_ref, q_ref, k_ref, b_ref, fq_ref, fr_ref, fb_ref):
        tri = jnp.where(_iota2((T, T), 0) >= _iota2((T, T), 1), 1.0, 0.0).astype(BF16)
        m0 = _pair_masks()
        lane = _iota2((1, LANES), 1)
        norms = [jnp.zeros((1, LANES), F32) for _ in range(nq + 1)]
        same_head = (_iota2((LANES, LANES), 0) // HEAD_DIM) == (_iota2((LANES, LANES), 1) // HEAD_DIM)
        bd = jnp.where(same_head, 1.0, 0.0).astype(BF16)
        for p in range(FOX_HEADS // 2):
            cols = slice(p * LANES, (p + 1) * LANES)
            q = (q_ref[:, cols] * jnp.asarray(Q_SCALE, BF16)).astype(F32)
            k = k_ref[:, cols].astype(F32)
            qn = _dot((q * q).astype(BF16), bd)
            kn = _dot((k * k).astype(BF16), bd)
            tops = [jnp.max(qn[j * T:(j + 1) * T], axis=0, keepdims=True) for j in range(nq)]
            tops.append(jnp.max(kn, axis=0, keepdims=True))
            tops = [jnp.sqrt(top) for top in tops]
            for h in range(2):
                at = h * HEAD_DIM
                norms = [jnp.where(lane == 2 * p + h, top[:, at:at + 1], row)
                         for top, row in zip(tops, norms)]
        for j in range(nq + 1):
            fb_ref[2 * nq + j:2 * nq + j + 1, :] = norms[j]
        fb_ref[3 * nq + 1:, :] = jnp.zeros((nrow - 3 * nq - 1, LANES), F32)
        carry = jnp.zeros((1, LANES), F32)
        for blk in range(s // T):
            rows = slice(blk * T, (blk + 1) * T)
            lf = _log_sigmoid(fl_ref[rows, :] + b_ref[...])
            c = _tri3(tri, lf) + carry
            carry = c[T - 1:T, :]
            for p in range(FOX_HEADS // 2):
                fq_ref[rows, p * LANES:(p + 1) * LANES] = jnp.where(
                    m0, c[:, 2 * p:2 * p + 1], c[:, 2 * p + 1:2 * p + 2])
            fr_ref[:, rows] = c.T[0:FOX_HEADS, :]
            fb_ref[2 * blk:2 * blk + 1, :] = c[0:1, :]
            fb_ref[2 * blk + 1:2 * blk + 2, :] = carry

    base = 3 * SB_W // FX_W
    return pl.pallas_call(
        body, name=name, grid=(1,),
        in_specs=[pl.BlockSpec((s, LANES), lambda i: (0, MIX_W // LANES)),
                  pl.BlockSpec((s, FX_W), lambda i: (0, base)),
                  pl.BlockSpec((s, FX_W), lambda i: (0, base + 1)),
                  pl.BlockSpec((1, LANES), lambda i: (0, 0))],
        out_specs=[pl.BlockSpec((s, FX_W), lambda i: (0, 0)),
                   pl.BlockSpec((FOX_HEADS, s), lambda i: (0, 0)),
                   pl.BlockSpec((nrow, LANES), lambda i: (0, 0))],
        out_shape=[jax.ShapeDtypeStruct((s, FX_W), F32), jax.ShapeDtypeStruct((FOX_HEADS, s), F32),
                   jax.ShapeDtypeStruct((nrow, LANES), F32)],
        compiler_params=_ARB1,
    )(gf, qkv, qkv, bpad)


def _sb_fwd(qkv, name, rider=None):
    s = qkv.shape[0]

    def body(q_ref, k_ref, v_ref, o_ref, acc_ref, r_ref, as_ref):
        m0 = _pair_masks()
        strict = _iota2((T, T), 0) > _iota2((T, T), 1)
        u2 = _stack2(jnp.where(strict, 1.0, 0.0).astype(BF16))
        hs = range(2)

        def query_tile(i, rows):
            qh = _split_pair(q_ref[rows, :] * jnp.asarray(Q_SCALE, BF16), m0)
            acc_ref[...] = jnp.zeros_like(acc_ref)
            r_ref[...] = jnp.zeros_like(r_ref)

            def flush(j):
                v = v_ref[pl.ds(pl.multiple_of(j * T, T), T), :]
                for h in hs:
                    acc_ref[h] += _dot(as_ref[h], v)

            def tile(j, diag):
                k = k_ref[pl.ds(pl.multiple_of(j * T, T), T), :]
                z = [_dot_nt(qh[h], k) for h in hs]
                if not diag:
                    flush(j + 1)
                la = [jnp.minimum(z[h], 0.0) - jnp.log(1.0 + jnp.exp(-jnp.abs(z[h]))) for h in hs]
                lf = [la[h] - z[h] for h in hs]
                if diag:
                    lf = [jnp.where(strict, lf[h], 0.0) for h in hs]
                cin = [_cum2(lf[h], u2) for h in hs]
                a = [jnp.exp(la[h] + cin[h] + r_ref[h]) for h in hs]
                if diag:
                    a = [jnp.where(strict, a[h], 0.0) for h in hs]
                for h in hs:
                    r_ref[h] += cin[h][:, 0:1] + lf[h][:, 0:1]
                    as_ref[h] = a[h].astype(BF16)

            tile(i, True)

            def more(state):
                j, top = state
                return jnp.logical_and(j >= 0, top > UNDERFLOW)

            def step(state):
                j, _ = state
                tile(j, False)
                return j - 1, _running_top(r_ref)

            j_left, _ = lax.while_loop(more, step, (i - 1, _running_top(r_ref)))
            flush(j_left + 1)
            o_ref[rows, :] = jnp.where(m0, acc_ref[0], acc_ref[1])

        for n in range(QPS):
            query_tile(QPS * pl.program_id(1) + n, slice(n * T, (n + 1) * T))

    nb = SB_W // LANES
    (ysb,), lands = _ride(dict(
        body=body, name=name, grid=(nb, s // (QPS * T)),
        in_specs=[pl.BlockSpec((QPS * T, LANES), lambda p, i: (i, p)),
                  pl.BlockSpec((s, LANES), lambda p, i: (0, nb + p)),
                  pl.BlockSpec((s, LANES), lambda p, i: (0, 2 * nb + p))],
        out_specs=[pl.BlockSpec((QPS * T, LANES), lambda p, i: (i, p))],
        out_shape=[jax.ShapeDtypeStruct((s, SB_W), F32)],
        scratch_shapes=[pltpu.VMEM((2, T, LANES), F32), pltpu.VMEM((2, T, 1), F32),
                        pltpu.VMEM((2, T, T), BF16)],
        compiler_params=_ARB2, operands=[qkv, qkv, qkv]), rider)
    return ysb, lands


def _fox_fwd(qkv, fqb, frow, fbounds, name, rider=None):
    s = qkv.shape[0]

    def body(q_ref, k_ref, v_ref, fq_ref, fr_ref, fb_ref, o_ref, lse_ref, acc_ref, m_ref, ps_ref):
        pair = pl.program_id(0)
        m0 = _pair_masks()
        causal = _iota2((T, T), 0) >= _iota2((T, T), 1)
        ones = jnp.ones((T, LANES), BF16)
        hs = range(2)

        def query_tile(i, rows):
            qh = _split_pair(q_ref[rows, :] * jnp.asarray(Q_SCALE, BF16), m0)
            fq = fq_ref[rows, :]
            fqh = (fq[:, 0:1], fq[:, HEAD_DIM:HEAD_DIM + 1])
            acc_ref[...] = jnp.zeros_like(acc_ref)
            m_ref[...] = jnp.full_like(m_ref, NEG)

            def flush(j):
                v = v_ref[pl.ds(pl.multiple_of(j * T, T), T), :]
                va2 = _stack2(jnp.concatenate([v, ones], axis=1))
                for h in hs:
                    acc_ref[h] += _dot(ps_ref[h], va2)

            def tile(j, diag):
                off = pl.multiple_of(j * T, T)
                k = k_ref[pl.ds(off, T), :]
                sc = [_dot_nt(qh[h], k) + fqh[h] - fr_ref[h:h + 1, pl.ds(off, T)] for h in hs]
                if not diag:
                    flush(j + 1)
                if diag:
                    sc = [jnp.where(causal, sc[h], NEG) for h in hs]
                m_new = [jnp.maximum(m_ref[h], jnp.max(sc[h], axis=1, keepdims=True)) for h in hs]
                p = [jnp.exp(sc[h] - m_new[h]) for h in hs]
                for h in hs:
                    acc_ref[h] = acc_ref[h] * jnp.exp(m_ref[h] - m_new[h])
                    m_ref[h] = m_new[h]
                    p_hi, p_lo = _split2(p[h])
                    ps_ref[h] = jnp.concatenate([p_hi, p_lo], axis=1)

            tile(i, True)
            j_left = _fox_tiles_left(i, pair, s // T, fb_ref, tile)
            flush(j_left + 1)
            acc = (acc_ref[0], acc_ref[1])
            o_ref[rows, :] = jnp.where(m0, acc[0][:, :LANES] / acc[0][:, LANES:],
                                       acc[1][:, :LANES] / acc[1][:, LANES:])
            lse_ref[rows, :] = jnp.where(m0, m_ref[0] + jnp.log(acc[0][:, LANES:]),
                                         m_ref[1] + jnp.log(acc[1][:, LANES:]))

        for n in range(QPS):
            query_tile(QPS * pl.program_id(1) + n, slice(n * T, (n + 1) * T))

    nb = FX_W // LANES
    base = 3 * SB_W // LANES
    (yfx, lse), lands = _ride(dict(
        body=body, name=name, grid=(nb, s // (QPS * T)),
        in_specs=[pl.BlockSpec((QPS * T, LANES), lambda p, i: (i, base + p)),
                  pl.BlockSpec((s, LANES), lambda p, i: (0, base + nb + p)),
                  pl.BlockSpec((s, LANES), lambda p, i: (0, base + 2 * nb + p)),
                  pl.BlockSpec((QPS * T, LANES), lambda p, i: (i, p)),
                  pl.BlockSpec((None, 2, s), lambda p, i: (p, 0, 0)),
                  pl.BlockSpec(memory_space=pltpu.SMEM)],
        out_specs=[pl.BlockSpec((QPS * T, LANES), lambda p, i: (i, p)),
                   pl.BlockSpec((QPS * T, LANES), lambda p, i: (i, p))],
        out_shape=[jax.ShapeDtypeStruct((s, FX_W), F32), jax.ShapeDtypeStruct((s, FX_W), F32)],
        scratch_shapes=[pltpu.VMEM((2, T, 2 * LANES), F32), pltpu.VMEM((2, T, 1), F32),
                        pltpu.VMEM((2, T, 2 * T), BF16)],
        compiler_params=_ARB2, operands=[qkv, qkv, qkv, fqb, frow, fbounds]), rider)
    return yfx, lse, lands


def _memkv_fwd(mem, mnw, wkv, name):
    n = mem.shape[0]

    def body(mem_ref, mnw_ref, w_ref, kv_ref):
        mv = mem_ref[...]
        r = lax.rsqrt(jnp.mean(mv * mv, axis=-1, keepdims=True) + EPS)
        hm = (mv * r * mnw_ref[...]).astype(BF16)
        kv_ref[...] = _dot(hm, w_ref[...]).astype(BF16)

    return pl.pallas_call(
        body, name=name, grid=(1,),
        in_specs=[pl.BlockSpec((n, D_MODEL), lambda i: (0, 0)),
                  pl.BlockSpec((1, D_MODEL), lambda i: (0, 0)),
                  pl.BlockSpec((D_MODEL, 2 * MEM_W), lambda i: (0, 0))],
        out_specs=pl.BlockSpec((n, 2 * MEM_W), lambda i: (0, 0)),
        out_shape=jax.ShapeDtypeStruct((n, 2 * MEM_W), BF16),
        compiler_params=_ARB1,
    )(mem, mnw, wkv)


def _mem_fwd(qkv, kv, name):
    s = qkv.shape[0]
    n = kv.shape[0]

    def body(q_ref, k_ref, v_ref, o_ref, lse_ref):
        m0 = _pair_masks()
        qh = _split_pair(q_ref[...] * jnp.asarray(Q_SCALE, BF16), m0)
        k = k_ref[...]
        v = v_ref[...]
        outs, lses = [], []
        for h in range(2):
            sc = _dot_nt(qh[h], k)
            mx = jnp.max(sc, axis=1, keepdims=True)
            p = jnp.exp(sc - mx)
            l = jnp.sum(p, axis=1, keepdims=True)
            outs.append(_dot(p.astype(BF16), v) / l)
            lses.append(mx + jnp.log(l))
        o_ref[...] = jnp.where(m0, outs[0], outs[1])
        lse_ref[...] = jnp.where(m0, lses[0], lses[1])

    nb = MEM_W // LANES
    base = (3 * SB_W + 3 * FX_W) // LANES
    return pl.pallas_call(
        body, name=name, grid=(nb, s // TQM),
        in_specs=[pl.BlockSpec((TQM, LANES), lambda p, i: (i, base + p)),
                  pl.BlockSpec((n, LANES), lambda p, i: (0, p)),
                  pl.BlockSpec((n, LANES), lambda p, i: (0, nb + p))],
        out_specs=[pl.BlockSpec((TQM, LANES), lambda p, i: (i, p)),
                   pl.BlockSpec((TQM, LANES), lambda p, i: (i, p))],
        out_shape=[jax.ShapeDtypeStruct((s, MEM_W), F32), jax.ShapeDtypeStruct((s, MEM_W), F32)],
        compiler_params=_ARB2,
    )(qkv, kv, kv)


def _mix_chunk(c, ysb_ref, yfx_ref, ym_ref):
    if c < SB_W // LANES:
        return ysb_ref[:, c * LANES:(c + 1) * LANES]
    c -= SB_W // LANES
    if c < FX_W // LANES:
        return yfx_ref[:, c * LANES:(c + 1) * LANES]
    c -= FX_W // LANES
    return ym_ref[:, c * LANES:(c + 1) * LANES]


def _outproj_fwd(ysb, yfx, ym, gf, onw, wout, x, name):
    s = x.shape[0]

    def body(ysb_ref, yfx_ref, ym_ref, g_ref, onw_ref, w_ref, x_ref, o_ref, yg_ref):
        bd = _head_block_diag()
        for c in range(MIX_W // LANES):
            sl = slice(c * LANES, (c + 1) * LANES)
            u = _mix_chunk(c, ysb_ref, yfx_ref, ym_ref)
            r = lax.rsqrt(_head_mean(u * u, bd) + EPS)
            g = g_ref[:, sl]
            yg_ref[:, sl] = (u * r * onw_ref[:, sl] * (g * _sigmoid(g))).astype(BF16)
        o_ref[...] = x_ref[...] + _dot(yg_ref[...], w_ref[...])

    return pl.pallas_call(
        body, name=name, grid=(s // TM,),
        in_specs=[pl.BlockSpec((TM, SB_W), lambda i: (i, 0)),
                  pl.BlockSpec((TM, FX_W), lambda i: (i, 0)),
                  pl.BlockSpec((TM, MEM_W), lambda i: (i, 0)),
                  pl.BlockSpec((TM, MIX_W), lambda i: (i, 0)),
                  pl.BlockSpec((1, MIX_W), lambda i: (0, 0)),
                  pl.BlockSpec((MIX_W, D_MODEL), lambda i: (0, 0)),
                  pl.BlockSpec((TM, D_MODEL), lambda i: (i, 0))],
        out_specs=pl.BlockSpec((TM, D_MODEL), lambda i: (i, 0)),
        out_shape=jax.ShapeDtypeStruct((s, D_MODEL), F32),
        scratch_shapes=[pltpu.VMEM((TM, MIX_W), BF16)],
        compiler_params=_ARB1,
    )(ysb, yfx, ym, gf, onw, wout, x)


def _final_fwd_bwd(x, fnw, target, name):
    s = x.shape[0]

    def body(x_ref, w_ref, t_ref, dx_ref, loss_ref, dw_ref):
        @pl.when(pl.program_id(0) == 0)
        def _():
            loss_ref[...] = jnp.zeros_like(loss_ref)
            dw_ref[...] = jnp.zeros_like(dw_ref)

        xv = x_ref[...]
        w = w_ref[...]
        r = lax.rsqrt(jnp.mean(xv * xv, axis=-1, keepdims=True) + EPS)
        xh = xv * r
        err = xh * w - t_ref[...]
        part = jnp.sum(jnp.sum(err * err, axis=1, keepdims=True), axis=0, keepdims=True)
        loss_ref[...] += part * (0.5 / D_MODEL)
        dy = err * (1.0 / D_MODEL)
        dw_ref[...] += jnp.sum(dy * xh, axis=0, keepdims=True)
        dxh = dy * w
        dx_ref[...] = r * (dxh - xh * jnp.mean(dxh * xh, axis=-1, keepdims=True))

    return pl.pallas_call(
        body, name=name, grid=(s // TM,),
        in_specs=[pl.BlockSpec((TM, D_MODEL), lambda i: (i, 0)),
                  pl.BlockSpec((1, D_MODEL), lambda i: (0, 0)),
                  pl.BlockSpec((TM, D_MODEL), lambda i: (i, 0))],
        out_specs=[pl.BlockSpec((TM, D_MODEL), lambda i: (i, 0)),
                   pl.BlockSpec((1, LANES), lambda i: (0, 0)),
                   pl.BlockSpec((1, D_MODEL), lambda i: (0, 0))],
        out_shape=[jax.ShapeDtypeStruct((s, D_MODEL), F32), jax.ShapeDtypeStruct((1, LANES), F32),
                   jax.ShapeDtypeStruct((1, D_MODEL), F32)],
        compiler_params=_ARB1,
    )(x, fnw, target)


def _outproj_bwd(dxo, wout, ysb, yfx, ym, gf, onw, name):
    s = dxo.shape[0]

    def body(dx_ref, w_ref, ysb_ref, yfx_ref, ym_ref, g_ref, onw_ref,
             dysb_ref, dyfx_ref, dym_ref, dg_ref, dw_ref, donw_ref, yg_ref):
        @pl.when(pl.program_id(0) == 0)
        def _():
            dw_ref[...] = jnp.zeros_like(dw_ref)
            donw_ref[...] = jnp.zeros_like(donw_ref)

        dxb = dx_ref[...].astype(BF16)
        dyg = _dot_nt(dxb, w_ref[...])
        bd = _head_block_diag()
        for c in range(MIX_W // LANES):
            sl = slice(c * LANES, (c + 1) * LANES)
            u = _mix_chunk(c, ysb_ref, yfx_ref, ym_ref)
            r = lax.rsqrt(_head_mean(u * u, bd) + EPS)
            yn = u * r
            g = g_ref[:, sl]
            sg = _sigmoid(g)
            sil = g * sg
            onw = onw_ref[:, sl]
            e = dyg[:, sl]
            yg_ref[:, sl] = (yn * onw * sil).astype(BF16)
            donw_ref[:, sl] += jnp.sum(e * yn * sil, axis=0, keepdims=True)
            dg_ref[:, sl] = (e * yn * onw * (sg * (1.0 + g * (1.0 - sg)))).astype(BF16)
            dyn = e * onw * sil
            du = (r * (dyn - yn * _head_mean(dyn * yn, bd))).astype(BF16)
            if c < 4:
                dysb_ref[:, c * LANES:(c + 1) * LANES] = du
            elif c < 8:
                dyfx_ref[:, (c - 4) * LANES:(c - 3) * LANES] = du
            else:
                dym_ref[:, (c - 8) * LANES:(c - 7) * LANES] = du
        dw_ref[...] += _dot_tn(yg_ref[...], dxb)

    return pl.pallas_call(
        body, name=name, grid=(s // TM,),
        in_specs=[pl.BlockSpec((TM, D_MODEL), lambda i: (i, 0)),
                  pl.BlockSpec((MIX_W, D_MODEL), lambda i: (0, 0)),
                  pl.BlockSpec((TM, SB_W), lambda i: (i, 0)),
                  pl.BlockSpec((TM, FX_W), lambda i: (i, 0)),
                  pl.BlockSpec((TM, MEM_W), lambda i: (i, 0)),
                  pl.BlockSpec((TM, MIX_W), lambda i: (i, 0)),
                  pl.BlockSpec((1, MIX_W), lambda i: (0, 0))],
        out_specs=[pl.BlockSpec((TM, SB_W), lambda i: (i, 0)),
                   pl.BlockSpec((TM, FX_W), lambda i: (i, 0)),
                   pl.BlockSpec((TM, MEM_W), lambda i: (i, 0)),
                   pl.BlockSpec((TM, MIX_W), lambda i: (i, 0)),
                   pl.BlockSpec((MIX_W, D_MODEL), lambda i: (0, 0)),
                   pl.BlockSpec((1, MIX_W), lambda i: (0, 0))],
        out_shape=[jax.ShapeDtypeStruct((s, SB_W), BF16), jax.ShapeDtypeStruct((s, FX_W), BF16),
                   jax.ShapeDtypeStruct((s, MEM_W), BF16), jax.ShapeDtypeStruct((s, MIX_W), BF16),
                   jax.ShapeDtypeStruct((MIX_W, D_MODEL), F32), jax.ShapeDtypeStruct((1, MIX_W), F32)],
        scratch_shapes=[pltpu.VMEM((TM, MIX_W), BF16)],
        compiler_params=_ARB1,
    )(dxo, wout, ysb, yfx, ym, gf, onw)


def _row_dots(do, o, m0):
    prod = do.astype(F32) * o
    zero = jnp.zeros_like(prod)
    return (jnp.sum(jnp.where(m0, prod, zero), axis=1, keepdims=True),
            jnp.sum(jnp.where(m0, zero, prod), axis=1, keepdims=True))


def _sb_bwd(qkv, o, do, name, rider=None):
    s = qkv.shape[0]
    nq = s // T

    def body(q_ref, k_ref, v_ref, o_ref, do_ref, dq_ref, dk_ref, dv_ref,
             dqa_ref, dka_ref, dva_ref, rl_ref, rg_ref, dzs_ref, abs_ref):
        step_id = pl.program_id(1)

        @pl.when(step_id == 0)
        def _():
            dka_ref[...] = jnp.zeros_like(dka_ref)
            dva_ref[...] = jnp.zeros_like(dva_ref)

        m0 = _pair_masks()
        strict = _iota2((T, T), 0) > _iota2((T, T), 1)
        u2 = _stack2(jnp.where(strict, 1.0, 0.0).astype(BF16))
        hs = range(2)

        def query_tile(i, rows):
            qh = _split_pair(q_ref[rows, :] * jnp.asarray(Q_SCALE, BF16), m0)
            doh = _split_pair(do_ref[rows, :], m0)
            dsum = _row_dots(do_ref[rows, :], o_ref[rows, :], m0)
            dqa_ref[...] = jnp.zeros_like(dqa_ref)
            rl_ref[...] = jnp.zeros_like(rl_ref)
            rg_ref[...] = jnp.zeros_like(rg_ref)

            def flush(j):
                off = pl.multiple_of(j * T, T)
                k = k_ref[pl.ds(off, T), :]
                for h in hs:
                    dqa_ref[h] += _dot(dzs_ref[h], k)
                dka_ref[pl.ds(off, T), :] += (_dot_tn(dzs_ref[0], qh[0])
                                              + _dot_tn(dzs_ref[1], qh[1]))
                dva_ref[pl.ds(off, T), :] += (_dot_tn(abs_ref[0], doh[0])
                                              + _dot_tn(abs_ref[1], doh[1]))

            def tile(j, diag):
                off = pl.multiple_of(j * T, T)
                k = k_ref[pl.ds(off, T), :]
                v = v_ref[pl.ds(off, T), :]
                z = [_dot_nt(qh[h], k) for h in hs]
                da = [_dot_nt(doh[h], v) for h in hs]
                if not diag:
                    flush(j + 1)
                la = [jnp.minimum(z[h], 0.0) - jnp.log(1.0 + jnp.exp(-jnp.abs(z[h]))) for h in hs]
                lf = [la[h] - z[h] for h in hs]
                if diag:
                    lf = [jnp.where(strict, lf[h], 0.0) for h in hs]
                cin = [_cum2(lf[h], u2) for h in hs]
                a = [jnp.exp(la[h] + cin[h] + rl_ref[h]) for h in hs]
                if diag:
                    a = [jnp.where(strict, a[h], 0.0) for h in hs]
                ab = [a[h].astype(BF16) for h in hs]
                g = [ab[h].astype(F32) * da[h] for h in hs]
                gin = [_cum2(g[h], u2) for h in hs]
                dz = [g[h] - jnp.exp(la[h]) * ((dsum[h] - rg_ref[h]) - gin[h]) for h in hs]
                if diag:
                    dz = [jnp.where(strict, dz[h], 0.0) for h in hs]
                for h in hs:
                    rl_ref[h] += cin[h][:, 0:1] + lf[h][:, 0:1]
                    rg_ref[h] += gin[h][:, 0:1] + g[h][:, 0:1]
                    dzs_ref[h] = dz[h].astype(BF16)
                    abs_ref[h] = ab[h]

            tile(i, True)

            def more(state):
                j, top = state
                return jnp.logical_and(j >= 0, top > UNDERFLOW)

            def step(state):
                j, _ = state
                tile(j, False)
                return j - 1, _running_top(rl_ref)

            j_left, _ = lax.while_loop(more, step, (i - 1, _running_top(rl_ref)))
            flush(j_left + 1)
            dq_ref[rows, :] = (jnp.where(m0, dqa_ref[0], dqa_ref[1]) * Q_SCALE).astype(BF16)

        for n in range(QPS):
            query_tile(QPS * step_id + n, slice(n * T, (n + 1) * T))

        @pl.when(step_id == nq // QPS - 1)
        def _():
            dk_ref[...] = dka_ref[...].astype(BF16)
            dv_ref[...] = dva_ref[...].astype(BF16)

    nb = SB_W // LANES
    (dq, dk, dv), lands = _ride(dict(
        body=body, name=name, grid=(nb, nq // QPS),
        in_specs=[pl.BlockSpec((QPS * T, LANES), lambda p, i: (i, p)),
                  pl.BlockSpec((s, LANES), lambda p, i: (0, nb + p)),
                  pl.BlockSpec((s, LANES), lambda p, i: (0, 2 * nb + p)),
                  pl.BlockSpec((QPS * T, LANES), lambda p, i: (i, p)),
                  pl.BlockSpec((QPS * T, LANES), lambda p, i: (i, p))],
        out_specs=[pl.BlockSpec((QPS * T, LANES), lambda p, i: (i, p)),
                   pl.BlockSpec((s, LANES), lambda p, i: (0, p)),
                   pl.BlockSpec((s, LANES), lambda p, i: (0, p))],
        out_shape=[jax.ShapeDtypeStruct((s, SB_W), BF16)] * 3,
        scratch_shapes=[pltpu.VMEM((2, T, LANES), F32), pltpu.VMEM((s, LANES), F32),
                        pltpu.VMEM((s, LANES), F32), pltpu.VMEM((2, T, 1), F32),
                        pltpu.VMEM((2, T, 1), F32), pltpu.VMEM((2, T, T), BF16),
                        pltpu.VMEM((2, T, T), BF16)],
        compiler_params=_ARB2, operands=[qkv, qkv, qkv, o, do]), rider)
    return dq, dk, dv, lands


def _fox_bwd(qkv, fqb, frow, fbounds, o, lse, do, name, rider=None):
    s = qkv.shape[0]
    nq = s // T

    def body(q_ref, k_ref, v_ref, fq_ref, fr_ref, fb_ref, o_ref, lse_ref, do_ref,
             dq_ref, dk_ref, dv_ref, df_ref, dqa_ref, dka_ref, dva_ref, dfa_ref, dls_ref, pbs_ref):
        step_id = pl.program_id(1)

        @pl.when(step_id == 0)
        def _():
            dka_ref[...] = jnp.zeros_like(dka_ref)
            dva_ref[...] = jnp.zeros_like(dva_ref)
            dfa_ref[...] = jnp.zeros_like(dfa_ref)

        m0 = _pair_masks()
        causal = _iota2((T, T), 0) >= _iota2((T, T), 1)
        hs = range(2)

        def query_tile(i, rows):
            qh = _split_pair(q_ref[rows, :] * jnp.asarray(Q_SCALE, BF16), m0)
            doh = _split_pair(do_ref[rows, :], m0)
            dsum = _row_dots(do_ref[rows, :], o_ref[rows, :], m0)
            fq = fq_ref[rows, :]
            fqh = (fq[:, 0:1], fq[:, HEAD_DIM:HEAD_DIM + 1])
            lse = lse_ref[rows, :]
            lseh = (lse[:, 0:1], lse[:, HEAD_DIM:HEAD_DIM + 1])
            dqa_ref[...] = jnp.zeros_like(dqa_ref)

            def flush(j):
                off = pl.multiple_of(j * T, T)
                k = k_ref[pl.ds(off, T), :]
                for h in hs:
                    dqa_ref[h] += _dot(dls_ref[h], k)
                dka_ref[pl.ds(off, T), :] += (_dot_tn(dls_ref[0], qh[0])
                                              + _dot_tn(dls_ref[1], qh[1]))
                dva_ref[pl.ds(off, T), :] += (_dot_tn(pbs_ref[0], doh[0])
                                              + _dot_tn(pbs_ref[1], doh[1]))

            def tile(j, diag):
                off = pl.multiple_of(j * T, T)
                k = k_ref[pl.ds(off, T), :]
                v = v_ref[pl.ds(off, T), :]
                sc = [_dot_nt(qh[h], k) + fqh[h] - fr_ref[h:h + 1, pl.ds(off, T)] for h in hs]
                dp = [_dot_nt(doh[h], v) for h in hs]
                if not diag:
                    flush(j + 1)
                p = [jnp.exp(sc[h] - lseh[h]) for h in hs]
                if diag:
                    p = [jnp.where(causal, p[h], 0.0) for h in hs]
                dl = [p[h] * (dp[h] - dsum[h]) for h in hs]
                for h in hs:
                    dls_ref[h] = dl[h].astype(BF16)
                    pbs_ref[h] = p[h].astype(BF16)
                    dfa_ref[h:h + 1, pl.ds(off, T)] -= jnp.sum(dl[h], axis=0, keepdims=True)

            tile(i, True)
            j_left = _fox_tiles_left(i, pl.program_id(0), nq, fb_ref, tile)
            flush(j_left + 1)
            dq_ref[rows, :] = (jnp.where(m0, dqa_ref[0], dqa_ref[1]) * Q_SCALE).astype(BF16)

        for n in range(QPS):
            query_tile(QPS * step_id + n, slice(n * T, (n + 1) * T))

        @pl.when(step_id == nq // QPS - 1)
        def _():
            dk_ref[...] = dka_ref[...].astype(BF16)
            dv_ref[...] = dva_ref[...].astype(BF16)
            df_ref[...] = dfa_ref[...]

    nb = FX_W // LANES
    base = 3 * SB_W // LANES
    (dq, dk, dv, df), lands = _ride(dict(
        body=body, name=name, grid=(nb, nq // QPS),
        in_specs=[pl.BlockSpec((QPS * T, LANES), lambda p, i: (i, base + p)),
                  pl.BlockSpec((s, LANES), lambda p, i: (0, base + nb + p)),
                  pl.BlockSpec((s, LANES), lambda p, i: (0, base + 2 * nb + p)),
                  pl.BlockSpec((QPS * T, LANES), lambda p, i: (i, p)),
                  pl.BlockSpec((None, 2, s), lambda p, i: (p, 0, 0)),
                  pl.BlockSpec(memory_space=pltpu.SMEM),
                  pl.BlockSpec((QPS * T, LANES), lambda p, i: (i, p)),
                  pl.BlockSpec((QPS * T, LANES), lambda p, i: (i, p)),
                  pl.BlockSpec((QPS * T, LANES), lambda p, i: (i, p))],
        out_specs=[pl.BlockSpec((QPS * T, LANES), lambda p, i: (i, p)),
                   pl.BlockSpec((s, LANES), lambda p, i: (0, p)),
                   pl.BlockSpec((s, LANES), lambda p, i: (0, p)),
                   pl.BlockSpec((None, 2, s), lambda p, i: (p, 0, 0))],
        out_shape=[jax.ShapeDtypeStruct((s, FX_W), BF16)] * 3
        + [jax.ShapeDtypeStruct((nb, 2, s), F32)],
        scratch_shapes=[pltpu.VMEM((2, T, LANES), F32), pltpu.VMEM((s, LANES), F32),
                        pltpu.VMEM((s, LANES), F32), pltpu.VMEM((2, s), F32),
                        pltpu.VMEM((2, T, T), BF16), pltpu.VMEM((2, T, T), BF16)],
        compiler_params=_ARB2, operands=[qkv, qkv, qkv, fqb, frow, fbounds, o, lse, do]), rider)
    return dq, dk, dv, df, lands


def _fox_prep_bwd(dfrow, gf, bpad, name):
    s = gf.shape[0]

    def body(df_ref, fl_ref, b_ref, dfl_ref, db_ref):
        tri = jnp.where(_iota2((T, T), 0) <= _iota2((T, T), 1), 1.0, 0.0).astype(BF16)
        carry = jnp.zeros((1, LANES), F32)
        db = jnp.zeros((1, LANES), F32)
        fill = jnp.zeros((LANES - FOX_HEADS, T), F32)
        for blk in reversed(range(s // T)):
            rows = slice(blk * T, (blk + 1) * T)
            c = _tri3(tri, jnp.concatenate([df_ref[:, rows], fill], axis=0), _dot_nt) + carry
            carry = c[0:1, :]
            dfl = c / (1.0 + jnp.exp(fl_ref[rows, :] + b_ref[...]))
            dfl_ref[rows, :] = dfl.astype(BF16)
            db = db + jnp.sum(dfl, axis=0, keepdims=True)
        db_ref[...] = db

    return pl.pallas_call(
        body, name=name, grid=(1,),
        in_specs=[pl.BlockSpec((FOX_HEADS, s), lambda i: (0, 0)),
                  pl.BlockSpec((s, LANES), lambda i: (0, MIX_W // LANES)),
                  pl.BlockSpec((1, LANES), lambda i: (0, 0))],
        out_specs=[pl.BlockSpec((s, LANES), lambda i: (0, 0)),
                   pl.BlockSpec((1, LANES), lambda i: (0, 0))],
        out_shape=[jax.ShapeDtypeStruct((s, LANES), BF16), jax.ShapeDtypeStruct((1, LANES), F32)],
        compiler_params=_ARB1,
    )(dfrow, gf, bpad)


def _mem_bwd(qkv, kv, o, lse, do, name):
    s = qkv.shape[0]
    n = kv.shape[0]

    def body(q_ref, k_ref, v_ref, o_ref, lse_ref, do_ref, dq_ref, dk_ref, dv_ref):
        @pl.when(pl.program_id(1) == 0)
        def _():
            dk_ref[...] = jnp.zeros_like(dk_ref)
            dv_ref[...] = jnp.zeros_like(dv_ref)

        m0 = _pair_masks()
        qh = _split_pair(q_ref[...] * jnp.asarray(Q_SCALE, BF16), m0)
        doh = _split_pair(do_ref[...], m0)
        dsum = _row_dots(do_ref[...], o_ref[...], m0)
        lse = lse_ref[...]
        lseh = (lse[:, 0:1], lse[:, HEAD_DIM:HEAD_DIM + 1])
        k = k_ref[...]
        v = v_ref[...]
        dqs = []
        for h in range(2):
            p = jnp.exp(_dot_nt(qh[h], k) - lseh[h])
            dl = p * (_dot_nt(doh[h], v) - dsum[h])
            dlb = dl.astype(BF16)
            dqs.append(_dot(dlb, k))
            dk_ref[...] += _dot_tn(dlb, qh[h])
            dv_ref[...] += _dot_tn(p.astype(BF16), doh[h])
        dq_ref[...] = (jnp.where(m0, dqs[0], dqs[1]) * Q_SCALE).astype(BF16)

    nb = MEM_W // LANES
    base = (3 * SB_W + 3 * FX_W) // LANES
    return pl.pallas_call(
        body, name=name, grid=(nb, s // TQM),
        in_specs=[pl.BlockSpec((TQM, LANES), lambda p, i: (i, base + p)),
                  pl.BlockSpec((n, LANES), lambda p, i: (0, p)),
                  pl.BlockSpec((n, LANES), lambda p, i: (0, nb + p)),
                  pl.BlockSpec((TQM, LANES), lambda p, i: (i, p)),
                  pl.BlockSpec((TQM, LANES), lambda p, i: (i, p)),
                  pl.BlockSpec((TQM, LANES), lambda p, i: (i, p))],
        out_specs=[pl.BlockSpec((TQM, LANES), lambda p, i: (i, p)),
                   pl.BlockSpec((n, LANES), lambda p, i: (0, p)),
                   pl.BlockSpec((n, LANES), lambda p, i: (0, p))],
        out_shape=[jax.ShapeDtypeStruct((s, MEM_W), BF16), jax.ShapeDtypeStruct((n, MEM_W), F32),
                   jax.ShapeDtypeStruct((n, MEM_W), F32)],
        compiler_params=_ARB2,
    )(qkv, kv, kv, o, lse, do)


def _memkv_bwd(mem, mnw, wkv, dk, dv, name):
    n = mem.shape[0]

    def body(mem_ref, mnw_ref, w_ref, dk_ref, dv_ref, dw_ref, dmnw_ref):
        mv = mem_ref[...]
        r = lax.rsqrt(jnp.mean(mv * mv, axis=-1, keepdims=True) + EPS)
        mh = mv * r
        hm = (mh * mnw_ref[...]).astype(BF16)
        dkv = jnp.concatenate([dk_ref[...], dv_ref[...]], axis=1).astype(BF16)
        dw_ref[...] = _dot_tn(hm, dkv)
        dhm = _dot_nt(dkv, w_ref[...])
        dmnw_ref[...] = jnp.sum(dhm * mh, axis=0, keepdims=True)

    return pl.pallas_call(
        body, name=name, grid=(1,),
        in_specs=[pl.BlockSpec((n, D_MODEL), lambda i: (0, 0)),
                  pl.BlockSpec((1, D_MODEL), lambda i: (0, 0)),
                  pl.BlockSpec((D_MODEL, 2 * MEM_W), lambda i: (0, 0)),
                  pl.BlockSpec((n, MEM_W), lambda i: (0, 0)),
                  pl.BlockSpec((n, MEM_W), lambda i: (0, 0))],
        out_specs=[pl.BlockSpec((D_MODEL, 2 * MEM_W), lambda i: (0, 0)),
                   pl.BlockSpec((1, D_MODEL), lambda i: (0, 0))],
        out_shape=[jax.ShapeDtypeStruct((D_MODEL, 2 * MEM_W), F32),
                   jax.ShapeDtypeStruct((1, D_MODEL), F32)],
        compiler_params=_ARB1,
    )(mem, mnw, wkv, dk, dv)


def _inproj_bwd_dx(pieces, w_r, x, nw, dxo, name, rider=None):
    s = x.shape[0]
    n = len(pieces)
    widths = [p.shape[1] for p in pieces]

    def body(*refs):
        piece_refs = refs[:n]
        w_ref, x_ref, nw_ref, dxo_ref, dx_ref, h_ref, dnw_ref, dp_ref = refs[n:]

        @pl.when(pl.program_id(0) == 0)
        def _():
            dnw_ref[...] = jnp.zeros_like(dnw_ref)

        col = 0
        for r, wd in zip(piece_refs, widths):
            dp_ref[:, col:col + wd] = r[...]
            col += wd
        dp_ref[:, col:] = jnp.zeros((TM, WR_W - col), BF16)
        dh = _dot(dp_ref[...], w_ref[...])
        xv = x_ref[...]
        nw = nw_ref[...]
        r = lax.rsqrt(jnp.mean(xv * xv, axis=-1, keepdims=True) + EPS)
        xh = xv * r
        h_ref[...] = (xh * nw).astype(BF16)
        dnw_ref[...] += jnp.sum(dh * xh, axis=0, keepdims=True)
        dxh = dh * nw
        dx_ref[...] = r * (dxh - xh * jnp.mean(dxh * xh, axis=-1, keepdims=True)) + dxo_ref[...]

    (dx, h, dnw, dproj), lands = _ride(dict(
        body=body, name=name, grid=(s // TM,),
        in_specs=[pl.BlockSpec((TM, wd), lambda i: (i, 0)) for wd in widths]
        + [pl.BlockSpec((WR_W, D_MODEL), lambda i: (0, 0)),
           pl.BlockSpec((TM, D_MODEL), lambda i: (i, 0)),
           pl.BlockSpec((1, D_MODEL), lambda i: (0, 0)),
           pl.BlockSpec((TM, D_MODEL), lambda i: (i, 0))],
        out_specs=[pl.BlockSpec((TM, D_MODEL), lambda i: (i, 0)),
                   pl.BlockSpec((TM, D_MODEL), lambda i: (i, 0)),
                   pl.BlockSpec((1, D_MODEL), lambda i: (0, 0)),
                   pl.BlockSpec((TM, WR_W), lambda i: (i, 0))],
        out_shape=[jax.ShapeDtypeStruct((s, D_MODEL), F32), jax.ShapeDtypeStruct((s, D_MODEL), BF16),
                   jax.ShapeDtypeStruct((1, D_MODEL), F32), jax.ShapeDtypeStruct((s, WR_W), BF16)],
        scratch_shapes=[], compiler_params=_ARB1, operands=[*pieces, w_r, x, nw, dxo]), rider)
    return dx, h, dnw, dproj, lands


def _inproj_bwd_dw(h, dproj, name):
    s = dproj.shape[0]
    tn = 256

    def body(h_ref, dp_ref, dw_ref):
        dw_ref[...] = _dot_tn(dp_ref[...], h_ref[...])

    return pl.pallas_call(
        body, name=name, grid=(WR_W // tn,),
        in_specs=[pl.BlockSpec((s, D_MODEL), lambda j: (0, 0)),
                  pl.BlockSpec((s, tn), lambda j: (0, j))],
        out_specs=pl.BlockSpec((tn, D_MODEL), lambda j: (j, 0)),
        out_shape=jax.ShapeDtypeStruct((WR_W, D_MODEL), F32),
        compiler_params=_ARB1,
    )(h, dproj)


def _rearrange_w_in(wt):
    pad = jnp.zeros((FL_PAD - FOX_HEADS,) + wt.shape[1:], wt.dtype)
    return jnp.concatenate([wt[:3072], wt[3080:3336], wt[3336:IN_W], wt[3072:3080], pad], axis=0)


def _restore_w_in(g):
    gate0 = QKV_W
    fl0 = QKV_W + MIX_W
    return jnp.concatenate(
        [g[:3072], g[fl0:fl0 + FOX_HEADS], g[3072:QKV_W], g[gate0:fl0]], axis=0)


def _pad_lanes(v, width=LANES):
    return jnp.pad(v, (0, width - v.shape[0])).reshape(1, width)


def _layer_fwd(xs, mem, nw, w_r, b_forget, mnw, late, onw, l, travel=None):
    s = xs.shape[0]
    bpad = _pad_lanes(b_forget)
    travel = _Travel(travel)
    qkv, gf, _ = travel.ride(2, _inproj_fwd, xs, nw, w_r, f"inproj_fwd_{l}")
    fqb, frow, fbounds = _fox_prep_fwd(gf, qkv, bpad, f"fox_prep_fwd_{l}")
    frow = frow.reshape(FOX_HEADS // 2, 2, s)
    ysb, _ = travel.ride(0, _sb_fwd, qkv, f"sb_fwd_{l}")
    yfx, lse_fx, _ = travel.ride(1, _fox_fwd, qkv, fqb, frow, fbounds, f"fox_fwd_{l}")
    wkv, wout = late(travel.lands)
    kv = _memkv_fwd(mem, mnw, wkv, f"memkv_fwd_{l}")
    ym, lse_m = _mem_fwd(qkv, kv, f"mem_fwd_{l}")
    xn = _outproj_fwd(ysb, yfx, ym, gf, onw, wout, xs, f"outproj_fwd_{l}")
    saved = (xs, nw, mnw, onw, bpad, qkv, gf, fqb, frow, fbounds, ysb, yfx, lse_fx, kv, ym, lse_m)
    return xn, saved, travel.lands, (wkv, wout)


class _Travel:
    def __init__(self, plan):
        self.plan = plan
        self.lands = None if plan is None else _new_lands(plan[0], plan[1])

    def ride(self, n, fn, *args):
        if self.plan is None or n >= len(self.plan[2]) or self.plan[2][n] is None:
            return fn(*args)
        srcs, scatter, legs = self.plan
        idx, rows = legs[n]
        out = fn(*args, rider=_Rider([srcs[a] for a in idx], [self.lands[a] for a in idx],
                                     scatter, rows, relay=not scatter))
        for a, land in zip(idx, out[-1]):
            self.lands[a] = land
        return out


def _layer_bwd(dx, saved, mem, w_r, wkv, wout, l, travel=None):
    xs, nw, mnw, onw, bpad, qkv, gf, fqb, frow, fbounds, ysb, yfx, lse_fx, kv, ym, lse_m = saved
    s = xs.shape[0]
    dysb, dyfx, dym, dgate, dwout, donw = _outproj_bwd(
        dx, wout, ysb, yfx, ym, gf, onw, f"outproj_bwd_{l}")
    travel = _Travel(None if travel is None else travel(dwout))
    sdq, sdk, sdv, _ = travel.ride(0, _sb_bwd, qkv, ysb, dysb, f"sb_bwd_{l}")
    fdq, fdk, fdv, dfrow, _ = travel.ride(1, _fox_bwd, qkv, fqb, frow, fbounds, yfx, lse_fx, dyfx,
                                          f"fox_bwd_{l}")
    dfl, db = _fox_prep_bwd(dfrow.reshape(FOX_HEADS, s), gf, bpad, f"fox_prep_bwd_{l}")
    dmq, dmk, dmv = _mem_bwd(qkv, kv, ym, lse_m, dym, f"mem_bwd_{l}")
    dwkv, dmnw = _memkv_bwd(mem, mnw, wkv, dmk, dmv, f"memkv_bwd_{l}")
    dx, ht, dnw, dproj, _ = travel.ride(2, _inproj_bwd_dx,
                                        [sdq, sdk, sdv, fdq, fdk, fdv, dmq, dgate, dfl],
                                        w_r, xs, nw, dx, f"inproj_bwd_dx_{l}")
    dwr = _inproj_bwd_dw(ht, dproj, f"inproj_bwd_dw_{l}")
    grads = dict(norm_w=dnw[0], w_r=dwr, b_forget=db[0, :FOX_HEADS], mem_norm_w=dmnw[0],
                 w_mem_kv=dwkv, out_norm_w=donw[0], w_out=dwout)
    return dx, grads, travel.lands


_ANY = pl.BlockSpec(memory_space=pl.ANY)


def _my_place():
    return lax.axis_index("x"), lax.axis_index("y"), lax.axis_index("c")


def _flip(v, bit):
    return 1 - v if bit else v


def _block_index(px, py, pc):
    return 4 * px + 2 * py + pc


def _all_gather_weights(shards, name):
    n = len(shards)

    def body(*refs):
        ins, outs = refs[:n], refs[n:2 * n]
        send_sems, recv_sems, local_sems = refs[2 * n:]
        x, y, c = _my_place()
        me = (x, y, c)
        sibling = (x, y, 1 - c)
        chips = [(1 - x, y), (x, 1 - y), (1 - x, 1 - y)]

        def copy(a, k, block, to, src=None):
            dst = outs[a].at[_block_index(*block)]
            return pltpu.make_async_remote_copy(
                src_ref=dst if src is None else src, dst_ref=dst,
                send_sem=send_sems.at[a, k], recv_sem=recv_sems.at[a, k],
                device_id=to, device_id_type=pl.DeviceIdType.MESH)

        mine = [pltpu.make_async_copy(ins[a], outs[a].at[_block_index(*me)], local_sems.at[a])
                for a in range(n)]
        for cp in mine:
            cp.start()
        first = []
        for a in range(n):
            first.append(copy(a, 0, me, sibling, src=ins[a]))
            first += [copy(a, 1 + j, me, (*chip, c), src=ins[a]) for j, chip in enumerate(chips)]
        for cp in first:
            cp.start()
        passed = []
        for j, chip in enumerate(chips):
            for a in range(n):
                copy(a, 1 + j, (*chip, c), me).wait_recv()
                fwd = copy(a, 4 + j, (*chip, c), sibling)
                fwd.start()
                passed.append(fwd)
        for a in range(n):
            copy(a, 0, sibling, me).wait_recv()
            for j, chip in enumerate(chips):
                copy(a, 4 + j, (*chip, 1 - c), me).wait_recv()
        for cp in first + passed:
            cp.wait_send()
        for cp in mine:
            cp.wait()

    return pl.pallas_call(
        body, name=name,
        in_specs=[_ANY] * n, out_specs=[_ANY] * n,
        out_shape=[jax.ShapeDtypeStruct((N_DEV,) + v.shape, v.dtype) for v in shards],
        scratch_shapes=[pltpu.SemaphoreType.DMA((n, 7)), pltpu.SemaphoreType.DMA((n, 7)),
                        pltpu.SemaphoreType.DMA((n,))],
    )(*shards)


def _exchange_blocks(blocked, name):
    n = len(blocked)

    def body(*refs):
        ins, outs = refs[:n], refs[n:2 * n]
        send_sems, recv_sems, local_sems = refs[2 * n:]
        x, y, c = _my_place()
        mine_idx = _block_index(x, y, c)
        local = [pltpu.make_async_copy(ins[a].at[mine_idx], outs[a].at[mine_idx], local_sems.at[a])
                 for a in range(n)]
        for cp in local:
            cp.start()
        sends, arrivals = [], []
        for r in range(1, N_DEV):
            peer = (_flip(x, r & 4), _flip(y, r & 2), _flip(c, r & 1))
            peer_idx = _block_index(*peer)
            for a in range(n):
                sems = dict(send_sem=send_sems.at[a, r - 1], recv_sem=recv_sems.at[a, r - 1],
                            device_id=peer, device_id_type=pl.DeviceIdType.MESH)
                sends.append(pltpu.make_async_remote_copy(
                    src_ref=ins[a].at[peer_idx], dst_ref=outs[a].at[mine_idx], **sems))
                arrivals.append(pltpu.make_async_remote_copy(
                    src_ref=ins[a].at[peer_idx], dst_ref=outs[a].at[peer_idx], **sems))
        for cp in sends:
            cp.start()
        for cp in arrivals:
            cp.wait_recv()
        for cp in sends:
            cp.wait_send()
        for cp in local:
            cp.wait()

    return pl.pallas_call(
        body, name=name,
        in_specs=[_ANY] * n, out_specs=[_ANY] * n,
        out_shape=[jax.ShapeDtypeStruct(v.shape, v.dtype) for v in blocked],
        scratch_shapes=[pltpu.SemaphoreType.DMA((n, 7)), pltpu.SemaphoreType.DMA((n, 7)),
                        pltpu.SemaphoreType.DMA((n,))],
    )(*blocked)


N_CHIP = N_DEV // 2


def _pair_swap(blocked, name):
    n = len(blocked)

    def body(*refs):
        ins, outs = refs[:n], refs[n:2 * n]
        send_sems, recv_sems = refs[2 * n:]
        x, y, c = _my_place()
        copies = [pltpu.make_async_remote_copy(
            src_ref=ins[a].at[j, 1 - c], dst_ref=outs[a].at[j],
            send_sem=send_sems.at[N_CHIP * a + j], recv_sem=recv_sems.at[N_CHIP * a + j],
            device_id=(x, y, 1 - c), device_id_type=pl.DeviceIdType.MESH)
            for a in range(n) for j in range(N_CHIP)]
        for cp in copies:
            cp.start()
        for cp in copies:
            cp.wait_recv()
        for cp in copies:
            cp.wait_send()

    return pl.pallas_call(
        body, name=name,
        in_specs=[_ANY] * n, out_specs=[_ANY] * n,
        out_shape=[jax.ShapeDtypeStruct((N_CHIP,) + v.shape[2:], v.dtype) for v in blocked],
        scratch_shapes=[pltpu.SemaphoreType.DMA((N_CHIP * n,)),
                        pltpu.SemaphoreType.DMA((N_CHIP * n,))],
    )(*blocked)


def _pair_add(by_core, theirs, core, name):
    n = len(by_core)

    def body(core_ref, *refs):
        for a in range(n):
            mine, got, out = refs[a], refs[n + a], refs[2 * n + a]
            out[...] = (mine[...].astype(F32) + got[...].astype(F32)).astype(BF16)

    def own(v):
        return pl.BlockSpec((None, None) + v.shape[2:], lambda j, core_ref: (j, core_ref[0], 0, 0))

    def block(v):
        return pl.BlockSpec((None,) + v.shape[1:], lambda j, core_ref: (j, 0, 0))

    return list(pl.pallas_call(
        body, name=name,
        grid_spec=pltpu.PrefetchScalarGridSpec(
            num_scalar_prefetch=1, grid=(N_CHIP,),
            in_specs=[own(v) for v in by_core] + [block(v) for v in theirs],
            out_specs=[block(v) for v in theirs]),
        out_shape=[jax.ShapeDtypeStruct(v.shape, BF16) for v in theirs],
        compiler_params=_ARB1,
    )(core, *by_core, *theirs))


def _chip_exchange(by_chip, to_all, name):
    n, m = len(by_chip), len(to_all)

    def body(*refs):
        ins, alls = refs[:n], refs[n:n + m]
        outs, all_outs = refs[n + m:2 * n + m], refs[2 * n + m:2 * (n + m)]
        send_sems, recv_sems, local_sems = refs[2 * (n + m):]
        x, y, c = _my_place()
        my_chip, mine_idx = 2 * x + y, _block_index(x, y, c)
        local = [pltpu.make_async_copy(ins[a].at[my_chip], outs[a].at[my_chip], local_sems.at[a])
                 for a in range(n)]
        local += [pltpu.make_async_copy(alls[b].at[mine_idx], all_outs[b].at[mine_idx],
                                        local_sems.at[n + b]) for b in range(m)]
        for cp in local:
            cp.start()
        sends, arrivals = [], []
        k = 0
        for r in range(1, N_DEV):
            peer = (_flip(x, r & 4), _flip(y, r & 2), _flip(c, r & 1))
            peer_chip, peer_idx = 2 * peer[0] + peer[1], _block_index(*peer)
            pairs = [(alls[b].at[mine_idx], all_outs[b].at[mine_idx], all_outs[b].at[peer_idx])
                     for b in range(m)]
            if not r & 1:
                pairs += [(ins[a].at[peer_chip], outs[a].at[my_chip], outs[a].at[peer_chip])
                          for a in range(n)]
            for src, there, here in pairs:
                sems = dict(send_sem=send_sems.at[k], recv_sem=recv_sems.at[k], device_id=peer,
                            device_id_type=pl.DeviceIdType.MESH)
                sends.append(pltpu.make_async_remote_copy(src_ref=src, dst_ref=there, **sems))
                arrivals.append(pltpu.make_async_remote_copy(src_ref=src, dst_ref=here, **sems))
                k += 1
        for cp in sends:
            cp.start()
        for cp in arrivals:
            cp.wait_recv()
        for cp in sends:
            cp.wait_send()
        for cp in local:
            cp.wait()

    n_copies = 7 * m + 3 * n
    return pl.pallas_call(
        body, name=name,
        in_specs=[_ANY] * (n + m), out_specs=[_ANY] * (n + m),
        out_shape=[jax.ShapeDtypeStruct(v.shape, v.dtype) for v in by_chip + to_all],
        scratch_shapes=[pltpu.SemaphoreType.DMA((n_copies,)), pltpu.SemaphoreType.DMA((n_copies,)),
                        pltpu.SemaphoreType.DMA((n + m,))],
    )(*by_chip, *to_all)


class _Rider(NamedTuple):
    srcs: list
    lands: list
    scatter: bool
    part: list
    relay: bool = False


def _window(ref, part):
    if part is None:
        return ref
    dim, start, size = part
    return ref.at[(slice(None),) * dim + (pl.ds(start, size),)]


def _relay_copies(srcs, lands, send_sems, recv_sems, rider):
    x, y, c = _my_place()
    me, sibling = (x, y, c), (x, y, 1 - c)
    chips = [(1 - x, y), (x, 1 - y), (1 - x, 1 - y)]
    first, from_chips, passed, last = [], [], [], []
    for a in range(len(srcs)):
        def copy(k, block, to, src=None, a=a):
            dst = _window(lands[a].at[_block_index(*block)], rider.part[a])
            return pltpu.make_async_remote_copy(
                src_ref=dst if src is None else src, dst_ref=dst,
                send_sem=send_sems.at[7 * a + k], recv_sem=recv_sems.at[7 * a + k],
                device_id=to, device_id_type=pl.DeviceIdType.MESH)

        mine = _window(srcs[a], rider.part[a])
        first.append(copy(0, me, sibling, src=mine))
        first += [copy(1 + j, me, (*chip, c), src=mine) for j, chip in enumerate(chips)]
        from_chips += [copy(1 + j, (*chip, c), me) for j, chip in enumerate(chips)]
        passed += [copy(4 + j, (*chip, c), sibling) for j, chip in enumerate(chips)]
        last.append(copy(0, sibling, me))
        last += [copy(4 + j, (*chip, 1 - c), me) for j, chip in enumerate(chips)]
    return first, from_chips, passed, last


def _new_lands(srcs, scatter):
    return [lax.empty(v.shape if scatter else (N_DEV,) + v.shape, v.dtype) for v in srcs]


def _rider_copies(srcs, lands, send_sems, recv_sems, rider):
    x, y, c = _my_place()
    mine_idx = _block_index(x, y, c)

    def window(ref, a):
        return _window(ref, rider.part[a])

    sends, arrivals = [], []
    for r in range(1, N_DEV):
        peer = (_flip(x, r & 4), _flip(y, r & 2), _flip(c, r & 1))
        peer_idx = _block_index(*peer)
        for a in range(len(srcs)):
            src = window(srcs[a].at[peer_idx] if rider.scatter else srcs[a], a)
            k = 7 * a + r - 1
            sems = dict(send_sem=send_sems.at[k], recv_sem=recv_sems.at[k],
                        device_id=peer, device_id_type=pl.DeviceIdType.MESH)
            sends.append(pltpu.make_async_remote_copy(
                src_ref=src, dst_ref=window(lands[a].at[mine_idx], a), **sems))
            arrivals.append(pltpu.make_async_remote_copy(
                src_ref=src, dst_ref=window(lands[a].at[peer_idx], a), **sems))
    return sends, arrivals


def _ride(call, rider):
    call = dict(call)
    body, grid = call.pop("body"), call["grid"]
    operands = call.pop("operands")
    if rider is None:
        return list(pl.pallas_call(body, **call)(*operands)), None
    n_in, n_out = len(call["in_specs"]), len(call["out_specs"])
    n_scratch = len(call["scratch_shapes"])
    m = len(rider.srcs)

    def riding(*refs):
        main_in, srcs, lands = refs[:n_in], refs[n_in:n_in + m], refs[n_in + m:n_in + 2 * m]
        main_out = refs[n_in + 2 * m:n_in + 2 * m + n_out]
        rest = refs[n_in + 3 * m + n_out:]
        send_sems, recv_sems = rest[n_scratch:]
        at = [pl.program_id(d) for d in range(len(grid))]
        first = functools.reduce(jnp.logical_and, [p == 0 for p in at])
        last = functools.reduce(jnp.logical_and, [p == g - 1 for p, g in zip(at, grid)])
        if rider.relay:
            sends, from_chips, passed, arrivals = _relay_copies(
                srcs, lands, send_sems, recv_sems, rider)
            step, steps = 0, 1
            for p, g in zip(at, grid):
                step, steps = step * g + p, steps * g
            assert steps >= 2, "a relayed gather needs a later grid step to pass blocks on"

            @pl.when(step == (3 * steps) // 4)
            def _():
                for cp in from_chips:
                    cp.wait_recv()
                for cp in passed:
                    cp.start()
        else:
            sends, arrivals = _rider_copies(srcs, lands, send_sems, recv_sems, rider)
            passed = []

        @pl.when(first)
        def _():
            for cp in sends:
                cp.start()

        body(*main_in, *main_out, *rest[:n_scratch])

        @pl.when(last)
        def _():
            for cp in arrivals:
                cp.wait_recv()
            for cp in sends + passed:
                cp.wait_send()

    call["in_specs"] = list(call["in_specs"]) + [_ANY] * (2 * m)
    call["out_specs"] = list(call["out_specs"]) + [_ANY] * m
    call["out_shape"] = list(call["out_shape"]) + [
        jax.ShapeDtypeStruct(v.shape, v.dtype) for v in rider.lands]
    call["scratch_shapes"] = list(call["scratch_shapes"]) + [
        pltpu.SemaphoreType.DMA((7 * m,)), pltpu.SemaphoreType.DMA((7 * m,))]
    call["input_output_aliases"] = {n_in + m + a: n_out + a for a in range(m)}
    outs = pl.pallas_call(riding, **call)(*operands, *rider.srcs, *rider.lands)
    return list(outs[:n_out]), list(outs[n_out:])


def _sum_parts(p_ref):
    g = p_ref[0].astype(F32)
    for k in range(1, p_ref.shape[0]):
        g = g + p_ref[k].astype(F32)
    return g


def _adamw(g, w, m, v):
    c1 = 1.0 / (1.0 - ADAM_B1 ** ADAM_STEP)
    c2 = 1.0 / (1.0 - ADAM_B2 ** ADAM_STEP)
    nm = ADAM_B1 * m + (1.0 - ADAM_B1) * g
    nv = ADAM_B2 * v + (1.0 - ADAM_B2) * (g * g)
    return nm, nv, -ADAM_LR * ((nm * c1) / (jnp.sqrt(nv * c2) + ADAM_EPS) + ADAM_WD * w)


def _adamw_w_in(parts, w, m, v, name):
    ncol_blk, depth, nfeat = w.shape
    cols = 256

    def body(*refs):
        p_refs = refs[:depth]
        w_ref, m_ref, v_ref, g_ref, d_ref, nm_ref, nv_ref = refs[depth:]
        for l in range(depth):
            g = _sum_parts(p_refs[l])
            nm, nv, d = _adamw(g, w_ref[:, l, :], m_ref[:, l, :], v_ref[:, l, :])
            g_ref[:, l, :] = g
            nm_ref[:, l, :] = nm
            nv_ref[:, l, :] = nv
            d_ref[:, l, :] = d

    blk = pl.BlockSpec((ncol_blk, depth, cols), lambda j: (0, 0, j))
    return pl.pallas_call(
        body, name=name, grid=(nfeat // cols,),
        in_specs=[pl.BlockSpec((p.shape[0], ncol_blk, cols), lambda j: (0, 0, j)) for p in parts]
        + [blk] * 3,
        out_specs=[blk] * 4,
        out_shape=[jax.ShapeDtypeStruct(w.shape, F32)] * 4,
        compiler_params=_ARB1,
    )(*parts, w, m, v)


def _adamw_sum(parts, w, m, v, tile, name):
    depth, nrow, ncol = w.shape
    rows, cols = tile

    def body(*refs):
        p_refs = refs[:depth]
        w_ref, m_ref, v_ref, g_ref, d_ref, nm_ref, nv_ref = refs[depth:]
        layer = pl.program_id(0)
        for l in range(depth):
            @pl.when(layer == l)
            def _(p_ref=p_refs[l]):
                g = _sum_parts(p_ref)
                nm, nv, d = _adamw(g, w_ref[...], m_ref[...], v_ref[...])
                g_ref[...] = g
                nm_ref[...] = nm
                nv_ref[...] = nv
                d_ref[...] = d

    def part_spec(l):
        return pl.BlockSpec((parts[l].shape[0], rows, cols), lambda q, i, j: (
            0, jnp.where(q == l, i, 0), jnp.where(q == l, j, 0)))

    blk = pl.BlockSpec((None, rows, cols), lambda q, i, j: (q, i, j))
    return pl.pallas_call(
        body, name=name, grid=(depth, nrow // rows, ncol // cols),
        in_specs=[part_spec(l) for l in range(depth)] + [blk, blk, blk],
        out_specs=[blk] * 4,
        out_shape=[jax.ShapeDtypeStruct(w.shape, F32)] * 4,
        compiler_params=pltpu.CompilerParams(
            dimension_semantics=("arbitrary", "arbitrary", "arbitrary")),
    )(*parts, w, m, v)


def _adamw_small(parts, ws, ms, vs, name):
    n = len(parts)

    def body(*refs):
        p, w, m, v = (refs[a * n:(a + 1) * n] for a in range(4))
        out = refs[4 * n:]
        for a in range(n):
            g = _sum_parts(p[a])
            nm, nv, d = _adamw(g, w[a][...], m[a][...], v[a][...])
            out[a][...] = g
            out[n + a][...] = d
            out[2 * n + a][...] = nm
            out[3 * n + a][...] = nv

    vmem = pl.BlockSpec(memory_space=pltpu.VMEM)
    outs = pl.pallas_call(
        body, name=name, in_specs=[vmem] * (4 * n), out_specs=[vmem] * (4 * n),
        out_shape=[jax.ShapeDtypeStruct(w.shape, F32) for w in ws] * 4,
    )(*parts, *ws, *ms, *vs)
    return [list(outs[a * n:(a + 1) * n]) for a in range(4)]


def _misc_rows(b_forget, extra=None):
    tile = jnp.pad(b_forget, ((0, 6), (0, LANES - FOX_HEADS)))
    return tile if extra is None else tile.at[2].set(extra)


def kernel(x, mem, norm_w, w_in, b_forget, mem_norm_w, w_mem_kv, out_norm_w, w_out, final_norm_w, loss_target, m_norm_w, m_w_in, m_b_forget, m_mem_norm_w, m_w_mem_kv, m_out_norm_w, m_w_out, m_final_norm_w, v_norm_w, v_w_in, v_b_forget, v_mem_norm_w, v_w_mem_kv, v_out_norm_w, v_w_out, v_final_norm_w):
    kv_rows = w_mem_kv.shape[1]
    out_rows = w_out.shape[1]
    me = _block_index(*_my_place())

    def shards(l):
        return [w_in[l].T.astype(BF16), w_mem_kv[l].astype(BF16), w_out[l].astype(BF16)]

    def full_in(g_in):
        return _rearrange_w_in(g_in.reshape(IN_W, D_MODEL))

    def full_kv_out(g_kv, g_out):
        return g_kv.reshape(D_MODEL, 2 * MEM_W), g_out.reshape(MIX_W, D_MODEL)

    def in_blocks(g):
        segments = [(0, 3072, 0), (3072, 3080, QKV_W + MIX_W), (3080, 3336, 3072),
                    (3336, IN_W, QKV_W)]

        def block(k):
            lo, hi = k * SHARD_W, (k + 1) * SHARD_W
            pieces = [g[at + max(lo, a) - a:at + min(hi, b) - a]
                      for a, b, at in segments if max(lo, a) < min(hi, b)]
            return jnp.concatenate(pieces, axis=0).astype(BF16)

        return jnp.stack([block(k) for k in range(N_DEV)])

    def kv_blocks(g):
        return g.reshape(N_DEV, kv_rows, 2 * MEM_W).astype(BF16)

    def out_blocks(g):
        return g.reshape(N_DEV, out_rows, D_MODEL).astype(BF16)

    def with_own(land, own):
        return lax.dynamic_update_slice(land, own[None], (me,) + (0,) * own.ndim)

    def with_own_of(land, blocked):
        return lax.dynamic_update_slice(land, lax.dynamic_slice_in_dim(blocked, me, 1, axis=0),
                                        (me,) + (0,) * (land.ndim - 1))

    def row(v):
        return v.reshape(1, -1)

    def cols(first, size):
        return (1, first, size)

    fwd_split, bwd_split = 4 * LANES, (5 * LANES, 6 * LANES)

    s_in0, s_kv0, s_out0 = shards(0)
    (g_in0,) = _all_gather_weights([s_in0], "all_gather_l0")
    w_r0 = full_in(g_in0)
    s_in1, s_kv1, s_out1 = shards(1)
    x1, saved0, (l_in1, _, _), (wkv0, wout0) = _layer_fwd(
        x[0], mem[0], row(norm_w[0]), w_r0, b_forget[0], row(mem_norm_w[0]),
        lambda lands: full_kv_out(with_own(lands[1], s_kv0), with_own(lands[2], s_out0)),
        row(out_norm_w[0]), 0,
        travel=([s_in1, s_kv0, s_out0], False,
                [([0], [cols(0, fwd_split)]), ([0], [cols(fwd_split, D_MODEL - fwd_split)]),
                 ([1, 2], [None, None])]))
    w_r1 = full_in(with_own(l_in1, s_in1))
    x2, saved1, _, (wkv1, wout1) = _layer_fwd(
        x1, mem[0], row(norm_w[1]), w_r1, b_forget[1], row(mem_norm_w[1]),
        lambda lands: full_kv_out(with_own(lands[0], s_kv1), with_own(lands[1], s_out1)),
        row(out_norm_w[1]), 1, travel=([s_kv1, s_out1], False, [([0, 1], [None, None]), None]))

    dx2, loss_part, dfnw = _final_fwd_bwd(x2, row(final_norm_w), loss_target[0], "final_fwd_bwd")

    dx1, gr1, (l_out1,) = _layer_bwd(
        dx2, saved1, mem[0], w_r1, wkv1, wout1, 1,
        travel=lambda dwout: ([out_blocks(dwout)], True, [([0], [None]), None]))
    p_in1, p_kv1 = in_blocks(gr1["w_r"]), kv_blocks(gr1["w_mem_kv"])
    grad_x, gr0, (l_in1, l_kv1, l_out0) = _layer_bwd(
        dx1, saved0, mem[0], w_r0, wkv0, wout0, 0,
        travel=lambda dwout: ([p_in1, p_kv1, out_blocks(dwout)], True,
                              [([0, 1], [cols(0, bwd_split[0]), None]),
                               ([0, 2], [cols(bwd_split[0], bwd_split[1] - bwd_split[0]), None]),
                               ([0], [cols(bwd_split[1], D_MODEL - bwd_split[1])])]))
    r_out1 = with_own_of(l_out1, out_blocks(gr1["w_out"]))
    r_in1, r_kv1 = with_own_of(l_in1, p_in1), with_own_of(l_kv1, p_kv1)
    r_out0 = with_own_of(l_out0, out_blocks(gr0["w_out"]))

    def both(name):
        return jnp.stack([gr0[name], gr1[name]])

    small = [both("norm_w"), both("mem_norm_w"), both("out_norm_w"), dfnw,
             _misc_rows(both("b_forget"), loss_part[0])]
    p_small = [jnp.broadcast_to(v[None], (N_DEV,) + v.shape) for v in small]
    by_core = [v.reshape((N_CHIP, 2) + v.shape[1:])
               for v in (in_blocks(gr0["w_r"]), kv_blocks(gr0["w_mem_kv"]))]
    from_sibling = _pair_swap(by_core, "grads_l0_pair_swap")
    core = lax.axis_index("c").astype(jnp.int32).reshape(1)
    chip_sums = _pair_add(by_core, from_sibling, core, "grads_l0_pair_add")
    r_in0, r_kv0, *r_small = _chip_exchange(chip_sums, p_small, "exchange_grads_l0")

    def view(v):
        return jnp.transpose(v, (2, 0, 1))

    g_w_in, d_w_in, nm_w_in, nv_w_in = [jnp.transpose(v, (1, 2, 0)) for v in _adamw_w_in(
        [r_in0, r_in1], view(w_in), view(m_w_in), view(v_w_in), "adamw_w_in")]
    g_w_kv, d_w_kv, nm_w_kv, nv_w_kv = _adamw_sum(
        [r_kv0, r_kv1], w_mem_kv, m_w_mem_kv, v_w_mem_kv, (kv_rows, 2 * MEM_W), "adamw_w_mem_kv")
    g_w_out, d_w_out, nm_w_out, nv_w_out = _adamw_sum(
        [r_out0, r_out1], w_out, m_w_out, v_w_out, (out_rows, D_MODEL), "adamw_w_out")
    def smalls(nw, mnw, onw, fnw, b):
        return [nw, mnw, onw, row(fnw), _misc_rows(b)]

    small_out = _adamw_small(
        r_small, smalls(norm_w, mem_norm_w, out_norm_w, final_norm_w, b_forget),
        smalls(m_norm_w, m_mem_norm_w, m_out_norm_w, m_final_norm_w, m_b_forget),
        smalls(v_norm_w, v_mem_norm_w, v_out_norm_w, v_final_norm_w, v_b_forget), "adamw_small")
    (g_nw, g_mnw, g_onw, g_fnw, g_b), (d_nw, d_mnw, d_onw, d_fnw, d_b), \
        (nm_nw, nm_mnw, nm_onw, nm_fnw, nm_b), (nv_nw, nv_mnw, nv_onw, nv_fnw, nv_b) = [
            (nw, mnw, onw, fnw[0], misc[:2, :FOX_HEADS]) for nw, mnw, onw, fnw, misc in small_out]
    loss = small_out[0][4][2, 0]

    return (loss, grad_x[None],
            g_nw, g_w_in, g_b, g_mnw, g_w_kv, g_onw, g_w_out, g_fnw,
            d_nw, d_w_in, d_b, d_mnw, d_w_kv, d_onw, d_w_out, d_fnw,
            nm_nw, nm_w_in, nm_b, nm_mnw, nm_w_kv, nm_onw, nm_w_out, nm_fnw,
            nv_nw, nv_w_in, nv_b, nv_mnw, nv_w_kv, nv_onw, nv_w_out, nv_fnw)
```

```python
import functools
from typing import NamedTuple

import jax
import jax.numpy as jnp
from jax import lax
from jax.experimental import pallas as pl
from jax.experimental.pallas import tpu as pltpu

F32 = jnp.float32
BF16 = jnp.bfloat16

N_DEV = 8
D_MODEL = 1024
HEAD_DIM = 64
LANES = 128
SB_W = 512
FX_W = 512
MEM_W = 256
MIX_W = 1280
FOX_HEADS = 8
IN_W = 4616
SHARD_W = IN_W // N_DEV
QKV_W = 3 * SB_W + 3 * FX_W + MEM_W
FL_PAD = 256
GF_W = MIX_W + FL_PAD
WR_W = QKV_W + GF_W
EPS = 1e-6
T = 256
QPS = 4
TM = 256
TQM = 512
Q_SCALE = 0.125
NEG = -1e30
UNDERFLOW = -110.0
NORM_SLACK = 1.01

ADAM_LR = 0.001
ADAM_B1 = 0.9
ADAM_B2 = 0.999
ADAM_EPS = 1e-08
ADAM_WD = 0.01
ADAM_STEP = 10


_NT = (((1,), (1,)), ((), ()))
_TN = (((0,), (0,)), ((), ()))

_ARB1 = pltpu.CompilerParams(dimension_semantics=("arbitrary",))
_ARB2 = pltpu.CompilerParams(dimension_semantics=("arbitrary", "arbitrary"))


def _dot(a, b):
    return jnp.dot(a, b, preferred_element_type=F32)


def _dot_nt(a, b):
    return lax.dot_general(a, b, _NT, preferred_element_type=F32)


def _dot_tn(a, b):
    return lax.dot_general(a, b, _TN, preferred_element_type=F32)


def _split2(x):
    hi = x.astype(BF16)
    lo = (x - hi.astype(F32)).astype(BF16)
    return hi, lo


def _stack2(u):
    return jnp.concatenate([u, u], axis=0)


def _cum2(x, u2):
    hi, lo = _split2(x)
    return _dot(jnp.concatenate([hi, lo], axis=1), u2)


def _tri3(tri, x, dot=None):
    dot = dot or _dot
    hi = x.astype(BF16)
    r1 = x - hi.astype(F32)
    mid = r1.astype(BF16)
    lo = (r1 - mid.astype(F32)).astype(BF16)
    return dot(tri, hi) + dot(tri, mid) + dot(tri, lo)


def _iota2(shape, dim):
    return lax.broadcasted_iota(jnp.int32, shape, dim)


def _head_block_diag():
    r = _iota2((LANES, LANES), 0) // HEAD_DIM
    c = _iota2((LANES, LANES), 1) // HEAD_DIM
    return _stack2(jnp.where(r == c, 1.0, 0.0).astype(BF16))


def _head_mean(x, bd):
    return _cum2(x, bd) * (1.0 / HEAD_DIM)


def _sigmoid(x):
    return 1.0 / (1.0 + jnp.exp(-x))


def _log_sigmoid(x):
    return jnp.minimum(x, 0.0) - jnp.log(1.0 + jnp.exp(-jnp.abs(x)))


def _running_top(r_ref):
    return jnp.max(jnp.maximum(r_ref[0], r_ref[1]))


def _fox_tiles_left(i, pair, nq, fb_ref, tile):
    def bound(j):
        b = []
        for h in range(2):
            head = 2 * pair + h
            b.append(2.0 * NORM_SLACK * fb_ref[2 * nq + i, head] * fb_ref[3 * nq, head]
                     + fb_ref[2 * i, head] - fb_ref[2 * j + 1, head])
        return jnp.maximum(b[0], b[1])

    def more(j):
        return jnp.logical_and(j >= 0, bound(jnp.maximum(j, 0)) > UNDERFLOW)

    def step(j):
        tile(j, False)
        return j - 1

    return lax.while_loop(more, step, i - 1)


def _pair_masks():
    lane = _iota2((1, LANES), 1)
    return lane < HEAD_DIM


def _split_pair(x, m0):
    zero = jnp.zeros_like(x)
    return jnp.where(m0, x, zero), jnp.where(m0, zero, x)


def _inproj_fwd(x, nw, w_r, name, rider=None):
    s = x.shape[0]

    def body(x_ref, nw_ref, w_ref, qkv_ref, gf_ref):
        xv = x_ref[...]
        r = lax.rsqrt(jnp.mean(xv * xv, axis=-1, keepdims=True) + EPS)
        h = (xv * r * nw_ref[...]).astype(BF16)
        for c in range(0, QKV_W, 256):
            qkv_ref[:, c:c + 256] = _dot_nt(h, w_ref[c:c + 256, :]).astype(BF16)
        for c in range(0, GF_W, 256):
            gf_ref[:, c:c + 256] = _dot_nt(h, w_ref[QKV_W + c:QKV_W + c + 256, :])

    (qkv, gf), lands = _ride(dict(
        body=body, name=name, grid=(s // TM,),
        in_specs=[pl.BlockSpec((TM, D_MODEL), lambda i: (i, 0)),
                  pl.BlockSpec((1, D_MODEL), lambda i: (0, 0)),
                  pl.BlockSpec((WR_W, D_MODEL), lambda i: (0, 0))],
        out_specs=[pl.BlockSpec((TM, QKV_W), lambda i: (i, 0)),
                   pl.BlockSpec((TM, GF_W), lambda i: (i, 0))],
        out_shape=[jax.ShapeDtypeStruct((s, QKV_W), BF16), jax.ShapeDtypeStruct((s, GF_W), F32)],
        scratch_shapes=[], compiler_params=_ARB1, operands=[x, nw, w_r]), rider)
    return qkv, gf, lands


def _fox_prep_fwd(gf, qkv, bpad, name):
    s = gf.shape[0]
    nq = s // T
    nrow = -(-(3 * nq + 1) // 8) * 8

    def body(fl_ref, q_ref, k_ref, b_ref, fq_ref, fr_ref, fb_ref):
        tri = jnp.where(_iota2((T, T), 0) >= _iota2((T, T), 1), 1.0, 0.0).astype(BF16)
        m0 = _pair_masks()
        lane = _iota2((1, LANES), 1)
        norms = [jnp.zeros((1, LANES), F32) for _ in range(nq + 1)]
        same_head = (_iota2((LANES, LANES), 0) // HEAD_DIM) == (_iota2((LANES, LANES), 1) // HEAD_DIM)
        bd = jnp.where(same_head, 1.0, 0.0).astype(BF16)
        for p in range(FOX_HEADS // 2):
            cols = slice(p * LANES, (p + 1) * LANES)
            q = (q_ref[:, cols] * jnp.asarray(Q_SCALE, BF16)).astype(F32)
            k = k_ref[:, cols].astype(F32)
            qn = _dot((q * q).astype(BF16), bd)
            kn = _dot((k * k).astype(BF16), bd)
            tops = [jnp.max(qn[j * T:(j + 1) * T], axis=0, keepdims=True) for j in range(nq)]
            tops.append(jnp.max(kn, axis=0, keepdims=True))
            tops = [jnp.sqrt(top) for top in tops]
            for h in range(2):
                at = h * HEAD_DIM
                norms = [jnp.where(lane == 2 * p + h, top[:, at:at + 1], row)
                         for top, row in zip(tops, norms)]
        for j in range(nq + 1):
            fb_ref[2 * nq + j:2 * nq + j + 1, :] = norms[j]
        fb_ref[3 * nq + 1:, :] = jnp.zeros((nrow - 3 * nq - 1, LANES), F32)
        carry = jnp.zeros((1, LANES), F32)
        for blk in range(s // T):
            rows = slice(blk * T, (blk + 1) * T)
            lf = _log_sigmoid(fl_ref[rows, :] + b_ref[...])
            c = _tri3(tri, lf) + carry
            carry = c[T - 1:T, :]
            for p in range(FOX_HEADS // 2):
                fq_ref[rows, p * LANES:(p + 1) * LANES] = jnp.where(
                    m0, c[:, 2 * p:2 * p + 1], c[:, 2 * p + 1:2 * p + 2])
            fr_ref[:, rows] = c.T[0:FOX_HEADS, :]
            fb_ref[2 * blk:2 * blk + 1, :] = c[0:1, :]
            fb_ref[2 * blk + 1:2 * blk + 2, :] = carry

    base = 3 * SB_W // FX_W
    return pl.pallas_call(
        body, name=name, grid=(1,),
        in_specs=[pl.BlockSpec((s, LANES), lambda i: (0, MIX_W // LANES)),
                  pl.BlockSpec((s, FX_W), lambda i: (0, base)),
                  pl.BlockSpec((s, FX_W), lambda i: (0, base + 1)),
                  pl.BlockSpec((1, LANES), lambda i: (0, 0))],
        out_specs=[pl.BlockSpec((s, FX_W), lambda i: (0, 0)),
                   pl.BlockSpec((FOX_HEADS, s), lambda i: (0, 0)),
                   pl.BlockSpec((nrow, LANES), lambda i: (0, 0))],
        out_shape=[jax.ShapeDtypeStruct((s, FX_W), F32), jax.ShapeDtypeStruct((FOX_HEADS, s), F32),
                   jax.ShapeDtypeStruct((nrow, LANES), F32)],
        compiler_params=_ARB1,
    )(gf, qkv, qkv, bpad)


def _sb_fwd(qkv, name, rider=None):
    s = qkv.shape[0]

    def body(q_ref, k_ref, v_ref, o_ref, acc_ref, r_ref, as_ref):
        m0 = _pair_masks()
        strict = _iota2((T, T), 0) > _iota2((T, T), 1)
        u2 = _stack2(jnp.where(strict, 1.0, 0.0).astype(BF16))
        hs = range(2)

        def query_tile(i, rows):
            qh = _split_pair(q_ref[rows, :] * jnp.asarray(Q_SCALE, BF16), m0)
            acc_ref[...] = jnp.zeros_like(acc_ref)
            r_ref[...] = jnp.zeros_like(r_ref)

            def flush(j):
                v = v_ref[pl.ds(pl.multiple_of(j * T, T), T), :]
                for h in hs:
                    acc_ref[h] += _dot(as_ref[h], v)

            def tile(j, diag):
                k = k_ref[pl.ds(pl.multiple_of(j * T, T), T), :]
                z = [_dot_nt(qh[h], k) for h in hs]
                if not diag:
                    flush(j + 1)
                la = [jnp.minimum(z[h], 0.0) - jnp.log(1.0 + jnp.exp(-jnp.abs(z[h]))) for h in hs]
                lf = [la[h] - z[h] for h in hs]
                if diag:
                    lf = [jnp.where(strict, lf[h], 0.0) for h in hs]
                cin = [_cum2(lf[h], u2) for h in hs]
                a = [jnp.exp(la[h] + cin[h] + r_ref[h]) for h in hs]
                if diag:
                    a = [jnp.where(strict, a[h], 0.0) for h in hs]
                for h in hs:
                    r_ref[h] += cin[h][:, 0:1] + lf[h][:, 0:1]
                    as_ref[h] = a[h].astype(BF16)

            tile(i, True)

            def more(state):
                j, top = state
                return jnp.logical_and(j >= 0, top > UNDERFLOW)

            def step(state):
                j, _ = state
                tile(j, False)
                return j - 1, _running_top(r_ref)

            j_left, _ = lax.while_loop(more, step, (i - 1, _running_top(r_ref)))
            flush(j_left + 1)
            o_ref[rows, :] = jnp.where(m0, acc_ref[0], acc_ref[1])

        for n in range(QPS):
            query_tile(QPS * pl.program_id(1) + n, slice(n * T, (n + 1) * T))

    nb = SB_W // LANES
    (ysb,), lands = _ride(dict(
        body=body, name=name, grid=(nb, s // (QPS * T)),
        in_specs=[pl.BlockSpec((QPS * T, LANES), lambda p, i: (i, p)),
                  pl.BlockSpec((s, LANES), lambda p, i: (0, nb + p)),
                  pl.BlockSpec((s, LANES), lambda p, i: (0, 2 * nb + p))],
        out_specs=[pl.BlockSpec((QPS * T, LANES), lambda p, i: (i, p))],
        out_shape=[jax.ShapeDtypeStruct((s, SB_W), F32)],
        scratch_shapes=[pltpu.VMEM((2, T, LANES), F32), pltpu.VMEM((2, T, 1), F32),
                        pltpu.VMEM((2, T, T), BF16)],
        compiler_params=_ARB2, operands=[qkv, qkv, qkv]), rider)
    return ysb, lands


def _fox_fwd(qkv, fqb, frow, fbounds, name, rider=None):
    s = qkv.shape[0]

    def body(q_ref, k_ref, v_ref, fq_ref, fr_ref, fb_ref, o_ref, lse_ref, acc_ref, m_ref, ps_ref):
        pair = pl.program_id(0)
        m0 = _pair_masks()
        causal = _iota2((T, T), 0) >= _iota2((T, T), 1)
        ones = jnp.ones((T, LANES), BF16)
        hs = range(2)

        def query_tile(i, rows):
            qh = _split_pair(q_ref[rows, :] * jnp.asarray(Q_SCALE, BF16), m0)
            fq = fq_ref[rows, :]
            fqh = (fq[:, 0:1], fq[:, HEAD_DIM:HEAD_DIM + 1])
            acc_ref[...] = jnp.zeros_like(acc_ref)
            m_ref[...] = jnp.full_like(m_ref, NEG)

            def flush(j):
                v = v_ref[pl.ds(pl.multiple_of(j * T, T), T), :]
                va2 = _stack2(jnp.concatenate([v, ones], axis=1))
                for h in hs:
                    acc_ref[h] += _dot(ps_ref[h], va2)

            def tile(j, diag):
                off = pl.multiple_of(j * T, T)
                k = k_ref[pl.ds(off, T), :]
                sc = [_dot_nt(qh[h], k) + fqh[h] - fr_ref[h:h + 1, pl.ds(off, T)] for h in hs]
                if not diag:
                    flush(j + 1)
                if diag:
                    sc = [jnp.where(causal, sc[h], NEG) for h in hs]
                m_new = [jnp.maximum(m_ref[h], jnp.max(sc[h], axis=1, keepdims=True)) for h in hs]
                p = [jnp.exp(sc[h] - m_new[h]) for h in hs]
                for h in hs:
                    acc_ref[h] = acc_ref[h] * jnp.exp(m_ref[h] - m_new[h])
                    m_ref[h] = m_new[h]
                    p_hi, p_lo = _split2(p[h])
                    ps_ref[h] = jnp.concatenate([p_hi, p_lo], axis=1)

            tile(i, True)
            j_left = _fox_tiles_left(i, pair, s // T, fb_ref, tile)
            flush(j_left + 1)
            acc = (acc_ref[0], acc_ref[1])
            o_ref[rows, :] = jnp.where(m0, acc[0][:, :LANES] / acc[0][:, LANES:],
                                       acc[1][:, :LANES] / acc[1][:, LANES:])
            lse_ref[rows, :] = jnp.where(m0, m_ref[0] + jnp.log(acc[0][:, LANES:]),
                                         m_ref[1] + jnp.log(acc[1][:, LANES:]))

        for n in range(QPS):
            query_tile(QPS * pl.program_id(1) + n, slice(n * T, (n + 1) * T))

    nb = FX_W // LANES
    base = 3 * SB_W // LANES
    (yfx, lse), lands = _ride(dict(
        body=body, name=name, grid=(nb, s // (QPS * T)),
        in_specs=[pl.BlockSpec((QPS * T, LANES), lambda p, i: (i, base + p)),
                  pl.BlockSpec((s, LANES), lambda p, i: (0, base + nb + p)),
                  pl.BlockSpec((s, LANES), lambda p, i: (0, base + 2 * nb + p)),
                  pl.BlockSpec((QPS * T, LANES), lambda p, i: (i, p)),
                  pl.BlockSpec((None, 2, s), lambda p, i: (p, 0, 0)),
                  pl.BlockSpec(memory_space=pltpu.SMEM)],
        out_specs=[pl.BlockSpec((QPS * T, LANES), lambda p, i: (i, p)),
                   pl.BlockSpec((QPS * T, LANES), lambda p, i: (i, p))],
        out_shape=[jax.ShapeDtypeStruct((s, FX_W), F32), jax.ShapeDtypeStruct((s, FX_W), F32)],
        scratch_shapes=[pltpu.VMEM((2, T, 2 * LANES), F32), pltpu.VMEM((2, T, 1), F32),
                        pltpu.VMEM((2, T, 2 * T), BF16)],
        compiler_params=_ARB2, operands=[qkv, qkv, qkv, fqb, frow, fbounds]), rider)
    return yfx, lse, lands


def _memkv_fwd(mem, mnw, wkv, name):
    n = mem.shape[0]

    def body(mem_ref, mnw_ref, w_ref, kv_ref):
        mv = mem_ref[...]
        r = lax.rsqrt(jnp.mean(mv * mv, axis=-1, keepdims=True) + EPS)
        hm = (mv * r * mnw_ref[...]).astype(BF16)
        kv_ref[...] = _dot(hm, w_ref[...]).astype(BF16)

    return pl.pallas_call(
        body, name=name, grid=(1,),
        in_specs=[pl.BlockSpec((n, D_MODEL), lambda i: (0, 0)),
                  pl.BlockSpec((1, D_MODEL), lambda i: (0, 0)),
                  pl.BlockSpec((D_MODEL, 2 * MEM_W), lambda i: (0, 0))],
        out_specs=pl.BlockSpec((n, 2 * MEM_W), lambda i: (0, 0)),
        out_shape=jax.ShapeDtypeStruct((n, 2 * MEM_W), BF16),
        compiler_params=_ARB1,
    )(mem, mnw, wkv)


def _mem_fwd(qkv, kv, name):
    s = qkv.shape[0]
    n = kv.shape[0]

    def body(q_ref, k_ref, v_ref, o_ref, lse_ref):
        m0 = _pair_masks()
        qh = _split_pair(q_ref[...] * jnp.asarray(Q_SCALE, BF16), m0)
        k = k_ref[...]
        v = v_ref[...]
        outs, lses = [], []
        for h in range(2):
            sc = _dot_nt(qh[h], k)
            mx = jnp.max(sc, axis=1, keepdims=True)
            p = jnp.exp(sc - mx)
            l = jnp.sum(p, axis=1, keepdims=True)
            outs.append(_dot(p.astype(BF16), v) / l)
            lses.append(mx + jnp.log(l))
        o_ref[...] = jnp.where(m0, outs[0], outs[1])
        lse_ref[...] = jnp.where(m0, lses[0], lses[1])

    nb = MEM_W // LANES
    base = (3 * SB_W + 3 * FX_W) // LANES
    return pl.pallas_call(
        body, name=name, grid=(nb, s // TQM),
        in_specs=[pl.BlockSpec((TQM, LANES), lambda p, i: (i, base + p)),
                  pl.BlockSpec((n, LANES), lambda p, i: (0, p)),
                  pl.BlockSpec((n, LANES), lambda p, i: (0, nb + p))],
        out_specs=[pl.BlockSpec((TQM, LANES), lambda p, i: (i, p)),
                   pl.BlockSpec((TQM, LANES), lambda p, i: (i, p))],
        out_shape=[jax.ShapeDtypeStruct((s, MEM_W), F32), jax.ShapeDtypeStruct((s, MEM_W), F32)],
        compiler_params=_ARB2,
    )(qkv, kv, kv)


def _mix_chunk(c, ysb_ref, yfx_ref, ym_ref):
    if c < SB_W // LANES:
        return ysb_ref[:, c * LANES:(c + 1) * LANES]
    c -= SB_W // LANES
    if c < FX_W // LANES:
        return yfx_ref[:, c * LANES:(c + 1) * LANES]
    c -= FX_W // LANES
    return ym_ref[:, c * LANES:(c + 1) * LANES]


def _outproj_fwd(ysb, yfx, ym, gf, onw, wout, x, name):
    s = x.shape[0]

    def body(ysb_ref, yfx_ref, ym_ref, g_ref, onw_ref, w_ref, x_ref, o_ref, yg_ref):
        bd = _head_block_diag()
        for c in range(MIX_W // LANES):
            sl = slice(c * LANES, (c + 1) * LANES)
            u = _mix_chunk(c, ysb_ref, yfx_ref, ym_ref)
            r = lax.rsqrt(_head_mean(u * u, bd) + EPS)
            g = g_ref[:, sl]
            yg_ref[:, sl] = (u * r * onw_ref[:, sl] * (g * _sigmoid(g))).astype(BF16)
        o_ref[...] = x_ref[...] + _dot(yg_ref[...], w_ref[...])

    return pl.pallas_call(
        body, name=name, grid=(s // TM,),
        in_specs=[pl.BlockSpec((TM, SB_W), lambda i: (i, 0)),
                  pl.BlockSpec((TM, FX_W), lambda i: (i, 0)),
                  pl.BlockSpec((TM, MEM_W), lambda i: (i, 0)),
                  pl.BlockSpec((TM, MIX_W), lambda i: (i, 0)),
                  pl.BlockSpec((1, MIX_W), lambda i: (0, 0)),
                  pl.BlockSpec((MIX_W, D_MODEL), lambda i: (0, 0)),
                  pl.BlockSpec((TM, D_MODEL), lambda i: (i, 0))],
        out_specs=pl.BlockSpec((TM, D_MODEL), lambda i: (i, 0)),
        out_shape=jax.ShapeDtypeStruct((s, D_MODEL), F32),
        scratch_shapes=[pltpu.VMEM((TM, MIX_W), BF16)],
        compiler_params=_ARB1,
    )(ysb, yfx, ym, gf, onw, wout, x)


def _final_fwd_bwd(x, fnw, target, name):
    s = x.shape[0]

    def body(x_ref, w_ref, t_ref, dx_ref, loss_ref, dw_ref):
        @pl.when(pl.program_id(0) == 0)
        def _():
            loss_ref[...] = jnp.zeros_like(loss_ref)
            dw_ref[...] = jnp.zeros_like(dw_ref)

        xv = x_ref[...]
        w = w_ref[...]
        r = lax.rsqrt(jnp.mean(xv * xv, axis=-1, keepdims=True) + EPS)
        xh = xv * r
        err = xh * w - t_ref[...]
        part = jnp.sum(jnp.sum(err * err, axis=1, keepdims=True), axis=0, keepdims=True)
        loss_ref[...] += part * (0.5 / D_MODEL)
        dy = err * (1.0 / D_MODEL)
        dw_ref[...] += jnp.sum(dy * xh, axis=0, keepdims=True)
        dxh = dy * w
        dx_ref[...] = r * (dxh - xh * jnp.mean(dxh * xh, axis=-1, keepdims=True))

    return pl.pallas_call(
        body, name=name, grid=(s // TM,),
        in_specs=[pl.BlockSpec((TM, D_MODEL), lambda i: (i, 0)),
                  pl.BlockSpec((1, D_MODEL), lambda i: (0, 0)),
                  pl.BlockSpec((TM, D_MODEL), lambda i: (i, 0))],
        out_specs=[pl.BlockSpec((TM, D_MODEL), lambda i: (i, 0)),
                   pl.BlockSpec((1, LANES), lambda i: (0, 0)),
                   pl.BlockSpec((1, D_MODEL), lambda i: (0, 0))],
        out_shape=[jax.ShapeDtypeStruct((s, D_MODEL), F32), jax.ShapeDtypeStruct((1, LANES), F32),
                   jax.ShapeDtypeStruct((1, D_MODEL), F32)],
        compiler_params=_ARB1,
    )(x, fnw, target)


def _outproj_bwd(dxo, wout, ysb, yfx, ym, gf, onw, name):
    s = dxo.shape[0]

    def body(dx_ref, w_ref, ysb_ref, yfx_ref, ym_ref, g_ref, onw_ref,
             dysb_ref, dyfx_ref, dym_ref, dg_ref, dw_ref, donw_ref, yg_ref):
        @pl.when(pl.program_id(0) == 0)
        def _():
            dw_ref[...] = jnp.zeros_like(dw_ref)
            donw_ref[...] = jnp.zeros_like(donw_ref)

        dxb = dx_ref[...].astype(BF16)
        dyg = _dot_nt(dxb, w_ref[...])
        bd = _head_block_diag()
        for c in range(MIX_W // LANES):
            sl = slice(c * LANES, (c + 1) * LANES)
            u = _mix_chunk(c, ysb_ref, yfx_ref, ym_ref)
            r = lax.rsqrt(_head_mean(u * u, bd) + EPS)
            yn = u * r
            g = g_ref[:, sl]
            sg = _sigmoid(g)
            sil = g * sg
            onw = onw_ref[:, sl]
            e = dyg[:, sl]
            yg_ref[:, sl] = (yn * onw * sil).astype(BF16)
            donw_ref[:, sl] += jnp.sum(e * yn * sil, axis=0, keepdims=True)
            dg_ref[:, sl] = (e * yn * onw * (sg * (1.0 + g * (1.0 - sg)))).astype(BF16)
            dyn = e * onw * sil
            du = (r * (dyn - yn * _head_mean(dyn * yn, bd))).astype(BF16)
            if c < 4:
                dysb_ref[:, c * LANES:(c + 1) * LANES] = du
            elif c < 8:
                dyfx_ref[:, (c - 4) * LANES:(c - 3) * LANES] = du
            else:
                dym_ref[:, (c - 8) * LANES:(c - 7) * LANES] = du
        dw_ref[...] += _dot_tn(yg_ref[...], dxb)

    return pl.pallas_call(
        body, name=name, grid=(s // TM,),
        in_specs=[pl.BlockSpec((TM, D_MODEL), lambda i: (i, 0)),
                  pl.BlockSpec((MIX_W, D_MODEL), lambda i: (0, 0)),
                  pl.BlockSpec((TM, SB_W), lambda i: (i, 0)),
                  pl.BlockSpec((TM, FX_W), lambda i: (i, 0)),
                  pl.BlockSpec((TM, MEM_W), lambda i: (i, 0)),
                  pl.BlockSpec((TM, MIX_W), lambda i: (i, 0)),
                  pl.BlockSpec((1, MIX_W), lambda i: (0, 0))],
        out_specs=[pl.BlockSpec((TM, SB_W), lambda i: (i, 0)),
                   pl.BlockSpec((TM, FX_W), lambda i: (i, 0)),
                   pl.BlockSpec((TM, MEM_W), lambda i: (i, 0)),
                   pl.BlockSpec((TM, MIX_W), lambda i: (i, 0)),
                   pl.BlockSpec((MIX_W, D_MODEL), lambda i: (0, 0)),
                   pl.BlockSpec((1, MIX_W), lambda i: (0, 0))],
        out_shape=[jax.ShapeDtypeStruct((s, SB_W), BF16), jax.ShapeDtypeStruct((s, FX_W), BF16),
                   jax.ShapeDtypeStruct((s, MEM_W), BF16), jax.ShapeDtypeStruct((s, MIX_W), BF16),
                   jax.ShapeDtypeStruct((MIX_W, D_MODEL), F32), jax.ShapeDtypeStruct((1, MIX_W), F32)],
        scratch_shapes=[pltpu.VMEM((TM, MIX_W), BF16)],
        compiler_params=_ARB1,
    )(dxo, wout, ysb, yfx, ym, gf, onw)


def _row_dots(do, o, m0):
    prod = do.astype(F32) * o
    zero = jnp.zeros_like(prod)
    return (jnp.sum(jnp.where(m0, prod, zero), axis=1, keepdims=True),
            jnp.sum(jnp.where(m0, zero, prod), axis=1, keepdims=True))


def _sb_bwd(qkv, o, do, name, rider=None):
    s = qkv.shape[0]
    nq = s // T

    def body(q_ref, k_ref, v_ref, o_ref, do_ref, dq_ref, dk_ref, dv_ref,
             dqa_ref, dka_ref, dva_ref, rl_ref, rg_ref, dzs_ref, abs_ref):
        step_id = pl.program_id(1)

        @pl.when(step_id == 0)
        def _():
            dka_ref[...] = jnp.zeros_like(dka_ref)
            dva_ref[...] = jnp.zeros_like(dva_ref)

        m0 = _pair_masks()
        strict = _iota2((T, T), 0) > _iota2((T, T), 1)
        u2 = _stack2(jnp.where(strict, 1.0, 0.0).astype(BF16))
        hs = range(2)

        def query_tile(i, rows):
            qh = _split_pair(q_ref[rows, :] * jnp.asarray(Q_SCALE, BF16), m0)
            doh = _split_pair(do_ref[rows, :], m0)
            dsum = _row_dots(do_ref[rows, :], o_ref[rows, :], m0)
            dqa_ref[...] = jnp.zeros_like(dqa_ref)
            rl_ref[...] = jnp.zeros_like(rl_ref)
            rg_ref[...] = jnp.zeros_like(rg_ref)

            def flush(j):
                off = pl.multiple_of(j * T, T)
                k = k_ref[pl.ds(off, T), :]
                for h in hs:
                    dqa_ref[h] += _dot(dzs_ref[h], k)
                dka_ref[pl.ds(off, T), :] += (_dot_tn(dzs_ref[0], qh[0])
                                              + _dot_tn(dzs_ref[1], qh[1]))
                dva_ref[pl.ds(off, T), :] += (_dot_tn(abs_ref[0], doh[0])
                                              + _dot_tn(abs_ref[1], doh[1]))

            def tile(j, diag):
                off = pl.multiple_of(j * T, T)
                k = k_ref[pl.ds(off, T), :]
                v = v_ref[pl.ds(off, T), :]
                z = [_dot_nt(qh[h], k) for h in hs]
                da = [_dot_nt(doh[h], v) for h in hs]
                if not diag:
                    flush(j + 1)
                la = [jnp.minimum(z[h], 0.0) - jnp.log(1.0 + jnp.exp(-jnp.abs(z[h]))) for h in hs]
                lf = [la[h] - z[h] for h in hs]
                if diag:
                    lf = [jnp.where(strict, lf[h], 0.0) for h in hs]
                cin = [_cum2(lf[h], u2) for h in hs]
                a = [jnp.exp(la[h] + cin[h] + rl_ref[h]) for h in hs]
                if diag:
                    a = [jnp.where(strict, a[h], 0.0) for h in hs]
                ab = [a[h].astype(BF16) for h in hs]
                g = [ab[h].astype(F32) * da[h] for h in hs]
                gin = [_cum2(g[h], u2) for h in hs]
                dz = [g[h] - jnp.exp(la[h]) * ((dsum[h] - rg_ref[h]) - gin[h]) for h in hs]
                if diag:
                    dz = [jnp.where(strict, dz[h], 0.0) for h in hs]
                for h in hs:
                    rl_ref[h] += cin[h][:, 0:1] + lf[h][:, 0:1]
                    rg_ref[h] += gin[h][:, 0:1] + g[h][:, 0:1]
                    dzs_ref[h] = dz[h].astype(BF16)
                    abs_ref[h] = ab[h]

            tile(i, True)

            def more(state):
                j, top = state
                return jnp.logical_and(j >= 0, top > UNDERFLOW)

            def step(state):
                j, _ = state
                tile(j, False)
                return j - 1, _running_top(rl_ref)

            j_left, _ = lax.while_loop(more, step, (i - 1, _running_top(rl_ref)))
            flush(j_left + 1)
            dq_ref[rows, :] = (jnp.where(m0, dqa_ref[0], dqa_ref[1]) * Q_SCALE).astype(BF16)

        for n in range(QPS):
            query_tile(QPS * step_id + n, slice(n * T, (n + 1) * T))

        @pl.when(step_id == nq // QPS - 1)
        def _():
            dk_ref[...] = dka_ref[...].astype(BF16)
            dv_ref[...] = dva_ref[...].astype(BF16)

    nb = SB_W // LANES
    (dq, dk, dv), lands = _ride(dict(
        body=body, name=name, grid=(nb, nq // QPS),
        in_specs=[pl.BlockSpec((QPS * T, LANES), lambda p, i: (i, p)),
                  pl.BlockSpec((s, LANES), lambda p, i: (0, nb + p)),
                  pl.BlockSpec((s, LANES), lambda p, i: (0, 2 * nb + p)),
                  pl.BlockSpec((QPS * T, LANES), lambda p, i: (i, p)),
                  pl.BlockSpec((QPS * T, LANES), lambda p, i: (i, p))],
        out_specs=[pl.BlockSpec((QPS * T, LANES), lambda p, i: (i, p)),
                   pl.BlockSpec((s, LANES), lambda p, i: (0, p)),
                   pl.BlockSpec((s, LANES), lambda p, i: (0, p))],
        out_shape=[jax.ShapeDtypeStruct((s, SB_W), BF16)] * 3,
        scratch_shapes=[pltpu.VMEM((2, T, LANES), F32), pltpu.VMEM((s, LANES), F32),
                        pltpu.VMEM((s, LANES), F32), pltpu.VMEM((2, T, 1), F32),
                        pltpu.VMEM((2, T, 1), F32), pltpu.VMEM((2, T, T), BF16),
                        pltpu.VMEM((2, T, T), BF16)],
        compiler_params=_ARB2, operands=[qkv, qkv, qkv, o, do]), rider)
    return dq, dk, dv, lands


def _fox_bwd(qkv, fqb, frow, fbounds, o, lse, do, name, rider=None):
    s = qkv.shape[0]
    nq = s // T

    def body(q_ref, k_ref, v_ref, fq_ref, fr_ref, fb_ref, o_ref, lse_ref, do_ref,
             dq_ref, dk_ref, dv_ref, df_ref, dqa_ref, dka_ref, dva_ref, dfa_ref, dls_ref, pbs_ref):
        step_id = pl.program_id(1)

        @pl.when(step_id == 0)
        def _():
            dka_ref[...] = jnp.zeros_like(dka_ref)
            dva_ref[...] = jnp.zeros_like(dva_ref)
            dfa_ref[...] = jnp.zeros_like(dfa_ref)

        m0 = _pair_masks()
        causal = _iota2((T, T), 0) >= _iota2((T, T), 1)
        hs = range(2)

        def query_tile(i, rows):
            qh = _split_pair(q_ref[rows, :] * jnp.asarray(Q_SCALE, BF16), m0)
            doh = _split_pair(do_ref[rows, :], m0)
            dsum = _row_dots(do_ref[rows, :], o_ref[rows, :], m0)
            fq = fq_ref[rows, :]
            fqh = (fq[:, 0:1], fq[:, HEAD_DIM:HEAD_DIM + 1])
            lse = lse_ref[rows, :]
            lseh = (lse[:, 0:1], lse[:, HEAD_DIM:HEAD_DIM + 1])
            dqa_ref[...] = jnp.zeros_like(dqa_ref)

            def flush(j):
                off = pl.multiple_of(j * T, T)
                k = k_ref[pl.ds(off, T), :]
                for h in hs:
                    dqa_ref[h] += _dot(dls_ref[h], k)
                dka_ref[pl.ds(off, T), :] += (_dot_tn(dls_ref[0], qh[0])
                                              + _dot_tn(dls_ref[1], qh[1]))
                dva_ref[pl.ds(off, T), :] += (_dot_tn(pbs_ref[0], doh[0])
                                              + _dot_tn(pbs_ref[1], doh[1]))

            def tile(j, diag):
                off = pl.multiple_of(j * T, T)
                k = k_ref[pl.ds(off, T), :]
                v = v_ref[pl.ds(off, T), :]
                sc = [_dot_nt(qh[h], k) + fqh[h] - fr_ref[h:h + 1, pl.ds(off, T)] for h in hs]
                dp = [_dot_nt(doh[h], v) for h in hs]
                if not diag:
                    flush(j + 1)
                p = [jnp.exp(sc[h] - lseh[h]) for h in hs]
                if diag:
                    p = [jnp.where(causal, p[h], 0.0) for h in hs]
                dl = [p[h] * (dp[h] - dsum[h]) for h in hs]
                for h in hs:
                    dls_ref[h] = dl[h].astype(BF16)
                    pbs_ref[h] = p[h].astype(BF16)
                    dfa_ref[h:h + 1, pl.ds(off, T)] -= jnp.sum(dl[h], axis=0, keepdims=True)

            tile(i, True)
            j_left = _fox_tiles_left(i, pl.program_id(0), nq, fb_ref, tile)
            flush(j_left + 1)
            dq_ref[rows, :] = (jnp.where(m0, dqa_ref[0], dqa_ref[1]) * Q_SCALE).astype(BF16)

        for n in range(QPS):
            query_tile(QPS * step_id + n, slice(n * T, (n + 1) * T))

        @pl.when(step_id == nq // QPS - 1)
        def _():
            dk_ref[...] = dka_ref[...].astype(BF16)
            dv_ref[...] = dva_ref[...].astype(BF16)
            df_ref[...] = dfa_ref[...]

    nb = FX_W // LANES
    base = 3 * SB_W // LANES
    (dq, dk, dv, df), lands = _ride(dict(
        body=body, name=name, grid=(nb, nq // QPS),
        in_specs=[pl.BlockSpec((QPS * T, LANES), lambda p, i: (i, base + p)),
                  pl.BlockSpec((s, LANES), lambda p, i: (0, base + nb + p)),
                  pl.BlockSpec((s, LANES), lambda p, i: (0, base + 2 * nb + p)),
                  pl.BlockSpec((QPS * T, LANES), lambda p, i: (i, p)),
                  pl.BlockSpec((None, 2, s), lambda p, i: (p, 0, 0)),
                  pl.BlockSpec(memory_space=pltpu.SMEM),
                  pl.BlockSpec((QPS * T, LANES), lambda p, i: (i, p)),
                  pl.BlockSpec((QPS * T, LANES), lambda p, i: (i, p)),
                  pl.BlockSpec((QPS * T, LANES), lambda p, i: (i, p))],
        out_specs=[pl.BlockSpec((QPS * T, LANES), lambda p, i: (i, p)),
                   pl.BlockSpec((s, LANES), lambda p, i: (0, p)),
                   pl.BlockSpec((s, LANES), lambda p, i: (0, p)),
                   pl.BlockSpec((None, 2, s), lambda p, i: (p, 0, 0))],
        out_shape=[jax.ShapeDtypeStruct((s, FX_W), BF16)] * 3
        + [jax.ShapeDtypeStruct((nb, 2, s), F32)],
        scratch_shapes=[pltpu.VMEM((2, T, LANES), F32), pltpu.VMEM((s, LANES), F32),
                        pltpu.VMEM((s, LANES), F32), pltpu.VMEM((2, s), F32),
                        pltpu.VMEM((2, T, T), BF16), pltpu.VMEM((2, T, T), BF16)],
        compiler_params=_ARB2, operands=[qkv, qkv, qkv, fqb, frow, fbounds, o, lse, do]), rider)
    return dq, dk, dv, df, lands


def _fox_prep_bwd(dfrow, gf, bpad, name):
    s = gf.shape[0]

    def body(df_ref, fl_ref, b_ref, dfl_ref, db_ref):
        tri = jnp.where(_iota2((T, T), 0) <= _iota2((T, T), 1), 1.0, 0.0).astype(BF16)
        carry = jnp.zeros((1, LANES), F32)
        db = jnp.zeros((1, LANES), F32)
        fill = jnp.zeros((LANES - FOX_HEADS, T), F32)
        for blk in reversed(range(s // T)):
            rows = slice(blk * T, (blk + 1) * T)
            c = _tri3(tri, jnp.concatenate([df_ref[:, rows], fill], axis=0), _dot_nt) + carry
            carry = c[0:1, :]
            dfl = c / (1.0 + jnp.exp(fl_ref[rows, :] + b_ref[...]))
            dfl_ref[rows, :] = dfl.astype(BF16)
            db = db + jnp.sum(dfl, axis=0, keepdims=True)
        db_ref[...] = db

    return pl.pallas_call(
        body, name=name, grid=(1,),
        in_specs=[pl.BlockSpec((FOX_HEADS, s), lambda i: (0, 0)),
                  pl.BlockSpec((s, LANES), lambda i: (0, MIX_W // LANES)),
                  pl.BlockSpec((1, LANES), lambda i: (0, 0))],
        out_specs=[pl.BlockSpec((s, LANES), lambda i: (0, 0)),
                   pl.BlockSpec((1, LANES), lambda i: (0, 0))],
        out_shape=[jax.ShapeDtypeStruct((s, LANES), BF16), jax.ShapeDtypeStruct((1, LANES), F32)],
        compiler_params=_ARB1,
    )(dfrow, gf, bpad)


def _mem_bwd(qkv, kv, o, lse, do, name):
    s = qkv.shape[0]
    n = kv.shape[0]

    def body(q_ref, k_ref, v_ref, o_ref, lse_ref, do_ref, dq_ref, dk_ref, dv_ref):
        @pl.when(pl.program_id(1) == 0)
        def _():
            dk_ref[...] = jnp.zeros_like(dk_ref)
            dv_ref[...] = jnp.zeros_like(dv_ref)

        m0 = _pair_masks()
        qh = _split_pair(q_ref[...] * jnp.asarray(Q_SCALE, BF16), m0)
        doh = _split_pair(do_ref[...], m0)
        dsum = _row_dots(do_ref[...], o_ref[...], m0)
        lse = lse_ref[...]
        lseh = (lse[:, 0:1], lse[:, HEAD_DIM:HEAD_DIM + 1])
        k = k_ref[...]
        v = v_ref[...]
        dqs = []
        for h in range(2):
            p = jnp.exp(_dot_nt(qh[h], k) - lseh[h])
            dl = p * (_dot_nt(doh[h], v) - dsum[h])
            dlb = dl.astype(BF16)
            dqs.append(_dot(dlb, k))
            dk_ref[...] += _dot_tn(dlb, qh[h])
            dv_ref[...] += _dot_tn(p.astype(BF16), doh[h])
        dq_ref[...] = (jnp.where(m0, dqs[0], dqs[1]) * Q_SCALE).astype(BF16)

    nb = MEM_W // LANES
    base = (3 * SB_W + 3 * FX_W) // LANES
    return pl.pallas_call(
        body, name=name, grid=(nb, s // TQM),
        in_specs=[pl.BlockSpec((TQM, LANES), lambda p, i: (i, base + p)),
                  pl.BlockSpec((n, LANES), lambda p, i: (0, p)),
                  pl.BlockSpec((n, LANES), lambda p, i: (0, nb + p)),
                  pl.BlockSpec((TQM, LANES), lambda p, i: (i, p)),
                  pl.BlockSpec((TQM, LANES), lambda p, i: (i, p)),
                  pl.BlockSpec((TQM, LANES), lambda p, i: (i, p))],
        out_specs=[pl.BlockSpec((TQM, LANES), lambda p, i: (i, p)),
                   pl.BlockSpec((n, LANES), lambda p, i: (0, p)),
                   pl.BlockSpec((n, LANES), lambda p, i: (0, p))],
        out_shape=[jax.ShapeDtypeStruct((s, MEM_W), BF16), jax.ShapeDtypeStruct((n, MEM_W), F32),
                   jax.ShapeDtypeStruct((n, MEM_W), F32)],
        compiler_params=_ARB2,
    )(qkv, kv, kv, o, lse, do)


def _memkv_bwd(mem, mnw, wkv, dk, dv, name):
    n = mem.shape[0]

    def body(mem_ref, mnw_ref, w_ref, dk_ref, dv_ref, dw_ref, dmnw_ref):
        mv = mem_ref[...]
        r = lax.rsqrt(jnp.mean(mv * mv, axis=-1, keepdims=True) + EPS)
        mh = mv * r
        hm = (mh * mnw_ref[...]).astype(BF16)
        dkv = jnp.concatenate([dk_ref[...], dv_ref[...]], axis=1).astype(BF16)
        dw_ref[...] = _dot_tn(hm, dkv)
        dhm = _dot_nt(dkv, w_ref[...])
        dmnw_ref[...] = jnp.sum(dhm * mh, axis=0, keepdims=True)

    return pl.pallas_call(
        body, name=name, grid=(1,),
        in_specs=[pl.BlockSpec((n, D_MODEL), lambda i: (0, 0)),
                  pl.BlockSpec((1, D_MODEL), lambda i: (0, 0)),
                  pl.BlockSpec((D_MODEL, 2 * MEM_W), lambda i: (0, 0)),
                  pl.BlockSpec((n, MEM_W), lambda i: (0, 0)),
                  pl.BlockSpec((n, MEM_W), lambda i: (0, 0))],
        out_specs=[pl.BlockSpec((D_MODEL, 2 * MEM_W), lambda i: (0, 0)),
                   pl.BlockSpec((1, D_MODEL), lambda i: (0, 0))],
        out_shape=[jax.ShapeDtypeStruct((D_MODEL, 2 * MEM_W), F32),
                   jax.ShapeDtypeStruct((1, D_MODEL), F32)],
        compiler_params=_ARB1,
    )(mem, mnw, wkv, dk, dv)


def _inproj_bwd_dx(pieces, w_r, x, nw, dxo, name, rider=None):
    s = x.shape[0]
    n = len(pieces)
    widths = [p.shape[1] for p in pieces]

    def body(*refs):
        piece_refs = refs[:n]
        w_ref, x_ref, nw_ref, dxo_ref, dx_ref, h_ref, dnw_ref, dp_ref = refs[n:]

        @pl.when(pl.program_id(0) == 0)
        def _():
            dnw_ref[...] = jnp.zeros_like(dnw_ref)

        col = 0
        for r, wd in zip(piece_refs, widths):
            dp_ref[:, col:col + wd] = r[...]
            col += wd
        dp_ref[:, col:] = jnp.zeros((TM, WR_W - col), BF16)
        dh = _dot(dp_ref[...], w_ref[...])
        xv = x_ref[...]
        nw = nw_ref[...]
        r = lax.rsqrt(jnp.mean(xv * xv, axis=-1, keepdims=True) + EPS)
        xh = xv * r
        h_ref[...] = (xh * nw).astype(BF16)
        dnw_ref[...] += jnp.sum(dh * xh, axis=0, keepdims=True)
        dxh = dh * nw
        dx_ref[...] = r * (dxh - xh * jnp.mean(dxh * xh, axis=-1, keepdims=True)) + dxo_ref[...]

    (dx, h, dnw, dproj), lands = _ride(dict(
        body=body, name=name, grid=(s // TM,),
        in_specs=[pl.BlockSpec((TM, wd), lambda i: (i, 0)) for wd in widths]
        + [pl.BlockSpec((WR_W, D_MODEL), lambda i: (0, 0)),
           pl.BlockSpec((TM, D_MODEL), lambda i: (i, 0)),
           pl.BlockSpec((1, D_MODEL), lambda i: (0, 0)),
           pl.BlockSpec((TM, D_MODEL), lambda i: (i, 0))],
        out_specs=[pl.BlockSpec((TM, D_MODEL), lambda i: (i, 0)),
                   pl.BlockSpec((TM, D_MODEL), lambda i: (i, 0)),
                   pl.BlockSpec((1, D_MODEL), lambda i: (0, 0)),
                   pl.BlockSpec((TM, WR_W), lambda i: (i, 0))],
        out_shape=[jax.ShapeDtypeStruct((s, D_MODEL), F32), jax.ShapeDtypeStruct((s, D_MODEL), BF16),
                   jax.ShapeDtypeStruct((1, D_MODEL), F32), jax.ShapeDtypeStruct((s, WR_W), BF16)],
        scratch_shapes=[], compiler_params=_ARB1, operands=[*pieces, w_r, x, nw, dxo]), rider)
    return dx, h, dnw, dproj, lands


def _inproj_bwd_dw(h, dproj, name):
    s = dproj.shape[0]
    tn = 256

    def body(h_ref, dp_ref, dw_ref):
        dw_ref[...] = _dot_tn(dp_ref[...], h_ref[...])

    return pl.pallas_call(
        body, name=name, grid=(WR_W // tn,),
        in_specs=[pl.BlockSpec((s, D_MODEL), lambda j: (0, 0)),
                  pl.BlockSpec((s, tn), lambda j: (0, j))],
        out_specs=pl.BlockSpec((tn, D_MODEL), lambda j: (j, 0)),
        out_shape=jax.ShapeDtypeStruct((WR_W, D_MODEL), F32),
        compiler_params=_ARB1,
    )(h, dproj)


def _rearrange_w_in(wt):
    pad = jnp.zeros((FL_PAD - FOX_HEADS,) + wt.shape[1:], wt.dtype)
    return jnp.concatenate([wt[:3072], wt[3080:3336], wt[3336:IN_W], wt[3072:3080], pad], axis=0)


_W_IN_SEGMENTS = [(0, 3072, 0), (3072, 3080, QKV_W + MIX_W), (3080, 3336, 3072), (3336, IN_W, QKV_W)]
ASSEMBLE_COLS = 256
ASSEMBLE_ROWS = 128


def _shard_pieces(k):
    lo, hi = k * SHARD_W, (k + 1) * SHARD_W
    return [(max(lo, a) - lo, at + max(lo, a) - a, min(hi, b) - max(lo, a))
            for a, b, at in _W_IN_SEGMENTS if max(lo, a) < min(hi, b)]


def _assemble_w_in(g, name):
    used = QKV_W + MIX_W + FOX_HEADS

    def body(g_ref, w_ref):
        for k in range(N_DEV):
            for src, dst, n in _shard_pieces(k):
                for r in range(0, n, ASSEMBLE_ROWS):
                    m = min(ASSEMBLE_ROWS, n - r)
                    w_ref[dst + r:dst + r + m, :] = g_ref[k, src + r:src + r + m, :]
        w_ref[used:, :] = jnp.zeros((WR_W - used, ASSEMBLE_COLS), BF16)

    return pl.pallas_call(
        body, name=name, grid=(D_MODEL // ASSEMBLE_COLS,),
        in_specs=[pl.BlockSpec((N_DEV, SHARD_W, ASSEMBLE_COLS), lambda j: (0, 0, j))],
        out_specs=pl.BlockSpec((WR_W, ASSEMBLE_COLS), lambda j: (0, j)),
        out_shape=jax.ShapeDtypeStruct((WR_W, D_MODEL), BF16), compiler_params=_ARB1)(g)


def _restore_w_in(g):
    gate0 = QKV_W
    fl0 = QKV_W + MIX_W
    return jnp.concatenate(
        [g[:3072], g[fl0:fl0 + FOX_HEADS], g[3072:QKV_W], g[gate0:fl0]], axis=0)


def _pad_lanes(v, width=LANES):
    return jnp.pad(v, (0, width - v.shape[0])).reshape(1, width)


def _layer_fwd(xs, mem, nw, w_r, b_forget, mnw, late, onw, l, travel=None):
    s = xs.shape[0]
    bpad = _pad_lanes(b_forget)
    travel = _Travel(travel)
    qkv, gf, _ = travel.ride(2, _inproj_fwd, xs, nw, w_r, f"inproj_fwd_{l}")
    fqb, frow, fbounds = _fox_prep_fwd(gf, qkv, bpad, f"fox_prep_fwd_{l}")
    frow = frow.reshape(FOX_HEADS // 2, 2, s)
    ysb, _ = travel.ride(0, _sb_fwd, qkv, f"sb_fwd_{l}")
    yfx, lse_fx, _ = travel.ride(1, _fox_fwd, qkv, fqb, frow, fbounds, f"fox_fwd_{l}")
    wkv, wout = late(travel.lands)
    kv = _memkv_fwd(mem, mnw, wkv, f"memkv_fwd_{l}")
    ym, lse_m = _mem_fwd(qkv, kv, f"mem_fwd_{l}")
    xn = _outproj_fwd(ysb, yfx, ym, gf, onw, wout, xs, f"outproj_fwd_{l}")
    saved = (xs, nw, mnw, onw, bpad, qkv, gf, fqb, frow, fbounds, ysb, yfx, lse_fx, kv, ym, lse_m)
    return xn, saved, travel.lands, (wkv, wout)


class _Travel:
    def __init__(self, plan):
        self.plan = plan
        self.lands = None if plan is None else _new_lands(plan[0], plan[1])

    def ride(self, n, fn, *args):
        if self.plan is None or n >= len(self.plan[2]) or self.plan[2][n] is None:
            return fn(*args)
        srcs, scatter, legs = self.plan
        idx, rows = legs[n]
        out = fn(*args, rider=_Rider([srcs[a] for a in idx], [self.lands[a] for a in idx],
                                     scatter, rows, relay=not scatter))
        for a, land in zip(idx, out[-1]):
            self.lands[a] = land
        return out


def _layer_bwd(dx, saved, mem, w_r, wkv, wout, l, travel=None):
    xs, nw, mnw, onw, bpad, qkv, gf, fqb, frow, fbounds, ysb, yfx, lse_fx, kv, ym, lse_m = saved
    s = xs.shape[0]
    dysb, dyfx, dym, dgate, dwout, donw = _outproj_bwd(
        dx, wout, ysb, yfx, ym, gf, onw, f"outproj_bwd_{l}")
    travel = _Travel(None if travel is None else travel(dwout))
    sdq, sdk, sdv, _ = travel.ride(0, _sb_bwd, qkv, ysb, dysb, f"sb_bwd_{l}")
    fdq, fdk, fdv, dfrow, _ = travel.ride(1, _fox_bwd, qkv, fqb, frow, fbounds, yfx, lse_fx, dyfx,
                                          f"fox_bwd_{l}")
    dfl, db = _fox_prep_bwd(dfrow.reshape(FOX_HEADS, s), gf, bpad, f"fox_prep_bwd_{l}")
    dmq, dmk, dmv = _mem_bwd(qkv, kv, ym, lse_m, dym, f"mem_bwd_{l}")
    dwkv, dmnw = _memkv_bwd(mem, mnw, wkv, dmk, dmv, f"memkv_bwd_{l}")
    dx, ht, dnw, dproj, _ = travel.ride(2, _inproj_bwd_dx,
                                        [sdq, sdk, sdv, fdq, fdk, fdv, dmq, dgate, dfl],
                                        w_r, xs, nw, dx, f"inproj_bwd_dx_{l}")
    dwr = _inproj_bwd_dw(ht, dproj, f"inproj_bwd_dw_{l}")
    grads = dict(norm_w=dnw[0], w_r=dwr, b_forget=db[0, :FOX_HEADS], mem_norm_w=dmnw[0],
                 w_mem_kv=dwkv, out_norm_w=donw[0], w_out=dwout)
    return dx, grads, travel.lands


_ANY = pl.BlockSpec(memory_space=pl.ANY)


def _my_place():
    return lax.axis_index("x"), lax.axis_index("y"), lax.axis_index("c")


def _flip(v, bit):
    return 1 - v if bit else v


def _block_index(px, py, pc):
    return 4 * px + 2 * py + pc


def _all_gather_weights(shards, name):
    n = len(shards)

    def body(*refs):
        ins, outs = refs[:n], refs[n:2 * n]
        send_sems, recv_sems, local_sems = refs[2 * n:]
        x, y, c = _my_place()
        me = (x, y, c)
        sibling = (x, y, 1 - c)
        chips = [(1 - x, y), (x, 1 - y), (1 - x, 1 - y)]

        def copy(a, k, block, to, src=None):
            dst = outs[a].at[_block_index(*block)]
            return pltpu.make_async_remote_copy(
                src_ref=dst if src is None else src, dst_ref=dst,
                send_sem=send_sems.at[a, k], recv_sem=recv_sems.at[a, k],
                device_id=to, device_id_type=pl.DeviceIdType.MESH)

        mine = [pltpu.make_async_copy(ins[a], outs[a].at[_block_index(*me)], local_sems.at[a])
                for a in range(n)]
        for cp in mine:
            cp.start()
        first = []
        for a in range(n):
            first.append(copy(a, 0, me, sibling, src=ins[a]))
            first += [copy(a, 1 + j, me, (*chip, c), src=ins[a]) for j, chip in enumerate(chips)]
        for cp in first:
            cp.start()
        passed = []
        for j, chip in enumerate(chips):
            for a in range(n):
                copy(a, 1 + j, (*chip, c), me).wait_recv()
                fwd = copy(a, 4 + j, (*chip, c), sibling)
                fwd.start()
                passed.append(fwd)
        for a in range(n):
            copy(a, 0, sibling, me).wait_recv()
            for j, chip in enumerate(chips):
                copy(a, 4 + j, (*chip, 1 - c), me).wait_recv()
        for cp in first + passed:
            cp.wait_send()
        for cp in mine:
            cp.wait()

    return pl.pallas_call(
        body, name=name,
        in_specs=[_ANY] * n, out_specs=[_ANY] * n,
        out_shape=[jax.ShapeDtypeStruct((N_DEV,) + v.shape, v.dtype) for v in shards],
        scratch_shapes=[pltpu.SemaphoreType.DMA((n, 7)), pltpu.SemaphoreType.DMA((n, 7)),
                        pltpu.SemaphoreType.DMA((n,))],
    )(*shards)


def _exchange_blocks(blocked, name):
    n = len(blocked)

    def body(*refs):
        ins, outs = refs[:n], refs[n:2 * n]
        send_sems, recv_sems, local_sems = refs[2 * n:]
        x, y, c = _my_place()
        mine_idx = _block_index(x, y, c)
        local = [pltpu.make_async_copy(ins[a].at[mine_idx], outs[a].at[mine_idx], local_sems.at[a])
                 for a in range(n)]
        for cp in local:
            cp.start()
        sends, arrivals = [], []
        for r in range(1, N_DEV):
            peer = (_flip(x, r & 4), _flip(y, r & 2), _flip(c, r & 1))
            peer_idx = _block_index(*peer)
            for a in range(n):
                sems = dict(send_sem=send_sems.at[a, r - 1], recv_sem=recv_sems.at[a, r - 1],
                            device_id=peer, device_id_type=pl.DeviceIdType.MESH)
                sends.append(pltpu.make_async_remote_copy(
                    src_ref=ins[a].at[peer_idx], dst_ref=outs[a].at[mine_idx], **sems))
                arrivals.append(pltpu.make_async_remote_copy(
                    src_ref=ins[a].at[peer_idx], dst_ref=outs[a].at[peer_idx], **sems))
        for cp in sends:
            cp.start()
        for cp in arrivals:
            cp.wait_recv()
        for cp in sends:
            cp.wait_send()
        for cp in local:
            cp.wait()

    return pl.pallas_call(
        body, name=name,
        in_specs=[_ANY] * n, out_specs=[_ANY] * n,
        out_shape=[jax.ShapeDtypeStruct(v.shape, v.dtype) for v in blocked],
        scratch_shapes=[pltpu.SemaphoreType.DMA((n, 7)), pltpu.SemaphoreType.DMA((n, 7)),
                        pltpu.SemaphoreType.DMA((n,))],
    )(*blocked)


N_CHIP = N_DEV // 2


def _pair_swap(blocked, name):
    n = len(blocked)

    def body(*refs):
        ins, outs = refs[:n], refs[n:2 * n]
        send_sems, recv_sems = refs[2 * n:]
        x, y, c = _my_place()
        copies = [pltpu.make_async_remote_copy(
            src_ref=ins[a].at[j, 1 - c], dst_ref=outs[a].at[j],
            send_sem=send_sems.at[N_CHIP * a + j], recv_sem=recv_sems.at[N_CHIP * a + j],
            device_id=(x, y, 1 - c), device_id_type=pl.DeviceIdType.MESH)
            for a in range(n) for j in range(N_CHIP)]
        for cp in copies:
            cp.start()
        for cp in copies:
            cp.wait_recv()
        for cp in copies:
            cp.wait_send()

    return pl.pallas_call(
        body, name=name,
        in_specs=[_ANY] * n, out_specs=[_ANY] * n,
        out_shape=[jax.ShapeDtypeStruct((N_CHIP,) + v.shape[2:], v.dtype) for v in blocked],
        scratch_shapes=[pltpu.SemaphoreType.DMA((N_CHIP * n,)),
                        pltpu.SemaphoreType.DMA((N_CHIP * n,))],
    )(*blocked)


def _pair_add(by_core, theirs, core, name):
    n = len(by_core)

    def body(core_ref, *refs):
        for a in range(n):
            mine, got, out = refs[a], refs[n + a], refs[2 * n + a]
            out[...] = (mine[...].astype(F32) + got[...].astype(F32)).astype(BF16)

    def own(v):
        return pl.BlockSpec((None, None) + v.shape[2:], lambda j, core_ref: (j, core_ref[0], 0, 0))

    def block(v):
        return pl.BlockSpec((None,) + v.shape[1:], lambda j, core_ref: (j, 0, 0))

    return list(pl.pallas_call(
        body, name=name,
        grid_spec=pltpu.PrefetchScalarGridSpec(
            num_scalar_prefetch=1, grid=(N_CHIP,),
            in_specs=[own(v) for v in by_core] + [block(v) for v in theirs],
            out_specs=[block(v) for v in theirs]),
        out_shape=[jax.ShapeDtypeStruct(v.shape, BF16) for v in theirs],
        compiler_params=_ARB1,
    )(core, *by_core, *theirs))


def _chip_exchange(by_chip, to_all, name):
    n, m = len(by_chip), len(to_all)

    def body(*refs):
        ins, alls = refs[:n], refs[n:n + m]
        outs, all_outs = refs[n + m:2 * n + m], refs[2 * n + m:2 * (n + m)]
        send_sems, recv_sems, local_sems = refs[2 * (n + m):]
        x, y, c = _my_place()
        my_chip, mine_idx = 2 * x + y, _block_index(x, y, c)
        local = [pltpu.make_async_copy(ins[a].at[my_chip], outs[a].at[my_chip], local_sems.at[a])
                 for a in range(n)]
        local += [pltpu.make_async_copy(alls[b].at[mine_idx], all_outs[b].at[mine_idx],
                                        local_sems.at[n + b]) for b in range(m)]
        for cp in local:
            cp.start()
        sends, arrivals = [], []
        k = 0
        for r in range(1, N_DEV):
            peer = (_flip(x, r & 4), _flip(y, r & 2), _flip(c, r & 1))
            peer_chip, peer_idx = 2 * peer[0] + peer[1], _block_index(*peer)
            pairs = [(alls[b].at[mine_idx], all_outs[b].at[mine_idx], all_outs[b].at[peer_idx])
                     for b in range(m)]
            if not r & 1:
                pairs += [(ins[a].at[peer_chip], outs[a].at[my_chip], outs[a].at[peer_chip])
                          for a in range(n)]
            for src, there, here in pairs:
                sems = dict(send_sem=send_sems.at[k], recv_sem=recv_sems.at[k], device_id=peer,
                            device_id_type=pl.DeviceIdType.MESH)
                sends.append(pltpu.make_async_remote_copy(src_ref=src, dst_ref=there, **sems))
                arrivals.append(pltpu.make_async_remote_copy(src_ref=src, dst_ref=here, **sems))
                k += 1
        for cp in sends:
            cp.start()
        for cp in arrivals:
            cp.wait_recv()
        for cp in sends:
            cp.wait_send()
        for cp in local:
            cp.wait()

    n_copies = 7 * m + 3 * n
    return pl.pallas_call(
        body, name=name,
        in_specs=[_ANY] * (n + m), out_specs=[_ANY] * (n + m),
        out_shape=[jax.ShapeDtypeStruct(v.shape, v.dtype) for v in by_chip + to_all],
        scratch_shapes=[pltpu.SemaphoreType.DMA((n_copies,)), pltpu.SemaphoreType.DMA((n_copies,)),
                        pltpu.SemaphoreType.DMA((n + m,))],
    )(*by_chip, *to_all)


class _Rider(NamedTuple):
    srcs: list
    lands: list
    scatter: bool
    part: list
    relay: bool = False


def _window(ref, part):
    if part is None:
        return ref
    dim, start, size = part
    return ref.at[(slice(None),) * dim + (pl.ds(start, size),)]


def _relay_copies(srcs, lands, send_sems, recv_sems, rider):
    x, y, c = _my_place()
    me, sibling = (x, y, c), (x, y, 1 - c)
    chips = [(1 - x, y), (x, 1 - y), (1 - x, 1 - y)]
    first, from_chips, passed, last = [], [], [], []
    for a in range(len(srcs)):
        def copy(k, block, to, src=None, a=a):
            dst = _window(lands[a].at[_block_index(*block)], rider.part[a])
            return pltpu.make_async_remote_copy(
                src_ref=dst if src is None else src, dst_ref=dst,
                send_sem=send_sems.at[7 * a + k], recv_sem=recv_sems.at[7 * a + k],
                device_id=to, device_id_type=pl.DeviceIdType.MESH)

        mine = _window(srcs[a], rider.part[a])
        first.append(copy(0, me, sibling, src=mine))
        first += [copy(1 + j, me, (*chip, c), src=mine) for j, chip in enumerate(chips)]
        from_chips += [copy(1 + j, (*chip, c), me) for j, chip in enumerate(chips)]
        passed += [copy(4 + j, (*chip, c), sibling) for j, chip in enumerate(chips)]
        last.append(copy(0, sibling, me))
        last += [copy(4 + j, (*chip, 1 - c), me) for j, chip in enumerate(chips)]
    return first, from_chips, passed, last


def _new_lands(srcs, scatter):
    return [lax.empty(v.shape if scatter else (N_DEV,) + v.shape, v.dtype) for v in srcs]


def _rider_copies(srcs, lands, send_sems, recv_sems, rider):
    x, y, c = _my_place()
    mine_idx = _block_index(x, y, c)

    def window(ref, a):
        return _window(ref, rider.part[a])

    sends, arrivals = [], []
    for r in range(1, N_DEV):
        peer = (_flip(x, r & 4), _flip(y, r & 2), _flip(c, r & 1))
        peer_idx = _block_index(*peer)
        for a in range(len(srcs)):
            src = window(srcs[a].at[peer_idx] if rider.scatter else srcs[a], a)
            k = 7 * a + r - 1
            sems = dict(send_sem=send_sems.at[k], recv_sem=recv_sems.at[k],
                        device_id=peer, device_id_type=pl.DeviceIdType.MESH)
            sends.append(pltpu.make_async_remote_copy(
                src_ref=src, dst_ref=window(lands[a].at[mine_idx], a), **sems))
            arrivals.append(pltpu.make_async_remote_copy(
                src_ref=src, dst_ref=window(lands[a].at[peer_idx], a), **sems))
    return sends, arrivals


def _ride(call, rider):
    call = dict(call)
    body, grid = call.pop("body"), call["grid"]
    operands = call.pop("operands")
    if rider is None:
        return list(pl.pallas_call(body, **call)(*operands)), None
    n_in, n_out = len(call["in_specs"]), len(call["out_specs"])
    n_scratch = len(call["scratch_shapes"])
    m = len(rider.srcs)

    def riding(*refs):
        main_in, srcs, lands = refs[:n_in], refs[n_in:n_in + m], refs[n_in + m:n_in + 2 * m]
        main_out = refs[n_in + 2 * m:n_in + 2 * m + n_out]
        rest = refs[n_in + 3 * m + n_out:]
        send_sems, recv_sems = rest[n_scratch:]
        at = [pl.program_id(d) for d in range(len(grid))]
        first = functools.reduce(jnp.logical_and, [p == 0 for p in at])
        last = functools.reduce(jnp.logical_and, [p == g - 1 for p, g in zip(at, grid)])
        if rider.relay:
            sends, from_chips, passed, arrivals = _relay_copies(
                srcs, lands, send_sems, recv_sems, rider)
            step, steps = 0, 1
            for p, g in zip(at, grid):
                step, steps = step * g + p, steps * g
            assert steps >= 2, "a relayed gather needs a later grid step to pass blocks on"

            @pl.when(step == (3 * steps) // 4)
            def _():
                for cp in from_chips:
                    cp.wait_recv()
                for cp in passed:
                    cp.start()
        else:
            sends, arrivals = _rider_copies(srcs, lands, send_sems, recv_sems, rider)
            passed = []

        @pl.when(first)
        def _():
            for cp in sends:
                cp.start()

        body(*main_in, *main_out, *rest[:n_scratch])

        @pl.when(last)
        def _():
            for cp in arrivals:
                cp.wait_recv()
            for cp in sends + passed:
                cp.wait_send()

    call["in_specs"] = list(call["in_specs"]) + [_ANY] * (2 * m)
    call["out_specs"] = list(call["out_specs"]) + [_ANY] * m
    call["out_shape"] = list(call["out_shape"]) + [
        jax.ShapeDtypeStruct(v.shape, v.dtype) for v in rider.lands]
    call["scratch_shapes"] = list(call["scratch_shapes"]) + [
        pltpu.SemaphoreType.DMA((7 * m,)), pltpu.SemaphoreType.DMA((7 * m,))]
    call["input_output_aliases"] = {n_in + m + a: n_out + a for a in range(m)}
    outs = pl.pallas_call(riding, **call)(*operands, *rider.srcs, *rider.lands)
    return list(outs[:n_out]), list(outs[n_out:])


def _sum_parts(p_ref):
    g = p_ref[0].astype(F32)
    for k in range(1, p_ref.shape[0]):
        g = g + p_ref[k].astype(F32)
    return g


def _adamw(g, w, m, v):
    c1 = 1.0 / (1.0 - ADAM_B1 ** ADAM_STEP)
    c2 = 1.0 / (1.0 - ADAM_B2 ** ADAM_STEP)
    nm = ADAM_B1 * m + (1.0 - ADAM_B1) * g
    nv = ADAM_B2 * v + (1.0 - ADAM_B2) * (g * g)
    return nm, nv, -ADAM_LR * ((nm * c1) / (jnp.sqrt(nv * c2) + ADAM_EPS) + ADAM_WD * w)


def _adamw_w_in(parts, w, m, v, name):
    ncol_blk, depth, nfeat = w.shape
    cols = 256

    def body(*refs):
        p_refs = refs[:depth]
        w_ref, m_ref, v_ref, g_ref, d_ref, nm_ref, nv_ref = refs[depth:]
        for l in range(depth):
            g = _sum_parts(p_refs[l])
            nm, nv, d = _adamw(g, w_ref[:, l, :], m_ref[:, l, :], v_ref[:, l, :])
            g_ref[:, l, :] = g
            nm_ref[:, l, :] = nm
            nv_ref[:, l, :] = nv
            d_ref[:, l, :] = d

    blk = pl.BlockSpec((ncol_blk, depth, cols), lambda j: (0, 0, j))
    return pl.pallas_call(
        body, name=name, grid=(nfeat // cols,),
        in_specs=[pl.BlockSpec((p.shape[0], ncol_blk, cols), lambda j: (0, 0, j)) for p in parts]
        + [blk] * 3,
        out_specs=[blk] * 4,
        out_shape=[jax.ShapeDtypeStruct(w.shape, F32)] * 4,
        compiler_params=_ARB1,
    )(*parts, w, m, v)


def _adamw_sum(parts, w, m, v, tile, name):
    depth, nrow, ncol = w.shape
    rows, cols = tile

    def body(*refs):
        p_refs = refs[:depth]
        w_ref, m_ref, v_ref, g_ref, d_ref, nm_ref, nv_ref = refs[depth:]
        layer = pl.program_id(0)
        for l in range(depth):
            @pl.when(layer == l)
            def _(p_ref=p_refs[l]):
                g = _sum_parts(p_ref)
                nm, nv, d = _adamw(g, w_ref[...], m_ref[...], v_ref[...])
                g_ref[...] = g
                nm_ref[...] = nm
                nv_ref[...] = nv
                d_ref[...] = d

    def part_spec(l):
        return pl.BlockSpec((parts[l].shape[0], rows, cols), lambda q, i, j: (
            0, jnp.where(q == l, i, 0), jnp.where(q == l, j, 0)))

    blk = pl.BlockSpec((None, rows, cols), lambda q, i, j: (q, i, j))
    return pl.pallas_call(
        body, name=name, grid=(depth, nrow // rows, ncol // cols),
        in_specs=[part_spec(l) for l in range(depth)] + [blk, blk, blk],
        out_specs=[blk] * 4,
        out_shape=[jax.ShapeDtypeStruct(w.shape, F32)] * 4,
        compiler_params=pltpu.CompilerParams(
            dimension_semantics=("arbitrary", "arbitrary", "arbitrary")),
    )(*parts, w, m, v)


def _adamw_small(parts, ws, ms, vs, name):
    n = len(parts)

    def body(*refs):
        p, w, m, v = (refs[a * n:(a + 1) * n] for a in range(4))
        out = refs[4 * n:]
        for a in range(n):
            g = _sum_parts(p[a])
            nm, nv, d = _adamw(g, w[a][...], m[a][...], v[a][...])
            out[a][...] = g
            out[n + a][...] = d
            out[2 * n + a][...] = nm
            out[3 * n + a][...] = nv

    vmem = pl.BlockSpec(memory_space=pltpu.VMEM)
    outs = pl.pallas_call(
        body, name=name, in_specs=[vmem] * (4 * n), out_specs=[vmem] * (4 * n),
        out_shape=[jax.ShapeDtypeStruct(w.shape, F32) for w in ws] * 4,
    )(*parts, *ws, *ms, *vs)
    return [list(outs[a * n:(a + 1) * n]) for a in range(4)]


def _misc_rows(b_forget, extra=None):
    tile = jnp.pad(b_forget, ((0, 6), (0, LANES - FOX_HEADS)))
    return tile if extra is None else tile.at[2].set(extra)


def kernel(x, mem, norm_w, w_in, b_forget, mem_norm_w, w_mem_kv, out_norm_w, w_out, final_norm_w, loss_target, m_norm_w, m_w_in, m_b_forget, m_mem_norm_w, m_w_mem_kv, m_out_norm_w, m_w_out, m_final_norm_w, v_norm_w, v_w_in, v_b_forget, v_mem_norm_w, v_w_mem_kv, v_out_norm_w, v_w_out, v_final_norm_w):
    kv_rows = w_mem_kv.shape[1]
    out_rows = w_out.shape[1]
    me = _block_index(*_my_place())

    def shards(l):
        return [w_in[l].T.astype(BF16), w_mem_kv[l].astype(BF16), w_out[l].astype(BF16)]


    def full_kv_out(g_kv, g_out):
        return g_kv.reshape(D_MODEL, 2 * MEM_W), g_out.reshape(MIX_W, D_MODEL)

    def in_blocks(g):
        segments = [(0, 3072, 0), (3072, 3080, QKV_W + MIX_W), (3080, 3336, 3072),
                    (3336, IN_W, QKV_W)]

        def block(k):
            lo, hi = k * SHARD_W, (k + 1) * SHARD_W
            pieces = [g[at + max(lo, a) - a:at + min(hi, b) - a]
                      for a, b, at in segments if max(lo, a) < min(hi, b)]
            return jnp.concatenate(pieces, axis=0).astype(BF16)

        return jnp.stack([block(k) for k in range(N_DEV)])

    def kv_blocks(g):
        return g.reshape(N_DEV, kv_rows, 2 * MEM_W).astype(BF16)

    def out_blocks(g):
        return g.reshape(N_DEV, out_rows, D_MODEL).astype(BF16)

    def with_own(land, own):
        return lax.dynamic_update_slice(land, own[None], (me,) + (0,) * own.ndim)

    def with_own_of(land, blocked):
        return lax.dynamic_update_slice(land, lax.dynamic_slice_in_dim(blocked, me, 1, axis=0),
                                        (me,) + (0,) * (land.ndim - 1))

    def row(v):
        return v.reshape(1, -1)

    def cols(first, size):
        return (1, first, size)

    fwd_split, bwd_split = 4 * LANES, (5 * LANES, 6 * LANES)

    s_in0, s_kv0, s_out0 = shards(0)
    (g_in0,) = _all_gather_weights([s_in0], "all_gather_l0")
    w_r0 = _assemble_w_in(g_in0, "assemble_w_in_0")
    s_in1, s_kv1, s_out1 = shards(1)
    x1, saved0, (l_in1, _, _), (wkv0, wout0) = _layer_fwd(
        x[0], mem[0], row(norm_w[0]), w_r0, b_forget[0], row(mem_norm_w[0]),
        lambda lands: full_kv_out(with_own(lands[1], s_kv0), with_own(lands[2], s_out0)),
        row(out_norm_w[0]), 0,
        travel=([s_in1, s_kv0, s_out0], False,
                [([0], [cols(0, fwd_split)]), ([0], [cols(fwd_split, D_MODEL - fwd_split)]),
                 ([1, 2], [None, None])]))
    w_r1 = _assemble_w_in(with_own(l_in1, s_in1), "assemble_w_in_1")
    x2, saved1, _, (wkv1, wout1) = _layer_fwd(
        x1, mem[0], row(norm_w[1]), w_r1, b_forget[1], row(mem_norm_w[1]),
        lambda lands: full_kv_out(with_own(lands[0], s_kv1), with_own(lands[1], s_out1)),
        row(out_norm_w[1]), 1, travel=([s_kv1, s_out1], False, [([0, 1], [None, None]), None]))

    dx2, loss_part, dfnw = _final_fwd_bwd(x2, row(final_norm_w), loss_target[0], "final_fwd_bwd")

    dx1, gr1, (l_out1,) = _layer_bwd(
        dx2, saved1, mem[0], w_r1, wkv1, wout1, 1,
        travel=lambda dwout: ([out_blocks(dwout)], True, [([0], [None]), None]))
    p_in1, p_kv1 = in_blocks(gr1["w_r"]), kv_blocks(gr1["w_mem_kv"])
    grad_x, gr0, (l_in1, l_kv1, l_out0) = _layer_bwd(
        dx1, saved0, mem[0], w_r0, wkv0, wout0, 0,
        travel=lambda dwout: ([p_in1, p_kv1, out_blocks(dwout)], True,
                              [([0, 1], [cols(0, bwd_split[0]), None]),
                               ([0, 2], [cols(bwd_split[0], bwd_split[1] - bwd_split[0]), None]),
                               ([0], [cols(bwd_split[1], D_MODEL - bwd_split[1])])]))
    r_out1 = with_own_of(l_out1, out_blocks(gr1["w_out"]))
    r_in1, r_kv1 = with_own_of(l_in1, p_in1), with_own_of(l_kv1, p_kv1)
    r_out0 = with_own_of(l_out0, out_blocks(gr0["w_out"]))

    def both(name):
        return jnp.stack([gr0[name], gr1[name]])

    small = [both("norm_w"), both("mem_norm_w"), both("out_norm_w"), dfnw,
             _misc_rows(both("b_forget"), loss_part[0])]
    p_small = [jnp.broadcast_to(v[None], (N_DEV,) + v.shape) for v in small]
    by_core = [v.reshape((N_CHIP, 2) + v.shape[1:])
               for v in (in_blocks(gr0["w_r"]), kv_blocks(gr0["w_mem_kv"]))]
    from_sibling = _pair_swap(by_core, "grads_l0_pair_swap")
    core = lax.axis_index("c").astype(jnp.int32).reshape(1)
    chip_sums = _pair_add(by_core, from_sibling, core, "grads_l0_pair_add")
    r_in0, r_kv0, *r_small = _chip_exchange(chip_sums, p_small, "exchange_grads_l0")

    def view(v):
        return jnp.transpose(v, (2, 0, 1))

    g_w_in, d_w_in, nm_w_in, nv_w_in = [jnp.transpose(v, (1, 2, 0)) for v in _adamw_w_in(
        [r_in0, r_in1], view(w_in), view(m_w_in), view(v_w_in), "adamw_w_in")]
    g_w_kv, d_w_kv, nm_w_kv, nv_w_kv = _adamw_sum(
        [r_kv0, r_kv1], w_mem_kv, m_w_mem_kv, v_w_mem_kv, (kv_rows, 2 * MEM_W), "adamw_w_mem_kv")
    g_w_out, d_w_out, nm_w_out, nv_w_out = _adamw_sum(
        [r_out0, r_out1], w_out, m_w_out, v_w_out, (out_rows, D_MODEL), "adamw_w_out")
    def smalls(nw, mnw, onw, fnw, b):
        return [nw, mnw, onw, row(fnw), _misc_rows(b)]

    small_out = _adamw_small(
        r_small, smalls(norm_w, mem_norm_w, out_norm_w, final_norm_w, b_forget),
        smalls(m_norm_w, m_mem_norm_w, m_out_norm_w, m_final_norm_w, m_b_forget),
        smalls(v_norm_w, v_mem_norm_w, v_out_norm_w, v_final_norm_w, v_b_forget), "adamw_small")
    (g_nw, g_mnw, g_onw, g_fnw, g_b), (d_nw, d_mnw, d_onw, d_fnw, d_b), \
        (nm_nw, nm_mnw, nm_onw, nm_fnw, nm_b), (nv_nw, nv_mnw, nv_onw, nv_fnw, nv_b) = [
            (nw, mnw, onw, fnw[0], misc[:2, :FOX_HEADS]) for nw, mnw, onw, fnw, misc in small_out]
    loss = small_out[0][4][2, 0]

    return (loss, grad_x[None],
            g_nw, g_w_in, g_b, g_mnw, g_w_kv, g_onw, g_w_out, g_fnw,
            d_nw, d_w_in, d_b, d_mnw, d_w_kv, d_onw, d_w_out, d_fnw,
            nm_nw, nm_w_in, nm_b, nm_mnw, nm_w_kv, nm_onw, nm_w_out, nm_fnw,
            nv_nw, nv_w_in, nv_b, nv_mnw, nv_w_kv, nv_onw, nv_w_out, nv_fnw)
```

```python
import functools
from typing import NamedTuple

import jax
import jax.numpy as jnp
from jax import lax
from jax.experimental import pallas as pl
from jax.experimental.pallas import tpu as pltpu

F32 = jnp.float32
BF16 = jnp.bfloat16

N_DEV = 8
D_MODEL = 1024
HEAD_DIM = 64
LANES = 128
SB_W = 512
FX_W = 512
MEM_W = 256
MIX_W = 1280
FOX_HEADS = 8
IN_W = 4616
SHARD_W = IN_W // N_DEV
QKV_W = 3 * SB_W + 3 * FX_W + MEM_W
FL_PAD = 256
GF_W = MIX_W + FL_PAD
WR_W = QKV_W + GF_W
EPS = 1e-6
T = 256
QPS = 4
TM = 256
TQM = 512
Q_SCALE = 0.125
NEG = -1e30
UNDERFLOW = -110.0
NORM_SLACK = 1.01

ADAM_LR = 0.001
ADAM_B1 = 0.9
ADAM_B2 = 0.999
ADAM_EPS = 1e-08
ADAM_WD = 0.01
ADAM_STEP = 10


_NT = (((1,), (1,)), ((), ()))
_TN = (((0,), (0,)), ((), ()))

_ARB1 = pltpu.CompilerParams(dimension_semantics=("arbitrary",))
_ARB2 = pltpu.CompilerParams(dimension_semantics=("arbitrary", "arbitrary"))


def _dot(a, b):
    return jnp.dot(a, b, preferred_element_type=F32)


def _dot_nt(a, b):
    return lax.dot_general(a, b, _NT, preferred_element_type=F32)


def _dot_tn(a, b):
    return lax.dot_general(a, b, _TN, preferred_element_type=F32)


def _split2(x):
    hi = x.astype(BF16)
    lo = (x - hi.astype(F32)).astype(BF16)
    return hi, lo


def _stack2(u):
    return jnp.concatenate([u, u], axis=0)


def _cum2(x, u2):
    hi, lo = _split2(x)
    return _dot(jnp.concatenate([hi, lo], axis=1), u2)


def _tri3(tri, x, dot=None):
    dot = dot or _dot
    hi = x.astype(BF16)
    r1 = x - hi.astype(F32)
    mid = r1.astype(BF16)
    lo = (r1 - mid.astype(F32)).astype(BF16)
    return dot(tri, hi) + dot(tri, mid) + dot(tri, lo)


def _iota2(shape, dim):
    return lax.broadcasted_iota(jnp.int32, shape, dim)


def _head_block_diag():
    r = _iota2((LANES, LANES), 0) // HEAD_DIM
    c = _iota2((LANES, LANES), 1) // HEAD_DIM
    return _stack2(jnp.where(r == c, 1.0, 0.0).astype(BF16))


def _head_mean(x, bd):
    return _cum2(x, bd) * (1.0 / HEAD_DIM)


def _sigmoid(x):
    return 1.0 / (1.0 + jnp.exp(-x))


def _log_sigmoid(x):
    return jnp.minimum(x, 0.0) - jnp.log(1.0 + jnp.exp(-jnp.abs(x)))


def _running_top(r_ref):
    return jnp.max(jnp.maximum(r_ref[0], r_ref[1]))


def _fox_tiles_left(i, pair, nq, fb_ref, tile):
    def bound(j):
        b = []
        for h in range(2):
            head = 2 * pair + h
            b.append(2.0 * NORM_SLACK * fb_ref[2 * nq + i, head] * fb_ref[3 * nq, head]
                     + fb_ref[2 * i, head] - fb_ref[2 * j + 1, head])
        return jnp.maximum(b[0], b[1])

    def more(j):
        return jnp.logical_and(j >= 0, bound(jnp.maximum(j, 0)) > UNDERFLOW)

    def step(j):
        tile(j, False)
        return j - 1

    return lax.while_loop(more, step, i - 1)


def _pair_masks():
    lane = _iota2((1, LANES), 1)
    return lane < HEAD_DIM


def _split_pair(x, m0):
    zero = jnp.zeros_like(x)
    return jnp.where(m0, x, zero), jnp.where(m0, zero, x)


def _inproj_fwd(x, nw, w_r, name, rider=None):
    s = x.shape[0]

    def body(x_ref, nw_ref, w_ref, qkv_ref, gf_ref):
        xv = x_ref[...]
        r = lax.rsqrt(jnp.mean(xv * xv, axis=-1, keepdims=True) + EPS)
        h = (xv * r * nw_ref[...]).astype(BF16)
        for c in range(0, QKV_W, 256):
            qkv_ref[:, c:c + 256] = _dot_nt(h, w_ref[c:c + 256, :]).astype(BF16)
        for c in range(0, GF_W, 256):
            gf_ref[:, c:c + 256] = _dot_nt(h, w_ref[QKV_W + c:QKV_W + c + 256, :])

    (qkv, gf), lands = _ride(dict(
        body=body, name=name, grid=(s // TM,),
        in_specs=[pl.BlockSpec((TM, D_MODEL), lambda i: (i, 0)),
                  pl.BlockSpec((1, D_MODEL), lambda i: (0, 0)),
                  pl.BlockSpec((WR_W, D_MODEL), lambda i: (0, 0))],
        out_specs=[pl.BlockSpec((TM, QKV_W), lambda i: (i, 0)),
                   pl.BlockSpec((TM, GF_W), lambda i: (i, 0))],
        out_shape=[jax.ShapeDtypeStruct((s, QKV_W), BF16), jax.ShapeDtypeStruct((s, GF_W), F32)],
        scratch_shapes=[], compiler_params=_ARB1, operands=[x, nw, w_r]), rider)
    return qkv, gf, lands


def _fox_prep_fwd(gf, qkv, bpad, name):
    s = gf.shape[0]
    nq = s // T
    nrow = -(-(3 * nq + 1) // 8) * 8

    def body(fl_ref, q_ref, k_ref, b_ref, fq_ref, fr_ref, fb_ref):
        tri = jnp.where(_iota2((T, T), 0) >= _iota2((T, T), 1), 1.0, 0.0).astype(BF16)
        m0 = _pair_masks()
        lane = _iota2((1, LANES), 1)
        norms = [jnp.zeros((1, LANES), F32) for _ in range(nq + 1)]
        same_head = (_iota2((LANES, LANES), 0) // HEAD_DIM) == (_iota2((LANES, LANES), 1) // HEAD_DIM)
        bd = jnp.where(same_head, 1.0, 0.0).astype(BF16)
        for p in range(FOX_HEADS // 2):
            cols = slice(p * LANES, (p + 1) * LANES)
            q = (q_ref[:, cols] * jnp.asarray(Q_SCALE, BF16)).astype(F32)
            k = k_ref[:, cols].astype(F32)
            qn = _dot((q * q).astype(BF16), bd)
            kn = _dot((k * k).astype(BF16), bd)
            tops = [jnp.max(qn[j * T:(j + 1) * T], axis=0, keepdims=True) for j in range(nq)]
            tops.append(jnp.max(kn, axis=0, keepdims=True))
            tops = [jnp.sqrt(top) for top in tops]
            for h in range(2):
                at = h * HEAD_DIM
                norms = [jnp.where(lane == 2 * p + h, top[:, at:at + 1], row)
                         for top, row in zip(tops, norms)]
        for j in range(nq + 1):
            fb_ref[2 * nq + j:2 * nq + j + 1, :] = norms[j]
        fb_ref[3 * nq + 1:, :] = jnp.zeros((nrow - 3 * nq - 1, LANES), F32)
        carry = jnp.zeros((1, LANES), F32)
        for blk in range(s // T):
            rows = slice(blk * T, (blk + 1) * T)
            lf = _log_sigmoid(fl_ref[rows, :] + b_ref[...])
            c = _tri3(tri, lf) + carry
            carry = c[T - 1:T, :]
            for p in range(FOX_HEADS // 2):
                fq_ref[rows, p * LANES:(p + 1) * LANES] = jnp.where(
                    m0, c[:, 2 * p:2 * p + 1], c[:, 2 * p + 1:2 * p + 2])
            fr_ref[:, rows] = c.T[0:FOX_HEADS, :]
            fb_ref[2 * blk:2 * blk + 1, :] = c[0:1, :]
            fb_ref[2 * blk + 1:2 * blk + 2, :] = carry

    base = 3 * SB_W // FX_W
    return pl.pallas_call(
        body, name=name, grid=(1,),
        in_specs=[pl.BlockSpec((s, LANES), lambda i: (0, MIX_W // LANES)),
                  pl.BlockSpec((s, FX_W), lambda i: (0, base)),
                  pl.BlockSpec((s, FX_W), lambda i: (0, base + 1)),
                  pl.BlockSpec((1, LANES), lambda i: (0, 0))],
        out_specs=[pl.BlockSpec((s, FX_W), lambda i: (0, 0)),
                   pl.BlockSpec((FOX_HEADS, s), lambda i: (0, 0)),
                   pl.BlockSpec((nrow, LANES), lambda i: (0, 0))],
        out_shape=[jax.ShapeDtypeStruct((s, FX_W), F32), jax.ShapeDtypeStruct((FOX_HEADS, s), F32),
                   jax.ShapeDtypeStruct((nrow, LANES), F32)],
        compiler_params=_ARB1,
    )(gf, qkv, qkv, bpad)


def _sb_fwd(qkv, name, rider=None):
    s = qkv.shape[0]

    def body(q_ref, k_ref, v_ref, o_ref, acc_ref, r_ref, as_ref):
        m0 = _pair_masks()
        strict = _iota2((T, T), 0) > _iota2((T, T), 1)
        u2 = _stack2(jnp.where(strict, 1.0, 0.0).astype(BF16))
        hs = range(2)

        def query_tile(i, rows):
            qh = _split_pair(q_ref[rows, :] * jnp.asarray(Q_SCALE, BF16), m0)
            acc_ref[...] = jnp.zeros_like(acc_ref)
            r_ref[...] = jnp.zeros_like(r_ref)

            def flush(j):
                v = v_ref[pl.ds(pl.multiple_of(j * T, T), T), :]
                for h in hs:
                    acc_ref[h] += _dot(as_ref[h], v)

            def tile(j, diag):
                k = k_ref[pl.ds(pl.multiple_of(j * T, T), T), :]
                z = [_dot_nt(qh[h], k) for h in hs]
                if not diag:
                    flush(j + 1)
                la = [jnp.minimum(z[h], 0.0) - jnp.log(1.0 + jnp.exp(-jnp.abs(z[h]))) for h in hs]
                lf = [la[h] - z[h] for h in hs]
                if diag:
                    lf = [jnp.where(strict, lf[h], 0.0) for h in hs]
                cin = [_cum2(lf[h], u2) for h in hs]
                a = [jnp.exp(la[h] + cin[h] + r_ref[h]) for h in hs]
                if diag:
                    a = [jnp.where(strict, a[h], 0.0) for h in hs]
                for h in hs:
                    r_ref[h] += cin[h][:, 0:1] + lf[h][:, 0:1]
                    as_ref[h] = a[h].astype(BF16)

            tile(i, True)

            def more(state):
                j, top = state
                return jnp.logical_and(j >= 0, top > UNDERFLOW)

            def step(state):
                j, _ = state
                tile(j, False)
                return j - 1, _running_top(r_ref)

            j_left, _ = lax.while_loop(more, step, (i - 1, _running_top(r_ref)))
            flush(j_left + 1)
            o_ref[rows, :] = jnp.where(m0, acc_ref[0], acc_ref[1])

        for n in range(QPS):
            query_tile(QPS * pl.program_id(1) + n, slice(n * T, (n + 1) * T))

    nb = SB_W // LANES
    (ysb,), lands = _ride(dict(
        body=body, name=name, grid=(nb, s // (QPS * T)),
        in_specs=[pl.BlockSpec((QPS * T, LANES), lambda p, i: (i, p)),
                  pl.BlockSpec((s, LANES), lambda p, i: (0, nb + p)),
                  pl.BlockSpec((s, LANES), lambda p, i: (0, 2 * nb + p))],
        out_specs=[pl.BlockSpec((QPS * T, LANES), lambda p, i: (i, p))],
        out_shape=[jax.ShapeDtypeStruct((s, SB_W), F32)],
        scratch_shapes=[pltpu.VMEM((2, T, LANES), F32), pltpu.VMEM((2, T, 1), F32),
                        pltpu.VMEM((2, T, T), BF16)],
        compiler_params=_ARB2, operands=[qkv, qkv, qkv]), rider)
    return ysb, lands


def _fox_fwd(qkv, fqb, frow, fbounds, name, rider=None):
    s = qkv.shape[0]

    def body(q_ref, k_ref, v_ref, fq_ref, fr_ref, fb_ref, o_ref, lse_ref, acc_ref, m_ref, ps_ref):
        pair = pl.program_id(0)
        m0 = _pair_masks()
        causal = _iota2((T, T), 0) >= _iota2((T, T), 1)
        ones = jnp.ones((T, LANES), BF16)
        hs = range(2)

        def query_tile(i, rows):
            qh = _split_pair(q_ref[rows, :] * jnp.asarray(Q_SCALE, BF16), m0)
            fq = fq_ref[rows, :]
            fqh = (fq[:, 0:1], fq[:, HEAD_DIM:HEAD_DIM + 1])
            acc_ref[...] = jnp.zeros_like(acc_ref)
            m_ref[...] = jnp.full_like(m_ref, NEG)

            def flush(j):
                v = v_ref[pl.ds(pl.multiple_of(j * T, T), T), :]
                va2 = _stack2(jnp.concatenate([v, ones], axis=1))
                for h in hs:
                    acc_ref[h] += _dot(ps_ref[h], va2)

            def tile(j, diag):
                off = pl.multiple_of(j * T, T)
                k = k_ref[pl.ds(off, T), :]
                sc = [_dot_nt(qh[h], k) + fqh[h] - fr_ref[h:h + 1, pl.ds(off, T)] for h in hs]
                if not diag:
                    flush(j + 1)
                if diag:
                    sc = [jnp.where(causal, sc[h], NEG) for h in hs]
                m_new = [jnp.maximum(m_ref[h], jnp.max(sc[h], axis=1, keepdims=True)) for h in hs]
                p = [jnp.exp(sc[h] - m_new[h]) for h in hs]
                for h in hs:
                    acc_ref[h] = acc_ref[h] * jnp.exp(m_ref[h] - m_new[h])
                    m_ref[h] = m_new[h]
                    p_hi, p_lo = _split2(p[h])
                    ps_ref[h] = jnp.concatenate([p_hi, p_lo], axis=1)

            tile(i, True)
            j_left = _fox_tiles_left(i, pair, s // T, fb_ref, tile)
            flush(j_left + 1)
            acc = (acc_ref[0], acc_ref[1])
            o_ref[rows, :] = jnp.where(m0, acc[0][:, :LANES] / acc[0][:, LANES:],
                                       acc[1][:, :LANES] / acc[1][:, LANES:])
            lse_ref[rows, :] = jnp.where(m0, m_ref[0] + jnp.log(acc[0][:, LANES:]),
                                         m_ref[1] + jnp.log(acc[1][:, LANES:]))

        for n in range(QPS):
            query_tile(QPS * pl.program_id(1) + n, slice(n * T, (n + 1) * T))

    nb = FX_W // LANES
    base = 3 * SB_W // LANES
    (yfx, lse), lands = _ride(dict(
        body=body, name=name, grid=(nb, s // (QPS * T)),
        in_specs=[pl.BlockSpec((QPS * T, LANES), lambda p, i: (i, base + p)),
                  pl.BlockSpec((s, LANES), lambda p, i: (0, base + nb + p)),
                  pl.BlockSpec((s, LANES), lambda p, i: (0, base + 2 * nb + p)),
                  pl.BlockSpec((QPS * T, LANES), lambda p, i: (i, p)),
                  pl.BlockSpec((None, 2, s), lambda p, i: (p, 0, 0)),
                  pl.BlockSpec(memory_space=pltpu.SMEM)],
        out_specs=[pl.BlockSpec((QPS * T, LANES), lambda p, i: (i, p)),
                   pl.BlockSpec((QPS * T, LANES), lambda p, i: (i, p))],
        out_shape=[jax.ShapeDtypeStruct((s, FX_W), F32), jax.ShapeDtypeStruct((s, FX_W), F32)],
        scratch_shapes=[pltpu.VMEM((2, T, 2 * LANES), F32), pltpu.VMEM((2, T, 1), F32),
                        pltpu.VMEM((2, T, 2 * T), BF16)],
        compiler_params=_ARB2, operands=[qkv, qkv, qkv, fqb, frow, fbounds]), rider)
    return yfx, lse, lands


def _memkv_fwd(mem, mnw, wkv, name):
    n = mem.shape[0]

    def body(mem_ref, mnw_ref, w_ref, kv_ref):
        mv = mem_ref[...]
        r = lax.rsqrt(jnp.mean(mv * mv, axis=-1, keepdims=True) + EPS)
        hm = (mv * r * mnw_ref[...]).astype(BF16)
        kv_ref[...] = _dot(hm, w_ref[...]).astype(BF16)

    return pl.pallas_call(
        body, name=name, grid=(1,),
        in_specs=[pl.BlockSpec((n, D_MODEL), lambda i: (0, 0)),
                  pl.BlockSpec((1, D_MODEL), lambda i: (0, 0)),
                  pl.BlockSpec((D_MODEL, 2 * MEM_W), lambda i: (0, 0))],
        out_specs=pl.BlockSpec((n, 2 * MEM_W), lambda i: (0, 0)),
        out_shape=jax.ShapeDtypeStruct((n, 2 * MEM_W), BF16),
        compiler_params=_ARB1,
    )(mem, mnw, wkv)


def _mem_fwd(qkv, kv, name):
    s = qkv.shape[0]
    n = kv.shape[0]

    def body(q_ref, k_ref, v_ref, o_ref, lse_ref):
        m0 = _pair_masks()
        qh = _split_pair(q_ref[...] * jnp.asarray(Q_SCALE, BF16), m0)
        k = k_ref[...]
        v = v_ref[...]
        outs, lses = [], []
        for h in range(2):
            sc = _dot_nt(qh[h], k)
            mx = jnp.max(sc, axis=1, keepdims=True)
            p = jnp.exp(sc - mx)
            l = jnp.sum(p, axis=1, keepdims=True)
            outs.append(_dot(p.astype(BF16), v) / l)
            lses.append(mx + jnp.log(l))
        o_ref[...] = jnp.where(m0, outs[0], outs[1])
        lse_ref[...] = jnp.where(m0, lses[0], lses[1])

    nb = MEM_W // LANES
    base = (3 * SB_W + 3 * FX_W) // LANES
    return pl.pallas_call(
        body, name=name, grid=(nb, s // TQM),
        in_specs=[pl.BlockSpec((TQM, LANES), lambda p, i: (i, base + p)),
                  pl.BlockSpec((n, LANES), lambda p, i: (0, p)),
                  pl.BlockSpec((n, LANES), lambda p, i: (0, nb + p))],
        out_specs=[pl.BlockSpec((TQM, LANES), lambda p, i: (i, p)),
                   pl.BlockSpec((TQM, LANES), lambda p, i: (i, p))],
        out_shape=[jax.ShapeDtypeStruct((s, MEM_W), F32), jax.ShapeDtypeStruct((s, MEM_W), F32)],
        compiler_params=_ARB2,
    )(qkv, kv, kv)


def _mix_chunk(c, ysb_ref, yfx_ref, ym_ref):
    if c < SB_W // LANES:
        return ysb_ref[:, c * LANES:(c + 1) * LANES]
    c -= SB_W // LANES
    if c < FX_W // LANES:
        return yfx_ref[:, c * LANES:(c + 1) * LANES]
    c -= FX_W // LANES
    return ym_ref[:, c * LANES:(c + 1) * LANES]


def _outproj_fwd(ysb, yfx, ym, gf, onw, wout, x, name):
    s = x.shape[0]

    def body(ysb_ref, yfx_ref, ym_ref, g_ref, onw_ref, w_ref, x_ref, o_ref, yg_ref):
        bd = _head_block_diag()
        for c in range(MIX_W // LANES):
            sl = slice(c * LANES, (c + 1) * LANES)
            u = _mix_chunk(c, ysb_ref, yfx_ref, ym_ref)
            r = lax.rsqrt(_head_mean(u * u, bd) + EPS)
            g = g_ref[:, sl]
            yg_ref[:, sl] = (u * r * onw_ref[:, sl] * (g * _sigmoid(g))).astype(BF16)
        o_ref[...] = x_ref[...] + _dot(yg_ref[...], w_ref[...])

    return pl.pallas_call(
        body, name=name, grid=(s // TM,),
        in_specs=[pl.BlockSpec((TM, SB_W), lambda i: (i, 0)),
                  pl.BlockSpec((TM, FX_W), lambda i: (i, 0)),
                  pl.BlockSpec((TM, MEM_W), lambda i: (i, 0)),
                  pl.BlockSpec((TM, MIX_W), lambda i: (i, 0)),
                  pl.BlockSpec((1, MIX_W), lambda i: (0, 0)),
                  pl.BlockSpec((MIX_W, D_MODEL), lambda i: (0, 0)),
                  pl.BlockSpec((TM, D_MODEL), lambda i: (i, 0))],
        out_specs=pl.BlockSpec((TM, D_MODEL), lambda i: (i, 0)),
        out_shape=jax.ShapeDtypeStruct((s, D_MODEL), F32),
        scratch_shapes=[pltpu.VMEM((TM, MIX_W), BF16)],
        compiler_params=_ARB1,
    )(ysb, yfx, ym, gf, onw, wout, x)


def _final_fwd_bwd(x, fnw, target, name):
    s = x.shape[0]

    def body(x_ref, w_ref, t_ref, dx_ref, loss_ref, dw_ref):
        @pl.when(pl.program_id(0) == 0)
        def _():
            loss_ref[...] = jnp.zeros_like(loss_ref)
            dw_ref[...] = jnp.zeros_like(dw_ref)

        xv = x_ref[...]
        w = w_ref[...]
        r = lax.rsqrt(jnp.mean(xv * xv, axis=-1, keepdims=True) + EPS)
        xh = xv * r
        err = xh * w - t_ref[...]
        part = jnp.sum(jnp.sum(err * err, axis=1, keepdims=True), axis=0, keepdims=True)
        loss_ref[...] += part * (0.5 / D_MODEL)
        dy = err * (1.0 / D_MODEL)
        dw_ref[...] += jnp.sum(dy * xh, axis=0, keepdims=True)
        dxh = dy * w
        dx_ref[...] = r * (dxh - xh * jnp.mean(dxh * xh, axis=-1, keepdims=True))

    return pl.pallas_call(
        body, name=name, grid=(s // TM,),
        in_specs=[pl.BlockSpec((TM, D_MODEL), lambda i: (i, 0)),
                  pl.BlockSpec((1, D_MODEL), lambda i: (0, 0)),
                  pl.BlockSpec((TM, D_MODEL), lambda i: (i, 0))],
        out_specs=[pl.BlockSpec((TM, D_MODEL), lambda i: (i, 0)),
                   pl.BlockSpec((1, LANES), lambda i: (0, 0)),
                   pl.BlockSpec((1, D_MODEL), lambda i: (0, 0))],
        out_shape=[jax.ShapeDtypeStruct((s, D_MODEL), F32), jax.ShapeDtypeStruct((1, LANES), F32),
                   jax.ShapeDtypeStruct((1, D_MODEL), F32)],
        compiler_params=_ARB1,
    )(x, fnw, target)


def _outproj_bwd(dxo, wout, ysb, yfx, ym, gf, onw, name, rider=None):
    s = dxo.shape[0]

    def body(dx_ref, w_ref, ysb_ref, yfx_ref, ym_ref, g_ref, onw_ref,
             dysb_ref, dyfx_ref, dym_ref, dg_ref, dw_ref, donw_ref, yg_ref):
        @pl.when(pl.program_id(0) == 0)
        def _():
            dw_ref[...] = jnp.zeros_like(dw_ref)
            donw_ref[...] = jnp.zeros_like(donw_ref)

        dxb = dx_ref[...].astype(BF16)
        dyg = _dot_nt(dxb, w_ref[...])
        bd = _head_block_diag()
        for c in range(MIX_W // LANES):
            sl = slice(c * LANES, (c + 1) * LANES)
            u = _mix_chunk(c, ysb_ref, yfx_ref, ym_ref)
            r = lax.rsqrt(_head_mean(u * u, bd) + EPS)
            yn = u * r
            g = g_ref[:, sl]
            sg = _sigmoid(g)
            sil = g * sg
            onw = onw_ref[:, sl]
            e = dyg[:, sl]
            yg_ref[:, sl] = (yn * onw * sil).astype(BF16)
            donw_ref[:, sl] += jnp.sum(e * yn * sil, axis=0, keepdims=True)
            dg_ref[:, sl] = (e * yn * onw * (sg * (1.0 + g * (1.0 - sg)))).astype(BF16)
            dyn = e * onw * sil
            du = (r * (dyn - yn * _head_mean(dyn * yn, bd))).astype(BF16)
            if c < 4:
                dysb_ref[:, c * LANES:(c + 1) * LANES] = du
            elif c < 8:
                dyfx_ref[:, (c - 4) * LANES:(c - 3) * LANES] = du
            else:
                dym_ref[:, (c - 8) * LANES:(c - 7) * LANES] = du
        dw_ref[...] += _dot_tn(yg_ref[...], dxb)

    outs, lands = _ride(dict(
        body=body, name=name, grid=(s // TM,),
        in_specs=[pl.BlockSpec((TM, D_MODEL), lambda i: (i, 0)),
                  pl.BlockSpec((MIX_W, D_MODEL), lambda i: (0, 0)),
                  pl.BlockSpec((TM, SB_W), lambda i: (i, 0)),
                  pl.BlockSpec((TM, FX_W), lambda i: (i, 0)),
                  pl.BlockSpec((TM, MEM_W), lambda i: (i, 0)),
                  pl.BlockSpec((TM, MIX_W), lambda i: (i, 0)),
                  pl.BlockSpec((1, MIX_W), lambda i: (0, 0))],
        out_specs=[pl.BlockSpec((TM, SB_W), lambda i: (i, 0)),
                   pl.BlockSpec((TM, FX_W), lambda i: (i, 0)),
                   pl.BlockSpec((TM, MEM_W), lambda i: (i, 0)),
                   pl.BlockSpec((TM, MIX_W), lambda i: (i, 0)),
                   pl.BlockSpec((MIX_W, D_MODEL), lambda i: (0, 0)),
                   pl.BlockSpec((1, MIX_W), lambda i: (0, 0))],
        out_shape=[jax.ShapeDtypeStruct((s, SB_W), BF16), jax.ShapeDtypeStruct((s, FX_W), BF16),
                   jax.ShapeDtypeStruct((s, MEM_W), BF16), jax.ShapeDtypeStruct((s, MIX_W), BF16),
                   jax.ShapeDtypeStruct((MIX_W, D_MODEL), F32), jax.ShapeDtypeStruct((1, MIX_W), F32)],
        scratch_shapes=[pltpu.VMEM((TM, MIX_W), BF16)],
        compiler_params=_ARB1, operands=[dxo, wout, ysb, yfx, ym, gf, onw]), rider)
    return (*outs, lands)


def _row_dots(do, o, m0):
    prod = do.astype(F32) * o
    zero = jnp.zeros_like(prod)
    return (jnp.sum(jnp.where(m0, prod, zero), axis=1, keepdims=True),
            jnp.sum(jnp.where(m0, zero, prod), axis=1, keepdims=True))


def _sb_bwd(qkv, o, do, name, rider=None):
    s = qkv.shape[0]
    nq = s // T

    def body(q_ref, k_ref, v_ref, o_ref, do_ref, dq_ref, dk_ref, dv_ref,
             dqa_ref, dka_ref, dva_ref, rl_ref, rg_ref, dzs_ref, abs_ref):
        step_id = pl.program_id(1)

        @pl.when(step_id == 0)
        def _():
            dka_ref[...] = jnp.zeros_like(dka_ref)
            dva_ref[...] = jnp.zeros_like(dva_ref)

        m0 = _pair_masks()
        strict = _iota2((T, T), 0) > _iota2((T, T), 1)
        u2 = _stack2(jnp.where(strict, 1.0, 0.0).astype(BF16))
        hs = range(2)

        def query_tile(i, rows):
            qh = _split_pair(q_ref[rows, :] * jnp.asarray(Q_SCALE, BF16), m0)
            doh = _split_pair(do_ref[rows, :], m0)
            dsum = _row_dots(do_ref[rows, :], o_ref[rows, :], m0)
            dqa_ref[...] = jnp.zeros_like(dqa_ref)
            rl_ref[...] = jnp.zeros_like(rl_ref)
            rg_ref[...] = jnp.zeros_like(rg_ref)

            def flush(j):
                off = pl.multiple_of(j * T, T)
                k = k_ref[pl.ds(off, T), :]
                for h in hs:
                    dqa_ref[h] += _dot(dzs_ref[h], k)
                dka_ref[pl.ds(off, T), :] += (_dot_tn(dzs_ref[0], qh[0])
                                              + _dot_tn(dzs_ref[1], qh[1]))
                dva_ref[pl.ds(off, T), :] += (_dot_tn(abs_ref[0], doh[0])
                                              + _dot_tn(abs_ref[1], doh[1]))

            def tile(j, diag):
                off = pl.multiple_of(j * T, T)
                k = k_ref[pl.ds(off, T), :]
                v = v_ref[pl.ds(off, T), :]
                z = [_dot_nt(qh[h], k) for h in hs]
                da = [_dot_nt(doh[h], v) for h in hs]
                if not diag:
                    flush(j + 1)
                la = [jnp.minimum(z[h], 0.0) - jnp.log(1.0 + jnp.exp(-jnp.abs(z[h]))) for h in hs]
                lf = [la[h] - z[h] for h in hs]
                if diag:
                    lf = [jnp.where(strict, lf[h], 0.0) for h in hs]
                cin = [_cum2(lf[h], u2) for h in hs]
                a = [jnp.exp(la[h] + cin[h] + rl_ref[h]) for h in hs]
                if diag:
                    a = [jnp.where(strict, a[h], 0.0) for h in hs]
                ab = [a[h].astype(BF16) for h in hs]
                g = [ab[h].astype(F32) * da[h] for h in hs]
                gin = [_cum2(g[h], u2) for h in hs]
                dz = [g[h] - jnp.exp(la[h]) * ((dsum[h] - rg_ref[h]) - gin[h]) for h in hs]
                if diag:
                    dz = [jnp.where(strict, dz[h], 0.0) for h in hs]
                for h in hs:
                    rl_ref[h] += cin[h][:, 0:1] + lf[h][:, 0:1]
                    rg_ref[h] += gin[h][:, 0:1] + g[h][:, 0:1]
                    dzs_ref[h] = dz[h].astype(BF16)
                    abs_ref[h] = ab[h]

            tile(i, True)

            def more(state):
                j, top = state
                return jnp.logical_and(j >= 0, top > UNDERFLOW)

            def step(state):
                j, _ = state
                tile(j, False)
                return j - 1, _running_top(rl_ref)

            j_left, _ = lax.while_loop(more, step, (i - 1, _running_top(rl_ref)))
            flush(j_left + 1)
            dq_ref[rows, :] = (jnp.where(m0, dqa_ref[0], dqa_ref[1]) * Q_SCALE).astype(BF16)

        for n in range(QPS):
            query_tile(QPS * step_id + n, slice(n * T, (n + 1) * T))

        @pl.when(step_id == nq // QPS - 1)
        def _():
            dk_ref[...] = dka_ref[...].astype(BF16)
            dv_ref[...] = dva_ref[...].astype(BF16)

    nb = SB_W // LANES
    (dq, dk, dv), lands = _ride(dict(
        body=body, name=name, grid=(nb, nq // QPS),
        in_specs=[pl.BlockSpec((QPS * T, LANES), lambda p, i: (i, p)),
                  pl.BlockSpec((s, LANES), lambda p, i: (0, nb + p)),
                  pl.BlockSpec((s, LANES), lambda p, i: (0, 2 * nb + p)),
                  pl.BlockSpec((QPS * T, LANES), lambda p, i: (i, p)),
                  pl.BlockSpec((QPS * T, LANES), lambda p, i: (i, p))],
        out_specs=[pl.BlockSpec((QPS * T, LANES), lambda p, i: (i, p)),
                   pl.BlockSpec((s, LANES), lambda p, i: (0, p)),
                   pl.BlockSpec((s, LANES), lambda p, i: (0, p))],
        out_shape=[jax.ShapeDtypeStruct((s, SB_W), BF16)] * 3,
        scratch_shapes=[pltpu.VMEM((2, T, LANES), F32), pltpu.VMEM((s, LANES), F32),
                        pltpu.VMEM((s, LANES), F32), pltpu.VMEM((2, T, 1), F32),
                        pltpu.VMEM((2, T, 1), F32), pltpu.VMEM((2, T, T), BF16),
                        pltpu.VMEM((2, T, T), BF16)],
        compiler_params=_ARB2, operands=[qkv, qkv, qkv, o, do]), rider)
    return dq, dk, dv, lands


def _fox_bwd(qkv, fqb, frow, fbounds, o, lse, do, name, rider=None):
    s = qkv.shape[0]
    nq = s // T

    def body(q_ref, k_ref, v_ref, fq_ref, fr_ref, fb_ref, o_ref, lse_ref, do_ref,
             dq_ref, dk_ref, dv_ref, df_ref, dqa_ref, dka_ref, dva_ref, dfa_ref, dls_ref, pbs_ref):
        step_id = pl.program_id(1)

        @pl.when(step_id == 0)
        def _():
            dka_ref[...] = jnp.zeros_like(dka_ref)
            dva_ref[...] = jnp.zeros_like(dva_ref)
            dfa_ref[...] = jnp.zeros_like(dfa_ref)

        m0 = _pair_masks()
        causal = _iota2((T, T), 0) >= _iota2((T, T), 1)
        hs = range(2)

        def query_tile(i, rows):
            qh = _split_pair(q_ref[rows, :] * jnp.asarray(Q_SCALE, BF16), m0)
            doh = _split_pair(do_ref[rows, :], m0)
            dsum = _row_dots(do_ref[rows, :], o_ref[rows, :], m0)
            fq = fq_ref[rows, :]
            fqh = (fq[:, 0:1], fq[:, HEAD_DIM:HEAD_DIM + 1])
            lse = lse_ref[rows, :]
            lseh = (lse[:, 0:1], lse[:, HEAD_DIM:HEAD_DIM + 1])
            dqa_ref[...] = jnp.zeros_like(dqa_ref)

            def flush(j):
                off = pl.multiple_of(j * T, T)
                k = k_ref[pl.ds(off, T), :]
                for h in hs:
                    dqa_ref[h] += _dot(dls_ref[h], k)
                dka_ref[pl.ds(off, T), :] += (_dot_tn(dls_ref[0], qh[0])
                                              + _dot_tn(dls_ref[1], qh[1]))
                dva_ref[pl.ds(off, T), :] += (_dot_tn(pbs_ref[0], doh[0])
                                              + _dot_tn(pbs_ref[1], doh[1]))

            def tile(j, diag):
                off = pl.multiple_of(j * T, T)
                k = k_ref[pl.ds(off, T), :]
                v = v_ref[pl.ds(off, T), :]
                sc = [_dot_nt(qh[h], k) + fqh[h] - fr_ref[h:h + 1, pl.ds(off, T)] for h in hs]
                dp = [_dot_nt(doh[h], v) for h in hs]
                if not diag:
                    flush(j + 1)
                p = [jnp.exp(sc[h] - lseh[h]) for h in hs]
                if diag:
                    p = [jnp.where(causal, p[h], 0.0) for h in hs]
                dl = [p[h] * (dp[h] - dsum[h]) for h in hs]
                for h in hs:
                    dls_ref[h] = dl[h].astype(BF16)
                    pbs_ref[h] = p[h].astype(BF16)
                    dfa_ref[h:h + 1, pl.ds(off, T)] -= jnp.sum(dl[h], axis=0, keepdims=True)

            tile(i, True)
            j_left = _fox_tiles_left(i, pl.program_id(0), nq, fb_ref, tile)
            flush(j_left + 1)
            dq_ref[rows, :] = (jnp.where(m0, dqa_ref[0], dqa_ref[1]) * Q_SCALE).astype(BF16)

        for n in range(QPS):
            query_tile(QPS * step_id + n, slice(n * T, (n + 1) * T))

        @pl.when(step_id == nq // QPS - 1)
        def _():
            dk_ref[...] = dka_ref[...].astype(BF16)
            dv_ref[...] = dva_ref[...].astype(BF16)
            df_ref[...] = dfa_ref[...]

    nb = FX_W // LANES
    base = 3 * SB_W // LANES
    (dq, dk, dv, df), lands = _ride(dict(
        body=body, name=name, grid=(nb, nq // QPS),
        in_specs=[pl.BlockSpec((QPS * T, LANES), lambda p, i: (i, base + p)),
                  pl.BlockSpec((s, LANES), lambda p, i: (0, base + nb + p)),
                  pl.BlockSpec((s, LANES), lambda p, i: (0, base + 2 * nb + p)),
                  pl.BlockSpec((QPS * T, LANES), lambda p, i: (i, p)),
                  pl.BlockSpec((None, 2, s), lambda p, i: (p, 0, 0)),
                  pl.BlockSpec(memory_space=pltpu.SMEM),
                  pl.BlockSpec((QPS * T, LANES), lambda p, i: (i, p)),
                  pl.BlockSpec((QPS * T, LANES), lambda p, i: (i, p)),
                  pl.BlockSpec((QPS * T, LANES), lambda p, i: (i, p))],
        out_specs=[pl.BlockSpec((QPS * T, LANES), lambda p, i: (i, p)),
                   pl.BlockSpec((s, LANES), lambda p, i: (0, p)),
                   pl.BlockSpec((s, LANES), lambda p, i: (0, p)),
                   pl.BlockSpec((None, 2, s), lambda p, i: (p, 0, 0))],
        out_shape=[jax.ShapeDtypeStruct((s, FX_W), BF16)] * 3
        + [jax.ShapeDtypeStruct((nb, 2, s), F32)],
        scratch_shapes=[pltpu.VMEM((2, T, LANES), F32), pltpu.VMEM((s, LANES), F32),
                        pltpu.VMEM((s, LANES), F32), pltpu.VMEM((2, s), F32),
                        pltpu.VMEM((2, T, T), BF16), pltpu.VMEM((2, T, T), BF16)],
        compiler_params=_ARB2, operands=[qkv, qkv, qkv, fqb, frow, fbounds, o, lse, do]), rider)
    return dq, dk, dv, df, lands


def _fox_prep_bwd(dfrow, gf, bpad, name):
    s = gf.shape[0]

    def body(df_ref, fl_ref, b_ref, dfl_ref, db_ref):
        tri = jnp.where(_iota2((T, T), 0) <= _iota2((T, T), 1), 1.0, 0.0).astype(BF16)
        carry = jnp.zeros((1, LANES), F32)
        db = jnp.zeros((1, LANES), F32)
        fill = jnp.zeros((LANES - FOX_HEADS, T), F32)
        for blk in reversed(range(s // T)):
            rows = slice(blk * T, (blk + 1) * T)
            c = _tri3(tri, jnp.concatenate([df_ref[:, rows], fill], axis=0), _dot_nt) + carry
            carry = c[0:1, :]
            dfl = c / (1.0 + jnp.exp(fl_ref[rows, :] + b_ref[...]))
            dfl_ref[rows, :] = dfl.astype(BF16)
            db = db + jnp.sum(dfl, axis=0, keepdims=True)
        db_ref[...] = db

    return pl.pallas_call(
        body, name=name, grid=(1,),
        in_specs=[pl.BlockSpec((FOX_HEADS, s), lambda i: (0, 0)),
                  pl.BlockSpec((s, LANES), lambda i: (0, MIX_W // LANES)),
                  pl.BlockSpec((1, LANES), lambda i: (0, 0))],
        out_specs=[pl.BlockSpec((s, LANES), lambda i: (0, 0)),
                   pl.BlockSpec((1, LANES), lambda i: (0, 0))],
        out_shape=[jax.ShapeDtypeStruct((s, LANES), BF16), jax.ShapeDtypeStruct((1, LANES), F32)],
        compiler_params=_ARB1,
    )(dfrow, gf, bpad)


def _mem_bwd(qkv, kv, o, lse, do, name):
    s = qkv.shape[0]
    n = kv.shape[0]

    def body(q_ref, k_ref, v_ref, o_ref, lse_ref, do_ref, dq_ref, dk_ref, dv_ref):
        @pl.when(pl.program_id(1) == 0)
        def _():
            dk_ref[...] = jnp.zeros_like(dk_ref)
            dv_ref[...] = jnp.zeros_like(dv_ref)

        m0 = _pair_masks()
        qh = _split_pair(q_ref[...] * jnp.asarray(Q_SCALE, BF16), m0)
        doh = _split_pair(do_ref[...], m0)
        dsum = _row_dots(do_ref[...], o_ref[...], m0)
        lse = lse_ref[...]
        lseh = (lse[:, 0:1], lse[:, HEAD_DIM:HEAD_DIM + 1])
        k = k_ref[...]
        v = v_ref[...]
        dqs = []
        for h in range(2):
            p = jnp.exp(_dot_nt(qh[h], k) - lseh[h])
            dl = p * (_dot_nt(doh[h], v) - dsum[h])
            dlb = dl.astype(BF16)
            dqs.append(_dot(dlb, k))
            dk_ref[...] += _dot_tn(dlb, qh[h])
            dv_ref[...] += _dot_tn(p.astype(BF16), doh[h])
        dq_ref[...] = (jnp.where(m0, dqs[0], dqs[1]) * Q_SCALE).astype(BF16)

    nb = MEM_W // LANES
    base = (3 * SB_W + 3 * FX_W) // LANES
    return pl.pallas_call(
        body, name=name, grid=(nb, s // TQM),
        in_specs=[pl.BlockSpec((TQM, LANES), lambda p, i: (i, base + p)),
                  pl.BlockSpec((n, LANES), lambda p, i: (0, p)),
                  pl.BlockSpec((n, LANES), lambda p, i: (0, nb + p)),
                  pl.BlockSpec((TQM, LANES), lambda p, i: (i, p)),
                  pl.BlockSpec((TQM, LANES), lambda p, i: (i, p)),
                  pl.BlockSpec((TQM, LANES), lambda p, i: (i, p))],
        out_specs=[pl.BlockSpec((TQM, LANES), lambda p, i: (i, p)),
                   pl.BlockSpec((n, LANES), lambda p, i: (0, p)),
                   pl.BlockSpec((n, LANES), lambda p, i: (0, p))],
        out_shape=[jax.ShapeDtypeStruct((s, MEM_W), BF16), jax.ShapeDtypeStruct((n, MEM_W), F32),
                   jax.ShapeDtypeStruct((n, MEM_W), F32)],
        compiler_params=_ARB2,
    )(qkv, kv, kv, o, lse, do)


def _memkv_bwd(mem, mnw, wkv, dk, dv, name):
    n = mem.shape[0]

    def body(mem_ref, mnw_ref, w_ref, dk_ref, dv_ref, dw_ref, dmnw_ref):
        mv = mem_ref[...]
        r = lax.rsqrt(jnp.mean(mv * mv, axis=-1, keepdims=True) + EPS)
        mh = mv * r
        hm = (mh * mnw_ref[...]).astype(BF16)
        dkv = jnp.concatenate([dk_ref[...], dv_ref[...]], axis=1).astype(BF16)
        dw_ref[...] = _dot_tn(hm, dkv)
        dhm = _dot_nt(dkv, w_ref[...])
        dmnw_ref[...] = jnp.sum(dhm * mh, axis=0, keepdims=True)

    return pl.pallas_call(
        body, name=name, grid=(1,),
        in_specs=[pl.BlockSpec((n, D_MODEL), lambda i: (0, 0)),
                  pl.BlockSpec((1, D_MODEL), lambda i: (0, 0)),
                  pl.BlockSpec((D_MODEL, 2 * MEM_W), lambda i: (0, 0)),
                  pl.BlockSpec((n, MEM_W), lambda i: (0, 0)),
                  pl.BlockSpec((n, MEM_W), lambda i: (0, 0))],
        out_specs=[pl.BlockSpec((D_MODEL, 2 * MEM_W), lambda i: (0, 0)),
                   pl.BlockSpec((1, D_MODEL), lambda i: (0, 0))],
        out_shape=[jax.ShapeDtypeStruct((D_MODEL, 2 * MEM_W), F32),
                   jax.ShapeDtypeStruct((1, D_MODEL), F32)],
        compiler_params=_ARB1,
    )(mem, mnw, wkv, dk, dv)


def _inproj_bwd_dx(pieces, w_r, x, nw, dxo, name, rider=None):
    s = x.shape[0]
    n = len(pieces)
    widths = [p.shape[1] for p in pieces]

    def body(*refs):
        piece_refs = refs[:n]
        w_ref, x_ref, nw_ref, dxo_ref, dx_ref, h_ref, dnw_ref, dp_ref = refs[n:]

        @pl.when(pl.program_id(0) == 0)
        def _():
            dnw_ref[...] = jnp.zeros_like(dnw_ref)

        col = 0
        for r, wd in zip(piece_refs, widths):
            dp_ref[:, col:col + wd] = r[...]
            col += wd
        dp_ref[:, col:] = jnp.zeros((TM, WR_W - col), BF16)
        dh = _dot(dp_ref[...], w_ref[...])
        xv = x_ref[...]
        nw = nw_ref[...]
        r = lax.rsqrt(jnp.mean(xv * xv, axis=-1, keepdims=True) + EPS)
        xh = xv * r
        h_ref[...] = (xh * nw).astype(BF16)
        dnw_ref[...] += jnp.sum(dh * xh, axis=0, keepdims=True)
        dxh = dh * nw
        dx_ref[...] = r * (dxh - xh * jnp.mean(dxh * xh, axis=-1, keepdims=True)) + dxo_ref[...]

    (dx, h, dnw, dproj), lands = _ride(dict(
        body=body, name=name, grid=(s // TM,),
        in_specs=[pl.BlockSpec((TM, wd), lambda i: (i, 0)) for wd in widths]
        + [pl.BlockSpec((WR_W, D_MODEL), lambda i: (0, 0)),
           pl.BlockSpec((TM, D_MODEL), lambda i: (i, 0)),
           pl.BlockSpec((1, D_MODEL), lambda i: (0, 0)),
           pl.BlockSpec((TM, D_MODEL), lambda i: (i, 0))],
        out_specs=[pl.BlockSpec((None, TM, D_MODEL), lambda i: (0, i, 0)),
                   pl.BlockSpec((TM, D_MODEL), lambda i: (i, 0)),
                   pl.BlockSpec((1, D_MODEL), lambda i: (0, 0)),
                   pl.BlockSpec((TM, WR_W), lambda i: (i, 0))],
        out_shape=[jax.ShapeDtypeStruct((1, s, D_MODEL), F32), jax.ShapeDtypeStruct((s, D_MODEL), BF16),
                   jax.ShapeDtypeStruct((1, D_MODEL), F32), jax.ShapeDtypeStruct((s, WR_W), BF16)],
        scratch_shapes=[], compiler_params=_ARB1, operands=[*pieces, w_r, x, nw, dxo]), rider)
    return dx, h, dnw, dproj, lands


def _inproj_bwd_dw(h, dproj, name):
    s = dproj.shape[0]
    tn = 256

    def body(h_ref, dp_ref, dw_ref):
        dw_ref[...] = _dot_tn(dp_ref[...], h_ref[...]).astype(BF16)

    return pl.pallas_call(
        body, name=name, grid=(WR_W // tn,),
        in_specs=[pl.BlockSpec((s, D_MODEL), lambda j: (0, 0)),
                  pl.BlockSpec((s, tn), lambda j: (0, j))],
        out_specs=pl.BlockSpec((tn, D_MODEL), lambda j: (j, 0)),
        out_shape=jax.ShapeDtypeStruct((WR_W, D_MODEL), BF16),
        compiler_params=_ARB1,
    )(h, dproj)


def _rearrange_w_in(wt):
    pad = jnp.zeros((FL_PAD - FOX_HEADS,) + wt.shape[1:], wt.dtype)
    return jnp.concatenate([wt[:3072], wt[3080:3336], wt[3336:IN_W], wt[3072:3080], pad], axis=0)


_W_IN_SEGMENTS = [(0, 3072, 0), (3072, 3080, QKV_W + MIX_W), (3080, 3336, 3072), (3336, IN_W, QKV_W)]
ASSEMBLE_COLS = 256
ASSEMBLE_ROWS = 128


def _shard_pieces(k):
    lo, hi = k * SHARD_W, (k + 1) * SHARD_W
    return [(max(lo, a) - lo, at + max(lo, a) - a, min(hi, b) - max(lo, a))
            for a, b, at in _W_IN_SEGMENTS if max(lo, a) < min(hi, b)]


def _assemble_w_in(g, name):
    used = QKV_W + MIX_W + FOX_HEADS

    def body(g_ref, w_ref):
        for k in range(N_DEV):
            for src, dst, n in _shard_pieces(k):
                for r in range(0, n, ASSEMBLE_ROWS):
                    m = min(ASSEMBLE_ROWS, n - r)
                    w_ref[dst + r:dst + r + m, :] = g_ref[k, src + r:src + r + m, :]
        w_ref[used:, :] = jnp.zeros((WR_W - used, ASSEMBLE_COLS), BF16)

    return pl.pallas_call(
        body, name=name, grid=(D_MODEL // ASSEMBLE_COLS,),
        in_specs=[pl.BlockSpec((N_DEV, SHARD_W, ASSEMBLE_COLS), lambda j: (0, 0, j))],
        out_specs=pl.BlockSpec((WR_W, ASSEMBLE_COLS), lambda j: (0, j)),
        out_shape=jax.ShapeDtypeStruct((WR_W, D_MODEL), BF16), compiler_params=_ARB1)(g)


def _restore_w_in(g):
    gate0 = QKV_W
    fl0 = QKV_W + MIX_W
    return jnp.concatenate(
        [g[:3072], g[fl0:fl0 + FOX_HEADS], g[3072:QKV_W], g[gate0:fl0]], axis=0)


def _pad_lanes(v, width=LANES):
    return jnp.pad(v, (0, width - v.shape[0])).reshape(1, width)


def _layer_fwd(xs, mem, nw, w_r, b_forget, mnw, late, onw, l, travel=None):
    s = xs.shape[0]
    bpad = _pad_lanes(b_forget)
    travel = _Travel(travel)
    qkv, gf, _ = travel.ride(2, _inproj_fwd, xs, nw, w_r, f"inproj_fwd_{l}")
    fqb, frow, fbounds = _fox_prep_fwd(gf, qkv, bpad, f"fox_prep_fwd_{l}")
    frow = frow.reshape(FOX_HEADS // 2, 2, s)
    ysb, _ = travel.ride(0, _sb_fwd, qkv, f"sb_fwd_{l}")
    yfx, lse_fx, _ = travel.ride(1, _fox_fwd, qkv, fqb, frow, fbounds, f"fox_fwd_{l}")
    wkv, wout = late(travel.lands)
    kv = _memkv_fwd(mem, mnw, wkv, f"memkv_fwd_{l}")
    ym, lse_m = _mem_fwd(qkv, kv, f"mem_fwd_{l}")
    xn = _outproj_fwd(ysb, yfx, ym, gf, onw, wout, xs, f"outproj_fwd_{l}")
    saved = (xs, nw, mnw, onw, bpad, qkv, gf, fqb, frow, fbounds, ysb, yfx, lse_fx, kv, ym, lse_m)
    return xn, saved, travel.lands, (wkv, wout)


class _Travel:
    def __init__(self, plan):
        self.plan = plan
        self.lands = None if plan is None else _new_lands(plan[0], plan[1])

    def ride(self, n, fn, *args):
        if self.plan is None or n >= len(self.plan[2]) or self.plan[2][n] is None:
            return fn(*args)
        srcs, scatter, legs = self.plan
        idx, rows = legs[n]
        out = fn(*args, rider=_Rider([srcs[a] for a in idx], [self.lands[a] for a in idx],
                                     scatter, rows, relay=not scatter))
        for a, land in zip(idx, out[-1]):
            self.lands[a] = land
        return out


def _layer_bwd(dx, saved, mem, w_r, wkv, wout, l, travel=None, own_w_out=None):
    xs, nw, mnw, onw, bpad, qkv, gf, fqb, frow, fbounds, ysb, yfx, lse_fx, kv, ym, lse_m = saved
    s = xs.shape[0]
    travel = _Travel(travel)
    dysb, dyfx, dym, dgate, dwout, donw, _ = travel.ride(
        3, _outproj_bwd, dx, wout, ysb, yfx, ym, gf, onw, f"outproj_bwd_{l}")
    if own_w_out is not None:
        travel.plan[0][own_w_out[0]] = own_w_out[1](dwout)
    sdq, sdk, sdv, _ = travel.ride(0, _sb_bwd, qkv, ysb, dysb, f"sb_bwd_{l}")
    fdq, fdk, fdv, dfrow, _ = travel.ride(1, _fox_bwd, qkv, fqb, frow, fbounds, yfx, lse_fx, dyfx,
                                          f"fox_bwd_{l}")
    dfl, db = _fox_prep_bwd(dfrow.reshape(FOX_HEADS, s), gf, bpad, f"fox_prep_bwd_{l}")
    dmq, dmk, dmv = _mem_bwd(qkv, kv, ym, lse_m, dym, f"mem_bwd_{l}")
    dwkv, dmnw = _memkv_bwd(mem, mnw, wkv, dmk, dmv, f"memkv_bwd_{l}")
    dx, ht, dnw, dproj, _ = travel.ride(2, _inproj_bwd_dx,
                                        [sdq, sdk, sdv, fdq, fdk, fdv, dmq, dgate, dfl],
                                        w_r, xs, nw, dx, f"inproj_bwd_dx_{l}")
    dwr = _inproj_bwd_dw(ht, dproj, f"inproj_bwd_dw_{l}")
    grads = dict(norm_w=dnw[0], w_r=dwr, b_forget=db[0, :FOX_HEADS], mem_norm_w=dmnw[0],
                 w_mem_kv=dwkv, out_norm_w=donw[0], w_out=dwout)
    return dx, grads, travel.lands


_ANY = pl.BlockSpec(memory_space=pl.ANY)


def _my_place():
    return lax.axis_index("x"), lax.axis_index("y"), lax.axis_index("c")


def _flip(v, bit):
    return 1 - v if bit else v


def _block_index(px, py, pc):
    return 4 * px + 2 * py + pc


def _all_gather_weights(shards, name):
    n = len(shards)

    def body(*refs):
        ins, outs = refs[:n], refs[n:2 * n]
        send_sems, recv_sems, local_sems = refs[2 * n:]
        x, y, c = _my_place()
        me = (x, y, c)
        sibling = (x, y, 1 - c)
        chips = [(1 - x, y), (x, 1 - y), (1 - x, 1 - y)]

        def copy(a, k, block, to, src=None):
            dst = outs[a].at[_block_index(*block)]
            return pltpu.make_async_remote_copy(
                src_ref=dst if src is None else src, dst_ref=dst,
                send_sem=send_sems.at[a, k], recv_sem=recv_sems.at[a, k],
                device_id=to, device_id_type=pl.DeviceIdType.MESH)

        mine = [pltpu.make_async_copy(ins[a], outs[a].at[_block_index(*me)], local_sems.at[a])
                for a in range(n)]
        for cp in mine:
            cp.start()
        first = []
        for a in range(n):
            first.append(copy(a, 0, me, sibling, src=ins[a]))
            first += [copy(a, 1 + j, me, (*chip, c), src=ins[a]) for j, chip in enumerate(chips)]
        for cp in first:
            cp.start()
        passed = []
        for j, chip in enumerate(chips):
            for a in range(n):
                copy(a, 1 + j, (*chip, c), me).wait_recv()
                fwd = copy(a, 4 + j, (*chip, c), sibling)
                fwd.start()
                passed.append(fwd)
        for a in range(n):
            copy(a, 0, sibling, me).wait_recv()
            for j, chip in enumerate(chips):
                copy(a, 4 + j, (*chip, 1 - c), me).wait_recv()
        for cp in first + passed:
            cp.wait_send()
        for cp in mine:
            cp.wait()

    return pl.pallas_call(
        body, name=name,
        in_specs=[_ANY] * n, out_specs=[_ANY] * n,
        out_shape=[jax.ShapeDtypeStruct((N_DEV,) + v.shape, v.dtype) for v in shards],
        scratch_shapes=[pltpu.SemaphoreType.DMA((n, 7)), pltpu.SemaphoreType.DMA((n, 7)),
                        pltpu.SemaphoreType.DMA((n,))],
    )(*shards)


def _exchange_blocks(blocked, name):
    n = len(blocked)

    def body(*refs):
        ins, outs = refs[:n], refs[n:2 * n]
        send_sems, recv_sems, local_sems = refs[2 * n:]
        x, y, c = _my_place()
        mine_idx = _block_index(x, y, c)
        local = [pltpu.make_async_copy(ins[a].at[mine_idx], outs[a].at[mine_idx], local_sems.at[a])
                 for a in range(n)]
        for cp in local:
            cp.start()
        sends, arrivals = [], []
        for r in range(1, N_DEV):
            peer = (_flip(x, r & 4), _flip(y, r & 2), _flip(c, r & 1))
            peer_idx = _block_index(*peer)
            for a in range(n):
                sems = dict(send_sem=send_sems.at[a, r - 1], recv_sem=recv_sems.at[a, r - 1],
                            device_id=peer, device_id_type=pl.DeviceIdType.MESH)
                sends.append(pltpu.make_async_remote_copy(
                    src_ref=ins[a].at[peer_idx], dst_ref=outs[a].at[mine_idx], **sems))
                arrivals.append(pltpu.make_async_remote_copy(
                    src_ref=ins[a].at[peer_idx], dst_ref=outs[a].at[peer_idx], **sems))
        for cp in sends:
            cp.start()
        for cp in arrivals:
            cp.wait_recv()
        for cp in sends:
            cp.wait_send()
        for cp in local:
            cp.wait()

    return pl.pallas_call(
        body, name=name,
        in_specs=[_ANY] * n, out_specs=[_ANY] * n,
        out_shape=[jax.ShapeDtypeStruct(v.shape, v.dtype) for v in blocked],
        scratch_shapes=[pltpu.SemaphoreType.DMA((n, 7)), pltpu.SemaphoreType.DMA((n, 7)),
                        pltpu.SemaphoreType.DMA((n,))],
    )(*blocked)


N_CHIP = N_DEV // 2


def _pair_swap(blocked, name):
    n = len(blocked)

    def body(*refs):
        ins, outs = refs[:n], refs[n:2 * n]
        send_sems, recv_sems = refs[2 * n:]
        x, y, c = _my_place()
        copies = [pltpu.make_async_remote_copy(
            src_ref=ins[a].at[j, 1 - c], dst_ref=outs[a].at[j],
            send_sem=send_sems.at[N_CHIP * a + j], recv_sem=recv_sems.at[N_CHIP * a + j],
            device_id=(x, y, 1 - c), device_id_type=pl.DeviceIdType.MESH)
            for a in range(n) for j in range(N_CHIP)]
        for cp in copies:
            cp.start()
        for cp in copies:
            cp.wait_recv()
        for cp in copies:
            cp.wait_send()

    return pl.pallas_call(
        body, name=name,
        in_specs=[_ANY] * n, out_specs=[_ANY] * n,
        out_shape=[jax.ShapeDtypeStruct((N_CHIP,) + v.shape[2:], v.dtype) for v in blocked],
        scratch_shapes=[pltpu.SemaphoreType.DMA((N_CHIP * n,)),
                        pltpu.SemaphoreType.DMA((N_CHIP * n,))],
    )(*blocked)


def _pair_add(by_core, theirs, core, name):
    n = len(by_core)

    def body(core_ref, *refs):
        for a in range(n):
            mine, got, out = refs[a], refs[n + a], refs[2 * n + a]
            out[...] = (mine[...].astype(F32) + got[...].astype(F32)).astype(BF16)

    def own(v):
        return pl.BlockSpec((None, None) + v.shape[2:], lambda j, core_ref: (j, core_ref[0], 0, 0))

    def block(v):
        return pl.BlockSpec((None,) + v.shape[1:], lambda j, core_ref: (j, 0, 0))

    return list(pl.pallas_call(
        body, name=name,
        grid_spec=pltpu.PrefetchScalarGridSpec(
            num_scalar_prefetch=1, grid=(N_CHIP,),
            in_specs=[own(v) for v in by_core] + [block(v) for v in theirs],
            out_specs=[block(v) for v in theirs]),
        out_shape=[jax.ShapeDtypeStruct(v.shape, BF16) for v in theirs],
        compiler_params=_ARB1,
    )(core, *by_core, *theirs))


def _chip_exchange(by_chip, to_all, name):
    n, m = len(by_chip), len(to_all)

    def body(*refs):
        ins, alls = refs[:n], refs[n:n + m]
        outs, all_outs = refs[n + m:2 * n + m], refs[2 * n + m:2 * (n + m)]
        send_sems, recv_sems, local_sems = refs[2 * (n + m):]
        x, y, c = _my_place()
        my_chip, mine_idx = 2 * x + y, _block_index(x, y, c)
        local = [pltpu.make_async_copy(ins[a].at[my_chip], outs[a].at[my_chip], local_sems.at[a])
                 for a in range(n)]
        local += [pltpu.make_async_copy(alls[b].at[mine_idx], all_outs[b].at[mine_idx],
                                        local_sems.at[n + b]) for b in range(m)]
        for cp in local:
            cp.start()
        sends, arrivals = [], []
        k = 0
        for r in range(1, N_DEV):
            peer = (_flip(x, r & 4), _flip(y, r & 2), _flip(c, r & 1))
            peer_chip, peer_idx = 2 * peer[0] + peer[1], _block_index(*peer)
            pairs = [(alls[b].at[mine_idx], all_outs[b].at[mine_idx], all_outs[b].at[peer_idx])
                     for b in range(m)]
            if not r & 1:
                pairs += [(ins[a].at[peer_chip], outs[a].at[my_chip], outs[a].at[peer_chip])
                          for a in range(n)]
            for src, there, here in pairs:
                sems = dict(send_sem=send_sems.at[k], recv_sem=recv_sems.at[k], device_id=peer,
                            device_id_type=pl.DeviceIdType.MESH)
                sends.append(pltpu.make_async_remote_copy(src_ref=src, dst_ref=there, **sems))
                arrivals.append(pltpu.make_async_remote_copy(src_ref=src, dst_ref=here, **sems))
                k += 1
        for cp in sends:
            cp.start()
        for cp in arrivals:
            cp.wait_recv()
        for cp in sends:
            cp.wait_send()
        for cp in local:
            cp.wait()

    n_copies = 7 * m + 3 * n
    return pl.pallas_call(
        body, name=name,
        in_specs=[_ANY] * (n + m), out_specs=[_ANY] * (n + m),
        out_shape=[jax.ShapeDtypeStruct(v.shape, v.dtype) for v in by_chip + to_all],
        scratch_shapes=[pltpu.SemaphoreType.DMA((n_copies,)), pltpu.SemaphoreType.DMA((n_copies,)),
                        pltpu.SemaphoreType.DMA((n + m,))],
    )(*by_chip, *to_all)


class _Rider(NamedTuple):
    srcs: list
    lands: list
    scatter: bool
    part: list
    relay: bool = False


def _window(ref, part):
    if part is None:
        return ref
    dim, start, size = part
    return ref.at[(slice(None),) * dim + (pl.ds(start, size),)]


def _relay_copies(srcs, lands, send_sems, recv_sems, rider):
    x, y, c = _my_place()
    me, sibling = (x, y, c), (x, y, 1 - c)
    chips = [(1 - x, y), (x, 1 - y), (1 - x, 1 - y)]
    first, from_chips, passed, last = [], [], [], []
    for a in range(len(srcs)):
        def copy(k, block, to, src=None, a=a):
            dst = _window(lands[a].at[_block_index(*block)], rider.part[a])
            return pltpu.make_async_remote_copy(
                src_ref=dst if src is None else src, dst_ref=dst,
                send_sem=send_sems.at[7 * a + k], recv_sem=recv_sems.at[7 * a + k],
                device_id=to, device_id_type=pl.DeviceIdType.MESH)

        mine = _window(srcs[a], rider.part[a])
        first.append(copy(0, me, sibling, src=mine))
        first += [copy(1 + j, me, (*chip, c), src=mine) for j, chip in enumerate(chips)]
        from_chips += [copy(1 + j, (*chip, c), me) for j, chip in enumerate(chips)]
        passed += [copy(4 + j, (*chip, c), sibling) for j, chip in enumerate(chips)]
        last.append(copy(0, sibling, me))
        last += [copy(4 + j, (*chip, 1 - c), me) for j, chip in enumerate(chips)]
    return first, from_chips, passed, last


def _new_lands(srcs, scatter):
    return [lax.empty(v.shape if scatter else (N_DEV,) + v.shape, v.dtype) for v in srcs]


def _rider_copies(srcs, lands, send_sems, recv_sems, rider):
    x, y, c = _my_place()
    mine_idx = _block_index(x, y, c)

    def window(ref, a):
        return _window(ref, rider.part[a])

    sends, arrivals = [], []
    for r in range(1, N_DEV):
        peer = (_flip(x, r & 4), _flip(y, r & 2), _flip(c, r & 1))
        peer_idx = _block_index(*peer)
        for a in range(len(srcs)):
            src = window(srcs[a].at[peer_idx] if rider.scatter else srcs[a], a)
            k = 7 * a + r - 1
            sems = dict(send_sem=send_sems.at[k], recv_sem=recv_sems.at[k],
                        device_id=peer, device_id_type=pl.DeviceIdType.MESH)
            sends.append(pltpu.make_async_remote_copy(
                src_ref=src, dst_ref=window(lands[a].at[mine_idx], a), **sems))
            arrivals.append(pltpu.make_async_remote_copy(
                src_ref=src, dst_ref=window(lands[a].at[peer_idx], a), **sems))
    return sends, arrivals


def _ride(call, rider):
    call = dict(call)
    body, grid = call.pop("body"), call["grid"]
    operands = call.pop("operands")
    if rider is None:
        return list(pl.pallas_call(body, **call)(*operands)), None
    n_in, n_out = len(call["in_specs"]), len(call["out_specs"])
    n_scratch = len(call["scratch_shapes"])
    m = len(rider.srcs)

    def riding(*refs):
        main_in, srcs, lands = refs[:n_in], refs[n_in:n_in + m], refs[n_in + m:n_in + 2 * m]
        main_out = refs[n_in + 2 * m:n_in + 2 * m + n_out]
        rest = refs[n_in + 3 * m + n_out:]
        send_sems, recv_sems = rest[n_scratch:]
        at = [pl.program_id(d) for d in range(len(grid))]
        first = functools.reduce(jnp.logical_and, [p == 0 for p in at])
        last = functools.reduce(jnp.logical_and, [p == g - 1 for p, g in zip(at, grid)])
        if rider.relay:
            sends, from_chips, passed, arrivals = _relay_copies(
                srcs, lands, send_sems, recv_sems, rider)
            step, steps = 0, 1
            for p, g in zip(at, grid):
                step, steps = step * g + p, steps * g
            assert steps >= 2, "a relayed gather needs a later grid step to pass blocks on"

            @pl.when(step == (3 * steps) // 4)
            def _():
                for cp in from_chips:
                    cp.wait_recv()
                for cp in passed:
                    cp.start()
        else:
            sends, arrivals = _rider_copies(srcs, lands, send_sems, recv_sems, rider)
            passed = []

        @pl.when(first)
        def _():
            for cp in sends:
                cp.start()

        body(*main_in, *main_out, *rest[:n_scratch])

        @pl.when(last)
        def _():
            for cp in arrivals:
                cp.wait_recv()
            for cp in sends + passed:
                cp.wait_send()

    call["in_specs"] = list(call["in_specs"]) + [_ANY] * (2 * m)
    call["out_specs"] = list(call["out_specs"]) + [_ANY] * m
    call["out_shape"] = list(call["out_shape"]) + [
        jax.ShapeDtypeStruct(v.shape, v.dtype) for v in rider.lands]
    call["scratch_shapes"] = list(call["scratch_shapes"]) + [
        pltpu.SemaphoreType.DMA((7 * m,)), pltpu.SemaphoreType.DMA((7 * m,))]
    call["input_output_aliases"] = {n_in + m + a: n_out + a for a in range(m)}
    outs = pl.pallas_call(riding, **call)(*operands, *rider.srcs, *rider.lands)
    return list(outs[:n_out]), list(outs[n_out:])


def _sum_parts(p_ref):
    g = p_ref[0].astype(F32)
    for k in range(1, p_ref.shape[0]):
        g = g + p_ref[k].astype(F32)
    return g


def _adamw(g, w, m, v):
    c1 = 1.0 / (1.0 - ADAM_B1 ** ADAM_STEP)
    c2 = 1.0 / (1.0 - ADAM_B2 ** ADAM_STEP)
    nm = ADAM_B1 * m + (1.0 - ADAM_B1) * g
    nv = ADAM_B2 * v + (1.0 - ADAM_B2) * (g * g)
    return nm, nv, -ADAM_LR * ((nm * c1) / (jnp.sqrt(nv * c2) + ADAM_EPS) + ADAM_WD * w)


def _adamw_w_in(parts, w, m, v, name):
    ncol_blk, depth, nfeat = w.shape
    cols = 256

    def body(*refs):
        p_refs = refs[:depth]
        w_ref, m_ref, v_ref, g_ref, d_ref, nm_ref, nv_ref = refs[depth:]
        for l in range(depth):
            g = _sum_parts(p_refs[l])
            nm, nv, d = _adamw(g, w_ref[:, l, :], m_ref[:, l, :], v_ref[:, l, :])
            g_ref[:, l, :] = g
            nm_ref[:, l, :] = nm
            nv_ref[:, l, :] = nv
            d_ref[:, l, :] = d

    blk = pl.BlockSpec((ncol_blk, depth, cols), lambda j: (0, 0, j))
    return pl.pallas_call(
        body, name=name, grid=(nfeat // cols,),
        in_specs=[pl.BlockSpec((p.shape[0], ncol_blk, cols), lambda j: (0, 0, j)) for p in parts]
        + [blk] * 3,
        out_specs=[blk] * 4,
        out_shape=[jax.ShapeDtypeStruct(w.shape, F32)] * 4,
        compiler_params=_ARB1,
    )(*parts, w, m, v)


def _adamw_sum(parts, w, m, v, tile, name):
    depth, nrow, ncol = w.shape
    rows, cols = tile

    def body(*refs):
        p_refs = refs[:depth]
        w_ref, m_ref, v_ref, g_ref, d_ref, nm_ref, nv_ref = refs[depth:]
        layer = pl.program_id(0)
        for l in range(depth):
            @pl.when(layer == l)
            def _(p_ref=p_refs[l]):
                g = _sum_parts(p_ref)
                nm, nv, d = _adamw(g, w_ref[...], m_ref[...], v_ref[...])
                g_ref[...] = g
                nm_ref[...] = nm
                nv_ref[...] = nv
                d_ref[...] = d

    def part_spec(l):
        return pl.BlockSpec((parts[l].shape[0], rows, cols), lambda q, i, j: (
            0, jnp.where(q == l, i, 0), jnp.where(q == l, j, 0)))

    blk = pl.BlockSpec((None, rows, cols), lambda q, i, j: (q, i, j))
    return pl.pallas_call(
        body, name=name, grid=(depth, nrow // rows, ncol // cols),
        in_specs=[part_spec(l) for l in range(depth)] + [blk, blk, blk],
        out_specs=[blk] * 4,
        out_shape=[jax.ShapeDtypeStruct(w.shape, F32)] * 4,
        compiler_params=pltpu.CompilerParams(
            dimension_semantics=("arbitrary", "arbitrary", "arbitrary")),
    )(*parts, w, m, v)


def _adamw_small(parts, ws, ms, vs, name):
    n = len(parts)

    def body(*refs):
        p, w, m, v = (refs[a * n:(a + 1) * n] for a in range(4))
        out = refs[4 * n:]
        for a in range(n):
            g = _sum_parts(p[a])
            nm, nv, d = _adamw(g, w[a][...], m[a][...], v[a][...])
            out[a][...] = g
            out[n + a][...] = d
            out[2 * n + a][...] = nm
            out[3 * n + a][...] = nv

    vmem = pl.BlockSpec(memory_space=pltpu.VMEM)
    outs = pl.pallas_call(
        body, name=name, in_specs=[vmem] * (4 * n), out_specs=[vmem] * (4 * n),
        out_shape=[jax.ShapeDtypeStruct(w.shape, F32) for w in ws] * 4,
    )(*parts, *ws, *ms, *vs)
    return [list(outs[a * n:(a + 1) * n]) for a in range(4)]


def _misc_rows(b_forget, extra=None):
    tile = jnp.pad(b_forget, ((0, 6), (0, LANES - FOX_HEADS)))
    return tile if extra is None else tile.at[2].set(extra)


def kernel(x, mem, norm_w, w_in, b_forget, mem_norm_w, w_mem_kv, out_norm_w, w_out, final_norm_w, loss_target, m_norm_w, m_w_in, m_b_forget, m_mem_norm_w, m_w_mem_kv, m_out_norm_w, m_w_out, m_final_norm_w, v_norm_w, v_w_in, v_b_forget, v_mem_norm_w, v_w_mem_kv, v_out_norm_w, v_w_out, v_final_norm_w):
    kv_rows = w_mem_kv.shape[1]
    out_rows = w_out.shape[1]
    me = _block_index(*_my_place())

    def shards(l):
        return [w_in[l].T.astype(BF16), w_mem_kv[l].astype(BF16), w_out[l].astype(BF16)]


    def full_kv_out(g_kv, g_out):
        return g_kv.reshape(D_MODEL, 2 * MEM_W), g_out.reshape(MIX_W, D_MODEL)

    def in_blocks(g):
        segments = [(0, 3072, 0), (3072, 3080, QKV_W + MIX_W), (3080, 3336, 3072),
                    (3336, IN_W, QKV_W)]

        def block(k):
            lo, hi = k * SHARD_W, (k + 1) * SHARD_W
            pieces = [g[at + max(lo, a) - a:at + min(hi, b) - a]
                      for a, b, at in segments if max(lo, a) < min(hi, b)]
            return jnp.concatenate(pieces, axis=0)

        return jnp.stack([block(k) for k in range(N_DEV)])

    def kv_blocks(g):
        return g.reshape(N_DEV, kv_rows, 2 * MEM_W).astype(BF16)

    def out_blocks(g):
        return g.reshape(N_DEV, out_rows, D_MODEL).astype(BF16)

    def with_own(land, own):
        return lax.dynamic_update_slice(land, own[None], (me,) + (0,) * own.ndim)

    def with_own_of(land, blocked):
        return lax.dynamic_update_slice(land, lax.dynamic_slice_in_dim(blocked, me, 1, axis=0),
                                        (me,) + (0,) * (land.ndim - 1))

    def row(v):
        return v.reshape(1, -1)

    def cols(first, size):
        return (1, first, size)

    fwd_split, bwd_split = 4 * LANES, (5 * LANES, 6 * LANES)

    s_in0, s_kv0, s_out0 = shards(0)
    (g_in0,) = _all_gather_weights([s_in0], "all_gather_l0")
    w_r0 = _assemble_w_in(g_in0, "assemble_w_in_0")
    s_in1, s_kv1, s_out1 = shards(1)
    x1, saved0, (l_in1, _, _), (wkv0, wout0) = _layer_fwd(
        x[0], mem[0], row(norm_w[0]), w_r0, b_forget[0], row(mem_norm_w[0]),
        lambda lands: full_kv_out(with_own(lands[1], s_kv0), with_own(lands[2], s_out0)),
        row(out_norm_w[0]), 0,
        travel=([s_in1, s_kv0, s_out0], False,
                [([0], [cols(0, fwd_split)]), ([0], [cols(fwd_split, D_MODEL - fwd_split)]),
                 ([1, 2], [None, None])]))
    w_r1 = _assemble_w_in(with_own(l_in1, s_in1), "assemble_w_in_1")
    x2, saved1, _, (wkv1, wout1) = _layer_fwd(
        x1, mem[0], row(norm_w[1]), w_r1, b_forget[1], row(mem_norm_w[1]),
        lambda lands: full_kv_out(with_own(lands[0], s_kv1), with_own(lands[1], s_out1)),
        row(out_norm_w[1]), 1, travel=([s_kv1, s_out1], False, [([0, 1], [None, None]), None]))

    dx2, loss_part, dfnw = _final_fwd_bwd(x2, row(final_norm_w), loss_target[0], "final_fwd_bwd")

    dw_out_blocks = jax.ShapeDtypeStruct((N_DEV, out_rows, D_MODEL), BF16)
    dx1, gr1, (l_out1,) = _layer_bwd(
        dx2, saved1, mem[0], w_r1, wkv1, wout1, 1,
        travel=([dw_out_blocks], True, [([0], [None])]), own_w_out=(0, out_blocks))
    p_in1, p_kv1 = in_blocks(gr1["w_r"]), kv_blocks(gr1["w_mem_kv"])
    grad_x, gr0, (l_in1, l_kv1, l_out0) = _layer_bwd(
        dx1[0], saved0, mem[0], w_r0, wkv0, wout0, 0,
        travel=([p_in1, p_kv1, dw_out_blocks], True,
                [([0, 1], [cols(0, bwd_split[0]), None]),
                 ([0, 2], [cols(bwd_split[0], bwd_split[1] - bwd_split[0]), None]),
                 None,
                 ([0], [cols(bwd_split[1], D_MODEL - bwd_split[1])])]),
        own_w_out=(2, out_blocks))
    r_out1 = with_own_of(l_out1, out_blocks(gr1["w_out"]))
    r_in1, r_kv1 = with_own_of(l_in1, p_in1), with_own_of(l_kv1, p_kv1)
    r_out0 = with_own_of(l_out0, out_blocks(gr0["w_out"]))

    def both(name):
        return jnp.stack([gr0[name], gr1[name]])

    small = [both("norm_w"), both("mem_norm_w"), both("out_norm_w"), dfnw,
             _misc_rows(both("b_forget"), loss_part[0])]
    p_small = [jnp.broadcast_to(v[None], (N_DEV,) + v.shape) for v in small]
    by_core = [v.reshape((N_CHIP, 2) + v.shape[1:])
               for v in (in_blocks(gr0["w_r"]), kv_blocks(gr0["w_mem_kv"]))]
    from_sibling = _pair_swap(by_core, "grads_l0_pair_swap")
    core = lax.axis_index("c").astype(jnp.int32).reshape(1)
    chip_sums = _pair_add(by_core, from_sibling, core, "grads_l0_pair_add")
    r_in0, r_kv0, *r_small = _chip_exchange(chip_sums, p_small, "exchange_grads_l0")

    def view(v):
        return jnp.transpose(v, (2, 0, 1))

    g_w_in, d_w_in, nm_w_in, nv_w_in = [jnp.transpose(v, (1, 2, 0)) for v in _adamw_w_in(
        [r_in0, r_in1], view(w_in), view(m_w_in), view(v_w_in), "adamw_w_in")]
    g_w_kv, d_w_kv, nm_w_kv, nv_w_kv = _adamw_sum(
        [r_kv0, r_kv1], w_mem_kv, m_w_mem_kv, v_w_mem_kv, (kv_rows, 2 * MEM_W), "adamw_w_mem_kv")
    g_w_out, d_w_out, nm_w_out, nv_w_out = _adamw_sum(
        [r_out0, r_out1], w_out, m_w_out, v_w_out, (out_rows, D_MODEL), "adamw_w_out")
    def smalls(nw, mnw, onw, fnw, b):
        return [nw, mnw, onw, row(fnw), _misc_rows(b)]

    small_out = _adamw_small(
        r_small, smalls(norm_w, mem_norm_w, out_norm_w, final_norm_w, b_forget),
        smalls(m_norm_w, m_mem_norm_w, m_out_norm_w, m_final_norm_w, m_b_forget),
        smalls(v_norm_w, v_mem_norm_w, v_out_norm_w, v_final_norm_w, v_b_forget), "adamw_small")
    (g_nw, g_mnw, g_onw, g_fnw, g_b), (d_nw, d_mnw, d_onw, d_fnw, d_b), \
        (nm_nw, nm_mnw, nm_onw, nm_fnw, nm_b), (nv_nw, nv_mnw, nv_onw, nv_fnw, nv_b) = [
            (nw, mnw, onw, fnw[0], misc[:2, :FOX_HEADS]) for nw, mnw, onw, fnw, misc in small_out]
    loss = small_out[0][4][2, 0]

    return (loss, grad_x,
            g_nw, g_w_in, g_b, g_mnw, g_w_kv, g_onw, g_w_out, g_fnw,
            d_nw, d_w_in, d_b, d_mnw, d_w_kv, d_onw, d_w_out, d_fnw,
            nm_nw, nm_w_in, nm_b, nm_mnw, nm_w_kv, nm_onw, nm_w_out, nm_fnw,
            nv_nw, nv_w_in, nv_b, nv_mnw, nv_w_kv, nv_onw, nv_w_out, nv_fnw)
```

```python
import functools
from typing import NamedTuple

import jax
import jax.numpy as jnp
from jax import lax
from jax.experimental import pallas as pl
from jax.experimental.pallas import tpu as pltpu

F32 = jnp.float32
BF16 = jnp.bfloat16

N_DEV = 8
D_MODEL = 1024
HEAD_DIM = 64
LANES = 128
SB_W = 512
FX_W = 512
MEM_W = 256
MIX_W = 1280
FOX_HEADS = 8
IN_W = 4616
SHARD_W = IN_W // N_DEV
QKV_W = 3 * SB_W + 3 * FX_W + MEM_W
FL_PAD = 256
GF_W = MIX_W + FL_PAD
WR_W = QKV_W + GF_W
EPS = 1e-6
T = 256
QPS = 4
TM = 256
TQM = 512
Q_SCALE = 0.125
NEG = -1e30
UNDERFLOW = -110.0
NORM_SLACK = 1.01

ADAM_LR = 0.001
ADAM_B1 = 0.9
ADAM_B2 = 0.999
ADAM_EPS = 1e-08
ADAM_WD = 0.01
ADAM_STEP = 10


_NT = (((1,), (1,)), ((), ()))
_TN = (((0,), (0,)), ((), ()))

_ARB1 = pltpu.CompilerParams(dimension_semantics=("arbitrary",))
_ARB2 = pltpu.CompilerParams(dimension_semantics=("arbitrary", "arbitrary"))


def _dot(a, b):
    return jnp.dot(a, b, preferred_element_type=F32)


def _dot_nt(a, b):
    return lax.dot_general(a, b, _NT, preferred_element_type=F32)


def _dot_tn(a, b):
    return lax.dot_general(a, b, _TN, preferred_element_type=F32)


def _split2(x):
    hi = x.astype(BF16)
    lo = (x - hi.astype(F32)).astype(BF16)
    return hi, lo


def _stack2(u):
    return jnp.concatenate([u, u], axis=0)


def _cum2(x, u2):
    hi, lo = _split2(x)
    return _dot(jnp.concatenate([hi, lo], axis=1), u2)


def _tri3(tri, x, dot=None):
    dot = dot or _dot
    hi = x.astype(BF16)
    r1 = x - hi.astype(F32)
    mid = r1.astype(BF16)
    lo = (r1 - mid.astype(F32)).astype(BF16)
    return dot(tri, hi) + dot(tri, mid) + dot(tri, lo)


def _iota2(shape, dim):
    return lax.broadcasted_iota(jnp.int32, shape, dim)


def _head_block_diag():
    r = _iota2((LANES, LANES), 0) // HEAD_DIM
    c = _iota2((LANES, LANES), 1) // HEAD_DIM
    return _stack2(jnp.where(r == c, 1.0, 0.0).astype(BF16))


def _head_mean(x, bd):
    return _cum2(x, bd) * (1.0 / HEAD_DIM)


def _sigmoid(x):
    return 1.0 / (1.0 + jnp.exp(-x))


def _log_sigmoid(x):
    return jnp.minimum(x, 0.0) - jnp.log(1.0 + jnp.exp(-jnp.abs(x)))


def _running_top(r_ref):
    return jnp.max(jnp.maximum(r_ref[0], r_ref[1]))


def _fox_tiles_left(i, pair, nq, fb_ref, tile):
    def bound(j):
        b = []
        for h in range(2):
            head = 2 * pair + h
            b.append(2.0 * NORM_SLACK * fb_ref[2 * nq + i, head] * fb_ref[3 * nq, head]
                     + fb_ref[2 * i, head] - fb_ref[2 * j + 1, head])
        return jnp.maximum(b[0], b[1])

    def more(j):
        return jnp.logical_and(j >= 0, bound(jnp.maximum(j, 0)) > UNDERFLOW)

    def step(j):
        tile(j, False)
        return j - 1

    return lax.while_loop(more, step, i - 1)


def _pair_masks():
    lane = _iota2((1, LANES), 1)
    return lane < HEAD_DIM


def _split_pair(x, m0):
    zero = jnp.zeros_like(x)
    return jnp.where(m0, x, zero), jnp.where(m0, zero, x)


def _inproj_fwd(x, nw, w_r, name, rider=None):
    s = x.shape[0]

    def body(x_ref, nw_ref, w_ref, qkv_ref, gf_ref):
        xv = x_ref[...]
        r = lax.rsqrt(jnp.mean(xv * xv, axis=-1, keepdims=True) + EPS)
        h = (xv * r * nw_ref[...]).astype(BF16)
        for c in range(0, QKV_W, 256):
            qkv_ref[:, c:c + 256] = _dot_nt(h, w_ref[c:c + 256, :]).astype(BF16)
        for c in range(0, GF_W, 256):
            gf_ref[:, c:c + 256] = _dot_nt(h, w_ref[QKV_W + c:QKV_W + c + 256, :])

    (qkv, gf), lands = _ride(dict(
        body=body, name=name, grid=(s // TM,),
        in_specs=[pl.BlockSpec((TM, D_MODEL), lambda i: (i, 0)),
                  pl.BlockSpec((1, D_MODEL), lambda i: (0, 0)),
                  pl.BlockSpec((WR_W, D_MODEL), lambda i: (0, 0))],
        out_specs=[pl.BlockSpec((TM, QKV_W), lambda i: (i, 0)),
                   pl.BlockSpec((TM, GF_W), lambda i: (i, 0))],
        out_shape=[jax.ShapeDtypeStruct((s, QKV_W), BF16), jax.ShapeDtypeStruct((s, GF_W), F32)],
        scratch_shapes=[], compiler_params=_ARB1, operands=[x, nw, w_r]), rider)
    return qkv, gf, lands


def _fox_prep_fwd(gf, qkv, bpad, name):
    s = gf.shape[0]
    nq = s // T
    nrow = -(-(3 * nq + 1) // 8) * 8

    def body(fl_ref, q_ref, k_ref, b_ref, fq_ref, fr_ref, fb_ref):
        tri = jnp.where(_iota2((T, T), 0) >= _iota2((T, T), 1), 1.0, 0.0).astype(BF16)
        m0 = _pair_masks()
        lane = _iota2((1, LANES), 1)
        norms = [jnp.zeros((1, LANES), F32) for _ in range(nq + 1)]
        same_head = (_iota2((LANES, LANES), 0) // HEAD_DIM) == (_iota2((LANES, LANES), 1) // HEAD_DIM)
        bd = jnp.where(same_head, 1.0, 0.0).astype(BF16)
        for p in range(FOX_HEADS // 2):
            cols = slice(p * LANES, (p + 1) * LANES)
            q = (q_ref[:, cols] * jnp.asarray(Q_SCALE, BF16)).astype(F32)
            k = k_ref[:, cols].astype(F32)
            qn = _dot((q * q).astype(BF16), bd)
            kn = _dot((k * k).astype(BF16), bd)
            tops = [jnp.max(qn[j * T:(j + 1) * T], axis=0, keepdims=True) for j in range(nq)]
            tops.append(jnp.max(kn, axis=0, keepdims=True))
            tops = [jnp.sqrt(top) for top in tops]
            for h in range(2):
                at = h * HEAD_DIM
                norms = [jnp.where(lane == 2 * p + h, top[:, at:at + 1], row)
                         for top, row in zip(tops, norms)]
        for j in range(nq + 1):
            fb_ref[2 * nq + j:2 * nq + j + 1, :] = norms[j]
        fb_ref[3 * nq + 1:, :] = jnp.zeros((nrow - 3 * nq - 1, LANES), F32)
        carry = jnp.zeros((1, LANES), F32)
        for blk in range(s // T):
            rows = slice(blk * T, (blk + 1) * T)
            lf = _log_sigmoid(fl_ref[rows, :] + b_ref[...])
            c = _tri3(tri, lf) + carry
            carry = c[T - 1:T, :]
            for p in range(FOX_HEADS // 2):
                fq_ref[rows, p * LANES:(p + 1) * LANES] = jnp.where(
                    m0, c[:, 2 * p:2 * p + 1], c[:, 2 * p + 1:2 * p + 2])
            fr_ref[:, rows] = c.T[0:FOX_HEADS, :]
            fb_ref[2 * blk:2 * blk + 1, :] = c[0:1, :]
            fb_ref[2 * blk + 1:2 * blk + 2, :] = carry

    base = 3 * SB_W // FX_W
    return pl.pallas_call(
        body, name=name, grid=(1,),
        in_specs=[pl.BlockSpec((s, LANES), lambda i: (0, MIX_W // LANES)),
                  pl.BlockSpec((s, FX_W), lambda i: (0, base)),
                  pl.BlockSpec((s, FX_W), lambda i: (0, base + 1)),
                  pl.BlockSpec((1, LANES), lambda i: (0, 0))],
        out_specs=[pl.BlockSpec((s, FX_W), lambda i: (0, 0)),
                   pl.BlockSpec((FOX_HEADS, s), lambda i: (0, 0)),
                   pl.BlockSpec((nrow, LANES), lambda i: (0, 0))],
        out_shape=[jax.ShapeDtypeStruct((s, FX_W), F32), jax.ShapeDtypeStruct((FOX_HEADS, s), F32),
                   jax.ShapeDtypeStruct((nrow, LANES), F32)],
        compiler_params=_ARB1,
    )(gf, qkv, qkv, bpad)


def _sb_fwd(qkv, name, rider=None):
    s = qkv.shape[0]

    def body(q_ref, k_ref, v_ref, o_ref, acc_ref, r_ref, as_ref):
        m0 = _pair_masks()
        strict = _iota2((T, T), 0) > _iota2((T, T), 1)
        u2 = _stack2(jnp.where(strict, 1.0, 0.0).astype(BF16))
        hs = range(2)

        def query_tile(i, rows):
            qh = _split_pair(q_ref[rows, :] * jnp.asarray(Q_SCALE, BF16), m0)
            acc_ref[...] = jnp.zeros_like(acc_ref)
            r_ref[...] = jnp.zeros_like(r_ref)

            def flush(j):
                v = v_ref[pl.ds(pl.multiple_of(j * T, T), T), :]
                for h in hs:
                    acc_ref[h] += _dot(as_ref[h], v)

            def tile(j, diag):
                k = k_ref[pl.ds(pl.multiple_of(j * T, T), T), :]
                z = [_dot_nt(qh[h], k) for h in hs]
                if not diag:
                    flush(j + 1)
                la = [jnp.minimum(z[h], 0.0) - jnp.log(1.0 + jnp.exp(-jnp.abs(z[h]))) for h in hs]
                lf = [la[h] - z[h] for h in hs]
                if diag:
                    lf = [jnp.where(strict, lf[h], 0.0) for h in hs]
                cin = [_cum2(lf[h], u2) for h in hs]
                a = [jnp.exp(la[h] + cin[h] + r_ref[h]) for h in hs]
                if diag:
                    a = [jnp.where(strict, a[h], 0.0) for h in hs]
                for h in hs:
                    r_ref[h] += cin[h][:, 0:1] + lf[h][:, 0:1]
                    as_ref[h] = a[h].astype(BF16)

            tile(i, True)

            def more(state):
                j, top = state
                return jnp.logical_and(j >= 0, top > UNDERFLOW)

            def step(state):
                j, _ = state
                tile(j, False)
                return j - 1, _running_top(r_ref)

            j_left, _ = lax.while_loop(more, step, (i - 1, _running_top(r_ref)))
            flush(j_left + 1)
            o_ref[rows, :] = jnp.where(m0, acc_ref[0], acc_ref[1])

        for n in range(QPS):
            query_tile(QPS * pl.program_id(1) + n, slice(n * T, (n + 1) * T))

    nb = SB_W // LANES
    (ysb,), lands = _ride(dict(
        body=body, name=name, grid=(nb, s // (QPS * T)),
        in_specs=[pl.BlockSpec((QPS * T, LANES), lambda p, i: (i, p)),
                  pl.BlockSpec((s, LANES), lambda p, i: (0, nb + p)),
                  pl.BlockSpec((s, LANES), lambda p, i: (0, 2 * nb + p))],
        out_specs=[pl.BlockSpec((QPS * T, LANES), lambda p, i: (i, p))],
        out_shape=[jax.ShapeDtypeStruct((s, SB_W), F32)],
        scratch_shapes=[pltpu.VMEM((2, T, LANES), F32), pltpu.VMEM((2, T, 1), F32),
                        pltpu.VMEM((2, T, T), BF16)],
        compiler_params=_ARB2, operands=[qkv, qkv, qkv]), rider)
    return ysb, lands


def _fox_fwd(qkv, fqb, frow, fbounds, name, rider=None):
    s = qkv.shape[0]

    def body(q_ref, k_ref, v_ref, fq_ref, fr_ref, fb_ref, o_ref, lse_ref, acc_ref, m_ref, ps_ref):
        pair = pl.program_id(0)
        m0 = _pair_masks()
        causal = _iota2((T, T), 0) >= _iota2((T, T), 1)
        ones = jnp.ones((T, LANES), BF16)
        hs = range(2)

        def query_tile(i, rows):
            qh = _split_pair(q_ref[rows, :] * jnp.asarray(Q_SCALE, BF16), m0)
            fq = fq_ref[rows, :]
            fqh = (fq[:, 0:1], fq[:, HEAD_DIM:HEAD_DIM + 1])
            acc_ref[...] = jnp.zeros_like(acc_ref)
            m_ref[...] = jnp.full_like(m_ref, NEG)

            def flush(j):
                v = v_ref[pl.ds(pl.multiple_of(j * T, T), T), :]
                va2 = _stack2(jnp.concatenate([v, ones], axis=1))
                for h in hs:
                    acc_ref[h] += _dot(ps_ref[h], va2)

            def tile(j, diag):
                off = pl.multiple_of(j * T, T)
                k = k_ref[pl.ds(off, T), :]
                sc = [_dot_nt(qh[h], k) + fqh[h] - fr_ref[h:h + 1, pl.ds(off, T)] for h in hs]
                if not diag:
                    flush(j + 1)
                if diag:
                    sc = [jnp.where(causal, sc[h], NEG) for h in hs]
                m_new = [jnp.maximum(m_ref[h], jnp.max(sc[h], axis=1, keepdims=True)) for h in hs]
                p = [jnp.exp(sc[h] - m_new[h]) for h in hs]
                for h in hs:
                    acc_ref[h] = acc_ref[h] * jnp.exp(m_ref[h] - m_new[h])
                    m_ref[h] = m_new[h]
                    p_hi, p_lo = _split2(p[h])
                    ps_ref[h] = jnp.concatenate([p_hi, p_lo], axis=1)

            tile(i, True)
            j_left = _fox_tiles_left(i, pair, s // T, fb_ref, tile)
            flush(j_left + 1)
            acc = (acc_ref[0], acc_ref[1])
            o_ref[rows, :] = jnp.where(m0, acc[0][:, :LANES] / acc[0][:, LANES:],
                                       acc[1][:, :LANES] / acc[1][:, LANES:])
            lse_ref[rows, :] = jnp.where(m0, m_ref[0] + jnp.log(acc[0][:, LANES:]),
                                         m_ref[1] + jnp.log(acc[1][:, LANES:]))

        for n in range(QPS):
            query_tile(QPS * pl.program_id(1) + n, slice(n * T, (n + 1) * T))

    nb = FX_W // LANES
    base = 3 * SB_W // LANES
    (yfx, lse), lands = _ride(dict(
        body=body, name=name, grid=(nb, s // (QPS * T)),
        in_specs=[pl.BlockSpec((QPS * T, LANES), lambda p, i: (i, base + p)),
                  pl.BlockSpec((s, LANES), lambda p, i: (0, base + nb + p)),
                  pl.BlockSpec((s, LANES), lambda p, i: (0, base + 2 * nb + p)),
                  pl.BlockSpec((QPS * T, LANES), lambda p, i: (i, p)),
                  pl.BlockSpec((None, 2, s), lambda p, i: (p, 0, 0)),
                  pl.BlockSpec(memory_space=pltpu.SMEM)],
        out_specs=[pl.BlockSpec((QPS * T, LANES), lambda p, i: (i, p)),
                   pl.BlockSpec((QPS * T, LANES), lambda p, i: (i, p))],
        out_shape=[jax.ShapeDtypeStruct((s, FX_W), F32), jax.ShapeDtypeStruct((s, FX_W), F32)],
        scratch_shapes=[pltpu.VMEM((2, T, 2 * LANES), F32), pltpu.VMEM((2, T, 1), F32),
                        pltpu.VMEM((2, T, 2 * T), BF16)],
        compiler_params=_ARB2, operands=[qkv, qkv, qkv, fqb, frow, fbounds]), rider)
    return yfx, lse, lands


def _memkv_fwd(mem, mnw, wkv, name):
    n = mem.shape[0]

    def body(mem_ref, mnw_ref, w_ref, kv_ref):
        mv = mem_ref[...]
        r = lax.rsqrt(jnp.mean(mv * mv, axis=-1, keepdims=True) + EPS)
        hm = (mv * r * mnw_ref[...]).astype(BF16)
        kv_ref[...] = _dot(hm, w_ref[...]).astype(BF16)

    return pl.pallas_call(
        body, name=name, grid=(1,),
        in_specs=[pl.BlockSpec((n, D_MODEL), lambda i: (0, 0)),
                  pl.BlockSpec((1, D_MODEL), lambda i: (0, 0)),
                  pl.BlockSpec((D_MODEL, 2 * MEM_W), lambda i: (0, 0))],
        out_specs=pl.BlockSpec((n, 2 * MEM_W), lambda i: (0, 0)),
        out_shape=jax.ShapeDtypeStruct((n, 2 * MEM_W), BF16),
        compiler_params=_ARB1,
    )(mem, mnw, wkv)


def _mem_fwd(qkv, kv, name):
    s = qkv.shape[0]
    n = kv.shape[0]

    def body(q_ref, k_ref, v_ref, o_ref, lse_ref):
        m0 = _pair_masks()
        qh = _split_pair(q_ref[...] * jnp.asarray(Q_SCALE, BF16), m0)
        k = k_ref[...]
        v = v_ref[...]
        outs, lses = [], []
        for h in range(2):
            sc = _dot_nt(qh[h], k)
            mx = jnp.max(sc, axis=1, keepdims=True)
            p = jnp.exp(sc - mx)
            l = jnp.sum(p, axis=1, keepdims=True)
            outs.append(_dot(p.astype(BF16), v) / l)
            lses.append(mx + jnp.log(l))
        o_ref[...] = jnp.where(m0, outs[0], outs[1])
        lse_ref[...] = jnp.where(m0, lses[0], lses[1])

    nb = MEM_W // LANES
    base = (3 * SB_W + 3 * FX_W) // LANES
    return pl.pallas_call(
        body, name=name, grid=(nb, s // TQM),
        in_specs=[pl.BlockSpec((TQM, LANES), lambda p, i: (i, base + p)),
                  pl.BlockSpec((n, LANES), lambda p, i: (0, p)),
                  pl.BlockSpec((n, LANES), lambda p, i: (0, nb + p))],
        out_specs=[pl.BlockSpec((TQM, LANES), lambda p, i: (i, p)),
                   pl.BlockSpec((TQM, LANES), lambda p, i: (i, p))],
        out_shape=[jax.ShapeDtypeStruct((s, MEM_W), F32), jax.ShapeDtypeStruct((s, MEM_W), F32)],
        compiler_params=_ARB2,
    )(qkv, kv, kv)


def _mix_chunk(c, ysb_ref, yfx_ref, ym_ref):
    if c < SB_W // LANES:
        return ysb_ref[:, c * LANES:(c + 1) * LANES]
    c -= SB_W // LANES
    if c < FX_W // LANES:
        return yfx_ref[:, c * LANES:(c + 1) * LANES]
    c -= FX_W // LANES
    return ym_ref[:, c * LANES:(c + 1) * LANES]


def _outproj_fwd(ysb, yfx, ym, gf, onw, wout, x, name):
    s = x.shape[0]

    def body(ysb_ref, yfx_ref, ym_ref, g_ref, onw_ref, w_ref, x_ref, o_ref, yg_ref):
        bd = _head_block_diag()
        for c in range(MIX_W // LANES):
            sl = slice(c * LANES, (c + 1) * LANES)
            u = _mix_chunk(c, ysb_ref, yfx_ref, ym_ref)
            r = lax.rsqrt(_head_mean(u * u, bd) + EPS)
            g = g_ref[:, sl]
            yg_ref[:, sl] = (u * r * onw_ref[:, sl] * (g * _sigmoid(g))).astype(BF16)
        o_ref[...] = x_ref[...] + _dot(yg_ref[...], w_ref[...])

    return pl.pallas_call(
        body, name=name, grid=(s // TM,),
        in_specs=[pl.BlockSpec((TM, SB_W), lambda i: (i, 0)),
                  pl.BlockSpec((TM, FX_W), lambda i: (i, 0)),
                  pl.BlockSpec((TM, MEM_W), lambda i: (i, 0)),
                  pl.BlockSpec((TM, MIX_W), lambda i: (i, 0)),
                  pl.BlockSpec((1, MIX_W), lambda i: (0, 0)),
                  pl.BlockSpec((MIX_W, D_MODEL), lambda i: (0, 0)),
                  pl.BlockSpec((TM, D_MODEL), lambda i: (i, 0))],
        out_specs=pl.BlockSpec((TM, D_MODEL), lambda i: (i, 0)),
        out_shape=jax.ShapeDtypeStruct((s, D_MODEL), F32),
        scratch_shapes=[pltpu.VMEM((TM, MIX_W), BF16)],
        compiler_params=_ARB1,
    )(ysb, yfx, ym, gf, onw, wout, x)


def _final_fwd_bwd(x, fnw, target, name):
    s = x.shape[0]

    def body(x_ref, w_ref, t_ref, dx_ref, loss_ref, dw_ref):
        @pl.when(pl.program_id(0) == 0)
        def _():
            loss_ref[...] = jnp.zeros_like(loss_ref)
            dw_ref[...] = jnp.zeros_like(dw_ref)

        xv = x_ref[...]
        w = w_ref[...]
        r = lax.rsqrt(jnp.mean(xv * xv, axis=-1, keepdims=True) + EPS)
        xh = xv * r
        err = xh * w - t_ref[...]
        part = jnp.sum(jnp.sum(err * err, axis=1, keepdims=True), axis=0, keepdims=True)
        loss_ref[...] += part * (0.5 / D_MODEL)
        dy = err * (1.0 / D_MODEL)
        dw_ref[...] += jnp.sum(dy * xh, axis=0, keepdims=True)
        dxh = dy * w
        dx_ref[...] = r * (dxh - xh * jnp.mean(dxh * xh, axis=-1, keepdims=True))

    return pl.pallas_call(
        body, name=name, grid=(s // TM,),
        in_specs=[pl.BlockSpec((TM, D_MODEL), lambda i: (i, 0)),
                  pl.BlockSpec((1, D_MODEL), lambda i: (0, 0)),
                  pl.BlockSpec((TM, D_MODEL), lambda i: (i, 0))],
        out_specs=[pl.BlockSpec((TM, D_MODEL), lambda i: (i, 0)),
                   pl.BlockSpec((1, LANES), lambda i: (0, 0)),
                   pl.BlockSpec((1, D_MODEL), lambda i: (0, 0))],
        out_shape=[jax.ShapeDtypeStruct((s, D_MODEL), F32), jax.ShapeDtypeStruct((1, LANES), F32),
                   jax.ShapeDtypeStruct((1, D_MODEL), F32)],
        compiler_params=_ARB1,
    )(x, fnw, target)


def _outproj_bwd(dxo, wout, ysb, yfx, ym, gf, onw, name, rider=None):
    s = dxo.shape[0]

    def body(dx_ref, w_ref, ysb_ref, yfx_ref, ym_ref, g_ref, onw_ref,
             dysb_ref, dyfx_ref, dym_ref, dg_ref, dw_ref, donw_ref, yg_ref):
        @pl.when(pl.program_id(0) == 0)
        def _():
            dw_ref[...] = jnp.zeros_like(dw_ref)
            donw_ref[...] = jnp.zeros_like(donw_ref)

        dxb = dx_ref[...].astype(BF16)
        dyg = _dot_nt(dxb, w_ref[...])
        bd = _head_block_diag()
        for c in range(MIX_W // LANES):
            sl = slice(c * LANES, (c + 1) * LANES)
            u = _mix_chunk(c, ysb_ref, yfx_ref, ym_ref)
            r = lax.rsqrt(_head_mean(u * u, bd) + EPS)
            yn = u * r
            g = g_ref[:, sl]
            sg = _sigmoid(g)
            sil = g * sg
            onw = onw_ref[:, sl]
            e = dyg[:, sl]
            yg_ref[:, sl] = (yn * onw * sil).astype(BF16)
            donw_ref[:, sl] += jnp.sum(e * yn * sil, axis=0, keepdims=True)
            dg_ref[:, sl] = (e * yn * onw * (sg * (1.0 + g * (1.0 - sg)))).astype(BF16)
            dyn = e * onw * sil
            du = (r * (dyn - yn * _head_mean(dyn * yn, bd))).astype(BF16)
            if c < 4:
                dysb_ref[:, c * LANES:(c + 1) * LANES] = du
            elif c < 8:
                dyfx_ref[:, (c - 4) * LANES:(c - 3) * LANES] = du
            else:
                dym_ref[:, (c - 8) * LANES:(c - 7) * LANES] = du
        dw_ref[...] += _dot_tn(yg_ref[...], dxb)

    outs, lands = _ride(dict(
        body=body, name=name, grid=(s // TM,),
        in_specs=[pl.BlockSpec((TM, D_MODEL), lambda i: (i, 0)),
                  pl.BlockSpec((MIX_W, D_MODEL), lambda i: (0, 0)),
                  pl.BlockSpec((TM, SB_W), lambda i: (i, 0)),
                  pl.BlockSpec((TM, FX_W), lambda i: (i, 0)),
                  pl.BlockSpec((TM, MEM_W), lambda i: (i, 0)),
                  pl.BlockSpec((TM, MIX_W), lambda i: (i, 0)),
                  pl.BlockSpec((1, MIX_W), lambda i: (0, 0))],
        out_specs=[pl.BlockSpec((TM, SB_W), lambda i: (i, 0)),
                   pl.BlockSpec((TM, FX_W), lambda i: (i, 0)),
                   pl.BlockSpec((TM, MEM_W), lambda i: (i, 0)),
                   pl.BlockSpec((TM, MIX_W), lambda i: (i, 0)),
                   pl.BlockSpec((MIX_W, D_MODEL), lambda i: (0, 0)),
                   pl.BlockSpec((1, MIX_W), lambda i: (0, 0))],
        out_shape=[jax.ShapeDtypeStruct((s, SB_W), BF16), jax.ShapeDtypeStruct((s, FX_W), BF16),
                   jax.ShapeDtypeStruct((s, MEM_W), BF16), jax.ShapeDtypeStruct((s, MIX_W), BF16),
                   jax.ShapeDtypeStruct((MIX_W, D_MODEL), F32), jax.ShapeDtypeStruct((1, MIX_W), F32)],
        scratch_shapes=[pltpu.VMEM((TM, MIX_W), BF16)],
        compiler_params=_ARB1, operands=[dxo, wout, ysb, yfx, ym, gf, onw]), rider)
    return (*outs, lands)


def _row_dots(do, o, m0):
    prod = do.astype(F32) * o
    zero = jnp.zeros_like(prod)
    return (jnp.sum(jnp.where(m0, prod, zero), axis=1, keepdims=True),
            jnp.sum(jnp.where(m0, zero, prod), axis=1, keepdims=True))


def _sb_bwd(qkv, o, do, name, rider=None):
    s = qkv.shape[0]
    nq = s // T

    def body(q_ref, k_ref, v_ref, o_ref, do_ref, dq_ref, dk_ref, dv_ref,
             dqa_ref, dka_ref, dva_ref, rl_ref, rg_ref, dzs_ref, abs_ref):
        step_id = pl.program_id(1)

        @pl.when(step_id == 0)
        def _():
            dka_ref[...] = jnp.zeros_like(dka_ref)
            dva_ref[...] = jnp.zeros_like(dva_ref)

        m0 = _pair_masks()
        strict = _iota2((T, T), 0) > _iota2((T, T), 1)
        u2 = _stack2(jnp.where(strict, 1.0, 0.0).astype(BF16))
        hs = range(2)

        def query_tile(i, rows):
            qh = _split_pair(q_ref[rows, :] * jnp.asarray(Q_SCALE, BF16), m0)
            doh = _split_pair(do_ref[rows, :], m0)
            dsum = _row_dots(do_ref[rows, :], o_ref[rows, :], m0)
            dqa_ref[...] = jnp.zeros_like(dqa_ref)
            rl_ref[...] = jnp.zeros_like(rl_ref)
            rg_ref[...] = jnp.zeros_like(rg_ref)

            def flush(j):
                off = pl.multiple_of(j * T, T)
                k = k_ref[pl.ds(off, T), :]
                for h in hs:
                    dqa_ref[h] += _dot(dzs_ref[h], k)
                dka_ref[pl.ds(off, T), :] += (_dot_tn(dzs_ref[0], qh[0])
                                              + _dot_tn(dzs_ref[1], qh[1]))
                dva_ref[pl.ds(off, T), :] += (_dot_tn(abs_ref[0], doh[0])
                                              + _dot_tn(abs_ref[1], doh[1]))

            def tile(j, diag):
                off = pl.multiple_of(j * T, T)
                k = k_ref[pl.ds(off, T), :]
                v = v_ref[pl.ds(off, T), :]
                z = [_dot_nt(qh[h], k) for h in hs]
                da = [_dot_nt(doh[h], v) for h in hs]
                if not diag:
                    flush(j + 1)
                la = [jnp.minimum(z[h], 0.0) - jnp.log(1.0 + jnp.exp(-jnp.abs(z[h]))) for h in hs]
                lf = [la[h] - z[h] for h in hs]
                if diag:
                    lf = [jnp.where(strict, lf[h], 0.0) for h in hs]
                cin = [_cum2(lf[h], u2) for h in hs]
                a = [jnp.exp(la[h] + cin[h] + rl_ref[h]) for h in hs]
                if diag:
                    a = [jnp.where(strict, a[h], 0.0) for h in hs]
                ab = [a[h].astype(BF16) for h in hs]
                g = [ab[h].astype(F32) * da[h] for h in hs]
                gin = [_cum2(g[h], u2) for h in hs]
                dz = [g[h] - jnp.exp(la[h]) * ((dsum[h] - rg_ref[h]) - gin[h]) for h in hs]
                if diag:
                    dz = [jnp.where(strict, dz[h], 0.0) for h in hs]
                for h in hs:
                    rl_ref[h] += cin[h][:, 0:1] + lf[h][:, 0:1]
                    rg_ref[h] += gin[h][:, 0:1] + g[h][:, 0:1]
                    dzs_ref[h] = dz[h].astype(BF16)
                    abs_ref[h] = ab[h]

            tile(i, True)

            def more(state):
                j, top = state
                return jnp.logical_and(j >= 0, top > UNDERFLOW)

            def step(state):
                j, _ = state
                tile(j, False)
                return j - 1, _running_top(rl_ref)

            j_left, _ = lax.while_loop(more, step, (i - 1, _running_top(rl_ref)))
            flush(j_left + 1)
            dq_ref[rows, :] = (jnp.where(m0, dqa_ref[0], dqa_ref[1]) * Q_SCALE).astype(BF16)

        for n in range(QPS):
            query_tile(QPS * step_id + n, slice(n * T, (n + 1) * T))

        @pl.when(step_id == nq // QPS - 1)
        def _():
            dk_ref[...] = dka_ref[...].astype(BF16)
            dv_ref[...] = dva_ref[...].astype(BF16)

    nb = SB_W // LANES
    (dq, dk, dv), lands = _ride(dict(
        body=body, name=name, grid=(nb, nq // QPS),
        in_specs=[pl.BlockSpec((QPS * T, LANES), lambda p, i: (i, p)),
                  pl.BlockSpec((s, LANES), lambda p, i: (0, nb + p)),
                  pl.BlockSpec((s, LANES), lambda p, i: (0, 2 * nb + p)),
                  pl.BlockSpec((QPS * T, LANES), lambda p, i: (i, p)),
                  pl.BlockSpec((QPS * T, LANES), lambda p, i: (i, p))],
        out_specs=[pl.BlockSpec((QPS * T, LANES), lambda p, i: (i, p)),
                   pl.BlockSpec((s, LANES), lambda p, i: (0, p)),
                   pl.BlockSpec((s, LANES), lambda p, i: (0, p))],
        out_shape=[jax.ShapeDtypeStruct((s, SB_W), BF16)] * 3,
        scratch_shapes=[pltpu.VMEM((2, T, LANES), F32), pltpu.VMEM((s, LANES), F32),
                        pltpu.VMEM((s, LANES), F32), pltpu.VMEM((2, T, 1), F32),
                        pltpu.VMEM((2, T, 1), F32), pltpu.VMEM((2, T, T), BF16),
                        pltpu.VMEM((2, T, T), BF16)],
        compiler_params=_ARB2, operands=[qkv, qkv, qkv, o, do]), rider)
    return dq, dk, dv, lands


def _fox_bwd(qkv, fqb, frow, fbounds, o, lse, do, name, rider=None):
    s = qkv.shape[0]
    nq = s // T

    def body(q_ref, k_ref, v_ref, fq_ref, fr_ref, fb_ref, o_ref, lse_ref, do_ref,
             dq_ref, dk_ref, dv_ref, df_ref, dqa_ref, dka_ref, dva_ref, dfa_ref, dls_ref, pbs_ref):
        step_id = pl.program_id(1)

        @pl.when(step_id == 0)
        def _():
            dka_ref[...] = jnp.zeros_like(dka_ref)
            dva_ref[...] = jnp.zeros_like(dva_ref)
            dfa_ref[...] = jnp.zeros_like(dfa_ref)

        m0 = _pair_masks()
        causal = _iota2((T, T), 0) >= _iota2((T, T), 1)
        hs = range(2)

        def query_tile(i, rows):
            qh = _split_pair(q_ref[rows, :] * jnp.asarray(Q_SCALE, BF16), m0)
            doh = _split_pair(do_ref[rows, :], m0)
            dsum = _row_dots(do_ref[rows, :], o_ref[rows, :], m0)
            fq = fq_ref[rows, :]
            fqh = (fq[:, 0:1], fq[:, HEAD_DIM:HEAD_DIM + 1])
            lse = lse_ref[rows, :]
            lseh = (lse[:, 0:1], lse[:, HEAD_DIM:HEAD_DIM + 1])
            dqa_ref[...] = jnp.zeros_like(dqa_ref)

            def flush(j):
                off = pl.multiple_of(j * T, T)
                k = k_ref[pl.ds(off, T), :]
                for h in hs:
                    dqa_ref[h] += _dot(dls_ref[h], k)
                dka_ref[pl.ds(off, T), :] += (_dot_tn(dls_ref[0], qh[0])
                                              + _dot_tn(dls_ref[1], qh[1]))
                dva_ref[pl.ds(off, T), :] += (_dot_tn(pbs_ref[0], doh[0])
                                              + _dot_tn(pbs_ref[1], doh[1]))

            def tile(j, diag):
                off = pl.multiple_of(j * T, T)
                k = k_ref[pl.ds(off, T), :]
                v = v_ref[pl.ds(off, T), :]
                sc = [_dot_nt(qh[h], k) + fqh[h] - fr_ref[h:h + 1, pl.ds(off, T)] for h in hs]
                dp = [_dot_nt(doh[h], v) for h in hs]
                if not diag:
                    flush(j + 1)
                p = [jnp.exp(sc[h] - lseh[h]) for h in hs]
                if diag:
                    p = [jnp.where(causal, p[h], 0.0) for h in hs]
                dl = [p[h] * (dp[h] - dsum[h]) for h in hs]
                for h in hs:
                    dls_ref[h] = dl[h].astype(BF16)
                    pbs_ref[h] = p[h].astype(BF16)
                    dfa_ref[h:h + 1, pl.ds(off, T)] -= jnp.sum(dl[h], axis=0, keepdims=True)

            tile(i, True)
            j_left = _fox_tiles_left(i, pl.program_id(0), nq, fb_ref, tile)
            flush(j_left + 1)
            dq_ref[rows, :] = (jnp.where(m0, dqa_ref[0], dqa_ref[1]) * Q_SCALE).astype(BF16)

        for n in range(QPS):
            query_tile(QPS * step_id + n, slice(n * T, (n + 1) * T))

        @pl.when(step_id == nq // QPS - 1)
        def _():
            dk_ref[...] = dka_ref[...].astype(BF16)
            dv_ref[...] = dva_ref[...].astype(BF16)
            df_ref[...] = dfa_ref[...]

    nb = FX_W // LANES
    base = 3 * SB_W // LANES
    (dq, dk, dv, df), lands = _ride(dict(
        body=body, name=name, grid=(nb, nq // QPS),
        in_specs=[pl.BlockSpec((QPS * T, LANES), lambda p, i: (i, base + p)),
                  pl.BlockSpec((s, LANES), lambda p, i: (0, base + nb + p)),
                  pl.BlockSpec((s, LANES), lambda p, i: (0, base + 2 * nb + p)),
                  pl.BlockSpec((QPS * T, LANES), lambda p, i: (i, p)),
                  pl.BlockSpec((None, 2, s), lambda p, i: (p, 0, 0)),
                  pl.BlockSpec(memory_space=pltpu.SMEM),
                  pl.BlockSpec((QPS * T, LANES), lambda p, i: (i, p)),
                  pl.BlockSpec((QPS * T, LANES), lambda p, i: (i, p)),
                  pl.BlockSpec((QPS * T, LANES), lambda p, i: (i, p))],
        out_specs=[pl.BlockSpec((QPS * T, LANES), lambda p, i: (i, p)),
                   pl.BlockSpec((s, LANES), lambda p, i: (0, p)),
                   pl.BlockSpec((s, LANES), lambda p, i: (0, p)),
                   pl.BlockSpec((None, 2, s), lambda p, i: (p, 0, 0))],
        out_shape=[jax.ShapeDtypeStruct((s, FX_W), BF16)] * 3
        + [jax.ShapeDtypeStruct((nb, 2, s), F32)],
        scratch_shapes=[pltpu.VMEM((2, T, LANES), F32), pltpu.VMEM((s, LANES), F32),
                        pltpu.VMEM((s, LANES), F32), pltpu.VMEM((2, s), F32),
                        pltpu.VMEM((2, T, T), BF16), pltpu.VMEM((2, T, T), BF16)],
        compiler_params=_ARB2, operands=[qkv, qkv, qkv, fqb, frow, fbounds, o, lse, do]), rider)
    return dq, dk, dv, df, lands


def _fox_prep_bwd(dfrow, gf, bpad, name):
    s = gf.shape[0]

    def body(df_ref, fl_ref, b_ref, dfl_ref, db_ref):
        tri = jnp.where(_iota2((T, T), 0) <= _iota2((T, T), 1), 1.0, 0.0).astype(BF16)
        carry = jnp.zeros((1, LANES), F32)
        db = jnp.zeros((1, LANES), F32)
        fill = jnp.zeros((LANES - FOX_HEADS, T), F32)
        for blk in reversed(range(s // T)):
            rows = slice(blk * T, (blk + 1) * T)
            c = _tri3(tri, jnp.concatenate([df_ref[:, rows], fill], axis=0), _dot_nt) + carry
            carry = c[0:1, :]
            dfl = c / (1.0 + jnp.exp(fl_ref[rows, :] + b_ref[...]))
            dfl_ref[rows, :] = dfl.astype(BF16)
            db = db + jnp.sum(dfl, axis=0, keepdims=True)
        db_ref[...] = db

    return pl.pallas_call(
        body, name=name, grid=(1,),
        in_specs=[pl.BlockSpec((FOX_HEADS, s), lambda i: (0, 0)),
                  pl.BlockSpec((s, LANES), lambda i: (0, MIX_W // LANES)),
                  pl.BlockSpec((1, LANES), lambda i: (0, 0))],
        out_specs=[pl.BlockSpec((s, LANES), lambda i: (0, 0)),
                   pl.BlockSpec((1, LANES), lambda i: (0, 0))],
        out_shape=[jax.ShapeDtypeStruct((s, LANES), BF16), jax.ShapeDtypeStruct((1, LANES), F32)],
        compiler_params=_ARB1,
    )(dfrow, gf, bpad)


def _mem_bwd(qkv, kv, o, lse, do, name):
    s = qkv.shape[0]
    n = kv.shape[0]

    def body(q_ref, k_ref, v_ref, o_ref, lse_ref, do_ref, dq_ref, dk_ref, dv_ref):
        @pl.when(pl.program_id(1) == 0)
        def _():
            dk_ref[...] = jnp.zeros_like(dk_ref)
            dv_ref[...] = jnp.zeros_like(dv_ref)

        m0 = _pair_masks()
        qh = _split_pair(q_ref[...] * jnp.asarray(Q_SCALE, BF16), m0)
        doh = _split_pair(do_ref[...], m0)
        dsum = _row_dots(do_ref[...], o_ref[...], m0)
        lse = lse_ref[...]
        lseh = (lse[:, 0:1], lse[:, HEAD_DIM:HEAD_DIM + 1])
        k = k_ref[...]
        v = v_ref[...]
        dqs = []
        for h in range(2):
            p = jnp.exp(_dot_nt(qh[h], k) - lseh[h])
            dl = p * (_dot_nt(doh[h], v) - dsum[h])
            dlb = dl.astype(BF16)
            dqs.append(_dot(dlb, k))
            dk_ref[...] += _dot_tn(dlb, qh[h])
            dv_ref[...] += _dot_tn(p.astype(BF16), doh[h])
        dq_ref[...] = (jnp.where(m0, dqs[0], dqs[1]) * Q_SCALE).astype(BF16)

    nb = MEM_W // LANES
    base = (3 * SB_W + 3 * FX_W) // LANES
    return pl.pallas_call(
        body, name=name, grid=(nb, s // TQM),
        in_specs=[pl.BlockSpec((TQM, LANES), lambda p, i: (i, base + p)),
                  pl.BlockSpec((n, LANES), lambda p, i: (0, p)),
                  pl.BlockSpec((n, LANES), lambda p, i: (0, nb + p)),
                  pl.BlockSpec((TQM, LANES), lambda p, i: (i, p)),
                  pl.BlockSpec((TQM, LANES), lambda p, i: (i, p)),
                  pl.BlockSpec((TQM, LANES), lambda p, i: (i, p))],
        out_specs=[pl.BlockSpec((TQM, LANES), lambda p, i: (i, p)),
                   pl.BlockSpec((n, LANES), lambda p, i: (0, p)),
                   pl.BlockSpec((n, LANES), lambda p, i: (0, p))],
        out_shape=[jax.ShapeDtypeStruct((s, MEM_W), BF16), jax.ShapeDtypeStruct((n, MEM_W), F32),
                   jax.ShapeDtypeStruct((n, MEM_W), F32)],
        compiler_params=_ARB2,
    )(qkv, kv, kv, o, lse, do)


def _memkv_bwd(mem, mnw, wkv, dk, dv, name):
    n = mem.shape[0]

    def body(mem_ref, mnw_ref, w_ref, dk_ref, dv_ref, dw_ref, dmnw_ref):
        mv = mem_ref[...]
        r = lax.rsqrt(jnp.mean(mv * mv, axis=-1, keepdims=True) + EPS)
        mh = mv * r
        hm = (mh * mnw_ref[...]).astype(BF16)
        dkv = jnp.concatenate([dk_ref[...], dv_ref[...]], axis=1).astype(BF16)
        dw_ref[...] = _dot_tn(hm, dkv)
        dhm = _dot_nt(dkv, w_ref[...])
        dmnw_ref[...] = jnp.sum(dhm * mh, axis=0, keepdims=True)

    return pl.pallas_call(
        body, name=name, grid=(1,),
        in_specs=[pl.BlockSpec((n, D_MODEL), lambda i: (0, 0)),
                  pl.BlockSpec((1, D_MODEL), lambda i: (0, 0)),
                  pl.BlockSpec((D_MODEL, 2 * MEM_W), lambda i: (0, 0)),
                  pl.BlockSpec((n, MEM_W), lambda i: (0, 0)),
                  pl.BlockSpec((n, MEM_W), lambda i: (0, 0))],
        out_specs=[pl.BlockSpec((D_MODEL, 2 * MEM_W), lambda i: (0, 0)),
                   pl.BlockSpec((1, D_MODEL), lambda i: (0, 0))],
        out_shape=[jax.ShapeDtypeStruct((D_MODEL, 2 * MEM_W), F32),
                   jax.ShapeDtypeStruct((1, D_MODEL), F32)],
        compiler_params=_ARB1,
    )(mem, mnw, wkv, dk, dv)


def _inproj_bwd_dx(pieces, w_r, x, nw, dxo, name, rider=None):
    s = x.shape[0]
    n = len(pieces)
    widths = [p.shape[1] for p in pieces]

    def body(*refs):
        piece_refs = refs[:n]
        w_ref, x_ref, nw_ref, dxo_ref, dx_ref, h_ref, dnw_ref, dp_ref = refs[n:]

        @pl.when(pl.program_id(0) == 0)
        def _():
            dnw_ref[...] = jnp.zeros_like(dnw_ref)

        col = 0
        for r, wd in zip(piece_refs, widths):
            dp_ref[:, col:col + wd] = r[...]
            col += wd
        dp_ref[:, col:] = jnp.zeros((TM, WR_W - col), BF16)
        dh = _dot(dp_ref[...], w_ref[...])
        xv = x_ref[...]
        nw = nw_ref[...]
        r = lax.rsqrt(jnp.mean(xv * xv, axis=-1, keepdims=True) + EPS)
        xh = xv * r
        h_ref[...] = (xh * nw).astype(BF16)
        dnw_ref[...] += jnp.sum(dh * xh, axis=0, keepdims=True)
        dxh = dh * nw
        dx_ref[...] = r * (dxh - xh * jnp.mean(dxh * xh, axis=-1, keepdims=True)) + dxo_ref[...]

    (dx, h, dnw, dproj), lands = _ride(dict(
        body=body, name=name, grid=(s // TM,),
        in_specs=[pl.BlockSpec((TM, wd), lambda i: (i, 0)) for wd in widths]
        + [pl.BlockSpec((WR_W, D_MODEL), lambda i: (0, 0)),
           pl.BlockSpec((TM, D_MODEL), lambda i: (i, 0)),
           pl.BlockSpec((1, D_MODEL), lambda i: (0, 0)),
           pl.BlockSpec((TM, D_MODEL), lambda i: (i, 0))],
        out_specs=[pl.BlockSpec((None, TM, D_MODEL), lambda i: (0, i, 0)),
                   pl.BlockSpec((TM, D_MODEL), lambda i: (i, 0)),
                   pl.BlockSpec((1, D_MODEL), lambda i: (0, 0)),
                   pl.BlockSpec((TM, WR_W), lambda i: (i, 0))],
        out_shape=[jax.ShapeDtypeStruct((1, s, D_MODEL), F32), jax.ShapeDtypeStruct((s, D_MODEL), BF16),
                   jax.ShapeDtypeStruct((1, D_MODEL), F32), jax.ShapeDtypeStruct((s, WR_W), BF16)],
        scratch_shapes=[], compiler_params=_ARB1, operands=[*pieces, w_r, x, nw, dxo]), rider)
    return dx, h, dnw, dproj, lands


def _inproj_bwd_dw(h, dproj, name):
    s = dproj.shape[0]
    tn = 256

    def body(h_ref, dp_ref, dw_ref):
        dw_ref[...] = _dot_tn(dp_ref[...], h_ref[...]).astype(BF16)

    return pl.pallas_call(
        body, name=name, grid=(WR_W // tn,),
        in_specs=[pl.BlockSpec((s, D_MODEL), lambda j: (0, 0)),
                  pl.BlockSpec((s, tn), lambda j: (0, j))],
        out_specs=pl.BlockSpec((tn, D_MODEL), lambda j: (j, 0)),
        out_shape=jax.ShapeDtypeStruct((WR_W, D_MODEL), BF16),
        compiler_params=_ARB1,
    )(h, dproj)


def _rearrange_w_in(wt):
    pad = jnp.zeros((FL_PAD - FOX_HEADS,) + wt.shape[1:], wt.dtype)
    return jnp.concatenate([wt[:3072], wt[3080:3336], wt[3336:IN_W], wt[3072:3080], pad], axis=0)


_W_IN_SEGMENTS = [(0, 3072, 0), (3072, 3080, QKV_W + MIX_W), (3080, 3336, 3072), (3336, IN_W, QKV_W)]
ASSEMBLE_COLS = 256
ASSEMBLE_ROWS = 128


def _shard_pieces(k):
    lo, hi = k * SHARD_W, (k + 1) * SHARD_W
    return [(max(lo, a) - lo, at + max(lo, a) - a, min(hi, b) - max(lo, a))
            for a, b, at in _W_IN_SEGMENTS if max(lo, a) < min(hi, b)]


def _assemble_w_in(v, to_blocks, name):
    used = QKV_W + MIX_W + FOX_HEADS

    def body(v_ref, o_ref):
        for k in range(N_DEV):
            for at, w_at, n in _shard_pieces(k):
                for r in range(0, n, ASSEMBLE_ROWS):
                    m = min(ASSEMBLE_ROWS, n - r)
                    if to_blocks:
                        o_ref[k, at + r:at + r + m, :] = v_ref[w_at + r:w_at + r + m, :]
                    else:
                        o_ref[w_at + r:w_at + r + m, :] = v_ref[k, at + r:at + r + m, :]
        if not to_blocks:
            o_ref[used:, :] = jnp.zeros((WR_W - used, ASSEMBLE_COLS), BF16)

    blocks = pl.BlockSpec((N_DEV, SHARD_W, ASSEMBLE_COLS), lambda j: (0, 0, j))
    rows = pl.BlockSpec((WR_W, ASSEMBLE_COLS), lambda j: (0, j))
    out_shape = (N_DEV, SHARD_W, D_MODEL) if to_blocks else (WR_W, D_MODEL)
    return pl.pallas_call(
        body, name=name, grid=(D_MODEL // ASSEMBLE_COLS,),
        in_specs=[rows if to_blocks else blocks], out_specs=blocks if to_blocks else rows,
        out_shape=jax.ShapeDtypeStruct(out_shape, BF16), compiler_params=_ARB1)(v)


def _restore_w_in(g):
    gate0 = QKV_W
    fl0 = QKV_W + MIX_W
    return jnp.concatenate(
        [g[:3072], g[fl0:fl0 + FOX_HEADS], g[3072:QKV_W], g[gate0:fl0]], axis=0)


def _pad_lanes(v, width=LANES):
    return jnp.pad(v, (0, width - v.shape[0])).reshape(1, width)


def _layer_fwd(xs, mem, nw, w_r, b_forget, mnw, late, onw, l, travel=None):
    s = xs.shape[0]
    bpad = _pad_lanes(b_forget)
    travel = _Travel(travel)
    qkv, gf, _ = travel.ride(2, _inproj_fwd, xs, nw, w_r, f"inproj_fwd_{l}")
    fqb, frow, fbounds = _fox_prep_fwd(gf, qkv, bpad, f"fox_prep_fwd_{l}")
    frow = frow.reshape(FOX_HEADS // 2, 2, s)
    ysb, _ = travel.ride(0, _sb_fwd, qkv, f"sb_fwd_{l}")
    yfx, lse_fx, _ = travel.ride(1, _fox_fwd, qkv, fqb, frow, fbounds, f"fox_fwd_{l}")
    wkv, wout = late(travel.lands)
    kv = _memkv_fwd(mem, mnw, wkv, f"memkv_fwd_{l}")
    ym, lse_m = _mem_fwd(qkv, kv, f"mem_fwd_{l}")
    xn = _outproj_fwd(ysb, yfx, ym, gf, onw, wout, xs, f"outproj_fwd_{l}")
    saved = (xs, nw, mnw, onw, bpad, qkv, gf, fqb, frow, fbounds, ysb, yfx, lse_fx, kv, ym, lse_m)
    return xn, saved, travel.lands, (wkv, wout)


class _Travel:
    def __init__(self, plan):
        self.plan = plan
        self.lands = None if plan is None else _new_lands(plan[0], plan[1])

    def ride(self, n, fn, *args):
        if self.plan is None or n >= len(self.plan[2]) or self.plan[2][n] is None:
            return fn(*args)
        srcs, scatter, legs = self.plan
        idx, rows = legs[n]
        out = fn(*args, rider=_Rider([srcs[a] for a in idx], [self.lands[a] for a in idx],
                                     scatter, rows, relay=not scatter))
        for a, land in zip(idx, out[-1]):
            self.lands[a] = land
        return out


def _layer_bwd(dx, saved, mem, w_r, wkv, wout, l, travel=None, own_w_out=None):
    xs, nw, mnw, onw, bpad, qkv, gf, fqb, frow, fbounds, ysb, yfx, lse_fx, kv, ym, lse_m = saved
    s = xs.shape[0]
    travel = _Travel(travel)
    dysb, dyfx, dym, dgate, dwout, donw, _ = travel.ride(
        3, _outproj_bwd, dx, wout, ysb, yfx, ym, gf, onw, f"outproj_bwd_{l}")
    if own_w_out is not None:
        travel.plan[0][own_w_out[0]] = own_w_out[1](dwout)
    sdq, sdk, sdv, _ = travel.ride(0, _sb_bwd, qkv, ysb, dysb, f"sb_bwd_{l}")
    fdq, fdk, fdv, dfrow, _ = travel.ride(1, _fox_bwd, qkv, fqb, frow, fbounds, yfx, lse_fx, dyfx,
                                          f"fox_bwd_{l}")
    dfl, db = _fox_prep_bwd(dfrow.reshape(FOX_HEADS, s), gf, bpad, f"fox_prep_bwd_{l}")
    dmq, dmk, dmv = _mem_bwd(qkv, kv, ym, lse_m, dym, f"mem_bwd_{l}")
    dwkv, dmnw = _memkv_bwd(mem, mnw, wkv, dmk, dmv, f"memkv_bwd_{l}")
    dx, ht, dnw, dproj, _ = travel.ride(2, _inproj_bwd_dx,
                                        [sdq, sdk, sdv, fdq, fdk, fdv, dmq, dgate, dfl],
                                        w_r, xs, nw, dx, f"inproj_bwd_dx_{l}")
    dwr = _inproj_bwd_dw(ht, dproj, f"inproj_bwd_dw_{l}")
    grads = dict(norm_w=dnw[0], w_r=dwr, b_forget=db[0, :FOX_HEADS], mem_norm_w=dmnw[0],
                 w_mem_kv=dwkv, out_norm_w=donw[0], w_out=dwout)
    return dx, grads, travel.lands


_ANY = pl.BlockSpec(memory_space=pl.ANY)


def _my_place():
    return lax.axis_index("x"), lax.axis_index("y"), lax.axis_index("c")


def _flip(v, bit):
    return 1 - v if bit else v


def _block_index(px, py, pc):
    return 4 * px + 2 * py + pc


def _all_gather_weights(shards, name):
    n = len(shards)

    def body(*refs):
        ins, outs = refs[:n], refs[n:2 * n]
        send_sems, recv_sems, local_sems = refs[2 * n:]
        x, y, c = _my_place()
        me = (x, y, c)
        sibling = (x, y, 1 - c)
        chips = [(1 - x, y), (x, 1 - y), (1 - x, 1 - y)]

        def copy(a, k, block, to, src=None):
            dst = outs[a].at[_block_index(*block)]
            return pltpu.make_async_remote_copy(
                src_ref=dst if src is None else src, dst_ref=dst,
                send_sem=send_sems.at[a, k], recv_sem=recv_sems.at[a, k],
                device_id=to, device_id_type=pl.DeviceIdType.MESH)

        mine = [pltpu.make_async_copy(ins[a], outs[a].at[_block_index(*me)], local_sems.at[a])
                for a in range(n)]
        for cp in mine:
            cp.start()
        first = []
        for a in range(n):
            first.append(copy(a, 0, me, sibling, src=ins[a]))
            first += [copy(a, 1 + j, me, (*chip, c), src=ins[a]) for j, chip in enumerate(chips)]
        for cp in first:
            cp.start()
        passed = []
        for j, chip in enumerate(chips):
            for a in range(n):
                copy(a, 1 + j, (*chip, c), me).wait_recv()
                fwd = copy(a, 4 + j, (*chip, c), sibling)
                fwd.start()
                passed.append(fwd)
        for a in range(n):
            copy(a, 0, sibling, me).wait_recv()
            for j, chip in enumerate(chips):
                copy(a, 4 + j, (*chip, 1 - c), me).wait_recv()
        for cp in first + passed:
            cp.wait_send()
        for cp in mine:
            cp.wait()

    return pl.pallas_call(
        body, name=name,
        in_specs=[_ANY] * n, out_specs=[_ANY] * n,
        out_shape=[jax.ShapeDtypeStruct((N_DEV,) + v.shape, v.dtype) for v in shards],
        scratch_shapes=[pltpu.SemaphoreType.DMA((n, 7)), pltpu.SemaphoreType.DMA((n, 7)),
                        pltpu.SemaphoreType.DMA((n,))],
    )(*shards)


def _exchange_blocks(blocked, name):
    n = len(blocked)

    def body(*refs):
        ins, outs = refs[:n], refs[n:2 * n]
        send_sems, recv_sems, local_sems = refs[2 * n:]
        x, y, c = _my_place()
        mine_idx = _block_index(x, y, c)
        local = [pltpu.make_async_copy(ins[a].at[mine_idx], outs[a].at[mine_idx], local_sems.at[a])
                 for a in range(n)]
        for cp in local:
            cp.start()
        sends, arrivals = [], []
        for r in range(1, N_DEV):
            peer = (_flip(x, r & 4), _flip(y, r & 2), _flip(c, r & 1))
            peer_idx = _block_index(*peer)
            for a in range(n):
                sems = dict(send_sem=send_sems.at[a, r - 1], recv_sem=recv_sems.at[a, r - 1],
                            device_id=peer, device_id_type=pl.DeviceIdType.MESH)
                sends.append(pltpu.make_async_remote_copy(
                    src_ref=ins[a].at[peer_idx], dst_ref=outs[a].at[mine_idx], **sems))
                arrivals.append(pltpu.make_async_remote_copy(
                    src_ref=ins[a].at[peer_idx], dst_ref=outs[a].at[peer_idx], **sems))
        for cp in sends:
            cp.start()
        for cp in arrivals:
            cp.wait_recv()
        for cp in sends:
            cp.wait_send()
        for cp in local:
            cp.wait()

    return pl.pallas_call(
        body, name=name,
        in_specs=[_ANY] * n, out_specs=[_ANY] * n,
        out_shape=[jax.ShapeDtypeStruct(v.shape, v.dtype) for v in blocked],
        scratch_shapes=[pltpu.SemaphoreType.DMA((n, 7)), pltpu.SemaphoreType.DMA((n, 7)),
                        pltpu.SemaphoreType.DMA((n,))],
    )(*blocked)


N_CHIP = N_DEV // 2


def _pair_swap(blocked, name):
    n = len(blocked)

    def body(*refs):
        ins, outs = refs[:n], refs[n:2 * n]
        send_sems, recv_sems = refs[2 * n:]
        x, y, c = _my_place()
        copies = [pltpu.make_async_remote_copy(
            src_ref=ins[a].at[j, 1 - c], dst_ref=outs[a].at[j],
            send_sem=send_sems.at[N_CHIP * a + j], recv_sem=recv_sems.at[N_CHIP * a + j],
            device_id=(x, y, 1 - c), device_id_type=pl.DeviceIdType.MESH)
            for a in range(n) for j in range(N_CHIP)]
        for cp in copies:
            cp.start()
        for cp in copies:
            cp.wait_recv()
        for cp in copies:
            cp.wait_send()

    return pl.pallas_call(
        body, name=name,
        in_specs=[_ANY] * n, out_specs=[_ANY] * n,
        out_shape=[jax.ShapeDtypeStruct((N_CHIP,) + v.shape[2:], v.dtype) for v in blocked],
        scratch_shapes=[pltpu.SemaphoreType.DMA((N_CHIP * n,)),
                        pltpu.SemaphoreType.DMA((N_CHIP * n,))],
    )(*blocked)


def _pair_add(by_core, theirs, core, name):
    n = len(by_core)

    def body(core_ref, *refs):
        for a in range(n):
            mine, got, out = refs[a], refs[n + a], refs[2 * n + a]
            out[...] = (mine[...].astype(F32) + got[...].astype(F32)).astype(BF16)

    def own(v):
        return pl.BlockSpec((None, None) + v.shape[2:], lambda j, core_ref: (j, core_ref[0], 0, 0))

    def block(v):
        return pl.BlockSpec((None,) + v.shape[1:], lambda j, core_ref: (j, 0, 0))

    return list(pl.pallas_call(
        body, name=name,
        grid_spec=pltpu.PrefetchScalarGridSpec(
            num_scalar_prefetch=1, grid=(N_CHIP,),
            in_specs=[own(v) for v in by_core] + [block(v) for v in theirs],
            out_specs=[block(v) for v in theirs]),
        out_shape=[jax.ShapeDtypeStruct(v.shape, BF16) for v in theirs],
        compiler_params=_ARB1,
    )(core, *by_core, *theirs))


def _chip_exchange(by_chip, to_all, name):
    n, m = len(by_chip), len(to_all)

    def body(*refs):
        ins, alls = refs[:n], refs[n:n + m]
        outs, all_outs = refs[n + m:2 * n + m], refs[2 * n + m:2 * (n + m)]
        send_sems, recv_sems, local_sems = refs[2 * (n + m):]
        x, y, c = _my_place()
        my_chip, mine_idx = 2 * x + y, _block_index(x, y, c)
        local = [pltpu.make_async_copy(ins[a].at[my_chip], outs[a].at[my_chip], local_sems.at[a])
                 for a in range(n)]
        local += [pltpu.make_async_copy(alls[b].at[mine_idx], all_outs[b].at[mine_idx],
                                        local_sems.at[n + b]) for b in range(m)]
        for cp in local:
            cp.start()
        sends, arrivals = [], []
        k = 0
        for r in range(1, N_DEV):
            peer = (_flip(x, r & 4), _flip(y, r & 2), _flip(c, r & 1))
            peer_chip, peer_idx = 2 * peer[0] + peer[1], _block_index(*peer)
            pairs = [(alls[b].at[mine_idx], all_outs[b].at[mine_idx], all_outs[b].at[peer_idx])
                     for b in range(m)]
            if not r & 1:
                pairs += [(ins[a].at[peer_chip], outs[a].at[my_chip], outs[a].at[peer_chip])
                          for a in range(n)]
            for src, there, here in pairs:
                sems = dict(send_sem=send_sems.at[k], recv_sem=recv_sems.at[k], device_id=peer,
                            device_id_type=pl.DeviceIdType.MESH)
                sends.append(pltpu.make_async_remote_copy(src_ref=src, dst_ref=there, **sems))
                arrivals.append(pltpu.make_async_remote_copy(src_ref=src, dst_ref=here, **sems))
                k += 1
        for cp in sends:
            cp.start()
        for cp in arrivals:
            cp.wait_recv()
        for cp in sends:
            cp.wait_send()
        for cp in local:
            cp.wait()

    n_copies = 7 * m + 3 * n
    return pl.pallas_call(
        body, name=name,
        in_specs=[_ANY] * (n + m), out_specs=[_ANY] * (n + m),
        out_shape=[jax.ShapeDtypeStruct(v.shape, v.dtype) for v in by_chip + to_all],
        scratch_shapes=[pltpu.SemaphoreType.DMA((n_copies,)), pltpu.SemaphoreType.DMA((n_copies,)),
                        pltpu.SemaphoreType.DMA((n + m,))],
    )(*by_chip, *to_all)


class _Rider(NamedTuple):
    srcs: list
    lands: list
    scatter: bool
    part: list
    relay: bool = False


def _window(ref, part):
    if part is None:
        return ref
    dim, start, size = part
    return ref.at[(slice(None),) * dim + (pl.ds(start, size),)]


def _relay_copies(srcs, lands, send_sems, recv_sems, rider):
    x, y, c = _my_place()
    me, sibling = (x, y, c), (x, y, 1 - c)
    chips = [(1 - x, y), (x, 1 - y), (1 - x, 1 - y)]
    first, from_chips, passed, last = [], [], [], []
    for a in range(len(srcs)):
        def copy(k, block, to, src=None, a=a):
            dst = _window(lands[a].at[_block_index(*block)], rider.part[a])
            return pltpu.make_async_remote_copy(
                src_ref=dst if src is None else src, dst_ref=dst,
                send_sem=send_sems.at[7 * a + k], recv_sem=recv_sems.at[7 * a + k],
                device_id=to, device_id_type=pl.DeviceIdType.MESH)

        mine = _window(srcs[a], rider.part[a])
        first.append(copy(0, me, sibling, src=mine))
        first += [copy(1 + j, me, (*chip, c), src=mine) for j, chip in enumerate(chips)]
        from_chips += [copy(1 + j, (*chip, c), me) for j, chip in enumerate(chips)]
        passed += [copy(4 + j, (*chip, c), sibling) for j, chip in enumerate(chips)]
        last.append(copy(0, sibling, me))
        last += [copy(4 + j, (*chip, 1 - c), me) for j, chip in enumerate(chips)]
    return first, from_chips, passed, last


def _new_lands(srcs, scatter):
    return [lax.empty(v.shape if scatter else (N_DEV,) + v.shape, v.dtype) for v in srcs]


def _rider_copies(srcs, lands, send_sems, recv_sems, rider):
    x, y, c = _my_place()
    mine_idx = _block_index(x, y, c)

    def window(ref, a):
        return _window(ref, rider.part[a])

    sends, arrivals = [], []
    for r in range(1, N_DEV):
        peer = (_flip(x, r & 4), _flip(y, r & 2), _flip(c, r & 1))
        peer_idx = _block_index(*peer)
        for a in range(len(srcs)):
            src = window(srcs[a].at[peer_idx] if rider.scatter else srcs[a], a)
            k = 7 * a + r - 1
            sems = dict(send_sem=send_sems.at[k], recv_sem=recv_sems.at[k],
                        device_id=peer, device_id_type=pl.DeviceIdType.MESH)
            sends.append(pltpu.make_async_remote_copy(
                src_ref=src, dst_ref=window(lands[a].at[mine_idx], a), **sems))
            arrivals.append(pltpu.make_async_remote_copy(
                src_ref=src, dst_ref=window(lands[a].at[peer_idx], a), **sems))
    return sends, arrivals


def _ride(call, rider):
    call = dict(call)
    body, grid = call.pop("body"), call["grid"]
    operands = call.pop("operands")
    if rider is None:
        return list(pl.pallas_call(body, **call)(*operands)), None
    n_in, n_out = len(call["in_specs"]), len(call["out_specs"])
    n_scratch = len(call["scratch_shapes"])
    m = len(rider.srcs)

    def riding(*refs):
        main_in, srcs, lands = refs[:n_in], refs[n_in:n_in + m], refs[n_in + m:n_in + 2 * m]
        main_out = refs[n_in + 2 * m:n_in + 2 * m + n_out]
        rest = refs[n_in + 3 * m + n_out:]
        send_sems, recv_sems = rest[n_scratch:]
        at = [pl.program_id(d) for d in range(len(grid))]
        first = functools.reduce(jnp.logical_and, [p == 0 for p in at])
        last = functools.reduce(jnp.logical_and, [p == g - 1 for p, g in zip(at, grid)])
        if rider.relay:
            sends, from_chips, passed, arrivals = _relay_copies(
                srcs, lands, send_sems, recv_sems, rider)
            step, steps = 0, 1
            for p, g in zip(at, grid):
                step, steps = step * g + p, steps * g
            assert steps >= 2, "a relayed gather needs a later grid step to pass blocks on"

            @pl.when(step == (3 * steps) // 4)
            def _():
                for cp in from_chips:
                    cp.wait_recv()
                for cp in passed:
                    cp.start()
        else:
            sends, arrivals = _rider_copies(srcs, lands, send_sems, recv_sems, rider)
            passed = []

        @pl.when(first)
        def _():
            for cp in sends:
                cp.start()

        body(*main_in, *main_out, *rest[:n_scratch])

        @pl.when(last)
        def _():
            for cp in arrivals:
                cp.wait_recv()
            for cp in sends + passed:
                cp.wait_send()

    call["in_specs"] = list(call["in_specs"]) + [_ANY] * (2 * m)
    call["out_specs"] = list(call["out_specs"]) + [_ANY] * m
    call["out_shape"] = list(call["out_shape"]) + [
        jax.ShapeDtypeStruct(v.shape, v.dtype) for v in rider.lands]
    call["scratch_shapes"] = list(call["scratch_shapes"]) + [
        pltpu.SemaphoreType.DMA((7 * m,)), pltpu.SemaphoreType.DMA((7 * m,))]
    call["input_output_aliases"] = {n_in + m + a: n_out + a for a in range(m)}
    outs = pl.pallas_call(riding, **call)(*operands, *rider.srcs, *rider.lands)
    return list(outs[:n_out]), list(outs[n_out:])


def _sum_parts(p_ref):
    g = p_ref[0].astype(F32)
    for k in range(1, p_ref.shape[0]):
        g = g + p_ref[k].astype(F32)
    return g


def _adamw(g, w, m, v):
    c1 = 1.0 / (1.0 - ADAM_B1 ** ADAM_STEP)
    c2 = 1.0 / (1.0 - ADAM_B2 ** ADAM_STEP)
    nm = ADAM_B1 * m + (1.0 - ADAM_B1) * g
    nv = ADAM_B2 * v + (1.0 - ADAM_B2) * (g * g)
    return nm, nv, -ADAM_LR * ((nm * c1) / (jnp.sqrt(nv * c2) + ADAM_EPS) + ADAM_WD * w)


def _adamw_w_in(parts, w, m, v, name):
    ncol_blk, depth, nfeat = w.shape
    cols = 256

    def body(*refs):
        p_refs = refs[:depth]
        w_ref, m_ref, v_ref, g_ref, d_ref, nm_ref, nv_ref = refs[depth:]
        for l in range(depth):
            g = _sum_parts(p_refs[l])
            nm, nv, d = _adamw(g, w_ref[:, l, :], m_ref[:, l, :], v_ref[:, l, :])
            g_ref[:, l, :] = g
            nm_ref[:, l, :] = nm
            nv_ref[:, l, :] = nv
            d_ref[:, l, :] = d

    blk = pl.BlockSpec((ncol_blk, depth, cols), lambda j: (0, 0, j))
    return pl.pallas_call(
        body, name=name, grid=(nfeat // cols,),
        in_specs=[pl.BlockSpec((p.shape[0], ncol_blk, cols), lambda j: (0, 0, j)) for p in parts]
        + [blk] * 3,
        out_specs=[blk] * 4,
        out_shape=[jax.ShapeDtypeStruct(w.shape, F32)] * 4,
        compiler_params=_ARB1,
    )(*parts, w, m, v)


def _adamw_sum(parts, w, m, v, tile, name):
    depth, nrow, ncol = w.shape
    rows, cols = tile

    def body(*refs):
        p_refs = refs[:depth]
        w_ref, m_ref, v_ref, g_ref, d_ref, nm_ref, nv_ref = refs[depth:]
        layer = pl.program_id(0)
        for l in range(depth):
            @pl.when(layer == l)
            def _(p_ref=p_refs[l]):
                g = _sum_parts(p_ref)
                nm, nv, d = _adamw(g, w_ref[...], m_ref[...], v_ref[...])
                g_ref[...] = g
                nm_ref[...] = nm
                nv_ref[...] = nv
                d_ref[...] = d

    def part_spec(l):
        return pl.BlockSpec((parts[l].shape[0], rows, cols), lambda q, i, j: (
            0, jnp.where(q == l, i, 0), jnp.where(q == l, j, 0)))

    blk = pl.BlockSpec((None, rows, cols), lambda q, i, j: (q, i, j))
    return pl.pallas_call(
        body, name=name, grid=(depth, nrow // rows, ncol // cols),
        in_specs=[part_spec(l) for l in range(depth)] + [blk, blk, blk],
        out_specs=[blk] * 4,
        out_shape=[jax.ShapeDtypeStruct(w.shape, F32)] * 4,
        compiler_params=pltpu.CompilerParams(
            dimension_semantics=("arbitrary", "arbitrary", "arbitrary")),
    )(*parts, w, m, v)


def _adamw_small(parts, ws, ms, vs, name):
    n = len(parts)

    def body(*refs):
        p, w, m, v = (refs[a * n:(a + 1) * n] for a in range(4))
        out = refs[4 * n:]
        for a in range(n):
            g = _sum_parts(p[a])
            nm, nv, d = _adamw(g, w[a][...], m[a][...], v[a][...])
            out[a][...] = g
            out[n + a][...] = d
            out[2 * n + a][...] = nm
            out[3 * n + a][...] = nv

    vmem = pl.BlockSpec(memory_space=pltpu.VMEM)
    outs = pl.pallas_call(
        body, name=name, in_specs=[vmem] * (4 * n), out_specs=[vmem] * (4 * n),
        out_shape=[jax.ShapeDtypeStruct(w.shape, F32) for w in ws] * 4,
    )(*parts, *ws, *ms, *vs)
    return [list(outs[a * n:(a + 1) * n]) for a in range(4)]


def _misc_rows(b_forget, extra=None):
    tile = jnp.pad(b_forget, ((0, 6), (0, LANES - FOX_HEADS)))
    return tile if extra is None else tile.at[2].set(extra)


def kernel(x, mem, norm_w, w_in, b_forget, mem_norm_w, w_mem_kv, out_norm_w, w_out, final_norm_w, loss_target, m_norm_w, m_w_in, m_b_forget, m_mem_norm_w, m_w_mem_kv, m_out_norm_w, m_w_out, m_final_norm_w, v_norm_w, v_w_in, v_b_forget, v_mem_norm_w, v_w_mem_kv, v_out_norm_w, v_w_out, v_final_norm_w):
    kv_rows = w_mem_kv.shape[1]
    out_rows = w_out.shape[1]
    me = _block_index(*_my_place())

    def shards(l):
        return [w_in[l].T.astype(BF16), w_mem_kv[l].astype(BF16), w_out[l].astype(BF16)]


    def full_kv_out(g_kv, g_out):
        return g_kv.reshape(D_MODEL, 2 * MEM_W), g_out.reshape(MIX_W, D_MODEL)

    def kv_blocks(g):
        return g.reshape(N_DEV, kv_rows, 2 * MEM_W).astype(BF16)

    def out_blocks(g):
        return g.reshape(N_DEV, out_rows, D_MODEL).astype(BF16)

    def with_own(land, own):
        return lax.dynamic_update_slice(land, own[None], (me,) + (0,) * own.ndim)

    def with_own_of(land, blocked):
        return lax.dynamic_update_slice(land, lax.dynamic_slice_in_dim(blocked, me, 1, axis=0),
                                        (me,) + (0,) * (land.ndim - 1))

    def row(v):
        return v.reshape(1, -1)

    def cols(first, size):
        return (1, first, size)

    fwd_split, bwd_split = 4 * LANES, (5 * LANES, 6 * LANES)

    s_in0, s_kv0, s_out0 = shards(0)
    (g_in0,) = _all_gather_weights([s_in0], "all_gather_l0")
    w_r0 = _assemble_w_in(g_in0, False, "assemble_w_in_0")
    s_in1, s_kv1, s_out1 = shards(1)
    x1, saved0, (l_in1, _, _), (wkv0, wout0) = _layer_fwd(
        x[0], mem[0], row(norm_w[0]), w_r0, b_forget[0], row(mem_norm_w[0]),
        lambda lands: full_kv_out(with_own(lands[1], s_kv0), with_own(lands[2], s_out0)),
        row(out_norm_w[0]), 0,
        travel=([s_in1, s_kv0, s_out0], False,
                [([0], [cols(0, fwd_split)]), ([0], [cols(fwd_split, D_MODEL - fwd_split)]),
                 ([1, 2], [None, None])]))
    w_r1 = _assemble_w_in(with_own(l_in1, s_in1), False, "assemble_w_in_1")
    x2, saved1, _, (wkv1, wout1) = _layer_fwd(
        x1, mem[0], row(norm_w[1]), w_r1, b_forget[1], row(mem_norm_w[1]),
        lambda lands: full_kv_out(with_own(lands[0], s_kv1), with_own(lands[1], s_out1)),
        row(out_norm_w[1]), 1, travel=([s_kv1, s_out1], False, [([0, 1], [None, None]), None]))

    dx2, loss_part, dfnw = _final_fwd_bwd(x2, row(final_norm_w), loss_target[0], "final_fwd_bwd")

    dw_out_blocks = jax.ShapeDtypeStruct((N_DEV, out_rows, D_MODEL), BF16)
    dx1, gr1, (l_out1,) = _layer_bwd(
        dx2, saved1, mem[0], w_r1, wkv1, wout1, 1,
        travel=([dw_out_blocks], True, [([0], [None])]), own_w_out=(0, out_blocks))
    p_in1 = _assemble_w_in(gr1["w_r"], True, "w_in_grad_blocks_1")
    p_kv1 = kv_blocks(gr1["w_mem_kv"])
    grad_x, gr0, (l_in1, l_kv1, l_out0) = _layer_bwd(
        dx1[0], saved0, mem[0], w_r0, wkv0, wout0, 0,
        travel=([p_in1, p_kv1, dw_out_blocks], True,
                [([0, 1], [cols(0, bwd_split[0]), None]),
                 ([0, 2], [cols(bwd_split[0], bwd_split[1] - bwd_split[0]), None]),
                 None,
                 ([0], [cols(bwd_split[1], D_MODEL - bwd_split[1])])]),
        own_w_out=(2, out_blocks))
    r_out1 = with_own_of(l_out1, out_blocks(gr1["w_out"]))
    r_in1, r_kv1 = with_own_of(l_in1, p_in1), with_own_of(l_kv1, p_kv1)
    r_out0 = with_own_of(l_out0, out_blocks(gr0["w_out"]))

    def both(name):
        return jnp.stack([gr0[name], gr1[name]])

    small = [both("norm_w"), both("mem_norm_w"), both("out_norm_w"), dfnw,
             _misc_rows(both("b_forget"), loss_part[0])]
    p_small = [jnp.broadcast_to(v[None], (N_DEV,) + v.shape) for v in small]
    by_core = [v.reshape((N_CHIP, 2) + v.shape[1:])
               for v in (_assemble_w_in(gr0["w_r"], True, "w_in_grad_blocks_0"),
                         kv_blocks(gr0["w_mem_kv"]))]
    from_sibling = _pair_swap(by_core, "grads_l0_pair_swap")
    core = lax.axis_index("c").astype(jnp.int32).reshape(1)
    chip_sums = _pair_add(by_core, from_sibling, core, "grads_l0_pair_add")
    r_in0, r_kv0, *r_small = _chip_exchange(chip_sums, p_small, "exchange_grads_l0")

    def view(v):
        return jnp.transpose(v, (2, 0, 1))

    g_w_in, d_w_in, nm_w_in, nv_w_in = [jnp.transpose(v, (1, 2, 0)) for v in _adamw_w_in(
        [r_in0, r_in1], view(w_in), view(m_w_in), view(v_w_in), "adamw_w_in")]
    g_w_kv, d_w_kv, nm_w_kv, nv_w_kv = _adamw_sum(
        [r_kv0, r_kv1], w_mem_kv, m_w_mem_kv, v_w_mem_kv, (kv_rows, 2 * MEM_W), "adamw_w_mem_kv")
    g_w_out, d_w_out, nm_w_out, nv_w_out = _adamw_sum(
        [r_out0, r_out1], w_out, m_w_out, v_w_out, (out_rows, D_MODEL), "adamw_w_out")
    def smalls(nw, mnw, onw, fnw, b):
        return [nw, mnw, onw, row(fnw), _misc_rows(b)]

    small_out = _adamw_small(
        r_small, smalls(norm_w, mem_norm_w, out_norm_w, final_norm_w, b_forget),
        smalls(m_norm_w, m_mem_norm_w, m_out_norm_w, m_final_norm_w, m_b_forget),
        smalls(v_norm_w, v_mem_norm_w, v_out_norm_w, v_final_norm_w, v_b_forget), "adamw_small")
    (g_nw, g_mnw, g_onw, g_fnw, g_b), (d_nw, d_mnw, d_onw, d_fnw, d_b), \
        (nm_nw, nm_mnw, nm_onw, nm_fnw, nm_b), (nv_nw, nv_mnw, nv_onw, nv_fnw, nv_b) = [
            (nw, mnw, onw, fnw[0], misc[:2, :FOX_HEADS]) for nw, mnw, onw, fnw, misc in small_out]
    loss = small_out[0][4][2, 0]

    return (loss, grad_x,
            g_nw, g_w_in, g_b, g_mnw, g_w_kv, g_onw, g_w_out, g_fnw,
            d_nw, d_w_in, d_b, d_mnw, d_w_kv, d_onw, d_w_out, d_fnw,
            nm_nw, nm_w_in, nm_b, nm_mnw, nm_w_kv, nm_onw, nm_w_out, nm_fnw,
            nv_nw, nv_w_in, nv_b, nv_mnw, nv_w_kv, nv_onw, nv_w_out, nv_fnw)
```

```python
import functools
from typing import NamedTuple

import jax
import jax.numpy as jnp
from jax import lax
from jax.experimental import pallas as pl
from jax.experimental.pallas import tpu as pltpu

F32 = jnp.float32
BF16 = jnp.bfloat16

N_DEV = 8
D_MODEL = 1024
HEAD_DIM = 64
LANES = 128
SB_W = 512
FX_W = 512
MEM_W = 256
MIX_W = 1280
FOX_HEADS = 8
IN_W = 4616
SHARD_W = IN_W // N_DEV
QKV_W = 3 * SB_W + 3 * FX_W + MEM_W
FL_PAD = 256
GF_W = MIX_W + FL_PAD
WR_W = QKV_W + GF_W
EPS = 1e-6
T = 256
QPS = 4
TM = 256
TQM = 512
Q_SCALE = 0.125
NEG = -1e30
UNDERFLOW = -110.0
NORM_SLACK = 1.01

ADAM_LR = 0.001
ADAM_B1 = 0.9
ADAM_B2 = 0.999
ADAM_EPS = 1e-08
ADAM_WD = 0.01
ADAM_STEP = 10


_NT = (((1,), (1,)), ((), ()))
_TN = (((0,), (0,)), ((), ()))

_ARB1 = pltpu.CompilerParams(dimension_semantics=("arbitrary",))
_ARB2 = pltpu.CompilerParams(dimension_semantics=("arbitrary", "arbitrary"))


def _dot(a, b):
    return jnp.dot(a, b, preferred_element_type=F32)


def _dot_nt(a, b):
    return lax.dot_general(a, b, _NT, preferred_element_type=F32)


def _dot_tn(a, b):
    return lax.dot_general(a, b, _TN, preferred_element_type=F32)


def _split2(x):
    hi = x.astype(BF16)
    lo = (x - hi.astype(F32)).astype(BF16)
    return hi, lo


def _stack2(u):
    return jnp.concatenate([u, u], axis=0)


def _cum2(x, u2):
    hi, lo = _split2(x)
    return _dot(jnp.concatenate([hi, lo], axis=1), u2)


def _tri3(tri, x, dot=None):
    dot = dot or _dot
    hi = x.astype(BF16)
    r1 = x - hi.astype(F32)
    mid = r1.astype(BF16)
    lo = (r1 - mid.astype(F32)).astype(BF16)
    return dot(tri, hi) + dot(tri, mid) + dot(tri, lo)


def _iota2(shape, dim):
    return lax.broadcasted_iota(jnp.int32, shape, dim)


def _head_block_diag():
    r = _iota2((LANES, LANES), 0) // HEAD_DIM
    c = _iota2((LANES, LANES), 1) // HEAD_DIM
    return _stack2(jnp.where(r == c, 1.0, 0.0).astype(BF16))


def _head_mean(x, bd):
    return _cum2(x, bd) * (1.0 / HEAD_DIM)


def _sigmoid(x):
    return 1.0 / (1.0 + jnp.exp(-x))


def _log_sigmoid(x):
    return jnp.minimum(x, 0.0) - jnp.log(1.0 + jnp.exp(-jnp.abs(x)))


def _running_top(r_ref):
    return jnp.max(jnp.maximum(r_ref[0], r_ref[1]))


def _fox_tiles_left(i, pair, nq, fb_ref, tile):
    def bound(j):
        b = []
        for h in range(2):
            head = 2 * pair + h
            b.append(2.0 * NORM_SLACK * fb_ref[2 * nq + i, head] * fb_ref[3 * nq, head]
                     + fb_ref[2 * i, head] - fb_ref[2 * j + 1, head])
        return jnp.maximum(b[0], b[1])

    def more(j):
        return jnp.logical_and(j >= 0, bound(jnp.maximum(j, 0)) > UNDERFLOW)

    def step(j):
        tile(j, False)
        return j - 1

    return lax.while_loop(more, step, i - 1)


def _pair_masks():
    lane = _iota2((1, LANES), 1)
    return lane < HEAD_DIM


def _split_pair(x, m0):
    zero = jnp.zeros_like(x)
    return jnp.where(m0, x, zero), jnp.where(m0, zero, x)


def _inproj_fwd(x, nw, w_r, name, rider=None):
    s = x.shape[0]

    def body(x_ref, nw_ref, w_ref, qkv_ref, gf_ref):
        xv = x_ref[...]
        r = lax.rsqrt(jnp.mean(xv * xv, axis=-1, keepdims=True) + EPS)
        h = (xv * r * nw_ref[...]).astype(BF16)
        for c in range(0, QKV_W, 256):
            qkv_ref[:, c:c + 256] = _dot_nt(h, w_ref[c:c + 256, :]).astype(BF16)
        for c in range(0, GF_W, 256):
            gf_ref[:, c:c + 256] = _dot_nt(h, w_ref[QKV_W + c:QKV_W + c + 256, :])

    (qkv, gf), lands = _ride(dict(
        body=body, name=name, grid=(s // TM,),
        in_specs=[pl.BlockSpec((TM, D_MODEL), lambda i: (i, 0)),
                  pl.BlockSpec((1, D_MODEL), lambda i: (0, 0)),
                  pl.BlockSpec((WR_W, D_MODEL), lambda i: (0, 0))],
        out_specs=[pl.BlockSpec((TM, QKV_W), lambda i: (i, 0)),
                   pl.BlockSpec((TM, GF_W), lambda i: (i, 0))],
        out_shape=[jax.ShapeDtypeStruct((s, QKV_W), BF16), jax.ShapeDtypeStruct((s, GF_W), F32)],
        scratch_shapes=[], compiler_params=_ARB1, operands=[x, nw, w_r]), rider)
    return qkv, gf, lands


def _fox_prep_fwd(gf, qkv, bpad, name):
    s = gf.shape[0]
    nq = s // T
    nrow = -(-(3 * nq + 1) // 8) * 8

    def body(fl_ref, q_ref, k_ref, b_ref, fq_ref, fr_ref, fb_ref):
        tri = jnp.where(_iota2((T, T), 0) >= _iota2((T, T), 1), 1.0, 0.0).astype(BF16)
        m0 = _pair_masks()
        lane = _iota2((1, LANES), 1)
        norms = [jnp.zeros((1, LANES), F32) for _ in range(nq + 1)]
        same_head = (_iota2((LANES, LANES), 0) // HEAD_DIM) == (_iota2((LANES, LANES), 1) // HEAD_DIM)
        bd = jnp.where(same_head, 1.0, 0.0).astype(BF16)
        for p in range(FOX_HEADS // 2):
            cols = slice(p * LANES, (p + 1) * LANES)
            q = (q_ref[:, cols] * jnp.asarray(Q_SCALE, BF16)).astype(F32)
            k = k_ref[:, cols].astype(F32)
            qn = _dot((q * q).astype(BF16), bd)
            kn = _dot((k * k).astype(BF16), bd)
            tops = [jnp.max(qn[j * T:(j + 1) * T], axis=0, keepdims=True) for j in range(nq)]
            tops.append(jnp.max(kn, axis=0, keepdims=True))
            tops = [jnp.sqrt(top) for top in tops]
            for h in range(2):
                at = h * HEAD_DIM
                norms = [jnp.where(lane == 2 * p + h, top[:, at:at + 1], row)
                         for top, row in zip(tops, norms)]
        for j in range(nq + 1):
            fb_ref[2 * nq + j:2 * nq + j + 1, :] = norms[j]
        fb_ref[3 * nq + 1:, :] = jnp.zeros((nrow - 3 * nq - 1, LANES), F32)
        carry = jnp.zeros((1, LANES), F32)
        for blk in range(s // T):
            rows = slice(blk * T, (blk + 1) * T)
            lf = _log_sigmoid(fl_ref[rows, :] + b_ref[...])
            c = _tri3(tri, lf) + carry
            carry = c[T - 1:T, :]
            for p in range(FOX_HEADS // 2):
                fq_ref[rows, p * LANES:(p + 1) * LANES] = jnp.where(
                    m0, c[:, 2 * p:2 * p + 1], c[:, 2 * p + 1:2 * p + 2])
            fr_ref[:, rows] = c.T[0:FOX_HEADS, :]
            fb_ref[2 * blk:2 * blk + 1, :] = c[0:1, :]
            fb_ref[2 * blk + 1:2 * blk + 2, :] = carry

    base = 3 * SB_W // FX_W
    return pl.pallas_call(
        body, name=name, grid=(1,),
        in_specs=[pl.BlockSpec((s, LANES), lambda i: (0, MIX_W // LANES)),
                  pl.BlockSpec((s, FX_W), lambda i: (0, base)),
                  pl.BlockSpec((s, FX_W), lambda i: (0, base + 1)),
                  pl.BlockSpec((1, LANES), lambda i: (0, 0))],
        out_specs=[pl.BlockSpec((s, FX_W), lambda i: (0, 0)),
                   pl.BlockSpec((FOX_HEADS, s), lambda i: (0, 0)),
                   pl.BlockSpec((nrow, LANES), lambda i: (0, 0))],
        out_shape=[jax.ShapeDtypeStruct((s, FX_W), F32), jax.ShapeDtypeStruct((FOX_HEADS, s), F32),
                   jax.ShapeDtypeStruct((nrow, LANES), F32)],
        compiler_params=_ARB1,
    )(gf, qkv, qkv, bpad)


def _sb_fwd(qkv, name, rider=None):
    s = qkv.shape[0]

    def body(q_ref, k_ref, v_ref, o_ref, acc_ref, r_ref, as_ref):
        m0 = _pair_masks()
        strict = _iota2((T, T), 0) > _iota2((T, T), 1)
        u2 = _stack2(jnp.where(strict, 1.0, 0.0).astype(BF16))
        hs = range(2)

        def query_tile(i, rows):
            qh = _split_pair(q_ref[rows, :] * jnp.asarray(Q_SCALE, BF16), m0)
            acc_ref[...] = jnp.zeros_like(acc_ref)
            r_ref[...] = jnp.zeros_like(r_ref)

            def flush(j):
                v = v_ref[pl.ds(pl.multiple_of(j * T, T), T), :]
                for h in hs:
                    acc_ref[h] += _dot(as_ref[h], v)

            def tile(j, diag):
                k = k_ref[pl.ds(pl.multiple_of(j * T, T), T), :]
                z = [_dot_nt(qh[h], k) for h in hs]
                if not diag:
                    flush(j + 1)
                la = [jnp.minimum(z[h], 0.0) - jnp.log(1.0 + jnp.exp(-jnp.abs(z[h]))) for h in hs]
                lf = [la[h] - z[h] for h in hs]
                if diag:
                    lf = [jnp.where(strict, lf[h], 0.0) for h in hs]
                cin = [_cum2(lf[h], u2) for h in hs]
                a = [jnp.exp(la[h] + cin[h] + r_ref[h]) for h in hs]
                if diag:
                    a = [jnp.where(strict, a[h], 0.0) for h in hs]
                for h in hs:
                    r_ref[h] += cin[h][:, 0:1] + lf[h][:, 0:1]
                    as_ref[h] = a[h].astype(BF16)

            tile(i, True)

            def more(state):
                j, top = state
                return jnp.logical_and(j >= 0, top > UNDERFLOW)

            def step(state):
                j, _ = state
                tile(j, False)
                return j - 1, _running_top(r_ref)

            j_left, _ = lax.while_loop(more, step, (i - 1, _running_top(r_ref)))
            flush(j_left + 1)
            o_ref[rows, :] = jnp.where(m0, acc_ref[0], acc_ref[1])

        for n in range(QPS):
            query_tile(QPS * pl.program_id(1) + n, slice(n * T, (n + 1) * T))

    nb = SB_W // LANES
    (ysb,), lands = _ride(dict(
        body=body, name=name, grid=(nb, s // (QPS * T)),
        in_specs=[pl.BlockSpec((QPS * T, LANES), lambda p, i: (i, p)),
                  pl.BlockSpec((s, LANES), lambda p, i: (0, nb + p)),
                  pl.BlockSpec((s, LANES), lambda p, i: (0, 2 * nb + p))],
        out_specs=[pl.BlockSpec((QPS * T, LANES), lambda p, i: (i, p))],
        out_shape=[jax.ShapeDtypeStruct((s, SB_W), F32)],
        scratch_shapes=[pltpu.VMEM((2, T, LANES), F32), pltpu.VMEM((2, T, 1), F32),
                        pltpu.VMEM((2, T, T), BF16)],
        compiler_params=_ARB2, operands=[qkv, qkv, qkv]), rider)
    return ysb, lands


def _fox_fwd(qkv, fqb, frow, fbounds, name, rider=None):
    s = qkv.shape[0]

    def body(q_ref, k_ref, v_ref, fq_ref, fr_ref, fb_ref, o_ref, lse_ref, acc_ref, m_ref, ps_ref):
        pair = pl.program_id(0)
        m0 = _pair_masks()
        causal = _iota2((T, T), 0) >= _iota2((T, T), 1)
        ones = jnp.ones((T, LANES), BF16)
        hs = range(2)

        def query_tile(i, rows):
            qh = _split_pair(q_ref[rows, :] * jnp.asarray(Q_SCALE, BF16), m0)
            fq = fq_ref[rows, :]
            fqh = (fq[:, 0:1], fq[:, HEAD_DIM:HEAD_DIM + 1])
            acc_ref[...] = jnp.zeros_like(acc_ref)
            m_ref[...] = jnp.full_like(m_ref, NEG)

            def flush(j):
                v = v_ref[pl.ds(pl.multiple_of(j * T, T), T), :]
                va2 = _stack2(jnp.concatenate([v, ones], axis=1))
                for h in hs:
                    acc_ref[h] += _dot(ps_ref[h], va2)

            def tile(j, diag):
                off = pl.multiple_of(j * T, T)
                k = k_ref[pl.ds(off, T), :]
                sc = [_dot_nt(qh[h], k) + fqh[h] - fr_ref[h:h + 1, pl.ds(off, T)] for h in hs]
                if not diag:
                    flush(j + 1)
                if diag:
                    sc = [jnp.where(causal, sc[h], NEG) for h in hs]
                m_new = [jnp.maximum(m_ref[h], jnp.max(sc[h], axis=1, keepdims=True)) for h in hs]
                p = [jnp.exp(sc[h] - m_new[h]) for h in hs]
                for h in hs:
                    acc_ref[h] = acc_ref[h] * jnp.exp(m_ref[h] - m_new[h])
                    m_ref[h] = m_new[h]
                    p_hi, p_lo = _split2(p[h])
                    ps_ref[h] = jnp.concatenate([p_hi, p_lo], axis=1)

            tile(i, True)
            j_left = _fox_tiles_left(i, pair, s // T, fb_ref, tile)
            flush(j_left + 1)
            acc = (acc_ref[0], acc_ref[1])
            o_ref[rows, :] = jnp.where(m0, acc[0][:, :LANES] / acc[0][:, LANES:],
                                       acc[1][:, :LANES] / acc[1][:, LANES:])
            lse_ref[rows, :] = jnp.where(m0, m_ref[0] + jnp.log(acc[0][:, LANES:]),
                                         m_ref[1] + jnp.log(acc[1][:, LANES:]))

        for n in range(QPS):
            query_tile(QPS * pl.program_id(1) + n, slice(n * T, (n + 1) * T))

    nb = FX_W // LANES
    base = 3 * SB_W // LANES
    (yfx, lse), lands = _ride(dict(
        body=body, name=name, grid=(nb, s // (QPS * T)),
        in_specs=[pl.BlockSpec((QPS * T, LANES), lambda p, i: (i, base + p)),
                  pl.BlockSpec((s, LANES), lambda p, i: (0, base + nb + p)),
                  pl.BlockSpec((s, LANES), lambda p, i: (0, base + 2 * nb + p)),
                  pl.BlockSpec((QPS * T, LANES), lambda p, i: (i, p)),
                  pl.BlockSpec((None, 2, s), lambda p, i: (p, 0, 0)),
                  pl.BlockSpec(memory_space=pltpu.SMEM)],
        out_specs=[pl.BlockSpec((QPS * T, LANES), lambda p, i: (i, p)),
                   pl.BlockSpec((QPS * T, LANES), lambda p, i: (i, p))],
        out_shape=[jax.ShapeDtypeStruct((s, FX_W), F32), jax.ShapeDtypeStruct((s, FX_W), F32)],
        scratch_shapes=[pltpu.VMEM((2, T, 2 * LANES), F32), pltpu.VMEM((2, T, 1), F32),
                        pltpu.VMEM((2, T, 2 * T), BF16)],
        compiler_params=_ARB2, operands=[qkv, qkv, qkv, fqb, frow, fbounds]), rider)
    return yfx, lse, lands


def _memkv_fwd(mem, mnw, wkv, name):
    n = mem.shape[0]

    def body(mem_ref, mnw_ref, w_ref, kv_ref):
        mv = mem_ref[...]
        r = lax.rsqrt(jnp.mean(mv * mv, axis=-1, keepdims=True) + EPS)
        hm = (mv * r * mnw_ref[...]).astype(BF16)
        kv_ref[...] = _dot(hm, w_ref[...]).astype(BF16)

    return pl.pallas_call(
        body, name=name, grid=(1,),
        in_specs=[pl.BlockSpec((n, D_MODEL), lambda i: (0, 0)),
                  pl.BlockSpec((1, D_MODEL), lambda i: (0, 0)),
                  pl.BlockSpec((D_MODEL, 2 * MEM_W), lambda i: (0, 0))],
        out_specs=pl.BlockSpec((n, 2 * MEM_W), lambda i: (0, 0)),
        out_shape=jax.ShapeDtypeStruct((n, 2 * MEM_W), BF16),
        compiler_params=_ARB1,
    )(mem, mnw, wkv)


def _mem_fwd(qkv, kv, name):
    s = qkv.shape[0]
    n = kv.shape[0]

    def body(q_ref, k_ref, v_ref, o_ref, lse_ref):
        m0 = _pair_masks()
        qh = _split_pair(q_ref[...] * jnp.asarray(Q_SCALE, BF16), m0)
        k = k_ref[...]
        v = v_ref[...]
        outs, lses = [], []
        for h in range(2):
            sc = _dot_nt(qh[h], k)
            mx = jnp.max(sc, axis=1, keepdims=True)
            p = jnp.exp(sc - mx)
            l = jnp.sum(p, axis=1, keepdims=True)
            outs.append(_dot(p.astype(BF16), v) / l)
            lses.append(mx + jnp.log(l))
        o_ref[...] = jnp.where(m0, outs[0], outs[1])
        lse_ref[...] = jnp.where(m0, lses[0], lses[1])

    nb = MEM_W // LANES
    base = (3 * SB_W + 3 * FX_W) // LANES
    return pl.pallas_call(
        body, name=name, grid=(nb, s // TQM),
        in_specs=[pl.BlockSpec((TQM, LANES), lambda p, i: (i, base + p)),
                  pl.BlockSpec((n, LANES), lambda p, i: (0, p)),
                  pl.BlockSpec((n, LANES), lambda p, i: (0, nb + p))],
        out_specs=[pl.BlockSpec((TQM, LANES), lambda p, i: (i, p)),
                   pl.BlockSpec((TQM, LANES), lambda p, i: (i, p))],
        out_shape=[jax.ShapeDtypeStruct((s, MEM_W), F32), jax.ShapeDtypeStruct((s, MEM_W), F32)],
        compiler_params=_ARB2,
    )(qkv, kv, kv)


def _mix_chunk(c, ysb_ref, yfx_ref, ym_ref):
    if c < SB_W // LANES:
        return ysb_ref[:, c * LANES:(c + 1) * LANES]
    c -= SB_W // LANES
    if c < FX_W // LANES:
        return yfx_ref[:, c * LANES:(c + 1) * LANES]
    c -= FX_W // LANES
    return ym_ref[:, c * LANES:(c + 1) * LANES]


def _outproj_fwd(ysb, yfx, ym, gf, onw, wout, x, name):
    s = x.shape[0]

    def body(ysb_ref, yfx_ref, ym_ref, g_ref, onw_ref, w_ref, x_ref, o_ref, yg_ref):
        bd = _head_block_diag()
        for c in range(MIX_W // LANES):
            sl = slice(c * LANES, (c + 1) * LANES)
            u = _mix_chunk(c, ysb_ref, yfx_ref, ym_ref)
            r = lax.rsqrt(_head_mean(u * u, bd) + EPS)
            g = g_ref[:, sl]
            yg_ref[:, sl] = (u * r * onw_ref[:, sl] * (g * _sigmoid(g))).astype(BF16)
        o_ref[...] = x_ref[...] + _dot(yg_ref[...], w_ref[...])

    return pl.pallas_call(
        body, name=name, grid=(s // TM,),
        in_specs=[pl.BlockSpec((TM, SB_W), lambda i: (i, 0)),
                  pl.BlockSpec((TM, FX_W), lambda i: (i, 0)),
                  pl.BlockSpec((TM, MEM_W), lambda i: (i, 0)),
                  pl.BlockSpec((TM, MIX_W), lambda i: (i, 0)),
                  pl.BlockSpec((1, MIX_W), lambda i: (0, 0)),
                  pl.BlockSpec((MIX_W, D_MODEL), lambda i: (0, 0)),
                  pl.BlockSpec((TM, D_MODEL), lambda i: (i, 0))],
        out_specs=pl.BlockSpec((TM, D_MODEL), lambda i: (i, 0)),
        out_shape=jax.ShapeDtypeStruct((s, D_MODEL), F32),
        scratch_shapes=[pltpu.VMEM((TM, MIX_W), BF16)],
        compiler_params=_ARB1,
    )(ysb, yfx, ym, gf, onw, wout, x)


def _final_fwd_bwd(x, fnw, target, name):
    s = x.shape[0]

    def body(x_ref, w_ref, t_ref, dx_ref, loss_ref, dw_ref):
        @pl.when(pl.program_id(0) == 0)
        def _():
            loss_ref[...] = jnp.zeros_like(loss_ref)
            dw_ref[...] = jnp.zeros_like(dw_ref)

        xv = x_ref[...]
        w = w_ref[...]
        r = lax.rsqrt(jnp.mean(xv * xv, axis=-1, keepdims=True) + EPS)
        xh = xv * r
        err = xh * w - t_ref[...]
        part = jnp.sum(jnp.sum(err * err, axis=1, keepdims=True), axis=0, keepdims=True)
        loss_ref[...] += part * (0.5 / D_MODEL)
        dy = err * (1.0 / D_MODEL)
        dw_ref[...] += jnp.sum(dy * xh, axis=0, keepdims=True)
        dxh = dy * w
        dx_ref[...] = r * (dxh - xh * jnp.mean(dxh * xh, axis=-1, keepdims=True))

    return pl.pallas_call(
        body, name=name, grid=(s // TM,),
        in_specs=[pl.BlockSpec((TM, D_MODEL), lambda i: (i, 0)),
                  pl.BlockSpec((1, D_MODEL), lambda i: (0, 0)),
                  pl.BlockSpec((TM, D_MODEL), lambda i: (i, 0))],
        out_specs=[pl.BlockSpec((TM, D_MODEL), lambda i: (i, 0)),
                   pl.BlockSpec((1, LANES), lambda i: (0, 0)),
                   pl.BlockSpec((1, D_MODEL), lambda i: (0, 0))],
        out_shape=[jax.ShapeDtypeStruct((s, D_MODEL), F32), jax.ShapeDtypeStruct((1, LANES), F32),
                   jax.ShapeDtypeStruct((1, D_MODEL), F32)],
        compiler_params=_ARB1,
    )(x, fnw, target)


def _outproj_bwd(dxo, wout, ysb, yfx, ym, gf, onw, name, rider=None):
    s = dxo.shape[0]

    def body(dx_ref, w_ref, ysb_ref, yfx_ref, ym_ref, g_ref, onw_ref,
             dysb_ref, dyfx_ref, dym_ref, dg_ref, dw_ref, donw_ref, yg_ref):
        @pl.when(pl.program_id(0) == 0)
        def _():
            dw_ref[...] = jnp.zeros_like(dw_ref)
            donw_ref[...] = jnp.zeros_like(donw_ref)

        dxb = dx_ref[...].astype(BF16)
        dyg = _dot_nt(dxb, w_ref[...])
        bd = _head_block_diag()
        for c in range(MIX_W // LANES):
            sl = slice(c * LANES, (c + 1) * LANES)
            u = _mix_chunk(c, ysb_ref, yfx_ref, ym_ref)
            r = lax.rsqrt(_head_mean(u * u, bd) + EPS)
            yn = u * r
            g = g_ref[:, sl]
            sg = _sigmoid(g)
            sil = g * sg
            onw = onw_ref[:, sl]
            e = dyg[:, sl]
            yg_ref[:, sl] = (yn * onw * sil).astype(BF16)
            donw_ref[:, sl] += jnp.sum(e * yn * sil, axis=0, keepdims=True)
            dg_ref[:, sl] = (e * yn * onw * (sg * (1.0 + g * (1.0 - sg)))).astype(BF16)
            dyn = e * onw * sil
            du = (r * (dyn - yn * _head_mean(dyn * yn, bd))).astype(BF16)
            if c < 4:
                dysb_ref[:, c * LANES:(c + 1) * LANES] = du
            elif c < 8:
                dyfx_ref[:, (c - 4) * LANES:(c - 3) * LANES] = du
            else:
                dym_ref[:, (c - 8) * LANES:(c - 7) * LANES] = du
        dw_ref[...] += _dot_tn(yg_ref[...], dxb)

    outs, lands = _ride(dict(
        body=body, name=name, grid=(s // TM,),
        in_specs=[pl.BlockSpec((TM, D_MODEL), lambda i: (i, 0)),
                  pl.BlockSpec((MIX_W, D_MODEL), lambda i: (0, 0)),
                  pl.BlockSpec((TM, SB_W), lambda i: (i, 0)),
                  pl.BlockSpec((TM, FX_W), lambda i: (i, 0)),
                  pl.BlockSpec((TM, MEM_W), lambda i: (i, 0)),
                  pl.BlockSpec((TM, MIX_W), lambda i: (i, 0)),
                  pl.BlockSpec((1, MIX_W), lambda i: (0, 0))],
        out_specs=[pl.BlockSpec((TM, SB_W), lambda i: (i, 0)),
                   pl.BlockSpec((TM, FX_W), lambda i: (i, 0)),
                   pl.BlockSpec((TM, MEM_W), lambda i: (i, 0)),
                   pl.BlockSpec((TM, MIX_W), lambda i: (i, 0)),
                   pl.BlockSpec((MIX_W, D_MODEL), lambda i: (0, 0)),
                   pl.BlockSpec((1, MIX_W), lambda i: (0, 0))],
        out_shape=[jax.ShapeDtypeStruct((s, SB_W), BF16), jax.ShapeDtypeStruct((s, FX_W), BF16),
                   jax.ShapeDtypeStruct((s, MEM_W), BF16), jax.ShapeDtypeStruct((s, MIX_W), BF16),
                   jax.ShapeDtypeStruct((MIX_W, D_MODEL), F32), jax.ShapeDtypeStruct((1, MIX_W), F32)],
        scratch_shapes=[pltpu.VMEM((TM, MIX_W), BF16)],
        compiler_params=_ARB1, operands=[dxo, wout, ysb, yfx, ym, gf, onw]), rider)
    return (*outs, lands)


def _row_dots(do, o, m0):
    prod = do.astype(F32) * o
    zero = jnp.zeros_like(prod)
    return (jnp.sum(jnp.where(m0, prod, zero), axis=1, keepdims=True),
            jnp.sum(jnp.where(m0, zero, prod), axis=1, keepdims=True))


def _sb_bwd(qkv, o, do, name, rider=None):
    s = qkv.shape[0]
    nq = s // T

    def body(q_ref, k_ref, v_ref, o_ref, do_ref, dq_ref, dk_ref, dv_ref,
             dqa_ref, dka_ref, dva_ref, rl_ref, rg_ref, dzs_ref, abs_ref):
        step_id = pl.program_id(1)

        @pl.when(step_id == 0)
        def _():
            dka_ref[...] = jnp.zeros_like(dka_ref)
            dva_ref[...] = jnp.zeros_like(dva_ref)

        m0 = _pair_masks()
        strict = _iota2((T, T), 0) > _iota2((T, T), 1)
        u2 = _stack2(jnp.where(strict, 1.0, 0.0).astype(BF16))
        hs = range(2)

        def query_tile(i, rows):
            qh = _split_pair(q_ref[rows, :] * jnp.asarray(Q_SCALE, BF16), m0)
            doh = _split_pair(do_ref[rows, :], m0)
            dsum = _row_dots(do_ref[rows, :], o_ref[rows, :], m0)
            dqa_ref[...] = jnp.zeros_like(dqa_ref)
            rl_ref[...] = jnp.zeros_like(rl_ref)
            rg_ref[...] = jnp.zeros_like(rg_ref)

            def flush(j):
                off = pl.multiple_of(j * T, T)
                k = k_ref[pl.ds(off, T), :]
                for h in hs:
                    dqa_ref[h] += _dot(dzs_ref[h], k)
                dka_ref[pl.ds(off, T), :] += (_dot_tn(dzs_ref[0], qh[0])
                                              + _dot_tn(dzs_ref[1], qh[1]))
                dva_ref[pl.ds(off, T), :] += (_dot_tn(abs_ref[0], doh[0])
                                              + _dot_tn(abs_ref[1], doh[1]))

            def tile(j, diag):
                off = pl.multiple_of(j * T, T)
                k = k_ref[pl.ds(off, T), :]
                v = v_ref[pl.ds(off, T), :]
                z = [_dot_nt(qh[h], k) for h in hs]
                da = [_dot_nt(doh[h], v) for h in hs]
                if not diag:
                    flush(j + 1)
                la = [jnp.minimum(z[h], 0.0) - jnp.log(1.0 + jnp.exp(-jnp.abs(z[h]))) for h in hs]
                lf = [la[h] - z[h] for h in hs]
                if diag:
                    lf = [jnp.where(strict, lf[h], 0.0) for h in hs]
                cin = [_cum2(lf[h], u2) for h in hs]
                a = [jnp.exp(la[h] + cin[h] + rl_ref[h]) for h in hs]
                if diag:
                    a = [jnp.where(strict, a[h], 0.0) for h in hs]
                ab = [a[h].astype(BF16) for h in hs]
                g = [ab[h].astype(F32) * da[h] for h in hs]
                gin = [_cum2(g[h], u2) for h in hs]
                dz = [g[h] - jnp.exp(la[h]) * ((dsum[h] - rg_ref[h]) - gin[h]) for h in hs]
                if diag:
                    dz = [jnp.where(strict, dz[h], 0.0) for h in hs]
                for h in hs:
                    rl_ref[h] += cin[h][:, 0:1] + lf[h][:, 0:1]
                    rg_ref[h] += gin[h][:, 0:1] + g[h][:, 0:1]
                    dzs_ref[h] = dz[h].astype(BF16)
                    abs_ref[h] = ab[h]

            tile(i, True)

            def more(state):
                j, top = state
                return jnp.logical_and(j >= 0, top > UNDERFLOW)

            def step(state):
                j, _ = state
                tile(j, False)
                return j - 1, _running_top(rl_ref)

            j_left, _ = lax.while_loop(more, step, (i - 1, _running_top(rl_ref)))
            flush(j_left + 1)
            dq_ref[rows, :] = (jnp.where(m0, dqa_ref[0], dqa_ref[1]) * Q_SCALE).astype(BF16)

        for n in range(QPS):
            query_tile(QPS * step_id + n, slice(n * T, (n + 1) * T))

        @pl.when(step_id == nq // QPS - 1)
        def _():
            dk_ref[...] = dka_ref[...].astype(BF16)
            dv_ref[...] = dva_ref[...].astype(BF16)

    nb = SB_W // LANES
    (dq, dk, dv), lands = _ride(dict(
        body=body, name=name, grid=(nb, nq // QPS),
        in_specs=[pl.BlockSpec((QPS * T, LANES), lambda p, i: (i, p)),
                  pl.BlockSpec((s, LANES), lambda p, i: (0, nb + p)),
                  pl.BlockSpec((s, LANES), lambda p, i: (0, 2 * nb + p)),
                  pl.BlockSpec((QPS * T, LANES), lambda p, i: (i, p)),
                  pl.BlockSpec((QPS * T, LANES), lambda p, i: (i, p))],
        out_specs=[pl.BlockSpec((QPS * T, LANES), lambda p, i: (i, p)),
                   pl.BlockSpec((s, LANES), lambda p, i: (0, p)),
                   pl.BlockSpec((s, LANES), lambda p, i: (0, p))],
        out_shape=[jax.ShapeDtypeStruct((s, SB_W), BF16)] * 3,
        scratch_shapes=[pltpu.VMEM((2, T, LANES), F32), pltpu.VMEM((s, LANES), F32),
                        pltpu.VMEM((s, LANES), F32), pltpu.VMEM((2, T, 1), F32),
                        pltpu.VMEM((2, T, 1), F32), pltpu.VMEM((2, T, T), BF16),
                        pltpu.VMEM((2, T, T), BF16)],
        compiler_params=_ARB2, operands=[qkv, qkv, qkv, o, do]), rider)
    return dq, dk, dv, lands


def _fox_bwd(qkv, fqb, frow, fbounds, o, lse, do, name, rider=None):
    s = qkv.shape[0]
    nq = s // T

    def body(q_ref, k_ref, v_ref, fq_ref, fr_ref, fb_ref, o_ref, lse_ref, do_ref,
             dq_ref, dk_ref, dv_ref, df_ref, dqa_ref, dka_ref, dva_ref, dfa_ref, dls_ref, pbs_ref):
        step_id = pl.program_id(1)

        @pl.when(step_id == 0)
        def _():
            dka_ref[...] = jnp.zeros_like(dka_ref)
            dva_ref[...] = jnp.zeros_like(dva_ref)
            dfa_ref[...] = jnp.zeros_like(dfa_ref)

        m0 = _pair_masks()
        causal = _iota2((T, T), 0) >= _iota2((T, T), 1)
        hs = range(2)

        def query_tile(i, rows):
            qh = _split_pair(q_ref[rows, :] * jnp.asarray(Q_SCALE, BF16), m0)
            doh = _split_pair(do_ref[rows, :], m0)
            dsum = _row_dots(do_ref[rows, :], o_ref[rows, :], m0)
            fq = fq_ref[rows, :]
            fqh = (fq[:, 0:1], fq[:, HEAD_DIM:HEAD_DIM + 1])
            lse = lse_ref[rows, :]
            lseh = (lse[:, 0:1], lse[:, HEAD_DIM:HEAD_DIM + 1])
            dqa_ref[...] = jnp.zeros_like(dqa_ref)

            def flush(j):
                off = pl.multiple_of(j * T, T)
                k = k_ref[pl.ds(off, T), :]
                for h in hs:
                    dqa_ref[h] += _dot(dls_ref[h], k)
                dka_ref[pl.ds(off, T), :] += (_dot_tn(dls_ref[0], qh[0])
                                              + _dot_tn(dls_ref[1], qh[1]))
                dva_ref[pl.ds(off, T), :] += (_dot_tn(pbs_ref[0], doh[0])
                                              + _dot_tn(pbs_ref[1], doh[1]))

            def tile(j, diag):
                off = pl.multiple_of(j * T, T)
                k = k_ref[pl.ds(off, T), :]
                v = v_ref[pl.ds(off, T), :]
                sc = [_dot_nt(qh[h], k) + fqh[h] - fr_ref[h:h + 1, pl.ds(off, T)] for h in hs]
                dp = [_dot_nt(doh[h], v) for h in hs]
                if not diag:
                    flush(j + 1)
                p = [jnp.exp(sc[h] - lseh[h]) for h in hs]
                if diag:
                    p = [jnp.where(causal, p[h], 0.0) for h in hs]
                dl = [p[h] * (dp[h] - dsum[h]) for h in hs]
                for h in hs:
                    dls_ref[h] = dl[h].astype(BF16)
                    pbs_ref[h] = p[h].astype(BF16)
                    dfa_ref[h:h + 1, pl.ds(off, T)] -= jnp.sum(dl[h], axis=0, keepdims=True)

            tile(i, True)
            j_left = _fox_tiles_left(i, pl.program_id(0), nq, fb_ref, tile)
            flush(j_left + 1)
            dq_ref[rows, :] = (jnp.where(m0, dqa_ref[0], dqa_ref[1]) * Q_SCALE).astype(BF16)

        for n in range(QPS):
            query_tile(QPS * step_id + n, slice(n * T, (n + 1) * T))

        @pl.when(step_id == nq // QPS - 1)
        def _():
            dk_ref[...] = dka_ref[...].astype(BF16)
            dv_ref[...] = dva_ref[...].astype(BF16)
            df_ref[...] = dfa_ref[...]

    nb = FX_W // LANES
    base = 3 * SB_W // LANES
    (dq, dk, dv, df), lands = _ride(dict(
        body=body, name=name, grid=(nb, nq // QPS),
        in_specs=[pl.BlockSpec((QPS * T, LANES), lambda p, i: (i, base + p)),
                  pl.BlockSpec((s, LANES), lambda p, i: (0, base + nb + p)),
                  pl.BlockSpec((s, LANES), lambda p, i: (0, base + 2 * nb + p)),
                  pl.BlockSpec((QPS * T, LANES), lambda p, i: (i, p)),
                  pl.BlockSpec((None, 2, s), lambda p, i: (p, 0, 0)),
                  pl.BlockSpec(memory_space=pltpu.SMEM),
                  pl.BlockSpec((QPS * T, LANES), lambda p, i: (i, p)),
                  pl.BlockSpec((QPS * T, LANES), lambda p, i: (i, p)),
                  pl.BlockSpec((QPS * T, LANES), lambda p, i: (i, p))],
        out_specs=[pl.BlockSpec((QPS * T, LANES), lambda p, i: (i, p)),
                   pl.BlockSpec((s, LANES), lambda p, i: (0, p)),
                   pl.BlockSpec((s, LANES), lambda p, i: (0, p)),
                   pl.BlockSpec((None, 2, s), lambda p, i: (p, 0, 0))],
        out_shape=[jax.ShapeDtypeStruct((s, FX_W), BF16)] * 3
        + [jax.ShapeDtypeStruct((nb, 2, s), F32)],
        scratch_shapes=[pltpu.VMEM((2, T, LANES), F32), pltpu.VMEM((s, LANES), F32),
                        pltpu.VMEM((s, LANES), F32), pltpu.VMEM((2, s), F32),
                        pltpu.VMEM((2, T, T), BF16), pltpu.VMEM((2, T, T), BF16)],
        compiler_params=_ARB2, operands=[qkv, qkv, qkv, fqb, frow, fbounds, o, lse, do]), rider)
    return dq, dk, dv, df, lands


def _fox_prep_bwd(dfrow, gf, bpad, name):
    s = gf.shape[0]

    def body(df_ref, fl_ref, b_ref, dfl_ref, db_ref):
        tri = jnp.where(_iota2((T, T), 0) <= _iota2((T, T), 1), 1.0, 0.0).astype(BF16)
        carry = jnp.zeros((1, LANES), F32)
        db = jnp.zeros((1, LANES), F32)
        fill = jnp.zeros((LANES - FOX_HEADS, T), F32)
        for blk in reversed(range(s // T)):
            rows = slice(blk * T, (blk + 1) * T)
            c = _tri3(tri, jnp.concatenate([df_ref[:, rows], fill], axis=0), _dot_nt) + carry
            carry = c[0:1, :]
            dfl = c / (1.0 + jnp.exp(fl_ref[rows, :] + b_ref[...]))
            dfl_ref[rows, :] = dfl.astype(BF16)
            db = db + jnp.sum(dfl, axis=0, keepdims=True)
        db_ref[...] = db

    return pl.pallas_call(
        body, name=name, grid=(1,),
        in_specs=[pl.BlockSpec((FOX_HEADS, s), lambda i: (0, 0)),
                  pl.BlockSpec((s, LANES), lambda i: (0, MIX_W // LANES)),
                  pl.BlockSpec((1, LANES), lambda i: (0, 0))],
        out_specs=[pl.BlockSpec((s, LANES), lambda i: (0, 0)),
                   pl.BlockSpec((1, LANES), lambda i: (0, 0))],
        out_shape=[jax.ShapeDtypeStruct((s, LANES), BF16), jax.ShapeDtypeStruct((1, LANES), F32)],
        compiler_params=_ARB1,
    )(dfrow, gf, bpad)


def _mem_bwd(qkv, kv, o, lse, do, name):
    s = qkv.shape[0]
    n = kv.shape[0]

    def body(q_ref, k_ref, v_ref, o_ref, lse_ref, do_ref, dq_ref, dk_ref, dv_ref):
        @pl.when(pl.program_id(1) == 0)
        def _():
            dk_ref[...] = jnp.zeros_like(dk_ref)
            dv_ref[...] = jnp.zeros_like(dv_ref)

        m0 = _pair_masks()
        qh = _split_pair(q_ref[...] * jnp.asarray(Q_SCALE, BF16), m0)
        doh = _split_pair(do_ref[...], m0)
        dsum = _row_dots(do_ref[...], o_ref[...], m0)
        lse = lse_ref[...]
        lseh = (lse[:, 0:1], lse[:, HEAD_DIM:HEAD_DIM + 1])
        k = k_ref[...]
        v = v_ref[...]
        dqs = []
        for h in range(2):
            p = jnp.exp(_dot_nt(qh[h], k) - lseh[h])
            dl = p * (_dot_nt(doh[h], v) - dsum[h])
            dlb = dl.astype(BF16)
            dqs.append(_dot(dlb, k))
            dk_ref[...] += _dot_tn(dlb, qh[h])
            dv_ref[...] += _dot_tn(p.astype(BF16), doh[h])
        dq_ref[...] = (jnp.where(m0, dqs[0], dqs[1]) * Q_SCALE).astype(BF16)

    nb = MEM_W // LANES
    base = (3 * SB_W + 3 * FX_W) // LANES
    return pl.pallas_call(
        body, name=name, grid=(nb, s // TQM),
        in_specs=[pl.BlockSpec((TQM, LANES), lambda p, i: (i, base + p)),
                  pl.BlockSpec((n, LANES), lambda p, i: (0, p)),
                  pl.BlockSpec((n, LANES), lambda p, i: (0, nb + p)),
                  pl.BlockSpec((TQM, LANES), lambda p, i: (i, p)),
                  pl.BlockSpec((TQM, LANES), lambda p, i: (i, p)),
                  pl.BlockSpec((TQM, LANES), lambda p, i: (i, p))],
        out_specs=[pl.BlockSpec((TQM, LANES), lambda p, i: (i, p)),
                   pl.BlockSpec((n, LANES), lambda p, i: (0, p)),
                   pl.BlockSpec((n, LANES), lambda p, i: (0, p))],
        out_shape=[jax.ShapeDtypeStruct((s, MEM_W), BF16), jax.ShapeDtypeStruct((n, MEM_W), F32),
                   jax.ShapeDtypeStruct((n, MEM_W), F32)],
        compiler_params=_ARB2,
    )(qkv, kv, kv, o, lse, do)


def _memkv_bwd(mem, mnw, wkv, dk, dv, name):
    n = mem.shape[0]

    def body(mem_ref, mnw_ref, w_ref, dk_ref, dv_ref, dw_ref, dmnw_ref):
        mv = mem_ref[...]
        r = lax.rsqrt(jnp.mean(mv * mv, axis=-1, keepdims=True) + EPS)
        mh = mv * r
        hm = (mh * mnw_ref[...]).astype(BF16)
        dkv = jnp.concatenate([dk_ref[...], dv_ref[...]], axis=1).astype(BF16)
        dw_ref[...] = _dot_tn(hm, dkv)
        dhm = _dot_nt(dkv, w_ref[...])
        dmnw_ref[...] = jnp.sum(dhm * mh, axis=0, keepdims=True)

    return pl.pallas_call(
        body, name=name, grid=(1,),
        in_specs=[pl.BlockSpec((n, D_MODEL), lambda i: (0, 0)),
                  pl.BlockSpec((1, D_MODEL), lambda i: (0, 0)),
                  pl.BlockSpec((D_MODEL, 2 * MEM_W), lambda i: (0, 0)),
                  pl.BlockSpec((n, MEM_W), lambda i: (0, 0)),
                  pl.BlockSpec((n, MEM_W), lambda i: (0, 0))],
        out_specs=[pl.BlockSpec((D_MODEL, 2 * MEM_W), lambda i: (0, 0)),
                   pl.BlockSpec((1, D_MODEL), lambda i: (0, 0))],
        out_shape=[jax.ShapeDtypeStruct((D_MODEL, 2 * MEM_W), F32),
                   jax.ShapeDtypeStruct((1, D_MODEL), F32)],
        compiler_params=_ARB1,
    )(mem, mnw, wkv, dk, dv)


def _inproj_bwd_dx(pieces, w_r, x, nw, dxo, name, rider=None):
    s = x.shape[0]
    n = len(pieces)
    widths = [p.shape[1] for p in pieces]

    def body(*refs):
        piece_refs = refs[:n]
        w_ref, x_ref, nw_ref, dxo_ref, dx_ref, h_ref, dnw_ref, dp_ref = refs[n:]

        @pl.when(pl.program_id(0) == 0)
        def _():
            dnw_ref[...] = jnp.zeros_like(dnw_ref)

        col = 0
        for r, wd in zip(piece_refs, widths):
            dp_ref[:, col:col + wd] = r[...]
            col += wd
        dp_ref[:, col:] = jnp.zeros((TM, WR_W - col), BF16)
        dh = _dot(dp_ref[...], w_ref[...])
        xv = x_ref[...]
        nw = nw_ref[...]
        r = lax.rsqrt(jnp.mean(xv * xv, axis=-1, keepdims=True) + EPS)
        xh = xv * r
        h_ref[...] = (xh * nw).astype(BF16)
        dnw_ref[...] += jnp.sum(dh * xh, axis=0, keepdims=True)
        dxh = dh * nw
        dx_ref[...] = r * (dxh - xh * jnp.mean(dxh * xh, axis=-1, keepdims=True)) + dxo_ref[...]

    (dx, h, dnw, dproj), lands = _ride(dict(
        body=body, name=name, grid=(s // TM,),
        in_specs=[pl.BlockSpec((TM, wd), lambda i: (i, 0)) for wd in widths]
        + [pl.BlockSpec((WR_W, D_MODEL), lambda i: (0, 0)),
           pl.BlockSpec((TM, D_MODEL), lambda i: (i, 0)),
           pl.BlockSpec((1, D_MODEL), lambda i: (0, 0)),
           pl.BlockSpec((TM, D_MODEL), lambda i: (i, 0))],
        out_specs=[pl.BlockSpec((None, TM, D_MODEL), lambda i: (0, i, 0)),
                   pl.BlockSpec((TM, D_MODEL), lambda i: (i, 0)),
                   pl.BlockSpec((1, D_MODEL), lambda i: (0, 0)),
                   pl.BlockSpec((TM, WR_W), lambda i: (i, 0))],
        out_shape=[jax.ShapeDtypeStruct((1, s, D_MODEL), F32), jax.ShapeDtypeStruct((s, D_MODEL), BF16),
                   jax.ShapeDtypeStruct((1, D_MODEL), F32), jax.ShapeDtypeStruct((s, WR_W), BF16)],
        scratch_shapes=[], compiler_params=_ARB1, operands=[*pieces, w_r, x, nw, dxo]), rider)
    return dx, h, dnw, dproj, lands


def _inproj_bwd_dw(h, dproj, name):
    s = dproj.shape[0]
    tn = 256

    def body(h_ref, dp_ref, dw_ref):
        dw_ref[...] = _dot_tn(dp_ref[...], h_ref[...]).astype(BF16)

    return pl.pallas_call(
        body, name=name, grid=(WR_W // tn,),
        in_specs=[pl.BlockSpec((s, D_MODEL), lambda j: (0, 0)),
                  pl.BlockSpec((s, tn), lambda j: (0, j))],
        out_specs=pl.BlockSpec((tn, D_MODEL), lambda j: (j, 0)),
        out_shape=jax.ShapeDtypeStruct((WR_W, D_MODEL), BF16),
        compiler_params=_ARB1,
    )(h, dproj)


def _rearrange_w_in(wt):
    pad = jnp.zeros((FL_PAD - FOX_HEADS,) + wt.shape[1:], wt.dtype)
    return jnp.concatenate([wt[:3072], wt[3080:3336], wt[3336:IN_W], wt[3072:3080], pad], axis=0)


_W_IN_SEGMENTS = [(0, 3072, 0), (3072, 3080, QKV_W + MIX_W), (3080, 3336, 3072), (3336, IN_W, QKV_W)]
ASSEMBLE_COLS = 256
ASSEMBLE_ROWS = 128


def _shard_pieces(k):
    lo, hi = k * SHARD_W, (k + 1) * SHARD_W
    return [(max(lo, a) - lo, at + max(lo, a) - a, min(hi, b) - max(lo, a))
            for a, b, at in _W_IN_SEGMENTS if max(lo, a) < min(hi, b)]


def _assemble_w_in(v, to_blocks, name):
    used = QKV_W + MIX_W + FOX_HEADS

    def body(v_ref, o_ref):
        for k in range(N_DEV):
            for at, w_at, n in _shard_pieces(k):
                for r in range(0, n, ASSEMBLE_ROWS):
                    m = min(ASSEMBLE_ROWS, n - r)
                    if to_blocks:
                        o_ref[k, at + r:at + r + m, :] = v_ref[w_at + r:w_at + r + m, :]
                    else:
                        o_ref[w_at + r:w_at + r + m, :] = v_ref[k, at + r:at + r + m, :]
        if not to_blocks:
            o_ref[used:, :] = jnp.zeros((WR_W - used, ASSEMBLE_COLS), BF16)

    blocks = pl.BlockSpec((N_DEV, SHARD_W, ASSEMBLE_COLS), lambda j: (0, 0, j))
    rows = pl.BlockSpec((WR_W, ASSEMBLE_COLS), lambda j: (0, j))
    out_shape = (N_DEV, SHARD_W, D_MODEL) if to_blocks else (WR_W, D_MODEL)
    return pl.pallas_call(
        body, name=name, grid=(D_MODEL // ASSEMBLE_COLS,),
        in_specs=[rows if to_blocks else blocks], out_specs=blocks if to_blocks else rows,
        out_shape=jax.ShapeDtypeStruct(out_shape, BF16), compiler_params=_ARB1)(v)


def _restore_w_in(g):
    gate0 = QKV_W
    fl0 = QKV_W + MIX_W
    return jnp.concatenate(
        [g[:3072], g[fl0:fl0 + FOX_HEADS], g[3072:QKV_W], g[gate0:fl0]], axis=0)


def _pad_lanes(v, width=LANES):
    return jnp.pad(v, (0, width - v.shape[0])).reshape(1, width)


def _layer_fwd(xs, mem, nw, w_r, b_forget, mnw, late, onw, l, travel=None):
    s = xs.shape[0]
    bpad = _pad_lanes(b_forget)
    travel = _Travel(travel)
    qkv, gf, _ = travel.ride(2, _inproj_fwd, xs, nw, w_r, f"inproj_fwd_{l}")
    fqb, frow, fbounds = _fox_prep_fwd(gf, qkv, bpad, f"fox_prep_fwd_{l}")
    frow = frow.reshape(FOX_HEADS // 2, 2, s)
    ysb, _ = travel.ride(0, _sb_fwd, qkv, f"sb_fwd_{l}")
    yfx, lse_fx, _ = travel.ride(1, _fox_fwd, qkv, fqb, frow, fbounds, f"fox_fwd_{l}")
    wkv, wout = late(travel.lands)
    kv = _memkv_fwd(mem, mnw, wkv, f"memkv_fwd_{l}")
    ym, lse_m = _mem_fwd(qkv, kv, f"mem_fwd_{l}")
    xn = _outproj_fwd(ysb, yfx, ym, gf, onw, wout, xs, f"outproj_fwd_{l}")
    saved = (xs, nw, mnw, onw, bpad, qkv, gf, fqb, frow, fbounds, ysb, yfx, lse_fx, kv, ym, lse_m)
    return xn, saved, travel.lands, (wkv, wout)


class _Travel:
    def __init__(self, plan):
        self.plan = plan
        self.lands = None if plan is None else _new_lands(plan[0], plan[1])

    def ride(self, n, fn, *args):
        if self.plan is None or n >= len(self.plan[2]) or self.plan[2][n] is None:
            return fn(*args)
        srcs, scatter, legs = self.plan
        idx, rows = legs[n]
        out = fn(*args, rider=_Rider([srcs[a] for a in idx], [self.lands[a] for a in idx],
                                     scatter, rows, relay=not scatter))
        for a, land in zip(idx, out[-1]):
            self.lands[a] = land
        return out


def _layer_bwd(dx, saved, mem, w_r, wkv, wout, l, travel=None, own_w_out=None):
    xs, nw, mnw, onw, bpad, qkv, gf, fqb, frow, fbounds, ysb, yfx, lse_fx, kv, ym, lse_m = saved
    s = xs.shape[0]
    travel = _Travel(travel)
    dysb, dyfx, dym, dgate, dwout, donw, _ = travel.ride(
        3, _outproj_bwd, dx, wout, ysb, yfx, ym, gf, onw, f"outproj_bwd_{l}")
    if own_w_out is not None:
        travel.plan[0][own_w_out[0]] = own_w_out[1](dwout)
    sdq, sdk, sdv, _ = travel.ride(0, _sb_bwd, qkv, ysb, dysb, f"sb_bwd_{l}")
    fdq, fdk, fdv, dfrow, _ = travel.ride(1, _fox_bwd, qkv, fqb, frow, fbounds, yfx, lse_fx, dyfx,
                                          f"fox_bwd_{l}")
    dfl, db = _fox_prep_bwd(dfrow.reshape(FOX_HEADS, s), gf, bpad, f"fox_prep_bwd_{l}")
    dmq, dmk, dmv = _mem_bwd(qkv, kv, ym, lse_m, dym, f"mem_bwd_{l}")
    dwkv, dmnw = _memkv_bwd(mem, mnw, wkv, dmk, dmv, f"memkv_bwd_{l}")
    dx, ht, dnw, dproj, _ = travel.ride(2, _inproj_bwd_dx,
                                        [sdq, sdk, sdv, fdq, fdk, fdv, dmq, dgate, dfl],
                                        w_r, xs, nw, dx, f"inproj_bwd_dx_{l}")
    dwr = _inproj_bwd_dw(ht, dproj, f"inproj_bwd_dw_{l}")
    grads = dict(norm_w=dnw[0], w_r=dwr, b_forget=db[0, :FOX_HEADS], mem_norm_w=dmnw[0],
                 w_mem_kv=dwkv, out_norm_w=donw[0], w_out=dwout)
    return dx, grads, travel.lands


_ANY = pl.BlockSpec(memory_space=pl.ANY)


def _my_place():
    return lax.axis_index("x"), lax.axis_index("y"), lax.axis_index("c")


def _flip(v, bit):
    return 1 - v if bit else v


def _block_index(px, py, pc):
    return 4 * px + 2 * py + pc


def _all_gather_weights(shards, name):
    n = len(shards)

    def body(*refs):
        ins, outs = refs[:n], refs[n:2 * n]
        send_sems, recv_sems, local_sems = refs[2 * n:]
        x, y, c = _my_place()
        me = (x, y, c)
        sibling = (x, y, 1 - c)
        chips = [(1 - x, y), (x, 1 - y), (1 - x, 1 - y)]

        def copy(a, k, block, to, src=None):
            dst = outs[a].at[_block_index(*block)]
            return pltpu.make_async_remote_copy(
                src_ref=dst if src is None else src, dst_ref=dst,
                send_sem=send_sems.at[a, k], recv_sem=recv_sems.at[a, k],
                device_id=to, device_id_type=pl.DeviceIdType.MESH)

        mine = [pltpu.make_async_copy(ins[a], outs[a].at[_block_index(*me)], local_sems.at[a])
                for a in range(n)]
        for cp in mine:
            cp.start()
        first = []
        for a in range(n):
            first.append(copy(a, 0, me, sibling, src=ins[a]))
            first += [copy(a, 1 + j, me, (*chip, c), src=ins[a]) for j, chip in enumerate(chips)]
        for cp in first:
            cp.start()
        passed = []
        for j, chip in enumerate(chips):
            for a in range(n):
                copy(a, 1 + j, (*chip, c), me).wait_recv()
                fwd = copy(a, 4 + j, (*chip, c), sibling)
                fwd.start()
                passed.append(fwd)
        for a in range(n):
            copy(a, 0, sibling, me).wait_recv()
            for j, chip in enumerate(chips):
                copy(a, 4 + j, (*chip, 1 - c), me).wait_recv()
        for cp in first + passed:
            cp.wait_send()
        for cp in mine:
            cp.wait()

    return pl.pallas_call(
        body, name=name,
        in_specs=[_ANY] * n, out_specs=[_ANY] * n,
        out_shape=[jax.ShapeDtypeStruct((N_DEV,) + v.shape, v.dtype) for v in shards],
        scratch_shapes=[pltpu.SemaphoreType.DMA((n, 7)), pltpu.SemaphoreType.DMA((n, 7)),
                        pltpu.SemaphoreType.DMA((n,))],
    )(*shards)


def _exchange_blocks(blocked, name):
    n = len(blocked)

    def body(*refs):
        ins, outs = refs[:n], refs[n:2 * n]
        send_sems, recv_sems, local_sems = refs[2 * n:]
        x, y, c = _my_place()
        mine_idx = _block_index(x, y, c)
        local = [pltpu.make_async_copy(ins[a].at[mine_idx], outs[a].at[mine_idx], local_sems.at[a])
                 for a in range(n)]
        for cp in local:
            cp.start()
        sends, arrivals = [], []
        for r in range(1, N_DEV):
            peer = (_flip(x, r & 4), _flip(y, r & 2), _flip(c, r & 1))
            peer_idx = _block_index(*peer)
            for a in range(n):
                sems = dict(send_sem=send_sems.at[a, r - 1], recv_sem=recv_sems.at[a, r - 1],
                            device_id=peer, device_id_type=pl.DeviceIdType.MESH)
                sends.append(pltpu.make_async_remote_copy(
                    src_ref=ins[a].at[peer_idx], dst_ref=outs[a].at[mine_idx], **sems))
                arrivals.append(pltpu.make_async_remote_copy(
                    src_ref=ins[a].at[peer_idx], dst_ref=outs[a].at[peer_idx], **sems))
        for cp in sends:
            cp.start()
        for cp in arrivals:
            cp.wait_recv()
        for cp in sends:
            cp.wait_send()
        for cp in local:
            cp.wait()

    return pl.pallas_call(
        body, name=name,
        in_specs=[_ANY] * n, out_specs=[_ANY] * n,
        out_shape=[jax.ShapeDtypeStruct(v.shape, v.dtype) for v in blocked],
        scratch_shapes=[pltpu.SemaphoreType.DMA((n, 7)), pltpu.SemaphoreType.DMA((n, 7)),
                        pltpu.SemaphoreType.DMA((n,))],
    )(*blocked)


N_CHIP = N_DEV // 2


def _pair_swap(blocked, name):
    n = len(blocked)

    def body(*refs):
        ins, outs = refs[:n], refs[n:2 * n]
        send_sems, recv_sems = refs[2 * n:]
        x, y, c = _my_place()
        copies = [pltpu.make_async_remote_copy(
            src_ref=ins[a].at[j, 1 - c], dst_ref=outs[a].at[j],
            send_sem=send_sems.at[N_CHIP * a + j], recv_sem=recv_sems.at[N_CHIP * a + j],
            device_id=(x, y, 1 - c), device_id_type=pl.DeviceIdType.MESH)
            for a in range(n) for j in range(N_CHIP)]
        for cp in copies:
            cp.start()
        for cp in copies:
            cp.wait_recv()
        for cp in copies:
            cp.wait_send()

    return pl.pallas_call(
        body, name=name,
        in_specs=[_ANY] * n, out_specs=[_ANY] * n,
        out_shape=[jax.ShapeDtypeStruct((N_CHIP,) + v.shape[2:], v.dtype) for v in blocked],
        scratch_shapes=[pltpu.SemaphoreType.DMA((N_CHIP * n,)),
                        pltpu.SemaphoreType.DMA((N_CHIP * n,))],
    )(*blocked)


def _pair_add(by_core, theirs, core, name):
    n = len(by_core)

    def body(core_ref, *refs):
        for a in range(n):
            mine, got, out = refs[a], refs[n + a], refs[2 * n + a]
            out[...] = (mine[...].astype(F32) + got[...].astype(F32)).astype(BF16)

    def own(v):
        return pl.BlockSpec((None, None) + v.shape[2:], lambda j, core_ref: (j, core_ref[0], 0, 0))

    def block(v):
        return pl.BlockSpec((None,) + v.shape[1:], lambda j, core_ref: (j, 0, 0))

    return list(pl.pallas_call(
        body, name=name,
        grid_spec=pltpu.PrefetchScalarGridSpec(
            num_scalar_prefetch=1, grid=(N_CHIP,),
            in_specs=[own(v) for v in by_core] + [block(v) for v in theirs],
            out_specs=[block(v) for v in theirs]),
        out_shape=[jax.ShapeDtypeStruct(v.shape, BF16) for v in theirs],
        compiler_params=_ARB1,
    )(core, *by_core, *theirs))


def _chip_exchange(by_chip, to_all, name):
    n, m = len(by_chip), len(to_all)

    def body(*refs):
        ins, alls = refs[:n], refs[n:n + m]
        outs, all_outs = refs[n + m:2 * n + m], refs[2 * n + m:2 * (n + m)]
        send_sems, recv_sems, local_sems = refs[2 * (n + m):]
        x, y, c = _my_place()
        my_chip, mine_idx = 2 * x + y, _block_index(x, y, c)
        local = [pltpu.make_async_copy(ins[a].at[my_chip], outs[a].at[my_chip], local_sems.at[a])
                 for a in range(n)]
        local += [pltpu.make_async_copy(alls[b].at[mine_idx], all_outs[b].at[mine_idx],
                                        local_sems.at[n + b]) for b in range(m)]
        for cp in local:
            cp.start()
        sends, arrivals = [], []
        k = 0
        for r in range(1, N_DEV):
            peer = (_flip(x, r & 4), _flip(y, r & 2), _flip(c, r & 1))
            peer_chip, peer_idx = 2 * peer[0] + peer[1], _block_index(*peer)
            pairs = [(alls[b].at[mine_idx], all_outs[b].at[mine_idx], all_outs[b].at[peer_idx])
                     for b in range(m)]
            if not r & 1:
                pairs += [(ins[a].at[peer_chip], outs[a].at[my_chip], outs[a].at[peer_chip])
                          for a in range(n)]
            for src, there, here in pairs:
                sems = dict(send_sem=send_sems.at[k], recv_sem=recv_sems.at[k], device_id=peer,
                            device_id_type=pl.DeviceIdType.MESH)
                sends.append(pltpu.make_async_remote_copy(src_ref=src, dst_ref=there, **sems))
                arrivals.append(pltpu.make_async_remote_copy(src_ref=src, dst_ref=here, **sems))
                k += 1
        for cp in sends:
            cp.start()
        for cp in arrivals:
            cp.wait_recv()
        for cp in sends:
            cp.wait_send()
        for cp in local:
            cp.wait()

    n_copies = 7 * m + 3 * n
    return pl.pallas_call(
        body, name=name,
        in_specs=[_ANY] * (n + m), out_specs=[_ANY] * (n + m),
        out_shape=[jax.ShapeDtypeStruct(v.shape, v.dtype) for v in by_chip + to_all],
        scratch_shapes=[pltpu.SemaphoreType.DMA((n_copies,)), pltpu.SemaphoreType.DMA((n_copies,)),
                        pltpu.SemaphoreType.DMA((n + m,))],
    )(*by_chip, *to_all)


class _Rider(NamedTuple):
    srcs: list
    lands: list
    scatter: bool
    part: list
    relay: bool = False


def _window(ref, part):
    if part is None:
        return ref
    dim, start, size = part
    return ref.at[(slice(None),) * dim + (pl.ds(start, size),)]


def _relay_copies(srcs, lands, send_sems, recv_sems, rider):
    x, y, c = _my_place()
    me, sibling = (x, y, c), (x, y, 1 - c)
    chips = [(1 - x, y), (x, 1 - y), (1 - x, 1 - y)]
    first, from_chips, passed, last = [], [], [], []
    for a in range(len(srcs)):
        def copy(k, block, to, src=None, a=a):
            dst = _window(lands[a].at[_block_index(*block)], rider.part[a])
            return pltpu.make_async_remote_copy(
                src_ref=dst if src is None else src, dst_ref=dst,
                send_sem=send_sems.at[7 * a + k], recv_sem=recv_sems.at[7 * a + k],
                device_id=to, device_id_type=pl.DeviceIdType.MESH)

        mine = _window(srcs[a], rider.part[a])
        first.append(copy(0, me, sibling, src=mine))
        first += [copy(1 + j, me, (*chip, c), src=mine) for j, chip in enumerate(chips)]
        from_chips += [copy(1 + j, (*chip, c), me) for j, chip in enumerate(chips)]
        passed += [copy(4 + j, (*chip, c), sibling) for j, chip in enumerate(chips)]
        last.append(copy(0, sibling, me))
        last += [copy(4 + j, (*chip, 1 - c), me) for j, chip in enumerate(chips)]
    return first, from_chips, passed, last


def _new_lands(srcs, scatter):
    return [lax.empty(v.shape if scatter else (N_DEV,) + v.shape, v.dtype) for v in srcs]


def _rider_copies(srcs, lands, send_sems, recv_sems, rider):
    x, y, c = _my_place()
    mine_idx = _block_index(x, y, c)

    def window(ref, a):
        return _window(ref, rider.part[a])

    sends, arrivals = [], []
    for r in range(1, N_DEV):
        peer = (_flip(x, r & 4), _flip(y, r & 2), _flip(c, r & 1))
        peer_idx = _block_index(*peer)
        for a in range(len(srcs)):
            src = window(srcs[a].at[peer_idx] if rider.scatter else srcs[a], a)
            k = 7 * a + r - 1
            sems = dict(send_sem=send_sems.at[k], recv_sem=recv_sems.at[k],
                        device_id=peer, device_id_type=pl.DeviceIdType.MESH)
            sends.append(pltpu.make_async_remote_copy(
                src_ref=src, dst_ref=window(lands[a].at[mine_idx], a), **sems))
            arrivals.append(pltpu.make_async_remote_copy(
                src_ref=src, dst_ref=window(lands[a].at[peer_idx], a), **sems))
    return sends, arrivals


def _ride(call, rider):
    call = dict(call)
    body, grid = call.pop("body"), call["grid"]
    operands = call.pop("operands")
    if rider is None:
        return list(pl.pallas_call(body, **call)(*operands)), None
    n_in, n_out = len(call["in_specs"]), len(call["out_specs"])
    n_scratch = len(call["scratch_shapes"])
    m = len(rider.srcs)

    def riding(*refs):
        main_in, srcs, lands = refs[:n_in], refs[n_in:n_in + m], refs[n_in + m:n_in + 2 * m]
        main_out = refs[n_in + 2 * m:n_in + 2 * m + n_out]
        rest = refs[n_in + 3 * m + n_out:]
        send_sems, recv_sems = rest[n_scratch:]
        at = [pl.program_id(d) for d in range(len(grid))]
        first = functools.reduce(jnp.logical_and, [p == 0 for p in at])
        last = functools.reduce(jnp.logical_and, [p == g - 1 for p, g in zip(at, grid)])
        if rider.relay:
            sends, from_chips, passed, arrivals = _relay_copies(
                srcs, lands, send_sems, recv_sems, rider)
            step, steps = 0, 1
            for p, g in zip(at, grid):
                step, steps = step * g + p, steps * g
            assert steps >= 2, "a relayed gather needs a later grid step to pass blocks on"

            @pl.when(step == (3 * steps) // 4)
            def _():
                for cp in from_chips:
                    cp.wait_recv()
                for cp in passed:
                    cp.start()
        else:
            sends, arrivals = _rider_copies(srcs, lands, send_sems, recv_sems, rider)
            passed = []

        @pl.when(first)
        def _():
            for cp in sends:
                cp.start()

        body(*main_in, *main_out, *rest[:n_scratch])

        @pl.when(last)
        def _():
            for cp in arrivals:
                cp.wait_recv()
            for cp in sends + passed:
                cp.wait_send()

    call["in_specs"] = list(call["in_specs"]) + [_ANY] * (2 * m)
    call["out_specs"] = list(call["out_specs"]) + [_ANY] * m
    call["out_shape"] = list(call["out_shape"]) + [
        jax.ShapeDtypeStruct(v.shape, v.dtype) for v in rider.lands]
    call["scratch_shapes"] = list(call["scratch_shapes"]) + [
        pltpu.SemaphoreType.DMA((7 * m,)), pltpu.SemaphoreType.DMA((7 * m,))]
    call["input_output_aliases"] = {n_in + m + a: n_out + a for a in range(m)}
    outs = pl.pallas_call(riding, **call)(*operands, *rider.srcs, *rider.lands)
    return list(outs[:n_out]), list(outs[n_out:])


def _sum_parts(p_ref):
    g = p_ref[0].astype(F32)
    for k in range(1, p_ref.shape[0]):
        g = g + p_ref[k].astype(F32)
    return g


def _adamw(g, w, m, v):
    c1 = 1.0 / (1.0 - ADAM_B1 ** ADAM_STEP)
    c2 = 1.0 / (1.0 - ADAM_B2 ** ADAM_STEP)
    nm = ADAM_B1 * m + (1.0 - ADAM_B1) * g
    nv = ADAM_B2 * v + (1.0 - ADAM_B2) * (g * g)
    return nm, nv, -ADAM_LR * ((nm * c1) / (jnp.sqrt(nv * c2) + ADAM_EPS) + ADAM_WD * w)


def _adamw_w_in(parts, w, m, v, name):
    ncol_blk, depth, nfeat = w.shape
    cols = 256

    def body(*refs):
        p_refs = refs[:depth]
        w_ref, m_ref, v_ref, g_ref, d_ref, nm_ref, nv_ref = refs[depth:]
        g = jnp.stack([_sum_parts(p_refs[l]) for l in range(depth)], axis=1)
        nm, nv, d = _adamw(g, w_ref[...], m_ref[...], v_ref[...])
        g_ref[...] = g
        nm_ref[...] = nm
        nv_ref[...] = nv
        d_ref[...] = d

    blk = pl.BlockSpec((ncol_blk, depth, cols), lambda j: (0, 0, j))
    return pl.pallas_call(
        body, name=name, grid=(nfeat // cols,),
        in_specs=[pl.BlockSpec((p.shape[0], ncol_blk, cols), lambda j: (0, 0, j)) for p in parts]
        + [blk] * 3,
        out_specs=[blk] * 4,
        out_shape=[jax.ShapeDtypeStruct(w.shape, F32)] * 4,
        compiler_params=_ARB1,
    )(*parts, w, m, v)


def _adamw_sum(parts, w, m, v, tile, name):
    depth, nrow, ncol = w.shape
    rows, cols = tile

    def body(*refs):
        p_refs = refs[:depth]
        w_ref, m_ref, v_ref, g_ref, d_ref, nm_ref, nv_ref = refs[depth:]
        layer = pl.program_id(0)
        for l in range(depth):
            @pl.when(layer == l)
            def _(p_ref=p_refs[l]):
                g = _sum_parts(p_ref)
                nm, nv, d = _adamw(g, w_ref[...], m_ref[...], v_ref[...])
                g_ref[...] = g
                nm_ref[...] = nm
                nv_ref[...] = nv
                d_ref[...] = d

    def part_spec(l):
        return pl.BlockSpec((parts[l].shape[0], rows, cols), lambda q, i, j: (
            0, jnp.where(q == l, i, 0), jnp.where(q == l, j, 0)))

    blk = pl.BlockSpec((None, rows, cols), lambda q, i, j: (q, i, j))
    return pl.pallas_call(
        body, name=name, grid=(depth, nrow // rows, ncol // cols),
        in_specs=[part_spec(l) for l in range(depth)] + [blk, blk, blk],
        out_specs=[blk] * 4,
        out_shape=[jax.ShapeDtypeStruct(w.shape, F32)] * 4,
        compiler_params=pltpu.CompilerParams(
            dimension_semantics=("arbitrary", "arbitrary", "arbitrary")),
    )(*parts, w, m, v)


def _adamw_small(parts, ws, ms, vs, name):
    n = len(parts)

    def body(*refs):
        p, w, m, v = (refs[a * n:(a + 1) * n] for a in range(4))
        out = refs[4 * n:]
        for a in range(n):
            g = _sum_parts(p[a])
            nm, nv, d = _adamw(g, w[a][...], m[a][...], v[a][...])
            out[a][...] = g
            out[n + a][...] = d
            out[2 * n + a][...] = nm
            out[3 * n + a][...] = nv

    vmem = pl.BlockSpec(memory_space=pltpu.VMEM)
    outs = pl.pallas_call(
        body, name=name, in_specs=[vmem] * (4 * n), out_specs=[vmem] * (4 * n),
        out_shape=[jax.ShapeDtypeStruct(w.shape, F32) for w in ws] * 4,
    )(*parts, *ws, *ms, *vs)
    return [list(outs[a * n:(a + 1) * n]) for a in range(4)]


def _misc_rows(b_forget, extra=None):
    tile = jnp.pad(b_forget, ((0, 6), (0, LANES - FOX_HEADS)))
    return tile if extra is None else tile.at[2].set(extra)


def kernel(x, mem, norm_w, w_in, b_forget, mem_norm_w, w_mem_kv, out_norm_w, w_out, final_norm_w, loss_target, m_norm_w, m_w_in, m_b_forget, m_mem_norm_w, m_w_mem_kv, m_out_norm_w, m_w_out, m_final_norm_w, v_norm_w, v_w_in, v_b_forget, v_mem_norm_w, v_w_mem_kv, v_out_norm_w, v_w_out, v_final_norm_w):
    kv_rows = w_mem_kv.shape[1]
    out_rows = w_out.shape[1]
    me = _block_index(*_my_place())

    def shards(l):
        return [w_in[l].T.astype(BF16), w_mem_kv[l].astype(BF16), w_out[l].astype(BF16)]


    def full_kv_out(g_kv, g_out):
        return g_kv.reshape(D_MODEL, 2 * MEM_W), g_out.reshape(MIX_W, D_MODEL)

    def kv_blocks(g):
        return g.reshape(N_DEV, kv_rows, 2 * MEM_W).astype(BF16)

    def out_blocks(g):
        return g.reshape(N_DEV, out_rows, D_MODEL).astype(BF16)

    def with_own(land, own):
        return lax.dynamic_update_slice(land, own[None], (me,) + (0,) * own.ndim)

    def with_own_of(land, blocked):
        return lax.dynamic_update_slice(land, lax.dynamic_slice_in_dim(blocked, me, 1, axis=0),
                                        (me,) + (0,) * (land.ndim - 1))

    def row(v):
        return v.reshape(1, -1)

    def cols(first, size):
        return (1, first, size)

    fwd_split, bwd_split = 4 * LANES, (5 * LANES, 6 * LANES)

    s_in0, s_kv0, s_out0 = shards(0)
    (g_in0,) = _all_gather_weights([s_in0], "all_gather_l0")
    w_r0 = _assemble_w_in(g_in0, False, "assemble_w_in_0")
    s_in1, s_kv1, s_out1 = shards(1)
    x1, saved0, (l_in1, _, _), (wkv0, wout0) = _layer_fwd(
        x[0], mem[0], row(norm_w[0]), w_r0, b_forget[0], row(mem_norm_w[0]),
        lambda lands: full_kv_out(with_own(lands[1], s_kv0), with_own(lands[2], s_out0)),
        row(out_norm_w[0]), 0,
        travel=([s_in1, s_kv0, s_out0], False,
                [([0], [cols(0, fwd_split)]), ([0], [cols(fwd_split, D_MODEL - fwd_split)]),
                 ([1, 2], [None, None])]))
    w_r1 = _assemble_w_in(with_own(l_in1, s_in1), False, "assemble_w_in_1")
    x2, saved1, _, (wkv1, wout1) = _layer_fwd(
        x1, mem[0], row(norm_w[1]), w_r1, b_forget[1], row(mem_norm_w[1]),
        lambda lands: full_kv_out(with_own(lands[0], s_kv1), with_own(lands[1], s_out1)),
        row(out_norm_w[1]), 1, travel=([s_kv1, s_out1], False, [([0, 1], [None, None]), None]))

    dx2, loss_part, dfnw = _final_fwd_bwd(x2, row(final_norm_w), loss_target[0], "final_fwd_bwd")

    dw_out_blocks = jax.ShapeDtypeStruct((N_DEV, out_rows, D_MODEL), BF16)
    dx1, gr1, (l_out1,) = _layer_bwd(
        dx2, saved1, mem[0], w_r1, wkv1, wout1, 1,
        travel=([dw_out_blocks], True, [([0], [None])]), own_w_out=(0, out_blocks))
    p_in1 = _assemble_w_in(gr1["w_r"], True, "w_in_grad_blocks_1")
    p_kv1 = kv_blocks(gr1["w_mem_kv"])
    grad_x, gr0, (l_in1, l_kv1, l_out0) = _layer_bwd(
        dx1[0], saved0, mem[0], w_r0, wkv0, wout0, 0,
        travel=([p_in1, p_kv1, dw_out_blocks], True,
                [([0, 1], [cols(0, bwd_split[0]), None]),
                 ([0, 2], [cols(bwd_split[0], bwd_split[1] - bwd_split[0]), None]),
                 None,
                 ([0], [cols(bwd_split[1], D_MODEL - bwd_split[1])])]),
        own_w_out=(2, out_blocks))
    r_out1 = with_own_of(l_out1, out_blocks(gr1["w_out"]))
    r_in1, r_kv1 = with_own_of(l_in1, p_in1), with_own_of(l_kv1, p_kv1)
    r_out0 = with_own_of(l_out0, out_blocks(gr0["w_out"]))

    def both(name):
        return jnp.stack([gr0[name], gr1[name]])

    small = [both("norm_w"), both("mem_norm_w"), both("out_norm_w"), dfnw,
             _misc_rows(both("b_forget"), loss_part[0])]
    p_small = [jnp.broadcast_to(v[None], (N_DEV,) + v.shape) for v in small]
    by_core = [v.reshape((N_CHIP, 2) + v.shape[1:])
               for v in (_assemble_w_in(gr0["w_r"], True, "w_in_grad_blocks_0"),
                         kv_blocks(gr0["w_mem_kv"]))]
    from_sibling = _pair_swap(by_core, "grads_l0_pair_swap")
    core = lax.axis_index("c").astype(jnp.int32).reshape(1)
    chip_sums = _pair_add(by_core, from_sibling, core, "grads_l0_pair_add")
    r_in0, r_kv0, *r_small = _chip_exchange(chip_sums, p_small, "exchange_grads_l0")

    def view(v):
        return jnp.transpose(v, (2, 0, 1))

    g_w_in, d_w_in, nm_w_in, nv_w_in = [jnp.transpose(v, (1, 2, 0)) for v in _adamw_w_in(
        [r_in0, r_in1], view(w_in), view(m_w_in), view(v_w_in), "adamw_w_in")]
    g_w_kv, d_w_kv, nm_w_kv, nv_w_kv = _adamw_sum(
        [r_kv0, r_kv1], w_mem_kv, m_w_mem_kv, v_w_mem_kv, (kv_rows, 2 * MEM_W), "adamw_w_mem_kv")
    g_w_out, d_w_out, nm_w_out, nv_w_out = _adamw_sum(
        [r_out0, r_out1], w_out, m_w_out, v_w_out, (out_rows, D_MODEL), "adamw_w_out")
    def smalls(nw, mnw, onw, fnw, b):
        return [nw, mnw, onw, row(fnw), _misc_rows(b)]

    small_out = _adamw_small(
        r_small, smalls(norm_w, mem_norm_w, out_norm_w, final_norm_w, b_forget),
        smalls(m_norm_w, m_mem_norm_w, m_out_norm_w, m_final_norm_w, m_b_forget),
        smalls(v_norm_w, v_mem_norm_w, v_out_norm_w, v_final_norm_w, v_b_forget), "adamw_small")
    (g_nw, g_mnw, g_onw, g_fnw, g_b), (d_nw, d_mnw, d_onw, d_fnw, d_b), \
        (nm_nw, nm_mnw, nm_onw, nm_fnw, nm_b), (nv_nw, nv_mnw, nv_onw, nv_fnw, nv_b) = [
            (nw, mnw, onw, fnw[0], misc[:2, :FOX_HEADS]) for nw, mnw, onw, fnw, misc in small_out]
    loss = small_out[0][4][2, 0]

    return (loss, grad_x,
            g_nw, g_w_in, g_b, g_mnw, g_w_kv, g_onw, g_w_out, g_fnw,
            d_nw, d_w_in, d_b, d_mnw, d_w_kv, d_onw, d_w_out, d_fnw,
            nm_nw, nm_w_in, nm_b, nm_mnw, nm_w_kv, nm_onw, nm_w_out, nm_fnw,
            nv_nw, nv_w_in, nv_b, nv_mnw, nv_w_kv, nv_onw, nv_w_out, nv_fnw)
```

```python
import functools
from typing import NamedTuple

import jax
import jax.numpy as jnp
from jax import lax
from jax.experimental import pallas as pl
from jax.experimental.pallas import tpu as pltpu

F32 = jnp.float32
BF16 = jnp.bfloat16

N_DEV = 8
D_MODEL = 1024
HEAD_DIM = 64
LANES = 128
SB_W = 512
FX_W = 512
MEM_W = 256
MIX_W = 1280
FOX_HEADS = 8
IN_W = 4616
SHARD_W = IN_W // N_DEV
QKV_W = 3 * SB_W + 3 * FX_W + MEM_W
FL_PAD = 256
GF_W = MIX_W + FL_PAD
WR_W = QKV_W + GF_W
EPS = 1e-6
T = 256
QPS = 4
TM = 256
TQM = 512
Q_SCALE = 0.125
NEG = -1e30
UNDERFLOW = -110.0
NORM_SLACK = 1.01

ADAM_LR = 0.001
ADAM_B1 = 0.9
ADAM_B2 = 0.999
ADAM_EPS = 1e-08
ADAM_WD = 0.01
ADAM_STEP = 10


_NT = (((1,), (1,)), ((), ()))
_TN = (((0,), (0,)), ((), ()))

_ARB1 = pltpu.CompilerParams(dimension_semantics=("arbitrary",))
_ARB2 = pltpu.CompilerParams(dimension_semantics=("arbitrary", "arbitrary"))


def _dot(a, b):
    return jnp.dot(a, b, preferred_element_type=F32)


def _dot_nt(a, b):
    return lax.dot_general(a, b, _NT, preferred_element_type=F32)


def _dot_tn(a, b):
    return lax.dot_general(a, b, _TN, preferred_element_type=F32)


def _split2(x):
    hi = x.astype(BF16)
    lo = (x - hi.astype(F32)).astype(BF16)
    return hi, lo


def _stack2(u):
    return jnp.concatenate([u, u], axis=0)


def _cum2(x, u2):
    hi, lo = _split2(x)
    return _dot(jnp.concatenate([hi, lo], axis=1), u2)


def _tri3(tri, x, dot=None):
    dot = dot or _dot
    hi = x.astype(BF16)
    r1 = x - hi.astype(F32)
    mid = r1.astype(BF16)
    lo = (r1 - mid.astype(F32)).astype(BF16)
    return dot(tri, hi) + dot(tri, mid) + dot(tri, lo)


def _iota2(shape, dim):
    return lax.broadcasted_iota(jnp.int32, shape, dim)


def _head_block_diag():
    r = _iota2((LANES, LANES), 0) // HEAD_DIM
    c = _iota2((LANES, LANES), 1) // HEAD_DIM
    return _stack2(jnp.where(r == c, 1.0, 0.0).astype(BF16))


def _head_mean(x, bd):
    return _cum2(x, bd) * (1.0 / HEAD_DIM)


def _sigmoid(x):
    return 1.0 / (1.0 + jnp.exp(-x))


def _log_sigmoid(x):
    return jnp.minimum(x, 0.0) - jnp.log(1.0 + jnp.exp(-jnp.abs(x)))


def _running_top(r_ref):
    return jnp.max(jnp.maximum(r_ref[0], r_ref[1]))


def _fox_tiles_left(i, pair, nq, fb_ref, tile):
    def bound(j):
        b = []
        for h in range(2):
            head = 2 * pair + h
            b.append(2.0 * NORM_SLACK * fb_ref[2 * nq + i, head] * fb_ref[3 * nq, head]
                     + fb_ref[2 * i, head] - fb_ref[2 * j + 1, head])
        return jnp.maximum(b[0], b[1])

    def more(j):
        return jnp.logical_and(j >= 0, bound(jnp.maximum(j, 0)) > UNDERFLOW)

    def step(j):
        tile(j, False)
        return j - 1

    return lax.while_loop(more, step, i - 1)


def _pair_masks():
    lane = _iota2((1, LANES), 1)
    return lane < HEAD_DIM


def _split_pair(x, m0):
    zero = jnp.zeros_like(x)
    return jnp.where(m0, x, zero), jnp.where(m0, zero, x)


def _inproj_fwd(x, nw, w_r, name, rider=None):
    s = x.shape[0]

    def body(x_ref, nw_ref, w_ref, qkv_ref, gf_ref):
        xv = x_ref[...]
        r = lax.rsqrt(jnp.mean(xv * xv, axis=-1, keepdims=True) + EPS)
        h = (xv * r * nw_ref[...]).astype(BF16)
        for c in range(0, QKV_W, 256):
            qkv_ref[:, c:c + 256] = _dot_nt(h, w_ref[c:c + 256, :]).astype(BF16)
        for c in range(0, GF_W, 256):
            gf_ref[:, c:c + 256] = _dot_nt(h, w_ref[QKV_W + c:QKV_W + c + 256, :])

    (qkv, gf), lands = _ride(dict(
        body=body, name=name, grid=(s // TM,),
        in_specs=[pl.BlockSpec((TM, D_MODEL), lambda i: (i, 0)),
                  pl.BlockSpec((1, D_MODEL), lambda i: (0, 0)),
                  pl.BlockSpec((WR_W, D_MODEL), lambda i: (0, 0))],
        out_specs=[pl.BlockSpec((TM, QKV_W), lambda i: (i, 0)),
                   pl.BlockSpec((TM, GF_W), lambda i: (i, 0))],
        out_shape=[jax.ShapeDtypeStruct((s, QKV_W), BF16), jax.ShapeDtypeStruct((s, GF_W), F32)],
        scratch_shapes=[], compiler_params=_ARB1, operands=[x, nw, w_r]), rider)
    return qkv, gf, lands


def _fox_prep_fwd(gf, qkv, bpad, name):
    s = gf.shape[0]
    nq = s // T
    nrow = -(-(3 * nq + 1) // 8) * 8

    def body(fl_ref, q_ref, k_ref, b_ref, fq_ref, fr_ref, fb_ref):
        tri = jnp.where(_iota2((T, T), 0) >= _iota2((T, T), 1), 1.0, 0.0).astype(BF16)
        m0 = _pair_masks()
        lane = _iota2((1, LANES), 1)
        norms = [jnp.zeros((1, LANES), F32) for _ in range(nq + 1)]
        same_head = (_iota2((LANES, LANES), 0) // HEAD_DIM) == (_iota2((LANES, LANES), 1) // HEAD_DIM)
        bd = jnp.where(same_head, 1.0, 0.0).astype(BF16)
        for p in range(FOX_HEADS // 2):
            cols = slice(p * LANES, (p + 1) * LANES)
            q = (q_ref[:, cols] * jnp.asarray(Q_SCALE, BF16)).astype(F32)
            k = k_ref[:, cols].astype(F32)
            qn = _dot((q * q).astype(BF16), bd)
            kn = _dot((k * k).astype(BF16), bd)
            tops = [jnp.max(qn[j * T:(j + 1) * T], axis=0, keepdims=True) for j in range(nq)]
            tops.append(jnp.max(kn, axis=0, keepdims=True))
            tops = [jnp.sqrt(top) for top in tops]
            for h in range(2):
                at = h * HEAD_DIM
                norms = [jnp.where(lane == 2 * p + h, top[:, at:at + 1], row)
                         for top, row in zip(tops, norms)]
        for j in range(nq + 1):
            fb_ref[2 * nq + j:2 * nq + j + 1, :] = norms[j]
        fb_ref[3 * nq + 1:, :] = jnp.zeros((nrow - 3 * nq - 1, LANES), F32)
        carry = jnp.zeros((1, LANES), F32)
        for blk in range(s // T):
            rows = slice(blk * T, (blk + 1) * T)
            lf = _log_sigmoid(fl_ref[rows, :] + b_ref[...])
            c = _tri3(tri, lf) + carry
            carry = c[T - 1:T, :]
            for p in range(FOX_HEADS // 2):
                fq_ref[rows, p * LANES:(p + 1) * LANES] = jnp.where(
                    m0, c[:, 2 * p:2 * p + 1], c[:, 2 * p + 1:2 * p + 2])
            fr_ref[:, rows] = c.T[0:FOX_HEADS, :]
            fb_ref[2 * blk:2 * blk + 1, :] = c[0:1, :]
            fb_ref[2 * blk + 1:2 * blk + 2, :] = carry

    base = 3 * SB_W // FX_W
    return pl.pallas_call(
        body, name=name, grid=(1,),
        in_specs=[pl.BlockSpec((s, LANES), lambda i: (0, MIX_W // LANES)),
                  pl.BlockSpec((s, FX_W), lambda i: (0, base)),
                  pl.BlockSpec((s, FX_W), lambda i: (0, base + 1)),
                  pl.BlockSpec((1, LANES), lambda i: (0, 0))],
        out_specs=[pl.BlockSpec((s, FX_W), lambda i: (0, 0)),
                   pl.BlockSpec((FOX_HEADS, s), lambda i: (0, 0)),
                   pl.BlockSpec((nrow, LANES), lambda i: (0, 0))],
        out_shape=[jax.ShapeDtypeStruct((s, FX_W), F32), jax.ShapeDtypeStruct((FOX_HEADS, s), F32),
                   jax.ShapeDtypeStruct((nrow, LANES), F32)],
        compiler_params=_ARB1,
    )(gf, qkv, qkv, bpad)


def _sb_fwd(qkv, name, rider=None):
    s = qkv.shape[0]

    def body(q_ref, k_ref, v_ref, o_ref, acc_ref, r_ref, as_ref):
        m0 = _pair_masks()
        strict = _iota2((T, T), 0) > _iota2((T, T), 1)
        u2 = _stack2(jnp.where(strict, 1.0, 0.0).astype(BF16))
        hs = range(2)

        def query_tile(i, rows):
            qh = _split_pair(q_ref[rows, :] * jnp.asarray(Q_SCALE, BF16), m0)
            acc_ref[...] = jnp.zeros_like(acc_ref)
            r_ref[...] = jnp.zeros_like(r_ref)

            def flush(j):
                v = v_ref[pl.ds(pl.multiple_of(j * T, T), T), :]
                for h in hs:
                    acc_ref[h] += _dot(as_ref[h], v)

            def tile(j, diag):
                k = k_ref[pl.ds(pl.multiple_of(j * T, T), T), :]
                z = [_dot_nt(qh[h], k) for h in hs]
                if not diag:
                    flush(j + 1)
                la = [jnp.minimum(z[h], 0.0) - jnp.log(1.0 + jnp.exp(-jnp.abs(z[h]))) for h in hs]
                lf = [la[h] - z[h] for h in hs]
                if diag:
                    lf = [jnp.where(strict, lf[h], 0.0) for h in hs]
                cin = [_cum2(lf[h], u2) for h in hs]
                a = [jnp.exp(la[h] + cin[h] + r_ref[h]) for h in hs]
                if diag:
                    a = [jnp.where(strict, a[h], 0.0) for h in hs]
                for h in hs:
                    r_ref[h] += cin[h][:, 0:1] + lf[h][:, 0:1]
                    as_ref[h] = a[h].astype(BF16)

            tile(i, True)

            def more(state):
                j, top = state
                return jnp.logical_and(j >= 0, top > UNDERFLOW)

            def step(state):
                j, _ = state
                tile(j, False)
                return j - 1, _running_top(r_ref)

            j_left, _ = lax.while_loop(more, step, (i - 1, _running_top(r_ref)))
            flush(j_left + 1)
            o_ref[rows, :] = jnp.where(m0, acc_ref[0], acc_ref[1])

        for n in range(QPS):
            query_tile(QPS * pl.program_id(1) + n, slice(n * T, (n + 1) * T))

    nb = SB_W // LANES
    (ysb,), lands = _ride(dict(
        body=body, name=name, grid=(nb, s // (QPS * T)),
        in_specs=[pl.BlockSpec((QPS * T, LANES), lambda p, i: (i, p)),
                  pl.BlockSpec((s, LANES), lambda p, i: (0, nb + p)),
                  pl.BlockSpec((s, LANES), lambda p, i: (0, 2 * nb + p))],
        out_specs=[pl.BlockSpec((QPS * T, LANES), lambda p, i: (i, p))],
        out_shape=[jax.ShapeDtypeStruct((s, SB_W), F32)],
        scratch_shapes=[pltpu.VMEM((2, T, LANES), F32), pltpu.VMEM((2, T, 1), F32),
                        pltpu.VMEM((2, T, T), BF16)],
        compiler_params=_ARB2, operands=[qkv, qkv, qkv]), rider)
    return ysb, lands


def _fox_fwd(qkv, fqb, frow, fbounds, name, rider=None):
    s = qkv.shape[0]

    def body(q_ref, k_ref, v_ref, fq_ref, fr_ref, fb_ref, o_ref, lse_ref, acc_ref, m_ref, ps_ref):
        pair = pl.program_id(0)
        m0 = _pair_masks()
        causal = _iota2((T, T), 0) >= _iota2((T, T), 1)
        ones = jnp.ones((T, LANES), BF16)
        hs = range(2)

        def query_tile(i, rows):
            qh = _split_pair(q_ref[rows, :] * jnp.asarray(Q_SCALE, BF16), m0)
            fq = fq_ref[rows, :]
            fqh = (fq[:, 0:1], fq[:, HEAD_DIM:HEAD_DIM + 1])
            acc_ref[...] = jnp.zeros_like(acc_ref)
            m_ref[...] = jnp.full_like(m_ref, NEG)

            def flush(j):
                v = v_ref[pl.ds(pl.multiple_of(j * T, T), T), :]
                va2 = _stack2(jnp.concatenate([v, ones], axis=1))
                for h in hs:
                    acc_ref[h] += _dot(ps_ref[h], va2)

            def tile(j, diag):
                off = pl.multiple_of(j * T, T)
                k = k_ref[pl.ds(off, T), :]
                sc = [_dot_nt(qh[h], k) + fqh[h] - fr_ref[h:h + 1, pl.ds(off, T)] for h in hs]
                if not diag:
                    flush(j + 1)
                if diag:
                    sc = [jnp.where(causal, sc[h], NEG) for h in hs]
                m_new = [jnp.maximum(m_ref[h], jnp.max(sc[h], axis=1, keepdims=True)) for h in hs]
                p = [jnp.exp(sc[h] - m_new[h]) for h in hs]
                for h in hs:
                    acc_ref[h] = acc_ref[h] * jnp.exp(m_ref[h] - m_new[h])
                    m_ref[h] = m_new[h]
                    p_hi, p_lo = _split2(p[h])
                    ps_ref[h] = jnp.concatenate([p_hi, p_lo], axis=1)

            tile(i, True)
            j_left = _fox_tiles_left(i, pair, s // T, fb_ref, tile)
            flush(j_left + 1)
            acc = (acc_ref[0], acc_ref[1])
            o_ref[rows, :] = jnp.where(m0, acc[0][:, :LANES] / acc[0][:, LANES:],
                                       acc[1][:, :LANES] / acc[1][:, LANES:])
            lse_ref[rows, :] = jnp.where(m0, m_ref[0] + jnp.log(acc[0][:, LANES:]),
                                         m_ref[1] + jnp.log(acc[1][:, LANES:]))

        for n in range(QPS):
            query_tile(QPS * pl.program_id(1) + n, slice(n * T, (n + 1) * T))

    nb = FX_W // LANES
    base = 3 * SB_W // LANES
    (yfx, lse), lands = _ride(dict(
        body=body, name=name, grid=(nb, s // (QPS * T)),
        in_specs=[pl.BlockSpec((QPS * T, LANES), lambda p, i: (i, base + p)),
                  pl.BlockSpec((s, LANES), lambda p, i: (0, base + nb + p)),
                  pl.BlockSpec((s, LANES), lambda p, i: (0, base + 2 * nb + p)),
                  pl.BlockSpec((QPS * T, LANES), lambda p, i: (i, p)),
                  pl.BlockSpec((None, 2, s), lambda p, i: (p, 0, 0)),
                  pl.BlockSpec(memory_space=pltpu.SMEM)],
        out_specs=[pl.BlockSpec((QPS * T, LANES), lambda p, i: (i, p)),
                   pl.BlockSpec((QPS * T, LANES), lambda p, i: (i, p))],
        out_shape=[jax.ShapeDtypeStruct((s, FX_W), F32), jax.ShapeDtypeStruct((s, FX_W), F32)],
        scratch_shapes=[pltpu.VMEM((2, T, 2 * LANES), F32), pltpu.VMEM((2, T, 1), F32),
                        pltpu.VMEM((2, T, 2 * T), BF16)],
        compiler_params=_ARB2, operands=[qkv, qkv, qkv, fqb, frow, fbounds]), rider)
    return yfx, lse, lands


def _memkv_fwd(mem, mnw, wkv, name):
    n = mem.shape[0]

    def body(mem_ref, mnw_ref, w_ref, kv_ref):
        mv = mem_ref[...]
        r = lax.rsqrt(jnp.mean(mv * mv, axis=-1, keepdims=True) + EPS)
        hm = (mv * r * mnw_ref[...]).astype(BF16)
        kv_ref[...] = _dot(hm, w_ref[...]).astype(BF16)

    return pl.pallas_call(
        body, name=name, grid=(1,),
        in_specs=[pl.BlockSpec((n, D_MODEL), lambda i: (0, 0)),
                  pl.BlockSpec((1, D_MODEL), lambda i: (0, 0)),
                  pl.BlockSpec((D_MODEL, 2 * MEM_W), lambda i: (0, 0))],
        out_specs=pl.BlockSpec((n, 2 * MEM_W), lambda i: (0, 0)),
        out_shape=jax.ShapeDtypeStruct((n, 2 * MEM_W), BF16),
        compiler_params=_ARB1,
    )(mem, mnw, wkv)


def _mem_fwd(qkv, kv, name):
    s = qkv.shape[0]
    n = kv.shape[0]

    def body(q_ref, k_ref, v_ref, o_ref, lse_ref):
        m0 = _pair_masks()
        qh = _split_pair(q_ref[...] * jnp.asarray(Q_SCALE, BF16), m0)
        k = k_ref[...]
        v = v_ref[...]
        outs, lses = [], []
        for h in range(2):
            sc = _dot_nt(qh[h], k)
            mx = jnp.max(sc, axis=1, keepdims=True)
            p = jnp.exp(sc - mx)
            l = jnp.sum(p, axis=1, keepdims=True)
            outs.append(_dot(p.astype(BF16), v) / l)
            lses.append(mx + jnp.log(l))
        o_ref[...] = jnp.where(m0, outs[0], outs[1])
        lse_ref[...] = jnp.where(m0, lses[0], lses[1])

    nb = MEM_W // LANES
    base = (3 * SB_W + 3 * FX_W) // LANES
    return pl.pallas_call(
        body, name=name, grid=(nb, s // TQM),
        in_specs=[pl.BlockSpec((TQM, LANES), lambda p, i: (i, base + p)),
                  pl.BlockSpec((n, LANES), lambda p, i: (0, p)),
                  pl.BlockSpec((n, LANES), lambda p, i: (0, nb + p))],
        out_specs=[pl.BlockSpec((TQM, LANES), lambda p, i: (i, p)),
                   pl.BlockSpec((TQM, LANES), lambda p, i: (i, p))],
        out_shape=[jax.ShapeDtypeStruct((s, MEM_W), F32), jax.ShapeDtypeStruct((s, MEM_W), F32)],
        compiler_params=_ARB2,
    )(qkv, kv, kv)


def _mix_chunk(c, ysb_ref, yfx_ref, ym_ref):
    if c < SB_W // LANES:
        return ysb_ref[:, c * LANES:(c + 1) * LANES]
    c -= SB_W // LANES
    if c < FX_W // LANES:
        return yfx_ref[:, c * LANES:(c + 1) * LANES]
    c -= FX_W // LANES
    return ym_ref[:, c * LANES:(c + 1) * LANES]


def _outproj_fwd(ysb, yfx, ym, gf, onw, wout, x, name):
    s = x.shape[0]

    def body(ysb_ref, yfx_ref, ym_ref, g_ref, onw_ref, w_ref, x_ref, o_ref, yg_ref):
        bd = _head_block_diag()
        for c in range(MIX_W // LANES):
            sl = slice(c * LANES, (c + 1) * LANES)
            u = _mix_chunk(c, ysb_ref, yfx_ref, ym_ref)
            r = lax.rsqrt(_head_mean(u * u, bd) + EPS)
            g = g_ref[:, sl]
            yg_ref[:, sl] = (u * r * onw_ref[:, sl] * (g * _sigmoid(g))).astype(BF16)
        o_ref[...] = x_ref[...] + _dot(yg_ref[...], w_ref[...])

    return pl.pallas_call(
        body, name=name, grid=(s // TM,),
        in_specs=[pl.BlockSpec((TM, SB_W), lambda i: (i, 0)),
                  pl.BlockSpec((TM, FX_W), lambda i: (i, 0)),
                  pl.BlockSpec((TM, MEM_W), lambda i: (i, 0)),
                  pl.BlockSpec((TM, MIX_W), lambda i: (i, 0)),
                  pl.BlockSpec((1, MIX_W), lambda i: (0, 0)),
                  pl.BlockSpec((MIX_W, D_MODEL), lambda i: (0, 0)),
                  pl.BlockSpec((TM, D_MODEL), lambda i: (i, 0))],
        out_specs=pl.BlockSpec((TM, D_MODEL), lambda i: (i, 0)),
        out_shape=jax.ShapeDtypeStruct((s, D_MODEL), F32),
        scratch_shapes=[pltpu.VMEM((TM, MIX_W), BF16)],
        compiler_params=_ARB1,
    )(ysb, yfx, ym, gf, onw, wout, x)


def _final_fwd_bwd(x, fnw, target, name):
    s = x.shape[0]

    def body(x_ref, w_ref, t_ref, dx_ref, loss_ref, dw_ref):
        @pl.when(pl.program_id(0) == 0)
        def _():
            loss_ref[...] = jnp.zeros_like(loss_ref)
            dw_ref[...] = jnp.zeros_like(dw_ref)

        xv = x_ref[...]
        w = w_ref[...]
        r = lax.rsqrt(jnp.mean(xv * xv, axis=-1, keepdims=True) + EPS)
        xh = xv * r
        err = xh * w - t_ref[...]
        part = jnp.sum(jnp.sum(err * err, axis=1, keepdims=True), axis=0, keepdims=True)
        loss_ref[...] += part * (0.5 / D_MODEL)
        dy = err * (1.0 / D_MODEL)
        dw_ref[...] += jnp.sum(dy * xh, axis=0, keepdims=True)
        dxh = dy * w
        dx_ref[...] = r * (dxh - xh * jnp.mean(dxh * xh, axis=-1, keepdims=True))

    return pl.pallas_call(
        body, name=name, grid=(s // TM,),
        in_specs=[pl.BlockSpec((TM, D_MODEL), lambda i: (i, 0)),
                  pl.BlockSpec((1, D_MODEL), lambda i: (0, 0)),
                  pl.BlockSpec((TM, D_MODEL), lambda i: (i, 0))],
        out_specs=[pl.BlockSpec((TM, D_MODEL), lambda i: (i, 0)),
                   pl.BlockSpec((1, LANES), lambda i: (0, 0)),
                   pl.BlockSpec((1, D_MODEL), lambda i: (0, 0))],
        out_shape=[jax.ShapeDtypeStruct((s, D_MODEL), F32), jax.ShapeDtypeStruct((1, LANES), F32),
                   jax.ShapeDtypeStruct((1, D_MODEL), F32)],
        compiler_params=_ARB1,
    )(x, fnw, target)


def _outproj_bwd(dxo, wout, ysb, yfx, ym, gf, onw, name, rider=None):
    s = dxo.shape[0]

    def body(dx_ref, w_ref, ysb_ref, yfx_ref, ym_ref, g_ref, onw_ref,
             dysb_ref, dyfx_ref, dym_ref, dg_ref, dw_ref, donw_ref, yg_ref):
        @pl.when(pl.program_id(0) == 0)
        def _():
            dw_ref[...] = jnp.zeros_like(dw_ref)
            donw_ref[...] = jnp.zeros_like(donw_ref)

        dxb = dx_ref[...].astype(BF16)
        dyg = _dot_nt(dxb, w_ref[...])
        bd = _head_block_diag()
        for c in range(MIX_W // LANES):
            sl = slice(c * LANES, (c + 1) * LANES)
            u = _mix_chunk(c, ysb_ref, yfx_ref, ym_ref)
            r = lax.rsqrt(_head_mean(u * u, bd) + EPS)
            yn = u * r
            g = g_ref[:, sl]
            sg = _sigmoid(g)
            sil = g * sg
            onw = onw_ref[:, sl]
            e = dyg[:, sl]
            yg_ref[:, sl] = (yn * onw * sil).astype(BF16)
            donw_ref[:, sl] += jnp.sum(e * yn * sil, axis=0, keepdims=True)
            dg_ref[:, sl] = (e * yn * onw * (sg * (1.0 + g * (1.0 - sg)))).astype(BF16)
            dyn = e * onw * sil
            du = (r * (dyn - yn * _head_mean(dyn * yn, bd))).astype(BF16)
            if c < 4:
                dysb_ref[:, c * LANES:(c + 1) * LANES] = du
            elif c < 8:
                dyfx_ref[:, (c - 4) * LANES:(c - 3) * LANES] = du
            else:
                dym_ref[:, (c - 8) * LANES:(c - 7) * LANES] = du
        dw_ref[...] += _dot_tn(yg_ref[...], dxb)

    outs, lands = _ride(dict(
        body=body, name=name, grid=(s // TM,),
        in_specs=[pl.BlockSpec((TM, D_MODEL), lambda i: (i, 0)),
                  pl.BlockSpec((MIX_W, D_MODEL), lambda i: (0, 0)),
                  pl.BlockSpec((TM, SB_W), lambda i: (i, 0)),
                  pl.BlockSpec((TM, FX_W), lambda i: (i, 0)),
                  pl.BlockSpec((TM, MEM_W), lambda i: (i, 0)),
                  pl.BlockSpec((TM, MIX_W), lambda i: (i, 0)),
                  pl.BlockSpec((1, MIX_W), lambda i: (0, 0))],
        out_specs=[pl.BlockSpec((TM, SB_W), lambda i: (i, 0)),
                   pl.BlockSpec((TM, FX_W), lambda i: (i, 0)),
                   pl.BlockSpec((TM, MEM_W), lambda i: (i, 0)),
                   pl.BlockSpec((TM, MIX_W), lambda i: (i, 0)),
                   pl.BlockSpec((MIX_W, D_MODEL), lambda i: (0, 0)),
                   pl.BlockSpec((1, MIX_W), lambda i: (0, 0))],
        out_shape=[jax.ShapeDtypeStruct((s, SB_W), BF16), jax.ShapeDtypeStruct((s, FX_W), BF16),
                   jax.ShapeDtypeStruct((s, MEM_W), BF16), jax.ShapeDtypeStruct((s, MIX_W), BF16),
                   jax.ShapeDtypeStruct((MIX_W, D_MODEL), F32), jax.ShapeDtypeStruct((1, MIX_W), F32)],
        scratch_shapes=[pltpu.VMEM((TM, MIX_W), BF16)],
        compiler_params=_ARB1, operands=[dxo, wout, ysb, yfx, ym, gf, onw]), rider)
    return (*outs, lands)


def _row_dots(do, o, m0):
    prod = do.astype(F32) * o
    zero = jnp.zeros_like(prod)
    return (jnp.sum(jnp.where(m0, prod, zero), axis=1, keepdims=True),
            jnp.sum(jnp.where(m0, zero, prod), axis=1, keepdims=True))


def _sb_bwd(qkv, o, do, name, rider=None):
    s = qkv.shape[0]
    nq = s // T

    def body(q_ref, k_ref, v_ref, o_ref, do_ref, dq_ref, dk_ref, dv_ref,
             dqa_ref, dka_ref, dva_ref, rl_ref, rg_ref, dzs_ref, abs_ref):
        step_id = pl.program_id(1)

        @pl.when(step_id == 0)
        def _():
            dka_ref[...] = jnp.zeros_like(dka_ref)
            dva_ref[...] = jnp.zeros_like(dva_ref)

        m0 = _pair_masks()
        strict = _iota2((T, T), 0) > _iota2((T, T), 1)
        u2 = _stack2(jnp.where(strict, 1.0, 0.0).astype(BF16))
        hs = range(2)

        def query_tile(i, rows):
            qh = _split_pair(q_ref[rows, :] * jnp.asarray(Q_SCALE, BF16), m0)
            doh = _split_pair(do_ref[rows, :], m0)
            dsum = _row_dots(do_ref[rows, :], o_ref[rows, :], m0)
            dqa_ref[...] = jnp.zeros_like(dqa_ref)
            rl_ref[...] = jnp.zeros_like(rl_ref)
            rg_ref[...] = jnp.zeros_like(rg_ref)

            def flush(j):
                off = pl.multiple_of(j * T, T)
                k = k_ref[pl.ds(off, T), :]
                for h in hs:
                    dqa_ref[h] += _dot(dzs_ref[h], k)
                dka_ref[pl.ds(off, T), :] += (_dot_tn(dzs_ref[0], qh[0])
                                              + _dot_tn(dzs_ref[1], qh[1]))
                dva_ref[pl.ds(off, T), :] += (_dot_tn(abs_ref[0], doh[0])
                                              + _dot_tn(abs_ref[1], doh[1]))

            def tile(j, diag):
                off = pl.multiple_of(j * T, T)
                k = k_ref[pl.ds(off, T), :]
                v = v_ref[pl.ds(off, T), :]
                z = [_dot_nt(qh[h], k) for h in hs]
                da = [_dot_nt(doh[h], v) for h in hs]
                if not diag:
                    flush(j + 1)
                la = [jnp.minimum(z[h], 0.0) - jnp.log(1.0 + jnp.exp(-jnp.abs(z[h]))) for h in hs]
                lf = [la[h] - z[h] for h in hs]
                if diag:
                    lf = [jnp.where(strict, lf[h], 0.0) for h in hs]
                cin = [_cum2(lf[h], u2) for h in hs]
                a = [jnp.exp(la[h] + cin[h] + rl_ref[h]) for h in hs]
                if diag:
                    a = [jnp.where(strict, a[h], 0.0) for h in hs]
                ab = [a[h].astype(BF16) for h in hs]
                g = [ab[h].astype(F32) * da[h] for h in hs]
                gin = [_cum2(g[h], u2) for h in hs]
                dz = [g[h] - jnp.exp(la[h]) * ((dsum[h] - rg_ref[h]) - gin[h]) for h in hs]
                if diag:
                    dz = [jnp.where(strict, dz[h], 0.0) for h in hs]
                for h in hs:
                    rl_ref[h] += cin[h][:, 0:1] + lf[h][:, 0:1]
                    rg_ref[h] += gin[h][:, 0:1] + g[h][:, 0:1]
                    dzs_ref[h] = dz[h].astype(BF16)
                    abs_ref[h] = ab[h]

            tile(i, True)

            def more(state):
                j, top = state
                return jnp.logical_and(j >= 0, top > UNDERFLOW)

            def step(state):
                j, _ = state
                tile(j, False)
                return j - 1, _running_top(rl_ref)

            j_left, _ = lax.while_loop(more, step, (i - 1, _running_top(rl_ref)))
            flush(j_left + 1)
            dq_ref[rows, :] = (jnp.where(m0, dqa_ref[0], dqa_ref[1]) * Q_SCALE).astype(BF16)

        for n in range(QPS):
            query_tile(QPS * step_id + n, slice(n * T, (n + 1) * T))

        @pl.when(step_id == nq // QPS - 1)
        def _():
            dk_ref[...] = dka_ref[...].astype(BF16)
            dv_ref[...] = dva_ref[...].astype(BF16)

    nb = SB_W // LANES
    (dq, dk, dv), lands = _ride(dict(
        body=body, name=name, grid=(nb, nq // QPS),
        in_specs=[pl.BlockSpec((QPS * T, LANES), lambda p, i: (i, p)),
                  pl.BlockSpec((s, LANES), lambda p, i: (0, nb + p)),
                  pl.BlockSpec((s, LANES), lambda p, i: (0, 2 * nb + p)),
                  pl.BlockSpec((QPS * T, LANES), lambda p, i: (i, p)),
                  pl.BlockSpec((QPS * T, LANES), lambda p, i: (i, p))],
        out_specs=[pl.BlockSpec((QPS * T, LANES), lambda p, i: (i, p)),
                   pl.BlockSpec((s, LANES), lambda p, i: (0, p)),
                   pl.BlockSpec((s, LANES), lambda p, i: (0, p))],
        out_shape=[jax.ShapeDtypeStruct((s, SB_W), BF16)] * 3,
        scratch_shapes=[pltpu.VMEM((2, T, LANES), F32), pltpu.VMEM((s, LANES), F32),
                        pltpu.VMEM((s, LANES), F32), pltpu.VMEM((2, T, 1), F32),
                        pltpu.VMEM((2, T, 1), F32), pltpu.VMEM((2, T, T), BF16),
                        pltpu.VMEM((2, T, T), BF16)],
        compiler_params=_ARB2, operands=[qkv, qkv, qkv, o, do]), rider)
    return dq, dk, dv, lands


def _fox_bwd(qkv, fqb, frow, fbounds, o, lse, do, name, rider=None):
    s = qkv.shape[0]
    nq = s // T

    def body(q_ref, k_ref, v_ref, fq_ref, fr_ref, fb_ref, o_ref, lse_ref, do_ref,
             dq_ref, dk_ref, dv_ref, df_ref, dqa_ref, dka_ref, dva_ref, dfa_ref, dls_ref, pbs_ref):
        step_id = pl.program_id(1)

        @pl.when(step_id == 0)
        def _():
            dka_ref[...] = jnp.zeros_like(dka_ref)
            dva_ref[...] = jnp.zeros_like(dva_ref)
            dfa_ref[...] = jnp.zeros_like(dfa_ref)

        m0 = _pair_masks()
        causal = _iota2((T, T), 0) >= _iota2((T, T), 1)
        hs = range(2)

        def query_tile(i, rows):
            qh = _split_pair(q_ref[rows, :] * jnp.asarray(Q_SCALE, BF16), m0)
            doh = _split_pair(do_ref[rows, :], m0)
            dsum = _row_dots(do_ref[rows, :], o_ref[rows, :], m0)
            fq = fq_ref[rows, :]
            fqh = (fq[:, 0:1], fq[:, HEAD_DIM:HEAD_DIM + 1])
            lse = lse_ref[rows, :]
            lseh = (lse[:, 0:1], lse[:, HEAD_DIM:HEAD_DIM + 1])
            dqa_ref[...] = jnp.zeros_like(dqa_ref)

            def flush(j):
                off = pl.multiple_of(j * T, T)
                k = k_ref[pl.ds(off, T), :]
                for h in hs:
                    dqa_ref[h] += _dot(dls_ref[h], k)
                dka_ref[pl.ds(off, T), :] += (_dot_tn(dls_ref[0], qh[0])
                                              + _dot_tn(dls_ref[1], qh[1]))
                dva_ref[pl.ds(off, T), :] += (_dot_tn(pbs_ref[0], doh[0])
                                              + _dot_tn(pbs_ref[1], doh[1]))

            def tile(j, diag):
                off = pl.multiple_of(j * T, T)
                k = k_ref[pl.ds(off, T), :]
                v = v_ref[pl.ds(off, T), :]
                sc = [_dot_nt(qh[h], k) + fqh[h] - fr_ref[h:h + 1, pl.ds(off, T)] for h in hs]
                dp = [_dot_nt(doh[h], v) for h in hs]
                if not diag:
                    flush(j + 1)
                p = [jnp.exp(sc[h] - lseh[h]) for h in hs]
                if diag:
                    p = [jnp.where(causal, p[h], 0.0) for h in hs]
                dl = [p[h] * (dp[h] - dsum[h]) for h in hs]
                for h in hs:
                    dls_ref[h] = dl[h].astype(BF16)
                    pbs_ref[h] = p[h].astype(BF16)
                    dfa_ref[h:h + 1, pl.ds(off, T)] -= jnp.sum(dl[h], axis=0, keepdims=True)

            tile(i, True)
            j_left = _fox_tiles_left(i, pl.program_id(0), nq, fb_ref, tile)
            flush(j_left + 1)
            dq_ref[rows, :] = (jnp.where(m0, dqa_ref[0], dqa_ref[1]) * Q_SCALE).astype(BF16)

        for n in range(QPS):
            query_tile(QPS * step_id + n, slice(n * T, (n + 1) * T))

        @pl.when(step_id == nq // QPS - 1)
        def _():
            dk_ref[...] = dka_ref[...].astype(BF16)
            dv_ref[...] = dva_ref[...].astype(BF16)
            df_ref[...] = dfa_ref[...]

    nb = FX_W // LANES
    base = 3 * SB_W // LANES
    (dq, dk, dv, df), lands = _ride(dict(
        body=body, name=name, grid=(nb, nq // QPS),
        in_specs=[pl.BlockSpec((QPS * T, LANES), lambda p, i: (i, base + p)),
                  pl.BlockSpec((s, LANES), lambda p, i: (0, base + nb + p)),
                  pl.BlockSpec((s, LANES), lambda p, i: (0, base + 2 * nb + p)),
                  pl.BlockSpec((QPS * T, LANES), lambda p, i: (i, p)),
                  pl.BlockSpec((None, 2, s), lambda p, i: (p, 0, 0)),
                  pl.BlockSpec(memory_space=pltpu.SMEM),
                  pl.BlockSpec((QPS * T, LANES), lambda p, i: (i, p)),
                  pl.BlockSpec((QPS * T, LANES), lambda p, i: (i, p)),
                  pl.BlockSpec((QPS * T, LANES), lambda p, i: (i, p))],
        out_specs=[pl.BlockSpec((QPS * T, LANES), lambda p, i: (i, p)),
                   pl.BlockSpec((s, LANES), lambda p, i: (0, p)),
                   pl.BlockSpec((s, LANES), lambda p, i: (0, p)),
                   pl.BlockSpec((None, 2, s), lambda p, i: (p, 0, 0))],
        out_shape=[jax.ShapeDtypeStruct((s, FX_W), BF16)] * 3
        + [jax.ShapeDtypeStruct((nb, 2, s), F32)],
        scratch_shapes=[pltpu.VMEM((2, T, LANES), F32), pltpu.VMEM((s, LANES), F32),
                        pltpu.VMEM((s, LANES), F32), pltpu.VMEM((2, s), F32),
                        pltpu.VMEM((2, T, T), BF16), pltpu.VMEM((2, T, T), BF16)],
        compiler_params=_ARB2, operands=[qkv, qkv, qkv, fqb, frow, fbounds, o, lse, do]), rider)
    return dq, dk, dv, df, lands


def _fox_prep_bwd(dfrow, gf, bpad, name):
    s = gf.shape[0]

    def body(df_ref, fl_ref, b_ref, dfl_ref, db_ref):
        tri = jnp.where(_iota2((T, T), 0) <= _iota2((T, T), 1), 1.0, 0.0).astype(BF16)
        carry = jnp.zeros((1, LANES), F32)
        db = jnp.zeros((1, LANES), F32)
        fill = jnp.zeros((LANES - FOX_HEADS, T), F32)
        for blk in reversed(range(s // T)):
            rows = slice(blk * T, (blk + 1) * T)
            c = _tri3(tri, jnp.concatenate([df_ref[:, rows], fill], axis=0), _dot_nt) + carry
            carry = c[0:1, :]
            dfl = c / (1.0 + jnp.exp(fl_ref[rows, :] + b_ref[...]))
            dfl_ref[rows, :] = dfl.astype(BF16)
            db = db + jnp.sum(dfl, axis=0, keepdims=True)
        db_ref[...] = db

    return pl.pallas_call(
        body, name=name, grid=(1,),
        in_specs=[pl.BlockSpec((FOX_HEADS, s), lambda i: (0, 0)),
                  pl.BlockSpec((s, LANES), lambda i: (0, MIX_W // LANES)),
                  pl.BlockSpec((1, LANES), lambda i: (0, 0))],
        out_specs=[pl.BlockSpec((s, LANES), lambda i: (0, 0)),
                   pl.BlockSpec((1, LANES), lambda i: (0, 0))],
        out_shape=[jax.ShapeDtypeStruct((s, LANES), BF16), jax.ShapeDtypeStruct((1, LANES), F32)],
        compiler_params=_ARB1,
    )(dfrow, gf, bpad)


def _mem_bwd(qkv, kv, o, lse, do, name):
    s = qkv.shape[0]
    n = kv.shape[0]

    def body(q_ref, k_ref, v_ref, o_ref, lse_ref, do_ref, dq_ref, dk_ref, dv_ref):
        @pl.when(pl.program_id(1) == 0)
        def _():
            dk_ref[...] = jnp.zeros_like(dk_ref)
            dv_ref[...] = jnp.zeros_like(dv_ref)

        m0 = _pair_masks()
        qh = _split_pair(q_ref[...] * jnp.asarray(Q_SCALE, BF16), m0)
        doh = _split_pair(do_ref[...], m0)
        dsum = _row_dots(do_ref[...], o_ref[...], m0)
        lse = lse_ref[...]
        lseh = (lse[:, 0:1], lse[:, HEAD_DIM:HEAD_DIM + 1])
        k = k_ref[...]
        v = v_ref[...]
        dqs = []
        for h in range(2):
            p = jnp.exp(_dot_nt(qh[h], k) - lseh[h])
            dl = p * (_dot_nt(doh[h], v) - dsum[h])
            dlb = dl.astype(BF16)
            dqs.append(_dot(dlb, k))
            dk_ref[...] += _dot_tn(dlb, qh[h])
            dv_ref[...] += _dot_tn(p.astype(BF16), doh[h])
        dq_ref[...] = (jnp.where(m0, dqs[0], dqs[1]) * Q_SCALE).astype(BF16)

    nb = MEM_W // LANES
    base = (3 * SB_W + 3 * FX_W) // LANES
    return pl.pallas_call(
        body, name=name, grid=(nb, s // TQM),
        in_specs=[pl.BlockSpec((TQM, LANES), lambda p, i: (i, base + p)),
                  pl.BlockSpec((n, LANES), lambda p, i: (0, p)),
                  pl.BlockSpec((n, LANES), lambda p, i: (0, nb + p)),
                  pl.BlockSpec((TQM, LANES), lambda p, i: (i, p)),
                  pl.BlockSpec((TQM, LANES), lambda p, i: (i, p)),
                  pl.BlockSpec((TQM, LANES), lambda p, i: (i, p))],
        out_specs=[pl.BlockSpec((TQM, LANES), lambda p, i: (i, p)),
                   pl.BlockSpec((n, LANES), lambda p, i: (0, p)),
                   pl.BlockSpec((n, LANES), lambda p, i: (0, p))],
        out_shape=[jax.ShapeDtypeStruct((s, MEM_W), BF16), jax.ShapeDtypeStruct((n, MEM_W), F32),
                   jax.ShapeDtypeStruct((n, MEM_W), F32)],
        compiler_params=_ARB2,
    )(qkv, kv, kv, o, lse, do)


def _memkv_bwd(mem, mnw, wkv, dk, dv, name):
    n = mem.shape[0]

    def body(mem_ref, mnw_ref, w_ref, dk_ref, dv_ref, dw_ref, dmnw_ref):
        mv = mem_ref[...]
        r = lax.rsqrt(jnp.mean(mv * mv, axis=-1, keepdims=True) + EPS)
        mh = mv * r
        hm = (mh * mnw_ref[...]).astype(BF16)
        dkv = jnp.concatenate([dk_ref[...], dv_ref[...]], axis=1).astype(BF16)
        dw_ref[...] = _dot_tn(hm, dkv)
        dhm = _dot_nt(dkv, w_ref[...])
        dmnw_ref[...] = jnp.sum(dhm * mh, axis=0, keepdims=True)

    return pl.pallas_call(
        body, name=name, grid=(1,),
        in_specs=[pl.BlockSpec((n, D_MODEL), lambda i: (0, 0)),
                  pl.BlockSpec((1, D_MODEL), lambda i: (0, 0)),
                  pl.BlockSpec((D_MODEL, 2 * MEM_W), lambda i: (0, 0)),
                  pl.BlockSpec((n, MEM_W), lambda i: (0, 0)),
                  pl.BlockSpec((n, MEM_W), lambda i: (0, 0))],
        out_specs=[pl.BlockSpec((D_MODEL, 2 * MEM_W), lambda i: (0, 0)),
                   pl.BlockSpec((1, D_MODEL), lambda i: (0, 0))],
        out_shape=[jax.ShapeDtypeStruct((D_MODEL, 2 * MEM_W), F32),
                   jax.ShapeDtypeStruct((1, D_MODEL), F32)],
        compiler_params=_ARB1,
    )(mem, mnw, wkv, dk, dv)


def _inproj_bwd_dx(pieces, w_r, x, nw, dxo, name, rider=None):
    s = x.shape[0]
    n = len(pieces)
    widths = [p.shape[1] for p in pieces]

    def body(*refs):
        piece_refs = refs[:n]
        w_ref, x_ref, nw_ref, dxo_ref, dx_ref, h_ref, dnw_ref, dp_ref = refs[n:]

        @pl.when(pl.program_id(0) == 0)
        def _():
            dnw_ref[...] = jnp.zeros_like(dnw_ref)

        col = 0
        for r, wd in zip(piece_refs, widths):
            dp_ref[:, col:col + wd] = r[...]
            col += wd
        dp_ref[:, col:] = jnp.zeros((TM, WR_W - col), BF16)
        dh = _dot(dp_ref[...], w_ref[...])
        xv = x_ref[...]
        nw = nw_ref[...]
        r = lax.rsqrt(jnp.mean(xv * xv, axis=-1, keepdims=True) + EPS)
        xh = xv * r
        h_ref[...] = (xh * nw).astype(BF16)
        dnw_ref[...] += jnp.sum(dh * xh, axis=0, keepdims=True)
        dxh = dh * nw
        dx_ref[...] = r * (dxh - xh * jnp.mean(dxh * xh, axis=-1, keepdims=True)) + dxo_ref[...]

    (dx, h, dnw, dproj), lands = _ride(dict(
        body=body, name=name, grid=(s // TM,),
        in_specs=[pl.BlockSpec((TM, wd), lambda i: (i, 0)) for wd in widths]
        + [pl.BlockSpec((WR_W, D_MODEL), lambda i: (0, 0)),
           pl.BlockSpec((TM, D_MODEL), lambda i: (i, 0)),
           pl.BlockSpec((1, D_MODEL), lambda i: (0, 0)),
           pl.BlockSpec((TM, D_MODEL), lambda i: (i, 0))],
        out_specs=[pl.BlockSpec((None, TM, D_MODEL), lambda i: (0, i, 0)),
                   pl.BlockSpec((TM, D_MODEL), lambda i: (i, 0)),
                   pl.BlockSpec((1, D_MODEL), lambda i: (0, 0)),
                   pl.BlockSpec((TM, WR_W), lambda i: (i, 0))],
        out_shape=[jax.ShapeDtypeStruct((1, s, D_MODEL), F32), jax.ShapeDtypeStruct((s, D_MODEL), BF16),
                   jax.ShapeDtypeStruct((1, D_MODEL), F32), jax.ShapeDtypeStruct((s, WR_W), BF16)],
        scratch_shapes=[], compiler_params=_ARB1, operands=[*pieces, w_r, x, nw, dxo]), rider)
    return dx, h, dnw, dproj, lands


def _inproj_bwd_dw(h, dproj, name):
    s = dproj.shape[0]
    tn = 256

    def body(h_ref, dp_ref, dw_ref):
        dw_ref[...] = _dot_tn(dp_ref[...], h_ref[...]).astype(BF16)

    return pl.pallas_call(
        body, name=name, grid=(WR_W // tn,),
        in_specs=[pl.BlockSpec((s, D_MODEL), lambda j: (0, 0)),
                  pl.BlockSpec((s, tn), lambda j: (0, j))],
        out_specs=pl.BlockSpec((tn, D_MODEL), lambda j: (j, 0)),
        out_shape=jax.ShapeDtypeStruct((WR_W, D_MODEL), BF16),
        compiler_params=_ARB1,
    )(h, dproj)


def _rearrange_w_in(wt):
    pad = jnp.zeros((FL_PAD - FOX_HEADS,) + wt.shape[1:], wt.dtype)
    return jnp.concatenate([wt[:3072], wt[3080:3336], wt[3336:IN_W], wt[3072:3080], pad], axis=0)


_W_IN_SEGMENTS = [(0, 3072, 0), (3072, 3080, QKV_W + MIX_W), (3080, 3336, 3072), (3336, IN_W, QKV_W)]
ASSEMBLE_COLS = 256
ASSEMBLE_ROWS = 128


def _shard_pieces(k):
    lo, hi = k * SHARD_W, (k + 1) * SHARD_W
    return [(max(lo, a) - lo, at + max(lo, a) - a, min(hi, b) - max(lo, a))
            for a, b, at in _W_IN_SEGMENTS if max(lo, a) < min(hi, b)]


def _assemble_w_in(v, to_blocks, name):
    used = QKV_W + MIX_W + FOX_HEADS

    def body(v_ref, o_ref):
        for k in range(N_DEV):
            for at, w_at, n in _shard_pieces(k):
                for r in range(0, n, ASSEMBLE_ROWS):
                    m = min(ASSEMBLE_ROWS, n - r)
                    if to_blocks:
                        o_ref[k, at + r:at + r + m, :] = v_ref[w_at + r:w_at + r + m, :]
                    else:
                        o_ref[w_at + r:w_at + r + m, :] = v_ref[k, at + r:at + r + m, :]
        if not to_blocks:
            o_ref[used:, :] = jnp.zeros((WR_W - used, ASSEMBLE_COLS), BF16)

    blocks = pl.BlockSpec((N_DEV, SHARD_W, ASSEMBLE_COLS), lambda j: (0, 0, j))
    rows = pl.BlockSpec((WR_W, ASSEMBLE_COLS), lambda j: (0, j))
    out_shape = (N_DEV, SHARD_W, D_MODEL) if to_blocks else (WR_W, D_MODEL)
    return pl.pallas_call(
        body, name=name, grid=(D_MODEL // ASSEMBLE_COLS,),
        in_specs=[rows if to_blocks else blocks], out_specs=blocks if to_blocks else rows,
        out_shape=jax.ShapeDtypeStruct(out_shape, BF16), compiler_params=_ARB1)(v)


def _restore_w_in(g):
    gate0 = QKV_W
    fl0 = QKV_W + MIX_W
    return jnp.concatenate(
        [g[:3072], g[fl0:fl0 + FOX_HEADS], g[3072:QKV_W], g[gate0:fl0]], axis=0)


def _pad_lanes(v, width=LANES):
    return jnp.pad(v, (0, width - v.shape[0])).reshape(1, width)


def _layer_fwd(xs, mem, nw, w_r, b_forget, mnw, late, onw, l, travel=None):
    s = xs.shape[0]
    bpad = _pad_lanes(b_forget)
    travel = _Travel(travel)
    qkv, gf, _ = travel.ride(2, _inproj_fwd, xs, nw, w_r, f"inproj_fwd_{l}")
    fqb, frow, fbounds = _fox_prep_fwd(gf, qkv, bpad, f"fox_prep_fwd_{l}")
    frow = frow.reshape(FOX_HEADS // 2, 2, s)
    ysb, _ = travel.ride(0, _sb_fwd, qkv, f"sb_fwd_{l}")
    yfx, lse_fx, _ = travel.ride(1, _fox_fwd, qkv, fqb, frow, fbounds, f"fox_fwd_{l}")
    wkv, wout = late(travel.lands)
    kv = _memkv_fwd(mem, mnw, wkv, f"memkv_fwd_{l}")
    ym, lse_m = _mem_fwd(qkv, kv, f"mem_fwd_{l}")
    xn = _outproj_fwd(ysb, yfx, ym, gf, onw, wout, xs, f"outproj_fwd_{l}")
    saved = (xs, nw, mnw, onw, bpad, qkv, gf, fqb, frow, fbounds, ysb, yfx, lse_fx, kv, ym, lse_m)
    return xn, saved, travel.lands, (wkv, wout)


class _Travel:
    def __init__(self, plan):
        self.plan = plan
        self.lands = None if plan is None else _new_lands(plan[0], plan[1])

    def ride(self, n, fn, *args):
        if self.plan is None or n >= len(self.plan[2]) or self.plan[2][n] is None:
            return fn(*args)
        srcs, scatter, legs = self.plan
        idx, rows = legs[n]
        out = fn(*args, rider=_Rider([srcs[a] for a in idx], [self.lands[a] for a in idx],
                                     scatter, rows, relay=not scatter))
        for a, land in zip(idx, out[-1]):
            self.lands[a] = land
        return out


def _layer_bwd(dx, saved, mem, w_r, wkv, wout, l, travel=None, own_w_out=None):
    xs, nw, mnw, onw, bpad, qkv, gf, fqb, frow, fbounds, ysb, yfx, lse_fx, kv, ym, lse_m = saved
    s = xs.shape[0]
    travel = _Travel(travel)
    dysb, dyfx, dym, dgate, dwout, donw, _ = travel.ride(
        3, _outproj_bwd, dx, wout, ysb, yfx, ym, gf, onw, f"outproj_bwd_{l}")
    if own_w_out is not None:
        travel.plan[0][own_w_out[0]] = own_w_out[1](dwout)
    sdq, sdk, sdv, _ = travel.ride(0, _sb_bwd, qkv, ysb, dysb, f"sb_bwd_{l}")
    fdq, fdk, fdv, dfrow, _ = travel.ride(1, _fox_bwd, qkv, fqb, frow, fbounds, yfx, lse_fx, dyfx,
                                          f"fox_bwd_{l}")
    dfl, db = _fox_prep_bwd(dfrow.reshape(FOX_HEADS, s), gf, bpad, f"fox_prep_bwd_{l}")
    dmq, dmk, dmv = _mem_bwd(qkv, kv, ym, lse_m, dym, f"mem_bwd_{l}")
    dwkv, dmnw = _memkv_bwd(mem, mnw, wkv, dmk, dmv, f"memkv_bwd_{l}")
    dx, ht, dnw, dproj, _ = travel.ride(2, _inproj_bwd_dx,
                                        [sdq, sdk, sdv, fdq, fdk, fdv, dmq, dgate, dfl],
                                        w_r, xs, nw, dx, f"inproj_bwd_dx_{l}")
    dwr = _inproj_bwd_dw(ht, dproj, f"inproj_bwd_dw_{l}")
    grads = dict(norm_w=dnw[0], w_r=dwr, b_forget=db[0, :FOX_HEADS], mem_norm_w=dmnw[0],
                 w_mem_kv=dwkv, out_norm_w=donw[0], w_out=dwout)
    return dx, grads, travel.lands


_ANY = pl.BlockSpec(memory_space=pl.ANY)


def _my_place():
    return lax.axis_index("x"), lax.axis_index("y"), lax.axis_index("c")


def _flip(v, bit):
    return 1 - v if bit else v


def _block_index(px, py, pc):
    return 4 * px + 2 * py + pc


def _all_gather_weights(shards, name):
    n = len(shards)

    def body(*refs):
        ins, outs = refs[:n], refs[n:2 * n]
        send_sems, recv_sems, local_sems = refs[2 * n:]
        x, y, c = _my_place()
        me = (x, y, c)
        sibling = (x, y, 1 - c)
        chips = [(1 - x, y), (x, 1 - y), (1 - x, 1 - y)]

        def copy(a, k, block, to, src=None):
            dst = outs[a].at[_block_index(*block)]
            return pltpu.make_async_remote_copy(
                src_ref=dst if src is None else src, dst_ref=dst,
                send_sem=send_sems.at[a, k], recv_sem=recv_sems.at[a, k],
                device_id=to, device_id_type=pl.DeviceIdType.MESH)

        mine = [pltpu.make_async_copy(ins[a], outs[a].at[_block_index(*me)], local_sems.at[a])
                for a in range(n)]
        for cp in mine:
            cp.start()
        first = []
        for a in range(n):
            first.append(copy(a, 0, me, sibling, src=ins[a]))
            first += [copy(a, 1 + j, me, (*chip, c), src=ins[a]) for j, chip in enumerate(chips)]
        for cp in first:
            cp.start()
        passed = []
        for j, chip in enumerate(chips):
            for a in range(n):
                copy(a, 1 + j, (*chip, c), me).wait_recv()
                fwd = copy(a, 4 + j, (*chip, c), sibling)
                fwd.start()
                passed.append(fwd)
        for a in range(n):
            copy(a, 0, sibling, me).wait_recv()
            for j, chip in enumerate(chips):
                copy(a, 4 + j, (*chip, 1 - c), me).wait_recv()
        for cp in first + passed:
            cp.wait_send()
        for cp in mine:
            cp.wait()

    return pl.pallas_call(
        body, name=name,
        in_specs=[_ANY] * n, out_specs=[_ANY] * n,
        out_shape=[jax.ShapeDtypeStruct((N_DEV,) + v.shape, v.dtype) for v in shards],
        scratch_shapes=[pltpu.SemaphoreType.DMA((n, 7)), pltpu.SemaphoreType.DMA((n, 7)),
                        pltpu.SemaphoreType.DMA((n,))],
    )(*shards)


def _exchange_blocks(blocked, name):
    n = len(blocked)

    def body(*refs):
        ins, outs = refs[:n], refs[n:2 * n]
        send_sems, recv_sems, local_sems = refs[2 * n:]
        x, y, c = _my_place()
        mine_idx = _block_index(x, y, c)
        local = [pltpu.make_async_copy(ins[a].at[mine_idx], outs[a].at[mine_idx], local_sems.at[a])
                 for a in range(n)]
        for cp in local:
            cp.start()
        sends, arrivals = [], []
        for r in range(1, N_DEV):
            peer = (_flip(x, r & 4), _flip(y, r & 2), _flip(c, r & 1))
            peer_idx = _block_index(*peer)
            for a in range(n):
                sems = dict(send_sem=send_sems.at[a, r - 1], recv_sem=recv_sems.at[a, r - 1],
                            device_id=peer, device_id_type=pl.DeviceIdType.MESH)
                sends.append(pltpu.make_async_remote_copy(
                    src_ref=ins[a].at[peer_idx], dst_ref=outs[a].at[mine_idx], **sems))
                arrivals.append(pltpu.make_async_remote_copy(
                    src_ref=ins[a].at[peer_idx], dst_ref=outs[a].at[peer_idx], **sems))
        for cp in sends:
            cp.start()
        for cp in arrivals:
            cp.wait_recv()
        for cp in sends:
            cp.wait_send()
        for cp in local:
            cp.wait()

    return pl.pallas_call(
        body, name=name,
        in_specs=[_ANY] * n, out_specs=[_ANY] * n,
        out_shape=[jax.ShapeDtypeStruct(v.shape, v.dtype) for v in blocked],
        scratch_shapes=[pltpu.SemaphoreType.DMA((n, 7)), pltpu.SemaphoreType.DMA((n, 7)),
                        pltpu.SemaphoreType.DMA((n,))],
    )(*blocked)


N_CHIP = N_DEV // 2


def _pair_swap(blocked, name):
    n = len(blocked)

    def body(*refs):
        ins, outs = refs[:n], refs[n:2 * n]
        send_sems, recv_sems = refs[2 * n:]
        x, y, c = _my_place()
        copies = [pltpu.make_async_remote_copy(
            src_ref=ins[a].at[j, 1 - c], dst_ref=outs[a].at[j],
            send_sem=send_sems.at[N_CHIP * a + j], recv_sem=recv_sems.at[N_CHIP * a + j],
            device_id=(x, y, 1 - c), device_id_type=pl.DeviceIdType.MESH)
            for a in range(n) for j in range(N_CHIP)]
        for cp in copies:
            cp.start()
        for cp in copies:
            cp.wait_recv()
        for cp in copies:
            cp.wait_send()

    return pl.pallas_call(
        body, name=name,
        in_specs=[_ANY] * n, out_specs=[_ANY] * n,
        out_shape=[jax.ShapeDtypeStruct((N_CHIP,) + v.shape[2:], v.dtype) for v in blocked],
        scratch_shapes=[pltpu.SemaphoreType.DMA((N_CHIP * n,)),
                        pltpu.SemaphoreType.DMA((N_CHIP * n,))],
    )(*blocked)


def _pair_add(by_core, theirs, core, name):
    n = len(by_core)

    def body(core_ref, *refs):
        for a in range(n):
            mine, got, out = refs[a], refs[n + a], refs[2 * n + a]
            out[...] = (mine[...].astype(F32) + got[...].astype(F32)).astype(BF16)

    def own(v):
        return pl.BlockSpec((None, None) + v.shape[2:], lambda j, core_ref: (j, core_ref[0], 0, 0))

    def block(v):
        return pl.BlockSpec((None,) + v.shape[1:], lambda j, core_ref: (j, 0, 0))

    return list(pl.pallas_call(
        body, name=name,
        grid_spec=pltpu.PrefetchScalarGridSpec(
            num_scalar_prefetch=1, grid=(N_CHIP,),
            in_specs=[own(v) for v in by_core] + [block(v) for v in theirs],
            out_specs=[block(v) for v in theirs]),
        out_shape=[jax.ShapeDtypeStruct(v.shape, BF16) for v in theirs],
        compiler_params=_ARB1,
    )(core, *by_core, *theirs))


def _chip_exchange(by_chip, to_all, name):
    n, m = len(by_chip), len(to_all)

    def body(*refs):
        ins, alls = refs[:n], refs[n:n + m]
        outs, all_outs = refs[n + m:2 * n + m], refs[2 * n + m:2 * (n + m)]
        send_sems, recv_sems, local_sems = refs[2 * (n + m):]
        x, y, c = _my_place()
        my_chip, mine_idx = 2 * x + y, _block_index(x, y, c)
        local = [pltpu.make_async_copy(ins[a].at[my_chip], outs[a].at[my_chip], local_sems.at[a])
                 for a in range(n)]
        local += [pltpu.make_async_copy(alls[b].at[mine_idx], all_outs[b].at[mine_idx],
                                        local_sems.at[n + b]) for b in range(m)]
        for cp in local:
            cp.start()
        sends, arrivals = [], []
        k = 0
        for r in range(1, N_DEV):
            peer = (_flip(x, r & 4), _flip(y, r & 2), _flip(c, r & 1))
            peer_chip, peer_idx = 2 * peer[0] + peer[1], _block_index(*peer)
            pairs = [(alls[b].at[mine_idx], all_outs[b].at[mine_idx], all_outs[b].at[peer_idx])
                     for b in range(m)]
            if not r & 1:
                pairs += [(ins[a].at[peer_chip], outs[a].at[my_chip], outs[a].at[peer_chip])
                          for a in range(n)]
            for src, there, here in pairs:
                sems = dict(send_sem=send_sems.at[k], recv_sem=recv_sems.at[k], device_id=peer,
                            device_id_type=pl.DeviceIdType.MESH)
                sends.append(pltpu.make_async_remote_copy(src_ref=src, dst_ref=there, **sems))
                arrivals.append(pltpu.make_async_remote_copy(src_ref=src, dst_ref=here, **sems))
                k += 1
        for cp in sends:
            cp.start()
        for cp in arrivals:
            cp.wait_recv()
        for cp in sends:
            cp.wait_send()
        for cp in local:
            cp.wait()

    n_copies = 7 * m + 3 * n
    return pl.pallas_call(
        body, name=name,
        in_specs=[_ANY] * (n + m), out_specs=[_ANY] * (n + m),
        out_shape=[jax.ShapeDtypeStruct(v.shape, v.dtype) for v in by_chip + to_all],
        scratch_shapes=[pltpu.SemaphoreType.DMA((n_copies,)), pltpu.SemaphoreType.DMA((n_copies,)),
                        pltpu.SemaphoreType.DMA((n + m,))],
    )(*by_chip, *to_all)


class _Rider(NamedTuple):
    srcs: list
    lands: list
    scatter: bool
    part: list
    relay: bool = False


def _window(ref, part):
    if part is None:
        return ref
    dim, start, size = part
    return ref.at[(slice(None),) * dim + (pl.ds(start, size),)]


def _relay_copies(srcs, lands, send_sems, recv_sems, rider):
    x, y, c = _my_place()
    me, sibling = (x, y, c), (x, y, 1 - c)
    chips = [(1 - x, y), (x, 1 - y), (1 - x, 1 - y)]
    first, from_chips, passed, last = [], [], [], []
    for a in range(len(srcs)):
        def copy(k, block, to, src=None, a=a):
            dst = _window(lands[a].at[_block_index(*block)], rider.part[a])
            return pltpu.make_async_remote_copy(
                src_ref=dst if src is None else src, dst_ref=dst,
                send_sem=send_sems.at[7 * a + k], recv_sem=recv_sems.at[7 * a + k],
                device_id=to, device_id_type=pl.DeviceIdType.MESH)

        mine = _window(srcs[a], rider.part[a])
        first.append(copy(0, me, sibling, src=mine))
        first += [copy(1 + j, me, (*chip, c), src=mine) for j, chip in enumerate(chips)]
        from_chips += [copy(1 + j, (*chip, c), me) for j, chip in enumerate(chips)]
        passed += [copy(4 + j, (*chip, c), sibling) for j, chip in enumerate(chips)]
        last.append(copy(0, sibling, me))
        last += [copy(4 + j, (*chip, 1 - c), me) for j, chip in enumerate(chips)]
    return first, from_chips, passed, last


def _new_lands(srcs, scatter):
    return [lax.empty(v.shape if scatter else (N_DEV,) + v.shape, v.dtype) for v in srcs]


def _rider_copies(srcs, lands, send_sems, recv_sems, rider):
    x, y, c = _my_place()
    mine_idx = _block_index(x, y, c)

    def window(ref, a):
        return _window(ref, rider.part[a])

    sends, arrivals = [], []
    for r in range(1, N_DEV):
        peer = (_flip(x, r & 4), _flip(y, r & 2), _flip(c, r & 1))
        peer_idx = _block_index(*peer)
        for a in range(len(srcs)):
            src = window(srcs[a].at[peer_idx] if rider.scatter else srcs[a], a)
            k = 7 * a + r - 1
            sems = dict(send_sem=send_sems.at[k], recv_sem=recv_sems.at[k],
                        device_id=peer, device_id_type=pl.DeviceIdType.MESH)
            sends.append(pltpu.make_async_remote_copy(
                src_ref=src, dst_ref=window(lands[a].at[mine_idx], a), **sems))
            arrivals.append(pltpu.make_async_remote_copy(
                src_ref=src, dst_ref=window(lands[a].at[peer_idx], a), **sems))
    return sends, arrivals


def _ride(call, rider):
    call = dict(call)
    body, grid = call.pop("body"), call["grid"]
    operands = call.pop("operands")
    if rider is None:
        return list(pl.pallas_call(body, **call)(*operands)), None
    n_in, n_out = len(call["in_specs"]), len(call["out_specs"])
    n_scratch = len(call["scratch_shapes"])
    m = len(rider.srcs)

    def riding(*refs):
        main_in, srcs, lands = refs[:n_in], refs[n_in:n_in + m], refs[n_in + m:n_in + 2 * m]
        main_out = refs[n_in + 2 * m:n_in + 2 * m + n_out]
        lands_out = refs[n_in + 2 * m + n_out:n_in + 3 * m + n_out]
        rest = refs[n_in + 3 * m + n_out:]
        send_sems, recv_sems, own_sems = rest[n_scratch:]
        me = _block_index(*_my_place())
        own = [pltpu.make_async_copy(
            _window(srcs[a].at[me] if rider.scatter else srcs[a], rider.part[a]),
            _window(lands_out[a].at[me], rider.part[a]), own_sems.at[a]) for a in range(m)]
        at = [pl.program_id(d) for d in range(len(grid))]
        first = functools.reduce(jnp.logical_and, [p == 0 for p in at])
        last = functools.reduce(jnp.logical_and, [p == g - 1 for p, g in zip(at, grid)])
        if rider.relay:
            sends, from_chips, passed, arrivals = _relay_copies(
                srcs, lands, send_sems, recv_sems, rider)
            step, steps = 0, 1
            for p, g in zip(at, grid):
                step, steps = step * g + p, steps * g
            assert steps >= 2, "a relayed gather needs a later grid step to pass blocks on"

            @pl.when(step == (3 * steps) // 4)
            def _():
                for cp in from_chips:
                    cp.wait_recv()
                for cp in passed:
                    cp.start()
        else:
            sends, arrivals = _rider_copies(srcs, lands, send_sems, recv_sems, rider)
            passed = []

        @pl.when(first)
        def _():
            for cp in sends + own:
                cp.start()

        body(*main_in, *main_out, *rest[:n_scratch])

        @pl.when(last)
        def _():
            for cp in arrivals:
                cp.wait_recv()
            for cp in sends + passed:
                cp.wait_send()
            for cp in own:
                cp.wait()

    call["in_specs"] = list(call["in_specs"]) + [_ANY] * (2 * m)
    call["out_specs"] = list(call["out_specs"]) + [_ANY] * m
    call["out_shape"] = list(call["out_shape"]) + [
        jax.ShapeDtypeStruct(v.shape, v.dtype) for v in rider.lands]
    call["scratch_shapes"] = list(call["scratch_shapes"]) + [
        pltpu.SemaphoreType.DMA((7 * m,)), pltpu.SemaphoreType.DMA((7 * m,)),
        pltpu.SemaphoreType.DMA((m,))]
    call["input_output_aliases"] = {n_in + m + a: n_out + a for a in range(m)}
    outs = pl.pallas_call(riding, **call)(*operands, *rider.srcs, *rider.lands)
    return list(outs[:n_out]), list(outs[n_out:])


def _sum_parts(p_ref):
    g = p_ref[0].astype(F32)
    for k in range(1, p_ref.shape[0]):
        g = g + p_ref[k].astype(F32)
    return g


def _adamw(g, w, m, v):
    c1 = 1.0 / (1.0 - ADAM_B1 ** ADAM_STEP)
    c2 = 1.0 / (1.0 - ADAM_B2 ** ADAM_STEP)
    nm = ADAM_B1 * m + (1.0 - ADAM_B1) * g
    nv = ADAM_B2 * v + (1.0 - ADAM_B2) * (g * g)
    return nm, nv, -ADAM_LR * ((nm * c1) / (jnp.sqrt(nv * c2) + ADAM_EPS) + ADAM_WD * w)


def _adamw_w_in(parts, w, m, v, name):
    ncol_blk, depth, nfeat = w.shape
    cols = 256

    def body(*refs):
        p_refs = refs[:depth]
        w_ref, m_ref, v_ref, g_ref, d_ref, nm_ref, nv_ref = refs[depth:]
        g = jnp.stack([_sum_parts(p_refs[l]) for l in range(depth)], axis=1)
        nm, nv, d = _adamw(g, w_ref[...], m_ref[...], v_ref[...])
        g_ref[...] = g
        nm_ref[...] = nm
        nv_ref[...] = nv
        d_ref[...] = d

    blk = pl.BlockSpec((ncol_blk, depth, cols), lambda j: (0, 0, j))
    return pl.pallas_call(
        body, name=name, grid=(nfeat // cols,),
        in_specs=[pl.BlockSpec((p.shape[0], ncol_blk, cols), lambda j: (0, 0, j)) for p in parts]
        + [blk] * 3,
        out_specs=[blk] * 4,
        out_shape=[jax.ShapeDtypeStruct(w.shape, F32)] * 4,
        compiler_params=_ARB1,
    )(*parts, w, m, v)


def _adamw_sum(parts, w, m, v, tile, name):
    depth, nrow, ncol = w.shape
    rows, cols = tile

    def body(*refs):
        p_refs = refs[:depth]
        w_ref, m_ref, v_ref, g_ref, d_ref, nm_ref, nv_ref = refs[depth:]
        layer = pl.program_id(0)
        for l in range(depth):
            @pl.when(layer == l)
            def _(p_ref=p_refs[l]):
                g = _sum_parts(p_ref)
                nm, nv, d = _adamw(g, w_ref[...], m_ref[...], v_ref[...])
                g_ref[...] = g
                nm_ref[...] = nm
                nv_ref[...] = nv
                d_ref[...] = d

    def part_spec(l):
        return pl.BlockSpec((parts[l].shape[0], rows, cols), lambda q, i, j: (
            0, jnp.where(q == l, i, 0), jnp.where(q == l, j, 0)))

    blk = pl.BlockSpec((None, rows, cols), lambda q, i, j: (q, i, j))
    return pl.pallas_call(
        body, name=name, grid=(depth, nrow // rows, ncol // cols),
        in_specs=[part_spec(l) for l in range(depth)] + [blk, blk, blk],
        out_specs=[blk] * 4,
        out_shape=[jax.ShapeDtypeStruct(w.shape, F32)] * 4,
        compiler_params=pltpu.CompilerParams(
            dimension_semantics=("arbitrary", "arbitrary", "arbitrary")),
    )(*parts, w, m, v)


def _adamw_small(parts, ws, ms, vs, name):
    n = len(parts)

    def body(*refs):
        p, w, m, v = (refs[a * n:(a + 1) * n] for a in range(4))
        out = refs[4 * n:]
        for a in range(n):
            g = _sum_parts(p[a])
            nm, nv, d = _adamw(g, w[a][...], m[a][...], v[a][...])
            out[a][...] = g
            out[n + a][...] = d
            out[2 * n + a][...] = nm
            out[3 * n + a][...] = nv

    vmem = pl.BlockSpec(memory_space=pltpu.VMEM)
    outs = pl.pallas_call(
        body, name=name, in_specs=[vmem] * (4 * n), out_specs=[vmem] * (4 * n),
        out_shape=[jax.ShapeDtypeStruct(w.shape, F32) for w in ws] * 4,
    )(*parts, *ws, *ms, *vs)
    return [list(outs[a * n:(a + 1) * n]) for a in range(4)]


def _misc_rows(b_forget, extra=None):
    tile = jnp.pad(b_forget, ((0, 6), (0, LANES - FOX_HEADS)))
    return tile if extra is None else tile.at[2].set(extra)


def kernel(x, mem, norm_w, w_in, b_forget, mem_norm_w, w_mem_kv, out_norm_w, w_out, final_norm_w, loss_target, m_norm_w, m_w_in, m_b_forget, m_mem_norm_w, m_w_mem_kv, m_out_norm_w, m_w_out, m_final_norm_w, v_norm_w, v_w_in, v_b_forget, v_mem_norm_w, v_w_mem_kv, v_out_norm_w, v_w_out, v_final_norm_w):
    kv_rows = w_mem_kv.shape[1]
    out_rows = w_out.shape[1]

    def shards(l):
        return [w_in[l].T.astype(BF16), w_mem_kv[l].astype(BF16), w_out[l].astype(BF16)]

    def full_kv_out(g_kv, g_out):
        return g_kv.reshape(D_MODEL, 2 * MEM_W), g_out.reshape(MIX_W, D_MODEL)

    def kv_blocks(g):
        return g.reshape(N_DEV, kv_rows, 2 * MEM_W).astype(BF16)

    def out_blocks(g):
        return g.reshape(N_DEV, out_rows, D_MODEL).astype(BF16)

    def row(v):
        return v.reshape(1, -1)

    def cols(first, size):
        return (1, first, size)

    fwd_split, bwd_split = 4 * LANES, (5 * LANES, 6 * LANES)

    s_in0, s_kv0, s_out0 = shards(0)
    (g_in0,) = _all_gather_weights([s_in0], "all_gather_l0")
    w_r0 = _assemble_w_in(g_in0, False, "assemble_w_in_0")
    s_in1, s_kv1, s_out1 = shards(1)
    x1, saved0, (l_in1, _, _), (wkv0, wout0) = _layer_fwd(
        x[0], mem[0], row(norm_w[0]), w_r0, b_forget[0], row(mem_norm_w[0]),
        lambda lands: full_kv_out(lands[1], lands[2]),
        row(out_norm_w[0]), 0,
        travel=([s_in1, s_kv0, s_out0], False,
                [([0], [cols(0, fwd_split)]), ([0], [cols(fwd_split, D_MODEL - fwd_split)]),
                 ([1, 2], [None, None])]))
    w_r1 = _assemble_w_in(l_in1, False, "assemble_w_in_1")
    x2, saved1, _, (wkv1, wout1) = _layer_fwd(
        x1, mem[0], row(norm_w[1]), w_r1, b_forget[1], row(mem_norm_w[1]),
        lambda lands: full_kv_out(lands[0], lands[1]),
        row(out_norm_w[1]), 1, travel=([s_kv1, s_out1], False, [([0, 1], [None, None]), None]))

    dx2, loss_part, dfnw = _final_fwd_bwd(x2, row(final_norm_w), loss_target[0], "final_fwd_bwd")

    dw_out_blocks = jax.ShapeDtypeStruct((N_DEV, out_rows, D_MODEL), BF16)
    dx1, gr1, (l_out1,) = _layer_bwd(
        dx2, saved1, mem[0], w_r1, wkv1, wout1, 1,
        travel=([dw_out_blocks], True, [([0], [None])]), own_w_out=(0, out_blocks))
    p_in1 = _assemble_w_in(gr1["w_r"], True, "w_in_grad_blocks_1")
    p_kv1 = kv_blocks(gr1["w_mem_kv"])
    grad_x, gr0, (l_in1, l_kv1, l_out0) = _layer_bwd(
        dx1[0], saved0, mem[0], w_r0, wkv0, wout0, 0,
        travel=([p_in1, p_kv1, dw_out_blocks], True,
                [([0, 1], [cols(0, bwd_split[0]), None]),
                 ([0, 2], [cols(bwd_split[0], bwd_split[1] - bwd_split[0]), None]),
                 None,
                 ([0], [cols(bwd_split[1], D_MODEL - bwd_split[1])])]),
        own_w_out=(2, out_blocks))
    r_out1, r_in1, r_kv1, r_out0 = l_out1, l_in1, l_kv1, l_out0

    def both(name):
        return jnp.stack([gr0[name], gr1[name]])

    small = [both("norm_w"), both("mem_norm_w"), both("out_norm_w"), dfnw,
             _misc_rows(both("b_forget"), loss_part[0])]
    p_small = [jnp.broadcast_to(v[None], (N_DEV,) + v.shape) for v in small]
    by_core = [v.reshape((N_CHIP, 2) + v.shape[1:])
               for v in (_assemble_w_in(gr0["w_r"], True, "w_in_grad_blocks_0"),
                         kv_blocks(gr0["w_mem_kv"]))]
    from_sibling = _pair_swap(by_core, "grads_l0_pair_swap")
    core = lax.axis_index("c").astype(jnp.int32).reshape(1)
    chip_sums = _pair_add(by_core, from_sibling, core, "grads_l0_pair_add")
    r_in0, r_kv0, *r_small = _chip_exchange(chip_sums, p_small, "exchange_grads_l0")

    def view(v):
        return jnp.transpose(v, (2, 0, 1))

    g_w_in, d_w_in, nm_w_in, nv_w_in = [jnp.transpose(v, (1, 2, 0)) for v in _adamw_w_in(
        [r_in0, r_in1], view(w_in), view(m_w_in), view(v_w_in), "adamw_w_in")]
    g_w_kv, d_w_kv, nm_w_kv, nv_w_kv = _adamw_sum(
        [r_kv0, r_kv1], w_mem_kv, m_w_mem_kv, v_w_mem_kv, (kv_rows, 2 * MEM_W), "adamw_w_mem_kv")
    g_w_out, d_w_out, nm_w_out, nv_w_out = _adamw_sum(
        [r_out0, r_out1], w_out, m_w_out, v_w_out, (out_rows, D_MODEL), "adamw_w_out")
    def smalls(nw, mnw, onw, fnw, b):
        return [nw, mnw, onw, row(fnw), _misc_rows(b)]

    small_out = _adamw_small(
        r_small, smalls(norm_w, mem_norm_w, out_norm_w, final_norm_w, b_forget),
        smalls(m_norm_w, m_mem_norm_w, m_out_norm_w, m_final_norm_w, m_b_forget),
        smalls(v_norm_w, v_mem_norm_w, v_out_norm_w, v_final_norm_w, v_b_forget), "adamw_small")
    (g_nw, g_mnw, g_onw, g_fnw, g_b), (d_nw, d_mnw, d_onw, d_fnw, d_b), \
        (nm_nw, nm_mnw, nm_onw, nm_fnw, nm_b), (nv_nw, nv_mnw, nv_onw, nv_fnw, nv_b) = [
            (nw, mnw, onw, fnw[0], misc[:2, :FOX_HEADS]) for nw, mnw, onw, fnw, misc in small_out]
    loss = small_out[0][4][2, 0]

    return (loss, grad_x,
            g_nw, g_w_in, g_b, g_mnw, g_w_kv, g_onw, g_w_out, g_fnw,
            d_nw, d_w_in, d_b, d_mnw, d_w_kv, d_onw, d_w_out, d_fnw,
            nm_nw, nm_w_in, nm_b, nm_mnw, nm_w_kv, nm_onw, nm_w_out, nm_fnw,
            nv_nw, nv_w_in, nv_b, nv_mnw, nv_w_kv, nv_onw, nv_w_out, nv_fnw)
```

```python
import functools
from typing import NamedTuple

import jax
import jax.numpy as jnp
from jax import lax
from jax.experimental import pallas as pl
from jax.experimental.pallas import tpu as pltpu

F32 = jnp.float32
BF16 = jnp.bfloat16

N_DEV = 8
D_MODEL = 1024
HEAD_DIM = 64
LANES = 128
SB_W = 512
FX_W = 512
MEM_W = 256
MIX_W = 1280
FOX_HEADS = 8
IN_W = 4616
SHARD_W = IN_W // N_DEV
QKV_W = 3 * SB_W + 3 * FX_W + MEM_W
FL_PAD = 256
GF_W = MIX_W + FL_PAD
WR_W = QKV_W + GF_W
EPS = 1e-6
T = 256
QPS = 4
TM = 256
TQM = 512
Q_SCALE = 0.125
NEG = -1e30
UNDERFLOW = -110.0
NORM_SLACK = 1.01

ADAM_LR = 0.001
ADAM_B1 = 0.9
ADAM_B2 = 0.999
ADAM_EPS = 1e-08
ADAM_WD = 0.01
ADAM_STEP = 10


_NT = (((1,), (1,)), ((), ()))
_TN = (((0,), (0,)), ((), ()))

_ARB1 = pltpu.CompilerParams(dimension_semantics=("arbitrary",))
_ARB2 = pltpu.CompilerParams(dimension_semantics=("arbitrary", "arbitrary"))


def _dot(a, b):
    return jnp.dot(a, b, preferred_element_type=F32)


def _dot_nt(a, b):
    return lax.dot_general(a, b, _NT, preferred_element_type=F32)


def _dot_tn(a, b):
    return lax.dot_general(a, b, _TN, preferred_element_type=F32)


def _split2(x):
    hi = x.astype(BF16)
    lo = (x - hi.astype(F32)).astype(BF16)
    return hi, lo


def _stack2(u):
    return jnp.concatenate([u, u], axis=0)


def _cum2(x, u2):
    hi, lo = _split2(x)
    return _dot(jnp.concatenate([hi, lo], axis=1), u2)


def _tri3(tri, x, dot=None):
    dot = dot or _dot
    hi = x.astype(BF16)
    r1 = x - hi.astype(F32)
    mid = r1.astype(BF16)
    lo = (r1 - mid.astype(F32)).astype(BF16)
    return dot(tri, hi) + dot(tri, mid) + dot(tri, lo)


def _iota2(shape, dim):
    return lax.broadcasted_iota(jnp.int32, shape, dim)


def _head_block_diag():
    r = _iota2((LANES, LANES), 0) // HEAD_DIM
    c = _iota2((LANES, LANES), 1) // HEAD_DIM
    return _stack2(jnp.where(r == c, 1.0, 0.0).astype(BF16))


def _head_mean(x, bd):
    return _cum2(x, bd) * (1.0 / HEAD_DIM)


def _sigmoid(x):
    return 1.0 / (1.0 + jnp.exp(-x))


def _log_sigmoid(x):
    return jnp.minimum(x, 0.0) - jnp.log(1.0 + jnp.exp(-jnp.abs(x)))


def _running_top(r_ref):
    return jnp.max(jnp.maximum(r_ref[0], r_ref[1]))


def _fox_tiles_left(i, pair, nq, fb_ref, tile):
    def bound(j):
        b = []
        for h in range(2):
            head = 2 * pair + h
            b.append(2.0 * NORM_SLACK * fb_ref[2 * nq + i, head] * fb_ref[3 * nq, head]
                     + fb_ref[2 * i, head] - fb_ref[2 * j + 1, head])
        return jnp.maximum(b[0], b[1])

    def more(j):
        return jnp.logical_and(j >= 0, bound(jnp.maximum(j, 0)) > UNDERFLOW)

    def step(j):
        tile(j, False)
        return j - 1

    return lax.while_loop(more, step, i - 1)


def _pair_masks():
    lane = _iota2((1, LANES), 1)
    return lane < HEAD_DIM


def _split_pair(x, m0):
    zero = jnp.zeros_like(x)
    return jnp.where(m0, x, zero), jnp.where(m0, zero, x)


def _inproj_fwd(x, nw, w_r, name, rider=None):
    s = x.shape[0]

    def body(x_ref, nw_ref, w_ref, qkv_ref, gf_ref):
        xv = x_ref[...]
        r = lax.rsqrt(jnp.mean(xv * xv, axis=-1, keepdims=True) + EPS)
        h = (xv * r * nw_ref[...]).astype(BF16)
        for c in range(0, QKV_W, 256):
            qkv_ref[:, c:c + 256] = _dot_nt(h, w_ref[c:c + 256, :]).astype(BF16)
        for c in range(0, GF_W, 256):
            gf_ref[:, c:c + 256] = _dot_nt(h, w_ref[QKV_W + c:QKV_W + c + 256, :])

    (qkv, gf), lands = _ride(dict(
        body=body, name=name, grid=(s // TM,),
        in_specs=[pl.BlockSpec((TM, D_MODEL), lambda i: (i, 0)),
                  pl.BlockSpec((1, D_MODEL), lambda i: (0, 0)),
                  pl.BlockSpec((WR_W, D_MODEL), lambda i: (0, 0))],
        out_specs=[pl.BlockSpec((TM, QKV_W), lambda i: (i, 0)),
                   pl.BlockSpec((TM, GF_W), lambda i: (i, 0))],
        out_shape=[jax.ShapeDtypeStruct((s, QKV_W), BF16), jax.ShapeDtypeStruct((s, GF_W), F32)],
        scratch_shapes=[], compiler_params=_ARB1, operands=[x, nw, w_r]), rider)
    return qkv, gf, lands


def _fox_prep_fwd(gf, qkv, bpad, name):
    s = gf.shape[0]
    nq = s // T
    nrow = -(-(3 * nq + 1) // 8) * 8

    def body(fl_ref, q_ref, k_ref, b_ref, fq_ref, fr_ref, fb_ref):
        tri = jnp.where(_iota2((T, T), 0) >= _iota2((T, T), 1), 1.0, 0.0).astype(BF16)
        m0 = _pair_masks()
        lane = _iota2((1, LANES), 1)
        norms = [jnp.zeros((1, LANES), F32) for _ in range(nq + 1)]
        same_head = (_iota2((LANES, LANES), 0) // HEAD_DIM) == (_iota2((LANES, LANES), 1) // HEAD_DIM)
        bd = jnp.where(same_head, 1.0, 0.0).astype(BF16)
        for p in range(FOX_HEADS // 2):
            cols = slice(p * LANES, (p + 1) * LANES)
            q = (q_ref[:, cols] * jnp.asarray(Q_SCALE, BF16)).astype(F32)
            k = k_ref[:, cols].astype(F32)
            qn = _dot((q * q).astype(BF16), bd)
            kn = _dot((k * k).astype(BF16), bd)
            tops = [jnp.max(qn[j * T:(j + 1) * T], axis=0, keepdims=True) for j in range(nq)]
            tops.append(jnp.max(kn, axis=0, keepdims=True))
            tops = [jnp.sqrt(top) for top in tops]
            for h in range(2):
                at = h * HEAD_DIM
                norms = [jnp.where(lane == 2 * p + h, top[:, at:at + 1], row)
                         for top, row in zip(tops, norms)]
        for j in range(nq + 1):
            fb_ref[2 * nq + j:2 * nq + j + 1, :] = norms[j]
        fb_ref[3 * nq + 1:, :] = jnp.zeros((nrow - 3 * nq - 1, LANES), F32)
        carry = jnp.zeros((1, LANES), F32)
        for blk in range(s // T):
            rows = slice(blk * T, (blk + 1) * T)
            lf = _log_sigmoid(fl_ref[rows, :] + b_ref[...])
            c = _tri3(tri, lf) + carry
            carry = c[T - 1:T, :]
            for p in range(FOX_HEADS // 2):
                fq_ref[rows, p * LANES:(p + 1) * LANES] = jnp.where(
                    m0, c[:, 2 * p:2 * p + 1], c[:, 2 * p + 1:2 * p + 2])
            fr_ref[:, rows] = c.T[0:FOX_HEADS, :]
            fb_ref[2 * blk:2 * blk + 1, :] = c[0:1, :]
            fb_ref[2 * blk + 1:2 * blk + 2, :] = carry

    base = 3 * SB_W // FX_W
    return pl.pallas_call(
        body, name=name, grid=(1,),
        in_specs=[pl.BlockSpec((s, LANES), lambda i: (0, MIX_W // LANES)),
                  pl.BlockSpec((s, FX_W), lambda i: (0, base)),
                  pl.BlockSpec((s, FX_W), lambda i: (0, base + 1)),
                  pl.BlockSpec((1, LANES), lambda i: (0, 0))],
        out_specs=[pl.BlockSpec((s, FX_W), lambda i: (0, 0)),
                   pl.BlockSpec((FOX_HEADS, s), lambda i: (0, 0)),
                   pl.BlockSpec((nrow, LANES), lambda i: (0, 0))],
        out_shape=[jax.ShapeDtypeStruct((s, FX_W), F32), jax.ShapeDtypeStruct((FOX_HEADS, s), F32),
                   jax.ShapeDtypeStruct((nrow, LANES), F32)],
        compiler_params=_ARB1,
    )(gf, qkv, qkv, bpad)


def _sb_fwd(qkv, name, rider=None):
    s = qkv.shape[0]

    def body(q_ref, k_ref, v_ref, o_ref, acc_ref, r_ref, as_ref):
        m0 = _pair_masks()
        strict = _iota2((T, T), 0) > _iota2((T, T), 1)
        u2 = _stack2(jnp.where(strict, 1.0, 0.0).astype(BF16))
        hs = range(2)

        def query_tile(i, rows):
            qh = _split_pair(q_ref[rows, :] * jnp.asarray(Q_SCALE, BF16), m0)
            acc_ref[...] = jnp.zeros_like(acc_ref)
            r_ref[...] = jnp.zeros_like(r_ref)

            def flush(j):
                v = v_ref[pl.ds(pl.multiple_of(j * T, T), T), :]
                for h in hs:
                    acc_ref[h] += _dot(as_ref[h], v)

            def tile(j, diag):
                k = k_ref[pl.ds(pl.multiple_of(j * T, T), T), :]
                z = [_dot_nt(qh[h], k) for h in hs]
                if not diag:
                    flush(j + 1)
                la = [jnp.minimum(z[h], 0.0) - jnp.log(1.0 + jnp.exp(-jnp.abs(z[h]))) for h in hs]
                lf = [la[h] - z[h] for h in hs]
                if diag:
                    lf = [jnp.where(strict, lf[h], 0.0) for h in hs]
                cin = [_cum2(lf[h], u2) for h in hs]
                a = [jnp.exp(la[h] + cin[h] + r_ref[h]) for h in hs]
                if diag:
                    a = [jnp.where(strict, a[h], 0.0) for h in hs]
                for h in hs:
                    r_ref[h] += cin[h][:, 0:1] + lf[h][:, 0:1]
                    as_ref[h] = a[h].astype(BF16)

            tile(i, True)

            def more(state):
                j, top = state
                return jnp.logical_and(j >= 0, top > UNDERFLOW)

            def step(state):
                j, _ = state
                tile(j, False)
                return j - 1, _running_top(r_ref)

            j_left, _ = lax.while_loop(more, step, (i - 1, _running_top(r_ref)))
            flush(j_left + 1)
            o_ref[rows, :] = jnp.where(m0, acc_ref[0], acc_ref[1])

        for n in range(QPS):
            query_tile(QPS * pl.program_id(1) + n, slice(n * T, (n + 1) * T))

    nb = SB_W // LANES
    (ysb,), lands = _ride(dict(
        body=body, name=name, grid=(nb, s // (QPS * T)),
        in_specs=[pl.BlockSpec((QPS * T, LANES), lambda p, i: (i, p)),
                  pl.BlockSpec((s, LANES), lambda p, i: (0, nb + p)),
                  pl.BlockSpec((s, LANES), lambda p, i: (0, 2 * nb + p))],
        out_specs=[pl.BlockSpec((QPS * T, LANES), lambda p, i: (i, p))],
        out_shape=[jax.ShapeDtypeStruct((s, SB_W), F32)],
        scratch_shapes=[pltpu.VMEM((2, T, LANES), F32), pltpu.VMEM((2, T, 1), F32),
                        pltpu.VMEM((2, T, T), BF16)],
        compiler_params=_ARB2, operands=[qkv, qkv, qkv]), rider)
    return ysb, lands


def _fox_fwd(qkv, fqb, frow, fbounds, name, rider=None):
    s = qkv.shape[0]

    def body(q_ref, k_ref, v_ref, fq_ref, fr_ref, fb_ref, o_ref, lse_ref, acc_ref, m_ref, ps_ref):
        pair = pl.program_id(0)
        m0 = _pair_masks()
        causal = _iota2((T, T), 0) >= _iota2((T, T), 1)
        ones = jnp.ones((T, LANES), BF16)
        hs = range(2)

        def query_tile(i, rows):
            qh = _split_pair(q_ref[rows, :] * jnp.asarray(Q_SCALE, BF16), m0)
            fq = fq_ref[rows, :]
            fqh = (fq[:, 0:1], fq[:, HEAD_DIM:HEAD_DIM + 1])
            acc_ref[...] = jnp.zeros_like(acc_ref)
            m_ref[...] = jnp.full_like(m_ref, NEG)

            def flush(j):
                v = v_ref[pl.ds(pl.multiple_of(j * T, T), T), :]
                va2 = _stack2(jnp.concatenate([v, ones], axis=1))
                for h in hs:
                    acc_ref[h] += _dot(ps_ref[h], va2)

            def tile(j, diag):
                off = pl.multiple_of(j * T, T)
                k = k_ref[pl.ds(off, T), :]
                sc = [_dot_nt(qh[h], k) + fqh[h] - fr_ref[h:h + 1, pl.ds(off, T)] for h in hs]
                if not diag:
                    flush(j + 1)
                if diag:
                    sc = [jnp.where(causal, sc[h], NEG) for h in hs]
                m_new = [jnp.maximum(m_ref[h], jnp.max(sc[h], axis=1, keepdims=True)) for h in hs]
                p = [jnp.exp(sc[h] - m_new[h]) for h in hs]
                for h in hs:
                    acc_ref[h] = acc_ref[h] * jnp.exp(m_ref[h] - m_new[h])
                    m_ref[h] = m_new[h]
                    p_hi, p_lo = _split2(p[h])
                    ps_ref[h] = jnp.concatenate([p_hi, p_lo], axis=1)

            tile(i, True)
            j_left = _fox_tiles_left(i, pair, s // T, fb_ref, tile)
            flush(j_left + 1)
            acc = (acc_ref[0], acc_ref[1])
            o_ref[rows, :] = jnp.where(m0, acc[0][:, :LANES] / acc[0][:, LANES:],
                                       acc[1][:, :LANES] / acc[1][:, LANES:])
            lse_ref[rows, :] = jnp.where(m0, m_ref[0] + jnp.log(acc[0][:, LANES:]),
                                         m_ref[1] + jnp.log(acc[1][:, LANES:]))

        for n in range(QPS):
            query_tile(QPS * pl.program_id(1) + n, slice(n * T, (n + 1) * T))

    nb = FX_W // LANES
    base = 3 * SB_W // LANES
    (yfx, lse), lands = _ride(dict(
        body=body, name=name, grid=(nb, s // (QPS * T)),
        in_specs=[pl.BlockSpec((QPS * T, LANES), lambda p, i: (i, base + p)),
                  pl.BlockSpec((s, LANES), lambda p, i: (0, base + nb + p)),
                  pl.BlockSpec((s, LANES), lambda p, i: (0, base + 2 * nb + p)),
                  pl.BlockSpec((QPS * T, LANES), lambda p, i: (i, p)),
                  pl.BlockSpec((None, 2, s), lambda p, i: (p, 0, 0)),
                  pl.BlockSpec(memory_space=pltpu.SMEM)],
        out_specs=[pl.BlockSpec((QPS * T, LANES), lambda p, i: (i, p)),
                   pl.BlockSpec((QPS * T, LANES), lambda p, i: (i, p))],
        out_shape=[jax.ShapeDtypeStruct((s, FX_W), F32), jax.ShapeDtypeStruct((s, FX_W), F32)],
        scratch_shapes=[pltpu.VMEM((2, T, 2 * LANES), F32), pltpu.VMEM((2, T, 1), F32),
                        pltpu.VMEM((2, T, 2 * T), BF16)],
        compiler_params=_ARB2, operands=[qkv, qkv, qkv, fqb, frow, fbounds]), rider)
    return yfx, lse, lands


def _memkv_fwd(mem, mnw, wkv, name):
    n = mem.shape[0]

    def body(mem_ref, mnw_ref, w_ref, kv_ref):
        mv = mem_ref[...]
        r = lax.rsqrt(jnp.mean(mv * mv, axis=-1, keepdims=True) + EPS)
        hm = (mv * r * mnw_ref[...]).astype(BF16)
        kv_ref[...] = _dot(hm, w_ref[...]).astype(BF16)

    return pl.pallas_call(
        body, name=name, grid=(1,),
        in_specs=[pl.BlockSpec((n, D_MODEL), lambda i: (0, 0)),
                  pl.BlockSpec((1, D_MODEL), lambda i: (0, 0)),
                  pl.BlockSpec((D_MODEL, 2 * MEM_W), lambda i: (0, 0))],
        out_specs=pl.BlockSpec((n, 2 * MEM_W), lambda i: (0, 0)),
        out_shape=jax.ShapeDtypeStruct((n, 2 * MEM_W), BF16),
        compiler_params=_ARB1,
    )(mem, mnw, wkv)


def _mem_fwd(qkv, kv, name):
    s = qkv.shape[0]
    n = kv.shape[0]

    def body(q_ref, k_ref, v_ref, o_ref, lse_ref):
        m0 = _pair_masks()
        qh = _split_pair(q_ref[...] * jnp.asarray(Q_SCALE, BF16), m0)
        k = k_ref[...]
        v = v_ref[...]
        outs, lses = [], []
        for h in range(2):
            sc = _dot_nt(qh[h], k)
            mx = jnp.max(sc, axis=1, keepdims=True)
            p = jnp.exp(sc - mx)
            l = jnp.sum(p, axis=1, keepdims=True)
            outs.append(_dot(p.astype(BF16), v) / l)
            lses.append(mx + jnp.log(l))
        o_ref[...] = jnp.where(m0, outs[0], outs[1])
        lse_ref[...] = jnp.where(m0, lses[0], lses[1])

    nb = MEM_W // LANES
    base = (3 * SB_W + 3 * FX_W) // LANES
    return pl.pallas_call(
        body, name=name, grid=(nb, s // TQM),
        in_specs=[pl.BlockSpec((TQM, LANES), lambda p, i: (i, base + p)),
                  pl.BlockSpec((n, LANES), lambda p, i: (0, p)),
                  pl.BlockSpec((n, LANES), lambda p, i: (0, nb + p))],
        out_specs=[pl.BlockSpec((TQM, LANES), lambda p, i: (i, p)),
                   pl.BlockSpec((TQM, LANES), lambda p, i: (i, p))],
        out_shape=[jax.ShapeDtypeStruct((s, MEM_W), F32), jax.ShapeDtypeStruct((s, MEM_W), F32)],
        compiler_params=_ARB2,
    )(qkv, kv, kv)


def _mix_chunk(c, ysb_ref, yfx_ref, ym_ref):
    if c < SB_W // LANES:
        return ysb_ref[:, c * LANES:(c + 1) * LANES]
    c -= SB_W // LANES
    if c < FX_W // LANES:
        return yfx_ref[:, c * LANES:(c + 1) * LANES]
    c -= FX_W // LANES
    return ym_ref[:, c * LANES:(c + 1) * LANES]


def _outproj_fwd(ysb, yfx, ym, gf, onw, wout, x, name):
    s = x.shape[0]

    def body(ysb_ref, yfx_ref, ym_ref, g_ref, onw_ref, w_ref, x_ref, o_ref, yg_ref):
        bd = _head_block_diag()
        for c in range(MIX_W // LANES):
            sl = slice(c * LANES, (c + 1) * LANES)
            u = _mix_chunk(c, ysb_ref, yfx_ref, ym_ref)
            r = lax.rsqrt(_head_mean(u * u, bd) + EPS)
            g = g_ref[:, sl]
            yg_ref[:, sl] = (u * r * onw_ref[:, sl] * (g * _sigmoid(g))).astype(BF16)
        o_ref[...] = x_ref[...] + _dot(yg_ref[...], w_ref[...])

    return pl.pallas_call(
        body, name=name, grid=(s // TM,),
        in_specs=[pl.BlockSpec((TM, SB_W), lambda i: (i, 0)),
                  pl.BlockSpec((TM, FX_W), lambda i: (i, 0)),
                  pl.BlockSpec((TM, MEM_W), lambda i: (i, 0)),
                  pl.BlockSpec((TM, MIX_W), lambda i: (i, 0)),
                  pl.BlockSpec((1, MIX_W), lambda i: (0, 0)),
                  pl.BlockSpec((MIX_W, D_MODEL), lambda i: (0, 0)),
                  pl.BlockSpec((TM, D_MODEL), lambda i: (i, 0))],
        out_specs=pl.BlockSpec((TM, D_MODEL), lambda i: (i, 0)),
        out_shape=jax.ShapeDtypeStruct((s, D_MODEL), F32),
        scratch_shapes=[pltpu.VMEM((TM, MIX_W), BF16)],
        compiler_params=_ARB1,
    )(ysb, yfx, ym, gf, onw, wout, x)


def _final_fwd_bwd(x, fnw, target, name):
    s = x.shape[0]

    def body(x_ref, w_ref, t_ref, dx_ref, loss_ref, dw_ref):
        @pl.when(pl.program_id(0) == 0)
        def _():
            loss_ref[...] = jnp.zeros_like(loss_ref)
            dw_ref[...] = jnp.zeros_like(dw_ref)

        xv = x_ref[...]
        w = w_ref[...]
        r = lax.rsqrt(jnp.mean(xv * xv, axis=-1, keepdims=True) + EPS)
        xh = xv * r
        err = xh * w - t_ref[...]
        part = jnp.sum(jnp.sum(err * err, axis=1, keepdims=True), axis=0, keepdims=True)
        loss_ref[...] += part * (0.5 / D_MODEL)
        dy = err * (1.0 / D_MODEL)
        dw_ref[...] += jnp.sum(dy * xh, axis=0, keepdims=True)
        dxh = dy * w
        dx_ref[...] = r * (dxh - xh * jnp.mean(dxh * xh, axis=-1, keepdims=True))

    return pl.pallas_call(
        body, name=name, grid=(s // TM,),
        in_specs=[pl.BlockSpec((TM, D_MODEL), lambda i: (i, 0)),
                  pl.BlockSpec((1, D_MODEL), lambda i: (0, 0)),
                  pl.BlockSpec((TM, D_MODEL), lambda i: (i, 0))],
        out_specs=[pl.BlockSpec((TM, D_MODEL), lambda i: (i, 0)),
                   pl.BlockSpec((1, LANES), lambda i: (0, 0)),
                   pl.BlockSpec((1, D_MODEL), lambda i: (0, 0))],
        out_shape=[jax.ShapeDtypeStruct((s, D_MODEL), F32), jax.ShapeDtypeStruct((1, LANES), F32),
                   jax.ShapeDtypeStruct((1, D_MODEL), F32)],
        compiler_params=_ARB1,
    )(x, fnw, target)


def _outproj_bwd(dxo, wout, ysb, yfx, ym, gf, onw, name, rider=None):
    s = dxo.shape[0]

    def body(dx_ref, w_ref, ysb_ref, yfx_ref, ym_ref, g_ref, onw_ref,
             dysb_ref, dyfx_ref, dym_ref, dg_ref, dw_ref, donw_ref, yg_ref):
        @pl.when(pl.program_id(0) == 0)
        def _():
            dw_ref[...] = jnp.zeros_like(dw_ref)
            donw_ref[...] = jnp.zeros_like(donw_ref)

        dxb = dx_ref[...].astype(BF16)
        dyg = _dot_nt(dxb, w_ref[...])
        bd = _head_block_diag()
        for c in range(MIX_W // LANES):
            sl = slice(c * LANES, (c + 1) * LANES)
            u = _mix_chunk(c, ysb_ref, yfx_ref, ym_ref)
            r = lax.rsqrt(_head_mean(u * u, bd) + EPS)
            yn = u * r
            g = g_ref[:, sl]
            sg = _sigmoid(g)
            sil = g * sg
            onw = onw_ref[:, sl]
            e = dyg[:, sl]
            yg_ref[:, sl] = (yn * onw * sil).astype(BF16)
            donw_ref[:, sl] += jnp.sum(e * yn * sil, axis=0, keepdims=True)
            dg_ref[:, sl] = (e * yn * onw * (sg * (1.0 + g * (1.0 - sg)))).astype(BF16)
            dyn = e * onw * sil
            du = (r * (dyn - yn * _head_mean(dyn * yn, bd))).astype(BF16)
            if c < 4:
                dysb_ref[:, c * LANES:(c + 1) * LANES] = du
            elif c < 8:
                dyfx_ref[:, (c - 4) * LANES:(c - 3) * LANES] = du
            else:
                dym_ref[:, (c - 8) * LANES:(c - 7) * LANES] = du
        dw_ref[...] += _dot_tn(yg_ref[...], dxb)

    outs, lands = _ride(dict(
        body=body, name=name, grid=(s // TM,),
        in_specs=[pl.BlockSpec((TM, D_MODEL), lambda i: (i, 0)),
                  pl.BlockSpec((MIX_W, D_MODEL), lambda i: (0, 0)),
                  pl.BlockSpec((TM, SB_W), lambda i: (i, 0)),
                  pl.BlockSpec((TM, FX_W), lambda i: (i, 0)),
                  pl.BlockSpec((TM, MEM_W), lambda i: (i, 0)),
                  pl.BlockSpec((TM, MIX_W), lambda i: (i, 0)),
                  pl.BlockSpec((1, MIX_W), lambda i: (0, 0))],
        out_specs=[pl.BlockSpec((TM, SB_W), lambda i: (i, 0)),
                   pl.BlockSpec((TM, FX_W), lambda i: (i, 0)),
                   pl.BlockSpec((TM, MEM_W), lambda i: (i, 0)),
                   pl.BlockSpec((TM, MIX_W), lambda i: (i, 0)),
                   pl.BlockSpec((MIX_W, D_MODEL), lambda i: (0, 0)),
                   pl.BlockSpec((1, MIX_W), lambda i: (0, 0))],
        out_shape=[jax.ShapeDtypeStruct((s, SB_W), BF16), jax.ShapeDtypeStruct((s, FX_W), BF16),
                   jax.ShapeDtypeStruct((s, MEM_W), BF16), jax.ShapeDtypeStruct((s, MIX_W), BF16),
                   jax.ShapeDtypeStruct((MIX_W, D_MODEL), F32), jax.ShapeDtypeStruct((1, MIX_W), F32)],
        scratch_shapes=[pltpu.VMEM((TM, MIX_W), BF16)],
        compiler_params=_ARB1, operands=[dxo, wout, ysb, yfx, ym, gf, onw]), rider)
    return (*outs, lands)


def _row_dots(do, o, m0):
    prod = do.astype(F32) * o
    zero = jnp.zeros_like(prod)
    return (jnp.sum(jnp.where(m0, prod, zero), axis=1, keepdims=True),
            jnp.sum(jnp.where(m0, zero, prod), axis=1, keepdims=True))


def _sb_bwd(qkv, o, do, name, rider=None):
    s = qkv.shape[0]
    nq = s // T

    def body(q_ref, k_ref, v_ref, o_ref, do_ref, dq_ref, dk_ref, dv_ref,
             dqa_ref, dka_ref, dva_ref, rl_ref, rg_ref, dzs_ref, abs_ref):
        step_id = pl.program_id(1)

        @pl.when(step_id == 0)
        def _():
            dka_ref[...] = jnp.zeros_like(dka_ref)
            dva_ref[...] = jnp.zeros_like(dva_ref)

        m0 = _pair_masks()
        strict = _iota2((T, T), 0) > _iota2((T, T), 1)
        u2 = _stack2(jnp.where(strict, 1.0, 0.0).astype(BF16))
        hs = range(2)

        def query_tile(i, rows):
            qh = _split_pair(q_ref[rows, :] * jnp.asarray(Q_SCALE, BF16), m0)
            doh = _split_pair(do_ref[rows, :], m0)
            dsum = _row_dots(do_ref[rows, :], o_ref[rows, :], m0)
            dqa_ref[...] = jnp.zeros_like(dqa_ref)
            rl_ref[...] = jnp.zeros_like(rl_ref)
            rg_ref[...] = jnp.zeros_like(rg_ref)

            def flush(j):
                off = pl.multiple_of(j * T, T)
                k = k_ref[pl.ds(off, T), :]
                for h in hs:
                    dqa_ref[h] += _dot(dzs_ref[h], k)
                dka_ref[pl.ds(off, T), :] += (_dot_tn(dzs_ref[0], qh[0])
                                              + _dot_tn(dzs_ref[1], qh[1]))
                dva_ref[pl.ds(off, T), :] += (_dot_tn(abs_ref[0], doh[0])
                                              + _dot_tn(abs_ref[1], doh[1]))

            def tile(j, diag):
                off = pl.multiple_of(j * T, T)
                k = k_ref[pl.ds(off, T), :]
                v = v_ref[pl.ds(off, T), :]
                z = [_dot_nt(qh[h], k) for h in hs]
                da = [_dot_nt(doh[h], v) for h in hs]
                if not diag:
                    flush(j + 1)
                la = [jnp.minimum(z[h], 0.0) - jnp.log(1.0 + jnp.exp(-jnp.abs(z[h]))) for h in hs]
                lf = [la[h] - z[h] for h in hs]
                if diag:
                    lf = [jnp.where(strict, lf[h], 0.0) for h in hs]
                cin = [_cum2(lf[h], u2) for h in hs]
                a = [jnp.exp(la[h] + cin[h] + rl_ref[h]) for h in hs]
                if diag:
                    a = [jnp.where(strict, a[h], 0.0) for h in hs]
                ab = [a[h].astype(BF16) for h in hs]
                g = [ab[h].astype(F32) * da[h] for h in hs]
                gin = [_cum2(g[h], u2) for h in hs]
                dz = [g[h] - jnp.exp(la[h]) * ((dsum[h] - rg_ref[h]) - gin[h]) for h in hs]
                if diag:
                    dz = [jnp.where(strict, dz[h], 0.0) for h in hs]
                for h in hs:
                    rl_ref[h] += cin[h][:, 0:1] + lf[h][:, 0:1]
                    rg_ref[h] += gin[h][:, 0:1] + g[h][:, 0:1]
                    dzs_ref[h] = dz[h].astype(BF16)
                    abs_ref[h] = ab[h]

            tile(i, True)

            def more(state):
                j, top = state
                return jnp.logical_and(j >= 0, top > UNDERFLOW)

            def step(state):
                j, _ = state
                tile(j, False)
                return j - 1, _running_top(rl_ref)

            j_left, _ = lax.while_loop(more, step, (i - 1, _running_top(rl_ref)))
            flush(j_left + 1)
            dq_ref[rows, :] = (jnp.where(m0, dqa_ref[0], dqa_ref[1]) * Q_SCALE).astype(BF16)

        for n in range(QPS):
            query_tile(QPS * step_id + n, slice(n * T, (n + 1) * T))

        @pl.when(step_id == nq // QPS - 1)
        def _():
            dk_ref[...] = dka_ref[...].astype(BF16)
            dv_ref[...] = dva_ref[...].astype(BF16)

    nb = SB_W // LANES
    (dq, dk, dv), lands = _ride(dict(
        body=body, name=name, grid=(nb, nq // QPS),
        in_specs=[pl.BlockSpec((QPS * T, LANES), lambda p, i: (i, p)),
                  pl.BlockSpec((s, LANES), lambda p, i: (0, nb + p)),
                  pl.BlockSpec((s, LANES), lambda p, i: (0, 2 * nb + p)),
                  pl.BlockSpec((QPS * T, LANES), lambda p, i: (i, p)),
                  pl.BlockSpec((QPS * T, LANES), lambda p, i: (i, p))],
        out_specs=[pl.BlockSpec((QPS * T, LANES), lambda p, i: (i, p)),
                   pl.BlockSpec((s, LANES), lambda p, i: (0, p)),
                   pl.BlockSpec((s, LANES), lambda p, i: (0, p))],
        out_shape=[jax.ShapeDtypeStruct((s, SB_W), BF16)] * 3,
        scratch_shapes=[pltpu.VMEM((2, T, LANES), F32), pltpu.VMEM((s, LANES), F32),
                        pltpu.VMEM((s, LANES), F32), pltpu.VMEM((2, T, 1), F32),
                        pltpu.VMEM((2, T, 1), F32), pltpu.VMEM((2, T, T), BF16),
                        pltpu.VMEM((2, T, T), BF16)],
        compiler_params=_ARB2, operands=[qkv, qkv, qkv, o, do]), rider)
    return dq, dk, dv, lands


def _fox_bwd(qkv, fqb, frow, fbounds, o, lse, do, name, rider=None):
    s = qkv.shape[0]
    nq = s // T

    def body(q_ref, k_ref, v_ref, fq_ref, fr_ref, fb_ref, o_ref, lse_ref, do_ref,
             dq_ref, dk_ref, dv_ref, df_ref, dqa_ref, dka_ref, dva_ref, dfa_ref, dls_ref, pbs_ref):
        step_id = pl.program_id(1)

        @pl.when(step_id == 0)
        def _():
            dka_ref[...] = jnp.zeros_like(dka_ref)
            dva_ref[...] = jnp.zeros_like(dva_ref)
            dfa_ref[...] = jnp.zeros_like(dfa_ref)

        m0 = _pair_masks()
        causal = _iota2((T, T), 0) >= _iota2((T, T), 1)
        hs = range(2)

        def query_tile(i, rows):
            qh = _split_pair(q_ref[rows, :] * jnp.asarray(Q_SCALE, BF16), m0)
            doh = _split_pair(do_ref[rows, :], m0)
            dsum = _row_dots(do_ref[rows, :], o_ref[rows, :], m0)
            fq = fq_ref[rows, :]
            fqh = (fq[:, 0:1], fq[:, HEAD_DIM:HEAD_DIM + 1])
            lse = lse_ref[rows, :]
            lseh = (lse[:, 0:1], lse[:, HEAD_DIM:HEAD_DIM + 1])
            dqa_ref[...] = jnp.zeros_like(dqa_ref)

            def flush(j):
                off = pl.multiple_of(j * T, T)
                k = k_ref[pl.ds(off, T), :]
                for h in hs:
                    dqa_ref[h] += _dot(dls_ref[h], k)
                dka_ref[pl.ds(off, T), :] += (_dot_tn(dls_ref[0], qh[0])
                                              + _dot_tn(dls_ref[1], qh[1]))
                dva_ref[pl.ds(off, T), :] += (_dot_tn(pbs_ref[0], doh[0])
                                              + _dot_tn(pbs_ref[1], doh[1]))

            def tile(j, diag):
                off = pl.multiple_of(j * T, T)
                k = k_ref[pl.ds(off, T), :]
                v = v_ref[pl.ds(off, T), :]
                sc = [_dot_nt(qh[h], k) + fqh[h] - fr_ref[h:h + 1, pl.ds(off, T)] for h in hs]
                dp = [_dot_nt(doh[h], v) for h in hs]
                if not diag:
                    flush(j + 1)
                p = [jnp.exp(sc[h] - lseh[h]) for h in hs]
                if diag:
                    p = [jnp.where(causal, p[h], 0.0) for h in hs]
                dl = [p[h] * (dp[h] - dsum[h]) for h in hs]
                for h in hs:
                    dls_ref[h] = dl[h].astype(BF16)
                    pbs_ref[h] = p[h].astype(BF16)
                    dfa_ref[h:h + 1, pl.ds(off, T)] -= jnp.sum(dl[h], axis=0, keepdims=True)

            tile(i, True)
            j_left = _fox_tiles_left(i, pl.program_id(0), nq, fb_ref, tile)
            flush(j_left + 1)
            dq_ref[rows, :] = (jnp.where(m0, dqa_ref[0], dqa_ref[1]) * Q_SCALE).astype(BF16)

        for n in range(QPS):
            query_tile(QPS * step_id + n, slice(n * T, (n + 1) * T))

        @pl.when(step_id == nq // QPS - 1)
        def _():
            dk_ref[...] = dka_ref[...].astype(BF16)
            dv_ref[...] = dva_ref[...].astype(BF16)
            df_ref[...] = dfa_ref[...]

    nb = FX_W // LANES
    base = 3 * SB_W // LANES
    (dq, dk, dv, df), lands = _ride(dict(
        body=body, name=name, grid=(nb, nq // QPS),
        in_specs=[pl.BlockSpec((QPS * T, LANES), lambda p, i: (i, base + p)),
                  pl.BlockSpec((s, LANES), lambda p, i: (0, base + nb + p)),
                  pl.BlockSpec((s, LANES), lambda p, i: (0, base + 2 * nb + p)),
                  pl.BlockSpec((QPS * T, LANES), lambda p, i: (i, p)),
                  pl.BlockSpec((None, 2, s), lambda p, i: (p, 0, 0)),
                  pl.BlockSpec(memory_space=pltpu.SMEM),
                  pl.BlockSpec((QPS * T, LANES), lambda p, i: (i, p)),
                  pl.BlockSpec((QPS * T, LANES), lambda p, i: (i, p)),
                  pl.BlockSpec((QPS * T, LANES), lambda p, i: (i, p))],
        out_specs=[pl.BlockSpec((QPS * T, LANES), lambda p, i: (i, p)),
                   pl.BlockSpec((s, LANES), lambda p, i: (0, p)),
                   pl.BlockSpec((s, LANES), lambda p, i: (0, p)),
                   pl.BlockSpec((None, 2, s), lambda p, i: (p, 0, 0))],
        out_shape=[jax.ShapeDtypeStruct((s, FX_W), BF16)] * 3
        + [jax.ShapeDtypeStruct((nb, 2, s), F32)],
        scratch_shapes=[pltpu.VMEM((2, T, LANES), F32), pltpu.VMEM((s, LANES), F32),
                        pltpu.VMEM((s, LANES), F32), pltpu.VMEM((2, s), F32),
                        pltpu.VMEM((2, T, T), BF16), pltpu.VMEM((2, T, T), BF16)],
        compiler_params=_ARB2, operands=[qkv, qkv, qkv, fqb, frow, fbounds, o, lse, do]), rider)
    return dq, dk, dv, df, lands


def _fox_prep_bwd(dfrow, gf, bpad, name):
    s = gf.shape[0]

    def body(df_ref, fl_ref, b_ref, dfl_ref, db_ref):
        tri = jnp.where(_iota2((T, T), 0) <= _iota2((T, T), 1), 1.0, 0.0).astype(BF16)
        carry = jnp.zeros((1, LANES), F32)
        db = jnp.zeros((1, LANES), F32)
        fill = jnp.zeros((LANES - FOX_HEADS, T), F32)
        for blk in reversed(range(s // T)):
            rows = slice(blk * T, (blk + 1) * T)
            c = _tri3(tri, jnp.concatenate([df_ref[:, rows], fill], axis=0), _dot_nt) + carry
            carry = c[0:1, :]
            dfl = c / (1.0 + jnp.exp(fl_ref[rows, :] + b_ref[...]))
            dfl_ref[rows, :] = dfl.astype(BF16)
            db = db + jnp.sum(dfl, axis=0, keepdims=True)
        db_ref[...] = db

    return pl.pallas_call(
        body, name=name, grid=(1,),
        in_specs=[pl.BlockSpec((FOX_HEADS, s), lambda i: (0, 0)),
                  pl.BlockSpec((s, LANES), lambda i: (0, MIX_W // LANES)),
                  pl.BlockSpec((1, LANES), lambda i: (0, 0))],
        out_specs=[pl.BlockSpec((s, LANES), lambda i: (0, 0)),
                   pl.BlockSpec((1, LANES), lambda i: (0, 0))],
        out_shape=[jax.ShapeDtypeStruct((s, LANES), BF16), jax.ShapeDtypeStruct((1, LANES), F32)],
        compiler_params=_ARB1,
    )(dfrow, gf, bpad)


def _mem_bwd(qkv, kv, o, lse, do, name):
    s = qkv.shape[0]
    n = kv.shape[0]

    def body(q_ref, k_ref, v_ref, o_ref, lse_ref, do_ref, dq_ref, dk_ref, dv_ref):
        @pl.when(pl.program_id(1) == 0)
        def _():
            dk_ref[...] = jnp.zeros_like(dk_ref)
            dv_ref[...] = jnp.zeros_like(dv_ref)

        m0 = _pair_masks()
        qh = _split_pair(q_ref[...] * jnp.asarray(Q_SCALE, BF16), m0)
        doh = _split_pair(do_ref[...], m0)
        dsum = _row_dots(do_ref[...], o_ref[...], m0)
        lse = lse_ref[...]
        lseh = (lse[:, 0:1], lse[:, HEAD_DIM:HEAD_DIM + 1])
        k = k_ref[...]
        v = v_ref[...]
        dqs = []
        for h in range(2):
            p = jnp.exp(_dot_nt(qh[h], k) - lseh[h])
            dl = p * (_dot_nt(doh[h], v) - dsum[h])
            dlb = dl.astype(BF16)
            dqs.append(_dot(dlb, k))
            dk_ref[...] += _dot_tn(dlb, qh[h])
            dv_ref[...] += _dot_tn(p.astype(BF16), doh[h])
        dq_ref[...] = (jnp.where(m0, dqs[0], dqs[1]) * Q_SCALE).astype(BF16)

    nb = MEM_W // LANES
    base = (3 * SB_W + 3 * FX_W) // LANES
    return pl.pallas_call(
        body, name=name, grid=(nb, s // TQM),
        in_specs=[pl.BlockSpec((TQM, LANES), lambda p, i: (i, base + p)),
                  pl.BlockSpec((n, LANES), lambda p, i: (0, p)),
                  pl.BlockSpec((n, LANES), lambda p, i: (0, nb + p)),
                  pl.BlockSpec((TQM, LANES), lambda p, i: (i, p)),
                  pl.BlockSpec((TQM, LANES), lambda p, i: (i, p)),
                  pl.BlockSpec((TQM, LANES), lambda p, i: (i, p))],
        out_specs=[pl.BlockSpec((TQM, LANES), lambda p, i: (i, p)),
                   pl.BlockSpec((n, LANES), lambda p, i: (0, p)),
                   pl.BlockSpec((n, LANES), lambda p, i: (0, p))],
        out_shape=[jax.ShapeDtypeStruct((s, MEM_W), BF16), jax.ShapeDtypeStruct((n, MEM_W), F32),
                   jax.ShapeDtypeStruct((n, MEM_W), F32)],
        compiler_params=_ARB2,
    )(qkv, kv, kv, o, lse, do)


def _memkv_bwd(mem, mnw, wkv, dk, dv, name):
    n = mem.shape[0]

    def body(mem_ref, mnw_ref, w_ref, dk_ref, dv_ref, dw_ref, dmnw_ref):
        mv = mem_ref[...]
        r = lax.rsqrt(jnp.mean(mv * mv, axis=-1, keepdims=True) + EPS)
        mh = mv * r
        hm = (mh * mnw_ref[...]).astype(BF16)
        dkv = jnp.concatenate([dk_ref[...], dv_ref[...]], axis=1).astype(BF16)
        dw_ref[...] = _dot_tn(hm, dkv)
        dhm = _dot_nt(dkv, w_ref[...])
        dmnw_ref[...] = jnp.sum(dhm * mh, axis=0, keepdims=True)

    return pl.pallas_call(
        body, name=name, grid=(1,),
        in_specs=[pl.BlockSpec((n, D_MODEL), lambda i: (0, 0)),
                  pl.BlockSpec((1, D_MODEL), lambda i: (0, 0)),
                  pl.BlockSpec((D_MODEL, 2 * MEM_W), lambda i: (0, 0)),
                  pl.BlockSpec((n, MEM_W), lambda i: (0, 0)),
                  pl.BlockSpec((n, MEM_W), lambda i: (0, 0))],
        out_specs=[pl.BlockSpec((D_MODEL, 2 * MEM_W), lambda i: (0, 0)),
                   pl.BlockSpec((1, D_MODEL), lambda i: (0, 0))],
        out_shape=[jax.ShapeDtypeStruct((D_MODEL, 2 * MEM_W), F32),
                   jax.ShapeDtypeStruct((1, D_MODEL), F32)],
        compiler_params=_ARB1,
    )(mem, mnw, wkv, dk, dv)


def _inproj_bwd_dx(pieces, w_r, x, nw, dxo, name, rider=None):
    s = x.shape[0]
    n = len(pieces)
    widths = [p.shape[1] for p in pieces]

    def body(*refs):
        piece_refs = refs[:n]
        w_ref, x_ref, nw_ref, dxo_ref, dx_ref, h_ref, dnw_ref, dp_ref = refs[n:]

        @pl.when(pl.program_id(0) == 0)
        def _():
            dnw_ref[...] = jnp.zeros_like(dnw_ref)

        col = 0
        for r, wd in zip(piece_refs, widths):
            dp_ref[:, col:col + wd] = r[...]
            col += wd
        dp_ref[:, col:] = jnp.zeros((TM, WR_W - col), BF16)
        dh = _dot(dp_ref[...], w_ref[...])
        xv = x_ref[...]
        nw = nw_ref[...]
        r = lax.rsqrt(jnp.mean(xv * xv, axis=-1, keepdims=True) + EPS)
        xh = xv * r
        h_ref[...] = (xh * nw).astype(BF16)
        dnw_ref[...] += jnp.sum(dh * xh, axis=0, keepdims=True)
        dxh = dh * nw
        dx_ref[...] = r * (dxh - xh * jnp.mean(dxh * xh, axis=-1, keepdims=True)) + dxo_ref[...]

    (dx, h, dnw, dproj), lands = _ride(dict(
        body=body, name=name, grid=(s // TM,),
        in_specs=[pl.BlockSpec((TM, wd), lambda i: (i, 0)) for wd in widths]
        + [pl.BlockSpec((WR_W, D_MODEL), lambda i: (0, 0)),
           pl.BlockSpec((TM, D_MODEL), lambda i: (i, 0)),
           pl.BlockSpec((1, D_MODEL), lambda i: (0, 0)),
           pl.BlockSpec((TM, D_MODEL), lambda i: (i, 0))],
        out_specs=[pl.BlockSpec((None, TM, D_MODEL), lambda i: (0, i, 0)),
                   pl.BlockSpec((TM, D_MODEL), lambda i: (i, 0)),
                   pl.BlockSpec((1, D_MODEL), lambda i: (0, 0)),
                   pl.BlockSpec((TM, WR_W), lambda i: (i, 0))],
        out_shape=[jax.ShapeDtypeStruct((1, s, D_MODEL), F32), jax.ShapeDtypeStruct((s, D_MODEL), BF16),
                   jax.ShapeDtypeStruct((1, D_MODEL), F32), jax.ShapeDtypeStruct((s, WR_W), BF16)],
        scratch_shapes=[], compiler_params=_ARB1, operands=[*pieces, w_r, x, nw, dxo]), rider)
    return dx, h, dnw, dproj, lands


def _inproj_bwd_dw(h, dproj, name):
    s = dproj.shape[0]
    tn = 256

    def body(h_ref, dp_ref, dw_ref):
        dw_ref[...] = _dot_tn(dp_ref[...], h_ref[...]).astype(BF16)

    return pl.pallas_call(
        body, name=name, grid=(WR_W // tn,),
        in_specs=[pl.BlockSpec((s, D_MODEL), lambda j: (0, 0)),
                  pl.BlockSpec((s, tn), lambda j: (0, j))],
        out_specs=pl.BlockSpec((tn, D_MODEL), lambda j: (j, 0)),
        out_shape=jax.ShapeDtypeStruct((WR_W, D_MODEL), BF16),
        compiler_params=_ARB1,
    )(h, dproj)


def _rearrange_w_in(wt):
    pad = jnp.zeros((FL_PAD - FOX_HEADS,) + wt.shape[1:], wt.dtype)
    return jnp.concatenate([wt[:3072], wt[3080:3336], wt[3336:IN_W], wt[3072:3080], pad], axis=0)


_W_IN_SEGMENTS = [(0, 3072, 0), (3072, 3080, QKV_W + MIX_W), (3080, 3336, 3072), (3336, IN_W, QKV_W)]
ASSEMBLE_COLS = 256
ASSEMBLE_ROWS = 128


def _shard_pieces(k):
    lo, hi = k * SHARD_W, (k + 1) * SHARD_W
    return [(max(lo, a) - lo, at + max(lo, a) - a, min(hi, b) - max(lo, a))
            for a, b, at in _W_IN_SEGMENTS if max(lo, a) < min(hi, b)]


def _assemble_w_in(v, to_blocks, name):
    used = QKV_W + MIX_W + FOX_HEADS

    def body(v_ref, o_ref):
        for k in range(N_DEV):
            for at, w_at, n in _shard_pieces(k):
                for r in range(0, n, ASSEMBLE_ROWS):
                    m = min(ASSEMBLE_ROWS, n - r)
                    if to_blocks:
                        o_ref[k, at + r:at + r + m, :] = v_ref[w_at + r:w_at + r + m, :]
                    else:
                        o_ref[w_at + r:w_at + r + m, :] = v_ref[k, at + r:at + r + m, :]
        if not to_blocks:
            o_ref[used:, :] = jnp.zeros((WR_W - used, ASSEMBLE_COLS), BF16)

    blocks = pl.BlockSpec((N_DEV, SHARD_W, ASSEMBLE_COLS), lambda j: (0, 0, j))
    rows = pl.BlockSpec((WR_W, ASSEMBLE_COLS), lambda j: (0, j))
    out_shape = (N_DEV, SHARD_W, D_MODEL) if to_blocks else (WR_W, D_MODEL)
    return pl.pallas_call(
        body, name=name, grid=(D_MODEL // ASSEMBLE_COLS,),
        in_specs=[rows if to_blocks else blocks], out_specs=blocks if to_blocks else rows,
        out_shape=jax.ShapeDtypeStruct(out_shape, BF16), compiler_params=_ARB1)(v)


def _restore_w_in(g):
    gate0 = QKV_W
    fl0 = QKV_W + MIX_W
    return jnp.concatenate(
        [g[:3072], g[fl0:fl0 + FOX_HEADS], g[3072:QKV_W], g[gate0:fl0]], axis=0)


def _pad_lanes(v, width=LANES):
    return jnp.pad(v, (0, width - v.shape[0])).reshape(1, width)


def _layer_fwd(xs, mem, nw, w_r, b_forget, mnw, late, onw, l, travel=None):
    s = xs.shape[0]
    bpad = _pad_lanes(b_forget)
    travel = _Travel(travel)
    qkv, gf, _ = travel.ride(2, _inproj_fwd, xs, nw, w_r, f"inproj_fwd_{l}")
    fqb, frow, fbounds = _fox_prep_fwd(gf, qkv, bpad, f"fox_prep_fwd_{l}")
    frow = frow.reshape(FOX_HEADS // 2, 2, s)
    ysb, _ = travel.ride(0, _sb_fwd, qkv, f"sb_fwd_{l}")
    yfx, lse_fx, _ = travel.ride(1, _fox_fwd, qkv, fqb, frow, fbounds, f"fox_fwd_{l}")
    wkv, wout = late(travel.lands)
    kv = _memkv_fwd(mem, mnw, wkv, f"memkv_fwd_{l}")
    ym, lse_m = _mem_fwd(qkv, kv, f"mem_fwd_{l}")
    xn = _outproj_fwd(ysb, yfx, ym, gf, onw, wout, xs, f"outproj_fwd_{l}")
    saved = (xs, nw, mnw, onw, bpad, qkv, gf, fqb, frow, fbounds, ysb, yfx, lse_fx, kv, ym, lse_m)
    return xn, saved, travel.lands, (wkv, wout)


class _Travel:
    def __init__(self, plan):
        self.plan = plan
        self.lands = None if plan is None else _new_lands(plan[0], plan[1])

    def ride(self, n, fn, *args):
        if self.plan is None or n >= len(self.plan[2]) or self.plan[2][n] is None:
            return fn(*args)
        srcs, scatter, legs = self.plan
        idx, rows = legs[n]
        out = fn(*args, rider=_Rider([srcs[a] for a in idx], [self.lands[a] for a in idx],
                                     scatter, rows, relay=not scatter))
        for a, land in zip(idx, out[-1]):
            self.lands[a] = land
        return out


def _layer_bwd(dx, saved, mem, w_r, wkv, wout, l, travel=None, own_w_out=None):
    xs, nw, mnw, onw, bpad, qkv, gf, fqb, frow, fbounds, ysb, yfx, lse_fx, kv, ym, lse_m = saved
    s = xs.shape[0]
    travel = _Travel(travel)
    dysb, dyfx, dym, dgate, dwout, donw, _ = travel.ride(
        3, _outproj_bwd, dx, wout, ysb, yfx, ym, gf, onw, f"outproj_bwd_{l}")
    if own_w_out is not None:
        travel.plan[0][own_w_out[0]] = own_w_out[1](dwout)
    sdq, sdk, sdv, _ = travel.ride(0, _sb_bwd, qkv, ysb, dysb, f"sb_bwd_{l}")
    fdq, fdk, fdv, dfrow, _ = travel.ride(1, _fox_bwd, qkv, fqb, frow, fbounds, yfx, lse_fx, dyfx,
                                          f"fox_bwd_{l}")
    dfl, db = _fox_prep_bwd(dfrow.reshape(FOX_HEADS, s), gf, bpad, f"fox_prep_bwd_{l}")
    dmq, dmk, dmv = _mem_bwd(qkv, kv, ym, lse_m, dym, f"mem_bwd_{l}")
    dwkv, dmnw = _memkv_bwd(mem, mnw, wkv, dmk, dmv, f"memkv_bwd_{l}")
    dx, ht, dnw, dproj, _ = travel.ride(2, _inproj_bwd_dx,
                                        [sdq, sdk, sdv, fdq, fdk, fdv, dmq, dgate, dfl],
                                        w_r, xs, nw, dx, f"inproj_bwd_dx_{l}")
    dwr = _inproj_bwd_dw(ht, dproj, f"inproj_bwd_dw_{l}")
    grads = dict(norm_w=dnw[0], w_r=dwr, b_forget=db[0, :FOX_HEADS], mem_norm_w=dmnw[0],
                 w_mem_kv=dwkv, out_norm_w=donw[0], w_out=dwout)
    return dx, grads, travel.lands


_ANY = pl.BlockSpec(memory_space=pl.ANY)


def _my_place():
    return lax.axis_index("x"), lax.axis_index("y"), lax.axis_index("c")


def _flip(v, bit):
    return 1 - v if bit else v


def _block_index(px, py, pc):
    return 4 * px + 2 * py + pc


def _all_gather_weights(shards, name):
    n = len(shards)

    def body(*refs):
        ins, outs = refs[:n], refs[n:2 * n]
        send_sems, recv_sems, local_sems = refs[2 * n:]
        x, y, c = _my_place()
        me = (x, y, c)
        sibling = (x, y, 1 - c)
        chips = [(1 - x, y), (x, 1 - y), (1 - x, 1 - y)]

        def copy(a, k, block, to, src=None):
            dst = outs[a].at[_block_index(*block)]
            return pltpu.make_async_remote_copy(
                src_ref=dst if src is None else src, dst_ref=dst,
                send_sem=send_sems.at[a, k], recv_sem=recv_sems.at[a, k],
                device_id=to, device_id_type=pl.DeviceIdType.MESH)

        mine = [pltpu.make_async_copy(ins[a], outs[a].at[_block_index(*me)], local_sems.at[a])
                for a in range(n)]
        for cp in mine:
            cp.start()
        first = []
        for a in range(n):
            first.append(copy(a, 0, me, sibling, src=ins[a]))
            first += [copy(a, 1 + j, me, (*chip, c), src=ins[a]) for j, chip in enumerate(chips)]
        for cp in first:
            cp.start()
        passed = []
        for j, chip in enumerate(chips):
            for a in range(n):
                copy(a, 1 + j, (*chip, c), me).wait_recv()
                fwd = copy(a, 4 + j, (*chip, c), sibling)
                fwd.start()
                passed.append(fwd)
        for a in range(n):
            copy(a, 0, sibling, me).wait_recv()
            for j, chip in enumerate(chips):
                copy(a, 4 + j, (*chip, 1 - c), me).wait_recv()
        for cp in first + passed:
            cp.wait_send()
        for cp in mine:
            cp.wait()

    return pl.pallas_call(
        body, name=name,
        in_specs=[_ANY] * n, out_specs=[_ANY] * n,
        out_shape=[jax.ShapeDtypeStruct((N_DEV,) + v.shape, v.dtype) for v in shards],
        scratch_shapes=[pltpu.SemaphoreType.DMA((n, 7)), pltpu.SemaphoreType.DMA((n, 7)),
                        pltpu.SemaphoreType.DMA((n,))],
    )(*shards)


def _exchange_blocks(blocked, name):
    n = len(blocked)

    def body(*refs):
        ins, outs = refs[:n], refs[n:2 * n]
        send_sems, recv_sems, local_sems = refs[2 * n:]
        x, y, c = _my_place()
        mine_idx = _block_index(x, y, c)
        local = [pltpu.make_async_copy(ins[a].at[mine_idx], outs[a].at[mine_idx], local_sems.at[a])
                 for a in range(n)]
        for cp in local:
            cp.start()
        sends, arrivals = [], []
        for r in range(1, N_DEV):
            peer = (_flip(x, r & 4), _flip(y, r & 2), _flip(c, r & 1))
            peer_idx = _block_index(*peer)
            for a in range(n):
                sems = dict(send_sem=send_sems.at[a, r - 1], recv_sem=recv_sems.at[a, r - 1],
                            device_id=peer, device_id_type=pl.DeviceIdType.MESH)
                sends.append(pltpu.make_async_remote_copy(
                    src_ref=ins[a].at[peer_idx], dst_ref=outs[a].at[mine_idx], **sems))
                arrivals.append(pltpu.make_async_remote_copy(
                    src_ref=ins[a].at[peer_idx], dst_ref=outs[a].at[peer_idx], **sems))
        for cp in sends:
            cp.start()
        for cp in arrivals:
            cp.wait_recv()
        for cp in sends:
            cp.wait_send()
        for cp in local:
            cp.wait()

    return pl.pallas_call(
        body, name=name,
        in_specs=[_ANY] * n, out_specs=[_ANY] * n,
        out_shape=[jax.ShapeDtypeStruct(v.shape, v.dtype) for v in blocked],
        scratch_shapes=[pltpu.SemaphoreType.DMA((n, 7)), pltpu.SemaphoreType.DMA((n, 7)),
                        pltpu.SemaphoreType.DMA((n,))],
    )(*blocked)


N_CHIP = N_DEV // 2


def _pair_swap(blocked, name):
    n = len(blocked)

    def body(*refs):
        ins, outs = refs[:n], refs[n:2 * n]
        send_sems, recv_sems = refs[2 * n:]
        x, y, c = _my_place()
        copies = [pltpu.make_async_remote_copy(
            src_ref=ins[a].at[j, 1 - c], dst_ref=outs[a].at[j],
            send_sem=send_sems.at[N_CHIP * a + j], recv_sem=recv_sems.at[N_CHIP * a + j],
            device_id=(x, y, 1 - c), device_id_type=pl.DeviceIdType.MESH)
            for a in range(n) for j in range(N_CHIP)]
        for cp in copies:
            cp.start()
        for cp in copies:
            cp.wait_recv()
        for cp in copies:
            cp.wait_send()

    return pl.pallas_call(
        body, name=name,
        in_specs=[_ANY] * n, out_specs=[_ANY] * n,
        out_shape=[jax.ShapeDtypeStruct((N_CHIP,) + v.shape[2:], v.dtype) for v in blocked],
        scratch_shapes=[pltpu.SemaphoreType.DMA((N_CHIP * n,)),
                        pltpu.SemaphoreType.DMA((N_CHIP * n,))],
    )(*blocked)


def _pair_add(by_core, theirs, core, name):
    n = len(by_core)

    def body(core_ref, *refs):
        for a in range(n):
            mine, got, out = refs[a], refs[n + a], refs[2 * n + a]
            out[...] = (mine[...].astype(F32) + got[...].astype(F32)).astype(BF16)

    def own(v):
        return pl.BlockSpec((None, None) + v.shape[2:], lambda j, core_ref: (j, core_ref[0], 0, 0))

    def block(v):
        return pl.BlockSpec((None,) + v.shape[1:], lambda j, core_ref: (j, 0, 0))

    return list(pl.pallas_call(
        body, name=name,
        grid_spec=pltpu.PrefetchScalarGridSpec(
            num_scalar_prefetch=1, grid=(N_CHIP,),
            in_specs=[own(v) for v in by_core] + [block(v) for v in theirs],
            out_specs=[block(v) for v in theirs]),
        out_shape=[jax.ShapeDtypeStruct(v.shape, BF16) for v in theirs],
        compiler_params=_ARB1,
    )(core, *by_core, *theirs))


def _chip_exchange(by_chip, to_all, name):
    n, m = len(by_chip), len(to_all)

    def body(*refs):
        ins, alls = refs[:n], refs[n:n + m]
        outs, all_outs = refs[n + m:2 * n + m], refs[2 * n + m:2 * (n + m)]
        send_sems, recv_sems, local_sems = refs[2 * (n + m):]
        x, y, c = _my_place()
        my_chip, mine_idx = 2 * x + y, _block_index(x, y, c)
        local = [pltpu.make_async_copy(ins[a].at[my_chip], outs[a].at[my_chip], local_sems.at[a])
                 for a in range(n)]
        local += [pltpu.make_async_copy(alls[b].at[mine_idx], all_outs[b].at[mine_idx],
                                        local_sems.at[n + b]) for b in range(m)]
        for cp in local:
            cp.start()
        sends, arrivals = [], []
        k = 0
        for r in range(1, N_DEV):
            peer = (_flip(x, r & 4), _flip(y, r & 2), _flip(c, r & 1))
            peer_chip, peer_idx = 2 * peer[0] + peer[1], _block_index(*peer)
            pairs = [(alls[b].at[mine_idx], all_outs[b].at[mine_idx], all_outs[b].at[peer_idx])
                     for b in range(m)]
            if not r & 1:
                pairs += [(ins[a].at[peer_chip], outs[a].at[my_chip], outs[a].at[peer_chip])
                          for a in range(n)]
            for src, there, here in pairs:
                sems = dict(send_sem=send_sems.at[k], recv_sem=recv_sems.at[k], device_id=peer,
                            device_id_type=pl.DeviceIdType.MESH)
                sends.append(pltpu.make_async_remote_copy(src_ref=src, dst_ref=there, **sems))
                arrivals.append(pltpu.make_async_remote_copy(src_ref=src, dst_ref=here, **sems))
                k += 1
        for cp in sends:
            cp.start()
        for cp in arrivals:
            cp.wait_recv()
        for cp in sends:
            cp.wait_send()
        for cp in local:
            cp.wait()

    n_copies = 7 * m + 3 * n
    return pl.pallas_call(
        body, name=name,
        in_specs=[_ANY] * (n + m), out_specs=[_ANY] * (n + m),
        out_shape=[jax.ShapeDtypeStruct(v.shape, v.dtype) for v in by_chip + to_all],
        scratch_shapes=[pltpu.SemaphoreType.DMA((n_copies,)), pltpu.SemaphoreType.DMA((n_copies,)),
                        pltpu.SemaphoreType.DMA((n + m,))],
    )(*by_chip, *to_all)


class _Rider(NamedTuple):
    srcs: list
    lands: list
    scatter: bool
    part: list
    relay: bool = False


def _window(ref, part):
    if part is None:
        return ref
    dim, start, size = part
    return ref.at[(slice(None),) * dim + (pl.ds(start, size),)]


def _relay_copies(srcs, lands, send_sems, recv_sems, rider):
    x, y, c = _my_place()
    me, sibling = (x, y, c), (x, y, 1 - c)
    chips = [(1 - x, y), (x, 1 - y), (1 - x, 1 - y)]
    first, from_chips, passed, last = [], [], [], []
    for a in range(len(srcs)):
        def copy(k, block, to, src=None, a=a):
            dst = _window(lands[a].at[_block_index(*block)], rider.part[a])
            return pltpu.make_async_remote_copy(
                src_ref=dst if src is None else src, dst_ref=dst,
                send_sem=send_sems.at[7 * a + k], recv_sem=recv_sems.at[7 * a + k],
                device_id=to, device_id_type=pl.DeviceIdType.MESH)

        mine = _window(srcs[a], rider.part[a])
        first.append(copy(0, me, sibling, src=mine))
        first += [copy(1 + j, me, (*chip, c), src=mine) for j, chip in enumerate(chips)]
        from_chips += [copy(1 + j, (*chip, c), me) for j, chip in enumerate(chips)]
        passed += [copy(4 + j, (*chip, c), sibling) for j, chip in enumerate(chips)]
        last.append(copy(0, sibling, me))
        last += [copy(4 + j, (*chip, 1 - c), me) for j, chip in enumerate(chips)]
    return first, from_chips, passed, last


def _new_lands(srcs, scatter):
    return [lax.empty(v.shape if scatter else (N_DEV,) + v.shape, v.dtype) for v in srcs]


def _rider_copies(srcs, lands, send_sems, recv_sems, rider):
    x, y, c = _my_place()
    mine_idx = _block_index(x, y, c)

    def window(ref, a):
        return _window(ref, rider.part[a])

    sends, arrivals = [], []
    for r in range(1, N_DEV):
        peer = (_flip(x, r & 4), _flip(y, r & 2), _flip(c, r & 1))
        peer_idx = _block_index(*peer)
        for a in range(len(srcs)):
            src = window(srcs[a].at[peer_idx] if rider.scatter else srcs[a], a)
            k = 7 * a + r - 1
            sems = dict(send_sem=send_sems.at[k], recv_sem=recv_sems.at[k],
                        device_id=peer, device_id_type=pl.DeviceIdType.MESH)
            sends.append(pltpu.make_async_remote_copy(
                src_ref=src, dst_ref=window(lands[a].at[mine_idx], a), **sems))
            arrivals.append(pltpu.make_async_remote_copy(
                src_ref=src, dst_ref=window(lands[a].at[peer_idx], a), **sems))
    return sends, arrivals


def _ride(call, rider):
    call = dict(call)
    body, grid = call.pop("body"), call["grid"]
    operands = call.pop("operands")
    if rider is None:
        return list(pl.pallas_call(body, **call)(*operands)), None
    n_in, n_out = len(call["in_specs"]), len(call["out_specs"])
    n_scratch = len(call["scratch_shapes"])
    m = len(rider.srcs)

    def riding(*refs):
        main_in, srcs, lands = refs[:n_in], refs[n_in:n_in + m], refs[n_in + m:n_in + 2 * m]
        main_out = refs[n_in + 2 * m:n_in + 2 * m + n_out]
        lands_out = refs[n_in + 2 * m + n_out:n_in + 3 * m + n_out]
        rest = refs[n_in + 3 * m + n_out:]
        send_sems, recv_sems, own_sems = rest[n_scratch:]
        me = _block_index(*_my_place())
        own = [pltpu.make_async_copy(
            _window(srcs[a].at[me] if rider.scatter else srcs[a], rider.part[a]),
            _window(lands_out[a].at[me], rider.part[a]), own_sems.at[a]) for a in range(m)]
        at = [pl.program_id(d) for d in range(len(grid))]
        first = functools.reduce(jnp.logical_and, [p == 0 for p in at])
        last = functools.reduce(jnp.logical_and, [p == g - 1 for p, g in zip(at, grid)])
        if rider.relay:
            sends, from_chips, passed, arrivals = _relay_copies(
                srcs, lands, send_sems, recv_sems, rider)
            step, steps = 0, 1
            for p, g in zip(at, grid):
                step, steps = step * g + p, steps * g
            assert steps >= 2, "a relayed gather needs a later grid step to pass blocks on"

            @pl.when(step == (7 * steps) // 8)
            def _():
                for cp in from_chips:
                    cp.wait_recv()
                for cp in passed:
                    cp.start()
        else:
            sends, arrivals = _rider_copies(srcs, lands, send_sems, recv_sems, rider)
            passed = []

        @pl.when(first)
        def _():
            for cp in sends + own:
                cp.start()

        body(*main_in, *main_out, *rest[:n_scratch])

        @pl.when(last)
        def _():
            for cp in arrivals:
                cp.wait_recv()
            for cp in sends + passed:
                cp.wait_send()
            for cp in own:
                cp.wait()

    call["in_specs"] = list(call["in_specs"]) + [_ANY] * (2 * m)
    call["out_specs"] = list(call["out_specs"]) + [_ANY] * m
    call["out_shape"] = list(call["out_shape"]) + [
        jax.ShapeDtypeStruct(v.shape, v.dtype) for v in rider.lands]
    call["scratch_shapes"] = list(call["scratch_shapes"]) + [
        pltpu.SemaphoreType.DMA((7 * m,)), pltpu.SemaphoreType.DMA((7 * m,)),
        pltpu.SemaphoreType.DMA((m,))]
    call["input_output_aliases"] = {n_in + m + a: n_out + a for a in range(m)}
    outs = pl.pallas_call(riding, **call)(*operands, *rider.srcs, *rider.lands)
    return list(outs[:n_out]), list(outs[n_out:])


def _sum_parts(p_ref):
    g = p_ref[0].astype(F32)
    for k in range(1, p_ref.shape[0]):
        g = g + p_ref[k].astype(F32)
    return g


def _adamw(g, w, m, v):
    c1 = 1.0 / (1.0 - ADAM_B1 ** ADAM_STEP)
    c2 = 1.0 / (1.0 - ADAM_B2 ** ADAM_STEP)
    nm = ADAM_B1 * m + (1.0 - ADAM_B1) * g
    nv = ADAM_B2 * v + (1.0 - ADAM_B2) * (g * g)
    return nm, nv, -ADAM_LR * ((nm * c1) / (jnp.sqrt(nv * c2) + ADAM_EPS) + ADAM_WD * w)


def _adamw_w_in(parts, w, m, v, name):
    ncol_blk, depth, nfeat = w.shape
    cols = 256

    def body(*refs):
        p_refs = refs[:depth]
        w_ref, m_ref, v_ref, g_ref, d_ref, nm_ref, nv_ref = refs[depth:]
        g = jnp.stack([_sum_parts(p_refs[l]) for l in range(depth)], axis=1)
        nm, nv, d = _adamw(g, w_ref[...], m_ref[...], v_ref[...])
        g_ref[...] = g
        nm_ref[...] = nm
        nv_ref[...] = nv
        d_ref[...] = d

    blk = pl.BlockSpec((ncol_blk, depth, cols), lambda j: (0, 0, j))
    return pl.pallas_call(
        body, name=name, grid=(nfeat // cols,),
        in_specs=[pl.BlockSpec((p.shape[0], ncol_blk, cols), lambda j: (0, 0, j)) for p in parts]
        + [blk] * 3,
        out_specs=[blk] * 4,
        out_shape=[jax.ShapeDtypeStruct(w.shape, F32)] * 4,
        compiler_params=_ARB1,
    )(*parts, w, m, v)


def _adamw_sum(parts, w, m, v, tile, name):
    depth, nrow, ncol = w.shape
    rows, cols = tile

    def body(*refs):
        p_refs = refs[:depth]
        w_ref, m_ref, v_ref, g_ref, d_ref, nm_ref, nv_ref = refs[depth:]
        layer = pl.program_id(0)
        for l in range(depth):
            @pl.when(layer == l)
            def _(p_ref=p_refs[l]):
                g = _sum_parts(p_ref)
                nm, nv, d = _adamw(g, w_ref[...], m_ref[...], v_ref[...])
                g_ref[...] = g
                nm_ref[...] = nm
                nv_ref[...] = nv
                d_ref[...] = d

    def part_spec(l):
        return pl.BlockSpec((parts[l].shape[0], rows, cols), lambda q, i, j: (
            0, jnp.where(q == l, i, 0), jnp.where(q == l, j, 0)))

    blk = pl.BlockSpec((None, rows, cols), lambda q, i, j: (q, i, j))
    return pl.pallas_call(
        body, name=name, grid=(depth, nrow // rows, ncol // cols),
        in_specs=[part_spec(l) for l in range(depth)] + [blk, blk, blk],
        out_specs=[blk] * 4,
        out_shape=[jax.ShapeDtypeStruct(w.shape, F32)] * 4,
        compiler_params=pltpu.CompilerParams(
            dimension_semantics=("arbitrary", "arbitrary", "arbitrary")),
    )(*parts, w, m, v)


def _adamw_small(parts, ws, ms, vs, name):
    n = len(parts)

    def body(*refs):
        p, w, m, v = (refs[a * n:(a + 1) * n] for a in range(4))
        out = refs[4 * n:]
        for a in range(n):
            g = _sum_parts(p[a])
            nm, nv, d = _adamw(g, w[a][...], m[a][...], v[a][...])
            out[a][...] = g
            out[n + a][...] = d
            out[2 * n + a][...] = nm
            out[3 * n + a][...] = nv

    vmem = pl.BlockSpec(memory_space=pltpu.VMEM)
    outs = pl.pallas_call(
        body, name=name, in_specs=[vmem] * (4 * n), out_specs=[vmem] * (4 * n),
        out_shape=[jax.ShapeDtypeStruct(w.shape, F32) for w in ws] * 4,
    )(*parts, *ws, *ms, *vs)
    return [list(outs[a * n:(a + 1) * n]) for a in range(4)]


def _misc_rows(b_forget, extra=None):
    tile = jnp.pad(b_forget, ((0, 6), (0, LANES - FOX_HEADS)))
    return tile if extra is None else tile.at[2].set(extra)


def kernel(x, mem, norm_w, w_in, b_forget, mem_norm_w, w_mem_kv, out_norm_w, w_out, final_norm_w, loss_target, m_norm_w, m_w_in, m_b_forget, m_mem_norm_w, m_w_mem_kv, m_out_norm_w, m_w_out, m_final_norm_w, v_norm_w, v_w_in, v_b_forget, v_mem_norm_w, v_w_mem_kv, v_out_norm_w, v_w_out, v_final_norm_w):
    kv_rows = w_mem_kv.shape[1]
    out_rows = w_out.shape[1]

    def shards(l):
        return [w_in[l].T.astype(BF16), w_mem_kv[l].astype(BF16), w_out[l].astype(BF16)]

    def full_kv_out(g_kv, g_out):
        return g_kv.reshape(D_MODEL, 2 * MEM_W), g_out.reshape(MIX_W, D_MODEL)

    def kv_blocks(g):
        return g.reshape(N_DEV, kv_rows, 2 * MEM_W).astype(BF16)

    def out_blocks(g):
        return g.reshape(N_DEV, out_rows, D_MODEL).astype(BF16)

    def row(v):
        return v.reshape(1, -1)

    def cols(first, size):
        return (1, first, size)

    fwd_split, bwd_split = 4 * LANES, (5 * LANES, 6 * LANES)

    s_in0, s_kv0, s_out0 = shards(0)
    (g_in0,) = _all_gather_weights([s_in0], "all_gather_l0")
    w_r0 = _assemble_w_in(g_in0, False, "assemble_w_in_0")
    s_in1, s_kv1, s_out1 = shards(1)
    x1, saved0, (l_in1, _, _), (wkv0, wout0) = _layer_fwd(
        x[0], mem[0], row(norm_w[0]), w_r0, b_forget[0], row(mem_norm_w[0]),
        lambda lands: full_kv_out(lands[1], lands[2]),
        row(out_norm_w[0]), 0,
        travel=([s_in1, s_kv0, s_out0], False,
                [([0], [cols(0, fwd_split)]), ([0], [cols(fwd_split, D_MODEL - fwd_split)]),
                 ([1, 2], [None, None])]))
    w_r1 = _assemble_w_in(l_in1, False, "assemble_w_in_1")
    x2, saved1, _, (wkv1, wout1) = _layer_fwd(
        x1, mem[0], row(norm_w[1]), w_r1, b_forget[1], row(mem_norm_w[1]),
        lambda lands: full_kv_out(lands[0], lands[1]),
        row(out_norm_w[1]), 1, travel=([s_kv1, s_out1], False, [([0, 1], [None, None]), None]))

    dx2, loss_part, dfnw = _final_fwd_bwd(x2, row(final_norm_w), loss_target[0], "final_fwd_bwd")

    dw_out_blocks = jax.ShapeDtypeStruct((N_DEV, out_rows, D_MODEL), BF16)
    dx1, gr1, (l_out1,) = _layer_bwd(
        dx2, saved1, mem[0], w_r1, wkv1, wout1, 1,
        travel=([dw_out_blocks], True, [([0], [None])]), own_w_out=(0, out_blocks))
    p_in1 = _assemble_w_in(gr1["w_r"], True, "w_in_grad_blocks_1")
    p_kv1 = kv_blocks(gr1["w_mem_kv"])
    grad_x, gr0, (l_in1, l_kv1, l_out0) = _layer_bwd(
        dx1[0], saved0, mem[0], w_r0, wkv0, wout0, 0,
        travel=([p_in1, p_kv1, dw_out_blocks], True,
                [([0, 1], [cols(0, bwd_split[0]), None]),
                 ([0, 2], [cols(bwd_split[0], bwd_split[1] - bwd_split[0]), None]),
                 None,
                 ([0], [cols(bwd_split[1], D_MODEL - bwd_split[1])])]),
        own_w_out=(2, out_blocks))
    r_out1, r_in1, r_kv1, r_out0 = l_out1, l_in1, l_kv1, l_out0

    def both(name):
        return jnp.stack([gr0[name], gr1[name]])

    small = [both("norm_w"), both("mem_norm_w"), both("out_norm_w"), dfnw,
             _misc_rows(both("b_forget"), loss_part[0])]
    p_small = [jnp.broadcast_to(v[None], (N_DEV,) + v.shape) for v in small]
    by_core = [v.reshape((N_CHIP, 2) + v.shape[1:])
               for v in (_assemble_w_in(gr0["w_r"], True, "w_in_grad_blocks_0"),
                         kv_blocks(gr0["w_mem_kv"]))]
    from_sibling = _pair_swap(by_core, "grads_l0_pair_swap")
    core = lax.axis_index("c").astype(jnp.int32).reshape(1)
    chip_sums = _pair_add(by_core, from_sibling, core, "grads_l0_pair_add")
    r_in0, r_kv0, *r_small = _chip_exchange(chip_sums, p_small, "exchange_grads_l0")

    def view(v):
        return jnp.transpose(v, (2, 0, 1))

    g_w_in, d_w_in, nm_w_in, nv_w_in = [jnp.transpose(v, (1, 2, 0)) for v in _adamw_w_in(
        [r_in0, r_in1], view(w_in), view(m_w_in), view(v_w_in), "adamw_w_in")]
    g_w_kv, d_w_kv, nm_w_kv, nv_w_kv = _adamw_sum(
        [r_kv0, r_kv1], w_mem_kv, m_w_mem_kv, v_w_mem_kv, (kv_rows, 2 * MEM_W), "adamw_w_mem_kv")
    g_w_out, d_w_out, nm_w_out, nv_w_out = _adamw_sum(
        [r_out0, r_out1], w_out, m_w_out, v_w_out, (out_rows, D_MODEL), "adamw_w_out")
    def smalls(nw, mnw, onw, fnw, b):
        return [nw, mnw, onw, row(fnw), _misc_rows(b)]

    small_out = _adamw_small(
        r_small, smalls(norm_w, mem_norm_w, out_norm_w, final_norm_w, b_forget),
        smalls(m_norm_w, m_mem_norm_w, m_out_norm_w, m_final_norm_w, m_b_forget),
        smalls(v_norm_w, v_mem_norm_w, v_out_norm_w, v_final_norm_w, v_b_forget), "adamw_small")
    (g_nw, g_mnw, g_onw, g_fnw, g_b), (d_nw, d_mnw, d_onw, d_fnw, d_b), \
        (nm_nw, nm_mnw, nm_onw, nm_fnw, nm_b), (nv_nw, nv_mnw, nv_onw, nv_fnw, nv_b) = [
            (nw, mnw, onw, fnw[0], misc[:2, :FOX_HEADS]) for nw, mnw, onw, fnw, misc in small_out]
    loss = small_out[0][4][2, 0]

    return (loss, grad_x,
            g_nw, g_w_in, g_b, g_mnw, g_w_kv, g_onw, g_w_out, g_fnw,
            d_nw, d_w_in, d_b, d_mnw, d_w_kv, d_onw, d_w_out, d_fnw,
            nm_nw, nm_w_in, nm_b, nm_mnw, nm_w_kv, nm_onw, nm_w_out, nm_fnw,
            nv_nw, nv_w_in, nv_b, nv_mnw, nv_w_kv, nv_onw, nv_w_out, nv_fnw)
```

```python
import functools
from typing import NamedTuple

import jax
import jax.numpy as jnp
from jax import lax
from jax.experimental import pallas as pl
from jax.experimental.pallas import tpu as pltpu

F32 = jnp.float32
BF16 = jnp.bfloat16

N_DEV = 8
D_MODEL = 1024
HEAD_DIM = 64
LANES = 128
SB_W = 512
FX_W = 512
MEM_W = 256
MIX_W = 1280
FOX_HEADS = 8
IN_W = 4616
SHARD_W = IN_W // N_DEV
QKV_W = 3 * SB_W + 3 * FX_W + MEM_W
FL_PAD = 256
GF_W = MIX_W + FL_PAD
WR_W = QKV_W + GF_W
EPS = 1e-6
T = 256
QPS = 4
TM = 256
TQM = 512
Q_SCALE = 0.125
NEG = -1e30
UNDERFLOW = -110.0
NORM_SLACK = 1.01

ADAM_LR = 0.001
ADAM_B1 = 0.9
ADAM_B2 = 0.999
ADAM_EPS = 1e-08
ADAM_WD = 0.01
ADAM_STEP = 10


_NT = (((1,), (1,)), ((), ()))
_TN = (((0,), (0,)), ((), ()))

_ARB1 = pltpu.CompilerParams(dimension_semantics=("arbitrary",))
_ARB2 = pltpu.CompilerParams(dimension_semantics=("arbitrary", "arbitrary"))


def _dot(a, b):
    return jnp.dot(a, b, preferred_element_type=F32)


def _dot_nt(a, b):
    return lax.dot_general(a, b, _NT, preferred_element_type=F32)


def _dot_tn(a, b):
    return lax.dot_general(a, b, _TN, preferred_element_type=F32)


def _split2(x):
    hi = x.astype(BF16)
    lo = (x - hi.astype(F32)).astype(BF16)
    return hi, lo


def _stack2(u):
    return jnp.concatenate([u, u], axis=0)


def _cum2(x, u2):
    hi, lo = _split2(x)
    return _dot(jnp.concatenate([hi, lo], axis=1), u2)


def _tri3(tri, x, dot=None):
    dot = dot or _dot
    hi = x.astype(BF16)
    r1 = x - hi.astype(F32)
    mid = r1.astype(BF16)
    lo = (r1 - mid.astype(F32)).astype(BF16)
    return dot(tri, hi) + dot(tri, mid) + dot(tri, lo)


def _iota2(shape, dim):
    return lax.broadcasted_iota(jnp.int32, shape, dim)


def _head_block_diag():
    r = _iota2((LANES, LANES), 0) // HEAD_DIM
    c = _iota2((LANES, LANES), 1) // HEAD_DIM
    return _stack2(jnp.where(r == c, 1.0, 0.0).astype(BF16))


def _head_mean(x, bd):
    return _cum2(x, bd) * (1.0 / HEAD_DIM)


def _sigmoid(x):
    return 1.0 / (1.0 + jnp.exp(-x))


def _log_sigmoid(x):
    return jnp.minimum(x, 0.0) - jnp.log(1.0 + jnp.exp(-jnp.abs(x)))


def _running_top(r_ref):
    return jnp.max(jnp.maximum(r_ref[0], r_ref[1]))


def _fox_tiles_left(i, pair, nq, fb_ref, tile):
    def bound(j):
        b = []
        for h in range(2):
            head = 2 * pair + h
            b.append(2.0 * NORM_SLACK * fb_ref[2 * nq + i, head] * fb_ref[3 * nq, head]
                     + fb_ref[2 * i, head] - fb_ref[2 * j + 1, head])
        return jnp.maximum(b[0], b[1])

    def more(j):
        return jnp.logical_and(j >= 0, bound(jnp.maximum(j, 0)) > UNDERFLOW)

    def step(j):
        tile(j, False)
        return j - 1

    return lax.while_loop(more, step, i - 1)


def _pair_masks():
    lane = _iota2((1, LANES), 1)
    return lane < HEAD_DIM


def _split_pair(x, m0):
    zero = jnp.zeros_like(x)
    return jnp.where(m0, x, zero), jnp.where(m0, zero, x)


def _inproj_fwd(x, nw, w_r, name, rider=None):
    s = x.shape[0]

    def body(x_ref, nw_ref, w_ref, qkv_ref, gf_ref):
        xv = x_ref[...]
        r = lax.rsqrt(jnp.mean(xv * xv, axis=-1, keepdims=True) + EPS)
        h = (xv * r * nw_ref[...]).astype(BF16)
        for c in range(0, QKV_W, 256):
            qkv_ref[:, c:c + 256] = _dot_nt(h, w_ref[c:c + 256, :]).astype(BF16)
        for c in range(0, GF_W, 256):
            gf_ref[:, c:c + 256] = _dot_nt(h, w_ref[QKV_W + c:QKV_W + c + 256, :])

    (qkv, gf), lands = _ride(dict(
        body=body, name=name, grid=(s // TM,),
        in_specs=[pl.BlockSpec((TM, D_MODEL), lambda i: (i, 0)),
                  pl.BlockSpec((1, D_MODEL), lambda i: (0, 0)),
                  pl.BlockSpec((WR_W, D_MODEL), lambda i: (0, 0))],
        out_specs=[pl.BlockSpec((TM, QKV_W), lambda i: (i, 0)),
                   pl.BlockSpec((TM, GF_W), lambda i: (i, 0))],
        out_shape=[jax.ShapeDtypeStruct((s, QKV_W), BF16), jax.ShapeDtypeStruct((s, GF_W), F32)],
        scratch_shapes=[], compiler_params=_ARB1, operands=[x, nw, w_r]), rider)
    return qkv, gf, lands


def _fox_prep_fwd(gf, qkv, bpad, name):
    s = gf.shape[0]
    nq = s // T
    nrow = -(-(3 * nq + 1) // 8) * 8

    def body(fl_ref, q_ref, k_ref, b_ref, fq_ref, fr_ref, fb_ref):
        tri = jnp.where(_iota2((T, T), 0) >= _iota2((T, T), 1), 1.0, 0.0).astype(BF16)
        m0 = _pair_masks()
        lane = _iota2((1, LANES), 1)
        norms = [jnp.zeros((1, LANES), F32) for _ in range(nq + 1)]
        same_head = (_iota2((LANES, LANES), 0) // HEAD_DIM) == (_iota2((LANES, LANES), 1) // HEAD_DIM)
        bd = jnp.where(same_head, 1.0, 0.0).astype(BF16)
        for p in range(FOX_HEADS // 2):
            cols = slice(p * LANES, (p + 1) * LANES)
            q = (q_ref[:, cols] * jnp.asarray(Q_SCALE, BF16)).astype(F32)
            k = k_ref[:, cols].astype(F32)
            qn = _dot((q * q).astype(BF16), bd)
            kn = _dot((k * k).astype(BF16), bd)
            tops = [jnp.max(qn[j * T:(j + 1) * T], axis=0, keepdims=True) for j in range(nq)]
            tops.append(jnp.max(kn, axis=0, keepdims=True))
            tops = [jnp.sqrt(top) for top in tops]
            for h in range(2):
                at = h * HEAD_DIM
                norms = [jnp.where(lane == 2 * p + h, top[:, at:at + 1], row)
                         for top, row in zip(tops, norms)]
        for j in range(nq + 1):
            fb_ref[2 * nq + j:2 * nq + j + 1, :] = norms[j]
        fb_ref[3 * nq + 1:, :] = jnp.zeros((nrow - 3 * nq - 1, LANES), F32)
        carry = jnp.zeros((1, LANES), F32)
        for blk in range(s // T):
            rows = slice(blk * T, (blk + 1) * T)
            lf = _log_sigmoid(fl_ref[rows, :] + b_ref[...])
            c = _tri3(tri, lf) + carry
            carry = c[T - 1:T, :]
            for p in range(FOX_HEADS // 2):
                fq_ref[rows, p * LANES:(p + 1) * LANES] = jnp.where(
                    m0, c[:, 2 * p:2 * p + 1], c[:, 2 * p + 1:2 * p + 2])
            fr_ref[:, rows] = c.T[0:FOX_HEADS, :]
            fb_ref[2 * blk:2 * blk + 1, :] = c[0:1, :]
            fb_ref[2 * blk + 1:2 * blk + 2, :] = carry

    base = 3 * SB_W // FX_W
    return pl.pallas_call(
        body, name=name, grid=(1,),
        in_specs=[pl.BlockSpec((s, LANES), lambda i: (0, MIX_W // LANES)),
                  pl.BlockSpec((s, FX_W), lambda i: (0, base)),
                  pl.BlockSpec((s, FX_W), lambda i: (0, base + 1)),
                  pl.BlockSpec((1, LANES), lambda i: (0, 0))],
        out_specs=[pl.BlockSpec((s, FX_W), lambda i: (0, 0)),
                   pl.BlockSpec((FOX_HEADS, s), lambda i: (0, 0)),
                   pl.BlockSpec((nrow, LANES), lambda i: (0, 0))],
        out_shape=[jax.ShapeDtypeStruct((s, FX_W), F32), jax.ShapeDtypeStruct((FOX_HEADS, s), F32),
                   jax.ShapeDtypeStruct((nrow, LANES), F32)],
        compiler_params=_ARB1,
    )(gf, qkv, qkv, bpad)


def _sb_fwd(qkv, name, rider=None):
    s = qkv.shape[0]

    def body(q_ref, k_ref, v_ref, o_ref, acc_ref, r_ref, as_ref):
        m0 = _pair_masks()
        strict = _iota2((T, T), 0) > _iota2((T, T), 1)
        u2 = _stack2(jnp.where(strict, 1.0, 0.0).astype(BF16))
        hs = range(2)

        def query_tile(i, rows):
            qh = _split_pair(q_ref[rows, :] * jnp.asarray(Q_SCALE, BF16), m0)
            acc_ref[...] = jnp.zeros_like(acc_ref)
            r_ref[...] = jnp.zeros_like(r_ref)

            def flush(j):
                v = v_ref[pl.ds(pl.multiple_of(j * T, T), T), :]
                for h in hs:
                    acc_ref[h] += _dot(as_ref[h], v)

            def tile(j, diag):
                k = k_ref[pl.ds(pl.multiple_of(j * T, T), T), :]
                z = [_dot_nt(qh[h], k) for h in hs]
                if not diag:
                    flush(j + 1)
                la = [jnp.minimum(z[h], 0.0) - jnp.log(1.0 + jnp.exp(-jnp.abs(z[h]))) for h in hs]
                lf = [la[h] - z[h] for h in hs]
                if diag:
                    lf = [jnp.where(strict, lf[h], 0.0) for h in hs]
                cin = [_cum2(lf[h], u2) for h in hs]
                a = [jnp.exp(la[h] + cin[h] + r_ref[h]) for h in hs]
                if diag:
                    a = [jnp.where(strict, a[h], 0.0) for h in hs]
                for h in hs:
                    r_ref[h] += cin[h][:, 0:1] + lf[h][:, 0:1]
                    as_ref[h] = a[h].astype(BF16)

            tile(i, True)

            def more(state):
                j, top = state
                return jnp.logical_and(j >= 0, top > UNDERFLOW)

            def step(state):
                j, _ = state
                tile(j, False)
                return j - 1, _running_top(r_ref)

            j_left, _ = lax.while_loop(more, step, (i - 1, _running_top(r_ref)))
            flush(j_left + 1)
            o_ref[rows, :] = jnp.where(m0, acc_ref[0], acc_ref[1])

        for n in range(QPS):
            query_tile(QPS * pl.program_id(1) + n, slice(n * T, (n + 1) * T))

    nb = SB_W // LANES
    (ysb,), lands = _ride(dict(
        body=body, name=name, grid=(nb, s // (QPS * T)),
        in_specs=[pl.BlockSpec((QPS * T, LANES), lambda p, i: (i, p)),
                  pl.BlockSpec((s, LANES), lambda p, i: (0, nb + p)),
                  pl.BlockSpec((s, LANES), lambda p, i: (0, 2 * nb + p))],
        out_specs=[pl.BlockSpec((QPS * T, LANES), lambda p, i: (i, p))],
        out_shape=[jax.ShapeDtypeStruct((s, SB_W), F32)],
        scratch_shapes=[pltpu.VMEM((2, T, LANES), F32), pltpu.VMEM((2, T, 1), F32),
                        pltpu.VMEM((2, T, T), BF16)],
        compiler_params=_ARB2, operands=[qkv, qkv, qkv]), rider)
    return ysb, lands


def _fox_fwd(qkv, fqb, frow, fbounds, name, rider=None):
    s = qkv.shape[0]

    def body(q_ref, k_ref, v_ref, fq_ref, fr_ref, fb_ref, o_ref, lse_ref, acc_ref, m_ref, ps_ref):
        pair = pl.program_id(0)
        m0 = _pair_masks()
        causal = _iota2((T, T), 0) >= _iota2((T, T), 1)
        ones = jnp.ones((T, LANES), BF16)
        hs = range(2)

        def query_tile(i, rows):
            qh = _split_pair(q_ref[rows, :] * jnp.asarray(Q_SCALE, BF16), m0)
            fq = fq_ref[rows, :]
            fqh = (fq[:, 0:1], fq[:, HEAD_DIM:HEAD_DIM + 1])
            acc_ref[...] = jnp.zeros_like(acc_ref)
            m_ref[...] = jnp.full_like(m_ref, NEG)

            def flush(j):
                v = v_ref[pl.ds(pl.multiple_of(j * T, T), T), :]
                va2 = _stack2(jnp.concatenate([v, ones], axis=1))
                for h in hs:
                    acc_ref[h] += _dot(ps_ref[h], va2)

            def tile(j, diag):
                off = pl.multiple_of(j * T, T)
                k = k_ref[pl.ds(off, T), :]
                sc = [_dot_nt(qh[h], k) + fqh[h] - fr_ref[h:h + 1, pl.ds(off, T)] for h in hs]
                if not diag:
                    flush(j + 1)
                if diag:
                    sc = [jnp.where(causal, sc[h], NEG) for h in hs]
                m_new = [jnp.maximum(m_ref[h], jnp.max(sc[h], axis=1, keepdims=True)) for h in hs]
                p = [jnp.exp(sc[h] - m_new[h]) for h in hs]
                for h in hs:
                    acc_ref[h] = acc_ref[h] * jnp.exp(m_ref[h] - m_new[h])
                    m_ref[h] = m_new[h]
                    p_hi, p_lo = _split2(p[h])
                    ps_ref[h] = jnp.concatenate([p_hi, p_lo], axis=1)

            tile(i, True)
            j_left = _fox_tiles_left(i, pair, s // T, fb_ref, tile)
            flush(j_left + 1)
            acc = (acc_ref[0], acc_ref[1])
            o_ref[rows, :] = jnp.where(m0, acc[0][:, :LANES] / acc[0][:, LANES:],
                                       acc[1][:, :LANES] / acc[1][:, LANES:])
            lse_ref[rows, :] = jnp.where(m0, m_ref[0] + jnp.log(acc[0][:, LANES:]),
                                         m_ref[1] + jnp.log(acc[1][:, LANES:]))

        for n in range(QPS):
            query_tile(QPS * pl.program_id(1) + n, slice(n * T, (n + 1) * T))

    nb = FX_W // LANES
    base = 3 * SB_W // LANES
    (yfx, lse), lands = _ride(dict(
        body=body, name=name, grid=(nb, s // (QPS * T)),
        in_specs=[pl.BlockSpec((QPS * T, LANES), lambda p, i: (i, base + p)),
                  pl.BlockSpec((s, LANES), lambda p, i: (0, base + nb + p)),
                  pl.BlockSpec((s, LANES), lambda p, i: (0, base + 2 * nb + p)),
                  pl.BlockSpec((QPS * T, LANES), lambda p, i: (i, p)),
                  pl.BlockSpec((None, 2, s), lambda p, i: (p, 0, 0)),
                  pl.BlockSpec(memory_space=pltpu.SMEM)],
        out_specs=[pl.BlockSpec((QPS * T, LANES), lambda p, i: (i, p)),
                   pl.BlockSpec((QPS * T, LANES), lambda p, i: (i, p))],
        out_shape=[jax.ShapeDtypeStruct((s, FX_W), F32), jax.ShapeDtypeStruct((s, FX_W), F32)],
        scratch_shapes=[pltpu.VMEM((2, T, 2 * LANES), F32), pltpu.VMEM((2, T, 1), F32),
                        pltpu.VMEM((2, T, 2 * T), BF16)],
        compiler_params=_ARB2, operands=[qkv, qkv, qkv, fqb, frow, fbounds]), rider)
    return yfx, lse, lands


def _memkv_fwd(mem, mnw, wkv, name):
    n = mem.shape[0]

    def body(mem_ref, mnw_ref, w_ref, kv_ref):
        mv = mem_ref[...]
        r = lax.rsqrt(jnp.mean(mv * mv, axis=-1, keepdims=True) + EPS)
        hm = (mv * r * mnw_ref[...]).astype(BF16)
        kv_ref[...] = _dot(hm, w_ref[...]).astype(BF16)

    return pl.pallas_call(
        body, name=name, grid=(1,),
        in_specs=[pl.BlockSpec((n, D_MODEL), lambda i: (0, 0)),
                  pl.BlockSpec((1, D_MODEL), lambda i: (0, 0)),
                  pl.BlockSpec((D_MODEL, 2 * MEM_W), lambda i: (0, 0))],
        out_specs=pl.BlockSpec((n, 2 * MEM_W), lambda i: (0, 0)),
        out_shape=jax.ShapeDtypeStruct((n, 2 * MEM_W), BF16),
        compiler_params=_ARB1,
    )(mem, mnw, wkv)


def _mem_fwd(qkv, kv, name):
    s = qkv.shape[0]
    n = kv.shape[0]

    def body(q_ref, k_ref, v_ref, o_ref, lse_ref):
        m0 = _pair_masks()
        qh = _split_pair(q_ref[...] * jnp.asarray(Q_SCALE, BF16), m0)
        k = k_ref[...]
        v = v_ref[...]
        outs, lses = [], []
        for h in range(2):
            sc = _dot_nt(qh[h], k)
            mx = jnp.max(sc, axis=1, keepdims=True)
            p = jnp.exp(sc - mx)
            l = jnp.sum(p, axis=1, keepdims=True)
            outs.append(_dot(p.astype(BF16), v) / l)
            lses.append(mx + jnp.log(l))
        o_ref[...] = jnp.where(m0, outs[0], outs[1])
        lse_ref[...] = jnp.where(m0, lses[0], lses[1])

    nb = MEM_W // LANES
    base = (3 * SB_W + 3 * FX_W) // LANES
    return pl.pallas_call(
        body, name=name, grid=(nb, s // TQM),
        in_specs=[pl.BlockSpec((TQM, LANES), lambda p, i: (i, base + p)),
                  pl.BlockSpec((n, LANES), lambda p, i: (0, p)),
                  pl.BlockSpec((n, LANES), lambda p, i: (0, nb + p))],
        out_specs=[pl.BlockSpec((TQM, LANES), lambda p, i: (i, p)),
                   pl.BlockSpec((TQM, LANES), lambda p, i: (i, p))],
        out_shape=[jax.ShapeDtypeStruct((s, MEM_W), F32), jax.ShapeDtypeStruct((s, MEM_W), F32)],
        compiler_params=_ARB2,
    )(qkv, kv, kv)


def _mix_chunk(c, ysb_ref, yfx_ref, ym_ref):
    if c < SB_W // LANES:
        return ysb_ref[:, c * LANES:(c + 1) * LANES]
    c -= SB_W // LANES
    if c < FX_W // LANES:
        return yfx_ref[:, c * LANES:(c + 1) * LANES]
    c -= FX_W // LANES
    return ym_ref[:, c * LANES:(c + 1) * LANES]


def _outproj_fwd(ysb, yfx, ym, gf, onw, wout, x, name, rider=None):
    s = x.shape[0]

    def body(ysb_ref, yfx_ref, ym_ref, g_ref, onw_ref, w_ref, x_ref, o_ref, yg_ref):
        bd = _head_block_diag()
        for c in range(MIX_W // LANES):
            sl = slice(c * LANES, (c + 1) * LANES)
            u = _mix_chunk(c, ysb_ref, yfx_ref, ym_ref)
            r = lax.rsqrt(_head_mean(u * u, bd) + EPS)
            g = g_ref[:, sl]
            yg_ref[:, sl] = (u * r * onw_ref[:, sl] * (g * _sigmoid(g))).astype(BF16)
        o_ref[...] = x_ref[...] + _dot(yg_ref[...], w_ref[...])

    (xn,), lands = _ride(dict(
        body=body, name=name, grid=(s // TM,),
        in_specs=[pl.BlockSpec((TM, SB_W), lambda i: (i, 0)),
                  pl.BlockSpec((TM, FX_W), lambda i: (i, 0)),
                  pl.BlockSpec((TM, MEM_W), lambda i: (i, 0)),
                  pl.BlockSpec((TM, MIX_W), lambda i: (i, 0)),
                  pl.BlockSpec((1, MIX_W), lambda i: (0, 0)),
                  pl.BlockSpec((MIX_W, D_MODEL), lambda i: (0, 0)),
                  pl.BlockSpec((TM, D_MODEL), lambda i: (i, 0))],
        out_specs=[pl.BlockSpec((TM, D_MODEL), lambda i: (i, 0))],
        out_shape=[jax.ShapeDtypeStruct((s, D_MODEL), F32)],
        scratch_shapes=[pltpu.VMEM((TM, MIX_W), BF16)],
        compiler_params=_ARB1, operands=[ysb, yfx, ym, gf, onw, wout, x]), rider)
    return xn, lands


def _final_fwd_bwd(x, fnw, target, name):
    s = x.shape[0]

    def body(x_ref, w_ref, t_ref, dx_ref, loss_ref, dw_ref):
        @pl.when(pl.program_id(0) == 0)
        def _():
            loss_ref[...] = jnp.zeros_like(loss_ref)
            dw_ref[...] = jnp.zeros_like(dw_ref)

        xv = x_ref[...]
        w = w_ref[...]
        r = lax.rsqrt(jnp.mean(xv * xv, axis=-1, keepdims=True) + EPS)
        xh = xv * r
        err = xh * w - t_ref[...]
        part = jnp.sum(jnp.sum(err * err, axis=1, keepdims=True), axis=0, keepdims=True)
        loss_ref[...] += part * (0.5 / D_MODEL)
        dy = err * (1.0 / D_MODEL)
        dw_ref[...] += jnp.sum(dy * xh, axis=0, keepdims=True)
        dxh = dy * w
        dx_ref[...] = r * (dxh - xh * jnp.mean(dxh * xh, axis=-1, keepdims=True))

    return pl.pallas_call(
        body, name=name, grid=(s // TM,),
        in_specs=[pl.BlockSpec((TM, D_MODEL), lambda i: (i, 0)),
                  pl.BlockSpec((1, D_MODEL), lambda i: (0, 0)),
                  pl.BlockSpec((TM, D_MODEL), lambda i: (i, 0))],
        out_specs=[pl.BlockSpec((TM, D_MODEL), lambda i: (i, 0)),
                   pl.BlockSpec((1, LANES), lambda i: (0, 0)),
                   pl.BlockSpec((1, D_MODEL), lambda i: (0, 0))],
        out_shape=[jax.ShapeDtypeStruct((s, D_MODEL), F32), jax.ShapeDtypeStruct((1, LANES), F32),
                   jax.ShapeDtypeStruct((1, D_MODEL), F32)],
        compiler_params=_ARB1,
    )(x, fnw, target)


def _outproj_bwd(dxo, wout, ysb, yfx, ym, gf, onw, name, rider=None):
    s = dxo.shape[0]

    def body(dx_ref, w_ref, ysb_ref, yfx_ref, ym_ref, g_ref, onw_ref,
             dysb_ref, dyfx_ref, dym_ref, dg_ref, dw_ref, donw_ref, yg_ref):
        @pl.when(pl.program_id(0) == 0)
        def _():
            dw_ref[...] = jnp.zeros_like(dw_ref)
            donw_ref[...] = jnp.zeros_like(donw_ref)

        dxb = dx_ref[...].astype(BF16)
        dyg = _dot_nt(dxb, w_ref[...])
        bd = _head_block_diag()
        for c in range(MIX_W // LANES):
            sl = slice(c * LANES, (c + 1) * LANES)
            u = _mix_chunk(c, ysb_ref, yfx_ref, ym_ref)
            r = lax.rsqrt(_head_mean(u * u, bd) + EPS)
            yn = u * r
            g = g_ref[:, sl]
            sg = _sigmoid(g)
            sil = g * sg
            onw = onw_ref[:, sl]
            e = dyg[:, sl]
            yg_ref[:, sl] = (yn * onw * sil).astype(BF16)
            donw_ref[:, sl] += jnp.sum(e * yn * sil, axis=0, keepdims=True)
            dg_ref[:, sl] = (e * yn * onw * (sg * (1.0 + g * (1.0 - sg)))).astype(BF16)
            dyn = e * onw * sil
            du = (r * (dyn - yn * _head_mean(dyn * yn, bd))).astype(BF16)
            if c < 4:
                dysb_ref[:, c * LANES:(c + 1) * LANES] = du
            elif c < 8:
                dyfx_ref[:, (c - 4) * LANES:(c - 3) * LANES] = du
            else:
                dym_ref[:, (c - 8) * LANES:(c - 7) * LANES] = du
        dw_ref[...] += _dot_tn(yg_ref[...], dxb)

    outs, lands = _ride(dict(
        body=body, name=name, grid=(s // TM,),
        in_specs=[pl.BlockSpec((TM, D_MODEL), lambda i: (i, 0)),
                  pl.BlockSpec((MIX_W, D_MODEL), lambda i: (0, 0)),
                  pl.BlockSpec((TM, SB_W), lambda i: (i, 0)),
                  pl.BlockSpec((TM, FX_W), lambda i: (i, 0)),
                  pl.BlockSpec((TM, MEM_W), lambda i: (i, 0)),
                  pl.BlockSpec((TM, MIX_W), lambda i: (i, 0)),
                  pl.BlockSpec((1, MIX_W), lambda i: (0, 0))],
        out_specs=[pl.BlockSpec((TM, SB_W), lambda i: (i, 0)),
                   pl.BlockSpec((TM, FX_W), lambda i: (i, 0)),
                   pl.BlockSpec((TM, MEM_W), lambda i: (i, 0)),
                   pl.BlockSpec((TM, MIX_W), lambda i: (i, 0)),
                   pl.BlockSpec((MIX_W, D_MODEL), lambda i: (0, 0)),
                   pl.BlockSpec((1, MIX_W), lambda i: (0, 0))],
        out_shape=[jax.ShapeDtypeStruct((s, SB_W), BF16), jax.ShapeDtypeStruct((s, FX_W), BF16),
                   jax.ShapeDtypeStruct((s, MEM_W), BF16), jax.ShapeDtypeStruct((s, MIX_W), BF16),
                   jax.ShapeDtypeStruct((MIX_W, D_MODEL), F32), jax.ShapeDtypeStruct((1, MIX_W), F32)],
        scratch_shapes=[pltpu.VMEM((TM, MIX_W), BF16)],
        compiler_params=_ARB1, operands=[dxo, wout, ysb, yfx, ym, gf, onw]), rider)
    return (*outs, lands)


def _row_dots(do, o, m0):
    prod = do.astype(F32) * o
    zero = jnp.zeros_like(prod)
    return (jnp.sum(jnp.where(m0, prod, zero), axis=1, keepdims=True),
            jnp.sum(jnp.where(m0, zero, prod), axis=1, keepdims=True))


def _sb_bwd(qkv, o, do, name, rider=None):
    s = qkv.shape[0]
    nq = s // T

    def body(q_ref, k_ref, v_ref, o_ref, do_ref, dq_ref, dk_ref, dv_ref,
             dqa_ref, dka_ref, dva_ref, rl_ref, rg_ref, dzs_ref, abs_ref):
        step_id = pl.program_id(1)

        @pl.when(step_id == 0)
        def _():
            dka_ref[...] = jnp.zeros_like(dka_ref)
            dva_ref[...] = jnp.zeros_like(dva_ref)

        m0 = _pair_masks()
        strict = _iota2((T, T), 0) > _iota2((T, T), 1)
        u2 = _stack2(jnp.where(strict, 1.0, 0.0).astype(BF16))
        hs = range(2)

        def query_tile(i, rows):
            qh = _split_pair(q_ref[rows, :] * jnp.asarray(Q_SCALE, BF16), m0)
            doh = _split_pair(do_ref[rows, :], m0)
            dsum = _row_dots(do_ref[rows, :], o_ref[rows, :], m0)
            dqa_ref[...] = jnp.zeros_like(dqa_ref)
            rl_ref[...] = jnp.zeros_like(rl_ref)
            rg_ref[...] = jnp.zeros_like(rg_ref)

            def flush(j):
                off = pl.multiple_of(j * T, T)
                k = k_ref[pl.ds(off, T), :]
                for h in hs:
                    dqa_ref[h] += _dot(dzs_ref[h], k)
                dka_ref[pl.ds(off, T), :] += (_dot_tn(dzs_ref[0], qh[0])
                                              + _dot_tn(dzs_ref[1], qh[1]))
                dva_ref[pl.ds(off, T), :] += (_dot_tn(abs_ref[0], doh[0])
                                              + _dot_tn(abs_ref[1], doh[1]))

            def tile(j, diag):
                off = pl.multiple_of(j * T, T)
                k = k_ref[pl.ds(off, T), :]
                v = v_ref[pl.ds(off, T), :]
                z = [_dot_nt(qh[h], k) for h in hs]
                da = [_dot_nt(doh[h], v) for h in hs]
                if not diag:
                    flush(j + 1)
                la = [jnp.minimum(z[h], 0.0) - jnp.log(1.0 + jnp.exp(-jnp.abs(z[h]))) for h in hs]
                lf = [la[h] - z[h] for h in hs]
                if diag:
                    lf = [jnp.where(strict, lf[h], 0.0) for h in hs]
                cin = [_cum2(lf[h], u2) for h in hs]
                a = [jnp.exp(la[h] + cin[h] + rl_ref[h]) for h in hs]
                if diag:
                    a = [jnp.where(strict, a[h], 0.0) for h in hs]
                ab = [a[h].astype(BF16) for h in hs]
                g = [ab[h].astype(F32) * da[h] for h in hs]
                gin = [_cum2(g[h], u2) for h in hs]
                dz = [g[h] - jnp.exp(la[h]) * ((dsum[h] - rg_ref[h]) - gin[h]) for h in hs]
                if diag:
                    dz = [jnp.where(strict, dz[h], 0.0) for h in hs]
                for h in hs:
                    rl_ref[h] += cin[h][:, 0:1] + lf[h][:, 0:1]
                    rg_ref[h] += gin[h][:, 0:1] + g[h][:, 0:1]
                    dzs_ref[h] = dz[h].astype(BF16)
                    abs_ref[h] = ab[h]

            tile(i, True)

            def more(state):
                j, top = state
                return jnp.logical_and(j >= 0, top > UNDERFLOW)

            def step(state):
                j, _ = state
                tile(j, False)
                return j - 1, _running_top(rl_ref)

            j_left, _ = lax.while_loop(more, step, (i - 1, _running_top(rl_ref)))
            flush(j_left + 1)
            dq_ref[rows, :] = (jnp.where(m0, dqa_ref[0], dqa_ref[1]) * Q_SCALE).astype(BF16)

        for n in range(QPS):
            query_tile(QPS * step_id + n, slice(n * T, (n + 1) * T))

        @pl.when(step_id == nq // QPS - 1)
        def _():
            dk_ref[...] = dka_ref[...].astype(BF16)
            dv_ref[...] = dva_ref[...].astype(BF16)

    nb = SB_W // LANES
    (dq, dk, dv), lands = _ride(dict(
        body=body, name=name, grid=(nb, nq // QPS),
        in_specs=[pl.BlockSpec((QPS * T, LANES), lambda p, i: (i, p)),
                  pl.BlockSpec((s, LANES), lambda p, i: (0, nb + p)),
                  pl.BlockSpec((s, LANES), lambda p, i: (0, 2 * nb + p)),
                  pl.BlockSpec((QPS * T, LANES), lambda p, i: (i, p)),
                  pl.BlockSpec((QPS * T, LANES), lambda p, i: (i, p))],
        out_specs=[pl.BlockSpec((QPS * T, LANES), lambda p, i: (i, p)),
                   pl.BlockSpec((s, LANES), lambda p, i: (0, p)),
                   pl.BlockSpec((s, LANES), lambda p, i: (0, p))],
        out_shape=[jax.ShapeDtypeStruct((s, SB_W), BF16)] * 3,
        scratch_shapes=[pltpu.VMEM((2, T, LANES), F32), pltpu.VMEM((s, LANES), F32),
                        pltpu.VMEM((s, LANES), F32), pltpu.VMEM((2, T, 1), F32),
                        pltpu.VMEM((2, T, 1), F32), pltpu.VMEM((2, T, T), BF16),
                        pltpu.VMEM((2, T, T), BF16)],
        compiler_params=_ARB2, operands=[qkv, qkv, qkv, o, do]), rider)
    return dq, dk, dv, lands


def _fox_bwd(qkv, fqb, frow, fbounds, o, lse, do, name, rider=None):
    s = qkv.shape[0]
    nq = s // T

    def body(q_ref, k_ref, v_ref, fq_ref, fr_ref, fb_ref, o_ref, lse_ref, do_ref,
             dq_ref, dk_ref, dv_ref, df_ref, dqa_ref, dka_ref, dva_ref, dfa_ref, dls_ref, pbs_ref):
        step_id = pl.program_id(1)

        @pl.when(step_id == 0)
        def _():
            dka_ref[...] = jnp.zeros_like(dka_ref)
            dva_ref[...] = jnp.zeros_like(dva_ref)
            dfa_ref[...] = jnp.zeros_like(dfa_ref)

        m0 = _pair_masks()
        causal = _iota2((T, T), 0) >= _iota2((T, T), 1)
        hs = range(2)

        def query_tile(i, rows):
            qh = _split_pair(q_ref[rows, :] * jnp.asarray(Q_SCALE, BF16), m0)
            doh = _split_pair(do_ref[rows, :], m0)
            dsum = _row_dots(do_ref[rows, :], o_ref[rows, :], m0)
            fq = fq_ref[rows, :]
            fqh = (fq[:, 0:1], fq[:, HEAD_DIM:HEAD_DIM + 1])
            lse = lse_ref[rows, :]
            lseh = (lse[:, 0:1], lse[:, HEAD_DIM:HEAD_DIM + 1])
            dqa_ref[...] = jnp.zeros_like(dqa_ref)

            def flush(j):
                off = pl.multiple_of(j * T, T)
                k = k_ref[pl.ds(off, T), :]
                for h in hs:
                    dqa_ref[h] += _dot(dls_ref[h], k)
                dka_ref[pl.ds(off, T), :] += (_dot_tn(dls_ref[0], qh[0])
                                              + _dot_tn(dls_ref[1], qh[1]))
                dva_ref[pl.ds(off, T), :] += (_dot_tn(pbs_ref[0], doh[0])
                                              + _dot_tn(pbs_ref[1], doh[1]))

            def tile(j, diag):
                off = pl.multiple_of(j * T, T)
                k = k_ref[pl.ds(off, T), :]
                v = v_ref[pl.ds(off, T), :]
                sc = [_dot_nt(qh[h], k) + fqh[h] - fr_ref[h:h + 1, pl.ds(off, T)] for h in hs]
                dp = [_dot_nt(doh[h], v) for h in hs]
                if not diag:
                    flush(j + 1)
                p = [jnp.exp(sc[h] - lseh[h]) for h in hs]
                if diag:
                    p = [jnp.where(causal, p[h], 0.0) for h in hs]
                dl = [p[h] * (dp[h] - dsum[h]) for h in hs]
                for h in hs:
                    dls_ref[h] = dl[h].astype(BF16)
                    pbs_ref[h] = p[h].astype(BF16)
                    dfa_ref[h:h + 1, pl.ds(off, T)] -= jnp.sum(dl[h], axis=0, keepdims=True)

            tile(i, True)
            j_left = _fox_tiles_left(i, pl.program_id(0), nq, fb_ref, tile)
            flush(j_left + 1)
            dq_ref[rows, :] = (jnp.where(m0, dqa_ref[0], dqa_ref[1]) * Q_SCALE).astype(BF16)

        for n in range(QPS):
            query_tile(QPS * step_id + n, slice(n * T, (n + 1) * T))

        @pl.when(step_id == nq // QPS - 1)
        def _():
            dk_ref[...] = dka_ref[...].astype(BF16)
            dv_ref[...] = dva_ref[...].astype(BF16)
            df_ref[...] = dfa_ref[...]

    nb = FX_W // LANES
    base = 3 * SB_W // LANES
    (dq, dk, dv, df), lands = _ride(dict(
        body=body, name=name, grid=(nb, nq // QPS),
        in_specs=[pl.BlockSpec((QPS * T, LANES), lambda p, i: (i, base + p)),
                  pl.BlockSpec((s, LANES), lambda p, i: (0, base + nb + p)),
                  pl.BlockSpec((s, LANES), lambda p, i: (0, base + 2 * nb + p)),
                  pl.BlockSpec((QPS * T, LANES), lambda p, i: (i, p)),
                  pl.BlockSpec((None, 2, s), lambda p, i: (p, 0, 0)),
                  pl.BlockSpec(memory_space=pltpu.SMEM),
                  pl.BlockSpec((QPS * T, LANES), lambda p, i: (i, p)),
                  pl.BlockSpec((QPS * T, LANES), lambda p, i: (i, p)),
                  pl.BlockSpec((QPS * T, LANES), lambda p, i: (i, p))],
        out_specs=[pl.BlockSpec((QPS * T, LANES), lambda p, i: (i, p)),
                   pl.BlockSpec((s, LANES), lambda p, i: (0, p)),
                   pl.BlockSpec((s, LANES), lambda p, i: (0, p)),
                   pl.BlockSpec((None, 2, s), lambda p, i: (p, 0, 0))],
        out_shape=[jax.ShapeDtypeStruct((s, FX_W), BF16)] * 3
        + [jax.ShapeDtypeStruct((nb, 2, s), F32)],
        scratch_shapes=[pltpu.VMEM((2, T, LANES), F32), pltpu.VMEM((s, LANES), F32),
                        pltpu.VMEM((s, LANES), F32), pltpu.VMEM((2, s), F32),
                        pltpu.VMEM((2, T, T), BF16), pltpu.VMEM((2, T, T), BF16)],
        compiler_params=_ARB2, operands=[qkv, qkv, qkv, fqb, frow, fbounds, o, lse, do]), rider)
    return dq, dk, dv, df, lands


def _fox_prep_bwd(dfrow, gf, bpad, name):
    s = gf.shape[0]

    def body(df_ref, fl_ref, b_ref, dfl_ref, db_ref):
        tri = jnp.where(_iota2((T, T), 0) <= _iota2((T, T), 1), 1.0, 0.0).astype(BF16)
        carry = jnp.zeros((1, LANES), F32)
        db = jnp.zeros((1, LANES), F32)
        fill = jnp.zeros((LANES - FOX_HEADS, T), F32)
        for blk in reversed(range(s // T)):
            rows = slice(blk * T, (blk + 1) * T)
            c = _tri3(tri, jnp.concatenate([df_ref[:, rows], fill], axis=0), _dot_nt) + carry
            carry = c[0:1, :]
            dfl = c / (1.0 + jnp.exp(fl_ref[rows, :] + b_ref[...]))
            dfl_ref[rows, :] = dfl.astype(BF16)
            db = db + jnp.sum(dfl, axis=0, keepdims=True)
        db_ref[...] = db

    return pl.pallas_call(
        body, name=name, grid=(1,),
        in_specs=[pl.BlockSpec((FOX_HEADS, s), lambda i: (0, 0)),
                  pl.BlockSpec((s, LANES), lambda i: (0, MIX_W // LANES)),
                  pl.BlockSpec((1, LANES), lambda i: (0, 0))],
        out_specs=[pl.BlockSpec((s, LANES), lambda i: (0, 0)),
                   pl.BlockSpec((1, LANES), lambda i: (0, 0))],
        out_shape=[jax.ShapeDtypeStruct((s, LANES), BF16), jax.ShapeDtypeStruct((1, LANES), F32)],
        compiler_params=_ARB1,
    )(dfrow, gf, bpad)


def _mem_bwd(qkv, kv, o, lse, do, name):
    s = qkv.shape[0]
    n = kv.shape[0]

    def body(q_ref, k_ref, v_ref, o_ref, lse_ref, do_ref, dq_ref, dk_ref, dv_ref):
        @pl.when(pl.program_id(1) == 0)
        def _():
            dk_ref[...] = jnp.zeros_like(dk_ref)
            dv_ref[...] = jnp.zeros_like(dv_ref)

        m0 = _pair_masks()
        qh = _split_pair(q_ref[...] * jnp.asarray(Q_SCALE, BF16), m0)
        doh = _split_pair(do_ref[...], m0)
        dsum = _row_dots(do_ref[...], o_ref[...], m0)
        lse = lse_ref[...]
        lseh = (lse[:, 0:1], lse[:, HEAD_DIM:HEAD_DIM + 1])
        k = k_ref[...]
        v = v_ref[...]
        dqs = []
        for h in range(2):
            p = jnp.exp(_dot_nt(qh[h], k) - lseh[h])
            dl = p * (_dot_nt(doh[h], v) - dsum[h])
            dlb = dl.astype(BF16)
            dqs.append(_dot(dlb, k))
            dk_ref[...] += _dot_tn(dlb, qh[h])
            dv_ref[...] += _dot_tn(p.astype(BF16), doh[h])
        dq_ref[...] = (jnp.where(m0, dqs[0], dqs[1]) * Q_SCALE).astype(BF16)

    nb = MEM_W // LANES
    base = (3 * SB_W + 3 * FX_W) // LANES
    return pl.pallas_call(
        body, name=name, grid=(nb, s // TQM),
        in_specs=[pl.BlockSpec((TQM, LANES), lambda p, i: (i, base + p)),
                  pl.BlockSpec((n, LANES), lambda p, i: (0, p)),
                  pl.BlockSpec((n, LANES), lambda p, i: (0, nb + p)),
                  pl.BlockSpec((TQM, LANES), lambda p, i: (i, p)),
                  pl.BlockSpec((TQM, LANES), lambda p, i: (i, p)),
                  pl.BlockSpec((TQM, LANES), lambda p, i: (i, p))],
        out_specs=[pl.BlockSpec((TQM, LANES), lambda p, i: (i, p)),
                   pl.BlockSpec((n, LANES), lambda p, i: (0, p)),
                   pl.BlockSpec((n, LANES), lambda p, i: (0, p))],
        out_shape=[jax.ShapeDtypeStruct((s, MEM_W), BF16), jax.ShapeDtypeStruct((n, MEM_W), F32),
                   jax.ShapeDtypeStruct((n, MEM_W), F32)],
        compiler_params=_ARB2,
    )(qkv, kv, kv, o, lse, do)


def _memkv_bwd(mem, mnw, wkv, dk, dv, name):
    n = mem.shape[0]

    def body(mem_ref, mnw_ref, w_ref, dk_ref, dv_ref, dw_ref, dmnw_ref):
        mv = mem_ref[...]
        r = lax.rsqrt(jnp.mean(mv * mv, axis=-1, keepdims=True) + EPS)
        mh = mv * r
        hm = (mh * mnw_ref[...]).astype(BF16)
        dkv = jnp.concatenate([dk_ref[...], dv_ref[...]], axis=1).astype(BF16)
        dw_ref[...] = _dot_tn(hm, dkv)
        dhm = _dot_nt(dkv, w_ref[...])
        dmnw_ref[...] = jnp.sum(dhm * mh, axis=0, keepdims=True)

    return pl.pallas_call(
        body, name=name, grid=(1,),
        in_specs=[pl.BlockSpec((n, D_MODEL), lambda i: (0, 0)),
                  pl.BlockSpec((1, D_MODEL), lambda i: (0, 0)),
                  pl.BlockSpec((D_MODEL, 2 * MEM_W), lambda i: (0, 0)),
                  pl.BlockSpec((n, MEM_W), lambda i: (0, 0)),
                  pl.BlockSpec((n, MEM_W), lambda i: (0, 0))],
        out_specs=[pl.BlockSpec((D_MODEL, 2 * MEM_W), lambda i: (0, 0)),
                   pl.BlockSpec((1, D_MODEL), lambda i: (0, 0))],
        out_shape=[jax.ShapeDtypeStruct((D_MODEL, 2 * MEM_W), F32),
                   jax.ShapeDtypeStruct((1, D_MODEL), F32)],
        compiler_params=_ARB1,
    )(mem, mnw, wkv, dk, dv)


def _inproj_bwd_dx(pieces, w_r, x, nw, dxo, name, rider=None):
    s = x.shape[0]
    n = len(pieces)
    widths = [p.shape[1] for p in pieces]

    def body(*refs):
        piece_refs = refs[:n]
        w_ref, x_ref, nw_ref, dxo_ref, dx_ref, h_ref, dnw_ref, dp_ref = refs[n:]

        @pl.when(pl.program_id(0) == 0)
        def _():
            dnw_ref[...] = jnp.zeros_like(dnw_ref)

        col = 0
        for r, wd in zip(piece_refs, widths):
            dp_ref[:, col:col + wd] = r[...]
            col += wd
        dp_ref[:, col:] = jnp.zeros((TM, WR_W - col), BF16)
        dh = _dot(dp_ref[...], w_ref[...])
        xv = x_ref[...]
        nw = nw_ref[...]
        r = lax.rsqrt(jnp.mean(xv * xv, axis=-1, keepdims=True) + EPS)
        xh = xv * r
        h_ref[...] = (xh * nw).astype(BF16)
        dnw_ref[...] += jnp.sum(dh * xh, axis=0, keepdims=True)
        dxh = dh * nw
        dx_ref[...] = r * (dxh - xh * jnp.mean(dxh * xh, axis=-1, keepdims=True)) + dxo_ref[...]

    (dx, h, dnw, dproj), lands = _ride(dict(
        body=body, name=name, grid=(s // TM,),
        in_specs=[pl.BlockSpec((TM, wd), lambda i: (i, 0)) for wd in widths]
        + [pl.BlockSpec((WR_W, D_MODEL), lambda i: (0, 0)),
           pl.BlockSpec((TM, D_MODEL), lambda i: (i, 0)),
           pl.BlockSpec((1, D_MODEL), lambda i: (0, 0)),
           pl.BlockSpec((TM, D_MODEL), lambda i: (i, 0))],
        out_specs=[pl.BlockSpec((None, TM, D_MODEL), lambda i: (0, i, 0)),
                   pl.BlockSpec((TM, D_MODEL), lambda i: (i, 0)),
                   pl.BlockSpec((1, D_MODEL), lambda i: (0, 0)),
                   pl.BlockSpec((TM, WR_W), lambda i: (i, 0))],
        out_shape=[jax.ShapeDtypeStruct((1, s, D_MODEL), F32), jax.ShapeDtypeStruct((s, D_MODEL), BF16),
                   jax.ShapeDtypeStruct((1, D_MODEL), F32), jax.ShapeDtypeStruct((s, WR_W), BF16)],
        scratch_shapes=[], compiler_params=_ARB1, operands=[*pieces, w_r, x, nw, dxo]), rider)
    return dx, h, dnw, dproj, lands


def _inproj_bwd_dw(h, dproj, name):
    s = dproj.shape[0]
    tn = 256

    def body(h_ref, dp_ref, dw_ref):
        dw_ref[...] = _dot_tn(dp_ref[...], h_ref[...]).astype(BF16)

    return pl.pallas_call(
        body, name=name, grid=(WR_W // tn,),
        in_specs=[pl.BlockSpec((s, D_MODEL), lambda j: (0, 0)),
                  pl.BlockSpec((s, tn), lambda j: (0, j))],
        out_specs=pl.BlockSpec((tn, D_MODEL), lambda j: (j, 0)),
        out_shape=jax.ShapeDtypeStruct((WR_W, D_MODEL), BF16),
        compiler_params=_ARB1,
    )(h, dproj)


def _rearrange_w_in(wt):
    pad = jnp.zeros((FL_PAD - FOX_HEADS,) + wt.shape[1:], wt.dtype)
    return jnp.concatenate([wt[:3072], wt[3080:3336], wt[3336:IN_W], wt[3072:3080], pad], axis=0)


_W_IN_SEGMENTS = [(0, 3072, 0), (3072, 3080, QKV_W + MIX_W), (3080, 3336, 3072), (3336, IN_W, QKV_W)]
ASSEMBLE_COLS = 256
ASSEMBLE_ROWS = 128


def _shard_pieces(k):
    lo, hi = k * SHARD_W, (k + 1) * SHARD_W
    return [(max(lo, a) - lo, at + max(lo, a) - a, min(hi, b) - max(lo, a))
            for a, b, at in _W_IN_SEGMENTS if max(lo, a) < min(hi, b)]


def _assemble_w_in(v, to_blocks, name):
    used = QKV_W + MIX_W + FOX_HEADS

    def body(v_ref, o_ref):
        for k in range(N_DEV):
            for at, w_at, n in _shard_pieces(k):
                for r in range(0, n, ASSEMBLE_ROWS):
                    m = min(ASSEMBLE_ROWS, n - r)
                    if to_blocks:
                        o_ref[k, at + r:at + r + m, :] = v_ref[w_at + r:w_at + r + m, :]
                    else:
                        o_ref[w_at + r:w_at + r + m, :] = v_ref[k, at + r:at + r + m, :]
        if not to_blocks:
            o_ref[used:, :] = jnp.zeros((WR_W - used, ASSEMBLE_COLS), BF16)

    blocks = pl.BlockSpec((N_DEV, SHARD_W, ASSEMBLE_COLS), lambda j: (0, 0, j))
    rows = pl.BlockSpec((WR_W, ASSEMBLE_COLS), lambda j: (0, j))
    out_shape = (N_DEV, SHARD_W, D_MODEL) if to_blocks else (WR_W, D_MODEL)
    return pl.pallas_call(
        body, name=name, grid=(D_MODEL // ASSEMBLE_COLS,),
        in_specs=[rows if to_blocks else blocks], out_specs=blocks if to_blocks else rows,
        out_shape=jax.ShapeDtypeStruct(out_shape, BF16), compiler_params=_ARB1)(v)


def _restore_w_in(g):
    gate0 = QKV_W
    fl0 = QKV_W + MIX_W
    return jnp.concatenate(
        [g[:3072], g[fl0:fl0 + FOX_HEADS], g[3072:QKV_W], g[gate0:fl0]], axis=0)


def _pad_lanes(v, width=LANES):
    return jnp.pad(v, (0, width - v.shape[0])).reshape(1, width)


def _layer_fwd(xs, mem, nw, w_r, b_forget, mnw, late, onw, l, travel=None):
    s = xs.shape[0]
    bpad = _pad_lanes(b_forget)
    travel = _Travel(travel)
    qkv, gf, _ = travel.ride(2, _inproj_fwd, xs, nw, w_r, f"inproj_fwd_{l}")
    fqb, frow, fbounds = _fox_prep_fwd(gf, qkv, bpad, f"fox_prep_fwd_{l}")
    frow = frow.reshape(FOX_HEADS // 2, 2, s)
    ysb, _ = travel.ride(0, _sb_fwd, qkv, f"sb_fwd_{l}")
    yfx, lse_fx, _ = travel.ride(1, _fox_fwd, qkv, fqb, frow, fbounds, f"fox_fwd_{l}")
    wkv, wout = late(travel.lands)
    kv = _memkv_fwd(mem, mnw, wkv, f"memkv_fwd_{l}")
    ym, lse_m = _mem_fwd(qkv, kv, f"mem_fwd_{l}")
    xn, _ = travel.ride(3, _outproj_fwd, ysb, yfx, ym, gf, onw, wout, xs, f"outproj_fwd_{l}")
    saved = (xs, nw, mnw, onw, bpad, qkv, gf, fqb, frow, fbounds, ysb, yfx, lse_fx, kv, ym, lse_m)
    return xn, saved, travel.lands, (wkv, wout)


class _Travel:
    def __init__(self, plan):
        self.plan = plan
        self.lands = None if plan is None else _new_lands(plan[0], plan[1])

    def ride(self, n, fn, *args):
        if self.plan is None or n >= len(self.plan[2]) or self.plan[2][n] is None:
            return fn(*args)
        srcs, scatter, legs = self.plan
        idx, rows = legs[n]
        out = fn(*args, rider=_Rider([srcs[a] for a in idx], [self.lands[a] for a in idx],
                                     scatter, rows, relay=not scatter))
        for a, land in zip(idx, out[-1]):
            self.lands[a] = land
        return out


def _layer_bwd(dx, saved, mem, w_r, wkv, wout, l, travel=None, own_w_out=None):
    xs, nw, mnw, onw, bpad, qkv, gf, fqb, frow, fbounds, ysb, yfx, lse_fx, kv, ym, lse_m = saved
    s = xs.shape[0]
    travel = _Travel(travel)
    dysb, dyfx, dym, dgate, dwout, donw, _ = travel.ride(
        3, _outproj_bwd, dx, wout, ysb, yfx, ym, gf, onw, f"outproj_bwd_{l}")
    if own_w_out is not None:
        travel.plan[0][own_w_out[0]] = own_w_out[1](dwout)
    sdq, sdk, sdv, _ = travel.ride(0, _sb_bwd, qkv, ysb, dysb, f"sb_bwd_{l}")
    fdq, fdk, fdv, dfrow, _ = travel.ride(1, _fox_bwd, qkv, fqb, frow, fbounds, yfx, lse_fx, dyfx,
                                          f"fox_bwd_{l}")
    dfl, db = _fox_prep_bwd(dfrow.reshape(FOX_HEADS, s), gf, bpad, f"fox_prep_bwd_{l}")
    dmq, dmk, dmv = _mem_bwd(qkv, kv, ym, lse_m, dym, f"mem_bwd_{l}")
    dwkv, dmnw = _memkv_bwd(mem, mnw, wkv, dmk, dmv, f"memkv_bwd_{l}")
    dx, ht, dnw, dproj, _ = travel.ride(2, _inproj_bwd_dx,
                                        [sdq, sdk, sdv, fdq, fdk, fdv, dmq, dgate, dfl],
                                        w_r, xs, nw, dx, f"inproj_bwd_dx_{l}")
    dwr = _inproj_bwd_dw(ht, dproj, f"inproj_bwd_dw_{l}")
    grads = dict(norm_w=dnw[0], w_r=dwr, b_forget=db[0, :FOX_HEADS], mem_norm_w=dmnw[0],
                 w_mem_kv=dwkv, out_norm_w=donw[0], w_out=dwout)
    return dx, grads, travel.lands


_ANY = pl.BlockSpec(memory_space=pl.ANY)


def _my_place():
    return lax.axis_index("x"), lax.axis_index("y"), lax.axis_index("c")


def _flip(v, bit):
    return 1 - v if bit else v


def _block_index(px, py, pc):
    return 4 * px + 2 * py + pc


def _all_gather_weights(shards, name):
    n = len(shards)

    def body(*refs):
        ins, outs = refs[:n], refs[n:2 * n]
        send_sems, recv_sems, local_sems = refs[2 * n:]
        x, y, c = _my_place()
        me = (x, y, c)
        sibling = (x, y, 1 - c)
        chips = [(1 - x, y), (x, 1 - y), (1 - x, 1 - y)]

        def copy(a, k, block, to, src=None):
            dst = outs[a].at[_block_index(*block)]
            return pltpu.make_async_remote_copy(
                src_ref=dst if src is None else src, dst_ref=dst,
                send_sem=send_sems.at[a, k], recv_sem=recv_sems.at[a, k],
                device_id=to, device_id_type=pl.DeviceIdType.MESH)

        mine = [pltpu.make_async_copy(ins[a], outs[a].at[_block_index(*me)], local_sems.at[a])
                for a in range(n)]
        for cp in mine:
            cp.start()
        first = []
        for a in range(n):
            first.append(copy(a, 0, me, sibling, src=ins[a]))
            first += [copy(a, 1 + j, me, (*chip, c), src=ins[a]) for j, chip in enumerate(chips)]
        for cp in first:
            cp.start()
        passed = []
        for j, chip in enumerate(chips):
            for a in range(n):
                copy(a, 1 + j, (*chip, c), me).wait_recv()
                fwd = copy(a, 4 + j, (*chip, c), sibling)
                fwd.start()
                passed.append(fwd)
        for a in range(n):
            copy(a, 0, sibling, me).wait_recv()
            for j, chip in enumerate(chips):
                copy(a, 4 + j, (*chip, 1 - c), me).wait_recv()
        for cp in first + passed:
            cp.wait_send()
        for cp in mine:
            cp.wait()

    return pl.pallas_call(
        body, name=name,
        in_specs=[_ANY] * n, out_specs=[_ANY] * n,
        out_shape=[jax.ShapeDtypeStruct((N_DEV,) + v.shape, v.dtype) for v in shards],
        scratch_shapes=[pltpu.SemaphoreType.DMA((n, 7)), pltpu.SemaphoreType.DMA((n, 7)),
                        pltpu.SemaphoreType.DMA((n,))],
    )(*shards)


def _exchange_blocks(blocked, name):
    n = len(blocked)

    def body(*refs):
        ins, outs = refs[:n], refs[n:2 * n]
        send_sems, recv_sems, local_sems = refs[2 * n:]
        x, y, c = _my_place()
        mine_idx = _block_index(x, y, c)
        local = [pltpu.make_async_copy(ins[a].at[mine_idx], outs[a].at[mine_idx], local_sems.at[a])
                 for a in range(n)]
        for cp in local:
            cp.start()
        sends, arrivals = [], []
        for r in range(1, N_DEV):
            peer = (_flip(x, r & 4), _flip(y, r & 2), _flip(c, r & 1))
            peer_idx = _block_index(*peer)
            for a in range(n):
                sems = dict(send_sem=send_sems.at[a, r - 1], recv_sem=recv_sems.at[a, r - 1],
                            device_id=peer, device_id_type=pl.DeviceIdType.MESH)
                sends.append(pltpu.make_async_remote_copy(
                    src_ref=ins[a].at[peer_idx], dst_ref=outs[a].at[mine_idx], **sems))
                arrivals.append(pltpu.make_async_remote_copy(
                    src_ref=ins[a].at[peer_idx], dst_ref=outs[a].at[peer_idx], **sems))
        for cp in sends:
            cp.start()
        for cp in arrivals:
            cp.wait_recv()
        for cp in sends:
            cp.wait_send()
        for cp in local:
            cp.wait()

    return pl.pallas_call(
        body, name=name,
        in_specs=[_ANY] * n, out_specs=[_ANY] * n,
        out_shape=[jax.ShapeDtypeStruct(v.shape, v.dtype) for v in blocked],
        scratch_shapes=[pltpu.SemaphoreType.DMA((n, 7)), pltpu.SemaphoreType.DMA((n, 7)),
                        pltpu.SemaphoreType.DMA((n,))],
    )(*blocked)


N_CHIP = N_DEV // 2


def _pair_swap(blocked, name):
    n = len(blocked)

    def body(*refs):
        ins, outs = refs[:n], refs[n:2 * n]
        send_sems, recv_sems = refs[2 * n:]
        x, y, c = _my_place()
        copies = [pltpu.make_async_remote_copy(
            src_ref=ins[a].at[j, 1 - c], dst_ref=outs[a].at[j],
            send_sem=send_sems.at[N_CHIP * a + j], recv_sem=recv_sems.at[N_CHIP * a + j],
            device_id=(x, y, 1 - c), device_id_type=pl.DeviceIdType.MESH)
            for a in range(n) for j in range(N_CHIP)]
        for cp in copies:
            cp.start()
        for cp in copies:
            cp.wait_recv()
        for cp in copies:
            cp.wait_send()

    return pl.pallas_call(
        body, name=name,
        in_specs=[_ANY] * n, out_specs=[_ANY] * n,
        out_shape=[jax.ShapeDtypeStruct((N_CHIP,) + v.shape[2:], v.dtype) for v in blocked],
        scratch_shapes=[pltpu.SemaphoreType.DMA((N_CHIP * n,)),
                        pltpu.SemaphoreType.DMA((N_CHIP * n,))],
    )(*blocked)


def _pair_add(by_core, theirs, core, name):
    n = len(by_core)

    def body(core_ref, *refs):
        for a in range(n):
            mine, got, out = refs[a], refs[n + a], refs[2 * n + a]
            out[...] = (mine[...].astype(F32) + got[...].astype(F32)).astype(BF16)

    def own(v):
        return pl.BlockSpec((None, None) + v.shape[2:], lambda j, core_ref: (j, core_ref[0], 0, 0))

    def block(v):
        return pl.BlockSpec((None,) + v.shape[1:], lambda j, core_ref: (j, 0, 0))

    return list(pl.pallas_call(
        body, name=name,
        grid_spec=pltpu.PrefetchScalarGridSpec(
            num_scalar_prefetch=1, grid=(N_CHIP,),
            in_specs=[own(v) for v in by_core] + [block(v) for v in theirs],
            out_specs=[block(v) for v in theirs]),
        out_shape=[jax.ShapeDtypeStruct(v.shape, BF16) for v in theirs],
        compiler_params=_ARB1,
    )(core, *by_core, *theirs))


def _chip_exchange(by_chip, to_all, name):
    n, m = len(by_chip), len(to_all)

    def body(*refs):
        ins, alls = refs[:n], refs[n:n + m]
        outs, all_outs = refs[n + m:2 * n + m], refs[2 * n + m:2 * (n + m)]
        send_sems, recv_sems, local_sems = refs[2 * (n + m):]
        x, y, c = _my_place()
        my_chip, mine_idx = 2 * x + y, _block_index(x, y, c)
        local = [pltpu.make_async_copy(ins[a].at[my_chip], outs[a].at[my_chip], local_sems.at[a])
                 for a in range(n)]
        local += [pltpu.make_async_copy(alls[b].at[mine_idx], all_outs[b].at[mine_idx],
                                        local_sems.at[n + b]) for b in range(m)]
        for cp in local:
            cp.start()
        sends, arrivals = [], []
        k = 0
        for r in range(1, N_DEV):
            peer = (_flip(x, r & 4), _flip(y, r & 2), _flip(c, r & 1))
            peer_chip, peer_idx = 2 * peer[0] + peer[1], _block_index(*peer)
            pairs = [(alls[b].at[mine_idx], all_outs[b].at[mine_idx], all_outs[b].at[peer_idx])
                     for b in range(m)]
            if not r & 1:
                pairs += [(ins[a].at[peer_chip], outs[a].at[my_chip], outs[a].at[peer_chip])
                          for a in range(n)]
            for src, there, here in pairs:
                sems = dict(send_sem=send_sems.at[k], recv_sem=recv_sems.at[k], device_id=peer,
                            device_id_type=pl.DeviceIdType.MESH)
                sends.append(pltpu.make_async_remote_copy(src_ref=src, dst_ref=there, **sems))
                arrivals.append(pltpu.make_async_remote_copy(src_ref=src, dst_ref=here, **sems))
                k += 1
        for cp in sends:
            cp.start()
        for cp in arrivals:
            cp.wait_recv()
        for cp in sends:
            cp.wait_send()
        for cp in local:
            cp.wait()

    n_copies = 7 * m + 3 * n
    return pl.pallas_call(
        body, name=name,
        in_specs=[_ANY] * (n + m), out_specs=[_ANY] * (n + m),
        out_shape=[jax.ShapeDtypeStruct(v.shape, v.dtype) for v in by_chip + to_all],
        scratch_shapes=[pltpu.SemaphoreType.DMA((n_copies,)), pltpu.SemaphoreType.DMA((n_copies,)),
                        pltpu.SemaphoreType.DMA((n + m,))],
    )(*by_chip, *to_all)


class _Rider(NamedTuple):
    srcs: list
    lands: list
    scatter: bool
    part: list
    relay: bool = False


def _window(ref, part):
    if part is None:
        return ref
    dim, start, size = part
    return ref.at[(slice(None),) * dim + (pl.ds(start, size),)]


def _relay_copies(srcs, lands, send_sems, recv_sems, rider):
    x, y, c = _my_place()
    me, sibling = (x, y, c), (x, y, 1 - c)
    chips = [(1 - x, y), (x, 1 - y), (1 - x, 1 - y)]
    first, from_chips, passed, last = [], [], [], []
    for a in range(len(srcs)):
        def copy(k, block, to, src=None, a=a):
            dst = _window(lands[a].at[_block_index(*block)], rider.part[a])
            return pltpu.make_async_remote_copy(
                src_ref=dst if src is None else src, dst_ref=dst,
                send_sem=send_sems.at[7 * a + k], recv_sem=recv_sems.at[7 * a + k],
                device_id=to, device_id_type=pl.DeviceIdType.MESH)

        mine = _window(srcs[a], rider.part[a])
        first.append(copy(0, me, sibling, src=mine))
        first += [copy(1 + j, me, (*chip, c), src=mine) for j, chip in enumerate(chips)]
        from_chips += [copy(1 + j, (*chip, c), me) for j, chip in enumerate(chips)]
        passed += [copy(4 + j, (*chip, c), sibling) for j, chip in enumerate(chips)]
        last.append(copy(0, sibling, me))
        last += [copy(4 + j, (*chip, 1 - c), me) for j, chip in enumerate(chips)]
    return first, from_chips, passed, last


def _new_lands(srcs, scatter):
    return [lax.empty(v.shape if scatter else (N_DEV,) + v.shape, v.dtype) for v in srcs]


def _rider_copies(srcs, lands, send_sems, recv_sems, rider):
    x, y, c = _my_place()
    mine_idx = _block_index(x, y, c)

    def window(ref, a):
        return _window(ref, rider.part[a])

    sends, arrivals = [], []
    for r in range(1, N_DEV):
        peer = (_flip(x, r & 4), _flip(y, r & 2), _flip(c, r & 1))
        peer_idx = _block_index(*peer)
        for a in range(len(srcs)):
            src = window(srcs[a].at[peer_idx] if rider.scatter else srcs[a], a)
            k = 7 * a + r - 1
            sems = dict(send_sem=send_sems.at[k], recv_sem=recv_sems.at[k],
                        device_id=peer, device_id_type=pl.DeviceIdType.MESH)
            sends.append(pltpu.make_async_remote_copy(
                src_ref=src, dst_ref=window(lands[a].at[mine_idx], a), **sems))
            arrivals.append(pltpu.make_async_remote_copy(
                src_ref=src, dst_ref=window(lands[a].at[peer_idx], a), **sems))
    return sends, arrivals


def _ride(call, rider):
    call = dict(call)
    body, grid = call.pop("body"), call["grid"]
    operands = call.pop("operands")
    if rider is None:
        return list(pl.pallas_call(body, **call)(*operands)), None
    n_in, n_out = len(call["in_specs"]), len(call["out_specs"])
    n_scratch = len(call["scratch_shapes"])
    m = len(rider.srcs)

    def riding(*refs):
        main_in, srcs, lands = refs[:n_in], refs[n_in:n_in + m], refs[n_in + m:n_in + 2 * m]
        main_out = refs[n_in + 2 * m:n_in + 2 * m + n_out]
        lands_out = refs[n_in + 2 * m + n_out:n_in + 3 * m + n_out]
        rest = refs[n_in + 3 * m + n_out:]
        send_sems, recv_sems, own_sems = rest[n_scratch:]
        me = _block_index(*_my_place())
        own = [pltpu.make_async_copy(
            _window(srcs[a].at[me] if rider.scatter else srcs[a], rider.part[a]),
            _window(lands_out[a].at[me], rider.part[a]), own_sems.at[a]) for a in range(m)]
        at = [pl.program_id(d) for d in range(len(grid))]
        first = functools.reduce(jnp.logical_and, [p == 0 for p in at])
        last = functools.reduce(jnp.logical_and, [p == g - 1 for p, g in zip(at, grid)])
        if rider.relay:
            sends, from_chips, passed, arrivals = _relay_copies(
                srcs, lands, send_sems, recv_sems, rider)
            step, steps = 0, 1
            for p, g in zip(at, grid):
                step, steps = step * g + p, steps * g
            assert steps >= 2, "a relayed gather needs a later grid step to pass blocks on"

            @pl.when(step == (7 * steps) // 8)
            def _():
                for cp in from_chips:
                    cp.wait_recv()
                for cp in passed:
                    cp.start()
        else:
            sends, arrivals = _rider_copies(srcs, lands, send_sems, recv_sems, rider)
            passed = []

        @pl.when(first)
        def _():
            for cp in sends + own:
                cp.start()

        body(*main_in, *main_out, *rest[:n_scratch])

        @pl.when(last)
        def _():
            for cp in arrivals:
                cp.wait_recv()
            for cp in sends + passed:
                cp.wait_send()
            for cp in own:
                cp.wait()

    call["in_specs"] = list(call["in_specs"]) + [_ANY] * (2 * m)
    call["out_specs"] = list(call["out_specs"]) + [_ANY] * m
    call["out_shape"] = list(call["out_shape"]) + [
        jax.ShapeDtypeStruct(v.shape, v.dtype) for v in rider.lands]
    call["scratch_shapes"] = list(call["scratch_shapes"]) + [
        pltpu.SemaphoreType.DMA((7 * m,)), pltpu.SemaphoreType.DMA((7 * m,)),
        pltpu.SemaphoreType.DMA((m,))]
    call["input_output_aliases"] = {n_in + m + a: n_out + a for a in range(m)}
    outs = pl.pallas_call(riding, **call)(*operands, *rider.srcs, *rider.lands)
    return list(outs[:n_out]), list(outs[n_out:])


def _sum_parts(p_ref):
    g = p_ref[0].astype(F32)
    for k in range(1, p_ref.shape[0]):
        g = g + p_ref[k].astype(F32)
    return g


def _adamw(g, w, m, v):
    c1 = 1.0 / (1.0 - ADAM_B1 ** ADAM_STEP)
    c2 = 1.0 / (1.0 - ADAM_B2 ** ADAM_STEP)
    nm = ADAM_B1 * m + (1.0 - ADAM_B1) * g
    nv = ADAM_B2 * v + (1.0 - ADAM_B2) * (g * g)
    return nm, nv, -ADAM_LR * ((nm * c1) / (jnp.sqrt(nv * c2) + ADAM_EPS) + ADAM_WD * w)


def _adamw_w_in(parts, w, m, v, name):
    ncol_blk, depth, nfeat = w.shape
    cols = 256

    def body(*refs):
        p_refs = refs[:depth]
        w_ref, m_ref, v_ref, g_ref, d_ref, nm_ref, nv_ref = refs[depth:]
        g = jnp.stack([_sum_parts(p_refs[l]) for l in range(depth)], axis=1)
        nm, nv, d = _adamw(g, w_ref[...], m_ref[...], v_ref[...])
        g_ref[...] = g
        nm_ref[...] = nm
        nv_ref[...] = nv
        d_ref[...] = d

    blk = pl.BlockSpec((ncol_blk, depth, cols), lambda j: (0, 0, j))
    return pl.pallas_call(
        body, name=name, grid=(nfeat // cols,),
        in_specs=[pl.BlockSpec((p.shape[0], ncol_blk, cols), lambda j: (0, 0, j)) for p in parts]
        + [blk] * 3,
        out_specs=[blk] * 4,
        out_shape=[jax.ShapeDtypeStruct(w.shape, F32)] * 4,
        compiler_params=_ARB1,
    )(*parts, w, m, v)


def _adamw_sum(parts, w, m, v, tile, name):
    depth, nrow, ncol = w.shape
    rows, cols = tile

    def body(*refs):
        p_refs = refs[:depth]
        w_ref, m_ref, v_ref, g_ref, d_ref, nm_ref, nv_ref = refs[depth:]
        layer = pl.program_id(0)
        for l in range(depth):
            @pl.when(layer == l)
            def _(p_ref=p_refs[l]):
                g = _sum_parts(p_ref)
                nm, nv, d = _adamw(g, w_ref[...], m_ref[...], v_ref[...])
                g_ref[...] = g
                nm_ref[...] = nm
                nv_ref[...] = nv
                d_ref[...] = d

    def part_spec(l):
        return pl.BlockSpec((parts[l].shape[0], rows, cols), lambda q, i, j: (
            0, jnp.where(q == l, i, 0), jnp.where(q == l, j, 0)))

    blk = pl.BlockSpec((None, rows, cols), lambda q, i, j: (q, i, j))
    return pl.pallas_call(
        body, name=name, grid=(depth, nrow // rows, ncol // cols),
        in_specs=[part_spec(l) for l in range(depth)] + [blk, blk, blk],
        out_specs=[blk] * 4,
        out_shape=[jax.ShapeDtypeStruct(w.shape, F32)] * 4,
        compiler_params=pltpu.CompilerParams(
            dimension_semantics=("arbitrary", "arbitrary", "arbitrary")),
    )(*parts, w, m, v)


def _adamw_small(parts, ws, ms, vs, name):
    n = len(parts)

    def body(*refs):
        p, w, m, v = (refs[a * n:(a + 1) * n] for a in range(4))
        out = refs[4 * n:]
        for a in range(n):
            g = _sum_parts(p[a])
            nm, nv, d = _adamw(g, w[a][...], m[a][...], v[a][...])
            out[a][...] = g
            out[n + a][...] = d
            out[2 * n + a][...] = nm
            out[3 * n + a][...] = nv

    vmem = pl.BlockSpec(memory_space=pltpu.VMEM)
    outs = pl.pallas_call(
        body, name=name, in_specs=[vmem] * (4 * n), out_specs=[vmem] * (4 * n),
        out_shape=[jax.ShapeDtypeStruct(w.shape, F32) for w in ws] * 4,
    )(*parts, *ws, *ms, *vs)
    return [list(outs[a * n:(a + 1) * n]) for a in range(4)]


def _misc_rows(b_forget, extra=None):
    tile = jnp.pad(b_forget, ((0, 6), (0, LANES - FOX_HEADS)))
    return tile if extra is None else tile.at[2].set(extra)


def kernel(x, mem, norm_w, w_in, b_forget, mem_norm_w, w_mem_kv, out_norm_w, w_out, final_norm_w, loss_target, m_norm_w, m_w_in, m_b_forget, m_mem_norm_w, m_w_mem_kv, m_out_norm_w, m_w_out, m_final_norm_w, v_norm_w, v_w_in, v_b_forget, v_mem_norm_w, v_w_mem_kv, v_out_norm_w, v_w_out, v_final_norm_w):
    kv_rows = w_mem_kv.shape[1]
    out_rows = w_out.shape[1]

    def shards(l):
        return [w_in[l].T.astype(BF16), w_mem_kv[l].astype(BF16), w_out[l].astype(BF16)]

    def full_kv_out(g_kv, g_out):
        return g_kv.reshape(D_MODEL, 2 * MEM_W), g_out.reshape(MIX_W, D_MODEL)

    def kv_blocks(g):
        return g.reshape(N_DEV, kv_rows, 2 * MEM_W).astype(BF16)

    def out_blocks(g):
        return g.reshape(N_DEV, out_rows, D_MODEL).astype(BF16)

    def row(v):
        return v.reshape(1, -1)

    def cols(first, size):
        return (1, first, size)

    fwd_split, bwd_split = (3 * LANES, 6 * LANES), (5 * LANES, 6 * LANES)

    s_in0, s_kv0, s_out0 = shards(0)
    (g_in0,) = _all_gather_weights([s_in0], "all_gather_l0")
    w_r0 = _assemble_w_in(g_in0, False, "assemble_w_in_0")
    s_in1, s_kv1, s_out1 = shards(1)
    x1, saved0, (l_in1, _, _), (wkv0, wout0) = _layer_fwd(
        x[0], mem[0], row(norm_w[0]), w_r0, b_forget[0], row(mem_norm_w[0]),
        lambda lands: full_kv_out(lands[1], lands[2]),
        row(out_norm_w[0]), 0,
        travel=([s_in1, s_kv0, s_out0], False,
                [([0, 1], [cols(0, fwd_split[0]), None]),
                 ([0], [cols(fwd_split[0], fwd_split[1] - fwd_split[0])]),
                 ([2], [None]),
                 ([0], [cols(fwd_split[1], D_MODEL - fwd_split[1])])]))
    w_r1 = _assemble_w_in(l_in1, False, "assemble_w_in_1")
    x2, saved1, _, (wkv1, wout1) = _layer_fwd(
        x1, mem[0], row(norm_w[1]), w_r1, b_forget[1], row(mem_norm_w[1]),
        lambda lands: full_kv_out(lands[0], lands[1]),
        row(out_norm_w[1]), 1, travel=([s_kv1, s_out1], False, [([0, 1], [None, None]), None]))

    dx2, loss_part, dfnw = _final_fwd_bwd(x2, row(final_norm_w), loss_target[0], "final_fwd_bwd")

    dw_out_blocks = jax.ShapeDtypeStruct((N_DEV, out_rows, D_MODEL), BF16)
    dx1, gr1, (l_out1,) = _layer_bwd(
        dx2, saved1, mem[0], w_r1, wkv1, wout1, 1,
        travel=([dw_out_blocks], True, [([0], [None])]), own_w_out=(0, out_blocks))
    p_in1 = _assemble_w_in(gr1["w_r"], True, "w_in_grad_blocks_1")
    p_kv1 = kv_blocks(gr1["w_mem_kv"])
    grad_x, gr0, (l_in1, l_kv1, l_out0) = _layer_bwd(
        dx1[0], saved0, mem[0], w_r0, wkv0, wout0, 0,
        travel=([p_in1, p_kv1, dw_out_blocks], True,
                [([0, 1], [cols(0, bwd_split[0]), None]),
                 ([0, 2], [cols(bwd_split[0], bwd_split[1] - bwd_split[0]), None]),
                 None,
                 ([0], [cols(bwd_split[1], D_MODEL - bwd_split[1])])]),
        own_w_out=(2, out_blocks))
    r_out1, r_in1, r_kv1, r_out0 = l_out1, l_in1, l_kv1, l_out0

    def both(name):
        return jnp.stack([gr0[name], gr1[name]])

    small = [both("norm_w"), both("mem_norm_w"), both("out_norm_w"), dfnw,
             _misc_rows(both("b_forget"), loss_part[0])]
    p_small = [jnp.broadcast_to(v[None], (N_DEV,) + v.shape) for v in small]
    by_core = [v.reshape((N_CHIP, 2) + v.shape[1:])
               for v in (_assemble_w_in(gr0["w_r"], True, "w_in_grad_blocks_0"),
                         kv_blocks(gr0["w_mem_kv"]))]
    from_sibling = _pair_swap(by_core, "grads_l0_pair_swap")
    core = lax.axis_index("c").astype(jnp.int32).reshape(1)
    chip_sums = _pair_add(by_core, from_sibling, core, "grads_l0_pair_add")
    r_in0, r_kv0, *r_small = _chip_exchange(chip_sums, p_small, "exchange_grads_l0")

    def view(v):
        return jnp.transpose(v, (2, 0, 1))

    g_w_in, d_w_in, nm_w_in, nv_w_in = [jnp.transpose(v, (1, 2, 0)) for v in _adamw_w_in(
        [r_in0, r_in1], view(w_in), view(m_w_in), view(v_w_in), "adamw_w_in")]
    g_w_kv, d_w_kv, nm_w_kv, nv_w_kv = _adamw_sum(
        [r_kv0, r_kv1], w_mem_kv, m_w_mem_kv, v_w_mem_kv, (kv_rows, 2 * MEM_W), "adamw_w_mem_kv")
    g_w_out, d_w_out, nm_w_out, nv_w_out = _adamw_sum(
        [r_out0, r_out1], w_out, m_w_out, v_w_out, (out_rows, D_MODEL), "adamw_w_out")
    def smalls(nw, mnw, onw, fnw, b):
        return [nw, mnw, onw, row(fnw), _misc_rows(b)]

    small_out = _adamw_small(
        r_small, smalls(norm_w, mem_norm_w, out_norm_w, final_norm_w, b_forget),
        smalls(m_norm_w, m_mem_norm_w, m_out_norm_w, m_final_norm_w, m_b_forget),
        smalls(v_norm_w, v_mem_norm_w, v_out_norm_w, v_final_norm_w, v_b_forget), "adamw_small")
    (g_nw, g_mnw, g_onw, g_fnw, g_b), (d_nw, d_mnw, d_onw, d_fnw, d_b), \
        (nm_nw, nm_mnw, nm_onw, nm_fnw, nm_b), (nv_nw, nv_mnw, nv_onw, nv_fnw, nv_b) = [
            (nw, mnw, onw, fnw[0], misc[:2, :FOX_HEADS]) for nw, mnw, onw, fnw, misc in small_out]
    loss = small_out[0][4][2, 0]

    return (loss, grad_x,
            g_nw, g_w_in, g_b, g_mnw, g_w_kv, g_onw, g_w_out, g_fnw,
            d_nw, d_w_in, d_b, d_mnw, d_w_kv, d_onw, d_w_out, d_fnw,
            nm_nw, nm_w_in, nm_b, nm_mnw, nm_w_kv, nm_onw, nm_w_out, nm_fnw,
            nv_nw, nv_w_in, nv_b, nv_mnw, nv_w_kv, nv_onw, nv_w_out, nv_fnw)
```

```python
import functools
from typing import NamedTuple

import jax
import jax.numpy as jnp
from jax import lax
from jax.experimental import pallas as pl
from jax.experimental.pallas import tpu as pltpu

F32 = jnp.float32
BF16 = jnp.bfloat16

N_DEV = 8
D_MODEL = 1024
HEAD_DIM = 64
LANES = 128
SB_W = 512
FX_W = 512
MEM_W = 256
MIX_W = 1280
FOX_HEADS = 8
IN_W = 4616
SHARD_W = IN_W // N_DEV
QKV_W = 3 * SB_W + 3 * FX_W + MEM_W
FL_PAD = 256
GF_W = MIX_W + FL_PAD
WR_W = QKV_W + GF_W
EPS = 1e-6
T = 256
QPS = 4
TM = 256
TQM = 512
Q_SCALE = 0.125
NEG = -1e30
UNDERFLOW = -110.0
NORM_SLACK = 1.01

ADAM_LR = 0.001
ADAM_B1 = 0.9
ADAM_B2 = 0.999
ADAM_EPS = 1e-08
ADAM_WD = 0.01
ADAM_STEP = 10


_NT = (((1,), (1,)), ((), ()))
_TN = (((0,), (0,)), ((), ()))

_ARB1 = pltpu.CompilerParams(dimension_semantics=("arbitrary",))
_ARB2 = pltpu.CompilerParams(dimension_semantics=("arbitrary", "arbitrary"))


def _dot(a, b):
    return jnp.dot(a, b, preferred_element_type=F32)


def _dot_nt(a, b):
    return lax.dot_general(a, b, _NT, preferred_element_type=F32)


def _dot_tn(a, b):
    return lax.dot_general(a, b, _TN, preferred_element_type=F32)


def _split2(x):
    hi = x.astype(BF16)
    lo = (x - hi.astype(F32)).astype(BF16)
    return hi, lo


def _stack2(u):
    return jnp.concatenate([u, u], axis=0)


def _cum2(x, u2):
    hi, lo = _split2(x)
    return _dot(jnp.concatenate([hi, lo], axis=1), u2)


def _tri3(tri, x, dot=None):
    dot = dot or _dot
    hi = x.astype(BF16)
    r1 = x - hi.astype(F32)
    mid = r1.astype(BF16)
    lo = (r1 - mid.astype(F32)).astype(BF16)
    return dot(tri, hi) + dot(tri, mid) + dot(tri, lo)


def _iota2(shape, dim):
    return lax.broadcasted_iota(jnp.int32, shape, dim)


def _head_block_diag():
    r = _iota2((LANES, LANES), 0) // HEAD_DIM
    c = _iota2((LANES, LANES), 1) // HEAD_DIM
    return _stack2(jnp.where(r == c, 1.0, 0.0).astype(BF16))


def _head_mean(x, bd):
    return _cum2(x, bd) * (1.0 / HEAD_DIM)


def _sigmoid(x):
    return 1.0 / (1.0 + jnp.exp(-x))


def _log_sigmoid(x):
    return jnp.minimum(x, 0.0) - jnp.log(1.0 + jnp.exp(-jnp.abs(x)))


def _running_top(r_ref):
    return jnp.max(jnp.maximum(r_ref[0], r_ref[1]))


def _fox_tiles_left(i, pair, nq, fb_ref, tile):
    def bound(j):
        b = []
        for h in range(2):
            head = 2 * pair + h
            b.append(2.0 * NORM_SLACK * fb_ref[2 * nq + i, head] * fb_ref[3 * nq, head]
                     + fb_ref[2 * i, head] - fb_ref[2 * j + 1, head])
        return jnp.maximum(b[0], b[1])

    def more(j):
        return jnp.logical_and(j >= 0, bound(jnp.maximum(j, 0)) > UNDERFLOW)

    def step(j):
        tile(j, False)
        return j - 1

    return lax.while_loop(more, step, i - 1)


def _pair_masks():
    lane = _iota2((1, LANES), 1)
    return lane < HEAD_DIM


def _split_pair(x, m0):
    zero = jnp.zeros_like(x)
    return jnp.where(m0, x, zero), jnp.where(m0, zero, x)


def _inproj_fwd(x, nw, w_r, name, rider=None):
    s = x.shape[0]

    def body(x_ref, nw_ref, w_ref, qkv_ref, gf_ref):
        xv = x_ref[...]
        r = lax.rsqrt(jnp.mean(xv * xv, axis=-1, keepdims=True) + EPS)
        h = (xv * r * nw_ref[...]).astype(BF16)
        for c in range(0, QKV_W, 256):
            qkv_ref[:, c:c + 256] = _dot_nt(h, w_ref[c:c + 256, :]).astype(BF16)
        for c in range(0, GF_W, 256):
            gf_ref[:, c:c + 256] = _dot_nt(h, w_ref[QKV_W + c:QKV_W + c + 256, :])

    (qkv, gf), lands = _ride(dict(
        body=body, name=name, grid=(s // TM,),
        in_specs=[pl.BlockSpec((TM, D_MODEL), lambda i: (i, 0)),
                  pl.BlockSpec((1, D_MODEL), lambda i: (0, 0)),
                  pl.BlockSpec((WR_W, D_MODEL), lambda i: (0, 0))],
        out_specs=[pl.BlockSpec((TM, QKV_W), lambda i: (i, 0)),
                   pl.BlockSpec((TM, GF_W), lambda i: (i, 0))],
        out_shape=[jax.ShapeDtypeStruct((s, QKV_W), BF16), jax.ShapeDtypeStruct((s, GF_W), F32)],
        scratch_shapes=[], compiler_params=_ARB1, operands=[x, nw, w_r]), rider)
    return qkv, gf, lands


def _fox_prep_fwd(gf, qkv, bpad, name):
    s = gf.shape[0]
    nq = s // T
    nrow = -(-(3 * nq + 1) // 8) * 8

    def body(fl_ref, q_ref, k_ref, b_ref, fq_ref, fr_ref, fb_ref):
        tri = jnp.where(_iota2((T, T), 0) >= _iota2((T, T), 1), 1.0, 0.0).astype(BF16)
        m0 = _pair_masks()
        lane = _iota2((1, LANES), 1)
        norms = [jnp.zeros((1, LANES), F32) for _ in range(nq + 1)]
        same_head = (_iota2((LANES, LANES), 0) // HEAD_DIM) == (_iota2((LANES, LANES), 1) // HEAD_DIM)
        bd = jnp.where(same_head, 1.0, 0.0).astype(BF16)
        for p in range(FOX_HEADS // 2):
            cols = slice(p * LANES, (p + 1) * LANES)
            q = (q_ref[:, cols] * jnp.asarray(Q_SCALE, BF16)).astype(F32)
            k = k_ref[:, cols].astype(F32)
            qn = _dot((q * q).astype(BF16), bd)
            kn = _dot((k * k).astype(BF16), bd)
            tops = [jnp.max(qn[j * T:(j + 1) * T], axis=0, keepdims=True) for j in range(nq)]
            tops.append(jnp.max(kn, axis=0, keepdims=True))
            tops = [jnp.sqrt(top) for top in tops]
            for h in range(2):
                at = h * HEAD_DIM
                norms = [jnp.where(lane == 2 * p + h, top[:, at:at + 1], row)
                         for top, row in zip(tops, norms)]
        for j in range(nq + 1):
            fb_ref[2 * nq + j:2 * nq + j + 1, :] = norms[j]
        fb_ref[3 * nq + 1:, :] = jnp.zeros((nrow - 3 * nq - 1, LANES), F32)
        carry = jnp.zeros((1, LANES), F32)
        for blk in range(s // T):
            rows = slice(blk * T, (blk + 1) * T)
            lf = _log_sigmoid(fl_ref[rows, :] + b_ref[...])
            c = _tri3(tri, lf) + carry
            carry = c[T - 1:T, :]
            for p in range(FOX_HEADS // 2):
                fq_ref[rows, p * LANES:(p + 1) * LANES] = jnp.where(
                    m0, c[:, 2 * p:2 * p + 1], c[:, 2 * p + 1:2 * p + 2])
            fr_ref[:, rows] = c.T[0:FOX_HEADS, :]
            fb_ref[2 * blk:2 * blk + 1, :] = c[0:1, :]
            fb_ref[2 * blk + 1:2 * blk + 2, :] = carry

    base = 3 * SB_W // FX_W
    return pl.pallas_call(
        body, name=name, grid=(1,),
        in_specs=[pl.BlockSpec((s, LANES), lambda i: (0, MIX_W // LANES)),
                  pl.BlockSpec((s, FX_W), lambda i: (0, base)),
                  pl.BlockSpec((s, FX_W), lambda i: (0, base + 1)),
                  pl.BlockSpec((1, LANES), lambda i: (0, 0))],
        out_specs=[pl.BlockSpec((s, FX_W), lambda i: (0, 0)),
                   pl.BlockSpec((FOX_HEADS, s), lambda i: (0, 0)),
                   pl.BlockSpec((nrow, LANES), lambda i: (0, 0))],
        out_shape=[jax.ShapeDtypeStruct((s, FX_W), F32), jax.ShapeDtypeStruct((FOX_HEADS, s), F32),
                   jax.ShapeDtypeStruct((nrow, LANES), F32)],
        compiler_params=_ARB1,
    )(gf, qkv, qkv, bpad)


def _sb_fwd(qkv, name, rider=None):
    s = qkv.shape[0]

    def body(q_ref, k_ref, v_ref, o_ref, acc_ref, r_ref, as_ref):
        m0 = _pair_masks()
        strict = _iota2((T, T), 0) > _iota2((T, T), 1)
        u2 = _stack2(jnp.where(strict, 1.0, 0.0).astype(BF16))
        hs = range(2)

        def query_tile(i, rows):
            qh = _split_pair(q_ref[rows, :] * jnp.asarray(Q_SCALE, BF16), m0)
            acc_ref[...] = jnp.zeros_like(acc_ref)
            r_ref[...] = jnp.zeros_like(r_ref)

            def flush(j):
                v = v_ref[pl.ds(pl.multiple_of(j * T, T), T), :]
                for h in hs:
                    acc_ref[h] += _dot(as_ref[h], v)

            def tile(j, diag):
                k = k_ref[pl.ds(pl.multiple_of(j * T, T), T), :]
                z = [_dot_nt(qh[h], k) for h in hs]
                if not diag:
                    flush(j + 1)
                la = [jnp.minimum(z[h], 0.0) - jnp.log(1.0 + jnp.exp(-jnp.abs(z[h]))) for h in hs]
                lf = [la[h] - z[h] for h in hs]
                if diag:
                    lf = [jnp.where(strict, lf[h], 0.0) for h in hs]
                cin = [_cum2(lf[h], u2) for h in hs]
                a = [jnp.exp(la[h] + cin[h] + r_ref[h]) for h in hs]
                if diag:
                    a = [jnp.where(strict, a[h], 0.0) for h in hs]
                for h in hs:
                    r_ref[h] += cin[h][:, 0:1] + lf[h][:, 0:1]
                    as_ref[h] = a[h].astype(BF16)

            tile(i, True)

            def more(state):
                j, top = state
                return jnp.logical_and(j >= 0, top > UNDERFLOW)

            def step(state):
                j, _ = state
                tile(j, False)
                return j - 1, _running_top(r_ref)

            j_left, _ = lax.while_loop(more, step, (i - 1, _running_top(r_ref)))
            flush(j_left + 1)
            o_ref[rows, :] = jnp.where(m0, acc_ref[0], acc_ref[1])

        for n in range(QPS):
            query_tile(QPS * pl.program_id(1) + n, slice(n * T, (n + 1) * T))

    nb = SB_W // LANES
    (ysb,), lands = _ride(dict(
        body=body, name=name, grid=(nb, s // (QPS * T)),
        in_specs=[pl.BlockSpec((QPS * T, LANES), lambda p, i: (i, p)),
                  pl.BlockSpec((s, LANES), lambda p, i: (0, nb + p)),
                  pl.BlockSpec((s, LANES), lambda p, i: (0, 2 * nb + p))],
        out_specs=[pl.BlockSpec((QPS * T, LANES), lambda p, i: (i, p))],
        out_shape=[jax.ShapeDtypeStruct((s, SB_W), F32)],
        scratch_shapes=[pltpu.VMEM((2, T, LANES), F32), pltpu.VMEM((2, T, 1), F32),
                        pltpu.VMEM((2, T, T), BF16)],
        compiler_params=_ARB2, operands=[qkv, qkv, qkv]), rider)
    return ysb, lands


def _fox_fwd(qkv, fqb, frow, fbounds, name, rider=None):
    s = qkv.shape[0]

    def body(q_ref, k_ref, v_ref, fq_ref, fr_ref, fb_ref, o_ref, lse_ref, acc_ref, m_ref, ps_ref):
        pair = pl.program_id(0)
        m0 = _pair_masks()
        causal = _iota2((T, T), 0) >= _iota2((T, T), 1)
        ones = jnp.ones((T, LANES), BF16)
        hs = range(2)

        def query_tile(i, rows):
            qh = _split_pair(q_ref[rows, :] * jnp.asarray(Q_SCALE, BF16), m0)
            fq = fq_ref[rows, :]
            fqh = (fq[:, 0:1], fq[:, HEAD_DIM:HEAD_DIM + 1])
            acc_ref[...] = jnp.zeros_like(acc_ref)
            m_ref[...] = jnp.full_like(m_ref, NEG)

            def flush(j):
                v = v_ref[pl.ds(pl.multiple_of(j * T, T), T), :]
                va2 = _stack2(jnp.concatenate([v, ones], axis=1))
                for h in hs:
                    acc_ref[h] += _dot(ps_ref[h], va2)

            def tile(j, diag):
                off = pl.multiple_of(j * T, T)
                k = k_ref[pl.ds(off, T), :]
                sc = [_dot_nt(qh[h], k) + fqh[h] - fr_ref[h:h + 1, pl.ds(off, T)] for h in hs]
                if not diag:
                    flush(j + 1)
                if diag:
                    sc = [jnp.where(causal, sc[h], NEG) for h in hs]
                m_new = [jnp.maximum(m_ref[h], jnp.max(sc[h], axis=1, keepdims=True)) for h in hs]
                p = [jnp.exp(sc[h] - m_new[h]) for h in hs]
                for h in hs:
                    acc_ref[h] = acc_ref[h] * jnp.exp(m_ref[h] - m_new[h])
                    m_ref[h] = m_new[h]
                    p_hi, p_lo = _split2(p[h])
                    ps_ref[h] = jnp.concatenate([p_hi, p_lo], axis=1)

            tile(i, True)
            j_left = _fox_tiles_left(i, pair, s // T, fb_ref, tile)
            flush(j_left + 1)
            acc = (acc_ref[0], acc_ref[1])
            o_ref[rows, :] = jnp.where(m0, acc[0][:, :LANES] / acc[0][:, LANES:],
                                       acc[1][:, :LANES] / acc[1][:, LANES:])
            lse_ref[rows, :] = jnp.where(m0, m_ref[0] + jnp.log(acc[0][:, LANES:]),
                                         m_ref[1] + jnp.log(acc[1][:, LANES:]))

        for n in range(QPS):
            query_tile(QPS * pl.program_id(1) + n, slice(n * T, (n + 1) * T))

    nb = FX_W // LANES
    base = 3 * SB_W // LANES
    (yfx, lse), lands = _ride(dict(
        body=body, name=name, grid=(nb, s // (QPS * T)),
        in_specs=[pl.BlockSpec((QPS * T, LANES), lambda p, i: (i, base + p)),
                  pl.BlockSpec((s, LANES), lambda p, i: (0, base + nb + p)),
                  pl.BlockSpec((s, LANES), lambda p, i: (0, base + 2 * nb + p)),
                  pl.BlockSpec((QPS * T, LANES), lambda p, i: (i, p)),
                  pl.BlockSpec((None, 2, s), lambda p, i: (p, 0, 0)),
                  pl.BlockSpec(memory_space=pltpu.SMEM)],
        out_specs=[pl.BlockSpec((QPS * T, LANES), lambda p, i: (i, p)),
                   pl.BlockSpec((QPS * T, LANES), lambda p, i: (i, p))],
        out_shape=[jax.ShapeDtypeStruct((s, FX_W), F32), jax.ShapeDtypeStruct((s, FX_W), F32)],
        scratch_shapes=[pltpu.VMEM((2, T, 2 * LANES), F32), pltpu.VMEM((2, T, 1), F32),
                        pltpu.VMEM((2, T, 2 * T), BF16)],
        compiler_params=_ARB2, operands=[qkv, qkv, qkv, fqb, frow, fbounds]), rider)
    return yfx, lse, lands


def _memkv_fwd(mem, mnw, wkv, name):
    n = mem.shape[0]

    def body(mem_ref, mnw_ref, w_ref, kv_ref):
        mv = mem_ref[...]
        r = lax.rsqrt(jnp.mean(mv * mv, axis=-1, keepdims=True) + EPS)
        hm = (mv * r * mnw_ref[...]).astype(BF16)
        kv_ref[...] = _dot(hm, w_ref[...]).astype(BF16)

    return pl.pallas_call(
        body, name=name, grid=(1,),
        in_specs=[pl.BlockSpec((n, D_MODEL), lambda i: (0, 0)),
                  pl.BlockSpec((1, D_MODEL), lambda i: (0, 0)),
                  pl.BlockSpec((D_MODEL, 2 * MEM_W), lambda i: (0, 0))],
        out_specs=pl.BlockSpec((n, 2 * MEM_W), lambda i: (0, 0)),
        out_shape=jax.ShapeDtypeStruct((n, 2 * MEM_W), BF16),
        compiler_params=_ARB1,
    )(mem, mnw, wkv)


def _mem_fwd(qkv, kv, name):
    s = qkv.shape[0]
    n = kv.shape[0]

    def body(q_ref, k_ref, v_ref, o_ref, lse_ref):
        m0 = _pair_masks()
        qh = _split_pair(q_ref[...] * jnp.asarray(Q_SCALE, BF16), m0)
        k = k_ref[...]
        v = v_ref[...]
        outs, lses = [], []
        for h in range(2):
            sc = _dot_nt(qh[h], k)
            mx = jnp.max(sc, axis=1, keepdims=True)
            p = jnp.exp(sc - mx)
            l = jnp.sum(p, axis=1, keepdims=True)
            outs.append(_dot(p.astype(BF16), v) / l)
            lses.append(mx + jnp.log(l))
        o_ref[...] = jnp.where(m0, outs[0], outs[1])
        lse_ref[...] = jnp.where(m0, lses[0], lses[1])

    nb = MEM_W // LANES
    base = (3 * SB_W + 3 * FX_W) // LANES
    return pl.pallas_call(
        body, name=name, grid=(nb, s // TQM),
        in_specs=[pl.BlockSpec((TQM, LANES), lambda p, i: (i, base + p)),
                  pl.BlockSpec((n, LANES), lambda p, i: (0, p)),
                  pl.BlockSpec((n, LANES), lambda p, i: (0, nb + p))],
        out_specs=[pl.BlockSpec((TQM, LANES), lambda p, i: (i, p)),
                   pl.BlockSpec((TQM, LANES), lambda p, i: (i, p))],
        out_shape=[jax.ShapeDtypeStruct((s, MEM_W), F32), jax.ShapeDtypeStruct((s, MEM_W), F32)],
        compiler_params=_ARB2,
    )(qkv, kv, kv)


def _mix_chunk(c, ysb_ref, yfx_ref, ym_ref):
    if c < SB_W // LANES:
        return ysb_ref[:, c * LANES:(c + 1) * LANES]
    c -= SB_W // LANES
    if c < FX_W // LANES:
        return yfx_ref[:, c * LANES:(c + 1) * LANES]
    c -= FX_W // LANES
    return ym_ref[:, c * LANES:(c + 1) * LANES]


def _outproj_fwd(ysb, yfx, ym, gf, onw, wout, x, name):
    s = x.shape[0]

    def body(ysb_ref, yfx_ref, ym_ref, g_ref, onw_ref, w_ref, x_ref, o_ref, yg_ref):
        bd = _head_block_diag()
        for c in range(MIX_W // LANES):
            sl = slice(c * LANES, (c + 1) * LANES)
            u = _mix_chunk(c, ysb_ref, yfx_ref, ym_ref)
            r = lax.rsqrt(_head_mean(u * u, bd) + EPS)
            g = g_ref[:, sl]
            yg_ref[:, sl] = (u * r * onw_ref[:, sl] * (g * _sigmoid(g))).astype(BF16)
        o_ref[...] = x_ref[...] + _dot(yg_ref[...], w_ref[...])

    return pl.pallas_call(
        body, name=name, grid=(s // TM,),
        in_specs=[pl.BlockSpec((TM, SB_W), lambda i: (i, 0)),
                  pl.BlockSpec((TM, FX_W), lambda i: (i, 0)),
                  pl.BlockSpec((TM, MEM_W), lambda i: (i, 0)),
                  pl.BlockSpec((TM, MIX_W), lambda i: (i, 0)),
                  pl.BlockSpec((1, MIX_W), lambda i: (0, 0)),
                  pl.BlockSpec((MIX_W, D_MODEL), lambda i: (0, 0)),
                  pl.BlockSpec((TM, D_MODEL), lambda i: (i, 0))],
        out_specs=pl.BlockSpec((TM, D_MODEL), lambda i: (i, 0)),
        out_shape=jax.ShapeDtypeStruct((s, D_MODEL), F32),
        scratch_shapes=[pltpu.VMEM((TM, MIX_W), BF16)],
        compiler_params=_ARB1,
    )(ysb, yfx, ym, gf, onw, wout, x)


def _final_fwd_bwd(x, fnw, target, name):
    s = x.shape[0]

    def body(x_ref, w_ref, t_ref, dx_ref, loss_ref, dw_ref):
        @pl.when(pl.program_id(0) == 0)
        def _():
            loss_ref[...] = jnp.zeros_like(loss_ref)
            dw_ref[...] = jnp.zeros_like(dw_ref)

        xv = x_ref[...]
        w = w_ref[...]
        r = lax.rsqrt(jnp.mean(xv * xv, axis=-1, keepdims=True) + EPS)
        xh = xv * r
        err = xh * w - t_ref[...]
        part = jnp.sum(jnp.sum(err * err, axis=1, keepdims=True), axis=0, keepdims=True)
        loss_ref[...] += part * (0.5 / D_MODEL)
        dy = err * (1.0 / D_MODEL)
        dw_ref[...] += jnp.sum(dy * xh, axis=0, keepdims=True)
        dxh = dy * w
        dx_ref[...] = r * (dxh - xh * jnp.mean(dxh * xh, axis=-1, keepdims=True))

    return pl.pallas_call(
        body, name=name, grid=(s // TM,),
        in_specs=[pl.BlockSpec((TM, D_MODEL), lambda i: (i, 0)),
                  pl.BlockSpec((1, D_MODEL), lambda i: (0, 0)),
                  pl.BlockSpec((TM, D_MODEL), lambda i: (i, 0))],
        out_specs=[pl.BlockSpec((TM, D_MODEL), lambda i: (i, 0)),
                   pl.BlockSpec((1, LANES), lambda i: (0, 0)),
                   pl.BlockSpec((1, D_MODEL), lambda i: (0, 0))],
        out_shape=[jax.ShapeDtypeStruct((s, D_MODEL), F32), jax.ShapeDtypeStruct((1, LANES), F32),
                   jax.ShapeDtypeStruct((1, D_MODEL), F32)],
        compiler_params=_ARB1,
    )(x, fnw, target)


def _outproj_bwd(dxo, wout, ysb, yfx, ym, gf, onw, name, rider=None):
    s = dxo.shape[0]

    def body(dx_ref, w_ref, ysb_ref, yfx_ref, ym_ref, g_ref, onw_ref,
             dysb_ref, dyfx_ref, dym_ref, dg_ref, dw_ref, donw_ref, yg_ref):
        @pl.when(pl.program_id(0) == 0)
        def _():
            dw_ref[...] = jnp.zeros_like(dw_ref)
            donw_ref[...] = jnp.zeros_like(donw_ref)

        dxb = dx_ref[...].astype(BF16)
        dyg = _dot_nt(dxb, w_ref[...])
        bd = _head_block_diag()
        for c in range(MIX_W // LANES):
            sl = slice(c * LANES, (c + 1) * LANES)
            u = _mix_chunk(c, ysb_ref, yfx_ref, ym_ref)
            r = lax.rsqrt(_head_mean(u * u, bd) + EPS)
            yn = u * r
            g = g_ref[:, sl]
            sg = _sigmoid(g)
            sil = g * sg
            onw = onw_ref[:, sl]
            e = dyg[:, sl]
            yg_ref[:, sl] = (yn * onw * sil).astype(BF16)
            donw_ref[:, sl] += jnp.sum(e * yn * sil, axis=0, keepdims=True)
            dg_ref[:, sl] = (e * yn * onw * (sg * (1.0 + g * (1.0 - sg)))).astype(BF16)
            dyn = e * onw * sil
            du = (r * (dyn - yn * _head_mean(dyn * yn, bd))).astype(BF16)
            if c < 4:
                dysb_ref[:, c * LANES:(c + 1) * LANES] = du
            elif c < 8:
                dyfx_ref[:, (c - 4) * LANES:(c - 3) * LANES] = du
            else:
                dym_ref[:, (c - 8) * LANES:(c - 7) * LANES] = du
        dw_ref[...] += _dot_tn(yg_ref[...], dxb)

    outs, lands = _ride(dict(
        body=body, name=name, grid=(s // TM,),
        in_specs=[pl.BlockSpec((TM, D_MODEL), lambda i: (i, 0)),
                  pl.BlockSpec((MIX_W, D_MODEL), lambda i: (0, 0)),
                  pl.BlockSpec((TM, SB_W), lambda i: (i, 0)),
                  pl.BlockSpec((TM, FX_W), lambda i: (i, 0)),
                  pl.BlockSpec((TM, MEM_W), lambda i: (i, 0)),
                  pl.BlockSpec((TM, MIX_W), lambda i: (i, 0)),
                  pl.BlockSpec((1, MIX_W), lambda i: (0, 0))],
        out_specs=[pl.BlockSpec((TM, SB_W), lambda i: (i, 0)),
                   pl.BlockSpec((TM, FX_W), lambda i: (i, 0)),
                   pl.BlockSpec((TM, MEM_W), lambda i: (i, 0)),
                   pl.BlockSpec((TM, MIX_W), lambda i: (i, 0)),
                   pl.BlockSpec((MIX_W, D_MODEL), lambda i: (0, 0)),
                   pl.BlockSpec((1, MIX_W), lambda i: (0, 0))],
        out_shape=[jax.ShapeDtypeStruct((s, SB_W), BF16), jax.ShapeDtypeStruct((s, FX_W), BF16),
                   jax.ShapeDtypeStruct((s, MEM_W), BF16), jax.ShapeDtypeStruct((s, MIX_W), BF16),
                   jax.ShapeDtypeStruct((MIX_W, D_MODEL), F32), jax.ShapeDtypeStruct((1, MIX_W), F32)],
        scratch_shapes=[pltpu.VMEM((TM, MIX_W), BF16)],
        compiler_params=_ARB1, operands=[dxo, wout, ysb, yfx, ym, gf, onw]), rider)
    return (*outs, lands)


def _row_dots(do, o, m0):
    prod = do.astype(F32) * o
    zero = jnp.zeros_like(prod)
    return (jnp.sum(jnp.where(m0, prod, zero), axis=1, keepdims=True),
            jnp.sum(jnp.where(m0, zero, prod), axis=1, keepdims=True))


def _sb_bwd(qkv, o, do, name, rider=None):
    s = qkv.shape[0]
    nq = s // T

    def body(q_ref, k_ref, v_ref, o_ref, do_ref, dq_ref, dk_ref, dv_ref,
             dqa_ref, dka_ref, dva_ref, rl_ref, rg_ref, dzs_ref, abs_ref):
        step_id = pl.program_id(1)

        @pl.when(step_id == 0)
        def _():
            dka_ref[...] = jnp.zeros_like(dka_ref)
            dva_ref[...] = jnp.zeros_like(dva_ref)

        m0 = _pair_masks()
        strict = _iota2((T, T), 0) > _iota2((T, T), 1)
        u2 = _stack2(jnp.where(strict, 1.0, 0.0).astype(BF16))
        hs = range(2)

        def query_tile(i, rows):
            qh = _split_pair(q_ref[rows, :] * jnp.asarray(Q_SCALE, BF16), m0)
            doh = _split_pair(do_ref[rows, :], m0)
            dsum = _row_dots(do_ref[rows, :], o_ref[rows, :], m0)
            dqa_ref[...] = jnp.zeros_like(dqa_ref)
            rl_ref[...] = jnp.zeros_like(rl_ref)
            rg_ref[...] = jnp.zeros_like(rg_ref)

            def flush(j):
                off = pl.multiple_of(j * T, T)
                k = k_ref[pl.ds(off, T), :]
                for h in hs:
                    dqa_ref[h] += _dot(dzs_ref[h], k)
                dka_ref[pl.ds(off, T), :] += (_dot_tn(dzs_ref[0], qh[0])
                                              + _dot_tn(dzs_ref[1], qh[1]))
                dva_ref[pl.ds(off, T), :] += (_dot_tn(abs_ref[0], doh[0])
                                              + _dot_tn(abs_ref[1], doh[1]))

            def tile(j, diag):
                off = pl.multiple_of(j * T, T)
                k = k_ref[pl.ds(off, T), :]
                v = v_ref[pl.ds(off, T), :]
                z = [_dot_nt(qh[h], k) for h in hs]
                da = [_dot_nt(doh[h], v) for h in hs]
                if not diag:
                    flush(j + 1)
                la = [jnp.minimum(z[h], 0.0) - jnp.log(1.0 + jnp.exp(-jnp.abs(z[h]))) for h in hs]
                lf = [la[h] - z[h] for h in hs]
                if diag:
                    lf = [jnp.where(strict, lf[h], 0.0) for h in hs]
                cin = [_cum2(lf[h], u2) for h in hs]
                a = [jnp.exp(la[h] + cin[h] + rl_ref[h]) for h in hs]
                if diag:
                    a = [jnp.where(strict, a[h], 0.0) for h in hs]
                ab = [a[h].astype(BF16) for h in hs]
                g = [ab[h].astype(F32) * da[h] for h in hs]
                gin = [_cum2(g[h], u2) for h in hs]
                dz = [g[h] - jnp.exp(la[h]) * ((dsum[h] - rg_ref[h]) - gin[h]) for h in hs]
                if diag:
                    dz = [jnp.where(strict, dz[h], 0.0) for h in hs]
                for h in hs:
                    rl_ref[h] += cin[h][:, 0:1] + lf[h][:, 0:1]
                    rg_ref[h] += gin[h][:, 0:1] + g[h][:, 0:1]
                    dzs_ref[h] = dz[h].astype(BF16)
                    abs_ref[h] = ab[h]

            tile(i, True)

            def more(state):
                j, top = state
                return jnp.logical_and(j >= 0, top > UNDERFLOW)

            def step(state):
                j, _ = state
                tile(j, False)
                return j - 1, _running_top(rl_ref)

            j_left, _ = lax.while_loop(more, step, (i - 1, _running_top(rl_ref)))
            flush(j_left + 1)
            dq_ref[rows, :] = (jnp.where(m0, dqa_ref[0], dqa_ref[1]) * Q_SCALE).astype(BF16)

        for n in range(QPS):
            query_tile(QPS * step_id + n, slice(n * T, (n + 1) * T))

        @pl.when(step_id == nq // QPS - 1)
        def _():
            dk_ref[...] = dka_ref[...].astype(BF16)
            dv_ref[...] = dva_ref[...].astype(BF16)

    nb = SB_W // LANES
    (dq, dk, dv), lands = _ride(dict(
        body=body, name=name, grid=(nb, nq // QPS),
        in_specs=[pl.BlockSpec((QPS * T, LANES), lambda p, i: (i, p)),
                  pl.BlockSpec((s, LANES), lambda p, i: (0, nb + p)),
                  pl.BlockSpec((s, LANES), lambda p, i: (0, 2 * nb + p)),
                  pl.BlockSpec((QPS * T, LANES), lambda p, i: (i, p)),
                  pl.BlockSpec((QPS * T, LANES), lambda p, i: (i, p))],
        out_specs=[pl.BlockSpec((QPS * T, LANES), lambda p, i: (i, p)),
                   pl.BlockSpec((s, LANES), lambda p, i: (0, p)),
                   pl.BlockSpec((s, LANES), lambda p, i: (0, p))],
        out_shape=[jax.ShapeDtypeStruct((s, SB_W), BF16)] * 3,
        scratch_shapes=[pltpu.VMEM((2, T, LANES), F32), pltpu.VMEM((s, LANES), F32),
                        pltpu.VMEM((s, LANES), F32), pltpu.VMEM((2, T, 1), F32),
                        pltpu.VMEM((2, T, 1), F32), pltpu.VMEM((2, T, T), BF16),
                        pltpu.VMEM((2, T, T), BF16)],
        compiler_params=_ARB2, operands=[qkv, qkv, qkv, o, do]), rider)
    return dq, dk, dv, lands


def _fox_bwd(qkv, fqb, frow, fbounds, o, lse, do, name, rider=None):
    s = qkv.shape[0]
    nq = s // T

    def body(q_ref, k_ref, v_ref, fq_ref, fr_ref, fb_ref, o_ref, lse_ref, do_ref,
             dq_ref, dk_ref, dv_ref, df_ref, dqa_ref, dka_ref, dva_ref, dfa_ref, dls_ref, pbs_ref):
        step_id = pl.program_id(1)

        @pl.when(step_id == 0)
        def _():
            dka_ref[...] = jnp.zeros_like(dka_ref)
            dva_ref[...] = jnp.zeros_like(dva_ref)
            dfa_ref[...] = jnp.zeros_like(dfa_ref)

        m0 = _pair_masks()
        causal = _iota2((T, T), 0) >= _iota2((T, T), 1)
        hs = range(2)

        def query_tile(i, rows):
            qh = _split_pair(q_ref[rows, :] * jnp.asarray(Q_SCALE, BF16), m0)
            doh = _split_pair(do_ref[rows, :], m0)
            dsum = _row_dots(do_ref[rows, :], o_ref[rows, :], m0)
            fq = fq_ref[rows, :]
            fqh = (fq[:, 0:1], fq[:, HEAD_DIM:HEAD_DIM + 1])
            lse = lse_ref[rows, :]
            lseh = (lse[:, 0:1], lse[:, HEAD_DIM:HEAD_DIM + 1])
            dqa_ref[...] = jnp.zeros_like(dqa_ref)

            def flush(j):
                off = pl.multiple_of(j * T, T)
                k = k_ref[pl.ds(off, T), :]
                for h in hs:
                    dqa_ref[h] += _dot(dls_ref[h], k)
                dka_ref[pl.ds(off, T), :] += (_dot_tn(dls_ref[0], qh[0])
                                              + _dot_tn(dls_ref[1], qh[1]))
                dva_ref[pl.ds(off, T), :] += (_dot_tn(pbs_ref[0], doh[0])
                                              + _dot_tn(pbs_ref[1], doh[1]))

            def tile(j, diag):
                off = pl.multiple_of(j * T, T)
                k = k_ref[pl.ds(off, T), :]
                v = v_ref[pl.ds(off, T), :]
                sc = [_dot_nt(qh[h], k) + fqh[h] - fr_ref[h:h + 1, pl.ds(off, T)] for h in hs]
                dp = [_dot_nt(doh[h], v) for h in hs]
                if not diag:
                    flush(j + 1)
                p = [jnp.exp(sc[h] - lseh[h]) for h in hs]
                if diag:
                    p = [jnp.where(causal, p[h], 0.0) for h in hs]
                dl = [p[h] * (dp[h] - dsum[h]) for h in hs]
                for h in hs:
                    dls_ref[h] = dl[h].astype(BF16)
                    pbs_ref[h] = p[h].astype(BF16)
                    dfa_ref[h:h + 1, pl.ds(off, T)] -= jnp.sum(dl[h], axis=0, keepdims=True)

            tile(i, True)
            j_left = _fox_tiles_left(i, pl.program_id(0), nq, fb_ref, tile)
            flush(j_left + 1)
            dq_ref[rows, :] = (jnp.where(m0, dqa_ref[0], dqa_ref[1]) * Q_SCALE).astype(BF16)

        for n in range(QPS):
            query_tile(QPS * step_id + n, slice(n * T, (n + 1) * T))

        @pl.when(step_id == nq // QPS - 1)
        def _():
            dk_ref[...] = dka_ref[...].astype(BF16)
            dv_ref[...] = dva_ref[...].astype(BF16)
            df_ref[...] = dfa_ref[...]

    nb = FX_W // LANES
    base = 3 * SB_W // LANES
    (dq, dk, dv, df), lands = _ride(dict(
        body=body, name=name, grid=(nb, nq // QPS),
        in_specs=[pl.BlockSpec((QPS * T, LANES), lambda p, i: (i, base + p)),
                  pl.BlockSpec((s, LANES), lambda p, i: (0, base + nb + p)),
                  pl.BlockSpec((s, LANES), lambda p, i: (0, base + 2 * nb + p)),
                  pl.BlockSpec((QPS * T, LANES), lambda p, i: (i, p)),
                  pl.BlockSpec((None, 2, s), lambda p, i: (p, 0, 0)),
                  pl.BlockSpec(memory_space=pltpu.SMEM),
                  pl.BlockSpec((QPS * T, LANES), lambda p, i: (i, p)),
                  pl.BlockSpec((QPS * T, LANES), lambda p, i: (i, p)),
                  pl.BlockSpec((QPS * T, LANES), lambda p, i: (i, p))],
        out_specs=[pl.BlockSpec((QPS * T, LANES), lambda p, i: (i, p)),
                   pl.BlockSpec((s, LANES), lambda p, i: (0, p)),
                   pl.BlockSpec((s, LANES), lambda p, i: (0, p)),
                   pl.BlockSpec((None, 2, s), lambda p, i: (p, 0, 0))],
        out_shape=[jax.ShapeDtypeStruct((s, FX_W), BF16)] * 3
        + [jax.ShapeDtypeStruct((nb, 2, s), F32)],
        scratch_shapes=[pltpu.VMEM((2, T, LANES), F32), pltpu.VMEM((s, LANES), F32),
                        pltpu.VMEM((s, LANES), F32), pltpu.VMEM((2, s), F32),
                        pltpu.VMEM((2, T, T), BF16), pltpu.VMEM((2, T, T), BF16)],
        compiler_params=_ARB2, operands=[qkv, qkv, qkv, fqb, frow, fbounds, o, lse, do]), rider)
    return dq, dk, dv, df, lands


def _fox_prep_bwd(dfrow, gf, bpad, name):
    s = gf.shape[0]

    def body(df_ref, fl_ref, b_ref, dfl_ref, db_ref):
        tri = jnp.where(_iota2((T, T), 0) <= _iota2((T, T), 1), 1.0, 0.0).astype(BF16)
        carry = jnp.zeros((1, LANES), F32)
        db = jnp.zeros((1, LANES), F32)
        fill = jnp.zeros((LANES - FOX_HEADS, T), F32)
        for blk in reversed(range(s // T)):
            rows = slice(blk * T, (blk + 1) * T)
            c = _tri3(tri, jnp.concatenate([df_ref[:, rows], fill], axis=0), _dot_nt) + carry
            carry = c[0:1, :]
            dfl = c / (1.0 + jnp.exp(fl_ref[rows, :] + b_ref[...]))
            dfl_ref[rows, :] = dfl.astype(BF16)
            db = db + jnp.sum(dfl, axis=0, keepdims=True)
        db_ref[...] = db

    return pl.pallas_call(
        body, name=name, grid=(1,),
        in_specs=[pl.BlockSpec((FOX_HEADS, s), lambda i: (0, 0)),
                  pl.BlockSpec((s, LANES), lambda i: (0, MIX_W // LANES)),
                  pl.BlockSpec((1, LANES), lambda i: (0, 0))],
        out_specs=[pl.BlockSpec((s, LANES), lambda i: (0, 0)),
                   pl.BlockSpec((1, LANES), lambda i: (0, 0))],
        out_shape=[jax.ShapeDtypeStruct((s, LANES), BF16), jax.ShapeDtypeStruct((1, LANES), F32)],
        compiler_params=_ARB1,
    )(dfrow, gf, bpad)


def _mem_bwd(qkv, kv, o, lse, do, name):
    s = qkv.shape[0]
    n = kv.shape[0]

    def body(q_ref, k_ref, v_ref, o_ref, lse_ref, do_ref, dq_ref, dk_ref, dv_ref):
        @pl.when(pl.program_id(1) == 0)
        def _():
            dk_ref[...] = jnp.zeros_like(dk_ref)
            dv_ref[...] = jnp.zeros_like(dv_ref)

        m0 = _pair_masks()
        qh = _split_pair(q_ref[...] * jnp.asarray(Q_SCALE, BF16), m0)
        doh = _split_pair(do_ref[...], m0)
        dsum = _row_dots(do_ref[...], o_ref[...], m0)
        lse = lse_ref[...]
        lseh = (lse[:, 0:1], lse[:, HEAD_DIM:HEAD_DIM + 1])
        k = k_ref[...]
        v = v_ref[...]
        dqs = []
        for h in range(2):
            p = jnp.exp(_dot_nt(qh[h], k) - lseh[h])
            dl = p * (_dot_nt(doh[h], v) - dsum[h])
            dlb = dl.astype(BF16)
            dqs.append(_dot(dlb, k))
            dk_ref[...] += _dot_tn(dlb, qh[h])
            dv_ref[...] += _dot_tn(p.astype(BF16), doh[h])
        dq_ref[...] = (jnp.where(m0, dqs[0], dqs[1]) * Q_SCALE).astype(BF16)

    nb = MEM_W // LANES
    base = (3 * SB_W + 3 * FX_W) // LANES
    return pl.pallas_call(
        body, name=name, grid=(nb, s // TQM),
        in_specs=[pl.BlockSpec((TQM, LANES), lambda p, i: (i, base + p)),
                  pl.BlockSpec((n, LANES), lambda p, i: (0, p)),
                  pl.BlockSpec((n, LANES), lambda p, i: (0, nb + p)),
                  pl.BlockSpec((TQM, LANES), lambda p, i: (i, p)),
                  pl.BlockSpec((TQM, LANES), lambda p, i: (i, p)),
                  pl.BlockSpec((TQM, LANES), lambda p, i: (i, p))],
        out_specs=[pl.BlockSpec((TQM, LANES), lambda p, i: (i, p)),
                   pl.BlockSpec((n, LANES), lambda p, i: (0, p)),
                   pl.BlockSpec((n, LANES), lambda p, i: (0, p))],
        out_shape=[jax.ShapeDtypeStruct((s, MEM_W), BF16), jax.ShapeDtypeStruct((n, MEM_W), F32),
                   jax.ShapeDtypeStruct((n, MEM_W), F32)],
        compiler_params=_ARB2,
    )(qkv, kv, kv, o, lse, do)


def _memkv_bwd(mem, mnw, wkv, dk, dv, name):
    n = mem.shape[0]

    def body(mem_ref, mnw_ref, w_ref, dk_ref, dv_ref, dw_ref, dmnw_ref):
        mv = mem_ref[...]
        r = lax.rsqrt(jnp.mean(mv * mv, axis=-1, keepdims=True) + EPS)
        mh = mv * r
        hm = (mh * mnw_ref[...]).astype(BF16)
        dkv = jnp.concatenate([dk_ref[...], dv_ref[...]], axis=1).astype(BF16)
        dw_ref[...] = _dot_tn(hm, dkv)
        dhm = _dot_nt(dkv, w_ref[...])
        dmnw_ref[...] = jnp.sum(dhm * mh, axis=0, keepdims=True)

    return pl.pallas_call(
        body, name=name, grid=(1,),
        in_specs=[pl.BlockSpec((n, D_MODEL), lambda i: (0, 0)),
                  pl.BlockSpec((1, D_MODEL), lambda i: (0, 0)),
                  pl.BlockSpec((D_MODEL, 2 * MEM_W), lambda i: (0, 0)),
                  pl.BlockSpec((n, MEM_W), lambda i: (0, 0)),
                  pl.BlockSpec((n, MEM_W), lambda i: (0, 0))],
        out_specs=[pl.BlockSpec((D_MODEL, 2 * MEM_W), lambda i: (0, 0)),
                   pl.BlockSpec((1, D_MODEL), lambda i: (0, 0))],
        out_shape=[jax.ShapeDtypeStruct((D_MODEL, 2 * MEM_W), F32),
                   jax.ShapeDtypeStruct((1, D_MODEL), F32)],
        compiler_params=_ARB1,
    )(mem, mnw, wkv, dk, dv)


def _inproj_bwd_dx(pieces, w_r, x, nw, dxo, name, rider=None):
    s = x.shape[0]
    n = len(pieces)
    widths = [p.shape[1] for p in pieces]

    def body(*refs):
        piece_refs = refs[:n]
        w_ref, x_ref, nw_ref, dxo_ref, dx_ref, h_ref, dnw_ref, dp_ref = refs[n:]

        @pl.when(pl.program_id(0) == 0)
        def _():
            dnw_ref[...] = jnp.zeros_like(dnw_ref)

        col = 0
        for r, wd in zip(piece_refs, widths):
            dp_ref[:, col:col + wd] = r[...]
            col += wd
        dp_ref[:, col:] = jnp.zeros((TM, WR_W - col), BF16)
        dh = _dot(dp_ref[...], w_ref[...])
        xv = x_ref[...]
        nw = nw_ref[...]
        r = lax.rsqrt(jnp.mean(xv * xv, axis=-1, keepdims=True) + EPS)
        xh = xv * r
        h_ref[...] = (xh * nw).astype(BF16)
        dnw_ref[...] += jnp.sum(dh * xh, axis=0, keepdims=True)
        dxh = dh * nw
        dx_ref[...] = r * (dxh - xh * jnp.mean(dxh * xh, axis=-1, keepdims=True)) + dxo_ref[...]

    (dx, h, dnw, dproj), lands = _ride(dict(
        body=body, name=name, grid=(s // TM,),
        in_specs=[pl.BlockSpec((TM, wd), lambda i: (i, 0)) for wd in widths]
        + [pl.BlockSpec((WR_W, D_MODEL), lambda i: (0, 0)),
           pl.BlockSpec((TM, D_MODEL), lambda i: (i, 0)),
           pl.BlockSpec((1, D_MODEL), lambda i: (0, 0)),
           pl.BlockSpec((TM, D_MODEL), lambda i: (i, 0))],
        out_specs=[pl.BlockSpec((None, TM, D_MODEL), lambda i: (0, i, 0)),
                   pl.BlockSpec((TM, D_MODEL), lambda i: (i, 0)),
                   pl.BlockSpec((1, D_MODEL), lambda i: (0, 0)),
                   pl.BlockSpec((TM, WR_W), lambda i: (i, 0))],
        out_shape=[jax.ShapeDtypeStruct((1, s, D_MODEL), F32), jax.ShapeDtypeStruct((s, D_MODEL), BF16),
                   jax.ShapeDtypeStruct((1, D_MODEL), F32), jax.ShapeDtypeStruct((s, WR_W), BF16)],
        scratch_shapes=[], compiler_params=_ARB1, operands=[*pieces, w_r, x, nw, dxo]), rider)
    return dx, h, dnw, dproj, lands


def _inproj_bwd_dw(h, dproj, name):
    s = dproj.shape[0]
    tn = 256

    def body(h_ref, dp_ref, dw_ref):
        dw_ref[...] = _dot_tn(dp_ref[...], h_ref[...]).astype(BF16)

    return pl.pallas_call(
        body, name=name, grid=(WR_W // tn,),
        in_specs=[pl.BlockSpec((s, D_MODEL), lambda j: (0, 0)),
                  pl.BlockSpec((s, tn), lambda j: (0, j))],
        out_specs=pl.BlockSpec((tn, D_MODEL), lambda j: (j, 0)),
        out_shape=jax.ShapeDtypeStruct((WR_W, D_MODEL), BF16),
        compiler_params=_ARB1,
    )(h, dproj)


def _rearrange_w_in(wt):
    pad = jnp.zeros((FL_PAD - FOX_HEADS,) + wt.shape[1:], wt.dtype)
    return jnp.concatenate([wt[:3072], wt[3080:3336], wt[3336:IN_W], wt[3072:3080], pad], axis=0)


_W_IN_SEGMENTS = [(0, 3072, 0), (3072, 3080, QKV_W + MIX_W), (3080, 3336, 3072), (3336, IN_W, QKV_W)]
ASSEMBLE_COLS = 256
ASSEMBLE_ROWS = 128


def _shard_pieces(k):
    lo, hi = k * SHARD_W, (k + 1) * SHARD_W
    return [(max(lo, a) - lo, at + max(lo, a) - a, min(hi, b) - max(lo, a))
            for a, b, at in _W_IN_SEGMENTS if max(lo, a) < min(hi, b)]


def _assemble_w_in(v, to_blocks, name):
    used = QKV_W + MIX_W + FOX_HEADS
    out_shape = (N_DEV, SHARD_W, D_MODEL) if to_blocks else (WR_W, D_MODEL)
    n_chunks = D_MODEL // ASSEMBLE_COLS

    def chunk(ref, shape, j):
        return _window(ref, (len(shape) - 1, j * ASSEMBLE_COLS, ASSEMBLE_COLS))

    def body(v_ref, o_ref, vbuf, obuf, in_sems, out_sems):
        loads = [pltpu.make_async_copy(chunk(v_ref, v.shape, j), vbuf.at[j], in_sems.at[j])
                 for j in range(n_chunks)]
        stores = [pltpu.make_async_copy(obuf.at[j], chunk(o_ref, out_shape, j), out_sems.at[j])
                  for j in range(n_chunks)]
        for cp in loads:
            cp.start()
        for j in range(n_chunks):
            loads[j].wait()
            src, dst = vbuf.at[j], obuf.at[j]
            for k in range(N_DEV):
                for at, w_at, n in _shard_pieces(k):
                    for r in range(0, n, ASSEMBLE_ROWS):
                        m = min(ASSEMBLE_ROWS, n - r)
                        if to_blocks:
                            dst[k, at + r:at + r + m, :] = src[w_at + r:w_at + r + m, :]
                        else:
                            dst[w_at + r:w_at + r + m, :] = src[k, at + r:at + r + m, :]
            if not to_blocks:
                dst[used:, :] = jnp.zeros((WR_W - used, ASSEMBLE_COLS), BF16)
            stores[j].start()
        for cp in stores:
            cp.wait()

    def buffers(shape):
        return pltpu.VMEM((n_chunks,) + shape[:-1] + (ASSEMBLE_COLS,), BF16)

    both_bytes = 2 * (N_DEV * (SHARD_W + 15) // 16 * 16 + WR_W) * D_MODEL
    return pl.pallas_call(
        body, name=name, in_specs=[_ANY], out_specs=_ANY,
        out_shape=jax.ShapeDtypeStruct(out_shape, BF16),
        scratch_shapes=[buffers(v.shape), buffers(out_shape),
                        pltpu.SemaphoreType.DMA((n_chunks,)), pltpu.SemaphoreType.DMA((n_chunks,))],
        compiler_params=pltpu.CompilerParams(vmem_limit_bytes=both_bytes + (4 << 20)))(v)


def _restore_w_in(g):
    gate0 = QKV_W
    fl0 = QKV_W + MIX_W
    return jnp.concatenate(
        [g[:3072], g[fl0:fl0 + FOX_HEADS], g[3072:QKV_W], g[gate0:fl0]], axis=0)


def _pad_lanes(v, width=LANES):
    return jnp.pad(v, (0, width - v.shape[0])).reshape(1, width)


def _layer_fwd(xs, mem, nw, w_r, b_forget, mnw, late, onw, l, travel=None):
    s = xs.shape[0]
    bpad = _pad_lanes(b_forget)
    travel = _Travel(travel)
    qkv, gf, _ = travel.ride(2, _inproj_fwd, xs, nw, w_r, f"inproj_fwd_{l}")
    fqb, frow, fbounds = _fox_prep_fwd(gf, qkv, bpad, f"fox_prep_fwd_{l}")
    frow = frow.reshape(FOX_HEADS // 2, 2, s)
    ysb, _ = travel.ride(0, _sb_fwd, qkv, f"sb_fwd_{l}")
    yfx, lse_fx, _ = travel.ride(1, _fox_fwd, qkv, fqb, frow, fbounds, f"fox_fwd_{l}")
    wkv, wout = late(travel.lands)
    kv = _memkv_fwd(mem, mnw, wkv, f"memkv_fwd_{l}")
    ym, lse_m = _mem_fwd(qkv, kv, f"mem_fwd_{l}")
    xn = _outproj_fwd(ysb, yfx, ym, gf, onw, wout, xs, f"outproj_fwd_{l}")
    saved = (xs, nw, mnw, onw, bpad, qkv, gf, fqb, frow, fbounds, ysb, yfx, lse_fx, kv, ym, lse_m)
    return xn, saved, travel.lands, (wkv, wout)


class _Travel:
    def __init__(self, plan):
        self.plan = plan
        self.lands = None if plan is None else _new_lands(plan[0], plan[1])

    def ride(self, n, fn, *args):
        if self.plan is None or n >= len(self.plan[2]) or self.plan[2][n] is None:
            return fn(*args)
        srcs, scatter, legs = self.plan
        idx, rows = legs[n]
        out = fn(*args, rider=_Rider([srcs[a] for a in idx], [self.lands[a] for a in idx],
                                     scatter, rows, relay=not scatter))
        for a, land in zip(idx, out[-1]):
            self.lands[a] = land
        return out


def _layer_bwd(dx, saved, mem, w_r, wkv, wout, l, travel=None, own_w_out=None):
    xs, nw, mnw, onw, bpad, qkv, gf, fqb, frow, fbounds, ysb, yfx, lse_fx, kv, ym, lse_m = saved
    s = xs.shape[0]
    travel = _Travel(travel)
    dysb, dyfx, dym, dgate, dwout, donw, _ = travel.ride(
        3, _outproj_bwd, dx, wout, ysb, yfx, ym, gf, onw, f"outproj_bwd_{l}")
    if own_w_out is not None:
        travel.plan[0][own_w_out[0]] = own_w_out[1](dwout)
    sdq, sdk, sdv, _ = travel.ride(0, _sb_bwd, qkv, ysb, dysb, f"sb_bwd_{l}")
    fdq, fdk, fdv, dfrow, _ = travel.ride(1, _fox_bwd, qkv, fqb, frow, fbounds, yfx, lse_fx, dyfx,
                                          f"fox_bwd_{l}")
    dfl, db = _fox_prep_bwd(dfrow.reshape(FOX_HEADS, s), gf, bpad, f"fox_prep_bwd_{l}")
    dmq, dmk, dmv = _mem_bwd(qkv, kv, ym, lse_m, dym, f"mem_bwd_{l}")
    dwkv, dmnw = _memkv_bwd(mem, mnw, wkv, dmk, dmv, f"memkv_bwd_{l}")
    dx, ht, dnw, dproj, _ = travel.ride(2, _inproj_bwd_dx,
                                        [sdq, sdk, sdv, fdq, fdk, fdv, dmq, dgate, dfl],
                                        w_r, xs, nw, dx, f"inproj_bwd_dx_{l}")
    dwr = _inproj_bwd_dw(ht, dproj, f"inproj_bwd_dw_{l}")
    grads = dict(norm_w=dnw[0], w_r=dwr, b_forget=db[0, :FOX_HEADS], mem_norm_w=dmnw[0],
                 w_mem_kv=dwkv, out_norm_w=donw[0], w_out=dwout)
    return dx, grads, travel.lands


_ANY = pl.BlockSpec(memory_space=pl.ANY)


def _my_place():
    return lax.axis_index("x"), lax.axis_index("y"), lax.axis_index("c")


def _flip(v, bit):
    return 1 - v if bit else v


def _block_index(px, py, pc):
    return 4 * px + 2 * py + pc


def _all_gather_weights(shards, name):
    n = len(shards)

    def body(*refs):
        ins, outs = refs[:n], refs[n:2 * n]
        send_sems, recv_sems, local_sems = refs[2 * n:]
        x, y, c = _my_place()
        me = (x, y, c)
        sibling = (x, y, 1 - c)
        chips = [(1 - x, y), (x, 1 - y), (1 - x, 1 - y)]

        def copy(a, k, block, to, src=None):
            dst = outs[a].at[_block_index(*block)]
            return pltpu.make_async_remote_copy(
                src_ref=dst if src is None else src, dst_ref=dst,
                send_sem=send_sems.at[a, k], recv_sem=recv_sems.at[a, k],
                device_id=to, device_id_type=pl.DeviceIdType.MESH)

        mine = [pltpu.make_async_copy(ins[a], outs[a].at[_block_index(*me)], local_sems.at[a])
                for a in range(n)]
        for cp in mine:
            cp.start()
        first = []
        for a in range(n):
            first.append(copy(a, 0, me, sibling, src=ins[a]))
            first += [copy(a, 1 + j, me, (*chip, c), src=ins[a]) for j, chip in enumerate(chips)]
        for cp in first:
            cp.start()
        passed = []
        for j, chip in enumerate(chips):
            for a in range(n):
                copy(a, 1 + j, (*chip, c), me).wait_recv()
                fwd = copy(a, 4 + j, (*chip, c), sibling)
                fwd.start()
                passed.append(fwd)
        for a in range(n):
            copy(a, 0, sibling, me).wait_recv()
            for j, chip in enumerate(chips):
                copy(a, 4 + j, (*chip, 1 - c), me).wait_recv()
        for cp in first + passed:
            cp.wait_send()
        for cp in mine:
            cp.wait()

    return pl.pallas_call(
        body, name=name,
        in_specs=[_ANY] * n, out_specs=[_ANY] * n,
        out_shape=[jax.ShapeDtypeStruct((N_DEV,) + v.shape, v.dtype) for v in shards],
        scratch_shapes=[pltpu.SemaphoreType.DMA((n, 7)), pltpu.SemaphoreType.DMA((n, 7)),
                        pltpu.SemaphoreType.DMA((n,))],
    )(*shards)


def _exchange_blocks(blocked, name):
    n = len(blocked)

    def body(*refs):
        ins, outs = refs[:n], refs[n:2 * n]
        send_sems, recv_sems, local_sems = refs[2 * n:]
        x, y, c = _my_place()
        mine_idx = _block_index(x, y, c)
        local = [pltpu.make_async_copy(ins[a].at[mine_idx], outs[a].at[mine_idx], local_sems.at[a])
                 for a in range(n)]
        for cp in local:
            cp.start()
        sends, arrivals = [], []
        for r in range(1, N_DEV):
            peer = (_flip(x, r & 4), _flip(y, r & 2), _flip(c, r & 1))
            peer_idx = _block_index(*peer)
            for a in range(n):
                sems = dict(send_sem=send_sems.at[a, r - 1], recv_sem=recv_sems.at[a, r - 1],
                            device_id=peer, device_id_type=pl.DeviceIdType.MESH)
                sends.append(pltpu.make_async_remote_copy(
                    src_ref=ins[a].at[peer_idx], dst_ref=outs[a].at[mine_idx], **sems))
                arrivals.append(pltpu.make_async_remote_copy(
                    src_ref=ins[a].at[peer_idx], dst_ref=outs[a].at[peer_idx], **sems))
        for cp in sends:
            cp.start()
        for cp in arrivals:
            cp.wait_recv()
        for cp in sends:
            cp.wait_send()
        for cp in local:
            cp.wait()

    return pl.pallas_call(
        body, name=name,
        in_specs=[_ANY] * n, out_specs=[_ANY] * n,
        out_shape=[jax.ShapeDtypeStruct(v.shape, v.dtype) for v in blocked],
        scratch_shapes=[pltpu.SemaphoreType.DMA((n, 7)), pltpu.SemaphoreType.DMA((n, 7)),
                        pltpu.SemaphoreType.DMA((n,))],
    )(*blocked)


N_CHIP = N_DEV // 2


def _pair_swap(blocked, name):
    n = len(blocked)

    def body(*refs):
        ins, outs = refs[:n], refs[n:2 * n]
        send_sems, recv_sems = refs[2 * n:]
        x, y, c = _my_place()
        copies = [pltpu.make_async_remote_copy(
            src_ref=ins[a].at[j, 1 - c], dst_ref=outs[a].at[j],
            send_sem=send_sems.at[N_CHIP * a + j], recv_sem=recv_sems.at[N_CHIP * a + j],
            device_id=(x, y, 1 - c), device_id_type=pl.DeviceIdType.MESH)
            for a in range(n) for j in range(N_CHIP)]
        for cp in copies:
            cp.start()
        for cp in copies:
            cp.wait_recv()
        for cp in copies:
            cp.wait_send()

    return pl.pallas_call(
        body, name=name,
        in_specs=[_ANY] * n, out_specs=[_ANY] * n,
        out_shape=[jax.ShapeDtypeStruct((N_CHIP,) + v.shape[2:], v.dtype) for v in blocked],
        scratch_shapes=[pltpu.SemaphoreType.DMA((N_CHIP * n,)),
                        pltpu.SemaphoreType.DMA((N_CHIP * n,))],
    )(*blocked)


def _pair_add(by_core, theirs, core, name):
    n = len(by_core)

    def body(core_ref, *refs):
        for a in range(n):
            mine, got, out = refs[a], refs[n + a], refs[2 * n + a]
            out[...] = (mine[...].astype(F32) + got[...].astype(F32)).astype(BF16)

    def own(v):
        return pl.BlockSpec((None, None) + v.shape[2:], lambda j, core_ref: (j, core_ref[0], 0, 0))

    def block(v):
        return pl.BlockSpec((None,) + v.shape[1:], lambda j, core_ref: (j, 0, 0))

    return list(pl.pallas_call(
        body, name=name,
        grid_spec=pltpu.PrefetchScalarGridSpec(
            num_scalar_prefetch=1, grid=(N_CHIP,),
            in_specs=[own(v) for v in by_core] + [block(v) for v in theirs],
            out_specs=[block(v) for v in theirs]),
        out_shape=[jax.ShapeDtypeStruct(v.shape, BF16) for v in theirs],
        compiler_params=_ARB1,
    )(core, *by_core, *theirs))


def _chip_exchange(by_chip, to_all, name):
    n, m = len(by_chip), len(to_all)

    def body(*refs):
        ins, alls = refs[:n], refs[n:n + m]
        outs, all_outs = refs[n + m:2 * n + m], refs[2 * n + m:2 * (n + m)]
        send_sems, recv_sems, local_sems = refs[2 * (n + m):]
        x, y, c = _my_place()
        my_chip, mine_idx = 2 * x + y, _block_index(x, y, c)
        local = [pltpu.make_async_copy(ins[a].at[my_chip], outs[a].at[my_chip], local_sems.at[a])
                 for a in range(n)]
        local += [pltpu.make_async_copy(alls[b].at[mine_idx], all_outs[b].at[mine_idx],
                                        local_sems.at[n + b]) for b in range(m)]
        for cp in local:
            cp.start()
        sends, arrivals = [], []
        k = 0
        for r in range(1, N_DEV):
            peer = (_flip(x, r & 4), _flip(y, r & 2), _flip(c, r & 1))
            peer_chip, peer_idx = 2 * peer[0] + peer[1], _block_index(*peer)
            pairs = [(alls[b].at[mine_idx], all_outs[b].at[mine_idx], all_outs[b].at[peer_idx])
                     for b in range(m)]
            if not r & 1:
                pairs += [(ins[a].at[peer_chip], outs[a].at[my_chip], outs[a].at[peer_chip])
                          for a in range(n)]
            for src, there, here in pairs:
                sems = dict(send_sem=send_sems.at[k], recv_sem=recv_sems.at[k], device_id=peer,
                            device_id_type=pl.DeviceIdType.MESH)
                sends.append(pltpu.make_async_remote_copy(src_ref=src, dst_ref=there, **sems))
                arrivals.append(pltpu.make_async_remote_copy(src_ref=src, dst_ref=here, **sems))
                k += 1
        for cp in sends:
            cp.start()
        for cp in arrivals:
            cp.wait_recv()
        for cp in sends:
            cp.wait_send()
        for cp in local:
            cp.wait()

    n_copies = 7 * m + 3 * n
    return pl.pallas_call(
        body, name=name,
        in_specs=[_ANY] * (n + m), out_specs=[_ANY] * (n + m),
        out_shape=[jax.ShapeDtypeStruct(v.shape, v.dtype) for v in by_chip + to_all],
        scratch_shapes=[pltpu.SemaphoreType.DMA((n_copies,)), pltpu.SemaphoreType.DMA((n_copies,)),
                        pltpu.SemaphoreType.DMA((n + m,))],
    )(*by_chip, *to_all)


class _Rider(NamedTuple):
    srcs: list
    lands: list
    scatter: bool
    part: list
    relay: bool = False


def _window(ref, part):
    if part is None:
        return ref
    dim, start, size = part
    return ref.at[(slice(None),) * dim + (pl.ds(start, size),)]


def _relay_copies(srcs, lands, send_sems, recv_sems, rider):
    x, y, c = _my_place()
    me, sibling = (x, y, c), (x, y, 1 - c)
    chips = [(1 - x, y), (x, 1 - y), (1 - x, 1 - y)]
    first, from_chips, passed, last = [], [], [], []
    for a in range(len(srcs)):
        def copy(k, block, to, src=None, a=a):
            dst = _window(lands[a].at[_block_index(*block)], rider.part[a])
            return pltpu.make_async_remote_copy(
                src_ref=dst if src is None else src, dst_ref=dst,
                send_sem=send_sems.at[7 * a + k], recv_sem=recv_sems.at[7 * a + k],
                device_id=to, device_id_type=pl.DeviceIdType.MESH)

        mine = _window(srcs[a], rider.part[a])
        first.append(copy(0, me, sibling, src=mine))
        first += [copy(1 + j, me, (*chip, c), src=mine) for j, chip in enumerate(chips)]
        from_chips += [copy(1 + j, (*chip, c), me) for j, chip in enumerate(chips)]
        passed += [copy(4 + j, (*chip, c), sibling) for j, chip in enumerate(chips)]
        last.append(copy(0, sibling, me))
        last += [copy(4 + j, (*chip, 1 - c), me) for j, chip in enumerate(chips)]
    return first, from_chips, passed, last


def _new_lands(srcs, scatter):
    return [lax.empty(v.shape if scatter else (N_DEV,) + v.shape, v.dtype) for v in srcs]


def _rider_copies(srcs, lands, send_sems, recv_sems, rider):
    x, y, c = _my_place()
    mine_idx = _block_index(x, y, c)

    def window(ref, a):
        return _window(ref, rider.part[a])

    sends, arrivals = [], []
    for r in range(1, N_DEV):
        peer = (_flip(x, r & 4), _flip(y, r & 2), _flip(c, r & 1))
        peer_idx = _block_index(*peer)
        for a in range(len(srcs)):
            src = window(srcs[a].at[peer_idx] if rider.scatter else srcs[a], a)
            k = 7 * a + r - 1
            sems = dict(send_sem=send_sems.at[k], recv_sem=recv_sems.at[k],
                        device_id=peer, device_id_type=pl.DeviceIdType.MESH)
            sends.append(pltpu.make_async_remote_copy(
                src_ref=src, dst_ref=window(lands[a].at[mine_idx], a), **sems))
            arrivals.append(pltpu.make_async_remote_copy(
                src_ref=src, dst_ref=window(lands[a].at[peer_idx], a), **sems))
    return sends, arrivals


def _ride(call, rider):
    call = dict(call)
    body, grid = call.pop("body"), call["grid"]
    operands = call.pop("operands")
    if rider is None:
        return list(pl.pallas_call(body, **call)(*operands)), None
    n_in, n_out = len(call["in_specs"]), len(call["out_specs"])
    n_scratch = len(call["scratch_shapes"])
    m = len(rider.srcs)

    def riding(*refs):
        main_in, srcs, lands = refs[:n_in], refs[n_in:n_in + m], refs[n_in + m:n_in + 2 * m]
        main_out = refs[n_in + 2 * m:n_in + 2 * m + n_out]
        lands_out = refs[n_in + 2 * m + n_out:n_in + 3 * m + n_out]
        rest = refs[n_in + 3 * m + n_out:]
        send_sems, recv_sems, own_sems = rest[n_scratch:]
        me = _block_index(*_my_place())
        own = [pltpu.make_async_copy(
            _window(srcs[a].at[me] if rider.scatter else srcs[a], rider.part[a]),
            _window(lands_out[a].at[me], rider.part[a]), own_sems.at[a]) for a in range(m)]
        at = [pl.program_id(d) for d in range(len(grid))]
        first = functools.reduce(jnp.logical_and, [p == 0 for p in at])
        last = functools.reduce(jnp.logical_and, [p == g - 1 for p, g in zip(at, grid)])
        if rider.relay:
            sends, from_chips, passed, arrivals = _relay_copies(
                srcs, lands, send_sems, recv_sems, rider)
            step, steps = 0, 1
            for p, g in zip(at, grid):
                step, steps = step * g + p, steps * g
            assert steps >= 2, "a relayed gather needs a later grid step to pass blocks on"

            @pl.when(step == (7 * steps) // 8)
            def _():
                for cp in from_chips:
                    cp.wait_recv()
                for cp in passed:
                    cp.start()
        else:
            sends, arrivals = _rider_copies(srcs, lands, send_sems, recv_sems, rider)
            passed = []

        @pl.when(first)
        def _():
            for cp in sends + own:
                cp.start()

        body(*main_in, *main_out, *rest[:n_scratch])

        @pl.when(last)
        def _():
            for cp in arrivals:
                cp.wait_recv()
            for cp in sends + passed:
                cp.wait_send()
            for cp in own:
                cp.wait()

    call["in_specs"] = list(call["in_specs"]) + [_ANY] * (2 * m)
    call["out_specs"] = list(call["out_specs"]) + [_ANY] * m
    call["out_shape"] = list(call["out_shape"]) + [
        jax.ShapeDtypeStruct(v.shape, v.dtype) for v in rider.lands]
    call["scratch_shapes"] = list(call["scratch_shapes"]) + [
        pltpu.SemaphoreType.DMA((7 * m,)), pltpu.SemaphoreType.DMA((7 * m,)),
        pltpu.SemaphoreType.DMA((m,))]
    call["input_output_aliases"] = {n_in + m + a: n_out + a for a in range(m)}
    outs = pl.pallas_call(riding, **call)(*operands, *rider.srcs, *rider.lands)
    return list(outs[:n_out]), list(outs[n_out:])


def _sum_parts(p_ref):
    g = p_ref[0].astype(F32)
    for k in range(1, p_ref.shape[0]):
        g = g + p_ref[k].astype(F32)
    return g


def _adamw(g, w, m, v):
    c1 = 1.0 / (1.0 - ADAM_B1 ** ADAM_STEP)
    c2 = 1.0 / (1.0 - ADAM_B2 ** ADAM_STEP)
    nm = ADAM_B1 * m + (1.0 - ADAM_B1) * g
    nv = ADAM_B2 * v + (1.0 - ADAM_B2) * (g * g)
    return nm, nv, -ADAM_LR * ((nm * c1) / (jnp.sqrt(nv * c2) + ADAM_EPS) + ADAM_WD * w)


def _adamw_w_in(parts, w, m, v, name):
    ncol_blk, depth, nfeat = w.shape
    cols = 256

    def body(*refs):
        p_refs = refs[:depth]
        w_ref, m_ref, v_ref, g_ref, d_ref, nm_ref, nv_ref = refs[depth:]
        g = jnp.stack([_sum_parts(p_refs[l]) for l in range(depth)], axis=1)
        nm, nv, d = _adamw(g, w_ref[...], m_ref[...], v_ref[...])
        g_ref[...] = g
        nm_ref[...] = nm
        nv_ref[...] = nv
        d_ref[...] = d

    blk = pl.BlockSpec((ncol_blk, depth, cols), lambda j: (0, 0, j))
    return pl.pallas_call(
        body, name=name, grid=(nfeat // cols,),
        in_specs=[pl.BlockSpec((p.shape[0], ncol_blk, cols), lambda j: (0, 0, j)) for p in parts]
        + [blk] * 3,
        out_specs=[blk] * 4,
        out_shape=[jax.ShapeDtypeStruct(w.shape, F32)] * 4,
        compiler_params=_ARB1,
    )(*parts, w, m, v)


def _adamw_sum(parts, w, m, v, tile, name):
    depth, nrow, ncol = w.shape
    rows, cols = tile

    def body(*refs):
        p_refs = refs[:depth]
        w_ref, m_ref, v_ref, g_ref, d_ref, nm_ref, nv_ref = refs[depth:]
        layer = pl.program_id(0)
        for l in range(depth):
            @pl.when(layer == l)
            def _(p_ref=p_refs[l]):
                g = _sum_parts(p_ref)
                nm, nv, d = _adamw(g, w_ref[...], m_ref[...], v_ref[...])
                g_ref[...] = g
                nm_ref[...] = nm
                nv_ref[...] = nv
                d_ref[...] = d

    def part_spec(l):
        return pl.BlockSpec((parts[l].shape[0], rows, cols), lambda q, i, j: (
            0, jnp.where(q == l, i, 0), jnp.where(q == l, j, 0)))

    blk = pl.BlockSpec((None, rows, cols), lambda q, i, j: (q, i, j))
    return pl.pallas_call(
        body, name=name, grid=(depth, nrow // rows, ncol // cols),
        in_specs=[part_spec(l) for l in range(depth)] + [blk, blk, blk],
        out_specs=[blk] * 4,
        out_shape=[jax.ShapeDtypeStruct(w.shape, F32)] * 4,
        compiler_params=pltpu.CompilerParams(
            dimension_semantics=("arbitrary", "arbitrary", "arbitrary")),
    )(*parts, w, m, v)


def _adamw_small(parts, ws, ms, vs, name):
    n = len(parts)

    def body(*refs):
        p, w, m, v = (refs[a * n:(a + 1) * n] for a in range(4))
        out = refs[4 * n:]
        for a in range(n):
            g = _sum_parts(p[a])
            nm, nv, d = _adamw(g, w[a][...], m[a][...], v[a][...])
            out[a][...] = g
            out[n + a][...] = d
            out[2 * n + a][...] = nm
            out[3 * n + a][...] = nv

    vmem = pl.BlockSpec(memory_space=pltpu.VMEM)
    outs = pl.pallas_call(
        body, name=name, in_specs=[vmem] * (4 * n), out_specs=[vmem] * (4 * n),
        out_shape=[jax.ShapeDtypeStruct(w.shape, F32) for w in ws] * 4,
    )(*parts, *ws, *ms, *vs)
    return [list(outs[a * n:(a + 1) * n]) for a in range(4)]


def _misc_rows(b_forget, extra=None):
    tile = jnp.pad(b_forget, ((0, 6), (0, LANES - FOX_HEADS)))
    return tile if extra is None else tile.at[2].set(extra)


def kernel(x, mem, norm_w, w_in, b_forget, mem_norm_w, w_mem_kv, out_norm_w, w_out, final_norm_w, loss_target, m_norm_w, m_w_in, m_b_forget, m_mem_norm_w, m_w_mem_kv, m_out_norm_w, m_w_out, m_final_norm_w, v_norm_w, v_w_in, v_b_forget, v_mem_norm_w, v_w_mem_kv, v_out_norm_w, v_w_out, v_final_norm_w):
    kv_rows = w_mem_kv.shape[1]
    out_rows = w_out.shape[1]

    def shards(l):
        return [w_in[l].T.astype(BF16), w_mem_kv[l].astype(BF16), w_out[l].astype(BF16)]

    def full_kv_out(g_kv, g_out):
        return g_kv.reshape(D_MODEL, 2 * MEM_W), g_out.reshape(MIX_W, D_MODEL)

    def kv_blocks(g):
        return g.reshape(N_DEV, kv_rows, 2 * MEM_W).astype(BF16)

    def out_blocks(g):
        return g.reshape(N_DEV, out_rows, D_MODEL).astype(BF16)

    def row(v):
        return v.reshape(1, -1)

    def cols(first, size):
        return (1, first, size)

    fwd_split, bwd_split = 4 * LANES, (5 * LANES, 6 * LANES)

    s_in0, s_kv0, s_out0 = shards(0)
    (g_in0,) = _all_gather_weights([s_in0], "all_gather_l0")
    w_r0 = _assemble_w_in(g_in0, False, "assemble_w_in_0")
    s_in1, s_kv1, s_out1 = shards(1)
    x1, saved0, (l_in1, _, _), (wkv0, wout0) = _layer_fwd(
        x[0], mem[0], row(norm_w[0]), w_r0, b_forget[0], row(mem_norm_w[0]),
        lambda lands: full_kv_out(lands[1], lands[2]),
        row(out_norm_w[0]), 0,
        travel=([s_in1, s_kv0, s_out0], False,
                [([0], [cols(0, fwd_split)]), ([0], [cols(fwd_split, D_MODEL - fwd_split)]),
                 ([1, 2], [None, None])]))
    w_r1 = _assemble_w_in(l_in1, False, "assemble_w_in_1")
    x2, saved1, _, (wkv1, wout1) = _layer_fwd(
        x1, mem[0], row(norm_w[1]), w_r1, b_forget[1], row(mem_norm_w[1]),
        lambda lands: full_kv_out(lands[0], lands[1]),
        row(out_norm_w[1]), 1, travel=([s_kv1, s_out1], False, [([0, 1], [None, None]), None]))

    dx2, loss_part, dfnw = _final_fwd_bwd(x2, row(final_norm_w), loss_target[0], "final_fwd_bwd")

    dw_out_blocks = jax.ShapeDtypeStruct((N_DEV, out_rows, D_MODEL), BF16)
    dx1, gr1, (l_out1,) = _layer_bwd(
        dx2, saved1, mem[0], w_r1, wkv1, wout1, 1,
        travel=([dw_out_blocks], True, [([0], [None])]), own_w_out=(0, out_blocks))
    p_in1 = _assemble_w_in(gr1["w_r"], True, "w_in_grad_blocks_1")
    p_kv1 = kv_blocks(gr1["w_mem_kv"])
    grad_x, gr0, (l_in1, l_kv1, l_out0) = _layer_bwd(
        dx1[0], saved0, mem[0], w_r0, wkv0, wout0, 0,
        travel=([p_in1, p_kv1, dw_out_blocks], True,
                [([0, 1], [cols(0, bwd_split[0]), None]),
                 ([0, 2], [cols(bwd_split[0], bwd_split[1] - bwd_split[0]), None]),
                 None,
                 ([0], [cols(bwd_split[1], D_MODEL - bwd_split[1])])]),
        own_w_out=(2, out_blocks))
    r_out1, r_in1, r_kv1, r_out0 = l_out1, l_in1, l_kv1, l_out0

    def both(name):
        return jnp.stack([gr0[name], gr1[name]])

    small = [both("norm_w"), both("mem_norm_w"), both("out_norm_w"), dfnw,
             _misc_rows(both("b_forget"), loss_part[0])]
    p_small = [jnp.broadcast_to(v[None], (N_DEV,) + v.shape) for v in small]
    by_core = [v.reshape((N_CHIP, 2) + v.shape[1:])
               for v in (_assemble_w_in(gr0["w_r"], True, "w_in_grad_blocks_0"),
                         kv_blocks(gr0["w_mem_kv"]))]
    from_sibling = _pair_swap(by_core, "grads_l0_pair_swap")
    core = lax.axis_index("c").astype(jnp.int32).reshape(1)
    chip_sums = _pair_add(by_core, from_sibling, core, "grads_l0_pair_add")
    r_in0, r_kv0, *r_small = _chip_exchange(chip_sums, p_small, "exchange_grads_l0")

    def view(v):
        return jnp.transpose(v, (2, 0, 1))

    g_w_in, d_w_in, nm_w_in, nv_w_in = [jnp.transpose(v, (1, 2, 0)) for v in _adamw_w_in(
        [r_in0, r_in1], view(w_in), view(m_w_in), view(v_w_in), "adamw_w_in")]
    g_w_kv, d_w_kv, nm_w_kv, nv_w_kv = _adamw_sum(
        [r_kv0, r_kv1], w_mem_kv, m_w_mem_kv, v_w_mem_kv, (kv_rows, 2 * MEM_W), "adamw_w_mem_kv")
    g_w_out, d_w_out, nm_w_out, nv_w_out = _adamw_sum(
        [r_out0, r_out1], w_out, m_w_out, v_w_out, (out_rows, D_MODEL), "adamw_w_out")
    def smalls(nw, mnw, onw, fnw, b):
        return [nw, mnw, onw, row(fnw), _misc_rows(b)]

    small_out = _adamw_small(
        r_small, smalls(norm_w, mem_norm_w, out_norm_w, final_norm_w, b_forget),
        smalls(m_norm_w, m_mem_norm_w, m_out_norm_w, m_final_norm_w, m_b_forget),
        smalls(v_norm_w, v_mem_norm_w, v_out_norm_w, v_final_norm_w, v_b_forget), "adamw_small")
    (g_nw, g_mnw, g_onw, g_fnw, g_b), (d_nw, d_mnw, d_onw, d_fnw, d_b), \
        (nm_nw, nm_mnw, nm_onw, nm_fnw, nm_b), (nv_nw, nv_mnw, nv_onw, nv_fnw, nv_b) = [
            (nw, mnw, onw, fnw[0], misc[:2, :FOX_HEADS]) for nw, mnw, onw, fnw, misc in small_out]
    loss = small_out[0][4][2, 0]

    return (loss, grad_x,
            g_nw, g_w_in, g_b, g_mnw, g_w_kv, g_onw, g_w_out, g_fnw,
            d_nw, d_w_in, d_b, d_mnw, d_w_kv, d_onw, d_w_out, d_fnw,
            nm_nw, nm_w_in, nm_b, nm_mnw, nm_w_kv, nm_onw, nm_w_out, nm_fnw,
            nv_nw, nv_w_in, nv_b, nv_mnw, nv_w_kv, nv_onw, nv_w_out, nv_fnw)
```
